```python
import math
import jax, jax.numpy as jnp
from jax import lax
import numpy as np

D_MODEL = 1024
BATCH = 8
SEQ = 2048
DEPTH = 1

CHUNK = 64
D_MIX = 2 * D_MODEL
D_SSD = D_MIX // 2
D_CONF = D_MIX - D_SSD
SSD_HEAD_DIM = 64
SSD_HEADS = D_SSD // SSD_HEAD_DIM
SSD_GROUPS = 2
SSD_STATE = 128
SSD_CONV = 4
CONF_CONV = 31
FFN_CONV = 3
D_FF = 2816
D_XBC = D_SSD + 2 * SSD_GROUPS * SSD_STATE
IN_Z = D_SSD
IN_XBC = IN_Z + D_XBC
IN_DT = IN_XBC + SSD_HEADS
D_IN = IN_DT + 2 * D_CONF
N_MOD = 6

kernel_name = "hymba_ssd_conformer_convffn_block"


def rmsnorm(x, w, eps=1e-6):
    xf = x.astype(jnp.float32)
    y = xf * lax.rsqrt(jnp.mean(xf * xf, axis=-1, keepdims=True) + eps)
    return (y * w.astype(jnp.float32)).astype(x.dtype)


def layernorm(x, w, b, eps=1e-5):
    xf = x.astype(jnp.float32)
    mu = jnp.mean(xf, axis=-1, keepdims=True)
    var = jnp.mean(jnp.square(xf - mu), axis=-1, keepdims=True)
    y = (xf - mu) * lax.rsqrt(var + eps)
    return (y * w.astype(jnp.float32) + b.astype(jnp.float32)).astype(x.dtype)


def causal_dwconv(x, w, b):
    k = w.shape[0]
    y = lax.conv_general_dilated(
        x, w[:, None, :].astype(x.dtype), window_strides=(1,),
        padding=[(k - 1, 0)], dimension_numbers=("NWC", "WIO", "NWC"),
        feature_group_count=x.shape[-1])
    return y + b


def segsum(a):
    t = a.shape[-1]
    cs = jnp.cumsum(a, axis=-1)
    diff = cs[..., :, None] - cs[..., None, :]
    mask = jnp.tril(jnp.ones((t, t), dtype=bool))
    return jnp.where(mask, diff, -jnp.inf)


def ssd_scan(x, dt, a, b_in, c_in):
    bsz, seq, nh, hp = x.shape
    g, n = b_in.shape[2], b_in.shape[3]
    r = nh // g
    nc = seq // CHUNK
    xdt = (x.astype(jnp.float32) * dt[..., None]).reshape(bsz, nc, CHUNK, g, r, hp)
    a_dt = (a * dt).reshape(bsz, nc, CHUNK, g, r)
    a_dt = jnp.transpose(a_dt, (0, 3, 4, 1, 2))
    bc = b_in.astype(jnp.float32).reshape(bsz, nc, CHUNK, g, n)
    cc = c_in.astype(jnp.float32).reshape(bsz, nc, CHUNK, g, n)
    a_cs = jnp.cumsum(a_dt, axis=-1)
    decay = jnp.exp(segsum(a_dt))
    cb = jnp.einsum("bclgn,bcsgn->bgcls", cc, bc)
    y_diag = jnp.einsum("bgcls,bgrcls,bcsgrp->bclgrp", cb, decay, xdt)
    decay_states = jnp.exp(a_cs[..., -1:] - a_cs)
    states = jnp.einsum("bclgn,bgrcl,bclgrp->bcgrpn", bc, decay_states, xdt)
    chunk_decay = jnp.exp(a_cs[..., -1])

    def step(h, inp):
        s, d = inp
        return h * d[..., None, None] + s, h

    h0 = jnp.zeros((bsz, g, r, hp, n), jnp.float32)
    _, prev = lax.scan(step, h0, (jnp.moveaxis(states, 1, 0), jnp.moveaxis(chunk_decay, -1, 0)))
    prev = jnp.moveaxis(prev, 0, 1)
    y_off = jnp.einsum("bclgn,bcgrpn,bgrcl->bclgrp", cc, prev, jnp.exp(a_cs))
    return (y_diag + y_off).reshape(bsz, seq, nh, hp)


def _fwd_setup_inputs(seed: int = 0) -> dict:
    key = jax.random.key(seed)
    ks = jax.random.split(key, 24)
    f32 = jnp.float32
    nrm = lambda k, shp, s: jax.random.normal(k, shp, f32) * s
    dt0 = jnp.exp(jax.random.uniform(ks[8], (DEPTH, SSD_HEADS), f32, math.log(1e-3), math.log(1e-1)))
    return {
        "x": nrm(ks[0], (BATCH, SEQ, D_MODEL), 1.0),
        "c": nrm(ks[1], (BATCH, D_MODEL), 1.0),
        "ada_w": nrm(ks[2], (DEPTH, D_MODEL, N_MOD * D_MODEL), 0.5 * D_MODEL ** -0.5),
        "ada_b": nrm(ks[3], (DEPTH, N_MOD * D_MODEL), 0.02),
        "norm1_w": 1.0 + nrm(ks[4], (DEPTH, D_MODEL), 0.02),
        "w_in": nrm(ks[5], (DEPTH, D_MODEL, D_IN), D_MODEL ** -0.5),
        "ssd_conv_w": nrm(ks[6], (DEPTH, SSD_CONV, D_XBC), SSD_CONV ** -0.5),
        "ssd_conv_b": nrm(ks[7], (DEPTH, D_XBC), 0.02),
        "dt_bias": dt0 + jnp.log(-jnp.expm1(-dt0)),
        "a_log": jnp.log(jax.random.uniform(ks[9], (DEPTH, SSD_HEADS), f32, 1.0, 16.0)),
        "d_skip": 1.0 + nrm(ks[10], (DEPTH, SSD_HEADS), 0.02),
        "ssd_norm_w": 1.0 + nrm(ks[11], (DEPTH, D_SSD), 0.02),
        "conf_conv_w": nrm(ks[12], (DEPTH, CONF_CONV, D_CONF), CONF_CONV ** -0.5),
        "conf_conv_b": nrm(ks[13], (DEPTH, D_CONF), 0.02),
        "conf_ln_w": 1.0 + nrm(ks[14], (DEPTH, D_CONF), 0.02),
        "conf_ln_b": nrm(ks[15], (DEPTH, D_CONF), 0.02),
        "w_out": nrm(ks[16], (DEPTH, D_MIX, D_MODEL), D_MIX ** -0.5),
        "norm2_w": 1.0 + nrm(ks[17], (DEPTH, D_MODEL), 0.02),
        "w_up": nrm(ks[18], (DEPTH, D_MODEL, 2 * D_FF), D_MODEL ** -0.5),
        "ffn_conv_w": nrm(ks[19], (DEPTH, FFN_CONV, 2 * D_FF), FFN_CONV ** -0.5),
        "ffn_conv_b": nrm(ks[20], (DEPTH, 2 * D_FF), 0.02),
        "w_down": nrm(ks[21], (DEPTH, D_FF, D_MODEL), D_FF ** -0.5),
        "final_norm_w": 1.0 + nrm(ks[22], (D_MODEL,), 0.02),
    }


def _fwd_reference(x, c, ada_w, ada_b, norm1_w, w_in, ssd_conv_w, ssd_conv_b, dt_bias,
              a_log, d_skip, ssd_norm_w, conf_conv_w, conf_conv_b, conf_ln_w,
              conf_ln_b, w_out, norm2_w, w_up, ffn_conv_w, ffn_conv_b, w_down,
              final_norm_w):
    bsz, seq, _ = x.shape
    c_act = jax.nn.silu(c)
    for i in range(DEPTH):
        mod = c_act @ ada_w[i] + ada_b[i]
        sh1, sc1, g1, sh2, sc2, g2 = [m[:, None, :] for m in jnp.split(mod, N_MOD, axis=-1)]

        h = rmsnorm(x, norm1_w[i]) * (1.0 + sc1) + sh1
        proj = h @ w_in[i]
        z = proj[..., :IN_Z]
        xbc = proj[..., IN_Z:IN_XBC]
        dt_raw = proj[..., IN_XBC:IN_DT]
        conf = proj[..., IN_DT:]

        xbc = jax.nn.silu(causal_dwconv(xbc, ssd_conv_w[i], ssd_conv_b[i]))
        xs = xbc[..., :D_SSD].reshape(bsz, seq, SSD_HEADS, SSD_HEAD_DIM)
        bs = xbc[..., D_SSD:D_SSD + SSD_GROUPS * SSD_STATE].reshape(bsz, seq, SSD_GROUPS, SSD_STATE)
        cs = xbc[..., D_SSD + SSD_GROUPS * SSD_STATE:].reshape(bsz, seq, SSD_GROUPS, SSD_STATE)
        dt = jax.nn.softplus(dt_raw.astype(jnp.float32) + dt_bias[i].astype(jnp.float32))
        a = -jnp.exp(a_log[i].astype(jnp.float32))
        y = ssd_scan(xs, dt, a, bs, cs)
        y = y + xs.astype(jnp.float32) * d_skip[i].astype(jnp.float32)[:, None]
        y = y.reshape(bsz, seq, D_SSD).astype(x.dtype)
        y_ssd = rmsnorm(y * jax.nn.silu(z), ssd_norm_w[i])

        u = conf[..., :D_CONF] * jax.nn.sigmoid(conf[..., D_CONF:])
        u = causal_dwconv(u, conf_conv_w[i], conf_conv_b[i])
        u = jax.nn.silu(layernorm(u, conf_ln_w[i], conf_ln_b[i]))

        mix = jnp.concatenate([y_ssd, u], axis=-1) @ w_out[i]
        x = x + g1 * mix

        h = rmsnorm(x, norm2_w[i]) * (1.0 + sc2) + sh2
        up = causal_dwconv(h @ w_up[i], ffn_conv_w[i], ffn_conv_b[i])
        gate, val = up[..., :D_FF], up[..., D_FF:]
        x = x + g2 * ((jax.nn.silu(gate) * val) @ w_down[i])
    return rmsnorm(x, final_norm_w)


import jax as _jax
import jax.numpy as _jnp

TWIN_FORMAT = 'train_step'
FWD_PARAMS = ['x', 'c', 'ada_w', 'ada_b', 'norm1_w', 'w_in', 'ssd_conv_w', 'ssd_conv_b', 'dt_bias', 'a_log', 'd_skip', 'ssd_norm_w', 'conf_conv_w', 'conf_conv_b', 'conf_ln_w', 'conf_ln_b', 'w_out', 'norm2_w', 'w_up', 'ffn_conv_w', 'ffn_conv_b', 'w_down', 'final_norm_w']
TWIN_WEIGHTS = ['ada_w', 'ada_b', 'norm1_w', 'w_in', 'ssd_conv_w', 'ssd_conv_b', 'dt_bias', 'a_log', 'd_skip', 'ssd_norm_w', 'conf_conv_w', 'conf_conv_b', 'conf_ln_w', 'conf_ln_b', 'w_out', 'norm2_w', 'w_up', 'ffn_conv_w', 'ffn_conv_b', 'w_down', 'final_norm_w']
TWIN_DIFF_INPUT = 'x'
TWIN_INPUTS = ['x', 'c', 'ada_w', 'ada_b', 'norm1_w', 'w_in', 'ssd_conv_w', 'ssd_conv_b', 'dt_bias', 'a_log', 'd_skip', 'ssd_norm_w', 'conf_conv_w', 'conf_conv_b', 'conf_ln_w', 'conf_ln_b', 'w_out', 'norm2_w', 'w_up', 'ffn_conv_w', 'ffn_conv_b', 'w_down', 'final_norm_w', 'loss_target', 'm_ada_w', 'm_ada_b', 'm_norm1_w', 'm_w_in', 'm_ssd_conv_w', 'm_ssd_conv_b', 'm_dt_bias', 'm_a_log', 'm_d_skip', 'm_ssd_norm_w', 'm_conf_conv_w', 'm_conf_conv_b', 'm_conf_ln_w', 'm_conf_ln_b', 'm_w_out', 'm_norm2_w', 'm_w_up', 'm_ffn_conv_w', 'm_ffn_conv_b', 'm_w_down', 'm_final_norm_w', 'v_ada_w', 'v_ada_b', 'v_norm1_w', 'v_w_in', 'v_ssd_conv_w', 'v_ssd_conv_b', 'v_dt_bias', 'v_a_log', 'v_d_skip', 'v_ssd_norm_w', 'v_conf_conv_w', 'v_conf_conv_b', 'v_conf_ln_w', 'v_conf_ln_b', 'v_w_out', 'v_norm2_w', 'v_w_up', 'v_ffn_conv_w', 'v_ffn_conv_b', 'v_w_down', 'v_final_norm_w']
TWIN_OUTPUTS = ['loss', 'grad_x', 'grad_ada_w', 'grad_ada_b', 'grad_norm1_w', 'grad_w_in', 'grad_ssd_conv_w', 'grad_ssd_conv_b', 'grad_dt_bias', 'grad_a_log', 'grad_d_skip', 'grad_ssd_norm_w', 'grad_conf_conv_w', 'grad_conf_conv_b', 'grad_conf_ln_w', 'grad_conf_ln_b', 'grad_w_out', 'grad_norm2_w', 'grad_w_up', 'grad_ffn_conv_w', 'grad_ffn_conv_b', 'grad_w_down', 'grad_final_norm_w', 'delta_ada_w', 'delta_ada_b', 'delta_norm1_w', 'delta_w_in', 'delta_ssd_conv_w', 'delta_ssd_conv_b', 'delta_dt_bias', 'delta_a_log', 'delta_d_skip', 'delta_ssd_norm_w', 'delta_conf_conv_w', 'delta_conf_conv_b', 'delta_conf_ln_w', 'delta_conf_ln_b', 'delta_w_out', 'delta_norm2_w', 'delta_w_up', 'delta_ffn_conv_w', 'delta_ffn_conv_b', 'delta_w_down', 'delta_final_norm_w', 'new_m_ada_w', 'new_m_ada_b', 'new_m_norm1_w', 'new_m_w_in', 'new_m_ssd_conv_w', 'new_m_ssd_conv_b', 'new_m_dt_bias', 'new_m_a_log', 'new_m_d_skip', 'new_m_ssd_norm_w', 'new_m_conf_conv_w', 'new_m_conf_conv_b', 'new_m_conf_ln_w', 'new_m_conf_ln_b', 'new_m_w_out', 'new_m_norm2_w', 'new_m_w_up', 'new_m_ffn_conv_w', 'new_m_ffn_conv_b', 'new_m_w_down', 'new_m_final_norm_w', 'new_v_ada_w', 'new_v_ada_b', 'new_v_norm1_w', 'new_v_w_in', 'new_v_ssd_conv_w', 'new_v_ssd_conv_b', 'new_v_dt_bias', 'new_v_a_log', 'new_v_d_skip', 'new_v_ssd_norm_w', 'new_v_conf_conv_w', 'new_v_conf_conv_b', 'new_v_conf_ln_w', 'new_v_conf_ln_b', 'new_v_w_out', 'new_v_norm2_w', 'new_v_w_up', 'new_v_ffn_conv_w', 'new_v_ffn_conv_b', 'new_v_w_down', 'new_v_final_norm_w']
TWIN_LEAF_KINDS = {'loss': 'loss', 'grad_x': 'grad_x', 'grad_ada_w': 'grad_w', 'grad_ada_b': 'grad_w', 'grad_norm1_w': 'grad_w', 'grad_w_in': 'grad_w', 'grad_ssd_conv_w': 'grad_w', 'grad_ssd_conv_b': 'grad_w', 'grad_dt_bias': 'grad_w', 'grad_a_log': 'grad_w', 'grad_d_skip': 'grad_w', 'grad_ssd_norm_w': 'grad_w', 'grad_conf_conv_w': 'grad_w', 'grad_conf_conv_b': 'grad_w', 'grad_conf_ln_w': 'grad_w', 'grad_conf_ln_b': 'grad_w', 'grad_w_out': 'grad_w', 'grad_norm2_w': 'grad_w', 'grad_w_up': 'grad_w', 'grad_ffn_conv_w': 'grad_w', 'grad_ffn_conv_b': 'grad_w', 'grad_w_down': 'grad_w', 'grad_final_norm_w': 'grad_w', 'delta_ada_w': 'delta_w', 'delta_ada_b': 'delta_w', 'delta_norm1_w': 'delta_w', 'delta_w_in': 'delta_w', 'delta_ssd_conv_w': 'delta_w', 'delta_ssd_conv_b': 'delta_w', 'delta_dt_bias': 'delta_w', 'delta_a_log': 'delta_w', 'delta_d_skip': 'delta_w', 'delta_ssd_norm_w': 'delta_w', 'delta_conf_conv_w': 'delta_w', 'delta_conf_conv_b': 'delta_w', 'delta_conf_ln_w': 'delta_w', 'delta_conf_ln_b': 'delta_w', 'delta_w_out': 'delta_w', 'delta_norm2_w': 'delta_w', 'delta_w_up': 'delta_w', 'delta_ffn_conv_w': 'delta_w', 'delta_ffn_conv_b': 'delta_w', 'delta_w_down': 'delta_w', 'delta_final_norm_w': 'delta_w', 'new_m_ada_w': 'new_m', 'new_m_ada_b': 'new_m', 'new_m_norm1_w': 'new_m', 'new_m_w_in': 'new_m', 'new_m_ssd_conv_w': 'new_m', 'new_m_ssd_conv_b': 'new_m', 'new_m_dt_bias': 'new_m', 'new_m_a_log': 'new_m', 'new_m_d_skip': 'new_m', 'new_m_ssd_norm_w': 'new_m', 'new_m_conf_conv_w': 'new_m', 'new_m_conf_conv_b': 'new_m', 'new_m_conf_ln_w': 'new_m', 'new_m_conf_ln_b': 'new_m', 'new_m_w_out': 'new_m', 'new_m_norm2_w': 'new_m', 'new_m_w_up': 'new_m', 'new_m_ffn_conv_w': 'new_m', 'new_m_ffn_conv_b': 'new_m', 'new_m_w_down': 'new_m', 'new_m_final_norm_w': 'new_m', 'new_v_ada_w': 'new_v', 'new_v_ada_b': 'new_v', 'new_v_norm1_w': 'new_v', 'new_v_w_in': 'new_v', 'new_v_ssd_conv_w': 'new_v', 'new_v_ssd_conv_b': 'new_v', 'new_v_dt_bias': 'new_v', 'new_v_a_log': 'new_v', 'new_v_d_skip': 'new_v', 'new_v_ssd_norm_w': 'new_v', 'new_v_conf_conv_w': 'new_v', 'new_v_conf_conv_b': 'new_v', 'new_v_conf_ln_w': 'new_v', 'new_v_conf_ln_b': 'new_v', 'new_v_w_out': 'new_v', 'new_v_norm2_w': 'new_v', 'new_v_w_up': 'new_v', 'new_v_ffn_conv_w': 'new_v', 'new_v_ffn_conv_b': 'new_v', 'new_v_w_down': 'new_v', 'new_v_final_norm_w': 'new_v'}


def _forward(args):
    return _fwd_reference(*[args[k] for k in FWD_PARAMS])


def _output_shape():
    out = _jax.eval_shape(lambda: _forward(_fwd_setup_inputs(0)))
    return out.shape, out.dtype

N_MICROBATCH = 1
ADAM_LR = 0.001
ADAM_B1 = 0.9
ADAM_B2 = 0.999
ADAM_EPS = 1e-08
ADAM_WD = 0.01
ADAM_STEP = 10
PER_EXAMPLE_BATCH_AXIS = {'x': 0, 'c': 0, 'loss_target': 0}
SHARED_INPUTS = []
_WEIGHT_DTYPES = {'ada_w': _jnp.float32, 'ada_b': _jnp.float32, 'norm1_w': _jnp.float32, 'w_in': _jnp.float32, 'ssd_conv_w': _jnp.float32, 'ssd_conv_b': _jnp.float32, 'dt_bias': _jnp.float32, 'a_log': _jnp.float32, 'd_skip': _jnp.float32, 'ssd_norm_w': _jnp.float32, 'conf_conv_w': _jnp.float32, 'conf_conv_b': _jnp.float32, 'conf_ln_w': _jnp.float32, 'conf_ln_b': _jnp.float32, 'w_out': _jnp.float32, 'norm2_w': _jnp.float32, 'w_up': _jnp.float32, 'ffn_conv_w': _jnp.float32, 'ffn_conv_b': _jnp.float32, 'w_down': _jnp.float32, 'final_norm_w': _jnp.float32}
MOMENT_SCALE = {'ada_w': 4.178712e-02, 'ada_b': 7.006129e-02, 'norm1_w': 4.113448e-02, 'w_in': 2.093659e-02, 'ssd_conv_w': 2.328536e-02, 'ssd_conv_b': 2.929341e-02, 'dt_bias': 7.896925e-02, 'a_log': 9.224119e-02, 'd_skip': 2.532010e-01, 'ssd_norm_w': 2.944042e-02, 'conf_conv_w': 1.667867e-02, 'conf_conv_b': 3.084372e-02, 'conf_ln_w': 1.949304e-02, 'conf_ln_b': 1.707243e-02, 'w_out': 3.163699e-02, 'norm2_w': 3.629374e-02, 'w_up': 1.637896e-02, 'ffn_conv_w': 1.642241e-02, 'ffn_conv_b': 1.490976e-02, 'w_down': 2.682461e-02, 'final_norm_w': 1.603441e+01}


def _to_microbatches(a, axis):
    t = _jnp.moveaxis(a, axis, 0)
    t = t.reshape((N_MICROBATCH, t.shape[0] // N_MICROBATCH) + t.shape[1:])
    return _jnp.moveaxis(t, 1, axis + 1)


def setup_inputs(seed: int = 0) -> dict:
    inp = _fwd_setup_inputs(seed)
    key = _jax.random.fold_in(_jax.random.key(seed), 7919)
    shape, _ = _output_shape()
    out = dict(inp)
    out["loss_target"] = _jax.random.normal(_jax.random.fold_in(key, 0), shape, _jnp.float32)
    for i, name in enumerate(TWIN_WEIGHTS):
        w = inp[name].astype(_jnp.float32)
        if MOMENT_SCALE is None:
            s = _jnp.sqrt(_jnp.mean(_jnp.square(w)) + 1e-30)
        else:
            s = MOMENT_SCALE[name]
        km, kv = _jax.random.split(_jax.random.fold_in(key, i + 1))
        out[name] = w
        out["m_" + name] = s * _jax.random.normal(km, w.shape, _jnp.float32)
        out["v_" + name] = (s * s) * _jax.random.uniform(kv, w.shape, _jnp.float32, 0.5, 1.5)
    if N_MICROBATCH > 1:
        for name, axis in PER_EXAMPLE_BATCH_AXIS.items():
            out[name] = _to_microbatches(out[name], axis)
    return {'x': out['x'], 'c': out['c'], 'ada_w': out['ada_w'], 'ada_b': out['ada_b'], 'norm1_w': out['norm1_w'], 'w_in': out['w_in'], 'ssd_conv_w': out['ssd_conv_w'], 'ssd_conv_b': out['ssd_conv_b'], 'dt_bias': out['dt_bias'], 'a_log': out['a_log'], 'd_skip': out['d_skip'], 'ssd_norm_w': out['ssd_norm_w'], 'conf_conv_w': out['conf_conv_w'], 'conf_conv_b': out['conf_conv_b'], 'conf_ln_w': out['conf_ln_w'], 'conf_ln_b': out['conf_ln_b'], 'w_out': out['w_out'], 'norm2_w': out['norm2_w'], 'w_up': out['w_up'], 'ffn_conv_w': out['ffn_conv_w'], 'ffn_conv_b': out['ffn_conv_b'], 'w_down': out['w_down'], 'final_norm_w': out['final_norm_w'], 'loss_target': out['loss_target'], 'm_ada_w': out['m_ada_w'], 'm_ada_b': out['m_ada_b'], 'm_norm1_w': out['m_norm1_w'], 'm_w_in': out['m_w_in'], 'm_ssd_conv_w': out['m_ssd_conv_w'], 'm_ssd_conv_b': out['m_ssd_conv_b'], 'm_dt_bias': out['m_dt_bias'], 'm_a_log': out['m_a_log'], 'm_d_skip': out['m_d_skip'], 'm_ssd_norm_w': out['m_ssd_norm_w'], 'm_conf_conv_w': out['m_conf_conv_w'], 'm_conf_conv_b': out['m_conf_conv_b'], 'm_conf_ln_w': out['m_conf_ln_w'], 'm_conf_ln_b': out['m_conf_ln_b'], 'm_w_out': out['m_w_out'], 'm_norm2_w': out['m_norm2_w'], 'm_w_up': out['m_w_up'], 'm_ffn_conv_w': out['m_ffn_conv_w'], 'm_ffn_conv_b': out['m_ffn_conv_b'], 'm_w_down': out['m_w_down'], 'm_final_norm_w': out['m_final_norm_w'], 'v_ada_w': out['v_ada_w'], 'v_ada_b': out['v_ada_b'], 'v_norm1_w': out['v_norm1_w'], 'v_w_in': out['v_w_in'], 'v_ssd_conv_w': out['v_ssd_conv_w'], 'v_ssd_conv_b': out['v_ssd_conv_b'], 'v_dt_bias': out['v_dt_bias'], 'v_a_log': out['v_a_log'], 'v_d_skip': out['v_d_skip'], 'v_ssd_norm_w': out['v_ssd_norm_w'], 'v_conf_conv_w': out['v_conf_conv_w'], 'v_conf_conv_b': out['v_conf_conv_b'], 'v_conf_ln_w': out['v_conf_ln_w'], 'v_conf_ln_b': out['v_conf_ln_b'], 'v_w_out': out['v_w_out'], 'v_norm2_w': out['v_norm2_w'], 'v_w_up': out['v_w_up'], 'v_ffn_conv_w': out['v_ffn_conv_w'], 'v_ffn_conv_b': out['v_ffn_conv_b'], 'v_w_down': out['v_w_down'], 'v_final_norm_w': out['v_final_norm_w']}


def _loss(weights, diff, rest, loss_target):
    with _jax.named_scope("forward"):
        args = {**rest, TWIN_DIFF_INPUT: diff, **{k: w.astype(_WEIGHT_DTYPES[k]) for k, w in weights.items()}}
        y = _forward(args)
    with _jax.named_scope("loss_head"):
        err = _jnp.square(y.astype(_jnp.float32) - loss_target)
        return 0.5 * _jnp.sum(_jnp.mean(err, axis=-1)) if err.ndim else 0.5 * err


def _adamw(w, g, m, v):
    m = ADAM_B1 * m + (1.0 - ADAM_B1) * g
    v = ADAM_B2 * v + (1.0 - ADAM_B2) * _jnp.square(g)
    m_hat = m / (1.0 - ADAM_B1 ** ADAM_STEP)
    v_hat = v / (1.0 - ADAM_B2 ** ADAM_STEP)
    delta = -ADAM_LR * (m_hat / (_jnp.sqrt(v_hat) + ADAM_EPS) + ADAM_WD * w)
    return delta, m, v


def reference(x, c, ada_w, ada_b, norm1_w, w_in, ssd_conv_w, ssd_conv_b, dt_bias, a_log, d_skip, ssd_norm_w, conf_conv_w, conf_conv_b, conf_ln_w, conf_ln_b, w_out, norm2_w, w_up, ffn_conv_w, ffn_conv_b, w_down, final_norm_w, loss_target, m_ada_w, m_ada_b, m_norm1_w, m_w_in, m_ssd_conv_w, m_ssd_conv_b, m_dt_bias, m_a_log, m_d_skip, m_ssd_norm_w, m_conf_conv_w, m_conf_conv_b, m_conf_ln_w, m_conf_ln_b, m_w_out, m_norm2_w, m_w_up, m_ffn_conv_w, m_ffn_conv_b, m_w_down, m_final_norm_w, v_ada_w, v_ada_b, v_norm1_w, v_w_in, v_ssd_conv_w, v_ssd_conv_b, v_dt_bias, v_a_log, v_d_skip, v_ssd_norm_w, v_conf_conv_w, v_conf_conv_b, v_conf_ln_w, v_conf_ln_b, v_w_out, v_norm2_w, v_w_up, v_ffn_conv_w, v_ffn_conv_b, v_w_down, v_final_norm_w):
    given = dict(x=x, c=c, ada_w=ada_w, ada_b=ada_b, norm1_w=norm1_w, w_in=w_in, ssd_conv_w=ssd_conv_w, ssd_conv_b=ssd_conv_b, dt_bias=dt_bias, a_log=a_log, d_skip=d_skip, ssd_norm_w=ssd_norm_w, conf_conv_w=conf_conv_w, conf_conv_b=conf_conv_b, conf_ln_w=conf_ln_w, conf_ln_b=conf_ln_b, w_out=w_out, norm2_w=norm2_w, w_up=w_up, ffn_conv_w=ffn_conv_w, ffn_conv_b=ffn_conv_b, w_down=w_down, final_norm_w=final_norm_w, loss_target=loss_target, m_ada_w=m_ada_w, m_ada_b=m_ada_b, m_norm1_w=m_norm1_w, m_w_in=m_w_in, m_ssd_conv_w=m_ssd_conv_w, m_ssd_conv_b=m_ssd_conv_b, m_dt_bias=m_dt_bias, m_a_log=m_a_log, m_d_skip=m_d_skip, m_ssd_norm_w=m_ssd_norm_w, m_conf_conv_w=m_conf_conv_w, m_conf_conv_b=m_conf_conv_b, m_conf_ln_w=m_conf_ln_w, m_conf_ln_b=m_conf_ln_b, m_w_out=m_w_out, m_norm2_w=m_norm2_w, m_w_up=m_w_up, m_ffn_conv_w=m_ffn_conv_w, m_ffn_conv_b=m_ffn_conv_b, m_w_down=m_w_down, m_final_norm_w=m_final_norm_w, v_ada_w=v_ada_w, v_ada_b=v_ada_b, v_norm1_w=v_norm1_w, v_w_in=v_w_in, v_ssd_conv_w=v_ssd_conv_w, v_ssd_conv_b=v_ssd_conv_b, v_dt_bias=v_dt_bias, v_a_log=v_a_log, v_d_skip=v_d_skip, v_ssd_norm_w=v_ssd_norm_w, v_conf_conv_w=v_conf_conv_w, v_conf_conv_b=v_conf_conv_b, v_conf_ln_w=v_conf_ln_w, v_conf_ln_b=v_conf_ln_b, v_w_out=v_w_out, v_norm2_w=v_norm2_w, v_w_up=v_w_up, v_ffn_conv_w=v_ffn_conv_w, v_ffn_conv_b=v_ffn_conv_b, v_w_down=v_w_down, v_final_norm_w=v_final_norm_w)
    weights = {n: given[n] for n in TWIN_WEIGHTS}
    shared = {n: given[n] for n in SHARED_INPUTS}
    per_example = {n: given[n] for n in ['x', 'c']}
    grad_fn = _jax.value_and_grad(_loss, argnums=(0, 1))

    def one_microbatch(ex, loss_target):
        ex = dict(ex)
        diff = ex.pop(TWIN_DIFF_INPUT)
        return grad_fn(weights, diff, {**shared, **ex}, loss_target)

    if N_MICROBATCH == 1:
        loss, (grad_w, grad_x) = one_microbatch(per_example, given["loss_target"])
    else:
        def body(carry, xs):
            loss_sum, grad_sum = carry
            l_k, (gw_k, gx_k) = one_microbatch(xs[0], xs[1])
            with _jax.named_scope("update"):
                return (loss_sum + l_k, _jax.tree.map(_jnp.add, grad_sum, gw_k)), gx_k

        init = (_jnp.zeros((), _jnp.float32), _jax.tree.map(_jnp.zeros_like, weights))
        (loss, grad_w), grad_x = _jax.lax.scan(body, init, (per_example, given["loss_target"]))
    with _jax.named_scope("update"):
        delta_w, new_m, new_v = {}, {}, {}
        for n in TWIN_WEIGHTS:
            delta_w[n], new_m[n], new_v[n] = _adamw(weights[n], grad_w[n], given["m_" + n], given["v_" + n])
    return (loss, grad_x, *[grad_w[n] for n in TWIN_WEIGHTS], *[delta_w[n] for n in TWIN_WEIGHTS],
            *[new_m[n] for n in TWIN_WEIGHTS], *[new_v[n] for n in TWIN_WEIGHTS])
```

```python
import functools

import jax
import jax.numpy as jnp
from jax import lax
from jax.experimental import pallas as pl
from jax.experimental.pallas import tpu as pltpu

f32 = jnp.float32
MX = jnp.bfloat16

D = 1024
HEADS = 16
HEAD_P = 64
STATE_N = 128
D_XBC = 1536
D_FF = 2816
UP_SHARD = 2 * D_FF // 4
K_SSD, K_CONF, K_FFN = 4, 31, 3
CHUNK = 64
OFF_Z, OFF_XBC, OFF_CA, OFF_CG, OFF_DT = 0, 1024, 2560, 3584, 4608
W_PACK = 4736
TM = 256
CW = 256
RC = 64
LANES = 128
VMEM_LIMIT = 56 * 1024 * 1024

ADAM_LR, ADAM_B1, ADAM_B2, ADAM_EPS, ADAM_WD, ADAM_STEP = 0.001, 0.9, 0.999, 1e-08, 0.01, 10

MESH = pl.DeviceIdType.MESH


def _cp(*sem):
    return pltpu.CompilerParams(dimension_semantics=sem, vmem_limit_bytes=VMEM_LIMIT)


def _resident(shape):
    nd = len(shape)
    return pl.BlockSpec(shape, lambda *_: (0,) * nd, pipeline_mode=pl.Buffered(1))


def _row(width=D):
    return pl.BlockSpec((1, width), lambda *_: (0, 0))


def _silu(v):
    return v * jax.nn.sigmoid(v)


def _dsilu(v):
    s = jax.nn.sigmoid(v)
    return s * (1.0 + v * (1.0 - s))


def _softplus(v):
    return jnp.maximum(v, 0.0) + jnp.log1p(jnp.exp(-jnp.abs(v)))


def _mm(a, b):
    return jnp.dot(a.astype(MX), b.astype(MX), preferred_element_type=f32)


def _mm_nt(a, b):
    return lax.dot_general(a.astype(MX), b.astype(MX), (((1,), (1,)), ((), ())), preferred_element_type=f32)


def _mm_tn(a, b):
    return lax.dot_general(a.astype(MX), b.astype(MX), (((0,), (0,)), ((), ())), preferred_element_type=f32)


def _ln_inproj(x, mod, norm1_w, w_pack):
    t = x.shape[0]

    def body(x_ref, mod_ref, nw_ref, w_ref, proj_ref, h_ref):
        xv = x_ref[...]
        rstd = lax.rsqrt(jnp.mean(xv * xv, axis=-1, keepdims=True) + 1e-6)
        h = (xv * rstd * nw_ref[...]) * (1.0 + mod_ref[:, D:2 * D]) + mod_ref[:, 0:D]
        hb = h.astype(MX)
        h_ref[...] = hb
        proj_ref[...] = jnp.dot(hb, w_ref[...], preferred_element_type=f32)

    return pl.pallas_call(
        body, name="ln_inproj", grid=(t // TM,),
        out_shape=(jax.ShapeDtypeStruct((t, W_PACK), f32), jax.ShapeDtypeStruct((t, D), MX)),
        in_specs=[pl.BlockSpec((TM, D), lambda i: (i, 0)), _row(6 * D), _row(), _resident((D, W_PACK))],
        out_specs=(pl.BlockSpec((TM, W_PACK), lambda i: (i, 0)), pl.BlockSpec((TM, D), lambda i: (i, 0))),
        compiler_params=_cp("arbitrary"),
    )(x, mod, norm1_w, w_pack)


def _ssd_gate_norm(y_scan, xbc_act, proj, d_skip_row, ssd_norm_w):
    t = y_scan.shape[0]

    def body(y_ref, xs_ref, z_ref, dsk_ref, nw_ref, o_ref):
        y = y_ref[...] + xs_ref[...] * dsk_ref[...]
        yz = y * _silu(z_ref[...])
        rstd = lax.rsqrt(jnp.mean(yz * yz, axis=-1, keepdims=True) + 1e-6)
        o_ref[...] = (yz * rstd * nw_ref[...]).astype(MX)

    blk = pl.BlockSpec((TM, D), lambda i: (i, 0))
    return pl.pallas_call(
        body, name="ssd_gate_norm", grid=(t // TM,), out_shape=jax.ShapeDtypeStruct((t, D), MX),
        in_specs=[blk, blk, blk, _row(), _row()], out_specs=blk, compiler_params=_cp("arbitrary"),
    )(y_scan, xbc_act, proj, d_skip_row, ssd_norm_w)


def _ln_silu(u_conv, ln_w, ln_b):
    t = u_conv.shape[0]

    def body(u_ref, w_ref, b_ref, o_ref):
        u = u_ref[...]
        mu = jnp.mean(u, axis=-1, keepdims=True)
        uc = u - mu
        rstd = lax.rsqrt(jnp.mean(uc * uc, axis=-1, keepdims=True) + 1e-5)
        o_ref[...] = _silu(uc * rstd * w_ref[...] + b_ref[...]).astype(MX)

    blk = pl.BlockSpec((TM, D), lambda i: (i, 0))
    return pl.pallas_call(
        body, name="ln_silu", grid=(t // TM,), out_shape=jax.ShapeDtypeStruct((t, D), MX),
        in_specs=[blk, _row(), _row()], out_specs=blk, compiler_params=_cp("arbitrary"),
    )(u_conv, ln_w, ln_b)


def _outproj_ln2_up(y_ssd, u, w_out, x, mod, norm2_w, w_up):
    t = x.shape[0]

    def body(y_ref, u_ref, wo_ref, x_ref, mod_ref, nw_ref, wu_ref, mix_ref, x1_ref, h2_ref, up_ref):
        mix = jnp.dot(y_ref[...], wo_ref[0:D, :], preferred_element_type=f32)
        mix = mix + jnp.dot(u_ref[...], wo_ref[D:2 * D, :], preferred_element_type=f32)
        mix_ref[...] = mix
        x1 = x_ref[...] + mod_ref[:, 2 * D:3 * D] * mix
        x1_ref[...] = x1
        rstd = lax.rsqrt(jnp.mean(x1 * x1, axis=-1, keepdims=True) + 1e-6)
        h2 = ((x1 * rstd * nw_ref[...]) * (1.0 + mod_ref[:, 4 * D:5 * D]) + mod_ref[:, 3 * D:4 * D]).astype(MX)
        h2_ref[...] = h2
        for k in range(4):
            up_ref[:, k * UP_SHARD:(k + 1) * UP_SHARD] = jnp.dot(h2, wu_ref[k], preferred_element_type=f32)

    blk = pl.BlockSpec((TM, D), lambda i: (i, 0))
    return pl.pallas_call(
        body, name="outproj_ln2_up", grid=(t // TM,),
        out_shape=(jax.ShapeDtypeStruct((t, D), f32), jax.ShapeDtypeStruct((t, D), f32),
                   jax.ShapeDtypeStruct((t, D), MX), jax.ShapeDtypeStruct((t, 2 * D_FF), f32)),
        in_specs=[blk, blk, _resident((2 * D, D)), blk, _row(6 * D), _row(), _resident((4, D, UP_SHARD))],
        out_specs=(blk, blk, blk, pl.BlockSpec((TM, 2 * D_FF), lambda i: (i, 0))),
        compiler_params=_cp("arbitrary"),
    )(y_ssd, u, w_out, x, mod, norm2_w, w_up)


def _down_loss(act, w_down, x1, mod, final_norm_w, target):
    t = x1.shape[0]

    def body(a_ref, wd_ref, x1_ref, mod_ref, wf_ref, tgt_ref, dx2_ref, dffn_ref, dact_ref, st_ref):
        @pl.when(pl.program_id(0) == 0)
        def _():
            st_ref[...] = jnp.zeros_like(st_ref)

        g2 = mod_ref[:, 5 * D:6 * D]
        ffn = jnp.dot(a_ref[...], wd_ref[...], preferred_element_type=f32)
        x2 = x1_ref[...] + g2 * ffn
        rstd = lax.rsqrt(jnp.mean(x2 * x2, axis=-1, keepdims=True) + 1e-6)
        xh = x2 * rstd
        wf = wf_ref[...]
        err = xh * wf - tgt_ref[...]
        dy = err * (1.0 / D)
        dxh = dy * wf
        dx2 = rstd * (dxh - xh * jnp.mean(dxh * xh, axis=-1, keepdims=True))
        dx2_ref[...] = dx2
        dffn = (g2 * dx2).astype(MX)
        dffn_ref[...] = dffn
        dact_ref[...] = lax.dot_general(dffn, wd_ref[...], (((1,), (1,)), ((), ())), preferred_element_type=f32)
        st_ref[0:1, :] += jnp.sum(dy * xh, axis=0, keepdims=True)
        st_ref[1:2, :] += jnp.sum(dx2 * ffn, axis=0, keepdims=True)
        st_ref[2:3, :] += jnp.sum(0.5 * jnp.mean(err * err, axis=-1, keepdims=True), axis=0, keepdims=True)

    blk = pl.BlockSpec((TM, D), lambda i: (i, 0))
    ablk = pl.BlockSpec((TM, D_FF), lambda i: (i, 0))
    return pl.pallas_call(
        body, name="down_loss", grid=(t // TM,),
        out_shape=(jax.ShapeDtypeStruct((t, D), f32), jax.ShapeDtypeStruct((t, D), MX),
                   jax.ShapeDtypeStruct((t, D_FF), f32), jax.ShapeDtypeStruct((8, D), f32)),
        in_specs=[ablk, _resident((D_FF, D)), blk, _row(6 * D), _row(), blk],
        out_specs=(blk, blk, ablk, pl.BlockSpec((8, D), lambda i: (0, 0))),
        compiler_params=_cp("arbitrary"),
    )(act, w_down, x1, mod, final_norm_w, target)


def _pad_of(k):
    return 8 * ((k - 1 + 7) // 8)


def _causal_win(ref, r, t, pad):
    base = pl.multiple_of(r * RC, RC)
    prev = ref[pl.ds(pl.multiple_of(jnp.maximum(base - pad, 0), 8), pad), :]
    prev = jnp.where(r > 0, prev, 0.0)
    return jnp.concatenate([prev, ref[pl.ds(base, RC), :]], axis=0)


def _anti_win(ref, r, t, pad):
    base = pl.multiple_of(r * RC, RC)
    nxt = ref[pl.ds(pl.multiple_of(jnp.minimum(base + RC, t - pad), 8), pad), :]
    nxt = jnp.where(r < t // RC - 1, nxt, 0.0)
    return jnp.concatenate([ref[pl.ds(base, RC), :], nxt], axis=0)


def _conv_taps(win, w_ref, k, pad):
    acc = None
    for j in range(k):
        o = pad - (k - 1) + j
        term = w_ref[j:j + 1, :] * win[o:o + RC, :]
        acc = term if acc is None else acc + term
    return acc


def _corr_taps(win, w_ref, k):
    acc = None
    for j in range(k):
        o = (k - 1) - j
        term = w_ref[j:j + 1, :] * win[o:o + RC, :]
        acc = term if acc is None else acc + term
    return acc


def _dw_accumulate(dw_scr, d, win, k, pad):
    for j in range(k):
        o = pad - (k - 1) + j
        prod = d * win[o:o + RC, :]
        dw_scr[8 * j:8 * j + 8, :] += prod.reshape(RC // 8, 8, prod.shape[-1]).sum(axis=0)


def _dw_finish(dw_scr, dw_ref, k):
    for j in range(k):
        dw_ref[j:j + 1, :] = jnp.sum(dw_scr[8 * j:8 * j + 8, :], axis=0, keepdims=True)


def _rows8(v):
    return v.reshape(RC // 8, 8, v.shape[-1]).sum(axis=0)


def _ssd_conv_fwd(proj, conv_w, conv_b):
    t = proj.shape[0]
    pad = _pad_of(K_SSD)
    c0 = OFF_XBC // CW

    def body(x_ref, w_ref, b_ref, o_ref):
        def step(r, carry):
            win = _causal_win(x_ref, r, t, pad)
            o_ref[pl.ds(pl.multiple_of(r * RC, RC), RC), :] = _silu(_conv_taps(win, w_ref, K_SSD, pad) + b_ref[...])
            return carry
        lax.fori_loop(0, t // RC, step, 0)

    return pl.pallas_call(
        body, name="ssd_conv_fwd", grid=(D_XBC // CW,), out_shape=jax.ShapeDtypeStruct((t, D_XBC), f32),
        in_specs=[pl.BlockSpec((t, CW), lambda j: (0, c0 + j)), pl.BlockSpec((K_SSD, CW), lambda j: (0, j)),
                  pl.BlockSpec((1, CW), lambda j: (0, j))],
        out_specs=pl.BlockSpec((t, CW), lambda j: (0, j)), compiler_params=_cp("arbitrary"),
    )(proj, conv_w, conv_b)


def _glu_conv_fwd(proj, conv_w, conv_b):
    t = proj.shape[0]
    pad = _pad_of(K_CONF)
    ca, cg = OFF_CA // CW, OFF_CG // CW

    def body(a_ref, g_ref, w_ref, b_ref, o_ref, v_scr):
        def glu(r, carry):
            rows = pl.ds(pl.multiple_of(r * RC, RC), RC)
            v_scr[rows, :] = a_ref[rows, :] * jax.nn.sigmoid(g_ref[rows, :])
            return carry
        lax.fori_loop(0, t // RC, glu, 0)

        def step(r, carry):
            win = _causal_win(v_scr, r, t, pad)
            o_ref[pl.ds(pl.multiple_of(r * RC, RC), RC), :] = _conv_taps(win, w_ref, K_CONF, pad) + b_ref[...]
            return carry
        lax.fori_loop(0, t // RC, step, 0)

    return pl.pallas_call(
        body, name="glu_conv_fwd", grid=(D // CW,), out_shape=jax.ShapeDtypeStruct((t, D), f32),
        in_specs=[pl.BlockSpec((t, CW), lambda j: (0, ca + j)), pl.BlockSpec((t, CW), lambda j: (0, cg + j)),
                  pl.BlockSpec((K_CONF, CW), lambda j: (0, j)), pl.BlockSpec((1, CW), lambda j: (0, j))],
        out_specs=pl.BlockSpec((t, CW), lambda j: (0, j)),
        scratch_shapes=[pltpu.VMEM((t, CW), f32)], compiler_params=_cp("arbitrary"),
    )(proj, proj, conv_w, conv_b)


def _ffn_conv_fwd(up, conv_w, conv_b):
    t = up.shape[0]
    pad = _pad_of(K_FFN)
    nb = D_FF // CW

    def body(g_ref, v_ref, wg_ref, wv_ref, bg_ref, bv_ref, o_ref):
        def step(r, carry):
            gc = _conv_taps(_causal_win(g_ref, r, t, pad), wg_ref, K_FFN, pad) + bg_ref[...]
            vc = _conv_taps(_causal_win(v_ref, r, t, pad), wv_ref, K_FFN, pad) + bv_ref[...]
            o_ref[pl.ds(pl.multiple_of(r * RC, RC), RC), :] = (_silu(gc) * vc).astype(MX)
            return carry
        lax.fori_loop(0, t // RC, step, 0)

    return pl.pallas_call(
        body, name="ffn_conv_fwd", grid=(nb,), out_shape=jax.ShapeDtypeStruct((t, D_FF), MX),
        in_specs=[pl.BlockSpec((t, CW), lambda j: (0, j)), pl.BlockSpec((t, CW), lambda j: (0, nb + j)),
                  pl.BlockSpec((K_FFN, CW), lambda j: (0, j)), pl.BlockSpec((K_FFN, CW), lambda j: (0, nb + j)),
                  pl.BlockSpec((1, CW), lambda j: (0, j)), pl.BlockSpec((1, CW), lambda j: (0, nb + j))],
        out_specs=pl.BlockSpec((t, CW), lambda j: (0, j)), compiler_params=_cp("arbitrary"),
    )(up, up, conv_w, conv_w, conv_b, conv_b)


def _ffn_conv_bwd(up, conv_w, conv_b, d_act):
    t = up.shape[0]
    pad = _pad_of(K_FFN)
    nb = D_FF // CW

    def body(g_ref, v_ref, wg_ref, wv_ref, bg_ref, bv_ref, da_ref, dup_ref, dw_ref, db_ref,
             dg_scr, dv_scr, dwg_scr, dwv_scr, db_scr):
        dwg_scr[...] = jnp.zeros_like(dwg_scr)
        dwv_scr[...] = jnp.zeros_like(dwv_scr)
        db_scr[...] = jnp.zeros_like(db_scr)

        def first(r, carry):
            rows = pl.ds(pl.multiple_of(r * RC, RC), RC)
            gwin = _causal_win(g_ref, r, t, pad)
            vwin = _causal_win(v_ref, r, t, pad)
            gc = _conv_taps(gwin, wg_ref, K_FFN, pad) + bg_ref[...]
            vc = _conv_taps(vwin, wv_ref, K_FFN, pad) + bv_ref[...]
            da = da_ref[rows, :]
            dgc = da * vc * _dsilu(gc)
            dvc = da * _silu(gc)
            dg_scr[rows, :] = dgc
            dv_scr[rows, :] = dvc
            _dw_accumulate(dwg_scr, dgc, gwin, K_FFN, pad)
            _dw_accumulate(dwv_scr, dvc, vwin, K_FFN, pad)
            db_scr[0:8, :] += _rows8(dgc)
            db_scr[8:16, :] += _rows8(dvc)
            return carry
        lax.fori_loop(0, t // RC, first, 0)

        def second(r, carry):
            rows = pl.ds(pl.multiple_of(r * RC, RC), RC)
            dup_ref[0, rows, :] = _corr_taps(_anti_win(dg_scr, r, t, pad), wg_ref, K_FFN).astype(MX)
            dup_ref[1, rows, :] = _corr_taps(_anti_win(dv_scr, r, t, pad), wv_ref, K_FFN).astype(MX)
            return carry
        lax.fori_loop(0, t // RC, second, 0)

        for j in range(K_FFN):
            dw_ref[0, j:j + 1, :] = jnp.sum(dwg_scr[8 * j:8 * j + 8, :], axis=0, keepdims=True)
            dw_ref[1, j:j + 1, :] = jnp.sum(dwv_scr[8 * j:8 * j + 8, :], axis=0, keepdims=True)
        db_ref[0] = jnp.sum(db_scr[0:8, :], axis=0, keepdims=True)
        db_ref[1] = jnp.sum(db_scr[8:16, :], axis=0, keepdims=True)

    return pl.pallas_call(
        body, name="ffn_conv_bwd", grid=(nb,),
        out_shape=(jax.ShapeDtypeStruct((2, t, D_FF), MX), jax.ShapeDtypeStruct((2, K_FFN, D_FF), f32),
                   jax.ShapeDtypeStruct((2, 1, D_FF), f32)),
        in_specs=[pl.BlockSpec((t, CW), lambda j: (0, j)), pl.BlockSpec((t, CW), lambda j: (0, nb + j)),
                  pl.BlockSpec((K_FFN, CW), lambda j: (0, j)), pl.BlockSpec((K_FFN, CW), lambda j: (0, nb + j)),
                  pl.BlockSpec((1, CW), lambda j: (0, j)), pl.BlockSpec((1, CW), lambda j: (0, nb + j)),
                  pl.BlockSpec((t, CW), lambda j: (0, j))],
        out_specs=(pl.BlockSpec((2, t, CW), lambda j: (0, 0, j)), pl.BlockSpec((2, K_FFN, CW), lambda j: (0, 0, j)),
                   pl.BlockSpec((2, 1, CW), lambda j: (0, 0, j))),
        scratch_shapes=[pltpu.VMEM((t, CW), f32), pltpu.VMEM((t, CW), f32), pltpu.VMEM((8 * K_FFN, CW), f32),
                        pltpu.VMEM((8 * K_FFN, CW), f32), pltpu.VMEM((16, CW), f32)],
        compiler_params=_cp("arbitrary"),
    )(up, up, conv_w, conv_w, conv_b, conv_b, d_act)


def _glu_conv_bwd(proj, conv_w, d_uconv):
    t = proj.shape[0]
    pad = _pad_of(K_CONF)
    ca, cg = OFF_CA // CW, OFF_CG // CW

    def body(a_ref, g_ref, w_ref, du_ref, dc_ref, dw_ref, db_ref, v_scr, dw_scr, db_scr):
        dw_scr[...] = jnp.zeros_like(dw_scr)
        db_scr[...] = jnp.zeros_like(db_scr)

        def glu(r, carry):
            rows = pl.ds(pl.multiple_of(r * RC, RC), RC)
            v_scr[rows, :] = a_ref[rows, :] * jax.nn.sigmoid(g_ref[rows, :])
            return carry
        lax.fori_loop(0, t // RC, glu, 0)

        def step(r, carry):
            rows = pl.ds(pl.multiple_of(r * RC, RC), RC)
            du = du_ref[rows, :]
            _dw_accumulate(dw_scr, du, _causal_win(v_scr, r, t, pad), K_CONF, pad)
            db_scr[...] += _rows8(du)
            dv = _corr_taps(_anti_win(du_ref, r, t, pad), w_ref, K_CONF)
            a = a_ref[rows, :]
            s = jax.nn.sigmoid(g_ref[rows, :])
            dc_ref[0, rows, :] = (dv * s).astype(MX)
            dc_ref[1, rows, :] = (dv * a * s * (1.0 - s)).astype(MX)
            return carry
        lax.fori_loop(0, t // RC, step, 0)
        _dw_finish(dw_scr, dw_ref, K_CONF)
        db_ref[...] = jnp.sum(db_scr[...], axis=0, keepdims=True)

    return pl.pallas_call(
        body, name="glu_conv_bwd", grid=(D // CW,),
        out_shape=(jax.ShapeDtypeStruct((2, t, D), MX), jax.ShapeDtypeStruct((K_CONF, D), f32),
                   jax.ShapeDtypeStruct((1, D), f32)),
        in_specs=[pl.BlockSpec((t, CW), lambda j: (0, ca + j)), pl.BlockSpec((t, CW), lambda j: (0, cg + j)),
                  pl.BlockSpec((K_CONF, CW), lambda j: (0, j)), pl.BlockSpec((t, CW), lambda j: (0, j))],
        out_specs=(pl.BlockSpec((2, t, CW), lambda j: (0, 0, j)), pl.BlockSpec((K_CONF, CW), lambda j: (0, j)),
                   pl.BlockSpec((1, CW), lambda j: (0, j))),
        scratch_shapes=[pltpu.VMEM((t, CW), f32), pltpu.VMEM((8 * K_CONF, CW), f32), pltpu.VMEM((8, CW), f32)],
        compiler_params=_cp("arbitrary"),
    )(proj, proj, conv_w, d_uconv)


def _ssd_conv_bwd_x(proj, conv_w, conv_b, d_xs, d_y, d_skip_row):
    t = proj.shape[0]
    pad = _pad_of(K_SSD)
    c0 = OFF_XBC // CW

    def body(x_ref, w_ref, b_ref, dxs_ref, dy_ref, dsk_ref, draw_ref, dw_ref, db_ref, dp_scr, dw_scr, db_scr):
        dw_scr[...] = jnp.zeros_like(dw_scr)
        db_scr[...] = jnp.zeros_like(db_scr)

        def first(r, carry):
            rows = pl.ds(pl.multiple_of(r * RC, RC), RC)
            win = _causal_win(x_ref, r, t, pad)
            pre = _conv_taps(win, w_ref, K_SSD, pad) + b_ref[...]
            dpre = (dxs_ref[rows, :] + dy_ref[rows, :] * dsk_ref[...]) * _dsilu(pre)
            dp_scr[rows, :] = dpre
            _dw_accumulate(dw_scr, dpre, win, K_SSD, pad)
            db_scr[...] += _rows8(dpre)
            return carry
        lax.fori_loop(0, t // RC, first, 0)

        def second(r, carry):
            rows = pl.ds(pl.multiple_of(r * RC, RC), RC)
            draw_ref[rows, :] = _corr_taps(_anti_win(dp_scr, r, t, pad), w_ref, K_SSD).astype(MX)
            return carry
        lax.fori_loop(0, t // RC, second, 0)
        _dw_finish(dw_scr, dw_ref, K_SSD)
        db_ref[...] = jnp.sum(db_scr[...], axis=0, keepdims=True)

    cb = pl.BlockSpec((t, CW), lambda j: (0, j))
    return pl.pallas_call(
        body, name="ssd_conv_bwd_x", grid=(D // CW,),
        out_shape=(jax.ShapeDtypeStruct((t, D), MX), jax.ShapeDtypeStruct((K_SSD, D), f32),
                   jax.ShapeDtypeStruct((1, D), f32)),
        in_specs=[pl.BlockSpec((t, CW), lambda j: (0, c0 + j)), pl.BlockSpec((K_SSD, CW), lambda j: (0, j)),
                  pl.BlockSpec((1, CW), lambda j: (0, j)), cb, cb, pl.BlockSpec((1, CW), lambda j: (0, j))],
        out_specs=(cb, pl.BlockSpec((K_SSD, CW), lambda j: (0, j)), pl.BlockSpec((1, CW), lambda j: (0, j))),
        scratch_shapes=[pltpu.VMEM((t, CW), f32), pltpu.VMEM((8 * K_SSD, CW), f32), pltpu.VMEM((8, CW), f32)],
        compiler_params=_cp("arbitrary"),
    )(proj, conv_w, conv_b, d_xs, d_y, d_skip_row)


def _ssd_conv_bwd_bc(proj, conv_w, conv_b, d_bc):
    t = proj.shape[0]
    pad = _pad_of(K_SSD)
    c0 = (OFF_XBC + D) // CW
    w0 = D // CW

    def body(x_ref, w_ref, b_ref, dbc_ref, draw_ref, dw_ref, db_ref, dp_scr, dw_scr, db_scr):
        dw_scr[...] = jnp.zeros_like(dw_scr)
        db_scr[...] = jnp.zeros_like(db_scr)

        def first(r, carry):
            rows = pl.ds(pl.multiple_of(r * RC, RC), RC)
            win = _causal_win(x_ref, r, t, pad)
            pre = _conv_taps(win, w_ref, K_SSD, pad) + b_ref[...]
            dpre = dbc_ref[0, rows, :] * _dsilu(pre)
            dp_scr[rows, :] = dpre
            _dw_accumulate(dw_scr, dpre, win, K_SSD, pad)
            db_scr[...] += _rows8(dpre)
            return carry
        lax.fori_loop(0, t // RC, first, 0)

        def second(r, carry):
            rows = pl.ds(pl.multiple_of(r * RC, RC), RC)
            draw_ref[rows, :] = _corr_taps(_anti_win(dp_scr, r, t, pad), w_ref, K_SSD).astype(MX)
            return carry
        lax.fori_loop(0, t // RC, second, 0)
        _dw_finish(dw_scr, dw_ref, K_SSD)
        db_ref[...] = jnp.sum(db_scr[...], axis=0, keepdims=True)

    return pl.pallas_call(
        body, name="ssd_conv_bwd_bc", grid=(2,),
        out_shape=(jax.ShapeDtypeStruct((t, 2 * CW), MX), jax.ShapeDtypeStruct((K_SSD, 2 * CW), f32),
                   jax.ShapeDtypeStruct((1, 2 * CW), f32)),
        in_specs=[pl.BlockSpec((t, CW), lambda j: (0, c0 + j)), pl.BlockSpec((K_SSD, CW), lambda j: (0, w0 + j)),
                  pl.BlockSpec((1, CW), lambda j: (0, w0 + j)), pl.BlockSpec((1, t, CW), lambda j: (j, 0, 0))],
        out_specs=(pl.BlockSpec((t, CW), lambda j: (0, j)), pl.BlockSpec((K_SSD, CW), lambda j: (0, j)),
                   pl.BlockSpec((1, CW), lambda j: (0, j))),
        scratch_shapes=[pltpu.VMEM((t, CW), f32), pltpu.VMEM((8 * K_SSD, CW), f32), pltpu.VMEM((8, CW), f32)],
        compiler_params=_cp("arbitrary"),
    )(proj, conv_w, conv_b, d_bc)


def _chunk_masks():
    ii = lax.broadcasted_iota(jnp.int32, (CHUNK, CHUNK), 0)
    jj = lax.broadcasted_iota(jnp.int32, (CHUNK, CHUNK), 1)
    return ii == jj, jj <= ii, jj >= ii


def _to_row(col, eye):
    return jnp.sum(jnp.where(eye, col, 0.0), axis=0, keepdims=True)


def _to_col(row, eye):
    return jnp.sum(jnp.where(eye, row, 0.0), axis=1, keepdims=True)


def _head_decay(dt_h, a_h, eye, tril):
    a_row = _to_row(dt_h * a_h, eye)
    cs = jnp.sum(jnp.where(tril, a_row, 0.0), axis=1, keepdims=True)
    cs_row = _to_row(cs, eye)
    decay = jnp.where(tril, jnp.exp(jnp.where(tril, cs - cs_row, 0.0)), 0.0)
    total = jnp.sum(a_row, axis=1, keepdims=True)
    return cs, decay, total


def _lane_pick(mat, lane, which):
    return jnp.sum(jnp.where(lane == which, mat, 0.0), axis=1, keepdims=True)


def _ssd_fwd(xbc_act, proj, dt_bias_row, a_log_row):
    t = xbc_act.shape[0]
    nc = t // CHUNK
    cb, cc, cdt = D // LANES, (D + 2 * STATE_N) // LANES, OFF_DT // LANES

    def body(x_ref, b_ref, c_ref, dt_ref, dtb_ref, alog_ref, y_ref, st_ref):
        j = pl.program_id(0)
        eye, tril, _ = _chunk_masks()
        lane = lax.broadcasted_iota(jnp.int32, (1, LANES), 1)
        first = lane < HEAD_P
        a_row = -jnp.exp(alog_ref[...])
        a_heads = [jnp.sum(jnp.where(lane == 2 * j + h, a_row, 0.0), axis=1, keepdims=True) for h in range(2)]

        def chunk(c, hprev):
            rows = pl.ds(pl.multiple_of(c * CHUNK, CHUNK), CHUNK)
            xv, bm, cm = x_ref[rows, :], b_ref[rows, :], c_ref[rows, :]
            dt = _softplus(dt_ref[rows, :] + dtb_ref[...])
            st_ref[c] = hprev
            g = _mm_nt(cm, bm)
            ch = _mm(cm, hprev)
            dts = [_lane_pick(dt, lane, 2 * j + h) for h in range(2)]
            xdt = xv * jnp.where(first, dts[0], dts[1])
            ys, hs = [], []
            for h in range(2):
                cs, decay, total = _head_decay(dts[h], a_heads[h], eye, tril)
                y = _mm(g * decay, xdt) + jnp.exp(cs) * ch
                s = _mm_tn(bm * jnp.exp(total - cs), xdt)
                ys.append(y)
                hs.append(jnp.exp(total) * hprev + s)
            y_ref[rows, :] = jnp.where(first, ys[0], ys[1])
            return jnp.where(first, hs[0], hs[1])

        lax.fori_loop(0, nc, chunk, jnp.zeros((STATE_N, LANES), f32))

    blk = lambda f: pl.BlockSpec((t, LANES), f)
    return pl.pallas_call(
        body, name="ssd_fwd", grid=(D // LANES,),
        out_shape=(jax.ShapeDtypeStruct((t, D), f32), jax.ShapeDtypeStruct((nc, STATE_N, D), f32)),
        in_specs=[blk(lambda j: (0, j)), blk(lambda j: (0, cb + j // 4)), blk(lambda j: (0, cc + j // 4)),
                  blk(lambda j: (0, cdt)), _row(LANES), _row(LANES)],
        out_specs=(blk(lambda j: (0, j)), pl.BlockSpec((nc, STATE_N, LANES), lambda j: (0, 0, j))),
        compiler_params=_cp("arbitrary"),
    )(xbc_act, xbc_act, xbc_act, proj, dt_bias_row, a_log_row)


def _ssd_bwd(xbc_act, proj, dt_bias_row, a_log_row, states, d_y):
    t = xbc_act.shape[0]
    nc = t // CHUNK
    cb, cc, cdt = D // LANES, (D + 2 * STATE_N) // LANES, OFF_DT // LANES

    def body(x_ref, b_ref, c_ref, dt_ref, dtb_ref, alog_ref, st_ref, dy_ref, dx_ref, dbc_ref, ddt_ref, da_ref):
        grp, p = pl.program_id(0), pl.program_id(1)
        j = 4 * grp + p
        eye, tril, triu = _chunk_masks()
        lane = lax.broadcasted_iota(jnp.int32, (1, LANES), 1)
        first = lane < HEAD_P
        last_row = lax.broadcasted_iota(jnp.int32, (CHUNK, 1), 0) == CHUNK - 1
        a_row = -jnp.exp(alog_ref[...])
        a_heads = [jnp.sum(jnp.where(lane == 2 * j + h, a_row, 0.0), axis=1, keepdims=True) for h in range(2)]

        @pl.when(p == 0)
        def _():
            dbc_ref[...] = jnp.zeros_like(dbc_ref)

        @pl.when(j == 0)
        def _():
            ddt_ref[...] = jnp.zeros_like(ddt_ref)
            da_ref[...] = jnp.zeros_like(da_ref)

        def chunk(i, dh):
            c = nc - 1 - i
            rows = pl.ds(pl.multiple_of(c * CHUNK, CHUNK), CHUNK)
            xv, bm, cm = x_ref[rows, :], b_ref[rows, :], c_ref[rows, :]
            dtr = dt_ref[rows, :] + dtb_ref[...]
            dt = _softplus(dtr)
            hprev = st_ref[c]
            dy = dy_ref[rows, :]
            g = _mm_nt(cm, bm)
            dts = [_lane_pick(dt, lane, 2 * j + h) for h in range(2)]
            xdt = xv * jnp.where(first, dts[0], dts[1])
            dxs, dhs = [], []
            db_sum, dc_sum = None, None
            ddt_mat = jnp.zeros((CHUNK, LANES), f32)
            da_acc = jnp.zeros((1, LANES), f32)
            for h in range(2):
                mine = first if h == 0 else jnp.logical_not(first)
                cs, decay, total = _head_decay(dts[h], a_heads[h], eye, tril)
                e_cs, e_tot = jnp.exp(cs), jnp.exp(total)
                dec_s = jnp.exp(total - cs)
                dyh = jnp.where(mine, dy, 0.0)
                xdth = jnp.where(mine, xdt, 0.0)
                dhh = jnp.where(mine, dh, 0.0)
                hph = jnp.where(mine, hprev, 0.0)
                m = g * decay
                dm = _mm_nt(dyh, xdth)
                dg = dm * decay
                w = dm * m
                bdec = bm * dec_s
                dxdt = _mm_tn(m, dyh) + _mm(bdec, dhh)
                dc_off = _mm_nt(dyh, hph) * e_cs
                db_s = _mm_nt(xdth, dhh) * dec_s
                dc_h = _mm(dg, bm) + dc_off
                db_h = _mm_tn(dg, cm) + db_s
                r_s = jnp.sum(db_s * bm, axis=1, keepdims=True)
                dtotal = jnp.sum(r_s, axis=0, keepdims=True) + e_tot * jnp.sum(
                    jnp.sum(dhh * hph, axis=1, keepdims=True), axis=0, keepdims=True)
                dcs = (jnp.sum(w, axis=1, keepdims=True) - _to_col(jnp.sum(w, axis=0, keepdims=True), eye)
                       + jnp.sum(dc_off * cm, axis=1, keepdims=True) - r_s + jnp.where(last_row, dtotal, 0.0))
                da_col = jnp.sum(jnp.where(triu, _to_row(dcs, eye), 0.0), axis=1, keepdims=True)
                ddt = da_col * a_heads[h] + jnp.sum(jnp.where(mine, dxdt * xv, 0.0), axis=1, keepdims=True)
                ddt_mat = ddt_mat + jnp.where(lane == 2 * j + h, ddt, 0.0)
                da_acc = da_acc + jnp.where(lane == 2 * j + h, jnp.sum(da_col * dts[h], axis=0, keepdims=True), 0.0)
                dxs.append(dxdt * dts[h])
                dhs.append(e_tot * dhh + _mm_tn(cm * e_cs, dyh))
                db_sum = db_h if db_sum is None else db_sum + db_h
                dc_sum = dc_h if dc_sum is None else dc_sum + dc_h
            dx_ref[rows, :] = jnp.where(first, dxs[0], dxs[1])
            dbc_ref[0, rows, :] += db_sum
            dbc_ref[1, rows, :] += dc_sum
            ddt_ref[rows, :] += ddt_mat * jax.nn.sigmoid(dtr)
            da_ref[...] += da_acc * a_row
            return jnp.where(first, dhs[0], dhs[1])

        lax.fori_loop(0, nc, chunk, jnp.zeros((STATE_N, LANES), f32))

    blk = lambda f: pl.BlockSpec((t, LANES), f)
    return pl.pallas_call(
        body, name="ssd_bwd", grid=(2, 4),
        out_shape=(jax.ShapeDtypeStruct((t, D), f32), jax.ShapeDtypeStruct((2, t, 2 * STATE_N), f32),
                   jax.ShapeDtypeStruct((t, LANES), f32), jax.ShapeDtypeStruct((1, LANES), f32)),
        in_specs=[blk(lambda g, p: (0, 4 * g + p)), blk(lambda g, p: (0, cb + g)), blk(lambda g, p: (0, cc + g)),
                  blk(lambda g, p: (0, cdt)), _row(LANES), _row(LANES),
                  pl.BlockSpec((nc, STATE_N, LANES), lambda g, p: (0, 0, 4 * g + p)), blk(lambda g, p: (0, 4 * g + p))],
        out_specs=(blk(lambda g, p: (0, 4 * g + p)), pl.BlockSpec((2, t, LANES), lambda g, p: (0, 0, g)),
                   blk(lambda g, p: (0, 0)), _row(LANES)),
        compiler_params=_cp("arbitrary", "arbitrary"),
    )(xbc_act, xbc_act, xbc_act, proj, dt_bias_row, a_log_row, states, d_y)


def _up_bwd(d_up, w_up, x1, mod, norm2_w, dx2, mix, w_out):
    t = x1.shape[0]

    def body(dup_ref, wu_ref, x1_ref, mod_ref, nw_ref, dx2_ref, mix_ref, wo_ref,
             dx1_ref, dmix_ref, dys_ref, du_ref, st_ref):
        @pl.when(pl.program_id(0) == 0)
        def _():
            st_ref[...] = jnp.zeros_like(st_ref)

        nt = (((1,), (1,)), ((), ()))
        dh = None
        for k in range(4):
            lo = (k % 2) * UP_SHARD
            part = lax.dot_general(dup_ref[k // 2, :, lo:lo + UP_SHARD], wu_ref[k], nt, preferred_element_type=f32)
            dh = part if dh is None else dh + part
        x1 = x1_ref[...]
        rstd = lax.rsqrt(jnp.mean(x1 * x1, axis=-1, keepdims=True) + 1e-6)
        xh = x1 * rstd
        nw = nw_ref[...]
        sc = 1.0 + mod_ref[:, 4 * D:5 * D]
        st_ref[0:1, :] += jnp.sum(dh, axis=0, keepdims=True)
        st_ref[1:2, :] += jnp.sum(dh * xh * nw, axis=0, keepdims=True)
        st_ref[2:3, :] += jnp.sum(dh * sc * xh, axis=0, keepdims=True)
        dxh = dh * sc * nw
        dx1 = dx2_ref[...] + rstd * (dxh - xh * jnp.mean(dxh * xh, axis=-1, keepdims=True))
        dx1_ref[...] = dx1
        st_ref[3:4, :] += jnp.sum(dx1 * mix_ref[...], axis=0, keepdims=True)
        dmix = (mod_ref[:, 2 * D:3 * D] * dx1).astype(MX)
        dmix_ref[...] = dmix
        dys_ref[...] = lax.dot_general(dmix, wo_ref[0:D, :], nt, preferred_element_type=f32)
        du_ref[...] = lax.dot_general(dmix, wo_ref[D:2 * D, :], nt, preferred_element_type=f32)

    blk = pl.BlockSpec((TM, D), lambda i: (i, 0))
    return pl.pallas_call(
        body, name="up_bwd", grid=(t // TM,),
        out_shape=(jax.ShapeDtypeStruct((t, D), f32), jax.ShapeDtypeStruct((t, D), MX),
                   jax.ShapeDtypeStruct((t, D), f32), jax.ShapeDtypeStruct((t, D), f32),
                   jax.ShapeDtypeStruct((8, D), f32)),
        in_specs=[pl.BlockSpec((2, TM, D_FF), lambda i: (0, i, 0)), _resident((4, D, UP_SHARD)), blk, _row(6 * D), _row(),
                  blk, blk, _resident((2 * D, D))],
        out_specs=(blk, blk, blk, blk, pl.BlockSpec((8, D), lambda i: (0, 0))),
        compiler_params=_cp("arbitrary"),
    )(d_up, w_up, x1, mod, norm2_w, dx2, mix, w_out)


def _ln_silu_bwd(d_u, u_conv, ln_w, ln_b):
    t = d_u.shape[0]

    def body(du_ref, u_ref, w_ref, b_ref, o_ref, st_ref):
        @pl.when(pl.program_id(0) == 0)
        def _():
            st_ref[...] = jnp.zeros_like(st_ref)

        u = u_ref[...]
        mu = jnp.mean(u, axis=-1, keepdims=True)
        uc = u - mu
        rstd = lax.rsqrt(jnp.mean(uc * uc, axis=-1, keepdims=True) + 1e-5)
        n = uc * rstd
        w = w_ref[...]
        dl = du_ref[...] * _dsilu(n * w + b_ref[...])
        st_ref[0:1, :] += jnp.sum(dl * n, axis=0, keepdims=True)
        st_ref[1:2, :] += jnp.sum(dl, axis=0, keepdims=True)
        dn = dl * w
        o_ref[...] = rstd * (dn - jnp.mean(dn, axis=-1, keepdims=True) - n * jnp.mean(dn * n, axis=-1, keepdims=True))

    blk = pl.BlockSpec((TM, D), lambda i: (i, 0))
    return pl.pallas_call(
        body, name="ln_silu_bwd", grid=(t // TM,),
        out_shape=(jax.ShapeDtypeStruct((t, D), f32), jax.ShapeDtypeStruct((8, D), f32)),
        in_specs=[blk, blk, _row(), _row()], out_specs=(blk, pl.BlockSpec((8, D), lambda i: (0, 0))),
        compiler_params=_cp("arbitrary"),
    )(d_u, u_conv, ln_w, ln_b)


def _ssd_gate_norm_bwd(d_out, y_scan, xbc_act, proj, d_skip_row, ssd_norm_w):
    t = d_out.shape[0]

    def body(do_ref, y_ref, xs_ref, z_ref, dsk_ref, nw_ref, dy_ref, dz_ref, st_ref):
        @pl.when(pl.program_id(0) == 0)
        def _():
            st_ref[...] = jnp.zeros_like(st_ref)

        xs = xs_ref[...]
        y = y_ref[...] + xs * dsk_ref[...]
        z = z_ref[...]
        s = _silu(z)
        yz = y * s
        rstd = lax.rsqrt(jnp.mean(yz * yz, axis=-1, keepdims=True) + 1e-6)
        n = yz * rstd
        do = do_ref[...]
        st_ref[0:1, :] += jnp.sum(do * n, axis=0, keepdims=True)
        dn = do * nw_ref[...]
        dyz = rstd * (dn - n * jnp.mean(dn * n, axis=-1, keepdims=True))
        dy = dyz * s
        dy_ref[...] = dy
        dz_ref[...] = (dyz * y * _dsilu(z)).astype(MX)
        st_ref[1:2, :] += jnp.sum(dy * xs, axis=0, keepdims=True)

    blk = pl.BlockSpec((TM, D), lambda i: (i, 0))
    return pl.pallas_call(
        body, name="ssd_gate_norm_bwd", grid=(t // TM,),
        out_shape=(jax.ShapeDtypeStruct((t, D), f32), jax.ShapeDtypeStruct((t, D), MX), jax.ShapeDtypeStruct((8, D), f32)),
        in_specs=[blk, blk, blk, blk, _row(), _row()], out_specs=(blk, blk, pl.BlockSpec((8, D), lambda i: (0, 0))),
        compiler_params=_cp("arbitrary"),
    )(d_out, y_scan, xbc_act, proj, d_skip_row, ssd_norm_w)


def _inproj_bwd(d_z, d_xraw, d_bcraw, d_conf, d_dt, w_pack, x, mod, norm1_w, dx1):
    t = x.shape[0]

    def body(dz_ref, dx_ref, dbc_ref, dcf_ref, ddt_ref, w_ref, x_ref, mod_ref, nw_ref, dx1_ref, gx_ref, st_ref):
        @pl.when(pl.program_id(0) == 0)
        def _():
            st_ref[...] = jnp.zeros_like(st_ref)

        nt = (((1,), (1,)), ((), ()))
        dot = lambda a, lo, hi: lax.dot_general(a, w_ref[:, lo:hi], nt, preferred_element_type=f32)
        dh = dot(dz_ref[...], OFF_Z, OFF_Z + D)
        dh = dh + dot(dx_ref[...], OFF_XBC, OFF_XBC + D)
        dh = dh + dot(dbc_ref[...], OFF_XBC + D, OFF_XBC + D_XBC)
        dh = dh + dot(dcf_ref[0], OFF_CA, OFF_CA + D)
        dh = dh + dot(dcf_ref[1], OFF_CG, OFF_CG + D)
        dh = dh + dot(ddt_ref[...].astype(MX), OFF_DT, OFF_DT + LANES)
        st_ref[3:4, 0:LANES] += jnp.sum(ddt_ref[...], axis=0, keepdims=True)
        xv = x_ref[...]
        rstd = lax.rsqrt(jnp.mean(xv * xv, axis=-1, keepdims=True) + 1e-6)
        xh = xv * rstd
        nw = nw_ref[...]
        sc = 1.0 + mod_ref[:, D:2 * D]
        st_ref[0:1, :] += jnp.sum(dh, axis=0, keepdims=True)
        st_ref[1:2, :] += jnp.sum(dh * xh * nw, axis=0, keepdims=True)
        st_ref[2:3, :] += jnp.sum(dh * sc * xh, axis=0, keepdims=True)
        dxh = dh * sc * nw
        gx_ref[...] = dx1_ref[...] + rstd * (dxh - xh * jnp.mean(dxh * xh, axis=-1, keepdims=True))

    blk = pl.BlockSpec((TM, D), lambda i: (i, 0))
    return pl.pallas_call(
        body, name="inproj_bwd", grid=(t // TM,),
        out_shape=(jax.ShapeDtypeStruct((t, D), f32), jax.ShapeDtypeStruct((8, D), f32)),
        in_specs=[blk, blk, pl.BlockSpec((TM, 2 * CW), lambda i: (i, 0)), pl.BlockSpec((2, TM, D), lambda i: (0, i, 0)),
                  pl.BlockSpec((TM, LANES), lambda i: (i, 0)), _resident((D, W_PACK)), blk, _row(6 * D), _row(), blk],
        out_specs=(blk, pl.BlockSpec((8, D), lambda i: (0, 0))),
        compiler_params=_cp("arbitrary"),
    )(d_z, d_xraw, d_bcraw, d_conf, d_dt, w_pack, x, mod, norm1_w, dx1)


def _wgrad(a, d, name, bn=256):
    t, k = a.shape
    n = d.shape[1]
    out_dtype = MX

    def body(a_ref, d_ref, o_ref):
        o_ref[...] = lax.dot_general(a_ref[...], d_ref[...].astype(MX), (((0,), (0,)), ((), ())),
                                     preferred_element_type=f32).astype(out_dtype)

    return pl.pallas_call(
        body, name=name, grid=(n // bn,), out_shape=jax.ShapeDtypeStruct((k, n), out_dtype),
        in_specs=[_resident((t, k)), pl.BlockSpec((t, bn), lambda j: (0, j))],
        out_specs=pl.BlockSpec((k, bn), lambda j: (0, j)), compiler_params=_cp("arbitrary"),
    )(a, d)


def _wgrad_stacked(a, d, name, bn):
    out_dtype = MX
    t, k = a.shape
    s, _, n = d.shape
    nb = n // bn

    def body(a_ref, d_ref, o_ref):
        o_ref[0] = lax.dot_general(a_ref[...], d_ref[0], (((0,), (0,)), ((), ())),
                                   preferred_element_type=f32).astype(out_dtype)

    return pl.pallas_call(
        body, name=name, grid=(s, nb), out_shape=jax.ShapeDtypeStruct((s * nb, k, bn), out_dtype),
        in_specs=[_resident((t, k)), pl.BlockSpec((1, t, bn), lambda i, j: (i, 0, j))],
        out_specs=pl.BlockSpec((1, k, bn), lambda i, j: (i * nb + j, 0, 0)), compiler_params=_cp("arbitrary", "arbitrary"),
    )(a, d)


def _pad_row(v, width=LANES):
    return jnp.pad(v.reshape(1, -1), ((0, 0), (0, width - v.size)))


def _local_step(x, mod, target, w_pack, w_out, w_up, w_down, small):
    dtb_row, alog_row = _pad_row(small["dt_bias"]), _pad_row(small["a_log"])
    dskip_row = jnp.repeat(small["d_skip"].reshape(-1), HEAD_P).reshape(1, D)

    proj, h = _ln_inproj(x, mod, small["norm1_w"], w_pack)
    xbc_act = _ssd_conv_fwd(proj, small["ssd_conv_w"], small["ssd_conv_b"])
    y_scan, states = _ssd_fwd(xbc_act, proj, dtb_row, alog_row)
    y_ssd = _ssd_gate_norm(y_scan, xbc_act, proj, dskip_row, small["ssd_norm_w"])
    u_conv = _glu_conv_fwd(proj, small["conf_conv_w"], small["conf_conv_b"])
    u = _ln_silu(u_conv, small["conf_ln_w"], small["conf_ln_b"])
    mix, x1, h2, up = _outproj_ln2_up(y_ssd, u, w_out, x, mod, small["norm2_w"], w_up)
    act = _ffn_conv_fwd(up, small["ffn_conv_w"], small["ffn_conv_b"])
    dx2, d_ffn, d_act, st_down = _down_loss(act, w_down, x1, mod, small["final_norm_w"], target)

    g_down = _wgrad(act, d_ffn, "wgrad_down")
    d_up, dw_ffn, db_ffn = _ffn_conv_bwd(up, small["ffn_conv_w"], small["ffn_conv_b"], d_act)
    g_up = _wgrad_stacked(h2, d_up, "wgrad_up", D_FF // 2)
    dx1, d_mix, d_yssd, d_u, st_up = _up_bwd(d_up, w_up, x1, mod, small["norm2_w"], dx2, mix, w_out)
    g_out_y = _wgrad(y_ssd, d_mix, "wgrad_out_y")
    g_out_u = _wgrad(u, d_mix, "wgrad_out_u")
    d_uconv, st_ln = _ln_silu_bwd(d_u, u_conv, small["conf_ln_w"], small["conf_ln_b"])
    d_conf, dw_conf, db_conf = _glu_conv_bwd(proj, small["conf_conv_w"], d_uconv)
    d_y, d_z, st_gn = _ssd_gate_norm_bwd(d_yssd, y_scan, xbc_act, proj, dskip_row, small["ssd_norm_w"])
    d_xs, d_bc, d_dt, d_alog = _ssd_bwd(xbc_act, proj, dtb_row, alog_row, states, d_y)
    d_xraw, dw_sx, db_sx = _ssd_conv_bwd_x(proj, small["ssd_conv_w"], small["ssd_conv_b"], d_xs, d_y, dskip_row)
    d_bcraw, dw_sbc, db_sbc = _ssd_conv_bwd_bc(proj, small["ssd_conv_w"], small["ssd_conv_b"], d_bc)
    grad_x, st_in = _inproj_bwd(d_z, d_xraw, d_bcraw, d_conf, d_dt, w_pack, x, mod, small["norm1_w"], dx1)
    g_in = dict(z=_wgrad(h, d_z, "wgrad_in_z"), x=_wgrad(h, d_xraw, "wgrad_in_x"), bc=_wgrad(h, d_bcraw, "wgrad_in_bc"),
                conf=_wgrad_stacked(h, d_conf, "wgrad_in_conf", D), dt=_wgrad(h, d_dt, "wgrad_in_dt", bn=LANES))

    d_mod = jnp.concatenate([st_in[0:1], st_in[1:2], st_up[3:4], st_up[0:1], st_up[1:2], st_down[1:2]], axis=1)
    gsmall = dict(
        norm1_w=st_in[2:3], ssd_conv_w=jnp.concatenate([dw_sx, dw_sbc], axis=1),
        ssd_conv_b=jnp.concatenate([db_sx, db_sbc], axis=1), dt_bias=st_in[3:4, 0:HEADS], a_log=d_alog[:, 0:HEADS],
        d_skip=st_gn[1].reshape(HEADS, HEAD_P).sum(axis=1).reshape(1, HEADS), ssd_norm_w=st_gn[0:1],
        conf_conv_w=dw_conf, conf_conv_b=db_conf, conf_ln_w=st_ln[0:1], conf_ln_b=st_ln[1:2], norm2_w=st_up[2:3],
        ffn_conv_w=jnp.concatenate([dw_ffn[0], dw_ffn[1]], axis=1), ffn_conv_b=jnp.concatenate([db_ffn[0], db_ffn[1]], axis=1),
        final_norm_w=st_down[0:1], mod=d_mod)
    gbig = dict(w_in=g_in, w_out=(g_out_y, g_out_u), w_up=g_up, w_down=g_down)
    return st_down[2, 0], grad_x, gbig, gsmall


W_IN_COLS = 4624
W_IN_SHARD = W_IN_COLS // 4
W_IN_SHARD_PAD = 1280
_SEGMENTS = ((0, 1024, OFF_Z), (1024, 2560, OFF_XBC), (2560, 2576, OFF_DT), (2576, 3600, OFF_CA), (3600, 4624, OFF_CG))


def _in_pieces(bounds=()):
    out = []
    for k in range(4):
        s0, s1 = k * W_IN_SHARD, (k + 1) * W_IN_SHARD
        for lo, hi, off in _SEGMENTS:
            a, b = max(lo, s0), min(hi, s1)
            while a < b:
                p = off + a - lo
                e = min([b - a] + [c - p for c in bounds if c > p])
                out.append((k, a - s0, p, e))
                a += e
    return out


def _pack_w_in(shards):
    pieces = _in_pieces()

    def body(s_ref, o_ref):
        o_ref[:, OFF_DT:W_PACK] = jnp.zeros((TM, W_PACK - OFF_DT), MX)
        for k, c, p, n in pieces:
            o_ref[:, p:p + n] = s_ref[k, :, c:c + n]

    return pl.pallas_call(
        body, name="pack_w_in", grid=(D // TM,), out_shape=jax.ShapeDtypeStruct((D, W_PACK), MX),
        in_specs=[pl.BlockSpec((4, TM, W_IN_SHARD_PAD), lambda i: (0, i, 0))],
        out_specs=pl.BlockSpec((TM, W_PACK), lambda i: (i, 0)), compiler_params=_cp("arbitrary"),
    )(shards)


def _unpack_g_in(g):
    srcs = ((OFF_Z, D), (OFF_XBC, D), (OFF_XBC + D, 2 * CW), (OFF_CA, D), (OFF_CG, D), (OFF_DT, LANES))
    pieces = _in_pieces(tuple(o for o, _ in srcs) + tuple(o + n for o, n in srcs))

    def body(z_ref, x_ref, bc_ref, cf_ref, dt_ref, o_ref):
        read = (lambda lo, hi: z_ref[:, lo:hi], lambda lo, hi: x_ref[:, lo:hi], lambda lo, hi: bc_ref[:, lo:hi],
                lambda lo, hi: cf_ref[0, :, lo:hi], lambda lo, hi: cf_ref[1, :, lo:hi], lambda lo, hi: dt_ref[:, lo:hi])
        o_ref[:, :, W_IN_SHARD - 4:W_IN_SHARD_PAD] = jnp.zeros((4, TM, W_IN_SHARD_PAD - W_IN_SHARD + 4), MX)
        for k, c, p, n in pieces:
            i = [q for q, (o, w) in enumerate(srcs) if o <= p < o + w][0]
            o_ref[k, :, c:c + n] = read[i](p - srcs[i][0], p - srcs[i][0] + n)

    blk = lambda w: pl.BlockSpec((TM, w), lambda i: (i, 0))
    return pl.pallas_call(
        body, name="unpack_g_in", grid=(D // TM,), out_shape=jax.ShapeDtypeStruct((4, D, W_IN_SHARD_PAD), MX),
        in_specs=[blk(D), blk(D), blk(2 * CW), pl.BlockSpec((2, TM, D), lambda i: (0, i, 0)), blk(LANES)],
        out_specs=pl.BlockSpec((4, TM, W_IN_SHARD_PAD), lambda i: (0, i, 0)), compiler_params=_cp("arbitrary"),
    )(g["z"], g["x"], g["bc"], g["conf"], g["dt"])


def _cast_pad(w, width):
    r, c = w.shape
    tm = TM if r % TM == 0 else r

    def body(w_ref, o_ref):
        v = w_ref[...].astype(MX)
        o_ref[...] = v if width == c else jnp.concatenate([v, jnp.zeros((tm, width - c), MX)], axis=1)

    return pl.pallas_call(
        body, name=f"cast_pad_{r}x{c}", grid=(r // tm,), out_shape=jax.ShapeDtypeStruct((r, width), MX),
        in_specs=[pl.BlockSpec((tm, c), lambda i: (i, 0))], out_specs=pl.BlockSpec((tm, width), lambda i: (i, 0)),
        compiler_params=_cp("arbitrary"),
    )(w)


ANY = pl.BlockSpec(memory_space=pl.ANY)


def _place():
    x, y, c = lax.axis_index("x"), lax.axis_index("y"), lax.axis_index("c")
    return x, y, c, [(1 - x, y), (x, 1 - y), (1 - x, 1 - y)]


def _gather_rows(block):
    m_per, n = block.shape

    def body(x_ref, out_ref, send_sems, recv_sems, local_sem):
        x, y, c, chips = _place()
        me, sibling = (x, y, c), (x, y, 1 - c)

        def rows(px, py, pc):
            return out_ref.at[pl.ds((4 * px + 2 * py + pc) * m_per, m_per), :]

        def copy(k, blk, to, src=None):
            return pltpu.make_async_remote_copy(
                src_ref=rows(*blk) if src is None else src, dst_ref=rows(*blk), send_sem=send_sems.at[k],
                recv_sem=recv_sems.at[k], device_id=to, device_id_type=MESH)

        mine = pltpu.make_async_copy(x_ref, rows(*me), local_sem)
        mine.start()
        first = [copy(0, me, sibling, src=x_ref)]
        first += [copy(1 + j, me, (*chip, c), src=x_ref) for j, chip in enumerate(chips)]
        for cp in first:
            cp.start()
        passed = [copy(4 + j, (*chip, c), sibling) for j, chip in enumerate(chips)]
        for j, chip in enumerate(chips):
            copy(1 + j, (*chip, c), me).wait_recv()
            passed[j].start()
        copy(0, sibling, me).wait_recv()
        for j, chip in enumerate(chips):
            copy(4 + j, (*chip, 1 - c), me).wait_recv()
        for cp in first + passed:
            cp.wait_send()
        mine.wait()

    return pl.pallas_call(
        body, name=f"gather_rows_{m_per}x{n}", out_shape=jax.ShapeDtypeStruct((8 * m_per, n), block.dtype),
        in_specs=[pl.BlockSpec(memory_space=pltpu.VMEM)], out_specs=pl.BlockSpec(memory_space=pltpu.VMEM),
        scratch_shapes=[pltpu.SemaphoreType.DMA((7,)), pltpu.SemaphoreType.DMA((7,)), pltpu.SemaphoreType.DMA],
        compiler_params=pltpu.CompilerParams(vmem_limit_bytes=VMEM_LIMIT),
    )(block)


def _gather_weights(shards):
    n = len(shards)

    def body(*refs):
        ins, outs = refs[:n], refs[n:2 * n]
        send_sems, recv_sems, local_sems = refs[2 * n:]
        x, y, c, chips = _place()
        k_me = 2 * x + y
        sibling = (x, y, 1 - c)
        local = [pltpu.make_async_copy(ins[a], outs[a].at[k_me], local_sems.at[a]) for a in range(n)]
        for cp in local:
            cp.start()

        def copy(a, j, k, half, to, src=None):
            dst = outs[a].at[k, half]
            return pltpu.make_async_remote_copy(
                src_ref=dst if src is None else src, dst_ref=dst, send_sem=send_sems.at[a, j],
                recv_sem=recv_sems.at[a, j], device_id=to, device_id_type=MESH)

        first = [copy(a, j, k_me, c, (*chip, c), src=ins[a].at[c]) for a in range(n) for j, chip in enumerate(chips)]
        for cp in first:
            cp.start()
        passed = []
        for a in range(n):
            for j, (px, py) in enumerate(chips):
                copy(a, j, 2 * px + py, c, (x, y, c)).wait_recv()
                fwd = copy(a, 3 + j, 2 * px + py, c, sibling)
                fwd.start()
                passed.append(fwd)
        for a in range(n):
            for j, (px, py) in enumerate(chips):
                copy(a, 3 + j, 2 * px + py, 1 - c, (x, y, c)).wait_recv()
        for cp in first + passed:
            cp.wait_send()
        for cp in local:
            cp.wait()

    return pl.pallas_call(
        body, name="gather_weights",
        out_shape=tuple(jax.ShapeDtypeStruct((4,) + s.shape, s.dtype) for s in shards),
        in_specs=[ANY] * n, out_specs=tuple([ANY] * n),
        scratch_shapes=[pltpu.SemaphoreType.DMA((n, 6)), pltpu.SemaphoreType.DMA((n, 6)), pltpu.SemaphoreType.DMA((n,))],
    )(*shards)


def _swap_halves(grads):
    n = len(grads)

    def body(*refs):
        ins, kept, got = refs[:n], refs[n:2 * n], refs[2 * n:3 * n]
        send_sems, recv_sems, local_sems = refs[3 * n:]
        x, y, c, _ = _place()
        sibling = (x, y, 1 - c)
        local = [pltpu.make_async_copy(ins[a].at[k, c], kept[a].at[k], local_sems.at[a, k]) for a in range(n) for k in range(4)]
        sent = [pltpu.make_async_remote_copy(
            src_ref=ins[a].at[k, 1 - c], dst_ref=got[a].at[k], send_sem=send_sems.at[a, k], recv_sem=recv_sems.at[a, k],
            device_id=sibling, device_id_type=MESH) for a in range(n) for k in range(4)]
        for cp in local + sent:
            cp.start()
        for cp in sent:
            cp.wait()
        for cp in local:
            cp.wait()

    return pl.pallas_call(
        body, name="swap_halves",
        out_shape=tuple(jax.ShapeDtypeStruct((4,) + g.shape[2:], g.dtype) for g in grads) * 2,
        in_specs=[ANY] * n, out_specs=tuple([ANY] * (2 * n)),
        scratch_shapes=[pltpu.SemaphoreType.DMA((n, 4)), pltpu.SemaphoreType.DMA((n, 4)), pltpu.SemaphoreType.DMA((n, 4))],
    )(*grads)


def _scatter_chips(parts):
    n = len(parts)

    def body(*refs):
        ins, own, others = refs[:n], refs[n:2 * n], refs[2 * n:3 * n]
        send_sems, recv_sems, local_sems = refs[3 * n:]
        x, y, c, chips = _place()
        local = [pltpu.make_async_copy(ins[a].at[2 * x + y], own[a], local_sems.at[a]) for a in range(n)]
        sent = [pltpu.make_async_remote_copy(
            src_ref=ins[a].at[2 * px + py], dst_ref=others[a].at[j], send_sem=send_sems.at[a, j],
            recv_sem=recv_sems.at[a, j], device_id=(px, py, c), device_id_type=MESH)
            for a in range(n) for j, (px, py) in enumerate(chips)]
        for cp in local + sent:
            cp.start()
        for cp in sent:
            cp.wait()
        for cp in local:
            cp.wait()

    return pl.pallas_call(
        body, name="scatter_chips",
        out_shape=tuple(jax.ShapeDtypeStruct(p.shape[1:], p.dtype) for p in parts)
        + tuple(jax.ShapeDtypeStruct((3,) + p.shape[1:], p.dtype) for p in parts),
        in_specs=[ANY] * n, out_specs=tuple([ANY] * (2 * n)),
        scratch_shapes=[pltpu.SemaphoreType.DMA((n, 3)), pltpu.SemaphoreType.DMA((n, 3)), pltpu.SemaphoreType.DMA((n,))],
    )(*parts)


def _join_halves(halves):
    n = len(halves)

    def body(*refs):
        ins, outs = refs[:n], refs[n:2 * n]
        send_sems, recv_sems, local_sems = refs[2 * n:]
        x, y, c, _ = _place()
        local = [pltpu.make_async_copy(ins[a], outs[a].at[c], local_sems.at[a]) for a in range(n)]
        sent = [pltpu.make_async_remote_copy(
            src_ref=ins[a], dst_ref=outs[a].at[c], send_sem=send_sems.at[a], recv_sem=recv_sems.at[a],
            device_id=(x, y, 1 - c), device_id_type=MESH) for a in range(n)]
        for cp in local + sent:
            cp.start()
        for a in range(n):
            pltpu.make_async_remote_copy(
                src_ref=ins[a], dst_ref=outs[a].at[1 - c], send_sem=send_sems.at[a], recv_sem=recv_sems.at[a],
                device_id=(x, y, 1 - c), device_id_type=MESH).wait()
        for cp in local:
            cp.wait()

    return pl.pallas_call(
        body, name="join_halves",
        out_shape=tuple(jax.ShapeDtypeStruct((2,) + s.shape, s.dtype) for s in halves),
        in_specs=[ANY] * n, out_specs=tuple([ANY] * n),
        scratch_shapes=[pltpu.SemaphoreType.DMA((n,)), pltpu.SemaphoreType.DMA((n,)), pltpu.SemaphoreType.DMA((n,))],
    )(*halves)


def _row_tile(r):
    for tm in (TM, 176, 128, 64, 32, 16, 8):
        if r % tm == 0:
            return tm
    return r


def _add_pair(a, b):
    k, h, c = a.shape
    tm = _row_tile(h)

    def body(a_ref, b_ref, o_ref):
        o_ref[...] = (a_ref[...].astype(f32) + b_ref[...].astype(f32)).astype(MX)

    blk = pl.BlockSpec((1, tm, c), lambda i, j: (i, j, 0))
    return pl.pallas_call(
        body, name=f"add_pair_{h}x{c}", grid=(k, h // tm), out_shape=jax.ShapeDtypeStruct(a.shape, MX),
        in_specs=[blk, blk], out_specs=blk, compiler_params=_cp("arbitrary", "arbitrary"),
    )(a, b)


def _add_chips(own, others):
    h, c = own.shape
    tm = _row_tile(h)

    def body(a_ref, b_ref, o_ref):
        s = a_ref[...].astype(f32) + b_ref[0].astype(f32)
        o_ref[...] = (s + b_ref[1].astype(f32)) + b_ref[2].astype(f32)

    return pl.pallas_call(
        body, name=f"add_chips_{h}x{c}", grid=(h // tm,), out_shape=jax.ShapeDtypeStruct((h, c), f32),
        in_specs=[pl.BlockSpec((tm, c), lambda i: (i, 0)), pl.BlockSpec((3, tm, c), lambda i: (0, i, 0))],
        out_specs=pl.BlockSpec((tm, c), lambda i: (i, 0)), compiler_params=_cp("arbitrary"),
    )(own, others)


def _adam_math(w, g, m, v):
    m = ADAM_B1 * m + (1.0 - ADAM_B1) * g
    v = ADAM_B2 * v + (1.0 - ADAM_B2) * (g * g)
    m_hat = m / (1.0 - ADAM_B1 ** ADAM_STEP)
    v_hat = v / (1.0 - ADAM_B2 ** ADAM_STEP)
    return -ADAM_LR * (m_hat / (jnp.sqrt(v_hat) + ADAM_EPS) + ADAM_WD * w), m, v


def _adamw(w, g, m, v, name):
    r, c = w.shape
    tm = _row_tile(r)

    def body(w_ref, g_ref, m_ref, v_ref, d_ref, nm_ref, nv_ref):
        d_ref[...], nm_ref[...], nv_ref[...] = _adam_math(w_ref[...], g_ref[...], m_ref[...], v_ref[...])

    blk = pl.BlockSpec((tm, c), lambda i: (i, 0))
    return pl.pallas_call(
        body, name=name, grid=(r // tm,), out_shape=tuple([jax.ShapeDtypeStruct((r, c), f32)] * 3),
        in_specs=[blk] * 4, out_specs=(blk,) * 3, compiler_params=_cp("arbitrary"),
    )(w, g, m, v)


def _ada_forward(c_all, ada_w):
    def body(c_ref, w_ref, o_ref):
        o_ref[...] = jnp.dot(_silu(c_ref[...]).astype(MX), w_ref[...].astype(MX), preferred_element_type=f32)

    return pl.pallas_call(body, name="ada_forward", out_shape=jax.ShapeDtypeStruct((8, ada_w.shape[1]), f32),
                          compiler_params=pltpu.CompilerParams(vmem_limit_bytes=VMEM_LIMIT))(c_all, ada_w)


def _ada_adamw(c_all_t, d_mod, w, m, v):
    r, c = w.shape
    tm = TM

    def body(ct_ref, dm_ref, w_ref, m_ref, v_ref, g_ref, d_ref, nm_ref, nv_ref):
        ca = _silu(ct_ref[...])
        g = ca[:, 0:1] * dm_ref[0:1, :]
        for b in range(1, 8):
            g = g + ca[:, b:b + 1] * dm_ref[b:b + 1, :]
        g_ref[...] = g
        d_ref[...], nm_ref[...], nv_ref[...] = _adam_math(w_ref[...], g, m_ref[...], v_ref[...])

    blk = pl.BlockSpec((tm, c), lambda i: (i, 0))
    return pl.pallas_call(
        body, name="ada_adamw", grid=(r // tm,), out_shape=tuple([jax.ShapeDtypeStruct((r, c), f32)] * 4),
        in_specs=[pl.BlockSpec((tm, 8), lambda i: (i, 0)), pl.BlockSpec((8, c), lambda i: (0, 0)), blk, blk, blk],
        out_specs=(blk,) * 4, compiler_params=_cp("arbitrary"),
    )(c_all_t, d_mod, w, m, v)


def _sum_devices(rows):
    n = rows.shape[1]

    def body(r_ref, o_ref):
        s = r_ref[0:8, :]
        for d in range(1, 8):
            s = s + r_ref[8 * d:8 * d + 8, :]
        o_ref[...] = s

    return pl.pallas_call(body, name="sum_devices", out_shape=jax.ShapeDtypeStruct((8, n), f32),
                          compiler_params=pltpu.CompilerParams(vmem_limit_bytes=VMEM_LIMIT))(rows)


WEIGHTS = ("ada_w", "ada_b", "norm1_w", "w_in", "ssd_conv_w", "ssd_conv_b", "dt_bias", "a_log", "d_skip", "ssd_norm_w",
           "conf_conv_w", "conf_conv_b", "conf_ln_w", "conf_ln_b", "w_out", "norm2_w", "w_up", "ffn_conv_w", "ffn_conv_b",
           "w_down", "final_norm_w")
VECTORS = ("ada_b", "norm1_w", "ssd_conv_b", "dt_bias", "a_log", "d_skip", "ssd_norm_w", "conf_conv_b", "conf_ln_w",
           "conf_ln_b", "norm2_w", "ffn_conv_b", "final_norm_w")
CONVS = {"ssd_conv_w": (K_SSD, D_XBC), "conf_conv_w": (K_CONF, D), "ffn_conv_w": (K_FFN, 2 * D_FF)}


def _pack8(pieces):
    flat = jnp.concatenate([p.reshape(-1) for p in pieces])
    n = -(-flat.size // (8 * LANES)) * LANES
    return jnp.pad(flat, (0, 8 * n - flat.size)).reshape(8, n)


def _unpack(flat, sizes):
    out, o = [], 0
    for n in sizes:
        out.append(flat[o:o + n])
        o += n
    return out


def kernel(x, c, ada_w, ada_b, norm1_w, w_in, ssd_conv_w, ssd_conv_b, dt_bias, a_log, d_skip, ssd_norm_w, conf_conv_w, conf_conv_b, conf_ln_w, conf_ln_b, w_out, norm2_w, w_up, ffn_conv_w, ffn_conv_b, w_down, final_norm_w, loss_target, m_ada_w, m_ada_b, m_norm1_w, m_w_in, m_ssd_conv_w, m_ssd_conv_b, m_dt_bias, m_a_log, m_d_skip, m_ssd_norm_w, m_conf_conv_w, m_conf_conv_b, m_conf_ln_w, m_conf_ln_b, m_w_out, m_norm2_w, m_w_up, m_ffn_conv_w, m_ffn_conv_b, m_w_down, m_final_norm_w, v_ada_w, v_ada_b, v_norm1_w, v_w_in, v_ssd_conv_w, v_ssd_conv_b, v_dt_bias, v_a_log, v_d_skip, v_ssd_norm_w, v_conf_conv_w, v_conf_conv_b, v_conf_ln_w, v_conf_ln_b, v_w_out, v_norm2_w, v_w_up, v_ffn_conv_w, v_ffn_conv_b, v_w_down, v_final_norm_w):
    given = dict(locals())
    w = {n: given[n] for n in WEIGHTS}
    mom = {n: given["m_" + n] for n in WEIGHTS}
    var = {n: given["v_" + n] for n in WEIGHTS}
    chip = 2 * lax.axis_index("x") + lax.axis_index("y")
    me = 2 * chip + lax.axis_index("c")

    sent = _pack8([c] + [w[n] for n in CONVS])
    got = _gather_rows(sent).reshape(8, -1)
    c_all = got[:, 0:D]
    o = D
    conv_full = {}
    for n, (taps, cols) in CONVS.items():
        per = taps * cols // 4
        shards = got[0::2, o:o + per].reshape(4, taps, cols // 4)
        conv_full[n] = jnp.concatenate([shards[k] for k in range(4)], axis=1)
        o += per

    mod_cols = _gather_rows(_ada_forward(c_all, ada_w[0])).reshape(8, 8, -1)[0::2]
    mod = lax.dynamic_index_in_dim(mod_cols, me, axis=1, keepdims=False).reshape(1, 6 * D) + ada_b

    halves = lambda a: a.reshape(2, a.shape[0] // 2, a.shape[1])
    a_in, a_out, a_up, a_down = _gather_weights([
        halves(_cast_pad(w_in[0], W_IN_SHARD_PAD)), halves(_cast_pad(w_out[0], D)),
        halves(_cast_pad(w_up[0], UP_SHARD)), halves(_cast_pad(w_down[0], D))])
    w_pack = _pack_w_in(a_in.reshape(4, D, W_IN_SHARD_PAD))

    small = {n: w[n].reshape(1, -1) for n in VECTORS if n != "ada_b"}
    small.update(conv_full)
    loss_mine, grad_x, gbig, gsmall = _local_step(
        x[0], mod, loss_target[0], w_pack, a_out.reshape(2 * D, D), a_up.reshape(4, D, UP_SHARD), a_down.reshape(D_FF, D), small)
    loss = lax.psum(loss_mine, ("x", "y", "c"))

    quarters = lambda a: a.reshape(4, 2, a.shape[0] // 8, a.shape[1])
    g_up = gbig["w_up"]
    local = [_unpack_g_in(gbig["w_in"]).reshape(4, 2, D // 2, W_IN_SHARD_PAD), quarters(jnp.concatenate(gbig["w_out"], axis=0)),
             g_up.reshape(4, 2, D // 2, UP_SHARD), quarters(gbig["w_down"])]
    swapped = _swap_halves(local)
    per_chip = [_add_pair(a, b) for a, b in zip(swapped[:4], swapped[4:])]
    scattered = _scatter_chips(per_chip)
    summed = [_add_chips(a, b) for a, b in zip(scattered[:4], scattered[4:])]
    big_grads = [g.reshape(2 * g.shape[1], g.shape[2]) for g in _join_halves(summed)]
    big_grads[0] = big_grads[0][:, 0:W_IN_SHARD]
    grads = dict(zip(("w_in", "w_out", "w_up", "w_down"), big_grads))

    sizes = [6 * D] + [w[n].size for n in VECTORS[1:]] + [taps * cols for taps, cols in CONVS.values()]
    sent = _pack8([gsmall["mod"]] + [gsmall[n] for n in VECTORS[1:]] + [gsmall[n] for n in CONVS])
    got = _gather_rows(sent)
    d_mod_all = got.reshape(8, -1)[:, 0:6 * D]
    pieces = _unpack(_sum_devices(got).reshape(-1), sizes)
    for n, g in zip(VECTORS, pieces[:len(VECTORS)]):
        grads[n] = g.reshape(w[n].shape)
    for (n, (taps, cols)), g in zip(CONVS.items(), pieces[len(VECTORS):]):
        grads[n] = lax.dynamic_slice_in_dim(g.reshape(taps, cols), chip * (cols // 4), cols // 4, axis=1).reshape(w[n].shape)

    delta, new_m, new_v = {}, {}, {}
    for n in ("w_in", "w_out", "w_up", "w_down"):
        res = _adamw(w[n][0], grads[n], mom[n][0], var[n][0], "adamw_" + n)
        grads[n] = grads[n][None]
        delta[n], new_m[n], new_v[n] = [r[None] for r in res]
    d_mod_mine = lax.dynamic_slice_in_dim(d_mod_all, chip * (6 * D // 4), 6 * D // 4, axis=1)
    res = _ada_adamw(c_all.T, d_mod_mine, ada_w[0], m_ada_w[0], v_ada_w[0])
    grads["ada_w"], delta["ada_w"], new_m["ada_w"], new_v["ada_w"] = [r[None] for r in res]
    names = VECTORS + tuple(CONVS)
    res = _adamw(*[_pack8([d[n] for n in names]) for d in (w, grads, mom, var)], "adamw_small")
    sizes = [w[n].size for n in names]
    for d, r in zip((delta, new_m, new_v), res):
        for n, piece in zip(names, _unpack(r.reshape(-1), sizes)):
            d[n] = piece.reshape(w[n].shape)

    return (loss, grad_x[None], *[grads[n] for n in WEIGHTS], *[delta[n] for n in WEIGHTS],
            *[new_m[n] for n in WEIGHTS], *[new_v[n] for n in WEIGHTS])
```

```python
import functools

import jax
import jax.numpy as jnp
from jax import lax
from jax.experimental import pallas as pl
from jax.experimental.pallas import tpu as pltpu

f32 = jnp.float32
MX = jnp.bfloat16

D = 1024
HEADS = 16
HEAD_P = 64
STATE_N = 128
D_XBC = 1536
D_FF = 2816
UP_SHARD = 2 * D_FF // 4
K_SSD, K_CONF, K_FFN = 4, 31, 3
CHUNK = 64
OFF_Z, OFF_XBC, OFF_CA, OFF_CG, OFF_DT = 0, 1024, 2560, 3584, 4608
W_PACK = 4736
TM = 256
CW = 256
RC = 64
LANES = 128
VMEM_LIMIT = 56 * 1024 * 1024

ADAM_LR, ADAM_B1, ADAM_B2, ADAM_EPS, ADAM_WD, ADAM_STEP = 0.001, 0.9, 0.999, 1e-08, 0.01, 10

MESH = pl.DeviceIdType.MESH


def _cp(*sem):
    return pltpu.CompilerParams(dimension_semantics=sem, vmem_limit_bytes=VMEM_LIMIT)


def _resident(shape):
    nd = len(shape)
    return pl.BlockSpec(shape, lambda *_: (0,) * nd, pipeline_mode=pl.Buffered(1))


def _row(width=D):
    return pl.BlockSpec((1, width), lambda *_: (0, 0))


def _silu(v):
    return v * jax.nn.sigmoid(v)


def _dsilu(v):
    s = jax.nn.sigmoid(v)
    return s * (1.0 + v * (1.0 - s))


def _softplus(v):
    return jnp.maximum(v, 0.0) + jnp.log1p(jnp.exp(-jnp.abs(v)))


def _mm(a, b):
    return jnp.dot(a.astype(MX), b.astype(MX), preferred_element_type=f32)


def _mm_nt(a, b):
    return lax.dot_general(a.astype(MX), b.astype(MX), (((1,), (1,)), ((), ())), preferred_element_type=f32)


def _mm_tn(a, b):
    return lax.dot_general(a.astype(MX), b.astype(MX), (((0,), (0,)), ((), ())), preferred_element_type=f32)


def _ln_inproj(x, mod, norm1_w, w_pack):
    t = x.shape[0]

    def body(x_ref, mod_ref, nw_ref, w_ref, proj_ref, h_ref):
        xv = x_ref[...]
        rstd = lax.rsqrt(jnp.mean(xv * xv, axis=-1, keepdims=True) + 1e-6)
        h = (xv * rstd * nw_ref[...]) * (1.0 + mod_ref[:, D:2 * D]) + mod_ref[:, 0:D]
        hb = h.astype(MX)
        h_ref[...] = hb
        proj_ref[...] = jnp.dot(hb, w_ref[...], preferred_element_type=f32)

    return pl.pallas_call(
        body, name="ln_inproj", grid=(t // TM,),
        out_shape=(jax.ShapeDtypeStruct((t, W_PACK), f32), jax.ShapeDtypeStruct((t, D), MX)),
        in_specs=[pl.BlockSpec((TM, D), lambda i: (i, 0)), _row(6 * D), _row(), _resident((D, W_PACK))],
        out_specs=(pl.BlockSpec((TM, W_PACK), lambda i: (i, 0)), pl.BlockSpec((TM, D), lambda i: (i, 0))),
        compiler_params=_cp("arbitrary"),
    )(x, mod, norm1_w, w_pack)


def _ssd_gate_norm(y_scan, xbc_act, proj, d_skip_row, ssd_norm_w):
    t = y_scan.shape[0]

    def body(y_ref, xs_ref, z_ref, dsk_ref, nw_ref, o_ref):
        y = y_ref[...] + xs_ref[...] * dsk_ref[...]
        yz = y * _silu(z_ref[...])
        rstd = lax.rsqrt(jnp.mean(yz * yz, axis=-1, keepdims=True) + 1e-6)
        o_ref[...] = (yz * rstd * nw_ref[...]).astype(MX)

    blk = pl.BlockSpec((TM, D), lambda i: (i, 0))
    return pl.pallas_call(
        body, name="ssd_gate_norm", grid=(t // TM,), out_shape=jax.ShapeDtypeStruct((t, D), MX),
        in_specs=[blk, blk, blk, _row(), _row()], out_specs=blk, compiler_params=_cp("arbitrary"),
    )(y_scan, xbc_act, proj, d_skip_row, ssd_norm_w)


def _ln_silu(u_conv, ln_w, ln_b):
    t = u_conv.shape[0]

    def body(u_ref, w_ref, b_ref, o_ref):
        u = u_ref[...]
        mu = jnp.mean(u, axis=-1, keepdims=True)
        uc = u - mu
        rstd = lax.rsqrt(jnp.mean(uc * uc, axis=-1, keepdims=True) + 1e-5)
        o_ref[...] = _silu(uc * rstd * w_ref[...] + b_ref[...]).astype(MX)

    blk = pl.BlockSpec((TM, D), lambda i: (i, 0))
    return pl.pallas_call(
        body, name="ln_silu", grid=(t // TM,), out_shape=jax.ShapeDtypeStruct((t, D), MX),
        in_specs=[blk, _row(), _row()], out_specs=blk, compiler_params=_cp("arbitrary"),
    )(u_conv, ln_w, ln_b)


def _outproj_ln2_up(y_ssd, u, w_out, x, mod, norm2_w, w_up):
    t = x.shape[0]

    def body(y_ref, u_ref, wo_ref, x_ref, mod_ref, nw_ref, wu_ref, mix_ref, x1_ref, h2_ref, up_ref):
        mix = jnp.dot(y_ref[...], wo_ref[0:D, :], preferred_element_type=f32)
        mix = mix + jnp.dot(u_ref[...], wo_ref[D:2 * D, :], preferred_element_type=f32)
        mix_ref[...] = mix
        x1 = x_ref[...] + mod_ref[:, 2 * D:3 * D] * mix
        x1_ref[...] = x1
        rstd = lax.rsqrt(jnp.mean(x1 * x1, axis=-1, keepdims=True) + 1e-6)
        h2 = ((x1 * rstd * nw_ref[...]) * (1.0 + mod_ref[:, 4 * D:5 * D]) + mod_ref[:, 3 * D:4 * D]).astype(MX)
        h2_ref[...] = h2
        for k in range(4):
            up_ref[:, k * UP_SHARD:(k + 1) * UP_SHARD] = jnp.dot(h2, wu_ref[k], preferred_element_type=f32)

    blk = pl.BlockSpec((TM, D), lambda i: (i, 0))
    return pl.pallas_call(
        body, name="outproj_ln2_up", grid=(t // TM,),
        out_shape=(jax.ShapeDtypeStruct((t, D), f32), jax.ShapeDtypeStruct((t, D), f32),
                   jax.ShapeDtypeStruct((t, D), MX), jax.ShapeDtypeStruct((t, 2 * D_FF), f32)),
        in_specs=[blk, blk, _resident((2 * D, D)), blk, _row(6 * D), _row(), _resident((4, D, UP_SHARD))],
        out_specs=(blk, blk, blk, pl.BlockSpec((TM, 2 * D_FF), lambda i: (i, 0))),
        compiler_params=_cp("arbitrary"),
    )(y_ssd, u, w_out, x, mod, norm2_w, w_up)


def _down_loss(act, w_down, x1, mod, final_norm_w, target):
    t = x1.shape[0]

    def body(a_ref, wd_ref, x1_ref, mod_ref, wf_ref, tgt_ref, dx2_ref, dffn_ref, dact_ref, st_ref):
        @pl.when(pl.program_id(0) == 0)
        def _():
            st_ref[...] = jnp.zeros_like(st_ref)

        g2 = mod_ref[:, 5 * D:6 * D]
        ffn = jnp.dot(a_ref[...], wd_ref[...], preferred_element_type=f32)
        x2 = x1_ref[...] + g2 * ffn
        rstd = lax.rsqrt(jnp.mean(x2 * x2, axis=-1, keepdims=True) + 1e-6)
        xh = x2 * rstd
        wf = wf_ref[...]
        err = xh * wf - tgt_ref[...]
        dy = err * (1.0 / D)
        dxh = dy * wf
        dx2 = rstd * (dxh - xh * jnp.mean(dxh * xh, axis=-1, keepdims=True))
        dx2_ref[...] = dx2
        dffn = (g2 * dx2).astype(MX)
        dffn_ref[...] = dffn
        dact_ref[...] = lax.dot_general(dffn, wd_ref[...], (((1,), (1,)), ((), ())), preferred_element_type=f32)
        st_ref[0:1, :] += jnp.sum(dy * xh, axis=0, keepdims=True)
        st_ref[1:2, :] += jnp.sum(dx2 * ffn, axis=0, keepdims=True)
        st_ref[2:3, :] += jnp.sum(0.5 * jnp.mean(err * err, axis=-1, keepdims=True), axis=0, keepdims=True)

    blk = pl.BlockSpec((TM, D), lambda i: (i, 0))
    ablk = pl.BlockSpec((TM, D_FF), lambda i: (i, 0))
    return pl.pallas_call(
        body, name="down_loss", grid=(t // TM,),
        out_shape=(jax.ShapeDtypeStruct((t, D), f32), jax.ShapeDtypeStruct((t, D), MX),
                   jax.ShapeDtypeStruct((t, D_FF), f32), jax.ShapeDtypeStruct((8, D), f32)),
        in_specs=[ablk, _resident((D_FF, D)), blk, _row(6 * D), _row(), blk],
        out_specs=(blk, blk, ablk, pl.BlockSpec((8, D), lambda i: (0, 0))),
        compiler_params=_cp("arbitrary"),
    )(act, w_down, x1, mod, final_norm_w, target)


def _pad_of(k):
    return 8 * ((k - 1 + 7) // 8)


def _causal_win(ref, r, t, pad):
    base = pl.multiple_of(r * RC, RC)
    prev = ref[pl.ds(pl.multiple_of(jnp.maximum(base - pad, 0), 8), pad), :]
    prev = jnp.where(r > 0, prev, 0.0)
    return jnp.concatenate([prev, ref[pl.ds(base, RC), :]], axis=0)


def _anti_win(ref, r, t, pad):
    base = pl.multiple_of(r * RC, RC)
    nxt = ref[pl.ds(pl.multiple_of(jnp.minimum(base + RC, t - pad), 8), pad), :]
    nxt = jnp.where(r < t // RC - 1, nxt, 0.0)
    return jnp.concatenate([ref[pl.ds(base, RC), :], nxt], axis=0)


def _conv_taps(win, w_ref, k, pad):
    acc = None
    for j in range(k):
        o = pad - (k - 1) + j
        term = w_ref[j:j + 1, :] * win[o:o + RC, :]
        acc = term if acc is None else acc + term
    return acc


def _corr_taps(win, w_ref, k):
    acc = None
    for j in range(k):
        o = (k - 1) - j
        term = w_ref[j:j + 1, :] * win[o:o + RC, :]
        acc = term if acc is None else acc + term
    return acc


def _dw_accumulate(dw_scr, d, win, k, pad):
    for j in range(k):
        o = pad - (k - 1) + j
        prod = d * win[o:o + RC, :]
        dw_scr[8 * j:8 * j + 8, :] += prod.reshape(RC // 8, 8, prod.shape[-1]).sum(axis=0)


def _dw_finish(dw_scr, dw_ref, k):
    for j in range(k):
        dw_ref[j:j + 1, :] = jnp.sum(dw_scr[8 * j:8 * j + 8, :], axis=0, keepdims=True)


def _rows8(v):
    return v.reshape(RC // 8, 8, v.shape[-1]).sum(axis=0)


def _ssd_conv_fwd(proj, conv_w, conv_b):
    t = proj.shape[0]
    pad = _pad_of(K_SSD)
    c0 = OFF_XBC // CW

    def body(x_ref, w_ref, b_ref, o_ref):
        def step(r, carry):
            win = _causal_win(x_ref, r, t, pad)
            o_ref[pl.ds(pl.multiple_of(r * RC, RC), RC), :] = _silu(_conv_taps(win, w_ref, K_SSD, pad) + b_ref[...])
            return carry
        lax.fori_loop(0, t // RC, step, 0)

    return pl.pallas_call(
        body, name="ssd_conv_fwd", grid=(D_XBC // CW,), out_shape=jax.ShapeDtypeStruct((t, D_XBC), f32),
        in_specs=[pl.BlockSpec((t, CW), lambda j: (0, c0 + j)), pl.BlockSpec((K_SSD, CW), lambda j: (0, j)),
                  pl.BlockSpec((1, CW), lambda j: (0, j))],
        out_specs=pl.BlockSpec((t, CW), lambda j: (0, j)), compiler_params=_cp("arbitrary"),
    )(proj, conv_w, conv_b)


def _glu_conv_fwd(proj, conv_w, conv_b):
    t = proj.shape[0]
    pad = _pad_of(K_CONF)
    ca, cg = OFF_CA // CW, OFF_CG // CW

    def body(a_ref, g_ref, w_ref, b_ref, o_ref, v_scr):
        def glu(r, carry):
            rows = pl.ds(pl.multiple_of(r * RC, RC), RC)
            v_scr[rows, :] = a_ref[rows, :] * jax.nn.sigmoid(g_ref[rows, :])
            return carry
        lax.fori_loop(0, t // RC, glu, 0)

        def step(r, carry):
            win = _causal_win(v_scr, r, t, pad)
            o_ref[pl.ds(pl.multiple_of(r * RC, RC), RC), :] = _conv_taps(win, w_ref, K_CONF, pad) + b_ref[...]
            return carry
        lax.fori_loop(0, t // RC, step, 0)

    return pl.pallas_call(
        body, name="glu_conv_fwd", grid=(D // CW,), out_shape=jax.ShapeDtypeStruct((t, D), f32),
        in_specs=[pl.BlockSpec((t, CW), lambda j: (0, ca + j)), pl.BlockSpec((t, CW), lambda j: (0, cg + j)),
                  pl.BlockSpec((K_CONF, CW), lambda j: (0, j)), pl.BlockSpec((1, CW), lambda j: (0, j))],
        out_specs=pl.BlockSpec((t, CW), lambda j: (0, j)),
        scratch_shapes=[pltpu.VMEM((t, CW), f32)], compiler_params=_cp("arbitrary"),
    )(proj, proj, conv_w, conv_b)


def _ffn_conv_fwd(up, conv_w, conv_b):
    t = up.shape[0]
    pad = _pad_of(K_FFN)
    nb = D_FF // CW

    def body(g_ref, v_ref, wg_ref, wv_ref, bg_ref, bv_ref, o_ref):
        def step(r, carry):
            gc = _conv_taps(_causal_win(g_ref, r, t, pad), wg_ref, K_FFN, pad) + bg_ref[...]
            vc = _conv_taps(_causal_win(v_ref, r, t, pad), wv_ref, K_FFN, pad) + bv_ref[...]
            o_ref[pl.ds(pl.multiple_of(r * RC, RC), RC), :] = (_silu(gc) * vc).astype(MX)
            return carry
        lax.fori_loop(0, t // RC, step, 0)

    return pl.pallas_call(
        body, name="ffn_conv_fwd", grid=(nb,), out_shape=jax.ShapeDtypeStruct((t, D_FF), MX),
        in_specs=[pl.BlockSpec((t, CW), lambda j: (0, j)), pl.BlockSpec((t, CW), lambda j: (0, nb + j)),
                  pl.BlockSpec((K_FFN, CW), lambda j: (0, j)), pl.BlockSpec((K_FFN, CW), lambda j: (0, nb + j)),
                  pl.BlockSpec((1, CW), lambda j: (0, j)), pl.BlockSpec((1, CW), lambda j: (0, nb + j))],
        out_specs=pl.BlockSpec((t, CW), lambda j: (0, j)), compiler_params=_cp("arbitrary"),
    )(up, up, conv_w, conv_w, conv_b, conv_b)


def _ffn_conv_bwd(up, conv_w, conv_b, d_act):
    t = up.shape[0]
    pad = _pad_of(K_FFN)
    nb = D_FF // CW

    def body(g_ref, v_ref, wg_ref, wv_ref, bg_ref, bv_ref, da_ref, dup_ref, dw_ref, db_ref,
             dg_scr, dv_scr, dwg_scr, dwv_scr, db_scr):
        dwg_scr[...] = jnp.zeros_like(dwg_scr)
        dwv_scr[...] = jnp.zeros_like(dwv_scr)
        db_scr[...] = jnp.zeros_like(db_scr)

        def first(r, carry):
            rows = pl.ds(pl.multiple_of(r * RC, RC), RC)
            gwin = _causal_win(g_ref, r, t, pad)
            vwin = _causal_win(v_ref, r, t, pad)
            gc = _conv_taps(gwin, wg_ref, K_FFN, pad) + bg_ref[...]
            vc = _conv_taps(vwin, wv_ref, K_FFN, pad) + bv_ref[...]
            da = da_ref[rows, :]
            dgc = da * vc * _dsilu(gc)
            dvc = da * _silu(gc)
            dg_scr[rows, :] = dgc
            dv_scr[rows, :] = dvc
            _dw_accumulate(dwg_scr, dgc, gwin, K_FFN, pad)
            _dw_accumulate(dwv_scr, dvc, vwin, K_FFN, pad)
            db_scr[0:8, :] += _rows8(dgc)
            db_scr[8:16, :] += _rows8(dvc)
            return carry
        lax.fori_loop(0, t // RC, first, 0)

        def second(r, carry):
            rows = pl.ds(pl.multiple_of(r * RC, RC), RC)
            dup_ref[0, rows, :] = _corr_taps(_anti_win(dg_scr, r, t, pad), wg_ref, K_FFN).astype(MX)
            dup_ref[1, rows, :] = _corr_taps(_anti_win(dv_scr, r, t, pad), wv_ref, K_FFN).astype(MX)
            return carry
        lax.fori_loop(0, t // RC, second, 0)

        for j in range(K_FFN):
            dw_ref[0, j:j + 1, :] = jnp.sum(dwg_scr[8 * j:8 * j + 8, :], axis=0, keepdims=True)
            dw_ref[1, j:j + 1, :] = jnp.sum(dwv_scr[8 * j:8 * j + 8, :], axis=0, keepdims=True)
        db_ref[0] = jnp.sum(db_scr[0:8, :], axis=0, keepdims=True)
        db_ref[1] = jnp.sum(db_scr[8:16, :], axis=0, keepdims=True)

    return pl.pallas_call(
        body, name="ffn_conv_bwd", grid=(nb,),
        out_shape=(jax.ShapeDtypeStruct((2, t, D_FF), MX), jax.ShapeDtypeStruct((2, K_FFN, D_FF), f32),
                   jax.ShapeDtypeStruct((2, 1, D_FF), f32)),
        in_specs=[pl.BlockSpec((t, CW), lambda j: (0, j)), pl.BlockSpec((t, CW), lambda j: (0, nb + j)),
                  pl.BlockSpec((K_FFN, CW), lambda j: (0, j)), pl.BlockSpec((K_FFN, CW), lambda j: (0, nb + j)),
                  pl.BlockSpec((1, CW), lambda j: (0, j)), pl.BlockSpec((1, CW), lambda j: (0, nb + j)),
                  pl.BlockSpec((t, CW), lambda j: (0, j))],
        out_specs=(pl.BlockSpec((2, t, CW), lambda j: (0, 0, j)), pl.BlockSpec((2, K_FFN, CW), lambda j: (0, 0, j)),
                   pl.BlockSpec((2, 1, CW), lambda j: (0, 0, j))),
        scratch_shapes=[pltpu.VMEM((t, CW), f32), pltpu.VMEM((t, CW), f32), pltpu.VMEM((8 * K_FFN, CW), f32),
                        pltpu.VMEM((8 * K_FFN, CW), f32), pltpu.VMEM((16, CW), f32)],
        compiler_params=_cp("arbitrary"),
    )(up, up, conv_w, conv_w, conv_b, conv_b, d_act)


def _glu_conv_bwd(proj, conv_w, d_uconv):
    t = proj.shape[0]
    pad = _pad_of(K_CONF)
    ca, cg = OFF_CA // CW, OFF_CG // CW

    def body(a_ref, g_ref, w_ref, du_ref, dc_ref, dw_ref, db_ref, v_scr, dw_scr, db_scr):
        dw_scr[...] = jnp.zeros_like(dw_scr)
        db_scr[...] = jnp.zeros_like(db_scr)

        def glu(r, carry):
            rows = pl.ds(pl.multiple_of(r * RC, RC), RC)
            v_scr[rows, :] = a_ref[rows, :] * jax.nn.sigmoid(g_ref[rows, :])
            return carry
        lax.fori_loop(0, t // RC, glu, 0)

        def step(r, carry):
            rows = pl.ds(pl.multiple_of(r * RC, RC), RC)
            du = du_ref[rows, :]
            _dw_accumulate(dw_scr, du, _causal_win(v_scr, r, t, pad), K_CONF, pad)
            db_scr[...] += _rows8(du)
            dv = _corr_taps(_anti_win(du_ref, r, t, pad), w_ref, K_CONF)
            a = a_ref[rows, :]
            s = jax.nn.sigmoid(g_ref[rows, :])
            dc_ref[0, rows, :] = (dv * s).astype(MX)
            dc_ref[1, rows, :] = (dv * a * s * (1.0 - s)).astype(MX)
            return carry
        lax.fori_loop(0, t // RC, step, 0)
        _dw_finish(dw_scr, dw_ref, K_CONF)
        db_ref[...] = jnp.sum(db_scr[...], axis=0, keepdims=True)

    return pl.pallas_call(
        body, name="glu_conv_bwd", grid=(D // CW,),
        out_shape=(jax.ShapeDtypeStruct((2, t, D), MX), jax.ShapeDtypeStruct((K_CONF, D), f32),
                   jax.ShapeDtypeStruct((1, D), f32)),
        in_specs=[pl.BlockSpec((t, CW), lambda j: (0, ca + j)), pl.BlockSpec((t, CW), lambda j: (0, cg + j)),
                  pl.BlockSpec((K_CONF, CW), lambda j: (0, j)), pl.BlockSpec((t, CW), lambda j: (0, j))],
        out_specs=(pl.BlockSpec((2, t, CW), lambda j: (0, 0, j)), pl.BlockSpec((K_CONF, CW), lambda j: (0, j)),
                   pl.BlockSpec((1, CW), lambda j: (0, j))),
        scratch_shapes=[pltpu.VMEM((t, CW), f32), pltpu.VMEM((8 * K_CONF, CW), f32), pltpu.VMEM((8, CW), f32)],
        compiler_params=_cp("arbitrary"),
    )(proj, proj, conv_w, d_uconv)


def _ssd_conv_bwd_x(proj, conv_w, conv_b, d_xs, d_y, d_skip_row):
    t = proj.shape[0]
    pad = _pad_of(K_SSD)
    c0 = OFF_XBC // CW

    def body(x_ref, w_ref, b_ref, dxs_ref, dy_ref, dsk_ref, draw_ref, dw_ref, db_ref, dp_scr, dw_scr, db_scr):
        dw_scr[...] = jnp.zeros_like(dw_scr)
        db_scr[...] = jnp.zeros_like(db_scr)

        def first(r, carry):
            rows = pl.ds(pl.multiple_of(r * RC, RC), RC)
            win = _causal_win(x_ref, r, t, pad)
            pre = _conv_taps(win, w_ref, K_SSD, pad) + b_ref[...]
            dpre = (dxs_ref[rows, :] + dy_ref[rows, :] * dsk_ref[...]) * _dsilu(pre)
            dp_scr[rows, :] = dpre
            _dw_accumulate(dw_scr, dpre, win, K_SSD, pad)
            db_scr[...] += _rows8(dpre)
            return carry
        lax.fori_loop(0, t // RC, first, 0)

        def second(r, carry):
            rows = pl.ds(pl.multiple_of(r * RC, RC), RC)
            draw_ref[rows, :] = _corr_taps(_anti_win(dp_scr, r, t, pad), w_ref, K_SSD).astype(MX)
            return carry
        lax.fori_loop(0, t // RC, second, 0)
        _dw_finish(dw_scr, dw_ref, K_SSD)
        db_ref[...] = jnp.sum(db_scr[...], axis=0, keepdims=True)

    cb = pl.BlockSpec((t, CW), lambda j: (0, j))
    return pl.pallas_call(
        body, name="ssd_conv_bwd_x", grid=(D // CW,),
        out_shape=(jax.ShapeDtypeStruct((t, D), MX), jax.ShapeDtypeStruct((K_SSD, D), f32),
                   jax.ShapeDtypeStruct((1, D), f32)),
        in_specs=[pl.BlockSpec((t, CW), lambda j: (0, c0 + j)), pl.BlockSpec((K_SSD, CW), lambda j: (0, j)),
                  pl.BlockSpec((1, CW), lambda j: (0, j)), cb, cb, pl.BlockSpec((1, CW), lambda j: (0, j))],
        out_specs=(cb, pl.BlockSpec((K_SSD, CW), lambda j: (0, j)), pl.BlockSpec((1, CW), lambda j: (0, j))),
        scratch_shapes=[pltpu.VMEM((t, CW), f32), pltpu.VMEM((8 * K_SSD, CW), f32), pltpu.VMEM((8, CW), f32)],
        compiler_params=_cp("arbitrary"),
    )(proj, conv_w, conv_b, d_xs, d_y, d_skip_row)


def _ssd_conv_bwd_bc(proj, conv_w, conv_b, d_bc):
    t = proj.shape[0]
    pad = _pad_of(K_SSD)
    c0 = (OFF_XBC + D) // CW
    w0 = D // CW

    def body(x_ref, w_ref, b_ref, dbc_ref, draw_ref, dw_ref, db_ref, dp_scr, dw_scr, db_scr):
        dw_scr[...] = jnp.zeros_like(dw_scr)
        db_scr[...] = jnp.zeros_like(db_scr)

        def first(r, carry):
            rows = pl.ds(pl.multiple_of(r * RC, RC), RC)
            win = _causal_win(x_ref, r, t, pad)
            pre = _conv_taps(win, w_ref, K_SSD, pad) + b_ref[...]
            dpre = dbc_ref[0, rows, :] * _dsilu(pre)
            dp_scr[rows, :] = dpre
            _dw_accumulate(dw_scr, dpre, win, K_SSD, pad)
            db_scr[...] += _rows8(dpre)
            return carry
        lax.fori_loop(0, t // RC, first, 0)

        def second(r, carry):
            rows = pl.ds(pl.multiple_of(r * RC, RC), RC)
            draw_ref[rows, :] = _corr_taps(_anti_win(dp_scr, r, t, pad), w_ref, K_SSD).astype(MX)
            return carry
        lax.fori_loop(0, t // RC, second, 0)
        _dw_finish(dw_scr, dw_ref, K_SSD)
        db_ref[...] = jnp.sum(db_scr[...], axis=0, keepdims=True)

    return pl.pallas_call(
        body, name="ssd_conv_bwd_bc", grid=(2,),
        out_shape=(jax.ShapeDtypeStruct((t, 2 * CW), MX), jax.ShapeDtypeStruct((K_SSD, 2 * CW), f32),
                   jax.ShapeDtypeStruct((1, 2 * CW), f32)),
        in_specs=[pl.BlockSpec((t, CW), lambda j: (0, c0 + j)), pl.BlockSpec((K_SSD, CW), lambda j: (0, w0 + j)),
                  pl.BlockSpec((1, CW), lambda j: (0, w0 + j)), pl.BlockSpec((1, t, CW), lambda j: (j, 0, 0))],
        out_specs=(pl.BlockSpec((t, CW), lambda j: (0, j)), pl.BlockSpec((K_SSD, CW), lambda j: (0, j)),
                   pl.BlockSpec((1, CW), lambda j: (0, j))),
        scratch_shapes=[pltpu.VMEM((t, CW), f32), pltpu.VMEM((8 * K_SSD, CW), f32), pltpu.VMEM((8, CW), f32)],
        compiler_params=_cp("arbitrary"),
    )(proj, conv_w, conv_b, d_bc)


def _chunk_masks():
    ii = lax.broadcasted_iota(jnp.int32, (CHUNK, CHUNK), 0)
    jj = lax.broadcasted_iota(jnp.int32, (CHUNK, CHUNK), 1)
    return ii == jj, jj <= ii, jj >= ii


def _to_row(col, eye):
    return jnp.sum(jnp.where(eye, col, 0.0), axis=0, keepdims=True)


def _to_col(row, eye):
    return jnp.sum(jnp.where(eye, row, 0.0), axis=1, keepdims=True)


def _head_decay(dt_h, a_h, eye, tril):
    a_row = _to_row(dt_h * a_h, eye)
    cs = jnp.sum(jnp.where(tril, a_row, 0.0), axis=1, keepdims=True)
    cs_row = _to_row(cs, eye)
    decay = jnp.where(tril, jnp.exp(jnp.where(tril, cs - cs_row, 0.0)), 0.0)
    total = jnp.sum(a_row, axis=1, keepdims=True)
    return cs, decay, total


def _lane_pick(mat, lane, which):
    return jnp.sum(jnp.where(lane == which, mat, 0.0), axis=1, keepdims=True)


def _ssd_fwd(xbc_act, proj, dt_bias_row, a_log_row):
    t = xbc_act.shape[0]
    nc = t // CHUNK
    cb, cc, cdt = D // LANES, (D + 2 * STATE_N) // LANES, OFF_DT // LANES

    def body(x_ref, b_ref, c_ref, dt_ref, dtb_ref, alog_ref, y_ref, st_ref):
        j = pl.program_id(0)
        eye, tril, _ = _chunk_masks()
        lane = lax.broadcasted_iota(jnp.int32, (1, LANES), 1)
        first = lane < HEAD_P
        a_row = -jnp.exp(alog_ref[...])
        a_heads = [jnp.sum(jnp.where(lane == 2 * j + h, a_row, 0.0), axis=1, keepdims=True) for h in range(2)]

        def chunk(c, hprev):
            rows = pl.ds(pl.multiple_of(c * CHUNK, CHUNK), CHUNK)
            xv, bm, cm = x_ref[rows, :], b_ref[rows, :], c_ref[rows, :]
            dt = _softplus(dt_ref[rows, :] + dtb_ref[...])
            st_ref[c] = hprev
            g = _mm_nt(cm, bm)
            ch = _mm(cm, hprev)
            dts = [_lane_pick(dt, lane, 2 * j + h) for h in range(2)]
            xdt = xv * jnp.where(first, dts[0], dts[1])
            ys, hs = [], []
            for h in range(2):
                cs, decay, total = _head_decay(dts[h], a_heads[h], eye, tril)
                y = _mm(g * decay, xdt) + jnp.exp(cs) * ch
                s = _mm_tn(bm * jnp.exp(total - cs), xdt)
                ys.append(y)
                hs.append(jnp.exp(total) * hprev + s)
            y_ref[rows, :] = jnp.where(first, ys[0], ys[1])
            return jnp.where(first, hs[0], hs[1])

        lax.fori_loop(0, nc, chunk, jnp.zeros((STATE_N, LANES), f32))

    blk = lambda f: pl.BlockSpec((t, LANES), f)
    return pl.pallas_call(
        body, name="ssd_fwd", grid=(D // LANES,),
        out_shape=(jax.ShapeDtypeStruct((t, D), f32), jax.ShapeDtypeStruct((nc, STATE_N, D), f32)),
        in_specs=[blk(lambda j: (0, j)), blk(lambda j: (0, cb + j // 4)), blk(lambda j: (0, cc + j // 4)),
                  blk(lambda j: (0, cdt)), _row(LANES), _row(LANES)],
        out_specs=(blk(lambda j: (0, j)), pl.BlockSpec((nc, STATE_N, LANES), lambda j: (0, 0, j))),
        compiler_params=_cp("arbitrary"),
    )(xbc_act, xbc_act, xbc_act, proj, dt_bias_row, a_log_row)


def _ssd_bwd(xbc_act, proj, dt_bias_row, a_log_row, states, d_y):
    t = xbc_act.shape[0]
    nc = t // CHUNK
    cb, cc, cdt = D // LANES, (D + 2 * STATE_N) // LANES, OFF_DT // LANES

    def body(x_ref, b_ref, c_ref, dt_ref, dtb_ref, alog_ref, st_ref, dy_ref, dx_ref, dbc_ref, ddt_ref, da_ref):
        grp, p = pl.program_id(0), pl.program_id(1)
        j = 4 * grp + p
        eye, tril, triu = _chunk_masks()
        lane = lax.broadcasted_iota(jnp.int32, (1, LANES), 1)
        first = lane < HEAD_P
        last_row = lax.broadcasted_iota(jnp.int32, (CHUNK, 1), 0) == CHUNK - 1
        a_row = -jnp.exp(alog_ref[...])
        a_heads = [jnp.sum(jnp.where(lane == 2 * j + h, a_row, 0.0), axis=1, keepdims=True) for h in range(2)]

        @pl.when(p == 0)
        def _():
            dbc_ref[...] = jnp.zeros_like(dbc_ref)

        @pl.when(j == 0)
        def _():
            ddt_ref[...] = jnp.zeros_like(ddt_ref)
            da_ref[...] = jnp.zeros_like(da_ref)

        def chunk(i, dh):
            c = nc - 1 - i
            rows = pl.ds(pl.multiple_of(c * CHUNK, CHUNK), CHUNK)
            xv, bm, cm = x_ref[rows, :], b_ref[rows, :], c_ref[rows, :]
            dtr = dt_ref[rows, :] + dtb_ref[...]
            dt = _softplus(dtr)
            hprev = st_ref[c]
            dy = dy_ref[rows, :]
            g = _mm_nt(cm, bm)
            dts = [_lane_pick(dt, lane, 2 * j + h) for h in range(2)]
            xdt = xv * jnp.where(first, dts[0], dts[1])
            dxs, dhs = [], []
            db_sum, dc_sum = None, None
            ddt_mat = jnp.zeros((CHUNK, LANES), f32)
            da_acc = jnp.zeros((1, LANES), f32)
            for h in range(2):
                mine = first if h == 0 else jnp.logical_not(first)
                cs, decay, total = _head_decay(dts[h], a_heads[h], eye, tril)
                e_cs, e_tot = jnp.exp(cs), jnp.exp(total)
                dec_s = jnp.exp(total - cs)
                dyh = jnp.where(mine, dy, 0.0)
                xdth = jnp.where(mine, xdt, 0.0)
                dhh = jnp.where(mine, dh, 0.0)
                hph = jnp.where(mine, hprev, 0.0)
                m = g * decay
                dm = _mm_nt(dyh, xdth)
                dg = dm * decay
                w = dm * m
                bdec = bm * dec_s
                dxdt = _mm_tn(m, dyh) + _mm(bdec, dhh)
                dc_off = _mm_nt(dyh, hph) * e_cs
                db_s = _mm_nt(xdth, dhh) * dec_s
                dc_h = _mm(dg, bm) + dc_off
                db_h = _mm_tn(dg, cm) + db_s
                r_s = jnp.sum(db_s * bm, axis=1, keepdims=True)
                dtotal = jnp.sum(r_s, axis=0, keepdims=True) + e_tot * jnp.sum(
                    jnp.sum(dhh * hph, axis=1, keepdims=True), axis=0, keepdims=True)
                dcs = (jnp.sum(w, axis=1, keepdims=True) - _to_col(jnp.sum(w, axis=0, keepdims=True), eye)
                       + jnp.sum(dc_off * cm, axis=1, keepdims=True) - r_s + jnp.where(last_row, dtotal, 0.0))
                da_col = jnp.sum(jnp.where(triu, _to_row(dcs, eye), 0.0), axis=1, keepdims=True)
                ddt = da_col * a_heads[h] + jnp.sum(jnp.where(mine, dxdt * xv, 0.0), axis=1, keepdims=True)
                ddt_mat = ddt_mat + jnp.where(lane == 2 * j + h, ddt, 0.0)
                da_acc = da_acc + jnp.where(lane == 2 * j + h, jnp.sum(da_col * dts[h], axis=0, keepdims=True), 0.0)
                dxs.append(dxdt * dts[h])
                dhs.append(e_tot * dhh + _mm_tn(cm * e_cs, dyh))
                db_sum = db_h if db_sum is None else db_sum + db_h
                dc_sum = dc_h if dc_sum is None else dc_sum + dc_h
            dx_ref[rows, :] = jnp.where(first, dxs[0], dxs[1])
            dbc_ref[0, rows, :] += db_sum
            dbc_ref[1, rows, :] += dc_sum
            ddt_ref[rows, :] += ddt_mat * jax.nn.sigmoid(dtr)
            da_ref[...] += da_acc * a_row
            return jnp.where(first, dhs[0], dhs[1])

        lax.fori_loop(0, nc, chunk, jnp.zeros((STATE_N, LANES), f32))

    blk = lambda f: pl.BlockSpec((t, LANES), f)
    return pl.pallas_call(
        body, name="ssd_bwd", grid=(2, 4),
        out_shape=(jax.ShapeDtypeStruct((t, D), f32), jax.ShapeDtypeStruct((2, t, 2 * STATE_N), f32),
                   jax.ShapeDtypeStruct((t, LANES), f32), jax.ShapeDtypeStruct((1, LANES), f32)),
        in_specs=[blk(lambda g, p: (0, 4 * g + p)), blk(lambda g, p: (0, cb + g)), blk(lambda g, p: (0, cc + g)),
                  blk(lambda g, p: (0, cdt)), _row(LANES), _row(LANES),
                  pl.BlockSpec((nc, STATE_N, LANES), lambda g, p: (0, 0, 4 * g + p)), blk(lambda g, p: (0, 4 * g + p))],
        out_specs=(blk(lambda g, p: (0, 4 * g + p)), pl.BlockSpec((2, t, LANES), lambda g, p: (0, 0, g)),
                   blk(lambda g, p: (0, 0)), _row(LANES)),
        compiler_params=_cp("arbitrary", "arbitrary"),
    )(xbc_act, xbc_act, xbc_act, proj, dt_bias_row, a_log_row, states, d_y)


def _up_bwd(d_up, w_up, x1, mod, norm2_w, dx2, mix, w_out):
    t = x1.shape[0]

    def body(dup_ref, wu_ref, x1_ref, mod_ref, nw_ref, dx2_ref, mix_ref, wo_ref,
             dx1_ref, dmix_ref, dys_ref, du_ref, st_ref):
        @pl.when(pl.program_id(0) == 0)
        def _():
            st_ref[...] = jnp.zeros_like(st_ref)

        nt = (((1,), (1,)), ((), ()))
        dh = None
        for k in range(4):
            lo = (k % 2) * UP_SHARD
            part = lax.dot_general(dup_ref[k // 2, :, lo:lo + UP_SHARD], wu_ref[k], nt, preferred_element_type=f32)
            dh = part if dh is None else dh + part
        x1 = x1_ref[...]
        rstd = lax.rsqrt(jnp.mean(x1 * x1, axis=-1, keepdims=True) + 1e-6)
        xh = x1 * rstd
        nw = nw_ref[...]
        sc = 1.0 + mod_ref[:, 4 * D:5 * D]
        st_ref[0:1, :] += jnp.sum(dh, axis=0, keepdims=True)
        st_ref[1:2, :] += jnp.sum(dh * xh * nw, axis=0, keepdims=True)
        st_ref[2:3, :] += jnp.sum(dh * sc * xh, axis=0, keepdims=True)
        dxh = dh * sc * nw
        dx1 = dx2_ref[...] + rstd * (dxh - xh * jnp.mean(dxh * xh, axis=-1, keepdims=True))
        dx1_ref[...] = dx1
        st_ref[3:4, :] += jnp.sum(dx1 * mix_ref[...], axis=0, keepdims=True)
        dmix = (mod_ref[:, 2 * D:3 * D] * dx1).astype(MX)
        dmix_ref[...] = dmix
        dys_ref[...] = lax.dot_general(dmix, wo_ref[0:D, :], nt, preferred_element_type=f32)
        du_ref[...] = lax.dot_general(dmix, wo_ref[D:2 * D, :], nt, preferred_element_type=f32)

    blk = pl.BlockSpec((TM, D), lambda i: (i, 0))
    return pl.pallas_call(
        body, name="up_bwd", grid=(t // TM,),
        out_shape=(jax.ShapeDtypeStruct((t, D), f32), jax.ShapeDtypeStruct((t, D), MX),
                   jax.ShapeDtypeStruct((t, D), f32), jax.ShapeDtypeStruct((t, D), f32),
                   jax.ShapeDtypeStruct((8, D), f32)),
        in_specs=[pl.BlockSpec((2, TM, D_FF), lambda i: (0, i, 0)), _resident((4, D, UP_SHARD)), blk, _row(6 * D), _row(),
                  blk, blk, _resident((2 * D, D))],
        out_specs=(blk, blk, blk, blk, pl.BlockSpec((8, D), lambda i: (0, 0))),
        compiler_params=_cp("arbitrary"),
    )(d_up, w_up, x1, mod, norm2_w, dx2, mix, w_out)


def _ln_silu_bwd(d_u, u_conv, ln_w, ln_b):
    t = d_u.shape[0]

    def body(du_ref, u_ref, w_ref, b_ref, o_ref, st_ref):
        @pl.when(pl.program_id(0) == 0)
        def _():
            st_ref[...] = jnp.zeros_like(st_ref)

        u = u_ref[...]
        mu = jnp.mean(u, axis=-1, keepdims=True)
        uc = u - mu
        rstd = lax.rsqrt(jnp.mean(uc * uc, axis=-1, keepdims=True) + 1e-5)
        n = uc * rstd
        w = w_ref[...]
        dl = du_ref[...] * _dsilu(n * w + b_ref[...])
        st_ref[0:1, :] += jnp.sum(dl * n, axis=0, keepdims=True)
        st_ref[1:2, :] += jnp.sum(dl, axis=0, keepdims=True)
        dn = dl * w
        o_ref[...] = rstd * (dn - jnp.mean(dn, axis=-1, keepdims=True) - n * jnp.mean(dn * n, axis=-1, keepdims=True))

    blk = pl.BlockSpec((TM, D), lambda i: (i, 0))
    return pl.pallas_call(
        body, name="ln_silu_bwd", grid=(t // TM,),
        out_shape=(jax.ShapeDtypeStruct((t, D), f32), jax.ShapeDtypeStruct((8, D), f32)),
        in_specs=[blk, blk, _row(), _row()], out_specs=(blk, pl.BlockSpec((8, D), lambda i: (0, 0))),
        compiler_params=_cp("arbitrary"),
    )(d_u, u_conv, ln_w, ln_b)


def _ssd_gate_norm_bwd(d_out, y_scan, xbc_act, proj, d_skip_row, ssd_norm_w):
    t = d_out.shape[0]

    def body(do_ref, y_ref, xs_ref, z_ref, dsk_ref, nw_ref, dy_ref, dz_ref, st_ref):
        @pl.when(pl.program_id(0) == 0)
        def _():
            st_ref[...] = jnp.zeros_like(st_ref)

        xs = xs_ref[...]
        y = y_ref[...] + xs * dsk_ref[...]
        z = z_ref[...]
        s = _silu(z)
        yz = y * s
        rstd = lax.rsqrt(jnp.mean(yz * yz, axis=-1, keepdims=True) + 1e-6)
        n = yz * rstd
        do = do_ref[...]
        st_ref[0:1, :] += jnp.sum(do * n, axis=0, keepdims=True)
        dn = do * nw_ref[...]
        dyz = rstd * (dn - n * jnp.mean(dn * n, axis=-1, keepdims=True))
        dy = dyz * s
        dy_ref[...] = dy
        dz_ref[...] = (dyz * y * _dsilu(z)).astype(MX)
        st_ref[1:2, :] += jnp.sum(dy * xs, axis=0, keepdims=True)

    blk = pl.BlockSpec((TM, D), lambda i: (i, 0))
    return pl.pallas_call(
        body, name="ssd_gate_norm_bwd", grid=(t // TM,),
        out_shape=(jax.ShapeDtypeStruct((t, D), f32), jax.ShapeDtypeStruct((t, D), MX), jax.ShapeDtypeStruct((8, D), f32)),
        in_specs=[blk, blk, blk, blk, _row(), _row()], out_specs=(blk, blk, pl.BlockSpec((8, D), lambda i: (0, 0))),
        compiler_params=_cp("arbitrary"),
    )(d_out, y_scan, xbc_act, proj, d_skip_row, ssd_norm_w)


def _inproj_bwd(d_z, d_xraw, d_bcraw, d_conf, d_dt, w_pack, x, mod, norm1_w, dx1):
    t = x.shape[0]

    def body(dz_ref, dx_ref, dbc_ref, dcf_ref, ddt_ref, w_ref, x_ref, mod_ref, nw_ref, dx1_ref, gx_ref, st_ref):
        @pl.when(pl.program_id(0) == 0)
        def _():
            st_ref[...] = jnp.zeros_like(st_ref)

        nt = (((1,), (1,)), ((), ()))
        dot = lambda a, lo, hi: lax.dot_general(a, w_ref[:, lo:hi], nt, preferred_element_type=f32)
        dh = dot(dz_ref[...], OFF_Z, OFF_Z + D)
        dh = dh + dot(dx_ref[...], OFF_XBC, OFF_XBC + D)
        dh = dh + dot(dbc_ref[...], OFF_XBC + D, OFF_XBC + D_XBC)
        dh = dh + dot(dcf_ref[0], OFF_CA, OFF_CA + D)
        dh = dh + dot(dcf_ref[1], OFF_CG, OFF_CG + D)
        dh = dh + dot(ddt_ref[...].astype(MX), OFF_DT, OFF_DT + LANES)
        st_ref[3:4, 0:LANES] += jnp.sum(ddt_ref[...], axis=0, keepdims=True)
        xv = x_ref[...]
        rstd = lax.rsqrt(jnp.mean(xv * xv, axis=-1, keepdims=True) + 1e-6)
        xh = xv * rstd
        nw = nw_ref[...]
        sc = 1.0 + mod_ref[:, D:2 * D]
        st_ref[0:1, :] += jnp.sum(dh, axis=0, keepdims=True)
        st_ref[1:2, :] += jnp.sum(dh * xh * nw, axis=0, keepdims=True)
        st_ref[2:3, :] += jnp.sum(dh * sc * xh, axis=0, keepdims=True)
        dxh = dh * sc * nw
        gx_ref[...] = dx1_ref[...] + rstd * (dxh - xh * jnp.mean(dxh * xh, axis=-1, keepdims=True))

    blk = pl.BlockSpec((TM, D), lambda i: (i, 0))
    return pl.pallas_call(
        body, name="inproj_bwd", grid=(t // TM,),
        out_shape=(jax.ShapeDtypeStruct((t, D), f32), jax.ShapeDtypeStruct((8, D), f32)),
        in_specs=[blk, blk, pl.BlockSpec((TM, 2 * CW), lambda i: (i, 0)), pl.BlockSpec((2, TM, D), lambda i: (0, i, 0)),
                  pl.BlockSpec((TM, LANES), lambda i: (i, 0)), _resident((D, W_PACK)), blk, _row(6 * D), _row(), blk],
        out_specs=(blk, pl.BlockSpec((8, D), lambda i: (0, 0))),
        compiler_params=_cp("arbitrary"),
    )(d_z, d_xraw, d_bcraw, d_conf, d_dt, w_pack, x, mod, norm1_w, dx1)


def _wgrad(a, d, name, bn=256):
    t, k = a.shape
    n = d.shape[1]
    out_dtype = MX

    def body(a_ref, d_ref, o_ref):
        o_ref[...] = lax.dot_general(a_ref[...], d_ref[...].astype(MX), (((0,), (0,)), ((), ())),
                                     preferred_element_type=f32).astype(out_dtype)

    return pl.pallas_call(
        body, name=name, grid=(n // bn,), out_shape=jax.ShapeDtypeStruct((k, n), out_dtype),
        in_specs=[_resident((t, k)), pl.BlockSpec((t, bn), lambda j: (0, j))],
        out_specs=pl.BlockSpec((k, bn), lambda j: (0, j)), compiler_params=_cp("arbitrary"),
    )(a, d)


def _wgrad_stacked(a, d, name, bn):
    out_dtype = MX
    t, k = a.shape
    s, _, n = d.shape
    nb = n // bn

    def body(a_ref, d_ref, o_ref):
        o_ref[0] = lax.dot_general(a_ref[...], d_ref[0], (((0,), (0,)), ((), ())),
                                   preferred_element_type=f32).astype(out_dtype)

    return pl.pallas_call(
        body, name=name, grid=(s, nb), out_shape=jax.ShapeDtypeStruct((s * nb, k, bn), out_dtype),
        in_specs=[_resident((t, k)), pl.BlockSpec((1, t, bn), lambda i, j: (i, 0, j))],
        out_specs=pl.BlockSpec((1, k, bn), lambda i, j: (i * nb + j, 0, 0)), compiler_params=_cp("arbitrary", "arbitrary"),
    )(a, d)


def _pad_row(v, width=LANES):
    return jnp.pad(v.reshape(1, -1), ((0, 0), (0, width - v.size)))


def _local_step(x, mod, target, w_pack, w_out, w_up, w_down, small):
    dtb_row, alog_row = _pad_row(small["dt_bias"]), _pad_row(small["a_log"])
    dskip_row = jnp.repeat(small["d_skip"].reshape(-1), HEAD_P).reshape(1, D)

    proj, h = _ln_inproj(x, mod, small["norm1_w"], w_pack)
    xbc_act = _ssd_conv_fwd(proj, small["ssd_conv_w"], small["ssd_conv_b"])
    y_scan, states = _ssd_fwd(xbc_act, proj, dtb_row, alog_row)
    y_ssd = _ssd_gate_norm(y_scan, xbc_act, proj, dskip_row, small["ssd_norm_w"])
    u_conv = _glu_conv_fwd(proj, small["conf_conv_w"], small["conf_conv_b"])
    u = _ln_silu(u_conv, small["conf_ln_w"], small["conf_ln_b"])
    mix, x1, h2, up = _outproj_ln2_up(y_ssd, u, w_out, x, mod, small["norm2_w"], w_up)
    act = _ffn_conv_fwd(up, small["ffn_conv_w"], small["ffn_conv_b"])
    dx2, d_ffn, d_act, st_down = _down_loss(act, w_down, x1, mod, small["final_norm_w"], target)

    g_down = _wgrad(act, d_ffn, "wgrad_down")
    d_up, dw_ffn, db_ffn = _ffn_conv_bwd(up, small["ffn_conv_w"], small["ffn_conv_b"], d_act)
    g_up = _wgrad_stacked(h2, d_up, "wgrad_up", D_FF // 2)
    dx1, d_mix, d_yssd, d_u, st_up = _up_bwd(d_up, w_up, x1, mod, small["norm2_w"], dx2, mix, w_out)
    g_out_y = _wgrad(y_ssd, d_mix, "wgrad_out_y")
    g_out_u = _wgrad(u, d_mix, "wgrad_out_u")
    d_uconv, st_ln = _ln_silu_bwd(d_u, u_conv, small["conf_ln_w"], small["conf_ln_b"])
    d_conf, dw_conf, db_conf = _glu_conv_bwd(proj, small["conf_conv_w"], d_uconv)
    d_y, d_z, st_gn = _ssd_gate_norm_bwd(d_yssd, y_scan, xbc_act, proj, dskip_row, small["ssd_norm_w"])
    d_xs, d_bc, d_dt, d_alog = _ssd_bwd(xbc_act, proj, dtb_row, alog_row, states, d_y)
    d_xraw, dw_sx, db_sx = _ssd_conv_bwd_x(proj, small["ssd_conv_w"], small["ssd_conv_b"], d_xs, d_y, dskip_row)
    d_bcraw, dw_sbc, db_sbc = _ssd_conv_bwd_bc(proj, small["ssd_conv_w"], small["ssd_conv_b"], d_bc)
    grad_x, st_in = _inproj_bwd(d_z, d_xraw, d_bcraw, d_conf, d_dt, w_pack, x, mod, small["norm1_w"], dx1)
    g_in = dict(z=_wgrad(h, d_z, "wgrad_in_z"), x=_wgrad(h, d_xraw, "wgrad_in_x"), bc=_wgrad(h, d_bcraw, "wgrad_in_bc"),
                conf=_wgrad_stacked(h, d_conf, "wgrad_in_conf", D), dt=_wgrad(h, d_dt, "wgrad_in_dt", bn=LANES))

    d_mod = jnp.concatenate([st_in[0:1], st_in[1:2], st_up[3:4], st_up[0:1], st_up[1:2], st_down[1:2]], axis=1)
    gsmall = dict(
        norm1_w=st_in[2:3], ssd_conv_w=jnp.concatenate([dw_sx, dw_sbc], axis=1),
        ssd_conv_b=jnp.concatenate([db_sx, db_sbc], axis=1), dt_bias=st_in[3:4, 0:HEADS], a_log=d_alog[:, 0:HEADS],
        d_skip=st_gn[1].reshape(HEADS, HEAD_P).sum(axis=1).reshape(1, HEADS), ssd_norm_w=st_gn[0:1],
        conf_conv_w=dw_conf, conf_conv_b=db_conf, conf_ln_w=st_ln[0:1], conf_ln_b=st_ln[1:2], norm2_w=st_up[2:3],
        ffn_conv_w=jnp.concatenate([dw_ffn[0], dw_ffn[1]], axis=1), ffn_conv_b=jnp.concatenate([db_ffn[0], db_ffn[1]], axis=1),
        final_norm_w=st_down[0:1], mod=d_mod)
    gbig = dict(w_in=g_in, w_out=(g_out_y, g_out_u), w_up=g_up, w_down=g_down)
    return st_down[2, 0], grad_x, gbig, gsmall


W_IN_COLS = 4624
W_IN_SHARD = W_IN_COLS // 4
W_IN_SHARD_PAD = 1280
_SEGMENTS = ((0, 1024, OFF_Z), (1024, 2560, OFF_XBC), (2560, 2576, OFF_DT), (2576, 3600, OFF_CA), (3600, 4624, OFF_CG))


def _in_pieces(bounds=()):
    out = []
    for k in range(4):
        s0, s1 = k * W_IN_SHARD, (k + 1) * W_IN_SHARD
        for lo, hi, off in _SEGMENTS:
            a, b = max(lo, s0), min(hi, s1)
            while a < b:
                p = off + a - lo
                e = min([b - a] + [c - p for c in bounds if c > p])
                out.append((k, a - s0, p, e))
                a += e
    return out


def _pack_w_in(shards):
    pieces = _in_pieces()

    def body(s_ref, o_ref):
        o_ref[:, OFF_DT:W_PACK] = jnp.zeros((TM, W_PACK - OFF_DT), MX)
        for k, c, p, n in pieces:
            o_ref[:, p:p + n] = s_ref[k, :, c:c + n]

    return pl.pallas_call(
        body, name="pack_w_in", grid=(D // TM,), out_shape=jax.ShapeDtypeStruct((D, W_PACK), MX),
        in_specs=[pl.BlockSpec((4, TM, W_IN_SHARD_PAD), lambda i: (0, i, 0))],
        out_specs=pl.BlockSpec((TM, W_PACK), lambda i: (i, 0)), compiler_params=_cp("arbitrary"),
    )(shards)


def _unpack_g_in(g):
    srcs = ((OFF_Z, D), (OFF_XBC, D), (OFF_XBC + D, 2 * CW), (OFF_CA, D), (OFF_CG, D), (OFF_DT, LANES))
    pieces = _in_pieces(tuple(o for o, _ in srcs) + tuple(o + n for o, n in srcs))

    def body(z_ref, x_ref, bc_ref, cf_ref, dt_ref, o_ref):
        read = (lambda lo, hi: z_ref[:, lo:hi], lambda lo, hi: x_ref[:, lo:hi], lambda lo, hi: bc_ref[:, lo:hi],
                lambda lo, hi: cf_ref[0, :, lo:hi], lambda lo, hi: cf_ref[1, :, lo:hi], lambda lo, hi: dt_ref[:, lo:hi])
        o_ref[:, :, W_IN_SHARD - 4:W_IN_SHARD_PAD] = jnp.zeros((4, TM, W_IN_SHARD_PAD - W_IN_SHARD + 4), MX)
        for k, c, p, n in pieces:
            i = [q for q, (o, w) in enumerate(srcs) if o <= p < o + w][0]
            o_ref[k, :, c:c + n] = read[i](p - srcs[i][0], p - srcs[i][0] + n)

    blk = lambda w: pl.BlockSpec((TM, w), lambda i: (i, 0))
    return pl.pallas_call(
        body, name="unpack_g_in", grid=(D // TM,), out_shape=jax.ShapeDtypeStruct((4, D, W_IN_SHARD_PAD), MX),
        in_specs=[blk(D), blk(D), blk(2 * CW), pl.BlockSpec((2, TM, D), lambda i: (0, i, 0)), blk(LANES)],
        out_specs=pl.BlockSpec((4, TM, W_IN_SHARD_PAD), lambda i: (0, i, 0)), compiler_params=_cp("arbitrary"),
    )(g["z"], g["x"], g["bc"], g["conf"], g["dt"])


def _scalar(v):
    return jnp.reshape(v, (1,)).astype(jnp.int32)


def _cast_into_slot(w, width, chip):
    r, c = w.shape
    h = r // 2
    tm = _row_tile(h)
    nj = h // tm

    def body(chip_ref, w_ref, o_ref):
        v = w_ref[...].astype(MX)
        o_ref[0, 0] = v if width == c else jnp.concatenate([v, jnp.zeros((tm, width - c), MX)], axis=1)

    return pl.pallas_call(
        body, name=f"cast_into_slot_{r}x{c}", out_shape=jax.ShapeDtypeStruct((4, 2, h, width), MX),
        grid_spec=pltpu.PrefetchScalarGridSpec(
            num_scalar_prefetch=1, grid=(2, nj),
            in_specs=[pl.BlockSpec((tm, c), lambda i, j, chip: (i * nj + j, 0))],
            out_specs=pl.BlockSpec((1, 1, tm, width), lambda i, j, chip: (chip[0], i, j, 0))),
        compiler_params=_cp("arbitrary", "arbitrary"),
    )(_scalar(chip), w)


ANY = pl.BlockSpec(memory_space=pl.ANY)


def _place():
    x, y, c = lax.axis_index("x"), lax.axis_index("y"), lax.axis_index("c")
    return x, y, c, [(1 - x, y), (x, 1 - y), (1 - x, 1 - y)]


def _gather_rows(block):
    m_per, n = block.shape

    def body(x_ref, out_ref, send_sems, recv_sems, local_sem):
        x, y, c, chips = _place()
        me, sibling = (x, y, c), (x, y, 1 - c)

        def rows(px, py, pc):
            return out_ref.at[pl.ds((4 * px + 2 * py + pc) * m_per, m_per), :]

        def copy(k, blk, to, src=None):
            return pltpu.make_async_remote_copy(
                src_ref=rows(*blk) if src is None else src, dst_ref=rows(*blk), send_sem=send_sems.at[k],
                recv_sem=recv_sems.at[k], device_id=to, device_id_type=MESH)

        mine = pltpu.make_async_copy(x_ref, rows(*me), local_sem)
        mine.start()
        first = [copy(0, me, sibling, src=x_ref)]
        first += [copy(1 + j, me, (*chip, c), src=x_ref) for j, chip in enumerate(chips)]
        for cp in first:
            cp.start()
        passed = [copy(4 + j, (*chip, c), sibling) for j, chip in enumerate(chips)]
        for j, chip in enumerate(chips):
            copy(1 + j, (*chip, c), me).wait_recv()
            passed[j].start()
        copy(0, sibling, me).wait_recv()
        for j, chip in enumerate(chips):
            copy(4 + j, (*chip, 1 - c), me).wait_recv()
        for cp in first + passed:
            cp.wait_send()
        mine.wait()

    return pl.pallas_call(
        body, name=f"gather_rows_{m_per}x{n}", out_shape=jax.ShapeDtypeStruct((8 * m_per, n), block.dtype),
        in_specs=[pl.BlockSpec(memory_space=pltpu.VMEM)], out_specs=pl.BlockSpec(memory_space=pltpu.VMEM),
        scratch_shapes=[pltpu.SemaphoreType.DMA((7,)), pltpu.SemaphoreType.DMA((7,)), pltpu.SemaphoreType.DMA],
        compiler_params=pltpu.CompilerParams(vmem_limit_bytes=VMEM_LIMIT),
    )(block)


def _gather_weights(slots):
    n = len(slots)

    def body(*refs):
        outs = refs[n:2 * n]
        send_sems, recv_sems = refs[2 * n:]
        x, y, c, chips = _place()
        k_me = 2 * x + y
        sibling = (x, y, 1 - c)

        def copy(a, j, k, half, to):
            dst = outs[a].at[k, half]
            return pltpu.make_async_remote_copy(
                src_ref=dst, dst_ref=dst, send_sem=send_sems.at[a, j], recv_sem=recv_sems.at[a, j],
                device_id=to, device_id_type=MESH)

        first = [copy(a, j, k_me, c, (*chip, c)) for a in range(n) for j, chip in enumerate(chips)]
        for cp in first:
            cp.start()
        passed = []
        for a in range(n):
            for j, (px, py) in enumerate(chips):
                copy(a, j, 2 * px + py, c, (x, y, c)).wait_recv()
                fwd = copy(a, 3 + j, 2 * px + py, c, sibling)
                fwd.start()
                passed.append(fwd)
        for a in range(n):
            for j, (px, py) in enumerate(chips):
                copy(a, 3 + j, 2 * px + py, 1 - c, (x, y, c)).wait_recv()
        for cp in first + passed:
            cp.wait_send()

    return pl.pallas_call(
        body, name="gather_weights",
        out_shape=tuple(jax.ShapeDtypeStruct(s.shape, s.dtype) for s in slots),
        in_specs=[ANY] * n, out_specs=tuple([ANY] * n), input_output_aliases={a: a for a in range(n)},
        scratch_shapes=[pltpu.SemaphoreType.DMA((n, 6)), pltpu.SemaphoreType.DMA((n, 6))],
    )(*slots)


def _swap_halves(grads):
    n = len(grads)

    def body(*refs):
        ins, got = refs[:n], refs[n:2 * n]
        send_sems, recv_sems = refs[2 * n:]
        x, y, c, _ = _place()
        sent = [pltpu.make_async_remote_copy(
            src_ref=ins[a].at[k, 1 - c], dst_ref=got[a].at[k], send_sem=send_sems.at[a, k], recv_sem=recv_sems.at[a, k],
            device_id=(x, y, 1 - c), device_id_type=MESH) for a in range(n) for k in range(4)]
        for cp in sent:
            cp.start()
        for cp in sent:
            cp.wait()

    return pl.pallas_call(
        body, name="swap_halves", out_shape=tuple(jax.ShapeDtypeStruct((4,) + g.shape[2:], g.dtype) for g in grads),
        in_specs=[ANY] * n, out_specs=tuple([ANY] * n),
        scratch_shapes=[pltpu.SemaphoreType.DMA((n, 4)), pltpu.SemaphoreType.DMA((n, 4))],
    )(*grads)


def _scatter_chips(parts):
    n = len(parts)

    def body(*refs):
        ins, others = refs[:n], refs[n:2 * n]
        send_sems, recv_sems = refs[2 * n:]
        x, y, c, chips = _place()
        sent = [pltpu.make_async_remote_copy(
            src_ref=ins[a].at[2 * px + py], dst_ref=others[a].at[j], send_sem=send_sems.at[a, j],
            recv_sem=recv_sems.at[a, j], device_id=(px, py, c), device_id_type=MESH)
            for a in range(n) for j, (px, py) in enumerate(chips)]
        for cp in sent:
            cp.start()
        for cp in sent:
            cp.wait()

    return pl.pallas_call(
        body, name="scatter_chips", out_shape=tuple(jax.ShapeDtypeStruct((3,) + p.shape[1:], p.dtype) for p in parts),
        in_specs=[ANY] * n, out_specs=tuple([ANY] * n),
        scratch_shapes=[pltpu.SemaphoreType.DMA((n, 3)), pltpu.SemaphoreType.DMA((n, 3))],
    )(*parts)


def _swap_sums(halves):
    n = len(halves)

    def body(*refs):
        ins, outs = refs[:n], refs[n:2 * n]
        send_sems, recv_sems = refs[2 * n:]
        x, y, c, _ = _place()
        sent = [pltpu.make_async_remote_copy(
            src_ref=ins[a], dst_ref=outs[a], send_sem=send_sems.at[a], recv_sem=recv_sems.at[a],
            device_id=(x, y, 1 - c), device_id_type=MESH) for a in range(n)]
        for cp in sent:
            cp.start()
        for cp in sent:
            cp.wait()

    return pl.pallas_call(
        body, name="swap_sums", out_shape=tuple(jax.ShapeDtypeStruct(s.shape, s.dtype) for s in halves),
        in_specs=[ANY] * n, out_specs=tuple([ANY] * n),
        scratch_shapes=[pltpu.SemaphoreType.DMA((n,)), pltpu.SemaphoreType.DMA((n,))],
    )(*halves)


def _row_tile(r):
    for tm in (TM, 176, 128, 64, 32, 16, 8):
        if r % tm == 0:
            return tm
    return r


def _add_pair(mine, got, core):
    k, _, h, c = mine.shape
    tm = _row_tile(h)

    def body(core_ref, a_ref, b_ref, o_ref):
        o_ref[0] = (a_ref[0, 0].astype(f32) + b_ref[0].astype(f32)).astype(MX)

    blk = pl.BlockSpec((1, tm, c), lambda i, j, core: (i, j, 0))
    return pl.pallas_call(
        body, name=f"add_pair_{h}x{c}", out_shape=jax.ShapeDtypeStruct((k, h, c), MX),
        grid_spec=pltpu.PrefetchScalarGridSpec(
            num_scalar_prefetch=1, grid=(k, h // tm),
            in_specs=[pl.BlockSpec((1, 1, tm, c), lambda i, j, core: (i, core[0], j, 0)), blk], out_specs=blk),
        compiler_params=_cp("arbitrary", "arbitrary"),
    )(_scalar(core), mine, got)


def _add_chips(parts, others, chip):
    _, h, c = parts.shape
    tm = _row_tile(h)

    def body(chip_ref, a_ref, b_ref, o_ref):
        s = a_ref[0].astype(f32) + b_ref[0].astype(f32)
        o_ref[...] = (s + b_ref[1].astype(f32)) + b_ref[2].astype(f32)

    return pl.pallas_call(
        body, name=f"add_chips_{h}x{c}", out_shape=jax.ShapeDtypeStruct((h, c), f32),
        grid_spec=pltpu.PrefetchScalarGridSpec(
            num_scalar_prefetch=1, grid=(h // tm,),
            in_specs=[pl.BlockSpec((1, tm, c), lambda i, chip: (chip[0], i, 0)),
                      pl.BlockSpec((3, tm, c), lambda i, chip: (0, i, 0))],
            out_specs=pl.BlockSpec((tm, c), lambda i, chip: (i, 0))),
        compiler_params=_cp("arbitrary"),
    )(_scalar(chip), parts, others)


def _adam_math(w, g, m, v):
    m = ADAM_B1 * m + (1.0 - ADAM_B1) * g
    v = ADAM_B2 * v + (1.0 - ADAM_B2) * (g * g)
    m_hat = m / (1.0 - ADAM_B1 ** ADAM_STEP)
    v_hat = v / (1.0 - ADAM_B2 ** ADAM_STEP)
    return -ADAM_LR * (m_hat / (jnp.sqrt(v_hat) + ADAM_EPS) + ADAM_WD * w), m, v


def _adamw(w, g, m, v, name):
    r, c = w.shape
    tm = _row_tile(r)

    def body(w_ref, g_ref, m_ref, v_ref, d_ref, nm_ref, nv_ref):
        d_ref[...], nm_ref[...], nv_ref[...] = _adam_math(w_ref[...], g_ref[...], m_ref[...], v_ref[...])

    blk = pl.BlockSpec((tm, c), lambda i: (i, 0))
    return pl.pallas_call(
        body, name=name, grid=(r // tm,), out_shape=tuple([jax.ShapeDtypeStruct((r, c), f32)] * 3),
        in_specs=[blk] * 4, out_specs=(blk,) * 3, compiler_params=_cp("arbitrary"),
    )(w, g, m, v)


def _adamw_halves(w, mine, other, m, v, core, name):
    r, c = w.shape
    h = r // 2
    tm = _row_tile(h)
    nj = h // tm
    cg = mine.shape[1]

    def body(core_ref, w_ref, a_ref, b_ref, m_ref, v_ref, g_ref, d_ref, nm_ref, nv_ref):
        g = jnp.where(pl.program_id(0) == core_ref[0], a_ref[:, 0:c], b_ref[:, 0:c])
        g_ref[...] = g
        d_ref[...], nm_ref[...], nv_ref[...] = _adam_math(w_ref[...], g, m_ref[...], v_ref[...])

    blk = pl.BlockSpec((tm, c), lambda i, j, core: (i * nj + j, 0))
    gblk = pl.BlockSpec((tm, cg), lambda i, j, core: (j, 0))
    return pl.pallas_call(
        body, name=name, out_shape=tuple([jax.ShapeDtypeStruct((r, c), f32)] * 4),
        grid_spec=pltpu.PrefetchScalarGridSpec(
            num_scalar_prefetch=1, grid=(2, nj), in_specs=[blk, gblk, gblk, blk, blk], out_specs=(blk,) * 4),
        compiler_params=_cp("arbitrary", "arbitrary"),
    )(_scalar(core), w, mine, other, m, v)


def _ada_forward(c_all, ada_w):
    def body(c_ref, w_ref, o_ref):
        o_ref[...] = jnp.dot(_silu(c_ref[...]).astype(MX), w_ref[...].astype(MX), preferred_element_type=f32)

    return pl.pallas_call(body, name="ada_forward", out_shape=jax.ShapeDtypeStruct((8, ada_w.shape[1]), f32),
                          compiler_params=pltpu.CompilerParams(vmem_limit_bytes=VMEM_LIMIT))(c_all, ada_w)


def _ada_adamw(c_all_t, d_mod, w, m, v):
    r, c = w.shape
    tm = TM

    def body(ct_ref, dm_ref, w_ref, m_ref, v_ref, g_ref, d_ref, nm_ref, nv_ref):
        ca = _silu(ct_ref[...])
        g = ca[:, 0:1] * dm_ref[0:1, :]
        for b in range(1, 8):
            g = g + ca[:, b:b + 1] * dm_ref[b:b + 1, :]
        g_ref[...] = g
        d_ref[...], nm_ref[...], nv_ref[...] = _adam_math(w_ref[...], g, m_ref[...], v_ref[...])

    blk = pl.BlockSpec((tm, c), lambda i: (i, 0))
    return pl.pallas_call(
        body, name="ada_adamw", grid=(r // tm,), out_shape=tuple([jax.ShapeDtypeStruct((r, c), f32)] * 4),
        in_specs=[pl.BlockSpec((tm, 8), lambda i: (i, 0)), pl.BlockSpec((8, c), lambda i: (0, 0)), blk, blk, blk],
        out_specs=(blk,) * 4, compiler_params=_cp("arbitrary"),
    )(c_all_t, d_mod, w, m, v)


def _sum_devices(rows):
    n = rows.shape[1]

    def body(r_ref, o_ref):
        s = r_ref[0:8, :]
        for d in range(1, 8):
            s = s + r_ref[8 * d:8 * d + 8, :]
        o_ref[...] = s

    return pl.pallas_call(body, name="sum_devices", out_shape=jax.ShapeDtypeStruct((8, n), f32),
                          compiler_params=pltpu.CompilerParams(vmem_limit_bytes=VMEM_LIMIT))(rows)


WEIGHTS = ("ada_w", "ada_b", "norm1_w", "w_in", "ssd_conv_w", "ssd_conv_b", "dt_bias", "a_log", "d_skip", "ssd_norm_w",
           "conf_conv_w", "conf_conv_b", "conf_ln_w", "conf_ln_b", "w_out", "norm2_w", "w_up", "ffn_conv_w", "ffn_conv_b",
           "w_down", "final_norm_w")
VECTORS = ("ada_b", "norm1_w", "ssd_conv_b", "dt_bias", "a_log", "d_skip", "ssd_norm_w", "conf_conv_b", "conf_ln_w",
           "conf_ln_b", "norm2_w", "ffn_conv_b", "final_norm_w")
CONVS = {"ssd_conv_w": (K_SSD, D_XBC), "conf_conv_w": (K_CONF, D), "ffn_conv_w": (K_FFN, 2 * D_FF)}


def _pack8(pieces):
    flat = jnp.concatenate([p.reshape(-1) for p in pieces])
    n = -(-flat.size // (8 * LANES)) * LANES
    return jnp.pad(flat, (0, 8 * n - flat.size)).reshape(8, n)


def _unpack(flat, sizes):
    out, o = [], 0
    for n in sizes:
        out.append(flat[o:o + n])
        o += n
    return out


def kernel(x, c, ada_w, ada_b, norm1_w, w_in, ssd_conv_w, ssd_conv_b, dt_bias, a_log, d_skip, ssd_norm_w, conf_conv_w, conf_conv_b, conf_ln_w, conf_ln_b, w_out, norm2_w, w_up, ffn_conv_w, ffn_conv_b, w_down, final_norm_w, loss_target, m_ada_w, m_ada_b, m_norm1_w, m_w_in, m_ssd_conv_w, m_ssd_conv_b, m_dt_bias, m_a_log, m_d_skip, m_ssd_norm_w, m_conf_conv_w, m_conf_conv_b, m_conf_ln_w, m_conf_ln_b, m_w_out, m_norm2_w, m_w_up, m_ffn_conv_w, m_ffn_conv_b, m_w_down, m_final_norm_w, v_ada_w, v_ada_b, v_norm1_w, v_w_in, v_ssd_conv_w, v_ssd_conv_b, v_dt_bias, v_a_log, v_d_skip, v_ssd_norm_w, v_conf_conv_w, v_conf_conv_b, v_conf_ln_w, v_conf_ln_b, v_w_out, v_norm2_w, v_w_up, v_ffn_conv_w, v_ffn_conv_b, v_w_down, v_final_norm_w):
    given = dict(locals())
    w = {n: given[n] for n in WEIGHTS}
    mom = {n: given["m_" + n] for n in WEIGHTS}
    var = {n: given["v_" + n] for n in WEIGHTS}
    chip = 2 * lax.axis_index("x") + lax.axis_index("y")
    me = 2 * chip + lax.axis_index("c")

    sent = _pack8([c] + [w[n] for n in CONVS])
    got = _gather_rows(sent).reshape(8, -1)
    c_all = got[:, 0:D]
    o = D
    conv_full = {}
    for n, (taps, cols) in CONVS.items():
        per = taps * cols // 4
        shards = got[0::2, o:o + per].reshape(4, taps, cols // 4)
        conv_full[n] = jnp.concatenate([shards[k] for k in range(4)], axis=1)
        o += per

    mod_cols = _gather_rows(_ada_forward(c_all, ada_w[0])).reshape(8, 8, -1)[0::2]
    mod = lax.dynamic_index_in_dim(mod_cols, me, axis=1, keepdims=False).reshape(1, 6 * D) + ada_b

    core = lax.axis_index("c")
    a_in, a_out, a_up, a_down = _gather_weights([
        _cast_into_slot(w_in[0], W_IN_SHARD_PAD, chip), _cast_into_slot(w_out[0], D, chip),
        _cast_into_slot(w_up[0], UP_SHARD, chip), _cast_into_slot(w_down[0], D, chip)])
    w_pack = _pack_w_in(a_in.reshape(4, D, W_IN_SHARD_PAD))

    small = {n: w[n].reshape(1, -1) for n in VECTORS if n != "ada_b"}
    small.update(conv_full)
    loss_mine, grad_x, gbig, gsmall = _local_step(
        x[0], mod, loss_target[0], w_pack, a_out.reshape(2 * D, D), a_up.reshape(4, D, UP_SHARD), a_down.reshape(D_FF, D), small)
    loss = lax.psum(loss_mine, ("x", "y", "c"))

    quarters = lambda a: a.reshape(4, 2, a.shape[0] // 8, a.shape[1])
    g_up = gbig["w_up"]
    local = [_unpack_g_in(gbig["w_in"]).reshape(4, 2, D // 2, W_IN_SHARD_PAD), quarters(jnp.concatenate(gbig["w_out"], axis=0)),
             g_up.reshape(4, 2, D // 2, UP_SHARD), quarters(gbig["w_down"])]
    per_chip = [_add_pair(a, b, core) for a, b in zip(local, _swap_halves(local))]
    summed = [_add_chips(a, b, chip) for a, b in zip(per_chip, _scatter_chips(per_chip))]
    big_halves = dict(zip(("w_in", "w_out", "w_up", "w_down"), zip(summed, _swap_sums(summed))))
    grads = {}

    sizes = [6 * D] + [w[n].size for n in VECTORS[1:]] + [taps * cols for taps, cols in CONVS.values()]
    sent = _pack8([gsmall["mod"]] + [gsmall[n] for n in VECTORS[1:]] + [gsmall[n] for n in CONVS])
    got = _gather_rows(sent)
    d_mod_all = got.reshape(8, -1)[:, 0:6 * D]
    pieces = _unpack(_sum_devices(got).reshape(-1), sizes)
    for n, g in zip(VECTORS, pieces[:len(VECTORS)]):
        grads[n] = g.reshape(w[n].shape)
    for (n, (taps, cols)), g in zip(CONVS.items(), pieces[len(VECTORS):]):
        grads[n] = lax.dynamic_slice_in_dim(g.reshape(taps, cols), chip * (cols // 4), cols // 4, axis=1).reshape(w[n].shape)

    delta, new_m, new_v = {}, {}, {}
    for n in ("w_in", "w_out", "w_up", "w_down"):
        res = _adamw_halves(w[n][0], *big_halves[n], mom[n][0], var[n][0], core, "adamw_" + n)
        grads[n], delta[n], new_m[n], new_v[n] = [r[None] for r in res]
    d_mod_mine = lax.dynamic_slice_in_dim(d_mod_all, chip * (6 * D // 4), 6 * D // 4, axis=1)
    res = _ada_adamw(c_all.T, d_mod_mine, ada_w[0], m_ada_w[0], v_ada_w[0])
    grads["ada_w"], delta["ada_w"], new_m["ada_w"], new_v["ada_w"] = [r[None] for r in res]
    names = VECTORS + tuple(CONVS)
    res = _adamw(*[_pack8([d[n] for n in names]) for d in (w, grads, mom, var)], "adamw_small")
    sizes = [w[n].size for n in names]
    for d, r in zip((delta, new_m, new_v), res):
        for n, piece in zip(names, _unpack(r.reshape(-1), sizes)):
            d[n] = piece.reshape(w[n].shape)

    return (loss, grad_x[None], *[grads[n] for n in WEIGHTS], *[delta[n] for n in WEIGHTS],
            *[new_m[n] for n in WEIGHTS], *[new_v[n] for n in WEIGHTS])
```

```python
import functools

import jax
import jax.numpy as jnp
from jax import lax
from jax.experimental import pallas as pl
from jax.experimental.pallas import tpu as pltpu

f32 = jnp.float32
MX = jnp.bfloat16

D = 1024
HEADS = 16
HEAD_P = 64
STATE_N = 128
D_XBC = 1536
D_FF = 2816
UP_SHARD = 2 * D_FF // 4
K_SSD, K_CONF, K_FFN = 4, 31, 3
CHUNK = 64
OFF_Z, OFF_XBC, OFF_CA, OFF_CG, OFF_DT = 0, 1024, 2560, 3584, 4608
W_PACK = 4736
TM = 256
CW = 256
RC = 64
LANES = 128
VMEM_LIMIT = 56 * 1024 * 1024

ADAM_LR, ADAM_B1, ADAM_B2, ADAM_EPS, ADAM_WD, ADAM_STEP = 0.001, 0.9, 0.999, 1e-08, 0.01, 10

MESH = pl.DeviceIdType.MESH


def _cp(*sem):
    return pltpu.CompilerParams(dimension_semantics=sem, vmem_limit_bytes=VMEM_LIMIT)


def _resident(shape):
    nd = len(shape)
    return pl.BlockSpec(shape, lambda *_: (0,) * nd, pipeline_mode=pl.Buffered(1))


def _row(width=D):
    return pl.BlockSpec((1, width), lambda *_: (0, 0))


def _call(body, *, name, grid, in_specs, out_specs, out_shape, args, sem, scratch_shapes=(), rider=None):
    if rider is None:
        outs = pl.pallas_call(body, name=name, grid=grid, in_specs=list(in_specs), out_specs=tuple(out_specs),
                              out_shape=tuple(out_shape), scratch_shapes=list(scratch_shapes), compiler_params=_cp(*sem))(*args)
        return tuple(outs), ()
    ni, no, ns = len(in_specs), len(out_specs), len(scratch_shapes)
    ri, ro = len(rider.inputs), len(rider.out_shape)

    def full(*refs):
        base_in, r_in = refs[:ni], refs[ni:ni + ri]
        base_out, r_out = refs[ni + ri:ni + ri + no], refs[ni + ri + no:ni + ri + no + ro]
        base_scr, r_scr = refs[ni + ri + no + ro:ni + ri + no + ro + ns], refs[ni + ri + no + ro + ns:]
        ids = [pl.program_id(a) for a in range(len(grid))]
        first = functools.reduce(jnp.logical_and, [i == 0 for i in ids])
        last = functools.reduce(jnp.logical_and, [i == g - 1 for i, g in zip(ids, grid)])

        @pl.when(first)
        def _():
            rider.start(r_in, r_out, r_scr)

        body(*base_in, *base_out, *base_scr)

        @pl.when(last)
        def _():
            rider.finish(r_in, r_out, r_scr)

    outs = pl.pallas_call(
        full, name=name, grid=grid, in_specs=list(in_specs) + [ANY] * ri, out_specs=tuple(out_specs) + (ANY,) * ro,
        out_shape=tuple(out_shape) + tuple(rider.out_shape), scratch_shapes=list(scratch_shapes) + list(rider.scratch),
        input_output_aliases={ni + i: no + j for i, j in rider.aliases.items()}, compiler_params=_cp(*sem),
    )(*args, *rider.inputs)
    return tuple(outs[:no]), tuple(outs[no:])


def _silu(v):
    return v * jax.nn.sigmoid(v)


def _dsilu(v):
    s = jax.nn.sigmoid(v)
    return s * (1.0 + v * (1.0 - s))


def _softplus(v):
    return jnp.maximum(v, 0.0) + jnp.log1p(jnp.exp(-jnp.abs(v)))


def _mm(a, b):
    return jnp.dot(a.astype(MX), b.astype(MX), preferred_element_type=f32)


def _mm_nt(a, b):
    return lax.dot_general(a.astype(MX), b.astype(MX), (((1,), (1,)), ((), ())), preferred_element_type=f32)


def _mm_tn(a, b):
    return lax.dot_general(a.astype(MX), b.astype(MX), (((0,), (0,)), ((), ())), preferred_element_type=f32)


def _ln_inproj(x, mod, norm1_w, w_pack):
    t = x.shape[0]

    def body(x_ref, mod_ref, nw_ref, w_ref, proj_ref, h_ref):
        xv = x_ref[...]
        rstd = lax.rsqrt(jnp.mean(xv * xv, axis=-1, keepdims=True) + 1e-6)
        h = (xv * rstd * nw_ref[...]) * (1.0 + mod_ref[:, D:2 * D]) + mod_ref[:, 0:D]
        hb = h.astype(MX)
        h_ref[...] = hb
        proj_ref[...] = jnp.dot(hb, w_ref[...], preferred_element_type=f32)

    return pl.pallas_call(
        body, name="ln_inproj", grid=(t // TM,),
        out_shape=(jax.ShapeDtypeStruct((t, W_PACK), f32), jax.ShapeDtypeStruct((t, D), MX)),
        in_specs=[pl.BlockSpec((TM, D), lambda i: (i, 0)), _row(6 * D), _row(), _resident((D, W_PACK))],
        out_specs=(pl.BlockSpec((TM, W_PACK), lambda i: (i, 0)), pl.BlockSpec((TM, D), lambda i: (i, 0))),
        compiler_params=_cp("arbitrary"),
    )(x, mod, norm1_w, w_pack)


def _ssd_gate_norm(y_scan, xbc_act, proj, d_skip_row, ssd_norm_w):
    t = y_scan.shape[0]

    def body(y_ref, xs_ref, z_ref, dsk_ref, nw_ref, o_ref):
        y = y_ref[...] + xs_ref[...] * dsk_ref[...]
        yz = y * _silu(z_ref[...])
        rstd = lax.rsqrt(jnp.mean(yz * yz, axis=-1, keepdims=True) + 1e-6)
        o_ref[...] = (yz * rstd * nw_ref[...]).astype(MX)

    blk = pl.BlockSpec((TM, D), lambda i: (i, 0))
    return pl.pallas_call(
        body, name="ssd_gate_norm", grid=(t // TM,), out_shape=jax.ShapeDtypeStruct((t, D), MX),
        in_specs=[blk, blk, blk, _row(), _row()], out_specs=blk, compiler_params=_cp("arbitrary"),
    )(y_scan, xbc_act, proj, d_skip_row, ssd_norm_w)


def _ln_silu(u_conv, ln_w, ln_b):
    t = u_conv.shape[0]

    def body(u_ref, w_ref, b_ref, o_ref):
        u = u_ref[...]
        mu = jnp.mean(u, axis=-1, keepdims=True)
        uc = u - mu
        rstd = lax.rsqrt(jnp.mean(uc * uc, axis=-1, keepdims=True) + 1e-5)
        o_ref[...] = _silu(uc * rstd * w_ref[...] + b_ref[...]).astype(MX)

    blk = pl.BlockSpec((TM, D), lambda i: (i, 0))
    return pl.pallas_call(
        body, name="ln_silu", grid=(t // TM,), out_shape=jax.ShapeDtypeStruct((t, D), MX),
        in_specs=[blk, _row(), _row()], out_specs=blk, compiler_params=_cp("arbitrary"),
    )(u_conv, ln_w, ln_b)


def _outproj_ln2_up(y_ssd, u, w_out, x, mod, norm2_w, w_up):
    t = x.shape[0]

    def body(y_ref, u_ref, wo_ref, x_ref, mod_ref, nw_ref, wu_ref, mix_ref, x1_ref, h2_ref, up_ref):
        mix = jnp.dot(y_ref[...], wo_ref[0:D, :], preferred_element_type=f32)
        mix = mix + jnp.dot(u_ref[...], wo_ref[D:2 * D, :], preferred_element_type=f32)
        mix_ref[...] = mix
        x1 = x_ref[...] + mod_ref[:, 2 * D:3 * D] * mix
        x1_ref[...] = x1
        rstd = lax.rsqrt(jnp.mean(x1 * x1, axis=-1, keepdims=True) + 1e-6)
        h2 = ((x1 * rstd * nw_ref[...]) * (1.0 + mod_ref[:, 4 * D:5 * D]) + mod_ref[:, 3 * D:4 * D]).astype(MX)
        h2_ref[...] = h2
        for k in range(4):
            up_ref[:, k * UP_SHARD:(k + 1) * UP_SHARD] = jnp.dot(h2, wu_ref[k], preferred_element_type=f32)

    blk = pl.BlockSpec((TM, D), lambda i: (i, 0))
    return pl.pallas_call(
        body, name="outproj_ln2_up", grid=(t // TM,),
        out_shape=(jax.ShapeDtypeStruct((t, D), f32), jax.ShapeDtypeStruct((t, D), f32),
                   jax.ShapeDtypeStruct((t, D), MX), jax.ShapeDtypeStruct((t, 2 * D_FF), f32)),
        in_specs=[blk, blk, _resident((2 * D, D)), blk, _row(6 * D), _row(), _resident((4, D, UP_SHARD))],
        out_specs=(blk, blk, blk, pl.BlockSpec((TM, 2 * D_FF), lambda i: (i, 0))),
        compiler_params=_cp("arbitrary"),
    )(y_ssd, u, w_out, x, mod, norm2_w, w_up)


def _down_loss(act, w_down, x1, mod, final_norm_w, target):
    t = x1.shape[0]

    def body(a_ref, wd_ref, x1_ref, mod_ref, wf_ref, tgt_ref, dx2_ref, dffn_ref, dact_ref, st_ref):
        @pl.when(pl.program_id(0) == 0)
        def _():
            st_ref[...] = jnp.zeros_like(st_ref)

        g2 = mod_ref[:, 5 * D:6 * D]
        ffn = jnp.dot(a_ref[...], wd_ref[...], preferred_element_type=f32)
        x2 = x1_ref[...] + g2 * ffn
        rstd = lax.rsqrt(jnp.mean(x2 * x2, axis=-1, keepdims=True) + 1e-6)
        xh = x2 * rstd
        wf = wf_ref[...]
        err = xh * wf - tgt_ref[...]
        dy = err * (1.0 / D)
        dxh = dy * wf
        dx2 = rstd * (dxh - xh * jnp.mean(dxh * xh, axis=-1, keepdims=True))
        dx2_ref[...] = dx2
        dffn = (g2 * dx2).astype(MX)
        dffn_ref[...] = dffn
        dact_ref[...] = lax.dot_general(dffn, wd_ref[...], (((1,), (1,)), ((), ())), preferred_element_type=f32)
        st_ref[0:1, :] += jnp.sum(dy * xh, axis=0, keepdims=True)
        st_ref[1:2, :] += jnp.sum(dx2 * ffn, axis=0, keepdims=True)
        st_ref[2:3, :] += jnp.sum(0.5 * jnp.mean(err * err, axis=-1, keepdims=True), axis=0, keepdims=True)

    blk = pl.BlockSpec((TM, D), lambda i: (i, 0))
    ablk = pl.BlockSpec((TM, D_FF), lambda i: (i, 0))
    return pl.pallas_call(
        body, name="down_loss", grid=(t // TM,),
        out_shape=(jax.ShapeDtypeStruct((t, D), f32), jax.ShapeDtypeStruct((t, D), MX),
                   jax.ShapeDtypeStruct((t, D_FF), f32), jax.ShapeDtypeStruct((8, D), f32)),
        in_specs=[ablk, _resident((D_FF, D)), blk, _row(6 * D), _row(), blk],
        out_specs=(blk, blk, ablk, pl.BlockSpec((8, D), lambda i: (0, 0))),
        compiler_params=_cp("arbitrary"),
    )(act, w_down, x1, mod, final_norm_w, target)


def _pad_of(k):
    return 8 * ((k - 1 + 7) // 8)


def _causal_win(ref, r, t, pad):
    base = pl.multiple_of(r * RC, RC)
    prev = ref[pl.ds(pl.multiple_of(jnp.maximum(base - pad, 0), 8), pad), :]
    prev = jnp.where(r > 0, prev, 0.0)
    return jnp.concatenate([prev, ref[pl.ds(base, RC), :]], axis=0)


def _anti_win(ref, r, t, pad):
    base = pl.multiple_of(r * RC, RC)
    nxt = ref[pl.ds(pl.multiple_of(jnp.minimum(base + RC, t - pad), 8), pad), :]
    nxt = jnp.where(r < t // RC - 1, nxt, 0.0)
    return jnp.concatenate([ref[pl.ds(base, RC), :], nxt], axis=0)


def _conv_taps(win, w_ref, k, pad):
    acc = None
    for j in range(k):
        o = pad - (k - 1) + j
        term = w_ref[j:j + 1, :] * win[o:o + RC, :]
        acc = term if acc is None else acc + term
    return acc


def _corr_taps(win, w_ref, k):
    acc = None
    for j in range(k):
        o = (k - 1) - j
        term = w_ref[j:j + 1, :] * win[o:o + RC, :]
        acc = term if acc is None else acc + term
    return acc


def _dw_accumulate(dw_scr, d, win, k, pad):
    for j in range(k):
        o = pad - (k - 1) + j
        prod = d * win[o:o + RC, :]
        dw_scr[8 * j:8 * j + 8, :] += prod.reshape(RC // 8, 8, prod.shape[-1]).sum(axis=0)


def _dw_finish(dw_scr, dw_ref, k):
    for j in range(k):
        dw_ref[j:j + 1, :] = jnp.sum(dw_scr[8 * j:8 * j + 8, :], axis=0, keepdims=True)


def _rows8(v):
    return v.reshape(RC // 8, 8, v.shape[-1]).sum(axis=0)


def _ssd_conv_fwd(proj, conv_w, conv_b):
    t = proj.shape[0]
    pad = _pad_of(K_SSD)
    c0 = OFF_XBC // CW

    def body(x_ref, w_ref, b_ref, o_ref):
        def step(r, carry):
            win = _causal_win(x_ref, r, t, pad)
            o_ref[pl.ds(pl.multiple_of(r * RC, RC), RC), :] = _silu(_conv_taps(win, w_ref, K_SSD, pad) + b_ref[...])
            return carry
        lax.fori_loop(0, t // RC, step, 0)

    return pl.pallas_call(
        body, name="ssd_conv_fwd", grid=(D_XBC // CW,), out_shape=jax.ShapeDtypeStruct((t, D_XBC), f32),
        in_specs=[pl.BlockSpec((t, CW), lambda j: (0, c0 + j)), pl.BlockSpec((K_SSD, CW), lambda j: (0, j)),
                  pl.BlockSpec((1, CW), lambda j: (0, j))],
        out_specs=pl.BlockSpec((t, CW), lambda j: (0, j)), compiler_params=_cp("arbitrary"),
    )(proj, conv_w, conv_b)


def _glu_conv_fwd(proj, conv_w, conv_b):
    t = proj.shape[0]
    pad = _pad_of(K_CONF)
    ca, cg = OFF_CA // CW, OFF_CG // CW

    def body(a_ref, g_ref, w_ref, b_ref, o_ref, v_scr):
        def glu(r, carry):
            rows = pl.ds(pl.multiple_of(r * RC, RC), RC)
            v_scr[rows, :] = a_ref[rows, :] * jax.nn.sigmoid(g_ref[rows, :])
            return carry
        lax.fori_loop(0, t // RC, glu, 0)

        def step(r, carry):
            win = _causal_win(v_scr, r, t, pad)
            o_ref[pl.ds(pl.multiple_of(r * RC, RC), RC), :] = _conv_taps(win, w_ref, K_CONF, pad) + b_ref[...]
            return carry
        lax.fori_loop(0, t // RC, step, 0)

    return pl.pallas_call(
        body, name="glu_conv_fwd", grid=(D // CW,), out_shape=jax.ShapeDtypeStruct((t, D), f32),
        in_specs=[pl.BlockSpec((t, CW), lambda j: (0, ca + j)), pl.BlockSpec((t, CW), lambda j: (0, cg + j)),
                  pl.BlockSpec((K_CONF, CW), lambda j: (0, j)), pl.BlockSpec((1, CW), lambda j: (0, j))],
        out_specs=pl.BlockSpec((t, CW), lambda j: (0, j)),
        scratch_shapes=[pltpu.VMEM((t, CW), f32)], compiler_params=_cp("arbitrary"),
    )(proj, proj, conv_w, conv_b)


def _ffn_conv_fwd(up, conv_w, conv_b):
    t = up.shape[0]
    pad = _pad_of(K_FFN)
    nb = D_FF // CW

    def body(g_ref, v_ref, wg_ref, wv_ref, bg_ref, bv_ref, o_ref):
        def step(r, carry):
            gc = _conv_taps(_causal_win(g_ref, r, t, pad), wg_ref, K_FFN, pad) + bg_ref[...]
            vc = _conv_taps(_causal_win(v_ref, r, t, pad), wv_ref, K_FFN, pad) + bv_ref[...]
            o_ref[pl.ds(pl.multiple_of(r * RC, RC), RC), :] = (_silu(gc) * vc).astype(MX)
            return carry
        lax.fori_loop(0, t // RC, step, 0)

    return pl.pallas_call(
        body, name="ffn_conv_fwd", grid=(nb,), out_shape=jax.ShapeDtypeStruct((t, D_FF), MX),
        in_specs=[pl.BlockSpec((t, CW), lambda j: (0, j)), pl.BlockSpec((t, CW), lambda j: (0, nb + j)),
                  pl.BlockSpec((K_FFN, CW), lambda j: (0, j)), pl.BlockSpec((K_FFN, CW), lambda j: (0, nb + j)),
                  pl.BlockSpec((1, CW), lambda j: (0, j)), pl.BlockSpec((1, CW), lambda j: (0, nb + j))],
        out_specs=pl.BlockSpec((t, CW), lambda j: (0, j)), compiler_params=_cp("arbitrary"),
    )(up, up, conv_w, conv_w, conv_b, conv_b)


def _ffn_conv_bwd(up, conv_w, conv_b, d_act, rider=None):
    t = up.shape[0]
    pad = _pad_of(K_FFN)
    nb = D_FF // CW

    def body(g_ref, v_ref, wg_ref, wv_ref, bg_ref, bv_ref, da_ref, dup_ref, dw_ref, db_ref,
             dg_scr, dv_scr, dwg_scr, dwv_scr, db_scr):
        dwg_scr[...] = jnp.zeros_like(dwg_scr)
        dwv_scr[...] = jnp.zeros_like(dwv_scr)
        db_scr[...] = jnp.zeros_like(db_scr)

        def first(r, carry):
            rows = pl.ds(pl.multiple_of(r * RC, RC), RC)
            gwin = _causal_win(g_ref, r, t, pad)
            vwin = _causal_win(v_ref, r, t, pad)
            gc = _conv_taps(gwin, wg_ref, K_FFN, pad) + bg_ref[...]
            vc = _conv_taps(vwin, wv_ref, K_FFN, pad) + bv_ref[...]
            da = da_ref[rows, :]
            dgc = da * vc * _dsilu(gc)
            dvc = da * _silu(gc)
            dg_scr[rows, :] = dgc
            dv_scr[rows, :] = dvc
            _dw_accumulate(dwg_scr, dgc, gwin, K_FFN, pad)
            _dw_accumulate(dwv_scr, dvc, vwin, K_FFN, pad)
            db_scr[0:8, :] += _rows8(dgc)
            db_scr[8:16, :] += _rows8(dvc)
            return carry
        lax.fori_loop(0, t // RC, first, 0)

        def second(r, carry):
            rows = pl.ds(pl.multiple_of(r * RC, RC), RC)
            dup_ref[0, rows, :] = _corr_taps(_anti_win(dg_scr, r, t, pad), wg_ref, K_FFN).astype(MX)
            dup_ref[1, rows, :] = _corr_taps(_anti_win(dv_scr, r, t, pad), wv_ref, K_FFN).astype(MX)
            return carry
        lax.fori_loop(0, t // RC, second, 0)

        for j in range(K_FFN):
            dw_ref[0, j:j + 1, :] = jnp.sum(dwg_scr[8 * j:8 * j + 8, :], axis=0, keepdims=True)
            dw_ref[1, j:j + 1, :] = jnp.sum(dwv_scr[8 * j:8 * j + 8, :], axis=0, keepdims=True)
        db_ref[0] = jnp.sum(db_scr[0:8, :], axis=0, keepdims=True)
        db_ref[1] = jnp.sum(db_scr[8:16, :], axis=0, keepdims=True)

    return _call(
        body, name="ffn_conv_bwd", grid=(nb,),
        out_shape=(jax.ShapeDtypeStruct((2, t, D_FF), MX), jax.ShapeDtypeStruct((2, K_FFN, D_FF), f32),
                   jax.ShapeDtypeStruct((2, 1, D_FF), f32)),
        in_specs=[pl.BlockSpec((t, CW), lambda j: (0, j)), pl.BlockSpec((t, CW), lambda j: (0, nb + j)),
                  pl.BlockSpec((K_FFN, CW), lambda j: (0, j)), pl.BlockSpec((K_FFN, CW), lambda j: (0, nb + j)),
                  pl.BlockSpec((1, CW), lambda j: (0, j)), pl.BlockSpec((1, CW), lambda j: (0, nb + j)),
                  pl.BlockSpec((t, CW), lambda j: (0, j))],
        out_specs=(pl.BlockSpec((2, t, CW), lambda j: (0, 0, j)), pl.BlockSpec((2, K_FFN, CW), lambda j: (0, 0, j)),
                   pl.BlockSpec((2, 1, CW), lambda j: (0, 0, j))),
        scratch_shapes=[pltpu.VMEM((t, CW), f32), pltpu.VMEM((t, CW), f32), pltpu.VMEM((8 * K_FFN, CW), f32),
                        pltpu.VMEM((8 * K_FFN, CW), f32), pltpu.VMEM((16, CW), f32)],
        sem=("arbitrary",), args=(up, up, conv_w, conv_w, conv_b, conv_b, d_act), rider=rider)


def _glu_conv_bwd(proj, conv_w, d_uconv, rider=None):
    t = proj.shape[0]
    pad = _pad_of(K_CONF)
    ca, cg = OFF_CA // CW, OFF_CG // CW

    def body(a_ref, g_ref, w_ref, du_ref, dc_ref, dw_ref, db_ref, v_scr, dw_scr, db_scr):
        dw_scr[...] = jnp.zeros_like(dw_scr)
        db_scr[...] = jnp.zeros_like(db_scr)

        def glu(r, carry):
            rows = pl.ds(pl.multiple_of(r * RC, RC), RC)
            v_scr[rows, :] = a_ref[rows, :] * jax.nn.sigmoid(g_ref[rows, :])
            return carry
        lax.fori_loop(0, t // RC, glu, 0)

        def step(r, carry):
            rows = pl.ds(pl.multiple_of(r * RC, RC), RC)
            du = du_ref[rows, :]
            _dw_accumulate(dw_scr, du, _causal_win(v_scr, r, t, pad), K_CONF, pad)
            db_scr[...] += _rows8(du)
            dv = _corr_taps(_anti_win(du_ref, r, t, pad), w_ref, K_CONF)
            a = a_ref[rows, :]
            s = jax.nn.sigmoid(g_ref[rows, :])
            dc_ref[0, rows, :] = (dv * s).astype(MX)
            dc_ref[1, rows, :] = (dv * a * s * (1.0 - s)).astype(MX)
            return carry
        lax.fori_loop(0, t // RC, step, 0)
        _dw_finish(dw_scr, dw_ref, K_CONF)
        db_ref[...] = jnp.sum(db_scr[...], axis=0, keepdims=True)

    return _call(
        body, name="glu_conv_bwd", grid=(D // CW,),
        out_shape=(jax.ShapeDtypeStruct((2, t, D), MX), jax.ShapeDtypeStruct((K_CONF, D), f32),
                   jax.ShapeDtypeStruct((1, D), f32)),
        in_specs=[pl.BlockSpec((t, CW), lambda j: (0, ca + j)), pl.BlockSpec((t, CW), lambda j: (0, cg + j)),
                  pl.BlockSpec((K_CONF, CW), lambda j: (0, j)), pl.BlockSpec((t, CW), lambda j: (0, j))],
        out_specs=(pl.BlockSpec((2, t, CW), lambda j: (0, 0, j)), pl.BlockSpec((K_CONF, CW), lambda j: (0, j)),
                   pl.BlockSpec((1, CW), lambda j: (0, j))),
        scratch_shapes=[pltpu.VMEM((t, CW), f32), pltpu.VMEM((8 * K_CONF, CW), f32), pltpu.VMEM((8, CW), f32)],
        sem=("arbitrary",), args=(proj, proj, conv_w, d_uconv), rider=rider)


def _ssd_conv_bwd_x(proj, conv_w, conv_b, d_xs, d_y, d_skip_row):
    t = proj.shape[0]
    pad = _pad_of(K_SSD)
    c0 = OFF_XBC // CW

    def body(x_ref, w_ref, b_ref, dxs_ref, dy_ref, dsk_ref, draw_ref, dw_ref, db_ref, dp_scr, dw_scr, db_scr):
        dw_scr[...] = jnp.zeros_like(dw_scr)
        db_scr[...] = jnp.zeros_like(db_scr)

        def first(r, carry):
            rows = pl.ds(pl.multiple_of(r * RC, RC), RC)
            win = _causal_win(x_ref, r, t, pad)
            pre = _conv_taps(win, w_ref, K_SSD, pad) + b_ref[...]
            dpre = (dxs_ref[rows, :] + dy_ref[rows, :] * dsk_ref[...]) * _dsilu(pre)
            dp_scr[rows, :] = dpre
            _dw_accumulate(dw_scr, dpre, win, K_SSD, pad)
            db_scr[...] += _rows8(dpre)
            return carry
        lax.fori_loop(0, t // RC, first, 0)

        def second(r, carry):
            rows = pl.ds(pl.multiple_of(r * RC, RC), RC)
            draw_ref[rows, :] = _corr_taps(_anti_win(dp_scr, r, t, pad), w_ref, K_SSD).astype(MX)
            return carry
        lax.fori_loop(0, t // RC, second, 0)
        _dw_finish(dw_scr, dw_ref, K_SSD)
        db_ref[...] = jnp.sum(db_scr[...], axis=0, keepdims=True)

    cb = pl.BlockSpec((t, CW), lambda j: (0, j))
    return pl.pallas_call(
        body, name="ssd_conv_bwd_x", grid=(D // CW,),
        out_shape=(jax.ShapeDtypeStruct((t, D), MX), jax.ShapeDtypeStruct((K_SSD, D), f32),
                   jax.ShapeDtypeStruct((1, D), f32)),
        in_specs=[pl.BlockSpec((t, CW), lambda j: (0, c0 + j)), pl.BlockSpec((K_SSD, CW), lambda j: (0, j)),
                  pl.BlockSpec((1, CW), lambda j: (0, j)), cb, cb, pl.BlockSpec((1, CW), lambda j: (0, j))],
        out_specs=(cb, pl.BlockSpec((K_SSD, CW), lambda j: (0, j)), pl.BlockSpec((1, CW), lambda j: (0, j))),
        scratch_shapes=[pltpu.VMEM((t, CW), f32), pltpu.VMEM((8 * K_SSD, CW), f32), pltpu.VMEM((8, CW), f32)],
        compiler_params=_cp("arbitrary"),
    )(proj, conv_w, conv_b, d_xs, d_y, d_skip_row)


def _ssd_conv_bwd_bc(proj, conv_w, conv_b, d_bc):
    t = proj.shape[0]
    pad = _pad_of(K_SSD)
    c0 = (OFF_XBC + D) // CW
    w0 = D // CW

    def body(x_ref, w_ref, b_ref, dbc_ref, draw_ref, dw_ref, db_ref, dp_scr, dw_scr, db_scr):
        dw_scr[...] = jnp.zeros_like(dw_scr)
        db_scr[...] = jnp.zeros_like(db_scr)

        def first(r, carry):
            rows = pl.ds(pl.multiple_of(r * RC, RC), RC)
            win = _causal_win(x_ref, r, t, pad)
            pre = _conv_taps(win, w_ref, K_SSD, pad) + b_ref[...]
            dpre = dbc_ref[0, rows, :] * _dsilu(pre)
            dp_scr[rows, :] = dpre
            _dw_accumulate(dw_scr, dpre, win, K_SSD, pad)
            db_scr[...] += _rows8(dpre)
            return carry
        lax.fori_loop(0, t // RC, first, 0)

        def second(r, carry):
            rows = pl.ds(pl.multiple_of(r * RC, RC), RC)
            draw_ref[rows, :] = _corr_taps(_anti_win(dp_scr, r, t, pad), w_ref, K_SSD).astype(MX)
            return carry
        lax.fori_loop(0, t // RC, second, 0)
        _dw_finish(dw_scr, dw_ref, K_SSD)
        db_ref[...] = jnp.sum(db_scr[...], axis=0, keepdims=True)

    return pl.pallas_call(
        body, name="ssd_conv_bwd_bc", grid=(2,),
        out_shape=(jax.ShapeDtypeStruct((t, 2 * CW), MX), jax.ShapeDtypeStruct((K_SSD, 2 * CW), f32),
                   jax.ShapeDtypeStruct((1, 2 * CW), f32)),
        in_specs=[pl.BlockSpec((t, CW), lambda j: (0, c0 + j)), pl.BlockSpec((K_SSD, CW), lambda j: (0, w0 + j)),
                  pl.BlockSpec((1, CW), lambda j: (0, w0 + j)), pl.BlockSpec((1, t, CW), lambda j: (j, 0, 0))],
        out_specs=(pl.BlockSpec((t, CW), lambda j: (0, j)), pl.BlockSpec((K_SSD, CW), lambda j: (0, j)),
                   pl.BlockSpec((1, CW), lambda j: (0, j))),
        scratch_shapes=[pltpu.VMEM((t, CW), f32), pltpu.VMEM((8 * K_SSD, CW), f32), pltpu.VMEM((8, CW), f32)],
        compiler_params=_cp("arbitrary"),
    )(proj, conv_w, conv_b, d_bc)


def _chunk_masks():
    ii = lax.broadcasted_iota(jnp.int32, (CHUNK, CHUNK), 0)
    jj = lax.broadcasted_iota(jnp.int32, (CHUNK, CHUNK), 1)
    return ii == jj, jj <= ii, jj >= ii


def _to_row(col, eye):
    return jnp.sum(jnp.where(eye, col, 0.0), axis=0, keepdims=True)


def _to_col(row, eye):
    return jnp.sum(jnp.where(eye, row, 0.0), axis=1, keepdims=True)


def _head_decay(dt_h, a_h, eye, tril):
    a_row = _to_row(dt_h * a_h, eye)
    cs = jnp.sum(jnp.where(tril, a_row, 0.0), axis=1, keepdims=True)
    cs_row = _to_row(cs, eye)
    decay = jnp.where(tril, jnp.exp(jnp.where(tril, cs - cs_row, 0.0)), 0.0)
    total = jnp.sum(a_row, axis=1, keepdims=True)
    return cs, decay, total


def _lane_pick(mat, lane, which):
    return jnp.sum(jnp.where(lane == which, mat, 0.0), axis=1, keepdims=True)


def _ssd_fwd(xbc_act, proj, dt_bias_row, a_log_row, rider=None):
    t = xbc_act.shape[0]
    nc = t // CHUNK
    cb, cc, cdt = D // LANES, (D + 2 * STATE_N) // LANES, OFF_DT // LANES

    def body(x_ref, b_ref, c_ref, dt_ref, dtb_ref, alog_ref, y_ref, st_ref):
        j = pl.program_id(0)
        eye, tril, _ = _chunk_masks()
        lane = lax.broadcasted_iota(jnp.int32, (1, LANES), 1)
        first = lane < HEAD_P
        a_row = -jnp.exp(alog_ref[...])
        a_heads = [jnp.sum(jnp.where(lane == 2 * j + h, a_row, 0.0), axis=1, keepdims=True) for h in range(2)]

        def chunk(c, hprev):
            rows = pl.ds(pl.multiple_of(c * CHUNK, CHUNK), CHUNK)
            xv, bm, cm = x_ref[rows, :], b_ref[rows, :], c_ref[rows, :]
            dt = _softplus(dt_ref[rows, :] + dtb_ref[...])
            st_ref[c] = hprev
            g = _mm_nt(cm, bm)
            ch = _mm(cm, hprev)
            dts = [_lane_pick(dt, lane, 2 * j + h) for h in range(2)]
            xdt = xv * jnp.where(first, dts[0], dts[1])
            ys, hs = [], []
            for h in range(2):
                cs, decay, total = _head_decay(dts[h], a_heads[h], eye, tril)
                y = _mm(g * decay, xdt) + jnp.exp(cs) * ch
                s = _mm_tn(bm * jnp.exp(total - cs), xdt)
                ys.append(y)
                hs.append(jnp.exp(total) * hprev + s)
            y_ref[rows, :] = jnp.where(first, ys[0], ys[1])
            return jnp.where(first, hs[0], hs[1])

        lax.fori_loop(0, nc, chunk, jnp.zeros((STATE_N, LANES), f32))

    blk = lambda f: pl.BlockSpec((t, LANES), f)
    return _call(
        body, name="ssd_fwd", grid=(D // LANES,),
        out_shape=(jax.ShapeDtypeStruct((t, D), f32), jax.ShapeDtypeStruct((nc, STATE_N, D), f32)),
        in_specs=[blk(lambda j: (0, j)), blk(lambda j: (0, cb + j // 4)), blk(lambda j: (0, cc + j // 4)),
                  blk(lambda j: (0, cdt)), _row(LANES), _row(LANES)],
        out_specs=(blk(lambda j: (0, j)), pl.BlockSpec((nc, STATE_N, LANES), lambda j: (0, 0, j))),
        sem=("arbitrary",), args=(xbc_act, xbc_act, xbc_act, proj, dt_bias_row, a_log_row), rider=rider)


def _ssd_bwd(xbc_act, proj, dt_bias_row, a_log_row, states, d_y, rider=None):
    t = xbc_act.shape[0]
    nc = t // CHUNK
    cb, cc, cdt = D // LANES, (D + 2 * STATE_N) // LANES, OFF_DT // LANES

    def body(x_ref, b_ref, c_ref, dt_ref, dtb_ref, alog_ref, st_ref, dy_ref, dx_ref, dbc_ref, ddt_ref, da_ref):
        grp, p = pl.program_id(0), pl.program_id(1)
        j = 4 * grp + p
        eye, tril, triu = _chunk_masks()
        lane = lax.broadcasted_iota(jnp.int32, (1, LANES), 1)
        first = lane < HEAD_P
        last_row = lax.broadcasted_iota(jnp.int32, (CHUNK, 1), 0) == CHUNK - 1
        a_row = -jnp.exp(alog_ref[...])
        a_heads = [jnp.sum(jnp.where(lane == 2 * j + h, a_row, 0.0), axis=1, keepdims=True) for h in range(2)]

        @pl.when(p == 0)
        def _():
            dbc_ref[...] = jnp.zeros_like(dbc_ref)

        @pl.when(j == 0)
        def _():
            ddt_ref[...] = jnp.zeros_like(ddt_ref)
            da_ref[...] = jnp.zeros_like(da_ref)

        def chunk(i, dh):
            c = nc - 1 - i
            rows = pl.ds(pl.multiple_of(c * CHUNK, CHUNK), CHUNK)
            xv, bm, cm = x_ref[rows, :], b_ref[rows, :], c_ref[rows, :]
            dtr = dt_ref[rows, :] + dtb_ref[...]
            dt = _softplus(dtr)
            hprev = st_ref[c]
            dy = dy_ref[rows, :]
            g = _mm_nt(cm, bm)
            dts = [_lane_pick(dt, lane, 2 * j + h) for h in range(2)]
            xdt = xv * jnp.where(first, dts[0], dts[1])
            dxs, dhs = [], []
            db_sum, dc_sum = None, None
            ddt_mat = jnp.zeros((CHUNK, LANES), f32)
            da_acc = jnp.zeros((1, LANES), f32)
            for h in range(2):
                mine = first if h == 0 else jnp.logical_not(first)
                cs, decay, total = _head_decay(dts[h], a_heads[h], eye, tril)
                e_cs, e_tot = jnp.exp(cs), jnp.exp(total)
                dec_s = jnp.exp(total - cs)
                dyh = jnp.where(mine, dy, 0.0)
                xdth = jnp.where(mine, xdt, 0.0)
                dhh = jnp.where(mine, dh, 0.0)
                hph = jnp.where(mine, hprev, 0.0)
                m = g * decay
                dm = _mm_nt(dyh, xdth)
                dg = dm * decay
                w = dm * m
                bdec = bm * dec_s
                dxdt = _mm_tn(m, dyh) + _mm(bdec, dhh)
                dc_off = _mm_nt(dyh, hph) * e_cs
                db_s = _mm_nt(xdth, dhh) * dec_s
                dc_h = _mm(dg, bm) + dc_off
                db_h = _mm_tn(dg, cm) + db_s
                r_s = jnp.sum(db_s * bm, axis=1, keepdims=True)
                dtotal = jnp.sum(r_s, axis=0, keepdims=True) + e_tot * jnp.sum(
                    jnp.sum(dhh * hph, axis=1, keepdims=True), axis=0, keepdims=True)
                dcs = (jnp.sum(w, axis=1, keepdims=True) - _to_col(jnp.sum(w, axis=0, keepdims=True), eye)
                       + jnp.sum(dc_off * cm, axis=1, keepdims=True) - r_s + jnp.where(last_row, dtotal, 0.0))
                da_col = jnp.sum(jnp.where(triu, _to_row(dcs, eye), 0.0), axis=1, keepdims=True)
                ddt = da_col * a_heads[h] + jnp.sum(jnp.where(mine, dxdt * xv, 0.0), axis=1, keepdims=True)
                ddt_mat = ddt_mat + jnp.where(lane == 2 * j + h, ddt, 0.0)
                da_acc = da_acc + jnp.where(lane == 2 * j + h, jnp.sum(da_col * dts[h], axis=0, keepdims=True), 0.0)
                dxs.append(dxdt * dts[h])
                dhs.append(e_tot * dhh + _mm_tn(cm * e_cs, dyh))
                db_sum = db_h if db_sum is None else db_sum + db_h
                dc_sum = dc_h if dc_sum is None else dc_sum + dc_h
            dx_ref[rows, :] = jnp.where(first, dxs[0], dxs[1])
            dbc_ref[0, rows, :] += db_sum
            dbc_ref[1, rows, :] += dc_sum
            ddt_ref[rows, :] += ddt_mat * jax.nn.sigmoid(dtr)
            da_ref[...] += da_acc * a_row
            return jnp.where(first, dhs[0], dhs[1])

        lax.fori_loop(0, nc, chunk, jnp.zeros((STATE_N, LANES), f32))

    blk = lambda f: pl.BlockSpec((t, LANES), f)
    return _call(
        body, name="ssd_bwd", grid=(2, 4),
        out_shape=(jax.ShapeDtypeStruct((t, D), f32), jax.ShapeDtypeStruct((2, t, 2 * STATE_N), f32),
                   jax.ShapeDtypeStruct((t, LANES), f32), jax.ShapeDtypeStruct((1, LANES), f32)),
        in_specs=[blk(lambda g, p: (0, 4 * g + p)), blk(lambda g, p: (0, cb + g)), blk(lambda g, p: (0, cc + g)),
                  blk(lambda g, p: (0, cdt)), _row(LANES), _row(LANES),
                  pl.BlockSpec((nc, STATE_N, LANES), lambda g, p: (0, 0, 4 * g + p)), blk(lambda g, p: (0, 4 * g + p))],
        out_specs=(blk(lambda g, p: (0, 4 * g + p)), pl.BlockSpec((2, t, LANES), lambda g, p: (0, 0, g)),
                   blk(lambda g, p: (0, 0)), _row(LANES)),
        sem=("arbitrary", "arbitrary"), args=(xbc_act, xbc_act, xbc_act, proj, dt_bias_row, a_log_row, states, d_y),
        rider=rider)


def _up_bwd(d_up, w_up, x1, mod, norm2_w, dx2, mix, w_out):
    t = x1.shape[0]

    def body(dup_ref, wu_ref, x1_ref, mod_ref, nw_ref, dx2_ref, mix_ref, wo_ref,
             dx1_ref, dmix_ref, dys_ref, du_ref, st_ref):
        @pl.when(pl.program_id(0) == 0)
        def _():
            st_ref[...] = jnp.zeros_like(st_ref)

        nt = (((1,), (1,)), ((), ()))
        dh = None
        for k in range(4):
            lo = (k % 2) * UP_SHARD
            part = lax.dot_general(dup_ref[k // 2, :, lo:lo + UP_SHARD], wu_ref[k], nt, preferred_element_type=f32)
            dh = part if dh is None else dh + part
        x1 = x1_ref[...]
        rstd = lax.rsqrt(jnp.mean(x1 * x1, axis=-1, keepdims=True) + 1e-6)
        xh = x1 * rstd
        nw = nw_ref[...]
        sc = 1.0 + mod_ref[:, 4 * D:5 * D]
        st_ref[0:1, :] += jnp.sum(dh, axis=0, keepdims=True)
        st_ref[1:2, :] += jnp.sum(dh * xh * nw, axis=0, keepdims=True)
        st_ref[2:3, :] += jnp.sum(dh * sc * xh, axis=0, keepdims=True)
        dxh = dh * sc * nw
        dx1 = dx2_ref[...] + rstd * (dxh - xh * jnp.mean(dxh * xh, axis=-1, keepdims=True))
        dx1_ref[...] = dx1
        st_ref[3:4, :] += jnp.sum(dx1 * mix_ref[...], axis=0, keepdims=True)
        dmix = (mod_ref[:, 2 * D:3 * D] * dx1).astype(MX)
        dmix_ref[...] = dmix
        dys_ref[...] = lax.dot_general(dmix, wo_ref[0:D, :], nt, preferred_element_type=f32)
        du_ref[...] = lax.dot_general(dmix, wo_ref[D:2 * D, :], nt, preferred_element_type=f32)

    blk = pl.BlockSpec((TM, D), lambda i: (i, 0))
    return pl.pallas_call(
        body, name="up_bwd", grid=(t // TM,),
        out_shape=(jax.ShapeDtypeStruct((t, D), f32), jax.ShapeDtypeStruct((t, D), MX),
                   jax.ShapeDtypeStruct((t, D), f32), jax.ShapeDtypeStruct((t, D), f32),
                   jax.ShapeDtypeStruct((8, D), f32)),
        in_specs=[pl.BlockSpec((2, TM, D_FF), lambda i: (0, i, 0)), _resident((4, D, UP_SHARD)), blk, _row(6 * D), _row(),
                  blk, blk, _resident((2 * D, D))],
        out_specs=(blk, blk, blk, blk, pl.BlockSpec((8, D), lambda i: (0, 0))),
        compiler_params=_cp("arbitrary"),
    )(d_up, w_up, x1, mod, norm2_w, dx2, mix, w_out)


def _ln_silu_bwd(d_u, u_conv, ln_w, ln_b):
    t = d_u.shape[0]

    def body(du_ref, u_ref, w_ref, b_ref, o_ref, st_ref):
        @pl.when(pl.program_id(0) == 0)
        def _():
            st_ref[...] = jnp.zeros_like(st_ref)

        u = u_ref[...]
        mu = jnp.mean(u, axis=-1, keepdims=True)
        uc = u - mu
        rstd = lax.rsqrt(jnp.mean(uc * uc, axis=-1, keepdims=True) + 1e-5)
        n = uc * rstd
        w = w_ref[...]
        dl = du_ref[...] * _dsilu(n * w + b_ref[...])
        st_ref[0:1, :] += jnp.sum(dl * n, axis=0, keepdims=True)
        st_ref[1:2, :] += jnp.sum(dl, axis=0, keepdims=True)
        dn = dl * w
        o_ref[...] = rstd * (dn - jnp.mean(dn, axis=-1, keepdims=True) - n * jnp.mean(dn * n, axis=-1, keepdims=True))

    blk = pl.BlockSpec((TM, D), lambda i: (i, 0))
    return pl.pallas_call(
        body, name="ln_silu_bwd", grid=(t // TM,),
        out_shape=(jax.ShapeDtypeStruct((t, D), f32), jax.ShapeDtypeStruct((8, D), f32)),
        in_specs=[blk, blk, _row(), _row()], out_specs=(blk, pl.BlockSpec((8, D), lambda i: (0, 0))),
        compiler_params=_cp("arbitrary"),
    )(d_u, u_conv, ln_w, ln_b)


def _ssd_gate_norm_bwd(d_out, y_scan, xbc_act, proj, d_skip_row, ssd_norm_w):
    t = d_out.shape[0]

    def body(do_ref, y_ref, xs_ref, z_ref, dsk_ref, nw_ref, dy_ref, dz_ref, st_ref):
        @pl.when(pl.program_id(0) == 0)
        def _():
            st_ref[...] = jnp.zeros_like(st_ref)

        xs = xs_ref[...]
        y = y_ref[...] + xs * dsk_ref[...]
        z = z_ref[...]
        s = _silu(z)
        yz = y * s
        rstd = lax.rsqrt(jnp.mean(yz * yz, axis=-1, keepdims=True) + 1e-6)
        n = yz * rstd
        do = do_ref[...]
        st_ref[0:1, :] += jnp.sum(do * n, axis=0, keepdims=True)
        dn = do * nw_ref[...]
        dyz = rstd * (dn - n * jnp.mean(dn * n, axis=-1, keepdims=True))
        dy = dyz * s
        dy_ref[...] = dy
        dz_ref[...] = (dyz * y * _dsilu(z)).astype(MX)
        st_ref[1:2, :] += jnp.sum(dy * xs, axis=0, keepdims=True)

    blk = pl.BlockSpec((TM, D), lambda i: (i, 0))
    return pl.pallas_call(
        body, name="ssd_gate_norm_bwd", grid=(t // TM,),
        out_shape=(jax.ShapeDtypeStruct((t, D), f32), jax.ShapeDtypeStruct((t, D), MX), jax.ShapeDtypeStruct((8, D), f32)),
        in_specs=[blk, blk, blk, blk, _row(), _row()], out_specs=(blk, blk, pl.BlockSpec((8, D), lambda i: (0, 0))),
        compiler_params=_cp("arbitrary"),
    )(d_out, y_scan, xbc_act, proj, d_skip_row, ssd_norm_w)


def _inproj_bwd(d_z, d_xraw, d_bcraw, d_conf, d_dt, w_pack, x, mod, norm1_w, dx1, rider=None):
    t = x.shape[0]

    def body(dz_ref, dx_ref, dbc_ref, dcf_ref, ddt_ref, w_ref, x_ref, mod_ref, nw_ref, dx1_ref, gx_ref, st_ref):
        @pl.when(pl.program_id(0) == 0)
        def _():
            st_ref[...] = jnp.zeros_like(st_ref)

        nt = (((1,), (1,)), ((), ()))
        dot = lambda a, lo, hi: lax.dot_general(a, w_ref[:, lo:hi], nt, preferred_element_type=f32)
        dh = dot(dz_ref[...], OFF_Z, OFF_Z + D)
        dh = dh + dot(dx_ref[...], OFF_XBC, OFF_XBC + D)
        dh = dh + dot(dbc_ref[...], OFF_XBC + D, OFF_XBC + D_XBC)
        dh = dh + dot(dcf_ref[0], OFF_CA, OFF_CA + D)
        dh = dh + dot(dcf_ref[1], OFF_CG, OFF_CG + D)
        dh = dh + dot(ddt_ref[...].astype(MX), OFF_DT, OFF_DT + LANES)
        st_ref[3:4, 0:LANES] += jnp.sum(ddt_ref[...], axis=0, keepdims=True)
        xv = x_ref[...]
        rstd = lax.rsqrt(jnp.mean(xv * xv, axis=-1, keepdims=True) + 1e-6)
        xh = xv * rstd
        nw = nw_ref[...]
        sc = 1.0 + mod_ref[:, D:2 * D]
        st_ref[0:1, :] += jnp.sum(dh, axis=0, keepdims=True)
        st_ref[1:2, :] += jnp.sum(dh * xh * nw, axis=0, keepdims=True)
        st_ref[2:3, :] += jnp.sum(dh * sc * xh, axis=0, keepdims=True)
        dxh = dh * sc * nw
        gx_ref[...] = dx1_ref[...] + rstd * (dxh - xh * jnp.mean(dxh * xh, axis=-1, keepdims=True))

    blk = pl.BlockSpec((TM, D), lambda i: (i, 0))
    return _call(
        body, name="inproj_bwd", grid=(t // TM,),
        out_shape=(jax.ShapeDtypeStruct((t, D), f32), jax.ShapeDtypeStruct((8, D), f32)),
        in_specs=[blk, blk, pl.BlockSpec((TM, 2 * CW), lambda i: (i, 0)), pl.BlockSpec((2, TM, D), lambda i: (0, i, 0)),
                  pl.BlockSpec((TM, LANES), lambda i: (i, 0)), _resident((D, W_PACK)), blk, _row(6 * D), _row(), blk],
        out_specs=(blk, pl.BlockSpec((8, D), lambda i: (0, 0))),
        sem=("arbitrary",), args=(d_z, d_xraw, d_bcraw, d_conf, d_dt, w_pack, x, mod, norm1_w, dx1), rider=rider)


def _wgrad(a, d, name, bn=256):
    t, k = a.shape
    n = d.shape[1]
    out_dtype = MX

    def body(a_ref, d_ref, o_ref):
        o_ref[...] = lax.dot_general(a_ref[...], d_ref[...].astype(MX), (((0,), (0,)), ((), ())),
                                     preferred_element_type=f32).astype(out_dtype)

    return pl.pallas_call(
        body, name=name, grid=(n // bn,), out_shape=jax.ShapeDtypeStruct((k, n), out_dtype),
        in_specs=[_resident((t, k)), pl.BlockSpec((t, bn), lambda j: (0, j))],
        out_specs=pl.BlockSpec((k, bn), lambda j: (0, j)), compiler_params=_cp("arbitrary"),
    )(a, d)


def _wgrad_stacked(a, d, name, bn):
    out_dtype = MX
    t, k = a.shape
    s, _, n = d.shape
    nb = n // bn

    def body(a_ref, d_ref, o_ref):
        o_ref[0] = lax.dot_general(a_ref[...], d_ref[0], (((0,), (0,)), ((), ())),
                                   preferred_element_type=f32).astype(out_dtype)

    return pl.pallas_call(
        body, name=name, grid=(s, nb), out_shape=jax.ShapeDtypeStruct((s * nb, k, bn), out_dtype),
        in_specs=[_resident((t, k)), pl.BlockSpec((1, t, bn), lambda i, j: (i, 0, j))],
        out_specs=pl.BlockSpec((1, k, bn), lambda i, j: (i * nb + j, 0, 0)), compiler_params=_cp("arbitrary", "arbitrary"),
    )(a, d)


def _pad_row(v, width=LANES):
    return jnp.pad(v.reshape(1, -1), ((0, 0), (0, width - v.size)))


def _quarters(a):
    return a.reshape(4, 2, a.shape[0] // 8, a.shape[1])


def _local_step(x, mod, target, w_pack, late, small, reducer=None):
    dtb_row, alog_row = _pad_row(small["dt_bias"]), _pad_row(small["a_log"])
    dskip_row = jnp.repeat(small["d_skip"].reshape(-1), HEAD_P).reshape(1, D)

    def reduce_on(name, grad, host, *args):
        if reducer is None:
            return host(*args)[0]
        outs, others = host(*args, rider=reducer.begin(name, grad))
        reducer.end(name, others)
        return outs

    proj, h = _ln_inproj(x, mod, small["norm1_w"], w_pack)
    xbc_act = _ssd_conv_fwd(proj, small["ssd_conv_w"], small["ssd_conv_b"])
    if isinstance(late, _GatherRider):
        (y_scan, states), (a_out, a_up, a_down) = _ssd_fwd(xbc_act, proj, dtb_row, alog_row, rider=late)
        w_out, w_up, w_down = a_out.reshape(2 * D, D), a_up.reshape(4, D, UP_SHARD), a_down.reshape(D_FF, D)
    else:
        (y_scan, states), _ = _ssd_fwd(xbc_act, proj, dtb_row, alog_row)
        w_out, w_up, w_down = late
    y_ssd = _ssd_gate_norm(y_scan, xbc_act, proj, dskip_row, small["ssd_norm_w"])
    u_conv = _glu_conv_fwd(proj, small["conf_conv_w"], small["conf_conv_b"])
    u = _ln_silu(u_conv, small["conf_ln_w"], small["conf_ln_b"])
    mix, x1, h2, up = _outproj_ln2_up(y_ssd, u, w_out, x, mod, small["norm2_w"], w_up)
    act = _ffn_conv_fwd(up, small["ffn_conv_w"], small["ffn_conv_b"])
    dx2, d_ffn, d_act, st_down = _down_loss(act, w_down, x1, mod, small["final_norm_w"], target)

    g_down = _quarters(_wgrad(act, d_ffn, "wgrad_down"))
    d_up, dw_ffn, db_ffn = reduce_on("w_down", g_down, _ffn_conv_bwd, up, small["ffn_conv_w"], small["ffn_conv_b"], d_act)
    g_up = _wgrad_stacked(h2, d_up, "wgrad_up", D_FF // 2).reshape(4, 2, D // 2, UP_SHARD)
    dx1, d_mix, d_yssd, d_u, st_up = _up_bwd(d_up, w_up, x1, mod, small["norm2_w"], dx2, mix, w_out)
    g_out = _quarters(jnp.concatenate([_wgrad(y_ssd, d_mix, "wgrad_out_y"), _wgrad(u, d_mix, "wgrad_out_u")], axis=0))
    d_uconv, st_ln = _ln_silu_bwd(d_u, u_conv, small["conf_ln_w"], small["conf_ln_b"])
    d_conf, dw_conf, db_conf = reduce_on("w_up", g_up, _glu_conv_bwd, proj, small["conf_conv_w"], d_uconv)
    d_y, d_z, st_gn = _ssd_gate_norm_bwd(d_yssd, y_scan, xbc_act, proj, dskip_row, small["ssd_norm_w"])
    d_xs, d_bc, d_dt, d_alog = reduce_on("w_out", g_out, _ssd_bwd, xbc_act, proj, dtb_row, alog_row, states, d_y)
    d_xraw, dw_sx, db_sx = _ssd_conv_bwd_x(proj, small["ssd_conv_w"], small["ssd_conv_b"], d_xs, d_y, dskip_row)
    d_bcraw, dw_sbc, db_sbc = _ssd_conv_bwd_bc(proj, small["ssd_conv_w"], small["ssd_conv_b"], d_bc)
    g_in = _unpack_g_in(dict(
        z=_wgrad(h, d_z, "wgrad_in_z"), x=_wgrad(h, d_xraw, "wgrad_in_x"), bc=_wgrad(h, d_bcraw, "wgrad_in_bc"),
        conf=_wgrad_stacked(h, d_conf, "wgrad_in_conf", D), dt=_wgrad(h, d_dt, "wgrad_in_dt", bn=LANES)))
    g_in = g_in.reshape(4, 2, D // 2, W_IN_SHARD_PAD)
    grad_x, st_in = reduce_on("w_in", g_in, _inproj_bwd, d_z, d_xraw, d_bcraw, d_conf, d_dt, w_pack, x, mod,
                              small["norm1_w"], dx1)

    d_mod = jnp.concatenate([st_in[0:1], st_in[1:2], st_up[3:4], st_up[0:1], st_up[1:2], st_down[1:2]], axis=1)
    gsmall = dict(
        norm1_w=st_in[2:3], ssd_conv_w=jnp.concatenate([dw_sx, dw_sbc], axis=1),
        ssd_conv_b=jnp.concatenate([db_sx, db_sbc], axis=1), dt_bias=st_in[3:4, 0:HEADS], a_log=d_alog[:, 0:HEADS],
        d_skip=st_gn[1].reshape(HEADS, HEAD_P).sum(axis=1).reshape(1, HEADS), ssd_norm_w=st_gn[0:1],
        conf_conv_w=dw_conf, conf_conv_b=db_conf, conf_ln_w=st_ln[0:1], conf_ln_b=st_ln[1:2], norm2_w=st_up[2:3],
        ffn_conv_w=jnp.concatenate([dw_ffn[0], dw_ffn[1]], axis=1), ffn_conv_b=jnp.concatenate([db_ffn[0], db_ffn[1]], axis=1),
        final_norm_w=st_down[0:1], mod=d_mod)
    gbig = None if reducer is not None else dict(w_in=g_in, w_out=g_out, w_up=g_up, w_down=g_down)
    return st_down[2, 0], grad_x, gbig, gsmall


W_IN_COLS = 4624
W_IN_SHARD = W_IN_COLS // 4
W_IN_SHARD_PAD = 1280
_SEGMENTS = ((0, 1024, OFF_Z), (1024, 2560, OFF_XBC), (2560, 2576, OFF_DT), (2576, 3600, OFF_CA), (3600, 4624, OFF_CG))


def _in_pieces(bounds=()):
    out = []
    for k in range(4):
        s0, s1 = k * W_IN_SHARD, (k + 1) * W_IN_SHARD
        for lo, hi, off in _SEGMENTS:
            a, b = max(lo, s0), min(hi, s1)
            while a < b:
                p = off + a - lo
                e = min([b - a] + [c - p for c in bounds if c > p])
                out.append((k, a - s0, p, e))
                a += e
    return out


def _pack_w_in(shards):
    pieces = _in_pieces()

    def body(s_ref, o_ref):
        o_ref[:, OFF_DT:W_PACK] = jnp.zeros((TM, W_PACK - OFF_DT), MX)
        for k, c, p, n in pieces:
            o_ref[:, p:p + n] = s_ref[k, :, c:c + n]

    return pl.pallas_call(
        body, name="pack_w_in", grid=(D // TM,), out_shape=jax.ShapeDtypeStruct((D, W_PACK), MX),
        in_specs=[pl.BlockSpec((4, TM, W_IN_SHARD_PAD), lambda i: (0, i, 0))],
        out_specs=pl.BlockSpec((TM, W_PACK), lambda i: (i, 0)), compiler_params=_cp("arbitrary"),
    )(shards)


def _unpack_g_in(g):
    srcs = ((OFF_Z, D), (OFF_XBC, D), (OFF_XBC + D, 2 * CW), (OFF_CA, D), (OFF_CG, D), (OFF_DT, LANES))
    pieces = _in_pieces(tuple(o for o, _ in srcs) + tuple(o + n for o, n in srcs))

    def body(z_ref, x_ref, bc_ref, cf_ref, dt_ref, o_ref):
        read = (lambda lo, hi: z_ref[:, lo:hi], lambda lo, hi: x_ref[:, lo:hi], lambda lo, hi: bc_ref[:, lo:hi],
                lambda lo, hi: cf_ref[0, :, lo:hi], lambda lo, hi: cf_ref[1, :, lo:hi], lambda lo, hi: dt_ref[:, lo:hi])
        o_ref[:, :, W_IN_SHARD - 4:W_IN_SHARD_PAD] = jnp.zeros((4, TM, W_IN_SHARD_PAD - W_IN_SHARD + 4), MX)
        for k, c, p, n in pieces:
            i = [q for q, (o, w) in enumerate(srcs) if o <= p < o + w][0]
            o_ref[k, :, c:c + n] = read[i](p - srcs[i][0], p - srcs[i][0] + n)

    blk = lambda w: pl.BlockSpec((TM, w), lambda i: (i, 0))
    return pl.pallas_call(
        body, name="unpack_g_in", grid=(D // TM,), out_shape=jax.ShapeDtypeStruct((4, D, W_IN_SHARD_PAD), MX),
        in_specs=[blk(D), blk(D), blk(2 * CW), pl.BlockSpec((2, TM, D), lambda i: (0, i, 0)), blk(LANES)],
        out_specs=pl.BlockSpec((4, TM, W_IN_SHARD_PAD), lambda i: (0, i, 0)), compiler_params=_cp("arbitrary"),
    )(g["z"], g["x"], g["bc"], g["conf"], g["dt"])


def _scalar(v):
    return jnp.reshape(v, (1,)).astype(jnp.int32)


def _cast_into_slot(w, width, chip):
    r, c = w.shape
    h = r // 2
    tm = _row_tile(h)
    nj = h // tm

    def body(chip_ref, w_ref, o_ref):
        v = w_ref[...].astype(MX)
        o_ref[0, 0] = v if width == c else jnp.concatenate([v, jnp.zeros((tm, width - c), MX)], axis=1)

    return pl.pallas_call(
        body, name=f"cast_into_slot_{r}x{c}", out_shape=jax.ShapeDtypeStruct((4, 2, h, width), MX),
        grid_spec=pltpu.PrefetchScalarGridSpec(
            num_scalar_prefetch=1, grid=(2, nj),
            in_specs=[pl.BlockSpec((tm, c), lambda i, j, chip: (i * nj + j, 0))],
            out_specs=pl.BlockSpec((1, 1, tm, width), lambda i, j, chip: (chip[0], i, j, 0))),
        compiler_params=_cp("arbitrary", "arbitrary"),
    )(_scalar(chip), w)


ANY = pl.BlockSpec(memory_space=pl.ANY)


def _place():
    x, y, c = lax.axis_index("x"), lax.axis_index("y"), lax.axis_index("c")
    return x, y, c, [(1 - x, y), (x, 1 - y), (1 - x, 1 - y)]


def _gather_rows(block):
    m_per, n = block.shape

    def body(x_ref, out_ref, send_sems, recv_sems, local_sem):
        x, y, c, chips = _place()
        me, sibling = (x, y, c), (x, y, 1 - c)

        def rows(px, py, pc):
            return out_ref.at[pl.ds((4 * px + 2 * py + pc) * m_per, m_per), :]

        def copy(k, blk, to, src=None):
            return pltpu.make_async_remote_copy(
                src_ref=rows(*blk) if src is None else src, dst_ref=rows(*blk), send_sem=send_sems.at[k],
                recv_sem=recv_sems.at[k], device_id=to, device_id_type=MESH)

        mine = pltpu.make_async_copy(x_ref, rows(*me), local_sem)
        mine.start()
        first = [copy(0, me, sibling, src=x_ref)]
        first += [copy(1 + j, me, (*chip, c), src=x_ref) for j, chip in enumerate(chips)]
        for cp in first:
            cp.start()
        passed = [copy(4 + j, (*chip, c), sibling) for j, chip in enumerate(chips)]
        for j, chip in enumerate(chips):
            copy(1 + j, (*chip, c), me).wait_recv()
            passed[j].start()
        copy(0, sibling, me).wait_recv()
        for j, chip in enumerate(chips):
            copy(4 + j, (*chip, 1 - c), me).wait_recv()
        for cp in first + passed:
            cp.wait_send()
        mine.wait()

    return pl.pallas_call(
        body, name=f"gather_rows_{m_per}x{n}", out_shape=jax.ShapeDtypeStruct((8 * m_per, n), block.dtype),
        in_specs=[pl.BlockSpec(memory_space=pltpu.VMEM)], out_specs=pl.BlockSpec(memory_space=pltpu.VMEM),
        scratch_shapes=[pltpu.SemaphoreType.DMA((7,)), pltpu.SemaphoreType.DMA((7,)), pltpu.SemaphoreType.DMA],
        compiler_params=pltpu.CompilerParams(vmem_limit_bytes=VMEM_LIMIT),
    )(block)


class _GatherRider:
    def __init__(self, slots):
        n = len(slots)
        self.n = n
        self.inputs = list(slots)
        self.out_shape = [jax.ShapeDtypeStruct(s.shape, s.dtype) for s in slots]
        self.scratch = [pltpu.SemaphoreType.DMA((n, 6)), pltpu.SemaphoreType.DMA((n, 6))]
        self.aliases = {a: a for a in range(n)}

    @staticmethod
    def _copy(outs, sems, a, j, k, half, to):
        dst = outs[a].at[k, half]
        return pltpu.make_async_remote_copy(src_ref=dst, dst_ref=dst, send_sem=sems[0].at[a, j], recv_sem=sems[1].at[a, j],
                                            device_id=to, device_id_type=MESH)

    def _first(self, outs, sems):
        x, y, c, chips = _place()
        return [self._copy(outs, sems, a, j, 2 * x + y, c, (*chip, c)) for a in range(self.n) for j, chip in enumerate(chips)]

    def start(self, ins, outs, sems):
        for cp in self._first(outs, sems):
            cp.start()

    def finish(self, ins, outs, sems):
        x, y, c, chips = _place()
        passed = []
        for a in range(self.n):
            for j, (px, py) in enumerate(chips):
                self._copy(outs, sems, a, j, 2 * px + py, c, (x, y, c)).wait_recv()
                fwd = self._copy(outs, sems, a, 3 + j, 2 * px + py, c, (x, y, 1 - c))
                fwd.start()
                passed.append(fwd)
        for a in range(self.n):
            for j, (px, py) in enumerate(chips):
                self._copy(outs, sems, a, 3 + j, 2 * px + py, 1 - c, (x, y, c)).wait_recv()
        for cp in self._first(outs, sems) + passed:
            cp.wait_send()


class _ScatterRider:
    def __init__(self, parts):
        n = len(parts)
        self.n = n
        self.inputs = list(parts)
        self.out_shape = [jax.ShapeDtypeStruct((3,) + p.shape[1:], p.dtype) for p in parts]
        self.scratch = [pltpu.SemaphoreType.DMA((n, 3)), pltpu.SemaphoreType.DMA((n, 3))]
        self.aliases = {}

    def _copies(self, ins, outs, sems):
        x, y, c, chips = _place()
        return [pltpu.make_async_remote_copy(
            src_ref=ins[a].at[2 * px + py], dst_ref=outs[a].at[j], send_sem=sems[0].at[a, j], recv_sem=sems[1].at[a, j],
            device_id=(px, py, c), device_id_type=MESH) for a in range(self.n) for j, (px, py) in enumerate(chips)]

    def start(self, ins, outs, sems):
        for cp in self._copies(ins, outs, sems):
            cp.start()

    def finish(self, ins, outs, sems):
        for cp in self._copies(ins, outs, sems):
            cp.wait()


def _ride_alone(rider, name):
    n = len(rider.inputs)

    def body(*refs):
        ins, outs, sems = refs[:n], refs[n:n + len(rider.out_shape)], refs[n + len(rider.out_shape):]
        rider.start(ins, outs, sems)
        rider.finish(ins, outs, sems)

    return pl.pallas_call(
        body, name=name, out_shape=tuple(rider.out_shape), in_specs=[ANY] * n, out_specs=tuple([ANY] * len(rider.out_shape)),
        input_output_aliases=dict(rider.aliases), scratch_shapes=list(rider.scratch),
    )(*rider.inputs)


def _swap_halves(grads, name):
    n = len(grads)

    def body(*refs):
        ins, got = refs[:n], refs[n:2 * n]
        send_sems, recv_sems = refs[2 * n:]
        x, y, c, _ = _place()
        sent = [pltpu.make_async_remote_copy(
            src_ref=ins[a].at[k, 1 - c], dst_ref=got[a].at[k], send_sem=send_sems.at[a, k], recv_sem=recv_sems.at[a, k],
            device_id=(x, y, 1 - c), device_id_type=MESH) for a in range(n) for k in range(4)]
        for cp in sent:
            cp.start()
        for cp in sent:
            cp.wait()

    return pl.pallas_call(
        body, name=name, out_shape=tuple(jax.ShapeDtypeStruct((4,) + g.shape[2:], g.dtype) for g in grads),
        in_specs=[ANY] * n, out_specs=tuple([ANY] * n),
        scratch_shapes=[pltpu.SemaphoreType.DMA((n, 4)), pltpu.SemaphoreType.DMA((n, 4))],
    )(*grads)


class _Reducer:
    def __init__(self, chip, core):
        self.chip, self.core, self.parts, self.sums = chip, core, {}, {}

    def begin(self, name, grad):
        got, = _swap_halves([grad], "swap_halves_" + name)
        self.parts[name] = _add_pair(grad, got, self.core, name)
        return _ScatterRider([self.parts[name]])

    def end(self, name, others):
        self.sums[name] = _add_chips(self.parts[name], others[0], self.chip, name)


def _swap_sums(halves):
    n = len(halves)

    def body(*refs):
        ins, outs = refs[:n], refs[n:2 * n]
        send_sems, recv_sems = refs[2 * n:]
        x, y, c, _ = _place()
        sent = [pltpu.make_async_remote_copy(
            src_ref=ins[a], dst_ref=outs[a], send_sem=send_sems.at[a], recv_sem=recv_sems.at[a],
            device_id=(x, y, 1 - c), device_id_type=MESH) for a in range(n)]
        for cp in sent:
            cp.start()
        for cp in sent:
            cp.wait()

    return pl.pallas_call(
        body, name="swap_sums", out_shape=tuple(jax.ShapeDtypeStruct(s.shape, s.dtype) for s in halves),
        in_specs=[ANY] * n, out_specs=tuple([ANY] * n),
        scratch_shapes=[pltpu.SemaphoreType.DMA((n,)), pltpu.SemaphoreType.DMA((n,))],
    )(*halves)


def _row_tile(r):
    for tm in (TM, 176, 128, 64, 32, 16, 8):
        if r % tm == 0:
            return tm
    return r


def _add_pair(mine, got, core, name):
    k, _, h, c = mine.shape
    tm = _row_tile(h)

    def body(core_ref, a_ref, b_ref, o_ref):
        o_ref[0] = (a_ref[0, 0].astype(f32) + b_ref[0].astype(f32)).astype(MX)

    blk = pl.BlockSpec((1, tm, c), lambda i, j, core: (i, j, 0))
    return pl.pallas_call(
        body, name="add_pair_" + name, out_shape=jax.ShapeDtypeStruct((k, h, c), MX),
        grid_spec=pltpu.PrefetchScalarGridSpec(
            num_scalar_prefetch=1, grid=(k, h // tm),
            in_specs=[pl.BlockSpec((1, 1, tm, c), lambda i, j, core: (i, core[0], j, 0)), blk], out_specs=blk),
        compiler_params=_cp("arbitrary", "arbitrary"),
    )(_scalar(core), mine, got)


def _add_chips(parts, others, chip, name):
    _, h, c = parts.shape
    tm = _row_tile(h)

    def body(chip_ref, a_ref, b_ref, o_ref):
        s = a_ref[0].astype(f32) + b_ref[0].astype(f32)
        o_ref[...] = (s + b_ref[1].astype(f32)) + b_ref[2].astype(f32)

    return pl.pallas_call(
        body, name="add_chips_" + name, out_shape=jax.ShapeDtypeStruct((h, c), f32),
        grid_spec=pltpu.PrefetchScalarGridSpec(
            num_scalar_prefetch=1, grid=(h // tm,),
            in_specs=[pl.BlockSpec((1, tm, c), lambda i, chip: (chip[0], i, 0)),
                      pl.BlockSpec((3, tm, c), lambda i, chip: (0, i, 0))],
            out_specs=pl.BlockSpec((tm, c), lambda i, chip: (i, 0))),
        compiler_params=_cp("arbitrary"),
    )(_scalar(chip), parts, others)


def _adam_math(w, g, m, v):
    m = ADAM_B1 * m + (1.0 - ADAM_B1) * g
    v = ADAM_B2 * v + (1.0 - ADAM_B2) * (g * g)
    m_hat = m / (1.0 - ADAM_B1 ** ADAM_STEP)
    v_hat = v / (1.0 - ADAM_B2 ** ADAM_STEP)
    return -ADAM_LR * (m_hat / (jnp.sqrt(v_hat) + ADAM_EPS) + ADAM_WD * w), m, v


def _adamw(w, g, m, v, name):
    r, c = w.shape
    tm = _row_tile(r)

    def body(w_ref, g_ref, m_ref, v_ref, d_ref, nm_ref, nv_ref):
        d_ref[...], nm_ref[...], nv_ref[...] = _adam_math(w_ref[...], g_ref[...], m_ref[...], v_ref[...])

    blk = pl.BlockSpec((tm, c), lambda i: (i, 0))
    return pl.pallas_call(
        body, name=name, grid=(r // tm,), out_shape=tuple([jax.ShapeDtypeStruct((r, c), f32)] * 3),
        in_specs=[blk] * 4, out_specs=(blk,) * 3, compiler_params=_cp("arbitrary"),
    )(w, g, m, v)


def _adamw_halves(w, mine, other, m, v, core, name):
    r, c = w.shape
    h = r // 2
    tm = _row_tile(h)
    nj = h // tm
    cg = mine.shape[1]

    def body(core_ref, w_ref, a_ref, b_ref, m_ref, v_ref, g_ref, d_ref, nm_ref, nv_ref):
        g = jnp.where(pl.program_id(0) == core_ref[0], a_ref[:, 0:c], b_ref[:, 0:c])
        g_ref[...] = g
        d_ref[...], nm_ref[...], nv_ref[...] = _adam_math(w_ref[...], g, m_ref[...], v_ref[...])

    blk = pl.BlockSpec((tm, c), lambda i, j, core: (i * nj + j, 0))
    gblk = pl.BlockSpec((tm, cg), lambda i, j, core: (j, 0))
    return pl.pallas_call(
        body, name=name, out_shape=tuple([jax.ShapeDtypeStruct((r, c), f32)] * 4),
        grid_spec=pltpu.PrefetchScalarGridSpec(
            num_scalar_prefetch=1, grid=(2, nj), in_specs=[blk, gblk, gblk, blk, blk], out_specs=(blk,) * 4),
        compiler_params=_cp("arbitrary", "arbitrary"),
    )(_scalar(core), w, mine, other, m, v)


def _ada_forward(c_all, ada_w):
    def body(c_ref, w_ref, o_ref):
        o_ref[...] = jnp.dot(_silu(c_ref[...]).astype(MX), w_ref[...].astype(MX), preferred_element_type=f32)

    return pl.pallas_call(body, name="ada_forward", out_shape=jax.ShapeDtypeStruct((8, ada_w.shape[1]), f32),
                          compiler_params=pltpu.CompilerParams(vmem_limit_bytes=VMEM_LIMIT))(c_all, ada_w)


def _ada_adamw(c_all_t, d_mod, w, m, v):
    r, c = w.shape
    tm = TM

    def body(ct_ref, dm_ref, w_ref, m_ref, v_ref, g_ref, d_ref, nm_ref, nv_ref):
        ca = _silu(ct_ref[...])
        g = ca[:, 0:1] * dm_ref[0:1, :]
        for b in range(1, 8):
            g = g + ca[:, b:b + 1] * dm_ref[b:b + 1, :]
        g_ref[...] = g
        d_ref[...], nm_ref[...], nv_ref[...] = _adam_math(w_ref[...], g, m_ref[...], v_ref[...])

    blk = pl.BlockSpec((tm, c), lambda i: (i, 0))
    return pl.pallas_call(
        body, name="ada_adamw", grid=(r // tm,), out_shape=tuple([jax.ShapeDtypeStruct((r, c), f32)] * 4),
        in_specs=[pl.BlockSpec((tm, 8), lambda i: (i, 0)), pl.BlockSpec((8, c), lambda i: (0, 0)), blk, blk, blk],
        out_specs=(blk,) * 4, compiler_params=_cp("arbitrary"),
    )(c_all_t, d_mod, w, m, v)


def _sum_devices(rows):
    n = rows.shape[1]

    def body(r_ref, o_ref):
        s = r_ref[0:8, :]
        for d in range(1, 8):
            s = s + r_ref[8 * d:8 * d + 8, :]
        o_ref[...] = s

    return pl.pallas_call(body, name="sum_devices", out_shape=jax.ShapeDtypeStruct((8, n), f32),
                          compiler_params=pltpu.CompilerParams(vmem_limit_bytes=VMEM_LIMIT))(rows)


WEIGHTS = ("ada_w", "ada_b", "norm1_w", "w_in", "ssd_conv_w", "ssd_conv_b", "dt_bias", "a_log", "d_skip", "ssd_norm_w",
           "conf_conv_w", "conf_conv_b", "conf_ln_w", "conf_ln_b", "w_out", "norm2_w", "w_up", "ffn_conv_w", "ffn_conv_b",
           "w_down", "final_norm_w")
VECTORS = ("ada_b", "norm1_w", "ssd_conv_b", "dt_bias", "a_log", "d_skip", "ssd_norm_w", "conf_conv_b", "conf_ln_w",
           "conf_ln_b", "norm2_w", "ffn_conv_b", "final_norm_w")
CONVS = {"ssd_conv_w": (K_SSD, D_XBC), "conf_conv_w": (K_CONF, D), "ffn_conv_w": (K_FFN, 2 * D_FF)}


def _pack8(pieces):
    flat = jnp.concatenate([p.reshape(-1) for p in pieces])
    n = -(-flat.size // (8 * LANES)) * LANES
    return jnp.pad(flat, (0, 8 * n - flat.size)).reshape(8, n)


def _unpack(flat, sizes):
    out, o = [], 0
    for n in sizes:
        out.append(flat[o:o + n])
        o += n
    return out


def kernel(x, c, ada_w, ada_b, norm1_w, w_in, ssd_conv_w, ssd_conv_b, dt_bias, a_log, d_skip, ssd_norm_w, conf_conv_w, conf_conv_b, conf_ln_w, conf_ln_b, w_out, norm2_w, w_up, ffn_conv_w, ffn_conv_b, w_down, final_norm_w, loss_target, m_ada_w, m_ada_b, m_norm1_w, m_w_in, m_ssd_conv_w, m_ssd_conv_b, m_dt_bias, m_a_log, m_d_skip, m_ssd_norm_w, m_conf_conv_w, m_conf_conv_b, m_conf_ln_w, m_conf_ln_b, m_w_out, m_norm2_w, m_w_up, m_ffn_conv_w, m_ffn_conv_b, m_w_down, m_final_norm_w, v_ada_w, v_ada_b, v_norm1_w, v_w_in, v_ssd_conv_w, v_ssd_conv_b, v_dt_bias, v_a_log, v_d_skip, v_ssd_norm_w, v_conf_conv_w, v_conf_conv_b, v_conf_ln_w, v_conf_ln_b, v_w_out, v_norm2_w, v_w_up, v_ffn_conv_w, v_ffn_conv_b, v_w_down, v_final_norm_w):
    given = dict(locals())
    w = {n: given[n] for n in WEIGHTS}
    mom = {n: given["m_" + n] for n in WEIGHTS}
    var = {n: given["v_" + n] for n in WEIGHTS}
    chip = 2 * lax.axis_index("x") + lax.axis_index("y")
    me = 2 * chip + lax.axis_index("c")

    sent = _pack8([c] + [w[n] for n in CONVS])
    got = _gather_rows(sent).reshape(8, -1)
    c_all = got[:, 0:D]
    o = D
    conv_full = {}
    for n, (taps, cols) in CONVS.items():
        per = taps * cols // 4
        shards = got[0::2, o:o + per].reshape(4, taps, cols // 4)
        conv_full[n] = jnp.concatenate([shards[k] for k in range(4)], axis=1)
        o += per

    mod_cols = _gather_rows(_ada_forward(c_all, ada_w[0])).reshape(8, 8, -1)[0::2]
    mod = lax.dynamic_index_in_dim(mod_cols, me, axis=1, keepdims=False).reshape(1, 6 * D) + ada_b

    core = lax.axis_index("c")
    a_in, = _ride_alone(_GatherRider([_cast_into_slot(w_in[0], W_IN_SHARD_PAD, chip)]), "gather_w_in")
    w_pack = _pack_w_in(a_in.reshape(4, D, W_IN_SHARD_PAD))
    late = _GatherRider([_cast_into_slot(w_out[0], D, chip), _cast_into_slot(w_up[0], UP_SHARD, chip),
                         _cast_into_slot(w_down[0], D, chip)])

    small = {n: w[n].reshape(1, -1) for n in VECTORS if n != "ada_b"}
    small.update(conv_full)
    reducer = _Reducer(chip, core)
    loss_mine, grad_x, _, gsmall = _local_step(x[0], mod, loss_target[0], w_pack, late, small, reducer)
    loss = lax.psum(loss_mine, ("x", "y", "c"))
    big = ("w_in", "w_out", "w_up", "w_down")
    summed = [reducer.sums[n] for n in big]
    big_halves = dict(zip(big, zip(summed, _swap_sums(summed))))
    grads = {}

    sizes = [6 * D] + [w[n].size for n in VECTORS[1:]] + [taps * cols for taps, cols in CONVS.values()]
    sent = _pack8([gsmall["mod"]] + [gsmall[n] for n in VECTORS[1:]] + [gsmall[n] for n in CONVS])
    got = _gather_rows(sent)
    d_mod_all = got.reshape(8, -1)[:, 0:6 * D]
    pieces = _unpack(_sum_devices(got).reshape(-1), sizes)
    for n, g in zip(VECTORS, pieces[:len(VECTORS)]):
        grads[n] = g.reshape(w[n].shape)
    for (n, (taps, cols)), g in zip(CONVS.items(), pieces[len(VECTORS):]):
        grads[n] = lax.dynamic_slice_in_dim(g.reshape(taps, cols), chip * (cols // 4), cols // 4, axis=1).reshape(w[n].shape)

    delta, new_m, new_v = {}, {}, {}
    for n in ("w_in", "w_out", "w_up", "w_down"):
        res = _adamw_halves(w[n][0], *big_halves[n], mom[n][0], var[n][0], core, "adamw_" + n)
        grads[n], delta[n], new_m[n], new_v[n] = [r[None] for r in res]
    d_mod_mine = lax.dynamic_slice_in_dim(d_mod_all, chip * (6 * D // 4), 6 * D // 4, axis=1)
    res = _ada_adamw(c_all.T, d_mod_mine, ada_w[0], m_ada_w[0], v_ada_w[0])
    grads["ada_w"], delta["ada_w"], new_m["ada_w"], new_v["ada_w"] = [r[None] for r in res]
    names = VECTORS + tuple(CONVS)
    res = _adamw(*[_pack8([d[n] for n in names]) for d in (w, grads, mom, var)], "adamw_small")
    sizes = [w[n].size for n in names]
    for d, r in zip((delta, new_m, new_v), res):
        for n, piece in zip(names, _unpack(r.reshape(-1), sizes)):
            d[n] = piece.reshape(w[n].shape)

    return (loss, grad_x[None], *[grads[n] for n in WEIGHTS], *[delta[n] for n in WEIGHTS],
            *[new_m[n] for n in WEIGHTS], *[new_v[n] for n in WEIGHTS])
```

```python
import functools

import jax
import jax.numpy as jnp
from jax import lax
from jax.experimental import pallas as pl
from jax.experimental.pallas import tpu as pltpu

f32 = jnp.float32
MX = jnp.bfloat16

D = 1024
HEADS = 16
HEAD_P = 64
STATE_N = 128
D_XBC = 1536
D_FF = 2816
UP_SHARD = 2 * D_FF // 4
K_SSD, K_CONF, K_FFN = 4, 31, 3
CHUNK = 128
OFF_Z, OFF_XBC, OFF_CA, OFF_CG, OFF_DT = 0, 1024, 2560, 3584, 4608
W_PACK = 4736
TM = 256
CW = 256
RC = 64
LANES = 128
VMEM_LIMIT = 56 * 1024 * 1024

ADAM_LR, ADAM_B1, ADAM_B2, ADAM_EPS, ADAM_WD, ADAM_STEP = 0.001, 0.9, 0.999, 1e-08, 0.01, 10

MESH = pl.DeviceIdType.MESH


def _cp(*sem):
    return pltpu.CompilerParams(dimension_semantics=sem, vmem_limit_bytes=VMEM_LIMIT)


def _resident(shape):
    nd = len(shape)
    return pl.BlockSpec(shape, lambda *_: (0,) * nd, pipeline_mode=pl.Buffered(1))


def _row(width=D):
    return pl.BlockSpec((1, width), lambda *_: (0, 0))


def _call(body, *, name, grid, in_specs, out_specs, out_shape, args, sem, scratch_shapes=(), rider=None):
    if rider is None:
        outs = pl.pallas_call(body, name=name, grid=grid, in_specs=list(in_specs), out_specs=tuple(out_specs),
                              out_shape=tuple(out_shape), scratch_shapes=list(scratch_shapes), compiler_params=_cp(*sem))(*args)
        return tuple(outs), ()
    ni, no, ns = len(in_specs), len(out_specs), len(scratch_shapes)
    ri, ro = len(rider.inputs), len(rider.out_shape)

    def full(*refs):
        base_in, r_in = refs[:ni], refs[ni:ni + ri]
        base_out, r_out = refs[ni + ri:ni + ri + no], refs[ni + ri + no:ni + ri + no + ro]
        base_scr, r_scr = refs[ni + ri + no + ro:ni + ri + no + ro + ns], refs[ni + ri + no + ro + ns:]
        ids = [pl.program_id(a) for a in range(len(grid))]
        first = functools.reduce(jnp.logical_and, [i == 0 for i in ids])
        last = functools.reduce(jnp.logical_and, [i == g - 1 for i, g in zip(ids, grid)])

        @pl.when(first)
        def _():
            rider.start(r_in, r_out, r_scr)

        body(*base_in, *base_out, *base_scr)

        @pl.when(last)
        def _():
            rider.finish(r_in, r_out, r_scr)

    outs = pl.pallas_call(
        full, name=name, grid=grid, in_specs=list(in_specs) + [ANY] * ri, out_specs=tuple(out_specs) + (ANY,) * ro,
        out_shape=tuple(out_shape) + tuple(rider.out_shape), scratch_shapes=list(scratch_shapes) + list(rider.scratch),
        input_output_aliases={ni + i: no + j for i, j in rider.aliases.items()}, compiler_params=_cp(*sem),
    )(*args, *rider.inputs)
    return tuple(outs[:no]), tuple(outs[no:])


def _silu(v):
    return v * jax.nn.sigmoid(v)


def _dsilu(v):
    s = jax.nn.sigmoid(v)
    return s * (1.0 + v * (1.0 - s))


def _softplus(v):
    return jnp.maximum(v, 0.0) + jnp.log1p(jnp.exp(-jnp.abs(v)))


def _mm(a, b):
    return jnp.dot(a.astype(MX), b.astype(MX), preferred_element_type=f32)


def _mm_nt(a, b):
    return lax.dot_general(a.astype(MX), b.astype(MX), (((1,), (1,)), ((), ())), preferred_element_type=f32)


def _mm_tn(a, b):
    return lax.dot_general(a.astype(MX), b.astype(MX), (((0,), (0,)), ((), ())), preferred_element_type=f32)


def _ln_inproj(x, mod, norm1_w, w_pack):
    t = x.shape[0]

    def body(x_ref, mod_ref, nw_ref, w_ref, proj_ref, h_ref):
        xv = x_ref[...]
        rstd = lax.rsqrt(jnp.mean(xv * xv, axis=-1, keepdims=True) + 1e-6)
        h = (xv * rstd * nw_ref[...]) * (1.0 + mod_ref[:, D:2 * D]) + mod_ref[:, 0:D]
        hb = h.astype(MX)
        h_ref[...] = hb
        proj_ref[...] = jnp.dot(hb, w_ref[...], preferred_element_type=f32)

    return pl.pallas_call(
        body, name="ln_inproj", grid=(t // TM,),
        out_shape=(jax.ShapeDtypeStruct((t, W_PACK), f32), jax.ShapeDtypeStruct((t, D), MX)),
        in_specs=[pl.BlockSpec((TM, D), lambda i: (i, 0)), _row(6 * D), _row(), _resident((D, W_PACK))],
        out_specs=(pl.BlockSpec((TM, W_PACK), lambda i: (i, 0)), pl.BlockSpec((TM, D), lambda i: (i, 0))),
        compiler_params=_cp("arbitrary"),
    )(x, mod, norm1_w, w_pack)


def _ssd_gate_norm(y_scan, xbc_act, proj, d_skip_row, ssd_norm_w):
    t = y_scan.shape[0]

    def body(y_ref, xs_ref, z_ref, dsk_ref, nw_ref, o_ref):
        y = y_ref[...] + xs_ref[...] * dsk_ref[...]
        yz = y * _silu(z_ref[...])
        rstd = lax.rsqrt(jnp.mean(yz * yz, axis=-1, keepdims=True) + 1e-6)
        o_ref[...] = (yz * rstd * nw_ref[...]).astype(MX)

    blk = pl.BlockSpec((TM, D), lambda i: (i, 0))
    return pl.pallas_call(
        body, name="ssd_gate_norm", grid=(t // TM,), out_shape=jax.ShapeDtypeStruct((t, D), MX),
        in_specs=[blk, blk, blk, _row(), _row()], out_specs=blk, compiler_params=_cp("arbitrary"),
    )(y_scan, xbc_act, proj, d_skip_row, ssd_norm_w)


def _ln_silu(u_conv, ln_w, ln_b):
    t = u_conv.shape[0]

    def body(u_ref, w_ref, b_ref, o_ref):
        u = u_ref[...]
        mu = jnp.mean(u, axis=-1, keepdims=True)
        uc = u - mu
        rstd = lax.rsqrt(jnp.mean(uc * uc, axis=-1, keepdims=True) + 1e-5)
        o_ref[...] = _silu(uc * rstd * w_ref[...] + b_ref[...]).astype(MX)

    blk = pl.BlockSpec((TM, D), lambda i: (i, 0))
    return pl.pallas_call(
        body, name="ln_silu", grid=(t // TM,), out_shape=jax.ShapeDtypeStruct((t, D), MX),
        in_specs=[blk, _row(), _row()], out_specs=blk, compiler_params=_cp("arbitrary"),
    )(u_conv, ln_w, ln_b)


def _outproj_ln2_up(y_ssd, u, w_out, x, mod, norm2_w, w_up):
    t = x.shape[0]

    def body(y_ref, u_ref, wo_ref, x_ref, mod_ref, nw_ref, wu_ref, mix_ref, x1_ref, h2_ref, up_ref):
        mix = jnp.dot(y_ref[...], wo_ref[0:D, :], preferred_element_type=f32)
        mix = mix + jnp.dot(u_ref[...], wo_ref[D:2 * D, :], preferred_element_type=f32)
        mix_ref[...] = mix
        x1 = x_ref[...] + mod_ref[:, 2 * D:3 * D] * mix
        x1_ref[...] = x1
        rstd = lax.rsqrt(jnp.mean(x1 * x1, axis=-1, keepdims=True) + 1e-6)
        h2 = ((x1 * rstd * nw_ref[...]) * (1.0 + mod_ref[:, 4 * D:5 * D]) + mod_ref[:, 3 * D:4 * D]).astype(MX)
        h2_ref[...] = h2
        for k in range(4):
            up_ref[:, k * UP_SHARD:(k + 1) * UP_SHARD] = jnp.dot(h2, wu_ref[k], preferred_element_type=f32)

    blk = pl.BlockSpec((TM, D), lambda i: (i, 0))
    return pl.pallas_call(
        body, name="outproj_ln2_up", grid=(t // TM,),
        out_shape=(jax.ShapeDtypeStruct((t, D), f32), jax.ShapeDtypeStruct((t, D), f32),
                   jax.ShapeDtypeStruct((t, D), MX), jax.ShapeDtypeStruct((t, 2 * D_FF), f32)),
        in_specs=[blk, blk, _resident((2 * D, D)), blk, _row(6 * D), _row(), _resident((4, D, UP_SHARD))],
        out_specs=(blk, blk, blk, pl.BlockSpec((TM, 2 * D_FF), lambda i: (i, 0))),
        compiler_params=_cp("arbitrary"),
    )(y_ssd, u, w_out, x, mod, norm2_w, w_up)


def _down_loss(act, w_down, x1, mod, final_norm_w, target):
    t = x1.shape[0]

    def body(a_ref, wd_ref, x1_ref, mod_ref, wf_ref, tgt_ref, dx2_ref, dffn_ref, dact_ref, st_ref):
        @pl.when(pl.program_id(0) == 0)
        def _():
            st_ref[...] = jnp.zeros_like(st_ref)

        g2 = mod_ref[:, 5 * D:6 * D]
        ffn = jnp.dot(a_ref[...], wd_ref[...], preferred_element_type=f32)
        x2 = x1_ref[...] + g2 * ffn
        rstd = lax.rsqrt(jnp.mean(x2 * x2, axis=-1, keepdims=True) + 1e-6)
        xh = x2 * rstd
        wf = wf_ref[...]
        err = xh * wf - tgt_ref[...]
        dy = err * (1.0 / D)
        dxh = dy * wf
        dx2 = rstd * (dxh - xh * jnp.mean(dxh * xh, axis=-1, keepdims=True))
        dx2_ref[...] = dx2
        dffn = (g2 * dx2).astype(MX)
        dffn_ref[...] = dffn
        dact_ref[...] = lax.dot_general(dffn, wd_ref[...], (((1,), (1,)), ((), ())), preferred_element_type=f32)
        st_ref[0:1, :] += jnp.sum(dy * xh, axis=0, keepdims=True)
        st_ref[1:2, :] += jnp.sum(dx2 * ffn, axis=0, keepdims=True)
        st_ref[2:3, :] += jnp.sum(0.5 * jnp.mean(err * err, axis=-1, keepdims=True), axis=0, keepdims=True)

    blk = pl.BlockSpec((TM, D), lambda i: (i, 0))
    ablk = pl.BlockSpec((TM, D_FF), lambda i: (i, 0))
    return pl.pallas_call(
        body, name="down_loss", grid=(t // TM,),
        out_shape=(jax.ShapeDtypeStruct((t, D), f32), jax.ShapeDtypeStruct((t, D), MX),
                   jax.ShapeDtypeStruct((t, D_FF), f32), jax.ShapeDtypeStruct((8, D), f32)),
        in_specs=[ablk, _resident((D_FF, D)), blk, _row(6 * D), _row(), blk],
        out_specs=(blk, blk, ablk, pl.BlockSpec((8, D), lambda i: (0, 0))),
        compiler_params=_cp("arbitrary"),
    )(act, w_down, x1, mod, final_norm_w, target)


def _pad_of(k):
    return 8 * ((k - 1 + 7) // 8)


def _causal_win(ref, r, t, pad):
    base = pl.multiple_of(r * RC, RC)
    prev = ref[pl.ds(pl.multiple_of(jnp.maximum(base - pad, 0), 8), pad), :]
    prev = jnp.where(r > 0, prev, 0.0)
    return jnp.concatenate([prev, ref[pl.ds(base, RC), :]], axis=0)


def _anti_win(ref, r, t, pad):
    base = pl.multiple_of(r * RC, RC)
    nxt = ref[pl.ds(pl.multiple_of(jnp.minimum(base + RC, t - pad), 8), pad), :]
    nxt = jnp.where(r < t // RC - 1, nxt, 0.0)
    return jnp.concatenate([ref[pl.ds(base, RC), :], nxt], axis=0)


def _conv_taps(win, w_ref, k, pad):
    acc = None
    for j in range(k):
        o = pad - (k - 1) + j
        term = w_ref[j:j + 1, :] * win[o:o + RC, :]
        acc = term if acc is None else acc + term
    return acc


def _corr_taps(win, w_ref, k):
    acc = None
    for j in range(k):
        o = (k - 1) - j
        term = w_ref[j:j + 1, :] * win[o:o + RC, :]
        acc = term if acc is None else acc + term
    return acc


def _dw_accumulate(dw_scr, d, win, k, pad):
    for j in range(k):
        o = pad - (k - 1) + j
        prod = d * win[o:o + RC, :]
        dw_scr[8 * j:8 * j + 8, :] += prod.reshape(RC // 8, 8, prod.shape[-1]).sum(axis=0)


def _dw_finish(dw_scr, dw_ref, k):
    for j in range(k):
        dw_ref[j:j + 1, :] = jnp.sum(dw_scr[8 * j:8 * j + 8, :], axis=0, keepdims=True)


def _rows8(v):
    return v.reshape(RC // 8, 8, v.shape[-1]).sum(axis=0)


def _ssd_conv_fwd(proj, conv_w, conv_b):
    t = proj.shape[0]
    pad = _pad_of(K_SSD)
    c0 = OFF_XBC // CW

    def body(x_ref, w_ref, b_ref, o_ref):
        def step(r, carry):
            win = _causal_win(x_ref, r, t, pad)
            o_ref[pl.ds(pl.multiple_of(r * RC, RC), RC), :] = _silu(_conv_taps(win, w_ref, K_SSD, pad) + b_ref[...])
            return carry
        lax.fori_loop(0, t // RC, step, 0)

    return pl.pallas_call(
        body, name="ssd_conv_fwd", grid=(D_XBC // CW,), out_shape=jax.ShapeDtypeStruct((t, D_XBC), f32),
        in_specs=[pl.BlockSpec((t, CW), lambda j: (0, c0 + j)), pl.BlockSpec((K_SSD, CW), lambda j: (0, j)),
                  pl.BlockSpec((1, CW), lambda j: (0, j))],
        out_specs=pl.BlockSpec((t, CW), lambda j: (0, j)), compiler_params=_cp("arbitrary"),
    )(proj, conv_w, conv_b)


def _glu_conv_fwd(proj, conv_w, conv_b):
    t = proj.shape[0]
    pad = _pad_of(K_CONF)
    ca, cg = OFF_CA // CW, OFF_CG // CW

    def body(a_ref, g_ref, w_ref, b_ref, o_ref, v_scr):
        def glu(r, carry):
            rows = pl.ds(pl.multiple_of(r * RC, RC), RC)
            v_scr[rows, :] = a_ref[rows, :] * jax.nn.sigmoid(g_ref[rows, :])
            return carry
        lax.fori_loop(0, t // RC, glu, 0)

        def step(r, carry):
            win = _causal_win(v_scr, r, t, pad)
            o_ref[pl.ds(pl.multiple_of(r * RC, RC), RC), :] = _conv_taps(win, w_ref, K_CONF, pad) + b_ref[...]
            return carry
        lax.fori_loop(0, t // RC, step, 0)

    return pl.pallas_call(
        body, name="glu_conv_fwd", grid=(D // CW,), out_shape=jax.ShapeDtypeStruct((t, D), f32),
        in_specs=[pl.BlockSpec((t, CW), lambda j: (0, ca + j)), pl.BlockSpec((t, CW), lambda j: (0, cg + j)),
                  pl.BlockSpec((K_CONF, CW), lambda j: (0, j)), pl.BlockSpec((1, CW), lambda j: (0, j))],
        out_specs=pl.BlockSpec((t, CW), lambda j: (0, j)),
        scratch_shapes=[pltpu.VMEM((t, CW), f32)], compiler_params=_cp("arbitrary"),
    )(proj, proj, conv_w, conv_b)


def _ffn_conv_fwd(up, conv_w, conv_b):
    t = up.shape[0]
    pad = _pad_of(K_FFN)
    nb = D_FF // CW

    def body(g_ref, v_ref, wg_ref, wv_ref, bg_ref, bv_ref, o_ref):
        def step(r, carry):
            gc = _conv_taps(_causal_win(g_ref, r, t, pad), wg_ref, K_FFN, pad) + bg_ref[...]
            vc = _conv_taps(_causal_win(v_ref, r, t, pad), wv_ref, K_FFN, pad) + bv_ref[...]
            o_ref[pl.ds(pl.multiple_of(r * RC, RC), RC), :] = (_silu(gc) * vc).astype(MX)
            return carry
        lax.fori_loop(0, t // RC, step, 0)

    return pl.pallas_call(
        body, name="ffn_conv_fwd", grid=(nb,), out_shape=jax.ShapeDtypeStruct((t, D_FF), MX),
        in_specs=[pl.BlockSpec((t, CW), lambda j: (0, j)), pl.BlockSpec((t, CW), lambda j: (0, nb + j)),
                  pl.BlockSpec((K_FFN, CW), lambda j: (0, j)), pl.BlockSpec((K_FFN, CW), lambda j: (0, nb + j)),
                  pl.BlockSpec((1, CW), lambda j: (0, j)), pl.BlockSpec((1, CW), lambda j: (0, nb + j))],
        out_specs=pl.BlockSpec((t, CW), lambda j: (0, j)), compiler_params=_cp("arbitrary"),
    )(up, up, conv_w, conv_w, conv_b, conv_b)


def _ffn_conv_bwd(up, conv_w, conv_b, d_act, rider=None):
    t = up.shape[0]
    pad = _pad_of(K_FFN)
    nb = D_FF // CW

    def body(g_ref, v_ref, wg_ref, wv_ref, bg_ref, bv_ref, da_ref, dup_ref, dw_ref, db_ref,
             dg_scr, dv_scr, dwg_scr, dwv_scr, db_scr):
        dwg_scr[...] = jnp.zeros_like(dwg_scr)
        dwv_scr[...] = jnp.zeros_like(dwv_scr)
        db_scr[...] = jnp.zeros_like(db_scr)

        def first(r, carry):
            rows = pl.ds(pl.multiple_of(r * RC, RC), RC)
            gwin = _causal_win(g_ref, r, t, pad)
            vwin = _causal_win(v_ref, r, t, pad)
            gc = _conv_taps(gwin, wg_ref, K_FFN, pad) + bg_ref[...]
            vc = _conv_taps(vwin, wv_ref, K_FFN, pad) + bv_ref[...]
            da = da_ref[rows, :]
            dgc = da * vc * _dsilu(gc)
            dvc = da * _silu(gc)
            dg_scr[rows, :] = dgc
            dv_scr[rows, :] = dvc
            _dw_accumulate(dwg_scr, dgc, gwin, K_FFN, pad)
            _dw_accumulate(dwv_scr, dvc, vwin, K_FFN, pad)
            db_scr[0:8, :] += _rows8(dgc)
            db_scr[8:16, :] += _rows8(dvc)
            return carry
        lax.fori_loop(0, t // RC, first, 0)

        def second(r, carry):
            rows = pl.ds(pl.multiple_of(r * RC, RC), RC)
            dup_ref[0, rows, :] = _corr_taps(_anti_win(dg_scr, r, t, pad), wg_ref, K_FFN).astype(MX)
            dup_ref[1, rows, :] = _corr_taps(_anti_win(dv_scr, r, t, pad), wv_ref, K_FFN).astype(MX)
            return carry
        lax.fori_loop(0, t // RC, second, 0)

        for j in range(K_FFN):
            dw_ref[0, j:j + 1, :] = jnp.sum(dwg_scr[8 * j:8 * j + 8, :], axis=0, keepdims=True)
            dw_ref[1, j:j + 1, :] = jnp.sum(dwv_scr[8 * j:8 * j + 8, :], axis=0, keepdims=True)
        db_ref[0] = jnp.sum(db_scr[0:8, :], axis=0, keepdims=True)
        db_ref[1] = jnp.sum(db_scr[8:16, :], axis=0, keepdims=True)

    return _call(
        body, name="ffn_conv_bwd", grid=(nb,),
        out_shape=(jax.ShapeDtypeStruct((2, t, D_FF), MX), jax.ShapeDtypeStruct((2, K_FFN, D_FF), f32),
                   jax.ShapeDtypeStruct((2, 1, D_FF), f32)),
        in_specs=[pl.BlockSpec((t, CW), lambda j: (0, j)), pl.BlockSpec((t, CW), lambda j: (0, nb + j)),
                  pl.BlockSpec((K_FFN, CW), lambda j: (0, j)), pl.BlockSpec((K_FFN, CW), lambda j: (0, nb + j)),
                  pl.BlockSpec((1, CW), lambda j: (0, j)), pl.BlockSpec((1, CW), lambda j: (0, nb + j)),
                  pl.BlockSpec((t, CW), lambda j: (0, j))],
        out_specs=(pl.BlockSpec((2, t, CW), lambda j: (0, 0, j)), pl.BlockSpec((2, K_FFN, CW), lambda j: (0, 0, j)),
                   pl.BlockSpec((2, 1, CW), lambda j: (0, 0, j))),
        scratch_shapes=[pltpu.VMEM((t, CW), f32), pltpu.VMEM((t, CW), f32), pltpu.VMEM((8 * K_FFN, CW), f32),
                        pltpu.VMEM((8 * K_FFN, CW), f32), pltpu.VMEM((16, CW), f32)],
        sem=("arbitrary",), args=(up, up, conv_w, conv_w, conv_b, conv_b, d_act), rider=rider)


def _glu_conv_bwd(proj, conv_w, d_uconv, rider=None):
    t = proj.shape[0]
    pad = _pad_of(K_CONF)
    ca, cg = OFF_CA // CW, OFF_CG // CW

    def body(a_ref, g_ref, w_ref, du_ref, dc_ref, dw_ref, db_ref, v_scr, dw_scr, db_scr):
        dw_scr[...] = jnp.zeros_like(dw_scr)
        db_scr[...] = jnp.zeros_like(db_scr)

        def glu(r, carry):
            rows = pl.ds(pl.multiple_of(r * RC, RC), RC)
            v_scr[rows, :] = a_ref[rows, :] * jax.nn.sigmoid(g_ref[rows, :])
            return carry
        lax.fori_loop(0, t // RC, glu, 0)

        def step(r, carry):
            rows = pl.ds(pl.multiple_of(r * RC, RC), RC)
            du = du_ref[rows, :]
            _dw_accumulate(dw_scr, du, _causal_win(v_scr, r, t, pad), K_CONF, pad)
            db_scr[...] += _rows8(du)
            dv = _corr_taps(_anti_win(du_ref, r, t, pad), w_ref, K_CONF)
            a = a_ref[rows, :]
            s = jax.nn.sigmoid(g_ref[rows, :])
            dc_ref[0, rows, :] = (dv * s).astype(MX)
            dc_ref[1, rows, :] = (dv * a * s * (1.0 - s)).astype(MX)
            return carry
        lax.fori_loop(0, t // RC, step, 0)
        _dw_finish(dw_scr, dw_ref, K_CONF)
        db_ref[...] = jnp.sum(db_scr[...], axis=0, keepdims=True)

    return _call(
        body, name="glu_conv_bwd", grid=(D // CW,),
        out_shape=(jax.ShapeDtypeStruct((2, t, D), MX), jax.ShapeDtypeStruct((K_CONF, D), f32),
                   jax.ShapeDtypeStruct((1, D), f32)),
        in_specs=[pl.BlockSpec((t, CW), lambda j: (0, ca + j)), pl.BlockSpec((t, CW), lambda j: (0, cg + j)),
                  pl.BlockSpec((K_CONF, CW), lambda j: (0, j)), pl.BlockSpec((t, CW), lambda j: (0, j))],
        out_specs=(pl.BlockSpec((2, t, CW), lambda j: (0, 0, j)), pl.BlockSpec((K_CONF, CW), lambda j: (0, j)),
                   pl.BlockSpec((1, CW), lambda j: (0, j))),
        scratch_shapes=[pltpu.VMEM((t, CW), f32), pltpu.VMEM((8 * K_CONF, CW), f32), pltpu.VMEM((8, CW), f32)],
        sem=("arbitrary",), args=(proj, proj, conv_w, d_uconv), rider=rider)


def _ssd_conv_bwd_x(proj, conv_w, conv_b, d_xs, d_y, d_skip_row):
    t = proj.shape[0]
    pad = _pad_of(K_SSD)
    c0 = OFF_XBC // CW

    def body(x_ref, w_ref, b_ref, dxs_ref, dy_ref, dsk_ref, draw_ref, dw_ref, db_ref, dp_scr, dw_scr, db_scr):
        dw_scr[...] = jnp.zeros_like(dw_scr)
        db_scr[...] = jnp.zeros_like(db_scr)

        def first(r, carry):
            rows = pl.ds(pl.multiple_of(r * RC, RC), RC)
            win = _causal_win(x_ref, r, t, pad)
            pre = _conv_taps(win, w_ref, K_SSD, pad) + b_ref[...]
            dpre = (dxs_ref[rows, :] + dy_ref[rows, :] * dsk_ref[...]) * _dsilu(pre)
            dp_scr[rows, :] = dpre
            _dw_accumulate(dw_scr, dpre, win, K_SSD, pad)
            db_scr[...] += _rows8(dpre)
            return carry
        lax.fori_loop(0, t // RC, first, 0)

        def second(r, carry):
            rows = pl.ds(pl.multiple_of(r * RC, RC), RC)
            draw_ref[rows, :] = _corr_taps(_anti_win(dp_scr, r, t, pad), w_ref, K_SSD).astype(MX)
            return carry
        lax.fori_loop(0, t // RC, second, 0)
        _dw_finish(dw_scr, dw_ref, K_SSD)
        db_ref[...] = jnp.sum(db_scr[...], axis=0, keepdims=True)

    cb = pl.BlockSpec((t, CW), lambda j: (0, j))
    return pl.pallas_call(
        body, name="ssd_conv_bwd_x", grid=(D // CW,),
        out_shape=(jax.ShapeDtypeStruct((t, D), MX), jax.ShapeDtypeStruct((K_SSD, D), f32),
                   jax.ShapeDtypeStruct((1, D), f32)),
        in_specs=[pl.BlockSpec((t, CW), lambda j: (0, c0 + j)), pl.BlockSpec((K_SSD, CW), lambda j: (0, j)),
                  pl.BlockSpec((1, CW), lambda j: (0, j)), cb, cb, pl.BlockSpec((1, CW), lambda j: (0, j))],
        out_specs=(cb, pl.BlockSpec((K_SSD, CW), lambda j: (0, j)), pl.BlockSpec((1, CW), lambda j: (0, j))),
        scratch_shapes=[pltpu.VMEM((t, CW), f32), pltpu.VMEM((8 * K_SSD, CW), f32), pltpu.VMEM((8, CW), f32)],
        compiler_params=_cp("arbitrary"),
    )(proj, conv_w, conv_b, d_xs, d_y, d_skip_row)


def _ssd_conv_bwd_bc(proj, conv_w, conv_b, d_bc):
    t = proj.shape[0]
    pad = _pad_of(K_SSD)
    c0 = (OFF_XBC + D) // CW
    w0 = D // CW

    def body(x_ref, w_ref, b_ref, dbc_ref, draw_ref, dw_ref, db_ref, dp_scr, dw_scr, db_scr):
        dw_scr[...] = jnp.zeros_like(dw_scr)
        db_scr[...] = jnp.zeros_like(db_scr)

        def first(r, carry):
            rows = pl.ds(pl.multiple_of(r * RC, RC), RC)
            win = _causal_win(x_ref, r, t, pad)
            pre = _conv_taps(win, w_ref, K_SSD, pad) + b_ref[...]
            dpre = dbc_ref[0, rows, :] * _dsilu(pre)
            dp_scr[rows, :] = dpre
            _dw_accumulate(dw_scr, dpre, win, K_SSD, pad)
            db_scr[...] += _rows8(dpre)
            return carry
        lax.fori_loop(0, t // RC, first, 0)

        def second(r, carry):
            rows = pl.ds(pl.multiple_of(r * RC, RC), RC)
            draw_ref[rows, :] = _corr_taps(_anti_win(dp_scr, r, t, pad), w_ref, K_SSD).astype(MX)
            return carry
        lax.fori_loop(0, t // RC, second, 0)
        _dw_finish(dw_scr, dw_ref, K_SSD)
        db_ref[...] = jnp.sum(db_scr[...], axis=0, keepdims=True)

    return pl.pallas_call(
        body, name="ssd_conv_bwd_bc", grid=(2,),
        out_shape=(jax.ShapeDtypeStruct((t, 2 * CW), MX), jax.ShapeDtypeStruct((K_SSD, 2 * CW), f32),
                   jax.ShapeDtypeStruct((1, 2 * CW), f32)),
        in_specs=[pl.BlockSpec((t, CW), lambda j: (0, c0 + j)), pl.BlockSpec((K_SSD, CW), lambda j: (0, w0 + j)),
                  pl.BlockSpec((1, CW), lambda j: (0, w0 + j)), pl.BlockSpec((1, t, CW), lambda j: (j, 0, 0))],
        out_specs=(pl.BlockSpec((t, CW), lambda j: (0, j)), pl.BlockSpec((K_SSD, CW), lambda j: (0, j)),
                   pl.BlockSpec((1, CW), lambda j: (0, j))),
        scratch_shapes=[pltpu.VMEM((t, CW), f32), pltpu.VMEM((8 * K_SSD, CW), f32), pltpu.VMEM((8, CW), f32)],
        compiler_params=_cp("arbitrary"),
    )(proj, conv_w, conv_b, d_bc)


def _chunk_masks():
    ii = lax.broadcasted_iota(jnp.int32, (CHUNK, CHUNK), 0)
    jj = lax.broadcasted_iota(jnp.int32, (CHUNK, CHUNK), 1)
    return ii == jj, jj <= ii, jj >= ii


def _to_row(col, eye):
    return jnp.sum(jnp.where(eye, col, 0.0), axis=0, keepdims=True)


def _to_col(row, eye):
    return jnp.sum(jnp.where(eye, row, 0.0), axis=1, keepdims=True)


def _head_decay(dt_h, a_h, eye, tril):
    a_row = _to_row(dt_h * a_h, eye)
    cs = jnp.sum(jnp.where(tril, a_row, 0.0), axis=1, keepdims=True)
    cs_row = _to_row(cs, eye)
    decay = jnp.where(tril, jnp.exp(jnp.where(tril, cs - cs_row, 0.0)), 0.0)
    total = jnp.sum(a_row, axis=1, keepdims=True)
    return cs, decay, total


SCAN_UNROLL = 4


def _unrolled_loop(n, step, init):
    unroll = min(SCAN_UNROLL, n)
    assert n % unroll == 0

    def trip(i, carry):
        for u in range(unroll):
            carry = step(unroll * i + u, carry)
        return carry
    return lax.fori_loop(0, n // unroll, trip, init)


def _lane_pick(mat, lane, which):
    return jnp.sum(jnp.where(lane == which, mat, 0.0), axis=1, keepdims=True)


def _ssd_fwd(xbc_act, proj, dt_bias_row, a_log_row, rider=None):
    t = xbc_act.shape[0]
    nc = t // CHUNK
    cb, cc, cdt = D // LANES, (D + 2 * STATE_N) // LANES, OFF_DT // LANES

    def body(x_ref, b_ref, c_ref, dt_ref, dtb_ref, alog_ref, y_ref, st_ref):
        j = pl.program_id(0)
        eye, tril, _ = _chunk_masks()
        lane = lax.broadcasted_iota(jnp.int32, (1, LANES), 1)
        first = lane < HEAD_P
        a_row = -jnp.exp(alog_ref[...])
        a_heads = [jnp.sum(jnp.where(lane == 2 * j + h, a_row, 0.0), axis=1, keepdims=True) for h in range(2)]

        def chunk(c, hprev):
            rows = pl.ds(pl.multiple_of(c * CHUNK, CHUNK), CHUNK)
            xv, bm, cm = x_ref[rows, :], b_ref[rows, :], c_ref[rows, :]
            dt = _softplus(dt_ref[rows, :] + dtb_ref[...])
            st_ref[c] = hprev
            g = _mm_nt(cm, bm)
            ch = _mm(cm, hprev)
            dts = [_lane_pick(dt, lane, 2 * j + h) for h in range(2)]
            xdt = xv * jnp.where(first, dts[0], dts[1])
            ys, hs = [], []
            for h in range(2):
                cs, decay, total = _head_decay(dts[h], a_heads[h], eye, tril)
                y = _mm(g * decay, xdt) + jnp.exp(cs) * ch
                s = _mm_tn(bm * jnp.exp(total - cs), xdt)
                ys.append(y)
                hs.append(jnp.exp(total) * hprev + s)
            y_ref[rows, :] = jnp.where(first, ys[0], ys[1])
            return jnp.where(first, hs[0], hs[1])

        _unrolled_loop(nc, chunk, jnp.zeros((STATE_N, LANES), f32))

    blk = lambda f: pl.BlockSpec((t, LANES), f)
    return _call(
        body, name="ssd_fwd", grid=(D // LANES,),
        out_shape=(jax.ShapeDtypeStruct((t, D), f32), jax.ShapeDtypeStruct((nc, STATE_N, D), f32)),
        in_specs=[blk(lambda j: (0, j)), blk(lambda j: (0, cb + j // 4)), blk(lambda j: (0, cc + j // 4)),
                  blk(lambda j: (0, cdt)), _row(LANES), _row(LANES)],
        out_specs=(blk(lambda j: (0, j)), pl.BlockSpec((nc, STATE_N, LANES), lambda j: (0, 0, j))),
        sem=("arbitrary",), args=(xbc_act, xbc_act, xbc_act, proj, dt_bias_row, a_log_row), rider=rider)


def _ssd_bwd(xbc_act, proj, dt_bias_row, a_log_row, states, d_y, rider=None):
    t = xbc_act.shape[0]
    nc = t // CHUNK
    cb, cc, cdt = D // LANES, (D + 2 * STATE_N) // LANES, OFF_DT // LANES

    def body(x_ref, b_ref, c_ref, dt_ref, dtb_ref, alog_ref, st_ref, dy_ref, dx_ref, dbc_ref, ddt_ref, da_ref):
        grp, p = pl.program_id(0), pl.program_id(1)
        j = 4 * grp + p
        eye, tril, triu = _chunk_masks()
        lane = lax.broadcasted_iota(jnp.int32, (1, LANES), 1)
        first = lane < HEAD_P
        last_row = lax.broadcasted_iota(jnp.int32, (CHUNK, 1), 0) == CHUNK - 1
        a_row = -jnp.exp(alog_ref[...])
        a_heads = [jnp.sum(jnp.where(lane == 2 * j + h, a_row, 0.0), axis=1, keepdims=True) for h in range(2)]

        @pl.when(p == 0)
        def _():
            dbc_ref[...] = jnp.zeros_like(dbc_ref)

        @pl.when(j == 0)
        def _():
            ddt_ref[...] = jnp.zeros_like(ddt_ref)
            da_ref[...] = jnp.zeros_like(da_ref)

        def chunk(i, dh):
            c = nc - 1 - i
            rows = pl.ds(pl.multiple_of(c * CHUNK, CHUNK), CHUNK)
            xv, bm, cm = x_ref[rows, :], b_ref[rows, :], c_ref[rows, :]
            dtr = dt_ref[rows, :] + dtb_ref[...]
            dt = _softplus(dtr)
            hprev = st_ref[c]
            dy = dy_ref[rows, :]
            g = _mm_nt(cm, bm)
            dts = [_lane_pick(dt, lane, 2 * j + h) for h in range(2)]
            xdt = xv * jnp.where(first, dts[0], dts[1])
            dxs, dhs = [], []
            db_sum, dc_sum = None, None
            ddt_mat = jnp.zeros((CHUNK, LANES), f32)
            da_acc = jnp.zeros((1, LANES), f32)
            for h in range(2):
                mine = first if h == 0 else jnp.logical_not(first)
                cs, decay, total = _head_decay(dts[h], a_heads[h], eye, tril)
                e_cs, e_tot = jnp.exp(cs), jnp.exp(total)
                dec_s = jnp.exp(total - cs)
                dyh = jnp.where(mine, dy, 0.0)
                xdth = jnp.where(mine, xdt, 0.0)
                dhh = jnp.where(mine, dh, 0.0)
                hph = jnp.where(mine, hprev, 0.0)
                m = g * decay
                dm = _mm_nt(dyh, xdth)
                dg = dm * decay
                w = dm * m
                bdec = bm * dec_s
                dxdt = _mm_tn(m, dyh) + _mm(bdec, dhh)
                dc_off = _mm_nt(dyh, hph) * e_cs
                db_s = _mm_nt(xdth, dhh) * dec_s
                dc_h = _mm(dg, bm) + dc_off
                db_h = _mm_tn(dg, cm) + db_s
                r_s = jnp.sum(db_s * bm, axis=1, keepdims=True)
                dtotal = jnp.sum(r_s, axis=0, keepdims=True) + e_tot * jnp.sum(
                    jnp.sum(dhh * hph, axis=1, keepdims=True), axis=0, keepdims=True)
                dcs = (jnp.sum(w, axis=1, keepdims=True) - _to_col(jnp.sum(w, axis=0, keepdims=True), eye)
                       + jnp.sum(dc_off * cm, axis=1, keepdims=True) - r_s + jnp.where(last_row, dtotal, 0.0))
                da_col = jnp.sum(jnp.where(triu, _to_row(dcs, eye), 0.0), axis=1, keepdims=True)
                ddt = da_col * a_heads[h] + jnp.sum(jnp.where(mine, dxdt * xv, 0.0), axis=1, keepdims=True)
                ddt_mat = ddt_mat + jnp.where(lane == 2 * j + h, ddt, 0.0)
                da_acc = da_acc + jnp.where(lane == 2 * j + h, jnp.sum(da_col * dts[h], axis=0, keepdims=True), 0.0)
                dxs.append(dxdt * dts[h])
                dhs.append(e_tot * dhh + _mm_tn(cm * e_cs, dyh))
                db_sum = db_h if db_sum is None else db_sum + db_h
                dc_sum = dc_h if dc_sum is None else dc_sum + dc_h
            dx_ref[rows, :] = jnp.where(first, dxs[0], dxs[1])
            dbc_ref[0, rows, :] += db_sum
            dbc_ref[1, rows, :] += dc_sum
            ddt_ref[rows, :] += ddt_mat * jax.nn.sigmoid(dtr)
            da_ref[...] += da_acc * a_row
            return jnp.where(first, dhs[0], dhs[1])

        _unrolled_loop(nc, chunk, jnp.zeros((STATE_N, LANES), f32))

    blk = lambda f: pl.BlockSpec((t, LANES), f)
    return _call(
        body, name="ssd_bwd", grid=(2, 4),
        out_shape=(jax.ShapeDtypeStruct((t, D), f32), jax.ShapeDtypeStruct((2, t, 2 * STATE_N), f32),
                   jax.ShapeDtypeStruct((t, LANES), f32), jax.ShapeDtypeStruct((1, LANES), f32)),
        in_specs=[blk(lambda g, p: (0, 4 * g + p)), blk(lambda g, p: (0, cb + g)), blk(lambda g, p: (0, cc + g)),
                  blk(lambda g, p: (0, cdt)), _row(LANES), _row(LANES),
                  pl.BlockSpec((nc, STATE_N, LANES), lambda g, p: (0, 0, 4 * g + p)), blk(lambda g, p: (0, 4 * g + p))],
        out_specs=(blk(lambda g, p: (0, 4 * g + p)), pl.BlockSpec((2, t, LANES), lambda g, p: (0, 0, g)),
                   blk(lambda g, p: (0, 0)), _row(LANES)),
        sem=("arbitrary", "arbitrary"), args=(xbc_act, xbc_act, xbc_act, proj, dt_bias_row, a_log_row, states, d_y),
        rider=rider)


def _up_bwd(d_up, w_up, x1, mod, norm2_w, dx2, mix, w_out):
    t = x1.shape[0]

    def body(dup_ref, wu_ref, x1_ref, mod_ref, nw_ref, dx2_ref, mix_ref, wo_ref,
             dx1_ref, dmix_ref, dys_ref, du_ref, st_ref):
        @pl.when(pl.program_id(0) == 0)
        def _():
            st_ref[...] = jnp.zeros_like(st_ref)

        nt = (((1,), (1,)), ((), ()))
        dh = None
        for k in range(4):
            lo = (k % 2) * UP_SHARD
            part = lax.dot_general(dup_ref[k // 2, :, lo:lo + UP_SHARD], wu_ref[k], nt, preferred_element_type=f32)
            dh = part if dh is None else dh + part
        x1 = x1_ref[...]
        rstd = lax.rsqrt(jnp.mean(x1 * x1, axis=-1, keepdims=True) + 1e-6)
        xh = x1 * rstd
        nw = nw_ref[...]
        sc = 1.0 + mod_ref[:, 4 * D:5 * D]
        st_ref[0:1, :] += jnp.sum(dh, axis=0, keepdims=True)
        st_ref[1:2, :] += jnp.sum(dh * xh * nw, axis=0, keepdims=True)
        st_ref[2:3, :] += jnp.sum(dh * sc * xh, axis=0, keepdims=True)
        dxh = dh * sc * nw
        dx1 = dx2_ref[...] + rstd * (dxh - xh * jnp.mean(dxh * xh, axis=-1, keepdims=True))
        dx1_ref[...] = dx1
        st_ref[3:4, :] += jnp.sum(dx1 * mix_ref[...], axis=0, keepdims=True)
        dmix = (mod_ref[:, 2 * D:3 * D] * dx1).astype(MX)
        dmix_ref[...] = dmix
        dys_ref[...] = lax.dot_general(dmix, wo_ref[0:D, :], nt, preferred_element_type=f32)
        du_ref[...] = lax.dot_general(dmix, wo_ref[D:2 * D, :], nt, preferred_element_type=f32)

    blk = pl.BlockSpec((TM, D), lambda i: (i, 0))
    return pl.pallas_call(
        body, name="up_bwd", grid=(t // TM,),
        out_shape=(jax.ShapeDtypeStruct((t, D), f32), jax.ShapeDtypeStruct((t, D), MX),
                   jax.ShapeDtypeStruct((t, D), f32), jax.ShapeDtypeStruct((t, D), f32),
                   jax.ShapeDtypeStruct((8, D), f32)),
        in_specs=[pl.BlockSpec((2, TM, D_FF), lambda i: (0, i, 0)), _resident((4, D, UP_SHARD)), blk, _row(6 * D), _row(),
                  blk, blk, _resident((2 * D, D))],
        out_specs=(blk, blk, blk, blk, pl.BlockSpec((8, D), lambda i: (0, 0))),
        compiler_params=_cp("arbitrary"),
    )(d_up, w_up, x1, mod, norm2_w, dx2, mix, w_out)


def _ln_silu_bwd(d_u, u_conv, ln_w, ln_b):
    t = d_u.shape[0]

    def body(du_ref, u_ref, w_ref, b_ref, o_ref, st_ref):
        @pl.when(pl.program_id(0) == 0)
        def _():
            st_ref[...] = jnp.zeros_like(st_ref)

        u = u_ref[...]
        mu = jnp.mean(u, axis=-1, keepdims=True)
        uc = u - mu
        rstd = lax.rsqrt(jnp.mean(uc * uc, axis=-1, keepdims=True) + 1e-5)
        n = uc * rstd
        w = w_ref[...]
        dl = du_ref[...] * _dsilu(n * w + b_ref[...])
        st_ref[0:1, :] += jnp.sum(dl * n, axis=0, keepdims=True)
        st_ref[1:2, :] += jnp.sum(dl, axis=0, keepdims=True)
        dn = dl * w
        o_ref[...] = rstd * (dn - jnp.mean(dn, axis=-1, keepdims=True) - n * jnp.mean(dn * n, axis=-1, keepdims=True))

    blk = pl.BlockSpec((TM, D), lambda i: (i, 0))
    return pl.pallas_call(
        body, name="ln_silu_bwd", grid=(t // TM,),
        out_shape=(jax.ShapeDtypeStruct((t, D), f32), jax.ShapeDtypeStruct((8, D), f32)),
        in_specs=[blk, blk, _row(), _row()], out_specs=(blk, pl.BlockSpec((8, D), lambda i: (0, 0))),
        compiler_params=_cp("arbitrary"),
    )(d_u, u_conv, ln_w, ln_b)


def _ssd_gate_norm_bwd(d_out, y_scan, xbc_act, proj, d_skip_row, ssd_norm_w):
    t = d_out.shape[0]

    def body(do_ref, y_ref, xs_ref, z_ref, dsk_ref, nw_ref, dy_ref, dz_ref, st_ref):
        @pl.when(pl.program_id(0) == 0)
        def _():
            st_ref[...] = jnp.zeros_like(st_ref)

        xs = xs_ref[...]
        y = y_ref[...] + xs * dsk_ref[...]
        z = z_ref[...]
        s = _silu(z)
        yz = y * s
        rstd = lax.rsqrt(jnp.mean(yz * yz, axis=-1, keepdims=True) + 1e-6)
        n = yz * rstd
        do = do_ref[...]
        st_ref[0:1, :] += jnp.sum(do * n, axis=0, keepdims=True)
        dn = do * nw_ref[...]
        dyz = rstd * (dn - n * jnp.mean(dn * n, axis=-1, keepdims=True))
        dy = dyz * s
        dy_ref[...] = dy
        dz_ref[...] = (dyz * y * _dsilu(z)).astype(MX)
        st_ref[1:2, :] += jnp.sum(dy * xs, axis=0, keepdims=True)

    blk = pl.BlockSpec((TM, D), lambda i: (i, 0))
    return pl.pallas_call(
        body, name="ssd_gate_norm_bwd", grid=(t // TM,),
        out_shape=(jax.ShapeDtypeStruct((t, D), f32), jax.ShapeDtypeStruct((t, D), MX), jax.ShapeDtypeStruct((8, D), f32)),
        in_specs=[blk, blk, blk, blk, _row(), _row()], out_specs=(blk, blk, pl.BlockSpec((8, D), lambda i: (0, 0))),
        compiler_params=_cp("arbitrary"),
    )(d_out, y_scan, xbc_act, proj, d_skip_row, ssd_norm_w)


def _inproj_bwd(d_z, d_xraw, d_bcraw, d_conf, d_dt, w_pack, x, mod, norm1_w, dx1, rider=None):
    t = x.shape[0]

    def body(dz_ref, dx_ref, dbc_ref, dcf_ref, ddt_ref, w_ref, x_ref, mod_ref, nw_ref, dx1_ref, gx_ref, st_ref):
        @pl.when(pl.program_id(0) == 0)
        def _():
            st_ref[...] = jnp.zeros_like(st_ref)

        nt = (((1,), (1,)), ((), ()))
        dot = lambda a, lo, hi: lax.dot_general(a, w_ref[:, lo:hi], nt, preferred_element_type=f32)
        dh = dot(dz_ref[...], OFF_Z, OFF_Z + D)
        dh = dh + dot(dx_ref[...], OFF_XBC, OFF_XBC + D)
        dh = dh + dot(dbc_ref[...], OFF_XBC + D, OFF_XBC + D_XBC)
        dh = dh + dot(dcf_ref[0], OFF_CA, OFF_CA + D)
        dh = dh + dot(dcf_ref[1], OFF_CG, OFF_CG + D)
        dh = dh + dot(ddt_ref[...].astype(MX), OFF_DT, OFF_DT + LANES)
        st_ref[3:4, 0:LANES] += jnp.sum(ddt_ref[...], axis=0, keepdims=True)
        xv = x_ref[...]
        rstd = lax.rsqrt(jnp.mean(xv * xv, axis=-1, keepdims=True) + 1e-6)
        xh = xv * rstd
        nw = nw_ref[...]
        sc = 1.0 + mod_ref[:, D:2 * D]
        st_ref[0:1, :] += jnp.sum(dh, axis=0, keepdims=True)
        st_ref[1:2, :] += jnp.sum(dh * xh * nw, axis=0, keepdims=True)
        st_ref[2:3, :] += jnp.sum(dh * sc * xh, axis=0, keepdims=True)
        dxh = dh * sc * nw
        gx_ref[...] = dx1_ref[...] + rstd * (dxh - xh * jnp.mean(dxh * xh, axis=-1, keepdims=True))

    blk = pl.BlockSpec((TM, D), lambda i: (i, 0))
    return _call(
        body, name="inproj_bwd", grid=(t // TM,),
        out_shape=(jax.ShapeDtypeStruct((t, D), f32), jax.ShapeDtypeStruct((8, D), f32)),
        in_specs=[blk, blk, pl.BlockSpec((TM, 2 * CW), lambda i: (i, 0)), pl.BlockSpec((2, TM, D), lambda i: (0, i, 0)),
                  pl.BlockSpec((TM, LANES), lambda i: (i, 0)), _resident((D, W_PACK)), blk, _row(6 * D), _row(), blk],
        out_specs=(blk, pl.BlockSpec((8, D), lambda i: (0, 0))),
        sem=("arbitrary",), args=(d_z, d_xraw, d_bcraw, d_conf, d_dt, w_pack, x, mod, norm1_w, dx1), rider=rider)


def _wgrad(a, d, name, bn=256):
    t, k = a.shape
    n = d.shape[1]
    out_dtype = MX

    def body(a_ref, d_ref, o_ref):
        o_ref[...] = lax.dot_general(a_ref[...], d_ref[...].astype(MX), (((0,), (0,)), ((), ())),
                                     preferred_element_type=f32).astype(out_dtype)

    return pl.pallas_call(
        body, name=name, grid=(n // bn,), out_shape=jax.ShapeDtypeStruct((k, n), out_dtype),
        in_specs=[_resident((t, k)), pl.BlockSpec((t, bn), lambda j: (0, j))],
        out_specs=pl.BlockSpec((k, bn), lambda j: (0, j)), compiler_params=_cp("arbitrary"),
    )(a, d)


def _wgrad_stacked(a, d, name, bn):
    out_dtype = MX
    t, k = a.shape
    s, _, n = d.shape
    nb = n // bn

    def body(a_ref, d_ref, o_ref):
        o_ref[0] = lax.dot_general(a_ref[...], d_ref[0], (((0,), (0,)), ((), ())),
                                   preferred_element_type=f32).astype(out_dtype)

    return pl.pallas_call(
        body, name=name, grid=(s, nb), out_shape=jax.ShapeDtypeStruct((s * nb, k, bn), out_dtype),
        in_specs=[_resident((t, k)), pl.BlockSpec((1, t, bn), lambda i, j: (i, 0, j))],
        out_specs=pl.BlockSpec((1, k, bn), lambda i, j: (i * nb + j, 0, 0)), compiler_params=_cp("arbitrary", "arbitrary"),
    )(a, d)


def _pad_row(v, width=LANES):
    return jnp.pad(v.reshape(1, -1), ((0, 0), (0, width - v.size)))


def _quarters(a):
    return a.reshape(4, 2, a.shape[0] // 8, a.shape[1])


def _local_step(x, mod, target, w_pack, late, small, reducer=None):
    dtb_row, alog_row = _pad_row(small["dt_bias"]), _pad_row(small["a_log"])
    dskip_row = jnp.repeat(small["d_skip"].reshape(-1), HEAD_P).reshape(1, D)

    def reduce_on(name, grad, host, *args):
        if reducer is None:
            return host(*args)[0]
        outs, others = host(*args, rider=reducer.begin(name, grad))
        reducer.end(name, others)
        return outs

    proj, h = _ln_inproj(x, mod, small["norm1_w"], w_pack)
    xbc_act = _ssd_conv_fwd(proj, small["ssd_conv_w"], small["ssd_conv_b"])
    if isinstance(late, _GatherRider):
        (y_scan, states), (a_out, a_up, a_down) = _ssd_fwd(xbc_act, proj, dtb_row, alog_row, rider=late)
        w_out, w_up, w_down = a_out.reshape(2 * D, D), a_up.reshape(4, D, UP_SHARD), a_down.reshape(D_FF, D)
    else:
        (y_scan, states), _ = _ssd_fwd(xbc_act, proj, dtb_row, alog_row)
        w_out, w_up, w_down = late
    y_ssd = _ssd_gate_norm(y_scan, xbc_act, proj, dskip_row, small["ssd_norm_w"])
    u_conv = _glu_conv_fwd(proj, small["conf_conv_w"], small["conf_conv_b"])
    u = _ln_silu(u_conv, small["conf_ln_w"], small["conf_ln_b"])
    mix, x1, h2, up = _outproj_ln2_up(y_ssd, u, w_out, x, mod, small["norm2_w"], w_up)
    act = _ffn_conv_fwd(up, small["ffn_conv_w"], small["ffn_conv_b"])
    dx2, d_ffn, d_act, st_down = _down_loss(act, w_down, x1, mod, small["final_norm_w"], target)

    g_down = _quarters(_wgrad(act, d_ffn, "wgrad_down"))
    d_up, dw_ffn, db_ffn = reduce_on("w_down", g_down, _ffn_conv_bwd, up, small["ffn_conv_w"], small["ffn_conv_b"], d_act)
    g_up = _wgrad_stacked(h2, d_up, "wgrad_up", D_FF // 2).reshape(4, 2, D // 2, UP_SHARD)
    dx1, d_mix, d_yssd, d_u, st_up = _up_bwd(d_up, w_up, x1, mod, small["norm2_w"], dx2, mix, w_out)
    g_out = _quarters(jnp.concatenate([_wgrad(y_ssd, d_mix, "wgrad_out_y"), _wgrad(u, d_mix, "wgrad_out_u")], axis=0))
    d_uconv, st_ln = _ln_silu_bwd(d_u, u_conv, small["conf_ln_w"], small["conf_ln_b"])
    d_conf, dw_conf, db_conf = reduce_on("w_up", g_up, _glu_conv_bwd, proj, small["conf_conv_w"], d_uconv)
    d_y, d_z, st_gn = _ssd_gate_norm_bwd(d_yssd, y_scan, xbc_act, proj, dskip_row, small["ssd_norm_w"])
    d_xs, d_bc, d_dt, d_alog = reduce_on("w_out", g_out, _ssd_bwd, xbc_act, proj, dtb_row, alog_row, states, d_y)
    d_xraw, dw_sx, db_sx = _ssd_conv_bwd_x(proj, small["ssd_conv_w"], small["ssd_conv_b"], d_xs, d_y, dskip_row)
    d_bcraw, dw_sbc, db_sbc = _ssd_conv_bwd_bc(proj, small["ssd_conv_w"], small["ssd_conv_b"], d_bc)
    g_in = _unpack_g_in(dict(
        z=_wgrad(h, d_z, "wgrad_in_z"), x=_wgrad(h, d_xraw, "wgrad_in_x"), bc=_wgrad(h, d_bcraw, "wgrad_in_bc"),
        conf=_wgrad_stacked(h, d_conf, "wgrad_in_conf", D), dt=_wgrad(h, d_dt, "wgrad_in_dt", bn=LANES)))
    g_in = g_in.reshape(4, 2, D // 2, W_IN_SHARD_PAD)
    grad_x, st_in = reduce_on("w_in", g_in, _inproj_bwd, d_z, d_xraw, d_bcraw, d_conf, d_dt, w_pack, x, mod,
                              small["norm1_w"], dx1)

    gsmall = _pack_small_grads(st_in, st_up, st_down, st_ln, st_gn, d_alog, dw_sx, dw_sbc, db_sx, db_sbc, dw_conf, db_conf,
                               dw_ffn, db_ffn)
    gbig = None if reducer is not None else dict(w_in=g_in, w_out=g_out, w_up=g_up, w_down=g_down)
    return st_down[2, 0], grad_x, gbig, gsmall


VECTORS = ("ada_b", "norm1_w", "ssd_conv_b", "dt_bias", "a_log", "d_skip", "ssd_norm_w", "conf_conv_b", "conf_ln_w",
           "conf_ln_b", "norm2_w", "ffn_conv_b", "final_norm_w")
VECTOR_SIZES = (6 * D, D, D_XBC, HEADS, HEADS, HEADS, D, D, D, D, D, 2 * D_FF, D)
CONVS = {"ssd_conv_w": (K_SSD, D_XBC), "conf_conv_w": (K_CONF, D), "ffn_conv_w": (K_FFN, 2 * D_FF)}


def _pack_rows(items):
    n = -(-sum(w for _, w in items) // (8 * LANES)) * LANES
    while True:
        fill, place = [0] * 8, {}
        for key, w in sorted(items, key=lambda kv: -kv[1]):
            rows = [r for r in range(8) if fill[r] + w <= n]
            if not rows:
                break
            place[key] = (rows[0], fill[rows[0]])
            fill[rows[0]] += w
        if len(place) == len(items):
            return n, place
        n += LANES


FRONT_N, FRONT = _pack_rows([("c", D)] + [((nm, j), cols // 4) for nm, (taps, cols) in CONVS.items() for j in range(taps)])
BACK_N, BACK = _pack_rows([(nm, -(-sz // LANES) * LANES) for nm, sz in zip(VECTORS, VECTOR_SIZES)]
                          + [((nm, j), cols) for nm, (taps, cols) in CONVS.items() for j in range(taps)])
_VM = pltpu.CompilerParams(vmem_limit_bytes=VMEM_LIMIT)


def _pack_front(c, shards):
    def body(c_ref, *refs):
        o_ref = refs[-1]
        o_ref[...] = jnp.zeros_like(o_ref)
        r, o = FRONT["c"]
        o_ref[r:r + 1, o:o + D] = c_ref[...]
        for ref, (nm, (taps, cols)) in zip(refs, CONVS.items()):
            for j in range(taps):
                r, o = FRONT[(nm, j)]
                o_ref[r:r + 1, o:o + cols // 4] = ref[0, j:j + 1, :]

    return pl.pallas_call(body, name="pack_front", out_shape=jax.ShapeDtypeStruct((8, FRONT_N), f32),
                          compiler_params=_VM)(c, *shards)


def _unpack_front(got):
    def body(g_ref, c_ref, *outs):
        r, o = FRONT["c"]
        for d in range(8):
            c_ref[d:d + 1, :] = g_ref[8 * d + r:8 * d + r + 1, o:o + D]
        for ref, (nm, (taps, cols)) in zip(outs, CONVS.items()):
            cw = cols // 4
            for j in range(taps):
                r, o = FRONT[(nm, j)]
                for k in range(4):
                    ref[j:j + 1, k * cw:(k + 1) * cw] = g_ref[16 * k + r:16 * k + r + 1, o:o + cw]

    return pl.pallas_call(
        body, name="unpack_front", compiler_params=_VM,
        out_shape=(jax.ShapeDtypeStruct((8, D), f32),) + tuple(jax.ShapeDtypeStruct(tc, f32) for tc in CONVS.values()),
    )(got)


def _pack_small_grads(st_in, st_up, st_down, st_ln, st_gn, d_alog, dw_sx, dw_sbc, db_sx, db_sbc, dw_conf, db_conf, dw_ffn,
                      db_ffn):
    def body(in_ref, up_ref, dn_ref, ln_ref, gn_ref, al_ref, wx_ref, wbc_ref, bx_ref, bbc_ref, wc_ref, bc_ref, wf_ref, bf_ref,
             o_ref):
        def put(key, val, shift=0):
            r, o = BACK[key]
            o_ref[r:r + 1, o + shift:o + shift + val.shape[1]] = val

        o_ref[...] = jnp.zeros_like(o_ref)
        for i, piece in enumerate((in_ref[0:1, :], in_ref[1:2, :], up_ref[3:4, :], up_ref[0:1, :], up_ref[1:2, :],
                                   dn_ref[1:2, :])):
            put("ada_b", piece, i * D)
        put("norm1_w", in_ref[2:3, :])
        put("ssd_conv_b", bx_ref[...])
        put("ssd_conv_b", bbc_ref[...], D)
        put("dt_bias", in_ref[3:4, 0:LANES])
        put("a_log", al_ref[...])
        lane = lax.broadcasted_iota(jnp.int32, (1, LANES), 1)
        col = lax.broadcasted_iota(jnp.int32, (1, D), 1)
        per_col = gn_ref[1:2, :]
        d_skip = jnp.zeros((1, LANES), f32)
        for h in range(HEADS):
            in_head = jnp.logical_and(col >= h * HEAD_P, col < (h + 1) * HEAD_P)
            s = jnp.sum(jnp.where(in_head, per_col, 0.0), axis=1, keepdims=True)
            d_skip = d_skip + jnp.where(lane == h, s, 0.0)
        put("d_skip", d_skip)
        put("ssd_norm_w", gn_ref[0:1, :])
        put("conf_conv_b", bc_ref[...])
        put("conf_ln_w", ln_ref[0:1, :])
        put("conf_ln_b", ln_ref[1:2, :])
        put("norm2_w", up_ref[2:3, :])
        put("ffn_conv_b", bf_ref[0])
        put("ffn_conv_b", bf_ref[1], D_FF)
        put("final_norm_w", dn_ref[0:1, :])
        for j in range(K_SSD):
            put(("ssd_conv_w", j), wx_ref[j:j + 1, :])
            put(("ssd_conv_w", j), wbc_ref[j:j + 1, :], D)
        for j in range(K_CONF):
            put(("conf_conv_w", j), wc_ref[j:j + 1, :])
        for j in range(K_FFN):
            put(("ffn_conv_w", j), wf_ref[0, j:j + 1, :])
            put(("ffn_conv_w", j), wf_ref[1, j:j + 1, :], D_FF)

    return pl.pallas_call(body, name="pack_small_grads", out_shape=jax.ShapeDtypeStruct((8, BACK_N), f32), compiler_params=_VM)(
        st_in, st_up, st_down, st_ln, st_gn, d_alog, dw_sx, dw_sbc, db_sx, db_sbc, dw_conf, db_conf, dw_ffn, db_ffn)


def _small_adamw(got, chip, w, m, v):
    names = VECTORS + tuple(CONVS)
    n_par = len(names)

    def body(chip_ref, g_ref, *refs):
        ins, outs = refs[:3 * n_par], refs[3 * n_par:]
        dm_ref, outs = outs[0], outs[1:]
        chip_id = chip_ref[0]

        def summed(key, width):
            r, o = BACK[key]
            s = g_ref[r:r + 1, o:o + width]
            for d in range(1, 8):
                s = s + g_ref[8 * d + r:8 * d + r + 1, o:o + width]
            return s

        def mine(full, cw):
            out = full[:, 0:cw]
            for k in range(1, 4):
                out = jnp.where(chip_id == k, full[:, k * cw:(k + 1) * cw], out)
            return out

        r, o = BACK["ada_b"]
        for d in range(8):
            dm_ref[d:d + 1, :] = mine(g_ref[8 * d + r:8 * d + r + 1, o:o + 6 * D], 6 * D // 4)
        for i, (nm, size) in enumerate(zip(VECTORS, VECTOR_SIZES)):
            g = summed(nm, -(-size // LANES) * LANES)[:, 0:size]
            res = _adam_math(ins[3 * i][...], g, ins[3 * i + 1][...], ins[3 * i + 2][...])
            for ref, val in zip(outs[4 * i:4 * i + 4], (g,) + res):
                ref[...] = val
        for i, (nm, (taps, cols)) in enumerate(CONVS.items(), start=len(VECTORS)):
            for j in range(taps):
                g = mine(summed((nm, j), cols), cols // 4)
                res = _adam_math(ins[3 * i][0, j:j + 1, :], g, ins[3 * i + 1][0, j:j + 1, :], ins[3 * i + 2][0, j:j + 1, :])
                for ref, val in zip(outs[4 * i:4 * i + 4], (g,) + res):
                    ref[0, j:j + 1, :] = val

    params = [a[nm] for nm in names for a in (w, m, v)]
    whole = lambda s: pl.BlockSpec(s, lambda i, chip, nd=len(s): (0,) * nd)
    out_shape = [jax.ShapeDtypeStruct((8, 6 * D // 4), f32)] + [jax.ShapeDtypeStruct(w[nm].shape, f32) for nm in names for _ in range(4)]
    outs = pl.pallas_call(
        body, name="small_adamw", out_shape=tuple(out_shape), compiler_params=_VM,
        grid_spec=pltpu.PrefetchScalarGridSpec(
            num_scalar_prefetch=1, grid=(1,), in_specs=[whole(got.shape)] + [whole(p.shape) for p in params],
            out_specs=tuple(whole(s.shape) for s in out_shape)),
    )(_scalar(chip), got, *params)
    return outs[0], {nm: outs[1 + 4 * i:5 + 4 * i] for i, nm in enumerate(names)}


W_IN_COLS = 4624
W_IN_SHARD = W_IN_COLS // 4
W_IN_SHARD_PAD = 1280
_SEGMENTS = ((0, 1024, OFF_Z), (1024, 2560, OFF_XBC), (2560, 2576, OFF_DT), (2576, 3600, OFF_CA), (3600, 4624, OFF_CG))


def _in_pieces(bounds=()):
    out = []
    for k in range(4):
        s0, s1 = k * W_IN_SHARD, (k + 1) * W_IN_SHARD
        for lo, hi, off in _SEGMENTS:
            a, b = max(lo, s0), min(hi, s1)
            while a < b:
                p = off + a - lo
                e = min([b - a] + [c - p for c in bounds if c > p])
                out.append((k, a - s0, p, e))
                a += e
    return out


def _pack_w_in(shards):
    pieces = _in_pieces()

    def body(s_ref, o_ref):
        o_ref[:, OFF_DT:W_PACK] = jnp.zeros((TM, W_PACK - OFF_DT), MX)
        for k, c, p, n in pieces:
            o_ref[:, p:p + n] = s_ref[k, :, c:c + n]

    return pl.pallas_call(
        body, name="pack_w_in", grid=(D // TM,), out_shape=jax.ShapeDtypeStruct((D, W_PACK), MX),
        in_specs=[pl.BlockSpec((4, TM, W_IN_SHARD_PAD), lambda i: (0, i, 0))],
        out_specs=pl.BlockSpec((TM, W_PACK), lambda i: (i, 0)), compiler_params=_cp("arbitrary"),
    )(shards)


def _unpack_g_in(g):
    srcs = ((OFF_Z, D), (OFF_XBC, D), (OFF_XBC + D, 2 * CW), (OFF_CA, D), (OFF_CG, D), (OFF_DT, LANES))
    pieces = _in_pieces(tuple(o for o, _ in srcs) + tuple(o + n for o, n in srcs))

    def body(z_ref, x_ref, bc_ref, cf_ref, dt_ref, o_ref):
        read = (lambda lo, hi: z_ref[:, lo:hi], lambda lo, hi: x_ref[:, lo:hi], lambda lo, hi: bc_ref[:, lo:hi],
                lambda lo, hi: cf_ref[0, :, lo:hi], lambda lo, hi: cf_ref[1, :, lo:hi], lambda lo, hi: dt_ref[:, lo:hi])
        o_ref[:, :, W_IN_SHARD - 4:W_IN_SHARD_PAD] = jnp.zeros((4, TM, W_IN_SHARD_PAD - W_IN_SHARD + 4), MX)
        for k, c, p, n in pieces:
            i = [q for q, (o, w) in enumerate(srcs) if o <= p < o + w][0]
            o_ref[k, :, c:c + n] = read[i](p - srcs[i][0], p - srcs[i][0] + n)

    blk = lambda w: pl.BlockSpec((TM, w), lambda i: (i, 0))
    return pl.pallas_call(
        body, name="unpack_g_in", grid=(D // TM,), out_shape=jax.ShapeDtypeStruct((4, D, W_IN_SHARD_PAD), MX),
        in_specs=[blk(D), blk(D), blk(2 * CW), pl.BlockSpec((2, TM, D), lambda i: (0, i, 0)), blk(LANES)],
        out_specs=pl.BlockSpec((4, TM, W_IN_SHARD_PAD), lambda i: (0, i, 0)), compiler_params=_cp("arbitrary"),
    )(g["z"], g["x"], g["bc"], g["conf"], g["dt"])


def _scalar(v):
    return jnp.reshape(v, (1,)).astype(jnp.int32)


def _cast_into_slot(w, width, chip):
    r, c = w.shape
    h = r // 2
    tm = _row_tile(h)
    nj = h // tm

    def body(chip_ref, w_ref, o_ref):
        v = w_ref[...].astype(MX)
        o_ref[0, 0] = v if width == c else jnp.concatenate([v, jnp.zeros((tm, width - c), MX)], axis=1)

    return pl.pallas_call(
        body, name=f"cast_into_slot_{r}x{c}", out_shape=jax.ShapeDtypeStruct((4, 2, h, width), MX),
        grid_spec=pltpu.PrefetchScalarGridSpec(
            num_scalar_prefetch=1, grid=(2, nj),
            in_specs=[pl.BlockSpec((tm, c), lambda i, j, chip: (i * nj + j, 0))],
            out_specs=pl.BlockSpec((1, 1, tm, width), lambda i, j, chip: (chip[0], i, j, 0))),
        compiler_params=_cp("arbitrary", "arbitrary"),
    )(_scalar(chip), w)


ANY = pl.BlockSpec(memory_space=pl.ANY)


def _place():
    x, y, c = lax.axis_index("x"), lax.axis_index("y"), lax.axis_index("c")
    return x, y, c, [(1 - x, y), (x, 1 - y), (1 - x, 1 - y)]


def _gather_rows(block):
    m_per, n = block.shape

    def body(x_ref, out_ref, send_sems, recv_sems, local_sem):
        x, y, c, chips = _place()
        me, sibling = (x, y, c), (x, y, 1 - c)

        def rows(px, py, pc):
            return out_ref.at[pl.ds((4 * px + 2 * py + pc) * m_per, m_per), :]

        def copy(k, blk, to, src=None):
            return pltpu.make_async_remote_copy(
                src_ref=rows(*blk) if src is None else src, dst_ref=rows(*blk), send_sem=send_sems.at[k],
                recv_sem=recv_sems.at[k], device_id=to, device_id_type=MESH)

        mine = pltpu.make_async_copy(x_ref, rows(*me), local_sem)
        mine.start()
        first = [copy(0, me, sibling, src=x_ref)]
        first += [copy(1 + j, me, (*chip, c), src=x_ref) for j, chip in enumerate(chips)]
        for cp in first:
            cp.start()
        passed = [copy(4 + j, (*chip, c), sibling) for j, chip in enumerate(chips)]
        for j, chip in enumerate(chips):
            copy(1 + j, (*chip, c), me).wait_recv()
            passed[j].start()
        copy(0, sibling, me).wait_recv()
        for j, chip in enumerate(chips):
            copy(4 + j, (*chip, 1 - c), me).wait_recv()
        for cp in first + passed:
            cp.wait_send()
        mine.wait()

    return pl.pallas_call(
        body, name=f"gather_rows_{m_per}x{n}", out_shape=jax.ShapeDtypeStruct((8 * m_per, n), block.dtype),
        in_specs=[pl.BlockSpec(memory_space=pltpu.VMEM)], out_specs=pl.BlockSpec(memory_space=pltpu.VMEM),
        scratch_shapes=[pltpu.SemaphoreType.DMA((7,)), pltpu.SemaphoreType.DMA((7,)), pltpu.SemaphoreType.DMA],
        compiler_params=pltpu.CompilerParams(vmem_limit_bytes=VMEM_LIMIT),
    )(block)


class _GatherRider:
    def __init__(self, slots):
        n = len(slots)
        self.n = n
        self.inputs = list(slots)
        self.out_shape = [jax.ShapeDtypeStruct(s.shape, s.dtype) for s in slots]
        self.scratch = [pltpu.SemaphoreType.DMA((n, 6)), pltpu.SemaphoreType.DMA((n, 6))]
        self.aliases = {a: a for a in range(n)}

    @staticmethod
    def _copy(outs, sems, a, j, k, half, to):
        dst = outs[a].at[k, half]
        return pltpu.make_async_remote_copy(src_ref=dst, dst_ref=dst, send_sem=sems[0].at[a, j], recv_sem=sems[1].at[a, j],
                                            device_id=to, device_id_type=MESH)

    def _first(self, outs, sems):
        x, y, c, chips = _place()
        return [self._copy(outs, sems, a, j, 2 * x + y, c, (*chip, c)) for a in range(self.n) for j, chip in enumerate(chips)]

    def start(self, ins, outs, sems):
        for cp in self._first(outs, sems):
            cp.start()

    def finish(self, ins, outs, sems):
        x, y, c, chips = _place()
        passed = []
        for a in range(self.n):
            for j, (px, py) in enumerate(chips):
                self._copy(outs, sems, a, j, 2 * px + py, c, (x, y, c)).wait_recv()
                fwd = self._copy(outs, sems, a, 3 + j, 2 * px + py, c, (x, y, 1 - c))
                fwd.start()
                passed.append(fwd)
        for a in range(self.n):
            for j, (px, py) in enumerate(chips):
                self._copy(outs, sems, a, 3 + j, 2 * px + py, 1 - c, (x, y, c)).wait_recv()
        for cp in self._first(outs, sems) + passed:
            cp.wait_send()


class _ScatterRider:
    def __init__(self, parts):
        n = len(parts)
        self.n = n
        self.inputs = list(parts)
        self.out_shape = [jax.ShapeDtypeStruct((3,) + p.shape[1:], p.dtype) for p in parts]
        self.scratch = [pltpu.SemaphoreType.DMA((n, 3)), pltpu.SemaphoreType.DMA((n, 3))]
        self.aliases = {}

    def _copies(self, ins, outs, sems):
        x, y, c, chips = _place()
        return [pltpu.make_async_remote_copy(
            src_ref=ins[a].at[2 * px + py], dst_ref=outs[a].at[j], send_sem=sems[0].at[a, j], recv_sem=sems[1].at[a, j],
            device_id=(px, py, c), device_id_type=MESH) for a in range(self.n) for j, (px, py) in enumerate(chips)]

    def start(self, ins, outs, sems):
        for cp in self._copies(ins, outs, sems):
            cp.start()

    def finish(self, ins, outs, sems):
        for cp in self._copies(ins, outs, sems):
            cp.wait()


def _ride_alone(rider, name):
    n = len(rider.inputs)

    def body(*refs):
        ins, outs, sems = refs[:n], refs[n:n + len(rider.out_shape)], refs[n + len(rider.out_shape):]
        rider.start(ins, outs, sems)
        rider.finish(ins, outs, sems)

    return pl.pallas_call(
        body, name=name, out_shape=tuple(rider.out_shape), in_specs=[ANY] * n, out_specs=tuple([ANY] * len(rider.out_shape)),
        input_output_aliases=dict(rider.aliases), scratch_shapes=list(rider.scratch),
    )(*rider.inputs)


def _swap_halves(grads, name):
    n = len(grads)

    def body(*refs):
        ins, got = refs[:n], refs[n:2 * n]
        send_sems, recv_sems = refs[2 * n:]
        x, y, c, _ = _place()
        sent = [pltpu.make_async_remote_copy(
            src_ref=ins[a].at[k, 1 - c], dst_ref=got[a].at[k], send_sem=send_sems.at[a, k], recv_sem=recv_sems.at[a, k],
            device_id=(x, y, 1 - c), device_id_type=MESH) for a in range(n) for k in range(4)]
        for cp in sent:
            cp.start()
        for cp in sent:
            cp.wait()

    return pl.pallas_call(
        body, name=name, out_shape=tuple(jax.ShapeDtypeStruct((4,) + g.shape[2:], g.dtype) for g in grads),
        in_specs=[ANY] * n, out_specs=tuple([ANY] * n),
        scratch_shapes=[pltpu.SemaphoreType.DMA((n, 4)), pltpu.SemaphoreType.DMA((n, 4))],
    )(*grads)


class _Reducer:
    def __init__(self, chip, core):
        self.chip, self.core, self.parts, self.sums = chip, core, {}, {}

    def begin(self, name, grad):
        got, = _swap_halves([grad], "swap_halves_" + name)
        self.parts[name] = _add_pair(grad, got, self.core, name)
        return _ScatterRider([self.parts[name]])

    def end(self, name, others):
        self.sums[name] = _add_chips(self.parts[name], others[0], self.chip, name)


def _swap_sums(halves):
    n = len(halves)

    def body(*refs):
        ins, outs = refs[:n], refs[n:2 * n]
        send_sems, recv_sems = refs[2 * n:]
        x, y, c, _ = _place()
        sent = [pltpu.make_async_remote_copy(
            src_ref=ins[a], dst_ref=outs[a], send_sem=send_sems.at[a], recv_sem=recv_sems.at[a],
            device_id=(x, y, 1 - c), device_id_type=MESH) for a in range(n)]
        for cp in sent:
            cp.start()
        for cp in sent:
            cp.wait()

    return pl.pallas_call(
        body, name="swap_sums", out_shape=tuple(jax.ShapeDtypeStruct(s.shape, s.dtype) for s in halves),
        in_specs=[ANY] * n, out_specs=tuple([ANY] * n),
        scratch_shapes=[pltpu.SemaphoreType.DMA((n,)), pltpu.SemaphoreType.DMA((n,))],
    )(*halves)


def _row_tile(r):
    for tm in (TM, 176, 128, 64, 32, 16, 8):
        if r % tm == 0:
            return tm
    return r


def _add_pair(mine, got, core, name):
    k, _, h, c = mine.shape
    tm = _row_tile(h)

    def body(core_ref, a_ref, b_ref, o_ref):
        o_ref[0] = (a_ref[0, 0].astype(f32) + b_ref[0].astype(f32)).astype(MX)

    blk = pl.BlockSpec((1, tm, c), lambda i, j, core: (i, j, 0))
    return pl.pallas_call(
        body, name="add_pair_" + name, out_shape=jax.ShapeDtypeStruct((k, h, c), MX),
        grid_spec=pltpu.PrefetchScalarGridSpec(
            num_scalar_prefetch=1, grid=(k, h // tm),
            in_specs=[pl.BlockSpec((1, 1, tm, c), lambda i, j, core: (i, core[0], j, 0)), blk], out_specs=blk),
        compiler_params=_cp("arbitrary", "arbitrary"),
    )(_scalar(core), mine, got)


def _add_chips(parts, others, chip, name):
    _, h, c = parts.shape
    tm = _row_tile(h)

    def body(chip_ref, a_ref, b_ref, o_ref):
        s = a_ref[0].astype(f32) + b_ref[0].astype(f32)
        o_ref[...] = (s + b_ref[1].astype(f32)) + b_ref[2].astype(f32)

    return pl.pallas_call(
        body, name="add_chips_" + name, out_shape=jax.ShapeDtypeStruct((h, c), f32),
        grid_spec=pltpu.PrefetchScalarGridSpec(
            num_scalar_prefetch=1, grid=(h // tm,),
            in_specs=[pl.BlockSpec((1, tm, c), lambda i, chip: (chip[0], i, 0)),
                      pl.BlockSpec((3, tm, c), lambda i, chip: (0, i, 0))],
            out_specs=pl.BlockSpec((tm, c), lambda i, chip: (i, 0))),
        compiler_params=_cp("arbitrary"),
    )(_scalar(chip), parts, others)


def _adam_math(w, g, m, v):
    m = ADAM_B1 * m + (1.0 - ADAM_B1) * g
    v = ADAM_B2 * v + (1.0 - ADAM_B2) * (g * g)
    m_hat = m / (1.0 - ADAM_B1 ** ADAM_STEP)
    v_hat = v / (1.0 - ADAM_B2 ** ADAM_STEP)
    return -ADAM_LR * (m_hat / (jnp.sqrt(v_hat) + ADAM_EPS) + ADAM_WD * w), m, v


def _adamw_halves(w, mine, other, m, v, core, name):
    r, c = w.shape
    h = r // 2
    tm = _row_tile(h)
    nj = h // tm
    cg = mine.shape[1]

    def body(core_ref, w_ref, a_ref, b_ref, m_ref, v_ref, g_ref, d_ref, nm_ref, nv_ref):
        g = jnp.where(pl.program_id(0) == core_ref[0], a_ref[:, 0:c], b_ref[:, 0:c])
        g_ref[...] = g
        d_ref[...], nm_ref[...], nv_ref[...] = _adam_math(w_ref[...], g, m_ref[...], v_ref[...])

    blk = pl.BlockSpec((tm, c), lambda i, j, core: (i * nj + j, 0))
    gblk = pl.BlockSpec((tm, cg), lambda i, j, core: (j, 0))
    return pl.pallas_call(
        body, name=name, out_shape=tuple([jax.ShapeDtypeStruct((r, c), f32)] * 4),
        grid_spec=pltpu.PrefetchScalarGridSpec(
            num_scalar_prefetch=1, grid=(2, nj), in_specs=[blk, gblk, gblk, blk, blk], out_specs=(blk,) * 4),
        compiler_params=_cp("arbitrary", "arbitrary"),
    )(_scalar(core), w, mine, other, m, v)


def _ada_forward(c_all, ada_w):
    def body(c_ref, w_ref, o_ref):
        o_ref[...] = jnp.dot(_silu(c_ref[...]).astype(MX), w_ref[...].astype(MX), preferred_element_type=f32)

    return pl.pallas_call(body, name="ada_forward", out_shape=jax.ShapeDtypeStruct((8, ada_w.shape[1]), f32),
                          compiler_params=pltpu.CompilerParams(vmem_limit_bytes=VMEM_LIMIT))(c_all, ada_w)


def _ada_adamw(c_all_t, d_mod, w, m, v):
    r, c = w.shape
    tm = TM

    def body(ct_ref, dm_ref, w_ref, m_ref, v_ref, g_ref, d_ref, nm_ref, nv_ref):
        ca = _silu(ct_ref[...])
        g = ca[:, 0:1] * dm_ref[0:1, :]
        for b in range(1, 8):
            g = g + ca[:, b:b + 1] * dm_ref[b:b + 1, :]
        g_ref[...] = g
        d_ref[...], nm_ref[...], nv_ref[...] = _adam_math(w_ref[...], g, m_ref[...], v_ref[...])

    blk = pl.BlockSpec((tm, c), lambda i: (i, 0))
    return pl.pallas_call(
        body, name="ada_adamw", grid=(r // tm,), out_shape=tuple([jax.ShapeDtypeStruct((r, c), f32)] * 4),
        in_specs=[pl.BlockSpec((tm, 8), lambda i: (i, 0)), pl.BlockSpec((8, c), lambda i: (0, 0)), blk, blk, blk],
        out_specs=(blk,) * 4, compiler_params=_cp("arbitrary"),
    )(c_all_t, d_mod, w, m, v)


WEIGHTS = ("ada_w", "ada_b", "norm1_w", "w_in", "ssd_conv_w", "ssd_conv_b", "dt_bias", "a_log", "d_skip", "ssd_norm_w",
           "conf_conv_w", "conf_conv_b", "conf_ln_w", "conf_ln_b", "w_out", "norm2_w", "w_up", "ffn_conv_w", "ffn_conv_b",
           "w_down", "final_norm_w")


def kernel(x, c, ada_w, ada_b, norm1_w, w_in, ssd_conv_w, ssd_conv_b, dt_bias, a_log, d_skip, ssd_norm_w, conf_conv_w, conf_conv_b, conf_ln_w, conf_ln_b, w_out, norm2_w, w_up, ffn_conv_w, ffn_conv_b, w_down, final_norm_w, loss_target, m_ada_w, m_ada_b, m_norm1_w, m_w_in, m_ssd_conv_w, m_ssd_conv_b, m_dt_bias, m_a_log, m_d_skip, m_ssd_norm_w, m_conf_conv_w, m_conf_conv_b, m_conf_ln_w, m_conf_ln_b, m_w_out, m_norm2_w, m_w_up, m_ffn_conv_w, m_ffn_conv_b, m_w_down, m_final_norm_w, v_ada_w, v_ada_b, v_norm1_w, v_w_in, v_ssd_conv_w, v_ssd_conv_b, v_dt_bias, v_a_log, v_d_skip, v_ssd_norm_w, v_conf_conv_w, v_conf_conv_b, v_conf_ln_w, v_conf_ln_b, v_w_out, v_norm2_w, v_w_up, v_ffn_conv_w, v_ffn_conv_b, v_w_down, v_final_norm_w):
    given = dict(locals())
    w = {n: given[n] for n in WEIGHTS}
    mom = {n: given["m_" + n] for n in WEIGHTS}
    var = {n: given["v_" + n] for n in WEIGHTS}
    chip = 2 * lax.axis_index("x") + lax.axis_index("y")
    me = 2 * chip + lax.axis_index("c")

    c_all, *convs = _unpack_front(_gather_rows(_pack_front(c, [w[n] for n in CONVS])))
    conv_full = dict(zip(CONVS, convs))

    mod_cols = _gather_rows(_ada_forward(c_all, ada_w[0])).reshape(8, 8, -1)[0::2]
    mod = lax.dynamic_index_in_dim(mod_cols, me, axis=1, keepdims=False).reshape(1, 6 * D) + ada_b

    core = lax.axis_index("c")
    a_in, = _ride_alone(_GatherRider([_cast_into_slot(w_in[0], W_IN_SHARD_PAD, chip)]), "gather_w_in")
    w_pack = _pack_w_in(a_in.reshape(4, D, W_IN_SHARD_PAD))
    late = _GatherRider([_cast_into_slot(w_out[0], D, chip), _cast_into_slot(w_up[0], UP_SHARD, chip),
                         _cast_into_slot(w_down[0], D, chip)])

    flat = lambda a: a.reshape(1, -1) if a.ndim == 1 else a
    small = {n: flat(w[n]) for n in VECTORS if n != "ada_b"}
    small.update(conv_full)
    reducer = _Reducer(chip, core)
    loss_mine, grad_x, _, gsmall = _local_step(x[0], mod, loss_target[0], w_pack, late, small, reducer)
    loss = lax.psum(loss_mine, ("x", "y", "c"))
    big = ("w_in", "w_out", "w_up", "w_down")
    summed = [reducer.sums[n] for n in big]
    big_halves = dict(zip(big, zip(summed, _swap_sums(summed))))
    grads, delta, new_m, new_v = {}, {}, {}, {}

    names = VECTORS + tuple(CONVS)
    d_mod_mine, res = _small_adamw(_gather_rows(gsmall), chip, *[{n: flat(d[n]) for n in names} for d in (w, mom, var)])
    for n in names:
        grads[n], delta[n], new_m[n], new_v[n] = [r.reshape(w[n].shape) for r in res[n]]

    for n in big:
        res = _adamw_halves(w[n][0], *big_halves[n], mom[n][0], var[n][0], core, "adamw_" + n)
        grads[n], delta[n], new_m[n], new_v[n] = [r[None] for r in res]
    res = _ada_adamw(c_all.T, d_mod_mine, ada_w[0], m_ada_w[0], v_ada_w[0])
    grads["ada_w"], delta["ada_w"], new_m["ada_w"], new_v["ada_w"] = [r[None] for r in res]

    return (loss, grad_x[None], *[grads[n] for n in WEIGHTS], *[delta[n] for n in WEIGHTS],
            *[new_m[n] for n in WEIGHTS], *[new_v[n] for n in WEIGHTS])
```

```python
import functools

import jax
import jax.numpy as jnp
from jax import lax
from jax.experimental import pallas as pl
from jax.experimental.pallas import tpu as pltpu

f32 = jnp.float32
MX = jnp.bfloat16

D = 1024
HEADS = 16
HEAD_P = 64
STATE_N = 128
D_XBC = 1536
D_FF = 2816
UP_SHARD = 2 * D_FF // 4
K_SSD, K_CONF, K_FFN = 4, 31, 3
CHUNK = 128
OFF_Z, OFF_XBC, OFF_CA, OFF_CG, OFF_DT = 0, 1024, 2560, 3584, 4608
W_PACK = 4736
TM = 256
CW = 256
RC = 64
LANES = 128
VMEM_LIMIT = 56 * 1024 * 1024

ADAM_LR, ADAM_B1, ADAM_B2, ADAM_EPS, ADAM_WD, ADAM_STEP = 0.001, 0.9, 0.999, 1e-08, 0.01, 10

MESH = pl.DeviceIdType.MESH


def _cp(*sem):
    return pltpu.CompilerParams(dimension_semantics=sem, vmem_limit_bytes=VMEM_LIMIT)


def _resident(shape):
    nd = len(shape)
    return pl.BlockSpec(shape, lambda *_: (0,) * nd, pipeline_mode=pl.Buffered(1))


def _row(width=D):
    return pl.BlockSpec((1, width), lambda *_: (0, 0))


def _call(body, *, name, grid, in_specs, out_specs, out_shape, args, sem, scratch_shapes=(), rider=None):
    if rider is None:
        outs = pl.pallas_call(body, name=name, grid=grid, in_specs=list(in_specs), out_specs=tuple(out_specs),
                              out_shape=tuple(out_shape), scratch_shapes=list(scratch_shapes), compiler_params=_cp(*sem))(*args)
        return tuple(outs), ()
    ni, no, ns = len(in_specs), len(out_specs), len(scratch_shapes)
    ri, ro = len(rider.inputs), len(rider.out_shape)

    def full(*refs):
        base_in, r_in = refs[:ni], refs[ni:ni + ri]
        base_out, r_out = refs[ni + ri:ni + ri + no], refs[ni + ri + no:ni + ri + no + ro]
        base_scr, r_scr = refs[ni + ri + no + ro:ni + ri + no + ro + ns], refs[ni + ri + no + ro + ns:]
        ids = [pl.program_id(a) for a in range(len(grid))]
        first = functools.reduce(jnp.logical_and, [i == 0 for i in ids])
        last = functools.reduce(jnp.logical_and, [i == g - 1 for i, g in zip(ids, grid)])

        @pl.when(first)
        def _():
            rider.start(r_in, r_out, r_scr)

        body(*base_in, *base_out, *base_scr)

        @pl.when(last)
        def _():
            rider.finish(r_in, r_out, r_scr)

    outs = pl.pallas_call(
        full, name=name, grid=grid, in_specs=list(in_specs) + [ANY] * ri, out_specs=tuple(out_specs) + (ANY,) * ro,
        out_shape=tuple(out_shape) + tuple(rider.out_shape), scratch_shapes=list(scratch_shapes) + list(rider.scratch),
        input_output_aliases={ni + i: no + j for i, j in rider.aliases.items()}, compiler_params=_cp(*sem),
    )(*args, *rider.inputs)
    return tuple(outs[:no]), tuple(outs[no:])


def _silu(v):
    return v * jax.nn.sigmoid(v)


def _dsilu(v):
    s = jax.nn.sigmoid(v)
    return s * (1.0 + v * (1.0 - s))


def _softplus(v):
    return jnp.maximum(v, 0.0) + jnp.log1p(jnp.exp(-jnp.abs(v)))


def _mm(a, b):
    return jnp.dot(a.astype(MX), b.astype(MX), preferred_element_type=f32)


def _mm_nt(a, b):
    return lax.dot_general(a.astype(MX), b.astype(MX), (((1,), (1,)), ((), ())), preferred_element_type=f32)


def _mm_tn(a, b):
    return lax.dot_general(a.astype(MX), b.astype(MX), (((0,), (0,)), ((), ())), preferred_element_type=f32)


def _ln_inproj(x, mod, norm1_w, w_pack):
    t = x.shape[0]

    def body(x_ref, mod_ref, nw_ref, w_ref, proj_ref, h_ref):
        xv = x_ref[...]
        rstd = lax.rsqrt(jnp.mean(xv * xv, axis=-1, keepdims=True) + 1e-6)
        h = (xv * rstd * nw_ref[...]) * (1.0 + mod_ref[:, D:2 * D]) + mod_ref[:, 0:D]
        hb = h.astype(MX)
        h_ref[...] = hb
        proj_ref[...] = jnp.dot(hb, w_ref[...], preferred_element_type=f32)

    return pl.pallas_call(
        body, name="ln_inproj", grid=(t // TM,),
        out_shape=(jax.ShapeDtypeStruct((t, W_PACK), f32), jax.ShapeDtypeStruct((t, D), MX)),
        in_specs=[pl.BlockSpec((TM, D), lambda i: (i, 0)), _row(6 * D), _row(), _resident((D, W_PACK))],
        out_specs=(pl.BlockSpec((TM, W_PACK), lambda i: (i, 0)), pl.BlockSpec((TM, D), lambda i: (i, 0))),
        compiler_params=_cp("arbitrary"),
    )(x, mod, norm1_w, w_pack)


def _ssd_gate_norm(y_scan, xbc_act, proj, d_skip_row, ssd_norm_w):
    t = y_scan.shape[0]

    def body(y_ref, xs_ref, z_ref, dsk_ref, nw_ref, o_ref):
        y = y_ref[...] + xs_ref[...] * dsk_ref[...]
        yz = y * _silu(z_ref[...])
        rstd = lax.rsqrt(jnp.mean(yz * yz, axis=-1, keepdims=True) + 1e-6)
        o_ref[...] = (yz * rstd * nw_ref[...]).astype(MX)

    blk = pl.BlockSpec((TM, D), lambda i: (i, 0))
    return pl.pallas_call(
        body, name="ssd_gate_norm", grid=(t // TM,), out_shape=jax.ShapeDtypeStruct((t, D), MX),
        in_specs=[blk, blk, blk, _row(), _row()], out_specs=blk, compiler_params=_cp("arbitrary"),
    )(y_scan, xbc_act, proj, d_skip_row, ssd_norm_w)


def _ln_silu(u_conv, ln_w, ln_b):
    t = u_conv.shape[0]

    def body(u_ref, w_ref, b_ref, o_ref):
        u = u_ref[...]
        mu = jnp.mean(u, axis=-1, keepdims=True)
        uc = u - mu
        rstd = lax.rsqrt(jnp.mean(uc * uc, axis=-1, keepdims=True) + 1e-5)
        o_ref[...] = _silu(uc * rstd * w_ref[...] + b_ref[...]).astype(MX)

    blk = pl.BlockSpec((TM, D), lambda i: (i, 0))
    return pl.pallas_call(
        body, name="ln_silu", grid=(t // TM,), out_shape=jax.ShapeDtypeStruct((t, D), MX),
        in_specs=[blk, _row(), _row()], out_specs=blk, compiler_params=_cp("arbitrary"),
    )(u_conv, ln_w, ln_b)


def _outproj_ln2_up(y_ssd, u, w_out, x, mod, norm2_w, w_up):
    t = x.shape[0]

    def body(y_ref, u_ref, wo_ref, x_ref, mod_ref, nw_ref, wu_ref, mix_ref, x1_ref, h2_ref, up_ref):
        mix = jnp.dot(y_ref[...], wo_ref[0:D, :], preferred_element_type=f32)
        mix = mix + jnp.dot(u_ref[...], wo_ref[D:2 * D, :], preferred_element_type=f32)
        mix_ref[...] = mix
        x1 = x_ref[...] + mod_ref[:, 2 * D:3 * D] * mix
        x1_ref[...] = x1
        rstd = lax.rsqrt(jnp.mean(x1 * x1, axis=-1, keepdims=True) + 1e-6)
        h2 = ((x1 * rstd * nw_ref[...]) * (1.0 + mod_ref[:, 4 * D:5 * D]) + mod_ref[:, 3 * D:4 * D]).astype(MX)
        h2_ref[...] = h2
        for k in range(4):
            up_ref[:, k * UP_SHARD:(k + 1) * UP_SHARD] = jnp.dot(h2, wu_ref[k], preferred_element_type=f32)

    blk = pl.BlockSpec((TM, D), lambda i: (i, 0))
    return pl.pallas_call(
        body, name="outproj_ln2_up", grid=(t // TM,),
        out_shape=(jax.ShapeDtypeStruct((t, D), f32), jax.ShapeDtypeStruct((t, D), f32),
                   jax.ShapeDtypeStruct((t, D), MX), jax.ShapeDtypeStruct((t, 2 * D_FF), f32)),
        in_specs=[blk, blk, _resident((2 * D, D)), blk, _row(6 * D), _row(), _resident((4, D, UP_SHARD))],
        out_specs=(blk, blk, blk, pl.BlockSpec((TM, 2 * D_FF), lambda i: (i, 0))),
        compiler_params=_cp("arbitrary"),
    )(y_ssd, u, w_out, x, mod, norm2_w, w_up)


def _down_loss(act, w_down, x1, mod, final_norm_w, target):
    t = x1.shape[0]

    def body(a_ref, wd_ref, x1_ref, mod_ref, wf_ref, tgt_ref, dx2_ref, dffn_ref, dact_ref, st_ref):
        @pl.when(pl.program_id(0) == 0)
        def _():
            st_ref[...] = jnp.zeros_like(st_ref)

        g2 = mod_ref[:, 5 * D:6 * D]
        ffn = jnp.dot(a_ref[...], wd_ref[...], preferred_element_type=f32)
        x2 = x1_ref[...] + g2 * ffn
        rstd = lax.rsqrt(jnp.mean(x2 * x2, axis=-1, keepdims=True) + 1e-6)
        xh = x2 * rstd
        wf = wf_ref[...]
        err = xh * wf - tgt_ref[...]
        dy = err * (1.0 / D)
        dxh = dy * wf
        dx2 = rstd * (dxh - xh * jnp.mean(dxh * xh, axis=-1, keepdims=True))
        dx2_ref[...] = dx2
        dffn = (g2 * dx2).astype(MX)
        dffn_ref[...] = dffn
        dact_ref[...] = lax.dot_general(dffn, wd_ref[...], (((1,), (1,)), ((), ())), preferred_element_type=f32)
        st_ref[0:1, :] += jnp.sum(dy * xh, axis=0, keepdims=True)
        st_ref[1:2, :] += jnp.sum(dx2 * ffn, axis=0, keepdims=True)
        st_ref[2:3, :] += jnp.sum(0.5 * jnp.mean(err * err, axis=-1, keepdims=True), axis=0, keepdims=True)

    blk = pl.BlockSpec((TM, D), lambda i: (i, 0))
    ablk = pl.BlockSpec((TM, D_FF), lambda i: (i, 0))
    return pl.pallas_call(
        body, name="down_loss", grid=(t // TM,),
        out_shape=(jax.ShapeDtypeStruct((t, D), f32), jax.ShapeDtypeStruct((t, D), MX),
                   jax.ShapeDtypeStruct((t, D_FF), f32), jax.ShapeDtypeStruct((8, D), f32)),
        in_specs=[ablk, _resident((D_FF, D)), blk, _row(6 * D), _row(), blk],
        out_specs=(blk, blk, ablk, pl.BlockSpec((8, D), lambda i: (0, 0))),
        compiler_params=_cp("arbitrary"),
    )(act, w_down, x1, mod, final_norm_w, target)


def _pad_of(k):
    return 8 * ((k - 1 + 7) // 8)


def _causal_win(ref, r, t, pad):
    base = pl.multiple_of(r * RC, RC)
    prev = ref[pl.ds(pl.multiple_of(jnp.maximum(base - pad, 0), 8), pad), :]
    prev = jnp.where(r > 0, prev, 0.0)
    return jnp.concatenate([prev, ref[pl.ds(base, RC), :]], axis=0)


def _anti_win(ref, r, t, pad):
    base = pl.multiple_of(r * RC, RC)
    nxt = ref[pl.ds(pl.multiple_of(jnp.minimum(base + RC, t - pad), 8), pad), :]
    nxt = jnp.where(r < t // RC - 1, nxt, 0.0)
    return jnp.concatenate([ref[pl.ds(base, RC), :], nxt], axis=0)


def _shifted(win, offsets):
    for r in range(8):
        mine = [o for o in offsets if o % 8 == r]
        if mine:
            rolled = win if r == 0 else pltpu.roll(win, win.shape[0] - r, 0)
            for o in mine:
                yield o, rolled[o - r:o - r + RC, :]


def _conv_taps(win, w_ref, k, pad):
    first = pad - (k - 1)
    acc = None
    for o, rows in _shifted(win, range(first, first + k)):
        term = w_ref[o - first:o - first + 1, :] * rows
        acc = term if acc is None else acc + term
    return acc


def _corr_taps(win, w_ref, k):
    acc = None
    for o, rows in _shifted(win, range(k)):
        term = w_ref[k - 1 - o:k - o, :] * rows
        acc = term if acc is None else acc + term
    return acc


def _dw_accumulate(dw_scr, d, win, k, pad):
    first = pad - (k - 1)
    for o, rows in _shifted(win, range(first, first + k)):
        j = o - first
        prod = d * rows
        dw_scr[8 * j:8 * j + 8, :] += prod.reshape(RC // 8, 8, prod.shape[-1]).sum(axis=0)


def _dw_finish(dw_scr, dw_ref, k):
    for j in range(k):
        dw_ref[j:j + 1, :] = jnp.sum(dw_scr[8 * j:8 * j + 8, :], axis=0, keepdims=True)


def _rows8(v):
    return v.reshape(RC // 8, 8, v.shape[-1]).sum(axis=0)


def _ssd_conv_fwd(proj, conv_w, conv_b):
    t = proj.shape[0]
    pad = _pad_of(K_SSD)
    c0 = OFF_XBC // CW

    def body(x_ref, w_ref, b_ref, o_ref):
        def step(r, carry):
            win = _causal_win(x_ref, r, t, pad)
            o_ref[pl.ds(pl.multiple_of(r * RC, RC), RC), :] = _silu(_conv_taps(win, w_ref, K_SSD, pad) + b_ref[...])
            return carry
        lax.fori_loop(0, t // RC, step, 0)

    return pl.pallas_call(
        body, name="ssd_conv_fwd", grid=(D_XBC // CW,), out_shape=jax.ShapeDtypeStruct((t, D_XBC), f32),
        in_specs=[pl.BlockSpec((t, CW), lambda j: (0, c0 + j)), pl.BlockSpec((K_SSD, CW), lambda j: (0, j)),
                  pl.BlockSpec((1, CW), lambda j: (0, j))],
        out_specs=pl.BlockSpec((t, CW), lambda j: (0, j)), compiler_params=_cp("arbitrary"),
    )(proj, conv_w, conv_b)


def _glu_conv_fwd(proj, conv_w, conv_b, rider=None):
    t = proj.shape[0]
    pad = _pad_of(K_CONF)
    ca, cg = OFF_CA // CW, OFF_CG // CW

    def body(a_ref, g_ref, w_ref, b_ref, o_ref, v_scr):
        def glu(r, carry):
            rows = pl.ds(pl.multiple_of(r * RC, RC), RC)
            v_scr[rows, :] = a_ref[rows, :] * jax.nn.sigmoid(g_ref[rows, :])
            return carry
        lax.fori_loop(0, t // RC, glu, 0)

        def step(r, carry):
            win = _causal_win(v_scr, r, t, pad)
            o_ref[pl.ds(pl.multiple_of(r * RC, RC), RC), :] = _conv_taps(win, w_ref, K_CONF, pad) + b_ref[...]
            return carry
        lax.fori_loop(0, t // RC, step, 0)

    return _call(
        body, name="glu_conv_fwd", grid=(D // CW,), out_shape=(jax.ShapeDtypeStruct((t, D), f32),),
        in_specs=[pl.BlockSpec((t, CW), lambda j: (0, ca + j)), pl.BlockSpec((t, CW), lambda j: (0, cg + j)),
                  pl.BlockSpec((K_CONF, CW), lambda j: (0, j)), pl.BlockSpec((1, CW), lambda j: (0, j))],
        out_specs=(pl.BlockSpec((t, CW), lambda j: (0, j)),),
        scratch_shapes=[pltpu.VMEM((t, CW), f32)], sem=("arbitrary",), args=(proj, proj, conv_w, conv_b), rider=rider)


def _ffn_conv_fwd(up, conv_w, conv_b, rider=None):
    t = up.shape[0]
    pad = _pad_of(K_FFN)
    nb = D_FF // CW

    def body(g_ref, v_ref, wg_ref, wv_ref, bg_ref, bv_ref, o_ref):
        def step(r, carry):
            gc = _conv_taps(_causal_win(g_ref, r, t, pad), wg_ref, K_FFN, pad) + bg_ref[...]
            vc = _conv_taps(_causal_win(v_ref, r, t, pad), wv_ref, K_FFN, pad) + bv_ref[...]
            o_ref[pl.ds(pl.multiple_of(r * RC, RC), RC), :] = (_silu(gc) * vc).astype(MX)
            return carry
        lax.fori_loop(0, t // RC, step, 0)

    return _call(
        body, name="ffn_conv_fwd", grid=(nb,), out_shape=(jax.ShapeDtypeStruct((t, D_FF), MX),),
        in_specs=[pl.BlockSpec((t, CW), lambda j: (0, j)), pl.BlockSpec((t, CW), lambda j: (0, nb + j)),
                  pl.BlockSpec((K_FFN, CW), lambda j: (0, j)), pl.BlockSpec((K_FFN, CW), lambda j: (0, nb + j)),
                  pl.BlockSpec((1, CW), lambda j: (0, j)), pl.BlockSpec((1, CW), lambda j: (0, nb + j))],
        out_specs=(pl.BlockSpec((t, CW), lambda j: (0, j)),), sem=("arbitrary",),
        args=(up, up, conv_w, conv_w, conv_b, conv_b), rider=rider)


def _ffn_conv_bwd(up, conv_w, conv_b, d_act, rider=None):
    t = up.shape[0]
    pad = _pad_of(K_FFN)
    nb = D_FF // CW

    def body(g_ref, v_ref, wg_ref, wv_ref, bg_ref, bv_ref, da_ref, dup_ref, dw_ref, db_ref,
             dg_scr, dv_scr, dwg_scr, dwv_scr, db_scr):
        dwg_scr[...] = jnp.zeros_like(dwg_scr)
        dwv_scr[...] = jnp.zeros_like(dwv_scr)
        db_scr[...] = jnp.zeros_like(db_scr)

        def first(r, carry):
            rows = pl.ds(pl.multiple_of(r * RC, RC), RC)
            gwin = _causal_win(g_ref, r, t, pad)
            vwin = _causal_win(v_ref, r, t, pad)
            gc = _conv_taps(gwin, wg_ref, K_FFN, pad) + bg_ref[...]
            vc = _conv_taps(vwin, wv_ref, K_FFN, pad) + bv_ref[...]
            da = da_ref[rows, :]
            dgc = da * vc * _dsilu(gc)
            dvc = da * _silu(gc)
            dg_scr[rows, :] = dgc
            dv_scr[rows, :] = dvc
            _dw_accumulate(dwg_scr, dgc, gwin, K_FFN, pad)
            _dw_accumulate(dwv_scr, dvc, vwin, K_FFN, pad)
            db_scr[0:8, :] += _rows8(dgc)
            db_scr[8:16, :] += _rows8(dvc)
            return carry
        lax.fori_loop(0, t // RC, first, 0)

        def second(r, carry):
            rows = pl.ds(pl.multiple_of(r * RC, RC), RC)
            dup_ref[0, rows, :] = _corr_taps(_anti_win(dg_scr, r, t, pad), wg_ref, K_FFN).astype(MX)
            dup_ref[1, rows, :] = _corr_taps(_anti_win(dv_scr, r, t, pad), wv_ref, K_FFN).astype(MX)
            return carry
        lax.fori_loop(0, t // RC, second, 0)

        for j in range(K_FFN):
            dw_ref[0, j:j + 1, :] = jnp.sum(dwg_scr[8 * j:8 * j + 8, :], axis=0, keepdims=True)
            dw_ref[1, j:j + 1, :] = jnp.sum(dwv_scr[8 * j:8 * j + 8, :], axis=0, keepdims=True)
        db_ref[0] = jnp.sum(db_scr[0:8, :], axis=0, keepdims=True)
        db_ref[1] = jnp.sum(db_scr[8:16, :], axis=0, keepdims=True)

    return _call(
        body, name="ffn_conv_bwd", grid=(nb,),
        out_shape=(jax.ShapeDtypeStruct((2, t, D_FF), MX), jax.ShapeDtypeStruct((2, K_FFN, D_FF), f32),
                   jax.ShapeDtypeStruct((2, 1, D_FF), f32)),
        in_specs=[pl.BlockSpec((t, CW), lambda j: (0, j)), pl.BlockSpec((t, CW), lambda j: (0, nb + j)),
                  pl.BlockSpec((K_FFN, CW), lambda j: (0, j)), pl.BlockSpec((K_FFN, CW), lambda j: (0, nb + j)),
                  pl.BlockSpec((1, CW), lambda j: (0, j)), pl.BlockSpec((1, CW), lambda j: (0, nb + j)),
                  pl.BlockSpec((t, CW), lambda j: (0, j))],
        out_specs=(pl.BlockSpec((2, t, CW), lambda j: (0, 0, j)), pl.BlockSpec((2, K_FFN, CW), lambda j: (0, 0, j)),
                   pl.BlockSpec((2, 1, CW), lambda j: (0, 0, j))),
        scratch_shapes=[pltpu.VMEM((t, CW), f32), pltpu.VMEM((t, CW), f32), pltpu.VMEM((8 * K_FFN, CW), f32),
                        pltpu.VMEM((8 * K_FFN, CW), f32), pltpu.VMEM((16, CW), f32)],
        sem=("arbitrary",), args=(up, up, conv_w, conv_w, conv_b, conv_b, d_act), rider=rider)


def _glu_conv_bwd(proj, conv_w, d_uconv, rider=None):
    t = proj.shape[0]
    pad = _pad_of(K_CONF)
    ca, cg = OFF_CA // CW, OFF_CG // CW

    def body(a_ref, g_ref, w_ref, du_ref, dc_ref, dw_ref, db_ref, v_scr, dw_scr, db_scr):
        dw_scr[...] = jnp.zeros_like(dw_scr)
        db_scr[...] = jnp.zeros_like(db_scr)

        def glu(r, carry):
            rows = pl.ds(pl.multiple_of(r * RC, RC), RC)
            v_scr[rows, :] = a_ref[rows, :] * jax.nn.sigmoid(g_ref[rows, :])
            return carry
        lax.fori_loop(0, t // RC, glu, 0)

        def step(r, carry):
            rows = pl.ds(pl.multiple_of(r * RC, RC), RC)
            du = du_ref[rows, :]
            _dw_accumulate(dw_scr, du, _causal_win(v_scr, r, t, pad), K_CONF, pad)
            db_scr[...] += _rows8(du)
            dv = _corr_taps(_anti_win(du_ref, r, t, pad), w_ref, K_CONF)
            a = a_ref[rows, :]
            s = jax.nn.sigmoid(g_ref[rows, :])
            dc_ref[0, rows, :] = (dv * s).astype(MX)
            dc_ref[1, rows, :] = (dv * a * s * (1.0 - s)).astype(MX)
            return carry
        lax.fori_loop(0, t // RC, step, 0)
        _dw_finish(dw_scr, dw_ref, K_CONF)
        db_ref[...] = jnp.sum(db_scr[...], axis=0, keepdims=True)

    return _call(
        body, name="glu_conv_bwd", grid=(D // CW,),
        out_shape=(jax.ShapeDtypeStruct((2, t, D), MX), jax.ShapeDtypeStruct((K_CONF, D), f32),
                   jax.ShapeDtypeStruct((1, D), f32)),
        in_specs=[pl.BlockSpec((t, CW), lambda j: (0, ca + j)), pl.BlockSpec((t, CW), lambda j: (0, cg + j)),
                  pl.BlockSpec((K_CONF, CW), lambda j: (0, j)), pl.BlockSpec((t, CW), lambda j: (0, j))],
        out_specs=(pl.BlockSpec((2, t, CW), lambda j: (0, 0, j)), pl.BlockSpec((K_CONF, CW), lambda j: (0, j)),
                   pl.BlockSpec((1, CW), lambda j: (0, j))),
        scratch_shapes=[pltpu.VMEM((t, CW), f32), pltpu.VMEM((8 * K_CONF, CW), f32), pltpu.VMEM((8, CW), f32)],
        sem=("arbitrary",), args=(proj, proj, conv_w, d_uconv), rider=rider)


def _ssd_conv_bwd_x(proj, conv_w, conv_b, d_xs, d_y, d_skip_row):
    t = proj.shape[0]
    pad = _pad_of(K_SSD)
    c0 = OFF_XBC // CW

    def body(x_ref, w_ref, b_ref, dxs_ref, dy_ref, dsk_ref, draw_ref, dw_ref, db_ref, dp_scr, dw_scr, db_scr):
        dw_scr[...] = jnp.zeros_like(dw_scr)
        db_scr[...] = jnp.zeros_like(db_scr)

        def first(r, carry):
            rows = pl.ds(pl.multiple_of(r * RC, RC), RC)
            win = _causal_win(x_ref, r, t, pad)
            pre = _conv_taps(win, w_ref, K_SSD, pad) + b_ref[...]
            dpre = (dxs_ref[rows, :] + dy_ref[rows, :] * dsk_ref[...]) * _dsilu(pre)
            dp_scr[rows, :] = dpre
            _dw_accumulate(dw_scr, dpre, win, K_SSD, pad)
            db_scr[...] += _rows8(dpre)
            return carry
        lax.fori_loop(0, t // RC, first, 0)

        def second(r, carry):
            rows = pl.ds(pl.multiple_of(r * RC, RC), RC)
            draw_ref[rows, :] = _corr_taps(_anti_win(dp_scr, r, t, pad), w_ref, K_SSD).astype(MX)
            return carry
        lax.fori_loop(0, t // RC, second, 0)
        _dw_finish(dw_scr, dw_ref, K_SSD)
        db_ref[...] = jnp.sum(db_scr[...], axis=0, keepdims=True)

    cb = pl.BlockSpec((t, CW), lambda j: (0, j))
    return pl.pallas_call(
        body, name="ssd_conv_bwd_x", grid=(D // CW,),
        out_shape=(jax.ShapeDtypeStruct((t, D), MX), jax.ShapeDtypeStruct((K_SSD, D), f32),
                   jax.ShapeDtypeStruct((1, D), f32)),
        in_specs=[pl.BlockSpec((t, CW), lambda j: (0, c0 + j)), pl.BlockSpec((K_SSD, CW), lambda j: (0, j)),
                  pl.BlockSpec((1, CW), lambda j: (0, j)), cb, cb, pl.BlockSpec((1, CW), lambda j: (0, j))],
        out_specs=(cb, pl.BlockSpec((K_SSD, CW), lambda j: (0, j)), pl.BlockSpec((1, CW), lambda j: (0, j))),
        scratch_shapes=[pltpu.VMEM((t, CW), f32), pltpu.VMEM((8 * K_SSD, CW), f32), pltpu.VMEM((8, CW), f32)],
        compiler_params=_cp("arbitrary"),
    )(proj, conv_w, conv_b, d_xs, d_y, d_skip_row)


def _ssd_conv_bwd_bc(proj, conv_w, conv_b, d_bc):
    t = proj.shape[0]
    pad = _pad_of(K_SSD)
    c0 = (OFF_XBC + D) // CW
    w0 = D // CW

    def body(x_ref, w_ref, b_ref, dbc_ref, draw_ref, dw_ref, db_ref, dp_scr, dw_scr, db_scr):
        dw_scr[...] = jnp.zeros_like(dw_scr)
        db_scr[...] = jnp.zeros_like(db_scr)

        def first(r, carry):
            rows = pl.ds(pl.multiple_of(r * RC, RC), RC)
            win = _causal_win(x_ref, r, t, pad)
            pre = _conv_taps(win, w_ref, K_SSD, pad) + b_ref[...]
            dpre = dbc_ref[0, rows, :] * _dsilu(pre)
            dp_scr[rows, :] = dpre
            _dw_accumulate(dw_scr, dpre, win, K_SSD, pad)
            db_scr[...] += _rows8(dpre)
            return carry
        lax.fori_loop(0, t // RC, first, 0)

        def second(r, carry):
            rows = pl.ds(pl.multiple_of(r * RC, RC), RC)
            draw_ref[rows, :] = _corr_taps(_anti_win(dp_scr, r, t, pad), w_ref, K_SSD).astype(MX)
            return carry
        lax.fori_loop(0, t // RC, second, 0)
        _dw_finish(dw_scr, dw_ref, K_SSD)
        db_ref[...] = jnp.sum(db_scr[...], axis=0, keepdims=True)

    return pl.pallas_call(
        body, name="ssd_conv_bwd_bc", grid=(2,),
        out_shape=(jax.ShapeDtypeStruct((t, 2 * CW), MX), jax.ShapeDtypeStruct((K_SSD, 2 * CW), f32),
                   jax.ShapeDtypeStruct((1, 2 * CW), f32)),
        in_specs=[pl.BlockSpec((t, CW), lambda j: (0, c0 + j)), pl.BlockSpec((K_SSD, CW), lambda j: (0, w0 + j)),
                  pl.BlockSpec((1, CW), lambda j: (0, w0 + j)), pl.BlockSpec((1, t, CW), lambda j: (j, 0, 0))],
        out_specs=(pl.BlockSpec((t, CW), lambda j: (0, j)), pl.BlockSpec((K_SSD, CW), lambda j: (0, j)),
                   pl.BlockSpec((1, CW), lambda j: (0, j))),
        scratch_shapes=[pltpu.VMEM((t, CW), f32), pltpu.VMEM((8 * K_SSD, CW), f32), pltpu.VMEM((8, CW), f32)],
        compiler_params=_cp("arbitrary"),
    )(proj, conv_w, conv_b, d_bc)


def _chunk_masks():
    ii = lax.broadcasted_iota(jnp.int32, (CHUNK, CHUNK), 0)
    jj = lax.broadcasted_iota(jnp.int32, (CHUNK, CHUNK), 1)
    return ii == jj, jj <= ii, jj >= ii


def _to_row(col, eye):
    return jnp.sum(jnp.where(eye, col, 0.0), axis=0, keepdims=True)


def _to_col(row, eye):
    return jnp.sum(jnp.where(eye, row, 0.0), axis=1, keepdims=True)


def _head_decay(dt_h, a_h, eye, tril):
    a_row = _to_row(dt_h * a_h, eye)
    cs = jnp.sum(jnp.where(tril, a_row, 0.0), axis=1, keepdims=True)
    cs_row = _to_row(cs, eye)
    decay = jnp.where(tril, jnp.exp(jnp.where(tril, cs - cs_row, 0.0)), 0.0)
    total = jnp.sum(a_row, axis=1, keepdims=True)
    return cs, decay, total


SCAN_UNROLL = 4


def _unrolled_loop(n, step, init):
    unroll = min(SCAN_UNROLL, n)
    assert n % unroll == 0

    def trip(i, carry):
        for u in range(unroll):
            carry = step(unroll * i + u, carry)
        return carry
    return lax.fori_loop(0, n // unroll, trip, init)


def _lane_pick(mat, lane, which):
    return jnp.sum(jnp.where(lane == which, mat, 0.0), axis=1, keepdims=True)


def _ssd_fwd(xbc_act, proj, dt_bias_row, a_log_row, rider=None):
    t = xbc_act.shape[0]
    nc = t // CHUNK
    cb, cc, cdt = D // LANES, (D + 2 * STATE_N) // LANES, OFF_DT // LANES

    def body(x_ref, b_ref, c_ref, dt_ref, dtb_ref, alog_ref, y_ref, st_ref):
        j = pl.program_id(0)
        eye, tril, _ = _chunk_masks()
        lane = lax.broadcasted_iota(jnp.int32, (1, LANES), 1)
        first = lane < HEAD_P
        a_row = -jnp.exp(alog_ref[...])
        a_heads = [jnp.sum(jnp.where(lane == 2 * j + h, a_row, 0.0), axis=1, keepdims=True) for h in range(2)]

        def chunk(c, hprev):
            rows = pl.ds(pl.multiple_of(c * CHUNK, CHUNK), CHUNK)
            xv, bm, cm = x_ref[rows, :], b_ref[rows, :], c_ref[rows, :]
            dt = _softplus(dt_ref[rows, :] + dtb_ref[...])
            st_ref[c] = hprev
            g = _mm_nt(cm, bm)
            ch = _mm(cm, hprev)
            dts = [_lane_pick(dt, lane, 2 * j + h) for h in range(2)]
            xdt = xv * jnp.where(first, dts[0], dts[1])
            ys, hs = [], []
            for h in range(2):
                cs, decay, total = _head_decay(dts[h], a_heads[h], eye, tril)
                y = _mm(g * decay, xdt) + jnp.exp(cs) * ch
                s = _mm_tn(bm * jnp.exp(total - cs), xdt)
                ys.append(y)
                hs.append(jnp.exp(total) * hprev + s)
            y_ref[rows, :] = jnp.where(first, ys[0], ys[1])
            return jnp.where(first, hs[0], hs[1])

        _unrolled_loop(nc, chunk, jnp.zeros((STATE_N, LANES), f32))

    blk = lambda f: pl.BlockSpec((t, LANES), f)
    return _call(
        body, name="ssd_fwd", grid=(D // LANES,),
        out_shape=(jax.ShapeDtypeStruct((t, D), f32), jax.ShapeDtypeStruct((nc, STATE_N, D), f32)),
        in_specs=[blk(lambda j: (0, j)), blk(lambda j: (0, cb + j // 4)), blk(lambda j: (0, cc + j // 4)),
                  blk(lambda j: (0, cdt)), _row(LANES), _row(LANES)],
        out_specs=(blk(lambda j: (0, j)), pl.BlockSpec((nc, STATE_N, LANES), lambda j: (0, 0, j))),
        sem=("arbitrary",), args=(xbc_act, xbc_act, xbc_act, proj, dt_bias_row, a_log_row), rider=rider)


def _ssd_bwd(xbc_act, proj, dt_bias_row, a_log_row, states, d_y, rider=None):
    t = xbc_act.shape[0]
    nc = t // CHUNK
    cb, cc, cdt = D // LANES, (D + 2 * STATE_N) // LANES, OFF_DT // LANES

    def body(x_ref, b_ref, c_ref, dt_ref, dtb_ref, alog_ref, st_ref, dy_ref, dx_ref, dbc_ref, ddt_ref, da_ref):
        grp, p = pl.program_id(0), pl.program_id(1)
        j = 4 * grp + p
        eye, tril, triu = _chunk_masks()
        lane = lax.broadcasted_iota(jnp.int32, (1, LANES), 1)
        first = lane < HEAD_P
        last_row = lax.broadcasted_iota(jnp.int32, (CHUNK, 1), 0) == CHUNK - 1
        a_row = -jnp.exp(alog_ref[...])
        a_heads = [jnp.sum(jnp.where(lane == 2 * j + h, a_row, 0.0), axis=1, keepdims=True) for h in range(2)]

        @pl.when(p == 0)
        def _():
            dbc_ref[...] = jnp.zeros_like(dbc_ref)

        @pl.when(j == 0)
        def _():
            ddt_ref[...] = jnp.zeros_like(ddt_ref)
            da_ref[...] = jnp.zeros_like(da_ref)

        def chunk(i, dh):
            c = nc - 1 - i
            rows = pl.ds(pl.multiple_of(c * CHUNK, CHUNK), CHUNK)
            xv, bm, cm = x_ref[rows, :], b_ref[rows, :], c_ref[rows, :]
            dtr = dt_ref[rows, :] + dtb_ref[...]
            dt = _softplus(dtr)
            hprev = st_ref[c]
            dy = dy_ref[rows, :]
            g = _mm_nt(cm, bm)
            dts = [_lane_pick(dt, lane, 2 * j + h) for h in range(2)]
            xdt = xv * jnp.where(first, dts[0], dts[1])
            dxs, dhs = [], []
            db_sum, dc_sum = None, None
            ddt_mat = jnp.zeros((CHUNK, LANES), f32)
            da_acc = jnp.zeros((1, LANES), f32)
            for h in range(2):
                mine = first if h == 0 else jnp.logical_not(first)
                cs, decay, total = _head_decay(dts[h], a_heads[h], eye, tril)
                e_cs, e_tot = jnp.exp(cs), jnp.exp(total)
                dec_s = jnp.exp(total - cs)
                dyh = jnp.where(mine, dy, 0.0)
                xdth = jnp.where(mine, xdt, 0.0)
                dhh = jnp.where(mine, dh, 0.0)
                hph = jnp.where(mine, hprev, 0.0)
                m = g * decay
                dm = _mm_nt(dyh, xdth)
                dg = dm * decay
                w = dm * m
                bdec = bm * dec_s
                dxdt = _mm_tn(m, dyh) + _mm(bdec, dhh)
                dc_off = _mm_nt(dyh, hph) * e_cs
                db_s = _mm_nt(xdth, dhh) * dec_s
                dc_h = _mm(dg, bm) + dc_off
                db_h = _mm_tn(dg, cm) + db_s
                r_s = jnp.sum(db_s * bm, axis=1, keepdims=True)
                dtotal = jnp.sum(r_s, axis=0, keepdims=True) + e_tot * jnp.sum(
                    jnp.sum(dhh * hph, axis=1, keepdims=True), axis=0, keepdims=True)
                dcs = (jnp.sum(w, axis=1, keepdims=True) - _to_col(jnp.sum(w, axis=0, keepdims=True), eye)
                       + jnp.sum(dc_off * cm, axis=1, keepdims=True) - r_s + jnp.where(last_row, dtotal, 0.0))
                da_col = jnp.sum(jnp.where(triu, _to_row(dcs, eye), 0.0), axis=1, keepdims=True)
                ddt = da_col * a_heads[h] + jnp.sum(jnp.where(mine, dxdt * xv, 0.0), axis=1, keepdims=True)
                ddt_mat = ddt_mat + jnp.where(lane == 2 * j + h, ddt, 0.0)
                da_acc = da_acc + jnp.where(lane == 2 * j + h, jnp.sum(da_col * dts[h], axis=0, keepdims=True), 0.0)
                dxs.append(dxdt * dts[h])
                dhs.append(e_tot * dhh + _mm_tn(cm * e_cs, dyh))
                db_sum = db_h if db_sum is None else db_sum + db_h
                dc_sum = dc_h if dc_sum is None else dc_sum + dc_h
            dx_ref[rows, :] = jnp.where(first, dxs[0], dxs[1])
            dbc_ref[0, rows, :] += db_sum
            dbc_ref[1, rows, :] += dc_sum
            ddt_ref[rows, :] += ddt_mat * jax.nn.sigmoid(dtr)
            da_ref[...] += da_acc * a_row
            return jnp.where(first, dhs[0], dhs[1])

        _unrolled_loop(nc, chunk, jnp.zeros((STATE_N, LANES), f32))

    blk = lambda f: pl.BlockSpec((t, LANES), f)
    return _call(
        body, name="ssd_bwd", grid=(2, 4),
        out_shape=(jax.ShapeDtypeStruct((t, D), f32), jax.ShapeDtypeStruct((2, t, 2 * STATE_N), f32),
                   jax.ShapeDtypeStruct((t, LANES), f32), jax.ShapeDtypeStruct((1, LANES), f32)),
        in_specs=[blk(lambda g, p: (0, 4 * g + p)), blk(lambda g, p: (0, cb + g)), blk(lambda g, p: (0, cc + g)),
                  blk(lambda g, p: (0, cdt)), _row(LANES), _row(LANES),
                  pl.BlockSpec((nc, STATE_N, LANES), lambda g, p: (0, 0, 4 * g + p)), blk(lambda g, p: (0, 4 * g + p))],
        out_specs=(blk(lambda g, p: (0, 4 * g + p)), pl.BlockSpec((2, t, LANES), lambda g, p: (0, 0, g)),
                   blk(lambda g, p: (0, 0)), _row(LANES)),
        sem=("arbitrary", "arbitrary"), args=(xbc_act, xbc_act, xbc_act, proj, dt_bias_row, a_log_row, states, d_y),
        rider=rider)


def _up_bwd(d_up, w_up, x1, mod, norm2_w, dx2, mix, w_out, rider=None):
    t = x1.shape[0]

    def body(dup_ref, wu_ref, x1_ref, mod_ref, nw_ref, dx2_ref, mix_ref, wo_ref,
             dx1_ref, dmix_ref, dys_ref, du_ref, st_ref):
        @pl.when(pl.program_id(0) == 0)
        def _():
            st_ref[...] = jnp.zeros_like(st_ref)

        nt = (((1,), (1,)), ((), ()))
        dh = None
        for k in range(4):
            lo = (k % 2) * UP_SHARD
            part = lax.dot_general(dup_ref[k // 2, :, lo:lo + UP_SHARD], wu_ref[k], nt, preferred_element_type=f32)
            dh = part if dh is None else dh + part
        x1 = x1_ref[...]
        rstd = lax.rsqrt(jnp.mean(x1 * x1, axis=-1, keepdims=True) + 1e-6)
        xh = x1 * rstd
        nw = nw_ref[...]
        sc = 1.0 + mod_ref[:, 4 * D:5 * D]
        st_ref[0:1, :] += jnp.sum(dh, axis=0, keepdims=True)
        st_ref[1:2, :] += jnp.sum(dh * xh * nw, axis=0, keepdims=True)
        st_ref[2:3, :] += jnp.sum(dh * sc * xh, axis=0, keepdims=True)
        dxh = dh * sc * nw
        dx1 = dx2_ref[...] + rstd * (dxh - xh * jnp.mean(dxh * xh, axis=-1, keepdims=True))
        dx1_ref[...] = dx1
        st_ref[3:4, :] += jnp.sum(dx1 * mix_ref[...], axis=0, keepdims=True)
        dmix = (mod_ref[:, 2 * D:3 * D] * dx1).astype(MX)
        dmix_ref[...] = dmix
        dys_ref[...] = lax.dot_general(dmix, wo_ref[0:D, :], nt, preferred_element_type=f32)
        du_ref[...] = lax.dot_general(dmix, wo_ref[D:2 * D, :], nt, preferred_element_type=f32)

    blk = pl.BlockSpec((TM, D), lambda i: (i, 0))
    return _call(
        body, name="up_bwd", grid=(t // TM,),
        out_shape=(jax.ShapeDtypeStruct((t, D), f32), jax.ShapeDtypeStruct((t, D), MX),
                   jax.ShapeDtypeStruct((t, D), f32), jax.ShapeDtypeStruct((t, D), f32),
                   jax.ShapeDtypeStruct((8, D), f32)),
        in_specs=[pl.BlockSpec((2, TM, D_FF), lambda i: (0, i, 0)), _resident((4, D, UP_SHARD)), blk, _row(6 * D), _row(),
                  blk, blk, _resident((2 * D, D))],
        out_specs=(blk, blk, blk, blk, pl.BlockSpec((8, D), lambda i: (0, 0))),
        sem=("arbitrary",), args=(d_up, w_up, x1, mod, norm2_w, dx2, mix, w_out), rider=rider)


def _ln_silu_bwd(d_u, u_conv, ln_w, ln_b):
    t = d_u.shape[0]

    def body(du_ref, u_ref, w_ref, b_ref, o_ref, st_ref):
        @pl.when(pl.program_id(0) == 0)
        def _():
            st_ref[...] = jnp.zeros_like(st_ref)

        u = u_ref[...]
        mu = jnp.mean(u, axis=-1, keepdims=True)
        uc = u - mu
        rstd = lax.rsqrt(jnp.mean(uc * uc, axis=-1, keepdims=True) + 1e-5)
        n = uc * rstd
        w = w_ref[...]
        dl = du_ref[...] * _dsilu(n * w + b_ref[...])
        st_ref[0:1, :] += jnp.sum(dl * n, axis=0, keepdims=True)
        st_ref[1:2, :] += jnp.sum(dl, axis=0, keepdims=True)
        dn = dl * w
        o_ref[...] = rstd * (dn - jnp.mean(dn, axis=-1, keepdims=True) - n * jnp.mean(dn * n, axis=-1, keepdims=True))

    blk = pl.BlockSpec((TM, D), lambda i: (i, 0))
    return pl.pallas_call(
        body, name="ln_silu_bwd", grid=(t // TM,),
        out_shape=(jax.ShapeDtypeStruct((t, D), f32), jax.ShapeDtypeStruct((8, D), f32)),
        in_specs=[blk, blk, _row(), _row()], out_specs=(blk, pl.BlockSpec((8, D), lambda i: (0, 0))),
        compiler_params=_cp("arbitrary"),
    )(d_u, u_conv, ln_w, ln_b)


def _ssd_gate_norm_bwd(d_out, y_scan, xbc_act, proj, d_skip_row, ssd_norm_w):
    t = d_out.shape[0]

    def body(do_ref, y_ref, xs_ref, z_ref, dsk_ref, nw_ref, dy_ref, dz_ref, st_ref):
        @pl.when(pl.program_id(0) == 0)
        def _():
            st_ref[...] = jnp.zeros_like(st_ref)

        xs = xs_ref[...]
        y = y_ref[...] + xs * dsk_ref[...]
        z = z_ref[...]
        s = _silu(z)
        yz = y * s
        rstd = lax.rsqrt(jnp.mean(yz * yz, axis=-1, keepdims=True) + 1e-6)
        n = yz * rstd
        do = do_ref[...]
        st_ref[0:1, :] += jnp.sum(do * n, axis=0, keepdims=True)
        dn = do * nw_ref[...]
        dyz = rstd * (dn - n * jnp.mean(dn * n, axis=-1, keepdims=True))
        dy = dyz * s
        dy_ref[...] = dy
        dz_ref[...] = (dyz * y * _dsilu(z)).astype(MX)
        st_ref[1:2, :] += jnp.sum(dy * xs, axis=0, keepdims=True)

    blk = pl.BlockSpec((TM, D), lambda i: (i, 0))
    return pl.pallas_call(
        body, name="ssd_gate_norm_bwd", grid=(t // TM,),
        out_shape=(jax.ShapeDtypeStruct((t, D), f32), jax.ShapeDtypeStruct((t, D), MX), jax.ShapeDtypeStruct((8, D), f32)),
        in_specs=[blk, blk, blk, blk, _row(), _row()], out_specs=(blk, blk, pl.BlockSpec((8, D), lambda i: (0, 0))),
        compiler_params=_cp("arbitrary"),
    )(d_out, y_scan, xbc_act, proj, d_skip_row, ssd_norm_w)


def _inproj_bwd(d_z, d_xraw, d_bcraw, d_conf, d_dt, w_pack, x, mod, norm1_w, dx1, rider=None):
    t = x.shape[0]

    def body(dz_ref, dx_ref, dbc_ref, dcf_ref, ddt_ref, w_ref, x_ref, mod_ref, nw_ref, dx1_ref, gx_ref, st_ref):
        @pl.when(pl.program_id(0) == 0)
        def _():
            st_ref[...] = jnp.zeros_like(st_ref)

        nt = (((1,), (1,)), ((), ()))
        dot = lambda a, lo, hi: lax.dot_general(a, w_ref[:, lo:hi], nt, preferred_element_type=f32)
        dh = dot(dz_ref[...], OFF_Z, OFF_Z + D)
        dh = dh + dot(dx_ref[...], OFF_XBC, OFF_XBC + D)
        dh = dh + dot(dbc_ref[...], OFF_XBC + D, OFF_XBC + D_XBC)
        dh = dh + dot(dcf_ref[0], OFF_CA, OFF_CA + D)
        dh = dh + dot(dcf_ref[1], OFF_CG, OFF_CG + D)
        dh = dh + dot(ddt_ref[...].astype(MX), OFF_DT, OFF_DT + LANES)
        st_ref[3:4, 0:LANES] += jnp.sum(ddt_ref[...], axis=0, keepdims=True)
        xv = x_ref[...]
        rstd = lax.rsqrt(jnp.mean(xv * xv, axis=-1, keepdims=True) + 1e-6)
        xh = xv * rstd
        nw = nw_ref[...]
        sc = 1.0 + mod_ref[:, D:2 * D]
        st_ref[0:1, :] += jnp.sum(dh, axis=0, keepdims=True)
        st_ref[1:2, :] += jnp.sum(dh * xh * nw, axis=0, keepdims=True)
        st_ref[2:3, :] += jnp.sum(dh * sc * xh, axis=0, keepdims=True)
        dxh = dh * sc * nw
        gx_ref[...] = dx1_ref[...] + rstd * (dxh - xh * jnp.mean(dxh * xh, axis=-1, keepdims=True))

    blk = pl.BlockSpec((TM, D), lambda i: (i, 0))
    return _call(
        body, name="inproj_bwd", grid=(t // TM,),
        out_shape=(jax.ShapeDtypeStruct((t, D), f32), jax.ShapeDtypeStruct((8, D), f32)),
        in_specs=[blk, blk, pl.BlockSpec((TM, 2 * CW), lambda i: (i, 0)), pl.BlockSpec((2, TM, D), lambda i: (0, i, 0)),
                  pl.BlockSpec((TM, LANES), lambda i: (i, 0)), _resident((D, W_PACK)), blk, _row(6 * D), _row(), blk],
        out_specs=(blk, pl.BlockSpec((8, D), lambda i: (0, 0))),
        sem=("arbitrary",), args=(d_z, d_xraw, d_bcraw, d_conf, d_dt, w_pack, x, mod, norm1_w, dx1), rider=rider)


def _wgrad(a, d, name, bn=256):
    t, k = a.shape
    n = d.shape[1]
    out_dtype = MX

    def body(a_ref, d_ref, o_ref):
        o_ref[...] = lax.dot_general(a_ref[...], d_ref[...].astype(MX), (((0,), (0,)), ((), ())),
                                     preferred_element_type=f32).astype(out_dtype)

    return pl.pallas_call(
        body, name=name, grid=(n // bn,), out_shape=jax.ShapeDtypeStruct((k, n), out_dtype),
        in_specs=[_resident((t, k)), pl.BlockSpec((t, bn), lambda j: (0, j))],
        out_specs=pl.BlockSpec((k, bn), lambda j: (0, j)), compiler_params=_cp("arbitrary"),
    )(a, d)


def _wgrad_stacked(a, d, name, bn):
    out_dtype = MX
    t, k = a.shape
    s, _, n = d.shape
    nb = n // bn

    def body(a_ref, d_ref, o_ref):
        o_ref[0] = lax.dot_general(a_ref[...], d_ref[0], (((0,), (0,)), ((), ())),
                                   preferred_element_type=f32).astype(out_dtype)

    return pl.pallas_call(
        body, name=name, grid=(s, nb), out_shape=jax.ShapeDtypeStruct((s * nb, k, bn), out_dtype),
        in_specs=[_resident((t, k)), pl.BlockSpec((1, t, bn), lambda i, j: (i, 0, j))],
        out_specs=pl.BlockSpec((1, k, bn), lambda i, j: (i * nb + j, 0, 0)), compiler_params=_cp("arbitrary", "arbitrary"),
    )(a, d)


def _pad_row(v, width=LANES):
    return jnp.pad(v.reshape(1, -1), ((0, 0), (0, width - v.size)))


def _quarters(a):
    return a.reshape(4, 2, a.shape[0] // 8, a.shape[1])


def _local_step(x, mod, target, w_pack, late, small, reducer=None):
    dtb_row, alog_row = _pad_row(small["dt_bias"]), _pad_row(small["a_log"])
    dskip_row = jnp.repeat(small["d_skip"].reshape(-1), HEAD_P).reshape(1, D)

    red = reducer

    def hosted(host, args, swap=None, scatter=None, gather=None):
        if red is None:
            return host(*args)[0]
        riders = ([red.scatter(scatter)] if scatter else []) + ([red.swap(*swap)] if swap else [])
        riders += [_GatherRider([gather])] if gather is not None else []
        both = _Riders(riders)
        outs, extra = host(*args, rider=both)
        extra = both.split(extra)
        if scatter:
            red.scattered(scatter, extra.pop(0))
        if swap:
            red.swapped(swap[0], extra.pop(0))
        return (outs, extra[0][0]) if gather is not None else outs

    proj, h = _ln_inproj(x, mod, small["norm1_w"], w_pack)
    xbc_act = _ssd_conv_fwd(proj, small["ssd_conv_w"], small["ssd_conv_b"])
    w_out, w_up, w_down = late
    if red is None:
        y_scan, states = hosted(_ssd_fwd, (xbc_act, proj, dtb_row, alog_row))
        u_conv, = hosted(_glu_conv_fwd, (proj, small["conf_conv_w"], small["conf_conv_b"]))
    else:
        (y_scan, states), w_up = hosted(_ssd_fwd, (xbc_act, proj, dtb_row, alog_row), gather=w_up)
        (u_conv,), w_out = hosted(_glu_conv_fwd, (proj, small["conf_conv_w"], small["conf_conv_b"]), gather=w_out)
        w_up, w_out = w_up.reshape(4, D, UP_SHARD), w_out.reshape(2 * D, D)
    y_ssd = _ssd_gate_norm(y_scan, xbc_act, proj, dskip_row, small["ssd_norm_w"])
    u = _ln_silu(u_conv, small["conf_ln_w"], small["conf_ln_b"])
    mix, x1, h2, up = _outproj_ln2_up(y_ssd, u, w_out, x, mod, small["norm2_w"], w_up)
    if red is None:
        act, = hosted(_ffn_conv_fwd, (up, small["ffn_conv_w"], small["ffn_conv_b"]))
    else:
        (act,), w_down = hosted(_ffn_conv_fwd, (up, small["ffn_conv_w"], small["ffn_conv_b"]), gather=w_down)
        w_down = w_down.reshape(D_FF, D)
    dx2, d_ffn, d_act, st_down = _down_loss(act, w_down, x1, mod, small["final_norm_w"], target)

    g_down = _quarters(_wgrad(act, d_ffn, "wgrad_down"))
    d_up, dw_ffn, db_ffn = hosted(_ffn_conv_bwd, (up, small["ffn_conv_w"], small["ffn_conv_b"], d_act), swap=("w_down", g_down))
    g_up = _wgrad_stacked(h2, d_up, "wgrad_up", D_FF // 2).reshape(4, 2, D // 2, UP_SHARD)
    dx1, d_mix, d_yssd, d_u, st_up = hosted(_up_bwd, (d_up, w_up, x1, mod, small["norm2_w"], dx2, mix, w_out),
                                            scatter="w_down", swap=("w_up", g_up))
    g_out = _quarters(jnp.concatenate([_wgrad(y_ssd, d_mix, "wgrad_out_y"), _wgrad(u, d_mix, "wgrad_out_u")], axis=0))
    d_uconv, st_ln = _ln_silu_bwd(d_u, u_conv, small["conf_ln_w"], small["conf_ln_b"])
    d_conf, dw_conf, db_conf = hosted(_glu_conv_bwd, (proj, small["conf_conv_w"], d_uconv), scatter="w_up",
                                      swap=("w_out", g_out))
    d_y, d_z, st_gn = _ssd_gate_norm_bwd(d_yssd, y_scan, xbc_act, proj, dskip_row, small["ssd_norm_w"])
    d_xs, d_bc, d_dt, d_alog = hosted(_ssd_bwd, (xbc_act, proj, dtb_row, alog_row, states, d_y), scatter="w_out")
    d_xraw, dw_sx, db_sx = _ssd_conv_bwd_x(proj, small["ssd_conv_w"], small["ssd_conv_b"], d_xs, d_y, dskip_row)
    d_bcraw, dw_sbc, db_sbc = _ssd_conv_bwd_bc(proj, small["ssd_conv_w"], small["ssd_conv_b"], d_bc)
    g_in = _unpack_g_in(dict(
        z=_wgrad(h, d_z, "wgrad_in_z"), x=_wgrad(h, d_xraw, "wgrad_in_x"), bc=_wgrad(h, d_bcraw, "wgrad_in_bc"),
        conf=_wgrad_stacked(h, d_conf, "wgrad_in_conf", D), dt=_wgrad(h, d_dt, "wgrad_in_dt", bn=LANES)))
    g_in = g_in.reshape(4, 2, D // 2, W_IN_SHARD_PAD)
    if red is not None:
        red.swapped("w_in", _ride_alone(red.swap("w_in", g_in), "swap_w_in"))
    grad_x, st_in = hosted(_inproj_bwd, (d_z, d_xraw, d_bcraw, d_conf, d_dt, w_pack, x, mod, small["norm1_w"], dx1),
                           scatter="w_in")

    gsmall = _pack_small_grads(st_in, st_up, st_down, st_ln, st_gn, d_alog, dw_sx, dw_sbc, db_sx, db_sbc, dw_conf, db_conf,
                               dw_ffn, db_ffn)
    gbig = None if reducer is not None else dict(w_in=g_in, w_out=g_out, w_up=g_up, w_down=g_down)
    return st_down[2, 0], grad_x, gbig, gsmall


VECTORS = ("ada_b", "norm1_w", "ssd_conv_b", "dt_bias", "a_log", "d_skip", "ssd_norm_w", "conf_conv_b", "conf_ln_w",
           "conf_ln_b", "norm2_w", "ffn_conv_b", "final_norm_w")
VECTOR_SIZES = (6 * D, D, D_XBC, HEADS, HEADS, HEADS, D, D, D, D, D, 2 * D_FF, D)
CONVS = {"ssd_conv_w": (K_SSD, D_XBC), "conf_conv_w": (K_CONF, D), "ffn_conv_w": (K_FFN, 2 * D_FF)}


def _pack_rows(items):
    n = -(-sum(w for _, w in items) // (8 * LANES)) * LANES
    while True:
        fill, place = [0] * 8, {}
        for key, w in sorted(items, key=lambda kv: -kv[1]):
            rows = [r for r in range(8) if fill[r] + w <= n]
            if not rows:
                break
            place[key] = (rows[0], fill[rows[0]])
            fill[rows[0]] += w
        if len(place) == len(items):
            return n, place
        n += LANES


FRONT_N, FRONT = _pack_rows([("c", D)] + [((nm, j), cols // 4) for nm, (taps, cols) in CONVS.items() for j in range(taps)])
BACK_N, BACK = _pack_rows([(nm, -(-sz // LANES) * LANES) for nm, sz in zip(VECTORS, VECTOR_SIZES)]
                          + [((nm, j), cols) for nm, (taps, cols) in CONVS.items() for j in range(taps)])
_VM = pltpu.CompilerParams(vmem_limit_bytes=VMEM_LIMIT)


def _pack_front(c, shards):
    def body(c_ref, *refs):
        o_ref = refs[-1]
        o_ref[...] = jnp.zeros_like(o_ref)
        r, o = FRONT["c"]
        o_ref[r:r + 1, o:o + D] = c_ref[...]
        for ref, (nm, (taps, cols)) in zip(refs, CONVS.items()):
            for j in range(taps):
                r, o = FRONT[(nm, j)]
                o_ref[r:r + 1, o:o + cols // 4] = ref[0, j:j + 1, :]

    return pl.pallas_call(body, name="pack_front", out_shape=jax.ShapeDtypeStruct((8, FRONT_N), f32),
                          compiler_params=_VM)(c, *shards)


def _unpack_front(got):
    def body(g_ref, c_ref, *outs):
        r, o = FRONT["c"]
        for d in range(8):
            c_ref[d:d + 1, :] = g_ref[8 * d + r:8 * d + r + 1, o:o + D]
        for ref, (nm, (taps, cols)) in zip(outs, CONVS.items()):
            cw = cols // 4
            for j in range(taps):
                r, o = FRONT[(nm, j)]
                for k in range(4):
                    ref[j:j + 1, k * cw:(k + 1) * cw] = g_ref[16 * k + r:16 * k + r + 1, o:o + cw]

    return pl.pallas_call(
        body, name="unpack_front", compiler_params=_VM,
        out_shape=(jax.ShapeDtypeStruct((8, D), f32),) + tuple(jax.ShapeDtypeStruct(tc, f32) for tc in CONVS.values()),
    )(got)


def _pack_small_grads(st_in, st_up, st_down, st_ln, st_gn, d_alog, dw_sx, dw_sbc, db_sx, db_sbc, dw_conf, db_conf, dw_ffn,
                      db_ffn):
    def body(in_ref, up_ref, dn_ref, ln_ref, gn_ref, al_ref, wx_ref, wbc_ref, bx_ref, bbc_ref, wc_ref, bc_ref, wf_ref, bf_ref,
             o_ref):
        def put(key, val, shift=0):
            r, o = BACK[key]
            o_ref[r:r + 1, o + shift:o + shift + val.shape[1]] = val

        o_ref[...] = jnp.zeros_like(o_ref)
        for i, piece in enumerate((in_ref[0:1, :], in_ref[1:2, :], up_ref[3:4, :], up_ref[0:1, :], up_ref[1:2, :],
                                   dn_ref[1:2, :])):
            put("ada_b", piece, i * D)
        put("norm1_w", in_ref[2:3, :])
        put("ssd_conv_b", bx_ref[...])
        put("ssd_conv_b", bbc_ref[...], D)
        put("dt_bias", in_ref[3:4, 0:LANES])
        put("a_log", al_ref[...])
        lane = lax.broadcasted_iota(jnp.int32, (1, LANES), 1)
        col = lax.broadcasted_iota(jnp.int32, (1, D), 1)
        per_col = gn_ref[1:2, :]
        d_skip = jnp.zeros((1, LANES), f32)
        for h in range(HEADS):
            in_head = jnp.logical_and(col >= h * HEAD_P, col < (h + 1) * HEAD_P)
            s = jnp.sum(jnp.where(in_head, per_col, 0.0), axis=1, keepdims=True)
            d_skip = d_skip + jnp.where(lane == h, s, 0.0)
        put("d_skip", d_skip)
        put("ssd_norm_w", gn_ref[0:1, :])
        put("conf_conv_b", bc_ref[...])
        put("conf_ln_w", ln_ref[0:1, :])
        put("conf_ln_b", ln_ref[1:2, :])
        put("norm2_w", up_ref[2:3, :])
        put("ffn_conv_b", bf_ref[0])
        put("ffn_conv_b", bf_ref[1], D_FF)
        put("final_norm_w", dn_ref[0:1, :])
        for j in range(K_SSD):
            put(("ssd_conv_w", j), wx_ref[j:j + 1, :])
            put(("ssd_conv_w", j), wbc_ref[j:j + 1, :], D)
        for j in range(K_CONF):
            put(("conf_conv_w", j), wc_ref[j:j + 1, :])
        for j in range(K_FFN):
            put(("ffn_conv_w", j), wf_ref[0, j:j + 1, :])
            put(("ffn_conv_w", j), wf_ref[1, j:j + 1, :], D_FF)

    return pl.pallas_call(body, name="pack_small_grads", out_shape=jax.ShapeDtypeStruct((8, BACK_N), f32), compiler_params=_VM)(
        st_in, st_up, st_down, st_ln, st_gn, d_alog, dw_sx, dw_sbc, db_sx, db_sbc, dw_conf, db_conf, dw_ffn, db_ffn)


def _small_adamw(got, chip, w, m, v):
    names = VECTORS + tuple(CONVS)
    n_par = len(names)

    def body(chip_ref, g_ref, *refs):
        ins, outs = refs[:3 * n_par], refs[3 * n_par:]
        dm_ref, outs = outs[0], outs[1:]
        chip_id = chip_ref[0]

        def summed(key, width):
            r, o = BACK[key]
            s = g_ref[r:r + 1, o:o + width]
            for d in range(1, 8):
                s = s + g_ref[8 * d + r:8 * d + r + 1, o:o + width]
            return s

        def mine(full, cw):
            out = full[:, 0:cw]
            for k in range(1, 4):
                out = jnp.where(chip_id == k, full[:, k * cw:(k + 1) * cw], out)
            return out

        r, o = BACK["ada_b"]
        for d in range(8):
            dm_ref[d:d + 1, :] = mine(g_ref[8 * d + r:8 * d + r + 1, o:o + 6 * D], 6 * D // 4)
        for i, (nm, size) in enumerate(zip(VECTORS, VECTOR_SIZES)):
            g = summed(nm, -(-size // LANES) * LANES)[:, 0:size]
            res = _adam_math(ins[3 * i][...], g, ins[3 * i + 1][...], ins[3 * i + 2][...])
            for ref, val in zip(outs[4 * i:4 * i + 4], (g,) + res):
                ref[...] = val
        for i, (nm, (taps, cols)) in enumerate(CONVS.items(), start=len(VECTORS)):
            for j in range(taps):
                g = mine(summed((nm, j), cols), cols // 4)
                res = _adam_math(ins[3 * i][0, j:j + 1, :], g, ins[3 * i + 1][0, j:j + 1, :], ins[3 * i + 2][0, j:j + 1, :])
                for ref, val in zip(outs[4 * i:4 * i + 4], (g,) + res):
                    ref[0, j:j + 1, :] = val

    params = [a[nm] for nm in names for a in (w, m, v)]
    whole = lambda s: pl.BlockSpec(s, lambda i, chip, nd=len(s): (0,) * nd)
    out_shape = [jax.ShapeDtypeStruct((8, 6 * D // 4), f32)] + [jax.ShapeDtypeStruct(w[nm].shape, f32) for nm in names for _ in range(4)]
    outs = pl.pallas_call(
        body, name="small_adamw", out_shape=tuple(out_shape), compiler_params=_VM,
        grid_spec=pltpu.PrefetchScalarGridSpec(
            num_scalar_prefetch=1, grid=(1,), in_specs=[whole(got.shape)] + [whole(p.shape) for p in params],
            out_specs=tuple(whole(s.shape) for s in out_shape)),
    )(_scalar(chip), got, *params)
    return outs[0], {nm: outs[1 + 4 * i:5 + 4 * i] for i, nm in enumerate(names)}


W_IN_COLS = 4624
W_IN_SHARD = W_IN_COLS // 4
W_IN_SHARD_PAD = 1280
_SEGMENTS = ((0, 1024, OFF_Z), (1024, 2560, OFF_XBC), (2560, 2576, OFF_DT), (2576, 3600, OFF_CA), (3600, 4624, OFF_CG))


def _in_pieces(bounds=()):
    out = []
    for k in range(4):
        s0, s1 = k * W_IN_SHARD, (k + 1) * W_IN_SHARD
        for lo, hi, off in _SEGMENTS:
            a, b = max(lo, s0), min(hi, s1)
            while a < b:
                p = off + a - lo
                e = min([b - a] + [c - p for c in bounds if c > p])
                out.append((k, a - s0, p, e))
                a += e
    return out


def _pack_w_in(shards):
    pieces = _in_pieces()

    def body(s_ref, o_ref):
        o_ref[:, OFF_DT:W_PACK] = jnp.zeros((TM, W_PACK - OFF_DT), MX)
        for k, c, p, n in pieces:
            o_ref[:, p:p + n] = s_ref[k, :, c:c + n]

    return pl.pallas_call(
        body, name="pack_w_in", grid=(D // TM,), out_shape=jax.ShapeDtypeStruct((D, W_PACK), MX),
        in_specs=[pl.BlockSpec((4, TM, W_IN_SHARD_PAD), lambda i: (0, i, 0))],
        out_specs=pl.BlockSpec((TM, W_PACK), lambda i: (i, 0)), compiler_params=_cp("arbitrary"),
    )(shards)


def _unpack_g_in(g):
    srcs = ((OFF_Z, D), (OFF_XBC, D), (OFF_XBC + D, 2 * CW), (OFF_CA, D), (OFF_CG, D), (OFF_DT, LANES))
    pieces = _in_pieces(tuple(o for o, _ in srcs) + tuple(o + n for o, n in srcs))

    def body(z_ref, x_ref, bc_ref, cf_ref, dt_ref, o_ref):
        read = (lambda lo, hi: z_ref[:, lo:hi], lambda lo, hi: x_ref[:, lo:hi], lambda lo, hi: bc_ref[:, lo:hi],
                lambda lo, hi: cf_ref[0, :, lo:hi], lambda lo, hi: cf_ref[1, :, lo:hi], lambda lo, hi: dt_ref[:, lo:hi])
        o_ref[:, :, W_IN_SHARD - 4:W_IN_SHARD_PAD] = jnp.zeros((4, TM, W_IN_SHARD_PAD - W_IN_SHARD + 4), MX)
        for k, c, p, n in pieces:
            i = [q for q, (o, w) in enumerate(srcs) if o <= p < o + w][0]
            o_ref[k, :, c:c + n] = read[i](p - srcs[i][0], p - srcs[i][0] + n)

    blk = lambda w: pl.BlockSpec((TM, w), lambda i: (i, 0))
    return pl.pallas_call(
        body, name="unpack_g_in", grid=(D // TM,), out_shape=jax.ShapeDtypeStruct((4, D, W_IN_SHARD_PAD), MX),
        in_specs=[blk(D), blk(D), blk(2 * CW), pl.BlockSpec((2, TM, D), lambda i: (0, i, 0)), blk(LANES)],
        out_specs=pl.BlockSpec((4, TM, W_IN_SHARD_PAD), lambda i: (0, i, 0)), compiler_params=_cp("arbitrary"),
    )(g["z"], g["x"], g["bc"], g["conf"], g["dt"])


def _scalar(v):
    return jnp.reshape(v, (1,)).astype(jnp.int32)


def _cast_into_slot(w, width, chip):
    r, c = w.shape
    h = r // 2
    tm = _row_tile(h)
    nj = h // tm

    def body(chip_ref, w_ref, o_ref):
        v = w_ref[...].astype(MX)
        o_ref[0, 0] = v if width == c else jnp.concatenate([v, jnp.zeros((tm, width - c), MX)], axis=1)

    return pl.pallas_call(
        body, name=f"cast_into_slot_{r}x{c}", out_shape=jax.ShapeDtypeStruct((4, 2, h, width), MX),
        grid_spec=pltpu.PrefetchScalarGridSpec(
            num_scalar_prefetch=1, grid=(2, nj),
            in_specs=[pl.BlockSpec((tm, c), lambda i, j, chip: (i * nj + j, 0))],
            out_specs=pl.BlockSpec((1, 1, tm, width), lambda i, j, chip: (chip[0], i, j, 0))),
        compiler_params=_cp("arbitrary", "arbitrary"),
    )(_scalar(chip), w)


ANY = pl.BlockSpec(memory_space=pl.ANY)


def _place():
    x, y, c = lax.axis_index("x"), lax.axis_index("y"), lax.axis_index("c")
    return x, y, c, [(1 - x, y), (x, 1 - y), (1 - x, 1 - y)]


def _gather_rows(block):
    m_per, n = block.shape

    def body(x_ref, out_ref, send_sems, recv_sems, local_sem):
        x, y, c, chips = _place()
        me, sibling = (x, y, c), (x, y, 1 - c)

        def rows(px, py, pc):
            return out_ref.at[pl.ds((4 * px + 2 * py + pc) * m_per, m_per), :]

        def copy(k, blk, to, src=None):
            return pltpu.make_async_remote_copy(
                src_ref=rows(*blk) if src is None else src, dst_ref=rows(*blk), send_sem=send_sems.at[k],
                recv_sem=recv_sems.at[k], device_id=to, device_id_type=MESH)

        mine = pltpu.make_async_copy(x_ref, rows(*me), local_sem)
        mine.start()
        first = [copy(0, me, sibling, src=x_ref)]
        first += [copy(1 + j, me, (*chip, c), src=x_ref) for j, chip in enumerate(chips)]
        for cp in first:
            cp.start()
        passed = [copy(4 + j, (*chip, c), sibling) for j, chip in enumerate(chips)]
        for j, chip in enumerate(chips):
            copy(1 + j, (*chip, c), me).wait_recv()
            passed[j].start()
        copy(0, sibling, me).wait_recv()
        for j, chip in enumerate(chips):
            copy(4 + j, (*chip, 1 - c), me).wait_recv()
        for cp in first + passed:
            cp.wait_send()
        mine.wait()

    return pl.pallas_call(
        body, name=f"gather_rows_{m_per}x{n}", out_shape=jax.ShapeDtypeStruct((8 * m_per, n), block.dtype),
        in_specs=[pl.BlockSpec(memory_space=pltpu.VMEM)], out_specs=pl.BlockSpec(memory_space=pltpu.VMEM),
        scratch_shapes=[pltpu.SemaphoreType.DMA((7,)), pltpu.SemaphoreType.DMA((7,)), pltpu.SemaphoreType.DMA],
        compiler_params=pltpu.CompilerParams(vmem_limit_bytes=VMEM_LIMIT),
    )(block)


class _GatherRider:
    def __init__(self, slots):
        n = len(slots)
        self.n = n
        self.inputs = list(slots)
        self.out_shape = [jax.ShapeDtypeStruct(s.shape, s.dtype) for s in slots]
        self.scratch = [pltpu.SemaphoreType.DMA((n, 6)), pltpu.SemaphoreType.DMA((n, 6))]
        self.aliases = {a: a for a in range(n)}

    @staticmethod
    def _copy(outs, sems, a, j, k, half, to):
        dst = outs[a].at[k, half]
        return pltpu.make_async_remote_copy(src_ref=dst, dst_ref=dst, send_sem=sems[0].at[a, j], recv_sem=sems[1].at[a, j],
                                            device_id=to, device_id_type=MESH)

    def _first(self, outs, sems):
        x, y, c, chips = _place()
        return [self._copy(outs, sems, a, j, 2 * x + y, c, (*chip, c)) for a in range(self.n) for j, chip in enumerate(chips)]

    def start(self, ins, outs, sems):
        for cp in self._first(outs, sems):
            cp.start()

    def finish(self, ins, outs, sems):
        x, y, c, chips = _place()
        passed = []
        for a in range(self.n):
            for j, (px, py) in enumerate(chips):
                self._copy(outs, sems, a, j, 2 * px + py, c, (x, y, c)).wait_recv()
                fwd = self._copy(outs, sems, a, 3 + j, 2 * px + py, c, (x, y, 1 - c))
                fwd.start()
                passed.append(fwd)
        for a in range(self.n):
            for j, (px, py) in enumerate(chips):
                self._copy(outs, sems, a, 3 + j, 2 * px + py, 1 - c, (x, y, c)).wait_recv()
        for cp in self._first(outs, sems) + passed:
            cp.wait_send()


class _ScatterRider:
    def __init__(self, parts):
        n = len(parts)
        self.n = n
        self.inputs = list(parts)
        self.out_shape = [jax.ShapeDtypeStruct((3,) + p.shape[1:], p.dtype) for p in parts]
        self.scratch = [pltpu.SemaphoreType.DMA((n, 3)), pltpu.SemaphoreType.DMA((n, 3))]
        self.aliases = {}

    def _copies(self, ins, outs, sems):
        x, y, c, chips = _place()
        return [pltpu.make_async_remote_copy(
            src_ref=ins[a].at[2 * px + py], dst_ref=outs[a].at[j], send_sem=sems[0].at[a, j], recv_sem=sems[1].at[a, j],
            device_id=(px, py, c), device_id_type=MESH) for a in range(self.n) for j, (px, py) in enumerate(chips)]

    def start(self, ins, outs, sems):
        for cp in self._copies(ins, outs, sems):
            cp.start()

    def finish(self, ins, outs, sems):
        for cp in self._copies(ins, outs, sems):
            cp.wait()


def _ride_alone(rider, name):
    n = len(rider.inputs)

    def body(*refs):
        ins, outs, sems = refs[:n], refs[n:n + len(rider.out_shape)], refs[n + len(rider.out_shape):]
        rider.start(ins, outs, sems)
        rider.finish(ins, outs, sems)

    return pl.pallas_call(
        body, name=name, out_shape=tuple(rider.out_shape), in_specs=[ANY] * n, out_specs=tuple([ANY] * len(rider.out_shape)),
        input_output_aliases=dict(rider.aliases), scratch_shapes=list(rider.scratch),
    )(*rider.inputs)


class _SwapRider:
    def __init__(self, grads):
        n = len(grads)
        self.n = n
        self.inputs = list(grads)
        self.out_shape = [jax.ShapeDtypeStruct((4,) + g.shape[2:], g.dtype) for g in grads]
        self.scratch = [pltpu.SemaphoreType.DMA((n, 4)), pltpu.SemaphoreType.DMA((n, 4))]
        self.aliases = {}

    def _copies(self, ins, outs, sems):
        x, y, c, _ = _place()
        return [pltpu.make_async_remote_copy(
            src_ref=ins[a].at[k, 1 - c], dst_ref=outs[a].at[k], send_sem=sems[0].at[a, k], recv_sem=sems[1].at[a, k],
            device_id=(x, y, 1 - c), device_id_type=MESH) for a in range(self.n) for k in range(4)]

    def start(self, ins, outs, sems):
        for cp in self._copies(ins, outs, sems):
            cp.start()

    def finish(self, ins, outs, sems):
        for cp in self._copies(ins, outs, sems):
            cp.wait()


class _Riders:
    def __init__(self, riders):
        self.riders = list(riders)
        self.inputs = [a for r in riders for a in r.inputs]
        self.out_shape = [s for r in riders for s in r.out_shape]
        self.scratch = [s for r in riders for s in r.scratch]
        self.aliases = {}
        i = o = 0
        for r in riders:
            self.aliases.update({i + a: o + b for a, b in r.aliases.items()})
            i, o = i + len(r.inputs), o + len(r.out_shape)

    def _each(self, ins, outs, sems):
        i = o = s = 0
        for r in self.riders:
            yield r, ins[i:i + len(r.inputs)], outs[o:o + len(r.out_shape)], sems[s:s + len(r.scratch)]
            i, o, s = i + len(r.inputs), o + len(r.out_shape), s + len(r.scratch)

    def start(self, ins, outs, sems):
        for r, a, b, c in self._each(ins, outs, sems):
            r.start(a, b, c)

    def finish(self, ins, outs, sems):
        for r, a, b, c in self._each(ins, outs, sems):
            r.finish(a, b, c)

    def split(self, outs):
        res, o = [], 0
        for r in self.riders:
            res.append(outs[o:o + len(r.out_shape)])
            o += len(r.out_shape)
        return res


class _Reducer:
    def __init__(self, chip, core):
        self.chip, self.core, self.grads, self.parts, self.sums = chip, core, {}, {}, {}

    def swap(self, name, grad):
        self.grads[name] = grad
        return _SwapRider([grad])

    def swapped(self, name, got):
        self.parts[name] = _add_pair(self.grads[name], got[0], self.core, name)

    def scatter(self, name):
        return _ScatterRider([self.parts[name]])

    def scattered(self, name, others):
        self.sums[name] = _add_chips(self.parts[name], others[0], self.chip, name)


def _swap_sums(halves):
    n = len(halves)

    def body(*refs):
        ins, outs = refs[:n], refs[n:2 * n]
        send_sems, recv_sems = refs[2 * n:]
        x, y, c, _ = _place()
        sent = [pltpu.make_async_remote_copy(
            src_ref=ins[a], dst_ref=outs[a], send_sem=send_sems.at[a], recv_sem=recv_sems.at[a],
            device_id=(x, y, 1 - c), device_id_type=MESH) for a in range(n)]
        for cp in sent:
            cp.start()
        for cp in sent:
            cp.wait()

    return pl.pallas_call(
        body, name="swap_sums", out_shape=tuple(jax.ShapeDtypeStruct(s.shape, s.dtype) for s in halves),
        in_specs=[ANY] * n, out_specs=tuple([ANY] * n),
        scratch_shapes=[pltpu.SemaphoreType.DMA((n,)), pltpu.SemaphoreType.DMA((n,))],
    )(*halves)


def _row_tile(r):
    for tm in (TM, 176, 128, 64, 32, 16, 8):
        if r % tm == 0:
            return tm
    return r


def _add_pair(mine, got, core, name):
    k, _, h, c = mine.shape
    tm = _row_tile(h)

    def body(core_ref, a_ref, b_ref, o_ref):
        o_ref[0] = (a_ref[0, 0].astype(f32) + b_ref[0].astype(f32)).astype(MX)

    blk = pl.BlockSpec((1, tm, c), lambda i, j, core: (i, j, 0))
    return pl.pallas_call(
        body, name="add_pair_" + name, out_shape=jax.ShapeDtypeStruct((k, h, c), MX),
        grid_spec=pltpu.PrefetchScalarGridSpec(
            num_scalar_prefetch=1, grid=(k, h // tm),
            in_specs=[pl.BlockSpec((1, 1, tm, c), lambda i, j, core: (i, core[0], j, 0)), blk], out_specs=blk),
        compiler_params=_cp("arbitrary", "arbitrary"),
    )(_scalar(core), mine, got)


def _add_chips(parts, others, chip, name):
    _, h, c = parts.shape
    tm = _row_tile(h)

    def body(chip_ref, a_ref, b_ref, o_ref):
        s = a_ref[0].astype(f32) + b_ref[0].astype(f32)
        o_ref[...] = (s + b_ref[1].astype(f32)) + b_ref[2].astype(f32)

    return pl.pallas_call(
        body, name="add_chips_" + name, out_shape=jax.ShapeDtypeStruct((h, c), f32),
        grid_spec=pltpu.PrefetchScalarGridSpec(
            num_scalar_prefetch=1, grid=(h // tm,),
            in_specs=[pl.BlockSpec((1, tm, c), lambda i, chip: (chip[0], i, 0)),
                      pl.BlockSpec((3, tm, c), lambda i, chip: (0, i, 0))],
            out_specs=pl.BlockSpec((tm, c), lambda i, chip: (i, 0))),
        compiler_params=_cp("arbitrary"),
    )(_scalar(chip), parts, others)


def _adam_math(w, g, m, v):
    m = ADAM_B1 * m + (1.0 - ADAM_B1) * g
    v = ADAM_B2 * v + (1.0 - ADAM_B2) * (g * g)
    m_hat = m / (1.0 - ADAM_B1 ** ADAM_STEP)
    v_hat = v / (1.0 - ADAM_B2 ** ADAM_STEP)
    return -ADAM_LR * (m_hat / (jnp.sqrt(v_hat) + ADAM_EPS) + ADAM_WD * w), m, v


def _adamw_halves(w, mine, other, m, v, core, name):
    r, c = w.shape
    h = r // 2
    tm = _row_tile(h)
    nj = h // tm
    cg = mine.shape[1]

    def body(core_ref, w_ref, a_ref, b_ref, m_ref, v_ref, g_ref, d_ref, nm_ref, nv_ref):
        g = jnp.where(pl.program_id(0) == core_ref[0], a_ref[:, 0:c], b_ref[:, 0:c])
        g_ref[...] = g
        d_ref[...], nm_ref[...], nv_ref[...] = _adam_math(w_ref[...], g, m_ref[...], v_ref[...])

    blk = pl.BlockSpec((tm, c), lambda i, j, core: (i * nj + j, 0))
    gblk = pl.BlockSpec((tm, cg), lambda i, j, core: (j, 0))
    return pl.pallas_call(
        body, name=name, out_shape=tuple([jax.ShapeDtypeStruct((r, c), f32)] * 4),
        grid_spec=pltpu.PrefetchScalarGridSpec(
            num_scalar_prefetch=1, grid=(2, nj), in_specs=[blk, gblk, gblk, blk, blk], out_specs=(blk,) * 4),
        compiler_params=_cp("arbitrary", "arbitrary"),
    )(_scalar(core), w, mine, other, m, v)


def _ada_forward(c_all, ada_w):
    def body(c_ref, w_ref, o_ref):
        o_ref[...] = jnp.dot(_silu(c_ref[...]).astype(MX), w_ref[...].astype(MX), preferred_element_type=f32)

    return pl.pallas_call(body, name="ada_forward", out_shape=jax.ShapeDtypeStruct((8, ada_w.shape[1]), f32),
                          compiler_params=pltpu.CompilerParams(vmem_limit_bytes=VMEM_LIMIT))(c_all, ada_w)


def _ada_adamw(c_all_t, d_mod, w, m, v):
    r, c = w.shape
    tm = TM

    def body(ct_ref, dm_ref, w_ref, m_ref, v_ref, g_ref, d_ref, nm_ref, nv_ref):
        ca = _silu(ct_ref[...])
        g = ca[:, 0:1] * dm_ref[0:1, :]
        for b in range(1, 8):
            g = g + ca[:, b:b + 1] * dm_ref[b:b + 1, :]
        g_ref[...] = g
        d_ref[...], nm_ref[...], nv_ref[...] = _adam_math(w_ref[...], g, m_ref[...], v_ref[...])

    blk = pl.BlockSpec((tm, c), lambda i: (i, 0))
    return pl.pallas_call(
        body, name="ada_adamw", grid=(r // tm,), out_shape=tuple([jax.ShapeDtypeStruct((r, c), f32)] * 4),
        in_specs=[pl.BlockSpec((tm, 8), lambda i: (i, 0)), pl.BlockSpec((8, c), lambda i: (0, 0)), blk, blk, blk],
        out_specs=(blk,) * 4, compiler_params=_cp("arbitrary"),
    )(c_all_t, d_mod, w, m, v)


WEIGHTS = ("ada_w", "ada_b", "norm1_w", "w_in", "ssd_conv_w", "ssd_conv_b", "dt_bias", "a_log", "d_skip", "ssd_norm_w",
           "conf_conv_w", "conf_conv_b", "conf_ln_w", "conf_ln_b", "w_out", "norm2_w", "w_up", "ffn_conv_w", "ffn_conv_b",
           "w_down", "final_norm_w")


def kernel(x, c, ada_w, ada_b, norm1_w, w_in, ssd_conv_w, ssd_conv_b, dt_bias, a_log, d_skip, ssd_norm_w, conf_conv_w, conf_conv_b, conf_ln_w, conf_ln_b, w_out, norm2_w, w_up, ffn_conv_w, ffn_conv_b, w_down, final_norm_w, loss_target, m_ada_w, m_ada_b, m_norm1_w, m_w_in, m_ssd_conv_w, m_ssd_conv_b, m_dt_bias, m_a_log, m_d_skip, m_ssd_norm_w, m_conf_conv_w, m_conf_conv_b, m_conf_ln_w, m_conf_ln_b, m_w_out, m_norm2_w, m_w_up, m_ffn_conv_w, m_ffn_conv_b, m_w_down, m_final_norm_w, v_ada_w, v_ada_b, v_norm1_w, v_w_in, v_ssd_conv_w, v_ssd_conv_b, v_dt_bias, v_a_log, v_d_skip, v_ssd_norm_w, v_conf_conv_w, v_conf_conv_b, v_conf_ln_w, v_conf_ln_b, v_w_out, v_norm2_w, v_w_up, v_ffn_conv_w, v_ffn_conv_b, v_w_down, v_final_norm_w):
    given = dict(locals())
    w = {n: given[n] for n in WEIGHTS}
    mom = {n: given["m_" + n] for n in WEIGHTS}
    var = {n: given["v_" + n] for n in WEIGHTS}
    chip = 2 * lax.axis_index("x") + lax.axis_index("y")
    me = 2 * chip + lax.axis_index("c")

    c_all, *convs = _unpack_front(_gather_rows(_pack_front(c, [w[n] for n in CONVS])))
    conv_full = dict(zip(CONVS, convs))

    mod_cols = _gather_rows(_ada_forward(c_all, ada_w[0])).reshape(8, 8, -1)[0::2]
    mod = lax.dynamic_index_in_dim(mod_cols, me, axis=1, keepdims=False).reshape(1, 6 * D) + ada_b

    core = lax.axis_index("c")
    a_in, = _ride_alone(_GatherRider([_cast_into_slot(w_in[0], W_IN_SHARD_PAD, chip)]), "gather_w_in")
    w_pack = _pack_w_in(a_in.reshape(4, D, W_IN_SHARD_PAD))
    late = (_cast_into_slot(w_out[0], D, chip), _cast_into_slot(w_up[0], UP_SHARD, chip), _cast_into_slot(w_down[0], D, chip))

    flat = lambda a: a.reshape(1, -1) if a.ndim == 1 else a
    small = {n: flat(w[n]) for n in VECTORS if n != "ada_b"}
    small.update(conv_full)
    reducer = _Reducer(chip, core)
    loss_mine, grad_x, _, gsmall = _local_step(x[0], mod, loss_target[0], w_pack, late, small, reducer)
    loss = lax.psum(loss_mine, ("x", "y", "c"))
    big = ("w_in", "w_out", "w_up", "w_down")
    summed = [reducer.sums[n] for n in big]
    big_halves = dict(zip(big, zip(summed, _swap_sums(summed))))
    grads, delta, new_m, new_v = {}, {}, {}, {}

    names = VECTORS + tuple(CONVS)
    d_mod_mine, res = _small_adamw(_gather_rows(gsmall), chip, *[{n: flat(d[n]) for n in names} for d in (w, mom, var)])
    for n in names:
        grads[n], delta[n], new_m[n], new_v[n] = [r.reshape(w[n].shape) for r in res[n]]

    for n in big:
        res = _adamw_halves(w[n][0], *big_halves[n], mom[n][0], var[n][0], core, "adamw_" + n)
        grads[n], delta[n], new_m[n], new_v[n] = [r[None] for r in res]
    res = _ada_adamw(c_all.T, d_mod_mine, ada_w[0], m_ada_w[0], v_ada_w[0])
    grads["ada_w"], delta["ada_w"], new_m["ada_w"], new_v["ada_w"] = [r[None] for r in res]

    return (loss, grad_x[None], *[grads[n] for n in WEIGHTS], *[delta[n] for n in WEIGHTS],
            *[new_m[n] for n in WEIGHTS], *[new_v[n] for n in WEIGHTS])
```

```python
import functools

import jax
import jax.numpy as jnp
from jax import lax
from jax.experimental import pallas as pl
from jax.experimental.pallas import tpu as pltpu

f32 = jnp.float32
MX = jnp.bfloat16

D = 1024
HEADS = 16
HEAD_P = 64
STATE_N = 128
D_XBC = 1536
D_FF = 2816
UP_SHARD = 2 * D_FF // 4
K_SSD, K_CONF, K_FFN = 4, 31, 3
CHUNK = 128
OFF_Z, OFF_XBC, OFF_CA, OFF_CG, OFF_DT = 0, 1024, 2560, 3584, 4608
W_PACK = 4736
TM = 256
CW = 256
RC = 64
LANES = 128
VMEM_LIMIT = 56 * 1024 * 1024

ADAM_LR, ADAM_B1, ADAM_B2, ADAM_EPS, ADAM_WD, ADAM_STEP = 0.001, 0.9, 0.999, 1e-08, 0.01, 10

MESH = pl.DeviceIdType.MESH


def _cp(*sem):
    return pltpu.CompilerParams(dimension_semantics=sem, vmem_limit_bytes=VMEM_LIMIT)


def _resident(shape):
    nd = len(shape)
    return pl.BlockSpec(shape, lambda *_: (0,) * nd, pipeline_mode=pl.Buffered(1))


def _row(width=D):
    return pl.BlockSpec((1, width), lambda *_: (0, 0))


def _call(body, *, name, grid, in_specs, out_specs, out_shape, args, sem, scratch_shapes=(), prefetch=(), rider=None):
    ni, no, ns, npf = len(in_specs), len(out_specs), len(scratch_shapes), len(prefetch)
    ri, ro = (len(rider.inputs), len(rider.out_shape)) if rider is not None else (0, 0)

    def full(*refs):
        pre, refs = refs[:npf], refs[npf:]
        base_in, r_in = refs[:ni], refs[ni:ni + ri]
        base_out, r_out = refs[ni + ri:ni + ri + no], refs[ni + ri + no:ni + ri + no + ro]
        base_scr, r_scr = refs[ni + ri + no + ro:ni + ri + no + ro + ns], refs[ni + ri + no + ro + ns:]
        if rider is None:
            return body(*pre, *base_in, *base_out, *base_scr)
        ids = [pl.program_id(a) for a in range(len(grid))]
        first = functools.reduce(jnp.logical_and, [i == 0 for i in ids])
        last = functools.reduce(jnp.logical_and, [i == g - 1 for i, g in zip(ids, grid)])

        @pl.when(first)
        def _():
            rider.start(r_in, r_out, r_scr)

        body(*pre, *base_in, *base_out, *base_scr)

        @pl.when(last)
        def _():
            rider.finish(r_in, r_out, r_scr)

    extra = dict(shapes=[], scratch=[], aliases={}, inputs=[]) if rider is None else dict(
        shapes=rider.out_shape, scratch=rider.scratch, inputs=rider.inputs,
        aliases={npf + ni + i: no + j for i, j in rider.aliases.items()})
    outs = pl.pallas_call(
        full, name=name, out_shape=tuple(out_shape) + tuple(extra["shapes"]), input_output_aliases=extra["aliases"],
        grid_spec=pltpu.PrefetchScalarGridSpec(
            num_scalar_prefetch=npf, grid=grid, in_specs=list(in_specs) + [ANY] * ri,
            out_specs=tuple(out_specs) + (ANY,) * ro, scratch_shapes=list(scratch_shapes) + list(extra["scratch"])),
        compiler_params=_cp(*sem),
    )(*prefetch, *args, *extra["inputs"])
    return tuple(outs[:no]), tuple(outs[no:])


def _silu(v):
    return v * jax.nn.sigmoid(v)


def _dsilu(v):
    s = jax.nn.sigmoid(v)
    return s * (1.0 + v * (1.0 - s))


def _softplus(v):
    return jnp.maximum(v, 0.0) + jnp.log1p(jnp.exp(-jnp.abs(v)))


def _mm(a, b):
    return jnp.dot(a.astype(MX), b.astype(MX), preferred_element_type=f32)


def _mm_nt(a, b):
    return lax.dot_general(a.astype(MX), b.astype(MX), (((1,), (1,)), ((), ())), preferred_element_type=f32)


def _mm_tn(a, b):
    return lax.dot_general(a.astype(MX), b.astype(MX), (((0,), (0,)), ((), ())), preferred_element_type=f32)


def _ln_inproj(x, mod, norm1_w, w_pack):
    t = x.shape[0]

    def body(x_ref, mod_ref, nw_ref, w_ref, proj_ref, h_ref):
        xv = x_ref[...]
        rstd = lax.rsqrt(jnp.mean(xv * xv, axis=-1, keepdims=True) + 1e-6)
        h = (xv * rstd * nw_ref[...]) * (1.0 + mod_ref[:, D:2 * D]) + mod_ref[:, 0:D]
        hb = h.astype(MX)
        h_ref[...] = hb
        proj_ref[...] = jnp.dot(hb, w_ref[...], preferred_element_type=f32)

    return pl.pallas_call(
        body, name="ln_inproj", grid=(t // TM,),
        out_shape=(jax.ShapeDtypeStruct((t, W_PACK), f32), jax.ShapeDtypeStruct((t, D), MX)),
        in_specs=[pl.BlockSpec((TM, D), lambda i: (i, 0)), _row(6 * D), _row(), _resident((D, W_PACK))],
        out_specs=(pl.BlockSpec((TM, W_PACK), lambda i: (i, 0)), pl.BlockSpec((TM, D), lambda i: (i, 0))),
        compiler_params=_cp("arbitrary"),
    )(x, mod, norm1_w, w_pack)


def _ssd_gate_norm(y_scan, xbc_act, proj, d_skip_row, ssd_norm_w):
    t = y_scan.shape[0]

    def body(y_ref, xs_ref, z_ref, dsk_ref, nw_ref, o_ref):
        y = y_ref[...] + xs_ref[...] * dsk_ref[...]
        yz = y * _silu(z_ref[...])
        rstd = lax.rsqrt(jnp.mean(yz * yz, axis=-1, keepdims=True) + 1e-6)
        o_ref[...] = (yz * rstd * nw_ref[...]).astype(MX)

    blk = pl.BlockSpec((TM, D), lambda i: (i, 0))
    return pl.pallas_call(
        body, name="ssd_gate_norm", grid=(t // TM,), out_shape=jax.ShapeDtypeStruct((t, D), MX),
        in_specs=[blk, blk, blk, _row(), _row()], out_specs=blk, compiler_params=_cp("arbitrary"),
    )(y_scan, xbc_act, proj, d_skip_row, ssd_norm_w)


def _ln_silu(u_conv, ln_w, ln_b):
    t = u_conv.shape[0]

    def body(u_ref, w_ref, b_ref, o_ref):
        u = u_ref[...]
        mu = jnp.mean(u, axis=-1, keepdims=True)
        uc = u - mu
        rstd = lax.rsqrt(jnp.mean(uc * uc, axis=-1, keepdims=True) + 1e-5)
        o_ref[...] = _silu(uc * rstd * w_ref[...] + b_ref[...]).astype(MX)

    blk = pl.BlockSpec((TM, D), lambda i: (i, 0))
    return pl.pallas_call(
        body, name="ln_silu", grid=(t // TM,), out_shape=jax.ShapeDtypeStruct((t, D), MX),
        in_specs=[blk, _row(), _row()], out_specs=blk, compiler_params=_cp("arbitrary"),
    )(u_conv, ln_w, ln_b)


def _outproj_ln2_up(y_ssd, u, w_out, x, mod, norm2_w, w_up):
    t = x.shape[0]

    def body(y_ref, u_ref, wo_ref, x_ref, mod_ref, nw_ref, wu_ref, mix_ref, x1_ref, h2_ref, up_ref):
        mix = jnp.dot(y_ref[...], wo_ref[0:D, :], preferred_element_type=f32)
        mix = mix + jnp.dot(u_ref[...], wo_ref[D:2 * D, :], preferred_element_type=f32)
        mix_ref[...] = mix
        x1 = x_ref[...] + mod_ref[:, 2 * D:3 * D] * mix
        x1_ref[...] = x1
        rstd = lax.rsqrt(jnp.mean(x1 * x1, axis=-1, keepdims=True) + 1e-6)
        h2 = ((x1 * rstd * nw_ref[...]) * (1.0 + mod_ref[:, 4 * D:5 * D]) + mod_ref[:, 3 * D:4 * D]).astype(MX)
        h2_ref[...] = h2
        for k in range(4):
            up_ref[:, k * UP_SHARD:(k + 1) * UP_SHARD] = jnp.dot(h2, wu_ref[k], preferred_element_type=f32)

    blk = pl.BlockSpec((TM, D), lambda i: (i, 0))
    return pl.pallas_call(
        body, name="outproj_ln2_up", grid=(t // TM,),
        out_shape=(jax.ShapeDtypeStruct((t, D), f32), jax.ShapeDtypeStruct((t, D), f32),
                   jax.ShapeDtypeStruct((t, D), MX), jax.ShapeDtypeStruct((t, 2 * D_FF), f32)),
        in_specs=[blk, blk, _resident((2 * D, D)), blk, _row(6 * D), _row(), _resident((4, D, UP_SHARD))],
        out_specs=(blk, blk, blk, pl.BlockSpec((TM, 2 * D_FF), lambda i: (i, 0))),
        compiler_params=_cp("arbitrary"),
    )(y_ssd, u, w_out, x, mod, norm2_w, w_up)


def _down_loss(act, w_down, x1, mod, final_norm_w, target):
    t = x1.shape[0]

    def body(a_ref, wd_ref, x1_ref, mod_ref, wf_ref, tgt_ref, dx2_ref, dffn_ref, dact_ref, st_ref):
        @pl.when(pl.program_id(0) == 0)
        def _():
            st_ref[...] = jnp.zeros_like(st_ref)

        g2 = mod_ref[:, 5 * D:6 * D]
        ffn = jnp.dot(a_ref[...], wd_ref[...], preferred_element_type=f32)
        x2 = x1_ref[...] + g2 * ffn
        rstd = lax.rsqrt(jnp.mean(x2 * x2, axis=-1, keepdims=True) + 1e-6)
        xh = x2 * rstd
        wf = wf_ref[...]
        err = xh * wf - tgt_ref[...]
        dy = err * (1.0 / D)
        dxh = dy * wf
        dx2 = rstd * (dxh - xh * jnp.mean(dxh * xh, axis=-1, keepdims=True))
        dx2_ref[...] = dx2
        dffn = (g2 * dx2).astype(MX)
        dffn_ref[...] = dffn
        dact_ref[...] = lax.dot_general(dffn, wd_ref[...], (((1,), (1,)), ((), ())), preferred_element_type=f32)
        st_ref[0:1, :] += jnp.sum(dy * xh, axis=0, keepdims=True)
        st_ref[1:2, :] += jnp.sum(dx2 * ffn, axis=0, keepdims=True)
        st_ref[2:3, :] += jnp.sum(0.5 * jnp.mean(err * err, axis=-1, keepdims=True), axis=0, keepdims=True)

    blk = pl.BlockSpec((TM, D), lambda i: (i, 0))
    ablk = pl.BlockSpec((TM, D_FF), lambda i: (i, 0))
    return pl.pallas_call(
        body, name="down_loss", grid=(t // TM,),
        out_shape=(jax.ShapeDtypeStruct((t, D), f32), jax.ShapeDtypeStruct((t, D), MX),
                   jax.ShapeDtypeStruct((t, D_FF), f32), jax.ShapeDtypeStruct((8, D), f32)),
        in_specs=[ablk, _resident((D_FF, D)), blk, _row(6 * D), _row(), blk],
        out_specs=(blk, blk, ablk, pl.BlockSpec((8, D), lambda i: (0, 0))),
        compiler_params=_cp("arbitrary"),
    )(act, w_down, x1, mod, final_norm_w, target)


def _pad_of(k):
    return 8 * ((k - 1 + 7) // 8)


def _causal_win(ref, r, t, pad):
    base = pl.multiple_of(r * RC, RC)
    prev = ref[pl.ds(pl.multiple_of(jnp.maximum(base - pad, 0), 8), pad), :]
    prev = jnp.where(r > 0, prev, 0.0)
    return jnp.concatenate([prev, ref[pl.ds(base, RC), :]], axis=0)


def _anti_win(ref, r, t, pad):
    base = pl.multiple_of(r * RC, RC)
    nxt = ref[pl.ds(pl.multiple_of(jnp.minimum(base + RC, t - pad), 8), pad), :]
    nxt = jnp.where(r < t // RC - 1, nxt, 0.0)
    return jnp.concatenate([ref[pl.ds(base, RC), :], nxt], axis=0)


def _shifted(win, offsets):
    for r in range(8):
        mine = [o for o in offsets if o % 8 == r]
        if mine:
            rolled = win if r == 0 else pltpu.roll(win, win.shape[0] - r, 0)
            for o in mine:
                yield o, rolled[o - r:o - r + RC, :]


def _conv_taps(win, w_ref, k, pad):
    first = pad - (k - 1)
    acc = None
    for o, rows in _shifted(win, range(first, first + k)):
        term = w_ref[o - first:o - first + 1, :] * rows
        acc = term if acc is None else acc + term
    return acc


def _corr_taps(win, w_ref, k):
    acc = None
    for o, rows in _shifted(win, range(k)):
        term = w_ref[k - 1 - o:k - o, :] * rows
        acc = term if acc is None else acc + term
    return acc


def _dw_accumulate(dw_scr, d, win, k, pad):
    first = pad - (k - 1)
    for o, rows in _shifted(win, range(first, first + k)):
        j = o - first
        prod = d * rows
        dw_scr[8 * j:8 * j + 8, :] += prod.reshape(RC // 8, 8, prod.shape[-1]).sum(axis=0)


def _dw_finish(dw_scr, dw_ref, k):
    for j in range(k):
        dw_ref[j:j + 1, :] = jnp.sum(dw_scr[8 * j:8 * j + 8, :], axis=0, keepdims=True)


def _rows8(v):
    return v.reshape(RC // 8, 8, v.shape[-1]).sum(axis=0)


def _ssd_conv_fwd(proj, conv_w, conv_b):
    t = proj.shape[0]
    pad = _pad_of(K_SSD)
    c0 = OFF_XBC // CW

    def body(x_ref, w_ref, b_ref, o_ref):
        def step(r, carry):
            win = _causal_win(x_ref, r, t, pad)
            o_ref[pl.ds(pl.multiple_of(r * RC, RC), RC), :] = _silu(_conv_taps(win, w_ref, K_SSD, pad) + b_ref[...])
            return carry
        lax.fori_loop(0, t // RC, step, 0)

    return pl.pallas_call(
        body, name="ssd_conv_fwd", grid=(D_XBC // CW,), out_shape=jax.ShapeDtypeStruct((t, D_XBC), f32),
        in_specs=[pl.BlockSpec((t, CW), lambda j: (0, c0 + j)), pl.BlockSpec((K_SSD, CW), lambda j: (0, j)),
                  pl.BlockSpec((1, CW), lambda j: (0, j))],
        out_specs=pl.BlockSpec((t, CW), lambda j: (0, j)), compiler_params=_cp("arbitrary"),
    )(proj, conv_w, conv_b)


def _glu_conv_fwd(proj, conv_w, conv_b, rider=None):
    t = proj.shape[0]
    pad = _pad_of(K_CONF)
    ca, cg = OFF_CA // CW, OFF_CG // CW

    def body(a_ref, g_ref, w_ref, b_ref, o_ref, v_scr):
        def glu(r, carry):
            rows = pl.ds(pl.multiple_of(r * RC, RC), RC)
            v_scr[rows, :] = a_ref[rows, :] * jax.nn.sigmoid(g_ref[rows, :])
            return carry
        lax.fori_loop(0, t // RC, glu, 0)

        def step(r, carry):
            win = _causal_win(v_scr, r, t, pad)
            o_ref[pl.ds(pl.multiple_of(r * RC, RC), RC), :] = _conv_taps(win, w_ref, K_CONF, pad) + b_ref[...]
            return carry
        lax.fori_loop(0, t // RC, step, 0)

    return _call(
        body, name="glu_conv_fwd", grid=(D // CW,), out_shape=(jax.ShapeDtypeStruct((t, D), f32),),
        in_specs=[pl.BlockSpec((t, CW), lambda j: (0, ca + j)), pl.BlockSpec((t, CW), lambda j: (0, cg + j)),
                  pl.BlockSpec((K_CONF, CW), lambda j: (0, j)), pl.BlockSpec((1, CW), lambda j: (0, j))],
        out_specs=(pl.BlockSpec((t, CW), lambda j: (0, j)),),
        scratch_shapes=[pltpu.VMEM((t, CW), f32)], sem=("arbitrary",), args=(proj, proj, conv_w, conv_b), rider=rider)


def _ffn_conv_fwd(up, conv_w, conv_b, rider=None):
    t = up.shape[0]
    pad = _pad_of(K_FFN)
    nb = D_FF // CW

    def body(g_ref, v_ref, wg_ref, wv_ref, bg_ref, bv_ref, o_ref):
        def step(r, carry):
            gc = _conv_taps(_causal_win(g_ref, r, t, pad), wg_ref, K_FFN, pad) + bg_ref[...]
            vc = _conv_taps(_causal_win(v_ref, r, t, pad), wv_ref, K_FFN, pad) + bv_ref[...]
            o_ref[pl.ds(pl.multiple_of(r * RC, RC), RC), :] = (_silu(gc) * vc).astype(MX)
            return carry
        lax.fori_loop(0, t // RC, step, 0)

    return _call(
        body, name="ffn_conv_fwd", grid=(nb,), out_shape=(jax.ShapeDtypeStruct((t, D_FF), MX),),
        in_specs=[pl.BlockSpec((t, CW), lambda j: (0, j)), pl.BlockSpec((t, CW), lambda j: (0, nb + j)),
                  pl.BlockSpec((K_FFN, CW), lambda j: (0, j)), pl.BlockSpec((K_FFN, CW), lambda j: (0, nb + j)),
                  pl.BlockSpec((1, CW), lambda j: (0, j)), pl.BlockSpec((1, CW), lambda j: (0, nb + j))],
        out_specs=(pl.BlockSpec((t, CW), lambda j: (0, j)),), sem=("arbitrary",),
        args=(up, up, conv_w, conv_w, conv_b, conv_b), rider=rider)


def _ffn_conv_bwd(up, conv_w, conv_b, d_act, rider=None):
    t = up.shape[0]
    pad = _pad_of(K_FFN)
    nb = D_FF // CW

    def body(g_ref, v_ref, wg_ref, wv_ref, bg_ref, bv_ref, da_ref, dup_ref, dw_ref, db_ref,
             dg_scr, dv_scr, dwg_scr, dwv_scr, db_scr):
        dwg_scr[...] = jnp.zeros_like(dwg_scr)
        dwv_scr[...] = jnp.zeros_like(dwv_scr)
        db_scr[...] = jnp.zeros_like(db_scr)

        def first(r, carry):
            rows = pl.ds(pl.multiple_of(r * RC, RC), RC)
            gwin = _causal_win(g_ref, r, t, pad)
            vwin = _causal_win(v_ref, r, t, pad)
            gc = _conv_taps(gwin, wg_ref, K_FFN, pad) + bg_ref[...]
            vc = _conv_taps(vwin, wv_ref, K_FFN, pad) + bv_ref[...]
            da = da_ref[rows, :]
            dgc = da * vc * _dsilu(gc)
            dvc = da * _silu(gc)
            dg_scr[rows, :] = dgc
            dv_scr[rows, :] = dvc
            _dw_accumulate(dwg_scr, dgc, gwin, K_FFN, pad)
            _dw_accumulate(dwv_scr, dvc, vwin, K_FFN, pad)
            db_scr[0:8, :] += _rows8(dgc)
            db_scr[8:16, :] += _rows8(dvc)
            return carry
        lax.fori_loop(0, t // RC, first, 0)

        def second(r, carry):
            rows = pl.ds(pl.multiple_of(r * RC, RC), RC)
            dup_ref[0, rows, :] = _corr_taps(_anti_win(dg_scr, r, t, pad), wg_ref, K_FFN).astype(MX)
            dup_ref[1, rows, :] = _corr_taps(_anti_win(dv_scr, r, t, pad), wv_ref, K_FFN).astype(MX)
            return carry
        lax.fori_loop(0, t // RC, second, 0)

        for j in range(K_FFN):
            dw_ref[0, j:j + 1, :] = jnp.sum(dwg_scr[8 * j:8 * j + 8, :], axis=0, keepdims=True)
            dw_ref[1, j:j + 1, :] = jnp.sum(dwv_scr[8 * j:8 * j + 8, :], axis=0, keepdims=True)
        db_ref[0] = jnp.sum(db_scr[0:8, :], axis=0, keepdims=True)
        db_ref[1] = jnp.sum(db_scr[8:16, :], axis=0, keepdims=True)

    return _call(
        body, name="ffn_conv_bwd", grid=(nb,),
        out_shape=(jax.ShapeDtypeStruct((2, t, D_FF), MX), jax.ShapeDtypeStruct((2, K_FFN, D_FF), f32),
                   jax.ShapeDtypeStruct((2, 1, D_FF), f32)),
        in_specs=[pl.BlockSpec((t, CW), lambda j: (0, j)), pl.BlockSpec((t, CW), lambda j: (0, nb + j)),
                  pl.BlockSpec((K_FFN, CW), lambda j: (0, j)), pl.BlockSpec((K_FFN, CW), lambda j: (0, nb + j)),
                  pl.BlockSpec((1, CW), lambda j: (0, j)), pl.BlockSpec((1, CW), lambda j: (0, nb + j)),
                  pl.BlockSpec((t, CW), lambda j: (0, j))],
        out_specs=(pl.BlockSpec((2, t, CW), lambda j: (0, 0, j)), pl.BlockSpec((2, K_FFN, CW), lambda j: (0, 0, j)),
                   pl.BlockSpec((2, 1, CW), lambda j: (0, 0, j))),
        scratch_shapes=[pltpu.VMEM((t, CW), f32), pltpu.VMEM((t, CW), f32), pltpu.VMEM((8 * K_FFN, CW), f32),
                        pltpu.VMEM((8 * K_FFN, CW), f32), pltpu.VMEM((16, CW), f32)],
        sem=("arbitrary",), args=(up, up, conv_w, conv_w, conv_b, conv_b, d_act), rider=rider)


def _glu_conv_bwd(proj, conv_w, d_uconv, rider=None):
    t = proj.shape[0]
    pad = _pad_of(K_CONF)
    ca, cg = OFF_CA // CW, OFF_CG // CW

    def body(a_ref, g_ref, w_ref, du_ref, dc_ref, dw_ref, db_ref, v_scr, dw_scr, db_scr):
        dw_scr[...] = jnp.zeros_like(dw_scr)
        db_scr[...] = jnp.zeros_like(db_scr)

        def glu(r, carry):
            rows = pl.ds(pl.multiple_of(r * RC, RC), RC)
            v_scr[rows, :] = a_ref[rows, :] * jax.nn.sigmoid(g_ref[rows, :])
            return carry
        lax.fori_loop(0, t // RC, glu, 0)

        def step(r, carry):
            rows = pl.ds(pl.multiple_of(r * RC, RC), RC)
            du = du_ref[rows, :]
            _dw_accumulate(dw_scr, du, _causal_win(v_scr, r, t, pad), K_CONF, pad)
            db_scr[...] += _rows8(du)
            dv = _corr_taps(_anti_win(du_ref, r, t, pad), w_ref, K_CONF)
            a = a_ref[rows, :]
            s = jax.nn.sigmoid(g_ref[rows, :])
            dc_ref[0, rows, :] = (dv * s).astype(MX)
            dc_ref[1, rows, :] = (dv * a * s * (1.0 - s)).astype(MX)
            return carry
        lax.fori_loop(0, t // RC, step, 0)
        _dw_finish(dw_scr, dw_ref, K_CONF)
        db_ref[...] = jnp.sum(db_scr[...], axis=0, keepdims=True)

    return _call(
        body, name="glu_conv_bwd", grid=(D // CW,),
        out_shape=(jax.ShapeDtypeStruct((2, t, D), MX), jax.ShapeDtypeStruct((K_CONF, D), f32),
                   jax.ShapeDtypeStruct((1, D), f32)),
        in_specs=[pl.BlockSpec((t, CW), lambda j: (0, ca + j)), pl.BlockSpec((t, CW), lambda j: (0, cg + j)),
                  pl.BlockSpec((K_CONF, CW), lambda j: (0, j)), pl.BlockSpec((t, CW), lambda j: (0, j))],
        out_specs=(pl.BlockSpec((2, t, CW), lambda j: (0, 0, j)), pl.BlockSpec((K_CONF, CW), lambda j: (0, j)),
                   pl.BlockSpec((1, CW), lambda j: (0, j))),
        scratch_shapes=[pltpu.VMEM((t, CW), f32), pltpu.VMEM((8 * K_CONF, CW), f32), pltpu.VMEM((8, CW), f32)],
        sem=("arbitrary",), args=(proj, proj, conv_w, d_uconv), rider=rider)


def _ssd_conv_bwd_x(proj, conv_w, conv_b, d_xs, d_y, d_skip_row):
    t = proj.shape[0]
    pad = _pad_of(K_SSD)
    c0 = OFF_XBC // CW

    def body(x_ref, w_ref, b_ref, dxs_ref, dy_ref, dsk_ref, draw_ref, dw_ref, db_ref, dp_scr, dw_scr, db_scr):
        dw_scr[...] = jnp.zeros_like(dw_scr)
        db_scr[...] = jnp.zeros_like(db_scr)

        def first(r, carry):
            rows = pl.ds(pl.multiple_of(r * RC, RC), RC)
            win = _causal_win(x_ref, r, t, pad)
            pre = _conv_taps(win, w_ref, K_SSD, pad) + b_ref[...]
            dpre = (dxs_ref[rows, :] + dy_ref[rows, :] * dsk_ref[...]) * _dsilu(pre)
            dp_scr[rows, :] = dpre
            _dw_accumulate(dw_scr, dpre, win, K_SSD, pad)
            db_scr[...] += _rows8(dpre)
            return carry
        lax.fori_loop(0, t // RC, first, 0)

        def second(r, carry):
            rows = pl.ds(pl.multiple_of(r * RC, RC), RC)
            draw_ref[rows, :] = _corr_taps(_anti_win(dp_scr, r, t, pad), w_ref, K_SSD).astype(MX)
            return carry
        lax.fori_loop(0, t // RC, second, 0)
        _dw_finish(dw_scr, dw_ref, K_SSD)
        db_ref[...] = jnp.sum(db_scr[...], axis=0, keepdims=True)

    cb = pl.BlockSpec((t, CW), lambda j: (0, j))
    return pl.pallas_call(
        body, name="ssd_conv_bwd_x", grid=(D // CW,),
        out_shape=(jax.ShapeDtypeStruct((t, D), MX), jax.ShapeDtypeStruct((K_SSD, D), f32),
                   jax.ShapeDtypeStruct((1, D), f32)),
        in_specs=[pl.BlockSpec((t, CW), lambda j: (0, c0 + j)), pl.BlockSpec((K_SSD, CW), lambda j: (0, j)),
                  pl.BlockSpec((1, CW), lambda j: (0, j)), cb, cb, pl.BlockSpec((1, CW), lambda j: (0, j))],
        out_specs=(cb, pl.BlockSpec((K_SSD, CW), lambda j: (0, j)), pl.BlockSpec((1, CW), lambda j: (0, j))),
        scratch_shapes=[pltpu.VMEM((t, CW), f32), pltpu.VMEM((8 * K_SSD, CW), f32), pltpu.VMEM((8, CW), f32)],
        compiler_params=_cp("arbitrary"),
    )(proj, conv_w, conv_b, d_xs, d_y, d_skip_row)


def _ssd_conv_bwd_bc(proj, conv_w, conv_b, d_bc):
    t = proj.shape[0]
    pad = _pad_of(K_SSD)
    c0 = (OFF_XBC + D) // CW
    w0 = D // CW

    def body(x_ref, w_ref, b_ref, dbc_ref, draw_ref, dw_ref, db_ref, dp_scr, dw_scr, db_scr):
        dw_scr[...] = jnp.zeros_like(dw_scr)
        db_scr[...] = jnp.zeros_like(db_scr)

        def first(r, carry):
            rows = pl.ds(pl.multiple_of(r * RC, RC), RC)
            win = _causal_win(x_ref, r, t, pad)
            pre = _conv_taps(win, w_ref, K_SSD, pad) + b_ref[...]
            dpre = dbc_ref[0, rows, :] * _dsilu(pre)
            dp_scr[rows, :] = dpre
            _dw_accumulate(dw_scr, dpre, win, K_SSD, pad)
            db_scr[...] += _rows8(dpre)
            return carry
        lax.fori_loop(0, t // RC, first, 0)

        def second(r, carry):
            rows = pl.ds(pl.multiple_of(r * RC, RC), RC)
            draw_ref[rows, :] = _corr_taps(_anti_win(dp_scr, r, t, pad), w_ref, K_SSD).astype(MX)
            return carry
        lax.fori_loop(0, t // RC, second, 0)
        _dw_finish(dw_scr, dw_ref, K_SSD)
        db_ref[...] = jnp.sum(db_scr[...], axis=0, keepdims=True)

    return pl.pallas_call(
        body, name="ssd_conv_bwd_bc", grid=(2,),
        out_shape=(jax.ShapeDtypeStruct((t, 2 * CW), MX), jax.ShapeDtypeStruct((K_SSD, 2 * CW), f32),
                   jax.ShapeDtypeStruct((1, 2 * CW), f32)),
        in_specs=[pl.BlockSpec((t, CW), lambda j: (0, c0 + j)), pl.BlockSpec((K_SSD, CW), lambda j: (0, w0 + j)),
                  pl.BlockSpec((1, CW), lambda j: (0, w0 + j)), pl.BlockSpec((1, t, CW), lambda j: (j, 0, 0))],
        out_specs=(pl.BlockSpec((t, CW), lambda j: (0, j)), pl.BlockSpec((K_SSD, CW), lambda j: (0, j)),
                   pl.BlockSpec((1, CW), lambda j: (0, j))),
        scratch_shapes=[pltpu.VMEM((t, CW), f32), pltpu.VMEM((8 * K_SSD, CW), f32), pltpu.VMEM((8, CW), f32)],
        compiler_params=_cp("arbitrary"),
    )(proj, conv_w, conv_b, d_bc)


def _chunk_masks():
    ii = lax.broadcasted_iota(jnp.int32, (CHUNK, CHUNK), 0)
    jj = lax.broadcasted_iota(jnp.int32, (CHUNK, CHUNK), 1)
    return ii == jj, jj <= ii, jj >= ii


def _to_row(col, eye):
    return jnp.sum(jnp.where(eye, col, 0.0), axis=0, keepdims=True)


def _to_col(row, eye):
    return jnp.sum(jnp.where(eye, row, 0.0), axis=1, keepdims=True)


def _head_decay(dt_h, a_h, eye, tril):
    a_row = _to_row(dt_h * a_h, eye)
    cs = jnp.sum(jnp.where(tril, a_row, 0.0), axis=1, keepdims=True)
    cs_row = _to_row(cs, eye)
    decay = jnp.where(tril, jnp.exp(jnp.where(tril, cs - cs_row, 0.0)), 0.0)
    total = jnp.sum(a_row, axis=1, keepdims=True)
    return cs, decay, total


SCAN_UNROLL = 4


def _unrolled_loop(n, step, init):
    unroll = min(SCAN_UNROLL, n)
    assert n % unroll == 0

    def trip(i, carry):
        for u in range(unroll):
            carry = step(unroll * i + u, carry)
        return carry
    return lax.fori_loop(0, n // unroll, trip, init)


def _lane_pick(mat, lane, which):
    return jnp.sum(jnp.where(lane == which, mat, 0.0), axis=1, keepdims=True)


def _ssd_fwd(xbc_act, proj, dt_bias_row, a_log_row, rider=None):
    t = xbc_act.shape[0]
    nc = t // CHUNK
    cb, cc, cdt = D // LANES, (D + 2 * STATE_N) // LANES, OFF_DT // LANES

    def body(x_ref, b_ref, c_ref, dt_ref, dtb_ref, alog_ref, y_ref, st_ref):
        j = pl.program_id(0)
        eye, tril, _ = _chunk_masks()
        lane = lax.broadcasted_iota(jnp.int32, (1, LANES), 1)
        first = lane < HEAD_P
        a_row = -jnp.exp(alog_ref[...])
        a_heads = [jnp.sum(jnp.where(lane == 2 * j + h, a_row, 0.0), axis=1, keepdims=True) for h in range(2)]

        def chunk(c, hprev):
            rows = pl.ds(pl.multiple_of(c * CHUNK, CHUNK), CHUNK)
            xv, bm, cm = x_ref[rows, :], b_ref[rows, :], c_ref[rows, :]
            dt = _softplus(dt_ref[rows, :] + dtb_ref[...])
            st_ref[c] = hprev
            g = _mm_nt(cm, bm)
            ch = _mm(cm, hprev)
            dts = [_lane_pick(dt, lane, 2 * j + h) for h in range(2)]
            xdt = xv * jnp.where(first, dts[0], dts[1])
            ys, hs = [], []
            for h in range(2):
                cs, decay, total = _head_decay(dts[h], a_heads[h], eye, tril)
                y = _mm(g * decay, xdt) + jnp.exp(cs) * ch
                s = _mm_tn(bm * jnp.exp(total - cs), xdt)
                ys.append(y)
                hs.append(jnp.exp(total) * hprev + s)
            y_ref[rows, :] = jnp.where(first, ys[0], ys[1])
            return jnp.where(first, hs[0], hs[1])

        _unrolled_loop(nc, chunk, jnp.zeros((STATE_N, LANES), f32))

    blk = lambda f: pl.BlockSpec((t, LANES), f)
    return _call(
        body, name="ssd_fwd", grid=(D // LANES,),
        out_shape=(jax.ShapeDtypeStruct((t, D), f32), jax.ShapeDtypeStruct((nc, STATE_N, D), f32)),
        in_specs=[blk(lambda j: (0, j)), blk(lambda j: (0, cb + j // 4)), blk(lambda j: (0, cc + j // 4)),
                  blk(lambda j: (0, cdt)), _row(LANES), _row(LANES)],
        out_specs=(blk(lambda j: (0, j)), pl.BlockSpec((nc, STATE_N, LANES), lambda j: (0, 0, j))),
        sem=("arbitrary",), args=(xbc_act, xbc_act, xbc_act, proj, dt_bias_row, a_log_row), rider=rider)


def _ssd_bwd(xbc_act, proj, dt_bias_row, a_log_row, states, d_y, rider=None):
    t = xbc_act.shape[0]
    nc = t // CHUNK
    cb, cc, cdt = D // LANES, (D + 2 * STATE_N) // LANES, OFF_DT // LANES

    def body(x_ref, b_ref, c_ref, dt_ref, dtb_ref, alog_ref, st_ref, dy_ref, dx_ref, dbc_ref, ddt_ref, da_ref):
        grp, p = pl.program_id(0), pl.program_id(1)
        j = 4 * grp + p
        eye, tril, triu = _chunk_masks()
        lane = lax.broadcasted_iota(jnp.int32, (1, LANES), 1)
        first = lane < HEAD_P
        last_row = lax.broadcasted_iota(jnp.int32, (CHUNK, 1), 0) == CHUNK - 1
        a_row = -jnp.exp(alog_ref[...])
        a_heads = [jnp.sum(jnp.where(lane == 2 * j + h, a_row, 0.0), axis=1, keepdims=True) for h in range(2)]

        @pl.when(p == 0)
        def _():
            dbc_ref[...] = jnp.zeros_like(dbc_ref)

        @pl.when(j == 0)
        def _():
            ddt_ref[...] = jnp.zeros_like(ddt_ref)
            da_ref[...] = jnp.zeros_like(da_ref)

        def chunk(i, dh):
            c = nc - 1 - i
            rows = pl.ds(pl.multiple_of(c * CHUNK, CHUNK), CHUNK)
            xv, bm, cm = x_ref[rows, :], b_ref[rows, :], c_ref[rows, :]
            dtr = dt_ref[rows, :] + dtb_ref[...]
            dt = _softplus(dtr)
            hprev = st_ref[c]
            dy = dy_ref[rows, :]
            g = _mm_nt(cm, bm)
            dts = [_lane_pick(dt, lane, 2 * j + h) for h in range(2)]
            xdt = xv * jnp.where(first, dts[0], dts[1])
            dxs, dhs = [], []
            db_sum, dc_sum = None, None
            ddt_mat = jnp.zeros((CHUNK, LANES), f32)
            da_acc = jnp.zeros((1, LANES), f32)
            for h in range(2):
                mine = first if h == 0 else jnp.logical_not(first)
                cs, decay, total = _head_decay(dts[h], a_heads[h], eye, tril)
                e_cs, e_tot = jnp.exp(cs), jnp.exp(total)
                dec_s = jnp.exp(total - cs)
                dyh = jnp.where(mine, dy, 0.0)
                xdth = jnp.where(mine, xdt, 0.0)
                dhh = jnp.where(mine, dh, 0.0)
                hph = jnp.where(mine, hprev, 0.0)
                m = g * decay
                dm = _mm_nt(dyh, xdth)
                dg = dm * decay
                w = dm * m
                bdec = bm * dec_s
                dxdt = _mm_tn(m, dyh) + _mm(bdec, dhh)
                dc_off = _mm_nt(dyh, hph) * e_cs
                db_s = _mm_nt(xdth, dhh) * dec_s
                dc_h = _mm(dg, bm) + dc_off
                db_h = _mm_tn(dg, cm) + db_s
                r_s = jnp.sum(db_s * bm, axis=1, keepdims=True)
                dtotal = jnp.sum(r_s, axis=0, keepdims=True) + e_tot * jnp.sum(
                    jnp.sum(dhh * hph, axis=1, keepdims=True), axis=0, keepdims=True)
                dcs = (jnp.sum(w, axis=1, keepdims=True) - _to_col(jnp.sum(w, axis=0, keepdims=True), eye)
                       + jnp.sum(dc_off * cm, axis=1, keepdims=True) - r_s + jnp.where(last_row, dtotal, 0.0))
                da_col = jnp.sum(jnp.where(triu, _to_row(dcs, eye), 0.0), axis=1, keepdims=True)
                ddt = da_col * a_heads[h] + jnp.sum(jnp.where(mine, dxdt * xv, 0.0), axis=1, keepdims=True)
                ddt_mat = ddt_mat + jnp.where(lane == 2 * j + h, ddt, 0.0)
                da_acc = da_acc + jnp.where(lane == 2 * j + h, jnp.sum(da_col * dts[h], axis=0, keepdims=True), 0.0)
                dxs.append(dxdt * dts[h])
                dhs.append(e_tot * dhh + _mm_tn(cm * e_cs, dyh))
                db_sum = db_h if db_sum is None else db_sum + db_h
                dc_sum = dc_h if dc_sum is None else dc_sum + dc_h
            dx_ref[rows, :] = jnp.where(first, dxs[0], dxs[1])
            dbc_ref[0, rows, :] += db_sum
            dbc_ref[1, rows, :] += dc_sum
            ddt_ref[rows, :] += ddt_mat * jax.nn.sigmoid(dtr)
            da_ref[...] += da_acc * a_row
            return jnp.where(first, dhs[0], dhs[1])

        _unrolled_loop(nc, chunk, jnp.zeros((STATE_N, LANES), f32))

    blk = lambda f: pl.BlockSpec((t, LANES), f)
    return _call(
        body, name="ssd_bwd", grid=(2, 4),
        out_shape=(jax.ShapeDtypeStruct((t, D), f32), jax.ShapeDtypeStruct((2, t, 2 * STATE_N), f32),
                   jax.ShapeDtypeStruct((t, LANES), f32), jax.ShapeDtypeStruct((1, LANES), f32)),
        in_specs=[blk(lambda g, p: (0, 4 * g + p)), blk(lambda g, p: (0, cb + g)), blk(lambda g, p: (0, cc + g)),
                  blk(lambda g, p: (0, cdt)), _row(LANES), _row(LANES),
                  pl.BlockSpec((nc, STATE_N, LANES), lambda g, p: (0, 0, 4 * g + p)), blk(lambda g, p: (0, 4 * g + p))],
        out_specs=(blk(lambda g, p: (0, 4 * g + p)), pl.BlockSpec((2, t, LANES), lambda g, p: (0, 0, g)),
                   blk(lambda g, p: (0, 0)), _row(LANES)),
        sem=("arbitrary", "arbitrary"), args=(xbc_act, xbc_act, xbc_act, proj, dt_bias_row, a_log_row, states, d_y),
        rider=rider)


def _up_bwd(d_up, w_up, x1, mod, norm2_w, dx2, mix, w_out, rider=None):
    t = x1.shape[0]

    def body(dup_ref, wu_ref, x1_ref, mod_ref, nw_ref, dx2_ref, mix_ref, wo_ref,
             dx1_ref, dmix_ref, dys_ref, du_ref, st_ref):
        @pl.when(pl.program_id(0) == 0)
        def _():
            st_ref[...] = jnp.zeros_like(st_ref)

        nt = (((1,), (1,)), ((), ()))
        dh = None
        for k in range(4):
            lo = (k % 2) * UP_SHARD
            part = lax.dot_general(dup_ref[k // 2, :, lo:lo + UP_SHARD], wu_ref[k], nt, preferred_element_type=f32)
            dh = part if dh is None else dh + part
        x1 = x1_ref[...]
        rstd = lax.rsqrt(jnp.mean(x1 * x1, axis=-1, keepdims=True) + 1e-6)
        xh = x1 * rstd
        nw = nw_ref[...]
        sc = 1.0 + mod_ref[:, 4 * D:5 * D]
        st_ref[0:1, :] += jnp.sum(dh, axis=0, keepdims=True)
        st_ref[1:2, :] += jnp.sum(dh * xh * nw, axis=0, keepdims=True)
        st_ref[2:3, :] += jnp.sum(dh * sc * xh, axis=0, keepdims=True)
        dxh = dh * sc * nw
        dx1 = dx2_ref[...] + rstd * (dxh - xh * jnp.mean(dxh * xh, axis=-1, keepdims=True))
        dx1_ref[...] = dx1
        st_ref[3:4, :] += jnp.sum(dx1 * mix_ref[...], axis=0, keepdims=True)
        dmix = (mod_ref[:, 2 * D:3 * D] * dx1).astype(MX)
        dmix_ref[...] = dmix
        dys_ref[...] = lax.dot_general(dmix, wo_ref[0:D, :], nt, preferred_element_type=f32)
        du_ref[...] = lax.dot_general(dmix, wo_ref[D:2 * D, :], nt, preferred_element_type=f32)

    blk = pl.BlockSpec((TM, D), lambda i: (i, 0))
    return _call(
        body, name="up_bwd", grid=(t // TM,),
        out_shape=(jax.ShapeDtypeStruct((t, D), f32), jax.ShapeDtypeStruct((t, D), MX),
                   jax.ShapeDtypeStruct((t, D), f32), jax.ShapeDtypeStruct((t, D), f32),
                   jax.ShapeDtypeStruct((8, D), f32)),
        in_specs=[pl.BlockSpec((2, TM, D_FF), lambda i: (0, i, 0)), _resident((4, D, UP_SHARD)), blk, _row(6 * D), _row(),
                  blk, blk, _resident((2 * D, D))],
        out_specs=(blk, blk, blk, blk, pl.BlockSpec((8, D), lambda i: (0, 0))),
        sem=("arbitrary",), args=(d_up, w_up, x1, mod, norm2_w, dx2, mix, w_out), rider=rider)


def _ln_silu_bwd(d_u, u_conv, ln_w, ln_b):
    t = d_u.shape[0]

    def body(du_ref, u_ref, w_ref, b_ref, o_ref, st_ref):
        @pl.when(pl.program_id(0) == 0)
        def _():
            st_ref[...] = jnp.zeros_like(st_ref)

        u = u_ref[...]
        mu = jnp.mean(u, axis=-1, keepdims=True)
        uc = u - mu
        rstd = lax.rsqrt(jnp.mean(uc * uc, axis=-1, keepdims=True) + 1e-5)
        n = uc * rstd
        w = w_ref[...]
        dl = du_ref[...] * _dsilu(n * w + b_ref[...])
        st_ref[0:1, :] += jnp.sum(dl * n, axis=0, keepdims=True)
        st_ref[1:2, :] += jnp.sum(dl, axis=0, keepdims=True)
        dn = dl * w
        o_ref[...] = rstd * (dn - jnp.mean(dn, axis=-1, keepdims=True) - n * jnp.mean(dn * n, axis=-1, keepdims=True))

    blk = pl.BlockSpec((TM, D), lambda i: (i, 0))
    return pl.pallas_call(
        body, name="ln_silu_bwd", grid=(t // TM,),
        out_shape=(jax.ShapeDtypeStruct((t, D), f32), jax.ShapeDtypeStruct((8, D), f32)),
        in_specs=[blk, blk, _row(), _row()], out_specs=(blk, pl.BlockSpec((8, D), lambda i: (0, 0))),
        compiler_params=_cp("arbitrary"),
    )(d_u, u_conv, ln_w, ln_b)


def _ssd_gate_norm_bwd(d_out, y_scan, xbc_act, proj, d_skip_row, ssd_norm_w):
    t = d_out.shape[0]

    def body(do_ref, y_ref, xs_ref, z_ref, dsk_ref, nw_ref, dy_ref, dz_ref, st_ref):
        @pl.when(pl.program_id(0) == 0)
        def _():
            st_ref[...] = jnp.zeros_like(st_ref)

        xs = xs_ref[...]
        y = y_ref[...] + xs * dsk_ref[...]
        z = z_ref[...]
        s = _silu(z)
        yz = y * s
        rstd = lax.rsqrt(jnp.mean(yz * yz, axis=-1, keepdims=True) + 1e-6)
        n = yz * rstd
        do = do_ref[...]
        st_ref[0:1, :] += jnp.sum(do * n, axis=0, keepdims=True)
        dn = do * nw_ref[...]
        dyz = rstd * (dn - n * jnp.mean(dn * n, axis=-1, keepdims=True))
        dy = dyz * s
        dy_ref[...] = dy
        dz_ref[...] = (dyz * y * _dsilu(z)).astype(MX)
        st_ref[1:2, :] += jnp.sum(dy * xs, axis=0, keepdims=True)

    blk = pl.BlockSpec((TM, D), lambda i: (i, 0))
    return pl.pallas_call(
        body, name="ssd_gate_norm_bwd", grid=(t // TM,),
        out_shape=(jax.ShapeDtypeStruct((t, D), f32), jax.ShapeDtypeStruct((t, D), MX), jax.ShapeDtypeStruct((8, D), f32)),
        in_specs=[blk, blk, blk, blk, _row(), _row()], out_specs=(blk, blk, pl.BlockSpec((8, D), lambda i: (0, 0))),
        compiler_params=_cp("arbitrary"),
    )(d_out, y_scan, xbc_act, proj, d_skip_row, ssd_norm_w)


def _inproj_bwd(d_z, d_xraw, d_bcraw, d_conf, d_dt, w_pack, x, mod, norm1_w, dx1, rider=None):
    t = x.shape[0]

    def body(dz_ref, dx_ref, dbc_ref, dcf_ref, ddt_ref, w_ref, x_ref, mod_ref, nw_ref, dx1_ref, gx_ref, st_ref):
        @pl.when(pl.program_id(0) == 0)
        def _():
            st_ref[...] = jnp.zeros_like(st_ref)

        nt = (((1,), (1,)), ((), ()))
        dot = lambda a, lo, hi: lax.dot_general(a, w_ref[:, lo:hi], nt, preferred_element_type=f32)
        dh = dot(dz_ref[...], OFF_Z, OFF_Z + D)
        dh = dh + dot(dx_ref[...], OFF_XBC, OFF_XBC + D)
        dh = dh + dot(dbc_ref[...], OFF_XBC + D, OFF_XBC + D_XBC)
        dh = dh + dot(dcf_ref[0], OFF_CA, OFF_CA + D)
        dh = dh + dot(dcf_ref[1], OFF_CG, OFF_CG + D)
        dh = dh + dot(ddt_ref[...].astype(MX), OFF_DT, OFF_DT + LANES)
        st_ref[3:4, 0:LANES] += jnp.sum(ddt_ref[...], axis=0, keepdims=True)
        xv = x_ref[...]
        rstd = lax.rsqrt(jnp.mean(xv * xv, axis=-1, keepdims=True) + 1e-6)
        xh = xv * rstd
        nw = nw_ref[...]
        sc = 1.0 + mod_ref[:, D:2 * D]
        st_ref[0:1, :] += jnp.sum(dh, axis=0, keepdims=True)
        st_ref[1:2, :] += jnp.sum(dh * xh * nw, axis=0, keepdims=True)
        st_ref[2:3, :] += jnp.sum(dh * sc * xh, axis=0, keepdims=True)
        dxh = dh * sc * nw
        gx_ref[...] = dx1_ref[...] + rstd * (dxh - xh * jnp.mean(dxh * xh, axis=-1, keepdims=True))

    blk = pl.BlockSpec((TM, D), lambda i: (i, 0))
    return _call(
        body, name="inproj_bwd", grid=(t // TM,),
        out_shape=(jax.ShapeDtypeStruct((t, D), f32), jax.ShapeDtypeStruct((8, D), f32)),
        in_specs=[blk, blk, pl.BlockSpec((TM, 2 * CW), lambda i: (i, 0)), pl.BlockSpec((2, TM, D), lambda i: (0, i, 0)),
                  pl.BlockSpec((TM, LANES), lambda i: (i, 0)), _resident((D, W_PACK)), blk, _row(6 * D), _row(), blk],
        out_specs=(blk, pl.BlockSpec((8, D), lambda i: (0, 0))),
        sem=("arbitrary",), args=(d_z, d_xraw, d_bcraw, d_conf, d_dt, w_pack, x, mod, norm1_w, dx1), rider=rider)


def _wgrad(a, d, name, bn=256):
    t, k = a.shape
    n = d.shape[1]
    out_dtype = MX

    def body(a_ref, d_ref, o_ref):
        o_ref[...] = lax.dot_general(a_ref[...], d_ref[...].astype(MX), (((0,), (0,)), ((), ())),
                                     preferred_element_type=f32).astype(out_dtype)

    return pl.pallas_call(
        body, name=name, grid=(n // bn,), out_shape=jax.ShapeDtypeStruct((k, n), out_dtype),
        in_specs=[_resident((t, k)), pl.BlockSpec((t, bn), lambda j: (0, j))],
        out_specs=pl.BlockSpec((k, bn), lambda j: (0, j)), compiler_params=_cp("arbitrary"),
    )(a, d)


def _wgrad_stacked(a, d, name, bn):
    out_dtype = MX
    t, k = a.shape
    s, _, n = d.shape
    nb = n // bn

    def body(a_ref, d_ref, o_ref):
        o_ref[0] = lax.dot_general(a_ref[...], d_ref[0], (((0,), (0,)), ((), ())),
                                   preferred_element_type=f32).astype(out_dtype)

    return pl.pallas_call(
        body, name=name, grid=(s, nb), out_shape=jax.ShapeDtypeStruct((s * nb, k, bn), out_dtype),
        in_specs=[_resident((t, k)), pl.BlockSpec((1, t, bn), lambda i, j: (i, 0, j))],
        out_specs=pl.BlockSpec((1, k, bn), lambda i, j: (i * nb + j, 0, 0)), compiler_params=_cp("arbitrary", "arbitrary"),
    )(a, d)


def _pad_row(v, width=LANES):
    return jnp.pad(v.reshape(1, -1), ((0, 0), (0, width - v.size)))


def _quarters(a):
    return a.reshape(4, 2, a.shape[0] // 8, a.shape[1])


def _local_step(x, mod, target, w_pack, late, small, reducer=None):
    dtb_row, alog_row = _pad_row(small["dt_bias"]), _pad_row(small["a_log"])
    dskip_row = jnp.repeat(small["d_skip"].reshape(-1), HEAD_P).reshape(1, D)

    red = reducer

    def hosted(host, args, swap=None, scatter=None, gather=None, sums=()):
        if red is None:
            return host(*args)[0]
        riders = ([red.scatter(scatter)] if scatter else []) + ([red.swap(*swap)] if swap else [])
        riders += [_SwapSumsRider([red.sums[n] for n in sums])] if sums else []
        riders += [_GatherRider([gather])] if gather is not None else []
        both = _Riders(riders)
        outs, extra = host(*args, rider=both)
        extra = both.split(extra)
        if scatter:
            red.scattered(scatter, extra.pop(0))
        if swap:
            red.swapped(swap[0], extra.pop(0))
        if sums:
            red.others.update(zip(sums, extra.pop(0)))
        return (outs, extra[0][0]) if gather is not None else outs

    proj, h = _ln_inproj(x, mod, small["norm1_w"], w_pack)
    xbc_act = _ssd_conv_fwd(proj, small["ssd_conv_w"], small["ssd_conv_b"])
    w_out, w_up, w_down = late
    if red is None:
        y_scan, states = hosted(_ssd_fwd, (xbc_act, proj, dtb_row, alog_row))
        u_conv, = hosted(_glu_conv_fwd, (proj, small["conf_conv_w"], small["conf_conv_b"]))
    else:
        (y_scan, states), w_up = hosted(_ssd_fwd, (xbc_act, proj, dtb_row, alog_row), gather=w_up)
        (u_conv,), w_out = hosted(_glu_conv_fwd, (proj, small["conf_conv_w"], small["conf_conv_b"]), gather=w_out)
        w_up, w_out = w_up.reshape(4, D, UP_SHARD), w_out.reshape(2 * D, D)
    y_ssd = _ssd_gate_norm(y_scan, xbc_act, proj, dskip_row, small["ssd_norm_w"])
    u = _ln_silu(u_conv, small["conf_ln_w"], small["conf_ln_b"])
    mix, x1, h2, up = _outproj_ln2_up(y_ssd, u, w_out, x, mod, small["norm2_w"], w_up)
    if red is None:
        act, = hosted(_ffn_conv_fwd, (up, small["ffn_conv_w"], small["ffn_conv_b"]))
    else:
        (act,), w_down = hosted(_ffn_conv_fwd, (up, small["ffn_conv_w"], small["ffn_conv_b"]), gather=w_down)
        w_down = w_down.reshape(D_FF, D)
    dx2, d_ffn, d_act, st_down = _down_loss(act, w_down, x1, mod, small["final_norm_w"], target)

    g_down = _quarters(_wgrad(act, d_ffn, "wgrad_down"))
    d_up, dw_ffn, db_ffn = hosted(_ffn_conv_bwd, (up, small["ffn_conv_w"], small["ffn_conv_b"], d_act), swap=("w_down", g_down))
    g_up = _wgrad_stacked(h2, d_up, "wgrad_up", D_FF // 2).reshape(4, 2, D // 2, UP_SHARD)
    dx1, d_mix, d_yssd, d_u, st_up = hosted(_up_bwd, (d_up, w_up, x1, mod, small["norm2_w"], dx2, mix, w_out),
                                            scatter="w_down", swap=("w_up", g_up))
    g_out = _quarters(jnp.concatenate([_wgrad(y_ssd, d_mix, "wgrad_out_y"), _wgrad(u, d_mix, "wgrad_out_u")], axis=0))
    d_uconv, st_ln = _ln_silu_bwd(d_u, u_conv, small["conf_ln_w"], small["conf_ln_b"])
    d_conf, dw_conf, db_conf = hosted(_glu_conv_bwd, (proj, small["conf_conv_w"], d_uconv), scatter="w_up",
                                      swap=("w_out", g_out))
    d_y, d_z, st_gn = _ssd_gate_norm_bwd(d_yssd, y_scan, xbc_act, proj, dskip_row, small["ssd_norm_w"])
    d_xs, d_bc, d_dt, d_alog = hosted(_ssd_bwd, (xbc_act, proj, dtb_row, alog_row, states, d_y), scatter="w_out")
    d_xraw, dw_sx, db_sx = _ssd_conv_bwd_x(proj, small["ssd_conv_w"], small["ssd_conv_b"], d_xs, d_y, dskip_row)
    d_bcraw, dw_sbc, db_sbc = _ssd_conv_bwd_bc(proj, small["ssd_conv_w"], small["ssd_conv_b"], d_bc)
    g_in = _unpack_g_in(dict(
        z=_wgrad(h, d_z, "wgrad_in_z"), x=_wgrad(h, d_xraw, "wgrad_in_x"), bc=_wgrad(h, d_bcraw, "wgrad_in_bc"),
        conf=_wgrad_stacked(h, d_conf, "wgrad_in_conf", D), dt=_wgrad(h, d_dt, "wgrad_in_dt", bn=LANES)))
    g_in = g_in.reshape(4, 2, D // 2, W_IN_SHARD_PAD)
    grad_x, st_in = hosted(_inproj_bwd, (d_z, d_xraw, d_bcraw, d_conf, d_dt, w_pack, x, mod, small["norm1_w"], dx1),
                           swap=("w_in", g_in), sums=("w_out", "w_up", "w_down"))

    gsmall = _pack_small_grads(st_in, st_up, st_down, st_ln, st_gn, d_alog, dw_sx, dw_sbc, db_sx, db_sbc, dw_conf, db_conf,
                               dw_ffn, db_ffn)
    gbig = None if reducer is not None else dict(w_in=g_in, w_out=g_out, w_up=g_up, w_down=g_down)
    return st_down[2, 0], grad_x, gbig, gsmall


VECTORS = ("ada_b", "norm1_w", "ssd_conv_b", "dt_bias", "a_log", "d_skip", "ssd_norm_w", "conf_conv_b", "conf_ln_w",
           "conf_ln_b", "norm2_w", "ffn_conv_b", "final_norm_w")
VECTOR_SIZES = (6 * D, D, D_XBC, HEADS, HEADS, HEADS, D, D, D, D, D, 2 * D_FF, D)
CONVS = {"ssd_conv_w": (K_SSD, D_XBC), "conf_conv_w": (K_CONF, D), "ffn_conv_w": (K_FFN, 2 * D_FF)}


def _pack_rows(items):
    n = -(-sum(w for _, w in items) // (8 * LANES)) * LANES
    while True:
        fill, place = [0] * 8, {}
        for key, w in sorted(items, key=lambda kv: -kv[1]):
            rows = [r for r in range(8) if fill[r] + w <= n]
            if not rows:
                break
            place[key] = (rows[0], fill[rows[0]])
            fill[rows[0]] += w
        if len(place) == len(items):
            return n, place
        n += LANES


FRONT_N, FRONT = _pack_rows([("c", D)] + [((nm, j), cols // 4) for nm, (taps, cols) in CONVS.items() for j in range(taps)])
BACK_N, BACK = _pack_rows([(nm, -(-sz // LANES) * LANES) for nm, sz in zip(VECTORS, VECTOR_SIZES)]
                          + [((nm, j), cols) for nm, (taps, cols) in CONVS.items() for j in range(taps)])
_VM = pltpu.CompilerParams(vmem_limit_bytes=VMEM_LIMIT)


def _pack_front(c, shards):
    def body(c_ref, *refs):
        o_ref = refs[-1]
        o_ref[...] = jnp.zeros_like(o_ref)
        r, o = FRONT["c"]
        o_ref[r:r + 1, o:o + D] = c_ref[...]
        for ref, (nm, (taps, cols)) in zip(refs, CONVS.items()):
            for j in range(taps):
                r, o = FRONT[(nm, j)]
                o_ref[r:r + 1, o:o + cols // 4] = ref[0, j:j + 1, :]

    return pl.pallas_call(body, name="pack_front", out_shape=jax.ShapeDtypeStruct((8, FRONT_N), f32),
                          compiler_params=_VM)(c, *shards)


def _unpack_front(got):
    def body(g_ref, c_ref, *outs):
        r, o = FRONT["c"]
        for d in range(8):
            c_ref[d:d + 1, :] = g_ref[8 * d + r:8 * d + r + 1, o:o + D]
        for ref, (nm, (taps, cols)) in zip(outs, CONVS.items()):
            cw = cols // 4
            for j in range(taps):
                r, o = FRONT[(nm, j)]
                for k in range(4):
                    ref[j:j + 1, k * cw:(k + 1) * cw] = g_ref[16 * k + r:16 * k + r + 1, o:o + cw]

    return pl.pallas_call(
        body, name="unpack_front", compiler_params=_VM,
        out_shape=(jax.ShapeDtypeStruct((8, D), f32),) + tuple(jax.ShapeDtypeStruct(tc, f32) for tc in CONVS.values()),
    )(got)


def _pack_small_grads(st_in, st_up, st_down, st_ln, st_gn, d_alog, dw_sx, dw_sbc, db_sx, db_sbc, dw_conf, db_conf, dw_ffn,
                      db_ffn):
    def body(in_ref, up_ref, dn_ref, ln_ref, gn_ref, al_ref, wx_ref, wbc_ref, bx_ref, bbc_ref, wc_ref, bc_ref, wf_ref, bf_ref,
             o_ref):
        def put(key, val, shift=0):
            r, o = BACK[key]
            o_ref[r:r + 1, o + shift:o + shift + val.shape[1]] = val

        o_ref[...] = jnp.zeros_like(o_ref)
        for i, piece in enumerate((in_ref[0:1, :], in_ref[1:2, :], up_ref[3:4, :], up_ref[0:1, :], up_ref[1:2, :],
                                   dn_ref[1:2, :])):
            put("ada_b", piece, i * D)
        put("norm1_w", in_ref[2:3, :])
        put("ssd_conv_b", bx_ref[...])
        put("ssd_conv_b", bbc_ref[...], D)
        put("dt_bias", in_ref[3:4, 0:LANES])
        put("a_log", al_ref[...])
        lane = lax.broadcasted_iota(jnp.int32, (1, LANES), 1)
        col = lax.broadcasted_iota(jnp.int32, (1, D), 1)
        per_col = gn_ref[1:2, :]
        d_skip = jnp.zeros((1, LANES), f32)
        for h in range(HEADS):
            in_head = jnp.logical_and(col >= h * HEAD_P, col < (h + 1) * HEAD_P)
            s = jnp.sum(jnp.where(in_head, per_col, 0.0), axis=1, keepdims=True)
            d_skip = d_skip + jnp.where(lane == h, s, 0.0)
        put("d_skip", d_skip)
        put("ssd_norm_w", gn_ref[0:1, :])
        put("conf_conv_b", bc_ref[...])
        put("conf_ln_w", ln_ref[0:1, :])
        put("conf_ln_b", ln_ref[1:2, :])
        put("norm2_w", up_ref[2:3, :])
        put("ffn_conv_b", bf_ref[0])
        put("ffn_conv_b", bf_ref[1], D_FF)
        put("final_norm_w", dn_ref[0:1, :])
        for j in range(K_SSD):
            put(("ssd_conv_w", j), wx_ref[j:j + 1, :])
            put(("ssd_conv_w", j), wbc_ref[j:j + 1, :], D)
        for j in range(K_CONF):
            put(("conf_conv_w", j), wc_ref[j:j + 1, :])
        for j in range(K_FFN):
            put(("ffn_conv_w", j), wf_ref[0, j:j + 1, :])
            put(("ffn_conv_w", j), wf_ref[1, j:j + 1, :], D_FF)

    return pl.pallas_call(body, name="pack_small_grads", out_shape=jax.ShapeDtypeStruct((8, BACK_N), f32), compiler_params=_VM)(
        st_in, st_up, st_down, st_ln, st_gn, d_alog, dw_sx, dw_sbc, db_sx, db_sbc, dw_conf, db_conf, dw_ffn, db_ffn)


def _small_adamw(got, chip, w, m, v):
    names = VECTORS + tuple(CONVS)
    n_par = len(names)

    def body(chip_ref, g_ref, *refs):
        ins, outs = refs[:3 * n_par], refs[3 * n_par:]
        dm_ref, outs = outs[0], outs[1:]
        chip_id = chip_ref[0]

        def summed(key, width):
            r, o = BACK[key]
            s = g_ref[r:r + 1, o:o + width]
            for d in range(1, 8):
                s = s + g_ref[8 * d + r:8 * d + r + 1, o:o + width]
            return s

        def mine(full, cw):
            out = full[:, 0:cw]
            for k in range(1, 4):
                out = jnp.where(chip_id == k, full[:, k * cw:(k + 1) * cw], out)
            return out

        r, o = BACK["ada_b"]
        for d in range(8):
            dm_ref[d:d + 1, :] = mine(g_ref[8 * d + r:8 * d + r + 1, o:o + 6 * D], 6 * D // 4)
        for i, (nm, size) in enumerate(zip(VECTORS, VECTOR_SIZES)):
            g = summed(nm, -(-size // LANES) * LANES)[:, 0:size]
            res = _adam_math(ins[3 * i][...], g, ins[3 * i + 1][...], ins[3 * i + 2][...])
            for ref, val in zip(outs[4 * i:4 * i + 4], (g,) + res):
                ref[...] = val
        for i, (nm, (taps, cols)) in enumerate(CONVS.items(), start=len(VECTORS)):
            for j in range(taps):
                g = mine(summed((nm, j), cols), cols // 4)
                res = _adam_math(ins[3 * i][0, j:j + 1, :], g, ins[3 * i + 1][0, j:j + 1, :], ins[3 * i + 2][0, j:j + 1, :])
                for ref, val in zip(outs[4 * i:4 * i + 4], (g,) + res):
                    ref[0, j:j + 1, :] = val

    params = [a[nm] for nm in names for a in (w, m, v)]
    whole = lambda s: pl.BlockSpec(s, lambda i, chip, nd=len(s): (0,) * nd)
    out_shape = [jax.ShapeDtypeStruct((8, 6 * D // 4), f32)] + [jax.ShapeDtypeStruct(w[nm].shape, f32) for nm in names for _ in range(4)]
    outs = pl.pallas_call(
        body, name="small_adamw", out_shape=tuple(out_shape), compiler_params=_VM,
        grid_spec=pltpu.PrefetchScalarGridSpec(
            num_scalar_prefetch=1, grid=(1,), in_specs=[whole(got.shape)] + [whole(p.shape) for p in params],
            out_specs=tuple(whole(s.shape) for s in out_shape)),
    )(_scalar(chip), got, *params)
    return outs[0], {nm: outs[1 + 4 * i:5 + 4 * i] for i, nm in enumerate(names)}


W_IN_COLS = 4624
W_IN_SHARD = W_IN_COLS // 4
W_IN_SHARD_PAD = 1280
W_IN_RIDES = (192, 192, 128)
_SEGMENTS = ((0, 1024, OFF_Z), (1024, 2560, OFF_XBC), (2560, 2576, OFF_DT), (2576, 3600, OFF_CA), (3600, 4624, OFF_CG))


def _in_pieces(bounds=()):
    out = []
    for k in range(4):
        s0, s1 = k * W_IN_SHARD, (k + 1) * W_IN_SHARD
        for lo, hi, off in _SEGMENTS:
            a, b = max(lo, s0), min(hi, s1)
            while a < b:
                p = off + a - lo
                e = min([b - a] + [c - p for c in bounds if c > p])
                out.append((k, a - s0, p, e))
                a += e
    return out


def _pack_w_in(shards):
    pieces = _in_pieces()

    def body(s_ref, o_ref):
        o_ref[:, OFF_DT:W_PACK] = jnp.zeros((TM, W_PACK - OFF_DT), MX)
        for k, c, p, n in pieces:
            o_ref[:, p:p + n] = s_ref[k, :, c:c + n]

    return pl.pallas_call(
        body, name="pack_w_in", grid=(D // TM,), out_shape=jax.ShapeDtypeStruct((D, W_PACK), MX),
        in_specs=[pl.BlockSpec((4, TM, W_IN_SHARD_PAD), lambda i: (0, i, 0))],
        out_specs=pl.BlockSpec((TM, W_PACK), lambda i: (i, 0)), compiler_params=_cp("arbitrary"),
    )(shards)


def _unpack_g_in(g):
    srcs = ((OFF_Z, D), (OFF_XBC, D), (OFF_XBC + D, 2 * CW), (OFF_CA, D), (OFF_CG, D), (OFF_DT, LANES))
    pieces = _in_pieces(tuple(o for o, _ in srcs) + tuple(o + n for o, n in srcs))

    def body(z_ref, x_ref, bc_ref, cf_ref, dt_ref, o_ref):
        read = (lambda lo, hi: z_ref[:, lo:hi], lambda lo, hi: x_ref[:, lo:hi], lambda lo, hi: bc_ref[:, lo:hi],
                lambda lo, hi: cf_ref[0, :, lo:hi], lambda lo, hi: cf_ref[1, :, lo:hi], lambda lo, hi: dt_ref[:, lo:hi])
        o_ref[:, :, W_IN_SHARD - 4:W_IN_SHARD_PAD] = jnp.zeros((4, TM, W_IN_SHARD_PAD - W_IN_SHARD + 4), MX)
        for k, c, p, n in pieces:
            i = [q for q, (o, w) in enumerate(srcs) if o <= p < o + w][0]
            o_ref[k, :, c:c + n] = read[i](p - srcs[i][0], p - srcs[i][0] + n)

    blk = lambda w: pl.BlockSpec((TM, w), lambda i: (i, 0))
    return pl.pallas_call(
        body, name="unpack_g_in", grid=(D // TM,), out_shape=jax.ShapeDtypeStruct((4, D, W_IN_SHARD_PAD), MX),
        in_specs=[blk(D), blk(D), blk(2 * CW), pl.BlockSpec((2, TM, D), lambda i: (0, i, 0)), blk(LANES)],
        out_specs=pl.BlockSpec((4, TM, W_IN_SHARD_PAD), lambda i: (0, i, 0)), compiler_params=_cp("arbitrary"),
    )(g["z"], g["x"], g["bc"], g["conf"], g["dt"])


def _scalar(v):
    return jnp.reshape(v, (1,)).astype(jnp.int32)


def _cast_into_slot(w, width, chip):
    r, c = w.shape
    h = r // 2
    tm = _row_tile(h)
    nj = h // tm

    def body(chip_ref, w_ref, o_ref):
        v = w_ref[...].astype(MX)
        o_ref[0, 0] = v if width == c else jnp.concatenate([v, jnp.zeros((tm, width - c), MX)], axis=1)

    return pl.pallas_call(
        body, name=f"cast_into_slot_{r}x{c}", out_shape=jax.ShapeDtypeStruct((4, 2, h, width), MX),
        grid_spec=pltpu.PrefetchScalarGridSpec(
            num_scalar_prefetch=1, grid=(2, nj),
            in_specs=[pl.BlockSpec((tm, c), lambda i, j, chip: (i * nj + j, 0))],
            out_specs=pl.BlockSpec((1, 1, tm, width), lambda i, j, chip: (chip[0], i, j, 0))),
        compiler_params=_cp("arbitrary", "arbitrary"),
    )(_scalar(chip), w)


ANY = pl.BlockSpec(memory_space=pl.ANY)


def _place():
    x, y, c = lax.axis_index("x"), lax.axis_index("y"), lax.axis_index("c")
    return x, y, c, [(1 - x, y), (x, 1 - y), (1 - x, 1 - y)]


def _gather_rows(block):
    m_per, n = block.shape

    def body(x_ref, out_ref, send_sems, recv_sems, local_sem):
        x, y, c, chips = _place()
        me, sibling = (x, y, c), (x, y, 1 - c)

        def rows(px, py, pc):
            return out_ref.at[pl.ds((4 * px + 2 * py + pc) * m_per, m_per), :]

        def copy(k, blk, to, src=None):
            return pltpu.make_async_remote_copy(
                src_ref=rows(*blk) if src is None else src, dst_ref=rows(*blk), send_sem=send_sems.at[k],
                recv_sem=recv_sems.at[k], device_id=to, device_id_type=MESH)

        mine = pltpu.make_async_copy(x_ref, rows(*me), local_sem)
        mine.start()
        first = [copy(0, me, sibling, src=x_ref)]
        first += [copy(1 + j, me, (*chip, c), src=x_ref) for j, chip in enumerate(chips)]
        for cp in first:
            cp.start()
        passed = [copy(4 + j, (*chip, c), sibling) for j, chip in enumerate(chips)]
        for j, chip in enumerate(chips):
            copy(1 + j, (*chip, c), me).wait_recv()
            passed[j].start()
        copy(0, sibling, me).wait_recv()
        for j, chip in enumerate(chips):
            copy(4 + j, (*chip, 1 - c), me).wait_recv()
        for cp in first + passed:
            cp.wait_send()
        mine.wait()

    return pl.pallas_call(
        body, name=f"gather_rows_{m_per}x{n}", out_shape=jax.ShapeDtypeStruct((8 * m_per, n), block.dtype),
        in_specs=[pl.BlockSpec(memory_space=pltpu.VMEM)], out_specs=pl.BlockSpec(memory_space=pltpu.VMEM),
        scratch_shapes=[pltpu.SemaphoreType.DMA((7,)), pltpu.SemaphoreType.DMA((7,)), pltpu.SemaphoreType.DMA],
        compiler_params=pltpu.CompilerParams(vmem_limit_bytes=VMEM_LIMIT),
    )(block)


class _GatherRider:
    def __init__(self, slots):
        n = len(slots)
        self.n = n
        self.inputs = list(slots)
        self.out_shape = [jax.ShapeDtypeStruct(s.shape, s.dtype) for s in slots]
        self.scratch = [pltpu.SemaphoreType.DMA((n, 6)), pltpu.SemaphoreType.DMA((n, 6))]
        self.aliases = {a: a for a in range(n)}

    @staticmethod
    def _copy(outs, sems, a, j, k, half, to):
        dst = outs[a].at[k, half]
        return pltpu.make_async_remote_copy(src_ref=dst, dst_ref=dst, send_sem=sems[0].at[a, j], recv_sem=sems[1].at[a, j],
                                            device_id=to, device_id_type=MESH)

    def _first(self, outs, sems):
        x, y, c, chips = _place()
        return [self._copy(outs, sems, a, j, 2 * x + y, c, (*chip, c)) for a in range(self.n) for j, chip in enumerate(chips)]

    def start(self, ins, outs, sems):
        for cp in self._first(outs, sems):
            cp.start()

    def finish(self, ins, outs, sems):
        x, y, c, chips = _place()
        passed = []
        for a in range(self.n):
            for j, (px, py) in enumerate(chips):
                self._copy(outs, sems, a, j, 2 * px + py, c, (x, y, c)).wait_recv()
                fwd = self._copy(outs, sems, a, 3 + j, 2 * px + py, c, (x, y, 1 - c))
                fwd.start()
                passed.append(fwd)
        for a in range(self.n):
            for j, (px, py) in enumerate(chips):
                self._copy(outs, sems, a, 3 + j, 2 * px + py, 1 - c, (x, y, c)).wait_recv()
        for cp in self._first(outs, sems) + passed:
            cp.wait_send()


class _ScatterRider:
    def __init__(self, parts, row0=0, nrows=None):
        n = len(parts)
        self.n = n
        self.rows = (row0, parts[0].shape[1] - row0 if nrows is None else nrows)
        self.inputs = list(parts)
        self.out_shape = [jax.ShapeDtypeStruct((3, self.rows[1], p.shape[2]), p.dtype) for p in parts]
        self.scratch = [pltpu.SemaphoreType.DMA((n, 3)), pltpu.SemaphoreType.DMA((n, 3))]
        self.aliases = {}

    def _copies(self, ins, outs, sems):
        x, y, c, chips = _place()
        return [pltpu.make_async_remote_copy(
            src_ref=ins[a].at[2 * px + py, pl.ds(*self.rows)], dst_ref=outs[a].at[j], send_sem=sems[0].at[a, j],
            recv_sem=sems[1].at[a, j], device_id=(px, py, c), device_id_type=MESH)
            for a in range(self.n) for j, (px, py) in enumerate(chips)]

    def start(self, ins, outs, sems):
        for cp in self._copies(ins, outs, sems):
            cp.start()

    def finish(self, ins, outs, sems):
        for cp in self._copies(ins, outs, sems):
            cp.wait()


def _ride_alone(rider, name):
    n = len(rider.inputs)

    def body(*refs):
        ins, outs, sems = refs[:n], refs[n:n + len(rider.out_shape)], refs[n + len(rider.out_shape):]
        rider.start(ins, outs, sems)
        rider.finish(ins, outs, sems)

    return pl.pallas_call(
        body, name=name, out_shape=tuple(rider.out_shape), in_specs=[ANY] * n, out_specs=tuple([ANY] * len(rider.out_shape)),
        input_output_aliases=dict(rider.aliases), scratch_shapes=list(rider.scratch),
    )(*rider.inputs)


class _SwapRider:
    def __init__(self, grads):
        n = len(grads)
        self.n = n
        self.inputs = list(grads)
        self.out_shape = [jax.ShapeDtypeStruct((4,) + g.shape[2:], g.dtype) for g in grads]
        self.scratch = [pltpu.SemaphoreType.DMA((n, 4)), pltpu.SemaphoreType.DMA((n, 4))]
        self.aliases = {}

    def _copies(self, ins, outs, sems):
        x, y, c, _ = _place()
        return [pltpu.make_async_remote_copy(
            src_ref=ins[a].at[k, 1 - c], dst_ref=outs[a].at[k], send_sem=sems[0].at[a, k], recv_sem=sems[1].at[a, k],
            device_id=(x, y, 1 - c), device_id_type=MESH) for a in range(self.n) for k in range(4)]

    def start(self, ins, outs, sems):
        for cp in self._copies(ins, outs, sems):
            cp.start()

    def finish(self, ins, outs, sems):
        for cp in self._copies(ins, outs, sems):
            cp.wait()


class _Riders:
    def __init__(self, riders):
        self.riders = list(riders)
        self.inputs = [a for r in riders for a in r.inputs]
        self.out_shape = [s for r in riders for s in r.out_shape]
        self.scratch = [s for r in riders for s in r.scratch]
        self.aliases = {}
        i = o = 0
        for r in riders:
            self.aliases.update({i + a: o + b for a, b in r.aliases.items()})
            i, o = i + len(r.inputs), o + len(r.out_shape)

    def _each(self, ins, outs, sems):
        i = o = s = 0
        for r in self.riders:
            yield r, ins[i:i + len(r.inputs)], outs[o:o + len(r.out_shape)], sems[s:s + len(r.scratch)]
            i, o, s = i + len(r.inputs), o + len(r.out_shape), s + len(r.scratch)

    def start(self, ins, outs, sems):
        for r, a, b, c in self._each(ins, outs, sems):
            r.start(a, b, c)

    def finish(self, ins, outs, sems):
        for r, a, b, c in self._each(ins, outs, sems):
            r.finish(a, b, c)

    def split(self, outs):
        res, o = [], 0
        for r in self.riders:
            res.append(outs[o:o + len(r.out_shape)])
            o += len(r.out_shape)
        return res


class _Reducer:
    def __init__(self, chip, core):
        self.chip, self.core, self.grads, self.parts, self.sums, self.others = chip, core, {}, {}, {}, {}

    def swap(self, name, grad):
        self.grads[name] = grad
        return _SwapRider([grad])

    def swapped(self, name, got):
        self.parts[name] = _add_pair(self.grads[name], got[0], self.core, name)

    def scatter(self, name, row0=0, nrows=None):
        return _ScatterRider([self.parts[name]], row0, nrows)

    def scattered(self, name, others):
        self.sums[name] = _add_chips(self.parts[name], others[0], self.chip, name)

    def scattered_rows(self, name, pieces):
        self.sums[name] = jnp.concatenate(
            [_add_chips(self.parts[name], others[0], self.chip, f"{name}_{row0}", row0) for row0, others in pieces], axis=0)


class _SwapSumsRider:
    def __init__(self, halves):
        n = len(halves)
        self.n = n
        self.inputs = list(halves)
        self.out_shape = [jax.ShapeDtypeStruct(s.shape, s.dtype) for s in halves]
        self.scratch = [pltpu.SemaphoreType.DMA((n,)), pltpu.SemaphoreType.DMA((n,))]
        self.aliases = {}

    def _copies(self, ins, outs, sems):
        x, y, c, _ = _place()
        return [pltpu.make_async_remote_copy(
            src_ref=ins[a], dst_ref=outs[a], send_sem=sems[0].at[a], recv_sem=sems[1].at[a],
            device_id=(x, y, 1 - c), device_id_type=MESH) for a in range(self.n)]

    def start(self, ins, outs, sems):
        for cp in self._copies(ins, outs, sems):
            cp.start()

    def finish(self, ins, outs, sems):
        for cp in self._copies(ins, outs, sems):
            cp.wait()


def _row_tile(r):
    for tm in (TM, 176, 128, 64, 32, 16, 8):
        if r % tm == 0:
            return tm
    return r


def _add_pair(mine, got, core, name):
    k, _, h, c = mine.shape
    tm = _row_tile(h)

    def body(core_ref, a_ref, b_ref, o_ref):
        o_ref[0] = (a_ref[0, 0].astype(f32) + b_ref[0].astype(f32)).astype(MX)

    blk = pl.BlockSpec((1, tm, c), lambda i, j, core: (i, j, 0))
    return pl.pallas_call(
        body, name="add_pair_" + name, out_shape=jax.ShapeDtypeStruct((k, h, c), MX),
        grid_spec=pltpu.PrefetchScalarGridSpec(
            num_scalar_prefetch=1, grid=(k, h // tm),
            in_specs=[pl.BlockSpec((1, 1, tm, c), lambda i, j, core: (i, core[0], j, 0)), blk], out_specs=blk),
        compiler_params=_cp("arbitrary", "arbitrary"),
    )(_scalar(core), mine, got)


def _add_chips(parts, others, chip, name, row0=0):
    _, n, c = others.shape
    tm = _row_tile(n)
    assert row0 % tm == 0
    i0 = row0 // tm

    def body(chip_ref, a_ref, b_ref, o_ref):
        s = a_ref[0].astype(f32) + b_ref[0].astype(f32)
        o_ref[...] = (s + b_ref[1].astype(f32)) + b_ref[2].astype(f32)

    return pl.pallas_call(
        body, name="add_chips_" + name, out_shape=jax.ShapeDtypeStruct((n, c), f32),
        grid_spec=pltpu.PrefetchScalarGridSpec(
            num_scalar_prefetch=1, grid=(n // tm,),
            in_specs=[pl.BlockSpec((1, tm, c), lambda i, chip: (chip[0], i0 + i, 0)),
                      pl.BlockSpec((3, tm, c), lambda i, chip: (0, i, 0))],
            out_specs=pl.BlockSpec((tm, c), lambda i, chip: (i, 0))),
        compiler_params=_cp("arbitrary"),
    )(_scalar(chip), parts, others)


def _adam_math(w, g, m, v):
    m = ADAM_B1 * m + (1.0 - ADAM_B1) * g
    v = ADAM_B2 * v + (1.0 - ADAM_B2) * (g * g)
    m_hat = m / (1.0 - ADAM_B1 ** ADAM_STEP)
    v_hat = v / (1.0 - ADAM_B2 ** ADAM_STEP)
    return -ADAM_LR * (m_hat / (jnp.sqrt(v_hat) + ADAM_EPS) + ADAM_WD * w), m, v


def _adamw_halves(w, mine, other, m, v, core, name, rider=None):
    r, c = w.shape
    h = r // 2
    tm = _row_tile(h)
    nj = h // tm
    cg = mine.shape[1]

    def body(core_ref, w_ref, a_ref, b_ref, m_ref, v_ref, g_ref, d_ref, nm_ref, nv_ref):
        g = jnp.where(pl.program_id(0) == core_ref[0], a_ref[:, 0:c], b_ref[:, 0:c])
        g_ref[...] = g
        d_ref[...], nm_ref[...], nv_ref[...] = _adam_math(w_ref[...], g, m_ref[...], v_ref[...])

    blk = pl.BlockSpec((tm, c), lambda i, j, core: (i * nj + j, 0))
    gblk = pl.BlockSpec((tm, cg), lambda i, j, core: (j, 0))
    return _call(body, name=name, grid=(2, nj), out_shape=[jax.ShapeDtypeStruct((r, c), f32)] * 4,
                 in_specs=[blk, gblk, gblk, blk, blk], out_specs=(blk,) * 4, sem=("arbitrary", "arbitrary"),
                 prefetch=(_scalar(core),), args=(w, mine, other, m, v), rider=rider)


def _ada_forward(c_all, ada_w):
    def body(c_ref, w_ref, o_ref):
        o_ref[...] = jnp.dot(_silu(c_ref[...]).astype(MX), w_ref[...].astype(MX), preferred_element_type=f32)

    return pl.pallas_call(body, name="ada_forward", out_shape=jax.ShapeDtypeStruct((8, ada_w.shape[1]), f32),
                          compiler_params=pltpu.CompilerParams(vmem_limit_bytes=VMEM_LIMIT))(c_all, ada_w)


def _ada_adamw(c_all_t, d_mod, w, m, v, rider=None):
    r, c = w.shape
    tm = TM

    def body(ct_ref, dm_ref, w_ref, m_ref, v_ref, g_ref, d_ref, nm_ref, nv_ref):
        ca = _silu(ct_ref[...])
        g = ca[:, 0:1] * dm_ref[0:1, :]
        for b in range(1, 8):
            g = g + ca[:, b:b + 1] * dm_ref[b:b + 1, :]
        g_ref[...] = g
        d_ref[...], nm_ref[...], nv_ref[...] = _adam_math(w_ref[...], g, m_ref[...], v_ref[...])

    blk = pl.BlockSpec((tm, c), lambda i: (i, 0))
    return _call(body, name="ada_adamw", grid=(r // tm,), out_shape=[jax.ShapeDtypeStruct((r, c), f32)] * 4,
                 in_specs=[pl.BlockSpec((tm, 8), lambda i: (i, 0)), pl.BlockSpec((8, c), lambda i: (0, 0)), blk, blk, blk],
                 out_specs=(blk,) * 4, sem=("arbitrary",), args=(c_all_t, d_mod, w, m, v), rider=rider)


WEIGHTS = ("ada_w", "ada_b", "norm1_w", "w_in", "ssd_conv_w", "ssd_conv_b", "dt_bias", "a_log", "d_skip", "ssd_norm_w",
           "conf_conv_w", "conf_conv_b", "conf_ln_w", "conf_ln_b", "w_out", "norm2_w", "w_up", "ffn_conv_w", "ffn_conv_b",
           "w_down", "final_norm_w")


def kernel(x, c, ada_w, ada_b, norm1_w, w_in, ssd_conv_w, ssd_conv_b, dt_bias, a_log, d_skip, ssd_norm_w, conf_conv_w, conf_conv_b, conf_ln_w, conf_ln_b, w_out, norm2_w, w_up, ffn_conv_w, ffn_conv_b, w_down, final_norm_w, loss_target, m_ada_w, m_ada_b, m_norm1_w, m_w_in, m_ssd_conv_w, m_ssd_conv_b, m_dt_bias, m_a_log, m_d_skip, m_ssd_norm_w, m_conf_conv_w, m_conf_conv_b, m_conf_ln_w, m_conf_ln_b, m_w_out, m_norm2_w, m_w_up, m_ffn_conv_w, m_ffn_conv_b, m_w_down, m_final_norm_w, v_ada_w, v_ada_b, v_norm1_w, v_w_in, v_ssd_conv_w, v_ssd_conv_b, v_dt_bias, v_a_log, v_d_skip, v_ssd_norm_w, v_conf_conv_w, v_conf_conv_b, v_conf_ln_w, v_conf_ln_b, v_w_out, v_norm2_w, v_w_up, v_ffn_conv_w, v_ffn_conv_b, v_w_down, v_final_norm_w):
    given = dict(locals())
    w = {n: given[n] for n in WEIGHTS}
    mom = {n: given["m_" + n] for n in WEIGHTS}
    var = {n: given["v_" + n] for n in WEIGHTS}
    chip = 2 * lax.axis_index("x") + lax.axis_index("y")
    me = 2 * chip + lax.axis_index("c")

    c_all, *convs = _unpack_front(_gather_rows(_pack_front(c, [w[n] for n in CONVS])))
    conv_full = dict(zip(CONVS, convs))

    mod_cols = _gather_rows(_ada_forward(c_all, ada_w[0])).reshape(8, 8, -1)[0::2]
    mod = lax.dynamic_index_in_dim(mod_cols, me, axis=1, keepdims=False).reshape(1, 6 * D) + ada_b

    core = lax.axis_index("c")
    a_in, = _ride_alone(_GatherRider([_cast_into_slot(w_in[0], W_IN_SHARD_PAD, chip)]), "gather_w_in")
    w_pack = _pack_w_in(a_in.reshape(4, D, W_IN_SHARD_PAD))
    late = (_cast_into_slot(w_out[0], D, chip), _cast_into_slot(w_up[0], UP_SHARD, chip), _cast_into_slot(w_down[0], D, chip))

    flat = lambda a: a.reshape(1, -1) if a.ndim == 1 else a
    small = {n: flat(w[n]) for n in VECTORS if n != "ada_b"}
    small.update(conv_full)
    reducer = _Reducer(chip, core)
    loss_mine, grad_x, _, gsmall = _local_step(x[0], mod, loss_target[0], w_pack, late, small, reducer)
    loss = lax.psum(loss_mine, ("x", "y", "c"))
    grads, delta, new_m, new_v = {}, {}, {}, {}

    names = VECTORS + tuple(CONVS)
    d_mod_mine, res = _small_adamw(_gather_rows(gsmall), chip, *[{n: flat(d[n]) for n in names} for d in (w, mom, var)])
    for n in names:
        grads[n], delta[n], new_m[n], new_v[n] = [r.reshape(w[n].shape) for r in res[n]]

    def adamw(n, rider=None):
        res, others = _adamw_halves(w[n][0], reducer.sums[n], reducer.others[n], mom[n][0], var[n][0], core, "adamw_" + n, rider)
        grads[n], delta[n], new_m[n], new_v[n] = [r[None] for r in res]
        return others

    pieces = [(0, adamw("w_up", reducer.scatter("w_in", 0, W_IN_RIDES[0])))]
    res, others = _ada_adamw(c_all.T, d_mod_mine, ada_w[0], m_ada_w[0], v_ada_w[0], reducer.scatter("w_in", *W_IN_RIDES[0:2]))
    grads["ada_w"], delta["ada_w"], new_m["ada_w"], new_v["ada_w"] = [r[None] for r in res]
    pieces += [(W_IN_RIDES[0], others), (sum(W_IN_RIDES[0:2]), adamw("w_down", reducer.scatter("w_in", sum(W_IN_RIDES[0:2]), W_IN_RIDES[2])))]
    adamw("w_out")
    reducer.scattered_rows("w_in", pieces)
    reducer.others["w_in"], = _ride_alone(_SwapSumsRider([reducer.sums["w_in"]]), "swap_sums_w_in")
    adamw("w_in")

    return (loss, grad_x[None], *[grads[n] for n in WEIGHTS], *[delta[n] for n in WEIGHTS],
            *[new_m[n] for n in WEIGHTS], *[new_v[n] for n in WEIGHTS])
```

```python
import functools

import jax
import jax.numpy as jnp
from jax import lax
from jax.experimental import pallas as pl
from jax.experimental.pallas import tpu as pltpu

f32 = jnp.float32
MX = jnp.bfloat16

D = 1024
HEADS = 16
HEAD_P = 64
STATE_N = 128
D_XBC = 1536
D_FF = 2816
UP_SHARD = 2 * D_FF // 4
K_SSD, K_CONF, K_FFN = 4, 31, 3
CHUNK = 128
OFF_Z, OFF_XBC, OFF_CA, OFF_CG, OFF_DT = 0, 1024, 2560, 3584, 4608
W_PACK = 4736
TM = 256
CW = 256
RC = 64
LANES = 128
VMEM_LIMIT = 56 * 1024 * 1024

ADAM_LR, ADAM_B1, ADAM_B2, ADAM_EPS, ADAM_WD, ADAM_STEP = 0.001, 0.9, 0.999, 1e-08, 0.01, 10

MESH = pl.DeviceIdType.MESH


def _cp(*sem):
    return pltpu.CompilerParams(dimension_semantics=sem, vmem_limit_bytes=VMEM_LIMIT)


def _resident(shape):
    nd = len(shape)
    return pl.BlockSpec(shape, lambda *_: (0,) * nd, pipeline_mode=pl.Buffered(1))


def _row(width=D):
    return pl.BlockSpec((1, width), lambda *_: (0, 0))


def _call(body, *, name, grid, in_specs, out_specs, out_shape, args, sem, scratch_shapes=(), prefetch=(), rider=None):
    ni, no, ns, npf = len(in_specs), len(out_specs), len(scratch_shapes), len(prefetch)
    ri, ro = (len(rider.inputs), len(rider.out_shape)) if rider is not None else (0, 0)

    def full(*refs):
        pre, refs = refs[:npf], refs[npf:]
        base_in, r_in = refs[:ni], refs[ni:ni + ri]
        base_out, r_out = refs[ni + ri:ni + ri + no], refs[ni + ri + no:ni + ri + no + ro]
        base_scr, r_scr = refs[ni + ri + no + ro:ni + ri + no + ro + ns], refs[ni + ri + no + ro + ns:]
        if rider is None:
            return body(*pre, *base_in, *base_out, *base_scr)
        ids = [pl.program_id(a) for a in range(len(grid))]
        first = functools.reduce(jnp.logical_and, [i == 0 for i in ids])
        last = functools.reduce(jnp.logical_and, [i == g - 1 for i, g in zip(ids, grid)])

        @pl.when(first)
        def _():
            rider.start(r_in, r_out, r_scr)

        body(*pre, *base_in, *base_out, *base_scr)

        @pl.when(last)
        def _():
            rider.finish(r_in, r_out, r_scr)

    extra = dict(shapes=[], scratch=[], aliases={}, inputs=[]) if rider is None else dict(
        shapes=rider.out_shape, scratch=rider.scratch, inputs=rider.inputs,
        aliases={npf + ni + i: no + j for i, j in rider.aliases.items()})
    outs = pl.pallas_call(
        full, name=name, out_shape=tuple(out_shape) + tuple(extra["shapes"]), input_output_aliases=extra["aliases"],
        grid_spec=pltpu.PrefetchScalarGridSpec(
            num_scalar_prefetch=npf, grid=grid, in_specs=list(in_specs) + [ANY] * ri,
            out_specs=tuple(out_specs) + (ANY,) * ro, scratch_shapes=list(scratch_shapes) + list(extra["scratch"])),
        compiler_params=_cp(*sem),
    )(*prefetch, *args, *extra["inputs"])
    return tuple(outs[:no]), tuple(outs[no:])


def _silu(v):
    return v * jax.nn.sigmoid(v)


def _dsilu(v):
    s = jax.nn.sigmoid(v)
    return s * (1.0 + v * (1.0 - s))


def _softplus(v):
    return jnp.maximum(v, 0.0) + jnp.log1p(jnp.exp(-jnp.abs(v)))


def _mm(a, b):
    return jnp.dot(a.astype(MX), b.astype(MX), preferred_element_type=f32)


def _mm_nt(a, b):
    return lax.dot_general(a.astype(MX), b.astype(MX), (((1,), (1,)), ((), ())), preferred_element_type=f32)


def _mm_tn(a, b):
    return lax.dot_general(a.astype(MX), b.astype(MX), (((0,), (0,)), ((), ())), preferred_element_type=f32)


def _ln_inproj(x, mod, norm1_w, w_pack):
    t = x.shape[0]

    def body(x_ref, mod_ref, nw_ref, w_ref, proj_ref, ht_ref):
        xv = x_ref[...]
        rstd = lax.rsqrt(jnp.mean(xv * xv, axis=-1, keepdims=True) + 1e-6)
        h = (xv * rstd * nw_ref[...]) * (1.0 + mod_ref[:, D:2 * D]) + mod_ref[:, 0:D]
        hb = h.astype(MX)
        ht_ref[...] = hb.T
        proj_ref[...] = jnp.dot(hb, w_ref[...], preferred_element_type=f32)

    return pl.pallas_call(
        body, name="ln_inproj", grid=(t // TM,),
        out_shape=(jax.ShapeDtypeStruct((t, W_PACK), f32), jax.ShapeDtypeStruct((D, t), MX)),
        in_specs=[pl.BlockSpec((TM, D), lambda i: (i, 0)), _row(6 * D), _row(), _resident((D, W_PACK))],
        out_specs=(pl.BlockSpec((TM, W_PACK), lambda i: (i, 0)), pl.BlockSpec((D, TM), lambda i: (0, i))),
        compiler_params=_cp("arbitrary"),
    )(x, mod, norm1_w, w_pack)


def _ssd_gate_norm(y_scan, xbc_act, proj, d_skip_row, ssd_norm_w):
    t = y_scan.shape[0]

    def body(y_ref, xs_ref, z_ref, dsk_ref, nw_ref, o_ref, ot_ref):
        y = y_ref[...] + xs_ref[...] * dsk_ref[...]
        yz = y * _silu(z_ref[...])
        rstd = lax.rsqrt(jnp.mean(yz * yz, axis=-1, keepdims=True) + 1e-6)
        out = (yz * rstd * nw_ref[...]).astype(MX)
        o_ref[...] = out
        ot_ref[...] = out.T

    blk = pl.BlockSpec((TM, D), lambda i: (i, 0))
    return pl.pallas_call(
        body, name="ssd_gate_norm", grid=(t // TM,),
        out_shape=(jax.ShapeDtypeStruct((t, D), MX), jax.ShapeDtypeStruct((D, t), MX)),
        in_specs=[blk, blk, blk, _row(), _row()], out_specs=(blk, pl.BlockSpec((D, TM), lambda i: (0, i))),
        compiler_params=_cp("arbitrary"),
    )(y_scan, xbc_act, proj, d_skip_row, ssd_norm_w)


def _ln_silu(u_conv, ln_w, ln_b):
    t = u_conv.shape[0]

    def body(u_ref, w_ref, b_ref, o_ref, ot_ref):
        u = u_ref[...]
        mu = jnp.mean(u, axis=-1, keepdims=True)
        uc = u - mu
        rstd = lax.rsqrt(jnp.mean(uc * uc, axis=-1, keepdims=True) + 1e-5)
        out = _silu(uc * rstd * w_ref[...] + b_ref[...]).astype(MX)
        o_ref[...] = out
        ot_ref[...] = out.T

    blk = pl.BlockSpec((TM, D), lambda i: (i, 0))
    return pl.pallas_call(
        body, name="ln_silu", grid=(t // TM,),
        out_shape=(jax.ShapeDtypeStruct((t, D), MX), jax.ShapeDtypeStruct((D, t), MX)),
        in_specs=[blk, _row(), _row()], out_specs=(blk, pl.BlockSpec((D, TM), lambda i: (0, i))),
        compiler_params=_cp("arbitrary"),
    )(u_conv, ln_w, ln_b)


def _outproj_ln2_up(y_ssd, u, w_out, x, mod, norm2_w, w_up):
    t = x.shape[0]

    def body(y_ref, u_ref, wo_ref, x_ref, mod_ref, nw_ref, wu_ref, mix_ref, x1_ref, h2t_ref, up_ref):
        mix = jnp.dot(y_ref[...], wo_ref[0:D, :], preferred_element_type=f32)
        mix = mix + jnp.dot(u_ref[...], wo_ref[D:2 * D, :], preferred_element_type=f32)
        mix_ref[...] = mix
        x1 = x_ref[...] + mod_ref[:, 2 * D:3 * D] * mix
        x1_ref[...] = x1
        rstd = lax.rsqrt(jnp.mean(x1 * x1, axis=-1, keepdims=True) + 1e-6)
        h2 = ((x1 * rstd * nw_ref[...]) * (1.0 + mod_ref[:, 4 * D:5 * D]) + mod_ref[:, 3 * D:4 * D]).astype(MX)
        h2t_ref[...] = h2.T
        for k in range(4):
            up_ref[:, k * UP_SHARD:(k + 1) * UP_SHARD] = jnp.dot(h2, wu_ref[k], preferred_element_type=f32)

    blk = pl.BlockSpec((TM, D), lambda i: (i, 0))
    return pl.pallas_call(
        body, name="outproj_ln2_up", grid=(t // TM,),
        out_shape=(jax.ShapeDtypeStruct((t, D), f32), jax.ShapeDtypeStruct((t, D), f32),
                   jax.ShapeDtypeStruct((D, t), MX), jax.ShapeDtypeStruct((t, 2 * D_FF), f32)),
        in_specs=[blk, blk, _resident((2 * D, D)), blk, _row(6 * D), _row(), _resident((4, D, UP_SHARD))],
        out_specs=(blk, blk, pl.BlockSpec((D, TM), lambda i: (0, i)), pl.BlockSpec((TM, 2 * D_FF), lambda i: (i, 0))),
        compiler_params=_cp("arbitrary"),
    )(y_ssd, u, w_out, x, mod, norm2_w, w_up)


def _down_loss(act, w_down, x1, mod, final_norm_w, target):
    t = x1.shape[0]

    def body(a_ref, wd_ref, x1_ref, mod_ref, wf_ref, tgt_ref, dx2_ref, dffn_ref, dact_ref, st_ref):
        @pl.when(pl.program_id(0) == 0)
        def _():
            st_ref[...] = jnp.zeros_like(st_ref)

        g2 = mod_ref[:, 5 * D:6 * D]
        ffn = jnp.dot(a_ref[...], wd_ref[...], preferred_element_type=f32)
        x2 = x1_ref[...] + g2 * ffn
        rstd = lax.rsqrt(jnp.mean(x2 * x2, axis=-1, keepdims=True) + 1e-6)
        xh = x2 * rstd
        wf = wf_ref[...]
        err = xh * wf - tgt_ref[...]
        dy = err * (1.0 / D)
        dxh = dy * wf
        dx2 = rstd * (dxh - xh * jnp.mean(dxh * xh, axis=-1, keepdims=True))
        dx2_ref[...] = dx2
        dffn = (g2 * dx2).astype(MX)
        dffn_ref[...] = dffn
        dact_ref[...] = lax.dot_general(dffn, wd_ref[...], (((1,), (1,)), ((), ())), preferred_element_type=f32)
        st_ref[0:1, :] += jnp.sum(dy * xh, axis=0, keepdims=True)
        st_ref[1:2, :] += jnp.sum(dx2 * ffn, axis=0, keepdims=True)
        st_ref[2:3, :] += jnp.sum(0.5 * jnp.mean(err * err, axis=-1, keepdims=True), axis=0, keepdims=True)

    blk = pl.BlockSpec((TM, D), lambda i: (i, 0))
    ablk = pl.BlockSpec((TM, D_FF), lambda i: (i, 0))
    return pl.pallas_call(
        body, name="down_loss", grid=(t // TM,),
        out_shape=(jax.ShapeDtypeStruct((t, D), f32), jax.ShapeDtypeStruct((t, D), MX),
                   jax.ShapeDtypeStruct((t, D_FF), f32), jax.ShapeDtypeStruct((8, D), f32)),
        in_specs=[ablk, _resident((D_FF, D)), blk, _row(6 * D), _row(), blk],
        out_specs=(blk, blk, ablk, pl.BlockSpec((8, D), lambda i: (0, 0))),
        compiler_params=_cp("arbitrary"),
    )(act, w_down, x1, mod, final_norm_w, target)


def _pad_of(k):
    return 8 * ((k - 1 + 7) // 8)


def _causal_win(ref, r, t, pad):
    base = pl.multiple_of(r * RC, RC)
    prev = ref[pl.ds(pl.multiple_of(jnp.maximum(base - pad, 0), 8), pad), :]
    prev = jnp.where(r > 0, prev, 0.0)
    return jnp.concatenate([prev, ref[pl.ds(base, RC), :]], axis=0)


def _anti_win(ref, r, t, pad):
    base = pl.multiple_of(r * RC, RC)
    nxt = ref[pl.ds(pl.multiple_of(jnp.minimum(base + RC, t - pad), 8), pad), :]
    nxt = jnp.where(r < t // RC - 1, nxt, 0.0)
    return jnp.concatenate([ref[pl.ds(base, RC), :], nxt], axis=0)


def _shifted(win, offsets):
    for r in range(8):
        mine = [o for o in offsets if o % 8 == r]
        if mine:
            rolled = win if r == 0 else pltpu.roll(win, win.shape[0] - r, 0)
            for o in mine:
                yield o, rolled[o - r:o - r + RC, :]


def _conv_taps(win, w_ref, k, pad):
    first = pad - (k - 1)
    acc = None
    for o, rows in _shifted(win, range(first, first + k)):
        term = w_ref[o - first:o - first + 1, :] * rows
        acc = term if acc is None else acc + term
    return acc


def _corr_taps(win, w_ref, k):
    acc = None
    for o, rows in _shifted(win, range(k)):
        term = w_ref[k - 1 - o:k - o, :] * rows
        acc = term if acc is None else acc + term
    return acc


def _dw_accumulate(dw_scr, d, win, k, pad):
    first = pad - (k - 1)
    for o, rows in _shifted(win, range(first, first + k)):
        j = o - first
        prod = d * rows
        dw_scr[8 * j:8 * j + 8, :] += prod.reshape(RC // 8, 8, prod.shape[-1]).sum(axis=0)


def _dw_finish(dw_scr, dw_ref, k):
    for j in range(k):
        dw_ref[j:j + 1, :] = jnp.sum(dw_scr[8 * j:8 * j + 8, :], axis=0, keepdims=True)


def _rows8(v):
    return v.reshape(RC // 8, 8, v.shape[-1]).sum(axis=0)


def _ssd_conv_fwd(proj, conv_w, conv_b):
    t = proj.shape[0]
    pad = _pad_of(K_SSD)
    c0 = OFF_XBC // CW

    def body(x_ref, w_ref, b_ref, o_ref):
        def step(r, carry):
            win = _causal_win(x_ref, r, t, pad)
            o_ref[pl.ds(pl.multiple_of(r * RC, RC), RC), :] = _silu(_conv_taps(win, w_ref, K_SSD, pad) + b_ref[...])
            return carry
        lax.fori_loop(0, t // RC, step, 0)

    return pl.pallas_call(
        body, name="ssd_conv_fwd", grid=(D_XBC // CW,), out_shape=jax.ShapeDtypeStruct((t, D_XBC), f32),
        in_specs=[pl.BlockSpec((t, CW), lambda j: (0, c0 + j)), pl.BlockSpec((K_SSD, CW), lambda j: (0, j)),
                  pl.BlockSpec((1, CW), lambda j: (0, j))],
        out_specs=pl.BlockSpec((t, CW), lambda j: (0, j)), compiler_params=_cp("arbitrary"),
    )(proj, conv_w, conv_b)


def _glu_conv_fwd(proj, conv_w, conv_b, rider=None):
    t = proj.shape[0]
    pad = _pad_of(K_CONF)
    ca, cg = OFF_CA // CW, OFF_CG // CW

    def body(a_ref, g_ref, w_ref, b_ref, o_ref, v_scr):
        def glu(r, carry):
            rows = pl.ds(pl.multiple_of(r * RC, RC), RC)
            v_scr[rows, :] = a_ref[rows, :] * jax.nn.sigmoid(g_ref[rows, :])
            return carry
        lax.fori_loop(0, t // RC, glu, 0)

        def step(r, carry):
            win = _causal_win(v_scr, r, t, pad)
            o_ref[pl.ds(pl.multiple_of(r * RC, RC), RC), :] = _conv_taps(win, w_ref, K_CONF, pad) + b_ref[...]
            return carry
        lax.fori_loop(0, t // RC, step, 0)

    return _call(
        body, name="glu_conv_fwd", grid=(D // CW,), out_shape=(jax.ShapeDtypeStruct((t, D), f32),),
        in_specs=[pl.BlockSpec((t, CW), lambda j: (0, ca + j)), pl.BlockSpec((t, CW), lambda j: (0, cg + j)),
                  pl.BlockSpec((K_CONF, CW), lambda j: (0, j)), pl.BlockSpec((1, CW), lambda j: (0, j))],
        out_specs=(pl.BlockSpec((t, CW), lambda j: (0, j)),),
        scratch_shapes=[pltpu.VMEM((t, CW), f32)], sem=("arbitrary",), args=(proj, proj, conv_w, conv_b), rider=rider)


def _ffn_conv_fwd(up, conv_w, conv_b, rider=None):
    t = up.shape[0]
    pad = _pad_of(K_FFN)
    nb = D_FF // CW

    def body(g_ref, v_ref, wg_ref, wv_ref, bg_ref, bv_ref, o_ref, ot_ref):
        def step(r, carry):
            gc = _conv_taps(_causal_win(g_ref, r, t, pad), wg_ref, K_FFN, pad) + bg_ref[...]
            vc = _conv_taps(_causal_win(v_ref, r, t, pad), wv_ref, K_FFN, pad) + bv_ref[...]
            o_ref[pl.ds(pl.multiple_of(r * RC, RC), RC), :] = (_silu(gc) * vc).astype(MX)
            return carry
        lax.fori_loop(0, t // RC, step, 0)
        ot_ref[...] = o_ref[...].T

    return _call(
        body, name="ffn_conv_fwd", grid=(nb,),
        out_shape=(jax.ShapeDtypeStruct((t, D_FF), MX), jax.ShapeDtypeStruct((D_FF, t), MX)),
        in_specs=[pl.BlockSpec((t, CW), lambda j: (0, j)), pl.BlockSpec((t, CW), lambda j: (0, nb + j)),
                  pl.BlockSpec((K_FFN, CW), lambda j: (0, j)), pl.BlockSpec((K_FFN, CW), lambda j: (0, nb + j)),
                  pl.BlockSpec((1, CW), lambda j: (0, j)), pl.BlockSpec((1, CW), lambda j: (0, nb + j))],
        out_specs=(pl.BlockSpec((t, CW), lambda j: (0, j)), pl.BlockSpec((CW, t), lambda j: (j, 0))), sem=("arbitrary",),
        args=(up, up, conv_w, conv_w, conv_b, conv_b), rider=rider)


def _ffn_conv_bwd(up, conv_w, conv_b, d_act, rider=None):
    t = up.shape[0]
    pad = _pad_of(K_FFN)
    nb = D_FF // CW

    def body(g_ref, v_ref, wg_ref, wv_ref, bg_ref, bv_ref, da_ref, dup_ref, dw_ref, db_ref,
             dg_scr, dv_scr, dwg_scr, dwv_scr, db_scr):
        dwg_scr[...] = jnp.zeros_like(dwg_scr)
        dwv_scr[...] = jnp.zeros_like(dwv_scr)
        db_scr[...] = jnp.zeros_like(db_scr)

        def first(r, carry):
            rows = pl.ds(pl.multiple_of(r * RC, RC), RC)
            gwin = _causal_win(g_ref, r, t, pad)
            vwin = _causal_win(v_ref, r, t, pad)
            gc = _conv_taps(gwin, wg_ref, K_FFN, pad) + bg_ref[...]
            vc = _conv_taps(vwin, wv_ref, K_FFN, pad) + bv_ref[...]
            da = da_ref[rows, :]
            dgc = da * vc * _dsilu(gc)
            dvc = da * _silu(gc)
            dg_scr[rows, :] = dgc
            dv_scr[rows, :] = dvc
            _dw_accumulate(dwg_scr, dgc, gwin, K_FFN, pad)
            _dw_accumulate(dwv_scr, dvc, vwin, K_FFN, pad)
            db_scr[0:8, :] += _rows8(dgc)
            db_scr[8:16, :] += _rows8(dvc)
            return carry
        lax.fori_loop(0, t // RC, first, 0)

        def second(r, carry):
            rows = pl.ds(pl.multiple_of(r * RC, RC), RC)
            dup_ref[0, rows, :] = _corr_taps(_anti_win(dg_scr, r, t, pad), wg_ref, K_FFN).astype(MX)
            dup_ref[1, rows, :] = _corr_taps(_anti_win(dv_scr, r, t, pad), wv_ref, K_FFN).astype(MX)
            return carry
        lax.fori_loop(0, t // RC, second, 0)

        for j in range(K_FFN):
            dw_ref[0, j:j + 1, :] = jnp.sum(dwg_scr[8 * j:8 * j + 8, :], axis=0, keepdims=True)
            dw_ref[1, j:j + 1, :] = jnp.sum(dwv_scr[8 * j:8 * j + 8, :], axis=0, keepdims=True)
        db_ref[0] = jnp.sum(db_scr[0:8, :], axis=0, keepdims=True)
        db_ref[1] = jnp.sum(db_scr[8:16, :], axis=0, keepdims=True)

    return _call(
        body, name="ffn_conv_bwd", grid=(nb,),
        out_shape=(jax.ShapeDtypeStruct((2, t, D_FF), MX), jax.ShapeDtypeStruct((2, K_FFN, D_FF), f32),
                   jax.ShapeDtypeStruct((2, 1, D_FF), f32)),
        in_specs=[pl.BlockSpec((t, CW), lambda j: (0, j)), pl.BlockSpec((t, CW), lambda j: (0, nb + j)),
                  pl.BlockSpec((K_FFN, CW), lambda j: (0, j)), pl.BlockSpec((K_FFN, CW), lambda j: (0, nb + j)),
                  pl.BlockSpec((1, CW), lambda j: (0, j)), pl.BlockSpec((1, CW), lambda j: (0, nb + j)),
                  pl.BlockSpec((t, CW), lambda j: (0, j))],
        out_specs=(pl.BlockSpec((2, t, CW), lambda j: (0, 0, j)), pl.BlockSpec((2, K_FFN, CW), lambda j: (0, 0, j)),
                   pl.BlockSpec((2, 1, CW), lambda j: (0, 0, j))),
        scratch_shapes=[pltpu.VMEM((t, CW), f32), pltpu.VMEM((t, CW), f32), pltpu.VMEM((8 * K_FFN, CW), f32),
                        pltpu.VMEM((8 * K_FFN, CW), f32), pltpu.VMEM((16, CW), f32)],
        sem=("arbitrary",), args=(up, up, conv_w, conv_w, conv_b, conv_b, d_act), rider=rider)


def _glu_conv_bwd(proj, conv_w, d_uconv, rider=None):
    t = proj.shape[0]
    pad = _pad_of(K_CONF)
    ca, cg = OFF_CA // CW, OFF_CG // CW

    def body(a_ref, g_ref, w_ref, du_ref, dc_ref, dw_ref, db_ref, v_scr, dw_scr, db_scr):
        dw_scr[...] = jnp.zeros_like(dw_scr)
        db_scr[...] = jnp.zeros_like(db_scr)

        def glu(r, carry):
            rows = pl.ds(pl.multiple_of(r * RC, RC), RC)
            v_scr[rows, :] = a_ref[rows, :] * jax.nn.sigmoid(g_ref[rows, :])
            return carry
        lax.fori_loop(0, t // RC, glu, 0)

        def step(r, carry):
            rows = pl.ds(pl.multiple_of(r * RC, RC), RC)
            du = du_ref[rows, :]
            _dw_accumulate(dw_scr, du, _causal_win(v_scr, r, t, pad), K_CONF, pad)
            db_scr[...] += _rows8(du)
            dv = _corr_taps(_anti_win(du_ref, r, t, pad), w_ref, K_CONF)
            a = a_ref[rows, :]
            s = jax.nn.sigmoid(g_ref[rows, :])
            dc_ref[0, rows, :] = (dv * s).astype(MX)
            dc_ref[1, rows, :] = (dv * a * s * (1.0 - s)).astype(MX)
            return carry
        lax.fori_loop(0, t // RC, step, 0)
        _dw_finish(dw_scr, dw_ref, K_CONF)
        db_ref[...] = jnp.sum(db_scr[...], axis=0, keepdims=True)

    return _call(
        body, name="glu_conv_bwd", grid=(D // CW,),
        out_shape=(jax.ShapeDtypeStruct((2, t, D), MX), jax.ShapeDtypeStruct((K_CONF, D), f32),
                   jax.ShapeDtypeStruct((1, D), f32)),
        in_specs=[pl.BlockSpec((t, CW), lambda j: (0, ca + j)), pl.BlockSpec((t, CW), lambda j: (0, cg + j)),
                  pl.BlockSpec((K_CONF, CW), lambda j: (0, j)), pl.BlockSpec((t, CW), lambda j: (0, j))],
        out_specs=(pl.BlockSpec((2, t, CW), lambda j: (0, 0, j)), pl.BlockSpec((K_CONF, CW), lambda j: (0, j)),
                   pl.BlockSpec((1, CW), lambda j: (0, j))),
        scratch_shapes=[pltpu.VMEM((t, CW), f32), pltpu.VMEM((8 * K_CONF, CW), f32), pltpu.VMEM((8, CW), f32)],
        sem=("arbitrary",), args=(proj, proj, conv_w, d_uconv), rider=rider)


def _ssd_conv_bwd_x(proj, conv_w, conv_b, d_xs, d_y, d_skip_row):
    t = proj.shape[0]
    pad = _pad_of(K_SSD)
    c0 = OFF_XBC // CW

    def body(x_ref, w_ref, b_ref, dxs_ref, dy_ref, dsk_ref, draw_ref, dw_ref, db_ref, dp_scr, dw_scr, db_scr):
        dw_scr[...] = jnp.zeros_like(dw_scr)
        db_scr[...] = jnp.zeros_like(db_scr)

        def first(r, carry):
            rows = pl.ds(pl.multiple_of(r * RC, RC), RC)
            win = _causal_win(x_ref, r, t, pad)
            pre = _conv_taps(win, w_ref, K_SSD, pad) + b_ref[...]
            dpre = (dxs_ref[rows, :] + dy_ref[rows, :] * dsk_ref[...]) * _dsilu(pre)
            dp_scr[rows, :] = dpre
            _dw_accumulate(dw_scr, dpre, win, K_SSD, pad)
            db_scr[...] += _rows8(dpre)
            return carry
        lax.fori_loop(0, t // RC, first, 0)

        def second(r, carry):
            rows = pl.ds(pl.multiple_of(r * RC, RC), RC)
            draw_ref[rows, :] = _corr_taps(_anti_win(dp_scr, r, t, pad), w_ref, K_SSD).astype(MX)
            return carry
        lax.fori_loop(0, t // RC, second, 0)
        _dw_finish(dw_scr, dw_ref, K_SSD)
        db_ref[...] = jnp.sum(db_scr[...], axis=0, keepdims=True)

    cb = pl.BlockSpec((t, CW), lambda j: (0, j))
    return pl.pallas_call(
        body, name="ssd_conv_bwd_x", grid=(D // CW,),
        out_shape=(jax.ShapeDtypeStruct((t, D), MX), jax.ShapeDtypeStruct((K_SSD, D), f32),
                   jax.ShapeDtypeStruct((1, D), f32)),
        in_specs=[pl.BlockSpec((t, CW), lambda j: (0, c0 + j)), pl.BlockSpec((K_SSD, CW), lambda j: (0, j)),
                  pl.BlockSpec((1, CW), lambda j: (0, j)), cb, cb, pl.BlockSpec((1, CW), lambda j: (0, j))],
        out_specs=(cb, pl.BlockSpec((K_SSD, CW), lambda j: (0, j)), pl.BlockSpec((1, CW), lambda j: (0, j))),
        scratch_shapes=[pltpu.VMEM((t, CW), f32), pltpu.VMEM((8 * K_SSD, CW), f32), pltpu.VMEM((8, CW), f32)],
        compiler_params=_cp("arbitrary"),
    )(proj, conv_w, conv_b, d_xs, d_y, d_skip_row)


def _ssd_conv_bwd_bc(proj, conv_w, conv_b, d_bc):
    t = proj.shape[0]
    pad = _pad_of(K_SSD)
    c0 = (OFF_XBC + D) // CW
    w0 = D // CW

    def body(x_ref, w_ref, b_ref, dbc_ref, draw_ref, dw_ref, db_ref, dp_scr, dw_scr, db_scr):
        dw_scr[...] = jnp.zeros_like(dw_scr)
        db_scr[...] = jnp.zeros_like(db_scr)

        def first(r, carry):
            rows = pl.ds(pl.multiple_of(r * RC, RC), RC)
            win = _causal_win(x_ref, r, t, pad)
            pre = _conv_taps(win, w_ref, K_SSD, pad) + b_ref[...]
            dpre = dbc_ref[0, rows, :] * _dsilu(pre)
            dp_scr[rows, :] = dpre
            _dw_accumulate(dw_scr, dpre, win, K_SSD, pad)
            db_scr[...] += _rows8(dpre)
            return carry
        lax.fori_loop(0, t // RC, first, 0)

        def second(r, carry):
            rows = pl.ds(pl.multiple_of(r * RC, RC), RC)
            draw_ref[rows, :] = _corr_taps(_anti_win(dp_scr, r, t, pad), w_ref, K_SSD).astype(MX)
            return carry
        lax.fori_loop(0, t // RC, second, 0)
        _dw_finish(dw_scr, dw_ref, K_SSD)
        db_ref[...] = jnp.sum(db_scr[...], axis=0, keepdims=True)

    return pl.pallas_call(
        body, name="ssd_conv_bwd_bc", grid=(2,),
        out_shape=(jax.ShapeDtypeStruct((t, 2 * CW), MX), jax.ShapeDtypeStruct((K_SSD, 2 * CW), f32),
                   jax.ShapeDtypeStruct((1, 2 * CW), f32)),
        in_specs=[pl.BlockSpec((t, CW), lambda j: (0, c0 + j)), pl.BlockSpec((K_SSD, CW), lambda j: (0, w0 + j)),
                  pl.BlockSpec((1, CW), lambda j: (0, w0 + j)), pl.BlockSpec((1, t, CW), lambda j: (j, 0, 0))],
        out_specs=(pl.BlockSpec((t, CW), lambda j: (0, j)), pl.BlockSpec((K_SSD, CW), lambda j: (0, j)),
                   pl.BlockSpec((1, CW), lambda j: (0, j))),
        scratch_shapes=[pltpu.VMEM((t, CW), f32), pltpu.VMEM((8 * K_SSD, CW), f32), pltpu.VMEM((8, CW), f32)],
        compiler_params=_cp("arbitrary"),
    )(proj, conv_w, conv_b, d_bc)


def _chunk_masks():
    ii = lax.broadcasted_iota(jnp.int32, (CHUNK, CHUNK), 0)
    jj = lax.broadcasted_iota(jnp.int32, (CHUNK, CHUNK), 1)
    return ii == jj, jj <= ii, jj >= ii


def _to_row(col, eye):
    return jnp.sum(jnp.where(eye, col, 0.0), axis=0, keepdims=True)


def _to_col(row, eye):
    return jnp.sum(jnp.where(eye, row, 0.0), axis=1, keepdims=True)


def _head_decay(dt_h, a_h, eye, tril):
    a_row = _to_row(dt_h * a_h, eye)
    cs = jnp.sum(jnp.where(tril, a_row, 0.0), axis=1, keepdims=True)
    cs_row = _to_row(cs, eye)
    decay = jnp.where(tril, jnp.exp(jnp.where(tril, cs - cs_row, 0.0)), 0.0)
    total = jnp.sum(a_row, axis=1, keepdims=True)
    return cs, decay, total


SCAN_UNROLL = 4


def _unrolled_loop(n, step, init):
    unroll = min(SCAN_UNROLL, n)
    assert n % unroll == 0

    def trip(i, carry):
        for u in range(unroll):
            carry = step(unroll * i + u, carry)
        return carry
    return lax.fori_loop(0, n // unroll, trip, init)


def _lane_pick(mat, lane, which):
    return jnp.sum(jnp.where(lane == which, mat, 0.0), axis=1, keepdims=True)


def _ssd_fwd(xbc_act, proj, dt_bias_row, a_log_row, rider=None):
    t = xbc_act.shape[0]
    nc = t // CHUNK
    cb, cc, cdt = D // LANES, (D + 2 * STATE_N) // LANES, OFF_DT // LANES

    def body(x_ref, b_ref, c_ref, dt_ref, dtb_ref, alog_ref, y_ref, st_ref):
        j = pl.program_id(0)
        eye, tril, _ = _chunk_masks()
        lane = lax.broadcasted_iota(jnp.int32, (1, LANES), 1)
        first = lane < HEAD_P
        a_row = -jnp.exp(alog_ref[...])
        a_heads = [jnp.sum(jnp.where(lane == 2 * j + h, a_row, 0.0), axis=1, keepdims=True) for h in range(2)]

        def chunk(c, hprev):
            rows = pl.ds(pl.multiple_of(c * CHUNK, CHUNK), CHUNK)
            xv, bm, cm = x_ref[rows, :], b_ref[rows, :], c_ref[rows, :]
            dt = _softplus(dt_ref[rows, :] + dtb_ref[...])
            st_ref[c] = hprev
            g = _mm_nt(cm, bm)
            ch = _mm(cm, hprev)
            dts = [_lane_pick(dt, lane, 2 * j + h) for h in range(2)]
            xdt = xv * jnp.where(first, dts[0], dts[1])
            ys, hs = [], []
            for h in range(2):
                cs, decay, total = _head_decay(dts[h], a_heads[h], eye, tril)
                y = _mm(g * decay, xdt) + jnp.exp(cs) * ch
                s = _mm_tn(bm * jnp.exp(total - cs), xdt)
                ys.append(y)
                hs.append(jnp.exp(total) * hprev + s)
            y_ref[rows, :] = jnp.where(first, ys[0], ys[1])
            return jnp.where(first, hs[0], hs[1])

        _unrolled_loop(nc, chunk, jnp.zeros((STATE_N, LANES), f32))

    blk = lambda f: pl.BlockSpec((t, LANES), f)
    return _call(
        body, name="ssd_fwd", grid=(D // LANES,),
        out_shape=(jax.ShapeDtypeStruct((t, D), f32), jax.ShapeDtypeStruct((nc, STATE_N, D), f32)),
        in_specs=[blk(lambda j: (0, j)), blk(lambda j: (0, cb + j // 4)), blk(lambda j: (0, cc + j // 4)),
                  blk(lambda j: (0, cdt)), _row(LANES), _row(LANES)],
        out_specs=(blk(lambda j: (0, j)), pl.BlockSpec((nc, STATE_N, LANES), lambda j: (0, 0, j))),
        sem=("arbitrary",), args=(xbc_act, xbc_act, xbc_act, proj, dt_bias_row, a_log_row), rider=rider)


def _ssd_bwd(xbc_act, proj, dt_bias_row, a_log_row, states, d_y, rider=None):
    t = xbc_act.shape[0]
    nc = t // CHUNK
    cb, cc, cdt = D // LANES, (D + 2 * STATE_N) // LANES, OFF_DT // LANES

    def body(x_ref, b_ref, c_ref, dt_ref, dtb_ref, alog_ref, st_ref, dy_ref, dx_ref, dbc_ref, ddt_ref, da_ref):
        grp, p = pl.program_id(0), pl.program_id(1)
        j = 4 * grp + p
        eye, tril, triu = _chunk_masks()
        lane = lax.broadcasted_iota(jnp.int32, (1, LANES), 1)
        first = lane < HEAD_P
        last_row = lax.broadcasted_iota(jnp.int32, (CHUNK, 1), 0) == CHUNK - 1
        a_row = -jnp.exp(alog_ref[...])
        a_heads = [jnp.sum(jnp.where(lane == 2 * j + h, a_row, 0.0), axis=1, keepdims=True) for h in range(2)]

        @pl.when(p == 0)
        def _():
            dbc_ref[...] = jnp.zeros_like(dbc_ref)

        @pl.when(j == 0)
        def _():
            ddt_ref[...] = jnp.zeros_like(ddt_ref)
            da_ref[...] = jnp.zeros_like(da_ref)

        def chunk(i, dh):
            c = nc - 1 - i
            rows = pl.ds(pl.multiple_of(c * CHUNK, CHUNK), CHUNK)
            xv, bm, cm = x_ref[rows, :], b_ref[rows, :], c_ref[rows, :]
            dtr = dt_ref[rows, :] + dtb_ref[...]
            dt = _softplus(dtr)
            hprev = st_ref[c]
            dy = dy_ref[rows, :]
            g = _mm_nt(cm, bm)
            dts = [_lane_pick(dt, lane, 2 * j + h) for h in range(2)]
            xdt = xv * jnp.where(first, dts[0], dts[1])
            dxs, dhs = [], []
            db_sum, dc_sum = None, None
            ddt_mat = jnp.zeros((CHUNK, LANES), f32)
            da_acc = jnp.zeros((1, LANES), f32)
            for h in range(2):
                mine = first if h == 0 else jnp.logical_not(first)
                cs, decay, total = _head_decay(dts[h], a_heads[h], eye, tril)
                e_cs, e_tot = jnp.exp(cs), jnp.exp(total)
                dec_s = jnp.exp(total - cs)
                dyh = jnp.where(mine, dy, 0.0)
                xdth = jnp.where(mine, xdt, 0.0)
                dhh = jnp.where(mine, dh, 0.0)
                hph = jnp.where(mine, hprev, 0.0)
                m = g * decay
                dm = _mm_nt(dyh, xdth)
                dg = dm * decay
                w = dm * m
                bdec = bm * dec_s
                dxdt = _mm_tn(m, dyh) + _mm(bdec, dhh)
                dc_off = _mm_nt(dyh, hph) * e_cs
                db_s = _mm_nt(xdth, dhh) * dec_s
                dc_h = _mm(dg, bm) + dc_off
                db_h = _mm_tn(dg, cm) + db_s
                r_s = jnp.sum(db_s * bm, axis=1, keepdims=True)
                dtotal = jnp.sum(r_s, axis=0, keepdims=True) + e_tot * jnp.sum(
                    jnp.sum(dhh * hph, axis=1, keepdims=True), axis=0, keepdims=True)
                dcs = (jnp.sum(w, axis=1, keepdims=True) - _to_col(jnp.sum(w, axis=0, keepdims=True), eye)
                       + jnp.sum(dc_off * cm, axis=1, keepdims=True) - r_s + jnp.where(last_row, dtotal, 0.0))
                da_col = jnp.sum(jnp.where(triu, _to_row(dcs, eye), 0.0), axis=1, keepdims=True)
                ddt = da_col * a_heads[h] + jnp.sum(jnp.where(mine, dxdt * xv, 0.0), axis=1, keepdims=True)
                ddt_mat = ddt_mat + jnp.where(lane == 2 * j + h, ddt, 0.0)
                da_acc = da_acc + jnp.where(lane == 2 * j + h, jnp.sum(da_col * dts[h], axis=0, keepdims=True), 0.0)
                dxs.append(dxdt * dts[h])
                dhs.append(e_tot * dhh + _mm_tn(cm * e_cs, dyh))
                db_sum = db_h if db_sum is None else db_sum + db_h
                dc_sum = dc_h if dc_sum is None else dc_sum + dc_h
            dx_ref[rows, :] = jnp.where(first, dxs[0], dxs[1])
            dbc_ref[0, rows, :] += db_sum
            dbc_ref[1, rows, :] += dc_sum
            ddt_ref[rows, :] += ddt_mat * jax.nn.sigmoid(dtr)
            da_ref[...] += da_acc * a_row
            return jnp.where(first, dhs[0], dhs[1])

        _unrolled_loop(nc, chunk, jnp.zeros((STATE_N, LANES), f32))

    blk = lambda f: pl.BlockSpec((t, LANES), f)
    return _call(
        body, name="ssd_bwd", grid=(2, 4),
        out_shape=(jax.ShapeDtypeStruct((t, D), f32), jax.ShapeDtypeStruct((2, t, 2 * STATE_N), f32),
                   jax.ShapeDtypeStruct((t, LANES), f32), jax.ShapeDtypeStruct((1, LANES), f32)),
        in_specs=[blk(lambda g, p: (0, 4 * g + p)), blk(lambda g, p: (0, cb + g)), blk(lambda g, p: (0, cc + g)),
                  blk(lambda g, p: (0, cdt)), _row(LANES), _row(LANES),
                  pl.BlockSpec((nc, STATE_N, LANES), lambda g, p: (0, 0, 4 * g + p)), blk(lambda g, p: (0, 4 * g + p))],
        out_specs=(blk(lambda g, p: (0, 4 * g + p)), pl.BlockSpec((2, t, LANES), lambda g, p: (0, 0, g)),
                   blk(lambda g, p: (0, 0)), _row(LANES)),
        sem=("arbitrary", "arbitrary"), args=(xbc_act, xbc_act, xbc_act, proj, dt_bias_row, a_log_row, states, d_y),
        rider=rider)


def _up_bwd(d_up, w_up, x1, mod, norm2_w, dx2, mix, w_out, rider=None):
    t = x1.shape[0]

    def body(dup_ref, wu_ref, x1_ref, mod_ref, nw_ref, dx2_ref, mix_ref, wo_ref,
             dx1_ref, dmix_ref, dys_ref, du_ref, st_ref):
        @pl.when(pl.program_id(0) == 0)
        def _():
            st_ref[...] = jnp.zeros_like(st_ref)

        nt = (((1,), (1,)), ((), ()))
        dh = None
        for k in range(4):
            lo = (k % 2) * UP_SHARD
            part = lax.dot_general(dup_ref[k // 2, :, lo:lo + UP_SHARD], wu_ref[k], nt, preferred_element_type=f32)
            dh = part if dh is None else dh + part
        x1 = x1_ref[...]
        rstd = lax.rsqrt(jnp.mean(x1 * x1, axis=-1, keepdims=True) + 1e-6)
        xh = x1 * rstd
        nw = nw_ref[...]
        sc = 1.0 + mod_ref[:, 4 * D:5 * D]
        st_ref[0:1, :] += jnp.sum(dh, axis=0, keepdims=True)
        st_ref[1:2, :] += jnp.sum(dh * xh * nw, axis=0, keepdims=True)
        st_ref[2:3, :] += jnp.sum(dh * sc * xh, axis=0, keepdims=True)
        dxh = dh * sc * nw
        dx1 = dx2_ref[...] + rstd * (dxh - xh * jnp.mean(dxh * xh, axis=-1, keepdims=True))
        dx1_ref[...] = dx1
        st_ref[3:4, :] += jnp.sum(dx1 * mix_ref[...], axis=0, keepdims=True)
        dmix = (mod_ref[:, 2 * D:3 * D] * dx1).astype(MX)
        dmix_ref[...] = dmix
        dys_ref[...] = lax.dot_general(dmix, wo_ref[0:D, :], nt, preferred_element_type=f32)
        du_ref[...] = lax.dot_general(dmix, wo_ref[D:2 * D, :], nt, preferred_element_type=f32)

    blk = pl.BlockSpec((TM, D), lambda i: (i, 0))
    return _call(
        body, name="up_bwd", grid=(t // TM,),
        out_shape=(jax.ShapeDtypeStruct((t, D), f32), jax.ShapeDtypeStruct((t, D), MX),
                   jax.ShapeDtypeStruct((t, D), f32), jax.ShapeDtypeStruct((t, D), f32),
                   jax.ShapeDtypeStruct((8, D), f32)),
        in_specs=[pl.BlockSpec((2, TM, D_FF), lambda i: (0, i, 0)), _resident((4, D, UP_SHARD)), blk, _row(6 * D), _row(),
                  blk, blk, _resident((2 * D, D))],
        out_specs=(blk, blk, blk, blk, pl.BlockSpec((8, D), lambda i: (0, 0))),
        sem=("arbitrary",), args=(d_up, w_up, x1, mod, norm2_w, dx2, mix, w_out), rider=rider)


def _ln_silu_bwd(d_u, u_conv, ln_w, ln_b):
    t = d_u.shape[0]

    def body(du_ref, u_ref, w_ref, b_ref, o_ref, st_ref):
        @pl.when(pl.program_id(0) == 0)
        def _():
            st_ref[...] = jnp.zeros_like(st_ref)

        u = u_ref[...]
        mu = jnp.mean(u, axis=-1, keepdims=True)
        uc = u - mu
        rstd = lax.rsqrt(jnp.mean(uc * uc, axis=-1, keepdims=True) + 1e-5)
        n = uc * rstd
        w = w_ref[...]
        dl = du_ref[...] * _dsilu(n * w + b_ref[...])
        st_ref[0:1, :] += jnp.sum(dl * n, axis=0, keepdims=True)
        st_ref[1:2, :] += jnp.sum(dl, axis=0, keepdims=True)
        dn = dl * w
        o_ref[...] = rstd * (dn - jnp.mean(dn, axis=-1, keepdims=True) - n * jnp.mean(dn * n, axis=-1, keepdims=True))

    blk = pl.BlockSpec((TM, D), lambda i: (i, 0))
    return pl.pallas_call(
        body, name="ln_silu_bwd", grid=(t // TM,),
        out_shape=(jax.ShapeDtypeStruct((t, D), f32), jax.ShapeDtypeStruct((8, D), f32)),
        in_specs=[blk, blk, _row(), _row()], out_specs=(blk, pl.BlockSpec((8, D), lambda i: (0, 0))),
        compiler_params=_cp("arbitrary"),
    )(d_u, u_conv, ln_w, ln_b)


def _ssd_gate_norm_bwd(d_out, y_scan, xbc_act, proj, d_skip_row, ssd_norm_w):
    t = d_out.shape[0]

    def body(do_ref, y_ref, xs_ref, z_ref, dsk_ref, nw_ref, dy_ref, dz_ref, st_ref):
        @pl.when(pl.program_id(0) == 0)
        def _():
            st_ref[...] = jnp.zeros_like(st_ref)

        xs = xs_ref[...]
        y = y_ref[...] + xs * dsk_ref[...]
        z = z_ref[...]
        s = _silu(z)
        yz = y * s
        rstd = lax.rsqrt(jnp.mean(yz * yz, axis=-1, keepdims=True) + 1e-6)
        n = yz * rstd
        do = do_ref[...]
        st_ref[0:1, :] += jnp.sum(do * n, axis=0, keepdims=True)
        dn = do * nw_ref[...]
        dyz = rstd * (dn - n * jnp.mean(dn * n, axis=-1, keepdims=True))
        dy = dyz * s
        dy_ref[...] = dy
        dz_ref[...] = (dyz * y * _dsilu(z)).astype(MX)
        st_ref[1:2, :] += jnp.sum(dy * xs, axis=0, keepdims=True)

    blk = pl.BlockSpec((TM, D), lambda i: (i, 0))
    return pl.pallas_call(
        body, name="ssd_gate_norm_bwd", grid=(t // TM,),
        out_shape=(jax.ShapeDtypeStruct((t, D), f32), jax.ShapeDtypeStruct((t, D), MX), jax.ShapeDtypeStruct((8, D), f32)),
        in_specs=[blk, blk, blk, blk, _row(), _row()], out_specs=(blk, blk, pl.BlockSpec((8, D), lambda i: (0, 0))),
        compiler_params=_cp("arbitrary"),
    )(d_out, y_scan, xbc_act, proj, d_skip_row, ssd_norm_w)


def _inproj_bwd(d_z, d_xraw, d_bcraw, d_conf, d_dt, w_pack, x, mod, norm1_w, dx1, rider=None):
    t = x.shape[0]

    def body(dz_ref, dx_ref, dbc_ref, dcf_ref, ddt_ref, w_ref, x_ref, mod_ref, nw_ref, dx1_ref, gx_ref, st_ref):
        @pl.when(pl.program_id(0) == 0)
        def _():
            st_ref[...] = jnp.zeros_like(st_ref)

        nt = (((1,), (1,)), ((), ()))
        dot = lambda a, lo, hi: lax.dot_general(a, w_ref[:, lo:hi], nt, preferred_element_type=f32)
        dh = dot(dz_ref[...], OFF_Z, OFF_Z + D)
        dh = dh + dot(dx_ref[...], OFF_XBC, OFF_XBC + D)
        dh = dh + dot(dbc_ref[...], OFF_XBC + D, OFF_XBC + D_XBC)
        dh = dh + dot(dcf_ref[0], OFF_CA, OFF_CA + D)
        dh = dh + dot(dcf_ref[1], OFF_CG, OFF_CG + D)
        dh = dh + dot(ddt_ref[...].astype(MX), OFF_DT, OFF_DT + LANES)
        st_ref[3:4, 0:LANES] += jnp.sum(ddt_ref[...], axis=0, keepdims=True)
        xv = x_ref[...]
        rstd = lax.rsqrt(jnp.mean(xv * xv, axis=-1, keepdims=True) + 1e-6)
        xh = xv * rstd
        nw = nw_ref[...]
        sc = 1.0 + mod_ref[:, D:2 * D]
        st_ref[0:1, :] += jnp.sum(dh, axis=0, keepdims=True)
        st_ref[1:2, :] += jnp.sum(dh * xh * nw, axis=0, keepdims=True)
        st_ref[2:3, :] += jnp.sum(dh * sc * xh, axis=0, keepdims=True)
        dxh = dh * sc * nw
        gx_ref[...] = dx1_ref[...] + rstd * (dxh - xh * jnp.mean(dxh * xh, axis=-1, keepdims=True))

    blk = pl.BlockSpec((TM, D), lambda i: (i, 0))
    return _call(
        body, name="inproj_bwd", grid=(t // TM,),
        out_shape=(jax.ShapeDtypeStruct((t, D), f32), jax.ShapeDtypeStruct((8, D), f32)),
        in_specs=[blk, blk, pl.BlockSpec((TM, 2 * CW), lambda i: (i, 0)), pl.BlockSpec((2, TM, D), lambda i: (0, i, 0)),
                  pl.BlockSpec((TM, LANES), lambda i: (i, 0)), _resident((D, W_PACK)), blk, _row(6 * D), _row(), blk],
        out_specs=(blk, pl.BlockSpec((8, D), lambda i: (0, 0))),
        sem=("arbitrary",), args=(d_z, d_xraw, d_bcraw, d_conf, d_dt, w_pack, x, mod, norm1_w, dx1), rider=rider)


def _wgrad(at, d, name, bn=256):
    k, t = at.shape
    n = d.shape[1]
    out_dtype = MX

    def body(a_ref, d_ref, o_ref):
        o_ref[...] = jnp.dot(a_ref[...], d_ref[...].astype(MX), preferred_element_type=f32).astype(out_dtype)

    return pl.pallas_call(
        body, name=name, grid=(n // bn,), out_shape=jax.ShapeDtypeStruct((k, n), out_dtype),
        in_specs=[_resident((k, t)), pl.BlockSpec((t, bn), lambda j: (0, j))],
        out_specs=pl.BlockSpec((k, bn), lambda j: (0, j)), compiler_params=_cp("arbitrary"),
    )(at, d)


def _wgrad_stacked(at, d, name, bn):
    out_dtype = MX
    k, t = at.shape
    s, _, n = d.shape
    nb = n // bn

    def body(a_ref, d_ref, o_ref):
        o_ref[0] = jnp.dot(a_ref[...], d_ref[0], preferred_element_type=f32).astype(out_dtype)

    return pl.pallas_call(
        body, name=name, grid=(s, nb), out_shape=jax.ShapeDtypeStruct((s * nb, k, bn), out_dtype),
        in_specs=[_resident((k, t)), pl.BlockSpec((1, t, bn), lambda i, j: (i, 0, j))],
        out_specs=pl.BlockSpec((1, k, bn), lambda i, j: (i * nb + j, 0, 0)), compiler_params=_cp("arbitrary", "arbitrary"),
    )(at, d)


def _pad_row(v, width=LANES):
    return jnp.pad(v.reshape(1, -1), ((0, 0), (0, width - v.size)))


def _quarters(a):
    return a.reshape(4, 2, a.shape[0] // 8, a.shape[1])


def _local_step(x, mod, target, w_pack, late, small, reducer=None):
    dtb_row, alog_row = _pad_row(small["dt_bias"]), _pad_row(small["a_log"])
    dskip_row = jnp.repeat(small["d_skip"].reshape(-1), HEAD_P).reshape(1, D)

    red = reducer

    def hosted(host, args, swap=None, scatter=None, gather=None, sums=()):
        if red is None:
            return host(*args)[0]
        riders = ([red.scatter(scatter)] if scatter else []) + ([red.swap(*swap)] if swap else [])
        riders += [_SwapSumsRider([red.sums[n] for n in sums])] if sums else []
        riders += [_GatherRider([gather])] if gather is not None else []
        both = _Riders(riders)
        outs, extra = host(*args, rider=both)
        extra = both.split(extra)
        if scatter:
            red.scattered(scatter, extra.pop(0))
        if swap:
            red.swapped(swap[0], extra.pop(0))
        if sums:
            red.others.update(zip(sums, extra.pop(0)))
        return (outs, extra[0][0]) if gather is not None else outs

    proj, h_t = _ln_inproj(x, mod, small["norm1_w"], w_pack)
    xbc_act = _ssd_conv_fwd(proj, small["ssd_conv_w"], small["ssd_conv_b"])
    w_out, w_up, w_down = late
    if red is None:
        y_scan, states = hosted(_ssd_fwd, (xbc_act, proj, dtb_row, alog_row))
        u_conv, = hosted(_glu_conv_fwd, (proj, small["conf_conv_w"], small["conf_conv_b"]))
    else:
        (y_scan, states), w_up = hosted(_ssd_fwd, (xbc_act, proj, dtb_row, alog_row), gather=w_up)
        (u_conv,), w_out = hosted(_glu_conv_fwd, (proj, small["conf_conv_w"], small["conf_conv_b"]), gather=w_out)
        w_up, w_out = w_up.reshape(4, D, UP_SHARD), w_out.reshape(2 * D, D)
    y_ssd, y_ssd_t = _ssd_gate_norm(y_scan, xbc_act, proj, dskip_row, small["ssd_norm_w"])
    u, u_t = _ln_silu(u_conv, small["conf_ln_w"], small["conf_ln_b"])
    mix, x1, h2_t, up = _outproj_ln2_up(y_ssd, u, w_out, x, mod, small["norm2_w"], w_up)
    if red is None:
        act, act_t = hosted(_ffn_conv_fwd, (up, small["ffn_conv_w"], small["ffn_conv_b"]))
    else:
        (act, act_t), w_down = hosted(_ffn_conv_fwd, (up, small["ffn_conv_w"], small["ffn_conv_b"]), gather=w_down)
        w_down = w_down.reshape(D_FF, D)
    dx2, d_ffn, d_act, st_down = _down_loss(act, w_down, x1, mod, small["final_norm_w"], target)

    g_down = _quarters(_wgrad(act_t, d_ffn, "wgrad_down"))
    d_up, dw_ffn, db_ffn = hosted(_ffn_conv_bwd, (up, small["ffn_conv_w"], small["ffn_conv_b"], d_act), swap=("w_down", g_down))
    g_up = _wgrad_stacked(h2_t, d_up, "wgrad_up", D_FF // 2).reshape(4, 2, D // 2, UP_SHARD)
    dx1, d_mix, d_yssd, d_u, st_up = hosted(_up_bwd, (d_up, w_up, x1, mod, small["norm2_w"], dx2, mix, w_out),
                                            scatter="w_down", swap=("w_up", g_up))
    g_out = _quarters(jnp.concatenate([_wgrad(y_ssd_t, d_mix, "wgrad_out_y"), _wgrad(u_t, d_mix, "wgrad_out_u")], axis=0))
    d_uconv, st_ln = _ln_silu_bwd(d_u, u_conv, small["conf_ln_w"], small["conf_ln_b"])
    d_conf, dw_conf, db_conf = hosted(_glu_conv_bwd, (proj, small["conf_conv_w"], d_uconv), scatter="w_up",
                                      swap=("w_out", g_out))
    d_y, d_z, st_gn = _ssd_gate_norm_bwd(d_yssd, y_scan, xbc_act, proj, dskip_row, small["ssd_norm_w"])
    d_xs, d_bc, d_dt, d_alog = hosted(_ssd_bwd, (xbc_act, proj, dtb_row, alog_row, states, d_y), scatter="w_out")
    d_xraw, dw_sx, db_sx = _ssd_conv_bwd_x(proj, small["ssd_conv_w"], small["ssd_conv_b"], d_xs, d_y, dskip_row)
    d_bcraw, dw_sbc, db_sbc = _ssd_conv_bwd_bc(proj, small["ssd_conv_w"], small["ssd_conv_b"], d_bc)
    g_in = _unpack_g_in(dict(
        z=_wgrad(h_t, d_z, "wgrad_in_z"), x=_wgrad(h_t, d_xraw, "wgrad_in_x"), bc=_wgrad(h_t, d_bcraw, "wgrad_in_bc"),
        conf=_wgrad_stacked(h_t, d_conf, "wgrad_in_conf", D), dt=_wgrad(h_t, d_dt, "wgrad_in_dt", bn=LANES)))
    g_in = g_in.reshape(4, 2, D // 2, W_IN_SHARD_PAD)
    if red is not None:
        red.swapped("w_in", _ride_alone(red.swap("w_in", g_in), "swap_w_in"))
    grad_x, st_in = hosted(_inproj_bwd, (d_z, d_xraw, d_bcraw, d_conf, d_dt, w_pack, x, mod, small["norm1_w"], dx1),
                           scatter="w_in", sums=("w_out", "w_up", "w_down"))

    gsmall = _pack_small_grads(st_in, st_up, st_down, st_ln, st_gn, d_alog, dw_sx, dw_sbc, db_sx, db_sbc, dw_conf, db_conf,
                               dw_ffn, db_ffn)
    gbig = None if reducer is not None else dict(w_in=g_in, w_out=g_out, w_up=g_up, w_down=g_down)
    return st_down[2, 0], grad_x, gbig, gsmall


VECTORS = ("ada_b", "norm1_w", "ssd_conv_b", "dt_bias", "a_log", "d_skip", "ssd_norm_w", "conf_conv_b", "conf_ln_w",
           "conf_ln_b", "norm2_w", "ffn_conv_b", "final_norm_w")
VECTOR_SIZES = (6 * D, D, D_XBC, HEADS, HEADS, HEADS, D, D, D, D, D, 2 * D_FF, D)
CONVS = {"ssd_conv_w": (K_SSD, D_XBC), "conf_conv_w": (K_CONF, D), "ffn_conv_w": (K_FFN, 2 * D_FF)}


def _pack_rows(items):
    n = -(-sum(w for _, w in items) // (8 * LANES)) * LANES
    while True:
        fill, place = [0] * 8, {}
        for key, w in sorted(items, key=lambda kv: -kv[1]):
            rows = [r for r in range(8) if fill[r] + w <= n]
            if not rows:
                break
            place[key] = (rows[0], fill[rows[0]])
            fill[rows[0]] += w
        if len(place) == len(items):
            return n, place
        n += LANES


FRONT_N, FRONT = _pack_rows([("c", D)] + [((nm, j), cols // 4) for nm, (taps, cols) in CONVS.items() for j in range(taps)])
BACK_N, BACK = _pack_rows([(nm, -(-sz // LANES) * LANES) for nm, sz in zip(VECTORS, VECTOR_SIZES)]
                          + [((nm, j), cols) for nm, (taps, cols) in CONVS.items() for j in range(taps)])
_VM = pltpu.CompilerParams(vmem_limit_bytes=VMEM_LIMIT)


def _pack_front(c, shards):
    def body(c_ref, *refs):
        o_ref = refs[-1]
        o_ref[...] = jnp.zeros_like(o_ref)
        r, o = FRONT["c"]
        o_ref[r:r + 1, o:o + D] = c_ref[...]
        for ref, (nm, (taps, cols)) in zip(refs, CONVS.items()):
            for j in range(taps):
                r, o = FRONT[(nm, j)]
                o_ref[r:r + 1, o:o + cols // 4] = ref[0, j:j + 1, :]

    return pl.pallas_call(body, name="pack_front", out_shape=jax.ShapeDtypeStruct((8, FRONT_N), f32),
                          compiler_params=_VM)(c, *shards)


def _unpack_front(got):
    def body(g_ref, c_ref, *outs):
        r, o = FRONT["c"]
        for d in range(8):
            c_ref[d:d + 1, :] = g_ref[8 * d + r:8 * d + r + 1, o:o + D]
        for ref, (nm, (taps, cols)) in zip(outs, CONVS.items()):
            cw = cols // 4
            for j in range(taps):
                r, o = FRONT[(nm, j)]
                for k in range(4):
                    ref[j:j + 1, k * cw:(k + 1) * cw] = g_ref[16 * k + r:16 * k + r + 1, o:o + cw]

    return pl.pallas_call(
        body, name="unpack_front", compiler_params=_VM,
        out_shape=(jax.ShapeDtypeStruct((8, D), f32),) + tuple(jax.ShapeDtypeStruct(tc, f32) for tc in CONVS.values()),
    )(got)


def _pack_small_grads(st_in, st_up, st_down, st_ln, st_gn, d_alog, dw_sx, dw_sbc, db_sx, db_sbc, dw_conf, db_conf, dw_ffn,
                      db_ffn):
    def body(in_ref, up_ref, dn_ref, ln_ref, gn_ref, al_ref, wx_ref, wbc_ref, bx_ref, bbc_ref, wc_ref, bc_ref, wf_ref, bf_ref,
             o_ref):
        def put(key, val, shift=0):
            r, o = BACK[key]
            o_ref[r:r + 1, o + shift:o + shift + val.shape[1]] = val

        o_ref[...] = jnp.zeros_like(o_ref)
        for i, piece in enumerate((in_ref[0:1, :], in_ref[1:2, :], up_ref[3:4, :], up_ref[0:1, :], up_ref[1:2, :],
                                   dn_ref[1:2, :])):
            put("ada_b", piece, i * D)
        put("norm1_w", in_ref[2:3, :])
        put("ssd_conv_b", bx_ref[...])
        put("ssd_conv_b", bbc_ref[...], D)
        put("dt_bias", in_ref[3:4, 0:LANES])
        put("a_log", al_ref[...])
        lane = lax.broadcasted_iota(jnp.int32, (1, LANES), 1)
        col = lax.broadcasted_iota(jnp.int32, (1, D), 1)
        per_col = gn_ref[1:2, :]
        d_skip = jnp.zeros((1, LANES), f32)
        for h in range(HEADS):
            in_head = jnp.logical_and(col >= h * HEAD_P, col < (h + 1) * HEAD_P)
            s = jnp.sum(jnp.where(in_head, per_col, 0.0), axis=1, keepdims=True)
            d_skip = d_skip + jnp.where(lane == h, s, 0.0)
        put("d_skip", d_skip)
        put("ssd_norm_w", gn_ref[0:1, :])
        put("conf_conv_b", bc_ref[...])
        put("conf_ln_w", ln_ref[0:1, :])
        put("conf_ln_b", ln_ref[1:2, :])
        put("norm2_w", up_ref[2:3, :])
        put("ffn_conv_b", bf_ref[0])
        put("ffn_conv_b", bf_ref[1], D_FF)
        put("final_norm_w", dn_ref[0:1, :])
        for j in range(K_SSD):
            put(("ssd_conv_w", j), wx_ref[j:j + 1, :])
            put(("ssd_conv_w", j), wbc_ref[j:j + 1, :], D)
        for j in range(K_CONF):
            put(("conf_conv_w", j), wc_ref[j:j + 1, :])
        for j in range(K_FFN):
            put(("ffn_conv_w", j), wf_ref[0, j:j + 1, :])
            put(("ffn_conv_w", j), wf_ref[1, j:j + 1, :], D_FF)

    return pl.pallas_call(body, name="pack_small_grads", out_shape=jax.ShapeDtypeStruct((8, BACK_N), f32), compiler_params=_VM)(
        st_in, st_up, st_down, st_ln, st_gn, d_alog, dw_sx, dw_sbc, db_sx, db_sbc, dw_conf, db_conf, dw_ffn, db_ffn)


def _small_adamw(got, chip, w, m, v):
    names = VECTORS + tuple(CONVS)
    n_par = len(names)

    def body(chip_ref, g_ref, *refs):
        ins, outs = refs[:3 * n_par], refs[3 * n_par:]
        dm_ref, outs = outs[0], outs[1:]
        chip_id = chip_ref[0]

        def summed(key, width):
            r, o = BACK[key]
            s = g_ref[r:r + 1, o:o + width]
            for d in range(1, 8):
                s = s + g_ref[8 * d + r:8 * d + r + 1, o:o + width]
            return s

        def mine(full, cw):
            out = full[:, 0:cw]
            for k in range(1, 4):
                out = jnp.where(chip_id == k, full[:, k * cw:(k + 1) * cw], out)
            return out

        r, o = BACK["ada_b"]
        for d in range(8):
            dm_ref[d:d + 1, :] = mine(g_ref[8 * d + r:8 * d + r + 1, o:o + 6 * D], 6 * D // 4)
        for i, (nm, size) in enumerate(zip(VECTORS, VECTOR_SIZES)):
            g = summed(nm, -(-size // LANES) * LANES)[:, 0:size]
            res = _adam_math(ins[3 * i][...], g, ins[3 * i + 1][...], ins[3 * i + 2][...])
            for ref, val in zip(outs[4 * i:4 * i + 4], (g,) + res):
                ref[...] = val
        for i, (nm, (taps, cols)) in enumerate(CONVS.items(), start=len(VECTORS)):
            for j in range(taps):
                g = mine(summed((nm, j), cols), cols // 4)
                res = _adam_math(ins[3 * i][0, j:j + 1, :], g, ins[3 * i + 1][0, j:j + 1, :], ins[3 * i + 2][0, j:j + 1, :])
                for ref, val in zip(outs[4 * i:4 * i + 4], (g,) + res):
                    ref[0, j:j + 1, :] = val

    params = [a[nm] for nm in names for a in (w, m, v)]
    whole = lambda s: pl.BlockSpec(s, lambda i, chip, nd=len(s): (0,) * nd)
    out_shape = [jax.ShapeDtypeStruct((8, 6 * D // 4), f32)] + [jax.ShapeDtypeStruct(w[nm].shape, f32) for nm in names for _ in range(4)]
    outs = pl.pallas_call(
        body, name="small_adamw", out_shape=tuple(out_shape), compiler_params=_VM,
        grid_spec=pltpu.PrefetchScalarGridSpec(
            num_scalar_prefetch=1, grid=(1,), in_specs=[whole(got.shape)] + [whole(p.shape) for p in params],
            out_specs=tuple(whole(s.shape) for s in out_shape)),
    )(_scalar(chip), got, *params)
    return outs[0], {nm: outs[1 + 4 * i:5 + 4 * i] for i, nm in enumerate(names)}


W_IN_COLS = 4624
W_IN_SHARD = W_IN_COLS // 4
W_IN_SHARD_PAD = 1280
_SEGMENTS = ((0, 1024, OFF_Z), (1024, 2560, OFF_XBC), (2560, 2576, OFF_DT), (2576, 3600, OFF_CA), (3600, 4624, OFF_CG))


def _in_pieces(bounds=()):
    out = []
    for k in range(4):
        s0, s1 = k * W_IN_SHARD, (k + 1) * W_IN_SHARD
        for lo, hi, off in _SEGMENTS:
            a, b = max(lo, s0), min(hi, s1)
            while a < b:
                p = off + a - lo
                e = min([b - a] + [c - p for c in bounds if c > p])
                out.append((k, a - s0, p, e))
                a += e
    return out


def _pack_w_in(shards):
    pieces = _in_pieces()

    def body(s_ref, o_ref):
        o_ref[:, OFF_DT:W_PACK] = jnp.zeros((TM, W_PACK - OFF_DT), MX)
        for k, c, p, n in pieces:
            o_ref[:, p:p + n] = s_ref[k, :, c:c + n]

    return pl.pallas_call(
        body, name="pack_w_in", grid=(D // TM,), out_shape=jax.ShapeDtypeStruct((D, W_PACK), MX),
        in_specs=[pl.BlockSpec((4, TM, W_IN_SHARD_PAD), lambda i: (0, i, 0))],
        out_specs=pl.BlockSpec((TM, W_PACK), lambda i: (i, 0)), compiler_params=_cp("arbitrary"),
    )(shards)


def _unpack_g_in(g):
    srcs = ((OFF_Z, D), (OFF_XBC, D), (OFF_XBC + D, 2 * CW), (OFF_CA, D), (OFF_CG, D), (OFF_DT, LANES))
    pieces = _in_pieces(tuple(o for o, _ in srcs) + tuple(o + n for o, n in srcs))

    def body(z_ref, x_ref, bc_ref, cf_ref, dt_ref, o_ref):
        read = (lambda lo, hi: z_ref[:, lo:hi], lambda lo, hi: x_ref[:, lo:hi], lambda lo, hi: bc_ref[:, lo:hi],
                lambda lo, hi: cf_ref[0, :, lo:hi], lambda lo, hi: cf_ref[1, :, lo:hi], lambda lo, hi: dt_ref[:, lo:hi])
        o_ref[:, :, W_IN_SHARD - 4:W_IN_SHARD_PAD] = jnp.zeros((4, TM, W_IN_SHARD_PAD - W_IN_SHARD + 4), MX)
        for k, c, p, n in pieces:
            i = [q for q, (o, w) in enumerate(srcs) if o <= p < o + w][0]
            o_ref[k, :, c:c + n] = read[i](p - srcs[i][0], p - srcs[i][0] + n)

    blk = lambda w: pl.BlockSpec((TM, w), lambda i: (i, 0))
    return pl.pallas_call(
        body, name="unpack_g_in", grid=(D // TM,), out_shape=jax.ShapeDtypeStruct((4, D, W_IN_SHARD_PAD), MX),
        in_specs=[blk(D), blk(D), blk(2 * CW), pl.BlockSpec((2, TM, D), lambda i: (0, i, 0)), blk(LANES)],
        out_specs=pl.BlockSpec((4, TM, W_IN_SHARD_PAD), lambda i: (0, i, 0)), compiler_params=_cp("arbitrary"),
    )(g["z"], g["x"], g["bc"], g["conf"], g["dt"])


def _scalar(v):
    return jnp.reshape(v, (1,)).astype(jnp.int32)


def _cast_into_slot(w, width, chip):
    r, c = w.shape
    h = r // 2
    tm = _row_tile(h)
    nj = h // tm

    def body(chip_ref, w_ref, o_ref):
        v = w_ref[...].astype(MX)
        o_ref[0, 0] = v if width == c else jnp.concatenate([v, jnp.zeros((tm, width - c), MX)], axis=1)

    return pl.pallas_call(
        body, name=f"cast_into_slot_{r}x{c}", out_shape=jax.ShapeDtypeStruct((4, 2, h, width), MX),
        grid_spec=pltpu.PrefetchScalarGridSpec(
            num_scalar_prefetch=1, grid=(2, nj),
            in_specs=[pl.BlockSpec((tm, c), lambda i, j, chip: (i * nj + j, 0))],
            out_specs=pl.BlockSpec((1, 1, tm, width), lambda i, j, chip: (chip[0], i, j, 0))),
        compiler_params=_cp("arbitrary", "arbitrary"),
    )(_scalar(chip), w)


ANY = pl.BlockSpec(memory_space=pl.ANY)


def _place():
    x, y, c = lax.axis_index("x"), lax.axis_index("y"), lax.axis_index("c")
    return x, y, c, [(1 - x, y), (x, 1 - y), (1 - x, 1 - y)]


def _gather_rows(block):
    m_per, n = block.shape

    def body(x_ref, out_ref, send_sems, recv_sems, local_sem):
        x, y, c, chips = _place()
        me, sibling = (x, y, c), (x, y, 1 - c)

        def rows(px, py, pc):
            return out_ref.at[pl.ds((4 * px + 2 * py + pc) * m_per, m_per), :]

        def copy(k, blk, to, src=None):
            return pltpu.make_async_remote_copy(
                src_ref=rows(*blk) if src is None else src, dst_ref=rows(*blk), send_sem=send_sems.at[k],
                recv_sem=recv_sems.at[k], device_id=to, device_id_type=MESH)

        mine = pltpu.make_async_copy(x_ref, rows(*me), local_sem)
        mine.start()
        first = [copy(0, me, sibling, src=x_ref)]
        first += [copy(1 + j, me, (*chip, c), src=x_ref) for j, chip in enumerate(chips)]
        for cp in first:
            cp.start()
        passed = [copy(4 + j, (*chip, c), sibling) for j, chip in enumerate(chips)]
        for j, chip in enumerate(chips):
            copy(1 + j, (*chip, c), me).wait_recv()
            passed[j].start()
        copy(0, sibling, me).wait_recv()
        for j, chip in enumerate(chips):
            copy(4 + j, (*chip, 1 - c), me).wait_recv()
        for cp in first + passed:
            cp.wait_send()
        mine.wait()

    return pl.pallas_call(
        body, name=f"gather_rows_{m_per}x{n}", out_shape=jax.ShapeDtypeStruct((8 * m_per, n), block.dtype),
        in_specs=[pl.BlockSpec(memory_space=pltpu.VMEM)], out_specs=pl.BlockSpec(memory_space=pltpu.VMEM),
        scratch_shapes=[pltpu.SemaphoreType.DMA((7,)), pltpu.SemaphoreType.DMA((7,)), pltpu.SemaphoreType.DMA],
        compiler_params=pltpu.CompilerParams(vmem_limit_bytes=VMEM_LIMIT),
    )(block)


class _GatherRider:
    def __init__(self, slots):
        n = len(slots)
        self.n = n
        self.inputs = list(slots)
        self.out_shape = [jax.ShapeDtypeStruct(s.shape, s.dtype) for s in slots]
        self.scratch = [pltpu.SemaphoreType.DMA((n, 6)), pltpu.SemaphoreType.DMA((n, 6))]
        self.aliases = {a: a for a in range(n)}

    @staticmethod
    def _copy(outs, sems, a, j, k, half, to):
        dst = outs[a].at[k, half]
        return pltpu.make_async_remote_copy(src_ref=dst, dst_ref=dst, send_sem=sems[0].at[a, j], recv_sem=sems[1].at[a, j],
                                            device_id=to, device_id_type=MESH)

    def _first(self, outs, sems):
        x, y, c, chips = _place()
        return [self._copy(outs, sems, a, j, 2 * x + y, c, (*chip, c)) for a in range(self.n) for j, chip in enumerate(chips)]

    def start(self, ins, outs, sems):
        for cp in self._first(outs, sems):
            cp.start()

    def finish(self, ins, outs, sems):
        x, y, c, chips = _place()
        passed = []
        for a in range(self.n):
            for j, (px, py) in enumerate(chips):
                self._copy(outs, sems, a, j, 2 * px + py, c, (x, y, c)).wait_recv()
                fwd = self._copy(outs, sems, a, 3 + j, 2 * px + py, c, (x, y, 1 - c))
                fwd.start()
                passed.append(fwd)
        for a in range(self.n):
            for j, (px, py) in enumerate(chips):
                self._copy(outs, sems, a, 3 + j, 2 * px + py, 1 - c, (x, y, c)).wait_recv()
        for cp in self._first(outs, sems) + passed:
            cp.wait_send()


class _ScatterRider:
    def __init__(self, parts, row0=0, nrows=None):
        n = len(parts)
        self.n = n
        self.rows = (row0, parts[0].shape[1] - row0 if nrows is None else nrows)
        self.inputs = list(parts)
        self.out_shape = [jax.ShapeDtypeStruct((3, self.rows[1], p.shape[2]), p.dtype) for p in parts]
        self.scratch = [pltpu.SemaphoreType.DMA((n, 3)), pltpu.SemaphoreType.DMA((n, 3))]
        self.aliases = {}

    def _copies(self, ins, outs, sems):
        x, y, c, chips = _place()
        return [pltpu.make_async_remote_copy(
            src_ref=ins[a].at[2 * px + py, pl.ds(*self.rows)], dst_ref=outs[a].at[j], send_sem=sems[0].at[a, j],
            recv_sem=sems[1].at[a, j], device_id=(px, py, c), device_id_type=MESH)
            for a in range(self.n) for j, (px, py) in enumerate(chips)]

    def start(self, ins, outs, sems):
        for cp in self._copies(ins, outs, sems):
            cp.start()

    def finish(self, ins, outs, sems):
        for cp in self._copies(ins, outs, sems):
            cp.wait()


def _ride_alone(rider, name):
    n = len(rider.inputs)

    def body(*refs):
        ins, outs, sems = refs[:n], refs[n:n + len(rider.out_shape)], refs[n + len(rider.out_shape):]
        rider.start(ins, outs, sems)
        rider.finish(ins, outs, sems)

    return pl.pallas_call(
        body, name=name, out_shape=tuple(rider.out_shape), in_specs=[ANY] * n, out_specs=tuple([ANY] * len(rider.out_shape)),
        input_output_aliases=dict(rider.aliases), scratch_shapes=list(rider.scratch),
    )(*rider.inputs)


class _SwapRider:
    def __init__(self, grads):
        n = len(grads)
        self.n = n
        self.inputs = list(grads)
        self.out_shape = [jax.ShapeDtypeStruct((4,) + g.shape[2:], g.dtype) for g in grads]
        self.scratch = [pltpu.SemaphoreType.DMA((n, 4)), pltpu.SemaphoreType.DMA((n, 4))]
        self.aliases = {}

    def _copies(self, ins, outs, sems):
        x, y, c, _ = _place()
        return [pltpu.make_async_remote_copy(
            src_ref=ins[a].at[k, 1 - c], dst_ref=outs[a].at[k], send_sem=sems[0].at[a, k], recv_sem=sems[1].at[a, k],
            device_id=(x, y, 1 - c), device_id_type=MESH) for a in range(self.n) for k in range(4)]

    def start(self, ins, outs, sems):
        for cp in self._copies(ins, outs, sems):
            cp.start()

    def finish(self, ins, outs, sems):
        for cp in self._copies(ins, outs, sems):
            cp.wait()


class _Riders:
    def __init__(self, riders):
        self.riders = list(riders)
        self.inputs = [a for r in riders for a in r.inputs]
        self.out_shape = [s for r in riders for s in r.out_shape]
        self.scratch = [s for r in riders for s in r.scratch]
        self.aliases = {}
        i = o = 0
        for r in riders:
            self.aliases.update({i + a: o + b for a, b in r.aliases.items()})
            i, o = i + len(r.inputs), o + len(r.out_shape)

    def _each(self, ins, outs, sems):
        i = o = s = 0
        for r in self.riders:
            yield r, ins[i:i + len(r.inputs)], outs[o:o + len(r.out_shape)], sems[s:s + len(r.scratch)]
            i, o, s = i + len(r.inputs), o + len(r.out_shape), s + len(r.scratch)

    def start(self, ins, outs, sems):
        for r, a, b, c in self._each(ins, outs, sems):
            r.start(a, b, c)

    def finish(self, ins, outs, sems):
        for r, a, b, c in self._each(ins, outs, sems):
            r.finish(a, b, c)

    def split(self, outs):
        res, o = [], 0
        for r in self.riders:
            res.append(outs[o:o + len(r.out_shape)])
            o += len(r.out_shape)
        return res


class _Reducer:
    def __init__(self, chip, core):
        self.chip, self.core, self.grads, self.parts, self.sums, self.others = chip, core, {}, {}, {}, {}

    def swap(self, name, grad):
        self.grads[name] = grad
        return _SwapRider([grad])

    def swapped(self, name, got):
        self.parts[name] = _add_pair(self.grads[name], got[0], self.core, name)

    def scatter(self, name, row0=0, nrows=None):
        return _ScatterRider([self.parts[name]], row0, nrows)

    def scattered(self, name, others):
        self.sums[name] = _add_chips(self.parts[name], others[0], self.chip, name)


class _SwapSumsRider:
    def __init__(self, halves):
        n = len(halves)
        self.n = n
        self.inputs = list(halves)
        self.out_shape = [jax.ShapeDtypeStruct(s.shape, s.dtype) for s in halves]
        self.scratch = [pltpu.SemaphoreType.DMA((n,)), pltpu.SemaphoreType.DMA((n,))]
        self.aliases = {}

    def _copies(self, ins, outs, sems):
        x, y, c, _ = _place()
        return [pltpu.make_async_remote_copy(
            src_ref=ins[a], dst_ref=outs[a], send_sem=sems[0].at[a], recv_sem=sems[1].at[a],
            device_id=(x, y, 1 - c), device_id_type=MESH) for a in range(self.n)]

    def start(self, ins, outs, sems):
        for cp in self._copies(ins, outs, sems):
            cp.start()

    def finish(self, ins, outs, sems):
        for cp in self._copies(ins, outs, sems):
            cp.wait()


def _row_tile(r):
    for tm in (TM, 176, 128, 64, 32, 16, 8):
        if r % tm == 0:
            return tm
    return r


def _add_pair(mine, got, core, name):
    k, _, h, c = mine.shape
    tm = _row_tile(h)

    def body(core_ref, a_ref, b_ref, o_ref):
        o_ref[0] = (a_ref[0, 0].astype(f32) + b_ref[0].astype(f32)).astype(MX)

    blk = pl.BlockSpec((1, tm, c), lambda i, j, core: (i, j, 0))
    return pl.pallas_call(
        body, name="add_pair_" + name, out_shape=jax.ShapeDtypeStruct((k, h, c), MX),
        grid_spec=pltpu.PrefetchScalarGridSpec(
            num_scalar_prefetch=1, grid=(k, h // tm),
            in_specs=[pl.BlockSpec((1, 1, tm, c), lambda i, j, core: (i, core[0], j, 0)), blk], out_specs=blk),
        compiler_params=_cp("arbitrary", "arbitrary"),
    )(_scalar(core), mine, got)


def _add_chips(parts, others, chip, name, row0=0):
    _, n, c = others.shape
    tm = _row_tile(n)
    assert row0 % tm == 0
    i0 = row0 // tm

    def body(chip_ref, a_ref, b_ref, o_ref):
        s = a_ref[0].astype(f32) + b_ref[0].astype(f32)
        o_ref[...] = (s + b_ref[1].astype(f32)) + b_ref[2].astype(f32)

    return pl.pallas_call(
        body, name="add_chips_" + name, out_shape=jax.ShapeDtypeStruct((n, c), f32),
        grid_spec=pltpu.PrefetchScalarGridSpec(
            num_scalar_prefetch=1, grid=(n // tm,),
            in_specs=[pl.BlockSpec((1, tm, c), lambda i, chip: (chip[0], i0 + i, 0)),
                      pl.BlockSpec((3, tm, c), lambda i, chip: (0, i, 0))],
            out_specs=pl.BlockSpec((tm, c), lambda i, chip: (i, 0))),
        compiler_params=_cp("arbitrary"),
    )(_scalar(chip), parts, others)


def _adam_math(w, g, m, v):
    m = ADAM_B1 * m + (1.0 - ADAM_B1) * g
    v = ADAM_B2 * v + (1.0 - ADAM_B2) * (g * g)
    m_hat = m / (1.0 - ADAM_B1 ** ADAM_STEP)
    v_hat = v / (1.0 - ADAM_B2 ** ADAM_STEP)
    return -ADAM_LR * (m_hat / (jnp.sqrt(v_hat) + ADAM_EPS) + ADAM_WD * w), m, v


def _adamw_halves(w, mine, other, m, v, core, name, rider=None):
    r, c = w.shape
    h = r // 2
    tm = _row_tile(h)
    nj = h // tm
    cg = mine.shape[1]

    def body(core_ref, w_ref, a_ref, b_ref, m_ref, v_ref, g_ref, d_ref, nm_ref, nv_ref):
        g = jnp.where(pl.program_id(0) == core_ref[0], a_ref[:, 0:c], b_ref[:, 0:c])
        g_ref[...] = g
        d_ref[...], nm_ref[...], nv_ref[...] = _adam_math(w_ref[...], g, m_ref[...], v_ref[...])

    blk = pl.BlockSpec((tm, c), lambda i, j, core: (i * nj + j, 0))
    gblk = pl.BlockSpec((tm, cg), lambda i, j, core: (j, 0))
    return _call(body, name=name, grid=(2, nj), out_shape=[jax.ShapeDtypeStruct((r, c), f32)] * 4,
                 in_specs=[blk, gblk, gblk, blk, blk], out_specs=(blk,) * 4, sem=("arbitrary", "arbitrary"),
                 prefetch=(_scalar(core),), args=(w, mine, other, m, v), rider=rider)


def _ada_forward(c_all, ada_w):
    def body(c_ref, w_ref, o_ref):
        o_ref[...] = jnp.dot(_silu(c_ref[...]).astype(MX), w_ref[...].astype(MX), preferred_element_type=f32)

    return pl.pallas_call(body, name="ada_forward", out_shape=jax.ShapeDtypeStruct((8, ada_w.shape[1]), f32),
                          compiler_params=pltpu.CompilerParams(vmem_limit_bytes=VMEM_LIMIT))(c_all, ada_w)


def _ada_adamw(c_all_t, d_mod, w, m, v, rider=None):
    r, c = w.shape
    tm = TM

    def body(ct_ref, dm_ref, w_ref, m_ref, v_ref, g_ref, d_ref, nm_ref, nv_ref):
        ca = _silu(ct_ref[...])
        g = ca[:, 0:1] * dm_ref[0:1, :]
        for b in range(1, 8):
            g = g + ca[:, b:b + 1] * dm_ref[b:b + 1, :]
        g_ref[...] = g
        d_ref[...], nm_ref[...], nv_ref[...] = _adam_math(w_ref[...], g, m_ref[...], v_ref[...])

    blk = pl.BlockSpec((tm, c), lambda i: (i, 0))
    return _call(body, name="ada_adamw", grid=(r // tm,), out_shape=[jax.ShapeDtypeStruct((r, c), f32)] * 4,
                 in_specs=[pl.BlockSpec((tm, 8), lambda i: (i, 0)), pl.BlockSpec((8, c), lambda i: (0, 0)), blk, blk, blk],
                 out_specs=(blk,) * 4, sem=("arbitrary",), args=(c_all_t, d_mod, w, m, v), rider=rider)


WEIGHTS = ("ada_w", "ada_b", "norm1_w", "w_in", "ssd_conv_w", "ssd_conv_b", "dt_bias", "a_log", "d_skip", "ssd_norm_w",
           "conf_conv_w", "conf_conv_b", "conf_ln_w", "conf_ln_b", "w_out", "norm2_w", "w_up", "ffn_conv_w", "ffn_conv_b",
           "w_down", "final_norm_w")


def kernel(x, c, ada_w, ada_b, norm1_w, w_in, ssd_conv_w, ssd_conv_b, dt_bias, a_log, d_skip, ssd_norm_w, conf_conv_w, conf_conv_b, conf_ln_w, conf_ln_b, w_out, norm2_w, w_up, ffn_conv_w, ffn_conv_b, w_down, final_norm_w, loss_target, m_ada_w, m_ada_b, m_norm1_w, m_w_in, m_ssd_conv_w, m_ssd_conv_b, m_dt_bias, m_a_log, m_d_skip, m_ssd_norm_w, m_conf_conv_w, m_conf_conv_b, m_conf_ln_w, m_conf_ln_b, m_w_out, m_norm2_w, m_w_up, m_ffn_conv_w, m_ffn_conv_b, m_w_down, m_final_norm_w, v_ada_w, v_ada_b, v_norm1_w, v_w_in, v_ssd_conv_w, v_ssd_conv_b, v_dt_bias, v_a_log, v_d_skip, v_ssd_norm_w, v_conf_conv_w, v_conf_conv_b, v_conf_ln_w, v_conf_ln_b, v_w_out, v_norm2_w, v_w_up, v_ffn_conv_w, v_ffn_conv_b, v_w_down, v_final_norm_w):
    given = dict(locals())
    w = {n: given[n] for n in WEIGHTS}
    mom = {n: given["m_" + n] for n in WEIGHTS}
    var = {n: given["v_" + n] for n in WEIGHTS}
    chip = 2 * lax.axis_index("x") + lax.axis_index("y")
    me = 2 * chip + lax.axis_index("c")

    c_all, *convs = _unpack_front(_gather_rows(_pack_front(c, [w[n] for n in CONVS])))
    conv_full = dict(zip(CONVS, convs))

    mod_cols = _gather_rows(_ada_forward(c_all, ada_w[0])).reshape(8, 8, -1)[0::2]
    mod = lax.dynamic_index_in_dim(mod_cols, me, axis=1, keepdims=False).reshape(1, 6 * D) + ada_b

    core = lax.axis_index("c")
    a_in, = _ride_alone(_GatherRider([_cast_into_slot(w_in[0], W_IN_SHARD_PAD, chip)]), "gather_w_in")
    w_pack = _pack_w_in(a_in.reshape(4, D, W_IN_SHARD_PAD))
    late = (_cast_into_slot(w_out[0], D, chip), _cast_into_slot(w_up[0], UP_SHARD, chip), _cast_into_slot(w_down[0], D, chip))

    flat = lambda a: a.reshape(1, -1) if a.ndim == 1 else a
    small = {n: flat(w[n]) for n in VECTORS if n != "ada_b"}
    small.update(conv_full)
    reducer = _Reducer(chip, core)
    loss_mine, grad_x, _, gsmall = _local_step(x[0], mod, loss_target[0], w_pack, late, small, reducer)
    loss = lax.psum(loss_mine, ("x", "y", "c"))
    grads, delta, new_m, new_v = {}, {}, {}, {}

    names = VECTORS + tuple(CONVS)
    d_mod_mine, res = _small_adamw(_gather_rows(gsmall), chip, *[{n: flat(d[n]) for n in names} for d in (w, mom, var)])
    for n in names:
        grads[n], delta[n], new_m[n], new_v[n] = [r.reshape(w[n].shape) for r in res[n]]

    reducer.others["w_in"], = _ride_alone(_SwapSumsRider([reducer.sums["w_in"]]), "swap_sums_w_in")
    for n in ("w_in", "w_out", "w_up", "w_down"):
        res, _ = _adamw_halves(w[n][0], reducer.sums[n], reducer.others[n], mom[n][0], var[n][0], core, "adamw_" + n)
        grads[n], delta[n], new_m[n], new_v[n] = [r[None] for r in res]
    res, _ = _ada_adamw(c_all.T, d_mod_mine, ada_w[0], m_ada_w[0], v_ada_w[0])
    grads["ada_w"], delta["ada_w"], new_m["ada_w"], new_v["ada_w"] = [r[None] for r in res]

    return (loss, grad_x[None], *[grads[n] for n in WEIGHTS], *[delta[n] for n in WEIGHTS],
            *[new_m[n] for n in WEIGHTS], *[new_v[n] for n in WEIGHTS])
```

```python
import functools

import jax
import jax.numpy as jnp
from jax import lax
from jax.experimental import pallas as pl
from jax.experimental.pallas import tpu as pltpu

f32 = jnp.float32
MX = jnp.bfloat16

D = 1024
HEADS = 16
HEAD_P = 64
STATE_N = 128
D_XBC = 1536
D_FF = 2816
UP_SHARD = 2 * D_FF // 4
K_SSD, K_CONF, K_FFN = 4, 31, 3
CHUNK = 128
OFF_Z, OFF_XBC, OFF_CA, OFF_CG, OFF_DT = 0, 1024, 2560, 3584, 4608
W_PACK = 4736
TM = 256
CW = 256
RC = 64
LANES = 128
VMEM_LIMIT = 56 * 1024 * 1024

ADAM_LR, ADAM_B1, ADAM_B2, ADAM_EPS, ADAM_WD, ADAM_STEP = 0.001, 0.9, 0.999, 1e-08, 0.01, 10

MESH = pl.DeviceIdType.MESH


def _cp(*sem):
    return pltpu.CompilerParams(dimension_semantics=sem, vmem_limit_bytes=VMEM_LIMIT)


def _resident(shape):
    nd = len(shape)
    return pl.BlockSpec(shape, lambda *_: (0,) * nd, pipeline_mode=pl.Buffered(1))


def _row(width=D):
    return pl.BlockSpec((1, width), lambda *_: (0, 0))


def _call(body, *, name, grid, in_specs, out_specs, out_shape, args, sem, scratch_shapes=(), prefetch=(), rider=None):
    ni, no, ns, npf = len(in_specs), len(out_specs), len(scratch_shapes), len(prefetch)
    ri, ro = (len(rider.inputs), len(rider.out_shape)) if rider is not None else (0, 0)

    def full(*refs):
        pre, refs = refs[:npf], refs[npf:]
        base_in, r_in = refs[:ni], refs[ni:ni + ri]
        base_out, r_out = refs[ni + ri:ni + ri + no], refs[ni + ri + no:ni + ri + no + ro]
        base_scr, r_scr = refs[ni + ri + no + ro:ni + ri + no + ro + ns], refs[ni + ri + no + ro + ns:]
        if rider is None:
            return body(*pre, *base_in, *base_out, *base_scr)
        ids = [pl.program_id(a) for a in range(len(grid))]
        first = functools.reduce(jnp.logical_and, [i == 0 for i in ids])
        last = functools.reduce(jnp.logical_and, [i == g - 1 for i, g in zip(ids, grid)])

        @pl.when(first)
        def _():
            rider.start(r_in, r_out, r_scr)

        body(*pre, *base_in, *base_out, *base_scr)

        @pl.when(last)
        def _():
            rider.finish(r_in, r_out, r_scr)

    extra = dict(shapes=[], scratch=[], aliases={}, inputs=[]) if rider is None else dict(
        shapes=rider.out_shape, scratch=rider.scratch, inputs=rider.inputs,
        aliases={npf + ni + i: no + j for i, j in rider.aliases.items()})
    outs = pl.pallas_call(
        full, name=name, out_shape=tuple(out_shape) + tuple(extra["shapes"]), input_output_aliases=extra["aliases"],
        grid_spec=pltpu.PrefetchScalarGridSpec(
            num_scalar_prefetch=npf, grid=grid, in_specs=list(in_specs) + [ANY] * ri,
            out_specs=tuple(out_specs) + (ANY,) * ro, scratch_shapes=list(scratch_shapes) + list(extra["scratch"])),
        compiler_params=_cp(*sem),
    )(*prefetch, *args, *extra["inputs"])
    return tuple(outs[:no]), tuple(outs[no:])


def _silu(v):
    return v * jax.nn.sigmoid(v)


def _dsilu(v):
    s = jax.nn.sigmoid(v)
    return s * (1.0 + v * (1.0 - s))


def _softplus(v):
    return jnp.maximum(v, 0.0) + jnp.log1p(jnp.exp(-jnp.abs(v)))


def _mm(a, b):
    return jnp.dot(a.astype(MX), b.astype(MX), preferred_element_type=f32)


def _mm_nt(a, b):
    return lax.dot_general(a.astype(MX), b.astype(MX), (((1,), (1,)), ((), ())), preferred_element_type=f32)


def _mm_tn(a, b):
    return lax.dot_general(a.astype(MX), b.astype(MX), (((0,), (0,)), ((), ())), preferred_element_type=f32)


def _ln_inproj(x, mod, norm1_w, w_pack):
    t = x.shape[0]

    def body(x_ref, mod_ref, nw_ref, w_ref, proj_ref, ht_ref):
        xv = x_ref[...]
        rstd = lax.rsqrt(jnp.mean(xv * xv, axis=-1, keepdims=True) + 1e-6)
        h = (xv * rstd * nw_ref[...]) * (1.0 + mod_ref[:, D:2 * D]) + mod_ref[:, 0:D]
        hb = h.astype(MX)
        ht_ref[...] = hb.T
        proj_ref[...] = jnp.dot(hb, w_ref[...], preferred_element_type=f32)

    return pl.pallas_call(
        body, name="ln_inproj", grid=(t // TM,),
        out_shape=(jax.ShapeDtypeStruct((t, W_PACK), f32), jax.ShapeDtypeStruct((D, t), MX)),
        in_specs=[pl.BlockSpec((TM, D), lambda i: (i, 0)), _row(6 * D), _row(), _resident((D, W_PACK))],
        out_specs=(pl.BlockSpec((TM, W_PACK), lambda i: (i, 0)), pl.BlockSpec((D, TM), lambda i: (0, i))),
        compiler_params=_cp("arbitrary"),
    )(x, mod, norm1_w, w_pack)


def _ssd_gate_norm(y_scan, xbc_act, proj, d_skip_row, ssd_norm_w):
    t = y_scan.shape[0]

    def body(y_ref, xs_ref, z_ref, dsk_ref, nw_ref, o_ref, ot_ref):
        y = y_ref[...] + xs_ref[...] * dsk_ref[...]
        yz = y * _silu(z_ref[...])
        rstd = lax.rsqrt(jnp.mean(yz * yz, axis=-1, keepdims=True) + 1e-6)
        out = (yz * rstd * nw_ref[...]).astype(MX)
        o_ref[...] = out
        ot_ref[...] = out.T

    blk = pl.BlockSpec((TM, D), lambda i: (i, 0))
    return pl.pallas_call(
        body, name="ssd_gate_norm", grid=(t // TM,),
        out_shape=(jax.ShapeDtypeStruct((t, D), MX), jax.ShapeDtypeStruct((D, t), MX)),
        in_specs=[blk, blk, blk, _row(), _row()], out_specs=(blk, pl.BlockSpec((D, TM), lambda i: (0, i))),
        compiler_params=_cp("arbitrary"),
    )(y_scan, xbc_act, proj, d_skip_row, ssd_norm_w)


def _ln_silu(u_conv, ln_w, ln_b):
    t = u_conv.shape[0]

    def body(u_ref, w_ref, b_ref, o_ref, ot_ref):
        u = u_ref[...]
        mu = jnp.mean(u, axis=-1, keepdims=True)
        uc = u - mu
        rstd = lax.rsqrt(jnp.mean(uc * uc, axis=-1, keepdims=True) + 1e-5)
        out = _silu(uc * rstd * w_ref[...] + b_ref[...]).astype(MX)
        o_ref[...] = out
        ot_ref[...] = out.T

    blk = pl.BlockSpec((TM, D), lambda i: (i, 0))
    return pl.pallas_call(
        body, name="ln_silu", grid=(t // TM,),
        out_shape=(jax.ShapeDtypeStruct((t, D), MX), jax.ShapeDtypeStruct((D, t), MX)),
        in_specs=[blk, _row(), _row()], out_specs=(blk, pl.BlockSpec((D, TM), lambda i: (0, i))),
        compiler_params=_cp("arbitrary"),
    )(u_conv, ln_w, ln_b)


def _outproj_ln2_up(y_ssd, u, w_out, x, mod, norm2_w, w_up):
    t = x.shape[0]

    def body(y_ref, u_ref, wo_ref, x_ref, mod_ref, nw_ref, wu_ref, mix_ref, x1_ref, h2t_ref, up_ref):
        mix = jnp.dot(y_ref[...], wo_ref[0:D, :], preferred_element_type=f32)
        mix = mix + jnp.dot(u_ref[...], wo_ref[D:2 * D, :], preferred_element_type=f32)
        mix_ref[...] = mix
        x1 = x_ref[...] + mod_ref[:, 2 * D:3 * D] * mix
        x1_ref[...] = x1
        rstd = lax.rsqrt(jnp.mean(x1 * x1, axis=-1, keepdims=True) + 1e-6)
        h2 = ((x1 * rstd * nw_ref[...]) * (1.0 + mod_ref[:, 4 * D:5 * D]) + mod_ref[:, 3 * D:4 * D]).astype(MX)
        h2t_ref[...] = h2.T
        for k in range(4):
            up_ref[:, k * UP_SHARD:(k + 1) * UP_SHARD] = jnp.dot(h2, wu_ref[k], preferred_element_type=f32)

    blk = pl.BlockSpec((TM, D), lambda i: (i, 0))
    return pl.pallas_call(
        body, name="outproj_ln2_up", grid=(t // TM,),
        out_shape=(jax.ShapeDtypeStruct((t, D), f32), jax.ShapeDtypeStruct((t, D), f32),
                   jax.ShapeDtypeStruct((D, t), MX), jax.ShapeDtypeStruct((t, 2 * D_FF), f32)),
        in_specs=[blk, blk, _resident((2 * D, D)), blk, _row(6 * D), _row(), _resident((4, D, UP_SHARD))],
        out_specs=(blk, blk, pl.BlockSpec((D, TM), lambda i: (0, i)), pl.BlockSpec((TM, 2 * D_FF), lambda i: (i, 0))),
        compiler_params=_cp("arbitrary"),
    )(y_ssd, u, w_out, x, mod, norm2_w, w_up)


def _down_loss(act, w_down, x1, mod, final_norm_w, target):
    t = x1.shape[0]

    def body(a_ref, wd_ref, x1_ref, mod_ref, wf_ref, tgt_ref, dx2_ref, dffn_ref, dact_ref, st_ref):
        @pl.when(pl.program_id(0) == 0)
        def _():
            st_ref[...] = jnp.zeros_like(st_ref)

        g2 = mod_ref[:, 5 * D:6 * D]
        ffn = jnp.dot(a_ref[...], wd_ref[...], preferred_element_type=f32)
        x2 = x1_ref[...] + g2 * ffn
        rstd = lax.rsqrt(jnp.mean(x2 * x2, axis=-1, keepdims=True) + 1e-6)
        xh = x2 * rstd
        wf = wf_ref[...]
        err = xh * wf - tgt_ref[...]
        dy = err * (1.0 / D)
        dxh = dy * wf
        dx2 = rstd * (dxh - xh * jnp.mean(dxh * xh, axis=-1, keepdims=True))
        dx2_ref[...] = dx2
        dffn = (g2 * dx2).astype(MX)
        dffn_ref[...] = dffn
        dact_ref[...] = lax.dot_general(dffn, wd_ref[...], (((1,), (1,)), ((), ())), preferred_element_type=f32)
        st_ref[0:1, :] += jnp.sum(dy * xh, axis=0, keepdims=True)
        st_ref[1:2, :] += jnp.sum(dx2 * ffn, axis=0, keepdims=True)
        st_ref[2:3, :] += jnp.sum(0.5 * jnp.mean(err * err, axis=-1, keepdims=True), axis=0, keepdims=True)

    blk = pl.BlockSpec((TM, D), lambda i: (i, 0))
    ablk = pl.BlockSpec((TM, D_FF), lambda i: (i, 0))
    return pl.pallas_call(
        body, name="down_loss", grid=(t // TM,),
        out_shape=(jax.ShapeDtypeStruct((t, D), f32), jax.ShapeDtypeStruct((t, D), MX),
                   jax.ShapeDtypeStruct((t, D_FF), f32), jax.ShapeDtypeStruct((8, D), f32)),
        in_specs=[ablk, _resident((D_FF, D)), blk, _row(6 * D), _row(), blk],
        out_specs=(blk, blk, ablk, pl.BlockSpec((8, D), lambda i: (0, 0))),
        compiler_params=_cp("arbitrary"),
    )(act, w_down, x1, mod, final_norm_w, target)


def _pad_of(k):
    return 8 * ((k - 1 + 7) // 8)


def _causal_win(ref, r, t, pad):
    base = pl.multiple_of(r * RC, RC)
    prev = ref[pl.ds(pl.multiple_of(jnp.maximum(base - pad, 0), 8), pad), :]
    prev = jnp.where(r > 0, prev, 0.0)
    return jnp.concatenate([prev, ref[pl.ds(base, RC), :]], axis=0)


def _anti_win(ref, r, t, pad):
    base = pl.multiple_of(r * RC, RC)
    nxt = ref[pl.ds(pl.multiple_of(jnp.minimum(base + RC, t - pad), 8), pad), :]
    nxt = jnp.where(r < t // RC - 1, nxt, 0.0)
    return jnp.concatenate([ref[pl.ds(base, RC), :], nxt], axis=0)


def _shifted(win, offsets):
    for r in range(8):
        mine = [o for o in offsets if o % 8 == r]
        if mine:
            rolled = win if r == 0 else pltpu.roll(win, win.shape[0] - r, 0)
            for o in mine:
                yield o, rolled[o - r:o - r + RC, :]


def _conv_taps(win, w_ref, k, pad):
    first = pad - (k - 1)
    acc = None
    for o, rows in _shifted(win, range(first, first + k)):
        term = w_ref[o - first:o - first + 1, :] * rows
        acc = term if acc is None else acc + term
    return acc


def _corr_taps(win, w_ref, k):
    acc = None
    for o, rows in _shifted(win, range(k)):
        term = w_ref[k - 1 - o:k - o, :] * rows
        acc = term if acc is None else acc + term
    return acc


def _dw_accumulate(dw_scr, d, win, k, pad):
    first = pad - (k - 1)
    for o, rows in _shifted(win, range(first, first + k)):
        j = o - first
        prod = d * rows
        dw_scr[8 * j:8 * j + 8, :] += prod.reshape(RC // 8, 8, prod.shape[-1]).sum(axis=0)


def _dw_finish(dw_scr, dw_ref, k):
    for j in range(k):
        dw_ref[j:j + 1, :] = jnp.sum(dw_scr[8 * j:8 * j + 8, :], axis=0, keepdims=True)


def _rows8(v):
    return v.reshape(RC // 8, 8, v.shape[-1]).sum(axis=0)


def _ssd_conv_fwd(proj, conv_w, conv_b):
    t = proj.shape[0]
    pad = _pad_of(K_SSD)
    c0 = OFF_XBC // CW

    def body(x_ref, w_ref, b_ref, o_ref):
        def step(r, carry):
            win = _causal_win(x_ref, r, t, pad)
            o_ref[pl.ds(pl.multiple_of(r * RC, RC), RC), :] = _silu(_conv_taps(win, w_ref, K_SSD, pad) + b_ref[...])
            return carry
        lax.fori_loop(0, t // RC, step, 0)

    return pl.pallas_call(
        body, name="ssd_conv_fwd", grid=(D_XBC // CW,), out_shape=jax.ShapeDtypeStruct((t, D_XBC), f32),
        in_specs=[pl.BlockSpec((t, CW), lambda j: (0, c0 + j)), pl.BlockSpec((K_SSD, CW), lambda j: (0, j)),
                  pl.BlockSpec((1, CW), lambda j: (0, j))],
        out_specs=pl.BlockSpec((t, CW), lambda j: (0, j)), compiler_params=_cp("arbitrary"),
    )(proj, conv_w, conv_b)


def _glu_conv_fwd(proj, conv_w, conv_b, rider=None):
    t = proj.shape[0]
    pad = _pad_of(K_CONF)
    ca, cg = OFF_CA // CW, OFF_CG // CW

    def body(a_ref, g_ref, w_ref, b_ref, o_ref, v_scr):
        def glu(r, carry):
            rows = pl.ds(pl.multiple_of(r * RC, RC), RC)
            v_scr[rows, :] = a_ref[rows, :] * jax.nn.sigmoid(g_ref[rows, :])
            return carry
        lax.fori_loop(0, t // RC, glu, 0)

        def step(r, carry):
            win = _causal_win(v_scr, r, t, pad)
            o_ref[pl.ds(pl.multiple_of(r * RC, RC), RC), :] = _conv_taps(win, w_ref, K_CONF, pad) + b_ref[...]
            return carry
        lax.fori_loop(0, t // RC, step, 0)

    return _call(
        body, name="glu_conv_fwd", grid=(D // CW,), out_shape=(jax.ShapeDtypeStruct((t, D), f32),),
        in_specs=[pl.BlockSpec((t, CW), lambda j: (0, ca + j)), pl.BlockSpec((t, CW), lambda j: (0, cg + j)),
                  pl.BlockSpec((K_CONF, CW), lambda j: (0, j)), pl.BlockSpec((1, CW), lambda j: (0, j))],
        out_specs=(pl.BlockSpec((t, CW), lambda j: (0, j)),),
        scratch_shapes=[pltpu.VMEM((t, CW), f32)], sem=("arbitrary",), args=(proj, proj, conv_w, conv_b), rider=rider)


def _ffn_conv_fwd(up, conv_w, conv_b, rider=None):
    t = up.shape[0]
    pad = _pad_of(K_FFN)
    nb = D_FF // CW

    def body(g_ref, v_ref, wg_ref, wv_ref, bg_ref, bv_ref, o_ref, ot_ref):
        def step(r, carry):
            gc = _conv_taps(_causal_win(g_ref, r, t, pad), wg_ref, K_FFN, pad) + bg_ref[...]
            vc = _conv_taps(_causal_win(v_ref, r, t, pad), wv_ref, K_FFN, pad) + bv_ref[...]
            o_ref[pl.ds(pl.multiple_of(r * RC, RC), RC), :] = (_silu(gc) * vc).astype(MX)
            return carry
        lax.fori_loop(0, t // RC, step, 0)
        ot_ref[...] = o_ref[...].T

    return _call(
        body, name="ffn_conv_fwd", grid=(nb,),
        out_shape=(jax.ShapeDtypeStruct((t, D_FF), MX), jax.ShapeDtypeStruct((D_FF, t), MX)),
        in_specs=[pl.BlockSpec((t, CW), lambda j: (0, j)), pl.BlockSpec((t, CW), lambda j: (0, nb + j)),
                  pl.BlockSpec((K_FFN, CW), lambda j: (0, j)), pl.BlockSpec((K_FFN, CW), lambda j: (0, nb + j)),
                  pl.BlockSpec((1, CW), lambda j: (0, j)), pl.BlockSpec((1, CW), lambda j: (0, nb + j))],
        out_specs=(pl.BlockSpec((t, CW), lambda j: (0, j)), pl.BlockSpec((CW, t), lambda j: (j, 0))), sem=("arbitrary",),
        args=(up, up, conv_w, conv_w, conv_b, conv_b), rider=rider)


def _ffn_conv_bwd(up, conv_w, conv_b, d_act, rider=None):
    t = up.shape[0]
    pad = _pad_of(K_FFN)
    nb = D_FF // CW

    def body(g_ref, v_ref, wg_ref, wv_ref, bg_ref, bv_ref, da_ref, dup_ref, dw_ref, db_ref,
             dg_scr, dv_scr, dwg_scr, dwv_scr, db_scr):
        dwg_scr[...] = jnp.zeros_like(dwg_scr)
        dwv_scr[...] = jnp.zeros_like(dwv_scr)
        db_scr[...] = jnp.zeros_like(db_scr)

        def first(r, carry):
            rows = pl.ds(pl.multiple_of(r * RC, RC), RC)
            gwin = _causal_win(g_ref, r, t, pad)
            vwin = _causal_win(v_ref, r, t, pad)
            gc = _conv_taps(gwin, wg_ref, K_FFN, pad) + bg_ref[...]
            vc = _conv_taps(vwin, wv_ref, K_FFN, pad) + bv_ref[...]
            da = da_ref[rows, :]
            dgc = da * vc * _dsilu(gc)
            dvc = da * _silu(gc)
            dg_scr[rows, :] = dgc
            dv_scr[rows, :] = dvc
            _dw_accumulate(dwg_scr, dgc, gwin, K_FFN, pad)
            _dw_accumulate(dwv_scr, dvc, vwin, K_FFN, pad)
            db_scr[0:8, :] += _rows8(dgc)
            db_scr[8:16, :] += _rows8(dvc)
            return carry
        lax.fori_loop(0, t // RC, first, 0)

        def second(r, carry):
            rows = pl.ds(pl.multiple_of(r * RC, RC), RC)
            dup_ref[0, rows, :] = _corr_taps(_anti_win(dg_scr, r, t, pad), wg_ref, K_FFN).astype(MX)
            dup_ref[1, rows, :] = _corr_taps(_anti_win(dv_scr, r, t, pad), wv_ref, K_FFN).astype(MX)
            return carry
        lax.fori_loop(0, t // RC, second, 0)

        for j in range(K_FFN):
            dw_ref[0, j:j + 1, :] = jnp.sum(dwg_scr[8 * j:8 * j + 8, :], axis=0, keepdims=True)
            dw_ref[1, j:j + 1, :] = jnp.sum(dwv_scr[8 * j:8 * j + 8, :], axis=0, keepdims=True)
        db_ref[0] = jnp.sum(db_scr[0:8, :], axis=0, keepdims=True)
        db_ref[1] = jnp.sum(db_scr[8:16, :], axis=0, keepdims=True)

    return _call(
        body, name="ffn_conv_bwd", grid=(nb,),
        out_shape=(jax.ShapeDtypeStruct((2, t, D_FF), MX), jax.ShapeDtypeStruct((2, K_FFN, D_FF), f32),
                   jax.ShapeDtypeStruct((2, 1, D_FF), f32)),
        in_specs=[pl.BlockSpec((t, CW), lambda j: (0, j)), pl.BlockSpec((t, CW), lambda j: (0, nb + j)),
                  pl.BlockSpec((K_FFN, CW), lambda j: (0, j)), pl.BlockSpec((K_FFN, CW), lambda j: (0, nb + j)),
                  pl.BlockSpec((1, CW), lambda j: (0, j)), pl.BlockSpec((1, CW), lambda j: (0, nb + j)),
                  pl.BlockSpec((t, CW), lambda j: (0, j))],
        out_specs=(pl.BlockSpec((2, t, CW), lambda j: (0, 0, j)), pl.BlockSpec((2, K_FFN, CW), lambda j: (0, 0, j)),
                   pl.BlockSpec((2, 1, CW), lambda j: (0, 0, j))),
        scratch_shapes=[pltpu.VMEM((t, CW), f32), pltpu.VMEM((t, CW), f32), pltpu.VMEM((8 * K_FFN, CW), f32),
                        pltpu.VMEM((8 * K_FFN, CW), f32), pltpu.VMEM((16, CW), f32)],
        sem=("arbitrary",), args=(up, up, conv_w, conv_w, conv_b, conv_b, d_act), rider=rider)


def _glu_conv_bwd(proj, conv_w, d_uconv, rider=None):
    t = proj.shape[0]
    pad = _pad_of(K_CONF)
    ca, cg = OFF_CA // CW, OFF_CG // CW

    def body(a_ref, g_ref, w_ref, du_ref, dc_ref, dw_ref, db_ref, v_scr, dw_scr, db_scr):
        dw_scr[...] = jnp.zeros_like(dw_scr)
        db_scr[...] = jnp.zeros_like(db_scr)

        def glu(r, carry):
            rows = pl.ds(pl.multiple_of(r * RC, RC), RC)
            v_scr[rows, :] = a_ref[rows, :] * jax.nn.sigmoid(g_ref[rows, :])
            return carry
        lax.fori_loop(0, t // RC, glu, 0)

        def step(r, carry):
            rows = pl.ds(pl.multiple_of(r * RC, RC), RC)
            du = du_ref[rows, :]
            _dw_accumulate(dw_scr, du, _causal_win(v_scr, r, t, pad), K_CONF, pad)
            db_scr[...] += _rows8(du)
            dv = _corr_taps(_anti_win(du_ref, r, t, pad), w_ref, K_CONF)
            a = a_ref[rows, :]
            s = jax.nn.sigmoid(g_ref[rows, :])
            dc_ref[0, rows, :] = (dv * s).astype(MX)
            dc_ref[1, rows, :] = (dv * a * s * (1.0 - s)).astype(MX)
            return carry
        lax.fori_loop(0, t // RC, step, 0)
        _dw_finish(dw_scr, dw_ref, K_CONF)
        db_ref[...] = jnp.sum(db_scr[...], axis=0, keepdims=True)

    return _call(
        body, name="glu_conv_bwd", grid=(D // CW,),
        out_shape=(jax.ShapeDtypeStruct((2, t, D), MX), jax.ShapeDtypeStruct((K_CONF, D), f32),
                   jax.ShapeDtypeStruct((1, D), f32)),
        in_specs=[pl.BlockSpec((t, CW), lambda j: (0, ca + j)), pl.BlockSpec((t, CW), lambda j: (0, cg + j)),
                  pl.BlockSpec((K_CONF, CW), lambda j: (0, j)), pl.BlockSpec((t, CW), lambda j: (0, j))],
        out_specs=(pl.BlockSpec((2, t, CW), lambda j: (0, 0, j)), pl.BlockSpec((K_CONF, CW), lambda j: (0, j)),
                   pl.BlockSpec((1, CW), lambda j: (0, j))),
        scratch_shapes=[pltpu.VMEM((t, CW), f32), pltpu.VMEM((8 * K_CONF, CW), f32), pltpu.VMEM((8, CW), f32)],
        sem=("arbitrary",), args=(proj, proj, conv_w, d_uconv), rider=rider)


def _ssd_conv_bwd_x(proj, conv_w, conv_b, d_xs, d_y, d_skip_row):
    t = proj.shape[0]
    pad = _pad_of(K_SSD)
    c0 = OFF_XBC // CW

    def body(x_ref, w_ref, b_ref, dxs_ref, dy_ref, dsk_ref, draw_ref, dw_ref, db_ref, dp_scr, dw_scr, db_scr):
        dw_scr[...] = jnp.zeros_like(dw_scr)
        db_scr[...] = jnp.zeros_like(db_scr)

        def first(r, carry):
            rows = pl.ds(pl.multiple_of(r * RC, RC), RC)
            win = _causal_win(x_ref, r, t, pad)
            pre = _conv_taps(win, w_ref, K_SSD, pad) + b_ref[...]
            dpre = (dxs_ref[rows, :] + dy_ref[rows, :] * dsk_ref[...]) * _dsilu(pre)
            dp_scr[rows, :] = dpre
            _dw_accumulate(dw_scr, dpre, win, K_SSD, pad)
            db_scr[...] += _rows8(dpre)
            return carry
        lax.fori_loop(0, t // RC, first, 0)

        def second(r, carry):
            rows = pl.ds(pl.multiple_of(r * RC, RC), RC)
            draw_ref[rows, :] = _corr_taps(_anti_win(dp_scr, r, t, pad), w_ref, K_SSD).astype(MX)
            return carry
        lax.fori_loop(0, t // RC, second, 0)
        _dw_finish(dw_scr, dw_ref, K_SSD)
        db_ref[...] = jnp.sum(db_scr[...], axis=0, keepdims=True)

    cb = pl.BlockSpec((t, CW), lambda j: (0, j))
    return pl.pallas_call(
        body, name="ssd_conv_bwd_x", grid=(D // CW,),
        out_shape=(jax.ShapeDtypeStruct((t, D), MX), jax.ShapeDtypeStruct((K_SSD, D), f32),
                   jax.ShapeDtypeStruct((1, D), f32)),
        in_specs=[pl.BlockSpec((t, CW), lambda j: (0, c0 + j)), pl.BlockSpec((K_SSD, CW), lambda j: (0, j)),
                  pl.BlockSpec((1, CW), lambda j: (0, j)), cb, cb, pl.BlockSpec((1, CW), lambda j: (0, j))],
        out_specs=(cb, pl.BlockSpec((K_SSD, CW), lambda j: (0, j)), pl.BlockSpec((1, CW), lambda j: (0, j))),
        scratch_shapes=[pltpu.VMEM((t, CW), f32), pltpu.VMEM((8 * K_SSD, CW), f32), pltpu.VMEM((8, CW), f32)],
        compiler_params=_cp("arbitrary"),
    )(proj, conv_w, conv_b, d_xs, d_y, d_skip_row)


def _ssd_conv_bwd_bc(proj, conv_w, conv_b, d_bc):
    t = proj.shape[0]
    pad = _pad_of(K_SSD)
    c0 = (OFF_XBC + D) // CW
    w0 = D // CW

    def body(x_ref, w_ref, b_ref, dbc_ref, draw_ref, dw_ref, db_ref, dp_scr, dw_scr, db_scr):
        dw_scr[...] = jnp.zeros_like(dw_scr)
        db_scr[...] = jnp.zeros_like(db_scr)

        def first(r, carry):
            rows = pl.ds(pl.multiple_of(r * RC, RC), RC)
            win = _causal_win(x_ref, r, t, pad)
            pre = _conv_taps(win, w_ref, K_SSD, pad) + b_ref[...]
            dpre = dbc_ref[0, rows, :] * _dsilu(pre)
            dp_scr[rows, :] = dpre
            _dw_accumulate(dw_scr, dpre, win, K_SSD, pad)
            db_scr[...] += _rows8(dpre)
            return carry
        lax.fori_loop(0, t // RC, first, 0)

        def second(r, carry):
            rows = pl.ds(pl.multiple_of(r * RC, RC), RC)
            draw_ref[rows, :] = _corr_taps(_anti_win(dp_scr, r, t, pad), w_ref, K_SSD).astype(MX)
            return carry
        lax.fori_loop(0, t // RC, second, 0)
        _dw_finish(dw_scr, dw_ref, K_SSD)
        db_ref[...] = jnp.sum(db_scr[...], axis=0, keepdims=True)

    return pl.pallas_call(
        body, name="ssd_conv_bwd_bc", grid=(2,),
        out_shape=(jax.ShapeDtypeStruct((t, 2 * CW), MX), jax.ShapeDtypeStruct((K_SSD, 2 * CW), f32),
                   jax.ShapeDtypeStruct((1, 2 * CW), f32)),
        in_specs=[pl.BlockSpec((t, CW), lambda j: (0, c0 + j)), pl.BlockSpec((K_SSD, CW), lambda j: (0, w0 + j)),
                  pl.BlockSpec((1, CW), lambda j: (0, w0 + j)), pl.BlockSpec((1, t, CW), lambda j: (j, 0, 0))],
        out_specs=(pl.BlockSpec((t, CW), lambda j: (0, j)), pl.BlockSpec((K_SSD, CW), lambda j: (0, j)),
                   pl.BlockSpec((1, CW), lambda j: (0, j))),
        scratch_shapes=[pltpu.VMEM((t, CW), f32), pltpu.VMEM((8 * K_SSD, CW), f32), pltpu.VMEM((8, CW), f32)],
        compiler_params=_cp("arbitrary"),
    )(proj, conv_w, conv_b, d_bc)


def _chunk_masks():
    ii = lax.broadcasted_iota(jnp.int32, (CHUNK, CHUNK), 0)
    jj = lax.broadcasted_iota(jnp.int32, (CHUNK, CHUNK), 1)
    return ii == jj, jj <= ii, jj >= ii


def _to_row(col, eye):
    return jnp.sum(jnp.where(eye, col, 0.0), axis=0, keepdims=True)


def _to_col(row, eye):
    return jnp.sum(jnp.where(eye, row, 0.0), axis=1, keepdims=True)


def _head_decay(dt_h, a_h, eye, tril):
    a_row = _to_row(dt_h * a_h, eye)
    cs = jnp.sum(jnp.where(tril, a_row, 0.0), axis=1, keepdims=True)
    cs_row = _to_row(cs, eye)
    decay = jnp.where(tril, jnp.exp(jnp.where(tril, cs - cs_row, 0.0)), 0.0)
    total = jnp.sum(a_row, axis=1, keepdims=True)
    return cs, decay, total


SCAN_UNROLL = 4


def _unrolled_loop(n, step, init):
    unroll = min(SCAN_UNROLL, n)
    assert n % unroll == 0

    def trip(i, carry):
        for u in range(unroll):
            carry = step(unroll * i + u, carry)
        return carry
    return lax.fori_loop(0, n // unroll, trip, init)


def _lane_pick(mat, lane, which):
    return jnp.sum(jnp.where(lane == which, mat, 0.0), axis=1, keepdims=True)


def _ssd_fwd(xbc_act, proj, dt_bias_row, a_log_row, rider=None):
    t = xbc_act.shape[0]
    nc = t // CHUNK
    cb, cc, cdt = D // LANES, (D + 2 * STATE_N) // LANES, OFF_DT // LANES

    def body(x_ref, b_ref, c_ref, dt_ref, dtb_ref, alog_ref, y_ref, st_ref):
        j = pl.program_id(0)
        eye, tril, _ = _chunk_masks()
        lane = lax.broadcasted_iota(jnp.int32, (1, LANES), 1)
        first = lane < HEAD_P
        a_row = -jnp.exp(alog_ref[...])
        a_heads = [jnp.sum(jnp.where(lane == 2 * j + h, a_row, 0.0), axis=1, keepdims=True) for h in range(2)]

        def chunk(c, hprev):
            rows = pl.ds(pl.multiple_of(c * CHUNK, CHUNK), CHUNK)
            xv, bm, cm = x_ref[rows, :], b_ref[rows, :], c_ref[rows, :]
            dt = _softplus(dt_ref[rows, :] + dtb_ref[...])
            st_ref[c] = hprev
            g = _mm_nt(cm, bm)
            ch = _mm(cm, hprev)
            dts = [_lane_pick(dt, lane, 2 * j + h) for h in range(2)]
            xdt = xv * jnp.where(first, dts[0], dts[1])
            ys, hs = [], []
            for h in range(2):
                cs, decay, total = _head_decay(dts[h], a_heads[h], eye, tril)
                y = _mm(g * decay, xdt) + jnp.exp(cs) * ch
                s = _mm_tn(bm * jnp.exp(total - cs), xdt)
                ys.append(y)
                hs.append(jnp.exp(total) * hprev + s)
            y_ref[rows, :] = jnp.where(first, ys[0], ys[1])
            return jnp.where(first, hs[0], hs[1])

        _unrolled_loop(nc, chunk, jnp.zeros((STATE_N, LANES), f32))

    blk = lambda f: pl.BlockSpec((t, LANES), f)
    return _call(
        body, name="ssd_fwd", grid=(D // LANES,),
        out_shape=(jax.ShapeDtypeStruct((t, D), f32), jax.ShapeDtypeStruct((nc, STATE_N, D), f32)),
        in_specs=[blk(lambda j: (0, j)), blk(lambda j: (0, cb + j // 4)), blk(lambda j: (0, cc + j // 4)),
                  blk(lambda j: (0, cdt)), _row(LANES), _row(LANES)],
        out_specs=(blk(lambda j: (0, j)), pl.BlockSpec((nc, STATE_N, LANES), lambda j: (0, 0, j))),
        sem=("arbitrary",), args=(xbc_act, xbc_act, xbc_act, proj, dt_bias_row, a_log_row), rider=rider)


def _ssd_bwd(xbc_act, proj, dt_bias_row, a_log_row, states, d_y, rider=None):
    t = xbc_act.shape[0]
    nc = t // CHUNK
    cb, cc, cdt = D // LANES, (D + 2 * STATE_N) // LANES, OFF_DT // LANES

    def body(x_ref, b_ref, c_ref, dt_ref, dtb_ref, alog_ref, st_ref, dy_ref, dx_ref, dbc_ref, ddt_ref, da_ref):
        grp, p = pl.program_id(0), pl.program_id(1)
        j = 4 * grp + p
        eye, tril, triu = _chunk_masks()
        lane = lax.broadcasted_iota(jnp.int32, (1, LANES), 1)
        first = lane < HEAD_P
        last_row = lax.broadcasted_iota(jnp.int32, (CHUNK, 1), 0) == CHUNK - 1
        a_row = -jnp.exp(alog_ref[...])
        a_heads = [jnp.sum(jnp.where(lane == 2 * j + h, a_row, 0.0), axis=1, keepdims=True) for h in range(2)]

        @pl.when(p == 0)
        def _():
            dbc_ref[...] = jnp.zeros_like(dbc_ref)

        @pl.when(j == 0)
        def _():
            ddt_ref[...] = jnp.zeros_like(ddt_ref)
            da_ref[...] = jnp.zeros_like(da_ref)

        def chunk(i, dh):
            c = nc - 1 - i
            rows = pl.ds(pl.multiple_of(c * CHUNK, CHUNK), CHUNK)
            xv, bm, cm = x_ref[rows, :], b_ref[rows, :], c_ref[rows, :]
            dtr = dt_ref[rows, :] + dtb_ref[...]
            dt = _softplus(dtr)
            hprev = st_ref[c]
            dy = dy_ref[rows, :]
            g = _mm_nt(cm, bm)
            dts = [_lane_pick(dt, lane, 2 * j + h) for h in range(2)]
            xdt = xv * jnp.where(first, dts[0], dts[1])
            dxs, dhs = [], []
            db_sum, dc_sum = None, None
            ddt_mat = jnp.zeros((CHUNK, LANES), f32)
            da_acc = jnp.zeros((1, LANES), f32)
            for h in range(2):
                mine = first if h == 0 else jnp.logical_not(first)
                cs, decay, total = _head_decay(dts[h], a_heads[h], eye, tril)
                e_cs, e_tot = jnp.exp(cs), jnp.exp(total)
                dec_s = jnp.exp(total - cs)
                dyh = jnp.where(mine, dy, 0.0)
                xdth = jnp.where(mine, xdt, 0.0)
                dhh = jnp.where(mine, dh, 0.0)
                hph = jnp.where(mine, hprev, 0.0)
                m = g * decay
                dm = _mm_nt(dyh, xdth)
                dg = dm * decay
                w = dm * m
                bdec = bm * dec_s
                dxdt = _mm_tn(m, dyh) + _mm(bdec, dhh)
                dc_off = _mm_nt(dyh, hph) * e_cs
                db_s = _mm_nt(xdth, dhh) * dec_s
                dc_h = _mm(dg, bm) + dc_off
                db_h = _mm_tn(dg, cm) + db_s
                r_s = jnp.sum(db_s * bm, axis=1, keepdims=True)
                dtotal = jnp.sum(r_s, axis=0, keepdims=True) + e_tot * jnp.sum(
                    jnp.sum(dhh * hph, axis=1, keepdims=True), axis=0, keepdims=True)
                dcs = (jnp.sum(w, axis=1, keepdims=True) - _to_col(jnp.sum(w, axis=0, keepdims=True), eye)
                       + jnp.sum(dc_off * cm, axis=1, keepdims=True) - r_s + jnp.where(last_row, dtotal, 0.0))
                da_col = jnp.sum(jnp.where(triu, _to_row(dcs, eye), 0.0), axis=1, keepdims=True)
                ddt = da_col * a_heads[h] + jnp.sum(jnp.where(mine, dxdt * xv, 0.0), axis=1, keepdims=True)
                ddt_mat = ddt_mat + jnp.where(lane == 2 * j + h, ddt, 0.0)
                da_acc = da_acc + jnp.where(lane == 2 * j + h, jnp.sum(da_col * dts[h], axis=0, keepdims=True), 0.0)
                dxs.append(dxdt * dts[h])
                dhs.append(e_tot * dhh + _mm_tn(cm * e_cs, dyh))
                db_sum = db_h if db_sum is None else db_sum + db_h
                dc_sum = dc_h if dc_sum is None else dc_sum + dc_h
            dx_ref[rows, :] = jnp.where(first, dxs[0], dxs[1])
            dbc_ref[0, rows, :] += db_sum
            dbc_ref[1, rows, :] += dc_sum
            ddt_ref[rows, :] += ddt_mat * jax.nn.sigmoid(dtr)
            da_ref[...] += da_acc * a_row
            return jnp.where(first, dhs[0], dhs[1])

        _unrolled_loop(nc, chunk, jnp.zeros((STATE_N, LANES), f32))

    blk = lambda f: pl.BlockSpec((t, LANES), f)
    return _call(
        body, name="ssd_bwd", grid=(2, 4),
        out_shape=(jax.ShapeDtypeStruct((t, D), f32), jax.ShapeDtypeStruct((2, t, 2 * STATE_N), f32),
                   jax.ShapeDtypeStruct((t, LANES), f32), jax.ShapeDtypeStruct((1, LANES), f32)),
        in_specs=[blk(lambda g, p: (0, 4 * g + p)), blk(lambda g, p: (0, cb + g)), blk(lambda g, p: (0, cc + g)),
                  blk(lambda g, p: (0, cdt)), _row(LANES), _row(LANES),
                  pl.BlockSpec((nc, STATE_N, LANES), lambda g, p: (0, 0, 4 * g + p)), blk(lambda g, p: (0, 4 * g + p))],
        out_specs=(blk(lambda g, p: (0, 4 * g + p)), pl.BlockSpec((2, t, LANES), lambda g, p: (0, 0, g)),
                   blk(lambda g, p: (0, 0)), _row(LANES)),
        sem=("arbitrary", "arbitrary"), args=(xbc_act, xbc_act, xbc_act, proj, dt_bias_row, a_log_row, states, d_y),
        rider=rider)


def _up_bwd(d_up, w_up, x1, mod, norm2_w, dx2, mix, w_out, rider=None):
    t = x1.shape[0]

    def body(dup_ref, wu_ref, x1_ref, mod_ref, nw_ref, dx2_ref, mix_ref, wo_ref,
             dx1_ref, dmix_ref, dys_ref, du_ref, st_ref):
        @pl.when(pl.program_id(0) == 0)
        def _():
            st_ref[...] = jnp.zeros_like(st_ref)

        nt = (((1,), (1,)), ((), ()))
        dh = None
        for k in range(4):
            lo = (k % 2) * UP_SHARD
            part = lax.dot_general(dup_ref[k // 2, :, lo:lo + UP_SHARD], wu_ref[k], nt, preferred_element_type=f32)
            dh = part if dh is None else dh + part
        x1 = x1_ref[...]
        rstd = lax.rsqrt(jnp.mean(x1 * x1, axis=-1, keepdims=True) + 1e-6)
        xh = x1 * rstd
        nw = nw_ref[...]
        sc = 1.0 + mod_ref[:, 4 * D:5 * D]
        st_ref[0:1, :] += jnp.sum(dh, axis=0, keepdims=True)
        st_ref[1:2, :] += jnp.sum(dh * xh * nw, axis=0, keepdims=True)
        st_ref[2:3, :] += jnp.sum(dh * sc * xh, axis=0, keepdims=True)
        dxh = dh * sc * nw
        dx1 = dx2_ref[...] + rstd * (dxh - xh * jnp.mean(dxh * xh, axis=-1, keepdims=True))
        dx1_ref[...] = dx1
        st_ref[3:4, :] += jnp.sum(dx1 * mix_ref[...], axis=0, keepdims=True)
        dmix = (mod_ref[:, 2 * D:3 * D] * dx1).astype(MX)
        dmix_ref[...] = dmix
        dys_ref[...] = lax.dot_general(dmix, wo_ref[0:D, :], nt, preferred_element_type=f32)
        du_ref[...] = lax.dot_general(dmix, wo_ref[D:2 * D, :], nt, preferred_element_type=f32)

    blk = pl.BlockSpec((TM, D), lambda i: (i, 0))
    return _call(
        body, name="up_bwd", grid=(t // TM,),
        out_shape=(jax.ShapeDtypeStruct((t, D), f32), jax.ShapeDtypeStruct((t, D), MX),
                   jax.ShapeDtypeStruct((t, D), f32), jax.ShapeDtypeStruct((t, D), f32),
                   jax.ShapeDtypeStruct((8, D), f32)),
        in_specs=[pl.BlockSpec((2, TM, D_FF), lambda i: (0, i, 0)), _resident((4, D, UP_SHARD)), blk, _row(6 * D), _row(),
                  blk, blk, _resident((2 * D, D))],
        out_specs=(blk, blk, blk, blk, pl.BlockSpec((8, D), lambda i: (0, 0))),
        sem=("arbitrary",), args=(d_up, w_up, x1, mod, norm2_w, dx2, mix, w_out), rider=rider)


def _ln_silu_bwd(d_u, u_conv, ln_w, ln_b):
    t = d_u.shape[0]

    def body(du_ref, u_ref, w_ref, b_ref, o_ref, st_ref):
        @pl.when(pl.program_id(0) == 0)
        def _():
            st_ref[...] = jnp.zeros_like(st_ref)

        u = u_ref[...]
        mu = jnp.mean(u, axis=-1, keepdims=True)
        uc = u - mu
        rstd = lax.rsqrt(jnp.mean(uc * uc, axis=-1, keepdims=True) + 1e-5)
        n = uc * rstd
        w = w_ref[...]
        dl = du_ref[...] * _dsilu(n * w + b_ref[...])
        st_ref[0:1, :] += jnp.sum(dl * n, axis=0, keepdims=True)
        st_ref[1:2, :] += jnp.sum(dl, axis=0, keepdims=True)
        dn = dl * w
        o_ref[...] = rstd * (dn - jnp.mean(dn, axis=-1, keepdims=True) - n * jnp.mean(dn * n, axis=-1, keepdims=True))

    blk = pl.BlockSpec((TM, D), lambda i: (i, 0))
    return pl.pallas_call(
        body, name="ln_silu_bwd", grid=(t // TM,),
        out_shape=(jax.ShapeDtypeStruct((t, D), f32), jax.ShapeDtypeStruct((8, D), f32)),
        in_specs=[blk, blk, _row(), _row()], out_specs=(blk, pl.BlockSpec((8, D), lambda i: (0, 0))),
        compiler_params=_cp("arbitrary"),
    )(d_u, u_conv, ln_w, ln_b)


def _ssd_gate_norm_bwd(d_out, y_scan, xbc_act, proj, d_skip_row, ssd_norm_w):
    t = d_out.shape[0]

    def body(do_ref, y_ref, xs_ref, z_ref, dsk_ref, nw_ref, dy_ref, dz_ref, st_ref):
        @pl.when(pl.program_id(0) == 0)
        def _():
            st_ref[...] = jnp.zeros_like(st_ref)

        xs = xs_ref[...]
        y = y_ref[...] + xs * dsk_ref[...]
        z = z_ref[...]
        s = _silu(z)
        yz = y * s
        rstd = lax.rsqrt(jnp.mean(yz * yz, axis=-1, keepdims=True) + 1e-6)
        n = yz * rstd
        do = do_ref[...]
        st_ref[0:1, :] += jnp.sum(do * n, axis=0, keepdims=True)
        dn = do * nw_ref[...]
        dyz = rstd * (dn - n * jnp.mean(dn * n, axis=-1, keepdims=True))
        dy = dyz * s
        dy_ref[...] = dy
        dz_ref[...] = (dyz * y * _dsilu(z)).astype(MX)
        st_ref[1:2, :] += jnp.sum(dy * xs, axis=0, keepdims=True)

    blk = pl.BlockSpec((TM, D), lambda i: (i, 0))
    return pl.pallas_call(
        body, name="ssd_gate_norm_bwd", grid=(t // TM,),
        out_shape=(jax.ShapeDtypeStruct((t, D), f32), jax.ShapeDtypeStruct((t, D), MX), jax.ShapeDtypeStruct((8, D), f32)),
        in_specs=[blk, blk, blk, blk, _row(), _row()], out_specs=(blk, blk, pl.BlockSpec((8, D), lambda i: (0, 0))),
        compiler_params=_cp("arbitrary"),
    )(d_out, y_scan, xbc_act, proj, d_skip_row, ssd_norm_w)


def _inproj_bwd(d_z, d_xraw, d_bcraw, d_conf, d_dt, w_pack, x, mod, norm1_w, dx1, rider=None):
    t = x.shape[0]

    def body(dz_ref, dx_ref, dbc_ref, dcf_ref, ddt_ref, w_ref, x_ref, mod_ref, nw_ref, dx1_ref, gx_ref, st_ref):
        @pl.when(pl.program_id(0) == 0)
        def _():
            st_ref[...] = jnp.zeros_like(st_ref)

        nt = (((1,), (1,)), ((), ()))
        dot = lambda a, lo, hi: lax.dot_general(a, w_ref[:, lo:hi], nt, preferred_element_type=f32)
        dh = dot(dz_ref[...], OFF_Z, OFF_Z + D)
        dh = dh + dot(dx_ref[...], OFF_XBC, OFF_XBC + D)
        dh = dh + dot(dbc_ref[...], OFF_XBC + D, OFF_XBC + D_XBC)
        dh = dh + dot(dcf_ref[0], OFF_CA, OFF_CA + D)
        dh = dh + dot(dcf_ref[1], OFF_CG, OFF_CG + D)
        dh = dh + dot(ddt_ref[...].astype(MX), OFF_DT, OFF_DT + LANES)
        st_ref[3:4, 0:LANES] += jnp.sum(ddt_ref[...], axis=0, keepdims=True)
        xv = x_ref[...]
        rstd = lax.rsqrt(jnp.mean(xv * xv, axis=-1, keepdims=True) + 1e-6)
        xh = xv * rstd
        nw = nw_ref[...]
        sc = 1.0 + mod_ref[:, D:2 * D]
        st_ref[0:1, :] += jnp.sum(dh, axis=0, keepdims=True)
        st_ref[1:2, :] += jnp.sum(dh * xh * nw, axis=0, keepdims=True)
        st_ref[2:3, :] += jnp.sum(dh * sc * xh, axis=0, keepdims=True)
        dxh = dh * sc * nw
        gx_ref[...] = dx1_ref[...] + rstd * (dxh - xh * jnp.mean(dxh * xh, axis=-1, keepdims=True))

    blk = pl.BlockSpec((TM, D), lambda i: (i, 0))
    return _call(
        body, name="inproj_bwd", grid=(t // TM,),
        out_shape=(jax.ShapeDtypeStruct((t, D), f32), jax.ShapeDtypeStruct((8, D), f32)),
        in_specs=[blk, blk, pl.BlockSpec((TM, 2 * CW), lambda i: (i, 0)), pl.BlockSpec((2, TM, D), lambda i: (0, i, 0)),
                  pl.BlockSpec((TM, LANES), lambda i: (i, 0)), _resident((D, W_PACK)), blk, _row(6 * D), _row(), blk],
        out_specs=(blk, pl.BlockSpec((8, D), lambda i: (0, 0))),
        sem=("arbitrary",), args=(d_z, d_xraw, d_bcraw, d_conf, d_dt, w_pack, x, mod, norm1_w, dx1), rider=rider)


def _wgrad(at, d, name, bn=256):
    k, t = at.shape
    n = d.shape[1]
    out_dtype = MX

    def body(a_ref, d_ref, o_ref):
        o_ref[...] = jnp.dot(a_ref[...], d_ref[...].astype(MX), preferred_element_type=f32).astype(out_dtype)

    return pl.pallas_call(
        body, name=name, grid=(n // bn,), out_shape=jax.ShapeDtypeStruct((k, n), out_dtype),
        in_specs=[_resident((k, t)), pl.BlockSpec((t, bn), lambda j: (0, j))],
        out_specs=pl.BlockSpec((k, bn), lambda j: (0, j)), compiler_params=_cp("arbitrary"),
    )(at, d)


def _wgrad_stacked(at, d, name, bn):
    out_dtype = MX
    k, t = at.shape
    s, _, n = d.shape
    nb = n // bn

    def body(a_ref, d_ref, o_ref):
        o_ref[0] = jnp.dot(a_ref[...], d_ref[0], preferred_element_type=f32).astype(out_dtype)

    return pl.pallas_call(
        body, name=name, grid=(s, nb), out_shape=jax.ShapeDtypeStruct((s * nb, k, bn), out_dtype),
        in_specs=[_resident((k, t)), pl.BlockSpec((1, t, bn), lambda i, j: (i, 0, j))],
        out_specs=pl.BlockSpec((1, k, bn), lambda i, j: (i * nb + j, 0, 0)), compiler_params=_cp("arbitrary", "arbitrary"),
    )(at, d)


def _pad_row(v, width=LANES):
    return jnp.pad(v.reshape(1, -1), ((0, 0), (0, width - v.size)))


def _quarters(a):
    return a.reshape(4, 2, a.shape[0] // 8, a.shape[1])


def _local_step(x, mod, target, w_pack, late, small, reducer=None):
    dtb_row, alog_row = _pad_row(small["dt_bias"]), _pad_row(small["a_log"])
    dskip_row = jnp.repeat(small["d_skip"].reshape(-1), HEAD_P).reshape(1, D)

    red = reducer

    def hosted(host, args, swap=None, scatter=None, gather=None, sums=()):
        if red is None:
            return host(*args)[0]
        riders = ([red.scatter(scatter)] if scatter else []) + ([red.swap(*swap)] if swap else [])
        riders += [_SwapSumsRider([red.sums[n] for n in sums])] if sums else []
        riders += [_GatherRider([gather])] if gather is not None else []
        both = _Riders(riders)
        outs, extra = host(*args, rider=both)
        extra = both.split(extra)
        if scatter:
            red.scattered(scatter, extra.pop(0))
        if swap:
            red.swapped(swap[0], extra.pop(0))
        if sums:
            red.others.update(zip(sums, extra.pop(0)))
        return (outs, extra[0][0]) if gather is not None else outs

    proj, h_t = _ln_inproj(x, mod, small["norm1_w"], w_pack)
    xbc_act = _ssd_conv_fwd(proj, small["ssd_conv_w"], small["ssd_conv_b"])
    w_out, w_up, w_down = late
    if red is None:
        y_scan, states = hosted(_ssd_fwd, (xbc_act, proj, dtb_row, alog_row))
        u_conv, = hosted(_glu_conv_fwd, (proj, small["conf_conv_w"], small["conf_conv_b"]))
    else:
        (y_scan, states), w_up = hosted(_ssd_fwd, (xbc_act, proj, dtb_row, alog_row), gather=w_up)
        (u_conv,), w_out = hosted(_glu_conv_fwd, (proj, small["conf_conv_w"], small["conf_conv_b"]), gather=w_out)
        w_up, w_out = w_up.reshape(4, D, UP_SHARD), w_out.reshape(2 * D, D)
    y_ssd, y_ssd_t = _ssd_gate_norm(y_scan, xbc_act, proj, dskip_row, small["ssd_norm_w"])
    u, u_t = _ln_silu(u_conv, small["conf_ln_w"], small["conf_ln_b"])
    mix, x1, h2_t, up = _outproj_ln2_up(y_ssd, u, w_out, x, mod, small["norm2_w"], w_up)
    if red is None:
        act, act_t = hosted(_ffn_conv_fwd, (up, small["ffn_conv_w"], small["ffn_conv_b"]))
    else:
        (act, act_t), w_down = hosted(_ffn_conv_fwd, (up, small["ffn_conv_w"], small["ffn_conv_b"]), gather=w_down)
        w_down = w_down.reshape(D_FF, D)
    dx2, d_ffn, d_act, st_down = _down_loss(act, w_down, x1, mod, small["final_norm_w"], target)

    g_down = _quarters(_wgrad(act_t, d_ffn, "wgrad_down"))
    d_up, dw_ffn, db_ffn = hosted(_ffn_conv_bwd, (up, small["ffn_conv_w"], small["ffn_conv_b"], d_act), swap=("w_down", g_down))
    g_up = _wgrad_stacked(h2_t, d_up, "wgrad_up", D_FF // 2).reshape(4, 2, D // 2, UP_SHARD)
    dx1, d_mix, d_yssd, d_u, st_up = hosted(_up_bwd, (d_up, w_up, x1, mod, small["norm2_w"], dx2, mix, w_out),
                                            scatter="w_down", swap=("w_up", g_up))
    g_out = _quarters(jnp.concatenate([_wgrad(y_ssd_t, d_mix, "wgrad_out_y"), _wgrad(u_t, d_mix, "wgrad_out_u")], axis=0))
    d_uconv, st_ln = _ln_silu_bwd(d_u, u_conv, small["conf_ln_w"], small["conf_ln_b"])
    d_conf, dw_conf, db_conf = hosted(_glu_conv_bwd, (proj, small["conf_conv_w"], d_uconv), scatter="w_up",
                                      swap=("w_out", g_out))
    d_y, d_z, st_gn = _ssd_gate_norm_bwd(d_yssd, y_scan, xbc_act, proj, dskip_row, small["ssd_norm_w"])
    d_xs, d_bc, d_dt, d_alog = hosted(_ssd_bwd, (xbc_act, proj, dtb_row, alog_row, states, d_y), scatter="w_out")
    d_xraw, dw_sx, db_sx = _ssd_conv_bwd_x(proj, small["ssd_conv_w"], small["ssd_conv_b"], d_xs, d_y, dskip_row)
    d_bcraw, dw_sbc, db_sbc = _ssd_conv_bwd_bc(proj, small["ssd_conv_w"], small["ssd_conv_b"], d_bc)
    g_in = _unpack_g_in(dict(
        z=_wgrad(h_t, d_z, "wgrad_in_z"), x=_wgrad(h_t, d_xraw, "wgrad_in_x"), bc=_wgrad(h_t, d_bcraw, "wgrad_in_bc"),
        conf=_wgrad_stacked(h_t, d_conf, "wgrad_in_conf", D), dt=_wgrad(h_t, d_dt, "wgrad_in_dt", bn=LANES)))
    g_in = g_in.reshape(4, 2, D // 2, W_IN_SHARD_PAD)
    if red is not None:
        red.swapped("w_in", _ride_alone(red.swap("w_in", g_in), "swap_w_in"))
    grad_x, st_in = hosted(_inproj_bwd, (d_z, d_xraw, d_bcraw, d_conf, d_dt, w_pack, x, mod, small["norm1_w"], dx1),
                           scatter="w_in", sums=("w_out", "w_up", "w_down"))

    gsmall = _pack_small_grads(st_in, st_up, st_down, st_ln, st_gn, d_alog, dw_sx, dw_sbc, db_sx, db_sbc, dw_conf, db_conf,
                               dw_ffn, db_ffn)
    gbig = None if reducer is not None else dict(w_in=g_in, w_out=g_out, w_up=g_up, w_down=g_down)
    return st_down[2, 0], grad_x, gbig, gsmall


VECTORS = ("ada_b", "norm1_w", "ssd_conv_b", "dt_bias", "a_log", "d_skip", "ssd_norm_w", "conf_conv_b", "conf_ln_w",
           "conf_ln_b", "norm2_w", "ffn_conv_b", "final_norm_w")
VECTOR_SIZES = (6 * D, D, D_XBC, HEADS, HEADS, HEADS, D, D, D, D, D, 2 * D_FF, D)
CONVS = {"ssd_conv_w": (K_SSD, D_XBC), "conf_conv_w": (K_CONF, D), "ffn_conv_w": (K_FFN, 2 * D_FF)}


def _pack_rows(items):
    n = -(-sum(w for _, w in items) // (8 * LANES)) * LANES
    while True:
        fill, place = [0] * 8, {}
        for key, w in sorted(items, key=lambda kv: -kv[1]):
            rows = [r for r in range(8) if fill[r] + w <= n]
            if not rows:
                break
            place[key] = (rows[0], fill[rows[0]])
            fill[rows[0]] += w
        if len(place) == len(items):
            return n, place
        n += LANES


FRONT_N, FRONT = _pack_rows([("c", D)] + [((nm, j), cols // 4) for nm, (taps, cols) in CONVS.items() for j in range(taps)])
BACK_N, BACK = _pack_rows([(nm, -(-sz // LANES) * LANES) for nm, sz in zip(VECTORS, VECTOR_SIZES)]
                          + [((nm, j), cols) for nm, (taps, cols) in CONVS.items() for j in range(taps)])
_VM = pltpu.CompilerParams(vmem_limit_bytes=VMEM_LIMIT)


def _pack_front(c, shards):
    def body(c_ref, *refs):
        o_ref = refs[-1]
        o_ref[...] = jnp.zeros_like(o_ref)
        r, o = FRONT["c"]
        o_ref[r:r + 1, o:o + D] = c_ref[...]
        for ref, (nm, (taps, cols)) in zip(refs, CONVS.items()):
            for j in range(taps):
                r, o = FRONT[(nm, j)]
                o_ref[r:r + 1, o:o + cols // 4] = ref[0, j:j + 1, :]

    return pl.pallas_call(body, name="pack_front", out_shape=jax.ShapeDtypeStruct((8, FRONT_N), f32),
                          compiler_params=_VM)(c, *shards)


def _unpack_front(got):
    def body(g_ref, c_ref, *outs):
        r, o = FRONT["c"]
        for d in range(8):
            c_ref[d:d + 1, :] = g_ref[8 * d + r:8 * d + r + 1, o:o + D]
        for ref, (nm, (taps, cols)) in zip(outs, CONVS.items()):
            cw = cols // 4
            for j in range(taps):
                r, o = FRONT[(nm, j)]
                for k in range(4):
                    ref[j:j + 1, k * cw:(k + 1) * cw] = g_ref[16 * k + r:16 * k + r + 1, o:o + cw]

    return pl.pallas_call(
        body, name="unpack_front", compiler_params=_VM,
        out_shape=(jax.ShapeDtypeStruct((8, D), f32),) + tuple(jax.ShapeDtypeStruct(tc, f32) for tc in CONVS.values()),
    )(got)


def _pack_small_grads(st_in, st_up, st_down, st_ln, st_gn, d_alog, dw_sx, dw_sbc, db_sx, db_sbc, dw_conf, db_conf, dw_ffn,
                      db_ffn):
    def body(in_ref, up_ref, dn_ref, ln_ref, gn_ref, al_ref, wx_ref, wbc_ref, bx_ref, bbc_ref, wc_ref, bc_ref, wf_ref, bf_ref,
             o_ref):
        def put(key, val, shift=0):
            r, o = BACK[key]
            o_ref[r:r + 1, o + shift:o + shift + val.shape[1]] = val

        o_ref[...] = jnp.zeros_like(o_ref)
        for i, piece in enumerate((in_ref[0:1, :], in_ref[1:2, :], up_ref[3:4, :], up_ref[0:1, :], up_ref[1:2, :],
                                   dn_ref[1:2, :])):
            put("ada_b", piece, i * D)
        put("norm1_w", in_ref[2:3, :])
        put("ssd_conv_b", bx_ref[...])
        put("ssd_conv_b", bbc_ref[...], D)
        put("dt_bias", in_ref[3:4, 0:LANES])
        put("a_log", al_ref[...])
        lane = lax.broadcasted_iota(jnp.int32, (1, LANES), 1)
        col = lax.broadcasted_iota(jnp.int32, (1, D), 1)
        per_col = gn_ref[1:2, :]
        d_skip = jnp.zeros((1, LANES), f32)
        for h in range(HEADS):
            in_head = jnp.logical_and(col >= h * HEAD_P, col < (h + 1) * HEAD_P)
            s = jnp.sum(jnp.where(in_head, per_col, 0.0), axis=1, keepdims=True)
            d_skip = d_skip + jnp.where(lane == h, s, 0.0)
        put("d_skip", d_skip)
        put("ssd_norm_w", gn_ref[0:1, :])
        put("conf_conv_b", bc_ref[...])
        put("conf_ln_w", ln_ref[0:1, :])
        put("conf_ln_b", ln_ref[1:2, :])
        put("norm2_w", up_ref[2:3, :])
        put("ffn_conv_b", bf_ref[0])
        put("ffn_conv_b", bf_ref[1], D_FF)
        put("final_norm_w", dn_ref[0:1, :])
        for j in range(K_SSD):
            put(("ssd_conv_w", j), wx_ref[j:j + 1, :])
            put(("ssd_conv_w", j), wbc_ref[j:j + 1, :], D)
        for j in range(K_CONF):
            put(("conf_conv_w", j), wc_ref[j:j + 1, :])
        for j in range(K_FFN):
            put(("ffn_conv_w", j), wf_ref[0, j:j + 1, :])
            put(("ffn_conv_w", j), wf_ref[1, j:j + 1, :], D_FF)

    return pl.pallas_call(body, name="pack_small_grads", out_shape=jax.ShapeDtypeStruct((8, BACK_N), f32), compiler_params=_VM)(
        st_in, st_up, st_down, st_ln, st_gn, d_alog, dw_sx, dw_sbc, db_sx, db_sbc, dw_conf, db_conf, dw_ffn, db_ffn)


def _small_adamw(got, chip, w, m, v):
    names = VECTORS + tuple(CONVS)
    n_par = len(names)

    def body(chip_ref, g_ref, *refs):
        ins, outs = refs[:3 * n_par], refs[3 * n_par:]
        dm_ref, outs = outs[0], outs[1:]
        chip_id = chip_ref[0]

        def summed(key, width):
            r, o = BACK[key]
            s = g_ref[r:r + 1, o:o + width]
            for d in range(1, 8):
                s = s + g_ref[8 * d + r:8 * d + r + 1, o:o + width]
            return s

        def mine(full, cw):
            out = full[:, 0:cw]
            for k in range(1, 4):
                out = jnp.where(chip_id == k, full[:, k * cw:(k + 1) * cw], out)
            return out

        r, o = BACK["ada_b"]
        for d in range(8):
            dm_ref[d:d + 1, :] = mine(g_ref[8 * d + r:8 * d + r + 1, o:o + 6 * D], 6 * D // 4)
        for i, (nm, size) in enumerate(zip(VECTORS, VECTOR_SIZES)):
            g = summed(nm, -(-size // LANES) * LANES)[:, 0:size]
            res = _adam_math(ins[3 * i][...], g, ins[3 * i + 1][...], ins[3 * i + 2][...])
            for ref, val in zip(outs[4 * i:4 * i + 4], (g,) + res):
                ref[...] = val
        for i, (nm, (taps, cols)) in enumerate(CONVS.items(), start=len(VECTORS)):
            for j in range(taps):
                g = mine(summed((nm, j), cols), cols // 4)
                res = _adam_math(ins[3 * i][0, j:j + 1, :], g, ins[3 * i + 1][0, j:j + 1, :], ins[3 * i + 2][0, j:j + 1, :])
                for ref, val in zip(outs[4 * i:4 * i + 4], (g,) + res):
                    ref[0, j:j + 1, :] = val

    params = [a[nm] for nm in names for a in (w, m, v)]
    whole = lambda s: pl.BlockSpec(s, lambda i, chip, nd=len(s): (0,) * nd)
    out_shape = [jax.ShapeDtypeStruct((8, 6 * D // 4), f32)] + [jax.ShapeDtypeStruct(w[nm].shape, f32) for nm in names for _ in range(4)]
    outs = pl.pallas_call(
        body, name="small_adamw", out_shape=tuple(out_shape), compiler_params=_VM,
        grid_spec=pltpu.PrefetchScalarGridSpec(
            num_scalar_prefetch=1, grid=(1,), in_specs=[whole(got.shape)] + [whole(p.shape) for p in params],
            out_specs=tuple(whole(s.shape) for s in out_shape)),
    )(_scalar(chip), got, *params)
    return outs[0], {nm: outs[1 + 4 * i:5 + 4 * i] for i, nm in enumerate(names)}


W_IN_COLS = 4624
W_IN_SHARD = W_IN_COLS // 4
W_IN_SHARD_PAD = 1280
_SEGMENTS = ((0, 1024, OFF_Z), (1024, 2560, OFF_XBC), (2560, 2576, OFF_DT), (2576, 3600, OFF_CA), (3600, 4624, OFF_CG))


def _in_pieces(bounds=()):
    out = []
    for k in range(4):
        s0, s1 = k * W_IN_SHARD, (k + 1) * W_IN_SHARD
        for lo, hi, off in _SEGMENTS:
            a, b = max(lo, s0), min(hi, s1)
            while a < b:
                p = off + a - lo
                e = min([b - a] + [c - p for c in bounds if c > p])
                out.append((k, a - s0, p, e))
                a += e
    return out


def _pack_w_in(shards):
    pieces = _in_pieces()

    def body(s_ref, o_ref):
        o_ref[:, OFF_DT:W_PACK] = jnp.zeros((TM, W_PACK - OFF_DT), MX)
        for k, c, p, n in pieces:
            o_ref[:, p:p + n] = s_ref[k, :, c:c + n]

    return pl.pallas_call(
        body, name="pack_w_in", grid=(D // TM,), out_shape=jax.ShapeDtypeStruct((D, W_PACK), MX),
        in_specs=[pl.BlockSpec((4, TM, W_IN_SHARD_PAD), lambda i: (0, i, 0))],
        out_specs=pl.BlockSpec((TM, W_PACK), lambda i: (i, 0)), compiler_params=_cp("arbitrary"),
    )(shards)


def _unpack_g_in(g):
    srcs = ((OFF_Z, D), (OFF_XBC, D), (OFF_XBC + D, 2 * CW), (OFF_CA, D), (OFF_CG, D), (OFF_DT, LANES))
    pieces = _in_pieces(tuple(o for o, _ in srcs) + tuple(o + n for o, n in srcs))

    def body(z_ref, x_ref, bc_ref, cf_ref, dt_ref, o_ref):
        read = (lambda lo, hi: z_ref[:, lo:hi], lambda lo, hi: x_ref[:, lo:hi], lambda lo, hi: bc_ref[:, lo:hi],
                lambda lo, hi: cf_ref[0, :, lo:hi], lambda lo, hi: cf_ref[1, :, lo:hi], lambda lo, hi: dt_ref[:, lo:hi])
        o_ref[:, :, W_IN_SHARD - 4:W_IN_SHARD_PAD] = jnp.zeros((4, TM, W_IN_SHARD_PAD - W_IN_SHARD + 4), MX)
        for k, c, p, n in pieces:
            i = [q for q, (o, w) in enumerate(srcs) if o <= p < o + w][0]
            o_ref[k, :, c:c + n] = read[i](p - srcs[i][0], p - srcs[i][0] + n)

    blk = lambda w: pl.BlockSpec((TM, w), lambda i: (i, 0))
    return pl.pallas_call(
        body, name="unpack_g_in", grid=(D // TM,), out_shape=jax.ShapeDtypeStruct((4, D, W_IN_SHARD_PAD), MX),
        in_specs=[blk(D), blk(D), blk(2 * CW), pl.BlockSpec((2, TM, D), lambda i: (0, i, 0)), blk(LANES)],
        out_specs=pl.BlockSpec((4, TM, W_IN_SHARD_PAD), lambda i: (0, i, 0)), compiler_params=_cp("arbitrary"),
    )(g["z"], g["x"], g["bc"], g["conf"], g["dt"])


def _scalar(v):
    return jnp.reshape(v, (1,)).astype(jnp.int32)


def _cast_into_slot(w, width, chip):
    r, c = w.shape
    h = r // 2
    tm = _row_tile(h)
    nj = h // tm

    def body(chip_ref, w_ref, o_ref):
        v = w_ref[...].astype(MX)
        o_ref[0, 0] = v if width == c else jnp.concatenate([v, jnp.zeros((tm, width - c), MX)], axis=1)

    return pl.pallas_call(
        body, name=f"cast_into_slot_{r}x{c}", out_shape=jax.ShapeDtypeStruct((4, 2, h, width), MX),
        grid_spec=pltpu.PrefetchScalarGridSpec(
            num_scalar_prefetch=1, grid=(2, nj),
            in_specs=[pl.BlockSpec((tm, c), lambda i, j, chip: (i * nj + j, 0))],
            out_specs=pl.BlockSpec((1, 1, tm, width), lambda i, j, chip: (chip[0], i, j, 0))),
        compiler_params=_cp("arbitrary", "arbitrary"),
    )(_scalar(chip), w)


def _columns_first(w):
    return jnp.transpose(w, (2, 0, 1))


def _cast_into_slot_w_in(w_t, chip):
    h = D // 2
    nj = h // TM
    pad = W_IN_SHARD_PAD - W_IN_SHARD

    def body(chip_ref, w_ref, o_ref):
        cols = jnp.concatenate([w_ref[:, 0, :], jnp.zeros((pad, TM), f32)], axis=0)
        o_ref[0, 0] = cols.T.astype(MX)

    return pl.pallas_call(
        body, name="cast_into_slot_w_in", out_shape=jax.ShapeDtypeStruct((4, 2, h, W_IN_SHARD_PAD), MX),
        grid_spec=pltpu.PrefetchScalarGridSpec(
            num_scalar_prefetch=1, grid=(2, nj),
            in_specs=[pl.BlockSpec((W_IN_SHARD, 1, TM), lambda i, j, chip: (0, 0, i * nj + j))],
            out_specs=pl.BlockSpec((1, 1, TM, W_IN_SHARD_PAD), lambda i, j, chip: (chip[0], i, j, 0))),
        compiler_params=_cp("arbitrary", "arbitrary"),
    )(_scalar(chip), w_t)


def _adamw_w_in(w_t, mine, other, m_t, v_t, core):
    h = D // 2
    nj = h // TM

    def body(core_ref, w_ref, a_ref, b_ref, m_ref, v_ref, g_ref, d_ref, nm_ref, nv_ref):
        g = jnp.where(pl.program_id(0) == core_ref[0], a_ref[...], b_ref[...]).T[0:W_IN_SHARD, :]
        g_ref[:, 0, :] = g
        d_ref[:, 0, :], nm_ref[:, 0, :], nv_ref[:, 0, :] = _adam_math(w_ref[:, 0, :], g, m_ref[:, 0, :], v_ref[:, 0, :])

    blk = pl.BlockSpec((W_IN_SHARD, 1, TM), lambda i, j, core: (0, 0, i * nj + j))
    gblk = pl.BlockSpec((TM, W_IN_SHARD_PAD), lambda i, j, core: (j, 0))
    return pl.pallas_call(
        body, name="adamw_w_in", out_shape=tuple([jax.ShapeDtypeStruct((W_IN_SHARD, 1, D), f32)] * 4),
        grid_spec=pltpu.PrefetchScalarGridSpec(
            num_scalar_prefetch=1, grid=(2, nj), in_specs=[blk, gblk, gblk, blk, blk], out_specs=(blk,) * 4),
        compiler_params=_cp("arbitrary", "arbitrary"),
    )(_scalar(core), w_t, mine, other, m_t, v_t)


ANY = pl.BlockSpec(memory_space=pl.ANY)


def _place():
    x, y, c = lax.axis_index("x"), lax.axis_index("y"), lax.axis_index("c")
    return x, y, c, [(1 - x, y), (x, 1 - y), (1 - x, 1 - y)]


def _gather_rows(block):
    m_per, n = block.shape

    def body(x_ref, out_ref, send_sems, recv_sems, local_sem):
        x, y, c, chips = _place()
        me, sibling = (x, y, c), (x, y, 1 - c)

        def rows(px, py, pc):
            return out_ref.at[pl.ds((4 * px + 2 * py + pc) * m_per, m_per), :]

        def copy(k, blk, to, src=None):
            return pltpu.make_async_remote_copy(
                src_ref=rows(*blk) if src is None else src, dst_ref=rows(*blk), send_sem=send_sems.at[k],
                recv_sem=recv_sems.at[k], device_id=to, device_id_type=MESH)

        mine = pltpu.make_async_copy(x_ref, rows(*me), local_sem)
        mine.start()
        first = [copy(0, me, sibling, src=x_ref)]
        first += [copy(1 + j, me, (*chip, c), src=x_ref) for j, chip in enumerate(chips)]
        for cp in first:
            cp.start()
        passed = [copy(4 + j, (*chip, c), sibling) for j, chip in enumerate(chips)]
        for j, chip in enumerate(chips):
            copy(1 + j, (*chip, c), me).wait_recv()
            passed[j].start()
        copy(0, sibling, me).wait_recv()
        for j, chip in enumerate(chips):
            copy(4 + j, (*chip, 1 - c), me).wait_recv()
        for cp in first + passed:
            cp.wait_send()
        mine.wait()

    return pl.pallas_call(
        body, name=f"gather_rows_{m_per}x{n}", out_shape=jax.ShapeDtypeStruct((8 * m_per, n), block.dtype),
        in_specs=[pl.BlockSpec(memory_space=pltpu.VMEM)], out_specs=pl.BlockSpec(memory_space=pltpu.VMEM),
        scratch_shapes=[pltpu.SemaphoreType.DMA((7,)), pltpu.SemaphoreType.DMA((7,)), pltpu.SemaphoreType.DMA],
        compiler_params=pltpu.CompilerParams(vmem_limit_bytes=VMEM_LIMIT),
    )(block)


class _GatherRider:
    def __init__(self, slots):
        n = len(slots)
        self.n = n
        self.inputs = list(slots)
        self.out_shape = [jax.ShapeDtypeStruct(s.shape, s.dtype) for s in slots]
        self.scratch = [pltpu.SemaphoreType.DMA((n, 6)), pltpu.SemaphoreType.DMA((n, 6))]
        self.aliases = {a: a for a in range(n)}

    @staticmethod
    def _copy(outs, sems, a, j, k, half, to):
        dst = outs[a].at[k, half]
        return pltpu.make_async_remote_copy(src_ref=dst, dst_ref=dst, send_sem=sems[0].at[a, j], recv_sem=sems[1].at[a, j],
                                            device_id=to, device_id_type=MESH)

    def _first(self, outs, sems):
        x, y, c, chips = _place()
        return [self._copy(outs, sems, a, j, 2 * x + y, c, (*chip, c)) for a in range(self.n) for j, chip in enumerate(chips)]

    def start(self, ins, outs, sems):
        for cp in self._first(outs, sems):
            cp.start()

    def finish(self, ins, outs, sems):
        x, y, c, chips = _place()
        passed = []
        for a in range(self.n):
            for j, (px, py) in enumerate(chips):
                self._copy(outs, sems, a, j, 2 * px + py, c, (x, y, c)).wait_recv()
                fwd = self._copy(outs, sems, a, 3 + j, 2 * px + py, c, (x, y, 1 - c))
                fwd.start()
                passed.append(fwd)
        for a in range(self.n):
            for j, (px, py) in enumerate(chips):
                self._copy(outs, sems, a, 3 + j, 2 * px + py, 1 - c, (x, y, c)).wait_recv()
        for cp in self._first(outs, sems) + passed:
            cp.wait_send()


class _ScatterRider:
    def __init__(self, parts, row0=0, nrows=None):
        n = len(parts)
        self.n = n
        self.rows = (row0, parts[0].shape[1] - row0 if nrows is None else nrows)
        self.inputs = list(parts)
        self.out_shape = [jax.ShapeDtypeStruct((3, self.rows[1], p.shape[2]), p.dtype) for p in parts]
        self.scratch = [pltpu.SemaphoreType.DMA((n, 3)), pltpu.SemaphoreType.DMA((n, 3))]
        self.aliases = {}

    def _copies(self, ins, outs, sems):
        x, y, c, chips = _place()
        return [pltpu.make_async_remote_copy(
            src_ref=ins[a].at[2 * px + py, pl.ds(*self.rows)], dst_ref=outs[a].at[j], send_sem=sems[0].at[a, j],
            recv_sem=sems[1].at[a, j], device_id=(px, py, c), device_id_type=MESH)
            for a in range(self.n) for j, (px, py) in enumerate(chips)]

    def start(self, ins, outs, sems):
        for cp in self._copies(ins, outs, sems):
            cp.start()

    def finish(self, ins, outs, sems):
        for cp in self._copies(ins, outs, sems):
            cp.wait()


def _ride_alone(rider, name):
    n = len(rider.inputs)

    def body(*refs):
        ins, outs, sems = refs[:n], refs[n:n + len(rider.out_shape)], refs[n + len(rider.out_shape):]
        rider.start(ins, outs, sems)
        rider.finish(ins, outs, sems)

    return pl.pallas_call(
        body, name=name, out_shape=tuple(rider.out_shape), in_specs=[ANY] * n, out_specs=tuple([ANY] * len(rider.out_shape)),
        input_output_aliases=dict(rider.aliases), scratch_shapes=list(rider.scratch),
    )(*rider.inputs)


class _SwapRider:
    def __init__(self, grads):
        n = len(grads)
        self.n = n
        self.inputs = list(grads)
        self.out_shape = [jax.ShapeDtypeStruct((4,) + g.shape[2:], g.dtype) for g in grads]
        self.scratch = [pltpu.SemaphoreType.DMA((n, 4)), pltpu.SemaphoreType.DMA((n, 4))]
        self.aliases = {}

    def _copies(self, ins, outs, sems):
        x, y, c, _ = _place()
        return [pltpu.make_async_remote_copy(
            src_ref=ins[a].at[k, 1 - c], dst_ref=outs[a].at[k], send_sem=sems[0].at[a, k], recv_sem=sems[1].at[a, k],
            device_id=(x, y, 1 - c), device_id_type=MESH) for a in range(self.n) for k in range(4)]

    def start(self, ins, outs, sems):
        for cp in self._copies(ins, outs, sems):
            cp.start()

    def finish(self, ins, outs, sems):
        for cp in self._copies(ins, outs, sems):
            cp.wait()


class _Riders:
    def __init__(self, riders):
        self.riders = list(riders)
        self.inputs = [a for r in riders for a in r.inputs]
        self.out_shape = [s for r in riders for s in r.out_shape]
        self.scratch = [s for r in riders for s in r.scratch]
        self.aliases = {}
        i = o = 0
        for r in riders:
            self.aliases.update({i + a: o + b for a, b in r.aliases.items()})
            i, o = i + len(r.inputs), o + len(r.out_shape)

    def _each(self, ins, outs, sems):
        i = o = s = 0
        for r in self.riders:
            yield r, ins[i:i + len(r.inputs)], outs[o:o + len(r.out_shape)], sems[s:s + len(r.scratch)]
            i, o, s = i + len(r.inputs), o + len(r.out_shape), s + len(r.scratch)

    def start(self, ins, outs, sems):
        for r, a, b, c in self._each(ins, outs, sems):
            r.start(a, b, c)

    def finish(self, ins, outs, sems):
        for r, a, b, c in self._each(ins, outs, sems):
            r.finish(a, b, c)

    def split(self, outs):
        res, o = [], 0
        for r in self.riders:
            res.append(outs[o:o + len(r.out_shape)])
            o += len(r.out_shape)
        return res


class _Reducer:
    def __init__(self, chip, core):
        self.chip, self.core, self.grads, self.parts, self.sums, self.others = chip, core, {}, {}, {}, {}

    def swap(self, name, grad):
        self.grads[name] = grad
        return _SwapRider([grad])

    def swapped(self, name, got):
        self.parts[name] = _add_pair(self.grads[name], got[0], self.core, name)

    def scatter(self, name, row0=0, nrows=None):
        return _ScatterRider([self.parts[name]], row0, nrows)

    def scattered(self, name, others):
        self.sums[name] = _add_chips(self.parts[name], others[0], self.chip, name)


class _SwapSumsRider:
    def __init__(self, halves):
        n = len(halves)
        self.n = n
        self.inputs = list(halves)
        self.out_shape = [jax.ShapeDtypeStruct(s.shape, s.dtype) for s in halves]
        self.scratch = [pltpu.SemaphoreType.DMA((n,)), pltpu.SemaphoreType.DMA((n,))]
        self.aliases = {}

    def _copies(self, ins, outs, sems):
        x, y, c, _ = _place()
        return [pltpu.make_async_remote_copy(
            src_ref=ins[a], dst_ref=outs[a], send_sem=sems[0].at[a], recv_sem=sems[1].at[a],
            device_id=(x, y, 1 - c), device_id_type=MESH) for a in range(self.n)]

    def start(self, ins, outs, sems):
        for cp in self._copies(ins, outs, sems):
            cp.start()

    def finish(self, ins, outs, sems):
        for cp in self._copies(ins, outs, sems):
            cp.wait()


def _row_tile(r):
    for tm in (TM, 176, 128, 64, 32, 16, 8):
        if r % tm == 0:
            return tm
    return r


def _add_pair(mine, got, core, name):
    k, _, h, c = mine.shape
    tm = _row_tile(h)

    def body(core_ref, a_ref, b_ref, o_ref):
        o_ref[0] = (a_ref[0, 0].astype(f32) + b_ref[0].astype(f32)).astype(MX)

    blk = pl.BlockSpec((1, tm, c), lambda i, j, core: (i, j, 0))
    return pl.pallas_call(
        body, name="add_pair_" + name, out_shape=jax.ShapeDtypeStruct((k, h, c), MX),
        grid_spec=pltpu.PrefetchScalarGridSpec(
            num_scalar_prefetch=1, grid=(k, h // tm),
            in_specs=[pl.BlockSpec((1, 1, tm, c), lambda i, j, core: (i, core[0], j, 0)), blk], out_specs=blk),
        compiler_params=_cp("arbitrary", "arbitrary"),
    )(_scalar(core), mine, got)


def _add_chips(parts, others, chip, name, row0=0):
    _, n, c = others.shape
    tm = _row_tile(n)
    assert row0 % tm == 0
    i0 = row0 // tm

    def body(chip_ref, a_ref, b_ref, o_ref):
        s = a_ref[0].astype(f32) + b_ref[0].astype(f32)
        o_ref[...] = (s + b_ref[1].astype(f32)) + b_ref[2].astype(f32)

    return pl.pallas_call(
        body, name="add_chips_" + name, out_shape=jax.ShapeDtypeStruct((n, c), f32),
        grid_spec=pltpu.PrefetchScalarGridSpec(
            num_scalar_prefetch=1, grid=(n // tm,),
            in_specs=[pl.BlockSpec((1, tm, c), lambda i, chip: (chip[0], i0 + i, 0)),
                      pl.BlockSpec((3, tm, c), lambda i, chip: (0, i, 0))],
            out_specs=pl.BlockSpec((tm, c), lambda i, chip: (i, 0))),
        compiler_params=_cp("arbitrary"),
    )(_scalar(chip), parts, others)


def _adam_math(w, g, m, v):
    m = ADAM_B1 * m + (1.0 - ADAM_B1) * g
    v = ADAM_B2 * v + (1.0 - ADAM_B2) * (g * g)
    m_hat = m / (1.0 - ADAM_B1 ** ADAM_STEP)
    v_hat = v / (1.0 - ADAM_B2 ** ADAM_STEP)
    return -ADAM_LR * (m_hat / (jnp.sqrt(v_hat) + ADAM_EPS) + ADAM_WD * w), m, v


def _adamw_halves(w, mine, other, m, v, core, name, rider=None):
    r, c = w.shape
    h = r // 2
    tm = _row_tile(h)
    nj = h // tm
    cg = mine.shape[1]

    def body(core_ref, w_ref, a_ref, b_ref, m_ref, v_ref, g_ref, d_ref, nm_ref, nv_ref):
        g = jnp.where(pl.program_id(0) == core_ref[0], a_ref[:, 0:c], b_ref[:, 0:c])
        g_ref[...] = g
        d_ref[...], nm_ref[...], nv_ref[...] = _adam_math(w_ref[...], g, m_ref[...], v_ref[...])

    blk = pl.BlockSpec((tm, c), lambda i, j, core: (i * nj + j, 0))
    gblk = pl.BlockSpec((tm, cg), lambda i, j, core: (j, 0))
    return _call(body, name=name, grid=(2, nj), out_shape=[jax.ShapeDtypeStruct((r, c), f32)] * 4,
                 in_specs=[blk, gblk, gblk, blk, blk], out_specs=(blk,) * 4, sem=("arbitrary", "arbitrary"),
                 prefetch=(_scalar(core),), args=(w, mine, other, m, v), rider=rider)


def _ada_forward(c_all, ada_w):
    def body(c_ref, w_ref, o_ref):
        o_ref[...] = jnp.dot(_silu(c_ref[...]).astype(MX), w_ref[...].astype(MX), preferred_element_type=f32)

    return pl.pallas_call(body, name="ada_forward", out_shape=jax.ShapeDtypeStruct((8, ada_w.shape[1]), f32),
                          compiler_params=pltpu.CompilerParams(vmem_limit_bytes=VMEM_LIMIT))(c_all, ada_w)


def _ada_adamw(c_all_t, d_mod, w, m, v, rider=None):
    r, c = w.shape
    tm = TM

    def body(ct_ref, dm_ref, w_ref, m_ref, v_ref, g_ref, d_ref, nm_ref, nv_ref):
        ca = _silu(ct_ref[...])
        g = ca[:, 0:1] * dm_ref[0:1, :]
        for b in range(1, 8):
            g = g + ca[:, b:b + 1] * dm_ref[b:b + 1, :]
        g_ref[...] = g
        d_ref[...], nm_ref[...], nv_ref[...] = _adam_math(w_ref[...], g, m_ref[...], v_ref[...])

    blk = pl.BlockSpec((tm, c), lambda i: (i, 0))
    return _call(body, name="ada_adamw", grid=(r // tm,), out_shape=[jax.ShapeDtypeStruct((r, c), f32)] * 4,
                 in_specs=[pl.BlockSpec((tm, 8), lambda i: (i, 0)), pl.BlockSpec((8, c), lambda i: (0, 0)), blk, blk, blk],
                 out_specs=(blk,) * 4, sem=("arbitrary",), args=(c_all_t, d_mod, w, m, v), rider=rider)


WEIGHTS = ("ada_w", "ada_b", "norm1_w", "w_in", "ssd_conv_w", "ssd_conv_b", "dt_bias", "a_log", "d_skip", "ssd_norm_w",
           "conf_conv_w", "conf_conv_b", "conf_ln_w", "conf_ln_b", "w_out", "norm2_w", "w_up", "ffn_conv_w", "ffn_conv_b",
           "w_down", "final_norm_w")


def kernel(x, c, ada_w, ada_b, norm1_w, w_in, ssd_conv_w, ssd_conv_b, dt_bias, a_log, d_skip, ssd_norm_w, conf_conv_w, conf_conv_b, conf_ln_w, conf_ln_b, w_out, norm2_w, w_up, ffn_conv_w, ffn_conv_b, w_down, final_norm_w, loss_target, m_ada_w, m_ada_b, m_norm1_w, m_w_in, m_ssd_conv_w, m_ssd_conv_b, m_dt_bias, m_a_log, m_d_skip, m_ssd_norm_w, m_conf_conv_w, m_conf_conv_b, m_conf_ln_w, m_conf_ln_b, m_w_out, m_norm2_w, m_w_up, m_ffn_conv_w, m_ffn_conv_b, m_w_down, m_final_norm_w, v_ada_w, v_ada_b, v_norm1_w, v_w_in, v_ssd_conv_w, v_ssd_conv_b, v_dt_bias, v_a_log, v_d_skip, v_ssd_norm_w, v_conf_conv_w, v_conf_conv_b, v_conf_ln_w, v_conf_ln_b, v_w_out, v_norm2_w, v_w_up, v_ffn_conv_w, v_ffn_conv_b, v_w_down, v_final_norm_w):
    given = dict(locals())
    w = {n: given[n] for n in WEIGHTS}
    mom = {n: given["m_" + n] for n in WEIGHTS}
    var = {n: given["v_" + n] for n in WEIGHTS}
    chip = 2 * lax.axis_index("x") + lax.axis_index("y")
    me = 2 * chip + lax.axis_index("c")

    c_all, *convs = _unpack_front(_gather_rows(_pack_front(c, [w[n] for n in CONVS])))
    conv_full = dict(zip(CONVS, convs))

    mod_cols = _gather_rows(_ada_forward(c_all, ada_w[0])).reshape(8, 8, -1)[0::2]
    mod = lax.dynamic_index_in_dim(mod_cols, me, axis=1, keepdims=False).reshape(1, 6 * D) + ada_b

    core = lax.axis_index("c")
    a_in, = _ride_alone(_GatherRider([_cast_into_slot_w_in(_columns_first(w_in), chip)]), "gather_w_in")
    w_pack = _pack_w_in(a_in.reshape(4, D, W_IN_SHARD_PAD))
    late = (_cast_into_slot(w_out[0], D, chip), _cast_into_slot(w_up[0], UP_SHARD, chip), _cast_into_slot(w_down[0], D, chip))

    flat = lambda a: a.reshape(1, -1) if a.ndim == 1 else a
    small = {n: flat(w[n]) for n in VECTORS if n != "ada_b"}
    small.update(conv_full)
    reducer = _Reducer(chip, core)
    loss_mine, grad_x, _, gsmall = _local_step(x[0], mod, loss_target[0], w_pack, late, small, reducer)
    loss = lax.psum(loss_mine, ("x", "y", "c"))
    grads, delta, new_m, new_v = {}, {}, {}, {}

    names = VECTORS + tuple(CONVS)
    d_mod_mine, res = _small_adamw(_gather_rows(gsmall), chip, *[{n: flat(d[n]) for n in names} for d in (w, mom, var)])
    for n in names:
        grads[n], delta[n], new_m[n], new_v[n] = [r.reshape(w[n].shape) for r in res[n]]

    reducer.others["w_in"], = _ride_alone(_SwapSumsRider([reducer.sums["w_in"]]), "swap_sums_w_in")
    res = _adamw_w_in(_columns_first(w_in), reducer.sums["w_in"], reducer.others["w_in"], _columns_first(m_w_in),
                      _columns_first(v_w_in), core)
    grads["w_in"], delta["w_in"], new_m["w_in"], new_v["w_in"] = [jnp.transpose(r, (1, 2, 0)) for r in res]
    for n in ("w_out", "w_up", "w_down"):
        res, _ = _adamw_halves(w[n][0], reducer.sums[n], reducer.others[n], mom[n][0], var[n][0], core, "adamw_" + n)
        grads[n], delta[n], new_m[n], new_v[n] = [r[None] for r in res]
    res, _ = _ada_adamw(c_all.T, d_mod_mine, ada_w[0], m_ada_w[0], v_ada_w[0])
    grads["ada_w"], delta["ada_w"], new_m["ada_w"], new_v["ada_w"] = [r[None] for r in res]

    return (loss, grad_x[None], *[grads[n] for n in WEIGHTS], *[delta[n] for n in WEIGHTS],
            *[new_m[n] for n in WEIGHTS], *[new_v[n] for n in WEIGHTS])
```

```python
import functools

import jax
import jax.numpy as jnp
from jax import lax
from jax.experimental import pallas as pl
from jax.experimental.pallas import tpu as pltpu

f32 = jnp.float32
MX = jnp.bfloat16

D = 1024
HEADS = 16
HEAD_P = 64
STATE_N = 128
D_XBC = 1536
D_FF = 2816
UP_SHARD = 2 * D_FF // 4
UP_EARLY_ROWS = 128
K_SSD, K_CONF, K_FFN = 4, 31, 3
CHUNK = 128
OFF_Z, OFF_XBC, OFF_CA, OFF_CG, OFF_DT = 0, 1024, 2560, 3584, 4608
W_PACK = 4736
TM = 256
CW = 256
RC = 64
LANES = 128
VMEM_LIMIT = 56 * 1024 * 1024

ADAM_LR, ADAM_B1, ADAM_B2, ADAM_EPS, ADAM_WD, ADAM_STEP = 0.001, 0.9, 0.999, 1e-08, 0.01, 10

MESH = pl.DeviceIdType.MESH


def _cp(*sem):
    return pltpu.CompilerParams(dimension_semantics=sem, vmem_limit_bytes=VMEM_LIMIT)


def _resident(shape):
    nd = len(shape)
    return pl.BlockSpec(shape, lambda *_: (0,) * nd, pipeline_mode=pl.Buffered(1))


def _row(width=D):
    return pl.BlockSpec((1, width), lambda *_: (0, 0))


def _call(body, *, name, grid, in_specs, out_specs, out_shape, args, sem, scratch_shapes=(), prefetch=(), rider=None):
    ni, no, ns, npf = len(in_specs), len(out_specs), len(scratch_shapes), len(prefetch)
    ri, ro = (len(rider.inputs), len(rider.out_shape)) if rider is not None else (0, 0)

    def full(*refs):
        pre, refs = refs[:npf], refs[npf:]
        base_in, r_in = refs[:ni], refs[ni:ni + ri]
        base_out, r_out = refs[ni + ri:ni + ri + no], refs[ni + ri + no:ni + ri + no + ro]
        base_scr, r_scr = refs[ni + ri + no + ro:ni + ri + no + ro + ns], refs[ni + ri + no + ro + ns:]
        if rider is None:
            return body(*pre, *base_in, *base_out, *base_scr)
        ids = [pl.program_id(a) for a in range(len(grid))]
        first = functools.reduce(jnp.logical_and, [i == 0 for i in ids])
        last = functools.reduce(jnp.logical_and, [i == g - 1 for i, g in zip(ids, grid)])

        @pl.when(first)
        def _():
            rider.start(r_in, r_out, r_scr)

        body(*pre, *base_in, *base_out, *base_scr)

        @pl.when(last)
        def _():
            rider.finish(r_in, r_out, r_scr)

    extra = dict(shapes=[], scratch=[], aliases={}, inputs=[]) if rider is None else dict(
        shapes=rider.out_shape, scratch=rider.scratch, inputs=rider.inputs,
        aliases={npf + ni + i: no + j for i, j in rider.aliases.items()})
    outs = pl.pallas_call(
        full, name=name, out_shape=tuple(out_shape) + tuple(extra["shapes"]), input_output_aliases=extra["aliases"],
        grid_spec=pltpu.PrefetchScalarGridSpec(
            num_scalar_prefetch=npf, grid=grid, in_specs=list(in_specs) + [ANY] * ri,
            out_specs=tuple(out_specs) + (ANY,) * ro, scratch_shapes=list(scratch_shapes) + list(extra["scratch"])),
        compiler_params=_cp(*sem),
    )(*prefetch, *args, *extra["inputs"])
    return tuple(outs[:no]), tuple(outs[no:])


def _silu(v):
    return v * jax.nn.sigmoid(v)


def _dsilu(v):
    s = jax.nn.sigmoid(v)
    return s * (1.0 + v * (1.0 - s))


def _softplus(v):
    return jnp.maximum(v, 0.0) + jnp.log1p(jnp.exp(-jnp.abs(v)))


def _mm(a, b):
    return jnp.dot(a.astype(MX), b.astype(MX), preferred_element_type=f32)


def _mm_nt(a, b):
    return lax.dot_general(a.astype(MX), b.astype(MX), (((1,), (1,)), ((), ())), preferred_element_type=f32)


def _mm_tn(a, b):
    return lax.dot_general(a.astype(MX), b.astype(MX), (((0,), (0,)), ((), ())), preferred_element_type=f32)


def _ln_inproj(x, mod, norm1_w, w_pack, rider=None):
    t = x.shape[0]

    def body(x_ref, mod_ref, nw_ref, w_ref, proj_ref, ht_ref):
        xv = x_ref[...]
        rstd = lax.rsqrt(jnp.mean(xv * xv, axis=-1, keepdims=True) + 1e-6)
        h = (xv * rstd * nw_ref[...]) * (1.0 + mod_ref[:, D:2 * D]) + mod_ref[:, 0:D]
        hb = h.astype(MX)
        ht_ref[...] = hb.T
        proj_ref[...] = jnp.dot(hb, w_ref[...], preferred_element_type=f32)

    return _call(
        body, name="ln_inproj", grid=(t // TM,),
        out_shape=(jax.ShapeDtypeStruct((t, W_PACK), f32), jax.ShapeDtypeStruct((D, t), MX)),
        in_specs=[pl.BlockSpec((TM, D), lambda i: (i, 0)), _row(6 * D), _row(), _resident((D, W_PACK))],
        out_specs=(pl.BlockSpec((TM, W_PACK), lambda i: (i, 0)), pl.BlockSpec((D, TM), lambda i: (0, i))),
        sem=("arbitrary",), args=(x, mod, norm1_w, w_pack), rider=rider)


def _ssd_gate_norm(y_scan, xbc_act, proj, d_skip_row, ssd_norm_w):
    t = y_scan.shape[0]

    def body(y_ref, xs_ref, z_ref, dsk_ref, nw_ref, o_ref, ot_ref):
        y = y_ref[...] + xs_ref[...] * dsk_ref[...]
        yz = y * _silu(z_ref[...])
        rstd = lax.rsqrt(jnp.mean(yz * yz, axis=-1, keepdims=True) + 1e-6)
        out = (yz * rstd * nw_ref[...]).astype(MX)
        o_ref[...] = out
        ot_ref[...] = out.T

    blk = pl.BlockSpec((TM, D), lambda i: (i, 0))
    return pl.pallas_call(
        body, name="ssd_gate_norm", grid=(t // TM,),
        out_shape=(jax.ShapeDtypeStruct((t, D), MX), jax.ShapeDtypeStruct((D, t), MX)),
        in_specs=[blk, blk, blk, _row(), _row()], out_specs=(blk, pl.BlockSpec((D, TM), lambda i: (0, i))),
        compiler_params=_cp("arbitrary"),
    )(y_scan, xbc_act, proj, d_skip_row, ssd_norm_w)


def _ln_silu(u_conv, ln_w, ln_b):
    t = u_conv.shape[0]

    def body(u_ref, w_ref, b_ref, o_ref, ot_ref):
        u = u_ref[...]
        mu = jnp.mean(u, axis=-1, keepdims=True)
        uc = u - mu
        rstd = lax.rsqrt(jnp.mean(uc * uc, axis=-1, keepdims=True) + 1e-5)
        out = _silu(uc * rstd * w_ref[...] + b_ref[...]).astype(MX)
        o_ref[...] = out
        ot_ref[...] = out.T

    blk = pl.BlockSpec((TM, D), lambda i: (i, 0))
    return pl.pallas_call(
        body, name="ln_silu", grid=(t // TM,),
        out_shape=(jax.ShapeDtypeStruct((t, D), MX), jax.ShapeDtypeStruct((D, t), MX)),
        in_specs=[blk, _row(), _row()], out_specs=(blk, pl.BlockSpec((D, TM), lambda i: (0, i))),
        compiler_params=_cp("arbitrary"),
    )(u_conv, ln_w, ln_b)


def _outproj_ln2_up(y_ssd, u, w_out, x, mod, norm2_w, w_up):
    t = x.shape[0]

    def body(y_ref, u_ref, wo_ref, x_ref, mod_ref, nw_ref, wu_ref, mix_ref, x1_ref, h2t_ref, up_ref):
        mix = jnp.dot(y_ref[...], wo_ref[0:D, :], preferred_element_type=f32)
        mix = mix + jnp.dot(u_ref[...], wo_ref[D:2 * D, :], preferred_element_type=f32)
        mix_ref[...] = mix
        x1 = x_ref[...] + mod_ref[:, 2 * D:3 * D] * mix
        x1_ref[...] = x1
        rstd = lax.rsqrt(jnp.mean(x1 * x1, axis=-1, keepdims=True) + 1e-6)
        h2 = ((x1 * rstd * nw_ref[...]) * (1.0 + mod_ref[:, 4 * D:5 * D]) + mod_ref[:, 3 * D:4 * D]).astype(MX)
        h2t_ref[...] = h2.T
        for k in range(4):
            up_ref[:, k * UP_SHARD:(k + 1) * UP_SHARD] = jnp.dot(h2, wu_ref[k], preferred_element_type=f32)

    blk = pl.BlockSpec((TM, D), lambda i: (i, 0))
    return pl.pallas_call(
        body, name="outproj_ln2_up", grid=(t // TM,),
        out_shape=(jax.ShapeDtypeStruct((t, D), f32), jax.ShapeDtypeStruct((t, D), f32),
                   jax.ShapeDtypeStruct((D, t), MX), jax.ShapeDtypeStruct((t, 2 * D_FF), f32)),
        in_specs=[blk, blk, _resident((2 * D, D)), blk, _row(6 * D), _row(), _resident((4, D, UP_SHARD))],
        out_specs=(blk, blk, pl.BlockSpec((D, TM), lambda i: (0, i)), pl.BlockSpec((TM, 2 * D_FF), lambda i: (i, 0))),
        compiler_params=_cp("arbitrary"),
    )(y_ssd, u, w_out, x, mod, norm2_w, w_up)


def _down_loss(act, w_down, x1, mod, final_norm_w, target):
    t = x1.shape[0]

    def body(a_ref, wd_ref, x1_ref, mod_ref, wf_ref, tgt_ref, dx2_ref, dffn_ref, dact_ref, st_ref):
        @pl.when(pl.program_id(0) == 0)
        def _():
            st_ref[...] = jnp.zeros_like(st_ref)

        g2 = mod_ref[:, 5 * D:6 * D]
        ffn = jnp.dot(a_ref[...], wd_ref[...], preferred_element_type=f32)
        x2 = x1_ref[...] + g2 * ffn
        rstd = lax.rsqrt(jnp.mean(x2 * x2, axis=-1, keepdims=True) + 1e-6)
        xh = x2 * rstd
        wf = wf_ref[...]
        err = xh * wf - tgt_ref[...]
        dy = err * (1.0 / D)
        dxh = dy * wf
        dx2 = rstd * (dxh - xh * jnp.mean(dxh * xh, axis=-1, keepdims=True))
        dx2_ref[...] = dx2
        dffn = (g2 * dx2).astype(MX)
        dffn_ref[...] = dffn
        dact_ref[...] = lax.dot_general(dffn, wd_ref[...], (((1,), (1,)), ((), ())), preferred_element_type=f32)
        st_ref[0:1, :] += jnp.sum(dy * xh, axis=0, keepdims=True)
        st_ref[1:2, :] += jnp.sum(dx2 * ffn, axis=0, keepdims=True)
        st_ref[2:3, :] += jnp.sum(0.5 * jnp.mean(err * err, axis=-1, keepdims=True), axis=0, keepdims=True)

    blk = pl.BlockSpec((TM, D), lambda i: (i, 0))
    ablk = pl.BlockSpec((TM, D_FF), lambda i: (i, 0))
    return pl.pallas_call(
        body, name="down_loss", grid=(t // TM,),
        out_shape=(jax.ShapeDtypeStruct((t, D), f32), jax.ShapeDtypeStruct((t, D), MX),
                   jax.ShapeDtypeStruct((t, D_FF), f32), jax.ShapeDtypeStruct((8, D), f32)),
        in_specs=[ablk, _resident((D_FF, D)), blk, _row(6 * D), _row(), blk],
        out_specs=(blk, blk, ablk, pl.BlockSpec((8, D), lambda i: (0, 0))),
        compiler_params=_cp("arbitrary"),
    )(act, w_down, x1, mod, final_norm_w, target)


def _pad_of(k):
    return 8 * ((k - 1 + 7) // 8)


def _causal_win(ref, r, t, pad):
    base = pl.multiple_of(r * RC, RC)
    prev = ref[pl.ds(pl.multiple_of(jnp.maximum(base - pad, 0), 8), pad), :]
    prev = jnp.where(r > 0, prev, 0.0)
    return jnp.concatenate([prev, ref[pl.ds(base, RC), :]], axis=0)


def _anti_win(ref, r, t, pad):
    base = pl.multiple_of(r * RC, RC)
    nxt = ref[pl.ds(pl.multiple_of(jnp.minimum(base + RC, t - pad), 8), pad), :]
    nxt = jnp.where(r < t // RC - 1, nxt, 0.0)
    return jnp.concatenate([ref[pl.ds(base, RC), :], nxt], axis=0)


def _shifted(win, offsets):
    for r in range(8):
        mine = [o for o in offsets if o % 8 == r]
        if mine:
            rolled = win if r == 0 else pltpu.roll(win, win.shape[0] - r, 0)
            for o in mine:
                yield o, rolled[o - r:o - r + RC, :]


def _conv_taps(win, w_ref, k, pad):
    first = pad - (k - 1)
    acc = None
    for o, rows in _shifted(win, range(first, first + k)):
        term = w_ref[o - first:o - first + 1, :] * rows
        acc = term if acc is None else acc + term
    return acc


def _corr_taps(win, w_ref, k):
    acc = None
    for o, rows in _shifted(win, range(k)):
        term = w_ref[k - 1 - o:k - o, :] * rows
        acc = term if acc is None else acc + term
    return acc


def _dw_accumulate(dw_scr, d, win, k, pad):
    first = pad - (k - 1)
    for o, rows in _shifted(win, range(first, first + k)):
        j = o - first
        prod = d * rows
        dw_scr[8 * j:8 * j + 8, :] += prod.reshape(RC // 8, 8, prod.shape[-1]).sum(axis=0)


def _dw_finish(dw_scr, dw_ref, k):
    for j in range(k):
        dw_ref[j:j + 1, :] = jnp.sum(dw_scr[8 * j:8 * j + 8, :], axis=0, keepdims=True)


def _rows8(v):
    return v.reshape(RC // 8, 8, v.shape[-1]).sum(axis=0)


def _ssd_conv_fwd(proj, conv_w, conv_b, rider=None):
    t = proj.shape[0]
    pad = _pad_of(K_SSD)
    c0 = OFF_XBC // CW

    def body(x_ref, w_ref, b_ref, o_ref):
        def step(r, carry):
            win = _causal_win(x_ref, r, t, pad)
            o_ref[pl.ds(pl.multiple_of(r * RC, RC), RC), :] = _silu(_conv_taps(win, w_ref, K_SSD, pad) + b_ref[...])
            return carry
        lax.fori_loop(0, t // RC, step, 0)

    return _call(
        body, name="ssd_conv_fwd", grid=(D_XBC // CW,), out_shape=(jax.ShapeDtypeStruct((t, D_XBC), f32),),
        in_specs=[pl.BlockSpec((t, CW), lambda j: (0, c0 + j)), pl.BlockSpec((K_SSD, CW), lambda j: (0, j)),
                  pl.BlockSpec((1, CW), lambda j: (0, j))],
        out_specs=(pl.BlockSpec((t, CW), lambda j: (0, j)),), sem=("arbitrary",), args=(proj, conv_w, conv_b), rider=rider)


def _glu_conv_fwd(proj, conv_w, conv_b, rider=None):
    t = proj.shape[0]
    pad = _pad_of(K_CONF)
    ca, cg = OFF_CA // CW, OFF_CG // CW

    def body(a_ref, g_ref, w_ref, b_ref, o_ref, v_scr):
        def glu(r, carry):
            rows = pl.ds(pl.multiple_of(r * RC, RC), RC)
            v_scr[rows, :] = a_ref[rows, :] * jax.nn.sigmoid(g_ref[rows, :])
            return carry
        lax.fori_loop(0, t // RC, glu, 0)

        def step(r, carry):
            win = _causal_win(v_scr, r, t, pad)
            o_ref[pl.ds(pl.multiple_of(r * RC, RC), RC), :] = _conv_taps(win, w_ref, K_CONF, pad) + b_ref[...]
            return carry
        lax.fori_loop(0, t // RC, step, 0)

    return _call(
        body, name="glu_conv_fwd", grid=(D // CW,), out_shape=(jax.ShapeDtypeStruct((t, D), f32),),
        in_specs=[pl.BlockSpec((t, CW), lambda j: (0, ca + j)), pl.BlockSpec((t, CW), lambda j: (0, cg + j)),
                  pl.BlockSpec((K_CONF, CW), lambda j: (0, j)), pl.BlockSpec((1, CW), lambda j: (0, j))],
        out_specs=(pl.BlockSpec((t, CW), lambda j: (0, j)),),
        scratch_shapes=[pltpu.VMEM((t, CW), f32)], sem=("arbitrary",), args=(proj, proj, conv_w, conv_b), rider=rider)


def _ffn_conv_fwd(up, conv_w, conv_b, rider=None):
    t = up.shape[0]
    pad = _pad_of(K_FFN)
    nb = D_FF // CW

    def body(g_ref, v_ref, wg_ref, wv_ref, bg_ref, bv_ref, o_ref, ot_ref):
        def step(r, carry):
            gc = _conv_taps(_causal_win(g_ref, r, t, pad), wg_ref, K_FFN, pad) + bg_ref[...]
            vc = _conv_taps(_causal_win(v_ref, r, t, pad), wv_ref, K_FFN, pad) + bv_ref[...]
            o_ref[pl.ds(pl.multiple_of(r * RC, RC), RC), :] = (_silu(gc) * vc).astype(MX)
            return carry
        lax.fori_loop(0, t // RC, step, 0)
        ot_ref[...] = o_ref[...].T

    return _call(
        body, name="ffn_conv_fwd", grid=(nb,),
        out_shape=(jax.ShapeDtypeStruct((t, D_FF), MX), jax.ShapeDtypeStruct((D_FF, t), MX)),
        in_specs=[pl.BlockSpec((t, CW), lambda j: (0, j)), pl.BlockSpec((t, CW), lambda j: (0, nb + j)),
                  pl.BlockSpec((K_FFN, CW), lambda j: (0, j)), pl.BlockSpec((K_FFN, CW), lambda j: (0, nb + j)),
                  pl.BlockSpec((1, CW), lambda j: (0, j)), pl.BlockSpec((1, CW), lambda j: (0, nb + j))],
        out_specs=(pl.BlockSpec((t, CW), lambda j: (0, j)), pl.BlockSpec((CW, t), lambda j: (j, 0))), sem=("arbitrary",),
        args=(up, up, conv_w, conv_w, conv_b, conv_b), rider=rider)


def _ffn_conv_bwd(up, conv_w, conv_b, d_act, rider=None):
    t = up.shape[0]
    pad = _pad_of(K_FFN)
    nb = D_FF // CW

    def body(g_ref, v_ref, wg_ref, wv_ref, bg_ref, bv_ref, da_ref, dup_ref, dw_ref, db_ref,
             dg_scr, dv_scr, dwg_scr, dwv_scr, db_scr):
        dwg_scr[...] = jnp.zeros_like(dwg_scr)
        dwv_scr[...] = jnp.zeros_like(dwv_scr)
        db_scr[...] = jnp.zeros_like(db_scr)

        def first(r, carry):
            rows = pl.ds(pl.multiple_of(r * RC, RC), RC)
            gwin = _causal_win(g_ref, r, t, pad)
            vwin = _causal_win(v_ref, r, t, pad)
            gc = _conv_taps(gwin, wg_ref, K_FFN, pad) + bg_ref[...]
            vc = _conv_taps(vwin, wv_ref, K_FFN, pad) + bv_ref[...]
            da = da_ref[rows, :]
            dgc = da * vc * _dsilu(gc)
            dvc = da * _silu(gc)
            dg_scr[rows, :] = dgc
            dv_scr[rows, :] = dvc
            _dw_accumulate(dwg_scr, dgc, gwin, K_FFN, pad)
            _dw_accumulate(dwv_scr, dvc, vwin, K_FFN, pad)
            db_scr[0:8, :] += _rows8(dgc)
            db_scr[8:16, :] += _rows8(dvc)
            return carry
        lax.fori_loop(0, t // RC, first, 0)

        def second(r, carry):
            rows = pl.ds(pl.multiple_of(r * RC, RC), RC)
            dup_ref[0, rows, :] = _corr_taps(_anti_win(dg_scr, r, t, pad), wg_ref, K_FFN).astype(MX)
            dup_ref[1, rows, :] = _corr_taps(_anti_win(dv_scr, r, t, pad), wv_ref, K_FFN).astype(MX)
            return carry
        lax.fori_loop(0, t // RC, second, 0)

        for j in range(K_FFN):
            dw_ref[0, j:j + 1, :] = jnp.sum(dwg_scr[8 * j:8 * j + 8, :], axis=0, keepdims=True)
            dw_ref[1, j:j + 1, :] = jnp.sum(dwv_scr[8 * j:8 * j + 8, :], axis=0, keepdims=True)
        db_ref[0] = jnp.sum(db_scr[0:8, :], axis=0, keepdims=True)
        db_ref[1] = jnp.sum(db_scr[8:16, :], axis=0, keepdims=True)

    return _call(
        body, name="ffn_conv_bwd", grid=(nb,),
        out_shape=(jax.ShapeDtypeStruct((2, t, D_FF), MX), jax.ShapeDtypeStruct((2, K_FFN, D_FF), f32),
                   jax.ShapeDtypeStruct((2, 1, D_FF), f32)),
        in_specs=[pl.BlockSpec((t, CW), lambda j: (0, j)), pl.BlockSpec((t, CW), lambda j: (0, nb + j)),
                  pl.BlockSpec((K_FFN, CW), lambda j: (0, j)), pl.BlockSpec((K_FFN, CW), lambda j: (0, nb + j)),
                  pl.BlockSpec((1, CW), lambda j: (0, j)), pl.BlockSpec((1, CW), lambda j: (0, nb + j)),
                  pl.BlockSpec((t, CW), lambda j: (0, j))],
        out_specs=(pl.BlockSpec((2, t, CW), lambda j: (0, 0, j)), pl.BlockSpec((2, K_FFN, CW), lambda j: (0, 0, j)),
                   pl.BlockSpec((2, 1, CW), lambda j: (0, 0, j))),
        scratch_shapes=[pltpu.VMEM((t, CW), f32), pltpu.VMEM((t, CW), f32), pltpu.VMEM((8 * K_FFN, CW), f32),
                        pltpu.VMEM((8 * K_FFN, CW), f32), pltpu.VMEM((16, CW), f32)],
        sem=("arbitrary",), args=(up, up, conv_w, conv_w, conv_b, conv_b, d_act), rider=rider)


def _glu_conv_bwd(proj, conv_w, d_uconv, rider=None):
    t = proj.shape[0]
    pad = _pad_of(K_CONF)
    ca, cg = OFF_CA // CW, OFF_CG // CW

    def body(a_ref, g_ref, w_ref, du_ref, dc_ref, dw_ref, db_ref, v_scr, dw_scr, db_scr):
        dw_scr[...] = jnp.zeros_like(dw_scr)
        db_scr[...] = jnp.zeros_like(db_scr)

        def glu(r, carry):
            rows = pl.ds(pl.multiple_of(r * RC, RC), RC)
            v_scr[rows, :] = a_ref[rows, :] * jax.nn.sigmoid(g_ref[rows, :])
            return carry
        lax.fori_loop(0, t // RC, glu, 0)

        def step(r, carry):
            rows = pl.ds(pl.multiple_of(r * RC, RC), RC)
            du = du_ref[rows, :]
            _dw_accumulate(dw_scr, du, _causal_win(v_scr, r, t, pad), K_CONF, pad)
            db_scr[...] += _rows8(du)
            dv = _corr_taps(_anti_win(du_ref, r, t, pad), w_ref, K_CONF)
            a = a_ref[rows, :]
            s = jax.nn.sigmoid(g_ref[rows, :])
            dc_ref[0, rows, :] = (dv * s).astype(MX)
            dc_ref[1, rows, :] = (dv * a * s * (1.0 - s)).astype(MX)
            return carry
        lax.fori_loop(0, t // RC, step, 0)
        _dw_finish(dw_scr, dw_ref, K_CONF)
        db_ref[...] = jnp.sum(db_scr[...], axis=0, keepdims=True)

    return _call(
        body, name="glu_conv_bwd", grid=(D // CW,),
        out_shape=(jax.ShapeDtypeStruct((2, t, D), MX), jax.ShapeDtypeStruct((K_CONF, D), f32),
                   jax.ShapeDtypeStruct((1, D), f32)),
        in_specs=[pl.BlockSpec((t, CW), lambda j: (0, ca + j)), pl.BlockSpec((t, CW), lambda j: (0, cg + j)),
                  pl.BlockSpec((K_CONF, CW), lambda j: (0, j)), pl.BlockSpec((t, CW), lambda j: (0, j))],
        out_specs=(pl.BlockSpec((2, t, CW), lambda j: (0, 0, j)), pl.BlockSpec((K_CONF, CW), lambda j: (0, j)),
                   pl.BlockSpec((1, CW), lambda j: (0, j))),
        scratch_shapes=[pltpu.VMEM((t, CW), f32), pltpu.VMEM((8 * K_CONF, CW), f32), pltpu.VMEM((8, CW), f32)],
        sem=("arbitrary",), args=(proj, proj, conv_w, d_uconv), rider=rider)


def _ssd_conv_bwd_x(proj, conv_w, conv_b, d_xs, d_y, d_skip_row):
    t = proj.shape[0]
    pad = _pad_of(K_SSD)
    c0 = OFF_XBC // CW

    def body(x_ref, w_ref, b_ref, dxs_ref, dy_ref, dsk_ref, draw_ref, dw_ref, db_ref, dp_scr, dw_scr, db_scr):
        dw_scr[...] = jnp.zeros_like(dw_scr)
        db_scr[...] = jnp.zeros_like(db_scr)

        def first(r, carry):
            rows = pl.ds(pl.multiple_of(r * RC, RC), RC)
            win = _causal_win(x_ref, r, t, pad)
            pre = _conv_taps(win, w_ref, K_SSD, pad) + b_ref[...]
            dpre = (dxs_ref[rows, :] + dy_ref[rows, :] * dsk_ref[...]) * _dsilu(pre)
            dp_scr[rows, :] = dpre
            _dw_accumulate(dw_scr, dpre, win, K_SSD, pad)
            db_scr[...] += _rows8(dpre)
            return carry
        lax.fori_loop(0, t // RC, first, 0)

        def second(r, carry):
            rows = pl.ds(pl.multiple_of(r * RC, RC), RC)
            draw_ref[rows, :] = _corr_taps(_anti_win(dp_scr, r, t, pad), w_ref, K_SSD).astype(MX)
            return carry
        lax.fori_loop(0, t // RC, second, 0)
        _dw_finish(dw_scr, dw_ref, K_SSD)
        db_ref[...] = jnp.sum(db_scr[...], axis=0, keepdims=True)

    cb = pl.BlockSpec((t, CW), lambda j: (0, j))
    return pl.pallas_call(
        body, name="ssd_conv_bwd_x", grid=(D // CW,),
        out_shape=(jax.ShapeDtypeStruct((t, D), MX), jax.ShapeDtypeStruct((K_SSD, D), f32),
                   jax.ShapeDtypeStruct((1, D), f32)),
        in_specs=[pl.BlockSpec((t, CW), lambda j: (0, c0 + j)), pl.BlockSpec((K_SSD, CW), lambda j: (0, j)),
                  pl.BlockSpec((1, CW), lambda j: (0, j)), cb, cb, pl.BlockSpec((1, CW), lambda j: (0, j))],
        out_specs=(cb, pl.BlockSpec((K_SSD, CW), lambda j: (0, j)), pl.BlockSpec((1, CW), lambda j: (0, j))),
        scratch_shapes=[pltpu.VMEM((t, CW), f32), pltpu.VMEM((8 * K_SSD, CW), f32), pltpu.VMEM((8, CW), f32)],
        compiler_params=_cp("arbitrary"),
    )(proj, conv_w, conv_b, d_xs, d_y, d_skip_row)


def _ssd_conv_bwd_bc(proj, conv_w, conv_b, d_bc):
    t = proj.shape[0]
    pad = _pad_of(K_SSD)
    c0 = (OFF_XBC + D) // CW
    w0 = D // CW

    def body(x_ref, w_ref, b_ref, dbc_ref, draw_ref, dw_ref, db_ref, dp_scr, dw_scr, db_scr):
        dw_scr[...] = jnp.zeros_like(dw_scr)
        db_scr[...] = jnp.zeros_like(db_scr)

        def first(r, carry):
            rows = pl.ds(pl.multiple_of(r * RC, RC), RC)
            win = _causal_win(x_ref, r, t, pad)
            pre = _conv_taps(win, w_ref, K_SSD, pad) + b_ref[...]
            dpre = dbc_ref[0, rows, :] * _dsilu(pre)
            dp_scr[rows, :] = dpre
            _dw_accumulate(dw_scr, dpre, win, K_SSD, pad)
            db_scr[...] += _rows8(dpre)
            return carry
        lax.fori_loop(0, t // RC, first, 0)

        def second(r, carry):
            rows = pl.ds(pl.multiple_of(r * RC, RC), RC)
            draw_ref[rows, :] = _corr_taps(_anti_win(dp_scr, r, t, pad), w_ref, K_SSD).astype(MX)
            return carry
        lax.fori_loop(0, t // RC, second, 0)
        _dw_finish(dw_scr, dw_ref, K_SSD)
        db_ref[...] = jnp.sum(db_scr[...], axis=0, keepdims=True)

    return pl.pallas_call(
        body, name="ssd_conv_bwd_bc", grid=(2,),
        out_shape=(jax.ShapeDtypeStruct((t, 2 * CW), MX), jax.ShapeDtypeStruct((K_SSD, 2 * CW), f32),
                   jax.ShapeDtypeStruct((1, 2 * CW), f32)),
        in_specs=[pl.BlockSpec((t, CW), lambda j: (0, c0 + j)), pl.BlockSpec((K_SSD, CW), lambda j: (0, w0 + j)),
                  pl.BlockSpec((1, CW), lambda j: (0, w0 + j)), pl.BlockSpec((1, t, CW), lambda j: (j, 0, 0))],
        out_specs=(pl.BlockSpec((t, CW), lambda j: (0, j)), pl.BlockSpec((K_SSD, CW), lambda j: (0, j)),
                   pl.BlockSpec((1, CW), lambda j: (0, j))),
        scratch_shapes=[pltpu.VMEM((t, CW), f32), pltpu.VMEM((8 * K_SSD, CW), f32), pltpu.VMEM((8, CW), f32)],
        compiler_params=_cp("arbitrary"),
    )(proj, conv_w, conv_b, d_bc)


def _chunk_masks():
    ii = lax.broadcasted_iota(jnp.int32, (CHUNK, CHUNK), 0)
    jj = lax.broadcasted_iota(jnp.int32, (CHUNK, CHUNK), 1)
    return ii == jj, jj <= ii, jj >= ii


def _to_row(col, eye):
    return jnp.sum(jnp.where(eye, col, 0.0), axis=0, keepdims=True)


def _to_col(row, eye):
    return jnp.sum(jnp.where(eye, row, 0.0), axis=1, keepdims=True)


def _head_decay(dt_h, a_h, eye, tril):
    a_row = _to_row(dt_h * a_h, eye)
    cs = jnp.sum(jnp.where(tril, a_row, 0.0), axis=1, keepdims=True)
    cs_row = _to_row(cs, eye)
    decay = jnp.where(tril, jnp.exp(jnp.where(tril, cs - cs_row, 0.0)), 0.0)
    total = jnp.sum(a_row, axis=1, keepdims=True)
    return cs, decay, total


SCAN_UNROLL = 4


def _unrolled_loop(n, step, init):
    unroll = min(SCAN_UNROLL, n)
    assert n % unroll == 0

    def trip(i, carry):
        for u in range(unroll):
            carry = step(unroll * i + u, carry)
        return carry
    return lax.fori_loop(0, n // unroll, trip, init)


def _lane_pick(mat, lane, which):
    return jnp.sum(jnp.where(lane == which, mat, 0.0), axis=1, keepdims=True)


def _ssd_fwd(xbc_act, proj, dt_bias_row, a_log_row, rider=None):
    t = xbc_act.shape[0]
    nc = t // CHUNK
    cb, cc, cdt = D // LANES, (D + 2 * STATE_N) // LANES, OFF_DT // LANES

    def body(x_ref, b_ref, c_ref, dt_ref, dtb_ref, alog_ref, y_ref, st_ref):
        j = pl.program_id(0)
        eye, tril, _ = _chunk_masks()
        lane = lax.broadcasted_iota(jnp.int32, (1, LANES), 1)
        first = lane < HEAD_P
        a_row = -jnp.exp(alog_ref[...])
        a_heads = [jnp.sum(jnp.where(lane == 2 * j + h, a_row, 0.0), axis=1, keepdims=True) for h in range(2)]

        def chunk(c, hprev):
            rows = pl.ds(pl.multiple_of(c * CHUNK, CHUNK), CHUNK)
            xv, bm, cm = x_ref[rows, :], b_ref[rows, :], c_ref[rows, :]
            dt = _softplus(dt_ref[rows, :] + dtb_ref[...])
            st_ref[c] = hprev
            g = _mm_nt(cm, bm)
            ch = _mm(cm, hprev)
            dts = [_lane_pick(dt, lane, 2 * j + h) for h in range(2)]
            xdt = xv * jnp.where(first, dts[0], dts[1])
            ys, hs = [], []
            for h in range(2):
                cs, decay, total = _head_decay(dts[h], a_heads[h], eye, tril)
                y = _mm(g * decay, xdt) + jnp.exp(cs) * ch
                s = _mm_tn(bm * jnp.exp(total - cs), xdt)
                ys.append(y)
                hs.append(jnp.exp(total) * hprev + s)
            y_ref[rows, :] = jnp.where(first, ys[0], ys[1])
            return jnp.where(first, hs[0], hs[1])

        _unrolled_loop(nc, chunk, jnp.zeros((STATE_N, LANES), f32))

    blk = lambda f: pl.BlockSpec((t, LANES), f)
    return _call(
        body, name="ssd_fwd", grid=(D // LANES,),
        out_shape=(jax.ShapeDtypeStruct((t, D), f32), jax.ShapeDtypeStruct((nc, STATE_N, D), f32)),
        in_specs=[blk(lambda j: (0, j)), blk(lambda j: (0, cb + j // 4)), blk(lambda j: (0, cc + j // 4)),
                  blk(lambda j: (0, cdt)), _row(LANES), _row(LANES)],
        out_specs=(blk(lambda j: (0, j)), pl.BlockSpec((nc, STATE_N, LANES), lambda j: (0, 0, j))),
        sem=("arbitrary",), args=(xbc_act, xbc_act, xbc_act, proj, dt_bias_row, a_log_row), rider=rider)


def _ssd_bwd(xbc_act, proj, dt_bias_row, a_log_row, states, d_y, rider=None):
    t = xbc_act.shape[0]
    nc = t // CHUNK
    cb, cc, cdt = D // LANES, (D + 2 * STATE_N) // LANES, OFF_DT // LANES

    def body(x_ref, b_ref, c_ref, dt_ref, dtb_ref, alog_ref, st_ref, dy_ref, dx_ref, dbc_ref, ddt_ref, da_ref):
        grp, p = pl.program_id(0), pl.program_id(1)
        j = 4 * grp + p
        eye, tril, triu = _chunk_masks()
        lane = lax.broadcasted_iota(jnp.int32, (1, LANES), 1)
        first = lane < HEAD_P
        last_row = lax.broadcasted_iota(jnp.int32, (CHUNK, 1), 0) == CHUNK - 1
        a_row = -jnp.exp(alog_ref[...])
        a_heads = [jnp.sum(jnp.where(lane == 2 * j + h, a_row, 0.0), axis=1, keepdims=True) for h in range(2)]

        @pl.when(p == 0)
        def _():
            dbc_ref[...] = jnp.zeros_like(dbc_ref)

        @pl.when(j == 0)
        def _():
            ddt_ref[...] = jnp.zeros_like(ddt_ref)
            da_ref[...] = jnp.zeros_like(da_ref)

        def chunk(i, dh):
            c = nc - 1 - i
            rows = pl.ds(pl.multiple_of(c * CHUNK, CHUNK), CHUNK)
            xv, bm, cm = x_ref[rows, :], b_ref[rows, :], c_ref[rows, :]
            dtr = dt_ref[rows, :] + dtb_ref[...]
            dt = _softplus(dtr)
            hprev = st_ref[c]
            dy = dy_ref[rows, :]
            g = _mm_nt(cm, bm)
            dts = [_lane_pick(dt, lane, 2 * j + h) for h in range(2)]
            xdt = xv * jnp.where(first, dts[0], dts[1])
            dxs, dhs = [], []
            db_sum, dc_sum = None, None
            ddt_mat = jnp.zeros((CHUNK, LANES), f32)
            da_acc = jnp.zeros((1, LANES), f32)
            for h in range(2):
                mine = first if h == 0 else jnp.logical_not(first)
                cs, decay, total = _head_decay(dts[h], a_heads[h], eye, tril)
                e_cs, e_tot = jnp.exp(cs), jnp.exp(total)
                dec_s = jnp.exp(total - cs)
                dyh = jnp.where(mine, dy, 0.0)
                xdth = jnp.where(mine, xdt, 0.0)
                dhh = jnp.where(mine, dh, 0.0)
                hph = jnp.where(mine, hprev, 0.0)
                m = g * decay
                dm = _mm_nt(dyh, xdth)
                dg = dm * decay
                w = dm * m
                bdec = bm * dec_s
                dxdt = _mm_tn(m, dyh) + _mm(bdec, dhh)
                dc_off = _mm_nt(dyh, hph) * e_cs
                db_s = _mm_nt(xdth, dhh) * dec_s
                dc_h = _mm(dg, bm) + dc_off
                db_h = _mm_tn(dg, cm) + db_s
                r_s = jnp.sum(db_s * bm, axis=1, keepdims=True)
                dtotal = jnp.sum(r_s, axis=0, keepdims=True) + e_tot * jnp.sum(
                    jnp.sum(dhh * hph, axis=1, keepdims=True), axis=0, keepdims=True)
                dcs = (jnp.sum(w, axis=1, keepdims=True) - _to_col(jnp.sum(w, axis=0, keepdims=True), eye)
                       + jnp.sum(dc_off * cm, axis=1, keepdims=True) - r_s + jnp.where(last_row, dtotal, 0.0))
                da_col = jnp.sum(jnp.where(triu, _to_row(dcs, eye), 0.0), axis=1, keepdims=True)
                ddt = da_col * a_heads[h] + jnp.sum(jnp.where(mine, dxdt * xv, 0.0), axis=1, keepdims=True)
                ddt_mat = ddt_mat + jnp.where(lane == 2 * j + h, ddt, 0.0)
                da_acc = da_acc + jnp.where(lane == 2 * j + h, jnp.sum(da_col * dts[h], axis=0, keepdims=True), 0.0)
                dxs.append(dxdt * dts[h])
                dhs.append(e_tot * dhh + _mm_tn(cm * e_cs, dyh))
                db_sum = db_h if db_sum is None else db_sum + db_h
                dc_sum = dc_h if dc_sum is None else dc_sum + dc_h
            dx_ref[rows, :] = jnp.where(first, dxs[0], dxs[1])
            dbc_ref[0, rows, :] += db_sum
            dbc_ref[1, rows, :] += dc_sum
            ddt_ref[rows, :] += ddt_mat * jax.nn.sigmoid(dtr)
            da_ref[...] += da_acc * a_row
            return jnp.where(first, dhs[0], dhs[1])

        _unrolled_loop(nc, chunk, jnp.zeros((STATE_N, LANES), f32))

    blk = lambda f: pl.BlockSpec((t, LANES), f)
    return _call(
        body, name="ssd_bwd", grid=(2, 4),
        out_shape=(jax.ShapeDtypeStruct((t, D), f32), jax.ShapeDtypeStruct((2, t, 2 * STATE_N), f32),
                   jax.ShapeDtypeStruct((t, LANES), f32), jax.ShapeDtypeStruct((1, LANES), f32)),
        in_specs=[blk(lambda g, p: (0, 4 * g + p)), blk(lambda g, p: (0, cb + g)), blk(lambda g, p: (0, cc + g)),
                  blk(lambda g, p: (0, cdt)), _row(LANES), _row(LANES),
                  pl.BlockSpec((nc, STATE_N, LANES), lambda g, p: (0, 0, 4 * g + p)), blk(lambda g, p: (0, 4 * g + p))],
        out_specs=(blk(lambda g, p: (0, 4 * g + p)), pl.BlockSpec((2, t, LANES), lambda g, p: (0, 0, g)),
                   blk(lambda g, p: (0, 0)), _row(LANES)),
        sem=("arbitrary", "arbitrary"), args=(xbc_act, xbc_act, xbc_act, proj, dt_bias_row, a_log_row, states, d_y),
        rider=rider)


def _up_bwd(d_up, w_up, x1, mod, norm2_w, dx2, mix, w_out, rider=None):
    t = x1.shape[0]

    def body(dup_ref, wu_ref, x1_ref, mod_ref, nw_ref, dx2_ref, mix_ref, wo_ref,
             dx1_ref, dmix_ref, dys_ref, du_ref, st_ref):
        @pl.when(pl.program_id(0) == 0)
        def _():
            st_ref[...] = jnp.zeros_like(st_ref)

        nt = (((1,), (1,)), ((), ()))
        dh = None
        for k in range(4):
            lo = (k % 2) * UP_SHARD
            part = lax.dot_general(dup_ref[k // 2, :, lo:lo + UP_SHARD], wu_ref[k], nt, preferred_element_type=f32)
            dh = part if dh is None else dh + part
        x1 = x1_ref[...]
        rstd = lax.rsqrt(jnp.mean(x1 * x1, axis=-1, keepdims=True) + 1e-6)
        xh = x1 * rstd
        nw = nw_ref[...]
        sc = 1.0 + mod_ref[:, 4 * D:5 * D]
        st_ref[0:1, :] += jnp.sum(dh, axis=0, keepdims=True)
        st_ref[1:2, :] += jnp.sum(dh * xh * nw, axis=0, keepdims=True)
        st_ref[2:3, :] += jnp.sum(dh * sc * xh, axis=0, keepdims=True)
        dxh = dh * sc * nw
        dx1 = dx2_ref[...] + rstd * (dxh - xh * jnp.mean(dxh * xh, axis=-1, keepdims=True))
        dx1_ref[...] = dx1
        st_ref[3:4, :] += jnp.sum(dx1 * mix_ref[...], axis=0, keepdims=True)
        dmix = (mod_ref[:, 2 * D:3 * D] * dx1).astype(MX)
        dmix_ref[...] = dmix
        dys_ref[...] = lax.dot_general(dmix, wo_ref[0:D, :], nt, preferred_element_type=f32)
        du_ref[...] = lax.dot_general(dmix, wo_ref[D:2 * D, :], nt, preferred_element_type=f32)

    blk = pl.BlockSpec((TM, D), lambda i: (i, 0))
    return _call(
        body, name="up_bwd", grid=(t // TM,),
        out_shape=(jax.ShapeDtypeStruct((t, D), f32), jax.ShapeDtypeStruct((t, D), MX),
                   jax.ShapeDtypeStruct((t, D), f32), jax.ShapeDtypeStruct((t, D), f32),
                   jax.ShapeDtypeStruct((8, D), f32)),
        in_specs=[pl.BlockSpec((2, TM, D_FF), lambda i: (0, i, 0)), _resident((4, D, UP_SHARD)), blk, _row(6 * D), _row(),
                  blk, blk, _resident((2 * D, D))],
        out_specs=(blk, blk, blk, blk, pl.BlockSpec((8, D), lambda i: (0, 0))),
        sem=("arbitrary",), args=(d_up, w_up, x1, mod, norm2_w, dx2, mix, w_out), rider=rider)


def _ln_silu_bwd(d_u, u_conv, ln_w, ln_b):
    t = d_u.shape[0]

    def body(du_ref, u_ref, w_ref, b_ref, o_ref, st_ref):
        @pl.when(pl.program_id(0) == 0)
        def _():
            st_ref[...] = jnp.zeros_like(st_ref)

        u = u_ref[...]
        mu = jnp.mean(u, axis=-1, keepdims=True)
        uc = u - mu
        rstd = lax.rsqrt(jnp.mean(uc * uc, axis=-1, keepdims=True) + 1e-5)
        n = uc * rstd
        w = w_ref[...]
        dl = du_ref[...] * _dsilu(n * w + b_ref[...])
        st_ref[0:1, :] += jnp.sum(dl * n, axis=0, keepdims=True)
        st_ref[1:2, :] += jnp.sum(dl, axis=0, keepdims=True)
        dn = dl * w
        o_ref[...] = rstd * (dn - jnp.mean(dn, axis=-1, keepdims=True) - n * jnp.mean(dn * n, axis=-1, keepdims=True))

    blk = pl.BlockSpec((TM, D), lambda i: (i, 0))
    return pl.pallas_call(
        body, name="ln_silu_bwd", grid=(t // TM,),
        out_shape=(jax.ShapeDtypeStruct((t, D), f32), jax.ShapeDtypeStruct((8, D), f32)),
        in_specs=[blk, blk, _row(), _row()], out_specs=(blk, pl.BlockSpec((8, D), lambda i: (0, 0))),
        compiler_params=_cp("arbitrary"),
    )(d_u, u_conv, ln_w, ln_b)


def _ssd_gate_norm_bwd(d_out, y_scan, xbc_act, proj, d_skip_row, ssd_norm_w):
    t = d_out.shape[0]

    def body(do_ref, y_ref, xs_ref, z_ref, dsk_ref, nw_ref, dy_ref, dz_ref, st_ref):
        @pl.when(pl.program_id(0) == 0)
        def _():
            st_ref[...] = jnp.zeros_like(st_ref)

        xs = xs_ref[...]
        y = y_ref[...] + xs * dsk_ref[...]
        z = z_ref[...]
        s = _silu(z)
        yz = y * s
        rstd = lax.rsqrt(jnp.mean(yz * yz, axis=-1, keepdims=True) + 1e-6)
        n = yz * rstd
        do = do_ref[...]
        st_ref[0:1, :] += jnp.sum(do * n, axis=0, keepdims=True)
        dn = do * nw_ref[...]
        dyz = rstd * (dn - n * jnp.mean(dn * n, axis=-1, keepdims=True))
        dy = dyz * s
        dy_ref[...] = dy
        dz_ref[...] = (dyz * y * _dsilu(z)).astype(MX)
        st_ref[1:2, :] += jnp.sum(dy * xs, axis=0, keepdims=True)

    blk = pl.BlockSpec((TM, D), lambda i: (i, 0))
    return pl.pallas_call(
        body, name="ssd_gate_norm_bwd", grid=(t // TM,),
        out_shape=(jax.ShapeDtypeStruct((t, D), f32), jax.ShapeDtypeStruct((t, D), MX), jax.ShapeDtypeStruct((8, D), f32)),
        in_specs=[blk, blk, blk, blk, _row(), _row()], out_specs=(blk, blk, pl.BlockSpec((8, D), lambda i: (0, 0))),
        compiler_params=_cp("arbitrary"),
    )(d_out, y_scan, xbc_act, proj, d_skip_row, ssd_norm_w)


def _inproj_bwd(d_z, d_xraw, d_bcraw, d_conf, d_dt, w_pack, x, mod, norm1_w, dx1, rider=None):
    t = x.shape[0]

    def body(dz_ref, dx_ref, dbc_ref, dcf_ref, ddt_ref, w_ref, x_ref, mod_ref, nw_ref, dx1_ref, gx_ref, st_ref):
        @pl.when(pl.program_id(0) == 0)
        def _():
            st_ref[...] = jnp.zeros_like(st_ref)

        nt = (((1,), (1,)), ((), ()))
        dot = lambda a, lo, hi: lax.dot_general(a, w_ref[:, lo:hi], nt, preferred_element_type=f32)
        dh = dot(dz_ref[...], OFF_Z, OFF_Z + D)
        dh = dh + dot(dx_ref[...], OFF_XBC, OFF_XBC + D)
        dh = dh + dot(dbc_ref[...], OFF_XBC + D, OFF_XBC + D_XBC)
        dh = dh + dot(dcf_ref[0], OFF_CA, OFF_CA + D)
        dh = dh + dot(dcf_ref[1], OFF_CG, OFF_CG + D)
        dh = dh + dot(ddt_ref[...].astype(MX), OFF_DT, OFF_DT + LANES)
        st_ref[3:4, 0:LANES] += jnp.sum(ddt_ref[...], axis=0, keepdims=True)
        xv = x_ref[...]
        rstd = lax.rsqrt(jnp.mean(xv * xv, axis=-1, keepdims=True) + 1e-6)
        xh = xv * rstd
        nw = nw_ref[...]
        sc = 1.0 + mod_ref[:, D:2 * D]
        st_ref[0:1, :] += jnp.sum(dh, axis=0, keepdims=True)
        st_ref[1:2, :] += jnp.sum(dh * xh * nw, axis=0, keepdims=True)
        st_ref[2:3, :] += jnp.sum(dh * sc * xh, axis=0, keepdims=True)
        dxh = dh * sc * nw
        gx_ref[...] = dx1_ref[...] + rstd * (dxh - xh * jnp.mean(dxh * xh, axis=-1, keepdims=True))

    blk = pl.BlockSpec((TM, D), lambda i: (i, 0))
    return _call(
        body, name="inproj_bwd", grid=(t // TM,),
        out_shape=(jax.ShapeDtypeStruct((t, D), f32), jax.ShapeDtypeStruct((8, D), f32)),
        in_specs=[blk, blk, pl.BlockSpec((TM, 2 * CW), lambda i: (i, 0)), pl.BlockSpec((2, TM, D), lambda i: (0, i, 0)),
                  pl.BlockSpec((TM, LANES), lambda i: (i, 0)), _resident((D, W_PACK)), blk, _row(6 * D), _row(), blk],
        out_specs=(blk, pl.BlockSpec((8, D), lambda i: (0, 0))),
        sem=("arbitrary",), args=(d_z, d_xraw, d_bcraw, d_conf, d_dt, w_pack, x, mod, norm1_w, dx1), rider=rider)


def _wgrad(at, d, name, bn=256):
    k, t = at.shape
    n = d.shape[1]
    out_dtype = MX

    def body(a_ref, d_ref, o_ref):
        o_ref[...] = jnp.dot(a_ref[...], d_ref[...].astype(MX), preferred_element_type=f32).astype(out_dtype)

    return pl.pallas_call(
        body, name=name, grid=(n // bn,), out_shape=jax.ShapeDtypeStruct((k, n), out_dtype),
        in_specs=[_resident((k, t)), pl.BlockSpec((t, bn), lambda j: (0, j))],
        out_specs=pl.BlockSpec((k, bn), lambda j: (0, j)), compiler_params=_cp("arbitrary"),
    )(at, d)


def _wgrad_stacked(at, d, name, bn):
    out_dtype = MX
    k, t = at.shape
    s, _, n = d.shape
    nb = n // bn

    def body(a_ref, d_ref, o_ref):
        o_ref[0] = jnp.dot(a_ref[...], d_ref[0], preferred_element_type=f32).astype(out_dtype)

    return pl.pallas_call(
        body, name=name, grid=(s, nb), out_shape=jax.ShapeDtypeStruct((s * nb, k, bn), out_dtype),
        in_specs=[_resident((k, t)), pl.BlockSpec((1, t, bn), lambda i, j: (i, 0, j))],
        out_specs=pl.BlockSpec((1, k, bn), lambda i, j: (i * nb + j, 0, 0)), compiler_params=_cp("arbitrary", "arbitrary"),
    )(at, d)


def _pad_row(v, width=LANES):
    return jnp.pad(v.reshape(1, -1), ((0, 0), (0, width - v.size)))


def _quarters(a):
    return a.reshape(4, 2, a.shape[0] // 8, a.shape[1])


def _local_step(x, mod, target, w_pack, late, small, reducer=None):
    dtb_row, alog_row = _pad_row(small["dt_bias"]), _pad_row(small["a_log"])
    dskip_row = jnp.repeat(small["d_skip"].reshape(-1), HEAD_P).reshape(1, D)

    red = reducer

    def hosted(host, args, swap=None, scatter=None, gather=None, sums=()):
        if red is None:
            return host(*args)[0]
        riders = ([red.scatter(scatter)] if scatter else []) + ([red.swap(*swap)] if swap else [])
        riders += [_SwapSumsRider([red.sums[n] for n in sums])] if sums else []
        riders += [_GatherRider([gather[0]], *gather[1:])] if gather is not None else []
        both = _Riders(riders)
        outs, extra = host(*args, rider=both)
        extra = both.split(extra)
        if scatter:
            red.scattered(scatter, extra.pop(0))
        if swap:
            red.swapped(swap[0], extra.pop(0))
        if sums:
            red.others.update(zip(sums, extra.pop(0)))
        return (outs, extra[0][0]) if gather is not None else outs

    w_out, w_up, w_down = late
    if red is None:
        proj, h_t = hosted(_ln_inproj, (x, mod, small["norm1_w"], w_pack))
        xbc_act, = hosted(_ssd_conv_fwd, (proj, small["ssd_conv_w"], small["ssd_conv_b"]))
        y_scan, states = hosted(_ssd_fwd, (xbc_act, proj, dtb_row, alog_row))
        u_conv, = hosted(_glu_conv_fwd, (proj, small["conf_conv_w"], small["conf_conv_b"]))
    else:
        (proj, h_t), w_out = hosted(_ln_inproj, (x, mod, small["norm1_w"], w_pack), gather=(w_out,))
        (xbc_act,), w_up = hosted(_ssd_conv_fwd, (proj, small["ssd_conv_w"], small["ssd_conv_b"]), gather=(w_up, 0, UP_EARLY_ROWS))
        (y_scan, states), w_up = hosted(_ssd_fwd, (xbc_act, proj, dtb_row, alog_row), gather=(w_up, UP_EARLY_ROWS, None))
        (u_conv,), w_down = hosted(_glu_conv_fwd, (proj, small["conf_conv_w"], small["conf_conv_b"]), gather=(w_down,))
        w_out, w_up, w_down = w_out.reshape(2 * D, D), w_up.reshape(4, D, UP_SHARD), w_down.reshape(D_FF, D)
    y_ssd, y_ssd_t = _ssd_gate_norm(y_scan, xbc_act, proj, dskip_row, small["ssd_norm_w"])
    u, u_t = _ln_silu(u_conv, small["conf_ln_w"], small["conf_ln_b"])
    mix, x1, h2_t, up = _outproj_ln2_up(y_ssd, u, w_out, x, mod, small["norm2_w"], w_up)
    act, act_t = _ffn_conv_fwd(up, small["ffn_conv_w"], small["ffn_conv_b"])[0]
    dx2, d_ffn, d_act, st_down = _down_loss(act, w_down, x1, mod, small["final_norm_w"], target)

    g_down = _quarters(_wgrad(act_t, d_ffn, "wgrad_down"))
    d_up, dw_ffn, db_ffn = hosted(_ffn_conv_bwd, (up, small["ffn_conv_w"], small["ffn_conv_b"], d_act), swap=("w_down", g_down))
    g_up = _wgrad_stacked(h2_t, d_up, "wgrad_up", D_FF // 2).reshape(4, 2, D // 2, UP_SHARD)
    dx1, d_mix, d_yssd, d_u, st_up = hosted(_up_bwd, (d_up, w_up, x1, mod, small["norm2_w"], dx2, mix, w_out),
                                            scatter="w_down", swap=("w_up", g_up))
    g_out = _quarters(jnp.concatenate([_wgrad(y_ssd_t, d_mix, "wgrad_out_y"), _wgrad(u_t, d_mix, "wgrad_out_u")], axis=0))
    d_uconv, st_ln = _ln_silu_bwd(d_u, u_conv, small["conf_ln_w"], small["conf_ln_b"])
    d_conf, dw_conf, db_conf = hosted(_glu_conv_bwd, (proj, small["conf_conv_w"], d_uconv), scatter="w_up",
                                      swap=("w_out", g_out))
    d_y, d_z, st_gn = _ssd_gate_norm_bwd(d_yssd, y_scan, xbc_act, proj, dskip_row, small["ssd_norm_w"])
    d_xs, d_bc, d_dt, d_alog = hosted(_ssd_bwd, (xbc_act, proj, dtb_row, alog_row, states, d_y), scatter="w_out")
    d_xraw, dw_sx, db_sx = _ssd_conv_bwd_x(proj, small["ssd_conv_w"], small["ssd_conv_b"], d_xs, d_y, dskip_row)
    d_bcraw, dw_sbc, db_sbc = _ssd_conv_bwd_bc(proj, small["ssd_conv_w"], small["ssd_conv_b"], d_bc)
    g_in = _unpack_g_in(dict(
        z=_wgrad(h_t, d_z, "wgrad_in_z"), x=_wgrad(h_t, d_xraw, "wgrad_in_x"), bc=_wgrad(h_t, d_bcraw, "wgrad_in_bc"),
        conf=_wgrad_stacked(h_t, d_conf, "wgrad_in_conf", D), dt=_wgrad(h_t, d_dt, "wgrad_in_dt", bn=LANES)))
    g_in = g_in.reshape(4, 2, D // 2, W_IN_SHARD_PAD)
    grad_x, st_in = hosted(_inproj_bwd, (d_z, d_xraw, d_bcraw, d_conf, d_dt, w_pack, x, mod, small["norm1_w"], dx1),
                           swap=("w_in", g_in), sums=("w_out", "w_up", "w_down"))

    gsmall = _pack_small_grads(st_in, st_up, st_down, st_ln, st_gn, d_alog, dw_sx, dw_sbc, db_sx, db_sbc, dw_conf, db_conf,
                               dw_ffn, db_ffn)
    gbig = None if reducer is not None else dict(w_in=g_in, w_out=g_out, w_up=g_up, w_down=g_down)
    return st_down[2, 0], grad_x, gbig, gsmall


VECTORS = ("ada_b", "norm1_w", "ssd_conv_b", "dt_bias", "a_log", "d_skip", "ssd_norm_w", "conf_conv_b", "conf_ln_w",
           "conf_ln_b", "norm2_w", "ffn_conv_b", "final_norm_w")
VECTOR_SIZES = (6 * D, D, D_XBC, HEADS, HEADS, HEADS, D, D, D, D, D, 2 * D_FF, D)
CONVS = {"ssd_conv_w": (K_SSD, D_XBC), "conf_conv_w": (K_CONF, D), "ffn_conv_w": (K_FFN, 2 * D_FF)}


def _pack_rows(items):
    n = -(-sum(w for _, w in items) // (8 * LANES)) * LANES
    while True:
        fill, place = [0] * 8, {}
        for key, w in sorted(items, key=lambda kv: -kv[1]):
            rows = [r for r in range(8) if fill[r] + w <= n]
            if not rows:
                break
            place[key] = (rows[0], fill[rows[0]])
            fill[rows[0]] += w
        if len(place) == len(items):
            return n, place
        n += LANES


FRONT_N, FRONT = _pack_rows([("c", D)] + [((nm, j), cols // 4) for nm, (taps, cols) in CONVS.items() for j in range(taps)])
BACK_N, BACK = _pack_rows([(nm, -(-sz // LANES) * LANES) for nm, sz in zip(VECTORS, VECTOR_SIZES)]
                          + [((nm, j), cols) for nm, (taps, cols) in CONVS.items() for j in range(taps)] + [("loss", LANES)])
_VM = pltpu.CompilerParams(vmem_limit_bytes=VMEM_LIMIT)


def _pack_front(c, shards):
    def body(c_ref, *refs):
        o_ref = refs[-1]
        o_ref[...] = jnp.zeros_like(o_ref)
        r, o = FRONT["c"]
        o_ref[r:r + 1, o:o + D] = c_ref[...]
        for ref, (nm, (taps, cols)) in zip(refs, CONVS.items()):
            for j in range(taps):
                r, o = FRONT[(nm, j)]
                o_ref[r:r + 1, o:o + cols // 4] = ref[0, j:j + 1, :]

    return pl.pallas_call(body, name="pack_front", out_shape=jax.ShapeDtypeStruct((8, FRONT_N), f32),
                          compiler_params=_VM)(c, *shards)


def _unpack_front(got):
    def body(g_ref, c_ref, *outs):
        r, o = FRONT["c"]
        for d in range(8):
            c_ref[d:d + 1, :] = g_ref[8 * d + r:8 * d + r + 1, o:o + D]
        for ref, (nm, (taps, cols)) in zip(outs, CONVS.items()):
            cw = cols // 4
            for j in range(taps):
                r, o = FRONT[(nm, j)]
                for k in range(4):
                    ref[j:j + 1, k * cw:(k + 1) * cw] = g_ref[16 * k + r:16 * k + r + 1, o:o + cw]

    return pl.pallas_call(
        body, name="unpack_front", compiler_params=_VM,
        out_shape=(jax.ShapeDtypeStruct((8, D), f32),) + tuple(jax.ShapeDtypeStruct(tc, f32) for tc in CONVS.values()),
    )(got)


def _pack_small_grads(st_in, st_up, st_down, st_ln, st_gn, d_alog, dw_sx, dw_sbc, db_sx, db_sbc, dw_conf, db_conf, dw_ffn,
                      db_ffn):
    def body(in_ref, up_ref, dn_ref, ln_ref, gn_ref, al_ref, wx_ref, wbc_ref, bx_ref, bbc_ref, wc_ref, bc_ref, wf_ref, bf_ref,
             o_ref):
        def put(key, val, shift=0):
            r, o = BACK[key]
            o_ref[r:r + 1, o + shift:o + shift + val.shape[1]] = val

        o_ref[...] = jnp.zeros_like(o_ref)
        for i, piece in enumerate((in_ref[0:1, :], in_ref[1:2, :], up_ref[3:4, :], up_ref[0:1, :], up_ref[1:2, :],
                                   dn_ref[1:2, :])):
            put("ada_b", piece, i * D)
        put("norm1_w", in_ref[2:3, :])
        put("ssd_conv_b", bx_ref[...])
        put("ssd_conv_b", bbc_ref[...], D)
        put("dt_bias", in_ref[3:4, 0:LANES])
        put("a_log", al_ref[...])
        lane = lax.broadcasted_iota(jnp.int32, (1, LANES), 1)
        col = lax.broadcasted_iota(jnp.int32, (1, D), 1)
        per_col = gn_ref[1:2, :]
        d_skip = jnp.zeros((1, LANES), f32)
        for h in range(HEADS):
            in_head = jnp.logical_and(col >= h * HEAD_P, col < (h + 1) * HEAD_P)
            s = jnp.sum(jnp.where(in_head, per_col, 0.0), axis=1, keepdims=True)
            d_skip = d_skip + jnp.where(lane == h, s, 0.0)
        put("d_skip", d_skip)
        put("ssd_norm_w", gn_ref[0:1, :])
        put("conf_conv_b", bc_ref[...])
        put("conf_ln_w", ln_ref[0:1, :])
        put("conf_ln_b", ln_ref[1:2, :])
        put("norm2_w", up_ref[2:3, :])
        put("ffn_conv_b", bf_ref[0])
        put("ffn_conv_b", bf_ref[1], D_FF)
        put("final_norm_w", dn_ref[0:1, :])
        put("loss", dn_ref[2:3, 0:LANES])
        for j in range(K_SSD):
            put(("ssd_conv_w", j), wx_ref[j:j + 1, :])
            put(("ssd_conv_w", j), wbc_ref[j:j + 1, :], D)
        for j in range(K_CONF):
            put(("conf_conv_w", j), wc_ref[j:j + 1, :])
        for j in range(K_FFN):
            put(("ffn_conv_w", j), wf_ref[0, j:j + 1, :])
            put(("ffn_conv_w", j), wf_ref[1, j:j + 1, :], D_FF)

    return pl.pallas_call(body, name="pack_small_grads", out_shape=jax.ShapeDtypeStruct((8, BACK_N), f32), compiler_params=_VM)(
        st_in, st_up, st_down, st_ln, st_gn, d_alog, dw_sx, dw_sbc, db_sx, db_sbc, dw_conf, db_conf, dw_ffn, db_ffn)


def _small_adamw(got, chip, w, m, v):
    names = VECTORS + tuple(CONVS)
    n_par = len(names)

    def body(chip_ref, g_ref, *refs):
        ins, outs = refs[:3 * n_par], refs[3 * n_par:]
        dm_ref, loss_ref, outs = outs[0], outs[1], outs[2:]
        chip_id = chip_ref[0]

        def summed(key, width):
            r, o = BACK[key]
            s = g_ref[r:r + 1, o:o + width]
            for d in range(1, 8):
                s = s + g_ref[8 * d + r:8 * d + r + 1, o:o + width]
            return s

        def mine(full, cw):
            out = full[:, 0:cw]
            for k in range(1, 4):
                out = jnp.where(chip_id == k, full[:, k * cw:(k + 1) * cw], out)
            return out

        r, o = BACK["ada_b"]
        for d in range(8):
            dm_ref[d:d + 1, :] = mine(g_ref[8 * d + r:8 * d + r + 1, o:o + 6 * D], 6 * D // 4)
        loss_ref[...] = summed("loss", LANES)
        for i, (nm, size) in enumerate(zip(VECTORS, VECTOR_SIZES)):
            g = summed(nm, -(-size // LANES) * LANES)[:, 0:size]
            res = _adam_math(ins[3 * i][...], g, ins[3 * i + 1][...], ins[3 * i + 2][...])
            for ref, val in zip(outs[4 * i:4 * i + 4], (g,) + res):
                ref[...] = val
        for i, (nm, (taps, cols)) in enumerate(CONVS.items(), start=len(VECTORS)):
            for j in range(taps):
                g = mine(summed((nm, j), cols), cols // 4)
                res = _adam_math(ins[3 * i][0, j:j + 1, :], g, ins[3 * i + 1][0, j:j + 1, :], ins[3 * i + 2][0, j:j + 1, :])
                for ref, val in zip(outs[4 * i:4 * i + 4], (g,) + res):
                    ref[0, j:j + 1, :] = val

    params = [a[nm] for nm in names for a in (w, m, v)]
    whole = lambda s: pl.BlockSpec(s, lambda i, chip, nd=len(s): (0,) * nd)
    out_shape = [jax.ShapeDtypeStruct((8, 6 * D // 4), f32), jax.ShapeDtypeStruct((1, LANES), f32)]
    out_shape += [jax.ShapeDtypeStruct(w[nm].shape, f32) for nm in names for _ in range(4)]
    outs = pl.pallas_call(
        body, name="small_adamw", out_shape=tuple(out_shape), compiler_params=_VM,
        grid_spec=pltpu.PrefetchScalarGridSpec(
            num_scalar_prefetch=1, grid=(1,), in_specs=[whole(got.shape)] + [whole(p.shape) for p in params],
            out_specs=tuple(whole(s.shape) for s in out_shape)),
    )(_scalar(chip), got, *params)
    return outs[0], outs[1][0, 0], {nm: outs[2 + 4 * i:6 + 4 * i] for i, nm in enumerate(names)}


W_IN_COLS = 4624
W_IN_SHARD = W_IN_COLS // 4
W_IN_SHARD_PAD = 1280
_SEGMENTS = ((0, 1024, OFF_Z), (1024, 2560, OFF_XBC), (2560, 2576, OFF_DT), (2576, 3600, OFF_CA), (3600, 4624, OFF_CG))


def _in_pieces(bounds=()):
    out = []
    for k in range(4):
        s0, s1 = k * W_IN_SHARD, (k + 1) * W_IN_SHARD
        for lo, hi, off in _SEGMENTS:
            a, b = max(lo, s0), min(hi, s1)
            while a < b:
                p = off + a - lo
                e = min([b - a] + [c - p for c in bounds if c > p])
                out.append((k, a - s0, p, e))
                a += e
    return out


def _pack_w_in(shards):
    pieces = _in_pieces()

    def body(s_ref, o_ref):
        o_ref[:, OFF_DT:W_PACK] = jnp.zeros((TM, W_PACK - OFF_DT), MX)
        for k, c, p, n in pieces:
            o_ref[:, p:p + n] = s_ref[k, :, c:c + n]

    return pl.pallas_call(
        body, name="pack_w_in", grid=(D // TM,), out_shape=jax.ShapeDtypeStruct((D, W_PACK), MX),
        in_specs=[pl.BlockSpec((4, TM, W_IN_SHARD_PAD), lambda i: (0, i, 0))],
        out_specs=pl.BlockSpec((TM, W_PACK), lambda i: (i, 0)), compiler_params=_cp("arbitrary"),
    )(shards)


def _unpack_g_in(g):
    srcs = ((OFF_Z, D), (OFF_XBC, D), (OFF_XBC + D, 2 * CW), (OFF_CA, D), (OFF_CG, D), (OFF_DT, LANES))
    pieces = _in_pieces(tuple(o for o, _ in srcs) + tuple(o + n for o, n in srcs))

    def body(z_ref, x_ref, bc_ref, cf_ref, dt_ref, o_ref):
        read = (lambda lo, hi: z_ref[:, lo:hi], lambda lo, hi: x_ref[:, lo:hi], lambda lo, hi: bc_ref[:, lo:hi],
                lambda lo, hi: cf_ref[0, :, lo:hi], lambda lo, hi: cf_ref[1, :, lo:hi], lambda lo, hi: dt_ref[:, lo:hi])
        o_ref[:, :, W_IN_SHARD - 4:W_IN_SHARD_PAD] = jnp.zeros((4, TM, W_IN_SHARD_PAD - W_IN_SHARD + 4), MX)
        for k, c, p, n in pieces:
            i = [q for q, (o, w) in enumerate(srcs) if o <= p < o + w][0]
            o_ref[k, :, c:c + n] = read[i](p - srcs[i][0], p - srcs[i][0] + n)

    blk = lambda w: pl.BlockSpec((TM, w), lambda i: (i, 0))
    return pl.pallas_call(
        body, name="unpack_g_in", grid=(D // TM,), out_shape=jax.ShapeDtypeStruct((4, D, W_IN_SHARD_PAD), MX),
        in_specs=[blk(D), blk(D), blk(2 * CW), pl.BlockSpec((2, TM, D), lambda i: (0, i, 0)), blk(LANES)],
        out_specs=pl.BlockSpec((4, TM, W_IN_SHARD_PAD), lambda i: (0, i, 0)), compiler_params=_cp("arbitrary"),
    )(g["z"], g["x"], g["bc"], g["conf"], g["dt"])


def _scalar(v):
    return jnp.reshape(v, (1,)).astype(jnp.int32)


def _cast_into_slot(w, width, chip):
    r, c = w.shape
    h = r // 2
    tm = _row_tile(h)
    nj = h // tm

    def body(chip_ref, w_ref, o_ref):
        v = w_ref[...].astype(MX)
        o_ref[0, 0] = v if width == c else jnp.concatenate([v, jnp.zeros((tm, width - c), MX)], axis=1)

    return pl.pallas_call(
        body, name=f"cast_into_slot_{r}x{c}", out_shape=jax.ShapeDtypeStruct((4, 2, h, width), MX),
        grid_spec=pltpu.PrefetchScalarGridSpec(
            num_scalar_prefetch=1, grid=(2, nj),
            in_specs=[pl.BlockSpec((tm, c), lambda i, j, chip: (i * nj + j, 0))],
            out_specs=pl.BlockSpec((1, 1, tm, width), lambda i, j, chip: (chip[0], i, j, 0))),
        compiler_params=_cp("arbitrary", "arbitrary"),
    )(_scalar(chip), w)


def _columns_first(w):
    return jnp.transpose(w, (2, 0, 1))


def _cast_into_slot_w_in(w_t, chip):
    h = D // 2
    nj = h // TM
    pad = W_IN_SHARD_PAD - W_IN_SHARD

    def body(chip_ref, w_ref, o_ref):
        cols = jnp.concatenate([w_ref[:, 0, :], jnp.zeros((pad, TM), f32)], axis=0)
        o_ref[0, 0] = cols.T.astype(MX)

    return pl.pallas_call(
        body, name="cast_into_slot_w_in", out_shape=jax.ShapeDtypeStruct((4, 2, h, W_IN_SHARD_PAD), MX),
        grid_spec=pltpu.PrefetchScalarGridSpec(
            num_scalar_prefetch=1, grid=(2, nj),
            in_specs=[pl.BlockSpec((W_IN_SHARD, 1, TM), lambda i, j, chip: (0, 0, i * nj + j))],
            out_specs=pl.BlockSpec((1, 1, TM, W_IN_SHARD_PAD), lambda i, j, chip: (chip[0], i, j, 0))),
        compiler_params=_cp("arbitrary", "arbitrary"),
    )(_scalar(chip), w_t)


def _adamw_w_in(w_t, mine, other, m_t, v_t, core):
    h = D // 2
    nj = h // TM

    def body(core_ref, w_ref, a_ref, b_ref, m_ref, v_ref, g_ref, d_ref, nm_ref, nv_ref):
        g = jnp.where(pl.program_id(0) == core_ref[0], a_ref[...], b_ref[...]).T[0:W_IN_SHARD, :]
        g_ref[:, 0, :] = g
        d_ref[:, 0, :], nm_ref[:, 0, :], nv_ref[:, 0, :] = _adam_math(w_ref[:, 0, :], g, m_ref[:, 0, :], v_ref[:, 0, :])

    blk = pl.BlockSpec((W_IN_SHARD, 1, TM), lambda i, j, core: (0, 0, i * nj + j))
    gblk = pl.BlockSpec((TM, W_IN_SHARD_PAD), lambda i, j, core: (j, 0))
    return pl.pallas_call(
        body, name="adamw_w_in", out_shape=tuple([jax.ShapeDtypeStruct((W_IN_SHARD, 1, D), f32)] * 4),
        grid_spec=pltpu.PrefetchScalarGridSpec(
            num_scalar_prefetch=1, grid=(2, nj), in_specs=[blk, gblk, gblk, blk, blk], out_specs=(blk,) * 4),
        compiler_params=_cp("arbitrary", "arbitrary"),
    )(_scalar(core), w_t, mine, other, m_t, v_t)


ANY = pl.BlockSpec(memory_space=pl.ANY)


def _place():
    x, y, c = lax.axis_index("x"), lax.axis_index("y"), lax.axis_index("c")
    return x, y, c, [(1 - x, y), (x, 1 - y), (1 - x, 1 - y)]


def _gather_rows(block, rider=None):
    m_per, n = block.shape
    ri, ro = (len(rider.inputs), len(rider.out_shape)) if rider is not None else (0, 0)

    def body(x_ref, *refs):
        r_in, out_ref, r_out = refs[:ri], refs[ri], refs[ri + 1:ri + 1 + ro]
        send_sems, recv_sems, local_sem, *r_scr = refs[ri + 1 + ro:]
        if rider is not None:
            rider.start(r_in, r_out, r_scr)
        x, y, c, chips = _place()
        me, sibling = (x, y, c), (x, y, 1 - c)

        def rows(px, py, pc):
            return out_ref.at[pl.ds((4 * px + 2 * py + pc) * m_per, m_per), :]

        def copy(k, blk, to, src=None):
            return pltpu.make_async_remote_copy(
                src_ref=rows(*blk) if src is None else src, dst_ref=rows(*blk), send_sem=send_sems.at[k],
                recv_sem=recv_sems.at[k], device_id=to, device_id_type=MESH)

        mine = pltpu.make_async_copy(x_ref, rows(*me), local_sem)
        mine.start()
        first = [copy(0, me, sibling, src=x_ref)]
        first += [copy(1 + j, me, (*chip, c), src=x_ref) for j, chip in enumerate(chips)]
        for cp in first:
            cp.start()
        passed = [copy(4 + j, (*chip, c), sibling) for j, chip in enumerate(chips)]
        for j, chip in enumerate(chips):
            copy(1 + j, (*chip, c), me).wait_recv()
            passed[j].start()
        copy(0, sibling, me).wait_recv()
        for j, chip in enumerate(chips):
            copy(4 + j, (*chip, 1 - c), me).wait_recv()
        for cp in first + passed:
            cp.wait_send()
        mine.wait()
        if rider is not None:
            rider.finish(r_in, r_out, r_scr)

    vmem = pl.BlockSpec(memory_space=pltpu.VMEM)
    gathered = jax.ShapeDtypeStruct((8 * m_per, n), block.dtype)
    if rider is None:
        return pl.pallas_call(
            body, name=f"gather_rows_{m_per}x{n}", out_shape=gathered, in_specs=[vmem], out_specs=vmem,
            scratch_shapes=[pltpu.SemaphoreType.DMA((7,)), pltpu.SemaphoreType.DMA((7,)), pltpu.SemaphoreType.DMA],
            compiler_params=_VM)(block)
    outs = pl.pallas_call(
        body, name=f"gather_rows_{m_per}x{n}", out_shape=(gathered,) + tuple(rider.out_shape),
        in_specs=[vmem] + [ANY] * ri, out_specs=(vmem,) + (ANY,) * ro,
        input_output_aliases={1 + i: 1 + j for i, j in rider.aliases.items()},
        scratch_shapes=[pltpu.SemaphoreType.DMA((7,)), pltpu.SemaphoreType.DMA((7,)), pltpu.SemaphoreType.DMA] + list(rider.scratch),
        compiler_params=_VM)(block, *rider.inputs)
    return outs[0], tuple(outs[1:])


class _GatherRider:
    def __init__(self, slots, row0=0, nrows=None):
        n = len(slots)
        self.n = n
        self.rows = (row0, slots[0].shape[2] - row0 if nrows is None else nrows)
        self.inputs = list(slots)
        self.out_shape = [jax.ShapeDtypeStruct(s.shape, s.dtype) for s in slots]
        self.scratch = [pltpu.SemaphoreType.DMA((n, 6)), pltpu.SemaphoreType.DMA((n, 6))]
        self.aliases = {a: a for a in range(n)}

    def _copy(self, outs, sems, a, j, k, half, to):
        dst = outs[a].at[k, half, pl.ds(*self.rows)]
        return pltpu.make_async_remote_copy(src_ref=dst, dst_ref=dst, send_sem=sems[0].at[a, j], recv_sem=sems[1].at[a, j],
                                            device_id=to, device_id_type=MESH)

    def _first(self, outs, sems):
        x, y, c, chips = _place()
        return [self._copy(outs, sems, a, j, 2 * x + y, c, (*chip, c)) for a in range(self.n) for j, chip in enumerate(chips)]

    def start(self, ins, outs, sems):
        for cp in self._first(outs, sems):
            cp.start()

    def finish(self, ins, outs, sems):
        x, y, c, chips = _place()
        passed = []
        for a in range(self.n):
            for j, (px, py) in enumerate(chips):
                self._copy(outs, sems, a, j, 2 * px + py, c, (x, y, c)).wait_recv()
                fwd = self._copy(outs, sems, a, 3 + j, 2 * px + py, c, (x, y, 1 - c))
                fwd.start()
                passed.append(fwd)
        for a in range(self.n):
            for j, (px, py) in enumerate(chips):
                self._copy(outs, sems, a, 3 + j, 2 * px + py, 1 - c, (x, y, c)).wait_recv()
        for cp in self._first(outs, sems) + passed:
            cp.wait_send()


class _ScatterRider:
    def __init__(self, parts, row0=0, nrows=None):
        n = len(parts)
        self.n = n
        self.rows = (row0, parts[0].shape[1] - row0 if nrows is None else nrows)
        self.inputs = list(parts)
        self.out_shape = [jax.ShapeDtypeStruct((3, self.rows[1], p.shape[2]), p.dtype) for p in parts]
        self.scratch = [pltpu.SemaphoreType.DMA((n, 3)), pltpu.SemaphoreType.DMA((n, 3))]
        self.aliases = {}

    def _copies(self, ins, outs, sems):
        x, y, c, chips = _place()
        return [pltpu.make_async_remote_copy(
            src_ref=ins[a].at[2 * px + py, pl.ds(*self.rows)], dst_ref=outs[a].at[j], send_sem=sems[0].at[a, j],
            recv_sem=sems[1].at[a, j], device_id=(px, py, c), device_id_type=MESH)
            for a in range(self.n) for j, (px, py) in enumerate(chips)]

    def start(self, ins, outs, sems):
        for cp in self._copies(ins, outs, sems):
            cp.start()

    def finish(self, ins, outs, sems):
        for cp in self._copies(ins, outs, sems):
            cp.wait()


def _ride_alone(rider, name):
    n = len(rider.inputs)

    def body(*refs):
        ins, outs, sems = refs[:n], refs[n:n + len(rider.out_shape)], refs[n + len(rider.out_shape):]
        rider.start(ins, outs, sems)
        rider.finish(ins, outs, sems)

    return pl.pallas_call(
        body, name=name, out_shape=tuple(rider.out_shape), in_specs=[ANY] * n, out_specs=tuple([ANY] * len(rider.out_shape)),
        input_output_aliases=dict(rider.aliases), scratch_shapes=list(rider.scratch),
    )(*rider.inputs)


class _SwapRider:
    def __init__(self, grads):
        n = len(grads)
        self.n = n
        self.inputs = list(grads)
        self.out_shape = [jax.ShapeDtypeStruct((4,) + g.shape[2:], g.dtype) for g in grads]
        self.scratch = [pltpu.SemaphoreType.DMA((n, 4)), pltpu.SemaphoreType.DMA((n, 4))]
        self.aliases = {}

    def _copies(self, ins, outs, sems):
        x, y, c, _ = _place()
        return [pltpu.make_async_remote_copy(
            src_ref=ins[a].at[k, 1 - c], dst_ref=outs[a].at[k], send_sem=sems[0].at[a, k], recv_sem=sems[1].at[a, k],
            device_id=(x, y, 1 - c), device_id_type=MESH) for a in range(self.n) for k in range(4)]

    def start(self, ins, outs, sems):
        for cp in self._copies(ins, outs, sems):
            cp.start()

    def finish(self, ins, outs, sems):
        for cp in self._copies(ins, outs, sems):
            cp.wait()


class _Riders:
    def __init__(self, riders):
        self.riders = list(riders)
        self.inputs = [a for r in riders for a in r.inputs]
        self.out_shape = [s for r in riders for s in r.out_shape]
        self.scratch = [s for r in riders for s in r.scratch]
        self.aliases = {}
        i = o = 0
        for r in riders:
            self.aliases.update({i + a: o + b for a, b in r.aliases.items()})
            i, o = i + len(r.inputs), o + len(r.out_shape)

    def _each(self, ins, outs, sems):
        i = o = s = 0
        for r in self.riders:
            yield r, ins[i:i + len(r.inputs)], outs[o:o + len(r.out_shape)], sems[s:s + len(r.scratch)]
            i, o, s = i + len(r.inputs), o + len(r.out_shape), s + len(r.scratch)

    def start(self, ins, outs, sems):
        for r, a, b, c in self._each(ins, outs, sems):
            r.start(a, b, c)

    def finish(self, ins, outs, sems):
        for r, a, b, c in self._each(ins, outs, sems):
            r.finish(a, b, c)

    def split(self, outs):
        res, o = [], 0
        for r in self.riders:
            res.append(outs[o:o + len(r.out_shape)])
            o += len(r.out_shape)
        return res


class _Reducer:
    def __init__(self, chip, core):
        self.chip, self.core, self.grads, self.parts, self.sums, self.others = chip, core, {}, {}, {}, {}

    def swap(self, name, grad):
        self.grads[name] = grad
        return _SwapRider([grad])

    def swapped(self, name, got):
        self.parts[name] = _add_pair(self.grads[name], got[0], self.core, name)

    def scatter(self, name, row0=0, nrows=None):
        return _ScatterRider([self.parts[name]], row0, nrows)

    def scattered(self, name, others):
        self.sums[name] = _add_chips(self.parts[name], others[0], self.chip, name)


class _SwapSumsRider:
    def __init__(self, halves):
        n = len(halves)
        self.n = n
        self.inputs = list(halves)
        self.out_shape = [jax.ShapeDtypeStruct(s.shape, s.dtype) for s in halves]
        self.scratch = [pltpu.SemaphoreType.DMA((n,)), pltpu.SemaphoreType.DMA((n,))]
        self.aliases = {}

    def _copies(self, ins, outs, sems):
        x, y, c, _ = _place()
        return [pltpu.make_async_remote_copy(
            src_ref=ins[a], dst_ref=outs[a], send_sem=sems[0].at[a], recv_sem=sems[1].at[a],
            device_id=(x, y, 1 - c), device_id_type=MESH) for a in range(self.n)]

    def start(self, ins, outs, sems):
        for cp in self._copies(ins, outs, sems):
            cp.start()

    def finish(self, ins, outs, sems):
        for cp in self._copies(ins, outs, sems):
            cp.wait()


def _row_tile(r):
    for tm in (TM, 176, 128, 64, 32, 16, 8):
        if r % tm == 0:
            return tm
    return r


def _add_pair(mine, got, core, name):
    k, _, h, c = mine.shape
    tm = _row_tile(h)

    def body(core_ref, a_ref, b_ref, o_ref):
        o_ref[0] = (a_ref[0, 0].astype(f32) + b_ref[0].astype(f32)).astype(MX)

    blk = pl.BlockSpec((1, tm, c), lambda i, j, core: (i, j, 0))
    return pl.pallas_call(
        body, name="add_pair_" + name, out_shape=jax.ShapeDtypeStruct((k, h, c), MX),
        grid_spec=pltpu.PrefetchScalarGridSpec(
            num_scalar_prefetch=1, grid=(k, h // tm),
            in_specs=[pl.BlockSpec((1, 1, tm, c), lambda i, j, core: (i, core[0], j, 0)), blk], out_specs=blk),
        compiler_params=_cp("arbitrary", "arbitrary"),
    )(_scalar(core), mine, got)


def _add_chips(parts, others, chip, name, row0=0):
    _, n, c = others.shape
    tm = _row_tile(n)
    assert row0 % tm == 0
    i0 = row0 // tm

    def body(chip_ref, a_ref, b_ref, o_ref):
        s = a_ref[0].astype(f32) + b_ref[0].astype(f32)
        o_ref[...] = (s + b_ref[1].astype(f32)) + b_ref[2].astype(f32)

    return pl.pallas_call(
        body, name="add_chips_" + name, out_shape=jax.ShapeDtypeStruct((n, c), f32),
        grid_spec=pltpu.PrefetchScalarGridSpec(
            num_scalar_prefetch=1, grid=(n // tm,),
            in_specs=[pl.BlockSpec((1, tm, c), lambda i, chip: (chip[0], i0 + i, 0)),
                      pl.BlockSpec((3, tm, c), lambda i, chip: (0, i, 0))],
            out_specs=pl.BlockSpec((tm, c), lambda i, chip: (i, 0))),
        compiler_params=_cp("arbitrary"),
    )(_scalar(chip), parts, others)


def _adam_math(w, g, m, v):
    m = ADAM_B1 * m + (1.0 - ADAM_B1) * g
    v = ADAM_B2 * v + (1.0 - ADAM_B2) * (g * g)
    m_hat = m / (1.0 - ADAM_B1 ** ADAM_STEP)
    v_hat = v / (1.0 - ADAM_B2 ** ADAM_STEP)
    return -ADAM_LR * (m_hat / (jnp.sqrt(v_hat) + ADAM_EPS) + ADAM_WD * w), m, v


def _adamw_halves(w, mine, other, m, v, core, name, rider=None):
    r, c = w.shape
    h = r // 2
    tm = _row_tile(h)
    nj = h // tm
    cg = mine.shape[1]

    def body(core_ref, w_ref, a_ref, b_ref, m_ref, v_ref, g_ref, d_ref, nm_ref, nv_ref):
        g = jnp.where(pl.program_id(0) == core_ref[0], a_ref[:, 0:c], b_ref[:, 0:c])
        g_ref[...] = g
        d_ref[...], nm_ref[...], nv_ref[...] = _adam_math(w_ref[...], g, m_ref[...], v_ref[...])

    blk = pl.BlockSpec((tm, c), lambda i, j, core: (i * nj + j, 0))
    gblk = pl.BlockSpec((tm, cg), lambda i, j, core: (j, 0))
    return _call(body, name=name, grid=(2, nj), out_shape=[jax.ShapeDtypeStruct((r, c), f32)] * 4,
                 in_specs=[blk, gblk, gblk, blk, blk], out_specs=(blk,) * 4, sem=("arbitrary", "arbitrary"),
                 prefetch=(_scalar(core),), args=(w, mine, other, m, v), rider=rider)


def _ada_forward(c_all, ada_w):
    def body(c_ref, w_ref, o_ref):
        o_ref[...] = jnp.dot(_silu(c_ref[...]).astype(MX), w_ref[...].astype(MX), preferred_element_type=f32)

    return pl.pallas_call(body, name="ada_forward", out_shape=jax.ShapeDtypeStruct((8, ada_w.shape[1]), f32),
                          compiler_params=pltpu.CompilerParams(vmem_limit_bytes=VMEM_LIMIT))(c_all, ada_w)


def _ada_adamw(c_all_t, d_mod, w, m, v, rider=None):
    r, c = w.shape
    tm = TM

    def body(ct_ref, dm_ref, w_ref, m_ref, v_ref, g_ref, d_ref, nm_ref, nv_ref):
        ca = _silu(ct_ref[...])
        g = ca[:, 0:1] * dm_ref[0:1, :]
        for b in range(1, 8):
            g = g + ca[:, b:b + 1] * dm_ref[b:b + 1, :]
        g_ref[...] = g
        d_ref[...], nm_ref[...], nv_ref[...] = _adam_math(w_ref[...], g, m_ref[...], v_ref[...])

    blk = pl.BlockSpec((tm, c), lambda i: (i, 0))
    return _call(body, name="ada_adamw", grid=(r // tm,), out_shape=[jax.ShapeDtypeStruct((r, c), f32)] * 4,
                 in_specs=[pl.BlockSpec((tm, 8), lambda i: (i, 0)), pl.BlockSpec((8, c), lambda i: (0, 0)), blk, blk, blk],
                 out_specs=(blk,) * 4, sem=("arbitrary",), args=(c_all_t, d_mod, w, m, v), rider=rider)


WEIGHTS = ("ada_w", "ada_b", "norm1_w", "w_in", "ssd_conv_w", "ssd_conv_b", "dt_bias", "a_log", "d_skip", "ssd_norm_w",
           "conf_conv_w", "conf_conv_b", "conf_ln_w", "conf_ln_b", "w_out", "norm2_w", "w_up", "ffn_conv_w", "ffn_conv_b",
           "w_down", "final_norm_w")


def kernel(x, c, ada_w, ada_b, norm1_w, w_in, ssd_conv_w, ssd_conv_b, dt_bias, a_log, d_skip, ssd_norm_w, conf_conv_w, conf_conv_b, conf_ln_w, conf_ln_b, w_out, norm2_w, w_up, ffn_conv_w, ffn_conv_b, w_down, final_norm_w, loss_target, m_ada_w, m_ada_b, m_norm1_w, m_w_in, m_ssd_conv_w, m_ssd_conv_b, m_dt_bias, m_a_log, m_d_skip, m_ssd_norm_w, m_conf_conv_w, m_conf_conv_b, m_conf_ln_w, m_conf_ln_b, m_w_out, m_norm2_w, m_w_up, m_ffn_conv_w, m_ffn_conv_b, m_w_down, m_final_norm_w, v_ada_w, v_ada_b, v_norm1_w, v_w_in, v_ssd_conv_w, v_ssd_conv_b, v_dt_bias, v_a_log, v_d_skip, v_ssd_norm_w, v_conf_conv_w, v_conf_conv_b, v_conf_ln_w, v_conf_ln_b, v_w_out, v_norm2_w, v_w_up, v_ffn_conv_w, v_ffn_conv_b, v_w_down, v_final_norm_w):
    given = dict(locals())
    w = {n: given[n] for n in WEIGHTS}
    mom = {n: given["m_" + n] for n in WEIGHTS}
    var = {n: given["v_" + n] for n in WEIGHTS}
    chip = 2 * lax.axis_index("x") + lax.axis_index("y")
    me = 2 * chip + lax.axis_index("c")

    core = lax.axis_index("c")
    a_in = _cast_into_slot_w_in(_columns_first(w_in), chip)
    got, (a_in,) = _gather_rows(_pack_front(c, [w[n] for n in CONVS]), _GatherRider([a_in], 0, D // 4))
    c_all, *convs = _unpack_front(got)
    conv_full = dict(zip(CONVS, convs))

    got, (a_in,) = _gather_rows(_ada_forward(c_all, ada_w[0]), _GatherRider([a_in], D // 4, D // 4))
    mod_cols = got.reshape(8, 8, -1)[0::2]
    mod = lax.dynamic_index_in_dim(mod_cols, me, axis=1, keepdims=False).reshape(1, 6 * D) + ada_b
    w_pack = _pack_w_in(a_in.reshape(4, D, W_IN_SHARD_PAD))
    late = (_cast_into_slot(w_out[0], D, chip), _cast_into_slot(w_up[0], UP_SHARD, chip), _cast_into_slot(w_down[0], D, chip))

    flat = lambda a: a.reshape(1, -1) if a.ndim == 1 else a
    small = {n: flat(w[n]) for n in VECTORS if n != "ada_b"}
    small.update(conv_full)
    reducer = _Reducer(chip, core)
    _, grad_x, _, gsmall = _local_step(x[0], mod, loss_target[0], w_pack, late, small, reducer)
    grads, delta, new_m, new_v = {}, {}, {}, {}

    names = VECTORS + tuple(CONVS)
    got, others = _gather_rows(gsmall, reducer.scatter("w_in"))
    reducer.scattered("w_in", others)
    d_mod_mine, loss, res = _small_adamw(got, chip, *[{n: flat(d[n]) for n in names} for d in (w, mom, var)])
    for n in names:
        grads[n], delta[n], new_m[n], new_v[n] = [r.reshape(w[n].shape) for r in res[n]]

    reducer.others["w_in"], = _ride_alone(_SwapSumsRider([reducer.sums["w_in"]]), "swap_sums_w_in")
    res = _adamw_w_in(_columns_first(w_in), reducer.sums["w_in"], reducer.others["w_in"], _columns_first(m_w_in),
                      _columns_first(v_w_in), core)
    grads["w_in"], delta["w_in"], new_m["w_in"], new_v["w_in"] = [jnp.transpose(r, (1, 2, 0)) for r in res]
    for n in ("w_out", "w_up", "w_down"):
        res, _ = _adamw_halves(w[n][0], reducer.sums[n], reducer.others[n], mom[n][0], var[n][0], core, "adamw_" + n)
        grads[n], delta[n], new_m[n], new_v[n] = [r[None] for r in res]
    res, _ = _ada_adamw(c_all.T, d_mod_mine, ada_w[0], m_ada_w[0], v_ada_w[0])
    grads["ada_w"], delta["ada_w"], new_m["ada_w"], new_v["ada_w"] = [r[None] for r in res]

    return (loss, grad_x[None], *[grads[n] for n in WEIGHTS], *[delta[n] for n in WEIGHTS],
            *[new_m[n] for n in WEIGHTS], *[new_v[n] for n in WEIGHTS])
```

```python
import functools

import jax
import jax.numpy as jnp
from jax import lax
from jax.experimental import pallas as pl
from jax.experimental.pallas import tpu as pltpu

f32 = jnp.float32
MX = jnp.bfloat16

D = 1024
HEADS = 16
HEAD_P = 64
STATE_N = 128
D_XBC = 1536
D_FF = 2816
UP_SHARD = 2 * D_FF // 4
UP_EARLY_ROWS = 128
K_SSD, K_CONF, K_FFN = 4, 31, 3
CHUNK = 128
OFF_Z, OFF_XBC, OFF_CA, OFF_CG, OFF_DT = 0, 1024, 2560, 3584, 4608
W_PACK = 4736
TM = 256
CW = 256
RC = 64
LANES = 128
VMEM_LIMIT = 56 * 1024 * 1024

ADAM_LR, ADAM_B1, ADAM_B2, ADAM_EPS, ADAM_WD, ADAM_STEP = 0.001, 0.9, 0.999, 1e-08, 0.01, 10

MESH = pl.DeviceIdType.MESH


def _cp(*sem):
    return pltpu.CompilerParams(dimension_semantics=sem, vmem_limit_bytes=VMEM_LIMIT)


def _resident(shape):
    nd = len(shape)
    return pl.BlockSpec(shape, lambda *_: (0,) * nd, pipeline_mode=pl.Buffered(1))


def _row(width=D):
    return pl.BlockSpec((1, width), lambda *_: (0, 0))


def _call(body, *, name, grid, in_specs, out_specs, out_shape, args, sem, scratch_shapes=(), prefetch=(), rider=None):
    ni, no, ns, npf = len(in_specs), len(out_specs), len(scratch_shapes), len(prefetch)
    ri, ro = (len(rider.inputs), len(rider.out_shape)) if rider is not None else (0, 0)

    def full(*refs):
        pre, refs = refs[:npf], refs[npf:]
        base_in, r_in = refs[:ni], refs[ni:ni + ri]
        base_out, r_out = refs[ni + ri:ni + ri + no], refs[ni + ri + no:ni + ri + no + ro]
        base_scr, r_scr = refs[ni + ri + no + ro:ni + ri + no + ro + ns], refs[ni + ri + no + ro + ns:]
        if rider is None:
            return body(*pre, *base_in, *base_out, *base_scr)
        ids = [pl.program_id(a) for a in range(len(grid))]
        first = functools.reduce(jnp.logical_and, [i == 0 for i in ids])
        last = functools.reduce(jnp.logical_and, [i == g - 1 for i, g in zip(ids, grid)])

        @pl.when(first)
        def _():
            rider.start(r_in, r_out, r_scr)

        body(*pre, *base_in, *base_out, *base_scr)

        @pl.when(last)
        def _():
            rider.finish(r_in, r_out, r_scr)

    extra = dict(shapes=[], scratch=[], aliases={}, inputs=[]) if rider is None else dict(
        shapes=rider.out_shape, scratch=rider.scratch, inputs=rider.inputs,
        aliases={npf + ni + i: no + j for i, j in rider.aliases.items()})
    outs = pl.pallas_call(
        full, name=name, out_shape=tuple(out_shape) + tuple(extra["shapes"]), input_output_aliases=extra["aliases"],
        grid_spec=pltpu.PrefetchScalarGridSpec(
            num_scalar_prefetch=npf, grid=grid, in_specs=list(in_specs) + [ANY] * ri,
            out_specs=tuple(out_specs) + (ANY,) * ro, scratch_shapes=list(scratch_shapes) + list(extra["scratch"])),
        compiler_params=_cp(*sem),
    )(*prefetch, *args, *extra["inputs"])
    return tuple(outs[:no]), tuple(outs[no:])


def _silu(v):
    return v * jax.nn.sigmoid(v)


def _dsilu(v):
    s = jax.nn.sigmoid(v)
    return s * (1.0 + v * (1.0 - s))


def _softplus(v):
    return jnp.maximum(v, 0.0) + jnp.log1p(jnp.exp(-jnp.abs(v)))


def _mm(a, b):
    return jnp.dot(a.astype(MX), b.astype(MX), preferred_element_type=f32)


def _mm_nt(a, b):
    return lax.dot_general(a.astype(MX), b.astype(MX), (((1,), (1,)), ((), ())), preferred_element_type=f32)


def _mm_tn(a, b):
    return lax.dot_general(a.astype(MX), b.astype(MX), (((0,), (0,)), ((), ())), preferred_element_type=f32)


def _ln_inproj(x, mod, norm1_w, w_pack, rider=None):
    t = x.shape[0]

    def body(x_ref, mod_ref, nw_ref, w_ref, proj_ref, ht_ref):
        xv = x_ref[...]
        rstd = lax.rsqrt(jnp.mean(xv * xv, axis=-1, keepdims=True) + 1e-6)
        h = (xv * rstd * nw_ref[...]) * (1.0 + mod_ref[:, D:2 * D]) + mod_ref[:, 0:D]
        hb = h.astype(MX)
        ht_ref[...] = hb.T
        proj_ref[...] = jnp.dot(hb, w_ref[...], preferred_element_type=f32)

    return _call(
        body, name="ln_inproj", grid=(t // TM,),
        out_shape=(jax.ShapeDtypeStruct((t, W_PACK), f32), jax.ShapeDtypeStruct((D, t), MX)),
        in_specs=[pl.BlockSpec((TM, D), lambda i: (i, 0)), _row(6 * D), _row(), _resident((D, W_PACK))],
        out_specs=(pl.BlockSpec((TM, W_PACK), lambda i: (i, 0)), pl.BlockSpec((D, TM), lambda i: (0, i))),
        sem=("arbitrary",), args=(x, mod, norm1_w, w_pack), rider=rider)


def _ssd_gate_norm(y_scan, xbc_act, proj, d_skip_row, ssd_norm_w):
    t = y_scan.shape[0]

    def body(y_ref, xs_ref, z_ref, dsk_ref, nw_ref, o_ref, ot_ref):
        y = y_ref[...] + xs_ref[...] * dsk_ref[...]
        yz = y * _silu(z_ref[...])
        rstd = lax.rsqrt(jnp.mean(yz * yz, axis=-1, keepdims=True) + 1e-6)
        out = (yz * rstd * nw_ref[...]).astype(MX)
        o_ref[...] = out
        ot_ref[...] = out.T

    blk = pl.BlockSpec((TM, D), lambda i: (i, 0))
    return pl.pallas_call(
        body, name="ssd_gate_norm", grid=(t // TM,),
        out_shape=(jax.ShapeDtypeStruct((t, D), MX), jax.ShapeDtypeStruct((D, t), MX)),
        in_specs=[blk, blk, blk, _row(), _row()], out_specs=(blk, pl.BlockSpec((D, TM), lambda i: (0, i))),
        compiler_params=_cp("arbitrary"),
    )(y_scan, xbc_act, proj, d_skip_row, ssd_norm_w)


def _ln_silu(u_conv, ln_w, ln_b):
    t = u_conv.shape[0]

    def body(u_ref, w_ref, b_ref, o_ref, ot_ref):
        u = u_ref[...]
        mu = jnp.mean(u, axis=-1, keepdims=True)
        uc = u - mu
        rstd = lax.rsqrt(jnp.mean(uc * uc, axis=-1, keepdims=True) + 1e-5)
        out = _silu(uc * rstd * w_ref[...] + b_ref[...]).astype(MX)
        o_ref[...] = out
        ot_ref[...] = out.T

    blk = pl.BlockSpec((TM, D), lambda i: (i, 0))
    return pl.pallas_call(
        body, name="ln_silu", grid=(t // TM,),
        out_shape=(jax.ShapeDtypeStruct((t, D), MX), jax.ShapeDtypeStruct((D, t), MX)),
        in_specs=[blk, _row(), _row()], out_specs=(blk, pl.BlockSpec((D, TM), lambda i: (0, i))),
        compiler_params=_cp("arbitrary"),
    )(u_conv, ln_w, ln_b)


def _outproj_ln2_up(y_ssd, u, w_out, x, mod, norm2_w, w_up):
    t = x.shape[0]

    def body(y_ref, u_ref, wo_ref, x_ref, mod_ref, nw_ref, wu_ref, mix_ref, x1_ref, h2t_ref, up_ref):
        mix = jnp.dot(y_ref[...], wo_ref[0:D, :], preferred_element_type=f32)
        mix = mix + jnp.dot(u_ref[...], wo_ref[D:2 * D, :], preferred_element_type=f32)
        mix_ref[...] = mix
        x1 = x_ref[...] + mod_ref[:, 2 * D:3 * D] * mix
        x1_ref[...] = x1
        rstd = lax.rsqrt(jnp.mean(x1 * x1, axis=-1, keepdims=True) + 1e-6)
        h2 = ((x1 * rstd * nw_ref[...]) * (1.0 + mod_ref[:, 4 * D:5 * D]) + mod_ref[:, 3 * D:4 * D]).astype(MX)
        h2t_ref[...] = h2.T
        for k in range(4):
            up_ref[:, k * UP_SHARD:(k + 1) * UP_SHARD] = jnp.dot(h2, wu_ref[k], preferred_element_type=f32)

    blk = pl.BlockSpec((TM, D), lambda i: (i, 0))
    return pl.pallas_call(
        body, name="outproj_ln2_up", grid=(t // TM,),
        out_shape=(jax.ShapeDtypeStruct((t, D), f32), jax.ShapeDtypeStruct((t, D), f32),
                   jax.ShapeDtypeStruct((D, t), MX), jax.ShapeDtypeStruct((t, 2 * D_FF), f32)),
        in_specs=[blk, blk, _resident((2 * D, D)), blk, _row(6 * D), _row(), _resident((4, D, UP_SHARD))],
        out_specs=(blk, blk, pl.BlockSpec((D, TM), lambda i: (0, i)), pl.BlockSpec((TM, 2 * D_FF), lambda i: (i, 0))),
        compiler_params=_cp("arbitrary"),
    )(y_ssd, u, w_out, x, mod, norm2_w, w_up)


def _down_loss(act, w_down, x1, mod, final_norm_w, target):
    t = x1.shape[0]

    def body(a_ref, wd_ref, x1_ref, mod_ref, wf_ref, tgt_ref, dx2_ref, dffn_ref, dact_ref, st_ref):
        @pl.when(pl.program_id(0) == 0)
        def _():
            st_ref[...] = jnp.zeros_like(st_ref)

        g2 = mod_ref[:, 5 * D:6 * D]
        ffn = jnp.dot(a_ref[...], wd_ref[...], preferred_element_type=f32)
        x2 = x1_ref[...] + g2 * ffn
        rstd = lax.rsqrt(jnp.mean(x2 * x2, axis=-1, keepdims=True) + 1e-6)
        xh = x2 * rstd
        wf = wf_ref[...]
        err = xh * wf - tgt_ref[...]
        dy = err * (1.0 / D)
        dxh = dy * wf
        dx2 = rstd * (dxh - xh * jnp.mean(dxh * xh, axis=-1, keepdims=True))
        dx2_ref[...] = dx2
        dffn = (g2 * dx2).astype(MX)
        dffn_ref[...] = dffn
        dact_ref[...] = lax.dot_general(dffn, wd_ref[...], (((1,), (1,)), ((), ())), preferred_element_type=f32)
        st_ref[0:1, :] += jnp.sum(dy * xh, axis=0, keepdims=True)
        st_ref[1:2, :] += jnp.sum(dx2 * ffn, axis=0, keepdims=True)
        st_ref[2:3, :] += jnp.sum(0.5 * jnp.mean(err * err, axis=-1, keepdims=True), axis=0, keepdims=True)

    blk = pl.BlockSpec((TM, D), lambda i: (i, 0))
    ablk = pl.BlockSpec((TM, D_FF), lambda i: (i, 0))
    return pl.pallas_call(
        body, name="down_loss", grid=(t // TM,),
        out_shape=(jax.ShapeDtypeStruct((t, D), f32), jax.ShapeDtypeStruct((t, D), MX),
                   jax.ShapeDtypeStruct((t, D_FF), f32), jax.ShapeDtypeStruct((8, D), f32)),
        in_specs=[ablk, _resident((D_FF, D)), blk, _row(6 * D), _row(), blk],
        out_specs=(blk, blk, ablk, pl.BlockSpec((8, D), lambda i: (0, 0))),
        compiler_params=_cp("arbitrary"),
    )(act, w_down, x1, mod, final_norm_w, target)


def _pad_of(k):
    return 8 * ((k - 1 + 7) // 8)


def _causal_win(ref, r, t, pad):
    base = pl.multiple_of(r * RC, RC)
    prev = ref[pl.ds(pl.multiple_of(jnp.maximum(base - pad, 0), 8), pad), :]
    prev = jnp.where(r > 0, prev, 0.0)
    return jnp.concatenate([prev, ref[pl.ds(base, RC), :]], axis=0)


def _anti_win(ref, r, t, pad):
    base = pl.multiple_of(r * RC, RC)
    nxt = ref[pl.ds(pl.multiple_of(jnp.minimum(base + RC, t - pad), 8), pad), :]
    nxt = jnp.where(r < t // RC - 1, nxt, 0.0)
    return jnp.concatenate([ref[pl.ds(base, RC), :], nxt], axis=0)


def _shifted(win, offsets):
    for r in range(8):
        mine = [o for o in offsets if o % 8 == r]
        if mine:
            rolled = win if r == 0 else pltpu.roll(win, win.shape[0] - r, 0)
            for o in mine:
                yield o, rolled[o - r:o - r + RC, :]


def _conv_taps(win, w_ref, k, pad):
    first = pad - (k - 1)
    acc = None
    for o, rows in _shifted(win, range(first, first + k)):
        term = w_ref[o - first:o - first + 1, :] * rows
        acc = term if acc is None else acc + term
    return acc


def _corr_taps(win, w_ref, k):
    acc = None
    for o, rows in _shifted(win, range(k)):
        term = w_ref[k - 1 - o:k - o, :] * rows
        acc = term if acc is None else acc + term
    return acc


def _dw_accumulate(dw_scr, d, win, k, pad):
    first = pad - (k - 1)
    for o, rows in _shifted(win, range(first, first + k)):
        j = o - first
        prod = d * rows
        dw_scr[8 * j:8 * j + 8, :] += prod.reshape(RC // 8, 8, prod.shape[-1]).sum(axis=0)


def _dw_finish(dw_scr, dw_ref, k):
    for j in range(k):
        dw_ref[j:j + 1, :] = jnp.sum(dw_scr[8 * j:8 * j + 8, :], axis=0, keepdims=True)


def _rows8(v):
    return v.reshape(RC // 8, 8, v.shape[-1]).sum(axis=0)


def _ssd_conv_fwd(proj, conv_w, conv_b, rider=None):
    t = proj.shape[0]
    pad = _pad_of(K_SSD)
    c0 = OFF_XBC // CW

    def body(x_ref, w_ref, b_ref, o_ref):
        def step(r, carry):
            win = _causal_win(x_ref, r, t, pad)
            o_ref[pl.ds(pl.multiple_of(r * RC, RC), RC), :] = _silu(_conv_taps(win, w_ref, K_SSD, pad) + b_ref[...])
            return carry
        lax.fori_loop(0, t // RC, step, 0)

    return _call(
        body, name="ssd_conv_fwd", grid=(D_XBC // CW,), out_shape=(jax.ShapeDtypeStruct((t, D_XBC), f32),),
        in_specs=[pl.BlockSpec((t, CW), lambda j: (0, c0 + j)), pl.BlockSpec((K_SSD, CW), lambda j: (0, j)),
                  pl.BlockSpec((1, CW), lambda j: (0, j))],
        out_specs=(pl.BlockSpec((t, CW), lambda j: (0, j)),), sem=("arbitrary",), args=(proj, conv_w, conv_b), rider=rider)


def _glu_conv_fwd(proj, conv_w, conv_b, rider=None):
    t = proj.shape[0]
    pad = _pad_of(K_CONF)
    ca, cg = OFF_CA // CW, OFF_CG // CW

    def body(a_ref, g_ref, w_ref, b_ref, o_ref, v_scr):
        def glu(r, carry):
            rows = pl.ds(pl.multiple_of(r * RC, RC), RC)
            v_scr[rows, :] = a_ref[rows, :] * jax.nn.sigmoid(g_ref[rows, :])
            return carry
        lax.fori_loop(0, t // RC, glu, 0)

        def step(r, carry):
            win = _causal_win(v_scr, r, t, pad)
            o_ref[pl.ds(pl.multiple_of(r * RC, RC), RC), :] = _conv_taps(win, w_ref, K_CONF, pad) + b_ref[...]
            return carry
        lax.fori_loop(0, t // RC, step, 0)

    return _call(
        body, name="glu_conv_fwd", grid=(D // CW,), out_shape=(jax.ShapeDtypeStruct((t, D), f32),),
        in_specs=[pl.BlockSpec((t, CW), lambda j: (0, ca + j)), pl.BlockSpec((t, CW), lambda j: (0, cg + j)),
                  pl.BlockSpec((K_CONF, CW), lambda j: (0, j)), pl.BlockSpec((1, CW), lambda j: (0, j))],
        out_specs=(pl.BlockSpec((t, CW), lambda j: (0, j)),),
        scratch_shapes=[pltpu.VMEM((t, CW), f32)], sem=("arbitrary",), args=(proj, proj, conv_w, conv_b), rider=rider)


def _ffn_conv_fwd(up, conv_w, conv_b, rider=None):
    t = up.shape[0]
    pad = _pad_of(K_FFN)
    nb = D_FF // CW

    def body(g_ref, v_ref, wg_ref, wv_ref, bg_ref, bv_ref, o_ref, ot_ref):
        def step(r, carry):
            gc = _conv_taps(_causal_win(g_ref, r, t, pad), wg_ref, K_FFN, pad) + bg_ref[...]
            vc = _conv_taps(_causal_win(v_ref, r, t, pad), wv_ref, K_FFN, pad) + bv_ref[...]
            o_ref[pl.ds(pl.multiple_of(r * RC, RC), RC), :] = (_silu(gc) * vc).astype(MX)
            return carry
        lax.fori_loop(0, t // RC, step, 0)
        ot_ref[...] = o_ref[...].T

    return _call(
        body, name="ffn_conv_fwd", grid=(nb,),
        out_shape=(jax.ShapeDtypeStruct((t, D_FF), MX), jax.ShapeDtypeStruct((D_FF, t), MX)),
        in_specs=[pl.BlockSpec((t, CW), lambda j: (0, j)), pl.BlockSpec((t, CW), lambda j: (0, nb + j)),
                  pl.BlockSpec((K_FFN, CW), lambda j: (0, j)), pl.BlockSpec((K_FFN, CW), lambda j: (0, nb + j)),
                  pl.BlockSpec((1, CW), lambda j: (0, j)), pl.BlockSpec((1, CW), lambda j: (0, nb + j))],
        out_specs=(pl.BlockSpec((t, CW), lambda j: (0, j)), pl.BlockSpec((CW, t), lambda j: (j, 0))), sem=("arbitrary",),
        args=(up, up, conv_w, conv_w, conv_b, conv_b), rider=rider)


def _ffn_conv_bwd(up, conv_w, conv_b, d_act, rider=None):
    t = up.shape[0]
    pad = _pad_of(K_FFN)
    nb = D_FF // CW

    def body(g_ref, v_ref, wg_ref, wv_ref, bg_ref, bv_ref, da_ref, dup_ref, dw_ref, db_ref,
             dg_scr, dv_scr, dwg_scr, dwv_scr, db_scr):
        dwg_scr[...] = jnp.zeros_like(dwg_scr)
        dwv_scr[...] = jnp.zeros_like(dwv_scr)
        db_scr[...] = jnp.zeros_like(db_scr)

        def first(r, carry):
            rows = pl.ds(pl.multiple_of(r * RC, RC), RC)
            gwin = _causal_win(g_ref, r, t, pad)
            vwin = _causal_win(v_ref, r, t, pad)
            gc = _conv_taps(gwin, wg_ref, K_FFN, pad) + bg_ref[...]
            vc = _conv_taps(vwin, wv_ref, K_FFN, pad) + bv_ref[...]
            da = da_ref[rows, :]
            dgc = da * vc * _dsilu(gc)
            dvc = da * _silu(gc)
            dg_scr[rows, :] = dgc
            dv_scr[rows, :] = dvc
            _dw_accumulate(dwg_scr, dgc, gwin, K_FFN, pad)
            _dw_accumulate(dwv_scr, dvc, vwin, K_FFN, pad)
            db_scr[0:8, :] += _rows8(dgc)
            db_scr[8:16, :] += _rows8(dvc)
            return carry
        lax.fori_loop(0, t // RC, first, 0)

        def second(r, carry):
            rows = pl.ds(pl.multiple_of(r * RC, RC), RC)
            dup_ref[0, rows, :] = _corr_taps(_anti_win(dg_scr, r, t, pad), wg_ref, K_FFN).astype(MX)
            dup_ref[1, rows, :] = _corr_taps(_anti_win(dv_scr, r, t, pad), wv_ref, K_FFN).astype(MX)
            return carry
        lax.fori_loop(0, t // RC, second, 0)

        for j in range(K_FFN):
            dw_ref[0, j:j + 1, :] = jnp.sum(dwg_scr[8 * j:8 * j + 8, :], axis=0, keepdims=True)
            dw_ref[1, j:j + 1, :] = jnp.sum(dwv_scr[8 * j:8 * j + 8, :], axis=0, keepdims=True)
        db_ref[0] = jnp.sum(db_scr[0:8, :], axis=0, keepdims=True)
        db_ref[1] = jnp.sum(db_scr[8:16, :], axis=0, keepdims=True)

    return _call(
        body, name="ffn_conv_bwd", grid=(nb,),
        out_shape=(jax.ShapeDtypeStruct((2, t, D_FF), MX), jax.ShapeDtypeStruct((2, K_FFN, D_FF), f32),
                   jax.ShapeDtypeStruct((2, 1, D_FF), f32)),
        in_specs=[pl.BlockSpec((t, CW), lambda j: (0, j)), pl.BlockSpec((t, CW), lambda j: (0, nb + j)),
                  pl.BlockSpec((K_FFN, CW), lambda j: (0, j)), pl.BlockSpec((K_FFN, CW), lambda j: (0, nb + j)),
                  pl.BlockSpec((1, CW), lambda j: (0, j)), pl.BlockSpec((1, CW), lambda j: (0, nb + j)),
                  pl.BlockSpec((t, CW), lambda j: (0, j))],
        out_specs=(pl.BlockSpec((2, t, CW), lambda j: (0, 0, j)), pl.BlockSpec((2, K_FFN, CW), lambda j: (0, 0, j)),
                   pl.BlockSpec((2, 1, CW), lambda j: (0, 0, j))),
        scratch_shapes=[pltpu.VMEM((t, CW), f32), pltpu.VMEM((t, CW), f32), pltpu.VMEM((8 * K_FFN, CW), f32),
                        pltpu.VMEM((8 * K_FFN, CW), f32), pltpu.VMEM((16, CW), f32)],
        sem=("arbitrary",), args=(up, up, conv_w, conv_w, conv_b, conv_b, d_act), rider=rider)


def _glu_conv_bwd(proj, conv_w, d_uconv, rider=None):
    t = proj.shape[0]
    pad = _pad_of(K_CONF)
    ca, cg = OFF_CA // CW, OFF_CG // CW

    def body(a_ref, g_ref, w_ref, du_ref, dc_ref, dw_ref, db_ref, v_scr, dw_scr, db_scr):
        dw_scr[...] = jnp.zeros_like(dw_scr)
        db_scr[...] = jnp.zeros_like(db_scr)

        def glu(r, carry):
            rows = pl.ds(pl.multiple_of(r * RC, RC), RC)
            v_scr[rows, :] = a_ref[rows, :] * jax.nn.sigmoid(g_ref[rows, :])
            return carry
        lax.fori_loop(0, t // RC, glu, 0)

        def step(r, carry):
            rows = pl.ds(pl.multiple_of(r * RC, RC), RC)
            du = du_ref[rows, :]
            _dw_accumulate(dw_scr, du, _causal_win(v_scr, r, t, pad), K_CONF, pad)
            db_scr[...] += _rows8(du)
            dv = _corr_taps(_anti_win(du_ref, r, t, pad), w_ref, K_CONF)
            a = a_ref[rows, :]
            s = jax.nn.sigmoid(g_ref[rows, :])
            dc_ref[0, rows, :] = (dv * s).astype(MX)
            dc_ref[1, rows, :] = (dv * a * s * (1.0 - s)).astype(MX)
            return carry
        lax.fori_loop(0, t // RC, step, 0)
        _dw_finish(dw_scr, dw_ref, K_CONF)
        db_ref[...] = jnp.sum(db_scr[...], axis=0, keepdims=True)

    return _call(
        body, name="glu_conv_bwd", grid=(D // CW,),
        out_shape=(jax.ShapeDtypeStruct((2, t, D), MX), jax.ShapeDtypeStruct((K_CONF, D), f32),
                   jax.ShapeDtypeStruct((1, D), f32)),
        in_specs=[pl.BlockSpec((t, CW), lambda j: (0, ca + j)), pl.BlockSpec((t, CW), lambda j: (0, cg + j)),
                  pl.BlockSpec((K_CONF, CW), lambda j: (0, j)), pl.BlockSpec((t, CW), lambda j: (0, j))],
        out_specs=(pl.BlockSpec((2, t, CW), lambda j: (0, 0, j)), pl.BlockSpec((K_CONF, CW), lambda j: (0, j)),
                   pl.BlockSpec((1, CW), lambda j: (0, j))),
        scratch_shapes=[pltpu.VMEM((t, CW), f32), pltpu.VMEM((8 * K_CONF, CW), f32), pltpu.VMEM((8, CW), f32)],
        sem=("arbitrary",), args=(proj, proj, conv_w, d_uconv), rider=rider)


def _ssd_conv_bwd_x(proj, conv_w, conv_b, d_xs, d_y, d_skip_row):
    t = proj.shape[0]
    pad = _pad_of(K_SSD)
    c0 = OFF_XBC // CW

    def body(x_ref, w_ref, b_ref, dxs_ref, dy_ref, dsk_ref, draw_ref, dw_ref, db_ref, dp_scr, dw_scr, db_scr):
        dw_scr[...] = jnp.zeros_like(dw_scr)
        db_scr[...] = jnp.zeros_like(db_scr)

        def first(r, carry):
            rows = pl.ds(pl.multiple_of(r * RC, RC), RC)
            win = _causal_win(x_ref, r, t, pad)
            pre = _conv_taps(win, w_ref, K_SSD, pad) + b_ref[...]
            dpre = (dxs_ref[rows, :] + dy_ref[rows, :] * dsk_ref[...]) * _dsilu(pre)
            dp_scr[rows, :] = dpre
            _dw_accumulate(dw_scr, dpre, win, K_SSD, pad)
            db_scr[...] += _rows8(dpre)
            return carry
        lax.fori_loop(0, t // RC, first, 0)

        def second(r, carry):
            rows = pl.ds(pl.multiple_of(r * RC, RC), RC)
            draw_ref[rows, :] = _corr_taps(_anti_win(dp_scr, r, t, pad), w_ref, K_SSD).astype(MX)
            return carry
        lax.fori_loop(0, t // RC, second, 0)
        _dw_finish(dw_scr, dw_ref, K_SSD)
        db_ref[...] = jnp.sum(db_scr[...], axis=0, keepdims=True)

    cb = pl.BlockSpec((t, CW), lambda j: (0, j))
    return pl.pallas_call(
        body, name="ssd_conv_bwd_x", grid=(D // CW,),
        out_shape=(jax.ShapeDtypeStruct((t, D), MX), jax.ShapeDtypeStruct((K_SSD, D), f32),
                   jax.ShapeDtypeStruct((1, D), f32)),
        in_specs=[pl.BlockSpec((t, CW), lambda j: (0, c0 + j)), pl.BlockSpec((K_SSD, CW), lambda j: (0, j)),
                  pl.BlockSpec((1, CW), lambda j: (0, j)), cb, cb, pl.BlockSpec((1, CW), lambda j: (0, j))],
        out_specs=(cb, pl.BlockSpec((K_SSD, CW), lambda j: (0, j)), pl.BlockSpec((1, CW), lambda j: (0, j))),
        scratch_shapes=[pltpu.VMEM((t, CW), f32), pltpu.VMEM((8 * K_SSD, CW), f32), pltpu.VMEM((8, CW), f32)],
        compiler_params=_cp("arbitrary"),
    )(proj, conv_w, conv_b, d_xs, d_y, d_skip_row)


def _ssd_conv_bwd_bc(proj, conv_w, conv_b, d_bc):
    t = proj.shape[0]
    pad = _pad_of(K_SSD)
    c0 = (OFF_XBC + D) // CW
    w0 = D // CW

    def body(x_ref, w_ref, b_ref, dbc_ref, draw_ref, dw_ref, db_ref, dp_scr, dw_scr, db_scr):
        dw_scr[...] = jnp.zeros_like(dw_scr)
        db_scr[...] = jnp.zeros_like(db_scr)

        def first(r, carry):
            rows = pl.ds(pl.multiple_of(r * RC, RC), RC)
            win = _causal_win(x_ref, r, t, pad)
            pre = _conv_taps(win, w_ref, K_SSD, pad) + b_ref[...]
            dpre = dbc_ref[0, rows, :] * _dsilu(pre)
            dp_scr[rows, :] = dpre
            _dw_accumulate(dw_scr, dpre, win, K_SSD, pad)
            db_scr[...] += _rows8(dpre)
            return carry
        lax.fori_loop(0, t // RC, first, 0)

        def second(r, carry):
            rows = pl.ds(pl.multiple_of(r * RC, RC), RC)
            draw_ref[rows, :] = _corr_taps(_anti_win(dp_scr, r, t, pad), w_ref, K_SSD).astype(MX)
            return carry
        lax.fori_loop(0, t // RC, second, 0)
        _dw_finish(dw_scr, dw_ref, K_SSD)
        db_ref[...] = jnp.sum(db_scr[...], axis=0, keepdims=True)

    return pl.pallas_call(
        body, name="ssd_conv_bwd_bc", grid=(2,),
        out_shape=(jax.ShapeDtypeStruct((t, 2 * CW), MX), jax.ShapeDtypeStruct((K_SSD, 2 * CW), f32),
                   jax.ShapeDtypeStruct((1, 2 * CW), f32)),
        in_specs=[pl.BlockSpec((t, CW), lambda j: (0, c0 + j)), pl.BlockSpec((K_SSD, CW), lambda j: (0, w0 + j)),
                  pl.BlockSpec((1, CW), lambda j: (0, w0 + j)), pl.BlockSpec((1, t, CW), lambda j: (j, 0, 0))],
        out_specs=(pl.BlockSpec((t, CW), lambda j: (0, j)), pl.BlockSpec((K_SSD, CW), lambda j: (0, j)),
                   pl.BlockSpec((1, CW), lambda j: (0, j))),
        scratch_shapes=[pltpu.VMEM((t, CW), f32), pltpu.VMEM((8 * K_SSD, CW), f32), pltpu.VMEM((8, CW), f32)],
        compiler_params=_cp("arbitrary"),
    )(proj, conv_w, conv_b, d_bc)


def _chunk_masks():
    ii = lax.broadcasted_iota(jnp.int32, (CHUNK, CHUNK), 0)
    jj = lax.broadcasted_iota(jnp.int32, (CHUNK, CHUNK), 1)
    return ii == jj, jj <= ii, jj >= ii


def _to_row(col, eye):
    return jnp.sum(jnp.where(eye, col, 0.0), axis=0, keepdims=True)


def _to_col(row, eye):
    return jnp.sum(jnp.where(eye, row, 0.0), axis=1, keepdims=True)


def _head_decay(dt_h, a_h, eye, tril):
    a_row = _to_row(dt_h * a_h, eye)
    cs = jnp.sum(jnp.where(tril, a_row, 0.0), axis=1, keepdims=True)
    cs_row = _to_row(cs, eye)
    decay = jnp.where(tril, jnp.exp(jnp.where(tril, cs - cs_row, 0.0)), 0.0)
    total = jnp.sum(a_row, axis=1, keepdims=True)
    return cs, decay, total


SCAN_UNROLL = 4


def _unrolled_loop(n, step, init):
    unroll = min(SCAN_UNROLL, n)
    assert n % unroll == 0

    def trip(i, carry):
        for u in range(unroll):
            carry = step(unroll * i + u, carry)
        return carry
    return lax.fori_loop(0, n // unroll, trip, init)


def _lane_pick(mat, lane, which):
    return jnp.sum(jnp.where(lane == which, mat, 0.0), axis=1, keepdims=True)


def _ssd_fwd(xbc_act, proj, dt_bias_row, a_log_row, rider=None):
    t = xbc_act.shape[0]
    nc = t // CHUNK
    cb, cc, cdt = D // LANES, (D + 2 * STATE_N) // LANES, OFF_DT // LANES

    def body(x_ref, b_ref, c_ref, dt_ref, dtb_ref, alog_ref, y_ref, st_ref):
        j = pl.program_id(0)
        eye, tril, _ = _chunk_masks()
        lane = lax.broadcasted_iota(jnp.int32, (1, LANES), 1)
        first = lane < HEAD_P
        a_row = -jnp.exp(alog_ref[...])
        a_heads = [jnp.sum(jnp.where(lane == 2 * j + h, a_row, 0.0), axis=1, keepdims=True) for h in range(2)]

        def chunk(c, hprev):
            rows = pl.ds(pl.multiple_of(c * CHUNK, CHUNK), CHUNK)
            xv, bm, cm = x_ref[rows, :], b_ref[rows, :], c_ref[rows, :]
            dt = _softplus(dt_ref[rows, :] + dtb_ref[...])
            st_ref[c] = hprev
            g = _mm_nt(cm, bm)
            ch = _mm(cm, hprev)
            dts = [_lane_pick(dt, lane, 2 * j + h) for h in range(2)]
            xdt = xv * jnp.where(first, dts[0], dts[1])
            ys, hs = [], []
            for h in range(2):
                cs, decay, total = _head_decay(dts[h], a_heads[h], eye, tril)
                y = _mm(g * decay, xdt) + jnp.exp(cs) * ch
                s = _mm_tn(bm * jnp.exp(total - cs), xdt)
                ys.append(y)
                hs.append(jnp.exp(total) * hprev + s)
            y_ref[rows, :] = jnp.where(first, ys[0], ys[1])
            return jnp.where(first, hs[0], hs[1])

        _unrolled_loop(nc, chunk, jnp.zeros((STATE_N, LANES), f32))

    blk = lambda f: pl.BlockSpec((t, LANES), f)
    return _call(
        body, name="ssd_fwd", grid=(D // LANES,),
        out_shape=(jax.ShapeDtypeStruct((t, D), f32), jax.ShapeDtypeStruct((nc, STATE_N, D), f32)),
        in_specs=[blk(lambda j: (0, j)), blk(lambda j: (0, cb + j // 4)), blk(lambda j: (0, cc + j // 4)),
                  blk(lambda j: (0, cdt)), _row(LANES), _row(LANES)],
        out_specs=(blk(lambda j: (0, j)), pl.BlockSpec((nc, STATE_N, LANES), lambda j: (0, 0, j))),
        sem=("arbitrary",), args=(xbc_act, xbc_act, xbc_act, proj, dt_bias_row, a_log_row), rider=rider)


def _ssd_bwd(xbc_act, proj, dt_bias_row, a_log_row, states, d_y, rider=None):
    t = xbc_act.shape[0]
    nc = t // CHUNK
    cb, cc, cdt = D // LANES, (D + 2 * STATE_N) // LANES, OFF_DT // LANES

    def body(x_ref, b_ref, c_ref, dt_ref, dtb_ref, alog_ref, st_ref, dy_ref, dx_ref, dbc_ref, ddt_ref, da_ref):
        grp, p = pl.program_id(0), pl.program_id(1)
        j = 4 * grp + p
        eye, tril, triu = _chunk_masks()
        lane = lax.broadcasted_iota(jnp.int32, (1, LANES), 1)
        first = lane < HEAD_P
        last_row = lax.broadcasted_iota(jnp.int32, (CHUNK, 1), 0) == CHUNK - 1
        a_row = -jnp.exp(alog_ref[...])
        a_heads = [jnp.sum(jnp.where(lane == 2 * j + h, a_row, 0.0), axis=1, keepdims=True) for h in range(2)]

        @pl.when(p == 0)
        def _():
            dbc_ref[...] = jnp.zeros_like(dbc_ref)

        @pl.when(j == 0)
        def _():
            ddt_ref[...] = jnp.zeros_like(ddt_ref)
            da_ref[...] = jnp.zeros_like(da_ref)

        def chunk(i, dh):
            c = nc - 1 - i
            rows = pl.ds(pl.multiple_of(c * CHUNK, CHUNK), CHUNK)
            xv, bm, cm = x_ref[rows, :], b_ref[rows, :], c_ref[rows, :]
            dtr = dt_ref[rows, :] + dtb_ref[...]
            dt = _softplus(dtr)
            hprev = st_ref[c]
            dy = dy_ref[rows, :]
            g = _mm_nt(cm, bm)
            dts = [_lane_pick(dt, lane, 2 * j + h) for h in range(2)]
            xdt = xv * jnp.where(first, dts[0], dts[1])
            dxs, dhs = [], []
            db_sum, dc_sum = None, None
            ddt_mat = jnp.zeros((CHUNK, LANES), f32)
            da_acc = jnp.zeros((1, LANES), f32)
            for h in range(2):
                mine = first if h == 0 else jnp.logical_not(first)
                cs, decay, total = _head_decay(dts[h], a_heads[h], eye, tril)
                e_cs, e_tot = jnp.exp(cs), jnp.exp(total)
                dec_s = jnp.exp(total - cs)
                dyh = jnp.where(mine, dy, 0.0)
                xdth = jnp.where(mine, xdt, 0.0)
                dhh = jnp.where(mine, dh, 0.0)
                hph = jnp.where(mine, hprev, 0.0)
                m = g * decay
                dm = _mm_nt(dyh, xdth)
                dg = dm * decay
                w = dm * m
                bdec = bm * dec_s
                dxdt = _mm_tn(m, dyh) + _mm(bdec, dhh)
                dc_off = _mm_nt(dyh, hph) * e_cs
                db_s = _mm_nt(xdth, dhh) * dec_s
                dc_h = _mm(dg, bm) + dc_off
                db_h = _mm_tn(dg, cm) + db_s
                r_s = jnp.sum(db_s * bm, axis=1, keepdims=True)
                dtotal = jnp.sum(r_s, axis=0, keepdims=True) + e_tot * jnp.sum(
                    jnp.sum(dhh * hph, axis=1, keepdims=True), axis=0, keepdims=True)
                dcs = (jnp.sum(w, axis=1, keepdims=True) - _to_col(jnp.sum(w, axis=0, keepdims=True), eye)
                       + jnp.sum(dc_off * cm, axis=1, keepdims=True) - r_s + jnp.where(last_row, dtotal, 0.0))
                da_col = jnp.sum(jnp.where(triu, _to_row(dcs, eye), 0.0), axis=1, keepdims=True)
                ddt = da_col * a_heads[h] + jnp.sum(jnp.where(mine, dxdt * xv, 0.0), axis=1, keepdims=True)
                ddt_mat = ddt_mat + jnp.where(lane == 2 * j + h, ddt, 0.0)
                da_acc = da_acc + jnp.where(lane == 2 * j + h, jnp.sum(da_col * dts[h], axis=0, keepdims=True), 0.0)
                dxs.append(dxdt * dts[h])
                dhs.append(e_tot * dhh + _mm_tn(cm * e_cs, dyh))
                db_sum = db_h if db_sum is None else db_sum + db_h
                dc_sum = dc_h if dc_sum is None else dc_sum + dc_h
            dx_ref[rows, :] = jnp.where(first, dxs[0], dxs[1])
            dbc_ref[0, rows, :] += db_sum
            dbc_ref[1, rows, :] += dc_sum
            ddt_ref[rows, :] += ddt_mat * jax.nn.sigmoid(dtr)
            da_ref[...] += da_acc * a_row
            return jnp.where(first, dhs[0], dhs[1])

        _unrolled_loop(nc, chunk, jnp.zeros((STATE_N, LANES), f32))

    blk = lambda f: pl.BlockSpec((t, LANES), f)
    return _call(
        body, name="ssd_bwd", grid=(2, 4),
        out_shape=(jax.ShapeDtypeStruct((t, D), f32), jax.ShapeDtypeStruct((2, t, 2 * STATE_N), f32),
                   jax.ShapeDtypeStruct((t, LANES), f32), jax.ShapeDtypeStruct((1, LANES), f32)),
        in_specs=[blk(lambda g, p: (0, 4 * g + p)), blk(lambda g, p: (0, cb + g)), blk(lambda g, p: (0, cc + g)),
                  blk(lambda g, p: (0, cdt)), _row(LANES), _row(LANES),
                  pl.BlockSpec((nc, STATE_N, LANES), lambda g, p: (0, 0, 4 * g + p)), blk(lambda g, p: (0, 4 * g + p))],
        out_specs=(blk(lambda g, p: (0, 4 * g + p)), pl.BlockSpec((2, t, LANES), lambda g, p: (0, 0, g)),
                   blk(lambda g, p: (0, 0)), _row(LANES)),
        sem=("arbitrary", "arbitrary"), args=(xbc_act, xbc_act, xbc_act, proj, dt_bias_row, a_log_row, states, d_y),
        rider=rider)


def _up_bwd(d_up, w_up, x1, mod, norm2_w, dx2, mix, w_out, rider=None):
    t = x1.shape[0]

    def body(dup_ref, wu_ref, x1_ref, mod_ref, nw_ref, dx2_ref, mix_ref, wo_ref,
             dx1_ref, dmix_ref, dys_ref, du_ref, st_ref):
        @pl.when(pl.program_id(0) == 0)
        def _():
            st_ref[...] = jnp.zeros_like(st_ref)

        nt = (((1,), (1,)), ((), ()))
        dh = None
        for k in range(4):
            lo = (k % 2) * UP_SHARD
            part = lax.dot_general(dup_ref[k // 2, :, lo:lo + UP_SHARD], wu_ref[k], nt, preferred_element_type=f32)
            dh = part if dh is None else dh + part
        x1 = x1_ref[...]
        rstd = lax.rsqrt(jnp.mean(x1 * x1, axis=-1, keepdims=True) + 1e-6)
        xh = x1 * rstd
        nw = nw_ref[...]
        sc = 1.0 + mod_ref[:, 4 * D:5 * D]
        st_ref[0:1, :] += jnp.sum(dh, axis=0, keepdims=True)
        st_ref[1:2, :] += jnp.sum(dh * xh * nw, axis=0, keepdims=True)
        st_ref[2:3, :] += jnp.sum(dh * sc * xh, axis=0, keepdims=True)
        dxh = dh * sc * nw
        dx1 = dx2_ref[...] + rstd * (dxh - xh * jnp.mean(dxh * xh, axis=-1, keepdims=True))
        dx1_ref[...] = dx1
        st_ref[3:4, :] += jnp.sum(dx1 * mix_ref[...], axis=0, keepdims=True)
        dmix = (mod_ref[:, 2 * D:3 * D] * dx1).astype(MX)
        dmix_ref[...] = dmix
        dys_ref[...] = lax.dot_general(dmix, wo_ref[0:D, :], nt, preferred_element_type=f32)
        du_ref[...] = lax.dot_general(dmix, wo_ref[D:2 * D, :], nt, preferred_element_type=f32)

    blk = pl.BlockSpec((TM, D), lambda i: (i, 0))
    return _call(
        body, name="up_bwd", grid=(t // TM,),
        out_shape=(jax.ShapeDtypeStruct((t, D), f32), jax.ShapeDtypeStruct((t, D), MX),
                   jax.ShapeDtypeStruct((t, D), f32), jax.ShapeDtypeStruct((t, D), f32),
                   jax.ShapeDtypeStruct((8, D), f32)),
        in_specs=[pl.BlockSpec((2, TM, D_FF), lambda i: (0, i, 0)), _resident((4, D, UP_SHARD)), blk, _row(6 * D), _row(),
                  blk, blk, _resident((2 * D, D))],
        out_specs=(blk, blk, blk, blk, pl.BlockSpec((8, D), lambda i: (0, 0))),
        sem=("arbitrary",), args=(d_up, w_up, x1, mod, norm2_w, dx2, mix, w_out), rider=rider)


def _ln_silu_bwd(d_u, u_conv, ln_w, ln_b):
    t = d_u.shape[0]

    def body(du_ref, u_ref, w_ref, b_ref, o_ref, st_ref):
        @pl.when(pl.program_id(0) == 0)
        def _():
            st_ref[...] = jnp.zeros_like(st_ref)

        u = u_ref[...]
        mu = jnp.mean(u, axis=-1, keepdims=True)
        uc = u - mu
        rstd = lax.rsqrt(jnp.mean(uc * uc, axis=-1, keepdims=True) + 1e-5)
        n = uc * rstd
        w = w_ref[...]
        dl = du_ref[...] * _dsilu(n * w + b_ref[...])
        st_ref[0:1, :] += jnp.sum(dl * n, axis=0, keepdims=True)
        st_ref[1:2, :] += jnp.sum(dl, axis=0, keepdims=True)
        dn = dl * w
        o_ref[...] = rstd * (dn - jnp.mean(dn, axis=-1, keepdims=True) - n * jnp.mean(dn * n, axis=-1, keepdims=True))

    blk = pl.BlockSpec((TM, D), lambda i: (i, 0))
    return pl.pallas_call(
        body, name="ln_silu_bwd", grid=(t // TM,),
        out_shape=(jax.ShapeDtypeStruct((t, D), f32), jax.ShapeDtypeStruct((8, D), f32)),
        in_specs=[blk, blk, _row(), _row()], out_specs=(blk, pl.BlockSpec((8, D), lambda i: (0, 0))),
        compiler_params=_cp("arbitrary"),
    )(d_u, u_conv, ln_w, ln_b)


def _ssd_gate_norm_bwd(d_out, y_scan, xbc_act, proj, d_skip_row, ssd_norm_w):
    t = d_out.shape[0]

    def body(do_ref, y_ref, xs_ref, z_ref, dsk_ref, nw_ref, dy_ref, dz_ref, st_ref):
        @pl.when(pl.program_id(0) == 0)
        def _():
            st_ref[...] = jnp.zeros_like(st_ref)

        xs = xs_ref[...]
        y = y_ref[...] + xs * dsk_ref[...]
        z = z_ref[...]
        s = _silu(z)
        yz = y * s
        rstd = lax.rsqrt(jnp.mean(yz * yz, axis=-1, keepdims=True) + 1e-6)
        n = yz * rstd
        do = do_ref[...]
        st_ref[0:1, :] += jnp.sum(do * n, axis=0, keepdims=True)
        dn = do * nw_ref[...]
        dyz = rstd * (dn - n * jnp.mean(dn * n, axis=-1, keepdims=True))
        dy = dyz * s
        dy_ref[...] = dy
        dz_ref[...] = (dyz * y * _dsilu(z)).astype(MX)
        st_ref[1:2, :] += jnp.sum(dy * xs, axis=0, keepdims=True)

    blk = pl.BlockSpec((TM, D), lambda i: (i, 0))
    return pl.pallas_call(
        body, name="ssd_gate_norm_bwd", grid=(t // TM,),
        out_shape=(jax.ShapeDtypeStruct((t, D), f32), jax.ShapeDtypeStruct((t, D), MX), jax.ShapeDtypeStruct((8, D), f32)),
        in_specs=[blk, blk, blk, blk, _row(), _row()], out_specs=(blk, blk, pl.BlockSpec((8, D), lambda i: (0, 0))),
        compiler_params=_cp("arbitrary"),
    )(d_out, y_scan, xbc_act, proj, d_skip_row, ssd_norm_w)


def _inproj_bwd(d_z, d_xraw, d_bcraw, d_conf, d_dt, w_pack, x, mod, norm1_w, dx1, rider=None):
    t = x.shape[0]

    def body(dz_ref, dx_ref, dbc_ref, dcf_ref, ddt_ref, w_ref, x_ref, mod_ref, nw_ref, dx1_ref, gx_ref, st_ref):
        @pl.when(pl.program_id(0) == 0)
        def _():
            st_ref[...] = jnp.zeros_like(st_ref)

        nt = (((1,), (1,)), ((), ()))
        dot = lambda a, lo, hi: lax.dot_general(a, w_ref[:, lo:hi], nt, preferred_element_type=f32)
        dh = dot(dz_ref[...], OFF_Z, OFF_Z + D)
        dh = dh + dot(dx_ref[...], OFF_XBC, OFF_XBC + D)
        dh = dh + dot(dbc_ref[...], OFF_XBC + D, OFF_XBC + D_XBC)
        dh = dh + dot(dcf_ref[0], OFF_CA, OFF_CA + D)
        dh = dh + dot(dcf_ref[1], OFF_CG, OFF_CG + D)
        dh = dh + dot(ddt_ref[...].astype(MX), OFF_DT, OFF_DT + LANES)
        st_ref[3:4, 0:LANES] += jnp.sum(ddt_ref[...], axis=0, keepdims=True)
        xv = x_ref[...]
        rstd = lax.rsqrt(jnp.mean(xv * xv, axis=-1, keepdims=True) + 1e-6)
        xh = xv * rstd
        nw = nw_ref[...]
        sc = 1.0 + mod_ref[:, D:2 * D]
        st_ref[0:1, :] += jnp.sum(dh, axis=0, keepdims=True)
        st_ref[1:2, :] += jnp.sum(dh * xh * nw, axis=0, keepdims=True)
        st_ref[2:3, :] += jnp.sum(dh * sc * xh, axis=0, keepdims=True)
        dxh = dh * sc * nw
        gx_ref[...] = dx1_ref[...] + rstd * (dxh - xh * jnp.mean(dxh * xh, axis=-1, keepdims=True))

    blk = pl.BlockSpec((TM, D), lambda i: (i, 0))
    return _call(
        body, name="inproj_bwd", grid=(t // TM,),
        out_shape=(jax.ShapeDtypeStruct((t, D), f32), jax.ShapeDtypeStruct((8, D), f32)),
        in_specs=[blk, blk, pl.BlockSpec((TM, 2 * CW), lambda i: (i, 0)), pl.BlockSpec((2, TM, D), lambda i: (0, i, 0)),
                  pl.BlockSpec((TM, LANES), lambda i: (i, 0)), _resident((D, W_PACK)), blk, _row(6 * D), _row(), blk],
        out_specs=(blk, pl.BlockSpec((8, D), lambda i: (0, 0))),
        sem=("arbitrary",), args=(d_z, d_xraw, d_bcraw, d_conf, d_dt, w_pack, x, mod, norm1_w, dx1), rider=rider)


def _wgrad(at, d, name, bn=256):
    k, t = at.shape
    n = d.shape[1]
    out_dtype = MX

    def body(a_ref, d_ref, o_ref):
        o_ref[...] = jnp.dot(a_ref[...], d_ref[...].astype(MX), preferred_element_type=f32).astype(out_dtype)

    return pl.pallas_call(
        body, name=name, grid=(n // bn,), out_shape=jax.ShapeDtypeStruct((k, n), out_dtype),
        in_specs=[_resident((k, t)), pl.BlockSpec((t, bn), lambda j: (0, j))],
        out_specs=pl.BlockSpec((k, bn), lambda j: (0, j)), compiler_params=_cp("arbitrary"),
    )(at, d)


def _wgrad_stacked(at, d, name, bn):
    out_dtype = MX
    k, t = at.shape
    s, _, n = d.shape
    nb = n // bn

    def body(a_ref, d_ref, o_ref):
        o_ref[0] = jnp.dot(a_ref[...], d_ref[0], preferred_element_type=f32).astype(out_dtype)

    return pl.pallas_call(
        body, name=name, grid=(s, nb), out_shape=jax.ShapeDtypeStruct((s * nb, k, bn), out_dtype),
        in_specs=[_resident((k, t)), pl.BlockSpec((1, t, bn), lambda i, j: (i, 0, j))],
        out_specs=pl.BlockSpec((1, k, bn), lambda i, j: (i * nb + j, 0, 0)), compiler_params=_cp("arbitrary", "arbitrary"),
    )(at, d)


def _pad_row(v, width=LANES):
    return jnp.pad(v.reshape(1, -1), ((0, 0), (0, width - v.size)))


def _quarters(a):
    return a.reshape(4, 2, a.shape[0] // 8, a.shape[1])


def _local_step(x, mod, target, w_pack, late, small, reducer=None):
    dtb_row, alog_row = _pad_row(small["dt_bias"]), _pad_row(small["a_log"])
    dskip_row = jnp.repeat(small["d_skip"].reshape(-1), HEAD_P).reshape(1, D)

    red = reducer

    def hosted(host, args, swap=None, scatter=None, gather=None, sums=()):
        if red is None:
            return host(*args)[0]
        riders = ([red.scatter(scatter)] if scatter else []) + ([red.swap(*swap)] if swap else [])
        riders += [_SwapSumsRider([red.sums[n] for n in sums])] if sums else []
        riders += [_GatherRider([gather[0]], *gather[1:])] if gather is not None else []
        both = _Riders(riders)
        outs, extra = host(*args, rider=both)
        extra = both.split(extra)
        if scatter:
            red.scattered(scatter, extra.pop(0))
        if swap:
            red.swapped(swap[0], extra.pop(0))
        if sums:
            red.others.update(zip(sums, extra.pop(0)))
        return (outs, extra[0][0]) if gather is not None else outs

    w_out, w_up, w_down = late
    if red is None:
        proj, h_t = hosted(_ln_inproj, (x, mod, small["norm1_w"], w_pack))
        xbc_act, = hosted(_ssd_conv_fwd, (proj, small["ssd_conv_w"], small["ssd_conv_b"]))
        y_scan, states = hosted(_ssd_fwd, (xbc_act, proj, dtb_row, alog_row))
        u_conv, = hosted(_glu_conv_fwd, (proj, small["conf_conv_w"], small["conf_conv_b"]))
    else:
        (proj, h_t), w_out = hosted(_ln_inproj, (x, mod, small["norm1_w"], w_pack), gather=(w_out,))
        (xbc_act,), w_up = hosted(_ssd_conv_fwd, (proj, small["ssd_conv_w"], small["ssd_conv_b"]), gather=(w_up, 0, UP_EARLY_ROWS))
        (y_scan, states), w_up = hosted(_ssd_fwd, (xbc_act, proj, dtb_row, alog_row), gather=(w_up, UP_EARLY_ROWS, None))
        (u_conv,), w_down = hosted(_glu_conv_fwd, (proj, small["conf_conv_w"], small["conf_conv_b"]), gather=(w_down,))
        w_out, w_up, w_down = w_out.reshape(2 * D, D), w_up.reshape(4, D, UP_SHARD), w_down.reshape(D_FF, D)
    y_ssd, y_ssd_t = _ssd_gate_norm(y_scan, xbc_act, proj, dskip_row, small["ssd_norm_w"])
    u, u_t = _ln_silu(u_conv, small["conf_ln_w"], small["conf_ln_b"])
    mix, x1, h2_t, up = _outproj_ln2_up(y_ssd, u, w_out, x, mod, small["norm2_w"], w_up)
    act, act_t = _ffn_conv_fwd(up, small["ffn_conv_w"], small["ffn_conv_b"])[0]
    dx2, d_ffn, d_act, st_down = _down_loss(act, w_down, x1, mod, small["final_norm_w"], target)

    g_down = _quarters(_wgrad(act_t, d_ffn, "wgrad_down"))
    d_up, dw_ffn, db_ffn = hosted(_ffn_conv_bwd, (up, small["ffn_conv_w"], small["ffn_conv_b"], d_act), swap=("w_down", g_down))
    g_up = _wgrad_stacked(h2_t, d_up, "wgrad_up", D_FF // 2).reshape(4, 2, D // 2, UP_SHARD)
    dx1, d_mix, d_yssd, d_u, st_up = hosted(_up_bwd, (d_up, w_up, x1, mod, small["norm2_w"], dx2, mix, w_out),
                                            scatter="w_down", swap=("w_up", g_up))
    g_out = _quarters(jnp.concatenate([_wgrad(y_ssd_t, d_mix, "wgrad_out_y"), _wgrad(u_t, d_mix, "wgrad_out_u")], axis=0))
    d_uconv, st_ln = _ln_silu_bwd(d_u, u_conv, small["conf_ln_w"], small["conf_ln_b"])
    d_conf, dw_conf, db_conf = hosted(_glu_conv_bwd, (proj, small["conf_conv_w"], d_uconv), scatter="w_up",
                                      swap=("w_out", g_out))
    d_y, d_z, st_gn = _ssd_gate_norm_bwd(d_yssd, y_scan, xbc_act, proj, dskip_row, small["ssd_norm_w"])
    d_xs, d_bc, d_dt, d_alog = hosted(_ssd_bwd, (xbc_act, proj, dtb_row, alog_row, states, d_y), scatter="w_out")
    d_xraw, dw_sx, db_sx = _ssd_conv_bwd_x(proj, small["ssd_conv_w"], small["ssd_conv_b"], d_xs, d_y, dskip_row)
    d_bcraw, dw_sbc, db_sbc = _ssd_conv_bwd_bc(proj, small["ssd_conv_w"], small["ssd_conv_b"], d_bc)
    g_in = _unpack_g_in(dict(
        z=_wgrad(h_t, d_z, "wgrad_in_z"), x=_wgrad(h_t, d_xraw, "wgrad_in_x"), bc=_wgrad(h_t, d_bcraw, "wgrad_in_bc"),
        conf=_wgrad_stacked(h_t, d_conf, "wgrad_in_conf", D), dt=_wgrad(h_t, d_dt, "wgrad_in_dt", bn=LANES)))
    g_in = g_in.reshape(4, 2, D // 2, W_IN_SHARD_PAD)
    grad_x, st_in = hosted(_inproj_bwd, (d_z, d_xraw, d_bcraw, d_conf, d_dt, w_pack, x, mod, small["norm1_w"], dx1),
                           swap=("w_in", g_in), sums=("w_out", "w_up", "w_down"))

    gsmall = _pack_small_grads(st_in, st_up, st_down, st_ln, st_gn, d_alog, dw_sx, dw_sbc, db_sx, db_sbc, dw_conf, db_conf,
                               dw_ffn, db_ffn)
    gbig = None if reducer is not None else dict(w_in=g_in, w_out=g_out, w_up=g_up, w_down=g_down)
    return st_down[2, 0], grad_x, gbig, gsmall


VECTORS = ("ada_b", "norm1_w", "ssd_conv_b", "dt_bias", "a_log", "d_skip", "ssd_norm_w", "conf_conv_b", "conf_ln_w",
           "conf_ln_b", "norm2_w", "ffn_conv_b", "final_norm_w")
VECTOR_SIZES = (6 * D, D, D_XBC, HEADS, HEADS, HEADS, D, D, D, D, D, 2 * D_FF, D)
CONVS = {"ssd_conv_w": (K_SSD, D_XBC), "conf_conv_w": (K_CONF, D), "ffn_conv_w": (K_FFN, 2 * D_FF)}


def _pack_rows(items):
    n = -(-sum(w for _, w in items) // (8 * LANES)) * LANES
    while True:
        fill, place = [0] * 8, {}
        for key, w in sorted(items, key=lambda kv: -kv[1]):
            rows = [r for r in range(8) if fill[r] + w <= n]
            if not rows:
                break
            place[key] = (rows[0], fill[rows[0]])
            fill[rows[0]] += w
        if len(place) == len(items):
            return n, place
        n += LANES


FRONT_N, FRONT = _pack_rows([("c", D)] + [((nm, j), cols // 4) for nm, (taps, cols) in CONVS.items() for j in range(taps)])
BACK_N, BACK = _pack_rows([(nm, -(-sz // LANES) * LANES) for nm, sz in zip(VECTORS, VECTOR_SIZES)]
                          + [((nm, j), cols) for nm, (taps, cols) in CONVS.items() for j in range(taps)] + [("loss", LANES)])
_VM = pltpu.CompilerParams(vmem_limit_bytes=VMEM_LIMIT)


def _pack_front(c, shards):
    def body(c_ref, *refs):
        o_ref = refs[-1]
        o_ref[...] = jnp.zeros_like(o_ref)
        r, o = FRONT["c"]
        o_ref[r:r + 1, o:o + D] = c_ref[...]
        for ref, (nm, (taps, cols)) in zip(refs, CONVS.items()):
            for j in range(taps):
                r, o = FRONT[(nm, j)]
                o_ref[r:r + 1, o:o + cols // 4] = ref[0, j:j + 1, :]

    return pl.pallas_call(body, name="pack_front", out_shape=jax.ShapeDtypeStruct((8, FRONT_N), f32),
                          compiler_params=_VM)(c, *shards)


def _unpack_front(got):
    def body(g_ref, c_ref, *outs):
        r, o = FRONT["c"]
        for d in range(8):
            c_ref[d:d + 1, :] = g_ref[8 * d + r:8 * d + r + 1, o:o + D]
        for ref, (nm, (taps, cols)) in zip(outs, CONVS.items()):
            cw = cols // 4
            for j in range(taps):
                r, o = FRONT[(nm, j)]
                for k in range(4):
                    ref[j:j + 1, k * cw:(k + 1) * cw] = g_ref[16 * k + r:16 * k + r + 1, o:o + cw]

    return pl.pallas_call(
        body, name="unpack_front", compiler_params=_VM,
        out_shape=(jax.ShapeDtypeStruct((8, D), f32),) + tuple(jax.ShapeDtypeStruct(tc, f32) for tc in CONVS.values()),
    )(got)


def _pack_small_grads(st_in, st_up, st_down, st_ln, st_gn, d_alog, dw_sx, dw_sbc, db_sx, db_sbc, dw_conf, db_conf, dw_ffn,
                      db_ffn):
    def body(in_ref, up_ref, dn_ref, ln_ref, gn_ref, al_ref, wx_ref, wbc_ref, bx_ref, bbc_ref, wc_ref, bc_ref, wf_ref, bf_ref,
             o_ref):
        def put(key, val, shift=0):
            r, o = BACK[key]
            o_ref[r:r + 1, o + shift:o + shift + val.shape[1]] = val

        o_ref[...] = jnp.zeros_like(o_ref)
        for i, piece in enumerate((in_ref[0:1, :], in_ref[1:2, :], up_ref[3:4, :], up_ref[0:1, :], up_ref[1:2, :],
                                   dn_ref[1:2, :])):
            put("ada_b", piece, i * D)
        put("norm1_w", in_ref[2:3, :])
        put("ssd_conv_b", bx_ref[...])
        put("ssd_conv_b", bbc_ref[...], D)
        put("dt_bias", in_ref[3:4, 0:LANES])
        put("a_log", al_ref[...])
        lane = lax.broadcasted_iota(jnp.int32, (1, LANES), 1)
        col = lax.broadcasted_iota(jnp.int32, (1, D), 1)
        per_col = gn_ref[1:2, :]
        d_skip = jnp.zeros((1, LANES), f32)
        for h in range(HEADS):
            in_head = jnp.logical_and(col >= h * HEAD_P, col < (h + 1) * HEAD_P)
            s = jnp.sum(jnp.where(in_head, per_col, 0.0), axis=1, keepdims=True)
            d_skip = d_skip + jnp.where(lane == h, s, 0.0)
        put("d_skip", d_skip)
        put("ssd_norm_w", gn_ref[0:1, :])
        put("conf_conv_b", bc_ref[...])
        put("conf_ln_w", ln_ref[0:1, :])
        put("conf_ln_b", ln_ref[1:2, :])
        put("norm2_w", up_ref[2:3, :])
        put("ffn_conv_b", bf_ref[0])
        put("ffn_conv_b", bf_ref[1], D_FF)
        put("final_norm_w", dn_ref[0:1, :])
        put("loss", dn_ref[2:3, 0:LANES])
        for j in range(K_SSD):
            put(("ssd_conv_w", j), wx_ref[j:j + 1, :])
            put(("ssd_conv_w", j), wbc_ref[j:j + 1, :], D)
        for j in range(K_CONF):
            put(("conf_conv_w", j), wc_ref[j:j + 1, :])
        for j in range(K_FFN):
            put(("ffn_conv_w", j), wf_ref[0, j:j + 1, :])
            put(("ffn_conv_w", j), wf_ref[1, j:j + 1, :], D_FF)

    return pl.pallas_call(body, name="pack_small_grads", out_shape=jax.ShapeDtypeStruct((8, BACK_N), f32), compiler_params=_VM)(
        st_in, st_up, st_down, st_ln, st_gn, d_alog, dw_sx, dw_sbc, db_sx, db_sbc, dw_conf, db_conf, dw_ffn, db_ffn)


def _small_adamw(got, chip, w, m, v):
    names = VECTORS + tuple(CONVS)
    n_par = len(names)

    def body(chip_ref, g_ref, *refs):
        ins, outs = refs[:3 * n_par], refs[3 * n_par:]
        dm_ref, loss_ref, outs = outs[0], outs[1], outs[2:]
        chip_id = chip_ref[0]

        def summed(key, width):
            r, o = BACK[key]
            s = g_ref[r:r + 1, o:o + width]
            for d in range(1, 8):
                s = s + g_ref[8 * d + r:8 * d + r + 1, o:o + width]
            return s

        def mine(full, cw):
            out = full[:, 0:cw]
            for k in range(1, 4):
                out = jnp.where(chip_id == k, full[:, k * cw:(k + 1) * cw], out)
            return out

        r, o = BACK["ada_b"]
        for d in range(8):
            dm_ref[d:d + 1, :] = mine(g_ref[8 * d + r:8 * d + r + 1, o:o + 6 * D], 6 * D // 4)
        loss_ref[...] = summed("loss", LANES)
        for i, (nm, size) in enumerate(zip(VECTORS, VECTOR_SIZES)):
            g = summed(nm, -(-size // LANES) * LANES)[:, 0:size]
            res = _adam_math(ins[3 * i][...], g, ins[3 * i + 1][...], ins[3 * i + 2][...])
            for ref, val in zip(outs[4 * i:4 * i + 4], (g,) + res):
                ref[...] = val
        for i, (nm, (taps, cols)) in enumerate(CONVS.items(), start=len(VECTORS)):
            for j in range(taps):
                g = mine(summed((nm, j), cols), cols // 4)
                res = _adam_math(ins[3 * i][0, j:j + 1, :], g, ins[3 * i + 1][0, j:j + 1, :], ins[3 * i + 2][0, j:j + 1, :])
                for ref, val in zip(outs[4 * i:4 * i + 4], (g,) + res):
                    ref[0, j:j + 1, :] = val

    params = [a[nm] for nm in names for a in (w, m, v)]
    whole = lambda s: pl.BlockSpec(s, lambda i, chip, nd=len(s): (0,) * nd)
    out_shape = [jax.ShapeDtypeStruct((8, 6 * D // 4), f32), jax.ShapeDtypeStruct((1, LANES), f32)]
    out_shape += [jax.ShapeDtypeStruct(w[nm].shape, f32) for nm in names for _ in range(4)]
    outs = pl.pallas_call(
        body, name="small_adamw", out_shape=tuple(out_shape), compiler_params=_VM,
        grid_spec=pltpu.PrefetchScalarGridSpec(
            num_scalar_prefetch=1, grid=(1,), in_specs=[whole(got.shape)] + [whole(p.shape) for p in params],
            out_specs=tuple(whole(s.shape) for s in out_shape)),
    )(_scalar(chip), got, *params)
    return outs[0], outs[1][0, 0], {nm: outs[2 + 4 * i:6 + 4 * i] for i, nm in enumerate(names)}


W_IN_COLS = 4624
W_IN_SHARD = W_IN_COLS // 4
W_IN_SHARD_PAD = 1280
_SEGMENTS = ((0, 1024, OFF_Z), (1024, 2560, OFF_XBC), (2560, 2576, OFF_DT), (2576, 3600, OFF_CA), (3600, 4624, OFF_CG))


def _in_pieces(bounds=()):
    out = []
    for k in range(4):
        s0, s1 = k * W_IN_SHARD, (k + 1) * W_IN_SHARD
        for lo, hi, off in _SEGMENTS:
            a, b = max(lo, s0), min(hi, s1)
            while a < b:
                p = off + a - lo
                e = min([b - a] + [c - p for c in bounds if c > p])
                out.append((k, a - s0, p, e))
                a += e
    return out


def _pack_w_in(shards):
    pieces = _in_pieces()

    def body(s_ref, o_ref):
        o_ref[:, OFF_DT:W_PACK] = jnp.zeros((TM, W_PACK - OFF_DT), MX)
        for k, c, p, n in pieces:
            o_ref[:, p:p + n] = s_ref[k, :, c:c + n]

    return pl.pallas_call(
        body, name="pack_w_in", grid=(D // TM,), out_shape=jax.ShapeDtypeStruct((D, W_PACK), MX),
        in_specs=[pl.BlockSpec((4, TM, W_IN_SHARD_PAD), lambda i: (0, i, 0))],
        out_specs=pl.BlockSpec((TM, W_PACK), lambda i: (i, 0)), compiler_params=_cp("arbitrary"),
    )(shards)


def _unpack_g_in(g):
    srcs = ((OFF_Z, D), (OFF_XBC, D), (OFF_XBC + D, 2 * CW), (OFF_CA, D), (OFF_CG, D), (OFF_DT, LANES))
    pieces = _in_pieces(tuple(o for o, _ in srcs) + tuple(o + n for o, n in srcs))

    def body(z_ref, x_ref, bc_ref, cf_ref, dt_ref, o_ref):
        read = (lambda lo, hi: z_ref[:, lo:hi], lambda lo, hi: x_ref[:, lo:hi], lambda lo, hi: bc_ref[:, lo:hi],
                lambda lo, hi: cf_ref[0, :, lo:hi], lambda lo, hi: cf_ref[1, :, lo:hi], lambda lo, hi: dt_ref[:, lo:hi])
        o_ref[:, :, W_IN_SHARD - 4:W_IN_SHARD_PAD] = jnp.zeros((4, TM, W_IN_SHARD_PAD - W_IN_SHARD + 4), MX)
        for k, c, p, n in pieces:
            i = [q for q, (o, w) in enumerate(srcs) if o <= p < o + w][0]
            o_ref[k, :, c:c + n] = read[i](p - srcs[i][0], p - srcs[i][0] + n)

    blk = lambda w: pl.BlockSpec((TM, w), lambda i: (i, 0))
    return pl.pallas_call(
        body, name="unpack_g_in", grid=(D // TM,), out_shape=jax.ShapeDtypeStruct((4, D, W_IN_SHARD_PAD), MX),
        in_specs=[blk(D), blk(D), blk(2 * CW), pl.BlockSpec((2, TM, D), lambda i: (0, i, 0)), blk(LANES)],
        out_specs=pl.BlockSpec((4, TM, W_IN_SHARD_PAD), lambda i: (0, i, 0)), compiler_params=_cp("arbitrary"),
    )(g["z"], g["x"], g["bc"], g["conf"], g["dt"])


def _scalar(v):
    return jnp.reshape(v, (1,)).astype(jnp.int32)


def _cast_into_slot(w, width, chip):
    r, c = w.shape
    h = r // 2
    tm = _row_tile(h)
    nj = h // tm

    def body(chip_ref, w_ref, o_ref):
        v = w_ref[...].astype(MX)
        o_ref[0, 0] = v if width == c else jnp.concatenate([v, jnp.zeros((tm, width - c), MX)], axis=1)

    return pl.pallas_call(
        body, name=f"cast_into_slot_{r}x{c}", out_shape=jax.ShapeDtypeStruct((4, 2, h, width), MX),
        grid_spec=pltpu.PrefetchScalarGridSpec(
            num_scalar_prefetch=1, grid=(2, nj),
            in_specs=[pl.BlockSpec((tm, c), lambda i, j, chip: (i * nj + j, 0))],
            out_specs=pl.BlockSpec((1, 1, tm, width), lambda i, j, chip: (chip[0], i, j, 0))),
        compiler_params=_cp("arbitrary", "arbitrary"),
    )(_scalar(chip), w)


def _columns_first(w):
    return jnp.transpose(w, (2, 0, 1))


def _cast_into_slot_w_in(w_t, chip):
    h = D // 2
    nj = h // TM
    pad = W_IN_SHARD_PAD - W_IN_SHARD

    def body(chip_ref, w_ref, o_ref):
        cols = jnp.concatenate([w_ref[:, 0, :], jnp.zeros((pad, TM), f32)], axis=0)
        o_ref[0, 0] = cols.T.astype(MX)

    return pl.pallas_call(
        body, name="cast_into_slot_w_in", out_shape=jax.ShapeDtypeStruct((4, 2, h, W_IN_SHARD_PAD), MX),
        grid_spec=pltpu.PrefetchScalarGridSpec(
            num_scalar_prefetch=1, grid=(2, nj),
            in_specs=[pl.BlockSpec((W_IN_SHARD, 1, TM), lambda i, j, chip: (0, 0, i * nj + j))],
            out_specs=pl.BlockSpec((1, 1, TM, W_IN_SHARD_PAD), lambda i, j, chip: (chip[0], i, j, 0))),
        compiler_params=_cp("arbitrary", "arbitrary"),
    )(_scalar(chip), w_t)


def _adamw_w_in(w_t, mine, other, m_t, v_t, core):
    h = D // 2
    nj = h // TM

    def body(core_ref, w_ref, a_ref, b_ref, m_ref, v_ref, g_ref, d_ref, nm_ref, nv_ref):
        g = jnp.where(pl.program_id(0) == core_ref[0], a_ref[...], b_ref[...]).T[0:W_IN_SHARD, :]
        g_ref[:, 0, :] = g
        d_ref[:, 0, :], nm_ref[:, 0, :], nv_ref[:, 0, :] = _adam_math(w_ref[:, 0, :], g, m_ref[:, 0, :], v_ref[:, 0, :])

    blk = pl.BlockSpec((W_IN_SHARD, 1, TM), lambda i, j, core: (0, 0, i * nj + j))
    gblk = pl.BlockSpec((TM, W_IN_SHARD_PAD), lambda i, j, core: (j, 0))
    return pl.pallas_call(
        body, name="adamw_w_in", out_shape=tuple([jax.ShapeDtypeStruct((W_IN_SHARD, 1, D), f32)] * 4),
        grid_spec=pltpu.PrefetchScalarGridSpec(
            num_scalar_prefetch=1, grid=(2, nj), in_specs=[blk, gblk, gblk, blk, blk], out_specs=(blk,) * 4),
        compiler_params=_cp("arbitrary", "arbitrary"),
    )(_scalar(core), w_t, mine, other, m_t, v_t)


ANY = pl.BlockSpec(memory_space=pl.ANY)


def _place():
    x, y, c = lax.axis_index("x"), lax.axis_index("y"), lax.axis_index("c")
    return x, y, c, [(1 - x, y), (x, 1 - y), (1 - x, 1 - y)]


def _gather_rows(block, rider=None):
    m_per, n = block.shape
    ri, ro = (len(rider.inputs), len(rider.out_shape)) if rider is not None else (0, 0)

    def body(x_ref, *refs):
        r_in, out_ref, r_out = refs[:ri], refs[ri], refs[ri + 1:ri + 1 + ro]
        send_sems, recv_sems, local_sem, *r_scr = refs[ri + 1 + ro:]
        x, y, c, chips = _place()
        me, sibling = (x, y, c), (x, y, 1 - c)

        def rows(px, py, pc):
            return out_ref.at[pl.ds((4 * px + 2 * py + pc) * m_per, m_per), :]

        def copy(k, blk, to, src=None):
            return pltpu.make_async_remote_copy(
                src_ref=rows(*blk) if src is None else src, dst_ref=rows(*blk), send_sem=send_sems.at[k],
                recv_sem=recv_sems.at[k], device_id=to, device_id_type=MESH)

        mine = pltpu.make_async_copy(x_ref, rows(*me), local_sem)
        mine.start()
        first = [copy(0, me, sibling, src=x_ref)]
        first += [copy(1 + j, me, (*chip, c), src=x_ref) for j, chip in enumerate(chips)]
        for cp in first:
            cp.start()
        if rider is not None:
            rider.start(r_in, r_out, r_scr)
        passed = [copy(4 + j, (*chip, c), sibling) for j, chip in enumerate(chips)]
        for j, chip in enumerate(chips):
            copy(1 + j, (*chip, c), me).wait_recv()
            passed[j].start()
        copy(0, sibling, me).wait_recv()
        for j, chip in enumerate(chips):
            copy(4 + j, (*chip, 1 - c), me).wait_recv()
        for cp in first + passed:
            cp.wait_send()
        mine.wait()
        if rider is not None:
            rider.finish(r_in, r_out, r_scr)

    vmem = pl.BlockSpec(memory_space=pltpu.VMEM)
    gathered = jax.ShapeDtypeStruct((8 * m_per, n), block.dtype)
    if rider is None:
        return pl.pallas_call(
            body, name=f"gather_rows_{m_per}x{n}", out_shape=gathered, in_specs=[vmem], out_specs=vmem,
            scratch_shapes=[pltpu.SemaphoreType.DMA((7,)), pltpu.SemaphoreType.DMA((7,)), pltpu.SemaphoreType.DMA],
            compiler_params=_VM)(block)
    outs = pl.pallas_call(
        body, name=f"gather_rows_{m_per}x{n}", out_shape=(gathered,) + tuple(rider.out_shape),
        in_specs=[vmem] + [ANY] * ri, out_specs=(vmem,) + (ANY,) * ro,
        input_output_aliases={1 + i: 1 + j for i, j in rider.aliases.items()},
        scratch_shapes=[pltpu.SemaphoreType.DMA((7,)), pltpu.SemaphoreType.DMA((7,)), pltpu.SemaphoreType.DMA] + list(rider.scratch),
        compiler_params=_VM)(block, *rider.inputs)
    return outs[0], tuple(outs[1:])


class _GatherRider:
    def __init__(self, slots, row0=0, nrows=None):
        n = len(slots)
        self.n = n
        self.rows = (row0, slots[0].shape[2] - row0 if nrows is None else nrows)
        self.inputs = list(slots)
        self.out_shape = [jax.ShapeDtypeStruct(s.shape, s.dtype) for s in slots]
        self.scratch = [pltpu.SemaphoreType.DMA((n, 6)), pltpu.SemaphoreType.DMA((n, 6))]
        self.aliases = {a: a for a in range(n)}

    def _copy(self, outs, sems, a, j, k, half, to):
        dst = outs[a].at[k, half, pl.ds(*self.rows)]
        return pltpu.make_async_remote_copy(src_ref=dst, dst_ref=dst, send_sem=sems[0].at[a, j], recv_sem=sems[1].at[a, j],
                                            device_id=to, device_id_type=MESH)

    def _first(self, outs, sems):
        x, y, c, chips = _place()
        return [self._copy(outs, sems, a, j, 2 * x + y, c, (*chip, c)) for a in range(self.n) for j, chip in enumerate(chips)]

    def start(self, ins, outs, sems):
        for cp in self._first(outs, sems):
            cp.start()

    def finish(self, ins, outs, sems):
        x, y, c, chips = _place()
        passed = []
        for a in range(self.n):
            for j, (px, py) in enumerate(chips):
                self._copy(outs, sems, a, j, 2 * px + py, c, (x, y, c)).wait_recv()
                fwd = self._copy(outs, sems, a, 3 + j, 2 * px + py, c, (x, y, 1 - c))
                fwd.start()
                passed.append(fwd)
        for a in range(self.n):
            for j, (px, py) in enumerate(chips):
                self._copy(outs, sems, a, 3 + j, 2 * px + py, 1 - c, (x, y, c)).wait_recv()
        for cp in self._first(outs, sems) + passed:
            cp.wait_send()


class _ScatterRider:
    def __init__(self, parts, row0=0, nrows=None):
        n = len(parts)
        self.n = n
        self.rows = (row0, parts[0].shape[1] - row0 if nrows is None else nrows)
        self.inputs = list(parts)
        self.out_shape = [jax.ShapeDtypeStruct((3, self.rows[1], p.shape[2]), p.dtype) for p in parts]
        self.scratch = [pltpu.SemaphoreType.DMA((n, 3)), pltpu.SemaphoreType.DMA((n, 3))]
        self.aliases = {}

    def _copies(self, ins, outs, sems):
        x, y, c, chips = _place()
        return [pltpu.make_async_remote_copy(
            src_ref=ins[a].at[2 * px + py, pl.ds(*self.rows)], dst_ref=outs[a].at[j], send_sem=sems[0].at[a, j],
            recv_sem=sems[1].at[a, j], device_id=(px, py, c), device_id_type=MESH)
            for a in range(self.n) for j, (px, py) in enumerate(chips)]

    def start(self, ins, outs, sems):
        for cp in self._copies(ins, outs, sems):
            cp.start()

    def finish(self, ins, outs, sems):
        for cp in self._copies(ins, outs, sems):
            cp.wait()


def _ride_alone(rider, name):
    n = len(rider.inputs)

    def body(*refs):
        ins, outs, sems = refs[:n], refs[n:n + len(rider.out_shape)], refs[n + len(rider.out_shape):]
        rider.start(ins, outs, sems)
        rider.finish(ins, outs, sems)

    return pl.pallas_call(
        body, name=name, out_shape=tuple(rider.out_shape), in_specs=[ANY] * n, out_specs=tuple([ANY] * len(rider.out_shape)),
        input_output_aliases=dict(rider.aliases), scratch_shapes=list(rider.scratch),
    )(*rider.inputs)


class _SwapRider:
    def __init__(self, grads):
        n = len(grads)
        self.n = n
        self.inputs = list(grads)
        self.out_shape = [jax.ShapeDtypeStruct((4,) + g.shape[2:], g.dtype) for g in grads]
        self.scratch = [pltpu.SemaphoreType.DMA((n, 4)), pltpu.SemaphoreType.DMA((n, 4))]
        self.aliases = {}

    def _copies(self, ins, outs, sems):
        x, y, c, _ = _place()
        return [pltpu.make_async_remote_copy(
            src_ref=ins[a].at[k, 1 - c], dst_ref=outs[a].at[k], send_sem=sems[0].at[a, k], recv_sem=sems[1].at[a, k],
            device_id=(x, y, 1 - c), device_id_type=MESH) for a in range(self.n) for k in range(4)]

    def start(self, ins, outs, sems):
        for cp in self._copies(ins, outs, sems):
            cp.start()

    def finish(self, ins, outs, sems):
        for cp in self._copies(ins, outs, sems):
            cp.wait()


class _Riders:
    def __init__(self, riders):
        self.riders = list(riders)
        self.inputs = [a for r in riders for a in r.inputs]
        self.out_shape = [s for r in riders for s in r.out_shape]
        self.scratch = [s for r in riders for s in r.scratch]
        self.aliases = {}
        i = o = 0
        for r in riders:
            self.aliases.update({i + a: o + b for a, b in r.aliases.items()})
            i, o = i + len(r.inputs), o + len(r.out_shape)

    def _each(self, ins, outs, sems):
        i = o = s = 0
        for r in self.riders:
            yield r, ins[i:i + len(r.inputs)], outs[o:o + len(r.out_shape)], sems[s:s + len(r.scratch)]
            i, o, s = i + len(r.inputs), o + len(r.out_shape), s + len(r.scratch)

    def start(self, ins, outs, sems):
        for r, a, b, c in self._each(ins, outs, sems):
            r.start(a, b, c)

    def finish(self, ins, outs, sems):
        for r, a, b, c in self._each(ins, outs, sems):
            r.finish(a, b, c)

    def split(self, outs):
        res, o = [], 0
        for r in self.riders:
            res.append(outs[o:o + len(r.out_shape)])
            o += len(r.out_shape)
        return res


class _Reducer:
    def __init__(self, chip, core):
        self.chip, self.core, self.grads, self.parts, self.sums, self.others = chip, core, {}, {}, {}, {}

    def swap(self, name, grad):
        self.grads[name] = grad
        return _SwapRider([grad])

    def swapped(self, name, got):
        self.parts[name] = _add_pair(self.grads[name], got[0], self.core, name)

    def scatter(self, name, row0=0, nrows=None):
        return _ScatterRider([self.parts[name]], row0, nrows)

    def scattered(self, name, others):
        self.sums[name] = _add_chips(self.parts[name], others[0], self.chip, name)


class _SwapSumsRider:
    def __init__(self, halves):
        n = len(halves)
        self.n = n
        self.inputs = list(halves)
        self.out_shape = [jax.ShapeDtypeStruct(s.shape, s.dtype) for s in halves]
        self.scratch = [pltpu.SemaphoreType.DMA((n,)), pltpu.SemaphoreType.DMA((n,))]
        self.aliases = {}

    def _copies(self, ins, outs, sems):
        x, y, c, _ = _place()
        return [pltpu.make_async_remote_copy(
            src_ref=ins[a], dst_ref=outs[a], send_sem=sems[0].at[a], recv_sem=sems[1].at[a],
            device_id=(x, y, 1 - c), device_id_type=MESH) for a in range(self.n)]

    def start(self, ins, outs, sems):
        for cp in self._copies(ins, outs, sems):
            cp.start()

    def finish(self, ins, outs, sems):
        for cp in self._copies(ins, outs, sems):
            cp.wait()


def _row_tile(r):
    for tm in (TM, 176, 128, 64, 32, 16, 8):
        if r % tm == 0:
            return tm
    return r


def _add_pair(mine, got, core, name):
    k, _, h, c = mine.shape
    tm = _row_tile(h)

    def body(core_ref, a_ref, b_ref, o_ref):
        o_ref[0] = (a_ref[0, 0].astype(f32) + b_ref[0].astype(f32)).astype(MX)

    blk = pl.BlockSpec((1, tm, c), lambda i, j, core: (i, j, 0))
    return pl.pallas_call(
        body, name="add_pair_" + name, out_shape=jax.ShapeDtypeStruct((k, h, c), MX),
        grid_spec=pltpu.PrefetchScalarGridSpec(
            num_scalar_prefetch=1, grid=(k, h // tm),
            in_specs=[pl.BlockSpec((1, 1, tm, c), lambda i, j, core: (i, core[0], j, 0)), blk], out_specs=blk),
        compiler_params=_cp("arbitrary", "arbitrary"),
    )(_scalar(core), mine, got)


def _add_chips(parts, others, chip, name, row0=0):
    _, n, c = others.shape
    tm = _row_tile(n)
    assert row0 % tm == 0
    i0 = row0 // tm

    def body(chip_ref, a_ref, b_ref, o_ref):
        s = a_ref[0].astype(f32) + b_ref[0].astype(f32)
        o_ref[...] = (s + b_ref[1].astype(f32)) + b_ref[2].astype(f32)

    return pl.pallas_call(
        body, name="add_chips_" + name, out_shape=jax.ShapeDtypeStruct((n, c), f32),
        grid_spec=pltpu.PrefetchScalarGridSpec(
            num_scalar_prefetch=1, grid=(n // tm,),
            in_specs=[pl.BlockSpec((1, tm, c), lambda i, chip: (chip[0], i0 + i, 0)),
                      pl.BlockSpec((3, tm, c), lambda i, chip: (0, i, 0))],
            out_specs=pl.BlockSpec((tm, c), lambda i, chip: (i, 0))),
        compiler_params=_cp("arbitrary"),
    )(_scalar(chip), parts, others)


def _adam_math(w, g, m, v):
    m = ADAM_B1 * m + (1.0 - ADAM_B1) * g
    v = ADAM_B2 * v + (1.0 - ADAM_B2) * (g * g)
    m_hat = m / (1.0 - ADAM_B1 ** ADAM_STEP)
    v_hat = v / (1.0 - ADAM_B2 ** ADAM_STEP)
    return -ADAM_LR * (m_hat / (jnp.sqrt(v_hat) + ADAM_EPS) + ADAM_WD * w), m, v


def _adamw_halves(w, mine, other, m, v, core, name, rider=None):
    r, c = w.shape
    h = r // 2
    tm = _row_tile(h)
    nj = h // tm
    cg = mine.shape[1]

    def body(core_ref, w_ref, a_ref, b_ref, m_ref, v_ref, g_ref, d_ref, nm_ref, nv_ref):
        g = jnp.where(pl.program_id(0) == core_ref[0], a_ref[:, 0:c], b_ref[:, 0:c])
        g_ref[...] = g
        d_ref[...], nm_ref[...], nv_ref[...] = _adam_math(w_ref[...], g, m_ref[...], v_ref[...])

    blk = pl.BlockSpec((tm, c), lambda i, j, core: (i * nj + j, 0))
    gblk = pl.BlockSpec((tm, cg), lambda i, j, core: (j, 0))
    return _call(body, name=name, grid=(2, nj), out_shape=[jax.ShapeDtypeStruct((r, c), f32)] * 4,
                 in_specs=[blk, gblk, gblk, blk, blk], out_specs=(blk,) * 4, sem=("arbitrary", "arbitrary"),
                 prefetch=(_scalar(core),), args=(w, mine, other, m, v), rider=rider)


def _ada_forward(c_all, ada_w):
    def body(c_ref, w_ref, o_ref):
        o_ref[...] = jnp.dot(_silu(c_ref[...]).astype(MX), w_ref[...].astype(MX), preferred_element_type=f32)

    return pl.pallas_call(body, name="ada_forward", out_shape=jax.ShapeDtypeStruct((8, ada_w.shape[1]), f32),
                          compiler_params=pltpu.CompilerParams(vmem_limit_bytes=VMEM_LIMIT))(c_all, ada_w)


def _ada_adamw(c_all_t, d_mod, w, m, v, rider=None):
    r, c = w.shape
    tm = TM

    def body(ct_ref, dm_ref, w_ref, m_ref, v_ref, g_ref, d_ref, nm_ref, nv_ref):
        ca = _silu(ct_ref[...])
        g = ca[:, 0:1] * dm_ref[0:1, :]
        for b in range(1, 8):
            g = g + ca[:, b:b + 1] * dm_ref[b:b + 1, :]
        g_ref[...] = g
        d_ref[...], nm_ref[...], nv_ref[...] = _adam_math(w_ref[...], g, m_ref[...], v_ref[...])

    blk = pl.BlockSpec((tm, c), lambda i: (i, 0))
    return _call(body, name="ada_adamw", grid=(r // tm,), out_shape=[jax.ShapeDtypeStruct((r, c), f32)] * 4,
                 in_specs=[pl.BlockSpec((tm, 8), lambda i: (i, 0)), pl.BlockSpec((8, c), lambda i: (0, 0)), blk, blk, blk],
                 out_specs=(blk,) * 4, sem=("arbitrary",), args=(c_all_t, d_mod, w, m, v), rider=rider)


WEIGHTS = ("ada_w", "ada_b", "norm1_w", "w_in", "ssd_conv_w", "ssd_conv_b", "dt_bias", "a_log", "d_skip", "ssd_norm_w",
           "conf_conv_w", "conf_conv_b", "conf_ln_w", "conf_ln_b", "w_out", "norm2_w", "w_up", "ffn_conv_w", "ffn_conv_b",
           "w_down", "final_norm_w")


def kernel(x, c, ada_w, ada_b, norm1_w, w_in, ssd_conv_w, ssd_conv_b, dt_bias, a_log, d_skip, ssd_norm_w, conf_conv_w, conf_conv_b, conf_ln_w, conf_ln_b, w_out, norm2_w, w_up, ffn_conv_w, ffn_conv_b, w_down, final_norm_w, loss_target, m_ada_w, m_ada_b, m_norm1_w, m_w_in, m_ssd_conv_w, m_ssd_conv_b, m_dt_bias, m_a_log, m_d_skip, m_ssd_norm_w, m_conf_conv_w, m_conf_conv_b, m_conf_ln_w, m_conf_ln_b, m_w_out, m_norm2_w, m_w_up, m_ffn_conv_w, m_ffn_conv_b, m_w_down, m_final_norm_w, v_ada_w, v_ada_b, v_norm1_w, v_w_in, v_ssd_conv_w, v_ssd_conv_b, v_dt_bias, v_a_log, v_d_skip, v_ssd_norm_w, v_conf_conv_w, v_conf_conv_b, v_conf_ln_w, v_conf_ln_b, v_w_out, v_norm2_w, v_w_up, v_ffn_conv_w, v_ffn_conv_b, v_w_down, v_final_norm_w):
    given = dict(locals())
    w = {n: given[n] for n in WEIGHTS}
    mom = {n: given["m_" + n] for n in WEIGHTS}
    var = {n: given["v_" + n] for n in WEIGHTS}
    chip = 2 * lax.axis_index("x") + lax.axis_index("y")
    me = 2 * chip + lax.axis_index("c")

    core = lax.axis_index("c")
    a_in = _cast_into_slot_w_in(_columns_first(w_in), chip)
    got, (a_in,) = _gather_rows(_pack_front(c, [w[n] for n in CONVS]), _GatherRider([a_in], 0, D // 4))
    c_all, *convs = _unpack_front(got)
    conv_full = dict(zip(CONVS, convs))

    got, (a_in,) = _gather_rows(_ada_forward(c_all, ada_w[0]), _GatherRider([a_in], D // 4, D // 4))
    mod_cols = got.reshape(8, 8, -1)[0::2]
    mod = lax.dynamic_index_in_dim(mod_cols, me, axis=1, keepdims=False).reshape(1, 6 * D) + ada_b
    w_pack = _pack_w_in(a_in.reshape(4, D, W_IN_SHARD_PAD))
    late = (_cast_into_slot(w_out[0], D, chip), _cast_into_slot(w_up[0], UP_SHARD, chip), _cast_into_slot(w_down[0], D, chip))

    flat = lambda a: a.reshape(1, -1) if a.ndim == 1 else a
    small = {n: flat(w[n]) for n in VECTORS if n != "ada_b"}
    small.update(conv_full)
    reducer = _Reducer(chip, core)
    _, grad_x, _, gsmall = _local_step(x[0], mod, loss_target[0], w_pack, late, small, reducer)
    grads, delta, new_m, new_v = {}, {}, {}, {}

    names = VECTORS + tuple(CONVS)
    got, others = _gather_rows(gsmall, reducer.scatter("w_in"))
    reducer.scattered("w_in", others)
    d_mod_mine, loss, res = _small_adamw(got, chip, *[{n: flat(d[n]) for n in names} for d in (w, mom, var)])
    for n in names:
        grads[n], delta[n], new_m[n], new_v[n] = [r.reshape(w[n].shape) for r in res[n]]

    reducer.others["w_in"], = _ride_alone(_SwapSumsRider([reducer.sums["w_in"]]), "swap_sums_w_in")
    res = _adamw_w_in(_columns_first(w_in), reducer.sums["w_in"], reducer.others["w_in"], _columns_first(m_w_in),
                      _columns_first(v_w_in), core)
    grads["w_in"], delta["w_in"], new_m["w_in"], new_v["w_in"] = [jnp.transpose(r, (1, 2, 0)) for r in res]
    for n in ("w_out", "w_up", "w_down"):
        res, _ = _adamw_halves(w[n][0], reducer.sums[n], reducer.others[n], mom[n][0], var[n][0], core, "adamw_" + n)
        grads[n], delta[n], new_m[n], new_v[n] = [r[None] for r in res]
    res, _ = _ada_adamw(c_all.T, d_mod_mine, ada_w[0], m_ada_w[0], v_ada_w[0])
    grads["ada_w"], delta["ada_w"], new_m["ada_w"], new_v["ada_w"] = [r[None] for r in res]

    return (loss, grad_x[None], *[grads[n] for n in WEIGHTS], *[delta[n] for n in WEIGHTS],
            *[new_m[n] for n in WEIGHTS], *[new_v[n] for n in WEIGHTS])
```

```python
import functools

import jax
import jax.numpy as jnp
from jax import lax
from jax.experimental import pallas as pl
from jax.experimental.pallas import tpu as pltpu

f32 = jnp.float32
MX = jnp.bfloat16

D = 1024
HEADS = 16
HEAD_P = 64
STATE_N = 128
D_XBC = 1536
D_FF = 2816
UP_SHARD = 2 * D_FF // 4
UP_EARLY_ROWS = 128
K_SSD, K_CONF, K_FFN = 4, 31, 3
CHUNK = 128
OFF_Z, OFF_XBC, OFF_CA, OFF_CG, OFF_DT = 0, 1024, 2560, 3584, 4608
W_PACK = 4736
TM = 256
CW = 256
RC = 64
LANES = 128
VMEM_LIMIT = 56 * 1024 * 1024

ADAM_LR, ADAM_B1, ADAM_B2, ADAM_EPS, ADAM_WD, ADAM_STEP = 0.001, 0.9, 0.999, 1e-08, 0.01, 10

MESH = pl.DeviceIdType.MESH


def _cp(*sem):
    return pltpu.CompilerParams(dimension_semantics=sem, vmem_limit_bytes=VMEM_LIMIT)


def _resident(shape):
    nd = len(shape)
    return pl.BlockSpec(shape, lambda *_: (0,) * nd, pipeline_mode=pl.Buffered(1))


def _row(width=D):
    return pl.BlockSpec((1, width), lambda *_: (0, 0))


def _call(body, *, name, grid, in_specs, out_specs, out_shape, args, sem, scratch_shapes=(), prefetch=(), rider=None):
    ni, no, ns, npf = len(in_specs), len(out_specs), len(scratch_shapes), len(prefetch)
    ri, ro = (len(rider.inputs), len(rider.out_shape)) if rider is not None else (0, 0)

    def full(*refs):
        pre, refs = refs[:npf], refs[npf:]
        base_in, r_in = refs[:ni], refs[ni:ni + ri]
        base_out, r_out = refs[ni + ri:ni + ri + no], refs[ni + ri + no:ni + ri + no + ro]
        base_scr, r_scr = refs[ni + ri + no + ro:ni + ri + no + ro + ns], refs[ni + ri + no + ro + ns:]
        if rider is None:
            return body(*pre, *base_in, *base_out, *base_scr)
        ids = [pl.program_id(a) for a in range(len(grid))]
        first = functools.reduce(jnp.logical_and, [i == 0 for i in ids])
        last = functools.reduce(jnp.logical_and, [i == g - 1 for i, g in zip(ids, grid)])

        @pl.when(first)
        def _():
            rider.start(r_in, r_out, r_scr)

        body(*pre, *base_in, *base_out, *base_scr)

        @pl.when(last)
        def _():
            rider.finish(r_in, r_out, r_scr)

    extra = dict(shapes=[], scratch=[], aliases={}, inputs=[]) if rider is None else dict(
        shapes=rider.out_shape, scratch=rider.scratch, inputs=rider.inputs,
        aliases={npf + ni + i: no + j for i, j in rider.aliases.items()})
    outs = pl.pallas_call(
        full, name=name, out_shape=tuple(out_shape) + tuple(extra["shapes"]), input_output_aliases=extra["aliases"],
        grid_spec=pltpu.PrefetchScalarGridSpec(
            num_scalar_prefetch=npf, grid=grid, in_specs=list(in_specs) + [ANY] * ri,
            out_specs=tuple(out_specs) + (ANY,) * ro, scratch_shapes=list(scratch_shapes) + list(extra["scratch"])),
        compiler_params=_cp(*sem),
    )(*prefetch, *args, *extra["inputs"])
    return tuple(outs[:no]), tuple(outs[no:])


def _silu(v):
    return v * jax.nn.sigmoid(v)


def _dsilu(v):
    s = jax.nn.sigmoid(v)
    return s * (1.0 + v * (1.0 - s))


def _softplus(v):
    return jnp.maximum(v, 0.0) + jnp.log1p(jnp.exp(-jnp.abs(v)))


def _mm(a, b):
    return jnp.dot(a.astype(MX), b.astype(MX), preferred_element_type=f32)


def _mm_nt(a, b):
    return lax.dot_general(a.astype(MX), b.astype(MX), (((1,), (1,)), ((), ())), preferred_element_type=f32)


def _mm_tn(a, b):
    return lax.dot_general(a.astype(MX), b.astype(MX), (((0,), (0,)), ((), ())), preferred_element_type=f32)


def _ln_inproj(x, mod, norm1_w, w_pack, rider=None):
    t = x.shape[0]

    def body(x_ref, mod_ref, nw_ref, w_ref, proj_ref, ht_ref):
        xv = x_ref[...]
        rstd = lax.rsqrt(jnp.mean(xv * xv, axis=-1, keepdims=True) + 1e-6)
        h = (xv * rstd * nw_ref[...]) * (1.0 + mod_ref[:, D:2 * D]) + mod_ref[:, 0:D]
        hb = h.astype(MX)
        ht_ref[...] = hb.T
        proj_ref[...] = jnp.dot(hb, w_ref[...], preferred_element_type=f32)

    return _call(
        body, name="ln_inproj", grid=(t // TM,),
        out_shape=(jax.ShapeDtypeStruct((t, W_PACK), f32), jax.ShapeDtypeStruct((D, t), MX)),
        in_specs=[pl.BlockSpec((TM, D), lambda i: (i, 0)), _row(6 * D), _row(), _resident((D, W_PACK))],
        out_specs=(pl.BlockSpec((TM, W_PACK), lambda i: (i, 0)), pl.BlockSpec((D, TM), lambda i: (0, i))),
        sem=("arbitrary",), args=(x, mod, norm1_w, w_pack), rider=rider)


def _ssd_gate_norm(y_scan, xbc_act, proj, d_skip_row, ssd_norm_w):
    t = y_scan.shape[0]

    def body(y_ref, xs_ref, z_ref, dsk_ref, nw_ref, o_ref, ot_ref):
        y = y_ref[...] + xs_ref[...] * dsk_ref[...]
        yz = y * _silu(z_ref[...])
        rstd = lax.rsqrt(jnp.mean(yz * yz, axis=-1, keepdims=True) + 1e-6)
        out = (yz * rstd * nw_ref[...]).astype(MX)
        o_ref[...] = out
        ot_ref[...] = out.T

    blk = pl.BlockSpec((TM, D), lambda i: (i, 0))
    return pl.pallas_call(
        body, name="ssd_gate_norm", grid=(t // TM,),
        out_shape=(jax.ShapeDtypeStruct((t, D), MX), jax.ShapeDtypeStruct((D, t), MX)),
        in_specs=[blk, blk, blk, _row(), _row()], out_specs=(blk, pl.BlockSpec((D, TM), lambda i: (0, i))),
        compiler_params=_cp("arbitrary"),
    )(y_scan, xbc_act, proj, d_skip_row, ssd_norm_w)


def _ln_silu(u_conv, ln_w, ln_b):
    t = u_conv.shape[0]

    def body(u_ref, w_ref, b_ref, o_ref, ot_ref):
        u = u_ref[...]
        mu = jnp.mean(u, axis=-1, keepdims=True)
        uc = u - mu
        rstd = lax.rsqrt(jnp.mean(uc * uc, axis=-1, keepdims=True) + 1e-5)
        out = _silu(uc * rstd * w_ref[...] + b_ref[...]).astype(MX)
        o_ref[...] = out
        ot_ref[...] = out.T

    blk = pl.BlockSpec((TM, D), lambda i: (i, 0))
    return pl.pallas_call(
        body, name="ln_silu", grid=(t // TM,),
        out_shape=(jax.ShapeDtypeStruct((t, D), MX), jax.ShapeDtypeStruct((D, t), MX)),
        in_specs=[blk, _row(), _row()], out_specs=(blk, pl.BlockSpec((D, TM), lambda i: (0, i))),
        compiler_params=_cp("arbitrary"),
    )(u_conv, ln_w, ln_b)


def _outproj_ln2_up(y_ssd, u, w_out, x, mod, norm2_w, w_up):
    t = x.shape[0]

    def body(y_ref, u_ref, wo_ref, x_ref, mod_ref, nw_ref, wu_ref, mix_ref, x1_ref, h2t_ref, up_ref):
        mix = jnp.dot(y_ref[...], wo_ref[0:D, :], preferred_element_type=f32)
        mix = mix + jnp.dot(u_ref[...], wo_ref[D:2 * D, :], preferred_element_type=f32)
        mix_ref[...] = mix
        x1 = x_ref[...] + mod_ref[:, 2 * D:3 * D] * mix
        x1_ref[...] = x1
        rstd = lax.rsqrt(jnp.mean(x1 * x1, axis=-1, keepdims=True) + 1e-6)
        h2 = ((x1 * rstd * nw_ref[...]) * (1.0 + mod_ref[:, 4 * D:5 * D]) + mod_ref[:, 3 * D:4 * D]).astype(MX)
        h2t_ref[...] = h2.T
        for k in range(4):
            up_ref[:, k * UP_SHARD:(k + 1) * UP_SHARD] = jnp.dot(h2, wu_ref[k], preferred_element_type=f32)

    blk = pl.BlockSpec((TM, D), lambda i: (i, 0))
    return pl.pallas_call(
        body, name="outproj_ln2_up", grid=(t // TM,),
        out_shape=(jax.ShapeDtypeStruct((t, D), f32), jax.ShapeDtypeStruct((t, D), f32),
                   jax.ShapeDtypeStruct((D, t), MX), jax.ShapeDtypeStruct((t, 2 * D_FF), f32)),
        in_specs=[blk, blk, _resident((2 * D, D)), blk, _row(6 * D), _row(), _resident((4, D, UP_SHARD))],
        out_specs=(blk, blk, pl.BlockSpec((D, TM), lambda i: (0, i)), pl.BlockSpec((TM, 2 * D_FF), lambda i: (i, 0))),
        compiler_params=_cp("arbitrary"),
    )(y_ssd, u, w_out, x, mod, norm2_w, w_up)


def _down_loss(act, w_down, x1, mod, final_norm_w, target):
    t = x1.shape[0]

    def body(a_ref, wd_ref, x1_ref, mod_ref, wf_ref, tgt_ref, dx2_ref, dffn_ref, dact_ref, st_ref):
        @pl.when(pl.program_id(0) == 0)
        def _():
            st_ref[...] = jnp.zeros_like(st_ref)

        g2 = mod_ref[:, 5 * D:6 * D]
        ffn = jnp.dot(a_ref[...], wd_ref[...], preferred_element_type=f32)
        x2 = x1_ref[...] + g2 * ffn
        rstd = lax.rsqrt(jnp.mean(x2 * x2, axis=-1, keepdims=True) + 1e-6)
        xh = x2 * rstd
        wf = wf_ref[...]
        err = xh * wf - tgt_ref[...]
        dy = err * (1.0 / D)
        dxh = dy * wf
        dx2 = rstd * (dxh - xh * jnp.mean(dxh * xh, axis=-1, keepdims=True))
        dx2_ref[...] = dx2
        dffn = (g2 * dx2).astype(MX)
        dffn_ref[...] = dffn
        dact_ref[...] = lax.dot_general(dffn, wd_ref[...], (((1,), (1,)), ((), ())), preferred_element_type=f32)
        st_ref[0:1, :] += jnp.sum(dy * xh, axis=0, keepdims=True)
        st_ref[1:2, :] += jnp.sum(dx2 * ffn, axis=0, keepdims=True)
        st_ref[2:3, :] += jnp.sum(0.5 * jnp.mean(err * err, axis=-1, keepdims=True), axis=0, keepdims=True)

    blk = pl.BlockSpec((TM, D), lambda i: (i, 0))
    ablk = pl.BlockSpec((TM, D_FF), lambda i: (i, 0))
    return pl.pallas_call(
        body, name="down_loss", grid=(t // TM,),
        out_shape=(jax.ShapeDtypeStruct((t, D), f32), jax.ShapeDtypeStruct((t, D), MX),
                   jax.ShapeDtypeStruct((t, D_FF), f32), jax.ShapeDtypeStruct((8, D), f32)),
        in_specs=[ablk, _resident((D_FF, D)), blk, _row(6 * D), _row(), blk],
        out_specs=(blk, blk, ablk, pl.BlockSpec((8, D), lambda i: (0, 0))),
        compiler_params=_cp("arbitrary"),
    )(act, w_down, x1, mod, final_norm_w, target)


def _pad_of(k):
    return 8 * ((k - 1 + 7) // 8)


def _causal_win(ref, r, t, pad):
    base = pl.multiple_of(r * RC, RC)
    prev = ref[pl.ds(pl.multiple_of(jnp.maximum(base - pad, 0), 8), pad), :]
    prev = jnp.where(r > 0, prev, 0.0)
    return jnp.concatenate([prev, ref[pl.ds(base, RC), :]], axis=0)


def _anti_win(ref, r, t, pad):
    base = pl.multiple_of(r * RC, RC)
    nxt = ref[pl.ds(pl.multiple_of(jnp.minimum(base + RC, t - pad), 8), pad), :]
    nxt = jnp.where(r < t // RC - 1, nxt, 0.0)
    return jnp.concatenate([ref[pl.ds(base, RC), :], nxt], axis=0)


def _shifted(win, offsets):
    for r in range(8):
        mine = [o for o in offsets if o % 8 == r]
        if mine:
            rolled = win if r == 0 else pltpu.roll(win, win.shape[0] - r, 0)
            for o in mine:
                yield o, rolled[o - r:o - r + RC, :]


def _conv_taps(win, w_ref, k, pad):
    first = pad - (k - 1)
    acc = None
    for o, rows in _shifted(win, range(first, first + k)):
        term = w_ref[o - first:o - first + 1, :] * rows
        acc = term if acc is None else acc + term
    return acc


def _corr_taps(win, w_ref, k):
    acc = None
    for o, rows in _shifted(win, range(k)):
        term = w_ref[k - 1 - o:k - o, :] * rows
        acc = term if acc is None else acc + term
    return acc


def _dw_accumulate(dw_scr, d, win, k, pad):
    first = pad - (k - 1)
    for o, rows in _shifted(win, range(first, first + k)):
        j = o - first
        prod = d * rows
        dw_scr[8 * j:8 * j + 8, :] += prod.reshape(RC // 8, 8, prod.shape[-1]).sum(axis=0)


def _dw_finish(dw_scr, dw_ref, k):
    for j in range(k):
        dw_ref[j:j + 1, :] = jnp.sum(dw_scr[8 * j:8 * j + 8, :], axis=0, keepdims=True)


def _rows8(v):
    return v.reshape(RC // 8, 8, v.shape[-1]).sum(axis=0)


def _ssd_conv_fwd(proj, conv_w, conv_b, rider=None):
    t = proj.shape[0]
    pad = _pad_of(K_SSD)
    c0 = OFF_XBC // CW

    def body(x_ref, w_ref, b_ref, o_ref):
        def step(r, carry):
            win = _causal_win(x_ref, r, t, pad)
            o_ref[pl.ds(pl.multiple_of(r * RC, RC), RC), :] = _silu(_conv_taps(win, w_ref, K_SSD, pad) + b_ref[...])
            return carry
        lax.fori_loop(0, t // RC, step, 0)

    return _call(
        body, name="ssd_conv_fwd", grid=(D_XBC // CW,), out_shape=(jax.ShapeDtypeStruct((t, D_XBC), f32),),
        in_specs=[pl.BlockSpec((t, CW), lambda j: (0, c0 + j)), pl.BlockSpec((K_SSD, CW), lambda j: (0, j)),
                  pl.BlockSpec((1, CW), lambda j: (0, j))],
        out_specs=(pl.BlockSpec((t, CW), lambda j: (0, j)),), sem=("arbitrary",), args=(proj, conv_w, conv_b), rider=rider)


def _glu_conv_fwd(proj, conv_w, conv_b, rider=None):
    t = proj.shape[0]
    pad = _pad_of(K_CONF)
    ca, cg = OFF_CA // CW, OFF_CG // CW

    def body(a_ref, g_ref, w_ref, b_ref, o_ref, v_scr):
        def glu(r, carry):
            rows = pl.ds(pl.multiple_of(r * RC, RC), RC)
            v_scr[rows, :] = a_ref[rows, :] * jax.nn.sigmoid(g_ref[rows, :])
            return carry
        lax.fori_loop(0, t // RC, glu, 0)

        def step(r, carry):
            win = _causal_win(v_scr, r, t, pad)
            o_ref[pl.ds(pl.multiple_of(r * RC, RC), RC), :] = _conv_taps(win, w_ref, K_CONF, pad) + b_ref[...]
            return carry
        lax.fori_loop(0, t // RC, step, 0)

    return _call(
        body, name="glu_conv_fwd", grid=(D // CW,), out_shape=(jax.ShapeDtypeStruct((t, D), f32),),
        in_specs=[pl.BlockSpec((t, CW), lambda j: (0, ca + j)), pl.BlockSpec((t, CW), lambda j: (0, cg + j)),
                  pl.BlockSpec((K_CONF, CW), lambda j: (0, j)), pl.BlockSpec((1, CW), lambda j: (0, j))],
        out_specs=(pl.BlockSpec((t, CW), lambda j: (0, j)),),
        scratch_shapes=[pltpu.VMEM((t, CW), f32)], sem=("arbitrary",), args=(proj, proj, conv_w, conv_b), rider=rider)


def _ffn_conv_fwd(up, conv_w, conv_b, rider=None):
    t = up.shape[0]
    pad = _pad_of(K_FFN)
    nb = D_FF // CW

    def body(g_ref, v_ref, wg_ref, wv_ref, bg_ref, bv_ref, o_ref, ot_ref):
        def step(r, carry):
            gc = _conv_taps(_causal_win(g_ref, r, t, pad), wg_ref, K_FFN, pad) + bg_ref[...]
            vc = _conv_taps(_causal_win(v_ref, r, t, pad), wv_ref, K_FFN, pad) + bv_ref[...]
            o_ref[pl.ds(pl.multiple_of(r * RC, RC), RC), :] = (_silu(gc) * vc).astype(MX)
            return carry
        lax.fori_loop(0, t // RC, step, 0)
        ot_ref[...] = o_ref[...].T

    return _call(
        body, name="ffn_conv_fwd", grid=(nb,),
        out_shape=(jax.ShapeDtypeStruct((t, D_FF), MX), jax.ShapeDtypeStruct((D_FF, t), MX)),
        in_specs=[pl.BlockSpec((t, CW), lambda j: (0, j)), pl.BlockSpec((t, CW), lambda j: (0, nb + j)),
                  pl.BlockSpec((K_FFN, CW), lambda j: (0, j)), pl.BlockSpec((K_FFN, CW), lambda j: (0, nb + j)),
                  pl.BlockSpec((1, CW), lambda j: (0, j)), pl.BlockSpec((1, CW), lambda j: (0, nb + j))],
        out_specs=(pl.BlockSpec((t, CW), lambda j: (0, j)), pl.BlockSpec((CW, t), lambda j: (j, 0))), sem=("arbitrary",),
        args=(up, up, conv_w, conv_w, conv_b, conv_b), rider=rider)


def _ffn_conv_bwd(up, conv_w, conv_b, d_act, rider=None):
    t = up.shape[0]
    pad = _pad_of(K_FFN)
    nb = D_FF // CW

    def body(g_ref, v_ref, wg_ref, wv_ref, bg_ref, bv_ref, da_ref, dup_ref, dw_ref, db_ref,
             dg_scr, dv_scr, dwg_scr, dwv_scr, db_scr):
        dwg_scr[...] = jnp.zeros_like(dwg_scr)
        dwv_scr[...] = jnp.zeros_like(dwv_scr)
        db_scr[...] = jnp.zeros_like(db_scr)

        def first(r, carry):
            rows = pl.ds(pl.multiple_of(r * RC, RC), RC)
            gwin = _causal_win(g_ref, r, t, pad)
            vwin = _causal_win(v_ref, r, t, pad)
            gc = _conv_taps(gwin, wg_ref, K_FFN, pad) + bg_ref[...]
            vc = _conv_taps(vwin, wv_ref, K_FFN, pad) + bv_ref[...]
            da = da_ref[rows, :]
            dgc = da * vc * _dsilu(gc)
            dvc = da * _silu(gc)
            dg_scr[rows, :] = dgc
            dv_scr[rows, :] = dvc
            _dw_accumulate(dwg_scr, dgc, gwin, K_FFN, pad)
            _dw_accumulate(dwv_scr, dvc, vwin, K_FFN, pad)
            db_scr[0:8, :] += _rows8(dgc)
            db_scr[8:16, :] += _rows8(dvc)
            return carry
        lax.fori_loop(0, t // RC, first, 0)

        def second(r, carry):
            rows = pl.ds(pl.multiple_of(r * RC, RC), RC)
            dup_ref[0, rows, :] = _corr_taps(_anti_win(dg_scr, r, t, pad), wg_ref, K_FFN).astype(MX)
            dup_ref[1, rows, :] = _corr_taps(_anti_win(dv_scr, r, t, pad), wv_ref, K_FFN).astype(MX)
            return carry
        lax.fori_loop(0, t // RC, second, 0)

        for j in range(K_FFN):
            dw_ref[0, j:j + 1, :] = jnp.sum(dwg_scr[8 * j:8 * j + 8, :], axis=0, keepdims=True)
            dw_ref[1, j:j + 1, :] = jnp.sum(dwv_scr[8 * j:8 * j + 8, :], axis=0, keepdims=True)
        db_ref[0] = jnp.sum(db_scr[0:8, :], axis=0, keepdims=True)
        db_ref[1] = jnp.sum(db_scr[8:16, :], axis=0, keepdims=True)

    return _call(
        body, name="ffn_conv_bwd", grid=(nb,),
        out_shape=(jax.ShapeDtypeStruct((2, t, D_FF), MX), jax.ShapeDtypeStruct((2, K_FFN, D_FF), f32),
                   jax.ShapeDtypeStruct((2, 1, D_FF), f32)),
        in_specs=[pl.BlockSpec((t, CW), lambda j: (0, j)), pl.BlockSpec((t, CW), lambda j: (0, nb + j)),
                  pl.BlockSpec((K_FFN, CW), lambda j: (0, j)), pl.BlockSpec((K_FFN, CW), lambda j: (0, nb + j)),
                  pl.BlockSpec((1, CW), lambda j: (0, j)), pl.BlockSpec((1, CW), lambda j: (0, nb + j)),
                  pl.BlockSpec((t, CW), lambda j: (0, j))],
        out_specs=(pl.BlockSpec((2, t, CW), lambda j: (0, 0, j)), pl.BlockSpec((2, K_FFN, CW), lambda j: (0, 0, j)),
                   pl.BlockSpec((2, 1, CW), lambda j: (0, 0, j))),
        scratch_shapes=[pltpu.VMEM((t, CW), f32), pltpu.VMEM((t, CW), f32), pltpu.VMEM((8 * K_FFN, CW), f32),
                        pltpu.VMEM((8 * K_FFN, CW), f32), pltpu.VMEM((16, CW), f32)],
        sem=("arbitrary",), args=(up, up, conv_w, conv_w, conv_b, conv_b, d_act), rider=rider)


def _glu_conv_bwd(proj, conv_w, d_uconv, rider=None):
    t = proj.shape[0]
    pad = _pad_of(K_CONF)
    ca, cg = OFF_CA // CW, OFF_CG // CW

    def body(a_ref, g_ref, w_ref, du_ref, dc_ref, dw_ref, db_ref, v_scr, dw_scr, db_scr):
        dw_scr[...] = jnp.zeros_like(dw_scr)
        db_scr[...] = jnp.zeros_like(db_scr)

        def glu(r, carry):
            rows = pl.ds(pl.multiple_of(r * RC, RC), RC)
            v_scr[rows, :] = a_ref[rows, :] * jax.nn.sigmoid(g_ref[rows, :])
            return carry
        lax.fori_loop(0, t // RC, glu, 0)

        def step(r, carry):
            rows = pl.ds(pl.multiple_of(r * RC, RC), RC)
            du = du_ref[rows, :]
            _dw_accumulate(dw_scr, du, _causal_win(v_scr, r, t, pad), K_CONF, pad)
            db_scr[...] += _rows8(du)
            dv = _corr_taps(_anti_win(du_ref, r, t, pad), w_ref, K_CONF)
            a = a_ref[rows, :]
            s = jax.nn.sigmoid(g_ref[rows, :])
            dc_ref[0, rows, :] = (dv * s).astype(MX)
            dc_ref[1, rows, :] = (dv * a * s * (1.0 - s)).astype(MX)
            return carry
        lax.fori_loop(0, t // RC, step, 0)
        _dw_finish(dw_scr, dw_ref, K_CONF)
        db_ref[...] = jnp.sum(db_scr[...], axis=0, keepdims=True)

    return _call(
        body, name="glu_conv_bwd", grid=(D // CW,),
        out_shape=(jax.ShapeDtypeStruct((2, t, D), MX), jax.ShapeDtypeStruct((K_CONF, D), f32),
                   jax.ShapeDtypeStruct((1, D), f32)),
        in_specs=[pl.BlockSpec((t, CW), lambda j: (0, ca + j)), pl.BlockSpec((t, CW), lambda j: (0, cg + j)),
                  pl.BlockSpec((K_CONF, CW), lambda j: (0, j)), pl.BlockSpec((t, CW), lambda j: (0, j))],
        out_specs=(pl.BlockSpec((2, t, CW), lambda j: (0, 0, j)), pl.BlockSpec((K_CONF, CW), lambda j: (0, j)),
                   pl.BlockSpec((1, CW), lambda j: (0, j))),
        scratch_shapes=[pltpu.VMEM((t, CW), f32), pltpu.VMEM((8 * K_CONF, CW), f32), pltpu.VMEM((8, CW), f32)],
        sem=("arbitrary",), args=(proj, proj, conv_w, d_uconv), rider=rider)


def _ssd_conv_bwd_x(proj, conv_w, conv_b, d_xs, d_y, d_skip_row):
    t = proj.shape[0]
    pad = _pad_of(K_SSD)
    c0 = OFF_XBC // CW

    def body(x_ref, w_ref, b_ref, dxs_ref, dy_ref, dsk_ref, draw_ref, dw_ref, db_ref, dp_scr, dw_scr, db_scr):
        dw_scr[...] = jnp.zeros_like(dw_scr)
        db_scr[...] = jnp.zeros_like(db_scr)

        def first(r, carry):
            rows = pl.ds(pl.multiple_of(r * RC, RC), RC)
            win = _causal_win(x_ref, r, t, pad)
            pre = _conv_taps(win, w_ref, K_SSD, pad) + b_ref[...]
            dpre = (dxs_ref[rows, :] + dy_ref[rows, :] * dsk_ref[...]) * _dsilu(pre)
            dp_scr[rows, :] = dpre
            _dw_accumulate(dw_scr, dpre, win, K_SSD, pad)
            db_scr[...] += _rows8(dpre)
            return carry
        lax.fori_loop(0, t // RC, first, 0)

        def second(r, carry):
            rows = pl.ds(pl.multiple_of(r * RC, RC), RC)
            draw_ref[rows, :] = _corr_taps(_anti_win(dp_scr, r, t, pad), w_ref, K_SSD).astype(MX)
            return carry
        lax.fori_loop(0, t // RC, second, 0)
        _dw_finish(dw_scr, dw_ref, K_SSD)
        db_ref[...] = jnp.sum(db_scr[...], axis=0, keepdims=True)

    cb = pl.BlockSpec((t, CW), lambda j: (0, j))
    return pl.pallas_call(
        body, name="ssd_conv_bwd_x", grid=(D // CW,),
        out_shape=(jax.ShapeDtypeStruct((t, D), MX), jax.ShapeDtypeStruct((K_SSD, D), f32),
                   jax.ShapeDtypeStruct((1, D), f32)),
        in_specs=[pl.BlockSpec((t, CW), lambda j: (0, c0 + j)), pl.BlockSpec((K_SSD, CW), lambda j: (0, j)),
                  pl.BlockSpec((1, CW), lambda j: (0, j)), cb, cb, pl.BlockSpec((1, CW), lambda j: (0, j))],
        out_specs=(cb, pl.BlockSpec((K_SSD, CW), lambda j: (0, j)), pl.BlockSpec((1, CW), lambda j: (0, j))),
        scratch_shapes=[pltpu.VMEM((t, CW), f32), pltpu.VMEM((8 * K_SSD, CW), f32), pltpu.VMEM((8, CW), f32)],
        compiler_params=_cp("arbitrary"),
    )(proj, conv_w, conv_b, d_xs, d_y, d_skip_row)


def _ssd_conv_bwd_bc(proj, conv_w, conv_b, d_bc):
    t = proj.shape[0]
    pad = _pad_of(K_SSD)
    c0 = (OFF_XBC + D) // CW
    w0 = D // CW

    def body(x_ref, w_ref, b_ref, dbc_ref, draw_ref, dw_ref, db_ref, dp_scr, dw_scr, db_scr):
        dw_scr[...] = jnp.zeros_like(dw_scr)
        db_scr[...] = jnp.zeros_like(db_scr)

        def first(r, carry):
            rows = pl.ds(pl.multiple_of(r * RC, RC), RC)
            win = _causal_win(x_ref, r, t, pad)
            pre = _conv_taps(win, w_ref, K_SSD, pad) + b_ref[...]
            dpre = dbc_ref[0, rows, :] * _dsilu(pre)
            dp_scr[rows, :] = dpre
            _dw_accumulate(dw_scr, dpre, win, K_SSD, pad)
            db_scr[...] += _rows8(dpre)
            return carry
        lax.fori_loop(0, t // RC, first, 0)

        def second(r, carry):
            rows = pl.ds(pl.multiple_of(r * RC, RC), RC)
            draw_ref[rows, :] = _corr_taps(_anti_win(dp_scr, r, t, pad), w_ref, K_SSD).astype(MX)
            return carry
        lax.fori_loop(0, t // RC, second, 0)
        _dw_finish(dw_scr, dw_ref, K_SSD)
        db_ref[...] = jnp.sum(db_scr[...], axis=0, keepdims=True)

    return pl.pallas_call(
        body, name="ssd_conv_bwd_bc", grid=(2,),
        out_shape=(jax.ShapeDtypeStruct((t, 2 * CW), MX), jax.ShapeDtypeStruct((K_SSD, 2 * CW), f32),
                   jax.ShapeDtypeStruct((1, 2 * CW), f32)),
        in_specs=[pl.BlockSpec((t, CW), lambda j: (0, c0 + j)), pl.BlockSpec((K_SSD, CW), lambda j: (0, w0 + j)),
                  pl.BlockSpec((1, CW), lambda j: (0, w0 + j)), pl.BlockSpec((1, t, CW), lambda j: (j, 0, 0))],
        out_specs=(pl.BlockSpec((t, CW), lambda j: (0, j)), pl.BlockSpec((K_SSD, CW), lambda j: (0, j)),
                   pl.BlockSpec((1, CW), lambda j: (0, j))),
        scratch_shapes=[pltpu.VMEM((t, CW), f32), pltpu.VMEM((8 * K_SSD, CW), f32), pltpu.VMEM((8, CW), f32)],
        compiler_params=_cp("arbitrary"),
    )(proj, conv_w, conv_b, d_bc)


def _chunk_masks():
    ii = lax.broadcasted_iota(jnp.int32, (CHUNK, CHUNK), 0)
    jj = lax.broadcasted_iota(jnp.int32, (CHUNK, CHUNK), 1)
    return ii == jj, jj <= ii, jj >= ii


def _to_row(col, eye):
    return jnp.sum(jnp.where(eye, col, 0.0), axis=0, keepdims=True)


def _to_col(row, eye):
    return jnp.sum(jnp.where(eye, row, 0.0), axis=1, keepdims=True)


def _head_decay(dt_h, a_h, eye, tril):
    a_row = _to_row(dt_h * a_h, eye)
    cs = jnp.sum(jnp.where(tril, a_row, 0.0), axis=1, keepdims=True)
    cs_row = _to_row(cs, eye)
    decay = jnp.where(tril, jnp.exp(jnp.where(tril, cs - cs_row, 0.0)), 0.0)
    total = jnp.sum(a_row, axis=1, keepdims=True)
    return cs, decay, total


SCAN_UNROLL = 4


def _unrolled_loop(n, step, init):
    unroll = min(SCAN_UNROLL, n)
    assert n % unroll == 0

    def trip(i, carry):
        for u in range(unroll):
            carry = step(unroll * i + u, carry)
        return carry
    return lax.fori_loop(0, n // unroll, trip, init)


def _lane_pick(mat, lane, which):
    return jnp.sum(jnp.where(lane == which, mat, 0.0), axis=1, keepdims=True)


def _ssd_fwd(xbc_act, proj, dt_bias_row, a_log_row, rider=None):
    t = xbc_act.shape[0]
    nc = t // CHUNK
    cb, cc, cdt = D // LANES, (D + 2 * STATE_N) // LANES, OFF_DT // LANES

    def body(x_ref, b_ref, c_ref, dt_ref, dtb_ref, alog_ref, y_ref, st_ref):
        j = pl.program_id(0)
        eye, tril, _ = _chunk_masks()
        lane = lax.broadcasted_iota(jnp.int32, (1, LANES), 1)
        first = lane < HEAD_P
        a_row = -jnp.exp(alog_ref[...])
        a_heads = [jnp.sum(jnp.where(lane == 2 * j + h, a_row, 0.0), axis=1, keepdims=True) for h in range(2)]

        def chunk(c, hprev):
            rows = pl.ds(pl.multiple_of(c * CHUNK, CHUNK), CHUNK)
            xv, bm, cm = x_ref[rows, :], b_ref[rows, :], c_ref[rows, :]
            dt = _softplus(dt_ref[rows, :] + dtb_ref[...])
            st_ref[c] = hprev
            g = _mm_nt(cm, bm)
            ch = _mm(cm, hprev)
            dts = [_lane_pick(dt, lane, 2 * j + h) for h in range(2)]
            xdt = xv * jnp.where(first, dts[0], dts[1])
            ys, hs = [], []
            for h in range(2):
                cs, decay, total = _head_decay(dts[h], a_heads[h], eye, tril)
                y = _mm(g * decay, xdt) + jnp.exp(cs) * ch
                s = _mm_tn(bm * jnp.exp(total - cs), xdt)
                ys.append(y)
                hs.append(jnp.exp(total) * hprev + s)
            y_ref[rows, :] = jnp.where(first, ys[0], ys[1])
            return jnp.where(first, hs[0], hs[1])

        _unrolled_loop(nc, chunk, jnp.zeros((STATE_N, LANES), f32))

    blk = lambda f: pl.BlockSpec((t, LANES), f)
    return _call(
        body, name="ssd_fwd", grid=(D // LANES,),
        out_shape=(jax.ShapeDtypeStruct((t, D), f32), jax.ShapeDtypeStruct((nc, STATE_N, D), f32)),
        in_specs=[blk(lambda j: (0, j)), blk(lambda j: (0, cb + j // 4)), blk(lambda j: (0, cc + j // 4)),
                  blk(lambda j: (0, cdt)), _row(LANES), _row(LANES)],
        out_specs=(blk(lambda j: (0, j)), pl.BlockSpec((nc, STATE_N, LANES), lambda j: (0, 0, j))),
        sem=("arbitrary",), args=(xbc_act, xbc_act, xbc_act, proj, dt_bias_row, a_log_row), rider=rider)


def _ssd_bwd(xbc_act, proj, dt_bias_row, a_log_row, states, d_y, rider=None):
    t = xbc_act.shape[0]
    nc = t // CHUNK
    cb, cc, cdt = D // LANES, (D + 2 * STATE_N) // LANES, OFF_DT // LANES

    def body(x_ref, b_ref, c_ref, dt_ref, dtb_ref, alog_ref, st_ref, dy_ref, dx_ref, dbc_ref, ddt_ref, da_ref):
        grp, p = pl.program_id(0), pl.program_id(1)
        j = 4 * grp + p
        eye, tril, triu = _chunk_masks()
        lane = lax.broadcasted_iota(jnp.int32, (1, LANES), 1)
        first = lane < HEAD_P
        last_row = lax.broadcasted_iota(jnp.int32, (CHUNK, 1), 0) == CHUNK - 1
        a_row = -jnp.exp(alog_ref[...])
        a_heads = [jnp.sum(jnp.where(lane == 2 * j + h, a_row, 0.0), axis=1, keepdims=True) for h in range(2)]

        @pl.when(p == 0)
        def _():
            dbc_ref[...] = jnp.zeros_like(dbc_ref)

        @pl.when(j == 0)
        def _():
            ddt_ref[...] = jnp.zeros_like(ddt_ref)
            da_ref[...] = jnp.zeros_like(da_ref)

        def chunk(i, dh):
            c = nc - 1 - i
            rows = pl.ds(pl.multiple_of(c * CHUNK, CHUNK), CHUNK)
            xv, bm, cm = x_ref[rows, :], b_ref[rows, :], c_ref[rows, :]
            dtr = dt_ref[rows, :] + dtb_ref[...]
            dt = _softplus(dtr)
            hprev = st_ref[c]
            dy = dy_ref[rows, :]
            g = _mm_nt(cm, bm)
            dts = [_lane_pick(dt, lane, 2 * j + h) for h in range(2)]
            xdt = xv * jnp.where(first, dts[0], dts[1])
            dxs, dhs = [], []
            db_sum, dc_sum = None, None
            ddt_mat = jnp.zeros((CHUNK, LANES), f32)
            da_acc = jnp.zeros((1, LANES), f32)
            for h in range(2):
                mine = first if h == 0 else jnp.logical_not(first)
                cs, decay, total = _head_decay(dts[h], a_heads[h], eye, tril)
                e_cs, e_tot = jnp.exp(cs), jnp.exp(total)
                dec_s = jnp.exp(total - cs)
                dyh = jnp.where(mine, dy, 0.0)
                xdth = jnp.where(mine, xdt, 0.0)
                dhh = jnp.where(mine, dh, 0.0)
                hph = jnp.where(mine, hprev, 0.0)
                m = g * decay
                dm = _mm_nt(dyh, xdth)
                dg = dm * decay
                w = dm * m
                bdec = bm * dec_s
                dxdt = _mm_tn(m, dyh) + _mm(bdec, dhh)
                dc_off = _mm_nt(dyh, hph) * e_cs
                db_s = _mm_nt(xdth, dhh) * dec_s
                dc_h = _mm(dg, bm) + dc_off
                db_h = _mm_tn(dg, cm) + db_s
                r_s = jnp.sum(db_s * bm, axis=1, keepdims=True)
                dtotal = jnp.sum(r_s, axis=0, keepdims=True) + e_tot * jnp.sum(
                    jnp.sum(dhh * hph, axis=1, keepdims=True), axis=0, keepdims=True)
                dcs = (jnp.sum(w, axis=1, keepdims=True) - _to_col(jnp.sum(w, axis=0, keepdims=True), eye)
                       + jnp.sum(dc_off * cm, axis=1, keepdims=True) - r_s + jnp.where(last_row, dtotal, 0.0))
                da_col = jnp.sum(jnp.where(triu, _to_row(dcs, eye), 0.0), axis=1, keepdims=True)
                ddt = da_col * a_heads[h] + jnp.sum(jnp.where(mine, dxdt * xv, 0.0), axis=1, keepdims=True)
                ddt_mat = ddt_mat + jnp.where(lane == 2 * j + h, ddt, 0.0)
                da_acc = da_acc + jnp.where(lane == 2 * j + h, jnp.sum(da_col * dts[h], axis=0, keepdims=True), 0.0)
                dxs.append(dxdt * dts[h])
                dhs.append(e_tot * dhh + _mm_tn(cm * e_cs, dyh))
                db_sum = db_h if db_sum is None else db_sum + db_h
                dc_sum = dc_h if dc_sum is None else dc_sum + dc_h
            dx_ref[rows, :] = jnp.where(first, dxs[0], dxs[1])
            dbc_ref[0, rows, :] += db_sum
            dbc_ref[1, rows, :] += dc_sum
            ddt_ref[rows, :] += ddt_mat * jax.nn.sigmoid(dtr)
            da_ref[...] += da_acc * a_row
            return jnp.where(first, dhs[0], dhs[1])

        _unrolled_loop(nc, chunk, jnp.zeros((STATE_N, LANES), f32))

    blk = lambda f: pl.BlockSpec((t, LANES), f)
    return _call(
        body, name="ssd_bwd", grid=(2, 4),
        out_shape=(jax.ShapeDtypeStruct((t, D), f32), jax.ShapeDtypeStruct((2, t, 2 * STATE_N), f32),
                   jax.ShapeDtypeStruct((t, LANES), f32), jax.ShapeDtypeStruct((1, LANES), f32)),
        in_specs=[blk(lambda g, p: (0, 4 * g + p)), blk(lambda g, p: (0, cb + g)), blk(lambda g, p: (0, cc + g)),
                  blk(lambda g, p: (0, cdt)), _row(LANES), _row(LANES),
                  pl.BlockSpec((nc, STATE_N, LANES), lambda g, p: (0, 0, 4 * g + p)), blk(lambda g, p: (0, 4 * g + p))],
        out_specs=(blk(lambda g, p: (0, 4 * g + p)), pl.BlockSpec((2, t, LANES), lambda g, p: (0, 0, g)),
                   blk(lambda g, p: (0, 0)), _row(LANES)),
        sem=("arbitrary", "arbitrary"), args=(xbc_act, xbc_act, xbc_act, proj, dt_bias_row, a_log_row, states, d_y),
        rider=rider)


def _up_bwd(d_up, w_up, x1, mod, norm2_w, dx2, mix, w_out, rider=None):
    t = x1.shape[0]

    def body(dup_ref, wu_ref, x1_ref, mod_ref, nw_ref, dx2_ref, mix_ref, wo_ref,
             dx1_ref, dmix_ref, dys_ref, du_ref, st_ref):
        @pl.when(pl.program_id(0) == 0)
        def _():
            st_ref[...] = jnp.zeros_like(st_ref)

        nt = (((1,), (1,)), ((), ()))
        dh = None
        for k in range(4):
            lo = (k % 2) * UP_SHARD
            part = lax.dot_general(dup_ref[k // 2, :, lo:lo + UP_SHARD], wu_ref[k], nt, preferred_element_type=f32)
            dh = part if dh is None else dh + part
        x1 = x1_ref[...]
        rstd = lax.rsqrt(jnp.mean(x1 * x1, axis=-1, keepdims=True) + 1e-6)
        xh = x1 * rstd
        nw = nw_ref[...]
        sc = 1.0 + mod_ref[:, 4 * D:5 * D]
        st_ref[0:1, :] += jnp.sum(dh, axis=0, keepdims=True)
        st_ref[1:2, :] += jnp.sum(dh * xh * nw, axis=0, keepdims=True)
        st_ref[2:3, :] += jnp.sum(dh * sc * xh, axis=0, keepdims=True)
        dxh = dh * sc * nw
        dx1 = dx2_ref[...] + rstd * (dxh - xh * jnp.mean(dxh * xh, axis=-1, keepdims=True))
        dx1_ref[...] = dx1
        st_ref[3:4, :] += jnp.sum(dx1 * mix_ref[...], axis=0, keepdims=True)
        dmix = (mod_ref[:, 2 * D:3 * D] * dx1).astype(MX)
        dmix_ref[...] = dmix
        dys_ref[...] = lax.dot_general(dmix, wo_ref[0:D, :], nt, preferred_element_type=f32)
        du_ref[...] = lax.dot_general(dmix, wo_ref[D:2 * D, :], nt, preferred_element_type=f32)

    blk = pl.BlockSpec((TM, D), lambda i: (i, 0))
    return _call(
        body, name="up_bwd", grid=(t // TM,),
        out_shape=(jax.ShapeDtypeStruct((t, D), f32), jax.ShapeDtypeStruct((t, D), MX),
                   jax.ShapeDtypeStruct((t, D), f32), jax.ShapeDtypeStruct((t, D), f32),
                   jax.ShapeDtypeStruct((8, D), f32)),
        in_specs=[pl.BlockSpec((2, TM, D_FF), lambda i: (0, i, 0)), _resident((4, D, UP_SHARD)), blk, _row(6 * D), _row(),
                  blk, blk, _resident((2 * D, D))],
        out_specs=(blk, blk, blk, blk, pl.BlockSpec((8, D), lambda i: (0, 0))),
        sem=("arbitrary",), args=(d_up, w_up, x1, mod, norm2_w, dx2, mix, w_out), rider=rider)


def _ln_silu_bwd(d_u, u_conv, ln_w, ln_b):
    t = d_u.shape[0]

    def body(du_ref, u_ref, w_ref, b_ref, o_ref, st_ref):
        @pl.when(pl.program_id(0) == 0)
        def _():
            st_ref[...] = jnp.zeros_like(st_ref)

        u = u_ref[...]
        mu = jnp.mean(u, axis=-1, keepdims=True)
        uc = u - mu
        rstd = lax.rsqrt(jnp.mean(uc * uc, axis=-1, keepdims=True) + 1e-5)
        n = uc * rstd
        w = w_ref[...]
        dl = du_ref[...] * _dsilu(n * w + b_ref[...])
        st_ref[0:1, :] += jnp.sum(dl * n, axis=0, keepdims=True)
        st_ref[1:2, :] += jnp.sum(dl, axis=0, keepdims=True)
        dn = dl * w
        o_ref[...] = rstd * (dn - jnp.mean(dn, axis=-1, keepdims=True) - n * jnp.mean(dn * n, axis=-1, keepdims=True))

    blk = pl.BlockSpec((TM, D), lambda i: (i, 0))
    return pl.pallas_call(
        body, name="ln_silu_bwd", grid=(t // TM,),
        out_shape=(jax.ShapeDtypeStruct((t, D), f32), jax.ShapeDtypeStruct((8, D), f32)),
        in_specs=[blk, blk, _row(), _row()], out_specs=(blk, pl.BlockSpec((8, D), lambda i: (0, 0))),
        compiler_params=_cp("arbitrary"),
    )(d_u, u_conv, ln_w, ln_b)


def _ssd_gate_norm_bwd(d_out, y_scan, xbc_act, proj, d_skip_row, ssd_norm_w):
    t = d_out.shape[0]

    def body(do_ref, y_ref, xs_ref, z_ref, dsk_ref, nw_ref, dy_ref, dz_ref, st_ref):
        @pl.when(pl.program_id(0) == 0)
        def _():
            st_ref[...] = jnp.zeros_like(st_ref)

        xs = xs_ref[...]
        y = y_ref[...] + xs * dsk_ref[...]
        z = z_ref[...]
        s = _silu(z)
        yz = y * s
        rstd = lax.rsqrt(jnp.mean(yz * yz, axis=-1, keepdims=True) + 1e-6)
        n = yz * rstd
        do = do_ref[...]
        st_ref[0:1, :] += jnp.sum(do * n, axis=0, keepdims=True)
        dn = do * nw_ref[...]
        dyz = rstd * (dn - n * jnp.mean(dn * n, axis=-1, keepdims=True))
        dy = dyz * s
        dy_ref[...] = dy
        dz_ref[...] = (dyz * y * _dsilu(z)).astype(MX)
        st_ref[1:2, :] += jnp.sum(dy * xs, axis=0, keepdims=True)

    blk = pl.BlockSpec((TM, D), lambda i: (i, 0))
    return pl.pallas_call(
        body, name="ssd_gate_norm_bwd", grid=(t // TM,),
        out_shape=(jax.ShapeDtypeStruct((t, D), f32), jax.ShapeDtypeStruct((t, D), MX), jax.ShapeDtypeStruct((8, D), f32)),
        in_specs=[blk, blk, blk, blk, _row(), _row()], out_specs=(blk, blk, pl.BlockSpec((8, D), lambda i: (0, 0))),
        compiler_params=_cp("arbitrary"),
    )(d_out, y_scan, xbc_act, proj, d_skip_row, ssd_norm_w)


def _inproj_bwd(d_z, d_xraw, d_bcraw, d_conf, d_dt, w_pack, x, mod, norm1_w, dx1, rider=None):
    t = x.shape[0]

    def body(dz_ref, dx_ref, dbc_ref, dcf_ref, ddt_ref, w_ref, x_ref, mod_ref, nw_ref, dx1_ref, gx_ref, st_ref):
        @pl.when(pl.program_id(0) == 0)
        def _():
            st_ref[...] = jnp.zeros_like(st_ref)

        nt = (((1,), (1,)), ((), ()))
        dot = lambda a, lo, hi: lax.dot_general(a, w_ref[:, lo:hi], nt, preferred_element_type=f32)
        dh = dot(dz_ref[...], OFF_Z, OFF_Z + D)
        dh = dh + dot(dx_ref[...], OFF_XBC, OFF_XBC + D)
        dh = dh + dot(dbc_ref[...], OFF_XBC + D, OFF_XBC + D_XBC)
        dh = dh + dot(dcf_ref[0], OFF_CA, OFF_CA + D)
        dh = dh + dot(dcf_ref[1], OFF_CG, OFF_CG + D)
        dh = dh + dot(ddt_ref[...].astype(MX), OFF_DT, OFF_DT + LANES)
        st_ref[3:4, 0:LANES] += jnp.sum(ddt_ref[...], axis=0, keepdims=True)
        xv = x_ref[...]
        rstd = lax.rsqrt(jnp.mean(xv * xv, axis=-1, keepdims=True) + 1e-6)
        xh = xv * rstd
        nw = nw_ref[...]
        sc = 1.0 + mod_ref[:, D:2 * D]
        st_ref[0:1, :] += jnp.sum(dh, axis=0, keepdims=True)
        st_ref[1:2, :] += jnp.sum(dh * xh * nw, axis=0, keepdims=True)
        st_ref[2:3, :] += jnp.sum(dh * sc * xh, axis=0, keepdims=True)
        dxh = dh * sc * nw
        gx_ref[...] = dx1_ref[...] + rstd * (dxh - xh * jnp.mean(dxh * xh, axis=-1, keepdims=True))

    blk = pl.BlockSpec((TM, D), lambda i: (i, 0))
    return _call(
        body, name="inproj_bwd", grid=(t // TM,),
        out_shape=(jax.ShapeDtypeStruct((t, D), f32), jax.ShapeDtypeStruct((8, D), f32)),
        in_specs=[blk, blk, pl.BlockSpec((TM, 2 * CW), lambda i: (i, 0)), pl.BlockSpec((2, TM, D), lambda i: (0, i, 0)),
                  pl.BlockSpec((TM, LANES), lambda i: (i, 0)), _resident((D, W_PACK)), blk, _row(6 * D), _row(), blk],
        out_specs=(blk, pl.BlockSpec((8, D), lambda i: (0, 0))),
        sem=("arbitrary",), args=(d_z, d_xraw, d_bcraw, d_conf, d_dt, w_pack, x, mod, norm1_w, dx1), rider=rider)


def _wgrad(at, d, name, bn=256):
    k, t = at.shape
    n = d.shape[1]
    out_dtype = MX

    def body(a_ref, d_ref, o_ref):
        o_ref[...] = jnp.dot(a_ref[...], d_ref[...].astype(MX), preferred_element_type=f32).astype(out_dtype)

    return pl.pallas_call(
        body, name=name, grid=(n // bn,), out_shape=jax.ShapeDtypeStruct((k, n), out_dtype),
        in_specs=[_resident((k, t)), pl.BlockSpec((t, bn), lambda j: (0, j))],
        out_specs=pl.BlockSpec((k, bn), lambda j: (0, j)), compiler_params=_cp("arbitrary"),
    )(at, d)


def _wgrad_stacked(at, d, name, bn):
    out_dtype = MX
    k, t = at.shape
    s, _, n = d.shape
    nb = n // bn

    def body(a_ref, d_ref, o_ref):
        o_ref[0] = jnp.dot(a_ref[...], d_ref[0], preferred_element_type=f32).astype(out_dtype)

    return pl.pallas_call(
        body, name=name, grid=(s, nb), out_shape=jax.ShapeDtypeStruct((s * nb, k, bn), out_dtype),
        in_specs=[_resident((k, t)), pl.BlockSpec((1, t, bn), lambda i, j: (i, 0, j))],
        out_specs=pl.BlockSpec((1, k, bn), lambda i, j: (i * nb + j, 0, 0)), compiler_params=_cp("arbitrary", "arbitrary"),
    )(at, d)


def _pad_row(v, width=LANES):
    return jnp.pad(v.reshape(1, -1), ((0, 0), (0, width - v.size)))


def _quarters(a):
    return a.reshape(4, 2, a.shape[0] // 8, a.shape[1])


def _local_step(x, mod, target, w_pack, late, small, reducer=None):
    dtb_row, alog_row = _pad_row(small["dt_bias"]), _pad_row(small["a_log"])
    dskip_row = jnp.repeat(small["d_skip"].reshape(-1), HEAD_P).reshape(1, D)

    red = reducer

    def hosted(host, args, swap=None, scatter=None, gather=None, sums=()):
        if red is None:
            return host(*args)[0]
        riders = ([red.scatter(scatter)] if scatter else []) + ([red.swap(*swap)] if swap else [])
        riders += [_SwapSumsRider([red.sums[n] for n in sums])] if sums else []
        riders += [_GatherRider([gather[0]], *gather[1:])] if gather is not None else []
        both = _Riders(riders)
        outs, extra = host(*args, rider=both)
        extra = both.split(extra)
        if scatter:
            red.scattered(scatter, extra.pop(0))
        if swap:
            red.swapped(swap[0], extra.pop(0))
        if sums:
            red.others.update(zip(sums, extra.pop(0)))
        return (outs, extra[0][0]) if gather is not None else outs

    w_out, w_up, w_down = late
    if red is None:
        proj, h_t = hosted(_ln_inproj, (x, mod, small["norm1_w"], w_pack))
        xbc_act, = hosted(_ssd_conv_fwd, (proj, small["ssd_conv_w"], small["ssd_conv_b"]))
        y_scan, states = hosted(_ssd_fwd, (xbc_act, proj, dtb_row, alog_row))
        u_conv, = hosted(_glu_conv_fwd, (proj, small["conf_conv_w"], small["conf_conv_b"]))
    else:
        (proj, h_t), w_out = hosted(_ln_inproj, (x, mod, small["norm1_w"], w_pack), gather=(w_out,))
        (xbc_act,), w_up = hosted(_ssd_conv_fwd, (proj, small["ssd_conv_w"], small["ssd_conv_b"]), gather=(w_up, 0, UP_EARLY_ROWS))
        (y_scan, states), w_up = hosted(_ssd_fwd, (xbc_act, proj, dtb_row, alog_row), gather=(w_up, UP_EARLY_ROWS, None))
        (u_conv,), w_down = hosted(_glu_conv_fwd, (proj, small["conf_conv_w"], small["conf_conv_b"]), gather=(w_down,))
        w_out, w_up, w_down = w_out.reshape(2 * D, D), w_up.reshape(4, D, UP_SHARD), w_down.reshape(D_FF, D)
    y_ssd, y_ssd_t = _ssd_gate_norm(y_scan, xbc_act, proj, dskip_row, small["ssd_norm_w"])
    u, u_t = _ln_silu(u_conv, small["conf_ln_w"], small["conf_ln_b"])
    mix, x1, h2_t, up = _outproj_ln2_up(y_ssd, u, w_out, x, mod, small["norm2_w"], w_up)
    act, act_t = _ffn_conv_fwd(up, small["ffn_conv_w"], small["ffn_conv_b"])[0]
    dx2, d_ffn, d_act, st_down = _down_loss(act, w_down, x1, mod, small["final_norm_w"], target)

    g_down = _quarters(_wgrad(act_t, d_ffn, "wgrad_down"))
    d_up, dw_ffn, db_ffn = hosted(_ffn_conv_bwd, (up, small["ffn_conv_w"], small["ffn_conv_b"], d_act), swap=("w_down", g_down))
    g_up = _wgrad_stacked(h2_t, d_up, "wgrad_up", D_FF // 2).reshape(4, 2, D // 2, UP_SHARD)
    dx1, d_mix, d_yssd, d_u, st_up = hosted(_up_bwd, (d_up, w_up, x1, mod, small["norm2_w"], dx2, mix, w_out),
                                            scatter="w_down", swap=("w_up", g_up))
    g_out = _quarters(jnp.concatenate([_wgrad(y_ssd_t, d_mix, "wgrad_out_y"), _wgrad(u_t, d_mix, "wgrad_out_u")], axis=0))
    d_uconv, st_ln = _ln_silu_bwd(d_u, u_conv, small["conf_ln_w"], small["conf_ln_b"])
    d_conf, dw_conf, db_conf = hosted(_glu_conv_bwd, (proj, small["conf_conv_w"], d_uconv), scatter="w_up",
                                      swap=("w_out", g_out))
    d_y, d_z, st_gn = _ssd_gate_norm_bwd(d_yssd, y_scan, xbc_act, proj, dskip_row, small["ssd_norm_w"])
    d_xs, d_bc, d_dt, d_alog = hosted(_ssd_bwd, (xbc_act, proj, dtb_row, alog_row, states, d_y), scatter="w_out")
    d_xraw, dw_sx, db_sx = _ssd_conv_bwd_x(proj, small["ssd_conv_w"], small["ssd_conv_b"], d_xs, d_y, dskip_row)
    d_bcraw, dw_sbc, db_sbc = _ssd_conv_bwd_bc(proj, small["ssd_conv_w"], small["ssd_conv_b"], d_bc)
    g_in = _unpack_g_in(dict(
        z=_wgrad(h_t, d_z, "wgrad_in_z"), x=_wgrad(h_t, d_xraw, "wgrad_in_x"), bc=_wgrad(h_t, d_bcraw, "wgrad_in_bc"),
        conf=_wgrad_stacked(h_t, d_conf, "wgrad_in_conf", D), dt=_wgrad(h_t, d_dt, "wgrad_in_dt", bn=LANES)))
    g_in = g_in.reshape(4, 2, D // 2, W_IN_SHARD_PAD)
    grad_x, st_in = hosted(_inproj_bwd, (d_z, d_xraw, d_bcraw, d_conf, d_dt, w_pack, x, mod, small["norm1_w"], dx1),
                           swap=("w_in", g_in), sums=("w_out", "w_up", "w_down"))

    gsmall = _pack_small_grads(st_in, st_up, st_down, st_ln, st_gn, d_alog, dw_sx, dw_sbc, db_sx, db_sbc, dw_conf, db_conf,
                               dw_ffn, db_ffn)
    gbig = None if reducer is not None else dict(w_in=g_in, w_out=g_out, w_up=g_up, w_down=g_down)
    return st_down[2, 0], grad_x, gbig, gsmall


VECTORS = ("ada_b", "norm1_w", "ssd_conv_b", "dt_bias", "a_log", "d_skip", "ssd_norm_w", "conf_conv_b", "conf_ln_w",
           "conf_ln_b", "norm2_w", "ffn_conv_b", "final_norm_w")
VECTOR_SIZES = (6 * D, D, D_XBC, HEADS, HEADS, HEADS, D, D, D, D, D, 2 * D_FF, D)
CONVS = {"ssd_conv_w": (K_SSD, D_XBC), "conf_conv_w": (K_CONF, D), "ffn_conv_w": (K_FFN, 2 * D_FF)}


def _pack_rows(items):
    n = -(-sum(w for _, w in items) // (8 * LANES)) * LANES
    while True:
        fill, place = [0] * 8, {}
        for key, w in sorted(items, key=lambda kv: -kv[1]):
            rows = [r for r in range(8) if fill[r] + w <= n]
            if not rows:
                break
            place[key] = (rows[0], fill[rows[0]])
            fill[rows[0]] += w
        if len(place) == len(items):
            return n, place
        n += LANES


FRONT_N, FRONT = _pack_rows([("c", D)] + [((nm, j), cols // 4) for nm, (taps, cols) in CONVS.items() for j in range(taps)])
BACK_N, BACK = _pack_rows([(nm, -(-sz // LANES) * LANES) for nm, sz in zip(VECTORS, VECTOR_SIZES)]
                          + [((nm, j), cols) for nm, (taps, cols) in CONVS.items() for j in range(taps)] + [("loss", LANES)])
_VM = pltpu.CompilerParams(vmem_limit_bytes=VMEM_LIMIT)


def _pack_front(c, shards):
    def body(c_ref, *refs):
        o_ref = refs[-1]
        o_ref[...] = jnp.zeros_like(o_ref)
        r, o = FRONT["c"]
        o_ref[r:r + 1, o:o + D] = c_ref[...]
        for ref, (nm, (taps, cols)) in zip(refs, CONVS.items()):
            for j in range(taps):
                r, o = FRONT[(nm, j)]
                o_ref[r:r + 1, o:o + cols // 4] = ref[0, j:j + 1, :]

    return pl.pallas_call(body, name="pack_front", out_shape=jax.ShapeDtypeStruct((8, FRONT_N), f32),
                          compiler_params=_VM)(c, *shards)


def _unpack_front(got):
    def body(g_ref, c_ref, *outs):
        r, o = FRONT["c"]
        for d in range(8):
            c_ref[d:d + 1, :] = g_ref[8 * d + r:8 * d + r + 1, o:o + D]
        for ref, (nm, (taps, cols)) in zip(outs, CONVS.items()):
            cw = cols // 4
            for j in range(taps):
                r, o = FRONT[(nm, j)]
                for k in range(4):
                    ref[j:j + 1, k * cw:(k + 1) * cw] = g_ref[16 * k + r:16 * k + r + 1, o:o + cw]

    return pl.pallas_call(
        body, name="unpack_front", compiler_params=_VM,
        out_shape=(jax.ShapeDtypeStruct((8, D), f32),) + tuple(jax.ShapeDtypeStruct(tc, f32) for tc in CONVS.values()),
    )(got)


def _pack_small_grads(st_in, st_up, st_down, st_ln, st_gn, d_alog, dw_sx, dw_sbc, db_sx, db_sbc, dw_conf, db_conf, dw_ffn,
                      db_ffn):
    def body(in_ref, up_ref, dn_ref, ln_ref, gn_ref, al_ref, wx_ref, wbc_ref, bx_ref, bbc_ref, wc_ref, bc_ref, wf_ref, bf_ref,
             o_ref):
        def put(key, val, shift=0):
            r, o = BACK[key]
            o_ref[r:r + 1, o + shift:o + shift + val.shape[1]] = val

        o_ref[...] = jnp.zeros_like(o_ref)
        for i, piece in enumerate((in_ref[0:1, :], in_ref[1:2, :], up_ref[3:4, :], up_ref[0:1, :], up_ref[1:2, :],
                                   dn_ref[1:2, :])):
            put("ada_b", piece, i * D)
        put("norm1_w", in_ref[2:3, :])
        put("ssd_conv_b", bx_ref[...])
        put("ssd_conv_b", bbc_ref[...], D)
        put("dt_bias", in_ref[3:4, 0:LANES])
        put("a_log", al_ref[...])
        lane = lax.broadcasted_iota(jnp.int32, (1, LANES), 1)
        col = lax.broadcasted_iota(jnp.int32, (1, D), 1)
        per_col = gn_ref[1:2, :]
        d_skip = jnp.zeros((1, LANES), f32)
        for h in range(HEADS):
            in_head = jnp.logical_and(col >= h * HEAD_P, col < (h + 1) * HEAD_P)
            s = jnp.sum(jnp.where(in_head, per_col, 0.0), axis=1, keepdims=True)
            d_skip = d_skip + jnp.where(lane == h, s, 0.0)
        put("d_skip", d_skip)
        put("ssd_norm_w", gn_ref[0:1, :])
        put("conf_conv_b", bc_ref[...])
        put("conf_ln_w", ln_ref[0:1, :])
        put("conf_ln_b", ln_ref[1:2, :])
        put("norm2_w", up_ref[2:3, :])
        put("ffn_conv_b", bf_ref[0])
        put("ffn_conv_b", bf_ref[1], D_FF)
        put("final_norm_w", dn_ref[0:1, :])
        put("loss", dn_ref[2:3, 0:LANES])
        for j in range(K_SSD):
            put(("ssd_conv_w", j), wx_ref[j:j + 1, :])
            put(("ssd_conv_w", j), wbc_ref[j:j + 1, :], D)
        for j in range(K_CONF):
            put(("conf_conv_w", j), wc_ref[j:j + 1, :])
        for j in range(K_FFN):
            put(("ffn_conv_w", j), wf_ref[0, j:j + 1, :])
            put(("ffn_conv_w", j), wf_ref[1, j:j + 1, :], D_FF)

    return pl.pallas_call(body, name="pack_small_grads", out_shape=jax.ShapeDtypeStruct((8, BACK_N), f32), compiler_params=_VM)(
        st_in, st_up, st_down, st_ln, st_gn, d_alog, dw_sx, dw_sbc, db_sx, db_sbc, dw_conf, db_conf, dw_ffn, db_ffn)


def _small_adamw(got, chip, w, m, v):
    names = VECTORS + tuple(CONVS)
    n_par = len(names)

    def body(chip_ref, g_ref, *refs):
        ins, outs = refs[:3 * n_par], refs[3 * n_par:]
        dm_ref, loss_ref, outs = outs[0], outs[1], outs[2:]
        chip_id = chip_ref[0]

        def summed(key, width):
            r, o = BACK[key]
            s = g_ref[r:r + 1, o:o + width]
            for d in range(1, 8):
                s = s + g_ref[8 * d + r:8 * d + r + 1, o:o + width]
            return s

        def mine(full, cw):
            out = full[:, 0:cw]
            for k in range(1, 4):
                out = jnp.where(chip_id == k, full[:, k * cw:(k + 1) * cw], out)
            return out

        r, o = BACK["ada_b"]
        for d in range(8):
            dm_ref[d:d + 1, :] = mine(g_ref[8 * d + r:8 * d + r + 1, o:o + 6 * D], 6 * D // 4)
        loss_ref[...] = summed("loss", LANES)
        for i, (nm, size) in enumerate(zip(VECTORS, VECTOR_SIZES)):
            g = summed(nm, -(-size // LANES) * LANES)[:, 0:size]
            res = _adam_math(ins[3 * i][...], g, ins[3 * i + 1][...], ins[3 * i + 2][...])
            for ref, val in zip(outs[4 * i:4 * i + 4], (g,) + res):
                ref[...] = val
        for i, (nm, (taps, cols)) in enumerate(CONVS.items(), start=len(VECTORS)):
            for j in range(taps):
                g = mine(summed((nm, j), cols), cols // 4)
                res = _adam_math(ins[3 * i][0, j:j + 1, :], g, ins[3 * i + 1][0, j:j + 1, :], ins[3 * i + 2][0, j:j + 1, :])
                for ref, val in zip(outs[4 * i:4 * i + 4], (g,) + res):
                    ref[0, j:j + 1, :] = val

    params = [a[nm] for nm in names for a in (w, m, v)]
    whole = lambda s: pl.BlockSpec(s, lambda i, chip, nd=len(s): (0,) * nd)
    out_shape = [jax.ShapeDtypeStruct((8, 6 * D // 4), f32), jax.ShapeDtypeStruct((1, LANES), f32)]
    out_shape += [jax.ShapeDtypeStruct(w[nm].shape, f32) for nm in names for _ in range(4)]
    outs = pl.pallas_call(
        body, name="small_adamw", out_shape=tuple(out_shape), compiler_params=_VM,
        grid_spec=pltpu.PrefetchScalarGridSpec(
            num_scalar_prefetch=1, grid=(1,), in_specs=[whole(got.shape)] + [whole(p.shape) for p in params],
            out_specs=tuple(whole(s.shape) for s in out_shape)),
    )(_scalar(chip), got, *params)
    return outs[0], outs[1][0, 0], {nm: outs[2 + 4 * i:6 + 4 * i] for i, nm in enumerate(names)}


W_IN_COLS = 4624
W_IN_SHARD = W_IN_COLS // 4
W_IN_SHARD_PAD = 1280
_SEGMENTS = ((0, 1024, OFF_Z), (1024, 2560, OFF_XBC), (2560, 2576, OFF_DT), (2576, 3600, OFF_CA), (3600, 4624, OFF_CG))


def _in_pieces(bounds=()):
    out = []
    for k in range(4):
        s0, s1 = k * W_IN_SHARD, (k + 1) * W_IN_SHARD
        for lo, hi, off in _SEGMENTS:
            a, b = max(lo, s0), min(hi, s1)
            while a < b:
                p = off + a - lo
                e = min([b - a] + [c - p for c in bounds if c > p])
                out.append((k, a - s0, p, e))
                a += e
    return out


def _pack_w_in(shards):
    pieces = _in_pieces()

    def body(s_ref, o_ref):
        o_ref[:, OFF_DT:W_PACK] = jnp.zeros((TM, W_PACK - OFF_DT), MX)
        for k, c, p, n in pieces:
            o_ref[:, p:p + n] = s_ref[k, :, c:c + n]

    return pl.pallas_call(
        body, name="pack_w_in", grid=(D // TM,), out_shape=jax.ShapeDtypeStruct((D, W_PACK), MX),
        in_specs=[pl.BlockSpec((4, TM, W_IN_SHARD_PAD), lambda i: (0, i, 0))],
        out_specs=pl.BlockSpec((TM, W_PACK), lambda i: (i, 0)), compiler_params=_cp("arbitrary"),
    )(shards)


def _unpack_g_in(g):
    srcs = ((OFF_Z, D), (OFF_XBC, D), (OFF_XBC + D, 2 * CW), (OFF_CA, D), (OFF_CG, D), (OFF_DT, LANES))
    pieces = _in_pieces(tuple(o for o, _ in srcs) + tuple(o + n for o, n in srcs))

    def body(z_ref, x_ref, bc_ref, cf_ref, dt_ref, o_ref):
        read = (lambda lo, hi: z_ref[:, lo:hi], lambda lo, hi: x_ref[:, lo:hi], lambda lo, hi: bc_ref[:, lo:hi],
                lambda lo, hi: cf_ref[0, :, lo:hi], lambda lo, hi: cf_ref[1, :, lo:hi], lambda lo, hi: dt_ref[:, lo:hi])
        o_ref[:, :, W_IN_SHARD - 4:W_IN_SHARD_PAD] = jnp.zeros((4, TM, W_IN_SHARD_PAD - W_IN_SHARD + 4), MX)
        for k, c, p, n in pieces:
            i = [q for q, (o, w) in enumerate(srcs) if o <= p < o + w][0]
            o_ref[k, :, c:c + n] = read[i](p - srcs[i][0], p - srcs[i][0] + n)

    blk = lambda w: pl.BlockSpec((TM, w), lambda i: (i, 0))
    return pl.pallas_call(
        body, name="unpack_g_in", grid=(D // TM,), out_shape=jax.ShapeDtypeStruct((4, D, W_IN_SHARD_PAD), MX),
        in_specs=[blk(D), blk(D), blk(2 * CW), pl.BlockSpec((2, TM, D), lambda i: (0, i, 0)), blk(LANES)],
        out_specs=pl.BlockSpec((4, TM, W_IN_SHARD_PAD), lambda i: (0, i, 0)), compiler_params=_cp("arbitrary"),
    )(g["z"], g["x"], g["bc"], g["conf"], g["dt"])


def _scalar(v):
    return jnp.reshape(v, (1,)).astype(jnp.int32)


def _cast_into_slot(w, width, chip):
    r, c = w.shape
    h = r // 2
    tm = _row_tile(h)
    nj = h // tm

    def body(chip_ref, w_ref, o_ref):
        v = w_ref[...].astype(MX)
        o_ref[0, 0] = v if width == c else jnp.concatenate([v, jnp.zeros((tm, width - c), MX)], axis=1)

    return pl.pallas_call(
        body, name=f"cast_into_slot_{r}x{c}", out_shape=jax.ShapeDtypeStruct((4, 2, h, width), MX),
        grid_spec=pltpu.PrefetchScalarGridSpec(
            num_scalar_prefetch=1, grid=(2, nj),
            in_specs=[pl.BlockSpec((tm, c), lambda i, j, chip: (i * nj + j, 0))],
            out_specs=pl.BlockSpec((1, 1, tm, width), lambda i, j, chip: (chip[0], i, j, 0))),
        compiler_params=_cp("arbitrary", "arbitrary"),
    )(_scalar(chip), w)


def _columns_first(w):
    return jnp.transpose(w, (2, 0, 1))


def _cast_into_slot_w_in(w_t, chip):
    h = D // 2
    nj = h // TM
    pad = W_IN_SHARD_PAD - W_IN_SHARD

    def body(chip_ref, w_ref, o_ref):
        cols = jnp.concatenate([w_ref[:, 0, :], jnp.zeros((pad, TM), f32)], axis=0)
        o_ref[0, 0] = cols.T.astype(MX)

    return pl.pallas_call(
        body, name="cast_into_slot_w_in", out_shape=jax.ShapeDtypeStruct((4, 2, h, W_IN_SHARD_PAD), MX),
        grid_spec=pltpu.PrefetchScalarGridSpec(
            num_scalar_prefetch=1, grid=(2, nj),
            in_specs=[pl.BlockSpec((W_IN_SHARD, 1, TM), lambda i, j, chip: (0, 0, i * nj + j))],
            out_specs=pl.BlockSpec((1, 1, TM, W_IN_SHARD_PAD), lambda i, j, chip: (chip[0], i, j, 0))),
        compiler_params=_cp("arbitrary", "arbitrary"),
    )(_scalar(chip), w_t)


def _adamw_w_in(w_t, mine, other, m_t, v_t, core):
    h = D // 2
    nj = h // TM

    def body(core_ref, w_ref, a_ref, b_ref, m_ref, v_ref, g_ref, d_ref, nm_ref, nv_ref):
        g = jnp.where(pl.program_id(0) == core_ref[0], a_ref[...], b_ref[...]).T[0:W_IN_SHARD, :]
        g_ref[:, 0, :] = g
        d_ref[:, 0, :], nm_ref[:, 0, :], nv_ref[:, 0, :] = _adam_math(w_ref[:, 0, :], g, m_ref[:, 0, :], v_ref[:, 0, :])

    blk = pl.BlockSpec((W_IN_SHARD, 1, TM), lambda i, j, core: (0, 0, i * nj + j))
    gblk = pl.BlockSpec((TM, W_IN_SHARD_PAD), lambda i, j, core: (j, 0))
    return pl.pallas_call(
        body, name="adamw_w_in", out_shape=tuple([jax.ShapeDtypeStruct((W_IN_SHARD, 1, D), f32)] * 4),
        grid_spec=pltpu.PrefetchScalarGridSpec(
            num_scalar_prefetch=1, grid=(2, nj), in_specs=[blk, gblk, gblk, blk, blk], out_specs=(blk,) * 4),
        compiler_params=_cp("arbitrary", "arbitrary"),
    )(_scalar(core), w_t, mine, other, m_t, v_t)


ANY = pl.BlockSpec(memory_space=pl.ANY)


def _place():
    x, y, c = lax.axis_index("x"), lax.axis_index("y"), lax.axis_index("c")
    return x, y, c, [(1 - x, y), (x, 1 - y), (1 - x, 1 - y)]


def _gather_rows(block, rider=None):
    m_per, n = block.shape
    ri, ro = (len(rider.inputs), len(rider.out_shape)) if rider is not None else (0, 0)

    def body(x_ref, *refs):
        r_in, out_ref, r_out = refs[:ri], refs[ri], refs[ri + 1:ri + 1 + ro]
        send_sems, recv_sems, local_sem, *r_scr = refs[ri + 1 + ro:]
        x, y, c, chips = _place()
        me, sibling = (x, y, c), (x, y, 1 - c)

        def rows(px, py, pc):
            return out_ref.at[pl.ds((4 * px + 2 * py + pc) * m_per, m_per), :]

        def copy(k, blk, to, src=None):
            return pltpu.make_async_remote_copy(
                src_ref=rows(*blk) if src is None else src, dst_ref=rows(*blk), send_sem=send_sems.at[k],
                recv_sem=recv_sems.at[k], device_id=to, device_id_type=MESH)

        mine = pltpu.make_async_copy(x_ref, rows(*me), local_sem)
        mine.start()
        first = [copy(0, me, sibling, src=x_ref)]
        first += [copy(1 + j, me, (*chip, c), src=x_ref) for j, chip in enumerate(chips)]
        for cp in first:
            cp.start()
        if rider is not None:
            rider.start(r_in, r_out, r_scr)
        passed = [copy(4 + j, (*chip, c), sibling) for j, chip in enumerate(chips)]
        for j, chip in enumerate(chips):
            copy(1 + j, (*chip, c), me).wait_recv()
            passed[j].start()
        copy(0, sibling, me).wait_recv()
        for j, chip in enumerate(chips):
            copy(4 + j, (*chip, 1 - c), me).wait_recv()
        for cp in first + passed:
            cp.wait_send()
        mine.wait()
        if rider is not None:
            rider.finish(r_in, r_out, r_scr)

    vmem = pl.BlockSpec(memory_space=pltpu.VMEM)
    gathered = jax.ShapeDtypeStruct((8 * m_per, n), block.dtype)
    if rider is None:
        return pl.pallas_call(
            body, name=f"gather_rows_{m_per}x{n}", out_shape=gathered, in_specs=[vmem], out_specs=vmem,
            scratch_shapes=[pltpu.SemaphoreType.DMA((7,)), pltpu.SemaphoreType.DMA((7,)), pltpu.SemaphoreType.DMA],
            compiler_params=_VM)(block)
    outs = pl.pallas_call(
        body, name=f"gather_rows_{m_per}x{n}", out_shape=(gathered,) + tuple(rider.out_shape),
        in_specs=[vmem] + [ANY] * ri, out_specs=(vmem,) + (ANY,) * ro,
        input_output_aliases={1 + i: 1 + j for i, j in rider.aliases.items()},
        scratch_shapes=[pltpu.SemaphoreType.DMA((7,)), pltpu.SemaphoreType.DMA((7,)), pltpu.SemaphoreType.DMA] + list(rider.scratch),
        compiler_params=_VM)(block, *rider.inputs)
    return outs[0], tuple(outs[1:])


class _GatherRider:
    def __init__(self, slots, row0=0, nrows=None):
        n = len(slots)
        self.n = n
        self.rows = (row0, slots[0].shape[2] - row0 if nrows is None else nrows)
        self.inputs = list(slots)
        self.out_shape = [jax.ShapeDtypeStruct(s.shape, s.dtype) for s in slots]
        self.scratch = [pltpu.SemaphoreType.DMA((n, 6)), pltpu.SemaphoreType.DMA((n, 6))]
        self.aliases = {a: a for a in range(n)}

    def _copy(self, outs, sems, a, j, k, half, to):
        dst = outs[a].at[k, half, pl.ds(*self.rows)]
        return pltpu.make_async_remote_copy(src_ref=dst, dst_ref=dst, send_sem=sems[0].at[a, j], recv_sem=sems[1].at[a, j],
                                            device_id=to, device_id_type=MESH)

    def _first(self, outs, sems):
        x, y, c, chips = _place()
        return [self._copy(outs, sems, a, j, 2 * x + y, c, (*chip, c)) for a in range(self.n) for j, chip in enumerate(chips)]

    def start(self, ins, outs, sems):
        for cp in self._first(outs, sems):
            cp.start()

    def finish(self, ins, outs, sems):
        x, y, c, chips = _place()
        passed = []
        for a in range(self.n):
            for j, (px, py) in enumerate(chips):
                self._copy(outs, sems, a, j, 2 * px + py, c, (x, y, c)).wait_recv()
                fwd = self._copy(outs, sems, a, 3 + j, 2 * px + py, c, (x, y, 1 - c))
                fwd.start()
                passed.append(fwd)
        for a in range(self.n):
            for j, (px, py) in enumerate(chips):
                self._copy(outs, sems, a, 3 + j, 2 * px + py, 1 - c, (x, y, c)).wait_recv()
        for cp in self._first(outs, sems) + passed:
            cp.wait_send()


class _ScatterRider:
    def __init__(self, parts, row0=0, nrows=None):
        n = len(parts)
        self.n = n
        self.rows = (row0, parts[0].shape[1] - row0 if nrows is None else nrows)
        self.inputs = list(parts)
        self.out_shape = [jax.ShapeDtypeStruct((3, self.rows[1], p.shape[2]), p.dtype) for p in parts]
        self.scratch = [pltpu.SemaphoreType.DMA((n, 3)), pltpu.SemaphoreType.DMA((n, 3))]
        self.aliases = {}

    def _copies(self, ins, outs, sems):
        x, y, c, chips = _place()
        return [pltpu.make_async_remote_copy(
            src_ref=ins[a].at[2 * px + py, pl.ds(*self.rows)], dst_ref=outs[a].at[j], send_sem=sems[0].at[a, j],
            recv_sem=sems[1].at[a, j], device_id=(px, py, c), device_id_type=MESH)
            for a in range(self.n) for j, (px, py) in enumerate(chips)]

    def start(self, ins, outs, sems):
        for cp in self._copies(ins, outs, sems):
            cp.start()

    def finish(self, ins, outs, sems):
        for cp in self._copies(ins, outs, sems):
            cp.wait()


HBM = pl.BlockSpec(memory_space=pltpu.HBM)
SEM = pl.BlockSpec(memory_space=pltpu.SEMAPHORE)
EFFECT = pltpu.SideEffectType.DATAFLOW_SIDE_EFFECTING


def _scatter_copies(p_ref, land_ref, send_sems, recv_sems):
    x, y, c, chips = _place()
    return [pltpu.make_async_remote_copy(
        src_ref=p_ref.at[2 * px + py], dst_ref=land_ref.at[j], send_sem=send_sems.at[j], recv_sem=recv_sems.at[j],
        device_id=(px, py, c), device_id_type=MESH) for j, (px, py) in enumerate(chips)]


def _scatter_start(parts):
    land = jax.ShapeDtypeStruct((3,) + parts.shape[1:], parts.dtype)

    def body(p_ref, land_ref, send_sems, recv_sems, p_thru, land_thru, token):
        for cp in _scatter_copies(p_ref, land_ref, send_sems, recv_sems):
            cp.start()
        token[...] = jnp.zeros_like(token)

    return pl.pallas_call(
        body, name="scatter_start_w_in",
        out_shape=(pltpu.SemaphoreType.DMA((3,)), pltpu.SemaphoreType.DMA((3,)), pltpu.HBM(parts.shape, parts.dtype),
                   pltpu.HBM(land.shape, land.dtype), jax.ShapeDtypeStruct((8, LANES), f32)),
        in_specs=(HBM, HBM), out_specs=(SEM, SEM, HBM, HBM, pl.BlockSpec(memory_space=pltpu.VMEM)),
        input_output_aliases={0: 2, 1: 3}, compiler_params=pltpu.CompilerParams(has_side_effects=EFFECT),
    )(pltpu.with_memory_space_constraint(parts, pltpu.HBM),
      pltpu.with_memory_space_constraint(lax.empty(land.shape, land.dtype), pltpu.HBM))


def _scatter_wait(send_sems, recv_sems, p_thru, land_thru, after):
    def body(p_ref, land_ref, send_sems, recv_sems, after_ref, p_out, land_out):
        for cp in _scatter_copies(p_ref, land_ref, send_sems, recv_sems):
            cp.wait_send()
            cp.wait_recv()

    return pl.pallas_call(
        body, name="scatter_wait_w_in",
        out_shape=(pltpu.HBM(p_thru.shape, p_thru.dtype), pltpu.HBM(land_thru.shape, land_thru.dtype)),
        in_specs=(HBM, HBM, SEM, SEM, ANY), out_specs=(HBM, HBM), input_output_aliases={0: 0, 1: 1},
        compiler_params=pltpu.CompilerParams(has_side_effects=EFFECT),
    )(p_thru, land_thru, send_sems, recv_sems, after)


def _ride_alone(rider, name):
    n = len(rider.inputs)

    def body(*refs):
        ins, outs, sems = refs[:n], refs[n:n + len(rider.out_shape)], refs[n + len(rider.out_shape):]
        rider.start(ins, outs, sems)
        rider.finish(ins, outs, sems)

    return pl.pallas_call(
        body, name=name, out_shape=tuple(rider.out_shape), in_specs=[ANY] * n, out_specs=tuple([ANY] * len(rider.out_shape)),
        input_output_aliases=dict(rider.aliases), scratch_shapes=list(rider.scratch),
    )(*rider.inputs)


class _SwapRider:
    def __init__(self, grads):
        n = len(grads)
        self.n = n
        self.inputs = list(grads)
        self.out_shape = [jax.ShapeDtypeStruct((4,) + g.shape[2:], g.dtype) for g in grads]
        self.scratch = [pltpu.SemaphoreType.DMA((n, 4)), pltpu.SemaphoreType.DMA((n, 4))]
        self.aliases = {}

    def _copies(self, ins, outs, sems):
        x, y, c, _ = _place()
        return [pltpu.make_async_remote_copy(
            src_ref=ins[a].at[k, 1 - c], dst_ref=outs[a].at[k], send_sem=sems[0].at[a, k], recv_sem=sems[1].at[a, k],
            device_id=(x, y, 1 - c), device_id_type=MESH) for a in range(self.n) for k in range(4)]

    def start(self, ins, outs, sems):
        for cp in self._copies(ins, outs, sems):
            cp.start()

    def finish(self, ins, outs, sems):
        for cp in self._copies(ins, outs, sems):
            cp.wait()


class _Riders:
    def __init__(self, riders):
        self.riders = list(riders)
        self.inputs = [a for r in riders for a in r.inputs]
        self.out_shape = [s for r in riders for s in r.out_shape]
        self.scratch = [s for r in riders for s in r.scratch]
        self.aliases = {}
        i = o = 0
        for r in riders:
            self.aliases.update({i + a: o + b for a, b in r.aliases.items()})
            i, o = i + len(r.inputs), o + len(r.out_shape)

    def _each(self, ins, outs, sems):
        i = o = s = 0
        for r in self.riders:
            yield r, ins[i:i + len(r.inputs)], outs[o:o + len(r.out_shape)], sems[s:s + len(r.scratch)]
            i, o, s = i + len(r.inputs), o + len(r.out_shape), s + len(r.scratch)

    def start(self, ins, outs, sems):
        for r, a, b, c in self._each(ins, outs, sems):
            r.start(a, b, c)

    def finish(self, ins, outs, sems):
        for r, a, b, c in self._each(ins, outs, sems):
            r.finish(a, b, c)

    def split(self, outs):
        res, o = [], 0
        for r in self.riders:
            res.append(outs[o:o + len(r.out_shape)])
            o += len(r.out_shape)
        return res


class _Reducer:
    def __init__(self, chip, core):
        self.chip, self.core, self.grads, self.parts, self.sums, self.others = chip, core, {}, {}, {}, {}

    def swap(self, name, grad):
        self.grads[name] = grad
        return _SwapRider([grad])

    def swapped(self, name, got):
        self.parts[name] = _add_pair(self.grads[name], got[0], self.core, name)

    def scatter(self, name, row0=0, nrows=None):
        return _ScatterRider([self.parts[name]], row0, nrows)

    def scattered(self, name, others):
        self.sums[name] = _add_chips(self.parts[name], others[0], self.chip, name)


class _SwapSumsRider:
    def __init__(self, halves):
        n = len(halves)
        self.n = n
        self.inputs = list(halves)
        self.out_shape = [jax.ShapeDtypeStruct(s.shape, s.dtype) for s in halves]
        self.scratch = [pltpu.SemaphoreType.DMA((n,)), pltpu.SemaphoreType.DMA((n,))]
        self.aliases = {}

    def _copies(self, ins, outs, sems):
        x, y, c, _ = _place()
        return [pltpu.make_async_remote_copy(
            src_ref=ins[a], dst_ref=outs[a], send_sem=sems[0].at[a], recv_sem=sems[1].at[a],
            device_id=(x, y, 1 - c), device_id_type=MESH) for a in range(self.n)]

    def start(self, ins, outs, sems):
        for cp in self._copies(ins, outs, sems):
            cp.start()

    def finish(self, ins, outs, sems):
        for cp in self._copies(ins, outs, sems):
            cp.wait()


def _row_tile(r):
    for tm in (TM, 176, 128, 64, 32, 16, 8):
        if r % tm == 0:
            return tm
    return r


def _add_pair(mine, got, core, name):
    k, _, h, c = mine.shape
    tm = _row_tile(h)

    def body(core_ref, a_ref, b_ref, o_ref):
        o_ref[0] = (a_ref[0, 0].astype(f32) + b_ref[0].astype(f32)).astype(MX)

    blk = pl.BlockSpec((1, tm, c), lambda i, j, core: (i, j, 0))
    return pl.pallas_call(
        body, name="add_pair_" + name, out_shape=jax.ShapeDtypeStruct((k, h, c), MX),
        grid_spec=pltpu.PrefetchScalarGridSpec(
            num_scalar_prefetch=1, grid=(k, h // tm),
            in_specs=[pl.BlockSpec((1, 1, tm, c), lambda i, j, core: (i, core[0], j, 0)), blk], out_specs=blk),
        compiler_params=_cp("arbitrary", "arbitrary"),
    )(_scalar(core), mine, got)


def _add_chips(parts, others, chip, name, row0=0):
    _, n, c = others.shape
    tm = _row_tile(n)
    assert row0 % tm == 0
    i0 = row0 // tm

    def body(chip_ref, a_ref, b_ref, o_ref):
        s = a_ref[0].astype(f32) + b_ref[0].astype(f32)
        o_ref[...] = (s + b_ref[1].astype(f32)) + b_ref[2].astype(f32)

    return pl.pallas_call(
        body, name="add_chips_" + name, out_shape=jax.ShapeDtypeStruct((n, c), f32),
        grid_spec=pltpu.PrefetchScalarGridSpec(
            num_scalar_prefetch=1, grid=(n // tm,),
            in_specs=[pl.BlockSpec((1, tm, c), lambda i, chip: (chip[0], i0 + i, 0)),
                      pl.BlockSpec((3, tm, c), lambda i, chip: (0, i, 0))],
            out_specs=pl.BlockSpec((tm, c), lambda i, chip: (i, 0))),
        compiler_params=_cp("arbitrary"),
    )(_scalar(chip), parts, others)


def _adam_math(w, g, m, v):
    m = ADAM_B1 * m + (1.0 - ADAM_B1) * g
    v = ADAM_B2 * v + (1.0 - ADAM_B2) * (g * g)
    m_hat = m / (1.0 - ADAM_B1 ** ADAM_STEP)
    v_hat = v / (1.0 - ADAM_B2 ** ADAM_STEP)
    return -ADAM_LR * (m_hat / (jnp.sqrt(v_hat) + ADAM_EPS) + ADAM_WD * w), m, v


def _adamw_halves(w, mine, other, m, v, core, name, after):
    r, c = w.shape
    h = r // 2
    tm = _row_tile(h)
    nj = h // tm
    cg = mine.shape[1]

    def body(core_ref, w_ref, a_ref, b_ref, m_ref, v_ref, after_ref, g_ref, d_ref, nm_ref, nv_ref):
        g = jnp.where(pl.program_id(0) == core_ref[0], a_ref[:, 0:c], b_ref[:, 0:c])
        g_ref[...] = g
        d_ref[...], nm_ref[...], nv_ref[...] = _adam_math(w_ref[...], g, m_ref[...], v_ref[...])

    blk = pl.BlockSpec((tm, c), lambda i, j, core: (i * nj + j, 0))
    gblk = pl.BlockSpec((tm, cg), lambda i, j, core: (j, 0))
    return _call(body, name=name, grid=(2, nj), out_shape=[jax.ShapeDtypeStruct((r, c), f32)] * 4,
                 in_specs=[blk, gblk, gblk, blk, blk, ANY], out_specs=(blk,) * 4, sem=("arbitrary", "arbitrary"),
                 prefetch=(_scalar(core),), args=(w, mine, other, m, v, after))[0]


def _ada_forward(c_all, ada_w):
    def body(c_ref, w_ref, o_ref):
        o_ref[...] = jnp.dot(_silu(c_ref[...]).astype(MX), w_ref[...].astype(MX), preferred_element_type=f32)

    return pl.pallas_call(body, name="ada_forward", out_shape=jax.ShapeDtypeStruct((8, ada_w.shape[1]), f32),
                          compiler_params=pltpu.CompilerParams(vmem_limit_bytes=VMEM_LIMIT))(c_all, ada_w)


def _ada_adamw(c_all_t, d_mod, w, m, v, rider=None):
    r, c = w.shape
    tm = TM

    def body(ct_ref, dm_ref, w_ref, m_ref, v_ref, g_ref, d_ref, nm_ref, nv_ref):
        ca = _silu(ct_ref[...])
        g = ca[:, 0:1] * dm_ref[0:1, :]
        for b in range(1, 8):
            g = g + ca[:, b:b + 1] * dm_ref[b:b + 1, :]
        g_ref[...] = g
        d_ref[...], nm_ref[...], nv_ref[...] = _adam_math(w_ref[...], g, m_ref[...], v_ref[...])

    blk = pl.BlockSpec((tm, c), lambda i: (i, 0))
    return _call(body, name="ada_adamw", grid=(r // tm,), out_shape=[jax.ShapeDtypeStruct((r, c), f32)] * 4,
                 in_specs=[pl.BlockSpec((tm, 8), lambda i: (i, 0)), pl.BlockSpec((8, c), lambda i: (0, 0)), blk, blk, blk],
                 out_specs=(blk,) * 4, sem=("arbitrary",), args=(c_all_t, d_mod, w, m, v), rider=rider)


WEIGHTS = ("ada_w", "ada_b", "norm1_w", "w_in", "ssd_conv_w", "ssd_conv_b", "dt_bias", "a_log", "d_skip", "ssd_norm_w",
           "conf_conv_w", "conf_conv_b", "conf_ln_w", "conf_ln_b", "w_out", "norm2_w", "w_up", "ffn_conv_w", "ffn_conv_b",
           "w_down", "final_norm_w")


def kernel(x, c, ada_w, ada_b, norm1_w, w_in, ssd_conv_w, ssd_conv_b, dt_bias, a_log, d_skip, ssd_norm_w, conf_conv_w, conf_conv_b, conf_ln_w, conf_ln_b, w_out, norm2_w, w_up, ffn_conv_w, ffn_conv_b, w_down, final_norm_w, loss_target, m_ada_w, m_ada_b, m_norm1_w, m_w_in, m_ssd_conv_w, m_ssd_conv_b, m_dt_bias, m_a_log, m_d_skip, m_ssd_norm_w, m_conf_conv_w, m_conf_conv_b, m_conf_ln_w, m_conf_ln_b, m_w_out, m_norm2_w, m_w_up, m_ffn_conv_w, m_ffn_conv_b, m_w_down, m_final_norm_w, v_ada_w, v_ada_b, v_norm1_w, v_w_in, v_ssd_conv_w, v_ssd_conv_b, v_dt_bias, v_a_log, v_d_skip, v_ssd_norm_w, v_conf_conv_w, v_conf_conv_b, v_conf_ln_w, v_conf_ln_b, v_w_out, v_norm2_w, v_w_up, v_ffn_conv_w, v_ffn_conv_b, v_w_down, v_final_norm_w):
    given = dict(locals())
    w = {n: given[n] for n in WEIGHTS}
    mom = {n: given["m_" + n] for n in WEIGHTS}
    var = {n: given["v_" + n] for n in WEIGHTS}
    chip = 2 * lax.axis_index("x") + lax.axis_index("y")
    me = 2 * chip + lax.axis_index("c")

    core = lax.axis_index("c")
    a_in = _cast_into_slot_w_in(_columns_first(w_in), chip)
    got, (a_in,) = _gather_rows(_pack_front(c, [w[n] for n in CONVS]), _GatherRider([a_in], 0, D // 4))
    c_all, *convs = _unpack_front(got)
    conv_full = dict(zip(CONVS, convs))

    got, (a_in,) = _gather_rows(_ada_forward(c_all, ada_w[0]), _GatherRider([a_in], D // 4, D // 4))
    mod_cols = got.reshape(8, 8, -1)[0::2]
    mod = lax.dynamic_index_in_dim(mod_cols, me, axis=1, keepdims=False).reshape(1, 6 * D) + ada_b
    w_pack = _pack_w_in(a_in.reshape(4, D, W_IN_SHARD_PAD))
    late = (_cast_into_slot(w_out[0], D, chip), _cast_into_slot(w_up[0], UP_SHARD, chip), _cast_into_slot(w_down[0], D, chip))

    flat = lambda a: a.reshape(1, -1) if a.ndim == 1 else a
    small = {n: flat(w[n]) for n in VECTORS if n != "ada_b"}
    small.update(conv_full)
    reducer = _Reducer(chip, core)
    _, grad_x, _, gsmall = _local_step(x[0], mod, loss_target[0], w_pack, late, small, reducer)
    grads, delta, new_m, new_v = {}, {}, {}, {}

    send_sems, recv_sems, parts_in, land, token = _scatter_start(reducer.parts["w_in"])
    for n in ("w_up", "w_down", "w_out"):
        res = _adamw_halves(w[n][0], reducer.sums[n], reducer.others[n], mom[n][0], var[n][0], core, "adamw_" + n, token)
        grads[n], delta[n], new_m[n], new_v[n] = [r[None] for r in res]

    names = VECTORS + tuple(CONVS)
    d_mod_mine, loss, res = _small_adamw(_gather_rows(gsmall), chip, *[{n: flat(d[n]) for n in names} for d in (w, mom, var)])
    for n in names:
        grads[n], delta[n], new_m[n], new_v[n] = [r.reshape(w[n].shape) for r in res[n]]
    res, _ = _ada_adamw(c_all.T, d_mod_mine, ada_w[0], m_ada_w[0], v_ada_w[0])
    grads["ada_w"], delta["ada_w"], new_m["ada_w"], new_v["ada_w"] = [r[None] for r in res]

    reducer.parts["w_in"], others = _scatter_wait(send_sems, recv_sems, parts_in, land, res[1])
    reducer.scattered("w_in", [others])
    reducer.others["w_in"], = _ride_alone(_SwapSumsRider([reducer.sums["w_in"]]), "swap_sums_w_in")
    res = _adamw_w_in(_columns_first(w_in), reducer.sums["w_in"], reducer.others["w_in"], _columns_first(m_w_in),
                      _columns_first(v_w_in), core)
    grads["w_in"], delta["w_in"], new_m["w_in"], new_v["w_in"] = [jnp.transpose(r, (1, 2, 0)) for r in res]

    return (loss, grad_x[None], *[grads[n] for n in WEIGHTS], *[delta[n] for n in WEIGHTS],
            *[new_m[n] for n in WEIGHTS], *[new_v[n] for n in WEIGHTS])
```

```python
import functools

import jax
import jax.numpy as jnp
from jax import lax
from jax.experimental import pallas as pl
from jax.experimental.pallas import tpu as pltpu

f32 = jnp.float32
MX = jnp.bfloat16

D = 1024
HEADS = 16
HEAD_P = 64
STATE_N = 128
D_XBC = 1536
D_FF = 2816
UP_SHARD = 2 * D_FF // 4
UP_EARLY_ROWS = 128
K_SSD, K_CONF, K_FFN = 4, 31, 3
CHUNK = 128
OFF_Z, OFF_XBC, OFF_CA, OFF_CG, OFF_DT = 0, 1024, 2560, 3584, 4608
W_PACK = 4736
TM = 256
CW = 256
RC = 64
LANES = 128
VMEM_LIMIT = 56 * 1024 * 1024

ADAM_LR, ADAM_B1, ADAM_B2, ADAM_EPS, ADAM_WD, ADAM_STEP = 0.001, 0.9, 0.999, 1e-08, 0.01, 10

MESH = pl.DeviceIdType.MESH


def _cp(*sem):
    return pltpu.CompilerParams(dimension_semantics=sem, vmem_limit_bytes=VMEM_LIMIT)


def _resident(shape):
    nd = len(shape)
    return pl.BlockSpec(shape, lambda *_: (0,) * nd, pipeline_mode=pl.Buffered(1))


def _row(width=D):
    return pl.BlockSpec((1, width), lambda *_: (0, 0))


def _call(body, *, name, grid, in_specs, out_specs, out_shape, args, sem, scratch_shapes=(), prefetch=(), rider=None):
    ni, no, ns, npf = len(in_specs), len(out_specs), len(scratch_shapes), len(prefetch)
    ri, ro = (len(rider.inputs), len(rider.out_shape)) if rider is not None else (0, 0)

    def full(*refs):
        pre, refs = refs[:npf], refs[npf:]
        base_in, r_in = refs[:ni], refs[ni:ni + ri]
        base_out, r_out = refs[ni + ri:ni + ri + no], refs[ni + ri + no:ni + ri + no + ro]
        base_scr, r_scr = refs[ni + ri + no + ro:ni + ri + no + ro + ns], refs[ni + ri + no + ro + ns:]
        if rider is None:
            return body(*pre, *base_in, *base_out, *base_scr)
        ids = [pl.program_id(a) for a in range(len(grid))]
        first = functools.reduce(jnp.logical_and, [i == 0 for i in ids])
        last = functools.reduce(jnp.logical_and, [i == g - 1 for i, g in zip(ids, grid)])

        @pl.when(first)
        def _():
            rider.start(r_in, r_out, r_scr)

        body(*pre, *base_in, *base_out, *base_scr)

        @pl.when(last)
        def _():
            rider.finish(r_in, r_out, r_scr)

    extra = dict(shapes=[], scratch=[], aliases={}, inputs=[]) if rider is None else dict(
        shapes=rider.out_shape, scratch=rider.scratch, inputs=rider.inputs,
        aliases={npf + ni + i: no + j for i, j in rider.aliases.items()})
    outs = pl.pallas_call(
        full, name=name, out_shape=tuple(out_shape) + tuple(extra["shapes"]), input_output_aliases=extra["aliases"],
        grid_spec=pltpu.PrefetchScalarGridSpec(
            num_scalar_prefetch=npf, grid=grid, in_specs=list(in_specs) + [ANY] * ri,
            out_specs=tuple(out_specs) + (ANY,) * ro, scratch_shapes=list(scratch_shapes) + list(extra["scratch"])),
        compiler_params=_cp(*sem),
    )(*prefetch, *args, *extra["inputs"])
    return tuple(outs[:no]), tuple(outs[no:])


def _silu(v):
    return v * jax.nn.sigmoid(v)


def _dsilu(v):
    s = jax.nn.sigmoid(v)
    return s * (1.0 + v * (1.0 - s))


def _softplus(v):
    return jnp.maximum(v, 0.0) + jnp.log1p(jnp.exp(-jnp.abs(v)))


def _mm(a, b):
    return jnp.dot(a.astype(MX), b.astype(MX), preferred_element_type=f32)


def _mm_nt(a, b):
    return lax.dot_general(a.astype(MX), b.astype(MX), (((1,), (1,)), ((), ())), preferred_element_type=f32)


def _mm_tn(a, b):
    return lax.dot_general(a.astype(MX), b.astype(MX), (((0,), (0,)), ((), ())), preferred_element_type=f32)


def _ln_inproj(x, mod, norm1_w, w_pack, rider=None):
    t = x.shape[0]

    def body(x_ref, mod_ref, nw_ref, w_ref, proj_ref, ht_ref):
        xv = x_ref[...]
        rstd = lax.rsqrt(jnp.mean(xv * xv, axis=-1, keepdims=True) + 1e-6)
        h = (xv * rstd * nw_ref[...]) * (1.0 + mod_ref[:, D:2 * D]) + mod_ref[:, 0:D]
        hb = h.astype(MX)
        ht_ref[...] = hb.T
        proj_ref[...] = jnp.dot(hb, w_ref[...], preferred_element_type=f32)

    return _call(
        body, name="ln_inproj", grid=(t // TM,),
        out_shape=(jax.ShapeDtypeStruct((t, W_PACK), f32), jax.ShapeDtypeStruct((D, t), MX)),
        in_specs=[pl.BlockSpec((TM, D), lambda i: (i, 0)), _row(6 * D), _row(), _resident((D, W_PACK))],
        out_specs=(pl.BlockSpec((TM, W_PACK), lambda i: (i, 0)), pl.BlockSpec((D, TM), lambda i: (0, i))),
        sem=("arbitrary",), args=(x, mod, norm1_w, w_pack), rider=rider)


def _ssd_gate_norm(y_scan, xbc_act, proj, d_skip_row, ssd_norm_w):
    t = y_scan.shape[0]

    def body(y_ref, xs_ref, z_ref, dsk_ref, nw_ref, o_ref, ot_ref):
        y = y_ref[...] + xs_ref[...] * dsk_ref[...]
        yz = y * _silu(z_ref[...])
        rstd = lax.rsqrt(jnp.mean(yz * yz, axis=-1, keepdims=True) + 1e-6)
        out = (yz * rstd * nw_ref[...]).astype(MX)
        o_ref[...] = out
        ot_ref[...] = out.T

    blk = pl.BlockSpec((TM, D), lambda i: (i, 0))
    return pl.pallas_call(
        body, name="ssd_gate_norm", grid=(t // TM,),
        out_shape=(jax.ShapeDtypeStruct((t, D), MX), jax.ShapeDtypeStruct((D, t), MX)),
        in_specs=[blk, blk, blk, _row(), _row()], out_specs=(blk, pl.BlockSpec((D, TM), lambda i: (0, i))),
        compiler_params=_cp("arbitrary"),
    )(y_scan, xbc_act, proj, d_skip_row, ssd_norm_w)


def _ln_silu(u_conv, ln_w, ln_b):
    t = u_conv.shape[0]

    def body(u_ref, w_ref, b_ref, o_ref, ot_ref):
        u = u_ref[...]
        mu = jnp.mean(u, axis=-1, keepdims=True)
        uc = u - mu
        rstd = lax.rsqrt(jnp.mean(uc * uc, axis=-1, keepdims=True) + 1e-5)
        out = _silu(uc * rstd * w_ref[...] + b_ref[...]).astype(MX)
        o_ref[...] = out
        ot_ref[...] = out.T

    blk = pl.BlockSpec((TM, D), lambda i: (i, 0))
    return pl.pallas_call(
        body, name="ln_silu", grid=(t // TM,),
        out_shape=(jax.ShapeDtypeStruct((t, D), MX), jax.ShapeDtypeStruct((D, t), MX)),
        in_specs=[blk, _row(), _row()], out_specs=(blk, pl.BlockSpec((D, TM), lambda i: (0, i))),
        compiler_params=_cp("arbitrary"),
    )(u_conv, ln_w, ln_b)


def _outproj_ln2_up(y_ssd, u, w_out, x, mod, norm2_w, w_up):
    t = x.shape[0]

    def body(y_ref, u_ref, wo_ref, x_ref, mod_ref, nw_ref, wu_ref, mix_ref, x1_ref, h2t_ref, up_ref):
        mix = jnp.dot(y_ref[...], wo_ref[0:D, :], preferred_element_type=f32)
        mix = mix + jnp.dot(u_ref[...], wo_ref[D:2 * D, :], preferred_element_type=f32)
        mix_ref[...] = mix
        x1 = x_ref[...] + mod_ref[:, 2 * D:3 * D] * mix
        x1_ref[...] = x1
        rstd = lax.rsqrt(jnp.mean(x1 * x1, axis=-1, keepdims=True) + 1e-6)
        h2 = ((x1 * rstd * nw_ref[...]) * (1.0 + mod_ref[:, 4 * D:5 * D]) + mod_ref[:, 3 * D:4 * D]).astype(MX)
        h2t_ref[...] = h2.T
        for k in range(4):
            up_ref[:, k * UP_SHARD:(k + 1) * UP_SHARD] = jnp.dot(h2, wu_ref[k], preferred_element_type=f32)

    blk = pl.BlockSpec((TM, D), lambda i: (i, 0))
    return pl.pallas_call(
        body, name="outproj_ln2_up", grid=(t // TM,),
        out_shape=(jax.ShapeDtypeStruct((t, D), f32), jax.ShapeDtypeStruct((t, D), f32),
                   jax.ShapeDtypeStruct((D, t), MX), jax.ShapeDtypeStruct((t, 2 * D_FF), f32)),
        in_specs=[blk, blk, _resident((2 * D, D)), blk, _row(6 * D), _row(), _resident((4, D, UP_SHARD))],
        out_specs=(blk, blk, pl.BlockSpec((D, TM), lambda i: (0, i)), pl.BlockSpec((TM, 2 * D_FF), lambda i: (i, 0))),
        compiler_params=_cp("arbitrary"),
    )(y_ssd, u, w_out, x, mod, norm2_w, w_up)


def _down_loss(act, w_down, x1, mod, final_norm_w, target):
    t = x1.shape[0]

    def body(a_ref, wd_ref, x1_ref, mod_ref, wf_ref, tgt_ref, dx2_ref, dffn_ref, dact_ref, st_ref):
        @pl.when(pl.program_id(0) == 0)
        def _():
            st_ref[...] = jnp.zeros_like(st_ref)

        g2 = mod_ref[:, 5 * D:6 * D]
        ffn = jnp.dot(a_ref[...], wd_ref[...], preferred_element_type=f32)
        x2 = x1_ref[...] + g2 * ffn
        rstd = lax.rsqrt(jnp.mean(x2 * x2, axis=-1, keepdims=True) + 1e-6)
        xh = x2 * rstd
        wf = wf_ref[...]
        err = xh * wf - tgt_ref[...]
        dy = err * (1.0 / D)
        dxh = dy * wf
        dx2 = rstd * (dxh - xh * jnp.mean(dxh * xh, axis=-1, keepdims=True))
        dx2_ref[...] = dx2
        dffn = (g2 * dx2).astype(MX)
        dffn_ref[...] = dffn
        dact_ref[...] = lax.dot_general(dffn, wd_ref[...], (((1,), (1,)), ((), ())), preferred_element_type=f32)
        st_ref[0:1, :] += jnp.sum(dy * xh, axis=0, keepdims=True)
        st_ref[1:2, :] += jnp.sum(dx2 * ffn, axis=0, keepdims=True)
        st_ref[2:3, :] += jnp.sum(0.5 * jnp.mean(err * err, axis=-1, keepdims=True), axis=0, keepdims=True)

    blk = pl.BlockSpec((TM, D), lambda i: (i, 0))
    ablk = pl.BlockSpec((TM, D_FF), lambda i: (i, 0))
    return pl.pallas_call(
        body, name="down_loss", grid=(t // TM,),
        out_shape=(jax.ShapeDtypeStruct((t, D), f32), jax.ShapeDtypeStruct((t, D), MX),
                   jax.ShapeDtypeStruct((t, D_FF), f32), jax.ShapeDtypeStruct((8, D), f32)),
        in_specs=[ablk, _resident((D_FF, D)), blk, _row(6 * D), _row(), blk],
        out_specs=(blk, blk, ablk, pl.BlockSpec((8, D), lambda i: (0, 0))),
        compiler_params=_cp("arbitrary"),
    )(act, w_down, x1, mod, final_norm_w, target)


def _pad_of(k):
    return 8 * ((k - 1 + 7) // 8)


def _causal_win(ref, r, t, pad):
    base = pl.multiple_of(r * RC, RC)
    prev = ref[pl.ds(pl.multiple_of(jnp.maximum(base - pad, 0), 8), pad), :]
    prev = jnp.where(r > 0, prev, 0.0)
    return jnp.concatenate([prev, ref[pl.ds(base, RC), :]], axis=0)


def _anti_win(ref, r, t, pad):
    base = pl.multiple_of(r * RC, RC)
    nxt = ref[pl.ds(pl.multiple_of(jnp.minimum(base + RC, t - pad), 8), pad), :]
    nxt = jnp.where(r < t // RC - 1, nxt, 0.0)
    return jnp.concatenate([ref[pl.ds(base, RC), :], nxt], axis=0)


def _shifted(win, offsets):
    for r in range(8):
        mine = [o for o in offsets if o % 8 == r]
        if mine:
            rolled = win if r == 0 else pltpu.roll(win, win.shape[0] - r, 0)
            for o in mine:
                yield o, rolled[o - r:o - r + RC, :]


def _conv_taps(win, w_ref, k, pad):
    first = pad - (k - 1)
    acc = None
    for o, rows in _shifted(win, range(first, first + k)):
        term = w_ref[o - first:o - first + 1, :] * rows
        acc = term if acc is None else acc + term
    return acc


def _corr_taps(win, w_ref, k):
    acc = None
    for o, rows in _shifted(win, range(k)):
        term = w_ref[k - 1 - o:k - o, :] * rows
        acc = term if acc is None else acc + term
    return acc


def _dw_accumulate(dw_scr, d, win, k, pad):
    first = pad - (k - 1)
    for o, rows in _shifted(win, range(first, first + k)):
        j = o - first
        prod = d * rows
        dw_scr[8 * j:8 * j + 8, :] += prod.reshape(RC // 8, 8, prod.shape[-1]).sum(axis=0)


def _dw_finish(dw_scr, dw_ref, k):
    for j in range(k):
        dw_ref[j:j + 1, :] = jnp.sum(dw_scr[8 * j:8 * j + 8, :], axis=0, keepdims=True)


def _rows8(v):
    return v.reshape(RC // 8, 8, v.shape[-1]).sum(axis=0)


def _ssd_conv_fwd(proj, conv_w, conv_b, rider=None):
    t = proj.shape[0]
    pad = _pad_of(K_SSD)
    c0 = OFF_XBC // CW

    def body(x_ref, w_ref, b_ref, o_ref):
        def step(r, carry):
            win = _causal_win(x_ref, r, t, pad)
            o_ref[pl.ds(pl.multiple_of(r * RC, RC), RC), :] = _silu(_conv_taps(win, w_ref, K_SSD, pad) + b_ref[...])
            return carry
        lax.fori_loop(0, t // RC, step, 0)

    return _call(
        body, name="ssd_conv_fwd", grid=(D_XBC // CW,), out_shape=(jax.ShapeDtypeStruct((t, D_XBC), f32),),
        in_specs=[pl.BlockSpec((t, CW), lambda j: (0, c0 + j)), pl.BlockSpec((K_SSD, CW), lambda j: (0, j)),
                  pl.BlockSpec((1, CW), lambda j: (0, j))],
        out_specs=(pl.BlockSpec((t, CW), lambda j: (0, j)),), sem=("arbitrary",), args=(proj, conv_w, conv_b), rider=rider)


def _glu_conv_fwd(proj, conv_w, conv_b, rider=None):
    t = proj.shape[0]
    pad = _pad_of(K_CONF)
    ca, cg = OFF_CA // CW, OFF_CG // CW

    def body(a_ref, g_ref, w_ref, b_ref, o_ref, v_scr):
        def glu(r, carry):
            rows = pl.ds(pl.multiple_of(r * RC, RC), RC)
            v_scr[rows, :] = a_ref[rows, :] * jax.nn.sigmoid(g_ref[rows, :])
            return carry
        lax.fori_loop(0, t // RC, glu, 0)

        def step(r, carry):
            win = _causal_win(v_scr, r, t, pad)
            o_ref[pl.ds(pl.multiple_of(r * RC, RC), RC), :] = _conv_taps(win, w_ref, K_CONF, pad) + b_ref[...]
            return carry
        lax.fori_loop(0, t // RC, step, 0)

    return _call(
        body, name="glu_conv_fwd", grid=(D // CW,), out_shape=(jax.ShapeDtypeStruct((t, D), f32),),
        in_specs=[pl.BlockSpec((t, CW), lambda j: (0, ca + j)), pl.BlockSpec((t, CW), lambda j: (0, cg + j)),
                  pl.BlockSpec((K_CONF, CW), lambda j: (0, j)), pl.BlockSpec((1, CW), lambda j: (0, j))],
        out_specs=(pl.BlockSpec((t, CW), lambda j: (0, j)),),
        scratch_shapes=[pltpu.VMEM((t, CW), f32)], sem=("arbitrary",), args=(proj, proj, conv_w, conv_b), rider=rider)


def _ffn_conv_fwd(up, conv_w, conv_b, rider=None):
    t = up.shape[0]
    pad = _pad_of(K_FFN)
    nb = D_FF // CW

    def body(g_ref, v_ref, wg_ref, wv_ref, bg_ref, bv_ref, o_ref, ot_ref):
        def step(r, carry):
            gc = _conv_taps(_causal_win(g_ref, r, t, pad), wg_ref, K_FFN, pad) + bg_ref[...]
            vc = _conv_taps(_causal_win(v_ref, r, t, pad), wv_ref, K_FFN, pad) + bv_ref[...]
            o_ref[pl.ds(pl.multiple_of(r * RC, RC), RC), :] = (_silu(gc) * vc).astype(MX)
            return carry
        lax.fori_loop(0, t // RC, step, 0)
        ot_ref[...] = o_ref[...].T

    return _call(
        body, name="ffn_conv_fwd", grid=(nb,),
        out_shape=(jax.ShapeDtypeStruct((t, D_FF), MX), jax.ShapeDtypeStruct((D_FF, t), MX)),
        in_specs=[pl.BlockSpec((t, CW), lambda j: (0, j)), pl.BlockSpec((t, CW), lambda j: (0, nb + j)),
                  pl.BlockSpec((K_FFN, CW), lambda j: (0, j)), pl.BlockSpec((K_FFN, CW), lambda j: (0, nb + j)),
                  pl.BlockSpec((1, CW), lambda j: (0, j)), pl.BlockSpec((1, CW), lambda j: (0, nb + j))],
        out_specs=(pl.BlockSpec((t, CW), lambda j: (0, j)), pl.BlockSpec((CW, t), lambda j: (j, 0))), sem=("arbitrary",),
        args=(up, up, conv_w, conv_w, conv_b, conv_b), rider=rider)


def _ffn_conv_bwd(up, conv_w, conv_b, d_act, rider=None):
    t = up.shape[0]
    pad = _pad_of(K_FFN)
    nb = D_FF // CW

    def body(g_ref, v_ref, wg_ref, wv_ref, bg_ref, bv_ref, da_ref, dup_ref, dw_ref, db_ref,
             dg_scr, dv_scr, dwg_scr, dwv_scr, db_scr):
        dwg_scr[...] = jnp.zeros_like(dwg_scr)
        dwv_scr[...] = jnp.zeros_like(dwv_scr)
        db_scr[...] = jnp.zeros_like(db_scr)

        def first(r, carry):
            rows = pl.ds(pl.multiple_of(r * RC, RC), RC)
            gwin = _causal_win(g_ref, r, t, pad)
            vwin = _causal_win(v_ref, r, t, pad)
            gc = _conv_taps(gwin, wg_ref, K_FFN, pad) + bg_ref[...]
            vc = _conv_taps(vwin, wv_ref, K_FFN, pad) + bv_ref[...]
            da = da_ref[rows, :]
            dgc = da * vc * _dsilu(gc)
            dvc = da * _silu(gc)
            dg_scr[rows, :] = dgc
            dv_scr[rows, :] = dvc
            _dw_accumulate(dwg_scr, dgc, gwin, K_FFN, pad)
            _dw_accumulate(dwv_scr, dvc, vwin, K_FFN, pad)
            db_scr[0:8, :] += _rows8(dgc)
            db_scr[8:16, :] += _rows8(dvc)
            return carry
        lax.fori_loop(0, t // RC, first, 0)

        def second(r, carry):
            rows = pl.ds(pl.multiple_of(r * RC, RC), RC)
            dup_ref[0, rows, :] = _corr_taps(_anti_win(dg_scr, r, t, pad), wg_ref, K_FFN).astype(MX)
            dup_ref[1, rows, :] = _corr_taps(_anti_win(dv_scr, r, t, pad), wv_ref, K_FFN).astype(MX)
            return carry
        lax.fori_loop(0, t // RC, second, 0)

        for j in range(K_FFN):
            dw_ref[0, j:j + 1, :] = jnp.sum(dwg_scr[8 * j:8 * j + 8, :], axis=0, keepdims=True)
            dw_ref[1, j:j + 1, :] = jnp.sum(dwv_scr[8 * j:8 * j + 8, :], axis=0, keepdims=True)
        db_ref[0] = jnp.sum(db_scr[0:8, :], axis=0, keepdims=True)
        db_ref[1] = jnp.sum(db_scr[8:16, :], axis=0, keepdims=True)

    return _call(
        body, name="ffn_conv_bwd", grid=(nb,),
        out_shape=(jax.ShapeDtypeStruct((2, t, D_FF), MX), jax.ShapeDtypeStruct((2, K_FFN, D_FF), f32),
                   jax.ShapeDtypeStruct((2, 1, D_FF), f32)),
        in_specs=[pl.BlockSpec((t, CW), lambda j: (0, j)), pl.BlockSpec((t, CW), lambda j: (0, nb + j)),
                  pl.BlockSpec((K_FFN, CW), lambda j: (0, j)), pl.BlockSpec((K_FFN, CW), lambda j: (0, nb + j)),
                  pl.BlockSpec((1, CW), lambda j: (0, j)), pl.BlockSpec((1, CW), lambda j: (0, nb + j)),
                  pl.BlockSpec((t, CW), lambda j: (0, j))],
        out_specs=(pl.BlockSpec((2, t, CW), lambda j: (0, 0, j)), pl.BlockSpec((2, K_FFN, CW), lambda j: (0, 0, j)),
                   pl.BlockSpec((2, 1, CW), lambda j: (0, 0, j))),
        scratch_shapes=[pltpu.VMEM((t, CW), f32), pltpu.VMEM((t, CW), f32), pltpu.VMEM((8 * K_FFN, CW), f32),
                        pltpu.VMEM((8 * K_FFN, CW), f32), pltpu.VMEM((16, CW), f32)],
        sem=("arbitrary",), args=(up, up, conv_w, conv_w, conv_b, conv_b, d_act), rider=rider)


def _glu_conv_bwd(proj, conv_w, d_uconv, rider=None):
    t = proj.shape[0]
    pad = _pad_of(K_CONF)
    ca, cg = OFF_CA // CW, OFF_CG // CW

    def body(a_ref, g_ref, w_ref, du_ref, dc_ref, dw_ref, db_ref, v_scr, dw_scr, db_scr):
        dw_scr[...] = jnp.zeros_like(dw_scr)
        db_scr[...] = jnp.zeros_like(db_scr)

        def glu(r, carry):
            rows = pl.ds(pl.multiple_of(r * RC, RC), RC)
            v_scr[rows, :] = a_ref[rows, :] * jax.nn.sigmoid(g_ref[rows, :])
            return carry
        lax.fori_loop(0, t // RC, glu, 0)

        def step(r, carry):
            rows = pl.ds(pl.multiple_of(r * RC, RC), RC)
            du = du_ref[rows, :]
            _dw_accumulate(dw_scr, du, _causal_win(v_scr, r, t, pad), K_CONF, pad)
            db_scr[...] += _rows8(du)
            dv = _corr_taps(_anti_win(du_ref, r, t, pad), w_ref, K_CONF)
            a = a_ref[rows, :]
            s = jax.nn.sigmoid(g_ref[rows, :])
            dc_ref[0, rows, :] = (dv * s).astype(MX)
            dc_ref[1, rows, :] = (dv * a * s * (1.0 - s)).astype(MX)
            return carry
        lax.fori_loop(0, t // RC, step, 0)
        _dw_finish(dw_scr, dw_ref, K_CONF)
        db_ref[...] = jnp.sum(db_scr[...], axis=0, keepdims=True)

    return _call(
        body, name="glu_conv_bwd", grid=(D // CW,),
        out_shape=(jax.ShapeDtypeStruct((2, t, D), MX), jax.ShapeDtypeStruct((K_CONF, D), f32),
                   jax.ShapeDtypeStruct((1, D), f32)),
        in_specs=[pl.BlockSpec((t, CW), lambda j: (0, ca + j)), pl.BlockSpec((t, CW), lambda j: (0, cg + j)),
                  pl.BlockSpec((K_CONF, CW), lambda j: (0, j)), pl.BlockSpec((t, CW), lambda j: (0, j))],
        out_specs=(pl.BlockSpec((2, t, CW), lambda j: (0, 0, j)), pl.BlockSpec((K_CONF, CW), lambda j: (0, j)),
                   pl.BlockSpec((1, CW), lambda j: (0, j))),
        scratch_shapes=[pltpu.VMEM((t, CW), f32), pltpu.VMEM((8 * K_CONF, CW), f32), pltpu.VMEM((8, CW), f32)],
        sem=("arbitrary",), args=(proj, proj, conv_w, d_uconv), rider=rider)


def _ssd_conv_bwd_x(proj, conv_w, conv_b, d_xs, d_y, d_skip_row):
    t = proj.shape[0]
    pad = _pad_of(K_SSD)
    c0 = OFF_XBC // CW

    def body(x_ref, w_ref, b_ref, dxs_ref, dy_ref, dsk_ref, draw_ref, dw_ref, db_ref, dp_scr, dw_scr, db_scr):
        dw_scr[...] = jnp.zeros_like(dw_scr)
        db_scr[...] = jnp.zeros_like(db_scr)

        def first(r, carry):
            rows = pl.ds(pl.multiple_of(r * RC, RC), RC)
            win = _causal_win(x_ref, r, t, pad)
            pre = _conv_taps(win, w_ref, K_SSD, pad) + b_ref[...]
            dpre = (dxs_ref[rows, :] + dy_ref[rows, :] * dsk_ref[...]) * _dsilu(pre)
            dp_scr[rows, :] = dpre
            _dw_accumulate(dw_scr, dpre, win, K_SSD, pad)
            db_scr[...] += _rows8(dpre)
            return carry
        lax.fori_loop(0, t // RC, first, 0)

        def second(r, carry):
            rows = pl.ds(pl.multiple_of(r * RC, RC), RC)
            draw_ref[rows, :] = _corr_taps(_anti_win(dp_scr, r, t, pad), w_ref, K_SSD).astype(MX)
            return carry
        lax.fori_loop(0, t // RC, second, 0)
        _dw_finish(dw_scr, dw_ref, K_SSD)
        db_ref[...] = jnp.sum(db_scr[...], axis=0, keepdims=True)

    cb = pl.BlockSpec((t, CW), lambda j: (0, j))
    return pl.pallas_call(
        body, name="ssd_conv_bwd_x", grid=(D // CW,),
        out_shape=(jax.ShapeDtypeStruct((t, D), MX), jax.ShapeDtypeStruct((K_SSD, D), f32),
                   jax.ShapeDtypeStruct((1, D), f32)),
        in_specs=[pl.BlockSpec((t, CW), lambda j: (0, c0 + j)), pl.BlockSpec((K_SSD, CW), lambda j: (0, j)),
                  pl.BlockSpec((1, CW), lambda j: (0, j)), cb, cb, pl.BlockSpec((1, CW), lambda j: (0, j))],
        out_specs=(cb, pl.BlockSpec((K_SSD, CW), lambda j: (0, j)), pl.BlockSpec((1, CW), lambda j: (0, j))),
        scratch_shapes=[pltpu.VMEM((t, CW), f32), pltpu.VMEM((8 * K_SSD, CW), f32), pltpu.VMEM((8, CW), f32)],
        compiler_params=_cp("arbitrary"),
    )(proj, conv_w, conv_b, d_xs, d_y, d_skip_row)


def _ssd_conv_bwd_bc(proj, conv_w, conv_b, d_bc):
    t = proj.shape[0]
    pad = _pad_of(K_SSD)
    c0 = (OFF_XBC + D) // CW
    w0 = D // CW

    def body(x_ref, w_ref, b_ref, dbc_ref, draw_ref, dw_ref, db_ref, dp_scr, dw_scr, db_scr):
        dw_scr[...] = jnp.zeros_like(dw_scr)
        db_scr[...] = jnp.zeros_like(db_scr)

        def first(r, carry):
            rows = pl.ds(pl.multiple_of(r * RC, RC), RC)
            win = _causal_win(x_ref, r, t, pad)
            pre = _conv_taps(win, w_ref, K_SSD, pad) + b_ref[...]
            dpre = dbc_ref[0, rows, :] * _dsilu(pre)
            dp_scr[rows, :] = dpre
            _dw_accumulate(dw_scr, dpre, win, K_SSD, pad)
            db_scr[...] += _rows8(dpre)
            return carry
        lax.fori_loop(0, t // RC, first, 0)

        def second(r, carry):
            rows = pl.ds(pl.multiple_of(r * RC, RC), RC)
            draw_ref[rows, :] = _corr_taps(_anti_win(dp_scr, r, t, pad), w_ref, K_SSD).astype(MX)
            return carry
        lax.fori_loop(0, t // RC, second, 0)
        _dw_finish(dw_scr, dw_ref, K_SSD)
        db_ref[...] = jnp.sum(db_scr[...], axis=0, keepdims=True)

    return pl.pallas_call(
        body, name="ssd_conv_bwd_bc", grid=(2,),
        out_shape=(jax.ShapeDtypeStruct((t, 2 * CW), MX), jax.ShapeDtypeStruct((K_SSD, 2 * CW), f32),
                   jax.ShapeDtypeStruct((1, 2 * CW), f32)),
        in_specs=[pl.BlockSpec((t, CW), lambda j: (0, c0 + j)), pl.BlockSpec((K_SSD, CW), lambda j: (0, w0 + j)),
                  pl.BlockSpec((1, CW), lambda j: (0, w0 + j)), pl.BlockSpec((1, t, CW), lambda j: (j, 0, 0))],
        out_specs=(pl.BlockSpec((t, CW), lambda j: (0, j)), pl.BlockSpec((K_SSD, CW), lambda j: (0, j)),
                   pl.BlockSpec((1, CW), lambda j: (0, j))),
        scratch_shapes=[pltpu.VMEM((t, CW), f32), pltpu.VMEM((8 * K_SSD, CW), f32), pltpu.VMEM((8, CW), f32)],
        compiler_params=_cp("arbitrary"),
    )(proj, conv_w, conv_b, d_bc)


def _chunk_masks():
    ii = lax.broadcasted_iota(jnp.int32, (CHUNK, CHUNK), 0)
    jj = lax.broadcasted_iota(jnp.int32, (CHUNK, CHUNK), 1)
    return ii == jj, jj <= ii, jj >= ii


def _to_row(col, eye):
    return jnp.sum(jnp.where(eye, col, 0.0), axis=0, keepdims=True)


def _to_col(row, eye):
    return jnp.sum(jnp.where(eye, row, 0.0), axis=1, keepdims=True)


def _head_decay(dt_h, a_h, eye, tril):
    a_row = _to_row(dt_h * a_h, eye)
    cs = jnp.sum(jnp.where(tril, a_row, 0.0), axis=1, keepdims=True)
    cs_row = _to_row(cs, eye)
    decay = jnp.where(tril, jnp.exp(jnp.where(tril, cs - cs_row, 0.0)), 0.0)
    total = jnp.sum(a_row, axis=1, keepdims=True)
    return cs, decay, total


SCAN_UNROLL = 4


def _unrolled_loop(n, step, init):
    unroll = min(SCAN_UNROLL, n)
    assert n % unroll == 0

    def trip(i, carry):
        for u in range(unroll):
            carry = step(unroll * i + u, carry)
        return carry
    return lax.fori_loop(0, n // unroll, trip, init)


def _lane_pick(mat, lane, which):
    return jnp.sum(jnp.where(lane == which, mat, 0.0), axis=1, keepdims=True)


def _ssd_fwd(xbc_act, proj, dt_bias_row, a_log_row, rider=None):
    t = xbc_act.shape[0]
    nc = t // CHUNK
    cb, cc, cdt = D // LANES, (D + 2 * STATE_N) // LANES, OFF_DT // LANES

    def body(x_ref, b_ref, c_ref, dt_ref, dtb_ref, alog_ref, y_ref, st_ref):
        j = pl.program_id(0)
        eye, tril, _ = _chunk_masks()
        lane = lax.broadcasted_iota(jnp.int32, (1, LANES), 1)
        first = lane < HEAD_P
        a_row = -jnp.exp(alog_ref[...])
        a_heads = [jnp.sum(jnp.where(lane == 2 * j + h, a_row, 0.0), axis=1, keepdims=True) for h in range(2)]

        def chunk(c, hprev):
            rows = pl.ds(pl.multiple_of(c * CHUNK, CHUNK), CHUNK)
            xv, bm, cm = x_ref[rows, :], b_ref[rows, :], c_ref[rows, :]
            dt = _softplus(dt_ref[rows, :] + dtb_ref[...])
            st_ref[c] = hprev
            g = _mm_nt(cm, bm)
            ch = _mm(cm, hprev)
            dts = [_lane_pick(dt, lane, 2 * j + h) for h in range(2)]
            xdt = xv * jnp.where(first, dts[0], dts[1])
            ys, hs = [], []
            for h in range(2):
                cs, decay, total = _head_decay(dts[h], a_heads[h], eye, tril)
                y = _mm(g * decay, xdt) + jnp.exp(cs) * ch
                s = _mm_tn(bm * jnp.exp(total - cs), xdt)
                ys.append(y)
                hs.append(jnp.exp(total) * hprev + s)
            y_ref[rows, :] = jnp.where(first, ys[0], ys[1])
            return jnp.where(first, hs[0], hs[1])

        _unrolled_loop(nc, chunk, jnp.zeros((STATE_N, LANES), f32))

    blk = lambda f: pl.BlockSpec((t, LANES), f)
    return _call(
        body, name="ssd_fwd", grid=(D // LANES,),
        out_shape=(jax.ShapeDtypeStruct((t, D), f32), jax.ShapeDtypeStruct((nc, STATE_N, D), f32)),
        in_specs=[blk(lambda j: (0, j)), blk(lambda j: (0, cb + j // 4)), blk(lambda j: (0, cc + j // 4)),
                  blk(lambda j: (0, cdt)), _row(LANES), _row(LANES)],
        out_specs=(blk(lambda j: (0, j)), pl.BlockSpec((nc, STATE_N, LANES), lambda j: (0, 0, j))),
        sem=("arbitrary",), args=(xbc_act, xbc_act, xbc_act, proj, dt_bias_row, a_log_row), rider=rider)


def _ssd_bwd(xbc_act, proj, dt_bias_row, a_log_row, states, d_y, rider=None):
    t = xbc_act.shape[0]
    nc = t // CHUNK
    cb, cc, cdt = D // LANES, (D + 2 * STATE_N) // LANES, OFF_DT // LANES

    def body(x_ref, b_ref, c_ref, dt_ref, dtb_ref, alog_ref, st_ref, dy_ref, dx_ref, dbc_ref, ddt_ref, da_ref):
        grp, p = pl.program_id(0), pl.program_id(1)
        j = 4 * grp + p
        eye, tril, triu = _chunk_masks()
        lane = lax.broadcasted_iota(jnp.int32, (1, LANES), 1)
        first = lane < HEAD_P
        last_row = lax.broadcasted_iota(jnp.int32, (CHUNK, 1), 0) == CHUNK - 1
        a_row = -jnp.exp(alog_ref[...])
        a_heads = [jnp.sum(jnp.where(lane == 2 * j + h, a_row, 0.0), axis=1, keepdims=True) for h in range(2)]

        @pl.when(p == 0)
        def _():
            dbc_ref[...] = jnp.zeros_like(dbc_ref)

        @pl.when(j == 0)
        def _():
            ddt_ref[...] = jnp.zeros_like(ddt_ref)
            da_ref[...] = jnp.zeros_like(da_ref)

        def chunk(i, dh):
            c = nc - 1 - i
            rows = pl.ds(pl.multiple_of(c * CHUNK, CHUNK), CHUNK)
            xv, bm, cm = x_ref[rows, :], b_ref[rows, :], c_ref[rows, :]
            dtr = dt_ref[rows, :] + dtb_ref[...]
            dt = _softplus(dtr)
            hprev = st_ref[c]
            dy = dy_ref[rows, :]
            g = _mm_nt(cm, bm)
            dts = [_lane_pick(dt, lane, 2 * j + h) for h in range(2)]
            xdt = xv * jnp.where(first, dts[0], dts[1])
            dxs, dhs = [], []
            db_sum, dc_sum = None, None
            ddt_mat = jnp.zeros((CHUNK, LANES), f32)
            da_acc = jnp.zeros((1, LANES), f32)
            for h in range(2):
                mine = first if h == 0 else jnp.logical_not(first)
                cs, decay, total = _head_decay(dts[h], a_heads[h], eye, tril)
                e_cs, e_tot = jnp.exp(cs), jnp.exp(total)
                dec_s = jnp.exp(total - cs)
                dyh = jnp.where(mine, dy, 0.0)
                xdth = jnp.where(mine, xdt, 0.0)
                dhh = jnp.where(mine, dh, 0.0)
                hph = jnp.where(mine, hprev, 0.0)
                m = g * decay
                dm = _mm_nt(dyh, xdth)
                dg = dm * decay
                w = dm * m
                bdec = bm * dec_s
                dxdt = _mm_tn(m, dyh) + _mm(bdec, dhh)
                dc_off = _mm_nt(dyh, hph) * e_cs
                db_s = _mm_nt(xdth, dhh) * dec_s
                dc_h = _mm(dg, bm) + dc_off
                db_h = _mm_tn(dg, cm) + db_s
                r_s = jnp.sum(db_s * bm, axis=1, keepdims=True)
                dtotal = jnp.sum(r_s, axis=0, keepdims=True) + e_tot * jnp.sum(
                    jnp.sum(dhh * hph, axis=1, keepdims=True), axis=0, keepdims=True)
                dcs = (jnp.sum(w, axis=1, keepdims=True) - _to_col(jnp.sum(w, axis=0, keepdims=True), eye)
                       + jnp.sum(dc_off * cm, axis=1, keepdims=True) - r_s + jnp.where(last_row, dtotal, 0.0))
                da_col = jnp.sum(jnp.where(triu, _to_row(dcs, eye), 0.0), axis=1, keepdims=True)
                ddt = da_col * a_heads[h] + jnp.sum(jnp.where(mine, dxdt * xv, 0.0), axis=1, keepdims=True)
                ddt_mat = ddt_mat + jnp.where(lane == 2 * j + h, ddt, 0.0)
                da_acc = da_acc + jnp.where(lane == 2 * j + h, jnp.sum(da_col * dts[h], axis=0, keepdims=True), 0.0)
                dxs.append(dxdt * dts[h])
                dhs.append(e_tot * dhh + _mm_tn(cm * e_cs, dyh))
                db_sum = db_h if db_sum is None else db_sum + db_h
                dc_sum = dc_h if dc_sum is None else dc_sum + dc_h
            dx_ref[rows, :] = jnp.where(first, dxs[0], dxs[1])
            dbc_ref[0, rows, :] += db_sum
            dbc_ref[1, rows, :] += dc_sum
            ddt_ref[rows, :] += ddt_mat * jax.nn.sigmoid(dtr)
            da_ref[...] += da_acc * a_row
            return jnp.where(first, dhs[0], dhs[1])

        _unrolled_loop(nc, chunk, jnp.zeros((STATE_N, LANES), f32))

    blk = lambda f: pl.BlockSpec((t, LANES), f)
    return _call(
        body, name="ssd_bwd", grid=(2, 4),
        out_shape=(jax.ShapeDtypeStruct((t, D), f32), jax.ShapeDtypeStruct((2, t, 2 * STATE_N), f32),
                   jax.ShapeDtypeStruct((t, LANES), f32), jax.ShapeDtypeStruct((1, LANES), f32)),
        in_specs=[blk(lambda g, p: (0, 4 * g + p)), blk(lambda g, p: (0, cb + g)), blk(lambda g, p: (0, cc + g)),
                  blk(lambda g, p: (0, cdt)), _row(LANES), _row(LANES),
                  pl.BlockSpec((nc, STATE_N, LANES), lambda g, p: (0, 0, 4 * g + p)), blk(lambda g, p: (0, 4 * g + p))],
        out_specs=(blk(lambda g, p: (0, 4 * g + p)), pl.BlockSpec((2, t, LANES), lambda g, p: (0, 0, g)),
                   blk(lambda g, p: (0, 0)), _row(LANES)),
        sem=("arbitrary", "arbitrary"), args=(xbc_act, xbc_act, xbc_act, proj, dt_bias_row, a_log_row, states, d_y),
        rider=rider)


def _up_bwd(d_up, w_up, x1, mod, norm2_w, dx2, mix, w_out, rider=None):
    t = x1.shape[0]

    def body(dup_ref, wu_ref, x1_ref, mod_ref, nw_ref, dx2_ref, mix_ref, wo_ref,
             dx1_ref, dmix_ref, dys_ref, du_ref, st_ref):
        @pl.when(pl.program_id(0) == 0)
        def _():
            st_ref[...] = jnp.zeros_like(st_ref)

        nt = (((1,), (1,)), ((), ()))
        dh = None
        for k in range(4):
            lo = (k % 2) * UP_SHARD
            part = lax.dot_general(dup_ref[k // 2, :, lo:lo + UP_SHARD], wu_ref[k], nt, preferred_element_type=f32)
            dh = part if dh is None else dh + part
        x1 = x1_ref[...]
        rstd = lax.rsqrt(jnp.mean(x1 * x1, axis=-1, keepdims=True) + 1e-6)
        xh = x1 * rstd
        nw = nw_ref[...]
        sc = 1.0 + mod_ref[:, 4 * D:5 * D]
        st_ref[0:1, :] += jnp.sum(dh, axis=0, keepdims=True)
        st_ref[1:2, :] += jnp.sum(dh * xh * nw, axis=0, keepdims=True)
        st_ref[2:3, :] += jnp.sum(dh * sc * xh, axis=0, keepdims=True)
        dxh = dh * sc * nw
        dx1 = dx2_ref[...] + rstd * (dxh - xh * jnp.mean(dxh * xh, axis=-1, keepdims=True))
        dx1_ref[...] = dx1
        st_ref[3:4, :] += jnp.sum(dx1 * mix_ref[...], axis=0, keepdims=True)
        dmix = (mod_ref[:, 2 * D:3 * D] * dx1).astype(MX)
        dmix_ref[...] = dmix
        dys_ref[...] = lax.dot_general(dmix, wo_ref[0:D, :], nt, preferred_element_type=f32)
        du_ref[...] = lax.dot_general(dmix, wo_ref[D:2 * D, :], nt, preferred_element_type=f32)

    blk = pl.BlockSpec((TM, D), lambda i: (i, 0))
    return _call(
        body, name="up_bwd", grid=(t // TM,),
        out_shape=(jax.ShapeDtypeStruct((t, D), f32), jax.ShapeDtypeStruct((t, D), MX),
                   jax.ShapeDtypeStruct((t, D), f32), jax.ShapeDtypeStruct((t, D), f32),
                   jax.ShapeDtypeStruct((8, D), f32)),
        in_specs=[pl.BlockSpec((2, TM, D_FF), lambda i: (0, i, 0)), _resident((4, D, UP_SHARD)), blk, _row(6 * D), _row(),
                  blk, blk, _resident((2 * D, D))],
        out_specs=(blk, blk, blk, blk, pl.BlockSpec((8, D), lambda i: (0, 0))),
        sem=("arbitrary",), args=(d_up, w_up, x1, mod, norm2_w, dx2, mix, w_out), rider=rider)


def _ln_silu_bwd(d_u, u_conv, ln_w, ln_b):
    t = d_u.shape[0]

    def body(du_ref, u_ref, w_ref, b_ref, o_ref, st_ref):
        @pl.when(pl.program_id(0) == 0)
        def _():
            st_ref[...] = jnp.zeros_like(st_ref)

        u = u_ref[...]
        mu = jnp.mean(u, axis=-1, keepdims=True)
        uc = u - mu
        rstd = lax.rsqrt(jnp.mean(uc * uc, axis=-1, keepdims=True) + 1e-5)
        n = uc * rstd
        w = w_ref[...]
        dl = du_ref[...] * _dsilu(n * w + b_ref[...])
        st_ref[0:1, :] += jnp.sum(dl * n, axis=0, keepdims=True)
        st_ref[1:2, :] += jnp.sum(dl, axis=0, keepdims=True)
        dn = dl * w
        o_ref[...] = rstd * (dn - jnp.mean(dn, axis=-1, keepdims=True) - n * jnp.mean(dn * n, axis=-1, keepdims=True))

    blk = pl.BlockSpec((TM, D), lambda i: (i, 0))
    return pl.pallas_call(
        body, name="ln_silu_bwd", grid=(t // TM,),
        out_shape=(jax.ShapeDtypeStruct((t, D), f32), jax.ShapeDtypeStruct((8, D), f32)),
        in_specs=[blk, blk, _row(), _row()], out_specs=(blk, pl.BlockSpec((8, D), lambda i: (0, 0))),
        compiler_params=_cp("arbitrary"),
    )(d_u, u_conv, ln_w, ln_b)


def _ssd_gate_norm_bwd(d_out, y_scan, xbc_act, proj, d_skip_row, ssd_norm_w):
    t = d_out.shape[0]

    def body(do_ref, y_ref, xs_ref, z_ref, dsk_ref, nw_ref, dy_ref, dz_ref, st_ref):
        @pl.when(pl.program_id(0) == 0)
        def _():
            st_ref[...] = jnp.zeros_like(st_ref)

        xs = xs_ref[...]
        y = y_ref[...] + xs * dsk_ref[...]
        z = z_ref[...]
        s = _silu(z)
        yz = y * s
        rstd = lax.rsqrt(jnp.mean(yz * yz, axis=-1, keepdims=True) + 1e-6)
        n = yz * rstd
        do = do_ref[...]
        st_ref[0:1, :] += jnp.sum(do * n, axis=0, keepdims=True)
        dn = do * nw_ref[...]
        dyz = rstd * (dn - n * jnp.mean(dn * n, axis=-1, keepdims=True))
        dy = dyz * s
        dy_ref[...] = dy
        dz_ref[...] = (dyz * y * _dsilu(z)).astype(MX)
        st_ref[1:2, :] += jnp.sum(dy * xs, axis=0, keepdims=True)

    blk = pl.BlockSpec((TM, D), lambda i: (i, 0))
    return pl.pallas_call(
        body, name="ssd_gate_norm_bwd", grid=(t // TM,),
        out_shape=(jax.ShapeDtypeStruct((t, D), f32), jax.ShapeDtypeStruct((t, D), MX), jax.ShapeDtypeStruct((8, D), f32)),
        in_specs=[blk, blk, blk, blk, _row(), _row()], out_specs=(blk, blk, pl.BlockSpec((8, D), lambda i: (0, 0))),
        compiler_params=_cp("arbitrary"),
    )(d_out, y_scan, xbc_act, proj, d_skip_row, ssd_norm_w)


def _inproj_bwd(d_z, d_xraw, d_bcraw, d_conf, d_dt, w_pack, x, mod, norm1_w, dx1, after=None):
    t = x.shape[0]
    extra = [] if after is None else [after]

    def body(dz_ref, dx_ref, dbc_ref, dcf_ref, ddt_ref, w_ref, x_ref, mod_ref, nw_ref, dx1_ref, *rest):
        gx_ref, st_ref = rest[-2:]
        @pl.when(pl.program_id(0) == 0)
        def _():
            st_ref[...] = jnp.zeros_like(st_ref)

        nt = (((1,), (1,)), ((), ()))
        dot = lambda a, lo, hi: lax.dot_general(a, w_ref[:, lo:hi], nt, preferred_element_type=f32)
        dh = dot(dz_ref[...], OFF_Z, OFF_Z + D)
        dh = dh + dot(dx_ref[...], OFF_XBC, OFF_XBC + D)
        dh = dh + dot(dbc_ref[...], OFF_XBC + D, OFF_XBC + D_XBC)
        dh = dh + dot(dcf_ref[0], OFF_CA, OFF_CA + D)
        dh = dh + dot(dcf_ref[1], OFF_CG, OFF_CG + D)
        dh = dh + dot(ddt_ref[...].astype(MX), OFF_DT, OFF_DT + LANES)
        st_ref[3:4, 0:LANES] += jnp.sum(ddt_ref[...], axis=0, keepdims=True)
        xv = x_ref[...]
        rstd = lax.rsqrt(jnp.mean(xv * xv, axis=-1, keepdims=True) + 1e-6)
        xh = xv * rstd
        nw = nw_ref[...]
        sc = 1.0 + mod_ref[:, D:2 * D]
        st_ref[0:1, :] += jnp.sum(dh, axis=0, keepdims=True)
        st_ref[1:2, :] += jnp.sum(dh * xh * nw, axis=0, keepdims=True)
        st_ref[2:3, :] += jnp.sum(dh * sc * xh, axis=0, keepdims=True)
        dxh = dh * sc * nw
        gx_ref[...] = dx1_ref[...] + rstd * (dxh - xh * jnp.mean(dxh * xh, axis=-1, keepdims=True))

    blk = pl.BlockSpec((TM, D), lambda i: (i, 0))
    return _call(
        body, name="inproj_bwd", grid=(t // TM,),
        out_shape=(jax.ShapeDtypeStruct((t, D), f32), jax.ShapeDtypeStruct((8, D), f32)),
        in_specs=[blk, blk, pl.BlockSpec((TM, 2 * CW), lambda i: (i, 0)), pl.BlockSpec((2, TM, D), lambda i: (0, i, 0)),
                  pl.BlockSpec((TM, LANES), lambda i: (i, 0)), _resident((D, W_PACK)), blk, _row(6 * D), _row(), blk]
        + [ANY] * len(extra),
        out_specs=(blk, pl.BlockSpec((8, D), lambda i: (0, 0))),
        sem=("arbitrary",), args=(d_z, d_xraw, d_bcraw, d_conf, d_dt, w_pack, x, mod, norm1_w, dx1, *extra))[0]


def _wgrad(at, d, name, bn=256):
    k, t = at.shape
    n = d.shape[1]
    out_dtype = MX

    def body(a_ref, d_ref, o_ref):
        o_ref[...] = jnp.dot(a_ref[...], d_ref[...].astype(MX), preferred_element_type=f32).astype(out_dtype)

    return pl.pallas_call(
        body, name=name, grid=(n // bn,), out_shape=jax.ShapeDtypeStruct((k, n), out_dtype),
        in_specs=[_resident((k, t)), pl.BlockSpec((t, bn), lambda j: (0, j))],
        out_specs=pl.BlockSpec((k, bn), lambda j: (0, j)), compiler_params=_cp("arbitrary"),
    )(at, d)


def _wgrad_stacked(at, d, name, bn):
    out_dtype = MX
    k, t = at.shape
    s, _, n = d.shape
    nb = n // bn

    def body(a_ref, d_ref, o_ref):
        o_ref[0] = jnp.dot(a_ref[...], d_ref[0], preferred_element_type=f32).astype(out_dtype)

    return pl.pallas_call(
        body, name=name, grid=(s, nb), out_shape=jax.ShapeDtypeStruct((s * nb, k, bn), out_dtype),
        in_specs=[_resident((k, t)), pl.BlockSpec((1, t, bn), lambda i, j: (i, 0, j))],
        out_specs=pl.BlockSpec((1, k, bn), lambda i, j: (i * nb + j, 0, 0)), compiler_params=_cp("arbitrary", "arbitrary"),
    )(at, d)


def _pad_row(v, width=LANES):
    return jnp.pad(v.reshape(1, -1), ((0, 0), (0, width - v.size)))


def _quarters(a):
    return a.reshape(4, 2, a.shape[0] // 8, a.shape[1])


def _local_step(x, mod, target, w_pack, late, small, reducer=None):
    dtb_row, alog_row = _pad_row(small["dt_bias"]), _pad_row(small["a_log"])
    dskip_row = jnp.repeat(small["d_skip"].reshape(-1), HEAD_P).reshape(1, D)

    red = reducer

    def hosted(host, args, swap=None, scatter=None, gather=None, sums=()):
        if red is None:
            return host(*args)[0]
        riders = ([red.scatter(scatter)] if scatter else []) + ([red.swap(*swap)] if swap else [])
        riders += [_SwapSumsRider([red.sums[n] for n in sums])] if sums else []
        riders += [_GatherRider([gather[0]], *gather[1:])] if gather is not None else []
        both = _Riders(riders)
        outs, extra = host(*args, rider=both)
        extra = both.split(extra)
        if scatter:
            red.scattered(scatter, extra.pop(0))
        if swap:
            red.swapped(swap[0], extra.pop(0))
        if sums:
            red.others.update(zip(sums, extra.pop(0)))
        return (outs, extra[0][0]) if gather is not None else outs

    w_out, w_up, w_down = late
    if red is None:
        proj, h_t = hosted(_ln_inproj, (x, mod, small["norm1_w"], w_pack))
        xbc_act, = hosted(_ssd_conv_fwd, (proj, small["ssd_conv_w"], small["ssd_conv_b"]))
        y_scan, states = hosted(_ssd_fwd, (xbc_act, proj, dtb_row, alog_row))
        u_conv, = hosted(_glu_conv_fwd, (proj, small["conf_conv_w"], small["conf_conv_b"]))
    else:
        (proj, h_t), w_out = hosted(_ln_inproj, (x, mod, small["norm1_w"], w_pack), gather=(w_out,))
        (xbc_act,), w_up = hosted(_ssd_conv_fwd, (proj, small["ssd_conv_w"], small["ssd_conv_b"]), gather=(w_up, 0, UP_EARLY_ROWS))
        (y_scan, states), w_up = hosted(_ssd_fwd, (xbc_act, proj, dtb_row, alog_row), gather=(w_up, UP_EARLY_ROWS, None))
        (u_conv,), w_down = hosted(_glu_conv_fwd, (proj, small["conf_conv_w"], small["conf_conv_b"]), gather=(w_down,))
        w_out, w_up, w_down = w_out.reshape(2 * D, D), w_up.reshape(4, D, UP_SHARD), w_down.reshape(D_FF, D)
    y_ssd, y_ssd_t = _ssd_gate_norm(y_scan, xbc_act, proj, dskip_row, small["ssd_norm_w"])
    u, u_t = _ln_silu(u_conv, small["conf_ln_w"], small["conf_ln_b"])
    mix, x1, h2_t, up = _outproj_ln2_up(y_ssd, u, w_out, x, mod, small["norm2_w"], w_up)
    act, act_t = _ffn_conv_fwd(up, small["ffn_conv_w"], small["ffn_conv_b"])[0]
    dx2, d_ffn, d_act, st_down = _down_loss(act, w_down, x1, mod, small["final_norm_w"], target)

    g_down = _quarters(_wgrad(act_t, d_ffn, "wgrad_down"))
    d_up, dw_ffn, db_ffn = hosted(_ffn_conv_bwd, (up, small["ffn_conv_w"], small["ffn_conv_b"], d_act), swap=("w_down", g_down))
    g_up = _wgrad_stacked(h2_t, d_up, "wgrad_up", D_FF // 2).reshape(4, 2, D // 2, UP_SHARD)
    dx1, d_mix, d_yssd, d_u, st_up = hosted(_up_bwd, (d_up, w_up, x1, mod, small["norm2_w"], dx2, mix, w_out),
                                            scatter="w_down", swap=("w_up", g_up))
    g_out = _quarters(jnp.concatenate([_wgrad(y_ssd_t, d_mix, "wgrad_out_y"), _wgrad(u_t, d_mix, "wgrad_out_u")], axis=0))
    d_uconv, st_ln = _ln_silu_bwd(d_u, u_conv, small["conf_ln_w"], small["conf_ln_b"])
    d_conf, dw_conf, db_conf = hosted(_glu_conv_bwd, (proj, small["conf_conv_w"], d_uconv), scatter="w_up",
                                      swap=("w_out", g_out))
    d_y, d_z, st_gn = _ssd_gate_norm_bwd(d_yssd, y_scan, xbc_act, proj, dskip_row, small["ssd_norm_w"])
    d_xs, d_bc, d_dt, d_alog = hosted(_ssd_bwd, (xbc_act, proj, dtb_row, alog_row, states, d_y), scatter="w_out")
    d_xraw, dw_sx, db_sx = _ssd_conv_bwd_x(proj, small["ssd_conv_w"], small["ssd_conv_b"], d_xs, d_y, dskip_row)
    d_bcraw, dw_sbc, db_sbc = _ssd_conv_bwd_bc(proj, small["ssd_conv_w"], small["ssd_conv_b"], d_bc)
    g_in = _unpack_g_in(dict(
        z=_wgrad(h_t, d_z, "wgrad_in_z"), x=_wgrad(h_t, d_xraw, "wgrad_in_x"), bc=_wgrad(h_t, d_bcraw, "wgrad_in_bc"),
        conf=_wgrad_stacked(h_t, d_conf, "wgrad_in_conf", D), dt=_wgrad(h_t, d_dt, "wgrad_in_dt", bn=LANES)))
    g_in = g_in.reshape(4, 2, D // 2, W_IN_SHARD_PAD)
    args = (d_z, d_xraw, d_bcraw, d_conf, d_dt, w_pack, x, mod, small["norm1_w"], dx1)
    if red is None:
        grad_x, st_in = _inproj_bwd(*args)
    else:
        done = ("w_out", "w_up", "w_down")
        both = _Riders([red.swap("w_in", g_in), _SwapSumsRider([red.sums[n] for n in done])])
        handles, token = _split_start(both, "swap_start_w_in")
        grad_x, st_in = _inproj_bwd(*args, after=token)
        thru, outs = _split_wait(both, "swap_wait_w_in", handles, st_in)
        red.grads["w_in"] = thru[0]
        red.sums.update(zip(done, thru[1:]))
        got, others = both.split(outs)
        red.swapped("w_in", got)
        red.others.update(zip(done, others))

    gsmall = _pack_small_grads(st_in, st_up, st_down, st_ln, st_gn, d_alog, dw_sx, dw_sbc, db_sx, db_sbc, dw_conf, db_conf,
                               dw_ffn, db_ffn)
    gbig = None if reducer is not None else dict(w_in=g_in, w_out=g_out, w_up=g_up, w_down=g_down)
    return st_down[2, 0], grad_x, gbig, gsmall


VECTORS = ("ada_b", "norm1_w", "ssd_conv_b", "dt_bias", "a_log", "d_skip", "ssd_norm_w", "conf_conv_b", "conf_ln_w",
           "conf_ln_b", "norm2_w", "ffn_conv_b", "final_norm_w")
VECTOR_SIZES = (6 * D, D, D_XBC, HEADS, HEADS, HEADS, D, D, D, D, D, 2 * D_FF, D)
CONVS = {"ssd_conv_w": (K_SSD, D_XBC), "conf_conv_w": (K_CONF, D), "ffn_conv_w": (K_FFN, 2 * D_FF)}


def _pack_rows(items):
    n = -(-sum(w for _, w in items) // (8 * LANES)) * LANES
    while True:
        fill, place = [0] * 8, {}
        for key, w in sorted(items, key=lambda kv: -kv[1]):
            rows = [r for r in range(8) if fill[r] + w <= n]
            if not rows:
                break
            place[key] = (rows[0], fill[rows[0]])
            fill[rows[0]] += w
        if len(place) == len(items):
            return n, place
        n += LANES


FRONT_N, FRONT = _pack_rows([("c", D)] + [((nm, j), cols // 4) for nm, (taps, cols) in CONVS.items() for j in range(taps)])
BACK_N, BACK = _pack_rows([(nm, -(-sz // LANES) * LANES) for nm, sz in zip(VECTORS, VECTOR_SIZES)]
                          + [((nm, j), cols) for nm, (taps, cols) in CONVS.items() for j in range(taps)] + [("loss", LANES)])
_VM = pltpu.CompilerParams(vmem_limit_bytes=VMEM_LIMIT)


def _pack_front(c, shards):
    def body(c_ref, *refs):
        o_ref = refs[-1]
        o_ref[...] = jnp.zeros_like(o_ref)
        r, o = FRONT["c"]
        o_ref[r:r + 1, o:o + D] = c_ref[...]
        for ref, (nm, (taps, cols)) in zip(refs, CONVS.items()):
            for j in range(taps):
                r, o = FRONT[(nm, j)]
                o_ref[r:r + 1, o:o + cols // 4] = ref[0, j:j + 1, :]

    return pl.pallas_call(body, name="pack_front", out_shape=jax.ShapeDtypeStruct((8, FRONT_N), f32),
                          compiler_params=_VM)(c, *shards)


def _unpack_front(got):
    def body(g_ref, c_ref, *outs):
        r, o = FRONT["c"]
        for d in range(8):
            c_ref[d:d + 1, :] = g_ref[8 * d + r:8 * d + r + 1, o:o + D]
        for ref, (nm, (taps, cols)) in zip(outs, CONVS.items()):
            cw = cols // 4
            for j in range(taps):
                r, o = FRONT[(nm, j)]
                for k in range(4):
                    ref[j:j + 1, k * cw:(k + 1) * cw] = g_ref[16 * k + r:16 * k + r + 1, o:o + cw]

    return pl.pallas_call(
        body, name="unpack_front", compiler_params=_VM,
        out_shape=(jax.ShapeDtypeStruct((8, D), f32),) + tuple(jax.ShapeDtypeStruct(tc, f32) for tc in CONVS.values()),
    )(got)


def _pack_small_grads(st_in, st_up, st_down, st_ln, st_gn, d_alog, dw_sx, dw_sbc, db_sx, db_sbc, dw_conf, db_conf, dw_ffn,
                      db_ffn):
    def body(in_ref, up_ref, dn_ref, ln_ref, gn_ref, al_ref, wx_ref, wbc_ref, bx_ref, bbc_ref, wc_ref, bc_ref, wf_ref, bf_ref,
             o_ref):
        def put(key, val, shift=0):
            r, o = BACK[key]
            o_ref[r:r + 1, o + shift:o + shift + val.shape[1]] = val

        o_ref[...] = jnp.zeros_like(o_ref)
        for i, piece in enumerate((in_ref[0:1, :], in_ref[1:2, :], up_ref[3:4, :], up_ref[0:1, :], up_ref[1:2, :],
                                   dn_ref[1:2, :])):
            put("ada_b", piece, i * D)
        put("norm1_w", in_ref[2:3, :])
        put("ssd_conv_b", bx_ref[...])
        put("ssd_conv_b", bbc_ref[...], D)
        put("dt_bias", in_ref[3:4, 0:LANES])
        put("a_log", al_ref[...])
        lane = lax.broadcasted_iota(jnp.int32, (1, LANES), 1)
        col = lax.broadcasted_iota(jnp.int32, (1, D), 1)
        per_col = gn_ref[1:2, :]
        d_skip = jnp.zeros((1, LANES), f32)
        for h in range(HEADS):
            in_head = jnp.logical_and(col >= h * HEAD_P, col < (h + 1) * HEAD_P)
            s = jnp.sum(jnp.where(in_head, per_col, 0.0), axis=1, keepdims=True)
            d_skip = d_skip + jnp.where(lane == h, s, 0.0)
        put("d_skip", d_skip)
        put("ssd_norm_w", gn_ref[0:1, :])
        put("conf_conv_b", bc_ref[...])
        put("conf_ln_w", ln_ref[0:1, :])
        put("conf_ln_b", ln_ref[1:2, :])
        put("norm2_w", up_ref[2:3, :])
        put("ffn_conv_b", bf_ref[0])
        put("ffn_conv_b", bf_ref[1], D_FF)
        put("final_norm_w", dn_ref[0:1, :])
        put("loss", dn_ref[2:3, 0:LANES])
        for j in range(K_SSD):
            put(("ssd_conv_w", j), wx_ref[j:j + 1, :])
            put(("ssd_conv_w", j), wbc_ref[j:j + 1, :], D)
        for j in range(K_CONF):
            put(("conf_conv_w", j), wc_ref[j:j + 1, :])
        for j in range(K_FFN):
            put(("ffn_conv_w", j), wf_ref[0, j:j + 1, :])
            put(("ffn_conv_w", j), wf_ref[1, j:j + 1, :], D_FF)

    return pl.pallas_call(body, name="pack_small_grads", out_shape=jax.ShapeDtypeStruct((8, BACK_N), f32), compiler_params=_VM)(
        st_in, st_up, st_down, st_ln, st_gn, d_alog, dw_sx, dw_sbc, db_sx, db_sbc, dw_conf, db_conf, dw_ffn, db_ffn)


def _small_adamw(got, chip, w, m, v):
    names = VECTORS + tuple(CONVS)
    n_par = len(names)

    def body(chip_ref, g_ref, *refs):
        ins, outs = refs[:3 * n_par], refs[3 * n_par:]
        dm_ref, loss_ref, outs = outs[0], outs[1], outs[2:]
        chip_id = chip_ref[0]

        def summed(key, width):
            r, o = BACK[key]
            s = g_ref[r:r + 1, o:o + width]
            for d in range(1, 8):
                s = s + g_ref[8 * d + r:8 * d + r + 1, o:o + width]
            return s

        def mine(full, cw):
            out = full[:, 0:cw]
            for k in range(1, 4):
                out = jnp.where(chip_id == k, full[:, k * cw:(k + 1) * cw], out)
            return out

        r, o = BACK["ada_b"]
        for d in range(8):
            dm_ref[d:d + 1, :] = mine(g_ref[8 * d + r:8 * d + r + 1, o:o + 6 * D], 6 * D // 4)
        loss_ref[...] = summed("loss", LANES)
        for i, (nm, size) in enumerate(zip(VECTORS, VECTOR_SIZES)):
            g = summed(nm, -(-size // LANES) * LANES)[:, 0:size]
            res = _adam_math(ins[3 * i][...], g, ins[3 * i + 1][...], ins[3 * i + 2][...])
            for ref, val in zip(outs[4 * i:4 * i + 4], (g,) + res):
                ref[...] = val
        for i, (nm, (taps, cols)) in enumerate(CONVS.items(), start=len(VECTORS)):
            for j in range(taps):
                g = mine(summed((nm, j), cols), cols // 4)
                res = _adam_math(ins[3 * i][0, j:j + 1, :], g, ins[3 * i + 1][0, j:j + 1, :], ins[3 * i + 2][0, j:j + 1, :])
                for ref, val in zip(outs[4 * i:4 * i + 4], (g,) + res):
                    ref[0, j:j + 1, :] = val

    params = [a[nm] for nm in names for a in (w, m, v)]
    whole = lambda s: pl.BlockSpec(s, lambda i, chip, nd=len(s): (0,) * nd)
    out_shape = [jax.ShapeDtypeStruct((8, 6 * D // 4), f32), jax.ShapeDtypeStruct((1, LANES), f32)]
    out_shape += [jax.ShapeDtypeStruct(w[nm].shape, f32) for nm in names for _ in range(4)]
    outs = pl.pallas_call(
        body, name="small_adamw", out_shape=tuple(out_shape), compiler_params=_VM,
        grid_spec=pltpu.PrefetchScalarGridSpec(
            num_scalar_prefetch=1, grid=(1,), in_specs=[whole(got.shape)] + [whole(p.shape) for p in params],
            out_specs=tuple(whole(s.shape) for s in out_shape)),
    )(_scalar(chip), got, *params)
    return outs[0], outs[1][0, 0], {nm: outs[2 + 4 * i:6 + 4 * i] for i, nm in enumerate(names)}


W_IN_COLS = 4624
W_IN_SHARD = W_IN_COLS // 4
W_IN_SHARD_PAD = 1280
_SEGMENTS = ((0, 1024, OFF_Z), (1024, 2560, OFF_XBC), (2560, 2576, OFF_DT), (2576, 3600, OFF_CA), (3600, 4624, OFF_CG))


def _in_pieces(bounds=()):
    out = []
    for k in range(4):
        s0, s1 = k * W_IN_SHARD, (k + 1) * W_IN_SHARD
        for lo, hi, off in _SEGMENTS:
            a, b = max(lo, s0), min(hi, s1)
            while a < b:
                p = off + a - lo
                e = min([b - a] + [c - p for c in bounds if c > p])
                out.append((k, a - s0, p, e))
                a += e
    return out


def _pack_w_in(shards):
    pieces = _in_pieces()

    def body(s_ref, o_ref):
        o_ref[:, OFF_DT:W_PACK] = jnp.zeros((TM, W_PACK - OFF_DT), MX)
        for k, c, p, n in pieces:
            o_ref[:, p:p + n] = s_ref[k, :, c:c + n]

    return pl.pallas_call(
        body, name="pack_w_in", grid=(D // TM,), out_shape=jax.ShapeDtypeStruct((D, W_PACK), MX),
        in_specs=[pl.BlockSpec((4, TM, W_IN_SHARD_PAD), lambda i: (0, i, 0))],
        out_specs=pl.BlockSpec((TM, W_PACK), lambda i: (i, 0)), compiler_params=_cp("arbitrary"),
    )(shards)


def _unpack_g_in(g):
    srcs = ((OFF_Z, D), (OFF_XBC, D), (OFF_XBC + D, 2 * CW), (OFF_CA, D), (OFF_CG, D), (OFF_DT, LANES))
    pieces = _in_pieces(tuple(o for o, _ in srcs) + tuple(o + n for o, n in srcs))

    def body(z_ref, x_ref, bc_ref, cf_ref, dt_ref, o_ref):
        read = (lambda lo, hi: z_ref[:, lo:hi], lambda lo, hi: x_ref[:, lo:hi], lambda lo, hi: bc_ref[:, lo:hi],
                lambda lo, hi: cf_ref[0, :, lo:hi], lambda lo, hi: cf_ref[1, :, lo:hi], lambda lo, hi: dt_ref[:, lo:hi])
        o_ref[:, :, W_IN_SHARD - 4:W_IN_SHARD_PAD] = jnp.zeros((4, TM, W_IN_SHARD_PAD - W_IN_SHARD + 4), MX)
        for k, c, p, n in pieces:
            i = [q for q, (o, w) in enumerate(srcs) if o <= p < o + w][0]
            o_ref[k, :, c:c + n] = read[i](p - srcs[i][0], p - srcs[i][0] + n)

    blk = lambda w: pl.BlockSpec((TM, w), lambda i: (i, 0))
    return pl.pallas_call(
        body, name="unpack_g_in", grid=(D // TM,), out_shape=jax.ShapeDtypeStruct((4, D, W_IN_SHARD_PAD), MX),
        in_specs=[blk(D), blk(D), blk(2 * CW), pl.BlockSpec((2, TM, D), lambda i: (0, i, 0)), blk(LANES)],
        out_specs=pl.BlockSpec((4, TM, W_IN_SHARD_PAD), lambda i: (0, i, 0)), compiler_params=_cp("arbitrary"),
    )(g["z"], g["x"], g["bc"], g["conf"], g["dt"])


def _scalar(v):
    return jnp.reshape(v, (1,)).astype(jnp.int32)


def _cast_into_slot(w, width, chip):
    r, c = w.shape
    h = r // 2
    tm = _row_tile(h)
    nj = h // tm

    def body(chip_ref, w_ref, o_ref):
        v = w_ref[...].astype(MX)
        o_ref[0, 0] = v if width == c else jnp.concatenate([v, jnp.zeros((tm, width - c), MX)], axis=1)

    return pl.pallas_call(
        body, name=f"cast_into_slot_{r}x{c}", out_shape=jax.ShapeDtypeStruct((4, 2, h, width), MX),
        grid_spec=pltpu.PrefetchScalarGridSpec(
            num_scalar_prefetch=1, grid=(2, nj),
            in_specs=[pl.BlockSpec((tm, c), lambda i, j, chip: (i * nj + j, 0))],
            out_specs=pl.BlockSpec((1, 1, tm, width), lambda i, j, chip: (chip[0], i, j, 0))),
        compiler_params=_cp("arbitrary", "arbitrary"),
    )(_scalar(chip), w)


def _columns_first(w):
    return jnp.transpose(w, (2, 0, 1))


def _cast_into_slot_w_in(w_t, chip):
    h = D // 2
    nj = h // TM
    pad = W_IN_SHARD_PAD - W_IN_SHARD

    def body(chip_ref, w_ref, o_ref):
        cols = jnp.concatenate([w_ref[:, 0, :], jnp.zeros((pad, TM), f32)], axis=0)
        o_ref[0, 0] = cols.T.astype(MX)

    return pl.pallas_call(
        body, name="cast_into_slot_w_in", out_shape=jax.ShapeDtypeStruct((4, 2, h, W_IN_SHARD_PAD), MX),
        grid_spec=pltpu.PrefetchScalarGridSpec(
            num_scalar_prefetch=1, grid=(2, nj),
            in_specs=[pl.BlockSpec((W_IN_SHARD, 1, TM), lambda i, j, chip: (0, 0, i * nj + j))],
            out_specs=pl.BlockSpec((1, 1, TM, W_IN_SHARD_PAD), lambda i, j, chip: (chip[0], i, j, 0))),
        compiler_params=_cp("arbitrary", "arbitrary"),
    )(_scalar(chip), w_t)


def _adamw_w_in(w_t, mine, other, m_t, v_t, core):
    h = D // 2
    nj = h // TM

    def body(core_ref, w_ref, a_ref, b_ref, m_ref, v_ref, g_ref, d_ref, nm_ref, nv_ref):
        g = jnp.where(pl.program_id(0) == core_ref[0], a_ref[...], b_ref[...]).T[0:W_IN_SHARD, :]
        g_ref[:, 0, :] = g
        d_ref[:, 0, :], nm_ref[:, 0, :], nv_ref[:, 0, :] = _adam_math(w_ref[:, 0, :], g, m_ref[:, 0, :], v_ref[:, 0, :])

    blk = pl.BlockSpec((W_IN_SHARD, 1, TM), lambda i, j, core: (0, 0, i * nj + j))
    gblk = pl.BlockSpec((TM, W_IN_SHARD_PAD), lambda i, j, core: (j, 0))
    return pl.pallas_call(
        body, name="adamw_w_in", out_shape=tuple([jax.ShapeDtypeStruct((W_IN_SHARD, 1, D), f32)] * 4),
        grid_spec=pltpu.PrefetchScalarGridSpec(
            num_scalar_prefetch=1, grid=(2, nj), in_specs=[blk, gblk, gblk, blk, blk], out_specs=(blk,) * 4),
        compiler_params=_cp("arbitrary", "arbitrary"),
    )(_scalar(core), w_t, mine, other, m_t, v_t)


ANY = pl.BlockSpec(memory_space=pl.ANY)


def _place():
    x, y, c = lax.axis_index("x"), lax.axis_index("y"), lax.axis_index("c")
    return x, y, c, [(1 - x, y), (x, 1 - y), (1 - x, 1 - y)]


def _gather_rows(block, rider=None):
    m_per, n = block.shape
    ri, ro = (len(rider.inputs), len(rider.out_shape)) if rider is not None else (0, 0)

    def body(x_ref, *refs):
        r_in, out_ref, r_out = refs[:ri], refs[ri], refs[ri + 1:ri + 1 + ro]
        send_sems, recv_sems, local_sem, *r_scr = refs[ri + 1 + ro:]
        x, y, c, chips = _place()
        me, sibling = (x, y, c), (x, y, 1 - c)

        def rows(px, py, pc):
            return out_ref.at[pl.ds((4 * px + 2 * py + pc) * m_per, m_per), :]

        def copy(k, blk, to, src=None):
            return pltpu.make_async_remote_copy(
                src_ref=rows(*blk) if src is None else src, dst_ref=rows(*blk), send_sem=send_sems.at[k],
                recv_sem=recv_sems.at[k], device_id=to, device_id_type=MESH)

        mine = pltpu.make_async_copy(x_ref, rows(*me), local_sem)
        mine.start()
        first = [copy(0, me, sibling, src=x_ref)]
        first += [copy(1 + j, me, (*chip, c), src=x_ref) for j, chip in enumerate(chips)]
        for cp in first:
            cp.start()
        if rider is not None:
            rider.start(r_in, r_out, r_scr)
        passed = [copy(4 + j, (*chip, c), sibling) for j, chip in enumerate(chips)]
        for j, chip in enumerate(chips):
            copy(1 + j, (*chip, c), me).wait_recv()
            passed[j].start()
        copy(0, sibling, me).wait_recv()
        for j, chip in enumerate(chips):
            copy(4 + j, (*chip, 1 - c), me).wait_recv()
        for cp in first + passed:
            cp.wait_send()
        mine.wait()
        if rider is not None:
            rider.finish(r_in, r_out, r_scr)

    vmem = pl.BlockSpec(memory_space=pltpu.VMEM)
    gathered = jax.ShapeDtypeStruct((8 * m_per, n), block.dtype)
    if rider is None:
        return pl.pallas_call(
            body, name=f"gather_rows_{m_per}x{n}", out_shape=gathered, in_specs=[vmem], out_specs=vmem,
            scratch_shapes=[pltpu.SemaphoreType.DMA((7,)), pltpu.SemaphoreType.DMA((7,)), pltpu.SemaphoreType.DMA],
            compiler_params=_VM)(block)
    outs = pl.pallas_call(
        body, name=f"gather_rows_{m_per}x{n}", out_shape=(gathered,) + tuple(rider.out_shape),
        in_specs=[vmem] + [ANY] * ri, out_specs=(vmem,) + (ANY,) * ro,
        input_output_aliases={1 + i: 1 + j for i, j in rider.aliases.items()},
        scratch_shapes=[pltpu.SemaphoreType.DMA((7,)), pltpu.SemaphoreType.DMA((7,)), pltpu.SemaphoreType.DMA] + list(rider.scratch),
        compiler_params=_VM)(block, *rider.inputs)
    return outs[0], tuple(outs[1:])


class _GatherRider:
    def __init__(self, slots, row0=0, nrows=None):
        n = len(slots)
        self.n = n
        self.rows = (row0, slots[0].shape[2] - row0 if nrows is None else nrows)
        self.inputs = list(slots)
        self.out_shape = [jax.ShapeDtypeStruct(s.shape, s.dtype) for s in slots]
        self.scratch = [pltpu.SemaphoreType.DMA((n, 6)), pltpu.SemaphoreType.DMA((n, 6))]
        self.aliases = {a: a for a in range(n)}

    def _copy(self, outs, sems, a, j, k, half, to):
        dst = outs[a].at[k, half, pl.ds(*self.rows)]
        return pltpu.make_async_remote_copy(src_ref=dst, dst_ref=dst, send_sem=sems[0].at[a, j], recv_sem=sems[1].at[a, j],
                                            device_id=to, device_id_type=MESH)

    def _first(self, outs, sems):
        x, y, c, chips = _place()
        return [self._copy(outs, sems, a, j, 2 * x + y, c, (*chip, c)) for a in range(self.n) for j, chip in enumerate(chips)]

    def start(self, ins, outs, sems):
        for cp in self._first(outs, sems):
            cp.start()

    def finish(self, ins, outs, sems):
        x, y, c, chips = _place()
        passed = []
        for a in range(self.n):
            for j, (px, py) in enumerate(chips):
                self._copy(outs, sems, a, j, 2 * px + py, c, (x, y, c)).wait_recv()
                fwd = self._copy(outs, sems, a, 3 + j, 2 * px + py, c, (x, y, 1 - c))
                fwd.start()
                passed.append(fwd)
        for a in range(self.n):
            for j, (px, py) in enumerate(chips):
                self._copy(outs, sems, a, 3 + j, 2 * px + py, 1 - c, (x, y, c)).wait_recv()
        for cp in self._first(outs, sems) + passed:
            cp.wait_send()


class _ScatterRider:
    def __init__(self, parts, row0=0, nrows=None):
        n = len(parts)
        self.n = n
        self.rows = (row0, parts[0].shape[1] - row0 if nrows is None else nrows)
        self.inputs = list(parts)
        self.out_shape = [jax.ShapeDtypeStruct((3, self.rows[1], p.shape[2]), p.dtype) for p in parts]
        self.scratch = [pltpu.SemaphoreType.DMA((3 * n,)), pltpu.SemaphoreType.DMA((3 * n,))]
        self.aliases = {}

    def _copies(self, ins, outs, sems):
        x, y, c, chips = _place()
        return [pltpu.make_async_remote_copy(
            src_ref=ins[a].at[2 * px + py, pl.ds(*self.rows)], dst_ref=outs[a].at[j], send_sem=sems[0].at[3 * a + j],
            recv_sem=sems[1].at[3 * a + j], device_id=(px, py, c), device_id_type=MESH)
            for a in range(self.n) for j, (px, py) in enumerate(chips)]

    def start(self, ins, outs, sems):
        for cp in self._copies(ins, outs, sems):
            cp.start()

    def finish(self, ins, outs, sems):
        for cp in self._copies(ins, outs, sems):
            cp.wait()


HBM = pl.BlockSpec(memory_space=pltpu.HBM)
SEM = pl.BlockSpec(memory_space=pltpu.SEMAPHORE)
EFFECT = pltpu.SideEffectType.DATAFLOW_SIDE_EFFECTING


def _split_start(rider, name, after=None):
    ni, no, ns = len(rider.inputs), len(rider.out_shape), len(rider.scratch)
    extra = [] if after is None else [after]

    def body(*refs):
        ins, lands = refs[:ni], refs[ni:ni + no]
        sems = refs[ni + no + len(extra):ni + no + len(extra) + ns]
        rider.start(ins, lands, sems)
        refs[-1][...] = jnp.zeros_like(refs[-1])

    bufs = list(rider.inputs) + [lax.empty(s.shape, s.dtype) for s in rider.out_shape]
    outs = pl.pallas_call(
        body, name=name,
        out_shape=tuple(rider.scratch) + tuple(pltpu.HBM(b.shape, b.dtype) for b in bufs) + (jax.ShapeDtypeStruct((8, LANES), f32),),
        in_specs=[HBM] * (ni + no) + [ANY] * len(extra),
        out_specs=(SEM,) * ns + (HBM,) * (ni + no) + (pl.BlockSpec(memory_space=pltpu.VMEM),),
        input_output_aliases={i: ns + i for i in range(ni + no)},
        compiler_params=pltpu.CompilerParams(has_side_effects=EFFECT),
    )(*[pltpu.with_memory_space_constraint(b, pltpu.HBM) for b in bufs], *extra)
    return outs[:-1], outs[-1]


def _split_wait(rider, name, handles, after):
    ni, no, ns = len(rider.inputs), len(rider.out_shape), len(rider.scratch)
    sems, bufs = handles[:ns], handles[ns:]

    def body(*refs):
        rider.finish(refs[:ni], refs[ni:ni + no], refs[ni + no:ni + no + ns])

    outs = pl.pallas_call(
        body, name=name, out_shape=tuple(pltpu.HBM(b.shape, b.dtype) for b in bufs),
        in_specs=[HBM] * (ni + no) + [SEM] * ns + [ANY], out_specs=(HBM,) * (ni + no),
        input_output_aliases={i: i for i in range(ni + no)}, compiler_params=pltpu.CompilerParams(has_side_effects=EFFECT),
    )(*bufs, *sems, after)
    return outs[:ni], outs[ni:]


def _ride_alone(rider, name):
    n = len(rider.inputs)

    def body(*refs):
        ins, outs, sems = refs[:n], refs[n:n + len(rider.out_shape)], refs[n + len(rider.out_shape):]
        rider.start(ins, outs, sems)
        rider.finish(ins, outs, sems)

    return pl.pallas_call(
        body, name=name, out_shape=tuple(rider.out_shape), in_specs=[ANY] * n, out_specs=tuple([ANY] * len(rider.out_shape)),
        input_output_aliases=dict(rider.aliases), scratch_shapes=list(rider.scratch),
    )(*rider.inputs)


class _SwapRider:
    def __init__(self, grads):
        n = len(grads)
        self.n = n
        self.inputs = list(grads)
        self.out_shape = [jax.ShapeDtypeStruct((4,) + g.shape[2:], g.dtype) for g in grads]
        self.scratch = [pltpu.SemaphoreType.DMA((4 * n,)), pltpu.SemaphoreType.DMA((4 * n,))]
        self.aliases = {}

    def _copies(self, ins, outs, sems):
        x, y, c, _ = _place()
        return [pltpu.make_async_remote_copy(
            src_ref=ins[a].at[k, 1 - c], dst_ref=outs[a].at[k], send_sem=sems[0].at[4 * a + k], recv_sem=sems[1].at[4 * a + k],
            device_id=(x, y, 1 - c), device_id_type=MESH) for a in range(self.n) for k in range(4)]

    def start(self, ins, outs, sems):
        for cp in self._copies(ins, outs, sems):
            cp.start()

    def finish(self, ins, outs, sems):
        for cp in self._copies(ins, outs, sems):
            cp.wait()


class _Riders:
    def __init__(self, riders):
        self.riders = list(riders)
        self.inputs = [a for r in riders for a in r.inputs]
        self.out_shape = [s for r in riders for s in r.out_shape]
        self.scratch = [s for r in riders for s in r.scratch]
        self.aliases = {}
        i = o = 0
        for r in riders:
            self.aliases.update({i + a: o + b for a, b in r.aliases.items()})
            i, o = i + len(r.inputs), o + len(r.out_shape)

    def _each(self, ins, outs, sems):
        i = o = s = 0
        for r in self.riders:
            yield r, ins[i:i + len(r.inputs)], outs[o:o + len(r.out_shape)], sems[s:s + len(r.scratch)]
            i, o, s = i + len(r.inputs), o + len(r.out_shape), s + len(r.scratch)

    def start(self, ins, outs, sems):
        for r, a, b, c in self._each(ins, outs, sems):
            r.start(a, b, c)

    def finish(self, ins, outs, sems):
        for r, a, b, c in self._each(ins, outs, sems):
            r.finish(a, b, c)

    def split(self, outs):
        res, o = [], 0
        for r in self.riders:
            res.append(outs[o:o + len(r.out_shape)])
            o += len(r.out_shape)
        return res


class _Reducer:
    def __init__(self, chip, core):
        self.chip, self.core, self.grads, self.parts, self.sums, self.others = chip, core, {}, {}, {}, {}

    def swap(self, name, grad):
        self.grads[name] = grad
        return _SwapRider([grad])

    def swapped(self, name, got):
        self.parts[name] = _add_pair(self.grads[name], got[0], self.core, name)

    def scatter(self, name, row0=0, nrows=None):
        return _ScatterRider([self.parts[name]], row0, nrows)

    def scattered(self, name, others):
        self.sums[name] = _add_chips(self.parts[name], others[0], self.chip, name)


class _SwapSumsRider:
    def __init__(self, halves):
        n = len(halves)
        self.n = n
        self.inputs = list(halves)
        self.out_shape = [jax.ShapeDtypeStruct(s.shape, s.dtype) for s in halves]
        self.scratch = [pltpu.SemaphoreType.DMA((n,)), pltpu.SemaphoreType.DMA((n,))]
        self.aliases = {}

    def _copies(self, ins, outs, sems):
        x, y, c, _ = _place()
        return [pltpu.make_async_remote_copy(
            src_ref=ins[a], dst_ref=outs[a], send_sem=sems[0].at[a], recv_sem=sems[1].at[a],
            device_id=(x, y, 1 - c), device_id_type=MESH) for a in range(self.n)]

    def start(self, ins, outs, sems):
        for cp in self._copies(ins, outs, sems):
            cp.start()

    def finish(self, ins, outs, sems):
        for cp in self._copies(ins, outs, sems):
            cp.wait()


def _row_tile(r):
    for tm in (TM, 176, 128, 64, 32, 16, 8):
        if r % tm == 0:
            return tm
    return r


def _add_pair(mine, got, core, name):
    k, _, h, c = mine.shape
    tm = _row_tile(h)

    def body(core_ref, a_ref, b_ref, o_ref):
        o_ref[0] = (a_ref[0, 0].astype(f32) + b_ref[0].astype(f32)).astype(MX)

    blk = pl.BlockSpec((1, tm, c), lambda i, j, core: (i, j, 0))
    return pl.pallas_call(
        body, name="add_pair_" + name, out_shape=jax.ShapeDtypeStruct((k, h, c), MX),
        grid_spec=pltpu.PrefetchScalarGridSpec(
            num_scalar_prefetch=1, grid=(k, h // tm),
            in_specs=[pl.BlockSpec((1, 1, tm, c), lambda i, j, core: (i, core[0], j, 0)), blk], out_specs=blk),
        compiler_params=_cp("arbitrary", "arbitrary"),
    )(_scalar(core), mine, got)


def _add_chips(parts, others, chip, name, row0=0):
    _, n, c = others.shape
    tm = _row_tile(n)
    assert row0 % tm == 0
    i0 = row0 // tm

    def body(chip_ref, a_ref, b_ref, o_ref):
        s = a_ref[0].astype(f32) + b_ref[0].astype(f32)
        o_ref[...] = (s + b_ref[1].astype(f32)) + b_ref[2].astype(f32)

    return pl.pallas_call(
        body, name="add_chips_" + name, out_shape=jax.ShapeDtypeStruct((n, c), f32),
        grid_spec=pltpu.PrefetchScalarGridSpec(
            num_scalar_prefetch=1, grid=(n // tm,),
            in_specs=[pl.BlockSpec((1, tm, c), lambda i, chip: (chip[0], i0 + i, 0)),
                      pl.BlockSpec((3, tm, c), lambda i, chip: (0, i, 0))],
            out_specs=pl.BlockSpec((tm, c), lambda i, chip: (i, 0))),
        compiler_params=_cp("arbitrary"),
    )(_scalar(chip), parts, others)


def _adam_math(w, g, m, v):
    m = ADAM_B1 * m + (1.0 - ADAM_B1) * g
    v = ADAM_B2 * v + (1.0 - ADAM_B2) * (g * g)
    m_hat = m / (1.0 - ADAM_B1 ** ADAM_STEP)
    v_hat = v / (1.0 - ADAM_B2 ** ADAM_STEP)
    return -ADAM_LR * (m_hat / (jnp.sqrt(v_hat) + ADAM_EPS) + ADAM_WD * w), m, v


def _adamw_halves(w, mine, other, m, v, core, name, after):
    r, c = w.shape
    h = r // 2
    tm = _row_tile(h)
    nj = h // tm
    cg = mine.shape[1]

    def body(core_ref, w_ref, a_ref, b_ref, m_ref, v_ref, after_ref, g_ref, d_ref, nm_ref, nv_ref):
        g = jnp.where(pl.program_id(0) == core_ref[0], a_ref[:, 0:c], b_ref[:, 0:c])
        g_ref[...] = g
        d_ref[...], nm_ref[...], nv_ref[...] = _adam_math(w_ref[...], g, m_ref[...], v_ref[...])

    blk = pl.BlockSpec((tm, c), lambda i, j, core: (i * nj + j, 0))
    gblk = pl.BlockSpec((tm, cg), lambda i, j, core: (j, 0))
    return _call(body, name=name, grid=(2, nj), out_shape=[jax.ShapeDtypeStruct((r, c), f32)] * 4,
                 in_specs=[blk, gblk, gblk, blk, blk, ANY], out_specs=(blk,) * 4, sem=("arbitrary", "arbitrary"),
                 prefetch=(_scalar(core),), args=(w, mine, other, m, v, after))[0]


def _ada_forward(c_all, ada_w):
    def body(c_ref, w_ref, o_ref):
        o_ref[...] = jnp.dot(_silu(c_ref[...]).astype(MX), w_ref[...].astype(MX), preferred_element_type=f32)

    return pl.pallas_call(body, name="ada_forward", out_shape=jax.ShapeDtypeStruct((8, ada_w.shape[1]), f32),
                          compiler_params=pltpu.CompilerParams(vmem_limit_bytes=VMEM_LIMIT))(c_all, ada_w)


def _ada_adamw(c_all_t, d_mod, w, m, v, after):
    r, c = w.shape
    tm = TM

    def body(ct_ref, dm_ref, w_ref, m_ref, v_ref, after_ref, g_ref, d_ref, nm_ref, nv_ref):
        ca = _silu(ct_ref[...])
        g = ca[:, 0:1] * dm_ref[0:1, :]
        for b in range(1, 8):
            g = g + ca[:, b:b + 1] * dm_ref[b:b + 1, :]
        g_ref[...] = g
        d_ref[...], nm_ref[...], nv_ref[...] = _adam_math(w_ref[...], g, m_ref[...], v_ref[...])

    blk = pl.BlockSpec((tm, c), lambda i: (i, 0))
    return _call(body, name="ada_adamw", grid=(r // tm,), out_shape=[jax.ShapeDtypeStruct((r, c), f32)] * 4,
                 in_specs=[pl.BlockSpec((tm, 8), lambda i: (i, 0)), pl.BlockSpec((8, c), lambda i: (0, 0)), blk, blk, blk, ANY],
                 out_specs=(blk,) * 4, sem=("arbitrary",), args=(c_all_t, d_mod, w, m, v, after))[0]


WEIGHTS = ("ada_w", "ada_b", "norm1_w", "w_in", "ssd_conv_w", "ssd_conv_b", "dt_bias", "a_log", "d_skip", "ssd_norm_w",
           "conf_conv_w", "conf_conv_b", "conf_ln_w", "conf_ln_b", "w_out", "norm2_w", "w_up", "ffn_conv_w", "ffn_conv_b",
           "w_down", "final_norm_w")


def kernel(x, c, ada_w, ada_b, norm1_w, w_in, ssd_conv_w, ssd_conv_b, dt_bias, a_log, d_skip, ssd_norm_w, conf_conv_w, conf_conv_b, conf_ln_w, conf_ln_b, w_out, norm2_w, w_up, ffn_conv_w, ffn_conv_b, w_down, final_norm_w, loss_target, m_ada_w, m_ada_b, m_norm1_w, m_w_in, m_ssd_conv_w, m_ssd_conv_b, m_dt_bias, m_a_log, m_d_skip, m_ssd_norm_w, m_conf_conv_w, m_conf_conv_b, m_conf_ln_w, m_conf_ln_b, m_w_out, m_norm2_w, m_w_up, m_ffn_conv_w, m_ffn_conv_b, m_w_down, m_final_norm_w, v_ada_w, v_ada_b, v_norm1_w, v_w_in, v_ssd_conv_w, v_ssd_conv_b, v_dt_bias, v_a_log, v_d_skip, v_ssd_norm_w, v_conf_conv_w, v_conf_conv_b, v_conf_ln_w, v_conf_ln_b, v_w_out, v_norm2_w, v_w_up, v_ffn_conv_w, v_ffn_conv_b, v_w_down, v_final_norm_w):
    given = dict(locals())
    w = {n: given[n] for n in WEIGHTS}
    mom = {n: given["m_" + n] for n in WEIGHTS}
    var = {n: given["v_" + n] for n in WEIGHTS}
    chip = 2 * lax.axis_index("x") + lax.axis_index("y")
    me = 2 * chip + lax.axis_index("c")

    core = lax.axis_index("c")
    a_in = _cast_into_slot_w_in(_columns_first(w_in), chip)
    got, (a_in,) = _gather_rows(_pack_front(c, [w[n] for n in CONVS]), _GatherRider([a_in], 0, D // 4))
    c_all, *convs = _unpack_front(got)
    conv_full = dict(zip(CONVS, convs))

    got, (a_in,) = _gather_rows(_ada_forward(c_all, ada_w[0]), _GatherRider([a_in], D // 4, D // 4))
    mod_cols = got.reshape(8, 8, -1)[0::2]
    mod = lax.dynamic_index_in_dim(mod_cols, me, axis=1, keepdims=False).reshape(1, 6 * D) + ada_b
    w_pack = _pack_w_in(a_in.reshape(4, D, W_IN_SHARD_PAD))
    late = (_cast_into_slot(w_out[0], D, chip), _cast_into_slot(w_up[0], UP_SHARD, chip), _cast_into_slot(w_down[0], D, chip))

    flat = lambda a: a.reshape(1, -1) if a.ndim == 1 else a
    small = {n: flat(w[n]) for n in VECTORS if n != "ada_b"}
    small.update(conv_full)
    reducer = _Reducer(chip, core)
    _, grad_x, _, gsmall = _local_step(x[0], mod, loss_target[0], w_pack, late, small, reducer)
    grads, delta, new_m, new_v = {}, {}, {}, {}

    names = VECTORS + tuple(CONVS)
    d_mod_mine, loss, res = _small_adamw(_gather_rows(gsmall), chip, *[{n: flat(d[n]) for n in names} for d in (w, mom, var)])
    for n in names:
        grads[n], delta[n], new_m[n], new_v[n] = [r.reshape(w[n].shape) for r in res[n]]

    scatter = reducer.scatter("w_in")
    handles, token = _split_start(scatter, "scatter_start_w_in", after=d_mod_mine)
    for n in ("w_up", "w_down", "w_out"):
        res = _adamw_halves(w[n][0], reducer.sums[n], reducer.others[n], mom[n][0], var[n][0], core, "adamw_" + n, token)
        grads[n], delta[n], new_m[n], new_v[n] = [r[None] for r in res]
    res = _ada_adamw(c_all.T, d_mod_mine, ada_w[0], m_ada_w[0], v_ada_w[0], token)
    grads["ada_w"], delta["ada_w"], new_m["ada_w"], new_v["ada_w"] = [r[None] for r in res]
    (reducer.parts["w_in"],), others = _split_wait(scatter, "scatter_wait_w_in", handles, res[1])
    reducer.scattered("w_in", others)
    reducer.others["w_in"], = _ride_alone(_SwapSumsRider([reducer.sums["w_in"]]), "swap_sums_w_in")
    res = _adamw_w_in(_columns_first(w_in), reducer.sums["w_in"], reducer.others["w_in"], _columns_first(m_w_in),
                      _columns_first(v_w_in), core)
    grads["w_in"], delta["w_in"], new_m["w_in"], new_v["w_in"] = [jnp.transpose(r, (1, 2, 0)) for r in res]

    return (loss, grad_x[None], *[grads[n] for n in WEIGHTS], *[delta[n] for n in WEIGHTS],
            *[new_m[n] for n in WEIGHTS], *[new_v[n] for n in WEIGHTS])
```

```python
import functools

import jax
import jax.numpy as jnp
from jax import lax
from jax.experimental import pallas as pl
from jax.experimental.pallas import tpu as pltpu

f32 = jnp.float32
MX = jnp.bfloat16

D = 1024
HEADS = 16
HEAD_P = 64
STATE_N = 128
D_XBC = 1536
D_FF = 2816
UP_SHARD = 2 * D_FF // 4
UP_EARLY_ROWS = 128
K_SSD, K_CONF, K_FFN = 4, 31, 3
CHUNK = 128
OFF_Z, OFF_XBC, OFF_CA, OFF_CG, OFF_DT = 0, 1024, 2560, 3584, 4608
W_PACK = 4736
TM = 256
CW = 256
RC = 64
LANES = 128
VMEM_LIMIT = 56 * 1024 * 1024

ADAM_LR, ADAM_B1, ADAM_B2, ADAM_EPS, ADAM_WD, ADAM_STEP = 0.001, 0.9, 0.999, 1e-08, 0.01, 10

MESH = pl.DeviceIdType.MESH


def _cp(*sem):
    return pltpu.CompilerParams(dimension_semantics=sem, vmem_limit_bytes=VMEM_LIMIT)


def _resident(shape):
    nd = len(shape)
    return pl.BlockSpec(shape, lambda *_: (0,) * nd, pipeline_mode=pl.Buffered(1))


def _row(width=D):
    return pl.BlockSpec((1, width), lambda *_: (0, 0))


def _call(body, *, name, grid, in_specs, out_specs, out_shape, args, sem, scratch_shapes=(), prefetch=(), rider=None):
    ni, no, ns, npf = len(in_specs), len(out_specs), len(scratch_shapes), len(prefetch)
    ri, ro = (len(rider.inputs), len(rider.out_shape)) if rider is not None else (0, 0)

    def full(*refs):
        pre, refs = refs[:npf], refs[npf:]
        base_in, r_in = refs[:ni], refs[ni:ni + ri]
        base_out, r_out = refs[ni + ri:ni + ri + no], refs[ni + ri + no:ni + ri + no + ro]
        base_scr, r_scr = refs[ni + ri + no + ro:ni + ri + no + ro + ns], refs[ni + ri + no + ro + ns:]
        if rider is None:
            return body(*pre, *base_in, *base_out, *base_scr)
        ids = [pl.program_id(a) for a in range(len(grid))]
        first = functools.reduce(jnp.logical_and, [i == 0 for i in ids])
        last = functools.reduce(jnp.logical_and, [i == g - 1 for i, g in zip(ids, grid)])

        @pl.when(first)
        def _():
            rider.start(r_in, r_out, r_scr)

        body(*pre, *base_in, *base_out, *base_scr)

        @pl.when(last)
        def _():
            rider.finish(r_in, r_out, r_scr)

    extra = dict(shapes=[], scratch=[], aliases={}, inputs=[]) if rider is None else dict(
        shapes=rider.out_shape, scratch=rider.scratch, inputs=rider.inputs,
        aliases={npf + ni + i: no + j for i, j in rider.aliases.items()})
    outs = pl.pallas_call(
        full, name=name, out_shape=tuple(out_shape) + tuple(extra["shapes"]), input_output_aliases=extra["aliases"],
        grid_spec=pltpu.PrefetchScalarGridSpec(
            num_scalar_prefetch=npf, grid=grid, in_specs=list(in_specs) + [ANY] * ri,
            out_specs=tuple(out_specs) + (ANY,) * ro, scratch_shapes=list(scratch_shapes) + list(extra["scratch"])),
        compiler_params=_cp(*sem),
    )(*prefetch, *args, *extra["inputs"])
    return tuple(outs[:no]), tuple(outs[no:])


def _silu(v):
    return v * jax.nn.sigmoid(v)


def _dsilu(v):
    s = jax.nn.sigmoid(v)
    return s * (1.0 + v * (1.0 - s))


def _softplus(v):
    return jnp.maximum(v, 0.0) + jnp.log1p(jnp.exp(-jnp.abs(v)))


def _mm(a, b):
    return jnp.dot(a.astype(MX), b.astype(MX), preferred_element_type=f32)


def _mm_nt(a, b):
    return lax.dot_general(a.astype(MX), b.astype(MX), (((1,), (1,)), ((), ())), preferred_element_type=f32)


def _mm_tn(a, b):
    return lax.dot_general(a.astype(MX), b.astype(MX), (((0,), (0,)), ((), ())), preferred_element_type=f32)


def _ln_inproj(x, mod, norm1_w, w_pack, rider=None):
    t = x.shape[0]

    def body(x_ref, mod_ref, nw_ref, w_ref, proj_ref, ht_ref):
        xv = x_ref[...]
        rstd = lax.rsqrt(jnp.mean(xv * xv, axis=-1, keepdims=True) + 1e-6)
        h = (xv * rstd * nw_ref[...]) * (1.0 + mod_ref[:, D:2 * D]) + mod_ref[:, 0:D]
        hb = h.astype(MX)
        ht_ref[...] = hb.T
        proj_ref[...] = jnp.dot(hb, w_ref[...], preferred_element_type=f32)

    return _call(
        body, name="ln_inproj", grid=(t // TM,),
        out_shape=(jax.ShapeDtypeStruct((t, W_PACK), f32), jax.ShapeDtypeStruct((D, t), MX)),
        in_specs=[pl.BlockSpec((TM, D), lambda i: (i, 0)), _row(6 * D), _row(), _resident((D, W_PACK))],
        out_specs=(pl.BlockSpec((TM, W_PACK), lambda i: (i, 0)), pl.BlockSpec((D, TM), lambda i: (0, i))),
        sem=("arbitrary",), args=(x, mod, norm1_w, w_pack), rider=rider)


def _ssd_gate_norm(y_scan, xbc_act, proj, d_skip_row, ssd_norm_w):
    t = y_scan.shape[0]

    def body(y_ref, xs_ref, z_ref, dsk_ref, nw_ref, o_ref, ot_ref):
        y = y_ref[...] + xs_ref[...] * dsk_ref[...]
        yz = y * _silu(z_ref[...])
        rstd = lax.rsqrt(jnp.mean(yz * yz, axis=-1, keepdims=True) + 1e-6)
        out = (yz * rstd * nw_ref[...]).astype(MX)
        o_ref[...] = out
        ot_ref[...] = out.T

    blk = pl.BlockSpec((TM, D), lambda i: (i, 0))
    return pl.pallas_call(
        body, name="ssd_gate_norm", grid=(t // TM,),
        out_shape=(jax.ShapeDtypeStruct((t, D), MX), jax.ShapeDtypeStruct((D, t), MX)),
        in_specs=[blk, blk, blk, _row(), _row()], out_specs=(blk, pl.BlockSpec((D, TM), lambda i: (0, i))),
        compiler_params=_cp("arbitrary"),
    )(y_scan, xbc_act, proj, d_skip_row, ssd_norm_w)


def _ln_silu(u_conv, ln_w, ln_b):
    t = u_conv.shape[0]

    def body(u_ref, w_ref, b_ref, o_ref, ot_ref):
        u = u_ref[...]
        mu = jnp.mean(u, axis=-1, keepdims=True)
        uc = u - mu
        rstd = lax.rsqrt(jnp.mean(uc * uc, axis=-1, keepdims=True) + 1e-5)
        out = _silu(uc * rstd * w_ref[...] + b_ref[...]).astype(MX)
        o_ref[...] = out
        ot_ref[...] = out.T

    blk = pl.BlockSpec((TM, D), lambda i: (i, 0))
    return pl.pallas_call(
        body, name="ln_silu", grid=(t // TM,),
        out_shape=(jax.ShapeDtypeStruct((t, D), MX), jax.ShapeDtypeStruct((D, t), MX)),
        in_specs=[blk, _row(), _row()], out_specs=(blk, pl.BlockSpec((D, TM), lambda i: (0, i))),
        compiler_params=_cp("arbitrary"),
    )(u_conv, ln_w, ln_b)


def _outproj_ln2_up(y_ssd, u, w_out, x, mod, norm2_w, w_up):
    t = x.shape[0]

    def body(y_ref, u_ref, wo_ref, x_ref, mod_ref, nw_ref, wu_ref, mix_ref, x1_ref, h2t_ref, up_ref):
        mix = jnp.dot(y_ref[...], wo_ref[0:D, :], preferred_element_type=f32)
        mix = mix + jnp.dot(u_ref[...], wo_ref[D:2 * D, :], preferred_element_type=f32)
        mix_ref[...] = mix
        x1 = x_ref[...] + mod_ref[:, 2 * D:3 * D] * mix
        x1_ref[...] = x1
        rstd = lax.rsqrt(jnp.mean(x1 * x1, axis=-1, keepdims=True) + 1e-6)
        h2 = ((x1 * rstd * nw_ref[...]) * (1.0 + mod_ref[:, 4 * D:5 * D]) + mod_ref[:, 3 * D:4 * D]).astype(MX)
        h2t_ref[...] = h2.T
        for k in range(4):
            up_ref[:, k * UP_SHARD:(k + 1) * UP_SHARD] = jnp.dot(h2, wu_ref[k], preferred_element_type=f32)

    blk = pl.BlockSpec((TM, D), lambda i: (i, 0))
    return pl.pallas_call(
        body, name="outproj_ln2_up", grid=(t // TM,),
        out_shape=(jax.ShapeDtypeStruct((t, D), f32), jax.ShapeDtypeStruct((t, D), f32),
                   jax.ShapeDtypeStruct((D, t), MX), jax.ShapeDtypeStruct((t, 2 * D_FF), f32)),
        in_specs=[blk, blk, _resident((2 * D, D)), blk, _row(6 * D), _row(), _resident((4, D, UP_SHARD))],
        out_specs=(blk, blk, pl.BlockSpec((D, TM), lambda i: (0, i)), pl.BlockSpec((TM, 2 * D_FF), lambda i: (i, 0))),
        compiler_params=_cp("arbitrary"),
    )(y_ssd, u, w_out, x, mod, norm2_w, w_up)


def _down_loss(act, w_down, x1, mod, final_norm_w, target):
    t = x1.shape[0]

    def body(a_ref, wd_ref, x1_ref, mod_ref, wf_ref, tgt_ref, dx2_ref, dffn_ref, dact_ref, st_ref):
        @pl.when(pl.program_id(0) == 0)
        def _():
            st_ref[...] = jnp.zeros_like(st_ref)

        g2 = mod_ref[:, 5 * D:6 * D]
        ffn = jnp.dot(a_ref[...], wd_ref[...], preferred_element_type=f32)
        x2 = x1_ref[...] + g2 * ffn
        rstd = lax.rsqrt(jnp.mean(x2 * x2, axis=-1, keepdims=True) + 1e-6)
        xh = x2 * rstd
        wf = wf_ref[...]
        err = xh * wf - tgt_ref[...]
        dy = err * (1.0 / D)
        dxh = dy * wf
        dx2 = rstd * (dxh - xh * jnp.mean(dxh * xh, axis=-1, keepdims=True))
        dx2_ref[...] = dx2
        dffn = (g2 * dx2).astype(MX)
        dffn_ref[...] = dffn
        dact_ref[...] = lax.dot_general(dffn, wd_ref[...], (((1,), (1,)), ((), ())), preferred_element_type=f32)
        st_ref[0:1, :] += jnp.sum(dy * xh, axis=0, keepdims=True)
        st_ref[1:2, :] += jnp.sum(dx2 * ffn, axis=0, keepdims=True)
        st_ref[2:3, :] += jnp.sum(0.5 * jnp.mean(err * err, axis=-1, keepdims=True), axis=0, keepdims=True)

    blk = pl.BlockSpec((TM, D), lambda i: (i, 0))
    ablk = pl.BlockSpec((TM, D_FF), lambda i: (i, 0))
    return pl.pallas_call(
        body, name="down_loss", grid=(t // TM,),
        out_shape=(jax.ShapeDtypeStruct((t, D), f32), jax.ShapeDtypeStruct((t, D), MX),
                   jax.ShapeDtypeStruct((t, D_FF), f32), jax.ShapeDtypeStruct((8, D), f32)),
        in_specs=[ablk, _resident((D_FF, D)), blk, _row(6 * D), _row(), blk],
        out_specs=(blk, blk, ablk, pl.BlockSpec((8, D), lambda i: (0, 0))),
        compiler_params=_cp("arbitrary"),
    )(act, w_down, x1, mod, final_norm_w, target)


def _pad_of(k):
    return 8 * ((k - 1 + 7) // 8)


def _causal_win(ref, r, t, pad):
    base = pl.multiple_of(r * RC, RC)
    prev = ref[pl.ds(pl.multiple_of(jnp.maximum(base - pad, 0), 8), pad), :]
    prev = jnp.where(r > 0, prev, 0.0)
    return jnp.concatenate([prev, ref[pl.ds(base, RC), :]], axis=0)


def _anti_win(ref, r, t, pad):
    base = pl.multiple_of(r * RC, RC)
    nxt = ref[pl.ds(pl.multiple_of(jnp.minimum(base + RC, t - pad), 8), pad), :]
    nxt = jnp.where(r < t // RC - 1, nxt, 0.0)
    return jnp.concatenate([ref[pl.ds(base, RC), :], nxt], axis=0)


def _shifted(win, offsets):
    for r in range(8):
        mine = [o for o in offsets if o % 8 == r]
        if mine:
            rolled = win if r == 0 else pltpu.roll(win, win.shape[0] - r, 0)
            for o in mine:
                yield o, rolled[o - r:o - r + RC, :]


def _conv_taps(win, w_ref, k, pad):
    first = pad - (k - 1)
    acc = None
    for o, rows in _shifted(win, range(first, first + k)):
        term = w_ref[o - first:o - first + 1, :] * rows
        acc = term if acc is None else acc + term
    return acc


def _corr_taps(win, w_ref, k):
    acc = None
    for o, rows in _shifted(win, range(k)):
        term = w_ref[k - 1 - o:k - o, :] * rows
        acc = term if acc is None else acc + term
    return acc


def _dw_accumulate(dw_scr, d, win, k, pad):
    first = pad - (k - 1)
    for o, rows in _shifted(win, range(first, first + k)):
        j = o - first
        prod = d * rows
        dw_scr[8 * j:8 * j + 8, :] += prod.reshape(RC // 8, 8, prod.shape[-1]).sum(axis=0)


def _dw_finish(dw_scr, dw_ref, k):
    for j in range(k):
        dw_ref[j:j + 1, :] = jnp.sum(dw_scr[8 * j:8 * j + 8, :], axis=0, keepdims=True)


def _rows8(v):
    return v.reshape(RC // 8, 8, v.shape[-1]).sum(axis=0)


def _ssd_conv_fwd(proj, conv_w, conv_b, rider=None):
    t = proj.shape[0]
    pad = _pad_of(K_SSD)
    c0 = OFF_XBC // CW

    def body(x_ref, w_ref, b_ref, o_ref):
        def step(r, carry):
            win = _causal_win(x_ref, r, t, pad)
            o_ref[pl.ds(pl.multiple_of(r * RC, RC), RC), :] = _silu(_conv_taps(win, w_ref, K_SSD, pad) + b_ref[...])
            return carry
        lax.fori_loop(0, t // RC, step, 0)

    return _call(
        body, name="ssd_conv_fwd", grid=(D_XBC // CW,), out_shape=(jax.ShapeDtypeStruct((t, D_XBC), f32),),
        in_specs=[pl.BlockSpec((t, CW), lambda j: (0, c0 + j)), pl.BlockSpec((K_SSD, CW), lambda j: (0, j)),
                  pl.BlockSpec((1, CW), lambda j: (0, j))],
        out_specs=(pl.BlockSpec((t, CW), lambda j: (0, j)),), sem=("arbitrary",), args=(proj, conv_w, conv_b), rider=rider)


def _glu_conv_fwd(proj, conv_w, conv_b, rider=None):
    t = proj.shape[0]
    pad = _pad_of(K_CONF)
    ca, cg = OFF_CA // CW, OFF_CG // CW

    def body(a_ref, g_ref, w_ref, b_ref, o_ref, v_scr):
        def glu(r, carry):
            rows = pl.ds(pl.multiple_of(r * RC, RC), RC)
            v_scr[rows, :] = a_ref[rows, :] * jax.nn.sigmoid(g_ref[rows, :])
            return carry
        lax.fori_loop(0, t // RC, glu, 0)

        def step(r, carry):
            win = _causal_win(v_scr, r, t, pad)
            o_ref[pl.ds(pl.multiple_of(r * RC, RC), RC), :] = _conv_taps(win, w_ref, K_CONF, pad) + b_ref[...]
            return carry
        lax.fori_loop(0, t // RC, step, 0)

    return _call(
        body, name="glu_conv_fwd", grid=(D // CW,), out_shape=(jax.ShapeDtypeStruct((t, D), f32),),
        in_specs=[pl.BlockSpec((t, CW), lambda j: (0, ca + j)), pl.BlockSpec((t, CW), lambda j: (0, cg + j)),
                  pl.BlockSpec((K_CONF, CW), lambda j: (0, j)), pl.BlockSpec((1, CW), lambda j: (0, j))],
        out_specs=(pl.BlockSpec((t, CW), lambda j: (0, j)),),
        scratch_shapes=[pltpu.VMEM((t, CW), f32)], sem=("arbitrary",), args=(proj, proj, conv_w, conv_b), rider=rider)


def _ffn_conv_fwd(up, conv_w, conv_b, rider=None):
    t = up.shape[0]
    pad = _pad_of(K_FFN)
    nb = D_FF // CW

    def body(g_ref, v_ref, wg_ref, wv_ref, bg_ref, bv_ref, o_ref, ot_ref):
        def step(r, carry):
            gc = _conv_taps(_causal_win(g_ref, r, t, pad), wg_ref, K_FFN, pad) + bg_ref[...]
            vc = _conv_taps(_causal_win(v_ref, r, t, pad), wv_ref, K_FFN, pad) + bv_ref[...]
            o_ref[pl.ds(pl.multiple_of(r * RC, RC), RC), :] = (_silu(gc) * vc).astype(MX)
            return carry
        lax.fori_loop(0, t // RC, step, 0)
        ot_ref[...] = o_ref[...].T

    return _call(
        body, name="ffn_conv_fwd", grid=(nb,),
        out_shape=(jax.ShapeDtypeStruct((t, D_FF), MX), jax.ShapeDtypeStruct((D_FF, t), MX)),
        in_specs=[pl.BlockSpec((t, CW), lambda j: (0, j)), pl.BlockSpec((t, CW), lambda j: (0, nb + j)),
                  pl.BlockSpec((K_FFN, CW), lambda j: (0, j)), pl.BlockSpec((K_FFN, CW), lambda j: (0, nb + j)),
                  pl.BlockSpec((1, CW), lambda j: (0, j)), pl.BlockSpec((1, CW), lambda j: (0, nb + j))],
        out_specs=(pl.BlockSpec((t, CW), lambda j: (0, j)), pl.BlockSpec((CW, t), lambda j: (j, 0))), sem=("arbitrary",),
        args=(up, up, conv_w, conv_w, conv_b, conv_b), rider=rider)


def _ffn_conv_bwd(up, conv_w, conv_b, d_act, rider=None):
    t = up.shape[0]
    pad = _pad_of(K_FFN)
    nb = D_FF // CW

    def body(g_ref, v_ref, wg_ref, wv_ref, bg_ref, bv_ref, da_ref, dup_ref, dw_ref, db_ref,
             dg_scr, dv_scr, dwg_scr, dwv_scr, db_scr):
        dwg_scr[...] = jnp.zeros_like(dwg_scr)
        dwv_scr[...] = jnp.zeros_like(dwv_scr)
        db_scr[...] = jnp.zeros_like(db_scr)

        def first(r, carry):
            rows = pl.ds(pl.multiple_of(r * RC, RC), RC)
            gwin = _causal_win(g_ref, r, t, pad)
            vwin = _causal_win(v_ref, r, t, pad)
            gc = _conv_taps(gwin, wg_ref, K_FFN, pad) + bg_ref[...]
            vc = _conv_taps(vwin, wv_ref, K_FFN, pad) + bv_ref[...]
            da = da_ref[rows, :]
            dgc = da * vc * _dsilu(gc)
            dvc = da * _silu(gc)
            dg_scr[rows, :] = dgc
            dv_scr[rows, :] = dvc
            _dw_accumulate(dwg_scr, dgc, gwin, K_FFN, pad)
            _dw_accumulate(dwv_scr, dvc, vwin, K_FFN, pad)
            db_scr[0:8, :] += _rows8(dgc)
            db_scr[8:16, :] += _rows8(dvc)
            return carry
        lax.fori_loop(0, t // RC, first, 0)

        def second(r, carry):
            rows = pl.ds(pl.multiple_of(r * RC, RC), RC)
            dup_ref[0, rows, :] = _corr_taps(_anti_win(dg_scr, r, t, pad), wg_ref, K_FFN).astype(MX)
            dup_ref[1, rows, :] = _corr_taps(_anti_win(dv_scr, r, t, pad), wv_ref, K_FFN).astype(MX)
            return carry
        lax.fori_loop(0, t // RC, second, 0)

        for j in range(K_FFN):
            dw_ref[0, j:j + 1, :] = jnp.sum(dwg_scr[8 * j:8 * j + 8, :], axis=0, keepdims=True)
            dw_ref[1, j:j + 1, :] = jnp.sum(dwv_scr[8 * j:8 * j + 8, :], axis=0, keepdims=True)
        db_ref[0] = jnp.sum(db_scr[0:8, :], axis=0, keepdims=True)
        db_ref[1] = jnp.sum(db_scr[8:16, :], axis=0, keepdims=True)

    return _call(
        body, name="ffn_conv_bwd", grid=(nb,),
        out_shape=(jax.ShapeDtypeStruct((2, t, D_FF), MX), jax.ShapeDtypeStruct((2, K_FFN, D_FF), f32),
                   jax.ShapeDtypeStruct((2, 1, D_FF), f32)),
        in_specs=[pl.BlockSpec((t, CW), lambda j: (0, j)), pl.BlockSpec((t, CW), lambda j: (0, nb + j)),
                  pl.BlockSpec((K_FFN, CW), lambda j: (0, j)), pl.BlockSpec((K_FFN, CW), lambda j: (0, nb + j)),
                  pl.BlockSpec((1, CW), lambda j: (0, j)), pl.BlockSpec((1, CW), lambda j: (0, nb + j)),
                  pl.BlockSpec((t, CW), lambda j: (0, j))],
        out_specs=(pl.BlockSpec((2, t, CW), lambda j: (0, 0, j)), pl.BlockSpec((2, K_FFN, CW), lambda j: (0, 0, j)),
                   pl.BlockSpec((2, 1, CW), lambda j: (0, 0, j))),
        scratch_shapes=[pltpu.VMEM((t, CW), f32), pltpu.VMEM((t, CW), f32), pltpu.VMEM((8 * K_FFN, CW), f32),
                        pltpu.VMEM((8 * K_FFN, CW), f32), pltpu.VMEM((16, CW), f32)],
        sem=("arbitrary",), args=(up, up, conv_w, conv_w, conv_b, conv_b, d_act), rider=rider)


def _glu_conv_bwd(proj, conv_w, d_uconv, rider=None):
    t = proj.shape[0]
    pad = _pad_of(K_CONF)
    ca, cg = OFF_CA // CW, OFF_CG // CW

    def body(a_ref, g_ref, w_ref, du_ref, dc_ref, dw_ref, db_ref, v_scr, dw_scr, db_scr):
        dw_scr[...] = jnp.zeros_like(dw_scr)
        db_scr[...] = jnp.zeros_like(db_scr)

        def glu(r, carry):
            rows = pl.ds(pl.multiple_of(r * RC, RC), RC)
            v_scr[rows, :] = a_ref[rows, :] * jax.nn.sigmoid(g_ref[rows, :])
            return carry
        lax.fori_loop(0, t // RC, glu, 0)

        def step(r, carry):
            rows = pl.ds(pl.multiple_of(r * RC, RC), RC)
            du = du_ref[rows, :]
            _dw_accumulate(dw_scr, du, _causal_win(v_scr, r, t, pad), K_CONF, pad)
            db_scr[...] += _rows8(du)
            dv = _corr_taps(_anti_win(du_ref, r, t, pad), w_ref, K_CONF)
            a = a_ref[rows, :]
            s = jax.nn.sigmoid(g_ref[rows, :])
            dc_ref[0, rows, :] = (dv * s).astype(MX)
            dc_ref[1, rows, :] = (dv * a * s * (1.0 - s)).astype(MX)
            return carry
        lax.fori_loop(0, t // RC, step, 0)
        _dw_finish(dw_scr, dw_ref, K_CONF)
        db_ref[...] = jnp.sum(db_scr[...], axis=0, keepdims=True)

    return _call(
        body, name="glu_conv_bwd", grid=(D // CW,),
        out_shape=(jax.ShapeDtypeStruct((2, t, D), MX), jax.ShapeDtypeStruct((K_CONF, D), f32),
                   jax.ShapeDtypeStruct((1, D), f32)),
        in_specs=[pl.BlockSpec((t, CW), lambda j: (0, ca + j)), pl.BlockSpec((t, CW), lambda j: (0, cg + j)),
                  pl.BlockSpec((K_CONF, CW), lambda j: (0, j)), pl.BlockSpec((t, CW), lambda j: (0, j))],
        out_specs=(pl.BlockSpec((2, t, CW), lambda j: (0, 0, j)), pl.BlockSpec((K_CONF, CW), lambda j: (0, j)),
                   pl.BlockSpec((1, CW), lambda j: (0, j))),
        scratch_shapes=[pltpu.VMEM((t, CW), f32), pltpu.VMEM((8 * K_CONF, CW), f32), pltpu.VMEM((8, CW), f32)],
        sem=("arbitrary",), args=(proj, proj, conv_w, d_uconv), rider=rider)


def _ssd_conv_bwd_x(proj, conv_w, conv_b, d_xs, d_y, d_skip_row):
    t = proj.shape[0]
    pad = _pad_of(K_SSD)
    c0 = OFF_XBC // CW

    def body(x_ref, w_ref, b_ref, dxs_ref, dy_ref, dsk_ref, draw_ref, dw_ref, db_ref, dp_scr, dw_scr, db_scr):
        dw_scr[...] = jnp.zeros_like(dw_scr)
        db_scr[...] = jnp.zeros_like(db_scr)

        def first(r, carry):
            rows = pl.ds(pl.multiple_of(r * RC, RC), RC)
            win = _causal_win(x_ref, r, t, pad)
            pre = _conv_taps(win, w_ref, K_SSD, pad) + b_ref[...]
            dpre = (dxs_ref[rows, :] + dy_ref[rows, :] * dsk_ref[...]) * _dsilu(pre)
            dp_scr[rows, :] = dpre
            _dw_accumulate(dw_scr, dpre, win, K_SSD, pad)
            db_scr[...] += _rows8(dpre)
            return carry
        lax.fori_loop(0, t // RC, first, 0)

        def second(r, carry):
            rows = pl.ds(pl.multiple_of(r * RC, RC), RC)
            draw_ref[rows, :] = _corr_taps(_anti_win(dp_scr, r, t, pad), w_ref, K_SSD).astype(MX)
            return carry
        lax.fori_loop(0, t // RC, second, 0)
        _dw_finish(dw_scr, dw_ref, K_SSD)
        db_ref[...] = jnp.sum(db_scr[...], axis=0, keepdims=True)

    cb = pl.BlockSpec((t, CW), lambda j: (0, j))
    return pl.pallas_call(
        body, name="ssd_conv_bwd_x", grid=(D // CW,),
        out_shape=(jax.ShapeDtypeStruct((t, D), MX), jax.ShapeDtypeStruct((K_SSD, D), f32),
                   jax.ShapeDtypeStruct((1, D), f32)),
        in_specs=[pl.BlockSpec((t, CW), lambda j: (0, c0 + j)), pl.BlockSpec((K_SSD, CW), lambda j: (0, j)),
                  pl.BlockSpec((1, CW), lambda j: (0, j)), cb, cb, pl.BlockSpec((1, CW), lambda j: (0, j))],
        out_specs=(cb, pl.BlockSpec((K_SSD, CW), lambda j: (0, j)), pl.BlockSpec((1, CW), lambda j: (0, j))),
        scratch_shapes=[pltpu.VMEM((t, CW), f32), pltpu.VMEM((8 * K_SSD, CW), f32), pltpu.VMEM((8, CW), f32)],
        compiler_params=_cp("arbitrary"),
    )(proj, conv_w, conv_b, d_xs, d_y, d_skip_row)


def _ssd_conv_bwd_bc(proj, conv_w, conv_b, d_bc):
    t = proj.shape[0]
    pad = _pad_of(K_SSD)
    c0 = (OFF_XBC + D) // CW
    w0 = D // CW

    def body(x_ref, w_ref, b_ref, dbc_ref, draw_ref, dw_ref, db_ref, dp_scr, dw_scr, db_scr):
        dw_scr[...] = jnp.zeros_like(dw_scr)
        db_scr[...] = jnp.zeros_like(db_scr)

        def first(r, carry):
            rows = pl.ds(pl.multiple_of(r * RC, RC), RC)
            win = _causal_win(x_ref, r, t, pad)
            pre = _conv_taps(win, w_ref, K_SSD, pad) + b_ref[...]
            dpre = dbc_ref[0, rows, :] * _dsilu(pre)
            dp_scr[rows, :] = dpre
            _dw_accumulate(dw_scr, dpre, win, K_SSD, pad)
            db_scr[...] += _rows8(dpre)
            return carry
        lax.fori_loop(0, t // RC, first, 0)

        def second(r, carry):
            rows = pl.ds(pl.multiple_of(r * RC, RC), RC)
            draw_ref[rows, :] = _corr_taps(_anti_win(dp_scr, r, t, pad), w_ref, K_SSD).astype(MX)
            return carry
        lax.fori_loop(0, t // RC, second, 0)
        _dw_finish(dw_scr, dw_ref, K_SSD)
        db_ref[...] = jnp.sum(db_scr[...], axis=0, keepdims=True)

    return pl.pallas_call(
        body, name="ssd_conv_bwd_bc", grid=(2,),
        out_shape=(jax.ShapeDtypeStruct((t, 2 * CW), MX), jax.ShapeDtypeStruct((K_SSD, 2 * CW), f32),
                   jax.ShapeDtypeStruct((1, 2 * CW), f32)),
        in_specs=[pl.BlockSpec((t, CW), lambda j: (0, c0 + j)), pl.BlockSpec((K_SSD, CW), lambda j: (0, w0 + j)),
                  pl.BlockSpec((1, CW), lambda j: (0, w0 + j)), pl.BlockSpec((1, t, CW), lambda j: (j, 0, 0))],
        out_specs=(pl.BlockSpec((t, CW), lambda j: (0, j)), pl.BlockSpec((K_SSD, CW), lambda j: (0, j)),
                   pl.BlockSpec((1, CW), lambda j: (0, j))),
        scratch_shapes=[pltpu.VMEM((t, CW), f32), pltpu.VMEM((8 * K_SSD, CW), f32), pltpu.VMEM((8, CW), f32)],
        compiler_params=_cp("arbitrary"),
    )(proj, conv_w, conv_b, d_bc)


def _chunk_masks():
    ii = lax.broadcasted_iota(jnp.int32, (CHUNK, CHUNK), 0)
    jj = lax.broadcasted_iota(jnp.int32, (CHUNK, CHUNK), 1)
    return ii == jj, jj <= ii, jj >= ii


def _to_row(col, eye):
    return jnp.sum(jnp.where(eye, col, 0.0), axis=0, keepdims=True)


def _to_col(row, eye):
    return jnp.sum(jnp.where(eye, row, 0.0), axis=1, keepdims=True)


def _head_decay(dt_h, a_h, eye, tril):
    a_row = _to_row(dt_h * a_h, eye)
    cs = jnp.sum(jnp.where(tril, a_row, 0.0), axis=1, keepdims=True)
    cs_row = _to_row(cs, eye)
    decay = jnp.where(tril, jnp.exp(jnp.where(tril, cs - cs_row, 0.0)), 0.0)
    total = jnp.sum(a_row, axis=1, keepdims=True)
    return cs, decay, total


SCAN_UNROLL = 4


def _unrolled_loop(n, step, init):
    unroll = min(SCAN_UNROLL, n)
    assert n % unroll == 0

    def trip(i, carry):
        for u in range(unroll):
            carry = step(unroll * i + u, carry)
        return carry
    return lax.fori_loop(0, n // unroll, trip, init)


def _dt_pairs(proj, dt_bias_row):
    t = proj.shape[0]
    cdt = OFF_DT // LANES

    def body(dt_ref, dtb_ref, o_ref):
        dt = _softplus(dt_ref[...] + dtb_ref[...])
        for h in range(HEADS):
            o_ref[h] = jnp.broadcast_to(dt[:, h:h + 1], (TM, LANES))

    return pl.pallas_call(
        body, name="dt_pairs", grid=(t // TM,), out_shape=jax.ShapeDtypeStruct((HEADS, t, LANES), f32),
        in_specs=[pl.BlockSpec((TM, LANES), lambda i: (i, cdt)), _row(LANES)],
        out_specs=pl.BlockSpec((HEADS, TM, LANES), lambda i: (0, i, 0)), compiler_params=_cp("arbitrary"),
    )(proj, dt_bias_row)


def _ssd_fwd(xbc_act, dt_pairs, a_log_row, rider=None):
    t = xbc_act.shape[0]
    nc = t // CHUNK
    cb, cc = D // LANES, (D + 2 * STATE_N) // LANES

    def body(x_ref, b_ref, c_ref, dtp_ref, alog_ref, y_ref, st_ref):
        j = pl.program_id(0)
        eye, tril, _ = _chunk_masks()
        lane = lax.broadcasted_iota(jnp.int32, (1, LANES), 1)
        first = lane < HEAD_P
        a_row = -jnp.exp(alog_ref[...])
        a_heads = [jnp.sum(jnp.where(lane == 2 * j + h, a_row, 0.0), axis=1, keepdims=True) for h in range(2)]

        def chunk(c, hprev):
            rows = pl.ds(pl.multiple_of(c * CHUNK, CHUNK), CHUNK)
            xv, bm, cm = x_ref[rows, :], b_ref[rows, :], c_ref[rows, :]
            dts = [dtp_ref[h, rows, :] for h in range(2)]
            st_ref[c] = hprev
            g = _mm_nt(cm, bm)
            ch = _mm(cm, hprev)
            xdt = xv * jnp.where(first, dts[0], dts[1])
            ys, hs = [], []
            for h in range(2):
                cs, decay, total = _head_decay(dts[h], a_heads[h], eye, tril)
                y = _mm(g * decay, xdt) + jnp.exp(cs) * ch
                s = _mm_tn(bm * jnp.exp(total - cs), xdt)
                ys.append(y)
                hs.append(jnp.exp(total) * hprev + s)
            y_ref[rows, :] = jnp.where(first, ys[0], ys[1])
            return jnp.where(first, hs[0], hs[1])

        _unrolled_loop(nc, chunk, jnp.zeros((STATE_N, LANES), f32))

    blk = lambda f: pl.BlockSpec((t, LANES), f)
    return _call(
        body, name="ssd_fwd", grid=(D // LANES,),
        out_shape=(jax.ShapeDtypeStruct((t, D), f32), jax.ShapeDtypeStruct((nc, STATE_N, D), f32)),
        in_specs=[blk(lambda j: (0, j)), blk(lambda j: (0, cb + j // 4)), blk(lambda j: (0, cc + j // 4)),
                  pl.BlockSpec((2, t, LANES), lambda j: (j, 0, 0)), _row(LANES)],
        out_specs=(blk(lambda j: (0, j)), pl.BlockSpec((nc, STATE_N, LANES), lambda j: (0, 0, j))),
        sem=("arbitrary",), args=(xbc_act, xbc_act, xbc_act, dt_pairs, a_log_row), rider=rider)


def _dt_raw_grad(d_dt, proj, dt_bias_row):
    t = d_dt.shape[0]
    cdt = OFF_DT // LANES

    def body(d_ref, dt_ref, dtb_ref, o_ref):
        o_ref[...] = d_ref[...] * jax.nn.sigmoid(dt_ref[...] + dtb_ref[...])

    blk = pl.BlockSpec((TM, LANES), lambda i: (i, 0))
    return pl.pallas_call(
        body, name="dt_raw_grad", grid=(t // TM,), out_shape=jax.ShapeDtypeStruct((t, LANES), f32),
        in_specs=[blk, pl.BlockSpec((TM, LANES), lambda i: (i, cdt)), _row(LANES)], out_specs=blk,
        compiler_params=_cp("arbitrary"),
    )(d_dt, proj, dt_bias_row)


def _ssd_bwd(xbc_act, dt_pairs, a_log_row, states, d_y, rider=None):
    t = xbc_act.shape[0]
    nc = t // CHUNK
    cb, cc = D // LANES, (D + 2 * STATE_N) // LANES

    def body(x_ref, b_ref, c_ref, dtp_ref, alog_ref, st_ref, dy_ref, dx_ref, dbc_ref, ddt_ref, da_ref):
        grp, p = pl.program_id(0), pl.program_id(1)
        j = 4 * grp + p
        eye, tril, triu = _chunk_masks()
        lane = lax.broadcasted_iota(jnp.int32, (1, LANES), 1)
        first = lane < HEAD_P
        last_row = lax.broadcasted_iota(jnp.int32, (CHUNK, 1), 0) == CHUNK - 1
        a_row = -jnp.exp(alog_ref[...])
        a_heads = [jnp.sum(jnp.where(lane == 2 * j + h, a_row, 0.0), axis=1, keepdims=True) for h in range(2)]

        @pl.when(p == 0)
        def _():
            dbc_ref[...] = jnp.zeros_like(dbc_ref)

        @pl.when(j == 0)
        def _():
            ddt_ref[...] = jnp.zeros_like(ddt_ref)
            da_ref[...] = jnp.zeros_like(da_ref)

        def chunk(i, dh):
            c = nc - 1 - i
            rows = pl.ds(pl.multiple_of(c * CHUNK, CHUNK), CHUNK)
            xv, bm, cm = x_ref[rows, :], b_ref[rows, :], c_ref[rows, :]
            dts = [dtp_ref[h, rows, :] for h in range(2)]
            hprev = st_ref[c]
            dy = dy_ref[rows, :]
            g = _mm_nt(cm, bm)
            xdt = xv * jnp.where(first, dts[0], dts[1])
            dxs, dhs = [], []
            db_sum, dc_sum = None, None
            ddt_mat = jnp.zeros((CHUNK, LANES), f32)
            da_acc = jnp.zeros((1, LANES), f32)
            for h in range(2):
                mine = first if h == 0 else jnp.logical_not(first)
                cs, decay, total = _head_decay(dts[h], a_heads[h], eye, tril)
                e_cs, e_tot = jnp.exp(cs), jnp.exp(total)
                dec_s = jnp.exp(total - cs)
                dyh = jnp.where(mine, dy, 0.0)
                xdth = jnp.where(mine, xdt, 0.0)
                dhh = jnp.where(mine, dh, 0.0)
                hph = jnp.where(mine, hprev, 0.0)
                m = g * decay
                dm = _mm_nt(dyh, xdth)
                dg = dm * decay
                w = dm * m
                bdec = bm * dec_s
                dxdt = _mm_tn(m, dyh) + _mm(bdec, dhh)
                dc_off = _mm_nt(dyh, hph) * e_cs
                db_s = _mm_nt(xdth, dhh) * dec_s
                dc_h = _mm(dg, bm) + dc_off
                db_h = _mm_tn(dg, cm) + db_s
                r_s = jnp.sum(db_s * bm, axis=1, keepdims=True)
                dtotal = jnp.sum(r_s, axis=0, keepdims=True) + e_tot * jnp.sum(
                    jnp.sum(dhh * hph, axis=1, keepdims=True), axis=0, keepdims=True)
                dcs = (jnp.sum(w, axis=1, keepdims=True) - _to_col(jnp.sum(w, axis=0, keepdims=True), eye)
                       + jnp.sum(dc_off * cm, axis=1, keepdims=True) - r_s + jnp.where(last_row, dtotal, 0.0))
                da_col = jnp.sum(jnp.where(triu, _to_row(dcs, eye), 0.0), axis=1, keepdims=True)
                ddt = da_col * a_heads[h] + jnp.sum(jnp.where(mine, dxdt * xv, 0.0), axis=1, keepdims=True)
                ddt_mat = ddt_mat + jnp.where(lane == 2 * j + h, ddt, 0.0)
                da_acc = da_acc + jnp.where(lane == 2 * j + h, jnp.sum(da_col * dts[h], axis=0, keepdims=True), 0.0)
                dxs.append(dxdt * dts[h])
                dhs.append(e_tot * dhh + _mm_tn(cm * e_cs, dyh))
                db_sum = db_h if db_sum is None else db_sum + db_h
                dc_sum = dc_h if dc_sum is None else dc_sum + dc_h
            dx_ref[rows, :] = jnp.where(first, dxs[0], dxs[1])
            dbc_ref[0, rows, :] += db_sum
            dbc_ref[1, rows, :] += dc_sum
            ddt_ref[rows, :] += ddt_mat
            da_ref[...] += da_acc * a_row
            return jnp.where(first, dhs[0], dhs[1])

        _unrolled_loop(nc, chunk, jnp.zeros((STATE_N, LANES), f32))

    blk = lambda f: pl.BlockSpec((t, LANES), f)
    return _call(
        body, name="ssd_bwd", grid=(2, 4),
        out_shape=(jax.ShapeDtypeStruct((t, D), f32), jax.ShapeDtypeStruct((2, t, 2 * STATE_N), f32),
                   jax.ShapeDtypeStruct((t, LANES), f32), jax.ShapeDtypeStruct((1, LANES), f32)),
        in_specs=[blk(lambda g, p: (0, 4 * g + p)), blk(lambda g, p: (0, cb + g)), blk(lambda g, p: (0, cc + g)),
                  pl.BlockSpec((2, t, LANES), lambda g, p: (4 * g + p, 0, 0)), _row(LANES),
                  pl.BlockSpec((nc, STATE_N, LANES), lambda g, p: (0, 0, 4 * g + p)), blk(lambda g, p: (0, 4 * g + p))],
        out_specs=(blk(lambda g, p: (0, 4 * g + p)), pl.BlockSpec((2, t, LANES), lambda g, p: (0, 0, g)),
                   blk(lambda g, p: (0, 0)), _row(LANES)),
        sem=("arbitrary", "arbitrary"), args=(xbc_act, xbc_act, xbc_act, dt_pairs, a_log_row, states, d_y), rider=rider)


def _up_bwd(d_up, w_up, x1, mod, norm2_w, dx2, mix, w_out, rider=None):
    t = x1.shape[0]

    def body(dup_ref, wu_ref, x1_ref, mod_ref, nw_ref, dx2_ref, mix_ref, wo_ref,
             dx1_ref, dmix_ref, dys_ref, du_ref, st_ref):
        @pl.when(pl.program_id(0) == 0)
        def _():
            st_ref[...] = jnp.zeros_like(st_ref)

        nt = (((1,), (1,)), ((), ()))
        dh = None
        for k in range(4):
            lo = (k % 2) * UP_SHARD
            part = lax.dot_general(dup_ref[k // 2, :, lo:lo + UP_SHARD], wu_ref[k], nt, preferred_element_type=f32)
            dh = part if dh is None else dh + part
        x1 = x1_ref[...]
        rstd = lax.rsqrt(jnp.mean(x1 * x1, axis=-1, keepdims=True) + 1e-6)
        xh = x1 * rstd
        nw = nw_ref[...]
        sc = 1.0 + mod_ref[:, 4 * D:5 * D]
        st_ref[0:1, :] += jnp.sum(dh, axis=0, keepdims=True)
        st_ref[1:2, :] += jnp.sum(dh * xh * nw, axis=0, keepdims=True)
        st_ref[2:3, :] += jnp.sum(dh * sc * xh, axis=0, keepdims=True)
        dxh = dh * sc * nw
        dx1 = dx2_ref[...] + rstd * (dxh - xh * jnp.mean(dxh * xh, axis=-1, keepdims=True))
        dx1_ref[...] = dx1
        st_ref[3:4, :] += jnp.sum(dx1 * mix_ref[...], axis=0, keepdims=True)
        dmix = (mod_ref[:, 2 * D:3 * D] * dx1).astype(MX)
        dmix_ref[...] = dmix
        dys_ref[...] = lax.dot_general(dmix, wo_ref[0:D, :], nt, preferred_element_type=f32)
        du_ref[...] = lax.dot_general(dmix, wo_ref[D:2 * D, :], nt, preferred_element_type=f32)

    blk = pl.BlockSpec((TM, D), lambda i: (i, 0))
    return _call(
        body, name="up_bwd", grid=(t // TM,),
        out_shape=(jax.ShapeDtypeStruct((t, D), f32), jax.ShapeDtypeStruct((t, D), MX),
                   jax.ShapeDtypeStruct((t, D), f32), jax.ShapeDtypeStruct((t, D), f32),
                   jax.ShapeDtypeStruct((8, D), f32)),
        in_specs=[pl.BlockSpec((2, TM, D_FF), lambda i: (0, i, 0)), _resident((4, D, UP_SHARD)), blk, _row(6 * D), _row(),
                  blk, blk, _resident((2 * D, D))],
        out_specs=(blk, blk, blk, blk, pl.BlockSpec((8, D), lambda i: (0, 0))),
        sem=("arbitrary",), args=(d_up, w_up, x1, mod, norm2_w, dx2, mix, w_out), rider=rider)


def _ln_silu_bwd(d_u, u_conv, ln_w, ln_b):
    t = d_u.shape[0]

    def body(du_ref, u_ref, w_ref, b_ref, o_ref, st_ref):
        @pl.when(pl.program_id(0) == 0)
        def _():
            st_ref[...] = jnp.zeros_like(st_ref)

        u = u_ref[...]
        mu = jnp.mean(u, axis=-1, keepdims=True)
        uc = u - mu
        rstd = lax.rsqrt(jnp.mean(uc * uc, axis=-1, keepdims=True) + 1e-5)
        n = uc * rstd
        w = w_ref[...]
        dl = du_ref[...] * _dsilu(n * w + b_ref[...])
        st_ref[0:1, :] += jnp.sum(dl * n, axis=0, keepdims=True)
        st_ref[1:2, :] += jnp.sum(dl, axis=0, keepdims=True)
        dn = dl * w
        o_ref[...] = rstd * (dn - jnp.mean(dn, axis=-1, keepdims=True) - n * jnp.mean(dn * n, axis=-1, keepdims=True))

    blk = pl.BlockSpec((TM, D), lambda i: (i, 0))
    return pl.pallas_call(
        body, name="ln_silu_bwd", grid=(t // TM,),
        out_shape=(jax.ShapeDtypeStruct((t, D), f32), jax.ShapeDtypeStruct((8, D), f32)),
        in_specs=[blk, blk, _row(), _row()], out_specs=(blk, pl.BlockSpec((8, D), lambda i: (0, 0))),
        compiler_params=_cp("arbitrary"),
    )(d_u, u_conv, ln_w, ln_b)


def _ssd_gate_norm_bwd(d_out, y_scan, xbc_act, proj, d_skip_row, ssd_norm_w):
    t = d_out.shape[0]

    def body(do_ref, y_ref, xs_ref, z_ref, dsk_ref, nw_ref, dy_ref, dz_ref, st_ref):
        @pl.when(pl.program_id(0) == 0)
        def _():
            st_ref[...] = jnp.zeros_like(st_ref)

        xs = xs_ref[...]
        y = y_ref[...] + xs * dsk_ref[...]
        z = z_ref[...]
        s = _silu(z)
        yz = y * s
        rstd = lax.rsqrt(jnp.mean(yz * yz, axis=-1, keepdims=True) + 1e-6)
        n = yz * rstd
        do = do_ref[...]
        st_ref[0:1, :] += jnp.sum(do * n, axis=0, keepdims=True)
        dn = do * nw_ref[...]
        dyz = rstd * (dn - n * jnp.mean(dn * n, axis=-1, keepdims=True))
        dy = dyz * s
        dy_ref[...] = dy
        dz_ref[...] = (dyz * y * _dsilu(z)).astype(MX)
        st_ref[1:2, :] += jnp.sum(dy * xs, axis=0, keepdims=True)

    blk = pl.BlockSpec((TM, D), lambda i: (i, 0))
    return pl.pallas_call(
        body, name="ssd_gate_norm_bwd", grid=(t // TM,),
        out_shape=(jax.ShapeDtypeStruct((t, D), f32), jax.ShapeDtypeStruct((t, D), MX), jax.ShapeDtypeStruct((8, D), f32)),
        in_specs=[blk, blk, blk, blk, _row(), _row()], out_specs=(blk, blk, pl.BlockSpec((8, D), lambda i: (0, 0))),
        compiler_params=_cp("arbitrary"),
    )(d_out, y_scan, xbc_act, proj, d_skip_row, ssd_norm_w)


def _inproj_bwd(d_z, d_xraw, d_bcraw, d_conf, d_dt, w_pack, x, mod, norm1_w, dx1, after=None):
    t = x.shape[0]
    extra = [] if after is None else [after]

    def body(dz_ref, dx_ref, dbc_ref, dcf_ref, ddt_ref, w_ref, x_ref, mod_ref, nw_ref, dx1_ref, *rest):
        gx_ref, st_ref = rest[-2:]
        @pl.when(pl.program_id(0) == 0)
        def _():
            st_ref[...] = jnp.zeros_like(st_ref)

        nt = (((1,), (1,)), ((), ()))
        dot = lambda a, lo, hi: lax.dot_general(a, w_ref[:, lo:hi], nt, preferred_element_type=f32)
        dh = dot(dz_ref[...], OFF_Z, OFF_Z + D)
        dh = dh + dot(dx_ref[...], OFF_XBC, OFF_XBC + D)
        dh = dh + dot(dbc_ref[...], OFF_XBC + D, OFF_XBC + D_XBC)
        dh = dh + dot(dcf_ref[0], OFF_CA, OFF_CA + D)
        dh = dh + dot(dcf_ref[1], OFF_CG, OFF_CG + D)
        dh = dh + dot(ddt_ref[...].astype(MX), OFF_DT, OFF_DT + LANES)
        st_ref[3:4, 0:LANES] += jnp.sum(ddt_ref[...], axis=0, keepdims=True)
        xv = x_ref[...]
        rstd = lax.rsqrt(jnp.mean(xv * xv, axis=-1, keepdims=True) + 1e-6)
        xh = xv * rstd
        nw = nw_ref[...]
        sc = 1.0 + mod_ref[:, D:2 * D]
        st_ref[0:1, :] += jnp.sum(dh, axis=0, keepdims=True)
        st_ref[1:2, :] += jnp.sum(dh * xh * nw, axis=0, keepdims=True)
        st_ref[2:3, :] += jnp.sum(dh * sc * xh, axis=0, keepdims=True)
        dxh = dh * sc * nw
        gx_ref[...] = dx1_ref[...] + rstd * (dxh - xh * jnp.mean(dxh * xh, axis=-1, keepdims=True))

    blk = pl.BlockSpec((TM, D), lambda i: (i, 0))
    return _call(
        body, name="inproj_bwd", grid=(t // TM,),
        out_shape=(jax.ShapeDtypeStruct((t, D), f32), jax.ShapeDtypeStruct((8, D), f32)),
        in_specs=[blk, blk, pl.BlockSpec((TM, 2 * CW), lambda i: (i, 0)), pl.BlockSpec((2, TM, D), lambda i: (0, i, 0)),
                  pl.BlockSpec((TM, LANES), lambda i: (i, 0)), _resident((D, W_PACK)), blk, _row(6 * D), _row(), blk]
        + [ANY] * len(extra),
        out_specs=(blk, pl.BlockSpec((8, D), lambda i: (0, 0))),
        sem=("arbitrary",), args=(d_z, d_xraw, d_bcraw, d_conf, d_dt, w_pack, x, mod, norm1_w, dx1, *extra))[0]


def _wgrad(at, d, name, bn=256):
    k, t = at.shape
    n = d.shape[1]
    out_dtype = MX

    def body(a_ref, d_ref, o_ref):
        o_ref[...] = jnp.dot(a_ref[...], d_ref[...].astype(MX), preferred_element_type=f32).astype(out_dtype)

    return pl.pallas_call(
        body, name=name, grid=(n // bn,), out_shape=jax.ShapeDtypeStruct((k, n), out_dtype),
        in_specs=[_resident((k, t)), pl.BlockSpec((t, bn), lambda j: (0, j))],
        out_specs=pl.BlockSpec((k, bn), lambda j: (0, j)), compiler_params=_cp("arbitrary"),
    )(at, d)


def _wgrad_stacked(at, d, name, bn):
    out_dtype = MX
    k, t = at.shape
    s, _, n = d.shape
    nb = n // bn

    def body(a_ref, d_ref, o_ref):
        o_ref[0] = jnp.dot(a_ref[...], d_ref[0], preferred_element_type=f32).astype(out_dtype)

    return pl.pallas_call(
        body, name=name, grid=(s, nb), out_shape=jax.ShapeDtypeStruct((s * nb, k, bn), out_dtype),
        in_specs=[_resident((k, t)), pl.BlockSpec((1, t, bn), lambda i, j: (i, 0, j))],
        out_specs=pl.BlockSpec((1, k, bn), lambda i, j: (i * nb + j, 0, 0)), compiler_params=_cp("arbitrary", "arbitrary"),
    )(at, d)


def _pad_row(v, width=LANES):
    return jnp.pad(v.reshape(1, -1), ((0, 0), (0, width - v.size)))


def _quarters(a):
    return a.reshape(4, 2, a.shape[0] // 8, a.shape[1])


def _local_step(x, mod, target, w_pack, late, small, reducer=None):
    dtb_row, alog_row = _pad_row(small["dt_bias"]), _pad_row(small["a_log"])
    dskip_row = jnp.repeat(small["d_skip"].reshape(-1), HEAD_P).reshape(1, D)

    red = reducer

    def hosted(host, args, swap=None, scatter=None, gather=None, sums=()):
        if red is None:
            return host(*args)[0]
        riders = ([red.scatter(scatter)] if scatter else []) + ([red.swap(*swap)] if swap else [])
        riders += [_SwapSumsRider([red.sums[n] for n in sums])] if sums else []
        riders += [_GatherRider([gather[0]], *gather[1:])] if gather is not None else []
        both = _Riders(riders)
        outs, extra = host(*args, rider=both)
        extra = both.split(extra)
        if scatter:
            red.scattered(scatter, extra.pop(0))
        if swap:
            red.swapped(swap[0], extra.pop(0))
        if sums:
            red.others.update(zip(sums, extra.pop(0)))
        return (outs, extra[0][0]) if gather is not None else outs

    w_out, w_up, w_down = late
    if red is None:
        proj, h_t = hosted(_ln_inproj, (x, mod, small["norm1_w"], w_pack))
        dt_pairs = _dt_pairs(proj, dtb_row)
        xbc_act, = hosted(_ssd_conv_fwd, (proj, small["ssd_conv_w"], small["ssd_conv_b"]))
        y_scan, states = hosted(_ssd_fwd, (xbc_act, dt_pairs, alog_row))
        u_conv, = hosted(_glu_conv_fwd, (proj, small["conf_conv_w"], small["conf_conv_b"]))
    else:
        (proj, h_t), w_out = hosted(_ln_inproj, (x, mod, small["norm1_w"], w_pack), gather=(w_out,))
        dt_pairs = _dt_pairs(proj, dtb_row)
        (xbc_act,), w_up = hosted(_ssd_conv_fwd, (proj, small["ssd_conv_w"], small["ssd_conv_b"]), gather=(w_up, 0, UP_EARLY_ROWS))
        (y_scan, states), w_up = hosted(_ssd_fwd, (xbc_act, dt_pairs, alog_row), gather=(w_up, UP_EARLY_ROWS, None))
        (u_conv,), w_down = hosted(_glu_conv_fwd, (proj, small["conf_conv_w"], small["conf_conv_b"]), gather=(w_down,))
        w_out, w_up, w_down = w_out.reshape(2 * D, D), w_up.reshape(4, D, UP_SHARD), w_down.reshape(D_FF, D)
    y_ssd, y_ssd_t = _ssd_gate_norm(y_scan, xbc_act, proj, dskip_row, small["ssd_norm_w"])
    u, u_t = _ln_silu(u_conv, small["conf_ln_w"], small["conf_ln_b"])
    mix, x1, h2_t, up = _outproj_ln2_up(y_ssd, u, w_out, x, mod, small["norm2_w"], w_up)
    act, act_t = _ffn_conv_fwd(up, small["ffn_conv_w"], small["ffn_conv_b"])[0]
    dx2, d_ffn, d_act, st_down = _down_loss(act, w_down, x1, mod, small["final_norm_w"], target)

    g_down = _quarters(_wgrad(act_t, d_ffn, "wgrad_down"))
    d_up, dw_ffn, db_ffn = hosted(_ffn_conv_bwd, (up, small["ffn_conv_w"], small["ffn_conv_b"], d_act), swap=("w_down", g_down))
    g_up = _wgrad_stacked(h2_t, d_up, "wgrad_up", D_FF // 2).reshape(4, 2, D // 2, UP_SHARD)
    dx1, d_mix, d_yssd, d_u, st_up = hosted(_up_bwd, (d_up, w_up, x1, mod, small["norm2_w"], dx2, mix, w_out),
                                            scatter="w_down", swap=("w_up", g_up))
    g_out = _quarters(jnp.concatenate([_wgrad(y_ssd_t, d_mix, "wgrad_out_y"), _wgrad(u_t, d_mix, "wgrad_out_u")], axis=0))
    d_uconv, st_ln = _ln_silu_bwd(d_u, u_conv, small["conf_ln_w"], small["conf_ln_b"])
    d_conf, dw_conf, db_conf = hosted(_glu_conv_bwd, (proj, small["conf_conv_w"], d_uconv), scatter="w_up",
                                      swap=("w_out", g_out))
    d_y, d_z, st_gn = _ssd_gate_norm_bwd(d_yssd, y_scan, xbc_act, proj, dskip_row, small["ssd_norm_w"])
    d_xs, d_bc, d_dt, d_alog = hosted(_ssd_bwd, (xbc_act, dt_pairs, alog_row, states, d_y), scatter="w_out")
    d_dt = _dt_raw_grad(d_dt, proj, dtb_row)
    d_xraw, dw_sx, db_sx = _ssd_conv_bwd_x(proj, small["ssd_conv_w"], small["ssd_conv_b"], d_xs, d_y, dskip_row)
    d_bcraw, dw_sbc, db_sbc = _ssd_conv_bwd_bc(proj, small["ssd_conv_w"], small["ssd_conv_b"], d_bc)
    g_in = _unpack_g_in(dict(
        z=_wgrad(h_t, d_z, "wgrad_in_z"), x=_wgrad(h_t, d_xraw, "wgrad_in_x"), bc=_wgrad(h_t, d_bcraw, "wgrad_in_bc"),
        conf=_wgrad_stacked(h_t, d_conf, "wgrad_in_conf", D), dt=_wgrad(h_t, d_dt, "wgrad_in_dt", bn=LANES)))
    g_in = g_in.reshape(4, 2, D // 2, W_IN_SHARD_PAD)
    args = (d_z, d_xraw, d_bcraw, d_conf, d_dt, w_pack, x, mod, small["norm1_w"], dx1)
    if red is None:
        grad_x, st_in = _inproj_bwd(*args)
    else:
        done = ("w_out", "w_up", "w_down")
        both = _Riders([red.swap("w_in", g_in), _SwapSumsRider([red.sums[n] for n in done])])
        handles, token = _split_start(both, "swap_start_w_in")
        grad_x, st_in = _inproj_bwd(*args, after=token)
        thru, outs = _split_wait(both, "swap_wait_w_in", handles, st_in)
        red.grads["w_in"] = thru[0]
        red.sums.update(zip(done, thru[1:]))
        got, others = both.split(outs)
        red.swapped("w_in", got)
        red.others.update(zip(done, others))

    gsmall = _pack_small_grads(st_in, st_up, st_down, st_ln, st_gn, d_alog, dw_sx, dw_sbc, db_sx, db_sbc, dw_conf, db_conf,
                               dw_ffn, db_ffn)
    gbig = None if reducer is not None else dict(w_in=g_in, w_out=g_out, w_up=g_up, w_down=g_down)
    return st_down[2, 0], grad_x, gbig, gsmall


VECTORS = ("ada_b", "norm1_w", "ssd_conv_b", "dt_bias", "a_log", "d_skip", "ssd_norm_w", "conf_conv_b", "conf_ln_w",
           "conf_ln_b", "norm2_w", "ffn_conv_b", "final_norm_w")
VECTOR_SIZES = (6 * D, D, D_XBC, HEADS, HEADS, HEADS, D, D, D, D, D, 2 * D_FF, D)
CONVS = {"ssd_conv_w": (K_SSD, D_XBC), "conf_conv_w": (K_CONF, D), "ffn_conv_w": (K_FFN, 2 * D_FF)}


def _pack_rows(items):
    n = -(-sum(w for _, w in items) // (8 * LANES)) * LANES
    while True:
        fill, place = [0] * 8, {}
        for key, w in sorted(items, key=lambda kv: -kv[1]):
            rows = [r for r in range(8) if fill[r] + w <= n]
            if not rows:
                break
            place[key] = (rows[0], fill[rows[0]])
            fill[rows[0]] += w
        if len(place) == len(items):
            return n, place
        n += LANES


FRONT_N, FRONT = _pack_rows([("c", D)] + [((nm, j), cols // 4) for nm, (taps, cols) in CONVS.items() for j in range(taps)])
BACK_N, BACK = _pack_rows([(nm, -(-sz // LANES) * LANES) for nm, sz in zip(VECTORS, VECTOR_SIZES)]
                          + [((nm, j), cols) for nm, (taps, cols) in CONVS.items() for j in range(taps)] + [("loss", LANES)])
_VM = pltpu.CompilerParams(vmem_limit_bytes=VMEM_LIMIT)


def _pack_front(c, shards):
    def body(c_ref, *refs):
        o_ref = refs[-1]
        o_ref[...] = jnp.zeros_like(o_ref)
        r, o = FRONT["c"]
        o_ref[r:r + 1, o:o + D] = c_ref[...]
        for ref, (nm, (taps, cols)) in zip(refs, CONVS.items()):
            for j in range(taps):
                r, o = FRONT[(nm, j)]
                o_ref[r:r + 1, o:o + cols // 4] = ref[0, j:j + 1, :]

    return pl.pallas_call(body, name="pack_front", out_shape=jax.ShapeDtypeStruct((8, FRONT_N), f32),
                          compiler_params=_VM)(c, *shards)


def _unpack_front(got):
    def body(g_ref, c_ref, *outs):
        r, o = FRONT["c"]
        for d in range(8):
            c_ref[d:d + 1, :] = g_ref[8 * d + r:8 * d + r + 1, o:o + D]
        for ref, (nm, (taps, cols)) in zip(outs, CONVS.items()):
            cw = cols // 4
            for j in range(taps):
                r, o = FRONT[(nm, j)]
                for k in range(4):
                    ref[j:j + 1, k * cw:(k + 1) * cw] = g_ref[16 * k + r:16 * k + r + 1, o:o + cw]

    return pl.pallas_call(
        body, name="unpack_front", compiler_params=_VM,
        out_shape=(jax.ShapeDtypeStruct((8, D), f32),) + tuple(jax.ShapeDtypeStruct(tc, f32) for tc in CONVS.values()),
    )(got)


def _pack_small_grads(st_in, st_up, st_down, st_ln, st_gn, d_alog, dw_sx, dw_sbc, db_sx, db_sbc, dw_conf, db_conf, dw_ffn,
                      db_ffn):
    def body(in_ref, up_ref, dn_ref, ln_ref, gn_ref, al_ref, wx_ref, wbc_ref, bx_ref, bbc_ref, wc_ref, bc_ref, wf_ref, bf_ref,
             o_ref):
        def put(key, val, shift=0):
            r, o = BACK[key]
            o_ref[r:r + 1, o + shift:o + shift + val.shape[1]] = val

        o_ref[...] = jnp.zeros_like(o_ref)
        for i, piece in enumerate((in_ref[0:1, :], in_ref[1:2, :], up_ref[3:4, :], up_ref[0:1, :], up_ref[1:2, :],
                                   dn_ref[1:2, :])):
            put("ada_b", piece, i * D)
        put("norm1_w", in_ref[2:3, :])
        put("ssd_conv_b", bx_ref[...])
        put("ssd_conv_b", bbc_ref[...], D)
        put("dt_bias", in_ref[3:4, 0:LANES])
        put("a_log", al_ref[...])
        lane = lax.broadcasted_iota(jnp.int32, (1, LANES), 1)
        col = lax.broadcasted_iota(jnp.int32, (1, D), 1)
        per_col = gn_ref[1:2, :]
        d_skip = jnp.zeros((1, LANES), f32)
        for h in range(HEADS):
            in_head = jnp.logical_and(col >= h * HEAD_P, col < (h + 1) * HEAD_P)
            s = jnp.sum(jnp.where(in_head, per_col, 0.0), axis=1, keepdims=True)
            d_skip = d_skip + jnp.where(lane == h, s, 0.0)
        put("d_skip", d_skip)
        put("ssd_norm_w", gn_ref[0:1, :])
        put("conf_conv_b", bc_ref[...])
        put("conf_ln_w", ln_ref[0:1, :])
        put("conf_ln_b", ln_ref[1:2, :])
        put("norm2_w", up_ref[2:3, :])
        put("ffn_conv_b", bf_ref[0])
        put("ffn_conv_b", bf_ref[1], D_FF)
        put("final_norm_w", dn_ref[0:1, :])
        put("loss", dn_ref[2:3, 0:LANES])
        for j in range(K_SSD):
            put(("ssd_conv_w", j), wx_ref[j:j + 1, :])
            put(("ssd_conv_w", j), wbc_ref[j:j + 1, :], D)
        for j in range(K_CONF):
            put(("conf_conv_w", j), wc_ref[j:j + 1, :])
        for j in range(K_FFN):
            put(("ffn_conv_w", j), wf_ref[0, j:j + 1, :])
            put(("ffn_conv_w", j), wf_ref[1, j:j + 1, :], D_FF)

    return pl.pallas_call(body, name="pack_small_grads", out_shape=jax.ShapeDtypeStruct((8, BACK_N), f32), compiler_params=_VM)(
        st_in, st_up, st_down, st_ln, st_gn, d_alog, dw_sx, dw_sbc, db_sx, db_sbc, dw_conf, db_conf, dw_ffn, db_ffn)


def _small_adamw(got, chip, w, m, v):
    names = VECTORS + tuple(CONVS)
    n_par = len(names)

    def body(chip_ref, g_ref, *refs):
        ins, outs = refs[:3 * n_par], refs[3 * n_par:]
        dm_ref, loss_ref, outs = outs[0], outs[1], outs[2:]
        chip_id = chip_ref[0]

        def summed(key, width):
            r, o = BACK[key]
            s = g_ref[r:r + 1, o:o + width]
            for d in range(1, 8):
                s = s + g_ref[8 * d + r:8 * d + r + 1, o:o + width]
            return s

        def mine(full, cw):
            out = full[:, 0:cw]
            for k in range(1, 4):
                out = jnp.where(chip_id == k, full[:, k * cw:(k + 1) * cw], out)
            return out

        r, o = BACK["ada_b"]
        for d in range(8):
            dm_ref[d:d + 1, :] = mine(g_ref[8 * d + r:8 * d + r + 1, o:o + 6 * D], 6 * D // 4)
        loss_ref[...] = summed("loss", LANES)
        for i, (nm, size) in enumerate(zip(VECTORS, VECTOR_SIZES)):
            g = summed(nm, -(-size // LANES) * LANES)[:, 0:size]
            res = _adam_math(ins[3 * i][...], g, ins[3 * i + 1][...], ins[3 * i + 2][...])
            for ref, val in zip(outs[4 * i:4 * i + 4], (g,) + res):
                ref[...] = val
        for i, (nm, (taps, cols)) in enumerate(CONVS.items(), start=len(VECTORS)):
            for j in range(taps):
                g = mine(summed((nm, j), cols), cols // 4)
                res = _adam_math(ins[3 * i][0, j:j + 1, :], g, ins[3 * i + 1][0, j:j + 1, :], ins[3 * i + 2][0, j:j + 1, :])
                for ref, val in zip(outs[4 * i:4 * i + 4], (g,) + res):
                    ref[0, j:j + 1, :] = val

    params = [a[nm] for nm in names for a in (w, m, v)]
    whole = lambda s: pl.BlockSpec(s, lambda i, chip, nd=len(s): (0,) * nd)
    out_shape = [jax.ShapeDtypeStruct((8, 6 * D // 4), f32), jax.ShapeDtypeStruct((1, LANES), f32)]
    out_shape += [jax.ShapeDtypeStruct(w[nm].shape, f32) for nm in names for _ in range(4)]
    outs = pl.pallas_call(
        body, name="small_adamw", out_shape=tuple(out_shape), compiler_params=_VM,
        grid_spec=pltpu.PrefetchScalarGridSpec(
            num_scalar_prefetch=1, grid=(1,), in_specs=[whole(got.shape)] + [whole(p.shape) for p in params],
            out_specs=tuple(whole(s.shape) for s in out_shape)),
    )(_scalar(chip), got, *params)
    return outs[0], outs[1][0, 0], {nm: outs[2 + 4 * i:6 + 4 * i] for i, nm in enumerate(names)}


W_IN_COLS = 4624
W_IN_SHARD = W_IN_COLS // 4
W_IN_SHARD_PAD = 1280
_SEGMENTS = ((0, 1024, OFF_Z), (1024, 2560, OFF_XBC), (2560, 2576, OFF_DT), (2576, 3600, OFF_CA), (3600, 4624, OFF_CG))


def _in_pieces(bounds=()):
    out = []
    for k in range(4):
        s0, s1 = k * W_IN_SHARD, (k + 1) * W_IN_SHARD
        for lo, hi, off in _SEGMENTS:
            a, b = max(lo, s0), min(hi, s1)
            while a < b:
                p = off + a - lo
                e = min([b - a] + [c - p for c in bounds if c > p])
                out.append((k, a - s0, p, e))
                a += e
    return out


def _pack_w_in(shards):
    pieces = _in_pieces()

    def body(s_ref, o_ref):
        o_ref[:, OFF_DT:W_PACK] = jnp.zeros((TM, W_PACK - OFF_DT), MX)
        for k, c, p, n in pieces:
            o_ref[:, p:p + n] = s_ref[k, :, c:c + n]

    return pl.pallas_call(
        body, name="pack_w_in", grid=(D // TM,), out_shape=jax.ShapeDtypeStruct((D, W_PACK), MX),
        in_specs=[pl.BlockSpec((4, TM, W_IN_SHARD_PAD), lambda i: (0, i, 0))],
        out_specs=pl.BlockSpec((TM, W_PACK), lambda i: (i, 0)), compiler_params=_cp("arbitrary"),
    )(shards)


def _unpack_g_in(g):
    srcs = ((OFF_Z, D), (OFF_XBC, D), (OFF_XBC + D, 2 * CW), (OFF_CA, D), (OFF_CG, D), (OFF_DT, LANES))
    pieces = _in_pieces(tuple(o for o, _ in srcs) + tuple(o + n for o, n in srcs))

    def body(z_ref, x_ref, bc_ref, cf_ref, dt_ref, o_ref):
        read = (lambda lo, hi: z_ref[:, lo:hi], lambda lo, hi: x_ref[:, lo:hi], lambda lo, hi: bc_ref[:, lo:hi],
                lambda lo, hi: cf_ref[0, :, lo:hi], lambda lo, hi: cf_ref[1, :, lo:hi], lambda lo, hi: dt_ref[:, lo:hi])
        o_ref[:, :, W_IN_SHARD - 4:W_IN_SHARD_PAD] = jnp.zeros((4, TM, W_IN_SHARD_PAD - W_IN_SHARD + 4), MX)
        for k, c, p, n in pieces:
            i = [q for q, (o, w) in enumerate(srcs) if o <= p < o + w][0]
            o_ref[k, :, c:c + n] = read[i](p - srcs[i][0], p - srcs[i][0] + n)

    blk = lambda w: pl.BlockSpec((TM, w), lambda i: (i, 0))
    return pl.pallas_call(
        body, name="unpack_g_in", grid=(D // TM,), out_shape=jax.ShapeDtypeStruct((4, D, W_IN_SHARD_PAD), MX),
        in_specs=[blk(D), blk(D), blk(2 * CW), pl.BlockSpec((2, TM, D), lambda i: (0, i, 0)), blk(LANES)],
        out_specs=pl.BlockSpec((4, TM, W_IN_SHARD_PAD), lambda i: (0, i, 0)), compiler_params=_cp("arbitrary"),
    )(g["z"], g["x"], g["bc"], g["conf"], g["dt"])


def _scalar(v):
    return jnp.reshape(v, (1,)).astype(jnp.int32)


def _cast_into_slot(w, width, chip):
    r, c = w.shape
    h = r // 2
    tm = _row_tile(h)
    nj = h // tm

    def body(chip_ref, w_ref, o_ref):
        v = w_ref[...].astype(MX)
        o_ref[0, 0] = v if width == c else jnp.concatenate([v, jnp.zeros((tm, width - c), MX)], axis=1)

    return pl.pallas_call(
        body, name=f"cast_into_slot_{r}x{c}", out_shape=jax.ShapeDtypeStruct((4, 2, h, width), MX),
        grid_spec=pltpu.PrefetchScalarGridSpec(
            num_scalar_prefetch=1, grid=(2, nj),
            in_specs=[pl.BlockSpec((tm, c), lambda i, j, chip: (i * nj + j, 0))],
            out_specs=pl.BlockSpec((1, 1, tm, width), lambda i, j, chip: (chip[0], i, j, 0))),
        compiler_params=_cp("arbitrary", "arbitrary"),
    )(_scalar(chip), w)


def _columns_first(w):
    return jnp.transpose(w, (2, 0, 1))


def _cast_into_slot_w_in(w_t, chip):
    h = D // 2
    nj = h // TM
    pad = W_IN_SHARD_PAD - W_IN_SHARD

    def body(chip_ref, w_ref, o_ref):
        cols = jnp.concatenate([w_ref[:, 0, :], jnp.zeros((pad, TM), f32)], axis=0)
        o_ref[0, 0] = cols.T.astype(MX)

    return pl.pallas_call(
        body, name="cast_into_slot_w_in", out_shape=jax.ShapeDtypeStruct((4, 2, h, W_IN_SHARD_PAD), MX),
        grid_spec=pltpu.PrefetchScalarGridSpec(
            num_scalar_prefetch=1, grid=(2, nj),
            in_specs=[pl.BlockSpec((W_IN_SHARD, 1, TM), lambda i, j, chip: (0, 0, i * nj + j))],
            out_specs=pl.BlockSpec((1, 1, TM, W_IN_SHARD_PAD), lambda i, j, chip: (chip[0], i, j, 0))),
        compiler_params=_cp("arbitrary", "arbitrary"),
    )(_scalar(chip), w_t)


def _adamw_w_in(w_t, mine, other, m_t, v_t, core):
    h = D // 2
    nj = h // TM

    def body(core_ref, w_ref, a_ref, b_ref, m_ref, v_ref, g_ref, d_ref, nm_ref, nv_ref):
        g = jnp.where(pl.program_id(0) == core_ref[0], a_ref[...], b_ref[...]).T[0:W_IN_SHARD, :]
        g_ref[:, 0, :] = g
        d_ref[:, 0, :], nm_ref[:, 0, :], nv_ref[:, 0, :] = _adam_math(w_ref[:, 0, :], g, m_ref[:, 0, :], v_ref[:, 0, :])

    blk = pl.BlockSpec((W_IN_SHARD, 1, TM), lambda i, j, core: (0, 0, i * nj + j))
    gblk = pl.BlockSpec((TM, W_IN_SHARD_PAD), lambda i, j, core: (j, 0))
    return pl.pallas_call(
        body, name="adamw_w_in", out_shape=tuple([jax.ShapeDtypeStruct((W_IN_SHARD, 1, D), f32)] * 4),
        grid_spec=pltpu.PrefetchScalarGridSpec(
            num_scalar_prefetch=1, grid=(2, nj), in_specs=[blk, gblk, gblk, blk, blk], out_specs=(blk,) * 4),
        compiler_params=_cp("arbitrary", "arbitrary"),
    )(_scalar(core), w_t, mine, other, m_t, v_t)


ANY = pl.BlockSpec(memory_space=pl.ANY)


def _place():
    x, y, c = lax.axis_index("x"), lax.axis_index("y"), lax.axis_index("c")
    return x, y, c, [(1 - x, y), (x, 1 - y), (1 - x, 1 - y)]


def _gather_rows(block, rider=None):
    m_per, n = block.shape
    ri, ro = (len(rider.inputs), len(rider.out_shape)) if rider is not None else (0, 0)

    def body(x_ref, *refs):
        r_in, out_ref, r_out = refs[:ri], refs[ri], refs[ri + 1:ri + 1 + ro]
        send_sems, recv_sems, local_sem, *r_scr = refs[ri + 1 + ro:]
        x, y, c, chips = _place()
        me, sibling = (x, y, c), (x, y, 1 - c)

        def rows(px, py, pc):
            return out_ref.at[pl.ds((4 * px + 2 * py + pc) * m_per, m_per), :]

        def copy(k, blk, to, src=None):
            return pltpu.make_async_remote_copy(
                src_ref=rows(*blk) if src is None else src, dst_ref=rows(*blk), send_sem=send_sems.at[k],
                recv_sem=recv_sems.at[k], device_id=to, device_id_type=MESH)

        mine = pltpu.make_async_copy(x_ref, rows(*me), local_sem)
        mine.start()
        first = [copy(0, me, sibling, src=x_ref)]
        first += [copy(1 + j, me, (*chip, c), src=x_ref) for j, chip in enumerate(chips)]
        for cp in first:
            cp.start()
        if rider is not None:
            rider.start(r_in, r_out, r_scr)
        passed = [copy(4 + j, (*chip, c), sibling) for j, chip in enumerate(chips)]
        for j, chip in enumerate(chips):
            copy(1 + j, (*chip, c), me).wait_recv()
            passed[j].start()
        copy(0, sibling, me).wait_recv()
        for j, chip in enumerate(chips):
            copy(4 + j, (*chip, 1 - c), me).wait_recv()
        for cp in first + passed:
            cp.wait_send()
        mine.wait()
        if rider is not None:
            rider.finish(r_in, r_out, r_scr)

    vmem = pl.BlockSpec(memory_space=pltpu.VMEM)
    gathered = jax.ShapeDtypeStruct((8 * m_per, n), block.dtype)
    if rider is None:
        return pl.pallas_call(
            body, name=f"gather_rows_{m_per}x{n}", out_shape=gathered, in_specs=[vmem], out_specs=vmem,
            scratch_shapes=[pltpu.SemaphoreType.DMA((7,)), pltpu.SemaphoreType.DMA((7,)), pltpu.SemaphoreType.DMA],
            compiler_params=_VM)(block)
    outs = pl.pallas_call(
        body, name=f"gather_rows_{m_per}x{n}", out_shape=(gathered,) + tuple(rider.out_shape),
        in_specs=[vmem] + [ANY] * ri, out_specs=(vmem,) + (ANY,) * ro,
        input_output_aliases={1 + i: 1 + j for i, j in rider.aliases.items()},
        scratch_shapes=[pltpu.SemaphoreType.DMA((7,)), pltpu.SemaphoreType.DMA((7,)), pltpu.SemaphoreType.DMA] + list(rider.scratch),
        compiler_params=_VM)(block, *rider.inputs)
    return outs[0], tuple(outs[1:])


class _GatherRider:
    def __init__(self, slots, row0=0, nrows=None):
        n = len(slots)
        self.n = n
        self.rows = (row0, slots[0].shape[2] - row0 if nrows is None else nrows)
        self.inputs = list(slots)
        self.out_shape = [jax.ShapeDtypeStruct(s.shape, s.dtype) for s in slots]
        self.scratch = [pltpu.SemaphoreType.DMA((n, 6)), pltpu.SemaphoreType.DMA((n, 6))]
        self.aliases = {a: a for a in range(n)}

    def _copy(self, outs, sems, a, j, k, half, to):
        dst = outs[a].at[k, half, pl.ds(*self.rows)]
        return pltpu.make_async_remote_copy(src_ref=dst, dst_ref=dst, send_sem=sems[0].at[a, j], recv_sem=sems[1].at[a, j],
                                            device_id=to, device_id_type=MESH)

    def _first(self, outs, sems):
        x, y, c, chips = _place()
        return [self._copy(outs, sems, a, j, 2 * x + y, c, (*chip, c)) for a in range(self.n) for j, chip in enumerate(chips)]

    def start(self, ins, outs, sems):
        for cp in self._first(outs, sems):
            cp.start()

    def finish(self, ins, outs, sems):
        x, y, c, chips = _place()
        passed = []
        for a in range(self.n):
            for j, (px, py) in enumerate(chips):
                self._copy(outs, sems, a, j, 2 * px + py, c, (x, y, c)).wait_recv()
                fwd = self._copy(outs, sems, a, 3 + j, 2 * px + py, c, (x, y, 1 - c))
                fwd.start()
                passed.append(fwd)
        for a in range(self.n):
            for j, (px, py) in enumerate(chips):
                self._copy(outs, sems, a, 3 + j, 2 * px + py, 1 - c, (x, y, c)).wait_recv()
        for cp in self._first(outs, sems) + passed:
            cp.wait_send()


class _ScatterRider:
    def __init__(self, parts, row0=0, nrows=None):
        n = len(parts)
        self.n = n
        self.rows = (row0, parts[0].shape[1] - row0 if nrows is None else nrows)
        self.inputs = list(parts)
        self.out_shape = [jax.ShapeDtypeStruct((3, self.rows[1], p.shape[2]), p.dtype) for p in parts]
        self.scratch = [pltpu.SemaphoreType.DMA((3 * n,)), pltpu.SemaphoreType.DMA((3 * n,))]
        self.aliases = {}

    def _copies(self, ins, outs, sems):
        x, y, c, chips = _place()
        return [pltpu.make_async_remote_copy(
            src_ref=ins[a].at[2 * px + py, pl.ds(*self.rows)], dst_ref=outs[a].at[j], send_sem=sems[0].at[3 * a + j],
            recv_sem=sems[1].at[3 * a + j], device_id=(px, py, c), device_id_type=MESH)
            for a in range(self.n) for j, (px, py) in enumerate(chips)]

    def start(self, ins, outs, sems):
        for cp in self._copies(ins, outs, sems):
            cp.start()

    def finish(self, ins, outs, sems):
        for cp in self._copies(ins, outs, sems):
            cp.wait()


HBM = pl.BlockSpec(memory_space=pltpu.HBM)
SEM = pl.BlockSpec(memory_space=pltpu.SEMAPHORE)
EFFECT = pltpu.SideEffectType.DATAFLOW_SIDE_EFFECTING


def _split_start(rider, name, after=None):
    ni, no, ns = len(rider.inputs), len(rider.out_shape), len(rider.scratch)
    extra = [] if after is None else [after]

    def body(*refs):
        ins, lands = refs[:ni], refs[ni:ni + no]
        sems = refs[ni + no + len(extra):ni + no + len(extra) + ns]
        rider.start(ins, lands, sems)
        refs[-1][...] = jnp.zeros_like(refs[-1])

    bufs = list(rider.inputs) + [lax.empty(s.shape, s.dtype) for s in rider.out_shape]
    outs = pl.pallas_call(
        body, name=name,
        out_shape=tuple(rider.scratch) + tuple(pltpu.HBM(b.shape, b.dtype) for b in bufs) + (jax.ShapeDtypeStruct((8, LANES), f32),),
        in_specs=[HBM] * (ni + no) + [ANY] * len(extra),
        out_specs=(SEM,) * ns + (HBM,) * (ni + no) + (pl.BlockSpec(memory_space=pltpu.VMEM),),
        input_output_aliases={i: ns + i for i in range(ni + no)},
        compiler_params=pltpu.CompilerParams(has_side_effects=EFFECT),
    )(*[pltpu.with_memory_space_constraint(b, pltpu.HBM) for b in bufs], *extra)
    return outs[:-1], outs[-1]


def _split_wait(rider, name, handles, after):
    ni, no, ns = len(rider.inputs), len(rider.out_shape), len(rider.scratch)
    sems, bufs = handles[:ns], handles[ns:]

    def body(*refs):
        rider.finish(refs[:ni], refs[ni:ni + no], refs[ni + no:ni + no + ns])

    outs = pl.pallas_call(
        body, name=name, out_shape=tuple(pltpu.HBM(b.shape, b.dtype) for b in bufs),
        in_specs=[HBM] * (ni + no) + [SEM] * ns + [ANY], out_specs=(HBM,) * (ni + no),
        input_output_aliases={i: i for i in range(ni + no)}, compiler_params=pltpu.CompilerParams(has_side_effects=EFFECT),
    )(*bufs, *sems, after)
    return outs[:ni], outs[ni:]


def _ride_alone(rider, name):
    n = len(rider.inputs)

    def body(*refs):
        ins, outs, sems = refs[:n], refs[n:n + len(rider.out_shape)], refs[n + len(rider.out_shape):]
        rider.start(ins, outs, sems)
        rider.finish(ins, outs, sems)

    return pl.pallas_call(
        body, name=name, out_shape=tuple(rider.out_shape), in_specs=[ANY] * n, out_specs=tuple([ANY] * len(rider.out_shape)),
        input_output_aliases=dict(rider.aliases), scratch_shapes=list(rider.scratch),
    )(*rider.inputs)


class _SwapRider:
    def __init__(self, grads):
        n = len(grads)
        self.n = n
        self.inputs = list(grads)
        self.out_shape = [jax.ShapeDtypeStruct((4,) + g.shape[2:], g.dtype) for g in grads]
        self.scratch = [pltpu.SemaphoreType.DMA((4 * n,)), pltpu.SemaphoreType.DMA((4 * n,))]
        self.aliases = {}

    def _copies(self, ins, outs, sems):
        x, y, c, _ = _place()
        return [pltpu.make_async_remote_copy(
            src_ref=ins[a].at[k, 1 - c], dst_ref=outs[a].at[k], send_sem=sems[0].at[4 * a + k], recv_sem=sems[1].at[4 * a + k],
            device_id=(x, y, 1 - c), device_id_type=MESH) for a in range(self.n) for k in range(4)]

    def start(self, ins, outs, sems):
        for cp in self._copies(ins, outs, sems):
            cp.start()

    def finish(self, ins, outs, sems):
        for cp in self._copies(ins, outs, sems):
            cp.wait()


class _Riders:
    def __init__(self, riders):
        self.riders = list(riders)
        self.inputs = [a for r in riders for a in r.inputs]
        self.out_shape = [s for r in riders for s in r.out_shape]
        self.scratch = [s for r in riders for s in r.scratch]
        self.aliases = {}
        i = o = 0
        for r in riders:
            self.aliases.update({i + a: o + b for a, b in r.aliases.items()})
            i, o = i + len(r.inputs), o + len(r.out_shape)

    def _each(self, ins, outs, sems):
        i = o = s = 0
        for r in self.riders:
            yield r, ins[i:i + len(r.inputs)], outs[o:o + len(r.out_shape)], sems[s:s + len(r.scratch)]
            i, o, s = i + len(r.inputs), o + len(r.out_shape), s + len(r.scratch)

    def start(self, ins, outs, sems):
        for r, a, b, c in self._each(ins, outs, sems):
            r.start(a, b, c)

    def finish(self, ins, outs, sems):
        for r, a, b, c in self._each(ins, outs, sems):
            r.finish(a, b, c)

    def split(self, outs):
        res, o = [], 0
        for r in self.riders:
            res.append(outs[o:o + len(r.out_shape)])
            o += len(r.out_shape)
        return res


class _Reducer:
    def __init__(self, chip, core):
        self.chip, self.core, self.grads, self.parts, self.sums, self.others = chip, core, {}, {}, {}, {}

    def swap(self, name, grad):
        self.grads[name] = grad
        return _SwapRider([grad])

    def swapped(self, name, got):
        self.parts[name] = _add_pair(self.grads[name], got[0], self.core, name)

    def scatter(self, name, row0=0, nrows=None):
        return _ScatterRider([self.parts[name]], row0, nrows)

    def scattered(self, name, others):
        self.sums[name] = _add_chips(self.parts[name], others[0], self.chip, name)


class _SwapSumsRider:
    def __init__(self, halves):
        n = len(halves)
        self.n = n
        self.inputs = list(halves)
        self.out_shape = [jax.ShapeDtypeStruct(s.shape, s.dtype) for s in halves]
        self.scratch = [pltpu.SemaphoreType.DMA((n,)), pltpu.SemaphoreType.DMA((n,))]
        self.aliases = {}

    def _copies(self, ins, outs, sems):
        x, y, c, _ = _place()
        return [pltpu.make_async_remote_copy(
            src_ref=ins[a], dst_ref=outs[a], send_sem=sems[0].at[a], recv_sem=sems[1].at[a],
            device_id=(x, y, 1 - c), device_id_type=MESH) for a in range(self.n)]

    def start(self, ins, outs, sems):
        for cp in self._copies(ins, outs, sems):
            cp.start()

    def finish(self, ins, outs, sems):
        for cp in self._copies(ins, outs, sems):
            cp.wait()


def _row_tile(r):
    for tm in (TM, 176, 128, 64, 32, 16, 8):
        if r % tm == 0:
            return tm
    return r


def _add_pair(mine, got, core, name):
    k, _, h, c = mine.shape
    tm = _row_tile(h)

    def body(core_ref, a_ref, b_ref, o_ref):
        o_ref[0] = (a_ref[0, 0].astype(f32) + b_ref[0].astype(f32)).astype(MX)

    blk = pl.BlockSpec((1, tm, c), lambda i, j, core: (i, j, 0))
    return pl.pallas_call(
        body, name="add_pair_" + name, out_shape=jax.ShapeDtypeStruct((k, h, c), MX),
        grid_spec=pltpu.PrefetchScalarGridSpec(
            num_scalar_prefetch=1, grid=(k, h // tm),
            in_specs=[pl.BlockSpec((1, 1, tm, c), lambda i, j, core: (i, core[0], j, 0)), blk], out_specs=blk),
        compiler_params=_cp("arbitrary", "arbitrary"),
    )(_scalar(core), mine, got)


def _add_chips(parts, others, chip, name, row0=0):
    _, n, c = others.shape
    tm = _row_tile(n)
    assert row0 % tm == 0
    i0 = row0 // tm

    def body(chip_ref, a_ref, b_ref, o_ref):
        s = a_ref[0].astype(f32) + b_ref[0].astype(f32)
        o_ref[...] = (s + b_ref[1].astype(f32)) + b_ref[2].astype(f32)

    return pl.pallas_call(
        body, name="add_chips_" + name, out_shape=jax.ShapeDtypeStruct((n, c), f32),
        grid_spec=pltpu.PrefetchScalarGridSpec(
            num_scalar_prefetch=1, grid=(n // tm,),
            in_specs=[pl.BlockSpec((1, tm, c), lambda i, chip: (chip[0], i0 + i, 0)),
                      pl.BlockSpec((3, tm, c), lambda i, chip: (0, i, 0))],
            out_specs=pl.BlockSpec((tm, c), lambda i, chip: (i, 0))),
        compiler_params=_cp("arbitrary"),
    )(_scalar(chip), parts, others)


def _adam_math(w, g, m, v):
    m = ADAM_B1 * m + (1.0 - ADAM_B1) * g
    v = ADAM_B2 * v + (1.0 - ADAM_B2) * (g * g)
    m_hat = m / (1.0 - ADAM_B1 ** ADAM_STEP)
    v_hat = v / (1.0 - ADAM_B2 ** ADAM_STEP)
    return -ADAM_LR * (m_hat / (jnp.sqrt(v_hat) + ADAM_EPS) + ADAM_WD * w), m, v


def _adamw_halves(w, mine, other, m, v, core, name, after):
    r, c = w.shape
    h = r // 2
    tm = _row_tile(h)
    nj = h // tm
    cg = mine.shape[1]

    def body(core_ref, w_ref, a_ref, b_ref, m_ref, v_ref, after_ref, g_ref, d_ref, nm_ref, nv_ref):
        g = jnp.where(pl.program_id(0) == core_ref[0], a_ref[:, 0:c], b_ref[:, 0:c])
        g_ref[...] = g
        d_ref[...], nm_ref[...], nv_ref[...] = _adam_math(w_ref[...], g, m_ref[...], v_ref[...])

    blk = pl.BlockSpec((tm, c), lambda i, j, core: (i * nj + j, 0))
    gblk = pl.BlockSpec((tm, cg), lambda i, j, core: (j, 0))
    return _call(body, name=name, grid=(2, nj), out_shape=[jax.ShapeDtypeStruct((r, c), f32)] * 4,
                 in_specs=[blk, gblk, gblk, blk, blk, ANY], out_specs=(blk,) * 4, sem=("arbitrary", "arbitrary"),
                 prefetch=(_scalar(core),), args=(w, mine, other, m, v, after))[0]


def _ada_forward(c_all, ada_w):
    def body(c_ref, w_ref, o_ref):
        o_ref[...] = jnp.dot(_silu(c_ref[...]).astype(MX), w_ref[...].astype(MX), preferred_element_type=f32)

    return pl.pallas_call(body, name="ada_forward", out_shape=jax.ShapeDtypeStruct((8, ada_w.shape[1]), f32),
                          compiler_params=pltpu.CompilerParams(vmem_limit_bytes=VMEM_LIMIT))(c_all, ada_w)


def _ada_adamw(c_all_t, d_mod, w, m, v, after):
    r, c = w.shape
    tm = TM

    def body(ct_ref, dm_ref, w_ref, m_ref, v_ref, after_ref, g_ref, d_ref, nm_ref, nv_ref):
        ca = _silu(ct_ref[...])
        g = ca[:, 0:1] * dm_ref[0:1, :]
        for b in range(1, 8):
            g = g + ca[:, b:b + 1] * dm_ref[b:b + 1, :]
        g_ref[...] = g
        d_ref[...], nm_ref[...], nv_ref[...] = _adam_math(w_ref[...], g, m_ref[...], v_ref[...])

    blk = pl.BlockSpec((tm, c), lambda i: (i, 0))
    return _call(body, name="ada_adamw", grid=(r // tm,), out_shape=[jax.ShapeDtypeStruct((r, c), f32)] * 4,
                 in_specs=[pl.BlockSpec((tm, 8), lambda i: (i, 0)), pl.BlockSpec((8, c), lambda i: (0, 0)), blk, blk, blk, ANY],
                 out_specs=(blk,) * 4, sem=("arbitrary",), args=(c_all_t, d_mod, w, m, v, after))[0]


WEIGHTS = ("ada_w", "ada_b", "norm1_w", "w_in", "ssd_conv_w", "ssd_conv_b", "dt_bias", "a_log", "d_skip", "ssd_norm_w",
           "conf_conv_w", "conf_conv_b", "conf_ln_w", "conf_ln_b", "w_out", "norm2_w", "w_up", "ffn_conv_w", "ffn_conv_b",
           "w_down", "final_norm_w")


def kernel(x, c, ada_w, ada_b, norm1_w, w_in, ssd_conv_w, ssd_conv_b, dt_bias, a_log, d_skip, ssd_norm_w, conf_conv_w, conf_conv_b, conf_ln_w, conf_ln_b, w_out, norm2_w, w_up, ffn_conv_w, ffn_conv_b, w_down, final_norm_w, loss_target, m_ada_w, m_ada_b, m_norm1_w, m_w_in, m_ssd_conv_w, m_ssd_conv_b, m_dt_bias, m_a_log, m_d_skip, m_ssd_norm_w, m_conf_conv_w, m_conf_conv_b, m_conf_ln_w, m_conf_ln_b, m_w_out, m_norm2_w, m_w_up, m_ffn_conv_w, m_ffn_conv_b, m_w_down, m_final_norm_w, v_ada_w, v_ada_b, v_norm1_w, v_w_in, v_ssd_conv_w, v_ssd_conv_b, v_dt_bias, v_a_log, v_d_skip, v_ssd_norm_w, v_conf_conv_w, v_conf_conv_b, v_conf_ln_w, v_conf_ln_b, v_w_out, v_norm2_w, v_w_up, v_ffn_conv_w, v_ffn_conv_b, v_w_down, v_final_norm_w):
    given = dict(locals())
    w = {n: given[n] for n in WEIGHTS}
    mom = {n: given["m_" + n] for n in WEIGHTS}
    var = {n: given["v_" + n] for n in WEIGHTS}
    chip = 2 * lax.axis_index("x") + lax.axis_index("y")
    me = 2 * chip + lax.axis_index("c")

    core = lax.axis_index("c")
    a_in = _cast_into_slot_w_in(_columns_first(w_in), chip)
    got, (a_in,) = _gather_rows(_pack_front(c, [w[n] for n in CONVS]), _GatherRider([a_in], 0, D // 4))
    c_all, *convs = _unpack_front(got)
    conv_full = dict(zip(CONVS, convs))

    got, (a_in,) = _gather_rows(_ada_forward(c_all, ada_w[0]), _GatherRider([a_in], D // 4, D // 4))
    mod_cols = got.reshape(8, 8, -1)[0::2]
    mod = lax.dynamic_index_in_dim(mod_cols, me, axis=1, keepdims=False).reshape(1, 6 * D) + ada_b
    w_pack = _pack_w_in(a_in.reshape(4, D, W_IN_SHARD_PAD))
    late = (_cast_into_slot(w_out[0], D, chip), _cast_into_slot(w_up[0], UP_SHARD, chip), _cast_into_slot(w_down[0], D, chip))

    flat = lambda a: a.reshape(1, -1) if a.ndim == 1 else a
    small = {n: flat(w[n]) for n in VECTORS if n != "ada_b"}
    small.update(conv_full)
    reducer = _Reducer(chip, core)
    _, grad_x, _, gsmall = _local_step(x[0], mod, loss_target[0], w_pack, late, small, reducer)
    grads, delta, new_m, new_v = {}, {}, {}, {}

    names = VECTORS + tuple(CONVS)
    d_mod_mine, loss, res = _small_adamw(_gather_rows(gsmall), chip, *[{n: flat(d[n]) for n in names} for d in (w, mom, var)])
    for n in names:
        grads[n], delta[n], new_m[n], new_v[n] = [r.reshape(w[n].shape) for r in res[n]]

    scatter = reducer.scatter("w_in")
    handles, token = _split_start(scatter, "scatter_start_w_in", after=d_mod_mine)
    for n in ("w_up", "w_down", "w_out"):
        res = _adamw_halves(w[n][0], reducer.sums[n], reducer.others[n], mom[n][0], var[n][0], core, "adamw_" + n, token)
        grads[n], delta[n], new_m[n], new_v[n] = [r[None] for r in res]
    res = _ada_adamw(c_all.T, d_mod_mine, ada_w[0], m_ada_w[0], v_ada_w[0], token)
    grads["ada_w"], delta["ada_w"], new_m["ada_w"], new_v["ada_w"] = [r[None] for r in res]
    (reducer.parts["w_in"],), others = _split_wait(scatter, "scatter_wait_w_in", handles, res[1])
    reducer.scattered("w_in", others)
    reducer.others["w_in"], = _ride_alone(_SwapSumsRider([reducer.sums["w_in"]]), "swap_sums_w_in")
    res = _adamw_w_in(_columns_first(w_in), reducer.sums["w_in"], reducer.others["w_in"], _columns_first(m_w_in),
                      _columns_first(v_w_in), core)
    grads["w_in"], delta["w_in"], new_m["w_in"], new_v["w_in"] = [jnp.transpose(r, (1, 2, 0)) for r in res]

    return (loss, grad_x[None], *[grads[n] for n in WEIGHTS], *[delta[n] for n in WEIGHTS],
            *[new_m[n] for n in WEIGHTS], *[new_v[n] for n in WEIGHTS])
```

```python
import functools

import jax
import jax.numpy as jnp
from jax import lax
from jax.experimental import pallas as pl
from jax.experimental.pallas import tpu as pltpu

f32 = jnp.float32
MX = jnp.bfloat16

D = 1024
HEADS = 16
HEAD_P = 64
STATE_N = 128
D_XBC = 1536
D_FF = 2816
UP_SHARD = 2 * D_FF // 4
UP_EARLY_ROWS = 128
K_SSD, K_CONF, K_FFN = 4, 31, 3
CHUNK = 128
OFF_Z, OFF_XBC, OFF_CA, OFF_CG, OFF_DT = 0, 1024, 2560, 3584, 4608
W_PACK = 4736
TM = 256
CW = 256
RC = 64
LANES = 128
VMEM_LIMIT = 56 * 1024 * 1024

ADAM_LR, ADAM_B1, ADAM_B2, ADAM_EPS, ADAM_WD, ADAM_STEP = 0.001, 0.9, 0.999, 1e-08, 0.01, 10

MESH = pl.DeviceIdType.MESH


def _cp(*sem):
    return pltpu.CompilerParams(dimension_semantics=sem, vmem_limit_bytes=VMEM_LIMIT)


def _resident(shape):
    nd = len(shape)
    return pl.BlockSpec(shape, lambda *_: (0,) * nd, pipeline_mode=pl.Buffered(1))


def _row(width=D):
    return pl.BlockSpec((1, width), lambda *_: (0, 0))


def _call(body, *, name, grid, in_specs, out_specs, out_shape, args, sem, scratch_shapes=(), prefetch=(), rider=None):
    ni, no, ns, npf = len(in_specs), len(out_specs), len(scratch_shapes), len(prefetch)
    ri, ro = (len(rider.inputs), len(rider.out_shape)) if rider is not None else (0, 0)

    def full(*refs):
        pre, refs = refs[:npf], refs[npf:]
        base_in, r_in = refs[:ni], refs[ni:ni + ri]
        base_out, r_out = refs[ni + ri:ni + ri + no], refs[ni + ri + no:ni + ri + no + ro]
        base_scr, r_scr = refs[ni + ri + no + ro:ni + ri + no + ro + ns], refs[ni + ri + no + ro + ns:]
        if rider is None:
            return body(*pre, *base_in, *base_out, *base_scr)
        ids = [pl.program_id(a) for a in range(len(grid))]
        first = functools.reduce(jnp.logical_and, [i == 0 for i in ids])
        last = functools.reduce(jnp.logical_and, [i == g - 1 for i, g in zip(ids, grid)])

        @pl.when(first)
        def _():
            rider.start(r_in, r_out, r_scr)

        body(*pre, *base_in, *base_out, *base_scr)

        @pl.when(last)
        def _():
            rider.finish(r_in, r_out, r_scr)

    extra = dict(shapes=[], scratch=[], aliases={}, inputs=[]) if rider is None else dict(
        shapes=rider.out_shape, scratch=rider.scratch, inputs=rider.inputs,
        aliases={npf + ni + i: no + j for i, j in rider.aliases.items()})
    outs = pl.pallas_call(
        full, name=name, out_shape=tuple(out_shape) + tuple(extra["shapes"]), input_output_aliases=extra["aliases"],
        grid_spec=pltpu.PrefetchScalarGridSpec(
            num_scalar_prefetch=npf, grid=grid, in_specs=list(in_specs) + [ANY] * ri,
            out_specs=tuple(out_specs) + (ANY,) * ro, scratch_shapes=list(scratch_shapes) + list(extra["scratch"])),
        compiler_params=_cp(*sem),
    )(*prefetch, *args, *extra["inputs"])
    return tuple(outs[:no]), tuple(outs[no:])


def _silu(v):
    return v * jax.nn.sigmoid(v)


def _dsilu(v):
    s = jax.nn.sigmoid(v)
    return s * (1.0 + v * (1.0 - s))


def _softplus(v):
    return jnp.maximum(v, 0.0) + jnp.log1p(jnp.exp(-jnp.abs(v)))


def _mm(a, b):
    return jnp.dot(a.astype(MX), b.astype(MX), preferred_element_type=f32)


def _mm_nt(a, b):
    return lax.dot_general(a.astype(MX), b.astype(MX), (((1,), (1,)), ((), ())), preferred_element_type=f32)


def _mm_tn(a, b):
    return lax.dot_general(a.astype(MX), b.astype(MX), (((0,), (0,)), ((), ())), preferred_element_type=f32)


def _ln_inproj(x, mod, norm1_w, w_pack, rider=None):
    t = x.shape[0]

    def body(x_ref, mod_ref, nw_ref, w_ref, proj_ref, ht_ref):
        xv = x_ref[...]
        rstd = lax.rsqrt(jnp.mean(xv * xv, axis=-1, keepdims=True) + 1e-6)
        h = (xv * rstd * nw_ref[...]) * (1.0 + mod_ref[:, D:2 * D]) + mod_ref[:, 0:D]
        hb = h.astype(MX)
        ht_ref[...] = hb.T
        proj_ref[...] = jnp.dot(hb, w_ref[...], preferred_element_type=f32)

    return _call(
        body, name="ln_inproj", grid=(t // TM,),
        out_shape=(jax.ShapeDtypeStruct((t, W_PACK), f32), jax.ShapeDtypeStruct((D, t), MX)),
        in_specs=[pl.BlockSpec((TM, D), lambda i: (i, 0)), _row(6 * D), _row(), _resident((D, W_PACK))],
        out_specs=(pl.BlockSpec((TM, W_PACK), lambda i: (i, 0)), pl.BlockSpec((D, TM), lambda i: (0, i))),
        sem=("arbitrary",), args=(x, mod, norm1_w, w_pack), rider=rider)


def _ssd_gate_norm(y_scan, xbc_act, proj, d_skip_row, ssd_norm_w):
    t = y_scan.shape[0]

    def body(y_ref, xs_ref, z_ref, dsk_ref, nw_ref, o_ref):
        y = y_ref[...] + xs_ref[...] * dsk_ref[...]
        yz = y * _silu(z_ref[...])
        rstd = lax.rsqrt(jnp.mean(yz * yz, axis=-1, keepdims=True) + 1e-6)
        o_ref[...] = (yz * rstd * nw_ref[...]).astype(MX)

    blk = pl.BlockSpec((TM, D), lambda i: (i, 0))
    return pl.pallas_call(
        body, name="ssd_gate_norm", grid=(t // TM,), out_shape=jax.ShapeDtypeStruct((t, D), MX),
        in_specs=[blk, blk, blk, _row(), _row()], out_specs=blk, compiler_params=_cp("arbitrary"),
    )(y_scan, xbc_act, proj, d_skip_row, ssd_norm_w)


def _ln_silu(u_conv, ln_w, ln_b):
    t = u_conv.shape[0]

    def body(u_ref, w_ref, b_ref, o_ref):
        u = u_ref[...]
        mu = jnp.mean(u, axis=-1, keepdims=True)
        uc = u - mu
        rstd = lax.rsqrt(jnp.mean(uc * uc, axis=-1, keepdims=True) + 1e-5)
        o_ref[...] = _silu(uc * rstd * w_ref[...] + b_ref[...]).astype(MX)

    blk = pl.BlockSpec((TM, D), lambda i: (i, 0))
    return pl.pallas_call(
        body, name="ln_silu", grid=(t // TM,), out_shape=jax.ShapeDtypeStruct((t, D), MX),
        in_specs=[blk, _row(), _row()], out_specs=blk, compiler_params=_cp("arbitrary"),
    )(u_conv, ln_w, ln_b)


def _outproj_ln2_up(y_ssd, u, w_out, x, mod, norm2_w, w_up):
    t = x.shape[0]

    def body(y_ref, u_ref, wo_ref, x_ref, mod_ref, nw_ref, wu_ref, mix_ref, x1_ref, h2t_ref, up_ref):
        mix = jnp.dot(y_ref[...], wo_ref[0:D, :], preferred_element_type=f32)
        mix = mix + jnp.dot(u_ref[...], wo_ref[D:2 * D, :], preferred_element_type=f32)
        mix_ref[...] = mix
        x1 = x_ref[...] + mod_ref[:, 2 * D:3 * D] * mix
        x1_ref[...] = x1
        rstd = lax.rsqrt(jnp.mean(x1 * x1, axis=-1, keepdims=True) + 1e-6)
        h2 = ((x1 * rstd * nw_ref[...]) * (1.0 + mod_ref[:, 4 * D:5 * D]) + mod_ref[:, 3 * D:4 * D]).astype(MX)
        h2t_ref[...] = h2.T
        for k in range(4):
            up_ref[:, k * UP_SHARD:(k + 1) * UP_SHARD] = jnp.dot(h2, wu_ref[k], preferred_element_type=f32)

    blk = pl.BlockSpec((TM, D), lambda i: (i, 0))
    return pl.pallas_call(
        body, name="outproj_ln2_up", grid=(t // TM,),
        out_shape=(jax.ShapeDtypeStruct((t, D), f32), jax.ShapeDtypeStruct((t, D), f32),
                   jax.ShapeDtypeStruct((D, t), MX), jax.ShapeDtypeStruct((t, 2 * D_FF), f32)),
        in_specs=[blk, blk, _resident((2 * D, D)), blk, _row(6 * D), _row(), _resident((4, D, UP_SHARD))],
        out_specs=(blk, blk, pl.BlockSpec((D, TM), lambda i: (0, i)), pl.BlockSpec((TM, 2 * D_FF), lambda i: (i, 0))),
        compiler_params=_cp("arbitrary"),
    )(y_ssd, u, w_out, x, mod, norm2_w, w_up)


def _down_loss(act, w_down, x1, mod, final_norm_w, target):
    t = x1.shape[0]

    def body(a_ref, wd_ref, x1_ref, mod_ref, wf_ref, tgt_ref, dx2_ref, dffn_ref, dact_ref, st_ref):
        @pl.when(pl.program_id(0) == 0)
        def _():
            st_ref[...] = jnp.zeros_like(st_ref)

        g2 = mod_ref[:, 5 * D:6 * D]
        ffn = jnp.dot(a_ref[...], wd_ref[...], preferred_element_type=f32)
        x2 = x1_ref[...] + g2 * ffn
        rstd = lax.rsqrt(jnp.mean(x2 * x2, axis=-1, keepdims=True) + 1e-6)
        xh = x2 * rstd
        wf = wf_ref[...]
        err = xh * wf - tgt_ref[...]
        dy = err * (1.0 / D)
        dxh = dy * wf
        dx2 = rstd * (dxh - xh * jnp.mean(dxh * xh, axis=-1, keepdims=True))
        dx2_ref[...] = dx2
        dffn = (g2 * dx2).astype(MX)
        dffn_ref[...] = dffn
        dact_ref[...] = lax.dot_general(dffn, wd_ref[...], (((1,), (1,)), ((), ())), preferred_element_type=f32)
        st_ref[0:1, :] += jnp.sum(dy * xh, axis=0, keepdims=True)
        st_ref[1:2, :] += jnp.sum(dx2 * ffn, axis=0, keepdims=True)
        st_ref[2:3, :] += jnp.sum(0.5 * jnp.mean(err * err, axis=-1, keepdims=True), axis=0, keepdims=True)

    blk = pl.BlockSpec((TM, D), lambda i: (i, 0))
    ablk = pl.BlockSpec((TM, D_FF), lambda i: (i, 0))
    return pl.pallas_call(
        body, name="down_loss", grid=(t // TM,),
        out_shape=(jax.ShapeDtypeStruct((t, D), f32), jax.ShapeDtypeStruct((t, D), MX),
                   jax.ShapeDtypeStruct((t, D_FF), f32), jax.ShapeDtypeStruct((8, D), f32)),
        in_specs=[ablk, _resident((D_FF, D)), blk, _row(6 * D), _row(), blk],
        out_specs=(blk, blk, ablk, pl.BlockSpec((8, D), lambda i: (0, 0))),
        compiler_params=_cp("arbitrary"),
    )(act, w_down, x1, mod, final_norm_w, target)


def _pad_of(k):
    return 8 * ((k - 1 + 7) // 8)


def _causal_win(ref, r, t, pad):
    base = pl.multiple_of(r * RC, RC)
    prev = ref[pl.ds(pl.multiple_of(jnp.maximum(base - pad, 0), 8), pad), :]
    prev = jnp.where(r > 0, prev, 0.0)
    return jnp.concatenate([prev, ref[pl.ds(base, RC), :]], axis=0)


def _anti_win(ref, r, t, pad):
    base = pl.multiple_of(r * RC, RC)
    nxt = ref[pl.ds(pl.multiple_of(jnp.minimum(base + RC, t - pad), 8), pad), :]
    nxt = jnp.where(r < t // RC - 1, nxt, 0.0)
    return jnp.concatenate([ref[pl.ds(base, RC), :], nxt], axis=0)


def _shifted(win, offsets):
    for r in range(8):
        mine = [o for o in offsets if o % 8 == r]
        if mine:
            rolled = win if r == 0 else pltpu.roll(win, win.shape[0] - r, 0)
            for o in mine:
                yield o, rolled[o - r:o - r + RC, :]


def _conv_taps(win, w_ref, k, pad):
    first = pad - (k - 1)
    acc = None
    for o, rows in _shifted(win, range(first, first + k)):
        term = w_ref[o - first:o - first + 1, :] * rows
        acc = term if acc is None else acc + term
    return acc


def _corr_taps(win, w_ref, k):
    acc = None
    for o, rows in _shifted(win, range(k)):
        term = w_ref[k - 1 - o:k - o, :] * rows
        acc = term if acc is None else acc + term
    return acc


def _dw_accumulate(dw_scr, d, win, k, pad):
    first = pad - (k - 1)
    for o, rows in _shifted(win, range(first, first + k)):
        j = o - first
        prod = d * rows
        dw_scr[8 * j:8 * j + 8, :] += prod.reshape(RC // 8, 8, prod.shape[-1]).sum(axis=0)


def _dw_finish(dw_scr, dw_ref, k):
    for j in range(k):
        dw_ref[j:j + 1, :] = jnp.sum(dw_scr[8 * j:8 * j + 8, :], axis=0, keepdims=True)


def _rows8(v):
    return v.reshape(RC // 8, 8, v.shape[-1]).sum(axis=0)


def _ssd_conv_fwd(proj, conv_w, conv_b, rider=None):
    t = proj.shape[0]
    pad = _pad_of(K_SSD)
    c0 = OFF_XBC // CW

    def body(x_ref, w_ref, b_ref, o_ref):
        def step(r, carry):
            win = _causal_win(x_ref, r, t, pad)
            o_ref[pl.ds(pl.multiple_of(r * RC, RC), RC), :] = _silu(_conv_taps(win, w_ref, K_SSD, pad) + b_ref[...])
            return carry
        lax.fori_loop(0, t // RC, step, 0)

    return _call(
        body, name="ssd_conv_fwd", grid=(D_XBC // CW,), out_shape=(jax.ShapeDtypeStruct((t, D_XBC), f32),),
        in_specs=[pl.BlockSpec((t, CW), lambda j: (0, c0 + j)), pl.BlockSpec((K_SSD, CW), lambda j: (0, j)),
                  pl.BlockSpec((1, CW), lambda j: (0, j))],
        out_specs=(pl.BlockSpec((t, CW), lambda j: (0, j)),), sem=("arbitrary",), args=(proj, conv_w, conv_b), rider=rider)


def _glu_conv_fwd(proj, conv_w, conv_b, rider=None):
    t = proj.shape[0]
    pad = _pad_of(K_CONF)
    ca, cg = OFF_CA // CW, OFF_CG // CW

    def body(a_ref, g_ref, w_ref, b_ref, o_ref, v_scr):
        def glu(r, carry):
            rows = pl.ds(pl.multiple_of(r * RC, RC), RC)
            v_scr[rows, :] = a_ref[rows, :] * jax.nn.sigmoid(g_ref[rows, :])
            return carry
        lax.fori_loop(0, t // RC, glu, 0)

        def step(r, carry):
            win = _causal_win(v_scr, r, t, pad)
            o_ref[pl.ds(pl.multiple_of(r * RC, RC), RC), :] = _conv_taps(win, w_ref, K_CONF, pad) + b_ref[...]
            return carry
        lax.fori_loop(0, t // RC, step, 0)

    return _call(
        body, name="glu_conv_fwd", grid=(D // CW,), out_shape=(jax.ShapeDtypeStruct((t, D), f32),),
        in_specs=[pl.BlockSpec((t, CW), lambda j: (0, ca + j)), pl.BlockSpec((t, CW), lambda j: (0, cg + j)),
                  pl.BlockSpec((K_CONF, CW), lambda j: (0, j)), pl.BlockSpec((1, CW), lambda j: (0, j))],
        out_specs=(pl.BlockSpec((t, CW), lambda j: (0, j)),),
        scratch_shapes=[pltpu.VMEM((t, CW), f32)], sem=("arbitrary",), args=(proj, proj, conv_w, conv_b), rider=rider)


def _ffn_conv_fwd(up, conv_w, conv_b, rider=None):
    t = up.shape[0]
    pad = _pad_of(K_FFN)
    nb = D_FF // CW

    def body(g_ref, v_ref, wg_ref, wv_ref, bg_ref, bv_ref, o_ref):
        def step(r, carry):
            gc = _conv_taps(_causal_win(g_ref, r, t, pad), wg_ref, K_FFN, pad) + bg_ref[...]
            vc = _conv_taps(_causal_win(v_ref, r, t, pad), wv_ref, K_FFN, pad) + bv_ref[...]
            o_ref[pl.ds(pl.multiple_of(r * RC, RC), RC), :] = (_silu(gc) * vc).astype(MX)
            return carry
        lax.fori_loop(0, t // RC, step, 0)

    return _call(
        body, name="ffn_conv_fwd", grid=(nb,), out_shape=(jax.ShapeDtypeStruct((t, D_FF), MX),),
        in_specs=[pl.BlockSpec((t, CW), lambda j: (0, j)), pl.BlockSpec((t, CW), lambda j: (0, nb + j)),
                  pl.BlockSpec((K_FFN, CW), lambda j: (0, j)), pl.BlockSpec((K_FFN, CW), lambda j: (0, nb + j)),
                  pl.BlockSpec((1, CW), lambda j: (0, j)), pl.BlockSpec((1, CW), lambda j: (0, nb + j))],
        out_specs=(pl.BlockSpec((t, CW), lambda j: (0, j)),), sem=("arbitrary",),
        args=(up, up, conv_w, conv_w, conv_b, conv_b), rider=rider)


def _ffn_conv_bwd(up, conv_w, conv_b, d_act, rider=None):
    t = up.shape[0]
    pad = _pad_of(K_FFN)
    nb = D_FF // CW

    def body(g_ref, v_ref, wg_ref, wv_ref, bg_ref, bv_ref, da_ref, dup_ref, dw_ref, db_ref,
             dg_scr, dv_scr, dwg_scr, dwv_scr, db_scr):
        dwg_scr[...] = jnp.zeros_like(dwg_scr)
        dwv_scr[...] = jnp.zeros_like(dwv_scr)
        db_scr[...] = jnp.zeros_like(db_scr)

        def first(r, carry):
            rows = pl.ds(pl.multiple_of(r * RC, RC), RC)
            gwin = _causal_win(g_ref, r, t, pad)
            vwin = _causal_win(v_ref, r, t, pad)
            gc = _conv_taps(gwin, wg_ref, K_FFN, pad) + bg_ref[...]
            vc = _conv_taps(vwin, wv_ref, K_FFN, pad) + bv_ref[...]
            da = da_ref[rows, :]
            dgc = da * vc * _dsilu(gc)
            dvc = da * _silu(gc)
            dg_scr[rows, :] = dgc
            dv_scr[rows, :] = dvc
            _dw_accumulate(dwg_scr, dgc, gwin, K_FFN, pad)
            _dw_accumulate(dwv_scr, dvc, vwin, K_FFN, pad)
            db_scr[0:8, :] += _rows8(dgc)
            db_scr[8:16, :] += _rows8(dvc)
            return carry
        lax.fori_loop(0, t // RC, first, 0)

        def second(r, carry):
            rows = pl.ds(pl.multiple_of(r * RC, RC), RC)
            dup_ref[0, rows, :] = _corr_taps(_anti_win(dg_scr, r, t, pad), wg_ref, K_FFN).astype(MX)
            dup_ref[1, rows, :] = _corr_taps(_anti_win(dv_scr, r, t, pad), wv_ref, K_FFN).astype(MX)
            return carry
        lax.fori_loop(0, t // RC, second, 0)

        for j in range(K_FFN):
            dw_ref[0, j:j + 1, :] = jnp.sum(dwg_scr[8 * j:8 * j + 8, :], axis=0, keepdims=True)
            dw_ref[1, j:j + 1, :] = jnp.sum(dwv_scr[8 * j:8 * j + 8, :], axis=0, keepdims=True)
        db_ref[0] = jnp.sum(db_scr[0:8, :], axis=0, keepdims=True)
        db_ref[1] = jnp.sum(db_scr[8:16, :], axis=0, keepdims=True)

    return _call(
        body, name="ffn_conv_bwd", grid=(nb,),
        out_shape=(jax.ShapeDtypeStruct((2, t, D_FF), MX), jax.ShapeDtypeStruct((2, K_FFN, D_FF), f32),
                   jax.ShapeDtypeStruct((2, 1, D_FF), f32)),
        in_specs=[pl.BlockSpec((t, CW), lambda j: (0, j)), pl.BlockSpec((t, CW), lambda j: (0, nb + j)),
                  pl.BlockSpec((K_FFN, CW), lambda j: (0, j)), pl.BlockSpec((K_FFN, CW), lambda j: (0, nb + j)),
                  pl.BlockSpec((1, CW), lambda j: (0, j)), pl.BlockSpec((1, CW), lambda j: (0, nb + j)),
                  pl.BlockSpec((t, CW), lambda j: (0, j))],
        out_specs=(pl.BlockSpec((2, t, CW), lambda j: (0, 0, j)), pl.BlockSpec((2, K_FFN, CW), lambda j: (0, 0, j)),
                   pl.BlockSpec((2, 1, CW), lambda j: (0, 0, j))),
        scratch_shapes=[pltpu.VMEM((t, CW), f32), pltpu.VMEM((t, CW), f32), pltpu.VMEM((8 * K_FFN, CW), f32),
                        pltpu.VMEM((8 * K_FFN, CW), f32), pltpu.VMEM((16, CW), f32)],
        sem=("arbitrary",), args=(up, up, conv_w, conv_w, conv_b, conv_b, d_act), rider=rider)


def _glu_conv_bwd(proj, conv_w, d_uconv, rider=None):
    t = proj.shape[0]
    pad = _pad_of(K_CONF)
    ca, cg = OFF_CA // CW, OFF_CG // CW

    def body(a_ref, g_ref, w_ref, du_ref, dc_ref, dw_ref, db_ref, v_scr, dw_scr, db_scr):
        dw_scr[...] = jnp.zeros_like(dw_scr)
        db_scr[...] = jnp.zeros_like(db_scr)

        def glu(r, carry):
            rows = pl.ds(pl.multiple_of(r * RC, RC), RC)
            v_scr[rows, :] = a_ref[rows, :] * jax.nn.sigmoid(g_ref[rows, :])
            return carry
        lax.fori_loop(0, t // RC, glu, 0)

        def step(r, carry):
            rows = pl.ds(pl.multiple_of(r * RC, RC), RC)
            du = du_ref[rows, :]
            _dw_accumulate(dw_scr, du, _causal_win(v_scr, r, t, pad), K_CONF, pad)
            db_scr[...] += _rows8(du)
            dv = _corr_taps(_anti_win(du_ref, r, t, pad), w_ref, K_CONF)
            a = a_ref[rows, :]
            s = jax.nn.sigmoid(g_ref[rows, :])
            dc_ref[0, rows, :] = (dv * s).astype(MX)
            dc_ref[1, rows, :] = (dv * a * s * (1.0 - s)).astype(MX)
            return carry
        lax.fori_loop(0, t // RC, step, 0)
        _dw_finish(dw_scr, dw_ref, K_CONF)
        db_ref[...] = jnp.sum(db_scr[...], axis=0, keepdims=True)

    return _call(
        body, name="glu_conv_bwd", grid=(D // CW,),
        out_shape=(jax.ShapeDtypeStruct((2, t, D), MX), jax.ShapeDtypeStruct((K_CONF, D), f32),
                   jax.ShapeDtypeStruct((1, D), f32)),
        in_specs=[pl.BlockSpec((t, CW), lambda j: (0, ca + j)), pl.BlockSpec((t, CW), lambda j: (0, cg + j)),
                  pl.BlockSpec((K_CONF, CW), lambda j: (0, j)), pl.BlockSpec((t, CW), lambda j: (0, j))],
        out_specs=(pl.BlockSpec((2, t, CW), lambda j: (0, 0, j)), pl.BlockSpec((K_CONF, CW), lambda j: (0, j)),
                   pl.BlockSpec((1, CW), lambda j: (0, j))),
        scratch_shapes=[pltpu.VMEM((t, CW), f32), pltpu.VMEM((8 * K_CONF, CW), f32), pltpu.VMEM((8, CW), f32)],
        sem=("arbitrary",), args=(proj, proj, conv_w, d_uconv), rider=rider)


def _ssd_conv_bwd_x(proj, conv_w, conv_b, d_xs, d_y, d_skip_row):
    t = proj.shape[0]
    pad = _pad_of(K_SSD)
    c0 = OFF_XBC // CW

    def body(x_ref, w_ref, b_ref, dxs_ref, dy_ref, dsk_ref, draw_ref, dw_ref, db_ref, dp_scr, dw_scr, db_scr):
        dw_scr[...] = jnp.zeros_like(dw_scr)
        db_scr[...] = jnp.zeros_like(db_scr)

        def first(r, carry):
            rows = pl.ds(pl.multiple_of(r * RC, RC), RC)
            win = _causal_win(x_ref, r, t, pad)
            pre = _conv_taps(win, w_ref, K_SSD, pad) + b_ref[...]
            dpre = (dxs_ref[rows, :] + dy_ref[rows, :] * dsk_ref[...]) * _dsilu(pre)
            dp_scr[rows, :] = dpre
            _dw_accumulate(dw_scr, dpre, win, K_SSD, pad)
            db_scr[...] += _rows8(dpre)
            return carry
        lax.fori_loop(0, t // RC, first, 0)

        def second(r, carry):
            rows = pl.ds(pl.multiple_of(r * RC, RC), RC)
            draw_ref[rows, :] = _corr_taps(_anti_win(dp_scr, r, t, pad), w_ref, K_SSD).astype(MX)
            return carry
        lax.fori_loop(0, t // RC, second, 0)
        _dw_finish(dw_scr, dw_ref, K_SSD)
        db_ref[...] = jnp.sum(db_scr[...], axis=0, keepdims=True)

    cb = pl.BlockSpec((t, CW), lambda j: (0, j))
    return pl.pallas_call(
        body, name="ssd_conv_bwd_x", grid=(D // CW,),
        out_shape=(jax.ShapeDtypeStruct((t, D), MX), jax.ShapeDtypeStruct((K_SSD, D), f32),
                   jax.ShapeDtypeStruct((1, D), f32)),
        in_specs=[pl.BlockSpec((t, CW), lambda j: (0, c0 + j)), pl.BlockSpec((K_SSD, CW), lambda j: (0, j)),
                  pl.BlockSpec((1, CW), lambda j: (0, j)), cb, cb, pl.BlockSpec((1, CW), lambda j: (0, j))],
        out_specs=(cb, pl.BlockSpec((K_SSD, CW), lambda j: (0, j)), pl.BlockSpec((1, CW), lambda j: (0, j))),
        scratch_shapes=[pltpu.VMEM((t, CW), f32), pltpu.VMEM((8 * K_SSD, CW), f32), pltpu.VMEM((8, CW), f32)],
        compiler_params=_cp("arbitrary"),
    )(proj, conv_w, conv_b, d_xs, d_y, d_skip_row)


def _ssd_conv_bwd_bc(proj, conv_w, conv_b, d_bc):
    t = proj.shape[0]
    pad = _pad_of(K_SSD)
    c0 = (OFF_XBC + D) // CW
    w0 = D // CW

    def body(x_ref, w_ref, b_ref, dbc_ref, draw_ref, dw_ref, db_ref, dp_scr, dw_scr, db_scr):
        dw_scr[...] = jnp.zeros_like(dw_scr)
        db_scr[...] = jnp.zeros_like(db_scr)

        def first(r, carry):
            rows = pl.ds(pl.multiple_of(r * RC, RC), RC)
            win = _causal_win(x_ref, r, t, pad)
            pre = _conv_taps(win, w_ref, K_SSD, pad) + b_ref[...]
            dpre = dbc_ref[0, rows, :] * _dsilu(pre)
            dp_scr[rows, :] = dpre
            _dw_accumulate(dw_scr, dpre, win, K_SSD, pad)
            db_scr[...] += _rows8(dpre)
            return carry
        lax.fori_loop(0, t // RC, first, 0)

        def second(r, carry):
            rows = pl.ds(pl.multiple_of(r * RC, RC), RC)
            draw_ref[rows, :] = _corr_taps(_anti_win(dp_scr, r, t, pad), w_ref, K_SSD).astype(MX)
            return carry
        lax.fori_loop(0, t // RC, second, 0)
        _dw_finish(dw_scr, dw_ref, K_SSD)
        db_ref[...] = jnp.sum(db_scr[...], axis=0, keepdims=True)

    return pl.pallas_call(
        body, name="ssd_conv_bwd_bc", grid=(2,),
        out_shape=(jax.ShapeDtypeStruct((t, 2 * CW), MX), jax.ShapeDtypeStruct((K_SSD, 2 * CW), f32),
                   jax.ShapeDtypeStruct((1, 2 * CW), f32)),
        in_specs=[pl.BlockSpec((t, CW), lambda j: (0, c0 + j)), pl.BlockSpec((K_SSD, CW), lambda j: (0, w0 + j)),
                  pl.BlockSpec((1, CW), lambda j: (0, w0 + j)), pl.BlockSpec((1, t, CW), lambda j: (j, 0, 0))],
        out_specs=(pl.BlockSpec((t, CW), lambda j: (0, j)), pl.BlockSpec((K_SSD, CW), lambda j: (0, j)),
                   pl.BlockSpec((1, CW), lambda j: (0, j))),
        scratch_shapes=[pltpu.VMEM((t, CW), f32), pltpu.VMEM((8 * K_SSD, CW), f32), pltpu.VMEM((8, CW), f32)],
        compiler_params=_cp("arbitrary"),
    )(proj, conv_w, conv_b, d_bc)


def _chunk_masks():
    ii = lax.broadcasted_iota(jnp.int32, (CHUNK, CHUNK), 0)
    jj = lax.broadcasted_iota(jnp.int32, (CHUNK, CHUNK), 1)
    return ii == jj, jj <= ii, jj >= ii


def _to_row(col, eye):
    return jnp.sum(jnp.where(eye, col, 0.0), axis=0, keepdims=True)


def _to_col(row, eye):
    return jnp.sum(jnp.where(eye, row, 0.0), axis=1, keepdims=True)


def _head_decay(dt_h, a_h, eye, tril):
    a_row = _to_row(dt_h * a_h, eye)
    cs = jnp.sum(jnp.where(tril, a_row, 0.0), axis=1, keepdims=True)
    cs_row = _to_row(cs, eye)
    decay = jnp.where(tril, jnp.exp(jnp.where(tril, cs - cs_row, 0.0)), 0.0)
    total = jnp.sum(a_row, axis=1, keepdims=True)
    return cs, decay, total


SCAN_UNROLL = 4


def _unrolled_loop(n, step, init):
    unroll = min(SCAN_UNROLL, n)
    assert n % unroll == 0

    def trip(i, carry):
        for u in range(unroll):
            carry = step(unroll * i + u, carry)
        return carry
    return lax.fori_loop(0, n // unroll, trip, init)


def _lane_pick(mat, lane, which):
    return jnp.sum(jnp.where(lane == which, mat, 0.0), axis=1, keepdims=True)


def _ssd_fwd(xbc_act, proj, dt_bias_row, a_log_row, rider=None):
    t = xbc_act.shape[0]
    nc = t // CHUNK
    cb, cc, cdt = D // LANES, (D + 2 * STATE_N) // LANES, OFF_DT // LANES

    def body(x_ref, b_ref, c_ref, dt_ref, dtb_ref, alog_ref, y_ref, st_ref):
        j = pl.program_id(0)
        eye, tril, _ = _chunk_masks()
        lane = lax.broadcasted_iota(jnp.int32, (1, LANES), 1)
        first = lane < HEAD_P
        a_row = -jnp.exp(alog_ref[...])
        a_heads = [jnp.sum(jnp.where(lane == 2 * j + h, a_row, 0.0), axis=1, keepdims=True) for h in range(2)]

        def chunk(c, hprev):
            rows = pl.ds(pl.multiple_of(c * CHUNK, CHUNK), CHUNK)
            xv, bm, cm = x_ref[rows, :], b_ref[rows, :], c_ref[rows, :]
            dt = _softplus(dt_ref[rows, :] + dtb_ref[...])
            st_ref[c] = hprev
            g = _mm_nt(cm, bm)
            ch = _mm(cm, hprev)
            dts = [_lane_pick(dt, lane, 2 * j + h) for h in range(2)]
            xdt = xv * jnp.where(first, dts[0], dts[1])
            ys, hs = [], []
            for h in range(2):
                cs, decay, total = _head_decay(dts[h], a_heads[h], eye, tril)
                y = _mm(g * decay, xdt) + jnp.exp(cs) * ch
                s = _mm_tn(bm * jnp.exp(total - cs), xdt)
                ys.append(y)
                hs.append(jnp.exp(total) * hprev + s)
            y_ref[rows, :] = jnp.where(first, ys[0], ys[1])
            return jnp.where(first, hs[0], hs[1])

        _unrolled_loop(nc, chunk, jnp.zeros((STATE_N, LANES), f32))

    blk = lambda f: pl.BlockSpec((t, LANES), f)
    return _call(
        body, name="ssd_fwd", grid=(D // LANES,),
        out_shape=(jax.ShapeDtypeStruct((t, D), f32), jax.ShapeDtypeStruct((nc, STATE_N, D), f32)),
        in_specs=[blk(lambda j: (0, j)), blk(lambda j: (0, cb + j // 4)), blk(lambda j: (0, cc + j // 4)),
                  blk(lambda j: (0, cdt)), _row(LANES), _row(LANES)],
        out_specs=(blk(lambda j: (0, j)), pl.BlockSpec((nc, STATE_N, LANES), lambda j: (0, 0, j))),
        sem=("arbitrary",), args=(xbc_act, xbc_act, xbc_act, proj, dt_bias_row, a_log_row), rider=rider)


def _ssd_bwd(xbc_act, proj, dt_bias_row, a_log_row, states, d_y, rider=None):
    t = xbc_act.shape[0]
    nc = t // CHUNK
    cb, cc, cdt = D // LANES, (D + 2 * STATE_N) // LANES, OFF_DT // LANES

    def body(x_ref, b_ref, c_ref, dt_ref, dtb_ref, alog_ref, st_ref, dy_ref, dx_ref, dbc_ref, ddt_ref, da_ref):
        grp, p = pl.program_id(0), pl.program_id(1)
        j = 4 * grp + p
        eye, tril, triu = _chunk_masks()
        lane = lax.broadcasted_iota(jnp.int32, (1, LANES), 1)
        first = lane < HEAD_P
        last_row = lax.broadcasted_iota(jnp.int32, (CHUNK, 1), 0) == CHUNK - 1
        a_row = -jnp.exp(alog_ref[...])
        a_heads = [jnp.sum(jnp.where(lane == 2 * j + h, a_row, 0.0), axis=1, keepdims=True) for h in range(2)]

        @pl.when(p == 0)
        def _():
            dbc_ref[...] = jnp.zeros_like(dbc_ref)

        @pl.when(j == 0)
        def _():
            ddt_ref[...] = jnp.zeros_like(ddt_ref)
            da_ref[...] = jnp.zeros_like(da_ref)

        def chunk(i, dh):
            c = nc - 1 - i
            rows = pl.ds(pl.multiple_of(c * CHUNK, CHUNK), CHUNK)
            xv, bm, cm = x_ref[rows, :], b_ref[rows, :], c_ref[rows, :]
            dtr = dt_ref[rows, :] + dtb_ref[...]
            dt = _softplus(dtr)
            hprev = st_ref[c]
            dy = dy_ref[rows, :]
            g = _mm_nt(cm, bm)
            dts = [_lane_pick(dt, lane, 2 * j + h) for h in range(2)]
            xdt = xv * jnp.where(first, dts[0], dts[1])
            dxs, dhs = [], []
            db_sum, dc_sum = None, None
            ddt_mat = jnp.zeros((CHUNK, LANES), f32)
            da_acc = jnp.zeros((1, LANES), f32)
            for h in range(2):
                mine = first if h == 0 else jnp.logical_not(first)
                cs, decay, total = _head_decay(dts[h], a_heads[h], eye, tril)
                e_cs, e_tot = jnp.exp(cs), jnp.exp(total)
                dec_s = jnp.exp(total - cs)
                dyh = jnp.where(mine, dy, 0.0)
                xdth = jnp.where(mine, xdt, 0.0)
                dhh = jnp.where(mine, dh, 0.0)
                hph = jnp.where(mine, hprev, 0.0)
                m = g * decay
                dm = _mm_nt(dyh, xdth)
                dg = dm * decay
                w = dm * m
                bdec = bm * dec_s
                dxdt = _mm_tn(m, dyh) + _mm(bdec, dhh)
                dc_off = _mm_nt(dyh, hph) * e_cs
                db_s = _mm_nt(xdth, dhh) * dec_s
                dc_h = _mm(dg, bm) + dc_off
                db_h = _mm_tn(dg, cm) + db_s
                r_s = jnp.sum(db_s * bm, axis=1, keepdims=True)
                dtotal = jnp.sum(r_s, axis=0, keepdims=True) + e_tot * jnp.sum(
                    jnp.sum(dhh * hph, axis=1, keepdims=True), axis=0, keepdims=True)
                dcs = (jnp.sum(w, axis=1, keepdims=True) - _to_col(jnp.sum(w, axis=0, keepdims=True), eye)
                       + jnp.sum(dc_off * cm, axis=1, keepdims=True) - r_s + jnp.where(last_row, dtotal, 0.0))
                da_col = jnp.sum(jnp.where(triu, _to_row(dcs, eye), 0.0), axis=1, keepdims=True)
                ddt = da_col * a_heads[h] + jnp.sum(jnp.where(mine, dxdt * xv, 0.0), axis=1, keepdims=True)
                ddt_mat = ddt_mat + jnp.where(lane == 2 * j + h, ddt, 0.0)
                da_acc = da_acc + jnp.where(lane == 2 * j + h, jnp.sum(da_col * dts[h], axis=0, keepdims=True), 0.0)
                dxs.append(dxdt * dts[h])
                dhs.append(e_tot * dhh + _mm_tn(cm * e_cs, dyh))
                db_sum = db_h if db_sum is None else db_sum + db_h
                dc_sum = dc_h if dc_sum is None else dc_sum + dc_h
            dx_ref[rows, :] = jnp.where(first, dxs[0], dxs[1])
            dbc_ref[0, rows, :] += db_sum
            dbc_ref[1, rows, :] += dc_sum
            ddt_ref[rows, :] += ddt_mat * jax.nn.sigmoid(dtr)
            da_ref[...] += da_acc * a_row
            return jnp.where(first, dhs[0], dhs[1])

        _unrolled_loop(nc, chunk, jnp.zeros((STATE_N, LANES), f32))

    blk = lambda f: pl.BlockSpec((t, LANES), f)
    return _call(
        body, name="ssd_bwd", grid=(2, 4),
        out_shape=(jax.ShapeDtypeStruct((t, D), f32), jax.ShapeDtypeStruct((2, t, 2 * STATE_N), f32),
                   jax.ShapeDtypeStruct((t, LANES), f32), jax.ShapeDtypeStruct((1, LANES), f32)),
        in_specs=[blk(lambda g, p: (0, 4 * g + p)), blk(lambda g, p: (0, cb + g)), blk(lambda g, p: (0, cc + g)),
                  blk(lambda g, p: (0, cdt)), _row(LANES), _row(LANES),
                  pl.BlockSpec((nc, STATE_N, LANES), lambda g, p: (0, 0, 4 * g + p)), blk(lambda g, p: (0, 4 * g + p))],
        out_specs=(blk(lambda g, p: (0, 4 * g + p)), pl.BlockSpec((2, t, LANES), lambda g, p: (0, 0, g)),
                   blk(lambda g, p: (0, 0)), _row(LANES)),
        sem=("arbitrary", "arbitrary"), args=(xbc_act, xbc_act, xbc_act, proj, dt_bias_row, a_log_row, states, d_y),
        rider=rider)


def _up_bwd(d_up, w_up, x1, mod, norm2_w, dx2, mix, w_out, rider=None):
    t = x1.shape[0]

    def body(dup_ref, wu_ref, x1_ref, mod_ref, nw_ref, dx2_ref, mix_ref, wo_ref,
             dx1_ref, dmix_ref, dys_ref, du_ref, st_ref):
        @pl.when(pl.program_id(0) == 0)
        def _():
            st_ref[...] = jnp.zeros_like(st_ref)

        nt = (((1,), (1,)), ((), ()))
        dh = None
        for k in range(4):
            lo = (k % 2) * UP_SHARD
            part = lax.dot_general(dup_ref[k // 2, :, lo:lo + UP_SHARD], wu_ref[k], nt, preferred_element_type=f32)
            dh = part if dh is None else dh + part
        x1 = x1_ref[...]
        rstd = lax.rsqrt(jnp.mean(x1 * x1, axis=-1, keepdims=True) + 1e-6)
        xh = x1 * rstd
        nw = nw_ref[...]
        sc = 1.0 + mod_ref[:, 4 * D:5 * D]
        st_ref[0:1, :] += jnp.sum(dh, axis=0, keepdims=True)
        st_ref[1:2, :] += jnp.sum(dh * xh * nw, axis=0, keepdims=True)
        st_ref[2:3, :] += jnp.sum(dh * sc * xh, axis=0, keepdims=True)
        dxh = dh * sc * nw
        dx1 = dx2_ref[...] + rstd * (dxh - xh * jnp.mean(dxh * xh, axis=-1, keepdims=True))
        dx1_ref[...] = dx1
        st_ref[3:4, :] += jnp.sum(dx1 * mix_ref[...], axis=0, keepdims=True)
        dmix = (mod_ref[:, 2 * D:3 * D] * dx1).astype(MX)
        dmix_ref[...] = dmix
        dys_ref[...] = lax.dot_general(dmix, wo_ref[0:D, :], nt, preferred_element_type=f32)
        du_ref[...] = lax.dot_general(dmix, wo_ref[D:2 * D, :], nt, preferred_element_type=f32)

    blk = pl.BlockSpec((TM, D), lambda i: (i, 0))
    return _call(
        body, name="up_bwd", grid=(t // TM,),
        out_shape=(jax.ShapeDtypeStruct((t, D), f32), jax.ShapeDtypeStruct((t, D), MX),
                   jax.ShapeDtypeStruct((t, D), f32), jax.ShapeDtypeStruct((t, D), f32),
                   jax.ShapeDtypeStruct((8, D), f32)),
        in_specs=[pl.BlockSpec((2, TM, D_FF), lambda i: (0, i, 0)), _resident((4, D, UP_SHARD)), blk, _row(6 * D), _row(),
                  blk, blk, _resident((2 * D, D))],
        out_specs=(blk, blk, blk, blk, pl.BlockSpec((8, D), lambda i: (0, 0))),
        sem=("arbitrary",), args=(d_up, w_up, x1, mod, norm2_w, dx2, mix, w_out), rider=rider)


def _ln_silu_bwd(d_u, u_conv, ln_w, ln_b):
    t = d_u.shape[0]

    def body(du_ref, u_ref, w_ref, b_ref, o_ref, st_ref):
        @pl.when(pl.program_id(0) == 0)
        def _():
            st_ref[...] = jnp.zeros_like(st_ref)

        u = u_ref[...]
        mu = jnp.mean(u, axis=-1, keepdims=True)
        uc = u - mu
        rstd = lax.rsqrt(jnp.mean(uc * uc, axis=-1, keepdims=True) + 1e-5)
        n = uc * rstd
        w = w_ref[...]
        dl = du_ref[...] * _dsilu(n * w + b_ref[...])
        st_ref[0:1, :] += jnp.sum(dl * n, axis=0, keepdims=True)
        st_ref[1:2, :] += jnp.sum(dl, axis=0, keepdims=True)
        dn = dl * w
        o_ref[...] = rstd * (dn - jnp.mean(dn, axis=-1, keepdims=True) - n * jnp.mean(dn * n, axis=-1, keepdims=True))

    blk = pl.BlockSpec((TM, D), lambda i: (i, 0))
    return pl.pallas_call(
        body, name="ln_silu_bwd", grid=(t // TM,),
        out_shape=(jax.ShapeDtypeStruct((t, D), f32), jax.ShapeDtypeStruct((8, D), f32)),
        in_specs=[blk, blk, _row(), _row()], out_specs=(blk, pl.BlockSpec((8, D), lambda i: (0, 0))),
        compiler_params=_cp("arbitrary"),
    )(d_u, u_conv, ln_w, ln_b)


def _ssd_gate_norm_bwd(d_out, y_scan, xbc_act, proj, d_skip_row, ssd_norm_w):
    t = d_out.shape[0]

    def body(do_ref, y_ref, xs_ref, z_ref, dsk_ref, nw_ref, dy_ref, dz_ref, st_ref):
        @pl.when(pl.program_id(0) == 0)
        def _():
            st_ref[...] = jnp.zeros_like(st_ref)

        xs = xs_ref[...]
        y = y_ref[...] + xs * dsk_ref[...]
        z = z_ref[...]
        s = _silu(z)
        yz = y * s
        rstd = lax.rsqrt(jnp.mean(yz * yz, axis=-1, keepdims=True) + 1e-6)
        n = yz * rstd
        do = do_ref[...]
        st_ref[0:1, :] += jnp.sum(do * n, axis=0, keepdims=True)
        dn = do * nw_ref[...]
        dyz = rstd * (dn - n * jnp.mean(dn * n, axis=-1, keepdims=True))
        dy = dyz * s
        dy_ref[...] = dy
        dz_ref[...] = (dyz * y * _dsilu(z)).astype(MX)
        st_ref[1:2, :] += jnp.sum(dy * xs, axis=0, keepdims=True)

    blk = pl.BlockSpec((TM, D), lambda i: (i, 0))
    return pl.pallas_call(
        body, name="ssd_gate_norm_bwd", grid=(t // TM,),
        out_shape=(jax.ShapeDtypeStruct((t, D), f32), jax.ShapeDtypeStruct((t, D), MX), jax.ShapeDtypeStruct((8, D), f32)),
        in_specs=[blk, blk, blk, blk, _row(), _row()], out_specs=(blk, blk, pl.BlockSpec((8, D), lambda i: (0, 0))),
        compiler_params=_cp("arbitrary"),
    )(d_out, y_scan, xbc_act, proj, d_skip_row, ssd_norm_w)


def _inproj_bwd(d_z, d_xraw, d_bcraw, d_conf, d_dt, w_pack, x, mod, norm1_w, dx1, after=None):
    t = x.shape[0]
    extra = [] if after is None else [after]

    def body(dz_ref, dx_ref, dbc_ref, dcf_ref, ddt_ref, w_ref, x_ref, mod_ref, nw_ref, dx1_ref, *rest):
        gx_ref, st_ref = rest[-2:]
        @pl.when(pl.program_id(0) == 0)
        def _():
            st_ref[...] = jnp.zeros_like(st_ref)

        nt = (((1,), (1,)), ((), ()))
        dot = lambda a, lo, hi: lax.dot_general(a, w_ref[:, lo:hi], nt, preferred_element_type=f32)
        dh = dot(dz_ref[...], OFF_Z, OFF_Z + D)
        dh = dh + dot(dx_ref[...], OFF_XBC, OFF_XBC + D)
        dh = dh + dot(dbc_ref[...], OFF_XBC + D, OFF_XBC + D_XBC)
        dh = dh + dot(dcf_ref[0], OFF_CA, OFF_CA + D)
        dh = dh + dot(dcf_ref[1], OFF_CG, OFF_CG + D)
        dh = dh + dot(ddt_ref[...].astype(MX), OFF_DT, OFF_DT + LANES)
        st_ref[3:4, 0:LANES] += jnp.sum(ddt_ref[...], axis=0, keepdims=True)
        xv = x_ref[...]
        rstd = lax.rsqrt(jnp.mean(xv * xv, axis=-1, keepdims=True) + 1e-6)
        xh = xv * rstd
        nw = nw_ref[...]
        sc = 1.0 + mod_ref[:, D:2 * D]
        st_ref[0:1, :] += jnp.sum(dh, axis=0, keepdims=True)
        st_ref[1:2, :] += jnp.sum(dh * xh * nw, axis=0, keepdims=True)
        st_ref[2:3, :] += jnp.sum(dh * sc * xh, axis=0, keepdims=True)
        dxh = dh * sc * nw
        gx_ref[...] = dx1_ref[...] + rstd * (dxh - xh * jnp.mean(dxh * xh, axis=-1, keepdims=True))

    blk = pl.BlockSpec((TM, D), lambda i: (i, 0))
    return _call(
        body, name="inproj_bwd", grid=(t // TM,),
        out_shape=(jax.ShapeDtypeStruct((t, D), f32), jax.ShapeDtypeStruct((8, D), f32)),
        in_specs=[blk, blk, pl.BlockSpec((TM, 2 * CW), lambda i: (i, 0)), pl.BlockSpec((2, TM, D), lambda i: (0, i, 0)),
                  pl.BlockSpec((TM, LANES), lambda i: (i, 0)), _resident((D, W_PACK)), blk, _row(6 * D), _row(), blk]
        + [ANY] * len(extra),
        out_specs=(blk, pl.BlockSpec((8, D), lambda i: (0, 0))),
        sem=("arbitrary",), args=(d_z, d_xraw, d_bcraw, d_conf, d_dt, w_pack, x, mod, norm1_w, dx1, *extra))[0]


def _wgrad(a, d, name, bn=256, transposed=True):
    k, t = a.shape if transposed else a.shape[::-1]
    n = d.shape[1]
    out_dtype = MX
    contract = (((1,), (0,)), ((), ())) if transposed else (((0,), (0,)), ((), ()))

    def body(a_ref, d_ref, o_ref):
        o_ref[...] = lax.dot_general(a_ref[...], d_ref[...].astype(MX), contract, preferred_element_type=f32).astype(out_dtype)

    return pl.pallas_call(
        body, name=name, grid=(n // bn,), out_shape=jax.ShapeDtypeStruct((k, n), out_dtype),
        in_specs=[_resident(a.shape), pl.BlockSpec((t, bn), lambda j: (0, j))],
        out_specs=pl.BlockSpec((k, bn), lambda j: (0, j)), compiler_params=_cp("arbitrary"),
    )(a, d)


def _wgrad_stacked(at, d, name, bn):
    out_dtype = MX
    k, t = at.shape
    s, _, n = d.shape
    nb = n // bn

    def body(a_ref, d_ref, o_ref):
        o_ref[0] = jnp.dot(a_ref[...], d_ref[0], preferred_element_type=f32).astype(out_dtype)

    return pl.pallas_call(
        body, name=name, grid=(s, nb), out_shape=jax.ShapeDtypeStruct((s * nb, k, bn), out_dtype),
        in_specs=[_resident((k, t)), pl.BlockSpec((1, t, bn), lambda i, j: (i, 0, j))],
        out_specs=pl.BlockSpec((1, k, bn), lambda i, j: (i * nb + j, 0, 0)), compiler_params=_cp("arbitrary", "arbitrary"),
    )(at, d)


def _pad_row(v, width=LANES):
    return jnp.pad(v.reshape(1, -1), ((0, 0), (0, width - v.size)))


def _quarters(a):
    return a.reshape(4, 2, a.shape[0] // 8, a.shape[1])


def _local_step(x, mod, target, w_pack, late, small, reducer=None):
    dtb_row, alog_row = _pad_row(small["dt_bias"]), _pad_row(small["a_log"])
    dskip_row = jnp.repeat(small["d_skip"].reshape(-1), HEAD_P).reshape(1, D)

    red = reducer

    def hosted(host, args, swap=None, scatter=None, gather=None, sums=()):
        if red is None:
            return host(*args)[0]
        riders = ([red.scatter(scatter)] if scatter else []) + ([red.swap(*swap)] if swap else [])
        riders += [_SwapSumsRider([red.sums[n] for n in sums])] if sums else []
        riders += [_GatherRider([gather[0]], *gather[1:])] if gather is not None else []
        both = _Riders(riders)
        outs, extra = host(*args, rider=both)
        extra = both.split(extra)
        if scatter:
            red.scattered(scatter, extra.pop(0))
        if swap:
            red.swapped(swap[0], extra.pop(0))
        if sums:
            red.others.update(zip(sums, extra.pop(0)))
        return (outs, extra[0][0]) if gather is not None else outs

    w_out, w_up, w_down = late
    if red is None:
        proj, h_t = hosted(_ln_inproj, (x, mod, small["norm1_w"], w_pack))
        xbc_act, = hosted(_ssd_conv_fwd, (proj, small["ssd_conv_w"], small["ssd_conv_b"]))
        y_scan, states = hosted(_ssd_fwd, (xbc_act, proj, dtb_row, alog_row))
        u_conv, = hosted(_glu_conv_fwd, (proj, small["conf_conv_w"], small["conf_conv_b"]))
    else:
        (proj, h_t), w_out = hosted(_ln_inproj, (x, mod, small["norm1_w"], w_pack), gather=(w_out,))
        (xbc_act,), w_up = hosted(_ssd_conv_fwd, (proj, small["ssd_conv_w"], small["ssd_conv_b"]), gather=(w_up, 0, UP_EARLY_ROWS))
        (y_scan, states), w_up = hosted(_ssd_fwd, (xbc_act, proj, dtb_row, alog_row), gather=(w_up, UP_EARLY_ROWS, None))
        (u_conv,), w_down = hosted(_glu_conv_fwd, (proj, small["conf_conv_w"], small["conf_conv_b"]), gather=(w_down,))
        w_out, w_up, w_down = w_out.reshape(2 * D, D), w_up.reshape(4, D, UP_SHARD), w_down.reshape(D_FF, D)
    y_ssd = _ssd_gate_norm(y_scan, xbc_act, proj, dskip_row, small["ssd_norm_w"])
    u = _ln_silu(u_conv, small["conf_ln_w"], small["conf_ln_b"])
    mix, x1, h2_t, up = _outproj_ln2_up(y_ssd, u, w_out, x, mod, small["norm2_w"], w_up)
    act, = _ffn_conv_fwd(up, small["ffn_conv_w"], small["ffn_conv_b"])[0]
    dx2, d_ffn, d_act, st_down = _down_loss(act, w_down, x1, mod, small["final_norm_w"], target)

    g_down = _quarters(_wgrad(act, d_ffn, "wgrad_down", transposed=False))
    d_up, dw_ffn, db_ffn = hosted(_ffn_conv_bwd, (up, small["ffn_conv_w"], small["ffn_conv_b"], d_act), swap=("w_down", g_down))
    g_up = _wgrad_stacked(h2_t, d_up, "wgrad_up", D_FF // 2).reshape(4, 2, D // 2, UP_SHARD)
    dx1, d_mix, d_yssd, d_u, st_up = hosted(_up_bwd, (d_up, w_up, x1, mod, small["norm2_w"], dx2, mix, w_out),
                                            scatter="w_down", swap=("w_up", g_up))
    g_out = _quarters(jnp.concatenate([_wgrad(y_ssd, d_mix, "wgrad_out_y", transposed=False),
                                       _wgrad(u, d_mix, "wgrad_out_u", transposed=False)], axis=0))
    d_uconv, st_ln = _ln_silu_bwd(d_u, u_conv, small["conf_ln_w"], small["conf_ln_b"])
    d_conf, dw_conf, db_conf = hosted(_glu_conv_bwd, (proj, small["conf_conv_w"], d_uconv), scatter="w_up",
                                      swap=("w_out", g_out))
    d_y, d_z, st_gn = _ssd_gate_norm_bwd(d_yssd, y_scan, xbc_act, proj, dskip_row, small["ssd_norm_w"])
    d_xs, d_bc, d_dt, d_alog = hosted(_ssd_bwd, (xbc_act, proj, dtb_row, alog_row, states, d_y), scatter="w_out")
    d_xraw, dw_sx, db_sx = _ssd_conv_bwd_x(proj, small["ssd_conv_w"], small["ssd_conv_b"], d_xs, d_y, dskip_row)
    d_bcraw, dw_sbc, db_sbc = _ssd_conv_bwd_bc(proj, small["ssd_conv_w"], small["ssd_conv_b"], d_bc)
    g_in = _unpack_g_in(dict(
        z=_wgrad(h_t, d_z, "wgrad_in_z"), x=_wgrad(h_t, d_xraw, "wgrad_in_x"), bc=_wgrad(h_t, d_bcraw, "wgrad_in_bc"),
        conf=_wgrad_stacked(h_t, d_conf, "wgrad_in_conf", D), dt=_wgrad(h_t, d_dt, "wgrad_in_dt", bn=LANES)))
    g_in = g_in.reshape(4, 2, D // 2, W_IN_SHARD_PAD)
    args = (d_z, d_xraw, d_bcraw, d_conf, d_dt, w_pack, x, mod, small["norm1_w"], dx1)
    if red is None:
        grad_x, st_in = _inproj_bwd(*args)
    else:
        done = ("w_out", "w_up", "w_down")
        both = _Riders([red.swap("w_in", g_in), _SwapSumsRider([red.sums[n] for n in done])])
        handles, token = _split_start(both, "swap_start_w_in")
        grad_x, st_in = _inproj_bwd(*args, after=token)
        thru, outs = _split_wait(both, "swap_wait_w_in", handles, st_in)
        red.grads["w_in"] = thru[0]
        red.sums.update(zip(done, thru[1:]))
        got, others = both.split(outs)
        red.swapped("w_in", got)
        red.others.update(zip(done, others))

    gsmall = _pack_small_grads(st_in, st_up, st_down, st_ln, st_gn, d_alog, dw_sx, dw_sbc, db_sx, db_sbc, dw_conf, db_conf,
                               dw_ffn, db_ffn)
    gbig = None if reducer is not None else dict(w_in=g_in, w_out=g_out, w_up=g_up, w_down=g_down)
    return st_down[2, 0], grad_x, gbig, gsmall


VECTORS = ("ada_b", "norm1_w", "ssd_conv_b", "dt_bias", "a_log", "d_skip", "ssd_norm_w", "conf_conv_b", "conf_ln_w",
           "conf_ln_b", "norm2_w", "ffn_conv_b", "final_norm_w")
VECTOR_SIZES = (6 * D, D, D_XBC, HEADS, HEADS, HEADS, D, D, D, D, D, 2 * D_FF, D)
CONVS = {"ssd_conv_w": (K_SSD, D_XBC), "conf_conv_w": (K_CONF, D), "ffn_conv_w": (K_FFN, 2 * D_FF)}


def _pack_rows(items):
    n = -(-sum(w for _, w in items) // (8 * LANES)) * LANES
    while True:
        fill, place = [0] * 8, {}
        for key, w in sorted(items, key=lambda kv: -kv[1]):
            rows = [r for r in range(8) if fill[r] + w <= n]
            if not rows:
                break
            place[key] = (rows[0], fill[rows[0]])
            fill[rows[0]] += w
        if len(place) == len(items):
            return n, place
        n += LANES


FRONT_N, FRONT = _pack_rows([("c", D)] + [((nm, j), cols // 4) for nm, (taps, cols) in CONVS.items() for j in range(taps)])
BACK_N, BACK = _pack_rows([(nm, -(-sz // LANES) * LANES) for nm, sz in zip(VECTORS, VECTOR_SIZES)]
                          + [((nm, j), cols) for nm, (taps, cols) in CONVS.items() for j in range(taps)] + [("loss", LANES)])
_VM = pltpu.CompilerParams(vmem_limit_bytes=VMEM_LIMIT)


def _pack_front(c, shards):
    def body(c_ref, *refs):
        o_ref = refs[-1]
        o_ref[...] = jnp.zeros_like(o_ref)
        r, o = FRONT["c"]
        o_ref[r:r + 1, o:o + D] = c_ref[...]
        for ref, (nm, (taps, cols)) in zip(refs, CONVS.items()):
            for j in range(taps):
                r, o = FRONT[(nm, j)]
                o_ref[r:r + 1, o:o + cols // 4] = ref[0, j:j + 1, :]

    return pl.pallas_call(body, name="pack_front", out_shape=jax.ShapeDtypeStruct((8, FRONT_N), f32),
                          compiler_params=_VM)(c, *shards)


def _unpack_front(got):
    def body(g_ref, c_ref, *outs):
        r, o = FRONT["c"]
        for d in range(8):
            c_ref[d:d + 1, :] = g_ref[8 * d + r:8 * d + r + 1, o:o + D]
        for ref, (nm, (taps, cols)) in zip(outs, CONVS.items()):
            cw = cols // 4
            for j in range(taps):
                r, o = FRONT[(nm, j)]
                for k in range(4):
                    ref[j:j + 1, k * cw:(k + 1) * cw] = g_ref[16 * k + r:16 * k + r + 1, o:o + cw]

    return pl.pallas_call(
        body, name="unpack_front", compiler_params=_VM,
        out_shape=(jax.ShapeDtypeStruct((8, D), f32),) + tuple(jax.ShapeDtypeStruct(tc, f32) for tc in CONVS.values()),
    )(got)


def _pack_small_grads(st_in, st_up, st_down, st_ln, st_gn, d_alog, dw_sx, dw_sbc, db_sx, db_sbc, dw_conf, db_conf, dw_ffn,
                      db_ffn):
    def body(in_ref, up_ref, dn_ref, ln_ref, gn_ref, al_ref, wx_ref, wbc_ref, bx_ref, bbc_ref, wc_ref, bc_ref, wf_ref, bf_ref,
             o_ref):
        def put(key, val, shift=0):
            r, o = BACK[key]
            o_ref[r:r + 1, o + shift:o + shift + val.shape[1]] = val

        o_ref[...] = jnp.zeros_like(o_ref)
        for i, piece in enumerate((in_ref[0:1, :], in_ref[1:2, :], up_ref[3:4, :], up_ref[0:1, :], up_ref[1:2, :],
                                   dn_ref[1:2, :])):
            put("ada_b", piece, i * D)
        put("norm1_w", in_ref[2:3, :])
        put("ssd_conv_b", bx_ref[...])
        put("ssd_conv_b", bbc_ref[...], D)
        put("dt_bias", in_ref[3:4, 0:LANES])
        put("a_log", al_ref[...])
        lane = lax.broadcasted_iota(jnp.int32, (1, LANES), 1)
        col = lax.broadcasted_iota(jnp.int32, (1, D), 1)
        per_col = gn_ref[1:2, :]
        d_skip = jnp.zeros((1, LANES), f32)
        for h in range(HEADS):
            in_head = jnp.logical_and(col >= h * HEAD_P, col < (h + 1) * HEAD_P)
            s = jnp.sum(jnp.where(in_head, per_col, 0.0), axis=1, keepdims=True)
            d_skip = d_skip + jnp.where(lane == h, s, 0.0)
        put("d_skip", d_skip)
        put("ssd_norm_w", gn_ref[0:1, :])
        put("conf_conv_b", bc_ref[...])
        put("conf_ln_w", ln_ref[0:1, :])
        put("conf_ln_b", ln_ref[1:2, :])
        put("norm2_w", up_ref[2:3, :])
        put("ffn_conv_b", bf_ref[0])
        put("ffn_conv_b", bf_ref[1], D_FF)
        put("final_norm_w", dn_ref[0:1, :])
        put("loss", dn_ref[2:3, 0:LANES])
        for j in range(K_SSD):
            put(("ssd_conv_w", j), wx_ref[j:j + 1, :])
            put(("ssd_conv_w", j), wbc_ref[j:j + 1, :], D)
        for j in range(K_CONF):
            put(("conf_conv_w", j), wc_ref[j:j + 1, :])
        for j in range(K_FFN):
            put(("ffn_conv_w", j), wf_ref[0, j:j + 1, :])
            put(("ffn_conv_w", j), wf_ref[1, j:j + 1, :], D_FF)

    return pl.pallas_call(body, name="pack_small_grads", out_shape=jax.ShapeDtypeStruct((8, BACK_N), f32), compiler_params=_VM)(
        st_in, st_up, st_down, st_ln, st_gn, d_alog, dw_sx, dw_sbc, db_sx, db_sbc, dw_conf, db_conf, dw_ffn, db_ffn)


def _small_adamw(got, chip, w, m, v):
    names = VECTORS + tuple(CONVS)
    n_par = len(names)

    def body(chip_ref, g_ref, *refs):
        ins, outs = refs[:3 * n_par], refs[3 * n_par:]
        dm_ref, loss_ref, outs = outs[0], outs[1], outs[2:]
        chip_id = chip_ref[0]

        def summed(key, width):
            r, o = BACK[key]
            s = g_ref[r:r + 1, o:o + width]
            for d in range(1, 8):
                s = s + g_ref[8 * d + r:8 * d + r + 1, o:o + width]
            return s

        def mine(full, cw):
            out = full[:, 0:cw]
            for k in range(1, 4):
                out = jnp.where(chip_id == k, full[:, k * cw:(k + 1) * cw], out)
            return out

        r, o = BACK["ada_b"]
        for d in range(8):
            dm_ref[d:d + 1, :] = mine(g_ref[8 * d + r:8 * d + r + 1, o:o + 6 * D], 6 * D // 4)
        loss_ref[...] = summed("loss", LANES)
        for i, (nm, size) in enumerate(zip(VECTORS, VECTOR_SIZES)):
            g = summed(nm, -(-size // LANES) * LANES)[:, 0:size]
            res = _adam_math(ins[3 * i][...], g, ins[3 * i + 1][...], ins[3 * i + 2][...])
            for ref, val in zip(outs[4 * i:4 * i + 4], (g,) + res):
                ref[...] = val
        for i, (nm, (taps, cols)) in enumerate(CONVS.items(), start=len(VECTORS)):
            for j in range(taps):
                g = mine(summed((nm, j), cols), cols // 4)
                res = _adam_math(ins[3 * i][0, j:j + 1, :], g, ins[3 * i + 1][0, j:j + 1, :], ins[3 * i + 2][0, j:j + 1, :])
                for ref, val in zip(outs[4 * i:4 * i + 4], (g,) + res):
                    ref[0, j:j + 1, :] = val

    params = [a[nm] for nm in names for a in (w, m, v)]
    whole = lambda s: pl.BlockSpec(s, lambda i, chip, nd=len(s): (0,) * nd)
    out_shape = [jax.ShapeDtypeStruct((8, 6 * D // 4), f32), jax.ShapeDtypeStruct((1, LANES), f32)]
    out_shape += [jax.ShapeDtypeStruct(w[nm].shape, f32) for nm in names for _ in range(4)]
    outs = pl.pallas_call(
        body, name="small_adamw", out_shape=tuple(out_shape), compiler_params=_VM,
        grid_spec=pltpu.PrefetchScalarGridSpec(
            num_scalar_prefetch=1, grid=(1,), in_specs=[whole(got.shape)] + [whole(p.shape) for p in params],
            out_specs=tuple(whole(s.shape) for s in out_shape)),
    )(_scalar(chip), got, *params)
    return outs[0], outs[1][0, 0], {nm: outs[2 + 4 * i:6 + 4 * i] for i, nm in enumerate(names)}


W_IN_COLS = 4624
W_IN_SHARD = W_IN_COLS // 4
W_IN_SHARD_PAD = 1280
_SEGMENTS = ((0, 1024, OFF_Z), (1024, 2560, OFF_XBC), (2560, 2576, OFF_DT), (2576, 3600, OFF_CA), (3600, 4624, OFF_CG))


def _in_pieces(bounds=()):
    out = []
    for k in range(4):
        s0, s1 = k * W_IN_SHARD, (k + 1) * W_IN_SHARD
        for lo, hi, off in _SEGMENTS:
            a, b = max(lo, s0), min(hi, s1)
            while a < b:
                p = off + a - lo
                e = min([b - a] + [c - p for c in bounds if c > p])
                out.append((k, a - s0, p, e))
                a += e
    return out


def _pack_w_in(shards):
    pieces = _in_pieces()

    def body(s_ref, o_ref):
        o_ref[:, OFF_DT:W_PACK] = jnp.zeros((TM, W_PACK - OFF_DT), MX)
        for k, c, p, n in pieces:
            o_ref[:, p:p + n] = s_ref[k, :, c:c + n]

    return pl.pallas_call(
        body, name="pack_w_in", grid=(D // TM,), out_shape=jax.ShapeDtypeStruct((D, W_PACK), MX),
        in_specs=[pl.BlockSpec((4, TM, W_IN_SHARD_PAD), lambda i: (0, i, 0))],
        out_specs=pl.BlockSpec((TM, W_PACK), lambda i: (i, 0)), compiler_params=_cp("arbitrary"),
    )(shards)


def _unpack_g_in(g):
    srcs = ((OFF_Z, D), (OFF_XBC, D), (OFF_XBC + D, 2 * CW), (OFF_CA, D), (OFF_CG, D), (OFF_DT, LANES))
    pieces = _in_pieces(tuple(o for o, _ in srcs) + tuple(o + n for o, n in srcs))

    def body(z_ref, x_ref, bc_ref, cf_ref, dt_ref, o_ref):
        read = (lambda lo, hi: z_ref[:, lo:hi], lambda lo, hi: x_ref[:, lo:hi], lambda lo, hi: bc_ref[:, lo:hi],
                lambda lo, hi: cf_ref[0, :, lo:hi], lambda lo, hi: cf_ref[1, :, lo:hi], lambda lo, hi: dt_ref[:, lo:hi])
        o_ref[:, :, W_IN_SHARD - 4:W_IN_SHARD_PAD] = jnp.zeros((4, TM, W_IN_SHARD_PAD - W_IN_SHARD + 4), MX)
        for k, c, p, n in pieces:
            i = [q for q, (o, w) in enumerate(srcs) if o <= p < o + w][0]
            o_ref[k, :, c:c + n] = read[i](p - srcs[i][0], p - srcs[i][0] + n)

    blk = lambda w: pl.BlockSpec((TM, w), lambda i: (i, 0))
    return pl.pallas_call(
        body, name="unpack_g_in", grid=(D // TM,), out_shape=jax.ShapeDtypeStruct((4, D, W_IN_SHARD_PAD), MX),
        in_specs=[blk(D), blk(D), blk(2 * CW), pl.BlockSpec((2, TM, D), lambda i: (0, i, 0)), blk(LANES)],
        out_specs=pl.BlockSpec((4, TM, W_IN_SHARD_PAD), lambda i: (0, i, 0)), compiler_params=_cp("arbitrary"),
    )(g["z"], g["x"], g["bc"], g["conf"], g["dt"])


def _scalar(v):
    return jnp.reshape(v, (1,)).astype(jnp.int32)


def _cast_into_slot(w, width, chip):
    r, c = w.shape
    h = r // 2
    tm = _row_tile(h)
    nj = h // tm

    def body(chip_ref, w_ref, o_ref):
        v = w_ref[...].astype(MX)
        o_ref[0, 0] = v if width == c else jnp.concatenate([v, jnp.zeros((tm, width - c), MX)], axis=1)

    return pl.pallas_call(
        body, name=f"cast_into_slot_{r}x{c}", out_shape=jax.ShapeDtypeStruct((4, 2, h, width), MX),
        grid_spec=pltpu.PrefetchScalarGridSpec(
            num_scalar_prefetch=1, grid=(2, nj),
            in_specs=[pl.BlockSpec((tm, c), lambda i, j, chip: (i * nj + j, 0))],
            out_specs=pl.BlockSpec((1, 1, tm, width), lambda i, j, chip: (chip[0], i, j, 0))),
        compiler_params=_cp("arbitrary", "arbitrary"),
    )(_scalar(chip), w)


def _columns_first(w):
    return jnp.transpose(w, (2, 0, 1))


def _cast_into_slot_w_in(w_t, chip):
    h = D // 2
    nj = h // TM
    pad = W_IN_SHARD_PAD - W_IN_SHARD

    def body(chip_ref, w_ref, o_ref):
        cols = jnp.concatenate([w_ref[:, 0, :], jnp.zeros((pad, TM), f32)], axis=0)
        o_ref[0, 0] = cols.T.astype(MX)

    return pl.pallas_call(
        body, name="cast_into_slot_w_in", out_shape=jax.ShapeDtypeStruct((4, 2, h, W_IN_SHARD_PAD), MX),
        grid_spec=pltpu.PrefetchScalarGridSpec(
            num_scalar_prefetch=1, grid=(2, nj),
            in_specs=[pl.BlockSpec((W_IN_SHARD, 1, TM), lambda i, j, chip: (0, 0, i * nj + j))],
            out_specs=pl.BlockSpec((1, 1, TM, W_IN_SHARD_PAD), lambda i, j, chip: (chip[0], i, j, 0))),
        compiler_params=_cp("arbitrary", "arbitrary"),
    )(_scalar(chip), w_t)


def _adamw_w_in(w_t, mine, other, m_t, v_t, core):
    h = D // 2
    nj = h // TM

    def body(core_ref, w_ref, a_ref, b_ref, m_ref, v_ref, g_ref, d_ref, nm_ref, nv_ref):
        g = jnp.where(pl.program_id(0) == core_ref[0], a_ref[...], b_ref[...]).T[0:W_IN_SHARD, :]
        g_ref[:, 0, :] = g
        d_ref[:, 0, :], nm_ref[:, 0, :], nv_ref[:, 0, :] = _adam_math(w_ref[:, 0, :], g, m_ref[:, 0, :], v_ref[:, 0, :])

    blk = pl.BlockSpec((W_IN_SHARD, 1, TM), lambda i, j, core: (0, 0, i * nj + j))
    gblk = pl.BlockSpec((TM, W_IN_SHARD_PAD), lambda i, j, core: (j, 0))
    return pl.pallas_call(
        body, name="adamw_w_in", out_shape=tuple([jax.ShapeDtypeStruct((W_IN_SHARD, 1, D), f32)] * 4),
        grid_spec=pltpu.PrefetchScalarGridSpec(
            num_scalar_prefetch=1, grid=(2, nj), in_specs=[blk, gblk, gblk, blk, blk], out_specs=(blk,) * 4),
        compiler_params=_cp("arbitrary", "arbitrary"),
    )(_scalar(core), w_t, mine, other, m_t, v_t)


ANY = pl.BlockSpec(memory_space=pl.ANY)


def _place():
    x, y, c = lax.axis_index("x"), lax.axis_index("y"), lax.axis_index("c")
    return x, y, c, [(1 - x, y), (x, 1 - y), (1 - x, 1 - y)]


def _gather_rows(block, rider=None):
    m_per, n = block.shape
    ri, ro = (len(rider.inputs), len(rider.out_shape)) if rider is not None else (0, 0)

    def body(x_ref, *refs):
        r_in, out_ref, r_out = refs[:ri], refs[ri], refs[ri + 1:ri + 1 + ro]
        send_sems, recv_sems, local_sem, *r_scr = refs[ri + 1 + ro:]
        x, y, c, chips = _place()
        me, sibling = (x, y, c), (x, y, 1 - c)

        def rows(px, py, pc):
            return out_ref.at[pl.ds((4 * px + 2 * py + pc) * m_per, m_per), :]

        def copy(k, blk, to, src=None):
            return pltpu.make_async_remote_copy(
                src_ref=rows(*blk) if src is None else src, dst_ref=rows(*blk), send_sem=send_sems.at[k],
                recv_sem=recv_sems.at[k], device_id=to, device_id_type=MESH)

        mine = pltpu.make_async_copy(x_ref, rows(*me), local_sem)
        mine.start()
        first = [copy(0, me, sibling, src=x_ref)]
        first += [copy(1 + j, me, (*chip, c), src=x_ref) for j, chip in enumerate(chips)]
        for cp in first:
            cp.start()
        if rider is not None:
            rider.start(r_in, r_out, r_scr)
        passed = [copy(4 + j, (*chip, c), sibling) for j, chip in enumerate(chips)]
        for j, chip in enumerate(chips):
            copy(1 + j, (*chip, c), me).wait_recv()
            passed[j].start()
        copy(0, sibling, me).wait_recv()
        for j, chip in enumerate(chips):
            copy(4 + j, (*chip, 1 - c), me).wait_recv()
        for cp in first + passed:
            cp.wait_send()
        mine.wait()
        if rider is not None:
            rider.finish(r_in, r_out, r_scr)

    vmem = pl.BlockSpec(memory_space=pltpu.VMEM)
    gathered = jax.ShapeDtypeStruct((8 * m_per, n), block.dtype)
    if rider is None:
        return pl.pallas_call(
            body, name=f"gather_rows_{m_per}x{n}", out_shape=gathered, in_specs=[vmem], out_specs=vmem,
            scratch_shapes=[pltpu.SemaphoreType.DMA((7,)), pltpu.SemaphoreType.DMA((7,)), pltpu.SemaphoreType.DMA],
            compiler_params=_VM)(block)
    outs = pl.pallas_call(
        body, name=f"gather_rows_{m_per}x{n}", out_shape=(gathered,) + tuple(rider.out_shape),
        in_specs=[vmem] + [ANY] * ri, out_specs=(vmem,) + (ANY,) * ro,
        input_output_aliases={1 + i: 1 + j for i, j in rider.aliases.items()},
        scratch_shapes=[pltpu.SemaphoreType.DMA((7,)), pltpu.SemaphoreType.DMA((7,)), pltpu.SemaphoreType.DMA] + list(rider.scratch),
        compiler_params=_VM)(block, *rider.inputs)
    return outs[0], tuple(outs[1:])


class _GatherRider:
    def __init__(self, slots, row0=0, nrows=None):
        n = len(slots)
        self.n = n
        self.rows = (row0, slots[0].shape[2] - row0 if nrows is None else nrows)
        self.inputs = list(slots)
        self.out_shape = [jax.ShapeDtypeStruct(s.shape, s.dtype) for s in slots]
        self.scratch = [pltpu.SemaphoreType.DMA((n, 6)), pltpu.SemaphoreType.DMA((n, 6))]
        self.aliases = {a: a for a in range(n)}

    def _copy(self, outs, sems, a, j, k, half, to):
        dst = outs[a].at[k, half, pl.ds(*self.rows)]
        return pltpu.make_async_remote_copy(src_ref=dst, dst_ref=dst, send_sem=sems[0].at[a, j], recv_sem=sems[1].at[a, j],
                                            device_id=to, device_id_type=MESH)

    def _first(self, outs, sems):
        x, y, c, chips = _place()
        return [self._copy(outs, sems, a, j, 2 * x + y, c, (*chip, c)) for a in range(self.n) for j, chip in enumerate(chips)]

    def start(self, ins, outs, sems):
        for cp in self._first(outs, sems):
            cp.start()

    def finish(self, ins, outs, sems):
        x, y, c, chips = _place()
        passed = []
        for a in range(self.n):
            for j, (px, py) in enumerate(chips):
                self._copy(outs, sems, a, j, 2 * px + py, c, (x, y, c)).wait_recv()
                fwd = self._copy(outs, sems, a, 3 + j, 2 * px + py, c, (x, y, 1 - c))
                fwd.start()
                passed.append(fwd)
        for a in range(self.n):
            for j, (px, py) in enumerate(chips):
                self._copy(outs, sems, a, 3 + j, 2 * px + py, 1 - c, (x, y, c)).wait_recv()
        for cp in self._first(outs, sems) + passed:
            cp.wait_send()


class _ScatterRider:
    def __init__(self, parts):
        n = len(parts)
        self.n = n
        self.inputs = list(parts)
        self.out_shape = [jax.ShapeDtypeStruct((3,) + p.shape[1:], p.dtype) for p in parts]
        self.scratch = [pltpu.SemaphoreType.DMA((3 * n,)), pltpu.SemaphoreType.DMA((3 * n,))]
        self.aliases = {}

    def _copies(self, ins, outs, sems):
        x, y, c, chips = _place()
        return [pltpu.make_async_remote_copy(
            src_ref=ins[a].at[2 * px + py], dst_ref=outs[a].at[j], send_sem=sems[0].at[3 * a + j],
            recv_sem=sems[1].at[3 * a + j], device_id=(px, py, c), device_id_type=MESH)
            for a in range(self.n) for j, (px, py) in enumerate(chips)]

    def start(self, ins, outs, sems):
        for cp in self._copies(ins, outs, sems):
            cp.start()

    def finish(self, ins, outs, sems):
        for cp in self._copies(ins, outs, sems):
            cp.wait()


HBM = pl.BlockSpec(memory_space=pltpu.HBM)
SEM = pl.BlockSpec(memory_space=pltpu.SEMAPHORE)
EFFECT = pltpu.SideEffectType.DATAFLOW_SIDE_EFFECTING


def _split_start(rider, name, after=None):
    ni, no, ns = len(rider.inputs), len(rider.out_shape), len(rider.scratch)
    extra = [] if after is None else [after]

    def body(*refs):
        ins, lands = refs[:ni], refs[ni:ni + no]
        sems = refs[ni + no + len(extra):ni + no + len(extra) + ns]
        rider.start(ins, lands, sems)
        refs[-1][...] = jnp.zeros_like(refs[-1])

    bufs = list(rider.inputs) + [lax.empty(s.shape, s.dtype) for s in rider.out_shape]
    outs = pl.pallas_call(
        body, name=name,
        out_shape=tuple(rider.scratch) + tuple(pltpu.HBM(b.shape, b.dtype) for b in bufs) + (jax.ShapeDtypeStruct((8, LANES), f32),),
        in_specs=[HBM] * (ni + no) + [ANY] * len(extra),
        out_specs=(SEM,) * ns + (HBM,) * (ni + no) + (pl.BlockSpec(memory_space=pltpu.VMEM),),
        input_output_aliases={i: ns + i for i in range(ni + no)},
        compiler_params=pltpu.CompilerParams(has_side_effects=EFFECT),
    )(*[pltpu.with_memory_space_constraint(b, pltpu.HBM) for b in bufs], *extra)
    return outs[:-1], outs[-1]


def _split_wait(rider, name, handles, after):
    ni, no, ns = len(rider.inputs), len(rider.out_shape), len(rider.scratch)
    sems, bufs = handles[:ns], handles[ns:]

    def body(*refs):
        rider.finish(refs[:ni], refs[ni:ni + no], refs[ni + no:ni + no + ns])

    outs = pl.pallas_call(
        body, name=name, out_shape=tuple(pltpu.HBM(b.shape, b.dtype) for b in bufs),
        in_specs=[HBM] * (ni + no) + [SEM] * ns + [ANY], out_specs=(HBM,) * (ni + no),
        input_output_aliases={i: i for i in range(ni + no)}, compiler_params=pltpu.CompilerParams(has_side_effects=EFFECT),
    )(*bufs, *sems, after)
    return outs[:ni], outs[ni:]


def _ride_alone(rider, name):
    n = len(rider.inputs)

    def body(*refs):
        ins, outs, sems = refs[:n], refs[n:n + len(rider.out_shape)], refs[n + len(rider.out_shape):]
        rider.start(ins, outs, sems)
        rider.finish(ins, outs, sems)

    return pl.pallas_call(
        body, name=name, out_shape=tuple(rider.out_shape), in_specs=[ANY] * n, out_specs=tuple([ANY] * len(rider.out_shape)),
        input_output_aliases=dict(rider.aliases), scratch_shapes=list(rider.scratch),
    )(*rider.inputs)


class _SwapRider:
    def __init__(self, grads):
        n = len(grads)
        self.n = n
        self.inputs = list(grads)
        self.out_shape = [jax.ShapeDtypeStruct((4,) + g.shape[2:], g.dtype) for g in grads]
        self.scratch = [pltpu.SemaphoreType.DMA((4 * n,)), pltpu.SemaphoreType.DMA((4 * n,))]
        self.aliases = {}

    def _copies(self, ins, outs, sems):
        x, y, c, _ = _place()
        return [pltpu.make_async_remote_copy(
            src_ref=ins[a].at[k, 1 - c], dst_ref=outs[a].at[k], send_sem=sems[0].at[4 * a + k], recv_sem=sems[1].at[4 * a + k],
            device_id=(x, y, 1 - c), device_id_type=MESH) for a in range(self.n) for k in range(4)]

    def start(self, ins, outs, sems):
        for cp in self._copies(ins, outs, sems):
            cp.start()

    def finish(self, ins, outs, sems):
        for cp in self._copies(ins, outs, sems):
            cp.wait()


class _Riders:
    def __init__(self, riders):
        self.riders = list(riders)
        self.inputs = [a for r in riders for a in r.inputs]
        self.out_shape = [s for r in riders for s in r.out_shape]
        self.scratch = [s for r in riders for s in r.scratch]
        self.aliases = {}
        i = o = 0
        for r in riders:
            self.aliases.update({i + a: o + b for a, b in r.aliases.items()})
            i, o = i + len(r.inputs), o + len(r.out_shape)

    def _each(self, ins, outs, sems):
        i = o = s = 0
        for r in self.riders:
            yield r, ins[i:i + len(r.inputs)], outs[o:o + len(r.out_shape)], sems[s:s + len(r.scratch)]
            i, o, s = i + len(r.inputs), o + len(r.out_shape), s + len(r.scratch)

    def start(self, ins, outs, sems):
        for r, a, b, c in self._each(ins, outs, sems):
            r.start(a, b, c)

    def finish(self, ins, outs, sems):
        for r, a, b, c in self._each(ins, outs, sems):
            r.finish(a, b, c)

    def split(self, outs):
        res, o = [], 0
        for r in self.riders:
            res.append(outs[o:o + len(r.out_shape)])
            o += len(r.out_shape)
        return res


class _Reducer:
    def __init__(self, chip, core):
        self.chip, self.core, self.grads, self.parts, self.sums, self.others = chip, core, {}, {}, {}, {}

    def swap(self, name, grad):
        self.grads[name] = grad
        return _SwapRider([grad])

    def swapped(self, name, got):
        self.parts[name] = _add_pair(self.grads[name], got[0], self.core, name)

    def scatter(self, name):
        return _ScatterRider([self.parts[name]])

    def scattered(self, name, others):
        self.sums[name] = _add_chips(self.parts[name], others[0], self.chip, name)


class _SwapSumsRider:
    def __init__(self, halves):
        n = len(halves)
        self.n = n
        self.inputs = list(halves)
        self.out_shape = [jax.ShapeDtypeStruct(s.shape, s.dtype) for s in halves]
        self.scratch = [pltpu.SemaphoreType.DMA((n,)), pltpu.SemaphoreType.DMA((n,))]
        self.aliases = {}

    def _copies(self, ins, outs, sems):
        x, y, c, _ = _place()
        return [pltpu.make_async_remote_copy(
            src_ref=ins[a], dst_ref=outs[a], send_sem=sems[0].at[a], recv_sem=sems[1].at[a],
            device_id=(x, y, 1 - c), device_id_type=MESH) for a in range(self.n)]

    def start(self, ins, outs, sems):
        for cp in self._copies(ins, outs, sems):
            cp.start()

    def finish(self, ins, outs, sems):
        for cp in self._copies(ins, outs, sems):
            cp.wait()


def _row_tile(r):
    for tm in (TM, 176, 128, 64, 32, 16, 8):
        if r % tm == 0:
            return tm
    return r


def _add_pair(mine, got, core, name):
    k, _, h, c = mine.shape
    tm = _row_tile(h)

    def body(core_ref, a_ref, b_ref, o_ref):
        o_ref[0] = (a_ref[0, 0].astype(f32) + b_ref[0].astype(f32)).astype(MX)

    blk = pl.BlockSpec((1, tm, c), lambda i, j, core: (i, j, 0))
    return pl.pallas_call(
        body, name="add_pair_" + name, out_shape=jax.ShapeDtypeStruct((k, h, c), MX),
        grid_spec=pltpu.PrefetchScalarGridSpec(
            num_scalar_prefetch=1, grid=(k, h // tm),
            in_specs=[pl.BlockSpec((1, 1, tm, c), lambda i, j, core: (i, core[0], j, 0)), blk], out_specs=blk),
        compiler_params=_cp("arbitrary", "arbitrary"),
    )(_scalar(core), mine, got)


def _add_chips(parts, others, chip, name):
    _, n, c = others.shape
    tm = _row_tile(n)

    def body(chip_ref, a_ref, b_ref, o_ref):
        s = a_ref[0].astype(f32) + b_ref[0].astype(f32)
        o_ref[...] = (s + b_ref[1].astype(f32)) + b_ref[2].astype(f32)

    return pl.pallas_call(
        body, name="add_chips_" + name, out_shape=jax.ShapeDtypeStruct((n, c), f32),
        grid_spec=pltpu.PrefetchScalarGridSpec(
            num_scalar_prefetch=1, grid=(n // tm,),
            in_specs=[pl.BlockSpec((1, tm, c), lambda i, chip: (chip[0], i, 0)),
                      pl.BlockSpec((3, tm, c), lambda i, chip: (0, i, 0))],
            out_specs=pl.BlockSpec((tm, c), lambda i, chip: (i, 0))),
        compiler_params=_cp("arbitrary"),
    )(_scalar(chip), parts, others)


def _adam_math(w, g, m, v):
    m = ADAM_B1 * m + (1.0 - ADAM_B1) * g
    v = ADAM_B2 * v + (1.0 - ADAM_B2) * (g * g)
    m_hat = m / (1.0 - ADAM_B1 ** ADAM_STEP)
    v_hat = v / (1.0 - ADAM_B2 ** ADAM_STEP)
    return -ADAM_LR * (m_hat / (jnp.sqrt(v_hat) + ADAM_EPS) + ADAM_WD * w), m, v


def _adamw_halves(w, mine, other, m, v, core, name, after):
    r, c = w.shape
    h = r // 2
    tm = _row_tile(h)
    nj = h // tm
    cg = mine.shape[1]

    def body(core_ref, w_ref, a_ref, b_ref, m_ref, v_ref, after_ref, g_ref, d_ref, nm_ref, nv_ref):
        g = jnp.where(pl.program_id(0) == core_ref[0], a_ref[:, 0:c], b_ref[:, 0:c])
        g_ref[...] = g
        d_ref[...], nm_ref[...], nv_ref[...] = _adam_math(w_ref[...], g, m_ref[...], v_ref[...])

    blk = pl.BlockSpec((tm, c), lambda i, j, core: (i * nj + j, 0))
    gblk = pl.BlockSpec((tm, cg), lambda i, j, core: (j, 0))
    return _call(body, name=name, grid=(2, nj), out_shape=[jax.ShapeDtypeStruct((r, c), f32)] * 4,
                 in_specs=[blk, gblk, gblk, blk, blk, ANY], out_specs=(blk,) * 4, sem=("arbitrary", "arbitrary"),
                 prefetch=(_scalar(core),), args=(w, mine, other, m, v, after))[0]


def _ada_forward(c_all, ada_w):
    def body(c_ref, w_ref, o_ref):
        o_ref[...] = jnp.dot(_silu(c_ref[...]).astype(MX), w_ref[...].astype(MX), preferred_element_type=f32)

    return pl.pallas_call(body, name="ada_forward", out_shape=jax.ShapeDtypeStruct((8, ada_w.shape[1]), f32),
                          compiler_params=pltpu.CompilerParams(vmem_limit_bytes=VMEM_LIMIT))(c_all, ada_w)


def _ada_adamw(c_all_t, d_mod, w, m, v, after):
    r, c = w.shape
    tm = TM

    def body(ct_ref, dm_ref, w_ref, m_ref, v_ref, after_ref, g_ref, d_ref, nm_ref, nv_ref):
        ca = _silu(ct_ref[...])
        g = ca[:, 0:1] * dm_ref[0:1, :]
        for b in range(1, 8):
            g = g + ca[:, b:b + 1] * dm_ref[b:b + 1, :]
        g_ref[...] = g
        d_ref[...], nm_ref[...], nv_ref[...] = _adam_math(w_ref[...], g, m_ref[...], v_ref[...])

    blk = pl.BlockSpec((tm, c), lambda i: (i, 0))
    return _call(body, name="ada_adamw", grid=(r // tm,), out_shape=[jax.ShapeDtypeStruct((r, c), f32)] * 4,
                 in_specs=[pl.BlockSpec((tm, 8), lambda i: (i, 0)), pl.BlockSpec((8, c), lambda i: (0, 0)), blk, blk, blk, ANY],
                 out_specs=(blk,) * 4, sem=("arbitrary",), args=(c_all_t, d_mod, w, m, v, after))[0]


WEIGHTS = ("ada_w", "ada_b", "norm1_w", "w_in", "ssd_conv_w", "ssd_conv_b", "dt_bias", "a_log", "d_skip", "ssd_norm_w",
           "conf_conv_w", "conf_conv_b", "conf_ln_w", "conf_ln_b", "w_out", "norm2_w", "w_up", "ffn_conv_w", "ffn_conv_b",
           "w_down", "final_norm_w")


def kernel(x, c, ada_w, ada_b, norm1_w, w_in, ssd_conv_w, ssd_conv_b, dt_bias, a_log, d_skip, ssd_norm_w, conf_conv_w, conf_conv_b, conf_ln_w, conf_ln_b, w_out, norm2_w, w_up, ffn_conv_w, ffn_conv_b, w_down, final_norm_w, loss_target, m_ada_w, m_ada_b, m_norm1_w, m_w_in, m_ssd_conv_w, m_ssd_conv_b, m_dt_bias, m_a_log, m_d_skip, m_ssd_norm_w, m_conf_conv_w, m_conf_conv_b, m_conf_ln_w, m_conf_ln_b, m_w_out, m_norm2_w, m_w_up, m_ffn_conv_w, m_ffn_conv_b, m_w_down, m_final_norm_w, v_ada_w, v_ada_b, v_norm1_w, v_w_in, v_ssd_conv_w, v_ssd_conv_b, v_dt_bias, v_a_log, v_d_skip, v_ssd_norm_w, v_conf_conv_w, v_conf_conv_b, v_conf_ln_w, v_conf_ln_b, v_w_out, v_norm2_w, v_w_up, v_ffn_conv_w, v_ffn_conv_b, v_w_down, v_final_norm_w):
    given = dict(locals())
    w = {n: given[n] for n in WEIGHTS}
    mom = {n: given["m_" + n] for n in WEIGHTS}
    var = {n: given["v_" + n] for n in WEIGHTS}
    chip = 2 * lax.axis_index("x") + lax.axis_index("y")
    me = 2 * chip + lax.axis_index("c")

    core = lax.axis_index("c")
    a_in = _cast_into_slot_w_in(_columns_first(w_in), chip)
    got, (a_in,) = _gather_rows(_pack_front(c, [w[n] for n in CONVS]), _GatherRider([a_in], 0, D // 4))
    c_all, *convs = _unpack_front(got)
    conv_full = dict(zip(CONVS, convs))

    got, (a_in,) = _gather_rows(_ada_forward(c_all, ada_w[0]), _GatherRider([a_in], D // 4, D // 4))
    mod_cols = got.reshape(8, 8, -1)[0::2]
    mod = lax.dynamic_index_in_dim(mod_cols, me, axis=1, keepdims=False).reshape(1, 6 * D) + ada_b
    w_pack = _pack_w_in(a_in.reshape(4, D, W_IN_SHARD_PAD))
    late = (_cast_into_slot(w_out[0], D, chip), _cast_into_slot(w_up[0], UP_SHARD, chip), _cast_into_slot(w_down[0], D, chip))

    flat = lambda a: a.reshape(1, -1) if a.ndim == 1 else a
    small = {n: flat(w[n]) for n in VECTORS if n != "ada_b"}
    small.update(conv_full)
    reducer = _Reducer(chip, core)
    _, grad_x, _, gsmall = _local_step(x[0], mod, loss_target[0], w_pack, late, small, reducer)
    grads, delta, new_m, new_v = {}, {}, {}, {}

    names = VECTORS + tuple(CONVS)
    d_mod_mine, loss, res = _small_adamw(_gather_rows(gsmall), chip, *[{n: flat(d[n]) for n in names} for d in (w, mom, var)])
    for n in names:
        grads[n], delta[n], new_m[n], new_v[n] = [r.reshape(w[n].shape) for r in res[n]]

    scatter = reducer.scatter("w_in")
    handles, token = _split_start(scatter, "scatter_start_w_in", after=d_mod_mine)
    for n in ("w_up", "w_down", "w_out"):
        res = _adamw_halves(w[n][0], reducer.sums[n], reducer.others[n], mom[n][0], var[n][0], core, "adamw_" + n, token)
        grads[n], delta[n], new_m[n], new_v[n] = [r[None] for r in res]
    res = _ada_adamw(c_all.T, d_mod_mine, ada_w[0], m_ada_w[0], v_ada_w[0], token)
    grads["ada_w"], delta["ada_w"], new_m["ada_w"], new_v["ada_w"] = [r[None] for r in res]
    (reducer.parts["w_in"],), others = _split_wait(scatter, "scatter_wait_w_in", handles, res[1])
    reducer.scattered("w_in", others)
    reducer.others["w_in"], = _ride_alone(_SwapSumsRider([reducer.sums["w_in"]]), "swap_sums_w_in")
    res = _adamw_w_in(_columns_first(w_in), reducer.sums["w_in"], reducer.others["w_in"], _columns_first(m_w_in),
                      _columns_first(v_w_in), core)
    grads["w_in"], delta["w_in"], new_m["w_in"], new_v["w_in"] = [jnp.transpose(r, (1, 2, 0)) for r in res]

    return (loss, grad_x[None], *[grads[n] for n in WEIGHTS], *[delta[n] for n in WEIGHTS],
            *[new_m[n] for n in WEIGHTS], *[new_v[n] for n in WEIGHTS])
```

```python
import functools

import jax
import jax.numpy as jnp
from jax import lax
from jax.experimental import pallas as pl
from jax.experimental.pallas import tpu as pltpu

f32 = jnp.float32
MX = jnp.bfloat16

D = 1024
HEADS = 16
HEAD_P = 64
STATE_N = 128
D_XBC = 1536
D_FF = 2816
UP_SHARD = 2 * D_FF // 4
UP_EARLY_ROWS = 128
K_SSD, K_CONF, K_FFN = 4, 31, 3
CHUNK = 128
OFF_Z, OFF_XBC, OFF_CA, OFF_CG, OFF_DT = 0, 1024, 2560, 3584, 4608
W_PACK = 4736
TM = 256
CW = 256
RC = 64
LANES = 128
VMEM_LIMIT = 56 * 1024 * 1024

ADAM_LR, ADAM_B1, ADAM_B2, ADAM_EPS, ADAM_WD, ADAM_STEP = 0.001, 0.9, 0.999, 1e-08, 0.01, 10

MESH = pl.DeviceIdType.MESH


def _cp(*sem):
    return pltpu.CompilerParams(dimension_semantics=sem, vmem_limit_bytes=VMEM_LIMIT)


def _resident(shape):
    nd = len(shape)
    return pl.BlockSpec(shape, lambda *_: (0,) * nd, pipeline_mode=pl.Buffered(1))


def _row(width=D):
    return pl.BlockSpec((1, width), lambda *_: (0, 0))


def _call(body, *, name, grid, in_specs, out_specs, out_shape, args, sem, scratch_shapes=(), prefetch=(), rider=None):
    ni, no, ns, npf = len(in_specs), len(out_specs), len(scratch_shapes), len(prefetch)
    ri, ro = (len(rider.inputs), len(rider.out_shape)) if rider is not None else (0, 0)

    def full(*refs):
        pre, refs = refs[:npf], refs[npf:]
        base_in, r_in = refs[:ni], refs[ni:ni + ri]
        base_out, r_out = refs[ni + ri:ni + ri + no], refs[ni + ri + no:ni + ri + no + ro]
        base_scr, r_scr = refs[ni + ri + no + ro:ni + ri + no + ro + ns], refs[ni + ri + no + ro + ns:]
        if rider is None:
            return body(*pre, *base_in, *base_out, *base_scr)
        ids = [pl.program_id(a) for a in range(len(grid))]
        first = functools.reduce(jnp.logical_and, [i == 0 for i in ids])
        last = functools.reduce(jnp.logical_and, [i == g - 1 for i, g in zip(ids, grid)])

        @pl.when(first)
        def _():
            rider.start(r_in, r_out, r_scr)

        body(*pre, *base_in, *base_out, *base_scr)

        @pl.when(last)
        def _():
            rider.finish(r_in, r_out, r_scr)

    extra = dict(shapes=[], scratch=[], aliases={}, inputs=[]) if rider is None else dict(
        shapes=rider.out_shape, scratch=rider.scratch, inputs=rider.inputs,
        aliases={npf + ni + i: no + j for i, j in rider.aliases.items()})
    outs = pl.pallas_call(
        full, name=name, out_shape=tuple(out_shape) + tuple(extra["shapes"]), input_output_aliases=extra["aliases"],
        grid_spec=pltpu.PrefetchScalarGridSpec(
            num_scalar_prefetch=npf, grid=grid, in_specs=list(in_specs) + [ANY] * ri,
            out_specs=tuple(out_specs) + (ANY,) * ro, scratch_shapes=list(scratch_shapes) + list(extra["scratch"])),
        compiler_params=_cp(*sem),
    )(*prefetch, *args, *extra["inputs"])
    return tuple(outs[:no]), tuple(outs[no:])


def _silu(v):
    return v * jax.nn.sigmoid(v)


def _dsilu(v):
    s = jax.nn.sigmoid(v)
    return s * (1.0 + v * (1.0 - s))


def _softplus(v):
    return jnp.maximum(v, 0.0) + jnp.log1p(jnp.exp(-jnp.abs(v)))


def _mm(a, b):
    return jnp.dot(a.astype(MX), b.astype(MX), preferred_element_type=f32)


def _mm_nt(a, b):
    return lax.dot_general(a.astype(MX), b.astype(MX), (((1,), (1,)), ((), ())), preferred_element_type=f32)


def _mm_tn(a, b):
    return lax.dot_general(a.astype(MX), b.astype(MX), (((0,), (0,)), ((), ())), preferred_element_type=f32)


def _ln_inproj(x, mod, norm1_w, w_pack, rider=None):
    t = x.shape[0]

    def body(x_ref, mod_ref, nw_ref, w_ref, proj_ref, ht_ref):
        xv = x_ref[...]
        rstd = lax.rsqrt(jnp.mean(xv * xv, axis=-1, keepdims=True) + 1e-6)
        h = (xv * rstd * nw_ref[...]) * (1.0 + mod_ref[:, D:2 * D]) + mod_ref[:, 0:D]
        hb = h.astype(MX)
        ht_ref[...] = hb.T
        proj_ref[...] = jnp.dot(hb, w_ref[...], preferred_element_type=f32)

    return _call(
        body, name="ln_inproj", grid=(t // TM,),
        out_shape=(jax.ShapeDtypeStruct((t, W_PACK), f32), jax.ShapeDtypeStruct((D, t), MX)),
        in_specs=[pl.BlockSpec((TM, D), lambda i: (i, 0)), _row(6 * D), _row(), _resident((D, W_PACK))],
        out_specs=(pl.BlockSpec((TM, W_PACK), lambda i: (i, 0)), pl.BlockSpec((D, TM), lambda i: (0, i))),
        sem=("arbitrary",), args=(x, mod, norm1_w, w_pack), rider=rider)


def _ssd_gate_norm(y_scan, xbc_act, proj, d_skip_row, ssd_norm_w):
    t = y_scan.shape[0]

    def body(y_ref, xs_ref, z_ref, dsk_ref, nw_ref, o_ref):
        y = y_ref[...] + xs_ref[...] * dsk_ref[...]
        yz = y * _silu(z_ref[...])
        rstd = lax.rsqrt(jnp.mean(yz * yz, axis=-1, keepdims=True) + 1e-6)
        o_ref[...] = (yz * rstd * nw_ref[...]).astype(MX)

    blk = pl.BlockSpec((TM, D), lambda i: (i, 0))
    return pl.pallas_call(
        body, name="ssd_gate_norm", grid=(t // TM,), out_shape=jax.ShapeDtypeStruct((t, D), MX),
        in_specs=[blk, blk, blk, _row(), _row()], out_specs=blk, compiler_params=_cp("arbitrary"),
    )(y_scan, xbc_act, proj, d_skip_row, ssd_norm_w)


def _ln_silu(u_conv, ln_w, ln_b):
    t = u_conv.shape[0]

    def body(u_ref, w_ref, b_ref, o_ref):
        u = u_ref[...]
        mu = jnp.mean(u, axis=-1, keepdims=True)
        uc = u - mu
        rstd = lax.rsqrt(jnp.mean(uc * uc, axis=-1, keepdims=True) + 1e-5)
        o_ref[...] = _silu(uc * rstd * w_ref[...] + b_ref[...]).astype(MX)

    blk = pl.BlockSpec((TM, D), lambda i: (i, 0))
    return pl.pallas_call(
        body, name="ln_silu", grid=(t // TM,), out_shape=jax.ShapeDtypeStruct((t, D), MX),
        in_specs=[blk, _row(), _row()], out_specs=blk, compiler_params=_cp("arbitrary"),
    )(u_conv, ln_w, ln_b)


def _outproj_ln2_up(y_ssd, u, w_out, x, mod, norm2_w, w_up):
    t = x.shape[0]

    def body(y_ref, u_ref, wo_ref, x_ref, mod_ref, nw_ref, wu_ref, mix_ref, x1_ref, h2t_ref, up_ref):
        mix = jnp.dot(y_ref[...], wo_ref[0:D, :], preferred_element_type=f32)
        mix = mix + jnp.dot(u_ref[...], wo_ref[D:2 * D, :], preferred_element_type=f32)
        mix_ref[...] = mix
        x1 = x_ref[...] + mod_ref[:, 2 * D:3 * D] * mix
        x1_ref[...] = x1
        rstd = lax.rsqrt(jnp.mean(x1 * x1, axis=-1, keepdims=True) + 1e-6)
        h2 = ((x1 * rstd * nw_ref[...]) * (1.0 + mod_ref[:, 4 * D:5 * D]) + mod_ref[:, 3 * D:4 * D]).astype(MX)
        h2t_ref[...] = h2.T
        for k in range(4):
            up_ref[:, k * UP_SHARD:(k + 1) * UP_SHARD] = jnp.dot(h2, wu_ref[k], preferred_element_type=f32)

    blk = pl.BlockSpec((TM, D), lambda i: (i, 0))
    return pl.pallas_call(
        body, name="outproj_ln2_up", grid=(t // TM,),
        out_shape=(jax.ShapeDtypeStruct((t, D), f32), jax.ShapeDtypeStruct((t, D), f32),
                   jax.ShapeDtypeStruct((D, t), MX), jax.ShapeDtypeStruct((t, 2 * D_FF), f32)),
        in_specs=[blk, blk, _resident((2 * D, D)), blk, _row(6 * D), _row(), _resident((4, D, UP_SHARD))],
        out_specs=(blk, blk, pl.BlockSpec((D, TM), lambda i: (0, i)), pl.BlockSpec((TM, 2 * D_FF), lambda i: (i, 0))),
        compiler_params=_cp("arbitrary"),
    )(y_ssd, u, w_out, x, mod, norm2_w, w_up)


def _down_loss(act, w_down, x1, mod, final_norm_w, target):
    t = x1.shape[0]

    def body(a_ref, wd_ref, x1_ref, mod_ref, wf_ref, tgt_ref, dx2_ref, dffn_ref, dact_ref, st_ref):
        @pl.when(pl.program_id(0) == 0)
        def _():
            st_ref[...] = jnp.zeros_like(st_ref)

        g2 = mod_ref[:, 5 * D:6 * D]
        ffn = jnp.dot(a_ref[...], wd_ref[...], preferred_element_type=f32)
        x2 = x1_ref[...] + g2 * ffn
        rstd = lax.rsqrt(jnp.mean(x2 * x2, axis=-1, keepdims=True) + 1e-6)
        xh = x2 * rstd
        wf = wf_ref[...]
        err = xh * wf - tgt_ref[...]
        dy = err * (1.0 / D)
        dxh = dy * wf
        dx2 = rstd * (dxh - xh * jnp.mean(dxh * xh, axis=-1, keepdims=True))
        dx2_ref[...] = dx2
        dffn = (g2 * dx2).astype(MX)
        dffn_ref[...] = dffn
        dact_ref[...] = lax.dot_general(dffn, wd_ref[...], (((1,), (1,)), ((), ())), preferred_element_type=f32)
        st_ref[0:1, :] += jnp.sum(dy * xh, axis=0, keepdims=True)
        st_ref[1:2, :] += jnp.sum(dx2 * ffn, axis=0, keepdims=True)
        st_ref[2:3, :] += jnp.sum(0.5 * jnp.mean(err * err, axis=-1, keepdims=True), axis=0, keepdims=True)

    blk = pl.BlockSpec((TM, D), lambda i: (i, 0))
    ablk = pl.BlockSpec((TM, D_FF), lambda i: (i, 0))
    return pl.pallas_call(
        body, name="down_loss", grid=(t // TM,),
        out_shape=(jax.ShapeDtypeStruct((t, D), f32), jax.ShapeDtypeStruct((t, D), MX),
                   jax.ShapeDtypeStruct((t, D_FF), f32), jax.ShapeDtypeStruct((8, D), f32)),
        in_specs=[ablk, _resident((D_FF, D)), blk, _row(6 * D), _row(), blk],
        out_specs=(blk, blk, ablk, pl.BlockSpec((8, D), lambda i: (0, 0))),
        compiler_params=_cp("arbitrary"),
    )(act, w_down, x1, mod, final_norm_w, target)


def _pad_of(k):
    return 8 * ((k - 1 + 7) // 8)


def _causal_win(ref, r, t, pad):
    base = pl.multiple_of(r * RC, RC)
    prev = ref[pl.ds(pl.multiple_of(jnp.maximum(base - pad, 0), 8), pad), :]
    prev = jnp.where(r > 0, prev, 0.0)
    return jnp.concatenate([prev, ref[pl.ds(base, RC), :]], axis=0)


def _anti_win(ref, r, t, pad):
    base = pl.multiple_of(r * RC, RC)
    nxt = ref[pl.ds(pl.multiple_of(jnp.minimum(base + RC, t - pad), 8), pad), :]
    nxt = jnp.where(r < t // RC - 1, nxt, 0.0)
    return jnp.concatenate([ref[pl.ds(base, RC), :], nxt], axis=0)


def _shifted(win, offsets):
    for r in range(8):
        mine = [o for o in offsets if o % 8 == r]
        if mine:
            rolled = win if r == 0 else pltpu.roll(win, win.shape[0] - r, 0)
            for o in mine:
                yield o, rolled[o - r:o - r + RC, :]


def _conv_taps(win, w_ref, k, pad):
    first = pad - (k - 1)
    acc = None
    for o, rows in _shifted(win, range(first, first + k)):
        term = w_ref[o - first:o - first + 1, :] * rows
        acc = term if acc is None else acc + term
    return acc


def _corr_taps(win, w_ref, k):
    acc = None
    for o, rows in _shifted(win, range(k)):
        term = w_ref[k - 1 - o:k - o, :] * rows
        acc = term if acc is None else acc + term
    return acc


def _dw_accumulate(dw_scr, d, win, k, pad):
    first = pad - (k - 1)
    for o, rows in _shifted(win, range(first, first + k)):
        j = o - first
        prod = d * rows
        dw_scr[8 * j:8 * j + 8, :] += prod.reshape(RC // 8, 8, prod.shape[-1]).sum(axis=0)


def _dw_finish(dw_scr, dw_ref, k):
    for j in range(k):
        dw_ref[j:j + 1, :] = jnp.sum(dw_scr[8 * j:8 * j + 8, :], axis=0, keepdims=True)


def _rows8(v):
    return v.reshape(RC // 8, 8, v.shape[-1]).sum(axis=0)


def _ssd_conv_fwd(proj, conv_w, conv_b, rider=None):
    t = proj.shape[0]
    pad = _pad_of(K_SSD)
    c0 = OFF_XBC // CW

    def body(x_ref, w_ref, b_ref, o_ref):
        def step(r, carry):
            win = _causal_win(x_ref, r, t, pad)
            o_ref[pl.ds(pl.multiple_of(r * RC, RC), RC), :] = _silu(_conv_taps(win, w_ref, K_SSD, pad) + b_ref[...])
            return carry
        lax.fori_loop(0, t // RC, step, 0)

    return _call(
        body, name="ssd_conv_fwd", grid=(D_XBC // CW,), out_shape=(jax.ShapeDtypeStruct((t, D_XBC), f32),),
        in_specs=[pl.BlockSpec((t, CW), lambda j: (0, c0 + j)), pl.BlockSpec((K_SSD, CW), lambda j: (0, j)),
                  pl.BlockSpec((1, CW), lambda j: (0, j))],
        out_specs=(pl.BlockSpec((t, CW), lambda j: (0, j)),), sem=("arbitrary",), args=(proj, conv_w, conv_b), rider=rider)


def _glu_conv_fwd(proj, conv_w, conv_b, rider=None):
    t = proj.shape[0]
    pad = _pad_of(K_CONF)
    ca, cg = OFF_CA // CW, OFF_CG // CW

    def body(a_ref, g_ref, w_ref, b_ref, o_ref, v_scr):
        def glu(r, carry):
            rows = pl.ds(pl.multiple_of(r * RC, RC), RC)
            v_scr[rows, :] = a_ref[rows, :] * jax.nn.sigmoid(g_ref[rows, :])
            return carry
        lax.fori_loop(0, t // RC, glu, 0)

        def step(r, carry):
            win = _causal_win(v_scr, r, t, pad)
            o_ref[pl.ds(pl.multiple_of(r * RC, RC), RC), :] = _conv_taps(win, w_ref, K_CONF, pad) + b_ref[...]
            return carry
        lax.fori_loop(0, t // RC, step, 0)

    return _call(
        body, name="glu_conv_fwd", grid=(D // CW,), out_shape=(jax.ShapeDtypeStruct((t, D), f32),),
        in_specs=[pl.BlockSpec((t, CW), lambda j: (0, ca + j)), pl.BlockSpec((t, CW), lambda j: (0, cg + j)),
                  pl.BlockSpec((K_CONF, CW), lambda j: (0, j)), pl.BlockSpec((1, CW), lambda j: (0, j))],
        out_specs=(pl.BlockSpec((t, CW), lambda j: (0, j)),),
        scratch_shapes=[pltpu.VMEM((t, CW), f32)], sem=("arbitrary",), args=(proj, proj, conv_w, conv_b), rider=rider)


def _ffn_conv_fwd(up, conv_w, conv_b, rider=None):
    t = up.shape[0]
    pad = _pad_of(K_FFN)
    nb = D_FF // CW

    def body(g_ref, v_ref, wg_ref, wv_ref, bg_ref, bv_ref, o_ref):
        def step(r, carry):
            gc = _conv_taps(_causal_win(g_ref, r, t, pad), wg_ref, K_FFN, pad) + bg_ref[...]
            vc = _conv_taps(_causal_win(v_ref, r, t, pad), wv_ref, K_FFN, pad) + bv_ref[...]
            o_ref[pl.ds(pl.multiple_of(r * RC, RC), RC), :] = (_silu(gc) * vc).astype(MX)
            return carry
        lax.fori_loop(0, t // RC, step, 0)

    return _call(
        body, name="ffn_conv_fwd", grid=(nb,), out_shape=(jax.ShapeDtypeStruct((t, D_FF), MX),),
        in_specs=[pl.BlockSpec((t, CW), lambda j: (0, j)), pl.BlockSpec((t, CW), lambda j: (0, nb + j)),
                  pl.BlockSpec((K_FFN, CW), lambda j: (0, j)), pl.BlockSpec((K_FFN, CW), lambda j: (0, nb + j)),
                  pl.BlockSpec((1, CW), lambda j: (0, j)), pl.BlockSpec((1, CW), lambda j: (0, nb + j))],
        out_specs=(pl.BlockSpec((t, CW), lambda j: (0, j)),), sem=("arbitrary",),
        args=(up, up, conv_w, conv_w, conv_b, conv_b), rider=rider)


def _ffn_conv_bwd(up, conv_w, conv_b, d_act, rider=None):
    t = up.shape[0]
    pad = _pad_of(K_FFN)
    nb = D_FF // CW

    def body(g_ref, v_ref, wg_ref, wv_ref, bg_ref, bv_ref, da_ref, dup_ref, dw_ref, db_ref,
             dg_scr, dv_scr, dwg_scr, dwv_scr, db_scr):
        dwg_scr[...] = jnp.zeros_like(dwg_scr)
        dwv_scr[...] = jnp.zeros_like(dwv_scr)
        db_scr[...] = jnp.zeros_like(db_scr)

        def first(r, carry):
            rows = pl.ds(pl.multiple_of(r * RC, RC), RC)
            gwin = _causal_win(g_ref, r, t, pad)
            vwin = _causal_win(v_ref, r, t, pad)
            gc = _conv_taps(gwin, wg_ref, K_FFN, pad) + bg_ref[...]
            vc = _conv_taps(vwin, wv_ref, K_FFN, pad) + bv_ref[...]
            da = da_ref[rows, :]
            dgc = da * vc * _dsilu(gc)
            dvc = da * _silu(gc)
            dg_scr[rows, :] = dgc
            dv_scr[rows, :] = dvc
            _dw_accumulate(dwg_scr, dgc, gwin, K_FFN, pad)
            _dw_accumulate(dwv_scr, dvc, vwin, K_FFN, pad)
            db_scr[0:8, :] += _rows8(dgc)
            db_scr[8:16, :] += _rows8(dvc)
            return carry
        lax.fori_loop(0, t // RC, first, 0)

        def second(r, carry):
            rows = pl.ds(pl.multiple_of(r * RC, RC), RC)
            dup_ref[0, rows, :] = _corr_taps(_anti_win(dg_scr, r, t, pad), wg_ref, K_FFN).astype(MX)
            dup_ref[1, rows, :] = _corr_taps(_anti_win(dv_scr, r, t, pad), wv_ref, K_FFN).astype(MX)
            return carry
        lax.fori_loop(0, t // RC, second, 0)

        for j in range(K_FFN):
            dw_ref[0, j:j + 1, :] = jnp.sum(dwg_scr[8 * j:8 * j + 8, :], axis=0, keepdims=True)
            dw_ref[1, j:j + 1, :] = jnp.sum(dwv_scr[8 * j:8 * j + 8, :], axis=0, keepdims=True)
        db_ref[0] = jnp.sum(db_scr[0:8, :], axis=0, keepdims=True)
        db_ref[1] = jnp.sum(db_scr[8:16, :], axis=0, keepdims=True)

    return _call(
        body, name="ffn_conv_bwd", grid=(nb,),
        out_shape=(jax.ShapeDtypeStruct((2, t, D_FF), MX), jax.ShapeDtypeStruct((2, K_FFN, D_FF), f32),
                   jax.ShapeDtypeStruct((2, 1, D_FF), f32)),
        in_specs=[pl.BlockSpec((t, CW), lambda j: (0, j)), pl.BlockSpec((t, CW), lambda j: (0, nb + j)),
                  pl.BlockSpec((K_FFN, CW), lambda j: (0, j)), pl.BlockSpec((K_FFN, CW), lambda j: (0, nb + j)),
                  pl.BlockSpec((1, CW), lambda j: (0, j)), pl.BlockSpec((1, CW), lambda j: (0, nb + j)),
                  pl.BlockSpec((t, CW), lambda j: (0, j))],
        out_specs=(pl.BlockSpec((2, t, CW), lambda j: (0, 0, j)), pl.BlockSpec((2, K_FFN, CW), lambda j: (0, 0, j)),
                   pl.BlockSpec((2, 1, CW), lambda j: (0, 0, j))),
        scratch_shapes=[pltpu.VMEM((t, CW), f32), pltpu.VMEM((t, CW), f32), pltpu.VMEM((8 * K_FFN, CW), f32),
                        pltpu.VMEM((8 * K_FFN, CW), f32), pltpu.VMEM((16, CW), f32)],
        sem=("arbitrary",), args=(up, up, conv_w, conv_w, conv_b, conv_b, d_act), rider=rider)


def _glu_conv_bwd(proj, conv_w, d_uconv, rider=None):
    t = proj.shape[0]
    pad = _pad_of(K_CONF)
    ca, cg = OFF_CA // CW, OFF_CG // CW

    def body(a_ref, g_ref, w_ref, du_ref, dc_ref, dw_ref, db_ref, v_scr, dw_scr, db_scr):
        dw_scr[...] = jnp.zeros_like(dw_scr)
        db_scr[...] = jnp.zeros_like(db_scr)

        def glu(r, carry):
            rows = pl.ds(pl.multiple_of(r * RC, RC), RC)
            v_scr[rows, :] = a_ref[rows, :] * jax.nn.sigmoid(g_ref[rows, :])
            return carry
        lax.fori_loop(0, t // RC, glu, 0)

        def step(r, carry):
            rows = pl.ds(pl.multiple_of(r * RC, RC), RC)
            du = du_ref[rows, :]
            _dw_accumulate(dw_scr, du, _causal_win(v_scr, r, t, pad), K_CONF, pad)
            db_scr[...] += _rows8(du)
            dv = _corr_taps(_anti_win(du_ref, r, t, pad), w_ref, K_CONF)
            a = a_ref[rows, :]
            s = jax.nn.sigmoid(g_ref[rows, :])
            dc_ref[0, rows, :] = (dv * s).astype(MX)
            dc_ref[1, rows, :] = (dv * a * s * (1.0 - s)).astype(MX)
            return carry
        lax.fori_loop(0, t // RC, step, 0)
        _dw_finish(dw_scr, dw_ref, K_CONF)
        db_ref[...] = jnp.sum(db_scr[...], axis=0, keepdims=True)

    return _call(
        body, name="glu_conv_bwd", grid=(D // CW,),
        out_shape=(jax.ShapeDtypeStruct((2, t, D), MX), jax.ShapeDtypeStruct((K_CONF, D), f32),
                   jax.ShapeDtypeStruct((1, D), f32)),
        in_specs=[pl.BlockSpec((t, CW), lambda j: (0, ca + j)), pl.BlockSpec((t, CW), lambda j: (0, cg + j)),
                  pl.BlockSpec((K_CONF, CW), lambda j: (0, j)), pl.BlockSpec((t, CW), lambda j: (0, j))],
        out_specs=(pl.BlockSpec((2, t, CW), lambda j: (0, 0, j)), pl.BlockSpec((K_CONF, CW), lambda j: (0, j)),
                   pl.BlockSpec((1, CW), lambda j: (0, j))),
        scratch_shapes=[pltpu.VMEM((t, CW), f32), pltpu.VMEM((8 * K_CONF, CW), f32), pltpu.VMEM((8, CW), f32)],
        sem=("arbitrary",), args=(proj, proj, conv_w, d_uconv), rider=rider)


def _ssd_conv_bwd_x(proj, conv_w, conv_b, d_xs, d_y, d_skip_row):
    t = proj.shape[0]
    pad = _pad_of(K_SSD)
    c0 = OFF_XBC // CW

    def body(x_ref, w_ref, b_ref, dxs_ref, dy_ref, dsk_ref, draw_ref, dw_ref, db_ref, dp_scr, dw_scr, db_scr):
        dw_scr[...] = jnp.zeros_like(dw_scr)
        db_scr[...] = jnp.zeros_like(db_scr)

        def first(r, carry):
            rows = pl.ds(pl.multiple_of(r * RC, RC), RC)
            win = _causal_win(x_ref, r, t, pad)
            pre = _conv_taps(win, w_ref, K_SSD, pad) + b_ref[...]
            dpre = (dxs_ref[rows, :] + dy_ref[rows, :] * dsk_ref[...]) * _dsilu(pre)
            dp_scr[rows, :] = dpre
            _dw_accumulate(dw_scr, dpre, win, K_SSD, pad)
            db_scr[...] += _rows8(dpre)
            return carry
        lax.fori_loop(0, t // RC, first, 0)

        def second(r, carry):
            rows = pl.ds(pl.multiple_of(r * RC, RC), RC)
            draw_ref[rows, :] = _corr_taps(_anti_win(dp_scr, r, t, pad), w_ref, K_SSD).astype(MX)
            return carry
        lax.fori_loop(0, t // RC, second, 0)
        _dw_finish(dw_scr, dw_ref, K_SSD)
        db_ref[...] = jnp.sum(db_scr[...], axis=0, keepdims=True)

    cb = pl.BlockSpec((t, CW), lambda j: (0, j))
    return pl.pallas_call(
        body, name="ssd_conv_bwd_x", grid=(D // CW,),
        out_shape=(jax.ShapeDtypeStruct((t, D), MX), jax.ShapeDtypeStruct((K_SSD, D), f32),
                   jax.ShapeDtypeStruct((1, D), f32)),
        in_specs=[pl.BlockSpec((t, CW), lambda j: (0, c0 + j)), pl.BlockSpec((K_SSD, CW), lambda j: (0, j)),
                  pl.BlockSpec((1, CW), lambda j: (0, j)), cb, cb, pl.BlockSpec((1, CW), lambda j: (0, j))],
        out_specs=(cb, pl.BlockSpec((K_SSD, CW), lambda j: (0, j)), pl.BlockSpec((1, CW), lambda j: (0, j))),
        scratch_shapes=[pltpu.VMEM((t, CW), f32), pltpu.VMEM((8 * K_SSD, CW), f32), pltpu.VMEM((8, CW), f32)],
        compiler_params=_cp("arbitrary"),
    )(proj, conv_w, conv_b, d_xs, d_y, d_skip_row)


def _ssd_conv_bwd_bc(proj, conv_w, conv_b, d_bc):
    t = proj.shape[0]
    pad = _pad_of(K_SSD)
    c0 = (OFF_XBC + D) // CW
    w0 = D // CW

    def body(x_ref, w_ref, b_ref, dbc_ref, draw_ref, dw_ref, db_ref, dp_scr, dw_scr, db_scr):
        dw_scr[...] = jnp.zeros_like(dw_scr)
        db_scr[...] = jnp.zeros_like(db_scr)

        def first(r, carry):
            rows = pl.ds(pl.multiple_of(r * RC, RC), RC)
            win = _causal_win(x_ref, r, t, pad)
            pre = _conv_taps(win, w_ref, K_SSD, pad) + b_ref[...]
            dpre = dbc_ref[0, rows, :] * _dsilu(pre)
            dp_scr[rows, :] = dpre
            _dw_accumulate(dw_scr, dpre, win, K_SSD, pad)
            db_scr[...] += _rows8(dpre)
            return carry
        lax.fori_loop(0, t // RC, first, 0)

        def second(r, carry):
            rows = pl.ds(pl.multiple_of(r * RC, RC), RC)
            draw_ref[rows, :] = _corr_taps(_anti_win(dp_scr, r, t, pad), w_ref, K_SSD).astype(MX)
            return carry
        lax.fori_loop(0, t // RC, second, 0)
        _dw_finish(dw_scr, dw_ref, K_SSD)
        db_ref[...] = jnp.sum(db_scr[...], axis=0, keepdims=True)

    return pl.pallas_call(
        body, name="ssd_conv_bwd_bc", grid=(2,),
        out_shape=(jax.ShapeDtypeStruct((t, 2 * CW), MX), jax.ShapeDtypeStruct((K_SSD, 2 * CW), f32),
                   jax.ShapeDtypeStruct((1, 2 * CW), f32)),
        in_specs=[pl.BlockSpec((t, CW), lambda j: (0, c0 + j)), pl.BlockSpec((K_SSD, CW), lambda j: (0, w0 + j)),
                  pl.BlockSpec((1, CW), lambda j: (0, w0 + j)), pl.BlockSpec((1, t, CW), lambda j: (j, 0, 0))],
        out_specs=(pl.BlockSpec((t, CW), lambda j: (0, j)), pl.BlockSpec((K_SSD, CW), lambda j: (0, j)),
                   pl.BlockSpec((1, CW), lambda j: (0, j))),
        scratch_shapes=[pltpu.VMEM((t, CW), f32), pltpu.VMEM((8 * K_SSD, CW), f32), pltpu.VMEM((8, CW), f32)],
        compiler_params=_cp("arbitrary"),
    )(proj, conv_w, conv_b, d_bc)


def _chunk_masks():
    ii = lax.broadcasted_iota(jnp.int32, (CHUNK, CHUNK), 0)
    jj = lax.broadcasted_iota(jnp.int32, (CHUNK, CHUNK), 1)
    return ii == jj, jj <= ii, jj >= ii


def _to_row(col, eye):
    return jnp.sum(jnp.where(eye, col, 0.0), axis=0, keepdims=True)


def _to_col(row, eye):
    return jnp.sum(jnp.where(eye, row, 0.0), axis=1, keepdims=True)


def _head_decay(dt_h, a_h, eye, tril):
    a_row = _to_row(dt_h * a_h, eye)
    cs = jnp.sum(jnp.where(tril, a_row, 0.0), axis=1, keepdims=True)
    cs_row = _to_row(cs, eye)
    decay = jnp.where(tril, jnp.exp(jnp.where(tril, cs - cs_row, 0.0)), 0.0)
    total = jnp.sum(a_row, axis=1, keepdims=True)
    return cs, decay, total


SCAN_UNROLL = 4


def _unrolled_loop(n, step, init):
    unroll = min(SCAN_UNROLL, n)
    assert n % unroll == 0

    def trip(i, carry):
        for u in range(unroll):
            carry = step(unroll * i + u, carry)
        return carry
    return lax.fori_loop(0, n // unroll, trip, init)


def _lane_pick(mat, lane, which):
    return jnp.sum(jnp.where(lane == which, mat, 0.0), axis=1, keepdims=True)


def _ssd_fwd(xbc_act, proj, dt_bias_row, a_log_row, rider=None):
    t = xbc_act.shape[0]
    nc = t // CHUNK
    cb, cc, cdt = D // LANES, (D + 2 * STATE_N) // LANES, OFF_DT // LANES

    def body(x_ref, b_ref, c_ref, dt_ref, dtb_ref, alog_ref, y_ref, st_ref):
        j = pl.program_id(0)
        eye, tril, _ = _chunk_masks()
        lane = lax.broadcasted_iota(jnp.int32, (1, LANES), 1)
        first = lane < HEAD_P
        a_row = -jnp.exp(alog_ref[...])
        a_heads = [jnp.sum(jnp.where(lane == 2 * j + h, a_row, 0.0), axis=1, keepdims=True) for h in range(2)]

        def chunk(c, hprev):
            rows = pl.ds(pl.multiple_of(c * CHUNK, CHUNK), CHUNK)
            xv, bm, cm = x_ref[rows, :], b_ref[rows, :], c_ref[rows, :]
            dt = _softplus(dt_ref[rows, :] + dtb_ref[...])
            st_ref[c] = hprev
            g = _mm_nt(cm, bm)
            ch = _mm(cm, hprev)
            dts = [_lane_pick(dt, lane, 2 * j + h) for h in range(2)]
            xdt = xv * jnp.where(first, dts[0], dts[1])
            ys, hs = [], []
            for h in range(2):
                cs, decay, total = _head_decay(dts[h], a_heads[h], eye, tril)
                y = _mm(g * decay, xdt) + jnp.exp(cs) * ch
                s = _mm_tn(bm * jnp.exp(total - cs), xdt)
                ys.append(y)
                hs.append(jnp.exp(total) * hprev + s)
            y_ref[rows, :] = jnp.where(first, ys[0], ys[1])
            return jnp.where(first, hs[0], hs[1])

        _unrolled_loop(nc, chunk, jnp.zeros((STATE_N, LANES), f32))

    blk = lambda f: pl.BlockSpec((t, LANES), f)
    return _call(
        body, name="ssd_fwd", grid=(D // LANES,),
        out_shape=(jax.ShapeDtypeStruct((t, D), f32), jax.ShapeDtypeStruct((nc, STATE_N, D), f32)),
        in_specs=[blk(lambda j: (0, j)), blk(lambda j: (0, cb + j // 4)), blk(lambda j: (0, cc + j // 4)),
                  blk(lambda j: (0, cdt)), _row(LANES), _row(LANES)],
        out_specs=(blk(lambda j: (0, j)), pl.BlockSpec((nc, STATE_N, LANES), lambda j: (0, 0, j))),
        sem=("arbitrary",), args=(xbc_act, xbc_act, xbc_act, proj, dt_bias_row, a_log_row), rider=rider)


def _ssd_bwd(xbc_act, proj, dt_bias_row, a_log_row, states, d_y, rider=None):
    t = xbc_act.shape[0]
    nc = t // CHUNK
    cb, cc, cdt = D // LANES, (D + 2 * STATE_N) // LANES, OFF_DT // LANES

    def body(x_ref, b_ref, c_ref, dt_ref, dtb_ref, alog_ref, st_ref, dy_ref, dx_ref, dbc_ref, ddt_ref, da_ref):
        grp, p = pl.program_id(0), pl.program_id(1)
        j = 4 * grp + p
        eye, tril, triu = _chunk_masks()
        lane = lax.broadcasted_iota(jnp.int32, (1, LANES), 1)
        first = lane < HEAD_P
        last_row = lax.broadcasted_iota(jnp.int32, (CHUNK, 1), 0) == CHUNK - 1
        a_row = -jnp.exp(alog_ref[...])
        a_heads = [jnp.sum(jnp.where(lane == 2 * j + h, a_row, 0.0), axis=1, keepdims=True) for h in range(2)]

        @pl.when(p == 0)
        def _():
            dbc_ref[...] = jnp.zeros_like(dbc_ref)

        @pl.when(j == 0)
        def _():
            ddt_ref[...] = jnp.zeros_like(ddt_ref)
            da_ref[...] = jnp.zeros_like(da_ref)

        def chunk(i, dh):
            c = nc - 1 - i
            rows = pl.ds(pl.multiple_of(c * CHUNK, CHUNK), CHUNK)
            xv, bm, cm = x_ref[rows, :], b_ref[rows, :], c_ref[rows, :]
            dtr = dt_ref[rows, :] + dtb_ref[...]
            dt = _softplus(dtr)
            hprev = st_ref[c]
            dy = dy_ref[rows, :]
            g = _mm_nt(cm, bm)
            dts = [_lane_pick(dt, lane, 2 * j + h) for h in range(2)]
            xdt = xv * jnp.where(first, dts[0], dts[1])
            dxs, dhs = [], []
            db_sum, dc_sum = None, None
            ddt_mat = jnp.zeros((CHUNK, LANES), f32)
            da_acc = jnp.zeros((1, LANES), f32)
            for h in range(2):
                mine = first if h == 0 else jnp.logical_not(first)
                cs, decay, total = _head_decay(dts[h], a_heads[h], eye, tril)
                e_cs, e_tot = jnp.exp(cs), jnp.exp(total)
                dec_s = jnp.exp(total - cs)
                dyh = jnp.where(mine, dy, 0.0)
                xdth = jnp.where(mine, xdt, 0.0)
                dhh = jnp.where(mine, dh, 0.0)
                hph = jnp.where(mine, hprev, 0.0)
                m = g * decay
                dm = _mm_nt(dyh, xdth)
                dg = dm * decay
                w = dm * m
                bdec = bm * dec_s
                dxdt = _mm_tn(m, dyh) + _mm(bdec, dhh)
                dc_off = _mm_nt(dyh, hph) * e_cs
                db_s = _mm_nt(xdth, dhh) * dec_s
                dc_h = _mm(dg, bm) + dc_off
                db_h = _mm_tn(dg, cm) + db_s
                r_s = jnp.sum(db_s * bm, axis=1, keepdims=True)
                dtotal = jnp.sum(r_s, axis=0, keepdims=True) + e_tot * jnp.sum(
                    jnp.sum(dhh * hph, axis=1, keepdims=True), axis=0, keepdims=True)
                dcs = (jnp.sum(w, axis=1, keepdims=True) - _to_col(jnp.sum(w, axis=0, keepdims=True), eye)
                       + jnp.sum(dc_off * cm, axis=1, keepdims=True) - r_s + jnp.where(last_row, dtotal, 0.0))
                da_col = jnp.sum(jnp.where(triu, _to_row(dcs, eye), 0.0), axis=1, keepdims=True)
                ddt = da_col * a_heads[h] + jnp.sum(jnp.where(mine, dxdt * xv, 0.0), axis=1, keepdims=True)
                ddt_mat = ddt_mat + jnp.where(lane == 2 * j + h, ddt, 0.0)
                da_acc = da_acc + jnp.where(lane == 2 * j + h, jnp.sum(da_col * dts[h], axis=0, keepdims=True), 0.0)
                dxs.append(dxdt * dts[h])
                dhs.append(e_tot * dhh + _mm_tn(cm * e_cs, dyh))
                db_sum = db_h if db_sum is None else db_sum + db_h
                dc_sum = dc_h if dc_sum is None else dc_sum + dc_h
            dx_ref[rows, :] = jnp.where(first, dxs[0], dxs[1])
            dbc_ref[0, rows, :] += db_sum
            dbc_ref[1, rows, :] += dc_sum
            ddt_ref[rows, :] += ddt_mat * jax.nn.sigmoid(dtr)
            da_ref[...] += da_acc * a_row
            return jnp.where(first, dhs[0], dhs[1])

        _unrolled_loop(nc, chunk, jnp.zeros((STATE_N, LANES), f32))

    blk = lambda f: pl.BlockSpec((t, LANES), f)
    return _call(
        body, name="ssd_bwd", grid=(2, 4),
        out_shape=(jax.ShapeDtypeStruct((t, D), f32), jax.ShapeDtypeStruct((2, t, 2 * STATE_N), f32),
                   jax.ShapeDtypeStruct((t, LANES), f32), jax.ShapeDtypeStruct((1, LANES), f32)),
        in_specs=[blk(lambda g, p: (0, 4 * g + p)), blk(lambda g, p: (0, cb + g)), blk(lambda g, p: (0, cc + g)),
                  blk(lambda g, p: (0, cdt)), _row(LANES), _row(LANES),
                  pl.BlockSpec((nc, STATE_N, LANES), lambda g, p: (0, 0, 4 * g + p)), blk(lambda g, p: (0, 4 * g + p))],
        out_specs=(blk(lambda g, p: (0, 4 * g + p)), pl.BlockSpec((2, t, LANES), lambda g, p: (0, 0, g)),
                   blk(lambda g, p: (0, 0)), _row(LANES)),
        sem=("arbitrary", "arbitrary"), args=(xbc_act, xbc_act, xbc_act, proj, dt_bias_row, a_log_row, states, d_y),
        rider=rider)


def _up_bwd(d_up, w_up, x1, mod, norm2_w, dx2, mix, w_out, rider=None):
    t = x1.shape[0]

    def body(dup_ref, wu_ref, x1_ref, mod_ref, nw_ref, dx2_ref, mix_ref, wo_ref,
             dx1_ref, dmix_ref, dys_ref, du_ref, st_ref):
        @pl.when(pl.program_id(0) == 0)
        def _():
            st_ref[...] = jnp.zeros_like(st_ref)

        nt = (((1,), (1,)), ((), ()))
        dh = None
        for k in range(4):
            lo = (k % 2) * UP_SHARD
            part = lax.dot_general(dup_ref[k // 2, :, lo:lo + UP_SHARD], wu_ref[k], nt, preferred_element_type=f32)
            dh = part if dh is None else dh + part
        x1 = x1_ref[...]
        rstd = lax.rsqrt(jnp.mean(x1 * x1, axis=-1, keepdims=True) + 1e-6)
        xh = x1 * rstd
        nw = nw_ref[...]
        sc = 1.0 + mod_ref[:, 4 * D:5 * D]
        st_ref[0:1, :] += jnp.sum(dh, axis=0, keepdims=True)
        st_ref[1:2, :] += jnp.sum(dh * xh * nw, axis=0, keepdims=True)
        st_ref[2:3, :] += jnp.sum(dh * sc * xh, axis=0, keepdims=True)
        dxh = dh * sc * nw
        dx1 = dx2_ref[...] + rstd * (dxh - xh * jnp.mean(dxh * xh, axis=-1, keepdims=True))
        dx1_ref[...] = dx1
        st_ref[3:4, :] += jnp.sum(dx1 * mix_ref[...], axis=0, keepdims=True)
        dmix = (mod_ref[:, 2 * D:3 * D] * dx1).astype(MX)
        dmix_ref[...] = dmix
        dys_ref[...] = lax.dot_general(dmix, wo_ref[0:D, :], nt, preferred_element_type=f32)
        du_ref[...] = lax.dot_general(dmix, wo_ref[D:2 * D, :], nt, preferred_element_type=f32)

    blk = pl.BlockSpec((TM, D), lambda i: (i, 0))
    return _call(
        body, name="up_bwd", grid=(t // TM,),
        out_shape=(jax.ShapeDtypeStruct((t, D), f32), jax.ShapeDtypeStruct((t, D), MX),
                   jax.ShapeDtypeStruct((t, D), f32), jax.ShapeDtypeStruct((t, D), f32),
                   jax.ShapeDtypeStruct((8, D), f32)),
        in_specs=[pl.BlockSpec((2, TM, D_FF), lambda i: (0, i, 0)), _resident((4, D, UP_SHARD)), blk, _row(6 * D), _row(),
                  blk, blk, _resident((2 * D, D))],
        out_specs=(blk, blk, blk, blk, pl.BlockSpec((8, D), lambda i: (0, 0))),
        sem=("arbitrary",), args=(d_up, w_up, x1, mod, norm2_w, dx2, mix, w_out), rider=rider)


def _ln_silu_bwd(d_u, u_conv, ln_w, ln_b):
    t = d_u.shape[0]

    def body(du_ref, u_ref, w_ref, b_ref, o_ref, st_ref):
        @pl.when(pl.program_id(0) == 0)
        def _():
            st_ref[...] = jnp.zeros_like(st_ref)

        u = u_ref[...]
        mu = jnp.mean(u, axis=-1, keepdims=True)
        uc = u - mu
        rstd = lax.rsqrt(jnp.mean(uc * uc, axis=-1, keepdims=True) + 1e-5)
        n = uc * rstd
        w = w_ref[...]
        dl = du_ref[...] * _dsilu(n * w + b_ref[...])
        st_ref[0:1, :] += jnp.sum(dl * n, axis=0, keepdims=True)
        st_ref[1:2, :] += jnp.sum(dl, axis=0, keepdims=True)
        dn = dl * w
        o_ref[...] = rstd * (dn - jnp.mean(dn, axis=-1, keepdims=True) - n * jnp.mean(dn * n, axis=-1, keepdims=True))

    blk = pl.BlockSpec((TM, D), lambda i: (i, 0))
    return pl.pallas_call(
        body, name="ln_silu_bwd", grid=(t // TM,),
        out_shape=(jax.ShapeDtypeStruct((t, D), f32), jax.ShapeDtypeStruct((8, D), f32)),
        in_specs=[blk, blk, _row(), _row()], out_specs=(blk, pl.BlockSpec((8, D), lambda i: (0, 0))),
        compiler_params=_cp("arbitrary"),
    )(d_u, u_conv, ln_w, ln_b)


def _ssd_gate_norm_bwd(d_out, y_scan, xbc_act, proj, d_skip_row, ssd_norm_w):
    t = d_out.shape[0]

    def body(do_ref, y_ref, xs_ref, z_ref, dsk_ref, nw_ref, dy_ref, dz_ref, st_ref):
        @pl.when(pl.program_id(0) == 0)
        def _():
            st_ref[...] = jnp.zeros_like(st_ref)

        xs = xs_ref[...]
        y = y_ref[...] + xs * dsk_ref[...]
        z = z_ref[...]
        s = _silu(z)
        yz = y * s
        rstd = lax.rsqrt(jnp.mean(yz * yz, axis=-1, keepdims=True) + 1e-6)
        n = yz * rstd
        do = do_ref[...]
        st_ref[0:1, :] += jnp.sum(do * n, axis=0, keepdims=True)
        dn = do * nw_ref[...]
        dyz = rstd * (dn - n * jnp.mean(dn * n, axis=-1, keepdims=True))
        dy = dyz * s
        dy_ref[...] = dy
        dz_ref[...] = (dyz * y * _dsilu(z)).astype(MX)
        st_ref[1:2, :] += jnp.sum(dy * xs, axis=0, keepdims=True)

    blk = pl.BlockSpec((TM, D), lambda i: (i, 0))
    return pl.pallas_call(
        body, name="ssd_gate_norm_bwd", grid=(t // TM,),
        out_shape=(jax.ShapeDtypeStruct((t, D), f32), jax.ShapeDtypeStruct((t, D), MX), jax.ShapeDtypeStruct((8, D), f32)),
        in_specs=[blk, blk, blk, blk, _row(), _row()], out_specs=(blk, blk, pl.BlockSpec((8, D), lambda i: (0, 0))),
        compiler_params=_cp("arbitrary"),
    )(d_out, y_scan, xbc_act, proj, d_skip_row, ssd_norm_w)


def _inproj_bwd(d_z, d_xraw, d_bcraw, d_conf, d_dt, w_pack, x, mod, norm1_w, dx1, after=None):
    t = x.shape[0]
    extra = [] if after is None else [after]

    def body(dz_ref, dx_ref, dbc_ref, dcf_ref, ddt_ref, w_ref, x_ref, mod_ref, nw_ref, dx1_ref, *rest):
        gx_ref, st_ref = rest[-2:]
        @pl.when(pl.program_id(0) == 0)
        def _():
            st_ref[...] = jnp.zeros_like(st_ref)

        nt = (((1,), (1,)), ((), ()))
        dot = lambda a, lo, hi: lax.dot_general(a, w_ref[:, lo:hi], nt, preferred_element_type=f32)
        dh = dot(dz_ref[...], OFF_Z, OFF_Z + D)
        dh = dh + dot(dx_ref[...], OFF_XBC, OFF_XBC + D)
        dh = dh + dot(dbc_ref[...], OFF_XBC + D, OFF_XBC + D_XBC)
        dh = dh + dot(dcf_ref[0], OFF_CA, OFF_CA + D)
        dh = dh + dot(dcf_ref[1], OFF_CG, OFF_CG + D)
        dh = dh + dot(ddt_ref[...].astype(MX), OFF_DT, OFF_DT + LANES)
        st_ref[3:4, 0:LANES] += jnp.sum(ddt_ref[...], axis=0, keepdims=True)
        xv = x_ref[...]
        rstd = lax.rsqrt(jnp.mean(xv * xv, axis=-1, keepdims=True) + 1e-6)
        xh = xv * rstd
        nw = nw_ref[...]
        sc = 1.0 + mod_ref[:, D:2 * D]
        st_ref[0:1, :] += jnp.sum(dh, axis=0, keepdims=True)
        st_ref[1:2, :] += jnp.sum(dh * xh * nw, axis=0, keepdims=True)
        st_ref[2:3, :] += jnp.sum(dh * sc * xh, axis=0, keepdims=True)
        dxh = dh * sc * nw
        gx_ref[...] = dx1_ref[...] + rstd * (dxh - xh * jnp.mean(dxh * xh, axis=-1, keepdims=True))

    blk = pl.BlockSpec((TM, D), lambda i: (i, 0))
    return _call(
        body, name="inproj_bwd", grid=(t // TM,),
        out_shape=(jax.ShapeDtypeStruct((t, D), f32), jax.ShapeDtypeStruct((8, D), f32)),
        in_specs=[blk, blk, pl.BlockSpec((TM, 2 * CW), lambda i: (i, 0)), pl.BlockSpec((2, TM, D), lambda i: (0, i, 0)),
                  pl.BlockSpec((TM, LANES), lambda i: (i, 0)), _resident((D, W_PACK)), blk, _row(6 * D), _row(), blk]
        + [ANY] * len(extra),
        out_specs=(blk, pl.BlockSpec((8, D), lambda i: (0, 0))),
        sem=("arbitrary",), args=(d_z, d_xraw, d_bcraw, d_conf, d_dt, w_pack, x, mod, norm1_w, dx1, *extra))[0]


def _wgrad(a, d, name, bn=256, transposed=True):
    k, t = a.shape if transposed else a.shape[::-1]
    n = d.shape[1]
    out_dtype = MX
    contract = (((1,), (0,)), ((), ())) if transposed else (((0,), (0,)), ((), ()))

    def body(a_ref, d_ref, o_ref):
        o_ref[...] = lax.dot_general(a_ref[...], d_ref[...].astype(MX), contract, preferred_element_type=f32).astype(out_dtype)

    return pl.pallas_call(
        body, name=name, grid=(n // bn,), out_shape=jax.ShapeDtypeStruct((k, n), out_dtype),
        in_specs=[_resident(a.shape), pl.BlockSpec((t, bn), lambda j: (0, j))],
        out_specs=pl.BlockSpec((k, bn), lambda j: (0, j)), compiler_params=_cp("arbitrary"),
    )(a, d)


def _wgrad_stacked(at, d, name, bn):
    out_dtype = MX
    k, t = at.shape
    s, _, n = d.shape
    nb = n // bn

    def body(a_ref, d_ref, o_ref):
        o_ref[0] = jnp.dot(a_ref[...], d_ref[0], preferred_element_type=f32).astype(out_dtype)

    return pl.pallas_call(
        body, name=name, grid=(s, nb), out_shape=jax.ShapeDtypeStruct((s * nb, k, bn), out_dtype),
        in_specs=[_resident((k, t)), pl.BlockSpec((1, t, bn), lambda i, j: (i, 0, j))],
        out_specs=pl.BlockSpec((1, k, bn), lambda i, j: (i * nb + j, 0, 0)), compiler_params=_cp("arbitrary", "arbitrary"),
    )(at, d)


def _pad_row(v, width=LANES):
    return jnp.pad(v.reshape(1, -1), ((0, 0), (0, width - v.size)))


def _quarters(a):
    return a.reshape(4, 2, a.shape[0] // 8, a.shape[1])


def _local_step(x, mod, target, w_pack, late, small, reducer=None):
    dtb_row, alog_row = _pad_row(small["dt_bias"]), _pad_row(small["a_log"])
    dskip_row = jnp.repeat(small["d_skip"].reshape(-1), HEAD_P).reshape(1, D)

    red = reducer

    def hosted(host, args, swap=None, scatter=None, gather=None, sums=()):
        if red is None:
            return host(*args)[0]
        riders = ([red.scatter(scatter)] if scatter else []) + ([red.swap(*swap)] if swap else [])
        riders += [_SwapSumsRider([red.sums[n] for n in sums])] if sums else []
        riders += [_GatherRider([gather[0]], *gather[1:])] if gather is not None else []
        both = _Riders(riders)
        outs, extra = host(*args, rider=both)
        extra = both.split(extra)
        if scatter:
            red.scattered(scatter, extra.pop(0))
        if swap:
            red.swapped(swap[0], extra.pop(0))
        if sums:
            red.others.update(zip(sums, extra.pop(0)))
        return (outs, extra[0][0]) if gather is not None else outs

    w_out, w_up, w_down = late
    if red is None:
        proj, h_t = hosted(_ln_inproj, (x, mod, small["norm1_w"], w_pack))
        xbc_act, = hosted(_ssd_conv_fwd, (proj, small["ssd_conv_w"], small["ssd_conv_b"]))
        y_scan, states = hosted(_ssd_fwd, (xbc_act, proj, dtb_row, alog_row))
        u_conv, = hosted(_glu_conv_fwd, (proj, small["conf_conv_w"], small["conf_conv_b"]))
    else:
        (proj, h_t), w_out = hosted(_ln_inproj, (x, mod, small["norm1_w"], w_pack), gather=(w_out,))
        (xbc_act,), w_up = hosted(_ssd_conv_fwd, (proj, small["ssd_conv_w"], small["ssd_conv_b"]), gather=(w_up, 0, UP_EARLY_ROWS))
        (y_scan, states), w_up = hosted(_ssd_fwd, (xbc_act, proj, dtb_row, alog_row), gather=(w_up, UP_EARLY_ROWS, None))
        (u_conv,), w_down = hosted(_glu_conv_fwd, (proj, small["conf_conv_w"], small["conf_conv_b"]), gather=(w_down,))
        w_out, w_up, w_down = w_out.reshape(2 * D, D), w_up.reshape(4, D, UP_SHARD), w_down.reshape(D_FF, D)
    y_ssd = _ssd_gate_norm(y_scan, xbc_act, proj, dskip_row, small["ssd_norm_w"])
    u = _ln_silu(u_conv, small["conf_ln_w"], small["conf_ln_b"])
    mix, x1, h2_t, up = _outproj_ln2_up(y_ssd, u, w_out, x, mod, small["norm2_w"], w_up)
    act, = _ffn_conv_fwd(up, small["ffn_conv_w"], small["ffn_conv_b"])[0]
    dx2, d_ffn, d_act, st_down = _down_loss(act, w_down, x1, mod, small["final_norm_w"], target)

    g_down = _quarters(_wgrad(act, d_ffn, "wgrad_down", transposed=False))
    d_up, dw_ffn, db_ffn = hosted(_ffn_conv_bwd, (up, small["ffn_conv_w"], small["ffn_conv_b"], d_act), swap=("w_down", g_down))
    g_up = _wgrad_stacked(h2_t, d_up, "wgrad_up", D_FF // 2).reshape(4, 2, D // 2, UP_SHARD)
    dx1, d_mix, d_yssd, d_u, st_up = hosted(_up_bwd, (d_up, w_up, x1, mod, small["norm2_w"], dx2, mix, w_out),
                                            scatter="w_down", swap=("w_up", g_up))
    g_out = _quarters(jnp.concatenate([_wgrad(y_ssd, d_mix, "wgrad_out_y", transposed=False),
                                       _wgrad(u, d_mix, "wgrad_out_u", transposed=False)], axis=0))
    d_uconv, st_ln = _ln_silu_bwd(d_u, u_conv, small["conf_ln_w"], small["conf_ln_b"])
    d_conf, dw_conf, db_conf = hosted(_glu_conv_bwd, (proj, small["conf_conv_w"], d_uconv), scatter="w_up",
                                      swap=("w_out", g_out))
    d_y, d_z, st_gn = _ssd_gate_norm_bwd(d_yssd, y_scan, xbc_act, proj, dskip_row, small["ssd_norm_w"])
    d_xs, d_bc, d_dt, d_alog = hosted(_ssd_bwd, (xbc_act, proj, dtb_row, alog_row, states, d_y), scatter="w_out")
    d_xraw, dw_sx, db_sx = _ssd_conv_bwd_x(proj, small["ssd_conv_w"], small["ssd_conv_b"], d_xs, d_y, dskip_row)
    d_bcraw, dw_sbc, db_sbc = _ssd_conv_bwd_bc(proj, small["ssd_conv_w"], small["ssd_conv_b"], d_bc)
    g_in = _unpack_g_in(dict(
        z=_wgrad(h_t, d_z, "wgrad_in_z"), x=_wgrad(h_t, d_xraw, "wgrad_in_x"), bc=_wgrad(h_t, d_bcraw, "wgrad_in_bc"),
        conf=_wgrad_stacked(h_t, d_conf, "wgrad_in_conf", D), dt=_wgrad(h_t, d_dt, "wgrad_in_dt", bn=LANES)))
    g_in = g_in.reshape(4, 2, D // 2, W_IN_SHARD_PAD)
    args = (d_z, d_xraw, d_bcraw, d_conf, d_dt, w_pack, x, mod, small["norm1_w"], dx1)
    if red is None:
        grad_x, st_in = _inproj_bwd(*args)
    else:
        done = ("w_out", "w_up", "w_down")
        both = _Riders([red.swap("w_in", g_in), _SwapSumsRider([red.sums[n] for n in done])])
        handles, token = _split_start(both, "swap_start_w_in")
        grad_x, st_in = _inproj_bwd(*args, after=token)
        thru, outs = _split_wait(both, "swap_wait_w_in", handles, st_in)
        red.grads["w_in"] = thru[0]
        red.sums.update(zip(done, thru[1:]))
        got, others = both.split(outs)
        red.swapped("w_in", got)
        red.others.update(zip(done, others))

    gsmall = _pack_small_grads(st_in, st_up, st_down, st_ln, st_gn, d_alog, dw_sx, dw_sbc, db_sx, db_sbc, dw_conf, db_conf,
                               dw_ffn, db_ffn)
    gbig = None if reducer is not None else dict(w_in=g_in, w_out=g_out, w_up=g_up, w_down=g_down)
    return st_down[2, 0], grad_x, gbig, gsmall


VECTORS = ("ada_b", "norm1_w", "ssd_conv_b", "dt_bias", "a_log", "d_skip", "ssd_norm_w", "conf_conv_b", "conf_ln_w",
           "conf_ln_b", "norm2_w", "ffn_conv_b", "final_norm_w")
VECTOR_SIZES = (6 * D, D, D_XBC, HEADS, HEADS, HEADS, D, D, D, D, D, 2 * D_FF, D)
CONVS = {"ssd_conv_w": (K_SSD, D_XBC), "conf_conv_w": (K_CONF, D), "ffn_conv_w": (K_FFN, 2 * D_FF)}


def _pack_rows(items):
    n = -(-sum(w for _, w in items) // (8 * LANES)) * LANES
    while True:
        fill, place = [0] * 8, {}
        for key, w in sorted(items, key=lambda kv: -kv[1]):
            rows = [r for r in range(8) if fill[r] + w <= n]
            if not rows:
                break
            place[key] = (rows[0], fill[rows[0]])
            fill[rows[0]] += w
        if len(place) == len(items):
            return n, place
        n += LANES


FRONT_N, FRONT = _pack_rows([("c", D)] + [((nm, j), cols // 4) for nm, (taps, cols) in CONVS.items() for j in range(taps)])
BACK_N, BACK = _pack_rows([(nm, -(-sz // LANES) * LANES) for nm, sz in zip(VECTORS, VECTOR_SIZES)]
                          + [((nm, j), cols) for nm, (taps, cols) in CONVS.items() for j in range(taps)] + [("loss", LANES)])
_VM = pltpu.CompilerParams(vmem_limit_bytes=VMEM_LIMIT)


def _pack_front(c, shards):
    def body(c_ref, *refs):
        o_ref = refs[-1]
        o_ref[...] = jnp.zeros_like(o_ref)
        r, o = FRONT["c"]
        o_ref[r:r + 1, o:o + D] = c_ref[...]
        for ref, (nm, (taps, cols)) in zip(refs, CONVS.items()):
            for j in range(taps):
                r, o = FRONT[(nm, j)]
                o_ref[r:r + 1, o:o + cols // 4] = ref[0, j:j + 1, :]

    return pl.pallas_call(body, name="pack_front", out_shape=jax.ShapeDtypeStruct((8, FRONT_N), f32),
                          compiler_params=_VM)(c, *shards)


def _unpack_front(got):
    def body(g_ref, c_ref, *outs):
        r, o = FRONT["c"]
        for d in range(8):
            c_ref[d:d + 1, :] = g_ref[8 * d + r:8 * d + r + 1, o:o + D]
        for ref, (nm, (taps, cols)) in zip(outs, CONVS.items()):
            cw = cols // 4
            for j in range(taps):
                r, o = FRONT[(nm, j)]
                for k in range(4):
                    ref[j:j + 1, k * cw:(k + 1) * cw] = g_ref[16 * k + r:16 * k + r + 1, o:o + cw]

    return pl.pallas_call(
        body, name="unpack_front", compiler_params=_VM,
        out_shape=(jax.ShapeDtypeStruct((8, D), f32),) + tuple(jax.ShapeDtypeStruct(tc, f32) for tc in CONVS.values()),
    )(got)


def _pack_small_grads(st_in, st_up, st_down, st_ln, st_gn, d_alog, dw_sx, dw_sbc, db_sx, db_sbc, dw_conf, db_conf, dw_ffn,
                      db_ffn):
    def body(in_ref, up_ref, dn_ref, ln_ref, gn_ref, al_ref, wx_ref, wbc_ref, bx_ref, bbc_ref, wc_ref, bc_ref, wf_ref, bf_ref,
             o_ref):
        def put(key, val, shift=0):
            r, o = BACK[key]
            o_ref[r:r + 1, o + shift:o + shift + val.shape[1]] = val

        o_ref[...] = jnp.zeros_like(o_ref)
        for i, piece in enumerate((in_ref[0:1, :], in_ref[1:2, :], up_ref[3:4, :], up_ref[0:1, :], up_ref[1:2, :],
                                   dn_ref[1:2, :])):
            put("ada_b", piece, i * D)
        put("norm1_w", in_ref[2:3, :])
        put("ssd_conv_b", bx_ref[...])
        put("ssd_conv_b", bbc_ref[...], D)
        put("dt_bias", in_ref[3:4, 0:LANES])
        put("a_log", al_ref[...])
        lane = lax.broadcasted_iota(jnp.int32, (1, LANES), 1)
        col = lax.broadcasted_iota(jnp.int32, (1, D), 1)
        per_col = gn_ref[1:2, :]
        d_skip = jnp.zeros((1, LANES), f32)
        for h in range(HEADS):
            in_head = jnp.logical_and(col >= h * HEAD_P, col < (h + 1) * HEAD_P)
            s = jnp.sum(jnp.where(in_head, per_col, 0.0), axis=1, keepdims=True)
            d_skip = d_skip + jnp.where(lane == h, s, 0.0)
        put("d_skip", d_skip)
        put("ssd_norm_w", gn_ref[0:1, :])
        put("conf_conv_b", bc_ref[...])
        put("conf_ln_w", ln_ref[0:1, :])
        put("conf_ln_b", ln_ref[1:2, :])
        put("norm2_w", up_ref[2:3, :])
        put("ffn_conv_b", bf_ref[0])
        put("ffn_conv_b", bf_ref[1], D_FF)
        put("final_norm_w", dn_ref[0:1, :])
        put("loss", dn_ref[2:3, 0:LANES])
        for j in range(K_SSD):
            put(("ssd_conv_w", j), wx_ref[j:j + 1, :])
            put(("ssd_conv_w", j), wbc_ref[j:j + 1, :], D)
        for j in range(K_CONF):
            put(("conf_conv_w", j), wc_ref[j:j + 1, :])
        for j in range(K_FFN):
            put(("ffn_conv_w", j), wf_ref[0, j:j + 1, :])
            put(("ffn_conv_w", j), wf_ref[1, j:j + 1, :], D_FF)

    return pl.pallas_call(body, name="pack_small_grads", out_shape=jax.ShapeDtypeStruct((8, BACK_N), f32), compiler_params=_VM)(
        st_in, st_up, st_down, st_ln, st_gn, d_alog, dw_sx, dw_sbc, db_sx, db_sbc, dw_conf, db_conf, dw_ffn, db_ffn)


def _small_adamw(got, chip, w, m, v):
    names = VECTORS + tuple(CONVS)
    n_par = len(names)

    def body(chip_ref, g_ref, *refs):
        ins, outs = refs[:3 * n_par], refs[3 * n_par:]
        dm_ref, loss_ref, outs = outs[0], outs[1], outs[2:]
        chip_id = chip_ref[0]

        def summed(key, width):
            r, o = BACK[key]
            s = g_ref[r:r + 1, o:o + width]
            for d in range(1, 8):
                s = s + g_ref[8 * d + r:8 * d + r + 1, o:o + width]
            return s

        def mine(full, cw):
            out = full[:, 0:cw]
            for k in range(1, 4):
                out = jnp.where(chip_id == k, full[:, k * cw:(k + 1) * cw], out)
            return out

        r, o = BACK["ada_b"]
        for d in range(8):
            dm_ref[d:d + 1, :] = mine(g_ref[8 * d + r:8 * d + r + 1, o:o + 6 * D], 6 * D // 4)
        loss_ref[...] = summed("loss", LANES)
        for i, (nm, size) in enumerate(zip(VECTORS, VECTOR_SIZES)):
            g = summed(nm, -(-size // LANES) * LANES)[:, 0:size]
            res = _adam_math(ins[3 * i][...], g, ins[3 * i + 1][...], ins[3 * i + 2][...])
            for ref, val in zip(outs[4 * i:4 * i + 4], (g,) + res):
                ref[...] = val
        for i, (nm, (taps, cols)) in enumerate(CONVS.items(), start=len(VECTORS)):
            for j in range(taps):
                g = mine(summed((nm, j), cols), cols // 4)
                res = _adam_math(ins[3 * i][0, j:j + 1, :], g, ins[3 * i + 1][0, j:j + 1, :], ins[3 * i + 2][0, j:j + 1, :])
                for ref, val in zip(outs[4 * i:4 * i + 4], (g,) + res):
                    ref[0, j:j + 1, :] = val

    params = [a[nm] for nm in names for a in (w, m, v)]
    whole = lambda s: pl.BlockSpec(s, lambda i, chip, nd=len(s): (0,) * nd)
    out_shape = [jax.ShapeDtypeStruct((8, 6 * D // 4), f32), jax.ShapeDtypeStruct((1, LANES), f32)]
    out_shape += [jax.ShapeDtypeStruct(w[nm].shape, f32) for nm in names for _ in range(4)]
    outs = pl.pallas_call(
        body, name="small_adamw", out_shape=tuple(out_shape), compiler_params=_VM,
        grid_spec=pltpu.PrefetchScalarGridSpec(
            num_scalar_prefetch=1, grid=(1,), in_specs=[whole(got.shape)] + [whole(p.shape) for p in params],
            out_specs=tuple(whole(s.shape) for s in out_shape)),
    )(_scalar(chip), got, *params)
    return outs[0], outs[1][0, 0], {nm: outs[2 + 4 * i:6 + 4 * i] for i, nm in enumerate(names)}


W_IN_COLS = 4624
W_IN_SHARD = W_IN_COLS // 4
W_IN_SHARD_PAD = 1280
_SEGMENTS = ((0, 1024, OFF_Z), (1024, 2560, OFF_XBC), (2560, 2576, OFF_DT), (2576, 3600, OFF_CA), (3600, 4624, OFF_CG))


def _in_pieces(bounds=()):
    out = []
    for k in range(4):
        s0, s1 = k * W_IN_SHARD, (k + 1) * W_IN_SHARD
        for lo, hi, off in _SEGMENTS:
            a, b = max(lo, s0), min(hi, s1)
            while a < b:
                p = off + a - lo
                e = min([b - a] + [c - p for c in bounds if c > p])
                out.append((k, a - s0, p, e))
                a += e
    return out


def _pack_w_in(shards):
    pieces = _in_pieces()

    def body(s_ref, o_ref):
        o_ref[:, OFF_DT:W_PACK] = jnp.zeros((TM, W_PACK - OFF_DT), MX)
        for k, c, p, n in pieces:
            o_ref[:, p:p + n] = s_ref[k, :, c:c + n]

    return pl.pallas_call(
        body, name="pack_w_in", grid=(D // TM,), out_shape=jax.ShapeDtypeStruct((D, W_PACK), MX),
        in_specs=[pl.BlockSpec((4, TM, W_IN_SHARD_PAD), lambda i: (0, i, 0))],
        out_specs=pl.BlockSpec((TM, W_PACK), lambda i: (i, 0)), compiler_params=_cp("arbitrary"),
    )(shards)


def _unpack_g_in(g):
    srcs = ((OFF_Z, D), (OFF_XBC, D), (OFF_XBC + D, 2 * CW), (OFF_CA, D), (OFF_CG, D), (OFF_DT, LANES))
    pieces = _in_pieces(tuple(o for o, _ in srcs) + tuple(o + n for o, n in srcs))

    def body(z_ref, x_ref, bc_ref, cf_ref, dt_ref, o_ref):
        read = (lambda lo, hi: z_ref[:, lo:hi], lambda lo, hi: x_ref[:, lo:hi], lambda lo, hi: bc_ref[:, lo:hi],
                lambda lo, hi: cf_ref[0, :, lo:hi], lambda lo, hi: cf_ref[1, :, lo:hi], lambda lo, hi: dt_ref[:, lo:hi])
        o_ref[:, :, W_IN_SHARD - 4:W_IN_SHARD_PAD] = jnp.zeros((4, TM, W_IN_SHARD_PAD - W_IN_SHARD + 4), MX)
        for k, c, p, n in pieces:
            i = [q for q, (o, w) in enumerate(srcs) if o <= p < o + w][0]
            o_ref[k, :, c:c + n] = read[i](p - srcs[i][0], p - srcs[i][0] + n)

    blk = lambda w: pl.BlockSpec((TM, w), lambda i: (i, 0))
    return pl.pallas_call(
        body, name="unpack_g_in", grid=(D // TM,), out_shape=jax.ShapeDtypeStruct((4, D, W_IN_SHARD_PAD), MX),
        in_specs=[blk(D), blk(D), blk(2 * CW), pl.BlockSpec((2, TM, D), lambda i: (0, i, 0)), blk(LANES)],
        out_specs=pl.BlockSpec((4, TM, W_IN_SHARD_PAD), lambda i: (0, i, 0)), compiler_params=_cp("arbitrary"),
    )(g["z"], g["x"], g["bc"], g["conf"], g["dt"])


def _scalar(v):
    return jnp.reshape(v, (1,)).astype(jnp.int32)


def _cast_into_slot(w, width, chip):
    r, c = w.shape
    h = r // 2
    tm = _row_tile(h)
    nj = h // tm

    def body(chip_ref, w_ref, o_ref):
        v = w_ref[...].astype(MX)
        o_ref[0, 0] = v if width == c else jnp.concatenate([v, jnp.zeros((tm, width - c), MX)], axis=1)

    return pl.pallas_call(
        body, name=f"cast_into_slot_{r}x{c}", out_shape=jax.ShapeDtypeStruct((4, 2, h, width), MX),
        grid_spec=pltpu.PrefetchScalarGridSpec(
            num_scalar_prefetch=1, grid=(2, nj),
            in_specs=[pl.BlockSpec((tm, c), lambda i, j, chip: (i * nj + j, 0))],
            out_specs=pl.BlockSpec((1, 1, tm, width), lambda i, j, chip: (chip[0], i, j, 0))),
        compiler_params=_cp("arbitrary", "arbitrary"),
    )(_scalar(chip), w)


def _columns_first(w):
    return jnp.transpose(w, (2, 0, 1))


def _cast_into_slot_w_in(w_t, chip):
    h = D // 2
    nj = h // TM
    pad = W_IN_SHARD_PAD - W_IN_SHARD

    def body(chip_ref, w_ref, o_ref):
        cols = jnp.concatenate([w_ref[:, 0, :], jnp.zeros((pad, TM), f32)], axis=0)
        o_ref[0, 0] = cols.T.astype(MX)

    return pl.pallas_call(
        body, name="cast_into_slot_w_in", out_shape=jax.ShapeDtypeStruct((4, 2, h, W_IN_SHARD_PAD), MX),
        grid_spec=pltpu.PrefetchScalarGridSpec(
            num_scalar_prefetch=1, grid=(2, nj),
            in_specs=[pl.BlockSpec((W_IN_SHARD, 1, TM), lambda i, j, chip: (0, 0, i * nj + j))],
            out_specs=pl.BlockSpec((1, 1, TM, W_IN_SHARD_PAD), lambda i, j, chip: (chip[0], i, j, 0))),
        compiler_params=_cp("arbitrary", "arbitrary"),
    )(_scalar(chip), w_t)


def _adamw_w_in(w_t, mine, other, m_t, v_t, core):
    h = D // 2
    nj = h // TM

    def body(core_ref, w_ref, a_ref, b_ref, m_ref, v_ref, g_ref, d_ref, nm_ref, nv_ref):
        g = jnp.where(pl.program_id(0) == core_ref[0], a_ref[...], b_ref[...]).T[0:W_IN_SHARD, :]
        g_ref[:, 0, :] = g
        d_ref[:, 0, :], nm_ref[:, 0, :], nv_ref[:, 0, :] = _adam_math(w_ref[:, 0, :], g, m_ref[:, 0, :], v_ref[:, 0, :])

    blk = pl.BlockSpec((W_IN_SHARD, 1, TM), lambda i, j, core: (0, 0, i * nj + j))
    gblk = pl.BlockSpec((TM, W_IN_SHARD_PAD), lambda i, j, core: (j, 0))
    return pl.pallas_call(
        body, name="adamw_w_in", out_shape=tuple([jax.ShapeDtypeStruct((W_IN_SHARD, 1, D), f32)] * 4),
        grid_spec=pltpu.PrefetchScalarGridSpec(
            num_scalar_prefetch=1, grid=(2, nj), in_specs=[blk, gblk, gblk, blk, blk], out_specs=(blk,) * 4),
        compiler_params=_cp("arbitrary", "arbitrary"),
    )(_scalar(core), w_t, mine, other, m_t, v_t)


ANY = pl.BlockSpec(memory_space=pl.ANY)


def _place():
    x, y, c = lax.axis_index("x"), lax.axis_index("y"), lax.axis_index("c")
    return x, y, c, [(1 - x, y), (x, 1 - y), (1 - x, 1 - y)]


_GATHER_SEMS = [pltpu.SemaphoreType.DMA((7,)), pltpu.SemaphoreType.DMA((7,)), pltpu.SemaphoreType.DMA]


def _gather_rows_steps(x_ref, out_ref, send_sems, recv_sems, local_sem, after_first=None):
    m_per = x_ref.shape[0]
    x, y, c, chips = _place()
    me, sibling = (x, y, c), (x, y, 1 - c)

    def rows(px, py, pc):
        return out_ref.at[pl.ds((4 * px + 2 * py + pc) * m_per, m_per), :]

    def copy(k, blk, to, src=None):
        return pltpu.make_async_remote_copy(
            src_ref=rows(*blk) if src is None else src, dst_ref=rows(*blk), send_sem=send_sems.at[k],
            recv_sem=recv_sems.at[k], device_id=to, device_id_type=MESH)

    mine = pltpu.make_async_copy(x_ref, rows(*me), local_sem)
    mine.start()
    first = [copy(0, me, sibling, src=x_ref)]
    first += [copy(1 + j, me, (*chip, c), src=x_ref) for j, chip in enumerate(chips)]
    for cp in first:
        cp.start()
    if after_first is not None:
        after_first()
    passed = [copy(4 + j, (*chip, c), sibling) for j, chip in enumerate(chips)]
    for j, chip in enumerate(chips):
        copy(1 + j, (*chip, c), me).wait_recv()
        passed[j].start()
    copy(0, sibling, me).wait_recv()
    for j, chip in enumerate(chips):
        copy(4 + j, (*chip, 1 - c), me).wait_recv()
    for cp in first + passed:
        cp.wait_send()
    mine.wait()


def _gather_rows(block):
    m_per, n = block.shape

    def body(x_ref, out_ref, send_sems, recv_sems, local_sem):
        _gather_rows_steps(x_ref, out_ref, send_sems, recv_sems, local_sem)

    vmem = pl.BlockSpec(memory_space=pltpu.VMEM)
    return pl.pallas_call(
        body, name=f"gather_rows_{m_per}x{n}", out_shape=jax.ShapeDtypeStruct((8 * m_per, n), block.dtype),
        in_specs=[vmem], out_specs=vmem, scratch_shapes=list(_GATHER_SEMS), compiler_params=_VM)(block)


def _front(block, ada_w, slot):
    rider = _GatherRider([slot])
    n_mod = ada_w.shape[1]

    def body(x_ref, w_ref, slot_ref, out_ref, mod_ref, slot_out, c_scr, mine_scr, *sems):
        r_scr = sems[6:]
        _gather_rows_steps(x_ref, out_ref, *sems[0:3], after_first=lambda: rider.start([slot_ref], [slot_out], r_scr))
        r, o = FRONT["c"]
        for d in range(8):
            c_scr[d:d + 1, :] = out_ref[8 * d + r:8 * d + r + 1, o:o + D]
        mine_scr[...] = jnp.dot(_silu(c_scr[...]).astype(MX), w_ref[...].astype(MX), preferred_element_type=f32)
        _gather_rows_steps(mine_scr, mod_ref, *sems[3:6])
        rider.finish([slot_ref], [slot_out], r_scr)

    vmem = pl.BlockSpec(memory_space=pltpu.VMEM)
    return pl.pallas_call(
        body, name="front",
        out_shape=(jax.ShapeDtypeStruct((64, block.shape[1]), f32), jax.ShapeDtypeStruct((64, n_mod), f32), rider.out_shape[0]),
        in_specs=[vmem, vmem, ANY], out_specs=(vmem, vmem, ANY), input_output_aliases={2: 2},
        scratch_shapes=[pltpu.VMEM((8, D), f32), pltpu.VMEM((8, n_mod), f32)] + list(_GATHER_SEMS) * 2 + list(rider.scratch),
        compiler_params=_VM,
    )(block, ada_w, slot)


class _GatherRider:
    def __init__(self, slots, row0=0, nrows=None):
        n = len(slots)
        self.n = n
        self.rows = (row0, slots[0].shape[2] - row0 if nrows is None else nrows)
        self.inputs = list(slots)
        self.out_shape = [jax.ShapeDtypeStruct(s.shape, s.dtype) for s in slots]
        self.scratch = [pltpu.SemaphoreType.DMA((n, 6)), pltpu.SemaphoreType.DMA((n, 6))]
        self.aliases = {a: a for a in range(n)}

    def _copy(self, outs, sems, a, j, k, half, to):
        dst = outs[a].at[k, half, pl.ds(*self.rows)]
        return pltpu.make_async_remote_copy(src_ref=dst, dst_ref=dst, send_sem=sems[0].at[a, j], recv_sem=sems[1].at[a, j],
                                            device_id=to, device_id_type=MESH)

    def _first(self, outs, sems):
        x, y, c, chips = _place()
        return [self._copy(outs, sems, a, j, 2 * x + y, c, (*chip, c)) for a in range(self.n) for j, chip in enumerate(chips)]

    def start(self, ins, outs, sems):
        for cp in self._first(outs, sems):
            cp.start()

    def finish(self, ins, outs, sems):
        x, y, c, chips = _place()
        passed = []
        for a in range(self.n):
            for j, (px, py) in enumerate(chips):
                self._copy(outs, sems, a, j, 2 * px + py, c, (x, y, c)).wait_recv()
                fwd = self._copy(outs, sems, a, 3 + j, 2 * px + py, c, (x, y, 1 - c))
                fwd.start()
                passed.append(fwd)
        for a in range(self.n):
            for j, (px, py) in enumerate(chips):
                self._copy(outs, sems, a, 3 + j, 2 * px + py, 1 - c, (x, y, c)).wait_recv()
        for cp in self._first(outs, sems) + passed:
            cp.wait_send()


class _ScatterRider:
    def __init__(self, parts):
        n = len(parts)
        self.n = n
        self.inputs = list(parts)
        self.out_shape = [jax.ShapeDtypeStruct((3,) + p.shape[1:], p.dtype) for p in parts]
        self.scratch = [pltpu.SemaphoreType.DMA((3 * n,)), pltpu.SemaphoreType.DMA((3 * n,))]
        self.aliases = {}

    def _copies(self, ins, outs, sems):
        x, y, c, chips = _place()
        return [pltpu.make_async_remote_copy(
            src_ref=ins[a].at[2 * px + py], dst_ref=outs[a].at[j], send_sem=sems[0].at[3 * a + j],
            recv_sem=sems[1].at[3 * a + j], device_id=(px, py, c), device_id_type=MESH)
            for a in range(self.n) for j, (px, py) in enumerate(chips)]

    def start(self, ins, outs, sems):
        for cp in self._copies(ins, outs, sems):
            cp.start()

    def finish(self, ins, outs, sems):
        for cp in self._copies(ins, outs, sems):
            cp.wait()


HBM = pl.BlockSpec(memory_space=pltpu.HBM)
SEM = pl.BlockSpec(memory_space=pltpu.SEMAPHORE)
EFFECT = pltpu.SideEffectType.DATAFLOW_SIDE_EFFECTING


def _split_start(rider, name, after=None):
    ni, no, ns = len(rider.inputs), len(rider.out_shape), len(rider.scratch)
    extra = [] if after is None else [after]

    def body(*refs):
        ins, lands = refs[:ni], refs[ni:ni + no]
        sems = refs[ni + no + len(extra):ni + no + len(extra) + ns]
        rider.start(ins, lands, sems)
        refs[-1][...] = jnp.zeros_like(refs[-1])

    bufs = list(rider.inputs) + [lax.empty(s.shape, s.dtype) for s in rider.out_shape]
    outs = pl.pallas_call(
        body, name=name,
        out_shape=tuple(rider.scratch) + tuple(pltpu.HBM(b.shape, b.dtype) for b in bufs) + (jax.ShapeDtypeStruct((8, LANES), f32),),
        in_specs=[HBM] * (ni + no) + [ANY] * len(extra),
        out_specs=(SEM,) * ns + (HBM,) * (ni + no) + (pl.BlockSpec(memory_space=pltpu.VMEM),),
        input_output_aliases={i: ns + i for i in range(ni + no)},
        compiler_params=pltpu.CompilerParams(has_side_effects=EFFECT),
    )(*[pltpu.with_memory_space_constraint(b, pltpu.HBM) for b in bufs], *extra)
    return outs[:-1], outs[-1]


def _split_wait(rider, name, handles, after):
    ni, no, ns = len(rider.inputs), len(rider.out_shape), len(rider.scratch)
    sems, bufs = handles[:ns], handles[ns:]

    def body(*refs):
        rider.finish(refs[:ni], refs[ni:ni + no], refs[ni + no:ni + no + ns])

    outs = pl.pallas_call(
        body, name=name, out_shape=tuple(pltpu.HBM(b.shape, b.dtype) for b in bufs),
        in_specs=[HBM] * (ni + no) + [SEM] * ns + [ANY], out_specs=(HBM,) * (ni + no),
        input_output_aliases={i: i for i in range(ni + no)}, compiler_params=pltpu.CompilerParams(has_side_effects=EFFECT),
    )(*bufs, *sems, after)
    return outs[:ni], outs[ni:]


def _ride_alone(rider, name):
    n = len(rider.inputs)

    def body(*refs):
        ins, outs, sems = refs[:n], refs[n:n + len(rider.out_shape)], refs[n + len(rider.out_shape):]
        rider.start(ins, outs, sems)
        rider.finish(ins, outs, sems)

    return pl.pallas_call(
        body, name=name, out_shape=tuple(rider.out_shape), in_specs=[ANY] * n, out_specs=tuple([ANY] * len(rider.out_shape)),
        input_output_aliases=dict(rider.aliases), scratch_shapes=list(rider.scratch),
    )(*rider.inputs)


class _SwapRider:
    def __init__(self, grads):
        n = len(grads)
        self.n = n
        self.inputs = list(grads)
        self.out_shape = [jax.ShapeDtypeStruct((4,) + g.shape[2:], g.dtype) for g in grads]
        self.scratch = [pltpu.SemaphoreType.DMA((4 * n,)), pltpu.SemaphoreType.DMA((4 * n,))]
        self.aliases = {}

    def _copies(self, ins, outs, sems):
        x, y, c, _ = _place()
        return [pltpu.make_async_remote_copy(
            src_ref=ins[a].at[k, 1 - c], dst_ref=outs[a].at[k], send_sem=sems[0].at[4 * a + k], recv_sem=sems[1].at[4 * a + k],
            device_id=(x, y, 1 - c), device_id_type=MESH) for a in range(self.n) for k in range(4)]

    def start(self, ins, outs, sems):
        for cp in self._copies(ins, outs, sems):
            cp.start()

    def finish(self, ins, outs, sems):
        for cp in self._copies(ins, outs, sems):
            cp.wait()


class _Riders:
    def __init__(self, riders):
        self.riders = list(riders)
        self.inputs = [a for r in riders for a in r.inputs]
        self.out_shape = [s for r in riders for s in r.out_shape]
        self.scratch = [s for r in riders for s in r.scratch]
        self.aliases = {}
        i = o = 0
        for r in riders:
            self.aliases.update({i + a: o + b for a, b in r.aliases.items()})
            i, o = i + len(r.inputs), o + len(r.out_shape)

    def _each(self, ins, outs, sems):
        i = o = s = 0
        for r in self.riders:
            yield r, ins[i:i + len(r.inputs)], outs[o:o + len(r.out_shape)], sems[s:s + len(r.scratch)]
            i, o, s = i + len(r.inputs), o + len(r.out_shape), s + len(r.scratch)

    def start(self, ins, outs, sems):
        for r, a, b, c in self._each(ins, outs, sems):
            r.start(a, b, c)

    def finish(self, ins, outs, sems):
        for r, a, b, c in self._each(ins, outs, sems):
            r.finish(a, b, c)

    def split(self, outs):
        res, o = [], 0
        for r in self.riders:
            res.append(outs[o:o + len(r.out_shape)])
            o += len(r.out_shape)
        return res


class _Reducer:
    def __init__(self, chip, core):
        self.chip, self.core, self.grads, self.parts, self.sums, self.others = chip, core, {}, {}, {}, {}

    def swap(self, name, grad):
        self.grads[name] = grad
        return _SwapRider([grad])

    def swapped(self, name, got):
        self.parts[name] = _add_pair(self.grads[name], got[0], self.core, name)

    def scatter(self, name):
        return _ScatterRider([self.parts[name]])

    def scattered(self, name, others):
        self.sums[name] = _add_chips(self.parts[name], others[0], self.chip, name)


class _SwapSumsRider:
    def __init__(self, halves):
        n = len(halves)
        self.n = n
        self.inputs = list(halves)
        self.out_shape = [jax.ShapeDtypeStruct(s.shape, s.dtype) for s in halves]
        self.scratch = [pltpu.SemaphoreType.DMA((n,)), pltpu.SemaphoreType.DMA((n,))]
        self.aliases = {}

    def _copies(self, ins, outs, sems):
        x, y, c, _ = _place()
        return [pltpu.make_async_remote_copy(
            src_ref=ins[a], dst_ref=outs[a], send_sem=sems[0].at[a], recv_sem=sems[1].at[a],
            device_id=(x, y, 1 - c), device_id_type=MESH) for a in range(self.n)]

    def start(self, ins, outs, sems):
        for cp in self._copies(ins, outs, sems):
            cp.start()

    def finish(self, ins, outs, sems):
        for cp in self._copies(ins, outs, sems):
            cp.wait()


def _row_tile(r):
    for tm in (TM, 176, 128, 64, 32, 16, 8):
        if r % tm == 0:
            return tm
    return r


def _add_pair(mine, got, core, name):
    k, _, h, c = mine.shape
    tm = _row_tile(h)

    def body(core_ref, a_ref, b_ref, o_ref):
        o_ref[0] = (a_ref[0, 0].astype(f32) + b_ref[0].astype(f32)).astype(MX)

    blk = pl.BlockSpec((1, tm, c), lambda i, j, core: (i, j, 0))
    return pl.pallas_call(
        body, name="add_pair_" + name, out_shape=jax.ShapeDtypeStruct((k, h, c), MX),
        grid_spec=pltpu.PrefetchScalarGridSpec(
            num_scalar_prefetch=1, grid=(k, h // tm),
            in_specs=[pl.BlockSpec((1, 1, tm, c), lambda i, j, core: (i, core[0], j, 0)), blk], out_specs=blk),
        compiler_params=_cp("arbitrary", "arbitrary"),
    )(_scalar(core), mine, got)


def _add_chips(parts, others, chip, name):
    _, n, c = others.shape
    tm = _row_tile(n)

    def body(chip_ref, a_ref, b_ref, o_ref):
        s = a_ref[0].astype(f32) + b_ref[0].astype(f32)
        o_ref[...] = (s + b_ref[1].astype(f32)) + b_ref[2].astype(f32)

    return pl.pallas_call(
        body, name="add_chips_" + name, out_shape=jax.ShapeDtypeStruct((n, c), f32),
        grid_spec=pltpu.PrefetchScalarGridSpec(
            num_scalar_prefetch=1, grid=(n // tm,),
            in_specs=[pl.BlockSpec((1, tm, c), lambda i, chip: (chip[0], i, 0)),
                      pl.BlockSpec((3, tm, c), lambda i, chip: (0, i, 0))],
            out_specs=pl.BlockSpec((tm, c), lambda i, chip: (i, 0))),
        compiler_params=_cp("arbitrary"),
    )(_scalar(chip), parts, others)


def _adam_math(w, g, m, v):
    m = ADAM_B1 * m + (1.0 - ADAM_B1) * g
    v = ADAM_B2 * v + (1.0 - ADAM_B2) * (g * g)
    m_hat = m / (1.0 - ADAM_B1 ** ADAM_STEP)
    v_hat = v / (1.0 - ADAM_B2 ** ADAM_STEP)
    return -ADAM_LR * (m_hat / (jnp.sqrt(v_hat) + ADAM_EPS) + ADAM_WD * w), m, v


def _adamw_halves(w, mine, other, m, v, core, name, after):
    r, c = w.shape
    h = r // 2
    tm = _row_tile(h)
    nj = h // tm
    cg = mine.shape[1]

    def body(core_ref, w_ref, a_ref, b_ref, m_ref, v_ref, after_ref, g_ref, d_ref, nm_ref, nv_ref):
        g = jnp.where(pl.program_id(0) == core_ref[0], a_ref[:, 0:c], b_ref[:, 0:c])
        g_ref[...] = g
        d_ref[...], nm_ref[...], nv_ref[...] = _adam_math(w_ref[...], g, m_ref[...], v_ref[...])

    blk = pl.BlockSpec((tm, c), lambda i, j, core: (i * nj + j, 0))
    gblk = pl.BlockSpec((tm, cg), lambda i, j, core: (j, 0))
    return _call(body, name=name, grid=(2, nj), out_shape=[jax.ShapeDtypeStruct((r, c), f32)] * 4,
                 in_specs=[blk, gblk, gblk, blk, blk, ANY], out_specs=(blk,) * 4, sem=("arbitrary", "arbitrary"),
                 prefetch=(_scalar(core),), args=(w, mine, other, m, v, after))[0]


def _ada_adamw(c_all_t, d_mod, w, m, v, after):
    r, c = w.shape
    tm = TM

    def body(ct_ref, dm_ref, w_ref, m_ref, v_ref, after_ref, g_ref, d_ref, nm_ref, nv_ref):
        ca = _silu(ct_ref[...])
        g = ca[:, 0:1] * dm_ref[0:1, :]
        for b in range(1, 8):
            g = g + ca[:, b:b + 1] * dm_ref[b:b + 1, :]
        g_ref[...] = g
        d_ref[...], nm_ref[...], nv_ref[...] = _adam_math(w_ref[...], g, m_ref[...], v_ref[...])

    blk = pl.BlockSpec((tm, c), lambda i: (i, 0))
    return _call(body, name="ada_adamw", grid=(r // tm,), out_shape=[jax.ShapeDtypeStruct((r, c), f32)] * 4,
                 in_specs=[pl.BlockSpec((tm, 8), lambda i: (i, 0)), pl.BlockSpec((8, c), lambda i: (0, 0)), blk, blk, blk, ANY],
                 out_specs=(blk,) * 4, sem=("arbitrary",), args=(c_all_t, d_mod, w, m, v, after))[0]


WEIGHTS = ("ada_w", "ada_b", "norm1_w", "w_in", "ssd_conv_w", "ssd_conv_b", "dt_bias", "a_log", "d_skip", "ssd_norm_w",
           "conf_conv_w", "conf_conv_b", "conf_ln_w", "conf_ln_b", "w_out", "norm2_w", "w_up", "ffn_conv_w", "ffn_conv_b",
           "w_down", "final_norm_w")


def kernel(x, c, ada_w, ada_b, norm1_w, w_in, ssd_conv_w, ssd_conv_b, dt_bias, a_log, d_skip, ssd_norm_w, conf_conv_w, conf_conv_b, conf_ln_w, conf_ln_b, w_out, norm2_w, w_up, ffn_conv_w, ffn_conv_b, w_down, final_norm_w, loss_target, m_ada_w, m_ada_b, m_norm1_w, m_w_in, m_ssd_conv_w, m_ssd_conv_b, m_dt_bias, m_a_log, m_d_skip, m_ssd_norm_w, m_conf_conv_w, m_conf_conv_b, m_conf_ln_w, m_conf_ln_b, m_w_out, m_norm2_w, m_w_up, m_ffn_conv_w, m_ffn_conv_b, m_w_down, m_final_norm_w, v_ada_w, v_ada_b, v_norm1_w, v_w_in, v_ssd_conv_w, v_ssd_conv_b, v_dt_bias, v_a_log, v_d_skip, v_ssd_norm_w, v_conf_conv_w, v_conf_conv_b, v_conf_ln_w, v_conf_ln_b, v_w_out, v_norm2_w, v_w_up, v_ffn_conv_w, v_ffn_conv_b, v_w_down, v_final_norm_w):
    given = dict(locals())
    w = {n: given[n] for n in WEIGHTS}
    mom = {n: given["m_" + n] for n in WEIGHTS}
    var = {n: given["v_" + n] for n in WEIGHTS}
    chip = 2 * lax.axis_index("x") + lax.axis_index("y")
    me = 2 * chip + lax.axis_index("c")

    core = lax.axis_index("c")
    got, mod_cols, a_in = _front(_pack_front(c, [w[n] for n in CONVS]), ada_w[0], _cast_into_slot_w_in(_columns_first(w_in), chip))
    c_all, *convs = _unpack_front(got)
    conv_full = dict(zip(CONVS, convs))
    mod_cols = mod_cols.reshape(8, 8, -1)[0::2]
    mod = lax.dynamic_index_in_dim(mod_cols, me, axis=1, keepdims=False).reshape(1, 6 * D) + ada_b
    w_pack = _pack_w_in(a_in.reshape(4, D, W_IN_SHARD_PAD))
    late = (_cast_into_slot(w_out[0], D, chip), _cast_into_slot(w_up[0], UP_SHARD, chip), _cast_into_slot(w_down[0], D, chip))

    flat = lambda a: a.reshape(1, -1) if a.ndim == 1 else a
    small = {n: flat(w[n]) for n in VECTORS if n != "ada_b"}
    small.update(conv_full)
    reducer = _Reducer(chip, core)
    _, grad_x, _, gsmall = _local_step(x[0], mod, loss_target[0], w_pack, late, small, reducer)
    grads, delta, new_m, new_v = {}, {}, {}, {}

    names = VECTORS + tuple(CONVS)
    d_mod_mine, loss, res = _small_adamw(_gather_rows(gsmall), chip, *[{n: flat(d[n]) for n in names} for d in (w, mom, var)])
    for n in names:
        grads[n], delta[n], new_m[n], new_v[n] = [r.reshape(w[n].shape) for r in res[n]]

    scatter = reducer.scatter("w_in")
    handles, token = _split_start(scatter, "scatter_start_w_in", after=d_mod_mine)
    for n in ("w_up", "w_down", "w_out"):
        res = _adamw_halves(w[n][0], reducer.sums[n], reducer.others[n], mom[n][0], var[n][0], core, "adamw_" + n, token)
        grads[n], delta[n], new_m[n], new_v[n] = [r[None] for r in res]
    res = _ada_adamw(c_all.T, d_mod_mine, ada_w[0], m_ada_w[0], v_ada_w[0], token)
    grads["ada_w"], delta["ada_w"], new_m["ada_w"], new_v["ada_w"] = [r[None] for r in res]
    (reducer.parts["w_in"],), others = _split_wait(scatter, "scatter_wait_w_in", handles, res[1])
    reducer.scattered("w_in", others)
    reducer.others["w_in"], = _ride_alone(_SwapSumsRider([reducer.sums["w_in"]]), "swap_sums_w_in")
    res = _adamw_w_in(_columns_first(w_in), reducer.sums["w_in"], reducer.others["w_in"], _columns_first(m_w_in),
                      _columns_first(v_w_in), core)
    grads["w_in"], delta["w_in"], new_m["w_in"], new_v["w_in"] = [jnp.transpose(r, (1, 2, 0)) for r in res]

    return (loss, grad_x[None], *[grads[n] for n in WEIGHTS], *[delta[n] for n in WEIGHTS],
            *[new_m[n] for n in WEIGHTS], *[new_v[n] for n in WEIGHTS])
```

```python
import functools

import jax
import jax.numpy as jnp
from jax import lax
from jax.experimental import pallas as pl
from jax.experimental.pallas import tpu as pltpu

f32 = jnp.float32
MX = jnp.bfloat16

D = 1024
HEADS = 16
HEAD_P = 64
STATE_N = 128
D_XBC = 1536
D_FF = 2816
UP_SHARD = 2 * D_FF // 4
UP_EARLY_ROWS = 128
K_SSD, K_CONF, K_FFN = 4, 31, 3
CHUNK = 128
OFF_Z, OFF_XBC, OFF_CA, OFF_CG, OFF_DT = 0, 1024, 2560, 3584, 4608
W_PACK = 4736
TM = 256
CW = 256
RC = 64
LANES = 128
VMEM_LIMIT = 56 * 1024 * 1024

ADAM_LR, ADAM_B1, ADAM_B2, ADAM_EPS, ADAM_WD, ADAM_STEP = 0.001, 0.9, 0.999, 1e-08, 0.01, 10

MESH = pl.DeviceIdType.MESH


def _cp(*sem):
    return pltpu.CompilerParams(dimension_semantics=sem, vmem_limit_bytes=VMEM_LIMIT)


def _resident(shape):
    nd = len(shape)
    return pl.BlockSpec(shape, lambda *_: (0,) * nd, pipeline_mode=pl.Buffered(1))


def _row(width=D):
    return pl.BlockSpec((1, width), lambda *_: (0, 0))


def _call(body, *, name, grid, in_specs, out_specs, out_shape, args, sem, scratch_shapes=(), prefetch=(), rider=None):
    ni, no, ns, npf = len(in_specs), len(out_specs), len(scratch_shapes), len(prefetch)
    ri, ro = (len(rider.inputs), len(rider.out_shape)) if rider is not None else (0, 0)

    def full(*refs):
        pre, refs = refs[:npf], refs[npf:]
        base_in, r_in = refs[:ni], refs[ni:ni + ri]
        base_out, r_out = refs[ni + ri:ni + ri + no], refs[ni + ri + no:ni + ri + no + ro]
        base_scr, r_scr = refs[ni + ri + no + ro:ni + ri + no + ro + ns], refs[ni + ri + no + ro + ns:]
        if rider is None:
            return body(*pre, *base_in, *base_out, *base_scr)
        ids = [pl.program_id(a) for a in range(len(grid))]
        first = functools.reduce(jnp.logical_and, [i == 0 for i in ids])
        last = functools.reduce(jnp.logical_and, [i == g - 1 for i, g in zip(ids, grid)])

        @pl.when(first)
        def _():
            rider.start(r_in, r_out, r_scr)

        body(*pre, *base_in, *base_out, *base_scr)

        @pl.when(last)
        def _():
            rider.finish(r_in, r_out, r_scr)

    extra = dict(shapes=[], scratch=[], aliases={}, inputs=[]) if rider is None else dict(
        shapes=rider.out_shape, scratch=rider.scratch, inputs=rider.inputs,
        aliases={npf + ni + i: no + j for i, j in rider.aliases.items()})
    outs = pl.pallas_call(
        full, name=name, out_shape=tuple(out_shape) + tuple(extra["shapes"]), input_output_aliases=extra["aliases"],
        grid_spec=pltpu.PrefetchScalarGridSpec(
            num_scalar_prefetch=npf, grid=grid, in_specs=list(in_specs) + [ANY] * ri,
            out_specs=tuple(out_specs) + (ANY,) * ro, scratch_shapes=list(scratch_shapes) + list(extra["scratch"])),
        compiler_params=_cp(*sem),
    )(*prefetch, *args, *extra["inputs"])
    return tuple(outs[:no]), tuple(outs[no:])


def _silu(v):
    return v * jax.nn.sigmoid(v)


def _dsilu(v):
    s = jax.nn.sigmoid(v)
    return s * (1.0 + v * (1.0 - s))


def _softplus(v):
    return jnp.maximum(v, 0.0) + jnp.log1p(jnp.exp(-jnp.abs(v)))


def _mm(a, b):
    return jnp.dot(a.astype(MX), b.astype(MX), preferred_element_type=f32)


def _mm_nt(a, b):
    return lax.dot_general(a.astype(MX), b.astype(MX), (((1,), (1,)), ((), ())), preferred_element_type=f32)


def _mm_tn(a, b):
    return lax.dot_general(a.astype(MX), b.astype(MX), (((0,), (0,)), ((), ())), preferred_element_type=f32)


def _ln_inproj(x, mod, norm1_w, w_pack, rider=None):
    t = x.shape[0]

    def body(x_ref, mod_ref, nw_ref, w_ref, proj_ref, ht_ref):
        xv = x_ref[...]
        rstd = lax.rsqrt(jnp.mean(xv * xv, axis=-1, keepdims=True) + 1e-6)
        h = (xv * rstd * nw_ref[...]) * (1.0 + mod_ref[:, D:2 * D]) + mod_ref[:, 0:D]
        hb = h.astype(MX)
        ht_ref[...] = hb.T
        proj_ref[...] = jnp.dot(hb, w_ref[...], preferred_element_type=f32)

    return _call(
        body, name="ln_inproj", grid=(t // TM,),
        out_shape=(jax.ShapeDtypeStruct((t, W_PACK), f32), jax.ShapeDtypeStruct((D, t), MX)),
        in_specs=[pl.BlockSpec((TM, D), lambda i: (i, 0)), _row(6 * D), _row(), _resident((D, W_PACK))],
        out_specs=(pl.BlockSpec((TM, W_PACK), lambda i: (i, 0)), pl.BlockSpec((D, TM), lambda i: (0, i))),
        sem=("arbitrary",), args=(x, mod, norm1_w, w_pack), rider=rider)


def _ssd_gate_norm(y_scan, xbc_act, proj, d_skip_row, ssd_norm_w):
    t = y_scan.shape[0]

    def body(y_ref, xs_ref, z_ref, dsk_ref, nw_ref, o_ref):
        y = y_ref[...] + xs_ref[...] * dsk_ref[...]
        yz = y * _silu(z_ref[...])
        rstd = lax.rsqrt(jnp.mean(yz * yz, axis=-1, keepdims=True) + 1e-6)
        o_ref[...] = (yz * rstd * nw_ref[...]).astype(MX)

    blk = pl.BlockSpec((TM, D), lambda i: (i, 0))
    return pl.pallas_call(
        body, name="ssd_gate_norm", grid=(t // TM,), out_shape=jax.ShapeDtypeStruct((t, D), MX),
        in_specs=[blk, blk, blk, _row(), _row()], out_specs=blk, compiler_params=_cp("arbitrary"),
    )(y_scan, xbc_act, proj, d_skip_row, ssd_norm_w)


def _ln_silu(u_conv, ln_w, ln_b):
    t = u_conv.shape[0]

    def body(u_ref, w_ref, b_ref, o_ref):
        u = u_ref[...]
        mu = jnp.mean(u, axis=-1, keepdims=True)
        uc = u - mu
        rstd = lax.rsqrt(jnp.mean(uc * uc, axis=-1, keepdims=True) + 1e-5)
        o_ref[...] = _silu(uc * rstd * w_ref[...] + b_ref[...]).astype(MX)

    blk = pl.BlockSpec((TM, D), lambda i: (i, 0))
    return pl.pallas_call(
        body, name="ln_silu", grid=(t // TM,), out_shape=jax.ShapeDtypeStruct((t, D), MX),
        in_specs=[blk, _row(), _row()], out_specs=blk, compiler_params=_cp("arbitrary"),
    )(u_conv, ln_w, ln_b)


def _outproj_ln2_up(y_ssd, u, w_out, x, mod, norm2_w, w_up):
    t = x.shape[0]

    def body(y_ref, u_ref, wo_ref, x_ref, mod_ref, nw_ref, wu_ref, mix_ref, x1_ref, h2t_ref, up_ref):
        mix = jnp.dot(y_ref[...], wo_ref[0:D, :], preferred_element_type=f32)
        mix = mix + jnp.dot(u_ref[...], wo_ref[D:2 * D, :], preferred_element_type=f32)
        mix_ref[...] = mix
        x1 = x_ref[...] + mod_ref[:, 2 * D:3 * D] * mix
        x1_ref[...] = x1
        rstd = lax.rsqrt(jnp.mean(x1 * x1, axis=-1, keepdims=True) + 1e-6)
        h2 = ((x1 * rstd * nw_ref[...]) * (1.0 + mod_ref[:, 4 * D:5 * D]) + mod_ref[:, 3 * D:4 * D]).astype(MX)
        h2t_ref[...] = h2.T
        for k in range(4):
            up_ref[:, k * UP_SHARD:(k + 1) * UP_SHARD] = jnp.dot(h2, wu_ref[k], preferred_element_type=f32)

    blk = pl.BlockSpec((TM, D), lambda i: (i, 0))
    return pl.pallas_call(
        body, name="outproj_ln2_up", grid=(t // TM,),
        out_shape=(jax.ShapeDtypeStruct((t, D), f32), jax.ShapeDtypeStruct((t, D), f32),
                   jax.ShapeDtypeStruct((D, t), MX), jax.ShapeDtypeStruct((t, 2 * D_FF), f32)),
        in_specs=[blk, blk, _resident((2 * D, D)), blk, _row(6 * D), _row(), _resident((4, D, UP_SHARD))],
        out_specs=(blk, blk, pl.BlockSpec((D, TM), lambda i: (0, i)), pl.BlockSpec((TM, 2 * D_FF), lambda i: (i, 0))),
        compiler_params=_cp("arbitrary"),
    )(y_ssd, u, w_out, x, mod, norm2_w, w_up)


def _down_loss(act, w_down, x1, mod, final_norm_w, target):
    t = x1.shape[0]

    def body(a_ref, wd_ref, x1_ref, mod_ref, wf_ref, tgt_ref, dx2_ref, dffn_ref, dact_ref, st_ref):
        @pl.when(pl.program_id(0) == 0)
        def _():
            st_ref[...] = jnp.zeros_like(st_ref)

        g2 = mod_ref[:, 5 * D:6 * D]
        ffn = jnp.dot(a_ref[...], wd_ref[...], preferred_element_type=f32)
        x2 = x1_ref[...] + g2 * ffn
        rstd = lax.rsqrt(jnp.mean(x2 * x2, axis=-1, keepdims=True) + 1e-6)
        xh = x2 * rstd
        wf = wf_ref[...]
        err = xh * wf - tgt_ref[...]
        dy = err * (1.0 / D)
        dxh = dy * wf
        dx2 = rstd * (dxh - xh * jnp.mean(dxh * xh, axis=-1, keepdims=True))
        dx2_ref[...] = dx2
        dffn = (g2 * dx2).astype(MX)
        dffn_ref[...] = dffn
        dact_ref[...] = lax.dot_general(dffn, wd_ref[...], (((1,), (1,)), ((), ())), preferred_element_type=f32)
        st_ref[0:1, :] += jnp.sum(dy * xh, axis=0, keepdims=True)
        st_ref[1:2, :] += jnp.sum(dx2 * ffn, axis=0, keepdims=True)
        st_ref[2:3, :] += jnp.sum(0.5 * jnp.mean(err * err, axis=-1, keepdims=True), axis=0, keepdims=True)

    blk = pl.BlockSpec((TM, D), lambda i: (i, 0))
    ablk = pl.BlockSpec((TM, D_FF), lambda i: (i, 0))
    return pl.pallas_call(
        body, name="down_loss", grid=(t // TM,),
        out_shape=(jax.ShapeDtypeStruct((t, D), f32), jax.ShapeDtypeStruct((t, D), MX),
                   jax.ShapeDtypeStruct((t, D_FF), f32), jax.ShapeDtypeStruct((8, D), f32)),
        in_specs=[ablk, _resident((D_FF, D)), blk, _row(6 * D), _row(), blk],
        out_specs=(blk, blk, ablk, pl.BlockSpec((8, D), lambda i: (0, 0))),
        compiler_params=_cp("arbitrary"),
    )(act, w_down, x1, mod, final_norm_w, target)


def _pad_of(k):
    return 8 * ((k - 1 + 7) // 8)


def _causal_win(ref, r, t, pad):
    base = pl.multiple_of(r * RC, RC)
    prev = ref[pl.ds(pl.multiple_of(jnp.maximum(base - pad, 0), 8), pad), :]
    prev = jnp.where(r > 0, prev, 0.0)
    return jnp.concatenate([prev, ref[pl.ds(base, RC), :]], axis=0)


def _anti_win(ref, r, t, pad):
    base = pl.multiple_of(r * RC, RC)
    nxt = ref[pl.ds(pl.multiple_of(jnp.minimum(base + RC, t - pad), 8), pad), :]
    nxt = jnp.where(r < t // RC - 1, nxt, 0.0)
    return jnp.concatenate([ref[pl.ds(base, RC), :], nxt], axis=0)


def _shifted(win, offsets):
    for r in range(8):
        mine = [o for o in offsets if o % 8 == r]
        if mine:
            rolled = win if r == 0 else pltpu.roll(win, win.shape[0] - r, 0)
            for o in mine:
                yield o, rolled[o - r:o - r + RC, :]


def _conv_taps(win, w_ref, k, pad):
    first = pad - (k - 1)
    acc = None
    for o, rows in _shifted(win, range(first, first + k)):
        term = w_ref[o - first:o - first + 1, :] * rows
        acc = term if acc is None else acc + term
    return acc


def _corr_taps(win, w_ref, k):
    acc = None
    for o, rows in _shifted(win, range(k)):
        term = w_ref[k - 1 - o:k - o, :] * rows
        acc = term if acc is None else acc + term
    return acc


def _dw_accumulate(dw_scr, d, win, k, pad):
    first = pad - (k - 1)
    for o, rows in _shifted(win, range(first, first + k)):
        j = o - first
        prod = d * rows
        dw_scr[8 * j:8 * j + 8, :] += prod.reshape(RC // 8, 8, prod.shape[-1]).sum(axis=0)


def _dw_finish(dw_scr, dw_ref, k):
    for j in range(k):
        dw_ref[j:j + 1, :] = jnp.sum(dw_scr[8 * j:8 * j + 8, :], axis=0, keepdims=True)


def _rows8(v):
    return v.reshape(RC // 8, 8, v.shape[-1]).sum(axis=0)


def _ssd_conv_fwd(proj, conv_w, conv_b, rider=None):
    t = proj.shape[0]
    pad = _pad_of(K_SSD)
    c0 = OFF_XBC // CW

    def body(x_ref, w_ref, b_ref, o_ref):
        def step(r, carry):
            win = _causal_win(x_ref, r, t, pad)
            o_ref[pl.ds(pl.multiple_of(r * RC, RC), RC), :] = _silu(_conv_taps(win, w_ref, K_SSD, pad) + b_ref[...])
            return carry
        lax.fori_loop(0, t // RC, step, 0)

    return _call(
        body, name="ssd_conv_fwd", grid=(D_XBC // CW,), out_shape=(jax.ShapeDtypeStruct((t, D_XBC), f32),),
        in_specs=[pl.BlockSpec((t, CW), lambda j: (0, c0 + j)), pl.BlockSpec((K_SSD, CW), lambda j: (0, j)),
                  pl.BlockSpec((1, CW), lambda j: (0, j))],
        out_specs=(pl.BlockSpec((t, CW), lambda j: (0, j)),), sem=("arbitrary",), args=(proj, conv_w, conv_b), rider=rider)


def _glu_conv_fwd(proj, conv_w, conv_b, rider=None):
    t = proj.shape[0]
    pad = _pad_of(K_CONF)
    ca, cg = OFF_CA // CW, OFF_CG // CW

    def body(a_ref, g_ref, w_ref, b_ref, o_ref, v_scr):
        def glu(r, carry):
            rows = pl.ds(pl.multiple_of(r * RC, RC), RC)
            v_scr[rows, :] = a_ref[rows, :] * jax.nn.sigmoid(g_ref[rows, :])
            return carry
        lax.fori_loop(0, t // RC, glu, 0)

        def step(r, carry):
            win = _causal_win(v_scr, r, t, pad)
            o_ref[pl.ds(pl.multiple_of(r * RC, RC), RC), :] = _conv_taps(win, w_ref, K_CONF, pad) + b_ref[...]
            return carry
        lax.fori_loop(0, t // RC, step, 0)

    return _call(
        body, name="glu_conv_fwd", grid=(D // CW,), out_shape=(jax.ShapeDtypeStruct((t, D), f32),),
        in_specs=[pl.BlockSpec((t, CW), lambda j: (0, ca + j)), pl.BlockSpec((t, CW), lambda j: (0, cg + j)),
                  pl.BlockSpec((K_CONF, CW), lambda j: (0, j)), pl.BlockSpec((1, CW), lambda j: (0, j))],
        out_specs=(pl.BlockSpec((t, CW), lambda j: (0, j)),),
        scratch_shapes=[pltpu.VMEM((t, CW), f32)], sem=("arbitrary",), args=(proj, proj, conv_w, conv_b), rider=rider)


def _ffn_conv_fwd(up, conv_w, conv_b, rider=None):
    t = up.shape[0]
    pad = _pad_of(K_FFN)
    nb = D_FF // CW

    def body(g_ref, v_ref, wg_ref, wv_ref, bg_ref, bv_ref, o_ref):
        def step(r, carry):
            gc = _conv_taps(_causal_win(g_ref, r, t, pad), wg_ref, K_FFN, pad) + bg_ref[...]
            vc = _conv_taps(_causal_win(v_ref, r, t, pad), wv_ref, K_FFN, pad) + bv_ref[...]
            o_ref[pl.ds(pl.multiple_of(r * RC, RC), RC), :] = (_silu(gc) * vc).astype(MX)
            return carry
        lax.fori_loop(0, t // RC, step, 0)

    return _call(
        body, name="ffn_conv_fwd", grid=(nb,), out_shape=(jax.ShapeDtypeStruct((t, D_FF), MX),),
        in_specs=[pl.BlockSpec((t, CW), lambda j: (0, j)), pl.BlockSpec((t, CW), lambda j: (0, nb + j)),
                  pl.BlockSpec((K_FFN, CW), lambda j: (0, j)), pl.BlockSpec((K_FFN, CW), lambda j: (0, nb + j)),
                  pl.BlockSpec((1, CW), lambda j: (0, j)), pl.BlockSpec((1, CW), lambda j: (0, nb + j))],
        out_specs=(pl.BlockSpec((t, CW), lambda j: (0, j)),), sem=("arbitrary",),
        args=(up, up, conv_w, conv_w, conv_b, conv_b), rider=rider)


def _ffn_conv_bwd(up, conv_w, conv_b, d_act, rider=None):
    t = up.shape[0]
    pad = _pad_of(K_FFN)
    nb = D_FF // CW

    def body(g_ref, v_ref, wg_ref, wv_ref, bg_ref, bv_ref, da_ref, dup_ref, dw_ref, db_ref,
             dg_scr, dv_scr, dwg_scr, dwv_scr, db_scr):
        dwg_scr[...] = jnp.zeros_like(dwg_scr)
        dwv_scr[...] = jnp.zeros_like(dwv_scr)
        db_scr[...] = jnp.zeros_like(db_scr)

        def first(r, carry):
            rows = pl.ds(pl.multiple_of(r * RC, RC), RC)
            gwin = _causal_win(g_ref, r, t, pad)
            vwin = _causal_win(v_ref, r, t, pad)
            gc = _conv_taps(gwin, wg_ref, K_FFN, pad) + bg_ref[...]
            vc = _conv_taps(vwin, wv_ref, K_FFN, pad) + bv_ref[...]
            da = da_ref[rows, :]
            dgc = da * vc * _dsilu(gc)
            dvc = da * _silu(gc)
            dg_scr[rows, :] = dgc
            dv_scr[rows, :] = dvc
            _dw_accumulate(dwg_scr, dgc, gwin, K_FFN, pad)
            _dw_accumulate(dwv_scr, dvc, vwin, K_FFN, pad)
            db_scr[0:8, :] += _rows8(dgc)
            db_scr[8:16, :] += _rows8(dvc)
            return carry
        lax.fori_loop(0, t // RC, first, 0)

        def second(r, carry):
            rows = pl.ds(pl.multiple_of(r * RC, RC), RC)
            dup_ref[0, rows, :] = _corr_taps(_anti_win(dg_scr, r, t, pad), wg_ref, K_FFN).astype(MX)
            dup_ref[1, rows, :] = _corr_taps(_anti_win(dv_scr, r, t, pad), wv_ref, K_FFN).astype(MX)
            return carry
        lax.fori_loop(0, t // RC, second, 0)

        for j in range(K_FFN):
            dw_ref[0, j:j + 1, :] = jnp.sum(dwg_scr[8 * j:8 * j + 8, :], axis=0, keepdims=True)
            dw_ref[1, j:j + 1, :] = jnp.sum(dwv_scr[8 * j:8 * j + 8, :], axis=0, keepdims=True)
        db_ref[0] = jnp.sum(db_scr[0:8, :], axis=0, keepdims=True)
        db_ref[1] = jnp.sum(db_scr[8:16, :], axis=0, keepdims=True)

    return _call(
        body, name="ffn_conv_bwd", grid=(nb,),
        out_shape=(jax.ShapeDtypeStruct((2, t, D_FF), MX), jax.ShapeDtypeStruct((2, K_FFN, D_FF), f32),
                   jax.ShapeDtypeStruct((2, 1, D_FF), f32)),
        in_specs=[pl.BlockSpec((t, CW), lambda j: (0, j)), pl.BlockSpec((t, CW), lambda j: (0, nb + j)),
                  pl.BlockSpec((K_FFN, CW), lambda j: (0, j)), pl.BlockSpec((K_FFN, CW), lambda j: (0, nb + j)),
                  pl.BlockSpec((1, CW), lambda j: (0, j)), pl.BlockSpec((1, CW), lambda j: (0, nb + j)),
                  pl.BlockSpec((t, CW), lambda j: (0, j))],
        out_specs=(pl.BlockSpec((2, t, CW), lambda j: (0, 0, j)), pl.BlockSpec((2, K_FFN, CW), lambda j: (0, 0, j)),
                   pl.BlockSpec((2, 1, CW), lambda j: (0, 0, j))),
        scratch_shapes=[pltpu.VMEM((t, CW), f32), pltpu.VMEM((t, CW), f32), pltpu.VMEM((8 * K_FFN, CW), f32),
                        pltpu.VMEM((8 * K_FFN, CW), f32), pltpu.VMEM((16, CW), f32)],
        sem=("arbitrary",), args=(up, up, conv_w, conv_w, conv_b, conv_b, d_act), rider=rider)


def _glu_conv_bwd(proj, conv_w, d_uconv, rider=None):
    t = proj.shape[0]
    pad = _pad_of(K_CONF)
    ca, cg = OFF_CA // CW, OFF_CG // CW

    def body(a_ref, g_ref, w_ref, du_ref, dc_ref, dw_ref, db_ref, v_scr, dw_scr, db_scr):
        dw_scr[...] = jnp.zeros_like(dw_scr)
        db_scr[...] = jnp.zeros_like(db_scr)

        def glu(r, carry):
            rows = pl.ds(pl.multiple_of(r * RC, RC), RC)
            v_scr[rows, :] = a_ref[rows, :] * jax.nn.sigmoid(g_ref[rows, :])
            return carry
        lax.fori_loop(0, t // RC, glu, 0)

        def step(r, carry):
            rows = pl.ds(pl.multiple_of(r * RC, RC), RC)
            du = du_ref[rows, :]
            _dw_accumulate(dw_scr, du, _causal_win(v_scr, r, t, pad), K_CONF, pad)
            db_scr[...] += _rows8(du)
            dv = _corr_taps(_anti_win(du_ref, r, t, pad), w_ref, K_CONF)
            a = a_ref[rows, :]
            s = jax.nn.sigmoid(g_ref[rows, :])
            dc_ref[0, rows, :] = (dv * s).astype(MX)
            dc_ref[1, rows, :] = (dv * a * s * (1.0 - s)).astype(MX)
            return carry
        lax.fori_loop(0, t // RC, step, 0)
        _dw_finish(dw_scr, dw_ref, K_CONF)
        db_ref[...] = jnp.sum(db_scr[...], axis=0, keepdims=True)

    return _call(
        body, name="glu_conv_bwd", grid=(D // CW,),
        out_shape=(jax.ShapeDtypeStruct((2, t, D), MX), jax.ShapeDtypeStruct((K_CONF, D), f32),
                   jax.ShapeDtypeStruct((1, D), f32)),
        in_specs=[pl.BlockSpec((t, CW), lambda j: (0, ca + j)), pl.BlockSpec((t, CW), lambda j: (0, cg + j)),
                  pl.BlockSpec((K_CONF, CW), lambda j: (0, j)), pl.BlockSpec((t, CW), lambda j: (0, j))],
        out_specs=(pl.BlockSpec((2, t, CW), lambda j: (0, 0, j)), pl.BlockSpec((K_CONF, CW), lambda j: (0, j)),
                   pl.BlockSpec((1, CW), lambda j: (0, j))),
        scratch_shapes=[pltpu.VMEM((t, CW), f32), pltpu.VMEM((8 * K_CONF, CW), f32), pltpu.VMEM((8, CW), f32)],
        sem=("arbitrary",), args=(proj, proj, conv_w, d_uconv), rider=rider)


def _ssd_conv_bwd_x(proj, conv_w, conv_b, d_xs, d_y, d_skip_row):
    t = proj.shape[0]
    pad = _pad_of(K_SSD)
    c0 = OFF_XBC // CW

    def body(x_ref, w_ref, b_ref, dxs_ref, dy_ref, dsk_ref, draw_ref, dw_ref, db_ref, dp_scr, dw_scr, db_scr):
        dw_scr[...] = jnp.zeros_like(dw_scr)
        db_scr[...] = jnp.zeros_like(db_scr)

        def first(r, carry):
            rows = pl.ds(pl.multiple_of(r * RC, RC), RC)
            win = _causal_win(x_ref, r, t, pad)
            pre = _conv_taps(win, w_ref, K_SSD, pad) + b_ref[...]
            dpre = (dxs_ref[rows, :] + dy_ref[rows, :] * dsk_ref[...]) * _dsilu(pre)
            dp_scr[rows, :] = dpre
            _dw_accumulate(dw_scr, dpre, win, K_SSD, pad)
            db_scr[...] += _rows8(dpre)
            return carry
        lax.fori_loop(0, t // RC, first, 0)

        def second(r, carry):
            rows = pl.ds(pl.multiple_of(r * RC, RC), RC)
            draw_ref[rows, :] = _corr_taps(_anti_win(dp_scr, r, t, pad), w_ref, K_SSD).astype(MX)
            return carry
        lax.fori_loop(0, t // RC, second, 0)
        _dw_finish(dw_scr, dw_ref, K_SSD)
        db_ref[...] = jnp.sum(db_scr[...], axis=0, keepdims=True)

    cb = pl.BlockSpec((t, CW), lambda j: (0, j))
    return pl.pallas_call(
        body, name="ssd_conv_bwd_x", grid=(D // CW,),
        out_shape=(jax.ShapeDtypeStruct((t, D), MX), jax.ShapeDtypeStruct((K_SSD, D), f32),
                   jax.ShapeDtypeStruct((1, D), f32)),
        in_specs=[pl.BlockSpec((t, CW), lambda j: (0, c0 + j)), pl.BlockSpec((K_SSD, CW), lambda j: (0, j)),
                  pl.BlockSpec((1, CW), lambda j: (0, j)), cb, cb, pl.BlockSpec((1, CW), lambda j: (0, j))],
        out_specs=(cb, pl.BlockSpec((K_SSD, CW), lambda j: (0, j)), pl.BlockSpec((1, CW), lambda j: (0, j))),
        scratch_shapes=[pltpu.VMEM((t, CW), f32), pltpu.VMEM((8 * K_SSD, CW), f32), pltpu.VMEM((8, CW), f32)],
        compiler_params=_cp("arbitrary"),
    )(proj, conv_w, conv_b, d_xs, d_y, d_skip_row)


def _ssd_conv_bwd_bc(proj, conv_w, conv_b, d_bc):
    t = proj.shape[0]
    pad = _pad_of(K_SSD)
    c0 = (OFF_XBC + D) // CW
    w0 = D // CW

    def body(x_ref, w_ref, b_ref, dbc_ref, draw_ref, dw_ref, db_ref, dp_scr, dw_scr, db_scr):
        dw_scr[...] = jnp.zeros_like(dw_scr)
        db_scr[...] = jnp.zeros_like(db_scr)

        def first(r, carry):
            rows = pl.ds(pl.multiple_of(r * RC, RC), RC)
            win = _causal_win(x_ref, r, t, pad)
            pre = _conv_taps(win, w_ref, K_SSD, pad) + b_ref[...]
            dpre = dbc_ref[0, rows, :] * _dsilu(pre)
            dp_scr[rows, :] = dpre
            _dw_accumulate(dw_scr, dpre, win, K_SSD, pad)
            db_scr[...] += _rows8(dpre)
            return carry
        lax.fori_loop(0, t // RC, first, 0)

        def second(r, carry):
            rows = pl.ds(pl.multiple_of(r * RC, RC), RC)
            draw_ref[rows, :] = _corr_taps(_anti_win(dp_scr, r, t, pad), w_ref, K_SSD).astype(MX)
            return carry
        lax.fori_loop(0, t // RC, second, 0)
        _dw_finish(dw_scr, dw_ref, K_SSD)
        db_ref[...] = jnp.sum(db_scr[...], axis=0, keepdims=True)

    return pl.pallas_call(
        body, name="ssd_conv_bwd_bc", grid=(2,),
        out_shape=(jax.ShapeDtypeStruct((t, 2 * CW), MX), jax.ShapeDtypeStruct((K_SSD, 2 * CW), f32),
                   jax.ShapeDtypeStruct((1, 2 * CW), f32)),
        in_specs=[pl.BlockSpec((t, CW), lambda j: (0, c0 + j)), pl.BlockSpec((K_SSD, CW), lambda j: (0, w0 + j)),
                  pl.BlockSpec((1, CW), lambda j: (0, w0 + j)), pl.BlockSpec((1, t, CW), lambda j: (j, 0, 0))],
        out_specs=(pl.BlockSpec((t, CW), lambda j: (0, j)), pl.BlockSpec((K_SSD, CW), lambda j: (0, j)),
                   pl.BlockSpec((1, CW), lambda j: (0, j))),
        scratch_shapes=[pltpu.VMEM((t, CW), f32), pltpu.VMEM((8 * K_SSD, CW), f32), pltpu.VMEM((8, CW), f32)],
        compiler_params=_cp("arbitrary"),
    )(proj, conv_w, conv_b, d_bc)


def _chunk_masks():
    ii = lax.broadcasted_iota(jnp.int32, (CHUNK, CHUNK), 0)
    jj = lax.broadcasted_iota(jnp.int32, (CHUNK, CHUNK), 1)
    return ii == jj, jj <= ii, jj >= ii


def _to_row(col, eye):
    return jnp.sum(jnp.where(eye, col, 0.0), axis=0, keepdims=True)


def _to_col(row, eye):
    return jnp.sum(jnp.where(eye, row, 0.0), axis=1, keepdims=True)


def _head_decay(dt_h, a_h, eye, tril):
    a_row = _to_row(dt_h * a_h, eye)
    cs = jnp.sum(jnp.where(tril, a_row, 0.0), axis=1, keepdims=True)
    cs_row = _to_row(cs, eye)
    decay = jnp.where(tril, jnp.exp(jnp.where(tril, cs - cs_row, 0.0)), 0.0)
    total = jnp.sum(a_row, axis=1, keepdims=True)
    return cs, decay, total


SCAN_UNROLL = 4


def _unrolled_loop(n, step, init):
    unroll = min(SCAN_UNROLL, n)
    assert n % unroll == 0

    def trip(i, carry):
        for u in range(unroll):
            carry = step(unroll * i + u, carry)
        return carry
    return lax.fori_loop(0, n // unroll, trip, init)


def _lane_pick(mat, lane, which):
    return jnp.sum(jnp.where(lane == which, mat, 0.0), axis=1, keepdims=True)


def _ssd_fwd(xbc_act, proj, dt_bias_row, a_log_row, rider=None):
    t = xbc_act.shape[0]
    nc = t // CHUNK
    cb, cc, cdt = D // LANES, (D + 2 * STATE_N) // LANES, OFF_DT // LANES

    def body(x_ref, b_ref, c_ref, dt_ref, dtb_ref, alog_ref, y_ref, st_ref):
        j = pl.program_id(0)
        eye, tril, _ = _chunk_masks()
        lane = lax.broadcasted_iota(jnp.int32, (1, LANES), 1)
        first = lane < HEAD_P
        a_row = -jnp.exp(alog_ref[...])
        a_heads = [jnp.sum(jnp.where(lane == 2 * j + h, a_row, 0.0), axis=1, keepdims=True) for h in range(2)]

        def chunk(c, hprev):
            rows = pl.ds(pl.multiple_of(c * CHUNK, CHUNK), CHUNK)
            xv, bm, cm = x_ref[rows, :], b_ref[rows, :], c_ref[rows, :]
            dt = _softplus(dt_ref[rows, :] + dtb_ref[...])
            st_ref[c] = hprev
            g = _mm_nt(cm, bm)
            ch = _mm(cm, hprev)
            dts = [_lane_pick(dt, lane, 2 * j + h) for h in range(2)]
            xdt = xv * jnp.where(first, dts[0], dts[1])
            ys, hs = [], []
            for h in range(2):
                cs, decay, total = _head_decay(dts[h], a_heads[h], eye, tril)
                y = _mm(g * decay, xdt) + jnp.exp(cs) * ch
                s = _mm_tn(bm * jnp.exp(total - cs), xdt)
                ys.append(y)
                hs.append(jnp.exp(total) * hprev + s)
            y_ref[rows, :] = jnp.where(first, ys[0], ys[1])
            return jnp.where(first, hs[0], hs[1])

        _unrolled_loop(nc, chunk, jnp.zeros((STATE_N, LANES), f32))

    blk = lambda f: pl.BlockSpec((t, LANES), f)
    return _call(
        body, name="ssd_fwd", grid=(D // LANES,),
        out_shape=(jax.ShapeDtypeStruct((t, D), f32), jax.ShapeDtypeStruct((nc, STATE_N, D), f32)),
        in_specs=[blk(lambda j: (0, j)), blk(lambda j: (0, cb + j // 4)), blk(lambda j: (0, cc + j // 4)),
                  blk(lambda j: (0, cdt)), _row(LANES), _row(LANES)],
        out_specs=(blk(lambda j: (0, j)), pl.BlockSpec((nc, STATE_N, LANES), lambda j: (0, 0, j))),
        sem=("arbitrary",), args=(xbc_act, xbc_act, xbc_act, proj, dt_bias_row, a_log_row), rider=rider)


def _ssd_bwd(xbc_act, proj, dt_bias_row, a_log_row, states, d_y, rider=None):
    t = xbc_act.shape[0]
    nc = t // CHUNK
    cb, cc, cdt = D // LANES, (D + 2 * STATE_N) // LANES, OFF_DT // LANES

    def body(x_ref, b_ref, c_ref, dt_ref, dtb_ref, alog_ref, st_ref, dy_ref, dx_ref, dbc_ref, ddt_ref, da_ref):
        grp, p = pl.program_id(0), pl.program_id(1)
        j = 4 * grp + p
        eye, tril, triu = _chunk_masks()
        lane = lax.broadcasted_iota(jnp.int32, (1, LANES), 1)
        first = lane < HEAD_P
        last_row = lax.broadcasted_iota(jnp.int32, (CHUNK, 1), 0) == CHUNK - 1
        a_row = -jnp.exp(alog_ref[...])
        a_heads = [jnp.sum(jnp.where(lane == 2 * j + h, a_row, 0.0), axis=1, keepdims=True) for h in range(2)]

        @pl.when(p == 0)
        def _():
            dbc_ref[...] = jnp.zeros_like(dbc_ref)

        @pl.when(j == 0)
        def _():
            ddt_ref[...] = jnp.zeros_like(ddt_ref)
            da_ref[...] = jnp.zeros_like(da_ref)

        def chunk(i, dh):
            c = nc - 1 - i
            rows = pl.ds(pl.multiple_of(c * CHUNK, CHUNK), CHUNK)
            xv, bm, cm = x_ref[rows, :], b_ref[rows, :], c_ref[rows, :]
            dtr = dt_ref[rows, :] + dtb_ref[...]
            dt = _softplus(dtr)
            hprev = st_ref[c]
            dy = dy_ref[rows, :]
            g = _mm_nt(cm, bm)
            dts = [_lane_pick(dt, lane, 2 * j + h) for h in range(2)]
            xdt = xv * jnp.where(first, dts[0], dts[1])
            dxs, dhs = [], []
            db_sum, dc_sum = None, None
            ddt_mat = jnp.zeros((CHUNK, LANES), f32)
            da_acc = jnp.zeros((1, LANES), f32)
            for h in range(2):
                mine = first if h == 0 else jnp.logical_not(first)
                cs, decay, total = _head_decay(dts[h], a_heads[h], eye, tril)
                e_cs, e_tot = jnp.exp(cs), jnp.exp(total)
                dec_s = jnp.exp(total - cs)
                dyh = jnp.where(mine, dy, 0.0)
                xdth = jnp.where(mine, xdt, 0.0)
                dhh = jnp.where(mine, dh, 0.0)
                hph = jnp.where(mine, hprev, 0.0)
                m = g * decay
                dm = _mm_nt(dyh, xdth)
                dg = dm * decay
                w = dm * m
                bdec = bm * dec_s
                dxdt = _mm_tn(m, dyh) + _mm(bdec, dhh)
                dc_off = _mm_nt(dyh, hph) * e_cs
                db_s = _mm_nt(xdth, dhh) * dec_s
                dc_h = _mm(dg, bm) + dc_off
                db_h = _mm_tn(dg, cm) + db_s
                r_s = jnp.sum(db_s * bm, axis=1, keepdims=True)
                dtotal = jnp.sum(r_s, axis=0, keepdims=True) + e_tot * jnp.sum(
                    jnp.sum(dhh * hph, axis=1, keepdims=True), axis=0, keepdims=True)
                dcs = (jnp.sum(w, axis=1, keepdims=True) - _to_col(jnp.sum(w, axis=0, keepdims=True), eye)
                       + jnp.sum(dc_off * cm, axis=1, keepdims=True) - r_s + jnp.where(last_row, dtotal, 0.0))
                da_col = jnp.sum(jnp.where(triu, _to_row(dcs, eye), 0.0), axis=1, keepdims=True)
                ddt = da_col * a_heads[h] + jnp.sum(jnp.where(mine, dxdt * xv, 0.0), axis=1, keepdims=True)
                ddt_mat = ddt_mat + jnp.where(lane == 2 * j + h, ddt, 0.0)
                da_acc = da_acc + jnp.where(lane == 2 * j + h, jnp.sum(da_col * dts[h], axis=0, keepdims=True), 0.0)
                dxs.append(dxdt * dts[h])
                dhs.append(e_tot * dhh + _mm_tn(cm * e_cs, dyh))
                db_sum = db_h if db_sum is None else db_sum + db_h
                dc_sum = dc_h if dc_sum is None else dc_sum + dc_h
            dx_ref[rows, :] = jnp.where(first, dxs[0], dxs[1])
            dbc_ref[0, rows, :] += db_sum
            dbc_ref[1, rows, :] += dc_sum
            ddt_ref[rows, :] += ddt_mat * jax.nn.sigmoid(dtr)
            da_ref[...] += da_acc * a_row
            return jnp.where(first, dhs[0], dhs[1])

        _unrolled_loop(nc, chunk, jnp.zeros((STATE_N, LANES), f32))

    blk = lambda f: pl.BlockSpec((t, LANES), f)
    return _call(
        body, name="ssd_bwd", grid=(2, 4),
        out_shape=(jax.ShapeDtypeStruct((t, D), f32), jax.ShapeDtypeStruct((2, t, 2 * STATE_N), f32),
                   jax.ShapeDtypeStruct((t, LANES), f32), jax.ShapeDtypeStruct((1, LANES), f32)),
        in_specs=[blk(lambda g, p: (0, 4 * g + p)), blk(lambda g, p: (0, cb + g)), blk(lambda g, p: (0, cc + g)),
                  blk(lambda g, p: (0, cdt)), _row(LANES), _row(LANES),
                  pl.BlockSpec((nc, STATE_N, LANES), lambda g, p: (0, 0, 4 * g + p)), blk(lambda g, p: (0, 4 * g + p))],
        out_specs=(blk(lambda g, p: (0, 4 * g + p)), pl.BlockSpec((2, t, LANES), lambda g, p: (0, 0, g)),
                   blk(lambda g, p: (0, 0)), _row(LANES)),
        sem=("arbitrary", "arbitrary"), args=(xbc_act, xbc_act, xbc_act, proj, dt_bias_row, a_log_row, states, d_y),
        rider=rider)


def _up_bwd(d_up, w_up, x1, mod, norm2_w, dx2, mix, w_out, rider=None):
    t = x1.shape[0]

    def body(dup_ref, wu_ref, x1_ref, mod_ref, nw_ref, dx2_ref, mix_ref, wo_ref,
             dx1_ref, dmix_ref, dys_ref, du_ref, st_ref):
        @pl.when(pl.program_id(0) == 0)
        def _():
            st_ref[...] = jnp.zeros_like(st_ref)

        nt = (((1,), (1,)), ((), ()))
        dh = None
        for k in range(4):
            lo = (k % 2) * UP_SHARD
            part = lax.dot_general(dup_ref[k // 2, :, lo:lo + UP_SHARD], wu_ref[k], nt, preferred_element_type=f32)
            dh = part if dh is None else dh + part
        x1 = x1_ref[...]
        rstd = lax.rsqrt(jnp.mean(x1 * x1, axis=-1, keepdims=True) + 1e-6)
        xh = x1 * rstd
        nw = nw_ref[...]
        sc = 1.0 + mod_ref[:, 4 * D:5 * D]
        st_ref[0:1, :] += jnp.sum(dh, axis=0, keepdims=True)
        st_ref[1:2, :] += jnp.sum(dh * xh * nw, axis=0, keepdims=True)
        st_ref[2:3, :] += jnp.sum(dh * sc * xh, axis=0, keepdims=True)
        dxh = dh * sc * nw
        dx1 = dx2_ref[...] + rstd * (dxh - xh * jnp.mean(dxh * xh, axis=-1, keepdims=True))
        dx1_ref[...] = dx1
        st_ref[3:4, :] += jnp.sum(dx1 * mix_ref[...], axis=0, keepdims=True)
        dmix = (mod_ref[:, 2 * D:3 * D] * dx1).astype(MX)
        dmix_ref[...] = dmix
        dys_ref[...] = lax.dot_general(dmix, wo_ref[0:D, :], nt, preferred_element_type=f32)
        du_ref[...] = lax.dot_general(dmix, wo_ref[D:2 * D, :], nt, preferred_element_type=f32)

    blk = pl.BlockSpec((TM, D), lambda i: (i, 0))
    return _call(
        body, name="up_bwd", grid=(t // TM,),
        out_shape=(jax.ShapeDtypeStruct((t, D), f32), jax.ShapeDtypeStruct((t, D), MX),
                   jax.ShapeDtypeStruct((t, D), f32), jax.ShapeDtypeStruct((t, D), f32),
                   jax.ShapeDtypeStruct((8, D), f32)),
        in_specs=[pl.BlockSpec((2, TM, D_FF), lambda i: (0, i, 0)), _resident((4, D, UP_SHARD)), blk, _row(6 * D), _row(),
                  blk, blk, _resident((2 * D, D))],
        out_specs=(blk, blk, blk, blk, pl.BlockSpec((8, D), lambda i: (0, 0))),
        sem=("arbitrary",), args=(d_up, w_up, x1, mod, norm2_w, dx2, mix, w_out), rider=rider)


def _ln_silu_bwd(d_u, u_conv, ln_w, ln_b):
    t = d_u.shape[0]

    def body(du_ref, u_ref, w_ref, b_ref, o_ref, st_ref):
        @pl.when(pl.program_id(0) == 0)
        def _():
            st_ref[...] = jnp.zeros_like(st_ref)

        u = u_ref[...]
        mu = jnp.mean(u, axis=-1, keepdims=True)
        uc = u - mu
        rstd = lax.rsqrt(jnp.mean(uc * uc, axis=-1, keepdims=True) + 1e-5)
        n = uc * rstd
        w = w_ref[...]
        dl = du_ref[...] * _dsilu(n * w + b_ref[...])
        st_ref[0:1, :] += jnp.sum(dl * n, axis=0, keepdims=True)
        st_ref[1:2, :] += jnp.sum(dl, axis=0, keepdims=True)
        dn = dl * w
        o_ref[...] = rstd * (dn - jnp.mean(dn, axis=-1, keepdims=True) - n * jnp.mean(dn * n, axis=-1, keepdims=True))

    blk = pl.BlockSpec((TM, D), lambda i: (i, 0))
    return pl.pallas_call(
        body, name="ln_silu_bwd", grid=(t // TM,),
        out_shape=(jax.ShapeDtypeStruct((t, D), f32), jax.ShapeDtypeStruct((8, D), f32)),
        in_specs=[blk, blk, _row(), _row()], out_specs=(blk, pl.BlockSpec((8, D), lambda i: (0, 0))),
        compiler_params=_cp("arbitrary"),
    )(d_u, u_conv, ln_w, ln_b)


def _ssd_gate_norm_bwd(d_out, y_scan, xbc_act, proj, d_skip_row, ssd_norm_w):
    t = d_out.shape[0]

    def body(do_ref, y_ref, xs_ref, z_ref, dsk_ref, nw_ref, dy_ref, dz_ref, st_ref):
        @pl.when(pl.program_id(0) == 0)
        def _():
            st_ref[...] = jnp.zeros_like(st_ref)

        xs = xs_ref[...]
        y = y_ref[...] + xs * dsk_ref[...]
        z = z_ref[...]
        s = _silu(z)
        yz = y * s
        rstd = lax.rsqrt(jnp.mean(yz * yz, axis=-1, keepdims=True) + 1e-6)
        n = yz * rstd
        do = do_ref[...]
        st_ref[0:1, :] += jnp.sum(do * n, axis=0, keepdims=True)
        dn = do * nw_ref[...]
        dyz = rstd * (dn - n * jnp.mean(dn * n, axis=-1, keepdims=True))
        dy = dyz * s
        dy_ref[...] = dy
        dz_ref[...] = (dyz * y * _dsilu(z)).astype(MX)
        st_ref[1:2, :] += jnp.sum(dy * xs, axis=0, keepdims=True)

    blk = pl.BlockSpec((TM, D), lambda i: (i, 0))
    return pl.pallas_call(
        body, name="ssd_gate_norm_bwd", grid=(t // TM,),
        out_shape=(jax.ShapeDtypeStruct((t, D), f32), jax.ShapeDtypeStruct((t, D), MX), jax.ShapeDtypeStruct((8, D), f32)),
        in_specs=[blk, blk, blk, blk, _row(), _row()], out_specs=(blk, blk, pl.BlockSpec((8, D), lambda i: (0, 0))),
        compiler_params=_cp("arbitrary"),
    )(d_out, y_scan, xbc_act, proj, d_skip_row, ssd_norm_w)


def _inproj_bwd(d_z, d_xraw, d_bcraw, d_conf, d_dt, w_pack, x, mod, norm1_w, dx1, after=None):
    t = x.shape[0]
    extra = [] if after is None else [after]

    def body(dz_ref, dx_ref, dbc_ref, dcf_ref, ddt_ref, w_ref, x_ref, mod_ref, nw_ref, dx1_ref, *rest):
        gx_ref, st_ref = rest[-2:]
        @pl.when(pl.program_id(0) == 0)
        def _():
            st_ref[...] = jnp.zeros_like(st_ref)

        nt = (((1,), (1,)), ((), ()))
        dot = lambda a, lo, hi: lax.dot_general(a, w_ref[:, lo:hi], nt, preferred_element_type=f32)
        dh = dot(dz_ref[...], OFF_Z, OFF_Z + D)
        dh = dh + dot(dx_ref[...], OFF_XBC, OFF_XBC + D)
        dh = dh + dot(dbc_ref[...], OFF_XBC + D, OFF_XBC + D_XBC)
        dh = dh + dot(dcf_ref[0], OFF_CA, OFF_CA + D)
        dh = dh + dot(dcf_ref[1], OFF_CG, OFF_CG + D)
        dh = dh + dot(ddt_ref[...].astype(MX), OFF_DT, OFF_DT + LANES)
        st_ref[3:4, 0:LANES] += jnp.sum(ddt_ref[...], axis=0, keepdims=True)
        xv = x_ref[...]
        rstd = lax.rsqrt(jnp.mean(xv * xv, axis=-1, keepdims=True) + 1e-6)
        xh = xv * rstd
        nw = nw_ref[...]
        sc = 1.0 + mod_ref[:, D:2 * D]
        st_ref[0:1, :] += jnp.sum(dh, axis=0, keepdims=True)
        st_ref[1:2, :] += jnp.sum(dh * xh * nw, axis=0, keepdims=True)
        st_ref[2:3, :] += jnp.sum(dh * sc * xh, axis=0, keepdims=True)
        dxh = dh * sc * nw
        gx_ref[...] = dx1_ref[...] + rstd * (dxh - xh * jnp.mean(dxh * xh, axis=-1, keepdims=True))

    blk = pl.BlockSpec((TM, D), lambda i: (i, 0))
    return _call(
        body, name="inproj_bwd", grid=(t // TM,),
        out_shape=(jax.ShapeDtypeStruct((t, D), f32), jax.ShapeDtypeStruct((8, D), f32)),
        in_specs=[blk, blk, pl.BlockSpec((TM, 2 * CW), lambda i: (i, 0)), pl.BlockSpec((2, TM, D), lambda i: (0, i, 0)),
                  pl.BlockSpec((TM, LANES), lambda i: (i, 0)), _resident((D, W_PACK)), blk, _row(6 * D), _row(), blk]
        + [ANY] * len(extra),
        out_specs=(blk, pl.BlockSpec((8, D), lambda i: (0, 0))),
        sem=("arbitrary",), args=(d_z, d_xraw, d_bcraw, d_conf, d_dt, w_pack, x, mod, norm1_w, dx1, *extra))[0]


def _wgrad(a, d, name, bn=256, transposed=True):
    k, t = a.shape if transposed else a.shape[::-1]
    n = d.shape[1]
    out_dtype = MX
    contract = (((1,), (0,)), ((), ())) if transposed else (((0,), (0,)), ((), ()))

    def body(a_ref, d_ref, o_ref):
        o_ref[...] = lax.dot_general(a_ref[...], d_ref[...].astype(MX), contract, preferred_element_type=f32).astype(out_dtype)

    return pl.pallas_call(
        body, name=name, grid=(n // bn,), out_shape=jax.ShapeDtypeStruct((k, n), out_dtype),
        in_specs=[_resident(a.shape), pl.BlockSpec((t, bn), lambda j: (0, j))],
        out_specs=pl.BlockSpec((k, bn), lambda j: (0, j)), compiler_params=_cp("arbitrary"),
    )(a, d)


def _wgrad_stacked(at, d, name, bn):
    out_dtype = MX
    k, t = at.shape
    s, _, n = d.shape
    nb = n // bn

    def body(a_ref, d_ref, o_ref):
        o_ref[0] = jnp.dot(a_ref[...], d_ref[0], preferred_element_type=f32).astype(out_dtype)

    return pl.pallas_call(
        body, name=name, grid=(s, nb), out_shape=jax.ShapeDtypeStruct((s * nb, k, bn), out_dtype),
        in_specs=[_resident((k, t)), pl.BlockSpec((1, t, bn), lambda i, j: (i, 0, j))],
        out_specs=pl.BlockSpec((1, k, bn), lambda i, j: (i * nb + j, 0, 0)), compiler_params=_cp("arbitrary", "arbitrary"),
    )(at, d)


def _pad_row(v, width=LANES):
    return jnp.pad(v.reshape(1, -1), ((0, 0), (0, width - v.size)))


def _quarters(a):
    return a.reshape(4, 2, a.shape[0] // 8, a.shape[1])


def _local_step(x, mod, target, w_pack, late, small, reducer=None):
    dtb_row, alog_row = _pad_row(small["dt_bias"]), _pad_row(small["a_log"])
    dskip_row = jnp.repeat(small["d_skip"].reshape(-1), HEAD_P).reshape(1, D)

    red = reducer

    def hosted(host, args, swap=None, scatter=None, gather=None, sums=()):
        if red is None:
            return host(*args)[0]
        riders = ([red.scatter(scatter)] if scatter else []) + ([red.swap(*swap)] if swap else [])
        riders += [_SwapSumsRider([red.sums[n] for n in sums])] if sums else []
        riders += [_GatherRider([gather[0]], *gather[1:])] if gather is not None else []
        both = _Riders(riders)
        outs, extra = host(*args, rider=both)
        extra = both.split(extra)
        if scatter:
            red.scattered(scatter, extra.pop(0))
        if swap:
            red.swapped(swap[0], extra.pop(0))
        if sums:
            red.others.update(zip(sums, extra.pop(0)))
        return (outs, extra[0][0]) if gather is not None else outs

    w_out, w_up, w_down = late
    if red is None:
        proj, h_t = hosted(_ln_inproj, (x, mod, small["norm1_w"], w_pack))
        xbc_act, = hosted(_ssd_conv_fwd, (proj, small["ssd_conv_w"], small["ssd_conv_b"]))
        y_scan, states = hosted(_ssd_fwd, (xbc_act, proj, dtb_row, alog_row))
        u_conv, = hosted(_glu_conv_fwd, (proj, small["conf_conv_w"], small["conf_conv_b"]))
    else:
        (proj, h_t), w_out = hosted(_ln_inproj, (x, mod, small["norm1_w"], w_pack), gather=(w_out,))
        (xbc_act,), w_up = hosted(_ssd_conv_fwd, (proj, small["ssd_conv_w"], small["ssd_conv_b"]), gather=(w_up, 0, UP_EARLY_ROWS))
        (y_scan, states), w_up = hosted(_ssd_fwd, (xbc_act, proj, dtb_row, alog_row), gather=(w_up, UP_EARLY_ROWS, None))
        (u_conv,), w_down = hosted(_glu_conv_fwd, (proj, small["conf_conv_w"], small["conf_conv_b"]), gather=(w_down,))
        w_out, w_up, w_down = w_out.reshape(2 * D, D), w_up.reshape(4, D, UP_SHARD), w_down.reshape(D_FF, D)
    y_ssd = _ssd_gate_norm(y_scan, xbc_act, proj, dskip_row, small["ssd_norm_w"])
    u = _ln_silu(u_conv, small["conf_ln_w"], small["conf_ln_b"])
    mix, x1, h2_t, up = _outproj_ln2_up(y_ssd, u, w_out, x, mod, small["norm2_w"], w_up)
    act, = _ffn_conv_fwd(up, small["ffn_conv_w"], small["ffn_conv_b"])[0]
    dx2, d_ffn, d_act, st_down = _down_loss(act, w_down, x1, mod, small["final_norm_w"], target)

    g_down = _quarters(_wgrad(act, d_ffn, "wgrad_down", transposed=False))
    d_up, dw_ffn, db_ffn = hosted(_ffn_conv_bwd, (up, small["ffn_conv_w"], small["ffn_conv_b"], d_act), swap=("w_down", g_down))
    g_up = _wgrad_stacked(h2_t, d_up, "wgrad_up", D_FF // 2).reshape(4, 2, D // 2, UP_SHARD)
    dx1, d_mix, d_yssd, d_u, st_up = hosted(_up_bwd, (d_up, w_up, x1, mod, small["norm2_w"], dx2, mix, w_out),
                                            scatter="w_down", swap=("w_up", g_up))
    g_out = _quarters(jnp.concatenate([_wgrad(y_ssd, d_mix, "wgrad_out_y", transposed=False),
                                       _wgrad(u, d_mix, "wgrad_out_u", transposed=False)], axis=0))
    d_uconv, st_ln = _ln_silu_bwd(d_u, u_conv, small["conf_ln_w"], small["conf_ln_b"])
    d_conf, dw_conf, db_conf = hosted(_glu_conv_bwd, (proj, small["conf_conv_w"], d_uconv), scatter="w_up",
                                      swap=("w_out", g_out))
    d_y, d_z, st_gn = _ssd_gate_norm_bwd(d_yssd, y_scan, xbc_act, proj, dskip_row, small["ssd_norm_w"])
    d_xs, d_bc, d_dt, d_alog = hosted(_ssd_bwd, (xbc_act, proj, dtb_row, alog_row, states, d_y), scatter="w_out")
    d_xraw, dw_sx, db_sx = _ssd_conv_bwd_x(proj, small["ssd_conv_w"], small["ssd_conv_b"], d_xs, d_y, dskip_row)
    d_bcraw, dw_sbc, db_sbc = _ssd_conv_bwd_bc(proj, small["ssd_conv_w"], small["ssd_conv_b"], d_bc)
    g_in = _unpack_g_in(dict(
        z=_wgrad(h_t, d_z, "wgrad_in_z"), x=_wgrad(h_t, d_xraw, "wgrad_in_x"), bc=_wgrad(h_t, d_bcraw, "wgrad_in_bc"),
        conf=_wgrad_stacked(h_t, d_conf, "wgrad_in_conf", D), dt=_wgrad(h_t, d_dt, "wgrad_in_dt", bn=LANES)))
    g_in = g_in.reshape(4, 2, D // 2, W_IN_SHARD_PAD)
    args = (d_z, d_xraw, d_bcraw, d_conf, d_dt, w_pack, x, mod, small["norm1_w"], dx1)
    if red is None:
        grad_x, st_in = _inproj_bwd(*args)
    else:
        done = ("w_out", "w_up", "w_down")
        both = _Riders([red.swap("w_in", g_in), _SwapSumsRider([red.sums[n] for n in done])])
        handles, token = _split_start(both, "swap_start_w_in")
        grad_x, st_in = _inproj_bwd(*args, after=token)
        thru, outs = _split_wait(both, "swap_wait_w_in", handles, st_in)
        red.grads["w_in"] = thru[0]
        red.sums.update(zip(done, thru[1:]))
        got, others = both.split(outs)
        red.swapped("w_in", got)
        red.others.update(zip(done, others))

    gsmall = _pack_small_grads(st_in, st_up, st_down, st_ln, st_gn, d_alog, dw_sx, dw_sbc, db_sx, db_sbc, dw_conf, db_conf,
                               dw_ffn, db_ffn)
    gbig = None if reducer is not None else dict(w_in=g_in, w_out=g_out, w_up=g_up, w_down=g_down)
    return st_down[2, 0], grad_x, gbig, gsmall


VECTORS = ("ada_b", "norm1_w", "ssd_conv_b", "dt_bias", "a_log", "d_skip", "ssd_norm_w", "conf_conv_b", "conf_ln_w",
           "conf_ln_b", "norm2_w", "ffn_conv_b", "final_norm_w")
VECTOR_SIZES = (6 * D, D, D_XBC, HEADS, HEADS, HEADS, D, D, D, D, D, 2 * D_FF, D)
CONVS = {"ssd_conv_w": (K_SSD, D_XBC), "conf_conv_w": (K_CONF, D), "ffn_conv_w": (K_FFN, 2 * D_FF)}


def _pack_rows(items):
    n = -(-sum(w for _, w in items) // (8 * LANES)) * LANES
    while True:
        fill, place = [0] * 8, {}
        for key, w in sorted(items, key=lambda kv: -kv[1]):
            rows = [r for r in range(8) if fill[r] + w <= n]
            if not rows:
                break
            place[key] = (rows[0], fill[rows[0]])
            fill[rows[0]] += w
        if len(place) == len(items):
            return n, place
        n += LANES


FRONT_N, FRONT = _pack_rows([("c", D)] + [((nm, j), cols // 4) for nm, (taps, cols) in CONVS.items() for j in range(taps)])
BACK_N, BACK = _pack_rows([(nm, -(-sz // LANES) * LANES) for nm, sz in zip(VECTORS, VECTOR_SIZES)]
                          + [((nm, j), cols) for nm, (taps, cols) in CONVS.items() for j in range(taps)] + [("loss", LANES)])
_VM = pltpu.CompilerParams(vmem_limit_bytes=VMEM_LIMIT)


def _pack_front(c, shards):
    def body(c_ref, *refs):
        o_ref = refs[-1]
        o_ref[...] = jnp.zeros_like(o_ref)
        r, o = FRONT["c"]
        o_ref[r:r + 1, o:o + D] = c_ref[...]
        for ref, (nm, (taps, cols)) in zip(refs, CONVS.items()):
            for j in range(taps):
                r, o = FRONT[(nm, j)]
                o_ref[r:r + 1, o:o + cols // 4] = ref[0, j:j + 1, :]

    return pl.pallas_call(body, name="pack_front", out_shape=jax.ShapeDtypeStruct((8, FRONT_N), f32),
                          compiler_params=_VM)(c, *shards)


def _unpack_front(got):
    def body(g_ref, c_ref, *outs):
        r, o = FRONT["c"]
        for d in range(8):
            c_ref[d:d + 1, :] = g_ref[8 * d + r:8 * d + r + 1, o:o + D]
        for ref, (nm, (taps, cols)) in zip(outs, CONVS.items()):
            cw = cols // 4
            for j in range(taps):
                r, o = FRONT[(nm, j)]
                for k in range(4):
                    ref[j:j + 1, k * cw:(k + 1) * cw] = g_ref[16 * k + r:16 * k + r + 1, o:o + cw]

    return pl.pallas_call(
        body, name="unpack_front", compiler_params=_VM,
        out_shape=(jax.ShapeDtypeStruct((8, D), f32),) + tuple(jax.ShapeDtypeStruct(tc, f32) for tc in CONVS.values()),
    )(got)


def _pack_small_grads(st_in, st_up, st_down, st_ln, st_gn, d_alog, dw_sx, dw_sbc, db_sx, db_sbc, dw_conf, db_conf, dw_ffn,
                      db_ffn):
    def body(in_ref, up_ref, dn_ref, ln_ref, gn_ref, al_ref, wx_ref, wbc_ref, bx_ref, bbc_ref, wc_ref, bc_ref, wf_ref, bf_ref,
             o_ref):
        def put(key, val, shift=0):
            r, o = BACK[key]
            o_ref[r:r + 1, o + shift:o + shift + val.shape[1]] = val

        o_ref[...] = jnp.zeros_like(o_ref)
        for i, piece in enumerate((in_ref[0:1, :], in_ref[1:2, :], up_ref[3:4, :], up_ref[0:1, :], up_ref[1:2, :],
                                   dn_ref[1:2, :])):
            put("ada_b", piece, i * D)
        put("norm1_w", in_ref[2:3, :])
        put("ssd_conv_b", bx_ref[...])
        put("ssd_conv_b", bbc_ref[...], D)
        put("dt_bias", in_ref[3:4, 0:LANES])
        put("a_log", al_ref[...])
        lane = lax.broadcasted_iota(jnp.int32, (1, LANES), 1)
        col = lax.broadcasted_iota(jnp.int32, (1, D), 1)
        per_col = gn_ref[1:2, :]
        d_skip = jnp.zeros((1, LANES), f32)
        for h in range(HEADS):
            in_head = jnp.logical_and(col >= h * HEAD_P, col < (h + 1) * HEAD_P)
            s = jnp.sum(jnp.where(in_head, per_col, 0.0), axis=1, keepdims=True)
            d_skip = d_skip + jnp.where(lane == h, s, 0.0)
        put("d_skip", d_skip)
        put("ssd_norm_w", gn_ref[0:1, :])
        put("conf_conv_b", bc_ref[...])
        put("conf_ln_w", ln_ref[0:1, :])
        put("conf_ln_b", ln_ref[1:2, :])
        put("norm2_w", up_ref[2:3, :])
        put("ffn_conv_b", bf_ref[0])
        put("ffn_conv_b", bf_ref[1], D_FF)
        put("final_norm_w", dn_ref[0:1, :])
        put("loss", dn_ref[2:3, 0:LANES])
        for j in range(K_SSD):
            put(("ssd_conv_w", j), wx_ref[j:j + 1, :])
            put(("ssd_conv_w", j), wbc_ref[j:j + 1, :], D)
        for j in range(K_CONF):
            put(("conf_conv_w", j), wc_ref[j:j + 1, :])
        for j in range(K_FFN):
            put(("ffn_conv_w", j), wf_ref[0, j:j + 1, :])
            put(("ffn_conv_w", j), wf_ref[1, j:j + 1, :], D_FF)

    return pl.pallas_call(body, name="pack_small_grads", out_shape=jax.ShapeDtypeStruct((8, BACK_N), f32), compiler_params=_VM)(
        st_in, st_up, st_down, st_ln, st_gn, d_alog, dw_sx, dw_sbc, db_sx, db_sbc, dw_conf, db_conf, dw_ffn, db_ffn)


def _small_adamw(got, chip, w, m, v):
    names = VECTORS + tuple(CONVS)
    n_par = len(names)

    def body(chip_ref, g_ref, *refs):
        ins, outs = refs[:3 * n_par], refs[3 * n_par:]
        dm_ref, loss_ref, outs = outs[0], outs[1], outs[2:]
        chip_id = chip_ref[0]

        def summed(key, width):
            r, o = BACK[key]
            s = g_ref[r:r + 1, o:o + width]
            for d in range(1, 8):
                s = s + g_ref[8 * d + r:8 * d + r + 1, o:o + width]
            return s

        def mine(full, cw):
            out = full[:, 0:cw]
            for k in range(1, 4):
                out = jnp.where(chip_id == k, full[:, k * cw:(k + 1) * cw], out)
            return out

        r, o = BACK["ada_b"]
        for d in range(8):
            dm_ref[d:d + 1, :] = mine(g_ref[8 * d + r:8 * d + r + 1, o:o + 6 * D], 6 * D // 4)
        loss_ref[...] = summed("loss", LANES)
        for i, (nm, size) in enumerate(zip(VECTORS, VECTOR_SIZES)):
            g = summed(nm, -(-size // LANES) * LANES)[:, 0:size]
            res = _adam_math(ins[3 * i][...], g, ins[3 * i + 1][...], ins[3 * i + 2][...])
            for ref, val in zip(outs[4 * i:4 * i + 4], (g,) + res):
                ref[...] = val
        for i, (nm, (taps, cols)) in enumerate(CONVS.items(), start=len(VECTORS)):
            for j in range(taps):
                g = mine(summed((nm, j), cols), cols // 4)
                res = _adam_math(ins[3 * i][0, j:j + 1, :], g, ins[3 * i + 1][0, j:j + 1, :], ins[3 * i + 2][0, j:j + 1, :])
                for ref, val in zip(outs[4 * i:4 * i + 4], (g,) + res):
                    ref[0, j:j + 1, :] = val

    params = [a[nm] for nm in names for a in (w, m, v)]
    whole = lambda s: pl.BlockSpec(s, lambda i, chip, nd=len(s): (0,) * nd)
    out_shape = [jax.ShapeDtypeStruct((8, 6 * D // 4), f32), jax.ShapeDtypeStruct((1, LANES), f32)]
    out_shape += [jax.ShapeDtypeStruct(w[nm].shape, f32) for nm in names for _ in range(4)]
    outs = pl.pallas_call(
        body, name="small_adamw", out_shape=tuple(out_shape), compiler_params=_VM,
        grid_spec=pltpu.PrefetchScalarGridSpec(
            num_scalar_prefetch=1, grid=(1,), in_specs=[whole(got.shape)] + [whole(p.shape) for p in params],
            out_specs=tuple(whole(s.shape) for s in out_shape)),
    )(_scalar(chip), got, *params)
    return outs[0], outs[1][0, 0], {nm: outs[2 + 4 * i:6 + 4 * i] for i, nm in enumerate(names)}


W_IN_COLS = 4624
W_IN_SHARD = W_IN_COLS // 4
W_IN_SHARD_PAD = 1280
_SEGMENTS = ((0, 1024, OFF_Z), (1024, 2560, OFF_XBC), (2560, 2576, OFF_DT), (2576, 3600, OFF_CA), (3600, 4624, OFF_CG))


def _in_pieces(bounds=()):
    out = []
    for k in range(4):
        s0, s1 = k * W_IN_SHARD, (k + 1) * W_IN_SHARD
        for lo, hi, off in _SEGMENTS:
            a, b = max(lo, s0), min(hi, s1)
            while a < b:
                p = off + a - lo
                e = min([b - a] + [c - p for c in bounds if c > p])
                out.append((k, a - s0, p, e))
                a += e
    return out


def _pack_w_in(shards):
    pieces = _in_pieces()

    def body(s_ref, o_ref):
        o_ref[:, OFF_DT:W_PACK] = jnp.zeros((TM, W_PACK - OFF_DT), MX)
        for k, c, p, n in pieces:
            o_ref[:, p:p + n] = s_ref[k, :, c:c + n]

    return pl.pallas_call(
        body, name="pack_w_in", grid=(D // TM,), out_shape=jax.ShapeDtypeStruct((D, W_PACK), MX),
        in_specs=[pl.BlockSpec((4, TM, W_IN_SHARD_PAD), lambda i: (0, i, 0))],
        out_specs=pl.BlockSpec((TM, W_PACK), lambda i: (i, 0)), compiler_params=_cp("arbitrary"),
    )(shards)


def _unpack_g_in(g):
    srcs = ((OFF_Z, D), (OFF_XBC, D), (OFF_XBC + D, 2 * CW), (OFF_CA, D), (OFF_CG, D), (OFF_DT, LANES))
    pieces = _in_pieces(tuple(o for o, _ in srcs) + tuple(o + n for o, n in srcs))

    def body(z_ref, x_ref, bc_ref, cf_ref, dt_ref, o_ref):
        read = (lambda lo, hi: z_ref[:, lo:hi], lambda lo, hi: x_ref[:, lo:hi], lambda lo, hi: bc_ref[:, lo:hi],
                lambda lo, hi: cf_ref[0, :, lo:hi], lambda lo, hi: cf_ref[1, :, lo:hi], lambda lo, hi: dt_ref[:, lo:hi])
        o_ref[:, :, W_IN_SHARD - 4:W_IN_SHARD_PAD] = jnp.zeros((4, TM, W_IN_SHARD_PAD - W_IN_SHARD + 4), MX)
        for k, c, p, n in pieces:
            i = [q for q, (o, w) in enumerate(srcs) if o <= p < o + w][0]
            o_ref[k, :, c:c + n] = read[i](p - srcs[i][0], p - srcs[i][0] + n)

    blk = lambda w: pl.BlockSpec((TM, w), lambda i: (i, 0))
    return pl.pallas_call(
        body, name="unpack_g_in", grid=(D // TM,), out_shape=jax.ShapeDtypeStruct((4, D, W_IN_SHARD_PAD), MX),
        in_specs=[blk(D), blk(D), blk(2 * CW), pl.BlockSpec((2, TM, D), lambda i: (0, i, 0)), blk(LANES)],
        out_specs=pl.BlockSpec((4, TM, W_IN_SHARD_PAD), lambda i: (0, i, 0)), compiler_params=_cp("arbitrary"),
    )(g["z"], g["x"], g["bc"], g["conf"], g["dt"])


def _scalar(v):
    return jnp.reshape(v, (1,)).astype(jnp.int32)


def _cast_into_slot(w, width, chip):
    r, c = w.shape
    h = r // 2
    tm = _row_tile(h)
    nj = h // tm

    def body(chip_ref, w_ref, o_ref):
        v = w_ref[...].astype(MX)
        o_ref[0, 0] = v if width == c else jnp.concatenate([v, jnp.zeros((tm, width - c), MX)], axis=1)

    return pl.pallas_call(
        body, name=f"cast_into_slot_{r}x{c}", out_shape=jax.ShapeDtypeStruct((4, 2, h, width), MX),
        grid_spec=pltpu.PrefetchScalarGridSpec(
            num_scalar_prefetch=1, grid=(2, nj),
            in_specs=[pl.BlockSpec((tm, c), lambda i, j, chip: (i * nj + j, 0))],
            out_specs=pl.BlockSpec((1, 1, tm, width), lambda i, j, chip: (chip[0], i, j, 0))),
        compiler_params=_cp("arbitrary", "arbitrary"),
    )(_scalar(chip), w)


def _columns_first(w):
    return jnp.transpose(w, (2, 0, 1))


def _cast_into_slot_w_in(w_t, chip):
    h = D // 2
    nj = h // TM
    pad = W_IN_SHARD_PAD - W_IN_SHARD

    def body(chip_ref, w_ref, o_ref):
        cols = jnp.concatenate([w_ref[:, 0, :], jnp.zeros((pad, TM), f32)], axis=0)
        o_ref[0, 0] = cols.T.astype(MX)

    return pl.pallas_call(
        body, name="cast_into_slot_w_in", out_shape=jax.ShapeDtypeStruct((4, 2, h, W_IN_SHARD_PAD), MX),
        grid_spec=pltpu.PrefetchScalarGridSpec(
            num_scalar_prefetch=1, grid=(2, nj),
            in_specs=[pl.BlockSpec((W_IN_SHARD, 1, TM), lambda i, j, chip: (0, 0, i * nj + j))],
            out_specs=pl.BlockSpec((1, 1, TM, W_IN_SHARD_PAD), lambda i, j, chip: (chip[0], i, j, 0))),
        compiler_params=_cp("arbitrary", "arbitrary"),
    )(_scalar(chip), w_t)


def _adamw_w_in(w_t, mine, other, m_t, v_t, core):
    h = D // 2
    nj = h // TM

    def body(core_ref, w_ref, a_ref, b_ref, m_ref, v_ref, g_ref, d_ref, nm_ref, nv_ref):
        g = jnp.where(pl.program_id(0) == core_ref[0], a_ref[...], b_ref[...]).T[0:W_IN_SHARD, :]
        g_ref[:, 0, :] = g
        d_ref[:, 0, :], nm_ref[:, 0, :], nv_ref[:, 0, :] = _adam_math(w_ref[:, 0, :], g, m_ref[:, 0, :], v_ref[:, 0, :])

    blk = pl.BlockSpec((W_IN_SHARD, 1, TM), lambda i, j, core: (0, 0, i * nj + j))
    gblk = pl.BlockSpec((TM, W_IN_SHARD_PAD), lambda i, j, core: (j, 0))
    return pl.pallas_call(
        body, name="adamw_w_in", out_shape=tuple([jax.ShapeDtypeStruct((W_IN_SHARD, 1, D), f32)] * 4),
        grid_spec=pltpu.PrefetchScalarGridSpec(
            num_scalar_prefetch=1, grid=(2, nj), in_specs=[blk, gblk, gblk, blk, blk], out_specs=(blk,) * 4),
        compiler_params=_cp("arbitrary", "arbitrary"),
    )(_scalar(core), w_t, mine, other, m_t, v_t)


ANY = pl.BlockSpec(memory_space=pl.ANY)


def _place():
    x, y, c = lax.axis_index("x"), lax.axis_index("y"), lax.axis_index("c")
    return x, y, c, [(1 - x, y), (x, 1 - y), (1 - x, 1 - y)]


_GATHER_SEMS = [pltpu.SemaphoreType.DMA((7,)), pltpu.SemaphoreType.DMA((7,)), pltpu.SemaphoreType.DMA]


def _gather_rows_steps(x_ref, out_ref, send_sems, recv_sems, local_sem, after_first=None):
    m_per = x_ref.shape[0]
    x, y, c, chips = _place()
    me, sibling = (x, y, c), (x, y, 1 - c)

    def rows(px, py, pc):
        return out_ref.at[pl.ds((4 * px + 2 * py + pc) * m_per, m_per), :]

    def copy(k, blk, to, src=None):
        return pltpu.make_async_remote_copy(
            src_ref=rows(*blk) if src is None else src, dst_ref=rows(*blk), send_sem=send_sems.at[k],
            recv_sem=recv_sems.at[k], device_id=to, device_id_type=MESH)

    mine = pltpu.make_async_copy(x_ref, rows(*me), local_sem)
    mine.start()
    first = [copy(0, me, sibling, src=x_ref)]
    first += [copy(1 + j, me, (*chip, c), src=x_ref) for j, chip in enumerate(chips)]
    for cp in first:
        cp.start()
    if after_first is not None:
        after_first()
    passed = [copy(4 + j, (*chip, c), sibling) for j, chip in enumerate(chips)]
    for j, chip in enumerate(chips):
        copy(1 + j, (*chip, c), me).wait_recv()
        passed[j].start()
    copy(0, sibling, me).wait_recv()
    for j, chip in enumerate(chips):
        copy(4 + j, (*chip, 1 - c), me).wait_recv()
    for cp in first + passed:
        cp.wait_send()
    mine.wait()


def _gather_rows(block):
    m_per, n = block.shape

    def body(x_ref, out_ref, send_sems, recv_sems, local_sem):
        _gather_rows_steps(x_ref, out_ref, send_sems, recv_sems, local_sem)

    vmem = pl.BlockSpec(memory_space=pltpu.VMEM)
    return pl.pallas_call(
        body, name=f"gather_rows_{m_per}x{n}", out_shape=jax.ShapeDtypeStruct((8 * m_per, n), block.dtype),
        in_specs=[vmem], out_specs=vmem, scratch_shapes=list(_GATHER_SEMS), compiler_params=_VM)(block)


def _front(block, ada_w, slot):
    half = slot.shape[2] // 2
    riders = [_GatherRider([slot], 0, half), _GatherRider([slot], half, half)]
    n_mod = ada_w.shape[1]

    def body(x_ref, w_ref, slot_ref, out_ref, mod_ref, slot_out, c_scr, mine_scr, *sems):
        r_scr = [sems[6:8], sems[8:10]]
        _gather_rows_steps(x_ref, out_ref, *sems[0:3], after_first=lambda: riders[0].start([slot_ref], [slot_out], r_scr[0]))
        r, o = FRONT["c"]
        for d in range(8):
            c_scr[d:d + 1, :] = out_ref[8 * d + r:8 * d + r + 1, o:o + D]
        mine_scr[...] = jnp.dot(_silu(c_scr[...]).astype(MX), w_ref[...].astype(MX), preferred_element_type=f32)
        _gather_rows_steps(mine_scr, mod_ref, *sems[3:6], after_first=lambda: riders[1].start([slot_ref], [slot_out], r_scr[1]))
        for rider, scr in zip(riders, r_scr):
            rider.finish([slot_ref], [slot_out], scr)

    vmem = pl.BlockSpec(memory_space=pltpu.VMEM)
    return pl.pallas_call(
        body, name="front",
        out_shape=(jax.ShapeDtypeStruct((64, block.shape[1]), f32), jax.ShapeDtypeStruct((64, n_mod), f32),
                   jax.ShapeDtypeStruct(slot.shape, slot.dtype)),
        in_specs=[vmem, vmem, ANY], out_specs=(vmem, vmem, ANY), input_output_aliases={2: 2},
        scratch_shapes=[pltpu.VMEM((8, D), f32), pltpu.VMEM((8, n_mod), f32)] + list(_GATHER_SEMS) * 2
        + riders[0].scratch + riders[1].scratch, compiler_params=_VM,
    )(block, ada_w, slot)


class _GatherRider:
    def __init__(self, slots, row0=0, nrows=None):
        n = len(slots)
        self.n = n
        self.rows = (row0, slots[0].shape[2] - row0 if nrows is None else nrows)
        self.inputs = list(slots)
        self.out_shape = [jax.ShapeDtypeStruct(s.shape, s.dtype) for s in slots]
        self.scratch = [pltpu.SemaphoreType.DMA((n, 6)), pltpu.SemaphoreType.DMA((n, 6))]
        self.aliases = {a: a for a in range(n)}

    def _copy(self, outs, sems, a, j, k, half, to):
        dst = outs[a].at[k, half, pl.ds(*self.rows)]
        return pltpu.make_async_remote_copy(src_ref=dst, dst_ref=dst, send_sem=sems[0].at[a, j], recv_sem=sems[1].at[a, j],
                                            device_id=to, device_id_type=MESH)

    def _first(self, outs, sems):
        x, y, c, chips = _place()
        return [self._copy(outs, sems, a, j, 2 * x + y, c, (*chip, c)) for a in range(self.n) for j, chip in enumerate(chips)]

    def start(self, ins, outs, sems):
        for cp in self._first(outs, sems):
            cp.start()

    def finish(self, ins, outs, sems):
        x, y, c, chips = _place()
        passed = []
        for a in range(self.n):
            for j, (px, py) in enumerate(chips):
                self._copy(outs, sems, a, j, 2 * px + py, c, (x, y, c)).wait_recv()
                fwd = self._copy(outs, sems, a, 3 + j, 2 * px + py, c, (x, y, 1 - c))
                fwd.start()
                passed.append(fwd)
        for a in range(self.n):
            for j, (px, py) in enumerate(chips):
                self._copy(outs, sems, a, 3 + j, 2 * px + py, 1 - c, (x, y, c)).wait_recv()
        for cp in self._first(outs, sems) + passed:
            cp.wait_send()


class _ScatterRider:
    def __init__(self, parts):
        n = len(parts)
        self.n = n
        self.inputs = list(parts)
        self.out_shape = [jax.ShapeDtypeStruct((3,) + p.shape[1:], p.dtype) for p in parts]
        self.scratch = [pltpu.SemaphoreType.DMA((3 * n,)), pltpu.SemaphoreType.DMA((3 * n,))]
        self.aliases = {}

    def _copies(self, ins, outs, sems):
        x, y, c, chips = _place()
        return [pltpu.make_async_remote_copy(
            src_ref=ins[a].at[2 * px + py], dst_ref=outs[a].at[j], send_sem=sems[0].at[3 * a + j],
            recv_sem=sems[1].at[3 * a + j], device_id=(px, py, c), device_id_type=MESH)
            for a in range(self.n) for j, (px, py) in enumerate(chips)]

    def start(self, ins, outs, sems):
        for cp in self._copies(ins, outs, sems):
            cp.start()

    def finish(self, ins, outs, sems):
        for cp in self._copies(ins, outs, sems):
            cp.wait()


HBM = pl.BlockSpec(memory_space=pltpu.HBM)
SEM = pl.BlockSpec(memory_space=pltpu.SEMAPHORE)
EFFECT = pltpu.SideEffectType.DATAFLOW_SIDE_EFFECTING


def _split_start(rider, name, after=None):
    ni, no, ns = len(rider.inputs), len(rider.out_shape), len(rider.scratch)
    extra = [] if after is None else [after]

    def body(*refs):
        ins, lands = refs[:ni], refs[ni:ni + no]
        sems = refs[ni + no + len(extra):ni + no + len(extra) + ns]
        rider.start(ins, lands, sems)
        refs[-1][...] = jnp.zeros_like(refs[-1])

    bufs = list(rider.inputs) + [lax.empty(s.shape, s.dtype) for s in rider.out_shape]
    outs = pl.pallas_call(
        body, name=name,
        out_shape=tuple(rider.scratch) + tuple(pltpu.HBM(b.shape, b.dtype) for b in bufs) + (jax.ShapeDtypeStruct((8, LANES), f32),),
        in_specs=[HBM] * (ni + no) + [ANY] * len(extra),
        out_specs=(SEM,) * ns + (HBM,) * (ni + no) + (pl.BlockSpec(memory_space=pltpu.VMEM),),
        input_output_aliases={i: ns + i for i in range(ni + no)},
        compiler_params=pltpu.CompilerParams(has_side_effects=EFFECT),
    )(*[pltpu.with_memory_space_constraint(b, pltpu.HBM) for b in bufs], *extra)
    return outs[:-1], outs[-1]


def _split_wait(rider, name, handles, after):
    ni, no, ns = len(rider.inputs), len(rider.out_shape), len(rider.scratch)
    sems, bufs = handles[:ns], handles[ns:]

    def body(*refs):
        rider.finish(refs[:ni], refs[ni:ni + no], refs[ni + no:ni + no + ns])

    outs = pl.pallas_call(
        body, name=name, out_shape=tuple(pltpu.HBM(b.shape, b.dtype) for b in bufs),
        in_specs=[HBM] * (ni + no) + [SEM] * ns + [ANY], out_specs=(HBM,) * (ni + no),
        input_output_aliases={i: i for i in range(ni + no)}, compiler_params=pltpu.CompilerParams(has_side_effects=EFFECT),
    )(*bufs, *sems, after)
    return outs[:ni], outs[ni:]


def _ride_alone(rider, name):
    n = len(rider.inputs)

    def body(*refs):
        ins, outs, sems = refs[:n], refs[n:n + len(rider.out_shape)], refs[n + len(rider.out_shape):]
        rider.start(ins, outs, sems)
        rider.finish(ins, outs, sems)

    return pl.pallas_call(
        body, name=name, out_shape=tuple(rider.out_shape), in_specs=[ANY] * n, out_specs=tuple([ANY] * len(rider.out_shape)),
        input_output_aliases=dict(rider.aliases), scratch_shapes=list(rider.scratch),
    )(*rider.inputs)


class _SwapRider:
    def __init__(self, grads):
        n = len(grads)
        self.n = n
        self.inputs = list(grads)
        self.out_shape = [jax.ShapeDtypeStruct((4,) + g.shape[2:], g.dtype) for g in grads]
        self.scratch = [pltpu.SemaphoreType.DMA((4 * n,)), pltpu.SemaphoreType.DMA((4 * n,))]
        self.aliases = {}

    def _copies(self, ins, outs, sems):
        x, y, c, _ = _place()
        return [pltpu.make_async_remote_copy(
            src_ref=ins[a].at[k, 1 - c], dst_ref=outs[a].at[k], send_sem=sems[0].at[4 * a + k], recv_sem=sems[1].at[4 * a + k],
            device_id=(x, y, 1 - c), device_id_type=MESH) for a in range(self.n) for k in range(4)]

    def start(self, ins, outs, sems):
        for cp in self._copies(ins, outs, sems):
            cp.start()

    def finish(self, ins, outs, sems):
        for cp in self._copies(ins, outs, sems):
            cp.wait()


class _Riders:
    def __init__(self, riders):
        self.riders = list(riders)
        self.inputs = [a for r in riders for a in r.inputs]
        self.out_shape = [s for r in riders for s in r.out_shape]
        self.scratch = [s for r in riders for s in r.scratch]
        self.aliases = {}
        i = o = 0
        for r in riders:
            self.aliases.update({i + a: o + b for a, b in r.aliases.items()})
            i, o = i + len(r.inputs), o + len(r.out_shape)

    def _each(self, ins, outs, sems):
        i = o = s = 0
        for r in self.riders:
            yield r, ins[i:i + len(r.inputs)], outs[o:o + len(r.out_shape)], sems[s:s + len(r.scratch)]
            i, o, s = i + len(r.inputs), o + len(r.out_shape), s + len(r.scratch)

    def start(self, ins, outs, sems):
        for r, a, b, c in self._each(ins, outs, sems):
            r.start(a, b, c)

    def finish(self, ins, outs, sems):
        for r, a, b, c in self._each(ins, outs, sems):
            r.finish(a, b, c)

    def split(self, outs):
        res, o = [], 0
        for r in self.riders:
            res.append(outs[o:o + len(r.out_shape)])
            o += len(r.out_shape)
        return res


class _Reducer:
    def __init__(self, chip, core):
        self.chip, self.core, self.grads, self.parts, self.sums, self.others = chip, core, {}, {}, {}, {}

    def swap(self, name, grad):
        self.grads[name] = grad
        return _SwapRider([grad])

    def swapped(self, name, got):
        self.parts[name] = _add_pair(self.grads[name], got[0], self.core, name)

    def scatter(self, name):
        return _ScatterRider([self.parts[name]])

    def scattered(self, name, others):
        self.sums[name] = _add_chips(self.parts[name], others[0], self.chip, name)


class _SwapSumsRider:
    def __init__(self, halves):
        n = len(halves)
        self.n = n
        self.inputs = list(halves)
        self.out_shape = [jax.ShapeDtypeStruct(s.shape, s.dtype) for s in halves]
        self.scratch = [pltpu.SemaphoreType.DMA((n,)), pltpu.SemaphoreType.DMA((n,))]
        self.aliases = {}

    def _copies(self, ins, outs, sems):
        x, y, c, _ = _place()
        return [pltpu.make_async_remote_copy(
            src_ref=ins[a], dst_ref=outs[a], send_sem=sems[0].at[a], recv_sem=sems[1].at[a],
            device_id=(x, y, 1 - c), device_id_type=MESH) for a in range(self.n)]

    def start(self, ins, outs, sems):
        for cp in self._copies(ins, outs, sems):
            cp.start()

    def finish(self, ins, outs, sems):
        for cp in self._copies(ins, outs, sems):
            cp.wait()


def _row_tile(r):
    for tm in (TM, 176, 128, 64, 32, 16, 8):
        if r % tm == 0:
            return tm
    return r


def _add_pair(mine, got, core, name):
    k, _, h, c = mine.shape
    tm = _row_tile(h)

    def body(core_ref, a_ref, b_ref, o_ref):
        o_ref[0] = (a_ref[0, 0].astype(f32) + b_ref[0].astype(f32)).astype(MX)

    blk = pl.BlockSpec((1, tm, c), lambda i, j, core: (i, j, 0))
    return pl.pallas_call(
        body, name="add_pair_" + name, out_shape=jax.ShapeDtypeStruct((k, h, c), MX),
        grid_spec=pltpu.PrefetchScalarGridSpec(
            num_scalar_prefetch=1, grid=(k, h // tm),
            in_specs=[pl.BlockSpec((1, 1, tm, c), lambda i, j, core: (i, core[0], j, 0)), blk], out_specs=blk),
        compiler_params=_cp("arbitrary", "arbitrary"),
    )(_scalar(core), mine, got)


def _add_chips(parts, others, chip, name):
    _, n, c = others.shape
    tm = _row_tile(n)

    def body(chip_ref, a_ref, b_ref, o_ref):
        s = a_ref[0].astype(f32) + b_ref[0].astype(f32)
        o_ref[...] = (s + b_ref[1].astype(f32)) + b_ref[2].astype(f32)

    return pl.pallas_call(
        body, name="add_chips_" + name, out_shape=jax.ShapeDtypeStruct((n, c), f32),
        grid_spec=pltpu.PrefetchScalarGridSpec(
            num_scalar_prefetch=1, grid=(n // tm,),
            in_specs=[pl.BlockSpec((1, tm, c), lambda i, chip: (chip[0], i, 0)),
                      pl.BlockSpec((3, tm, c), lambda i, chip: (0, i, 0))],
            out_specs=pl.BlockSpec((tm, c), lambda i, chip: (i, 0))),
        compiler_params=_cp("arbitrary"),
    )(_scalar(chip), parts, others)


def _adam_math(w, g, m, v):
    m = ADAM_B1 * m + (1.0 - ADAM_B1) * g
    v = ADAM_B2 * v + (1.0 - ADAM_B2) * (g * g)
    m_hat = m / (1.0 - ADAM_B1 ** ADAM_STEP)
    v_hat = v / (1.0 - ADAM_B2 ** ADAM_STEP)
    return -ADAM_LR * (m_hat / (jnp.sqrt(v_hat) + ADAM_EPS) + ADAM_WD * w), m, v


def _adamw_halves(w, mine, other, m, v, core, name, after):
    r, c = w.shape
    h = r // 2
    tm = _row_tile(h)
    nj = h // tm
    cg = mine.shape[1]

    def body(core_ref, w_ref, a_ref, b_ref, m_ref, v_ref, after_ref, g_ref, d_ref, nm_ref, nv_ref):
        g = jnp.where(pl.program_id(0) == core_ref[0], a_ref[:, 0:c], b_ref[:, 0:c])
        g_ref[...] = g
        d_ref[...], nm_ref[...], nv_ref[...] = _adam_math(w_ref[...], g, m_ref[...], v_ref[...])

    blk = pl.BlockSpec((tm, c), lambda i, j, core: (i * nj + j, 0))
    gblk = pl.BlockSpec((tm, cg), lambda i, j, core: (j, 0))
    return _call(body, name=name, grid=(2, nj), out_shape=[jax.ShapeDtypeStruct((r, c), f32)] * 4,
                 in_specs=[blk, gblk, gblk, blk, blk, ANY], out_specs=(blk,) * 4, sem=("arbitrary", "arbitrary"),
                 prefetch=(_scalar(core),), args=(w, mine, other, m, v, after))[0]


def _ada_adamw(c_all_t, d_mod, w, m, v, after):
    r, c = w.shape
    tm = TM

    def body(ct_ref, dm_ref, w_ref, m_ref, v_ref, after_ref, g_ref, d_ref, nm_ref, nv_ref):
        ca = _silu(ct_ref[...])
        g = ca[:, 0:1] * dm_ref[0:1, :]
        for b in range(1, 8):
            g = g + ca[:, b:b + 1] * dm_ref[b:b + 1, :]
        g_ref[...] = g
        d_ref[...], nm_ref[...], nv_ref[...] = _adam_math(w_ref[...], g, m_ref[...], v_ref[...])

    blk = pl.BlockSpec((tm, c), lambda i: (i, 0))
    return _call(body, name="ada_adamw", grid=(r // tm,), out_shape=[jax.ShapeDtypeStruct((r, c), f32)] * 4,
                 in_specs=[pl.BlockSpec((tm, 8), lambda i: (i, 0)), pl.BlockSpec((8, c), lambda i: (0, 0)), blk, blk, blk, ANY],
                 out_specs=(blk,) * 4, sem=("arbitrary",), args=(c_all_t, d_mod, w, m, v, after))[0]


WEIGHTS = ("ada_w", "ada_b", "norm1_w", "w_in", "ssd_conv_w", "ssd_conv_b", "dt_bias", "a_log", "d_skip", "ssd_norm_w",
           "conf_conv_w", "conf_conv_b", "conf_ln_w", "conf_ln_b", "w_out", "norm2_w", "w_up", "ffn_conv_w", "ffn_conv_b",
           "w_down", "final_norm_w")


def kernel(x, c, ada_w, ada_b, norm1_w, w_in, ssd_conv_w, ssd_conv_b, dt_bias, a_log, d_skip, ssd_norm_w, conf_conv_w, conf_conv_b, conf_ln_w, conf_ln_b, w_out, norm2_w, w_up, ffn_conv_w, ffn_conv_b, w_down, final_norm_w, loss_target, m_ada_w, m_ada_b, m_norm1_w, m_w_in, m_ssd_conv_w, m_ssd_conv_b, m_dt_bias, m_a_log, m_d_skip, m_ssd_norm_w, m_conf_conv_w, m_conf_conv_b, m_conf_ln_w, m_conf_ln_b, m_w_out, m_norm2_w, m_w_up, m_ffn_conv_w, m_ffn_conv_b, m_w_down, m_final_norm_w, v_ada_w, v_ada_b, v_norm1_w, v_w_in, v_ssd_conv_w, v_ssd_conv_b, v_dt_bias, v_a_log, v_d_skip, v_ssd_norm_w, v_conf_conv_w, v_conf_conv_b, v_conf_ln_w, v_conf_ln_b, v_w_out, v_norm2_w, v_w_up, v_ffn_conv_w, v_ffn_conv_b, v_w_down, v_final_norm_w):
    given = dict(locals())
    w = {n: given[n] for n in WEIGHTS}
    mom = {n: given["m_" + n] for n in WEIGHTS}
    var = {n: given["v_" + n] for n in WEIGHTS}
    chip = 2 * lax.axis_index("x") + lax.axis_index("y")
    me = 2 * chip + lax.axis_index("c")

    core = lax.axis_index("c")
    got, mod_cols, a_in = _front(_pack_front(c, [w[n] for n in CONVS]), ada_w[0], _cast_into_slot_w_in(_columns_first(w_in), chip))
    c_all, *convs = _unpack_front(got)
    conv_full = dict(zip(CONVS, convs))
    mod_cols = mod_cols.reshape(8, 8, -1)[0::2]
    mod = lax.dynamic_index_in_dim(mod_cols, me, axis=1, keepdims=False).reshape(1, 6 * D) + ada_b
    w_pack = _pack_w_in(a_in.reshape(4, D, W_IN_SHARD_PAD))
    late = (_cast_into_slot(w_out[0], D, chip), _cast_into_slot(w_up[0], UP_SHARD, chip), _cast_into_slot(w_down[0], D, chip))

    flat = lambda a: a.reshape(1, -1) if a.ndim == 1 else a
    small = {n: flat(w[n]) for n in VECTORS if n != "ada_b"}
    small.update(conv_full)
    reducer = _Reducer(chip, core)
    _, grad_x, _, gsmall = _local_step(x[0], mod, loss_target[0], w_pack, late, small, reducer)
    grads, delta, new_m, new_v = {}, {}, {}, {}

    names = VECTORS + tuple(CONVS)
    d_mod_mine, loss, res = _small_adamw(_gather_rows(gsmall), chip, *[{n: flat(d[n]) for n in names} for d in (w, mom, var)])
    for n in names:
        grads[n], delta[n], new_m[n], new_v[n] = [r.reshape(w[n].shape) for r in res[n]]

    scatter = reducer.scatter("w_in")
    handles, token = _split_start(scatter, "scatter_start_w_in", after=d_mod_mine)
    for n in ("w_up", "w_down", "w_out"):
        res = _adamw_halves(w[n][0], reducer.sums[n], reducer.others[n], mom[n][0], var[n][0], core, "adamw_" + n, token)
        grads[n], delta[n], new_m[n], new_v[n] = [r[None] for r in res]
    res = _ada_adamw(c_all.T, d_mod_mine, ada_w[0], m_ada_w[0], v_ada_w[0], token)
    grads["ada_w"], delta["ada_w"], new_m["ada_w"], new_v["ada_w"] = [r[None] for r in res]
    (reducer.parts["w_in"],), others = _split_wait(scatter, "scatter_wait_w_in", handles, res[1])
    reducer.scattered("w_in", others)
    reducer.others["w_in"], = _ride_alone(_SwapSumsRider([reducer.sums["w_in"]]), "swap_sums_w_in")
    res = _adamw_w_in(_columns_first(w_in), reducer.sums["w_in"], reducer.others["w_in"], _columns_first(m_w_in),
                      _columns_first(v_w_in), core)
    grads["w_in"], delta["w_in"], new_m["w_in"], new_v["w_in"] = [jnp.transpose(r, (1, 2, 0)) for r in res]

    return (loss, grad_x[None], *[grads[n] for n in WEIGHTS], *[delta[n] for n in WEIGHTS],
            *[new_m[n] for n in WEIGHTS], *[new_v[n] for n in WEIGHTS])
```

```python
import functools

import jax
import jax.numpy as jnp
from jax import lax
from jax.experimental import pallas as pl
from jax.experimental.pallas import tpu as pltpu

f32 = jnp.float32
MX = jnp.bfloat16

D = 1024
HEADS = 16
HEAD_P = 64
STATE_N = 128
D_XBC = 1536
D_FF = 2816
UP_SHARD = 2 * D_FF // 4
UP_EARLY_ROWS = 128
K_SSD, K_CONF, K_FFN = 4, 31, 3
CHUNK = 128
OFF_Z, OFF_XBC, OFF_CA, OFF_CG, OFF_DT = 0, 1024, 2560, 3584, 4608
W_PACK = 4736
TM = 256
CW = 256
RC = 64
LANES = 128
VMEM_LIMIT = 56 * 1024 * 1024

ADAM_LR, ADAM_B1, ADAM_B2, ADAM_EPS, ADAM_WD, ADAM_STEP = 0.001, 0.9, 0.999, 1e-08, 0.01, 10

MESH = pl.DeviceIdType.MESH


def _cp(*sem):
    return pltpu.CompilerParams(dimension_semantics=sem, vmem_limit_bytes=VMEM_LIMIT)


def _resident(shape):
    nd = len(shape)
    return pl.BlockSpec(shape, lambda *_: (0,) * nd, pipeline_mode=pl.Buffered(1))


def _row(width=D):
    return pl.BlockSpec((1, width), lambda *_: (0, 0))


def _call(body, *, name, grid, in_specs, out_specs, out_shape, args, sem, scratch_shapes=(), prefetch=(), rider=None):
    ni, no, ns, npf = len(in_specs), len(out_specs), len(scratch_shapes), len(prefetch)
    ri, ro = (len(rider.inputs), len(rider.out_shape)) if rider is not None else (0, 0)

    def full(*refs):
        pre, refs = refs[:npf], refs[npf:]
        base_in, r_in = refs[:ni], refs[ni:ni + ri]
        base_out, r_out = refs[ni + ri:ni + ri + no], refs[ni + ri + no:ni + ri + no + ro]
        base_scr, r_scr = refs[ni + ri + no + ro:ni + ri + no + ro + ns], refs[ni + ri + no + ro + ns:]
        if rider is None:
            return body(*pre, *base_in, *base_out, *base_scr)
        ids = [pl.program_id(a) for a in range(len(grid))]
        first = functools.reduce(jnp.logical_and, [i == 0 for i in ids])
        last = functools.reduce(jnp.logical_and, [i == g - 1 for i, g in zip(ids, grid)])

        @pl.when(first)
        def _():
            rider.start(r_in, r_out, r_scr)

        body(*pre, *base_in, *base_out, *base_scr)

        @pl.when(last)
        def _():
            rider.finish(r_in, r_out, r_scr)

    extra = dict(shapes=[], scratch=[], aliases={}, inputs=[]) if rider is None else dict(
        shapes=rider.out_shape, scratch=rider.scratch, inputs=rider.inputs,
        aliases={npf + ni + i: no + j for i, j in rider.aliases.items()})
    outs = pl.pallas_call(
        full, name=name, out_shape=tuple(out_shape) + tuple(extra["shapes"]), input_output_aliases=extra["aliases"],
        grid_spec=pltpu.PrefetchScalarGridSpec(
            num_scalar_prefetch=npf, grid=grid, in_specs=list(in_specs) + [ANY] * ri,
            out_specs=tuple(out_specs) + (ANY,) * ro, scratch_shapes=list(scratch_shapes) + list(extra["scratch"])),
        compiler_params=_cp(*sem),
    )(*prefetch, *args, *extra["inputs"])
    return tuple(outs[:no]), tuple(outs[no:])


def _silu(v):
    return v * jax.nn.sigmoid(v)


def _dsilu(v):
    s = jax.nn.sigmoid(v)
    return s * (1.0 + v * (1.0 - s))


def _softplus(v):
    return jnp.maximum(v, 0.0) + jnp.log1p(jnp.exp(-jnp.abs(v)))


def _mm(a, b):
    return jnp.dot(a.astype(MX), b.astype(MX), preferred_element_type=f32)


def _mm_nt(a, b):
    return lax.dot_general(a.astype(MX), b.astype(MX), (((1,), (1,)), ((), ())), preferred_element_type=f32)


def _mm_tn(a, b):
    return lax.dot_general(a.astype(MX), b.astype(MX), (((0,), (0,)), ((), ())), preferred_element_type=f32)


def _ln_inproj(x, mod, norm1_w, w_pack, rider=None):
    t = x.shape[0]

    def body(x_ref, mod_ref, nw_ref, w_ref, proj_ref, ht_ref):
        xv = x_ref[...]
        rstd = lax.rsqrt(jnp.mean(xv * xv, axis=-1, keepdims=True) + 1e-6)
        h = (xv * rstd * nw_ref[...]) * (1.0 + mod_ref[:, D:2 * D]) + mod_ref[:, 0:D]
        hb = h.astype(MX)
        ht_ref[...] = hb.T
        proj_ref[...] = jnp.dot(hb, w_ref[...], preferred_element_type=f32)

    return _call(
        body, name="ln_inproj", grid=(t // TM,),
        out_shape=(jax.ShapeDtypeStruct((t, W_PACK), f32), jax.ShapeDtypeStruct((D, t), MX)),
        in_specs=[pl.BlockSpec((TM, D), lambda i: (i, 0)), _row(6 * D), _row(), _resident((D, W_PACK))],
        out_specs=(pl.BlockSpec((TM, W_PACK), lambda i: (i, 0)), pl.BlockSpec((D, TM), lambda i: (0, i))),
        sem=("arbitrary",), args=(x, mod, norm1_w, w_pack), rider=rider)


def _ssd_gate_norm(y_scan, xbc_act, proj, d_skip_row, ssd_norm_w):
    t = y_scan.shape[0]

    def body(y_ref, xs_ref, z_ref, dsk_ref, nw_ref, o_ref):
        y = y_ref[...] + xs_ref[...] * dsk_ref[...]
        yz = y * _silu(z_ref[...])
        rstd = lax.rsqrt(jnp.mean(yz * yz, axis=-1, keepdims=True) + 1e-6)
        o_ref[...] = (yz * rstd * nw_ref[...]).astype(MX)

    blk = pl.BlockSpec((TM, D), lambda i: (i, 0))
    return pl.pallas_call(
        body, name="ssd_gate_norm", grid=(t // TM,), out_shape=jax.ShapeDtypeStruct((t, D), MX),
        in_specs=[blk, blk, blk, _row(), _row()], out_specs=blk, compiler_params=_cp("arbitrary"),
    )(y_scan, xbc_act, proj, d_skip_row, ssd_norm_w)


def _ln_silu(u_conv, ln_w, ln_b):
    t = u_conv.shape[0]

    def body(u_ref, w_ref, b_ref, o_ref):
        u = u_ref[...]
        mu = jnp.mean(u, axis=-1, keepdims=True)
        uc = u - mu
        rstd = lax.rsqrt(jnp.mean(uc * uc, axis=-1, keepdims=True) + 1e-5)
        o_ref[...] = _silu(uc * rstd * w_ref[...] + b_ref[...]).astype(MX)

    blk = pl.BlockSpec((TM, D), lambda i: (i, 0))
    return pl.pallas_call(
        body, name="ln_silu", grid=(t // TM,), out_shape=jax.ShapeDtypeStruct((t, D), MX),
        in_specs=[blk, _row(), _row()], out_specs=blk, compiler_params=_cp("arbitrary"),
    )(u_conv, ln_w, ln_b)


def _outproj_ln2_up(y_ssd, u, w_out, x, mod, norm2_w, w_up):
    t = x.shape[0]

    def body(y_ref, u_ref, wo_ref, x_ref, mod_ref, nw_ref, wu_ref, mix_ref, x1_ref, h2t_ref, up_ref):
        mix = jnp.dot(y_ref[...], wo_ref[0:D, :], preferred_element_type=f32)
        mix = mix + jnp.dot(u_ref[...], wo_ref[D:2 * D, :], preferred_element_type=f32)
        mix_ref[...] = mix
        x1 = x_ref[...] + mod_ref[:, 2 * D:3 * D] * mix
        x1_ref[...] = x1
        rstd = lax.rsqrt(jnp.mean(x1 * x1, axis=-1, keepdims=True) + 1e-6)
        h2 = ((x1 * rstd * nw_ref[...]) * (1.0 + mod_ref[:, 4 * D:5 * D]) + mod_ref[:, 3 * D:4 * D]).astype(MX)
        h2t_ref[...] = h2.T
        for k in range(4):
            up_ref[:, k * UP_SHARD:(k + 1) * UP_SHARD] = jnp.dot(h2, wu_ref[k], preferred_element_type=f32)

    blk = pl.BlockSpec((TM, D), lambda i: (i, 0))
    return pl.pallas_call(
        body, name="outproj_ln2_up", grid=(t // TM,),
        out_shape=(jax.ShapeDtypeStruct((t, D), f32), jax.ShapeDtypeStruct((t, D), f32),
                   jax.ShapeDtypeStruct((D, t), MX), jax.ShapeDtypeStruct((t, 2 * D_FF), f32)),
        in_specs=[blk, blk, _resident((2 * D, D)), blk, _row(6 * D), _row(), _resident((4, D, UP_SHARD))],
        out_specs=(blk, blk, pl.BlockSpec((D, TM), lambda i: (0, i)), pl.BlockSpec((TM, 2 * D_FF), lambda i: (i, 0))),
        compiler_params=_cp("arbitrary"),
    )(y_ssd, u, w_out, x, mod, norm2_w, w_up)


def _down_loss(act, w_down, x1, mod, final_norm_w, target):
    t = x1.shape[0]

    def body(a_ref, wd_ref, x1_ref, mod_ref, wf_ref, tgt_ref, dx2_ref, dffn_ref, dact_ref, st_ref):
        @pl.when(pl.program_id(0) == 0)
        def _():
            st_ref[...] = jnp.zeros_like(st_ref)

        g2 = mod_ref[:, 5 * D:6 * D]
        ffn = jnp.dot(a_ref[...], wd_ref[...], preferred_element_type=f32)
        x2 = x1_ref[...] + g2 * ffn
        rstd = lax.rsqrt(jnp.mean(x2 * x2, axis=-1, keepdims=True) + 1e-6)
        xh = x2 * rstd
        wf = wf_ref[...]
        err = xh * wf - tgt_ref[...]
        dy = err * (1.0 / D)
        dxh = dy * wf
        dx2 = rstd * (dxh - xh * jnp.mean(dxh * xh, axis=-1, keepdims=True))
        dx2_ref[...] = dx2
        dffn = (g2 * dx2).astype(MX)
        dffn_ref[...] = dffn
        dact_ref[...] = lax.dot_general(dffn, wd_ref[...], (((1,), (1,)), ((), ())), preferred_element_type=f32)
        st_ref[0:1, :] += jnp.sum(dy * xh, axis=0, keepdims=True)
        st_ref[1:2, :] += jnp.sum(dx2 * ffn, axis=0, keepdims=True)
        st_ref[2:3, :] += jnp.sum(0.5 * jnp.mean(err * err, axis=-1, keepdims=True), axis=0, keepdims=True)

    blk = pl.BlockSpec((TM, D), lambda i: (i, 0))
    ablk = pl.BlockSpec((TM, D_FF), lambda i: (i, 0))
    return pl.pallas_call(
        body, name="down_loss", grid=(t // TM,),
        out_shape=(jax.ShapeDtypeStruct((t, D), f32), jax.ShapeDtypeStruct((t, D), MX),
                   jax.ShapeDtypeStruct((t, D_FF), f32), jax.ShapeDtypeStruct((8, D), f32)),
        in_specs=[ablk, _resident((D_FF, D)), blk, _row(6 * D), _row(), blk],
        out_specs=(blk, blk, ablk, pl.BlockSpec((8, D), lambda i: (0, 0))),
        compiler_params=_cp("arbitrary"),
    )(act, w_down, x1, mod, final_norm_w, target)


def _pad_of(k):
    return 8 * ((k - 1 + 7) // 8)


def _causal_win(ref, r, t, pad):
    base = pl.multiple_of(r * RC, RC)
    prev = ref[pl.ds(pl.multiple_of(jnp.maximum(base - pad, 0), 8), pad), :]
    prev = jnp.where(r > 0, prev, 0.0)
    return jnp.concatenate([prev, ref[pl.ds(base, RC), :]], axis=0)


def _anti_win(ref, r, t, pad):
    base = pl.multiple_of(r * RC, RC)
    nxt = ref[pl.ds(pl.multiple_of(jnp.minimum(base + RC, t - pad), 8), pad), :]
    nxt = jnp.where(r < t // RC - 1, nxt, 0.0)
    return jnp.concatenate([ref[pl.ds(base, RC), :], nxt], axis=0)


def _shifted(win, offsets):
    for r in range(8):
        mine = [o for o in offsets if o % 8 == r]
        if mine:
            rolled = win if r == 0 else pltpu.roll(win, win.shape[0] - r, 0)
            for o in mine:
                yield o, rolled[o - r:o - r + RC, :]


def _conv_taps(win, w_ref, k, pad):
    first = pad - (k - 1)
    acc = None
    for o, rows in _shifted(win, range(first, first + k)):
        term = w_ref[o - first:o - first + 1, :] * rows
        acc = term if acc is None else acc + term
    return acc


def _corr_taps(win, w_ref, k):
    acc = None
    for o, rows in _shifted(win, range(k)):
        term = w_ref[k - 1 - o:k - o, :] * rows
        acc = term if acc is None else acc + term
    return acc


def _dw_accumulate(dw_scr, d, win, k, pad):
    first = pad - (k - 1)
    for o, rows in _shifted(win, range(first, first + k)):
        j = o - first
        prod = d * rows
        dw_scr[8 * j:8 * j + 8, :] += prod.reshape(RC // 8, 8, prod.shape[-1]).sum(axis=0)


def _dw_finish(dw_scr, dw_ref, k):
    for j in range(k):
        dw_ref[j:j + 1, :] = jnp.sum(dw_scr[8 * j:8 * j + 8, :], axis=0, keepdims=True)


def _rows8(v):
    return v.reshape(RC // 8, 8, v.shape[-1]).sum(axis=0)


def _ssd_conv_fwd(proj, conv_w, conv_b, rider=None):
    t = proj.shape[0]
    pad = _pad_of(K_SSD)
    c0 = OFF_XBC // CW

    def body(x_ref, w_ref, b_ref, o_ref):
        def step(r, carry):
            win = _causal_win(x_ref, r, t, pad)
            o_ref[pl.ds(pl.multiple_of(r * RC, RC), RC), :] = _silu(_conv_taps(win, w_ref, K_SSD, pad) + b_ref[...])
            return carry
        lax.fori_loop(0, t // RC, step, 0)

    return _call(
        body, name="ssd_conv_fwd", grid=(D_XBC // CW,), out_shape=(jax.ShapeDtypeStruct((t, D_XBC), f32),),
        in_specs=[pl.BlockSpec((t, CW), lambda j: (0, c0 + j)), pl.BlockSpec((K_SSD, CW), lambda j: (0, j)),
                  pl.BlockSpec((1, CW), lambda j: (0, j))],
        out_specs=(pl.BlockSpec((t, CW), lambda j: (0, j)),), sem=("arbitrary",), args=(proj, conv_w, conv_b), rider=rider)


def _glu_conv_fwd(proj, conv_w, conv_b, rider=None):
    t = proj.shape[0]
    pad = _pad_of(K_CONF)
    ca, cg = OFF_CA // CW, OFF_CG // CW

    def body(a_ref, g_ref, w_ref, b_ref, o_ref, v_scr):
        def glu(r, carry):
            rows = pl.ds(pl.multiple_of(r * RC, RC), RC)
            v_scr[rows, :] = a_ref[rows, :] * jax.nn.sigmoid(g_ref[rows, :])
            return carry
        lax.fori_loop(0, t // RC, glu, 0)

        def step(r, carry):
            win = _causal_win(v_scr, r, t, pad)
            o_ref[pl.ds(pl.multiple_of(r * RC, RC), RC), :] = _conv_taps(win, w_ref, K_CONF, pad) + b_ref[...]
            return carry
        lax.fori_loop(0, t // RC, step, 0)

    return _call(
        body, name="glu_conv_fwd", grid=(D // CW,), out_shape=(jax.ShapeDtypeStruct((t, D), f32),),
        in_specs=[pl.BlockSpec((t, CW), lambda j: (0, ca + j)), pl.BlockSpec((t, CW), lambda j: (0, cg + j)),
                  pl.BlockSpec((K_CONF, CW), lambda j: (0, j)), pl.BlockSpec((1, CW), lambda j: (0, j))],
        out_specs=(pl.BlockSpec((t, CW), lambda j: (0, j)),),
        scratch_shapes=[pltpu.VMEM((t, CW), f32)], sem=("arbitrary",), args=(proj, proj, conv_w, conv_b), rider=rider)


def _ffn_conv_fwd(up, conv_w, conv_b, rider=None):
    t = up.shape[0]
    pad = _pad_of(K_FFN)
    nb = D_FF // CW

    def body(g_ref, v_ref, wg_ref, wv_ref, bg_ref, bv_ref, o_ref):
        def step(r, carry):
            gc = _conv_taps(_causal_win(g_ref, r, t, pad), wg_ref, K_FFN, pad) + bg_ref[...]
            vc = _conv_taps(_causal_win(v_ref, r, t, pad), wv_ref, K_FFN, pad) + bv_ref[...]
            o_ref[pl.ds(pl.multiple_of(r * RC, RC), RC), :] = (_silu(gc) * vc).astype(MX)
            return carry
        lax.fori_loop(0, t // RC, step, 0)

    return _call(
        body, name="ffn_conv_fwd", grid=(nb,), out_shape=(jax.ShapeDtypeStruct((t, D_FF), MX),),
        in_specs=[pl.BlockSpec((t, CW), lambda j: (0, j)), pl.BlockSpec((t, CW), lambda j: (0, nb + j)),
                  pl.BlockSpec((K_FFN, CW), lambda j: (0, j)), pl.BlockSpec((K_FFN, CW), lambda j: (0, nb + j)),
                  pl.BlockSpec((1, CW), lambda j: (0, j)), pl.BlockSpec((1, CW), lambda j: (0, nb + j))],
        out_specs=(pl.BlockSpec((t, CW), lambda j: (0, j)),), sem=("arbitrary",),
        args=(up, up, conv_w, conv_w, conv_b, conv_b), rider=rider)


def _ffn_conv_bwd(up, conv_w, conv_b, d_act, rider=None):
    t = up.shape[0]
    pad = _pad_of(K_FFN)
    nb = D_FF // CW

    def body(g_ref, v_ref, wg_ref, wv_ref, bg_ref, bv_ref, da_ref, dup_ref, dw_ref, db_ref,
             dg_scr, dv_scr, dwg_scr, dwv_scr, db_scr):
        dwg_scr[...] = jnp.zeros_like(dwg_scr)
        dwv_scr[...] = jnp.zeros_like(dwv_scr)
        db_scr[...] = jnp.zeros_like(db_scr)

        def first(r, carry):
            rows = pl.ds(pl.multiple_of(r * RC, RC), RC)
            gwin = _causal_win(g_ref, r, t, pad)
            vwin = _causal_win(v_ref, r, t, pad)
            gc = _conv_taps(gwin, wg_ref, K_FFN, pad) + bg_ref[...]
            vc = _conv_taps(vwin, wv_ref, K_FFN, pad) + bv_ref[...]
            da = da_ref[rows, :]
            dgc = da * vc * _dsilu(gc)
            dvc = da * _silu(gc)
            dg_scr[rows, :] = dgc
            dv_scr[rows, :] = dvc
            _dw_accumulate(dwg_scr, dgc, gwin, K_FFN, pad)
            _dw_accumulate(dwv_scr, dvc, vwin, K_FFN, pad)
            db_scr[0:8, :] += _rows8(dgc)
            db_scr[8:16, :] += _rows8(dvc)
            return carry
        lax.fori_loop(0, t // RC, first, 0)

        def second(r, carry):
            rows = pl.ds(pl.multiple_of(r * RC, RC), RC)
            dup_ref[0, rows, :] = _corr_taps(_anti_win(dg_scr, r, t, pad), wg_ref, K_FFN).astype(MX)
            dup_ref[1, rows, :] = _corr_taps(_anti_win(dv_scr, r, t, pad), wv_ref, K_FFN).astype(MX)
            return carry
        lax.fori_loop(0, t // RC, second, 0)

        for j in range(K_FFN):
            dw_ref[0, j:j + 1, :] = jnp.sum(dwg_scr[8 * j:8 * j + 8, :], axis=0, keepdims=True)
            dw_ref[1, j:j + 1, :] = jnp.sum(dwv_scr[8 * j:8 * j + 8, :], axis=0, keepdims=True)
        db_ref[0] = jnp.sum(db_scr[0:8, :], axis=0, keepdims=True)
        db_ref[1] = jnp.sum(db_scr[8:16, :], axis=0, keepdims=True)

    return _call(
        body, name="ffn_conv_bwd", grid=(nb,),
        out_shape=(jax.ShapeDtypeStruct((2, t, D_FF), MX), jax.ShapeDtypeStruct((2, K_FFN, D_FF), f32),
                   jax.ShapeDtypeStruct((2, 1, D_FF), f32)),
        in_specs=[pl.BlockSpec((t, CW), lambda j: (0, j)), pl.BlockSpec((t, CW), lambda j: (0, nb + j)),
                  pl.BlockSpec((K_FFN, CW), lambda j: (0, j)), pl.BlockSpec((K_FFN, CW), lambda j: (0, nb + j)),
                  pl.BlockSpec((1, CW), lambda j: (0, j)), pl.BlockSpec((1, CW), lambda j: (0, nb + j)),
                  pl.BlockSpec((t, CW), lambda j: (0, j))],
        out_specs=(pl.BlockSpec((2, t, CW), lambda j: (0, 0, j)), pl.BlockSpec((2, K_FFN, CW), lambda j: (0, 0, j)),
                   pl.BlockSpec((2, 1, CW), lambda j: (0, 0, j))),
        scratch_shapes=[pltpu.VMEM((t, CW), f32), pltpu.VMEM((t, CW), f32), pltpu.VMEM((8 * K_FFN, CW), f32),
                        pltpu.VMEM((8 * K_FFN, CW), f32), pltpu.VMEM((16, CW), f32)],
        sem=("arbitrary",), args=(up, up, conv_w, conv_w, conv_b, conv_b, d_act), rider=rider)


def _glu_conv_bwd(proj, conv_w, d_uconv, rider=None):
    t = proj.shape[0]
    pad = _pad_of(K_CONF)
    ca, cg = OFF_CA // CW, OFF_CG // CW

    def body(a_ref, g_ref, w_ref, du_ref, dc_ref, dw_ref, db_ref, v_scr, dw_scr, db_scr):
        dw_scr[...] = jnp.zeros_like(dw_scr)
        db_scr[...] = jnp.zeros_like(db_scr)

        def glu(r, carry):
            rows = pl.ds(pl.multiple_of(r * RC, RC), RC)
            v_scr[rows, :] = a_ref[rows, :] * jax.nn.sigmoid(g_ref[rows, :])
            return carry
        lax.fori_loop(0, t // RC, glu, 0)

        def step(r, carry):
            rows = pl.ds(pl.multiple_of(r * RC, RC), RC)
            du = du_ref[rows, :]
            _dw_accumulate(dw_scr, du, _causal_win(v_scr, r, t, pad), K_CONF, pad)
            db_scr[...] += _rows8(du)
            dv = _corr_taps(_anti_win(du_ref, r, t, pad), w_ref, K_CONF)
            a = a_ref[rows, :]
            s = jax.nn.sigmoid(g_ref[rows, :])
            dc_ref[0, rows, :] = (dv * s).astype(MX)
            dc_ref[1, rows, :] = (dv * a * s * (1.0 - s)).astype(MX)
            return carry
        lax.fori_loop(0, t // RC, step, 0)
        _dw_finish(dw_scr, dw_ref, K_CONF)
        db_ref[...] = jnp.sum(db_scr[...], axis=0, keepdims=True)

    return _call(
        body, name="glu_conv_bwd", grid=(D // CW,),
        out_shape=(jax.ShapeDtypeStruct((2, t, D), MX), jax.ShapeDtypeStruct((K_CONF, D), f32),
                   jax.ShapeDtypeStruct((1, D), f32)),
        in_specs=[pl.BlockSpec((t, CW), lambda j: (0, ca + j)), pl.BlockSpec((t, CW), lambda j: (0, cg + j)),
                  pl.BlockSpec((K_CONF, CW), lambda j: (0, j)), pl.BlockSpec((t, CW), lambda j: (0, j))],
        out_specs=(pl.BlockSpec((2, t, CW), lambda j: (0, 0, j)), pl.BlockSpec((K_CONF, CW), lambda j: (0, j)),
                   pl.BlockSpec((1, CW), lambda j: (0, j))),
        scratch_shapes=[pltpu.VMEM((t, CW), f32), pltpu.VMEM((8 * K_CONF, CW), f32), pltpu.VMEM((8, CW), f32)],
        sem=("arbitrary",), args=(proj, proj, conv_w, d_uconv), rider=rider)


def _ssd_conv_bwd_x(proj, conv_w, conv_b, d_xs, d_y, d_skip_row):
    t = proj.shape[0]
    pad = _pad_of(K_SSD)
    c0 = OFF_XBC // CW

    def body(x_ref, w_ref, b_ref, dxs_ref, dy_ref, dsk_ref, draw_ref, dw_ref, db_ref, dp_scr, dw_scr, db_scr):
        dw_scr[...] = jnp.zeros_like(dw_scr)
        db_scr[...] = jnp.zeros_like(db_scr)

        def first(r, carry):
            rows = pl.ds(pl.multiple_of(r * RC, RC), RC)
            win = _causal_win(x_ref, r, t, pad)
            pre = _conv_taps(win, w_ref, K_SSD, pad) + b_ref[...]
            dpre = (dxs_ref[rows, :] + dy_ref[rows, :] * dsk_ref[...]) * _dsilu(pre)
            dp_scr[rows, :] = dpre
            _dw_accumulate(dw_scr, dpre, win, K_SSD, pad)
            db_scr[...] += _rows8(dpre)
            return carry
        lax.fori_loop(0, t // RC, first, 0)

        def second(r, carry):
            rows = pl.ds(pl.multiple_of(r * RC, RC), RC)
            draw_ref[rows, :] = _corr_taps(_anti_win(dp_scr, r, t, pad), w_ref, K_SSD).astype(MX)
            return carry
        lax.fori_loop(0, t // RC, second, 0)
        _dw_finish(dw_scr, dw_ref, K_SSD)
        db_ref[...] = jnp.sum(db_scr[...], axis=0, keepdims=True)

    cb = pl.BlockSpec((t, CW), lambda j: (0, j))
    return pl.pallas_call(
        body, name="ssd_conv_bwd_x", grid=(D // CW,),
        out_shape=(jax.ShapeDtypeStruct((t, D), MX), jax.ShapeDtypeStruct((K_SSD, D), f32),
                   jax.ShapeDtypeStruct((1, D), f32)),
        in_specs=[pl.BlockSpec((t, CW), lambda j: (0, c0 + j)), pl.BlockSpec((K_SSD, CW), lambda j: (0, j)),
                  pl.BlockSpec((1, CW), lambda j: (0, j)), cb, cb, pl.BlockSpec((1, CW), lambda j: (0, j))],
        out_specs=(cb, pl.BlockSpec((K_SSD, CW), lambda j: (0, j)), pl.BlockSpec((1, CW), lambda j: (0, j))),
        scratch_shapes=[pltpu.VMEM((t, CW), f32), pltpu.VMEM((8 * K_SSD, CW), f32), pltpu.VMEM((8, CW), f32)],
        compiler_params=_cp("arbitrary"),
    )(proj, conv_w, conv_b, d_xs, d_y, d_skip_row)


def _ssd_conv_bwd_bc(proj, conv_w, conv_b, d_bc):
    t = proj.shape[0]
    pad = _pad_of(K_SSD)
    c0 = (OFF_XBC + D) // CW
    w0 = D // CW

    def body(x_ref, w_ref, b_ref, dbc_ref, draw_ref, dw_ref, db_ref, dp_scr, dw_scr, db_scr):
        dw_scr[...] = jnp.zeros_like(dw_scr)
        db_scr[...] = jnp.zeros_like(db_scr)

        def first(r, carry):
            rows = pl.ds(pl.multiple_of(r * RC, RC), RC)
            win = _causal_win(x_ref, r, t, pad)
            pre = _conv_taps(win, w_ref, K_SSD, pad) + b_ref[...]
            dpre = dbc_ref[0, rows, :] * _dsilu(pre)
            dp_scr[rows, :] = dpre
            _dw_accumulate(dw_scr, dpre, win, K_SSD, pad)
            db_scr[...] += _rows8(dpre)
            return carry
        lax.fori_loop(0, t // RC, first, 0)

        def second(r, carry):
            rows = pl.ds(pl.multiple_of(r * RC, RC), RC)
            draw_ref[rows, :] = _corr_taps(_anti_win(dp_scr, r, t, pad), w_ref, K_SSD).astype(MX)
            return carry
        lax.fori_loop(0, t // RC, second, 0)
        _dw_finish(dw_scr, dw_ref, K_SSD)
        db_ref[...] = jnp.sum(db_scr[...], axis=0, keepdims=True)

    return pl.pallas_call(
        body, name="ssd_conv_bwd_bc", grid=(2,),
        out_shape=(jax.ShapeDtypeStruct((t, 2 * CW), MX), jax.ShapeDtypeStruct((K_SSD, 2 * CW), f32),
                   jax.ShapeDtypeStruct((1, 2 * CW), f32)),
        in_specs=[pl.BlockSpec((t, CW), lambda j: (0, c0 + j)), pl.BlockSpec((K_SSD, CW), lambda j: (0, w0 + j)),
                  pl.BlockSpec((1, CW), lambda j: (0, w0 + j)), pl.BlockSpec((1, t, CW), lambda j: (j, 0, 0))],
        out_specs=(pl.BlockSpec((t, CW), lambda j: (0, j)), pl.BlockSpec((K_SSD, CW), lambda j: (0, j)),
                   pl.BlockSpec((1, CW), lambda j: (0, j))),
        scratch_shapes=[pltpu.VMEM((t, CW), f32), pltpu.VMEM((8 * K_SSD, CW), f32), pltpu.VMEM((8, CW), f32)],
        compiler_params=_cp("arbitrary"),
    )(proj, conv_w, conv_b, d_bc)


def _chunk_masks():
    ii = lax.broadcasted_iota(jnp.int32, (CHUNK, CHUNK), 0)
    jj = lax.broadcasted_iota(jnp.int32, (CHUNK, CHUNK), 1)
    return ii == jj, jj <= ii, jj >= ii


def _to_row(col, eye):
    return jnp.sum(jnp.where(eye, col, 0.0), axis=0, keepdims=True)


def _to_col(row, eye):
    return jnp.sum(jnp.where(eye, row, 0.0), axis=1, keepdims=True)


def _head_decay(dt_h, a_h, eye, tril):
    a_row = _to_row(dt_h * a_h, eye)
    cs = jnp.sum(jnp.where(tril, a_row, 0.0), axis=1, keepdims=True)
    cs_row = _to_row(cs, eye)
    decay = jnp.where(tril, jnp.exp(jnp.where(tril, cs - cs_row, 0.0)), 0.0)
    total = jnp.sum(a_row, axis=1, keepdims=True)
    return cs, decay, total


SCAN_UNROLL = 4


def _unrolled_loop(n, step, init):
    unroll = min(SCAN_UNROLL, n)
    assert n % unroll == 0

    def trip(i, carry):
        for u in range(unroll):
            carry = step(unroll * i + u, carry)
        return carry
    return lax.fori_loop(0, n // unroll, trip, init)


def _lane_pick(mat, lane, which):
    return jnp.sum(jnp.where(lane == which, mat, 0.0), axis=1, keepdims=True)


def _ssd_fwd(xbc_act, proj, dt_bias_row, a_log_row, rider=None):
    t = xbc_act.shape[0]
    nc = t // CHUNK
    cb, cc, cdt = D // LANES, (D + 2 * STATE_N) // LANES, OFF_DT // LANES

    def body(x_ref, b_ref, c_ref, dt_ref, dtb_ref, alog_ref, y_ref, st_ref):
        j = pl.program_id(0)
        eye, tril, _ = _chunk_masks()
        lane = lax.broadcasted_iota(jnp.int32, (1, LANES), 1)
        first = lane < HEAD_P
        a_row = -jnp.exp(alog_ref[...])
        a_heads = [jnp.sum(jnp.where(lane == 2 * j + h, a_row, 0.0), axis=1, keepdims=True) for h in range(2)]

        def chunk(c, hprev):
            rows = pl.ds(pl.multiple_of(c * CHUNK, CHUNK), CHUNK)
            xv, bm, cm = x_ref[rows, :], b_ref[rows, :], c_ref[rows, :]
            dt = _softplus(dt_ref[rows, :] + dtb_ref[...])
            st_ref[c] = hprev
            g = _mm_nt(cm, bm)
            ch = _mm(cm, hprev)
            dts = [_lane_pick(dt, lane, 2 * j + h) for h in range(2)]
            xdt = xv * jnp.where(first, dts[0], dts[1])
            ys, hs = [], []
            for h in range(2):
                cs, decay, total = _head_decay(dts[h], a_heads[h], eye, tril)
                y = _mm(g * decay, xdt) + jnp.exp(cs) * ch
                s = _mm_tn(bm * jnp.exp(total - cs), xdt)
                ys.append(y)
                hs.append(jnp.exp(total) * hprev + s)
            y_ref[rows, :] = jnp.where(first, ys[0], ys[1])
            return jnp.where(first, hs[0], hs[1])

        _unrolled_loop(nc, chunk, jnp.zeros((STATE_N, LANES), f32))

    blk = lambda f: pl.BlockSpec((t, LANES), f)
    return _call(
        body, name="ssd_fwd", grid=(D // LANES,),
        out_shape=(jax.ShapeDtypeStruct((t, D), f32), jax.ShapeDtypeStruct((nc, STATE_N, D), f32)),
        in_specs=[blk(lambda j: (0, j)), blk(lambda j: (0, cb + j // 4)), blk(lambda j: (0, cc + j // 4)),
                  blk(lambda j: (0, cdt)), _row(LANES), _row(LANES)],
        out_specs=(blk(lambda j: (0, j)), pl.BlockSpec((nc, STATE_N, LANES), lambda j: (0, 0, j))),
        sem=("arbitrary",), args=(xbc_act, xbc_act, xbc_act, proj, dt_bias_row, a_log_row), rider=rider)


def _ssd_bwd(xbc_act, proj, dt_bias_row, a_log_row, states, d_y, rider=None):
    t = xbc_act.shape[0]
    nc = t // CHUNK
    cb, cc, cdt = D // LANES, (D + 2 * STATE_N) // LANES, OFF_DT // LANES

    def body(x_ref, b_ref, c_ref, dt_ref, dtb_ref, alog_ref, st_ref, dy_ref, dx_ref, dbc_ref, ddt_ref, da_ref):
        grp, p = pl.program_id(0), pl.program_id(1)
        j = 4 * grp + p
        eye, tril, triu = _chunk_masks()
        lane = lax.broadcasted_iota(jnp.int32, (1, LANES), 1)
        first = lane < HEAD_P
        last_row = lax.broadcasted_iota(jnp.int32, (CHUNK, 1), 0) == CHUNK - 1
        a_row = -jnp.exp(alog_ref[...])
        a_heads = [jnp.sum(jnp.where(lane == 2 * j + h, a_row, 0.0), axis=1, keepdims=True) for h in range(2)]

        @pl.when(p == 0)
        def _():
            dbc_ref[...] = jnp.zeros_like(dbc_ref)

        @pl.when(j == 0)
        def _():
            ddt_ref[...] = jnp.zeros_like(ddt_ref)
            da_ref[...] = jnp.zeros_like(da_ref)

        def chunk(i, dh):
            c = nc - 1 - i
            rows = pl.ds(pl.multiple_of(c * CHUNK, CHUNK), CHUNK)
            xv, bm, cm = x_ref[rows, :], b_ref[rows, :], c_ref[rows, :]
            dtr = dt_ref[rows, :] + dtb_ref[...]
            dt = _softplus(dtr)
            hprev = st_ref[c]
            dy = dy_ref[rows, :]
            g = _mm_nt(cm, bm)
            dts = [_lane_pick(dt, lane, 2 * j + h) for h in range(2)]
            xdt = xv * jnp.where(first, dts[0], dts[1])
            dxs, dhs = [], []
            db_sum, dc_sum = None, None
            ddt_mat = jnp.zeros((CHUNK, LANES), f32)
            da_acc = jnp.zeros((1, LANES), f32)
            for h in range(2):
                mine = first if h == 0 else jnp.logical_not(first)
                cs, decay, total = _head_decay(dts[h], a_heads[h], eye, tril)
                e_cs, e_tot = jnp.exp(cs), jnp.exp(total)
                dec_s = jnp.exp(total - cs)
                dyh = jnp.where(mine, dy, 0.0)
                xdth = jnp.where(mine, xdt, 0.0)
                dhh = jnp.where(mine, dh, 0.0)
                hph = jnp.where(mine, hprev, 0.0)
                m = g * decay
                dm = _mm_nt(dyh, xdth)
                dg = dm * decay
                w = dm * m
                bdec = bm * dec_s
                dxdt = _mm_tn(m, dyh) + _mm(bdec, dhh)
                dc_off = _mm_nt(dyh, hph) * e_cs
                db_s = _mm_nt(xdth, dhh) * dec_s
                dc_h = _mm(dg, bm) + dc_off
                db_h = _mm_tn(dg, cm) + db_s
                r_s = jnp.sum(db_s * bm, axis=1, keepdims=True)
                dtotal = jnp.sum(r_s, axis=0, keepdims=True) + e_tot * jnp.sum(
                    jnp.sum(dhh * hph, axis=1, keepdims=True), axis=0, keepdims=True)
                dcs = (jnp.sum(w, axis=1, keepdims=True) - _to_col(jnp.sum(w, axis=0, keepdims=True), eye)
                       + jnp.sum(dc_off * cm, axis=1, keepdims=True) - r_s + jnp.where(last_row, dtotal, 0.0))
                da_col = jnp.sum(jnp.where(triu, _to_row(dcs, eye), 0.0), axis=1, keepdims=True)
                ddt = da_col * a_heads[h] + jnp.sum(jnp.where(mine, dxdt * xv, 0.0), axis=1, keepdims=True)
                ddt_mat = ddt_mat + jnp.where(lane == 2 * j + h, ddt, 0.0)
                da_acc = da_acc + jnp.where(lane == 2 * j + h, jnp.sum(da_col * dts[h], axis=0, keepdims=True), 0.0)
                dxs.append(dxdt * dts[h])
                dhs.append(e_tot * dhh + _mm_tn(cm * e_cs, dyh))
                db_sum = db_h if db_sum is None else db_sum + db_h
                dc_sum = dc_h if dc_sum is None else dc_sum + dc_h
            dx_ref[rows, :] = jnp.where(first, dxs[0], dxs[1])
            dbc_ref[0, rows, :] += db_sum
            dbc_ref[1, rows, :] += dc_sum
            ddt_ref[rows, :] += ddt_mat * jax.nn.sigmoid(dtr)
            da_ref[...] += da_acc * a_row
            return jnp.where(first, dhs[0], dhs[1])

        _unrolled_loop(nc, chunk, jnp.zeros((STATE_N, LANES), f32))

    blk = lambda f: pl.BlockSpec((t, LANES), f)
    return _call(
        body, name="ssd_bwd", grid=(2, 4),
        out_shape=(jax.ShapeDtypeStruct((t, D), f32), jax.ShapeDtypeStruct((2, t, 2 * STATE_N), f32),
                   jax.ShapeDtypeStruct((t, LANES), f32), jax.ShapeDtypeStruct((1, LANES), f32)),
        in_specs=[blk(lambda g, p: (0, 4 * g + p)), blk(lambda g, p: (0, cb + g)), blk(lambda g, p: (0, cc + g)),
                  blk(lambda g, p: (0, cdt)), _row(LANES), _row(LANES),
                  pl.BlockSpec((nc, STATE_N, LANES), lambda g, p: (0, 0, 4 * g + p)), blk(lambda g, p: (0, 4 * g + p))],
        out_specs=(blk(lambda g, p: (0, 4 * g + p)), pl.BlockSpec((2, t, LANES), lambda g, p: (0, 0, g)),
                   blk(lambda g, p: (0, 0)), _row(LANES)),
        sem=("arbitrary", "arbitrary"), args=(xbc_act, xbc_act, xbc_act, proj, dt_bias_row, a_log_row, states, d_y),
        rider=rider)


def _up_bwd(d_up, w_up, x1, mod, norm2_w, dx2, mix, w_out, rider=None):
    t = x1.shape[0]

    def body(dup_ref, wu_ref, x1_ref, mod_ref, nw_ref, dx2_ref, mix_ref, wo_ref,
             dx1_ref, dmix_ref, dys_ref, du_ref, st_ref):
        @pl.when(pl.program_id(0) == 0)
        def _():
            st_ref[...] = jnp.zeros_like(st_ref)

        nt = (((1,), (1,)), ((), ()))
        dh = None
        for k in range(4):
            lo = (k % 2) * UP_SHARD
            part = lax.dot_general(dup_ref[k // 2, :, lo:lo + UP_SHARD], wu_ref[k], nt, preferred_element_type=f32)
            dh = part if dh is None else dh + part
        x1 = x1_ref[...]
        rstd = lax.rsqrt(jnp.mean(x1 * x1, axis=-1, keepdims=True) + 1e-6)
        xh = x1 * rstd
        nw = nw_ref[...]
        sc = 1.0 + mod_ref[:, 4 * D:5 * D]
        st_ref[0:1, :] += jnp.sum(dh, axis=0, keepdims=True)
        st_ref[1:2, :] += jnp.sum(dh * xh * nw, axis=0, keepdims=True)
        st_ref[2:3, :] += jnp.sum(dh * sc * xh, axis=0, keepdims=True)
        dxh = dh * sc * nw
        dx1 = dx2_ref[...] + rstd * (dxh - xh * jnp.mean(dxh * xh, axis=-1, keepdims=True))
        dx1_ref[...] = dx1
        st_ref[3:4, :] += jnp.sum(dx1 * mix_ref[...], axis=0, keepdims=True)
        dmix = (mod_ref[:, 2 * D:3 * D] * dx1).astype(MX)
        dmix_ref[...] = dmix
        dys_ref[...] = lax.dot_general(dmix, wo_ref[0:D, :], nt, preferred_element_type=f32)
        du_ref[...] = lax.dot_general(dmix, wo_ref[D:2 * D, :], nt, preferred_element_type=f32)

    blk = pl.BlockSpec((TM, D), lambda i: (i, 0))
    return _call(
        body, name="up_bwd", grid=(t // TM,),
        out_shape=(jax.ShapeDtypeStruct((t, D), f32), jax.ShapeDtypeStruct((t, D), MX),
                   jax.ShapeDtypeStruct((t, D), f32), jax.ShapeDtypeStruct((t, D), f32),
                   jax.ShapeDtypeStruct((8, D), f32)),
        in_specs=[pl.BlockSpec((2, TM, D_FF), lambda i: (0, i, 0)), _resident((4, D, UP_SHARD)), blk, _row(6 * D), _row(),
                  blk, blk, _resident((2 * D, D))],
        out_specs=(blk, blk, blk, blk, pl.BlockSpec((8, D), lambda i: (0, 0))),
        sem=("arbitrary",), args=(d_up, w_up, x1, mod, norm2_w, dx2, mix, w_out), rider=rider)


def _ln_silu_bwd(d_u, u_conv, ln_w, ln_b):
    t = d_u.shape[0]

    def body(du_ref, u_ref, w_ref, b_ref, o_ref, st_ref):
        @pl.when(pl.program_id(0) == 0)
        def _():
            st_ref[...] = jnp.zeros_like(st_ref)

        u = u_ref[...]
        mu = jnp.mean(u, axis=-1, keepdims=True)
        uc = u - mu
        rstd = lax.rsqrt(jnp.mean(uc * uc, axis=-1, keepdims=True) + 1e-5)
        n = uc * rstd
        w = w_ref[...]
        dl = du_ref[...] * _dsilu(n * w + b_ref[...])
        st_ref[0:1, :] += jnp.sum(dl * n, axis=0, keepdims=True)
        st_ref[1:2, :] += jnp.sum(dl, axis=0, keepdims=True)
        dn = dl * w
        o_ref[...] = rstd * (dn - jnp.mean(dn, axis=-1, keepdims=True) - n * jnp.mean(dn * n, axis=-1, keepdims=True))

    blk = pl.BlockSpec((TM, D), lambda i: (i, 0))
    return pl.pallas_call(
        body, name="ln_silu_bwd", grid=(t // TM,),
        out_shape=(jax.ShapeDtypeStruct((t, D), f32), jax.ShapeDtypeStruct((8, D), f32)),
        in_specs=[blk, blk, _row(), _row()], out_specs=(blk, pl.BlockSpec((8, D), lambda i: (0, 0))),
        compiler_params=_cp("arbitrary"),
    )(d_u, u_conv, ln_w, ln_b)


def _ssd_gate_norm_bwd(d_out, y_scan, xbc_act, proj, d_skip_row, ssd_norm_w):
    t = d_out.shape[0]

    def body(do_ref, y_ref, xs_ref, z_ref, dsk_ref, nw_ref, dy_ref, dz_ref, st_ref):
        @pl.when(pl.program_id(0) == 0)
        def _():
            st_ref[...] = jnp.zeros_like(st_ref)

        xs = xs_ref[...]
        y = y_ref[...] + xs * dsk_ref[...]
        z = z_ref[...]
        s = _silu(z)
        yz = y * s
        rstd = lax.rsqrt(jnp.mean(yz * yz, axis=-1, keepdims=True) + 1e-6)
        n = yz * rstd
        do = do_ref[...]
        st_ref[0:1, :] += jnp.sum(do * n, axis=0, keepdims=True)
        dn = do * nw_ref[...]
        dyz = rstd * (dn - n * jnp.mean(dn * n, axis=-1, keepdims=True))
        dy = dyz * s
        dy_ref[...] = dy
        dz_ref[...] = (dyz * y * _dsilu(z)).astype(MX)
        st_ref[1:2, :] += jnp.sum(dy * xs, axis=0, keepdims=True)

    blk = pl.BlockSpec((TM, D), lambda i: (i, 0))
    return pl.pallas_call(
        body, name="ssd_gate_norm_bwd", grid=(t // TM,),
        out_shape=(jax.ShapeDtypeStruct((t, D), f32), jax.ShapeDtypeStruct((t, D), MX), jax.ShapeDtypeStruct((8, D), f32)),
        in_specs=[blk, blk, blk, blk, _row(), _row()], out_specs=(blk, blk, pl.BlockSpec((8, D), lambda i: (0, 0))),
        compiler_params=_cp("arbitrary"),
    )(d_out, y_scan, xbc_act, proj, d_skip_row, ssd_norm_w)


def _inproj_bwd(d_z, d_xraw, d_bcraw, d_conf, d_dt, w_pack, x, mod, norm1_w, dx1, after=None):
    t = x.shape[0]
    extra = [] if after is None else [after]

    def body(dz_ref, dx_ref, dbc_ref, dcf_ref, ddt_ref, w_ref, x_ref, mod_ref, nw_ref, dx1_ref, *rest):
        gx_ref, st_ref = rest[-2:]
        @pl.when(pl.program_id(0) == 0)
        def _():
            st_ref[...] = jnp.zeros_like(st_ref)

        nt = (((1,), (1,)), ((), ()))
        dot = lambda a, lo, hi: lax.dot_general(a, w_ref[:, lo:hi], nt, preferred_element_type=f32)
        dh = dot(dz_ref[...], OFF_Z, OFF_Z + D)
        dh = dh + dot(dx_ref[...], OFF_XBC, OFF_XBC + D)
        dh = dh + dot(dbc_ref[...], OFF_XBC + D, OFF_XBC + D_XBC)
        dh = dh + dot(dcf_ref[0], OFF_CA, OFF_CA + D)
        dh = dh + dot(dcf_ref[1], OFF_CG, OFF_CG + D)
        dh = dh + dot(ddt_ref[...].astype(MX), OFF_DT, OFF_DT + LANES)
        st_ref[3:4, 0:LANES] += jnp.sum(ddt_ref[...], axis=0, keepdims=True)
        xv = x_ref[...]
        rstd = lax.rsqrt(jnp.mean(xv * xv, axis=-1, keepdims=True) + 1e-6)
        xh = xv * rstd
        nw = nw_ref[...]
        sc = 1.0 + mod_ref[:, D:2 * D]
        st_ref[0:1, :] += jnp.sum(dh, axis=0, keepdims=True)
        st_ref[1:2, :] += jnp.sum(dh * xh * nw, axis=0, keepdims=True)
        st_ref[2:3, :] += jnp.sum(dh * sc * xh, axis=0, keepdims=True)
        dxh = dh * sc * nw
        gx_ref[...] = dx1_ref[...] + rstd * (dxh - xh * jnp.mean(dxh * xh, axis=-1, keepdims=True))

    blk = pl.BlockSpec((TM, D), lambda i: (i, 0))
    return _call(
        body, name="inproj_bwd", grid=(t // TM,),
        out_shape=(jax.ShapeDtypeStruct((t, D), f32), jax.ShapeDtypeStruct((8, D), f32)),
        in_specs=[blk, blk, pl.BlockSpec((TM, 2 * CW), lambda i: (i, 0)), pl.BlockSpec((2, TM, D), lambda i: (0, i, 0)),
                  pl.BlockSpec((TM, LANES), lambda i: (i, 0)), _resident((D, W_PACK)), blk, _row(6 * D), _row(), blk]
        + [ANY] * len(extra),
        out_specs=(blk, pl.BlockSpec((8, D), lambda i: (0, 0))),
        sem=("arbitrary",), args=(d_z, d_xraw, d_bcraw, d_conf, d_dt, w_pack, x, mod, norm1_w, dx1, *extra))[0]


def _wgrad(a, d, name, bn=256, transposed=True):
    k, t = a.shape if transposed else a.shape[::-1]
    n = d.shape[1]
    out_dtype = MX
    contract = (((1,), (0,)), ((), ())) if transposed else (((0,), (0,)), ((), ()))

    def body(a_ref, d_ref, o_ref):
        o_ref[...] = lax.dot_general(a_ref[...], d_ref[...].astype(MX), contract, preferred_element_type=f32).astype(out_dtype)

    return pl.pallas_call(
        body, name=name, grid=(n // bn,), out_shape=jax.ShapeDtypeStruct((k, n), out_dtype),
        in_specs=[_resident(a.shape), pl.BlockSpec((t, bn), lambda j: (0, j))],
        out_specs=pl.BlockSpec((k, bn), lambda j: (0, j)), compiler_params=_cp("arbitrary"),
    )(a, d)


def _wgrad_stacked(at, d, name, bn):
    out_dtype = MX
    k, t = at.shape
    s, _, n = d.shape
    nb = n // bn

    def body(a_ref, d_ref, o_ref):
        o_ref[0] = jnp.dot(a_ref[...], d_ref[0], preferred_element_type=f32).astype(out_dtype)

    return pl.pallas_call(
        body, name=name, grid=(s, nb), out_shape=jax.ShapeDtypeStruct((s * nb, k, bn), out_dtype),
        in_specs=[_resident((k, t)), pl.BlockSpec((1, t, bn), lambda i, j: (i, 0, j))],
        out_specs=pl.BlockSpec((1, k, bn), lambda i, j: (i * nb + j, 0, 0)), compiler_params=_cp("arbitrary", "arbitrary"),
    )(at, d)


def _pad_row(v, width=LANES):
    return jnp.pad(v.reshape(1, -1), ((0, 0), (0, width - v.size)))


def _quarters(a):
    return a.reshape(4, 2, a.shape[0] // 8, a.shape[1])


def _local_step(x, mod, target, w_pack, late, small, reducer=None):
    dtb_row, alog_row = _pad_row(small["dt_bias"]), _pad_row(small["a_log"])
    dskip_row = jnp.repeat(small["d_skip"].reshape(-1), HEAD_P).reshape(1, D)

    red = reducer

    def hosted(host, args, swap=None, scatter=None, gather=(), sums=()):
        if red is None:
            return host(*args)[0]
        riders = ([red.scatter(scatter)] if scatter else []) + ([red.swap(*swap)] if swap else [])
        riders += [_SwapSumsRider([red.sums[n] for n in sums])] if sums else []
        riders += [_GatherRider([g[0]], *g[1:]) for g in gather]
        both = _Riders(riders)
        outs, extra = host(*args, rider=both)
        extra = both.split(extra)
        if scatter:
            red.scattered(scatter, extra.pop(0))
        if swap:
            red.swapped(swap[0], extra.pop(0))
        if sums:
            red.others.update(zip(sums, extra.pop(0)))
        return (outs, [e[0] for e in extra]) if gather else outs

    w_out, w_up, w_down = late
    if red is None:
        proj, h_t = hosted(_ln_inproj, (x, mod, small["norm1_w"], w_pack))
        xbc_act, = hosted(_ssd_conv_fwd, (proj, small["ssd_conv_w"], small["ssd_conv_b"]))
        y_scan, states = hosted(_ssd_fwd, (xbc_act, proj, dtb_row, alog_row))
        u_conv, = hosted(_glu_conv_fwd, (proj, small["conf_conv_w"], small["conf_conv_b"]))
    else:
        (proj, h_t), (w_out, w_up) = hosted(_ln_inproj, (x, mod, small["norm1_w"], w_pack),
                                            gather=[(w_out,), (w_up, 0, UP_EARLY_ROWS)])
        xbc_act, = _ssd_conv_fwd(proj, small["ssd_conv_w"], small["ssd_conv_b"])[0]
        (y_scan, states), (w_up,) = hosted(_ssd_fwd, (xbc_act, proj, dtb_row, alog_row), gather=[(w_up, UP_EARLY_ROWS, None)])
        (u_conv,), (w_down,) = hosted(_glu_conv_fwd, (proj, small["conf_conv_w"], small["conf_conv_b"]), gather=[(w_down,)])
        w_out, w_up, w_down = w_out.reshape(2 * D, D), w_up.reshape(4, D, UP_SHARD), w_down.reshape(D_FF, D)
    y_ssd = _ssd_gate_norm(y_scan, xbc_act, proj, dskip_row, small["ssd_norm_w"])
    u = _ln_silu(u_conv, small["conf_ln_w"], small["conf_ln_b"])
    mix, x1, h2_t, up = _outproj_ln2_up(y_ssd, u, w_out, x, mod, small["norm2_w"], w_up)
    act, = _ffn_conv_fwd(up, small["ffn_conv_w"], small["ffn_conv_b"])[0]
    dx2, d_ffn, d_act, st_down = _down_loss(act, w_down, x1, mod, small["final_norm_w"], target)

    g_down = _quarters(_wgrad(act, d_ffn, "wgrad_down", transposed=False))
    d_up, dw_ffn, db_ffn = hosted(_ffn_conv_bwd, (up, small["ffn_conv_w"], small["ffn_conv_b"], d_act), swap=("w_down", g_down))
    g_up = _wgrad_stacked(h2_t, d_up, "wgrad_up", D_FF // 2).reshape(4, 2, D // 2, UP_SHARD)
    dx1, d_mix, d_yssd, d_u, st_up = hosted(_up_bwd, (d_up, w_up, x1, mod, small["norm2_w"], dx2, mix, w_out),
                                            scatter="w_down", swap=("w_up", g_up))
    g_out = _quarters(jnp.concatenate([_wgrad(y_ssd, d_mix, "wgrad_out_y", transposed=False),
                                       _wgrad(u, d_mix, "wgrad_out_u", transposed=False)], axis=0))
    d_uconv, st_ln = _ln_silu_bwd(d_u, u_conv, small["conf_ln_w"], small["conf_ln_b"])
    d_conf, dw_conf, db_conf = hosted(_glu_conv_bwd, (proj, small["conf_conv_w"], d_uconv), scatter="w_up",
                                      swap=("w_out", g_out))
    d_y, d_z, st_gn = _ssd_gate_norm_bwd(d_yssd, y_scan, xbc_act, proj, dskip_row, small["ssd_norm_w"])
    d_xs, d_bc, d_dt, d_alog = hosted(_ssd_bwd, (xbc_act, proj, dtb_row, alog_row, states, d_y), scatter="w_out")
    d_xraw, dw_sx, db_sx = _ssd_conv_bwd_x(proj, small["ssd_conv_w"], small["ssd_conv_b"], d_xs, d_y, dskip_row)
    d_bcraw, dw_sbc, db_sbc = _ssd_conv_bwd_bc(proj, small["ssd_conv_w"], small["ssd_conv_b"], d_bc)
    g_in = _unpack_g_in(dict(
        z=_wgrad(h_t, d_z, "wgrad_in_z"), x=_wgrad(h_t, d_xraw, "wgrad_in_x"), bc=_wgrad(h_t, d_bcraw, "wgrad_in_bc"),
        conf=_wgrad_stacked(h_t, d_conf, "wgrad_in_conf", D), dt=_wgrad(h_t, d_dt, "wgrad_in_dt", bn=LANES)))
    g_in = g_in.reshape(4, 2, D // 2, W_IN_SHARD_PAD)
    args = (d_z, d_xraw, d_bcraw, d_conf, d_dt, w_pack, x, mod, small["norm1_w"], dx1)
    if red is None:
        grad_x, st_in = _inproj_bwd(*args)
    else:
        done = ("w_out", "w_up", "w_down")
        both = _Riders([red.swap("w_in", g_in), _SwapSumsRider([red.sums[n] for n in done])])
        handles, token = _split_start(both, "swap_start_w_in")
        grad_x, st_in = _inproj_bwd(*args, after=token)
        thru, outs = _split_wait(both, "swap_wait_w_in", handles, st_in)
        red.grads["w_in"] = thru[0]
        red.sums.update(zip(done, thru[1:]))
        got, others = both.split(outs)
        red.swapped("w_in", got)
        red.others.update(zip(done, others))

    gsmall = _pack_small_grads(st_in, st_up, st_down, st_ln, st_gn, d_alog, dw_sx, dw_sbc, db_sx, db_sbc, dw_conf, db_conf,
                               dw_ffn, db_ffn)
    gbig = None if reducer is not None else dict(w_in=g_in, w_out=g_out, w_up=g_up, w_down=g_down)
    return st_down[2, 0], grad_x, gbig, gsmall


VECTORS = ("ada_b", "norm1_w", "ssd_conv_b", "dt_bias", "a_log", "d_skip", "ssd_norm_w", "conf_conv_b", "conf_ln_w",
           "conf_ln_b", "norm2_w", "ffn_conv_b", "final_norm_w")
VECTOR_SIZES = (6 * D, D, D_XBC, HEADS, HEADS, HEADS, D, D, D, D, D, 2 * D_FF, D)
CONVS = {"ssd_conv_w": (K_SSD, D_XBC), "conf_conv_w": (K_CONF, D), "ffn_conv_w": (K_FFN, 2 * D_FF)}


def _pack_rows(items):
    n = -(-sum(w for _, w in items) // (8 * LANES)) * LANES
    while True:
        fill, place = [0] * 8, {}
        for key, w in sorted(items, key=lambda kv: -kv[1]):
            rows = [r for r in range(8) if fill[r] + w <= n]
            if not rows:
                break
            place[key] = (rows[0], fill[rows[0]])
            fill[rows[0]] += w
        if len(place) == len(items):
            return n, place
        n += LANES


FRONT_N, FRONT = _pack_rows([("c", D)] + [((nm, j), cols // 4) for nm, (taps, cols) in CONVS.items() for j in range(taps)])
BACK_N, BACK = _pack_rows([(nm, -(-sz // LANES) * LANES) for nm, sz in zip(VECTORS, VECTOR_SIZES)]
                          + [((nm, j), cols) for nm, (taps, cols) in CONVS.items() for j in range(taps)] + [("loss", LANES)])
_VM = pltpu.CompilerParams(vmem_limit_bytes=VMEM_LIMIT)


def _pack_front(c, shards):
    def body(c_ref, *refs):
        o_ref = refs[-1]
        o_ref[...] = jnp.zeros_like(o_ref)
        r, o = FRONT["c"]
        o_ref[r:r + 1, o:o + D] = c_ref[...]
        for ref, (nm, (taps, cols)) in zip(refs, CONVS.items()):
            for j in range(taps):
                r, o = FRONT[(nm, j)]
                o_ref[r:r + 1, o:o + cols // 4] = ref[0, j:j + 1, :]

    return pl.pallas_call(body, name="pack_front", out_shape=jax.ShapeDtypeStruct((8, FRONT_N), f32),
                          compiler_params=_VM)(c, *shards)


def _unpack_front(got):
    def body(g_ref, c_ref, *outs):
        r, o = FRONT["c"]
        for d in range(8):
            c_ref[d:d + 1, :] = g_ref[8 * d + r:8 * d + r + 1, o:o + D]
        for ref, (nm, (taps, cols)) in zip(outs, CONVS.items()):
            cw = cols // 4
            for j in range(taps):
                r, o = FRONT[(nm, j)]
                for k in range(4):
                    ref[j:j + 1, k * cw:(k + 1) * cw] = g_ref[16 * k + r:16 * k + r + 1, o:o + cw]

    return pl.pallas_call(
        body, name="unpack_front", compiler_params=_VM,
        out_shape=(jax.ShapeDtypeStruct((8, D), f32),) + tuple(jax.ShapeDtypeStruct(tc, f32) for tc in CONVS.values()),
    )(got)


def _pack_small_grads(st_in, st_up, st_down, st_ln, st_gn, d_alog, dw_sx, dw_sbc, db_sx, db_sbc, dw_conf, db_conf, dw_ffn,
                      db_ffn):
    def body(in_ref, up_ref, dn_ref, ln_ref, gn_ref, al_ref, wx_ref, wbc_ref, bx_ref, bbc_ref, wc_ref, bc_ref, wf_ref, bf_ref,
             o_ref):
        def put(key, val, shift=0):
            r, o = BACK[key]
            o_ref[r:r + 1, o + shift:o + shift + val.shape[1]] = val

        o_ref[...] = jnp.zeros_like(o_ref)
        for i, piece in enumerate((in_ref[0:1, :], in_ref[1:2, :], up_ref[3:4, :], up_ref[0:1, :], up_ref[1:2, :],
                                   dn_ref[1:2, :])):
            put("ada_b", piece, i * D)
        put("norm1_w", in_ref[2:3, :])
        put("ssd_conv_b", bx_ref[...])
        put("ssd_conv_b", bbc_ref[...], D)
        put("dt_bias", in_ref[3:4, 0:LANES])
        put("a_log", al_ref[...])
        lane = lax.broadcasted_iota(jnp.int32, (1, LANES), 1)
        col = lax.broadcasted_iota(jnp.int32, (1, D), 1)
        per_col = gn_ref[1:2, :]
        d_skip = jnp.zeros((1, LANES), f32)
        for h in range(HEADS):
            in_head = jnp.logical_and(col >= h * HEAD_P, col < (h + 1) * HEAD_P)
            s = jnp.sum(jnp.where(in_head, per_col, 0.0), axis=1, keepdims=True)
            d_skip = d_skip + jnp.where(lane == h, s, 0.0)
        put("d_skip", d_skip)
        put("ssd_norm_w", gn_ref[0:1, :])
        put("conf_conv_b", bc_ref[...])
        put("conf_ln_w", ln_ref[0:1, :])
        put("conf_ln_b", ln_ref[1:2, :])
        put("norm2_w", up_ref[2:3, :])
        put("ffn_conv_b", bf_ref[0])
        put("ffn_conv_b", bf_ref[1], D_FF)
        put("final_norm_w", dn_ref[0:1, :])
        put("loss", dn_ref[2:3, 0:LANES])
        for j in range(K_SSD):
            put(("ssd_conv_w", j), wx_ref[j:j + 1, :])
            put(("ssd_conv_w", j), wbc_ref[j:j + 1, :], D)
        for j in range(K_CONF):
            put(("conf_conv_w", j), wc_ref[j:j + 1, :])
        for j in range(K_FFN):
            put(("ffn_conv_w", j), wf_ref[0, j:j + 1, :])
            put(("ffn_conv_w", j), wf_ref[1, j:j + 1, :], D_FF)

    return pl.pallas_call(body, name="pack_small_grads", out_shape=jax.ShapeDtypeStruct((8, BACK_N), f32), compiler_params=_VM)(
        st_in, st_up, st_down, st_ln, st_gn, d_alog, dw_sx, dw_sbc, db_sx, db_sbc, dw_conf, db_conf, dw_ffn, db_ffn)


def _small_adamw(got, chip, w, m, v):
    names = VECTORS + tuple(CONVS)
    n_par = len(names)

    def body(chip_ref, g_ref, *refs):
        ins, outs = refs[:3 * n_par], refs[3 * n_par:]
        dm_ref, loss_ref, outs = outs[0], outs[1], outs[2:]
        chip_id = chip_ref[0]

        def summed(key, width):
            r, o = BACK[key]
            s = g_ref[r:r + 1, o:o + width]
            for d in range(1, 8):
                s = s + g_ref[8 * d + r:8 * d + r + 1, o:o + width]
            return s

        def mine(full, cw):
            out = full[:, 0:cw]
            for k in range(1, 4):
                out = jnp.where(chip_id == k, full[:, k * cw:(k + 1) * cw], out)
            return out

        r, o = BACK["ada_b"]
        for d in range(8):
            dm_ref[d:d + 1, :] = mine(g_ref[8 * d + r:8 * d + r + 1, o:o + 6 * D], 6 * D // 4)
        loss_ref[...] = summed("loss", LANES)
        for i, (nm, size) in enumerate(zip(VECTORS, VECTOR_SIZES)):
            g = summed(nm, -(-size // LANES) * LANES)[:, 0:size]
            res = _adam_math(ins[3 * i][...], g, ins[3 * i + 1][...], ins[3 * i + 2][...])
            for ref, val in zip(outs[4 * i:4 * i + 4], (g,) + res):
                ref[...] = val
        for i, (nm, (taps, cols)) in enumerate(CONVS.items(), start=len(VECTORS)):
            for j in range(taps):
                g = mine(summed((nm, j), cols), cols // 4)
                res = _adam_math(ins[3 * i][0, j:j + 1, :], g, ins[3 * i + 1][0, j:j + 1, :], ins[3 * i + 2][0, j:j + 1, :])
                for ref, val in zip(outs[4 * i:4 * i + 4], (g,) + res):
                    ref[0, j:j + 1, :] = val

    params = [a[nm] for nm in names for a in (w, m, v)]
    whole = lambda s: pl.BlockSpec(s, lambda i, chip, nd=len(s): (0,) * nd)
    out_shape = [jax.ShapeDtypeStruct((8, 6 * D // 4), f32), jax.ShapeDtypeStruct((1, LANES), f32)]
    out_shape += [jax.ShapeDtypeStruct(w[nm].shape, f32) for nm in names for _ in range(4)]
    outs = pl.pallas_call(
        body, name="small_adamw", out_shape=tuple(out_shape), compiler_params=_VM,
        grid_spec=pltpu.PrefetchScalarGridSpec(
            num_scalar_prefetch=1, grid=(1,), in_specs=[whole(got.shape)] + [whole(p.shape) for p in params],
            out_specs=tuple(whole(s.shape) for s in out_shape)),
    )(_scalar(chip), got, *params)
    return outs[0], outs[1][0, 0], {nm: outs[2 + 4 * i:6 + 4 * i] for i, nm in enumerate(names)}


W_IN_COLS = 4624
W_IN_SHARD = W_IN_COLS // 4
W_IN_SHARD_PAD = 1280
_SEGMENTS = ((0, 1024, OFF_Z), (1024, 2560, OFF_XBC), (2560, 2576, OFF_DT), (2576, 3600, OFF_CA), (3600, 4624, OFF_CG))


def _in_pieces(bounds=()):
    out = []
    for k in range(4):
        s0, s1 = k * W_IN_SHARD, (k + 1) * W_IN_SHARD
        for lo, hi, off in _SEGMENTS:
            a, b = max(lo, s0), min(hi, s1)
            while a < b:
                p = off + a - lo
                e = min([b - a] + [c - p for c in bounds if c > p])
                out.append((k, a - s0, p, e))
                a += e
    return out


def _pack_w_in(shards):
    pieces = _in_pieces()

    def body(s_ref, o_ref):
        o_ref[:, OFF_DT:W_PACK] = jnp.zeros((TM, W_PACK - OFF_DT), MX)
        for k, c, p, n in pieces:
            o_ref[:, p:p + n] = s_ref[k, :, c:c + n]

    return pl.pallas_call(
        body, name="pack_w_in", grid=(D // TM,), out_shape=jax.ShapeDtypeStruct((D, W_PACK), MX),
        in_specs=[pl.BlockSpec((4, TM, W_IN_SHARD_PAD), lambda i: (0, i, 0))],
        out_specs=pl.BlockSpec((TM, W_PACK), lambda i: (i, 0)), compiler_params=_cp("arbitrary"),
    )(shards)


def _unpack_g_in(g):
    srcs = ((OFF_Z, D), (OFF_XBC, D), (OFF_XBC + D, 2 * CW), (OFF_CA, D), (OFF_CG, D), (OFF_DT, LANES))
    pieces = _in_pieces(tuple(o for o, _ in srcs) + tuple(o + n for o, n in srcs))

    def body(z_ref, x_ref, bc_ref, cf_ref, dt_ref, o_ref):
        read = (lambda lo, hi: z_ref[:, lo:hi], lambda lo, hi: x_ref[:, lo:hi], lambda lo, hi: bc_ref[:, lo:hi],
                lambda lo, hi: cf_ref[0, :, lo:hi], lambda lo, hi: cf_ref[1, :, lo:hi], lambda lo, hi: dt_ref[:, lo:hi])
        o_ref[:, :, W_IN_SHARD - 4:W_IN_SHARD_PAD] = jnp.zeros((4, TM, W_IN_SHARD_PAD - W_IN_SHARD + 4), MX)
        for k, c, p, n in pieces:
            i = [q for q, (o, w) in enumerate(srcs) if o <= p < o + w][0]
            o_ref[k, :, c:c + n] = read[i](p - srcs[i][0], p - srcs[i][0] + n)

    blk = lambda w: pl.BlockSpec((TM, w), lambda i: (i, 0))
    return pl.pallas_call(
        body, name="unpack_g_in", grid=(D // TM,), out_shape=jax.ShapeDtypeStruct((4, D, W_IN_SHARD_PAD), MX),
        in_specs=[blk(D), blk(D), blk(2 * CW), pl.BlockSpec((2, TM, D), lambda i: (0, i, 0)), blk(LANES)],
        out_specs=pl.BlockSpec((4, TM, W_IN_SHARD_PAD), lambda i: (0, i, 0)), compiler_params=_cp("arbitrary"),
    )(g["z"], g["x"], g["bc"], g["conf"], g["dt"])


def _scalar(v):
    return jnp.reshape(v, (1,)).astype(jnp.int32)


def _cast_into_slot(w, width, chip):
    r, c = w.shape
    h = r // 2
    tm = _row_tile(h)
    nj = h // tm

    def body(chip_ref, w_ref, o_ref):
        v = w_ref[...].astype(MX)
        o_ref[0, 0] = v if width == c else jnp.concatenate([v, jnp.zeros((tm, width - c), MX)], axis=1)

    return pl.pallas_call(
        body, name=f"cast_into_slot_{r}x{c}", out_shape=jax.ShapeDtypeStruct((4, 2, h, width), MX),
        grid_spec=pltpu.PrefetchScalarGridSpec(
            num_scalar_prefetch=1, grid=(2, nj),
            in_specs=[pl.BlockSpec((tm, c), lambda i, j, chip: (i * nj + j, 0))],
            out_specs=pl.BlockSpec((1, 1, tm, width), lambda i, j, chip: (chip[0], i, j, 0))),
        compiler_params=_cp("arbitrary", "arbitrary"),
    )(_scalar(chip), w)


def _columns_first(w):
    return jnp.transpose(w, (2, 0, 1))


def _cast_into_slot_w_in(w_t, chip):
    h = D // 2
    nj = h // TM
    pad = W_IN_SHARD_PAD - W_IN_SHARD

    def body(chip_ref, w_ref, o_ref):
        cols = jnp.concatenate([w_ref[:, 0, :], jnp.zeros((pad, TM), f32)], axis=0)
        o_ref[0, 0] = cols.T.astype(MX)

    return pl.pallas_call(
        body, name="cast_into_slot_w_in", out_shape=jax.ShapeDtypeStruct((4, 2, h, W_IN_SHARD_PAD), MX),
        grid_spec=pltpu.PrefetchScalarGridSpec(
            num_scalar_prefetch=1, grid=(2, nj),
            in_specs=[pl.BlockSpec((W_IN_SHARD, 1, TM), lambda i, j, chip: (0, 0, i * nj + j))],
            out_specs=pl.BlockSpec((1, 1, TM, W_IN_SHARD_PAD), lambda i, j, chip: (chip[0], i, j, 0))),
        compiler_params=_cp("arbitrary", "arbitrary"),
    )(_scalar(chip), w_t)


def _adamw_w_in(w_t, mine, other, m_t, v_t, core):
    h = D // 2
    nj = h // TM

    def body(core_ref, w_ref, a_ref, b_ref, m_ref, v_ref, g_ref, d_ref, nm_ref, nv_ref):
        g = jnp.where(pl.program_id(0) == core_ref[0], a_ref[...], b_ref[...]).T[0:W_IN_SHARD, :]
        g_ref[:, 0, :] = g
        d_ref[:, 0, :], nm_ref[:, 0, :], nv_ref[:, 0, :] = _adam_math(w_ref[:, 0, :], g, m_ref[:, 0, :], v_ref[:, 0, :])

    blk = pl.BlockSpec((W_IN_SHARD, 1, TM), lambda i, j, core: (0, 0, i * nj + j))
    gblk = pl.BlockSpec((TM, W_IN_SHARD_PAD), lambda i, j, core: (j, 0))
    return pl.pallas_call(
        body, name="adamw_w_in", out_shape=tuple([jax.ShapeDtypeStruct((W_IN_SHARD, 1, D), f32)] * 4),
        grid_spec=pltpu.PrefetchScalarGridSpec(
            num_scalar_prefetch=1, grid=(2, nj), in_specs=[blk, gblk, gblk, blk, blk], out_specs=(blk,) * 4),
        compiler_params=_cp("arbitrary", "arbitrary"),
    )(_scalar(core), w_t, mine, other, m_t, v_t)


ANY = pl.BlockSpec(memory_space=pl.ANY)


def _place():
    x, y, c = lax.axis_index("x"), lax.axis_index("y"), lax.axis_index("c")
    return x, y, c, [(1 - x, y), (x, 1 - y), (1 - x, 1 - y)]


_GATHER_SEMS = [pltpu.SemaphoreType.DMA((7,)), pltpu.SemaphoreType.DMA((7,)), pltpu.SemaphoreType.DMA]


def _gather_rows_steps(x_ref, out_ref, send_sems, recv_sems, local_sem, after_first=None):
    m_per = x_ref.shape[0]
    x, y, c, chips = _place()
    me, sibling = (x, y, c), (x, y, 1 - c)

    def rows(px, py, pc):
        return out_ref.at[pl.ds((4 * px + 2 * py + pc) * m_per, m_per), :]

    def copy(k, blk, to, src=None):
        return pltpu.make_async_remote_copy(
            src_ref=rows(*blk) if src is None else src, dst_ref=rows(*blk), send_sem=send_sems.at[k],
            recv_sem=recv_sems.at[k], device_id=to, device_id_type=MESH)

    mine = pltpu.make_async_copy(x_ref, rows(*me), local_sem)
    mine.start()
    first = [copy(0, me, sibling, src=x_ref)]
    first += [copy(1 + j, me, (*chip, c), src=x_ref) for j, chip in enumerate(chips)]
    for cp in first:
        cp.start()
    if after_first is not None:
        after_first()
    passed = [copy(4 + j, (*chip, c), sibling) for j, chip in enumerate(chips)]
    for j, chip in enumerate(chips):
        copy(1 + j, (*chip, c), me).wait_recv()
        passed[j].start()
    copy(0, sibling, me).wait_recv()
    for j, chip in enumerate(chips):
        copy(4 + j, (*chip, 1 - c), me).wait_recv()
    for cp in first + passed:
        cp.wait_send()
    mine.wait()


def _gather_rows(block):
    m_per, n = block.shape

    def body(x_ref, out_ref, send_sems, recv_sems, local_sem):
        _gather_rows_steps(x_ref, out_ref, send_sems, recv_sems, local_sem)

    vmem = pl.BlockSpec(memory_space=pltpu.VMEM)
    return pl.pallas_call(
        body, name=f"gather_rows_{m_per}x{n}", out_shape=jax.ShapeDtypeStruct((8 * m_per, n), block.dtype),
        in_specs=[vmem], out_specs=vmem, scratch_shapes=list(_GATHER_SEMS), compiler_params=_VM)(block)


def _front(block, ada_w, slot):
    half = slot.shape[2] // 2
    riders = [_GatherRider([slot], 0, half), _GatherRider([slot], half, half)]
    n_mod = ada_w.shape[1]

    def body(x_ref, w_ref, slot_ref, out_ref, mod_ref, slot_out, c_scr, mine_scr, *sems):
        r_scr = [sems[6:8], sems[8:10]]
        _gather_rows_steps(x_ref, out_ref, *sems[0:3], after_first=lambda: riders[0].start([slot_ref], [slot_out], r_scr[0]))
        r, o = FRONT["c"]
        for d in range(8):
            c_scr[d:d + 1, :] = out_ref[8 * d + r:8 * d + r + 1, o:o + D]
        mine_scr[...] = jnp.dot(_silu(c_scr[...]).astype(MX), w_ref[...].astype(MX), preferred_element_type=f32)
        def second_half():
            riders[1].start([slot_ref], [slot_out], r_scr[1])
            riders[0].finish([slot_ref], [slot_out], r_scr[0])

        _gather_rows_steps(mine_scr, mod_ref, *sems[3:6], after_first=second_half)
        riders[1].finish([slot_ref], [slot_out], r_scr[1])

    vmem = pl.BlockSpec(memory_space=pltpu.VMEM)
    return pl.pallas_call(
        body, name="front",
        out_shape=(jax.ShapeDtypeStruct((64, block.shape[1]), f32), jax.ShapeDtypeStruct((64, n_mod), f32),
                   jax.ShapeDtypeStruct(slot.shape, slot.dtype)),
        in_specs=[vmem, vmem, ANY], out_specs=(vmem, vmem, ANY), input_output_aliases={2: 2},
        scratch_shapes=[pltpu.VMEM((8, D), f32), pltpu.VMEM((8, n_mod), f32)] + list(_GATHER_SEMS) * 2
        + riders[0].scratch + riders[1].scratch, compiler_params=_VM,
    )(block, ada_w, slot)


class _GatherRider:
    def __init__(self, slots, row0=0, nrows=None):
        n = len(slots)
        self.n = n
        self.rows = (row0, slots[0].shape[2] - row0 if nrows is None else nrows)
        self.inputs = list(slots)
        self.out_shape = [jax.ShapeDtypeStruct(s.shape, s.dtype) for s in slots]
        self.scratch = [pltpu.SemaphoreType.DMA((n, 6)), pltpu.SemaphoreType.DMA((n, 6))]
        self.aliases = {a: a for a in range(n)}

    def _copy(self, outs, sems, a, j, k, half, to):
        dst = outs[a].at[k, half, pl.ds(*self.rows)]
        return pltpu.make_async_remote_copy(src_ref=dst, dst_ref=dst, send_sem=sems[0].at[a, j], recv_sem=sems[1].at[a, j],
                                            device_id=to, device_id_type=MESH)

    def _first(self, outs, sems):
        x, y, c, chips = _place()
        return [self._copy(outs, sems, a, j, 2 * x + y, c, (*chip, c)) for a in range(self.n) for j, chip in enumerate(chips)]

    def start(self, ins, outs, sems):
        for cp in self._first(outs, sems):
            cp.start()

    def finish(self, ins, outs, sems):
        x, y, c, chips = _place()
        passed = []
        for a in range(self.n):
            for j, (px, py) in enumerate(chips):
                self._copy(outs, sems, a, j, 2 * px + py, c, (x, y, c)).wait_recv()
                fwd = self._copy(outs, sems, a, 3 + j, 2 * px + py, c, (x, y, 1 - c))
                fwd.start()
                passed.append(fwd)
        for a in range(self.n):
            for j, (px, py) in enumerate(chips):
                self._copy(outs, sems, a, 3 + j, 2 * px + py, 1 - c, (x, y, c)).wait_recv()
        for cp in self._first(outs, sems) + passed:
            cp.wait_send()


class _ScatterRider:
    def __init__(self, parts):
        n = len(parts)
        self.n = n
        self.inputs = list(parts)
        self.out_shape = [jax.ShapeDtypeStruct((3,) + p.shape[1:], p.dtype) for p in parts]
        self.scratch = [pltpu.SemaphoreType.DMA((3 * n,)), pltpu.SemaphoreType.DMA((3 * n,))]
        self.aliases = {}

    def _copies(self, ins, outs, sems):
        x, y, c, chips = _place()
        return [pltpu.make_async_remote_copy(
            src_ref=ins[a].at[2 * px + py], dst_ref=outs[a].at[j], send_sem=sems[0].at[3 * a + j],
            recv_sem=sems[1].at[3 * a + j], device_id=(px, py, c), device_id_type=MESH)
            for a in range(self.n) for j, (px, py) in enumerate(chips)]

    def start(self, ins, outs, sems):
        for cp in self._copies(ins, outs, sems):
            cp.start()

    def finish(self, ins, outs, sems):
        for cp in self._copies(ins, outs, sems):
            cp.wait()


HBM = pl.BlockSpec(memory_space=pltpu.HBM)
SEM = pl.BlockSpec(memory_space=pltpu.SEMAPHORE)
EFFECT = pltpu.SideEffectType.DATAFLOW_SIDE_EFFECTING


def _split_start(rider, name, after=None):
    ni, no, ns = len(rider.inputs), len(rider.out_shape), len(rider.scratch)
    extra = [] if after is None else [after]

    def body(*refs):
        ins, lands = refs[:ni], refs[ni:ni + no]
        sems = refs[ni + no + len(extra):ni + no + len(extra) + ns]
        rider.start(ins, lands, sems)
        refs[-1][...] = jnp.zeros_like(refs[-1])

    bufs = list(rider.inputs) + [lax.empty(s.shape, s.dtype) for s in rider.out_shape]
    outs = pl.pallas_call(
        body, name=name,
        out_shape=tuple(rider.scratch) + tuple(pltpu.HBM(b.shape, b.dtype) for b in bufs) + (jax.ShapeDtypeStruct((8, LANES), f32),),
        in_specs=[HBM] * (ni + no) + [ANY] * len(extra),
        out_specs=(SEM,) * ns + (HBM,) * (ni + no) + (pl.BlockSpec(memory_space=pltpu.VMEM),),
        input_output_aliases={i: ns + i for i in range(ni + no)},
        compiler_params=pltpu.CompilerParams(has_side_effects=EFFECT),
    )(*[pltpu.with_memory_space_constraint(b, pltpu.HBM) for b in bufs], *extra)
    return outs[:-1], outs[-1]


def _split_wait(rider, name, handles, after):
    ni, no, ns = len(rider.inputs), len(rider.out_shape), len(rider.scratch)
    sems, bufs = handles[:ns], handles[ns:]

    def body(*refs):
        rider.finish(refs[:ni], refs[ni:ni + no], refs[ni + no:ni + no + ns])

    outs = pl.pallas_call(
        body, name=name, out_shape=tuple(pltpu.HBM(b.shape, b.dtype) for b in bufs),
        in_specs=[HBM] * (ni + no) + [SEM] * ns + [ANY], out_specs=(HBM,) * (ni + no),
        input_output_aliases={i: i for i in range(ni + no)}, compiler_params=pltpu.CompilerParams(has_side_effects=EFFECT),
    )(*bufs, *sems, after)
    return outs[:ni], outs[ni:]


def _ride_alone(rider, name):
    n = len(rider.inputs)

    def body(*refs):
        ins, outs, sems = refs[:n], refs[n:n + len(rider.out_shape)], refs[n + len(rider.out_shape):]
        rider.start(ins, outs, sems)
        rider.finish(ins, outs, sems)

    return pl.pallas_call(
        body, name=name, out_shape=tuple(rider.out_shape), in_specs=[ANY] * n, out_specs=tuple([ANY] * len(rider.out_shape)),
        input_output_aliases=dict(rider.aliases), scratch_shapes=list(rider.scratch),
    )(*rider.inputs)


class _SwapRider:
    def __init__(self, grads):
        n = len(grads)
        self.n = n
        self.inputs = list(grads)
        self.out_shape = [jax.ShapeDtypeStruct((4,) + g.shape[2:], g.dtype) for g in grads]
        self.scratch = [pltpu.SemaphoreType.DMA((4 * n,)), pltpu.SemaphoreType.DMA((4 * n,))]
        self.aliases = {}

    def _copies(self, ins, outs, sems):
        x, y, c, _ = _place()
        return [pltpu.make_async_remote_copy(
            src_ref=ins[a].at[k, 1 - c], dst_ref=outs[a].at[k], send_sem=sems[0].at[4 * a + k], recv_sem=sems[1].at[4 * a + k],
            device_id=(x, y, 1 - c), device_id_type=MESH) for a in range(self.n) for k in range(4)]

    def start(self, ins, outs, sems):
        for cp in self._copies(ins, outs, sems):
            cp.start()

    def finish(self, ins, outs, sems):
        for cp in self._copies(ins, outs, sems):
            cp.wait()


class _Riders:
    def __init__(self, riders):
        self.riders = list(riders)
        self.inputs = [a for r in riders for a in r.inputs]
        self.out_shape = [s for r in riders for s in r.out_shape]
        self.scratch = [s for r in riders for s in r.scratch]
        self.aliases = {}
        i = o = 0
        for r in riders:
            self.aliases.update({i + a: o + b for a, b in r.aliases.items()})
            i, o = i + len(r.inputs), o + len(r.out_shape)

    def _each(self, ins, outs, sems):
        i = o = s = 0
        for r in self.riders:
            yield r, ins[i:i + len(r.inputs)], outs[o:o + len(r.out_shape)], sems[s:s + len(r.scratch)]
            i, o, s = i + len(r.inputs), o + len(r.out_shape), s + len(r.scratch)

    def start(self, ins, outs, sems):
        for r, a, b, c in self._each(ins, outs, sems):
            r.start(a, b, c)

    def finish(self, ins, outs, sems):
        for r, a, b, c in self._each(ins, outs, sems):
            r.finish(a, b, c)

    def split(self, outs):
        res, o = [], 0
        for r in self.riders:
            res.append(outs[o:o + len(r.out_shape)])
            o += len(r.out_shape)
        return res


class _Reducer:
    def __init__(self, chip, core):
        self.chip, self.core, self.grads, self.parts, self.sums, self.others = chip, core, {}, {}, {}, {}

    def swap(self, name, grad):
        self.grads[name] = grad
        return _SwapRider([grad])

    def swapped(self, name, got):
        self.parts[name] = _add_pair(self.grads[name], got[0], self.core, name)

    def scatter(self, name):
        return _ScatterRider([self.parts[name]])

    def scattered(self, name, others):
        self.sums[name] = _add_chips(self.parts[name], others[0], self.chip, name)


class _SwapSumsRider:
    def __init__(self, halves):
        n = len(halves)
        self.n = n
        self.inputs = list(halves)
        self.out_shape = [jax.ShapeDtypeStruct(s.shape, s.dtype) for s in halves]
        self.scratch = [pltpu.SemaphoreType.DMA((n,)), pltpu.SemaphoreType.DMA((n,))]
        self.aliases = {}

    def _copies(self, ins, outs, sems):
        x, y, c, _ = _place()
        return [pltpu.make_async_remote_copy(
            src_ref=ins[a], dst_ref=outs[a], send_sem=sems[0].at[a], recv_sem=sems[1].at[a],
            device_id=(x, y, 1 - c), device_id_type=MESH) for a in range(self.n)]

    def start(self, ins, outs, sems):
        for cp in self._copies(ins, outs, sems):
            cp.start()

    def finish(self, ins, outs, sems):
        for cp in self._copies(ins, outs, sems):
            cp.wait()


def _row_tile(r):
    for tm in (TM, 176, 128, 64, 32, 16, 8):
        if r % tm == 0:
            return tm
    return r


def _add_pair(mine, got, core, name):
    k, _, h, c = mine.shape
    tm = _row_tile(h)

    def body(core_ref, a_ref, b_ref, o_ref):
        o_ref[0] = (a_ref[0, 0].astype(f32) + b_ref[0].astype(f32)).astype(MX)

    blk = pl.BlockSpec((1, tm, c), lambda i, j, core: (i, j, 0))
    return pl.pallas_call(
        body, name="add_pair_" + name, out_shape=jax.ShapeDtypeStruct((k, h, c), MX),
        grid_spec=pltpu.PrefetchScalarGridSpec(
            num_scalar_prefetch=1, grid=(k, h // tm),
            in_specs=[pl.BlockSpec((1, 1, tm, c), lambda i, j, core: (i, core[0], j, 0)), blk], out_specs=blk),
        compiler_params=_cp("arbitrary", "arbitrary"),
    )(_scalar(core), mine, got)


def _add_chips(parts, others, chip, name):
    _, n, c = others.shape
    tm = _row_tile(n)

    def body(chip_ref, a_ref, b_ref, o_ref):
        s = a_ref[0].astype(f32) + b_ref[0].astype(f32)
        o_ref[...] = (s + b_ref[1].astype(f32)) + b_ref[2].astype(f32)

    return pl.pallas_call(
        body, name="add_chips_" + name, out_shape=jax.ShapeDtypeStruct((n, c), f32),
        grid_spec=pltpu.PrefetchScalarGridSpec(
            num_scalar_prefetch=1, grid=(n // tm,),
            in_specs=[pl.BlockSpec((1, tm, c), lambda i, chip: (chip[0], i, 0)),
                      pl.BlockSpec((3, tm, c), lambda i, chip: (0, i, 0))],
            out_specs=pl.BlockSpec((tm, c), lambda i, chip: (i, 0))),
        compiler_params=_cp("arbitrary"),
    )(_scalar(chip), parts, others)


def _adam_math(w, g, m, v):
    m = ADAM_B1 * m + (1.0 - ADAM_B1) * g
    v = ADAM_B2 * v + (1.0 - ADAM_B2) * (g * g)
    m_hat = m / (1.0 - ADAM_B1 ** ADAM_STEP)
    v_hat = v / (1.0 - ADAM_B2 ** ADAM_STEP)
    return -ADAM_LR * (m_hat / (jnp.sqrt(v_hat) + ADAM_EPS) + ADAM_WD * w), m, v


def _adamw_halves(w, mine, other, m, v, core, name, after):
    r, c = w.shape
    h = r // 2
    tm = _row_tile(h)
    nj = h // tm
    cg = mine.shape[1]

    def body(core_ref, w_ref, a_ref, b_ref, m_ref, v_ref, after_ref, g_ref, d_ref, nm_ref, nv_ref):
        g = jnp.where(pl.program_id(0) == core_ref[0], a_ref[:, 0:c], b_ref[:, 0:c])
        g_ref[...] = g
        d_ref[...], nm_ref[...], nv_ref[...] = _adam_math(w_ref[...], g, m_ref[...], v_ref[...])

    blk = pl.BlockSpec((tm, c), lambda i, j, core: (i * nj + j, 0))
    gblk = pl.BlockSpec((tm, cg), lambda i, j, core: (j, 0))
    return _call(body, name=name, grid=(2, nj), out_shape=[jax.ShapeDtypeStruct((r, c), f32)] * 4,
                 in_specs=[blk, gblk, gblk, blk, blk, ANY], out_specs=(blk,) * 4, sem=("arbitrary", "arbitrary"),
                 prefetch=(_scalar(core),), args=(w, mine, other, m, v, after))[0]


def _ada_adamw(c_all_t, d_mod, w, m, v, after):
    r, c = w.shape
    tm = TM

    def body(ct_ref, dm_ref, w_ref, m_ref, v_ref, after_ref, g_ref, d_ref, nm_ref, nv_ref):
        ca = _silu(ct_ref[...])
        g = ca[:, 0:1] * dm_ref[0:1, :]
        for b in range(1, 8):
            g = g + ca[:, b:b + 1] * dm_ref[b:b + 1, :]
        g_ref[...] = g
        d_ref[...], nm_ref[...], nv_ref[...] = _adam_math(w_ref[...], g, m_ref[...], v_ref[...])

    blk = pl.BlockSpec((tm, c), lambda i: (i, 0))
    return _call(body, name="ada_adamw", grid=(r // tm,), out_shape=[jax.ShapeDtypeStruct((r, c), f32)] * 4,
                 in_specs=[pl.BlockSpec((tm, 8), lambda i: (i, 0)), pl.BlockSpec((8, c), lambda i: (0, 0)), blk, blk, blk, ANY],
                 out_specs=(blk,) * 4, sem=("arbitrary",), args=(c_all_t, d_mod, w, m, v, after))[0]


WEIGHTS = ("ada_w", "ada_b", "norm1_w", "w_in", "ssd_conv_w", "ssd_conv_b", "dt_bias", "a_log", "d_skip", "ssd_norm_w",
           "conf_conv_w", "conf_conv_b", "conf_ln_w", "conf_ln_b", "w_out", "norm2_w", "w_up", "ffn_conv_w", "ffn_conv_b",
           "w_down", "final_norm_w")


def kernel(x, c, ada_w, ada_b, norm1_w, w_in, ssd_conv_w, ssd_conv_b, dt_bias, a_log, d_skip, ssd_norm_w, conf_conv_w, conf_conv_b, conf_ln_w, conf_ln_b, w_out, norm2_w, w_up, ffn_conv_w, ffn_conv_b, w_down, final_norm_w, loss_target, m_ada_w, m_ada_b, m_norm1_w, m_w_in, m_ssd_conv_w, m_ssd_conv_b, m_dt_bias, m_a_log, m_d_skip, m_ssd_norm_w, m_conf_conv_w, m_conf_conv_b, m_conf_ln_w, m_conf_ln_b, m_w_out, m_norm2_w, m_w_up, m_ffn_conv_w, m_ffn_conv_b, m_w_down, m_final_norm_w, v_ada_w, v_ada_b, v_norm1_w, v_w_in, v_ssd_conv_w, v_ssd_conv_b, v_dt_bias, v_a_log, v_d_skip, v_ssd_norm_w, v_conf_conv_w, v_conf_conv_b, v_conf_ln_w, v_conf_ln_b, v_w_out, v_norm2_w, v_w_up, v_ffn_conv_w, v_ffn_conv_b, v_w_down, v_final_norm_w):
    given = dict(locals())
    w = {n: given[n] for n in WEIGHTS}
    mom = {n: given["m_" + n] for n in WEIGHTS}
    var = {n: given["v_" + n] for n in WEIGHTS}
    chip = 2 * lax.axis_index("x") + lax.axis_index("y")
    me = 2 * chip + lax.axis_index("c")

    core = lax.axis_index("c")
    got, mod_cols, a_in = _front(_pack_front(c, [w[n] for n in CONVS]), ada_w[0], _cast_into_slot_w_in(_columns_first(w_in), chip))
    c_all, *convs = _unpack_front(got)
    conv_full = dict(zip(CONVS, convs))
    mod_cols = mod_cols.reshape(8, 8, -1)[0::2]
    mod = lax.dynamic_index_in_dim(mod_cols, me, axis=1, keepdims=False).reshape(1, 6 * D) + ada_b
    w_pack = _pack_w_in(a_in.reshape(4, D, W_IN_SHARD_PAD))
    late = (_cast_into_slot(w_out[0], D, chip), _cast_into_slot(w_up[0], UP_SHARD, chip), _cast_into_slot(w_down[0], D, chip))

    flat = lambda a: a.reshape(1, -1) if a.ndim == 1 else a
    small = {n: flat(w[n]) for n in VECTORS if n != "ada_b"}
    small.update(conv_full)
    reducer = _Reducer(chip, core)
    _, grad_x, _, gsmall = _local_step(x[0], mod, loss_target[0], w_pack, late, small, reducer)
    grads, delta, new_m, new_v = {}, {}, {}, {}

    names = VECTORS + tuple(CONVS)
    d_mod_mine, loss, res = _small_adamw(_gather_rows(gsmall), chip, *[{n: flat(d[n]) for n in names} for d in (w, mom, var)])
    for n in names:
        grads[n], delta[n], new_m[n], new_v[n] = [r.reshape(w[n].shape) for r in res[n]]

    scatter = reducer.scatter("w_in")
    handles, token = _split_start(scatter, "scatter_start_w_in", after=d_mod_mine)
    for n in ("w_up", "w_down", "w_out"):
        res = _adamw_halves(w[n][0], reducer.sums[n], reducer.others[n], mom[n][0], var[n][0], core, "adamw_" + n, token)
        grads[n], delta[n], new_m[n], new_v[n] = [r[None] for r in res]
    res = _ada_adamw(c_all.T, d_mod_mine, ada_w[0], m_ada_w[0], v_ada_w[0], token)
    grads["ada_w"], delta["ada_w"], new_m["ada_w"], new_v["ada_w"] = [r[None] for r in res]
    (reducer.parts["w_in"],), others = _split_wait(scatter, "scatter_wait_w_in", handles, res[1])
    reducer.scattered("w_in", others)
    reducer.others["w_in"], = _ride_alone(_SwapSumsRider([reducer.sums["w_in"]]), "swap_sums_w_in")
    res = _adamw_w_in(_columns_first(w_in), reducer.sums["w_in"], reducer.others["w_in"], _columns_first(m_w_in),
                      _columns_first(v_w_in), core)
    grads["w_in"], delta["w_in"], new_m["w_in"], new_v["w_in"] = [jnp.transpose(r, (1, 2, 0)) for r in res]

    return (loss, grad_x[None], *[grads[n] for n in WEIGHTS], *[delta[n] for n in WEIGHTS],
            *[new_m[n] for n in WEIGHTS], *[new_v[n] for n in WEIGHTS])
```

```python
import functools

import jax
import jax.numpy as jnp
from jax import lax
from jax.experimental import pallas as pl
from jax.experimental.pallas import tpu as pltpu

f32 = jnp.float32
MX = jnp.bfloat16

D = 1024
HEADS = 16
HEAD_P = 64
STATE_N = 128
D_XBC = 1536
D_FF = 2816
UP_SHARD = 2 * D_FF // 4
UP_EARLY_ROWS = 128
K_SSD, K_CONF, K_FFN = 4, 31, 3
CHUNK = 128
OFF_Z, OFF_XBC, OFF_CA, OFF_CG, OFF_DT = 0, 1024, 2560, 3584, 4608
W_PACK = 4736
TM = 256
CW = 256
RC = 64
LANES = 128
VMEM_LIMIT = 56 * 1024 * 1024

ADAM_LR, ADAM_B1, ADAM_B2, ADAM_EPS, ADAM_WD, ADAM_STEP = 0.001, 0.9, 0.999, 1e-08, 0.01, 10

MESH = pl.DeviceIdType.MESH


def _cp(*sem):
    return pltpu.CompilerParams(dimension_semantics=sem, vmem_limit_bytes=VMEM_LIMIT)


def _resident(shape):
    nd = len(shape)
    return pl.BlockSpec(shape, lambda *_: (0,) * nd, pipeline_mode=pl.Buffered(1))


def _row(width=D):
    return pl.BlockSpec((1, width), lambda *_: (0, 0))


def _call(body, *, name, grid, in_specs, out_specs, out_shape, args, sem, scratch_shapes=(), prefetch=(), rider=None):
    ni, no, ns, npf = len(in_specs), len(out_specs), len(scratch_shapes), len(prefetch)
    ri, ro = (len(rider.inputs), len(rider.out_shape)) if rider is not None else (0, 0)

    def full(*refs):
        pre, refs = refs[:npf], refs[npf:]
        base_in, r_in = refs[:ni], refs[ni:ni + ri]
        base_out, r_out = refs[ni + ri:ni + ri + no], refs[ni + ri + no:ni + ri + no + ro]
        base_scr, r_scr = refs[ni + ri + no + ro:ni + ri + no + ro + ns], refs[ni + ri + no + ro + ns:]
        if rider is None:
            return body(*pre, *base_in, *base_out, *base_scr)
        ids = [pl.program_id(a) for a in range(len(grid))]
        first = functools.reduce(jnp.logical_and, [i == 0 for i in ids])
        last = functools.reduce(jnp.logical_and, [i == g - 1 for i, g in zip(ids, grid)])

        @pl.when(first)
        def _():
            rider.start(r_in, r_out, r_scr)

        body(*pre, *base_in, *base_out, *base_scr)

        @pl.when(last)
        def _():
            rider.finish(r_in, r_out, r_scr)

    extra = dict(shapes=[], scratch=[], aliases={}, inputs=[]) if rider is None else dict(
        shapes=rider.out_shape, scratch=rider.scratch, inputs=rider.inputs,
        aliases={npf + ni + i: no + j for i, j in rider.aliases.items()})
    outs = pl.pallas_call(
        full, name=name, out_shape=tuple(out_shape) + tuple(extra["shapes"]), input_output_aliases=extra["aliases"],
        grid_spec=pltpu.PrefetchScalarGridSpec(
            num_scalar_prefetch=npf, grid=grid, in_specs=list(in_specs) + [ANY] * ri,
            out_specs=tuple(out_specs) + (ANY,) * ro, scratch_shapes=list(scratch_shapes) + list(extra["scratch"])),
        compiler_params=_cp(*sem),
    )(*prefetch, *args, *extra["inputs"])
    return tuple(outs[:no]), tuple(outs[no:])


def _silu(v):
    return v * jax.nn.sigmoid(v)


def _dsilu(v):
    s = jax.nn.sigmoid(v)
    return s * (1.0 + v * (1.0 - s))


def _softplus(v):
    return jnp.maximum(v, 0.0) + jnp.log1p(jnp.exp(-jnp.abs(v)))


def _mm(a, b):
    return jnp.dot(a.astype(MX), b.astype(MX), preferred_element_type=f32)


def _mm_nt(a, b):
    return lax.dot_general(a.astype(MX), b.astype(MX), (((1,), (1,)), ((), ())), preferred_element_type=f32)


def _mm_tn(a, b):
    return lax.dot_general(a.astype(MX), b.astype(MX), (((0,), (0,)), ((), ())), preferred_element_type=f32)


def _ln_inproj(x, mod, norm1_w, w_pack, rider=None):
    t = x.shape[0]

    def body(x_ref, mod_ref, nw_ref, w_ref, proj_ref, ht_ref):
        xv = x_ref[...]
        rstd = lax.rsqrt(jnp.mean(xv * xv, axis=-1, keepdims=True) + 1e-6)
        h = (xv * rstd * nw_ref[...]) * (1.0 + mod_ref[:, D:2 * D]) + mod_ref[:, 0:D]
        hb = h.astype(MX)
        ht_ref[...] = hb.T
        proj_ref[...] = jnp.dot(hb, w_ref[...], preferred_element_type=f32)

    return _call(
        body, name="ln_inproj", grid=(t // TM,),
        out_shape=(jax.ShapeDtypeStruct((t, W_PACK), f32), jax.ShapeDtypeStruct((D, t), MX)),
        in_specs=[pl.BlockSpec((TM, D), lambda i: (i, 0)), _row(6 * D), _row(), _resident((D, W_PACK))],
        out_specs=(pl.BlockSpec((TM, W_PACK), lambda i: (i, 0)), pl.BlockSpec((D, TM), lambda i: (0, i))),
        sem=("arbitrary",), args=(x, mod, norm1_w, w_pack), rider=rider)


def _ssd_gate_norm(y_scan, xbc_act, proj, d_skip_row, ssd_norm_w):
    t = y_scan.shape[0]

    def body(y_ref, xs_ref, z_ref, dsk_ref, nw_ref, o_ref):
        y = y_ref[...] + xs_ref[...] * dsk_ref[...]
        yz = y * _silu(z_ref[...])
        rstd = lax.rsqrt(jnp.mean(yz * yz, axis=-1, keepdims=True) + 1e-6)
        o_ref[...] = (yz * rstd * nw_ref[...]).astype(MX)

    blk = pl.BlockSpec((TM, D), lambda i: (i, 0))
    return pl.pallas_call(
        body, name="ssd_gate_norm", grid=(t // TM,), out_shape=jax.ShapeDtypeStruct((t, D), MX),
        in_specs=[blk, blk, blk, _row(), _row()], out_specs=blk, compiler_params=_cp("arbitrary"),
    )(y_scan, xbc_act, proj, d_skip_row, ssd_norm_w)


def _ln_silu(u_conv, ln_w, ln_b):
    t = u_conv.shape[0]

    def body(u_ref, w_ref, b_ref, o_ref):
        u = u_ref[...]
        mu = jnp.mean(u, axis=-1, keepdims=True)
        uc = u - mu
        rstd = lax.rsqrt(jnp.mean(uc * uc, axis=-1, keepdims=True) + 1e-5)
        o_ref[...] = _silu(uc * rstd * w_ref[...] + b_ref[...]).astype(MX)

    blk = pl.BlockSpec((TM, D), lambda i: (i, 0))
    return pl.pallas_call(
        body, name="ln_silu", grid=(t // TM,), out_shape=jax.ShapeDtypeStruct((t, D), MX),
        in_specs=[blk, _row(), _row()], out_specs=blk, compiler_params=_cp("arbitrary"),
    )(u_conv, ln_w, ln_b)


def _outproj_ln2_up(y_ssd, u, w_out, x, mod, norm2_w, w_up):
    t = x.shape[0]

    def body(y_ref, u_ref, wo_ref, x_ref, mod_ref, nw_ref, wu_ref, mix_ref, x1_ref, h2t_ref, up_ref):
        mix = jnp.dot(y_ref[...], wo_ref[0:D, :], preferred_element_type=f32)
        mix = mix + jnp.dot(u_ref[...], wo_ref[D:2 * D, :], preferred_element_type=f32)
        mix_ref[...] = mix
        x1 = x_ref[...] + mod_ref[:, 2 * D:3 * D] * mix
        x1_ref[...] = x1
        rstd = lax.rsqrt(jnp.mean(x1 * x1, axis=-1, keepdims=True) + 1e-6)
        h2 = ((x1 * rstd * nw_ref[...]) * (1.0 + mod_ref[:, 4 * D:5 * D]) + mod_ref[:, 3 * D:4 * D]).astype(MX)
        h2t_ref[...] = h2.T
        for k in range(4):
            up_ref[:, k * UP_SHARD:(k + 1) * UP_SHARD] = jnp.dot(h2, wu_ref[k], preferred_element_type=f32)

    blk = pl.BlockSpec((TM, D), lambda i: (i, 0))
    return pl.pallas_call(
        body, name="outproj_ln2_up", grid=(t // TM,),
        out_shape=(jax.ShapeDtypeStruct((t, D), f32), jax.ShapeDtypeStruct((t, D), f32),
                   jax.ShapeDtypeStruct((D, t), MX), jax.ShapeDtypeStruct((t, 2 * D_FF), f32)),
        in_specs=[blk, blk, _resident((2 * D, D)), blk, _row(6 * D), _row(), _resident((4, D, UP_SHARD))],
        out_specs=(blk, blk, pl.BlockSpec((D, TM), lambda i: (0, i)), pl.BlockSpec((TM, 2 * D_FF), lambda i: (i, 0))),
        compiler_params=_cp("arbitrary"),
    )(y_ssd, u, w_out, x, mod, norm2_w, w_up)


def _down_loss(act, w_down, x1, mod, final_norm_w, target):
    t = x1.shape[0]

    def body(a_ref, wd_ref, x1_ref, mod_ref, wf_ref, tgt_ref, dx2_ref, dffn_ref, dact_ref, st_ref):
        @pl.when(pl.program_id(0) == 0)
        def _():
            st_ref[...] = jnp.zeros_like(st_ref)

        g2 = mod_ref[:, 5 * D:6 * D]
        ffn = jnp.dot(a_ref[...], wd_ref[...], preferred_element_type=f32)
        x2 = x1_ref[...] + g2 * ffn
        rstd = lax.rsqrt(jnp.mean(x2 * x2, axis=-1, keepdims=True) + 1e-6)
        xh = x2 * rstd
        wf = wf_ref[...]
        err = xh * wf - tgt_ref[...]
        dy = err * (1.0 / D)
        dxh = dy * wf
        dx2 = rstd * (dxh - xh * jnp.mean(dxh * xh, axis=-1, keepdims=True))
        dx2_ref[...] = dx2
        dffn = (g2 * dx2).astype(MX)
        dffn_ref[...] = dffn
        dact_ref[...] = lax.dot_general(dffn, wd_ref[...], (((1,), (1,)), ((), ())), preferred_element_type=f32)
        st_ref[0:1, :] += jnp.sum(dy * xh, axis=0, keepdims=True)
        st_ref[1:2, :] += jnp.sum(dx2 * ffn, axis=0, keepdims=True)
        st_ref[2:3, :] += jnp.sum(0.5 * jnp.mean(err * err, axis=-1, keepdims=True), axis=0, keepdims=True)

    blk = pl.BlockSpec((TM, D), lambda i: (i, 0))
    ablk = pl.BlockSpec((TM, D_FF), lambda i: (i, 0))
    return pl.pallas_call(
        body, name="down_loss", grid=(t // TM,),
        out_shape=(jax.ShapeDtypeStruct((t, D), f32), jax.ShapeDtypeStruct((t, D), MX),
                   jax.ShapeDtypeStruct((t, D_FF), f32), jax.ShapeDtypeStruct((8, D), f32)),
        in_specs=[ablk, _resident((D_FF, D)), blk, _row(6 * D), _row(), blk],
        out_specs=(blk, blk, ablk, pl.BlockSpec((8, D), lambda i: (0, 0))),
        compiler_params=_cp("arbitrary"),
    )(act, w_down, x1, mod, final_norm_w, target)


def _pad_of(k):
    return 8 * ((k - 1 + 7) // 8)


def _causal_win(ref, r, t, pad):
    base = pl.multiple_of(r * RC, RC)
    prev = ref[pl.ds(pl.multiple_of(jnp.maximum(base - pad, 0), 8), pad), :]
    prev = jnp.where(r > 0, prev, 0.0)
    return jnp.concatenate([prev, ref[pl.ds(base, RC), :]], axis=0)


def _anti_win(ref, r, t, pad):
    base = pl.multiple_of(r * RC, RC)
    nxt = ref[pl.ds(pl.multiple_of(jnp.minimum(base + RC, t - pad), 8), pad), :]
    nxt = jnp.where(r < t // RC - 1, nxt, 0.0)
    return jnp.concatenate([ref[pl.ds(base, RC), :], nxt], axis=0)


def _shifted(win, offsets):
    for r in range(8):
        mine = [o for o in offsets if o % 8 == r]
        if mine:
            rolled = win if r == 0 else pltpu.roll(win, win.shape[0] - r, 0)
            for o in mine:
                yield o, rolled[o - r:o - r + RC, :]


def _conv_taps(win, w_ref, k, pad):
    first = pad - (k - 1)
    acc = None
    for o, rows in _shifted(win, range(first, first + k)):
        term = w_ref[o - first:o - first + 1, :] * rows
        acc = term if acc is None else acc + term
    return acc


def _corr_taps(win, w_ref, k):
    acc = None
    for o, rows in _shifted(win, range(k)):
        term = w_ref[k - 1 - o:k - o, :] * rows
        acc = term if acc is None else acc + term
    return acc


def _dw_accumulate(dw_scr, d, win, k, pad):
    first = pad - (k - 1)
    for o, rows in _shifted(win, range(first, first + k)):
        j = o - first
        prod = d * rows
        dw_scr[8 * j:8 * j + 8, :] += prod.reshape(RC // 8, 8, prod.shape[-1]).sum(axis=0)


def _dw_finish(dw_scr, dw_ref, k):
    for j in range(k):
        dw_ref[j:j + 1, :] = jnp.sum(dw_scr[8 * j:8 * j + 8, :], axis=0, keepdims=True)


def _rows8(v):
    return v.reshape(RC // 8, 8, v.shape[-1]).sum(axis=0)


def _ssd_conv_fwd(proj, conv_w, conv_b, rider=None):
    t = proj.shape[0]
    pad = _pad_of(K_SSD)
    c0 = OFF_XBC // CW

    def body(x_ref, w_ref, b_ref, o_ref):
        def step(r, carry):
            win = _causal_win(x_ref, r, t, pad)
            o_ref[pl.ds(pl.multiple_of(r * RC, RC), RC), :] = _silu(_conv_taps(win, w_ref, K_SSD, pad) + b_ref[...])
            return carry
        lax.fori_loop(0, t // RC, step, 0)

    return _call(
        body, name="ssd_conv_fwd", grid=(D_XBC // CW,), out_shape=(jax.ShapeDtypeStruct((t, D_XBC), f32),),
        in_specs=[pl.BlockSpec((t, CW), lambda j: (0, c0 + j)), pl.BlockSpec((K_SSD, CW), lambda j: (0, j)),
                  pl.BlockSpec((1, CW), lambda j: (0, j))],
        out_specs=(pl.BlockSpec((t, CW), lambda j: (0, j)),), sem=("arbitrary",), args=(proj, conv_w, conv_b), rider=rider)


def _glu_conv_fwd(proj, conv_w, conv_b, rider=None):
    t = proj.shape[0]
    pad = _pad_of(K_CONF)
    ca, cg = OFF_CA // CW, OFF_CG // CW

    def body(a_ref, g_ref, w_ref, b_ref, o_ref, v_scr):
        def glu(r, carry):
            rows = pl.ds(pl.multiple_of(r * RC, RC), RC)
            v_scr[rows, :] = a_ref[rows, :] * jax.nn.sigmoid(g_ref[rows, :])
            return carry
        lax.fori_loop(0, t // RC, glu, 0)

        def step(r, carry):
            win = _causal_win(v_scr, r, t, pad)
            o_ref[pl.ds(pl.multiple_of(r * RC, RC), RC), :] = _conv_taps(win, w_ref, K_CONF, pad) + b_ref[...]
            return carry
        lax.fori_loop(0, t // RC, step, 0)

    return _call(
        body, name="glu_conv_fwd", grid=(D // CW,), out_shape=(jax.ShapeDtypeStruct((t, D), f32),),
        in_specs=[pl.BlockSpec((t, CW), lambda j: (0, ca + j)), pl.BlockSpec((t, CW), lambda j: (0, cg + j)),
                  pl.BlockSpec((K_CONF, CW), lambda j: (0, j)), pl.BlockSpec((1, CW), lambda j: (0, j))],
        out_specs=(pl.BlockSpec((t, CW), lambda j: (0, j)),),
        scratch_shapes=[pltpu.VMEM((t, CW), f32)], sem=("arbitrary",), args=(proj, proj, conv_w, conv_b), rider=rider)


def _ffn_conv_fwd(up, conv_w, conv_b, rider=None):
    t = up.shape[0]
    pad = _pad_of(K_FFN)
    nb = D_FF // CW

    def body(g_ref, v_ref, wg_ref, wv_ref, bg_ref, bv_ref, o_ref):
        def step(r, carry):
            gc = _conv_taps(_causal_win(g_ref, r, t, pad), wg_ref, K_FFN, pad) + bg_ref[...]
            vc = _conv_taps(_causal_win(v_ref, r, t, pad), wv_ref, K_FFN, pad) + bv_ref[...]
            o_ref[pl.ds(pl.multiple_of(r * RC, RC), RC), :] = (_silu(gc) * vc).astype(MX)
            return carry
        lax.fori_loop(0, t // RC, step, 0)

    return _call(
        body, name="ffn_conv_fwd", grid=(nb,), out_shape=(jax.ShapeDtypeStruct((t, D_FF), MX),),
        in_specs=[pl.BlockSpec((t, CW), lambda j: (0, j)), pl.BlockSpec((t, CW), lambda j: (0, nb + j)),
                  pl.BlockSpec((K_FFN, CW), lambda j: (0, j)), pl.BlockSpec((K_FFN, CW), lambda j: (0, nb + j)),
                  pl.BlockSpec((1, CW), lambda j: (0, j)), pl.BlockSpec((1, CW), lambda j: (0, nb + j))],
        out_specs=(pl.BlockSpec((t, CW), lambda j: (0, j)),), sem=("arbitrary",),
        args=(up, up, conv_w, conv_w, conv_b, conv_b), rider=rider)


def _ffn_conv_bwd(up, conv_w, conv_b, d_act, rider=None):
    t = up.shape[0]
    pad = _pad_of(K_FFN)
    nb = D_FF // CW

    def body(g_ref, v_ref, wg_ref, wv_ref, bg_ref, bv_ref, da_ref, dup_ref, dw_ref, db_ref,
             dg_scr, dv_scr, dwg_scr, dwv_scr, db_scr):
        dwg_scr[...] = jnp.zeros_like(dwg_scr)
        dwv_scr[...] = jnp.zeros_like(dwv_scr)
        db_scr[...] = jnp.zeros_like(db_scr)

        def first(r, carry):
            rows = pl.ds(pl.multiple_of(r * RC, RC), RC)
            gwin = _causal_win(g_ref, r, t, pad)
            vwin = _causal_win(v_ref, r, t, pad)
            gc = _conv_taps(gwin, wg_ref, K_FFN, pad) + bg_ref[...]
            vc = _conv_taps(vwin, wv_ref, K_FFN, pad) + bv_ref[...]
            da = da_ref[rows, :]
            dgc = da * vc * _dsilu(gc)
            dvc = da * _silu(gc)
            dg_scr[rows, :] = dgc
            dv_scr[rows, :] = dvc
            _dw_accumulate(dwg_scr, dgc, gwin, K_FFN, pad)
            _dw_accumulate(dwv_scr, dvc, vwin, K_FFN, pad)
            db_scr[0:8, :] += _rows8(dgc)
            db_scr[8:16, :] += _rows8(dvc)
            return carry
        lax.fori_loop(0, t // RC, first, 0)

        def second(r, carry):
            rows = pl.ds(pl.multiple_of(r * RC, RC), RC)
            dup_ref[0, rows, :] = _corr_taps(_anti_win(dg_scr, r, t, pad), wg_ref, K_FFN).astype(MX)
            dup_ref[1, rows, :] = _corr_taps(_anti_win(dv_scr, r, t, pad), wv_ref, K_FFN).astype(MX)
            return carry
        lax.fori_loop(0, t // RC, second, 0)

        for j in range(K_FFN):
            dw_ref[0, j:j + 1, :] = jnp.sum(dwg_scr[8 * j:8 * j + 8, :], axis=0, keepdims=True)
            dw_ref[1, j:j + 1, :] = jnp.sum(dwv_scr[8 * j:8 * j + 8, :], axis=0, keepdims=True)
        db_ref[0] = jnp.sum(db_scr[0:8, :], axis=0, keepdims=True)
        db_ref[1] = jnp.sum(db_scr[8:16, :], axis=0, keepdims=True)

    return _call(
        body, name="ffn_conv_bwd", grid=(nb,),
        out_shape=(jax.ShapeDtypeStruct((2, t, D_FF), MX), jax.ShapeDtypeStruct((2, K_FFN, D_FF), f32),
                   jax.ShapeDtypeStruct((2, 1, D_FF), f32)),
        in_specs=[pl.BlockSpec((t, CW), lambda j: (0, j)), pl.BlockSpec((t, CW), lambda j: (0, nb + j)),
                  pl.BlockSpec((K_FFN, CW), lambda j: (0, j)), pl.BlockSpec((K_FFN, CW), lambda j: (0, nb + j)),
                  pl.BlockSpec((1, CW), lambda j: (0, j)), pl.BlockSpec((1, CW), lambda j: (0, nb + j)),
                  pl.BlockSpec((t, CW), lambda j: (0, j))],
        out_specs=(pl.BlockSpec((2, t, CW), lambda j: (0, 0, j)), pl.BlockSpec((2, K_FFN, CW), lambda j: (0, 0, j)),
                   pl.BlockSpec((2, 1, CW), lambda j: (0, 0, j))),
        scratch_shapes=[pltpu.VMEM((t, CW), f32), pltpu.VMEM((t, CW), f32), pltpu.VMEM((8 * K_FFN, CW), f32),
                        pltpu.VMEM((8 * K_FFN, CW), f32), pltpu.VMEM((16, CW), f32)],
        sem=("arbitrary",), args=(up, up, conv_w, conv_w, conv_b, conv_b, d_act), rider=rider)


def _glu_conv_bwd(proj, conv_w, d_uconv, rider=None):
    t = proj.shape[0]
    pad = _pad_of(K_CONF)
    ca, cg = OFF_CA // CW, OFF_CG // CW

    def body(a_ref, g_ref, w_ref, du_ref, dc_ref, dw_ref, db_ref, v_scr, dw_scr, db_scr):
        dw_scr[...] = jnp.zeros_like(dw_scr)
        db_scr[...] = jnp.zeros_like(db_scr)

        def glu(r, carry):
            rows = pl.ds(pl.multiple_of(r * RC, RC), RC)
            v_scr[rows, :] = a_ref[rows, :] * jax.nn.sigmoid(g_ref[rows, :])
            return carry
        lax.fori_loop(0, t // RC, glu, 0)

        def step(r, carry):
            rows = pl.ds(pl.multiple_of(r * RC, RC), RC)
            du = du_ref[rows, :]
            _dw_accumulate(dw_scr, du, _causal_win(v_scr, r, t, pad), K_CONF, pad)
            db_scr[...] += _rows8(du)
            dv = _corr_taps(_anti_win(du_ref, r, t, pad), w_ref, K_CONF)
            a = a_ref[rows, :]
            s = jax.nn.sigmoid(g_ref[rows, :])
            dc_ref[0, rows, :] = (dv * s).astype(MX)
            dc_ref[1, rows, :] = (dv * a * s * (1.0 - s)).astype(MX)
            return carry
        lax.fori_loop(0, t // RC, step, 0)
        _dw_finish(dw_scr, dw_ref, K_CONF)
        db_ref[...] = jnp.sum(db_scr[...], axis=0, keepdims=True)

    return _call(
        body, name="glu_conv_bwd", grid=(D // CW,),
        out_shape=(jax.ShapeDtypeStruct((2, t, D), MX), jax.ShapeDtypeStruct((K_CONF, D), f32),
                   jax.ShapeDtypeStruct((1, D), f32)),
        in_specs=[pl.BlockSpec((t, CW), lambda j: (0, ca + j)), pl.BlockSpec((t, CW), lambda j: (0, cg + j)),
                  pl.BlockSpec((K_CONF, CW), lambda j: (0, j)), pl.BlockSpec((t, CW), lambda j: (0, j))],
        out_specs=(pl.BlockSpec((2, t, CW), lambda j: (0, 0, j)), pl.BlockSpec((K_CONF, CW), lambda j: (0, j)),
                   pl.BlockSpec((1, CW), lambda j: (0, j))),
        scratch_shapes=[pltpu.VMEM((t, CW), f32), pltpu.VMEM((8 * K_CONF, CW), f32), pltpu.VMEM((8, CW), f32)],
        sem=("arbitrary",), args=(proj, proj, conv_w, d_uconv), rider=rider)


def _ssd_conv_bwd_x(proj, conv_w, conv_b, d_xs, d_y, d_skip_row):
    t = proj.shape[0]
    pad = _pad_of(K_SSD)
    c0 = OFF_XBC // CW

    def body(x_ref, w_ref, b_ref, dxs_ref, dy_ref, dsk_ref, draw_ref, dw_ref, db_ref, dp_scr, dw_scr, db_scr):
        dw_scr[...] = jnp.zeros_like(dw_scr)
        db_scr[...] = jnp.zeros_like(db_scr)

        def first(r, carry):
            rows = pl.ds(pl.multiple_of(r * RC, RC), RC)
            win = _causal_win(x_ref, r, t, pad)
            pre = _conv_taps(win, w_ref, K_SSD, pad) + b_ref[...]
            dpre = (dxs_ref[rows, :] + dy_ref[rows, :] * dsk_ref[...]) * _dsilu(pre)
            dp_scr[rows, :] = dpre
            _dw_accumulate(dw_scr, dpre, win, K_SSD, pad)
            db_scr[...] += _rows8(dpre)
            return carry
        lax.fori_loop(0, t // RC, first, 0)

        def second(r, carry):
            rows = pl.ds(pl.multiple_of(r * RC, RC), RC)
            draw_ref[rows, :] = _corr_taps(_anti_win(dp_scr, r, t, pad), w_ref, K_SSD).astype(MX)
            return carry
        lax.fori_loop(0, t // RC, second, 0)
        _dw_finish(dw_scr, dw_ref, K_SSD)
        db_ref[...] = jnp.sum(db_scr[...], axis=0, keepdims=True)

    cb = pl.BlockSpec((t, CW), lambda j: (0, j))
    return pl.pallas_call(
        body, name="ssd_conv_bwd_x", grid=(D // CW,),
        out_shape=(jax.ShapeDtypeStruct((t, D), MX), jax.ShapeDtypeStruct((K_SSD, D), f32),
                   jax.ShapeDtypeStruct((1, D), f32)),
        in_specs=[pl.BlockSpec((t, CW), lambda j: (0, c0 + j)), pl.BlockSpec((K_SSD, CW), lambda j: (0, j)),
                  pl.BlockSpec((1, CW), lambda j: (0, j)), cb, cb, pl.BlockSpec((1, CW), lambda j: (0, j))],
        out_specs=(cb, pl.BlockSpec((K_SSD, CW), lambda j: (0, j)), pl.BlockSpec((1, CW), lambda j: (0, j))),
        scratch_shapes=[pltpu.VMEM((t, CW), f32), pltpu.VMEM((8 * K_SSD, CW), f32), pltpu.VMEM((8, CW), f32)],
        compiler_params=_cp("arbitrary"),
    )(proj, conv_w, conv_b, d_xs, d_y, d_skip_row)


def _ssd_conv_bwd_bc(proj, conv_w, conv_b, d_bc):
    t = proj.shape[0]
    pad = _pad_of(K_SSD)
    c0 = (OFF_XBC + D) // CW
    w0 = D // CW

    def body(x_ref, w_ref, b_ref, dbc_ref, draw_ref, dw_ref, db_ref, dp_scr, dw_scr, db_scr):
        dw_scr[...] = jnp.zeros_like(dw_scr)
        db_scr[...] = jnp.zeros_like(db_scr)

        def first(r, carry):
            rows = pl.ds(pl.multiple_of(r * RC, RC), RC)
            win = _causal_win(x_ref, r, t, pad)
            pre = _conv_taps(win, w_ref, K_SSD, pad) + b_ref[...]
            dpre = dbc_ref[0, rows, :] * _dsilu(pre)
            dp_scr[rows, :] = dpre
            _dw_accumulate(dw_scr, dpre, win, K_SSD, pad)
            db_scr[...] += _rows8(dpre)
            return carry
        lax.fori_loop(0, t // RC, first, 0)

        def second(r, carry):
            rows = pl.ds(pl.multiple_of(r * RC, RC), RC)
            draw_ref[rows, :] = _corr_taps(_anti_win(dp_scr, r, t, pad), w_ref, K_SSD).astype(MX)
            return carry
        lax.fori_loop(0, t // RC, second, 0)
        _dw_finish(dw_scr, dw_ref, K_SSD)
        db_ref[...] = jnp.sum(db_scr[...], axis=0, keepdims=True)

    return pl.pallas_call(
        body, name="ssd_conv_bwd_bc", grid=(2,),
        out_shape=(jax.ShapeDtypeStruct((t, 2 * CW), MX), jax.ShapeDtypeStruct((K_SSD, 2 * CW), f32),
                   jax.ShapeDtypeStruct((1, 2 * CW), f32)),
        in_specs=[pl.BlockSpec((t, CW), lambda j: (0, c0 + j)), pl.BlockSpec((K_SSD, CW), lambda j: (0, w0 + j)),
                  pl.BlockSpec((1, CW), lambda j: (0, w0 + j)), pl.BlockSpec((1, t, CW), lambda j: (j, 0, 0))],
        out_specs=(pl.BlockSpec((t, CW), lambda j: (0, j)), pl.BlockSpec((K_SSD, CW), lambda j: (0, j)),
                   pl.BlockSpec((1, CW), lambda j: (0, j))),
        scratch_shapes=[pltpu.VMEM((t, CW), f32), pltpu.VMEM((8 * K_SSD, CW), f32), pltpu.VMEM((8, CW), f32)],
        compiler_params=_cp("arbitrary"),
    )(proj, conv_w, conv_b, d_bc)


def _chunk_masks():
    ii = lax.broadcasted_iota(jnp.int32, (CHUNK, CHUNK), 0)
    jj = lax.broadcasted_iota(jnp.int32, (CHUNK, CHUNK), 1)
    return ii == jj, jj <= ii, jj >= ii


def _to_row(col, eye):
    return jnp.sum(jnp.where(eye, col, 0.0), axis=0, keepdims=True)


def _to_col(row, eye):
    return jnp.sum(jnp.where(eye, row, 0.0), axis=1, keepdims=True)


def _head_decay(dt_h, a_h, eye, tril):
    a_row = _to_row(dt_h * a_h, eye)
    cs = jnp.sum(jnp.where(tril, a_row, 0.0), axis=1, keepdims=True)
    cs_row = _to_row(cs, eye)
    decay = jnp.where(tril, jnp.exp(jnp.where(tril, cs - cs_row, 0.0)), 0.0)
    total = jnp.sum(a_row, axis=1, keepdims=True)
    return cs, decay, total


SCAN_UNROLL = 4


def _unrolled_loop(n, step, init):
    unroll = min(SCAN_UNROLL, n)
    assert n % unroll == 0

    def trip(i, carry):
        for u in range(unroll):
            carry = step(unroll * i + u, carry)
        return carry
    return lax.fori_loop(0, n // unroll, trip, init)


def _lane_pick(mat, lane, which):
    return jnp.sum(jnp.where(lane == which, mat, 0.0), axis=1, keepdims=True)


def _ssd_fwd(xbc_act, proj, dt_bias_row, a_log_row, rider=None):
    t = xbc_act.shape[0]
    nc = t // CHUNK
    cb, cc, cdt = D // LANES, (D + 2 * STATE_N) // LANES, OFF_DT // LANES

    def body(x_ref, b_ref, c_ref, dt_ref, dtb_ref, alog_ref, y_ref, st_ref):
        j = pl.program_id(0)
        eye, tril, _ = _chunk_masks()
        lane = lax.broadcasted_iota(jnp.int32, (1, LANES), 1)
        first = lane < HEAD_P
        a_row = -jnp.exp(alog_ref[...])
        a_heads = [jnp.sum(jnp.where(lane == 2 * j + h, a_row, 0.0), axis=1, keepdims=True) for h in range(2)]

        def chunk(c, hprev):
            rows = pl.ds(pl.multiple_of(c * CHUNK, CHUNK), CHUNK)
            xv, bm, cm = x_ref[rows, :], b_ref[rows, :], c_ref[rows, :]
            dt = _softplus(dt_ref[rows, :] + dtb_ref[...])
            st_ref[c] = hprev
            g = _mm_nt(cm, bm)
            ch = _mm(cm, hprev)
            dts = [_lane_pick(dt, lane, 2 * j + h) for h in range(2)]
            xdt = xv * jnp.where(first, dts[0], dts[1])
            ys, hs = [], []
            for h in range(2):
                cs, decay, total = _head_decay(dts[h], a_heads[h], eye, tril)
                y = _mm(g * decay, xdt) + jnp.exp(cs) * ch
                s = _mm_tn(bm * jnp.exp(total - cs), xdt)
                ys.append(y)
                hs.append(jnp.exp(total) * hprev + s)
            y_ref[rows, :] = jnp.where(first, ys[0], ys[1])
            return jnp.where(first, hs[0], hs[1])

        _unrolled_loop(nc, chunk, jnp.zeros((STATE_N, LANES), f32))

    blk = lambda f: pl.BlockSpec((t, LANES), f)
    return _call(
        body, name="ssd_fwd", grid=(D // LANES,),
        out_shape=(jax.ShapeDtypeStruct((t, D), f32), jax.ShapeDtypeStruct((nc, STATE_N, D), f32)),
        in_specs=[blk(lambda j: (0, j)), blk(lambda j: (0, cb + j // 4)), blk(lambda j: (0, cc + j // 4)),
                  blk(lambda j: (0, cdt)), _row(LANES), _row(LANES)],
        out_specs=(blk(lambda j: (0, j)), pl.BlockSpec((nc, STATE_N, LANES), lambda j: (0, 0, j))),
        sem=("arbitrary",), args=(xbc_act, xbc_act, xbc_act, proj, dt_bias_row, a_log_row), rider=rider)


def _ssd_bwd(xbc_act, proj, dt_bias_row, a_log_row, states, d_y, rider=None):
    t = xbc_act.shape[0]
    nc = t // CHUNK
    cb, cc, cdt = D // LANES, (D + 2 * STATE_N) // LANES, OFF_DT // LANES

    def body(x_ref, b_ref, c_ref, dt_ref, dtb_ref, alog_ref, st_ref, dy_ref, dx_ref, dbc_ref, ddt_ref, da_ref):
        grp, p = pl.program_id(0), pl.program_id(1)
        j = 4 * grp + p
        eye, tril, triu = _chunk_masks()
        lane = lax.broadcasted_iota(jnp.int32, (1, LANES), 1)
        first = lane < HEAD_P
        last_row = lax.broadcasted_iota(jnp.int32, (CHUNK, 1), 0) == CHUNK - 1
        a_row = -jnp.exp(alog_ref[...])
        a_heads = [jnp.sum(jnp.where(lane == 2 * j + h, a_row, 0.0), axis=1, keepdims=True) for h in range(2)]

        @pl.when(p == 0)
        def _():
            dbc_ref[...] = jnp.zeros_like(dbc_ref)

        @pl.when(j == 0)
        def _():
            ddt_ref[...] = jnp.zeros_like(ddt_ref)
            da_ref[...] = jnp.zeros_like(da_ref)

        def chunk(i, dh):
            c = nc - 1 - i
            rows = pl.ds(pl.multiple_of(c * CHUNK, CHUNK), CHUNK)
            xv, bm, cm = x_ref[rows, :], b_ref[rows, :], c_ref[rows, :]
            dtr = dt_ref[rows, :] + dtb_ref[...]
            dt = _softplus(dtr)
            hprev = st_ref[c]
            dy = dy_ref[rows, :]
            g = _mm_nt(cm, bm)
            dts = [_lane_pick(dt, lane, 2 * j + h) for h in range(2)]
            xdt = xv * jnp.where(first, dts[0], dts[1])
            dxs, dhs = [], []
            db_sum, dc_sum = None, None
            ddt_mat = jnp.zeros((CHUNK, LANES), f32)
            da_acc = jnp.zeros((1, LANES), f32)
            for h in range(2):
                mine = first if h == 0 else jnp.logical_not(first)
                cs, decay, total = _head_decay(dts[h], a_heads[h], eye, tril)
                e_cs, e_tot = jnp.exp(cs), jnp.exp(total)
                dec_s = jnp.exp(total - cs)
                dyh = jnp.where(mine, dy, 0.0)
                xdth = jnp.where(mine, xdt, 0.0)
                dhh = jnp.where(mine, dh, 0.0)
                hph = jnp.where(mine, hprev, 0.0)
                m = g * decay
                dm = _mm_nt(dyh, xdth)
                dg = dm * decay
                w = dm * m
                bdec = bm * dec_s
                dxdt = _mm_tn(m, dyh) + _mm(bdec, dhh)
                dc_off = _mm_nt(dyh, hph) * e_cs
                db_s = _mm_nt(xdth, dhh) * dec_s
                dc_h = _mm(dg, bm) + dc_off
                db_h = _mm_tn(dg, cm) + db_s
                r_s = jnp.sum(db_s * bm, axis=1, keepdims=True)
                dtotal = jnp.sum(r_s, axis=0, keepdims=True) + e_tot * jnp.sum(
                    jnp.sum(dhh * hph, axis=1, keepdims=True), axis=0, keepdims=True)
                dcs = (jnp.sum(w, axis=1, keepdims=True) - _to_col(jnp.sum(w, axis=0, keepdims=True), eye)
                       + jnp.sum(dc_off * cm, axis=1, keepdims=True) - r_s + jnp.where(last_row, dtotal, 0.0))
                da_col = jnp.sum(jnp.where(triu, _to_row(dcs, eye), 0.0), axis=1, keepdims=True)
                ddt = da_col * a_heads[h] + jnp.sum(jnp.where(mine, dxdt * xv, 0.0), axis=1, keepdims=True)
                ddt_mat = ddt_mat + jnp.where(lane == 2 * j + h, ddt, 0.0)
                da_acc = da_acc + jnp.where(lane == 2 * j + h, jnp.sum(da_col * dts[h], axis=0, keepdims=True), 0.0)
                dxs.append(dxdt * dts[h])
                dhs.append(e_tot * dhh + _mm_tn(cm * e_cs, dyh))
                db_sum = db_h if db_sum is None else db_sum + db_h
                dc_sum = dc_h if dc_sum is None else dc_sum + dc_h
            dx_ref[rows, :] = jnp.where(first, dxs[0], dxs[1])
            dbc_ref[0, rows, :] += db_sum
            dbc_ref[1, rows, :] += dc_sum
            ddt_ref[rows, :] += ddt_mat * jax.nn.sigmoid(dtr)
            da_ref[...] += da_acc * a_row
            return jnp.where(first, dhs[0], dhs[1])

        _unrolled_loop(nc, chunk, jnp.zeros((STATE_N, LANES), f32))

    blk = lambda f: pl.BlockSpec((t, LANES), f)
    return _call(
        body, name="ssd_bwd", grid=(2, 4),
        out_shape=(jax.ShapeDtypeStruct((t, D), f32), jax.ShapeDtypeStruct((2, t, 2 * STATE_N), f32),
                   jax.ShapeDtypeStruct((t, LANES), f32), jax.ShapeDtypeStruct((1, LANES), f32)),
        in_specs=[blk(lambda g, p: (0, 4 * g + p)), blk(lambda g, p: (0, cb + g)), blk(lambda g, p: (0, cc + g)),
                  blk(lambda g, p: (0, cdt)), _row(LANES), _row(LANES),
                  pl.BlockSpec((nc, STATE_N, LANES), lambda g, p: (0, 0, 4 * g + p)), blk(lambda g, p: (0, 4 * g + p))],
        out_specs=(blk(lambda g, p: (0, 4 * g + p)), pl.BlockSpec((2, t, LANES), lambda g, p: (0, 0, g)),
                   blk(lambda g, p: (0, 0)), _row(LANES)),
        sem=("arbitrary", "arbitrary"), args=(xbc_act, xbc_act, xbc_act, proj, dt_bias_row, a_log_row, states, d_y),
        rider=rider)


def _up_bwd(d_up, w_up, x1, mod, norm2_w, dx2, mix, w_out, rider=None):
    t = x1.shape[0]

    def body(dup_ref, wu_ref, x1_ref, mod_ref, nw_ref, dx2_ref, mix_ref, wo_ref,
             dx1_ref, dmix_ref, dys_ref, du_ref, st_ref):
        @pl.when(pl.program_id(0) == 0)
        def _():
            st_ref[...] = jnp.zeros_like(st_ref)

        nt = (((1,), (1,)), ((), ()))
        dh = None
        for k in range(4):
            lo = (k % 2) * UP_SHARD
            part = lax.dot_general(dup_ref[k // 2, :, lo:lo + UP_SHARD], wu_ref[k], nt, preferred_element_type=f32)
            dh = part if dh is None else dh + part
        x1 = x1_ref[...]
        rstd = lax.rsqrt(jnp.mean(x1 * x1, axis=-1, keepdims=True) + 1e-6)
        xh = x1 * rstd
        nw = nw_ref[...]
        sc = 1.0 + mod_ref[:, 4 * D:5 * D]
        st_ref[0:1, :] += jnp.sum(dh, axis=0, keepdims=True)
        st_ref[1:2, :] += jnp.sum(dh * xh * nw, axis=0, keepdims=True)
        st_ref[2:3, :] += jnp.sum(dh * sc * xh, axis=0, keepdims=True)
        dxh = dh * sc * nw
        dx1 = dx2_ref[...] + rstd * (dxh - xh * jnp.mean(dxh * xh, axis=-1, keepdims=True))
        dx1_ref[...] = dx1
        st_ref[3:4, :] += jnp.sum(dx1 * mix_ref[...], axis=0, keepdims=True)
        dmix = (mod_ref[:, 2 * D:3 * D] * dx1).astype(MX)
        dmix_ref[...] = dmix
        dys_ref[...] = lax.dot_general(dmix, wo_ref[0:D, :], nt, preferred_element_type=f32)
        du_ref[...] = lax.dot_general(dmix, wo_ref[D:2 * D, :], nt, preferred_element_type=f32)

    blk = pl.BlockSpec((TM, D), lambda i: (i, 0))
    return _call(
        body, name="up_bwd", grid=(t // TM,),
        out_shape=(jax.ShapeDtypeStruct((t, D), f32), jax.ShapeDtypeStruct((t, D), MX),
                   jax.ShapeDtypeStruct((t, D), f32), jax.ShapeDtypeStruct((t, D), f32),
                   jax.ShapeDtypeStruct((8, D), f32)),
        in_specs=[pl.BlockSpec((2, TM, D_FF), lambda i: (0, i, 0)), _resident((4, D, UP_SHARD)), blk, _row(6 * D), _row(),
                  blk, blk, _resident((2 * D, D))],
        out_specs=(blk, blk, blk, blk, pl.BlockSpec((8, D), lambda i: (0, 0))),
        sem=("arbitrary",), args=(d_up, w_up, x1, mod, norm2_w, dx2, mix, w_out), rider=rider)


def _ln_silu_bwd(d_u, u_conv, ln_w, ln_b):
    t = d_u.shape[0]

    def body(du_ref, u_ref, w_ref, b_ref, o_ref, st_ref):
        @pl.when(pl.program_id(0) == 0)
        def _():
            st_ref[...] = jnp.zeros_like(st_ref)

        u = u_ref[...]
        mu = jnp.mean(u, axis=-1, keepdims=True)
        uc = u - mu
        rstd = lax.rsqrt(jnp.mean(uc * uc, axis=-1, keepdims=True) + 1e-5)
        n = uc * rstd
        w = w_ref[...]
        dl = du_ref[...] * _dsilu(n * w + b_ref[...])
        st_ref[0:1, :] += jnp.sum(dl * n, axis=0, keepdims=True)
        st_ref[1:2, :] += jnp.sum(dl, axis=0, keepdims=True)
        dn = dl * w
        o_ref[...] = rstd * (dn - jnp.mean(dn, axis=-1, keepdims=True) - n * jnp.mean(dn * n, axis=-1, keepdims=True))

    blk = pl.BlockSpec((TM, D), lambda i: (i, 0))
    return pl.pallas_call(
        body, name="ln_silu_bwd", grid=(t // TM,),
        out_shape=(jax.ShapeDtypeStruct((t, D), f32), jax.ShapeDtypeStruct((8, D), f32)),
        in_specs=[blk, blk, _row(), _row()], out_specs=(blk, pl.BlockSpec((8, D), lambda i: (0, 0))),
        compiler_params=_cp("arbitrary"),
    )(d_u, u_conv, ln_w, ln_b)


def _ssd_gate_norm_bwd(d_out, y_scan, xbc_act, proj, d_skip_row, ssd_norm_w):
    t = d_out.shape[0]

    def body(do_ref, y_ref, xs_ref, z_ref, dsk_ref, nw_ref, dy_ref, dz_ref, st_ref):
        @pl.when(pl.program_id(0) == 0)
        def _():
            st_ref[...] = jnp.zeros_like(st_ref)

        xs = xs_ref[...]
        y = y_ref[...] + xs * dsk_ref[...]
        z = z_ref[...]
        s = _silu(z)
        yz = y * s
        rstd = lax.rsqrt(jnp.mean(yz * yz, axis=-1, keepdims=True) + 1e-6)
        n = yz * rstd
        do = do_ref[...]
        st_ref[0:1, :] += jnp.sum(do * n, axis=0, keepdims=True)
        dn = do * nw_ref[...]
        dyz = rstd * (dn - n * jnp.mean(dn * n, axis=-1, keepdims=True))
        dy = dyz * s
        dy_ref[...] = dy
        dz_ref[...] = (dyz * y * _dsilu(z)).astype(MX)
        st_ref[1:2, :] += jnp.sum(dy * xs, axis=0, keepdims=True)

    blk = pl.BlockSpec((TM, D), lambda i: (i, 0))
    return pl.pallas_call(
        body, name="ssd_gate_norm_bwd", grid=(t // TM,),
        out_shape=(jax.ShapeDtypeStruct((t, D), f32), jax.ShapeDtypeStruct((t, D), MX), jax.ShapeDtypeStruct((8, D), f32)),
        in_specs=[blk, blk, blk, blk, _row(), _row()], out_specs=(blk, blk, pl.BlockSpec((8, D), lambda i: (0, 0))),
        compiler_params=_cp("arbitrary"),
    )(d_out, y_scan, xbc_act, proj, d_skip_row, ssd_norm_w)


def _inproj_bwd(d_z, d_xraw, d_bcraw, d_conf, d_dt, w_pack, x, mod, norm1_w, dx1, after=None):
    t = x.shape[0]
    extra = [] if after is None else [after]

    def body(dz_ref, dx_ref, dbc_ref, dcf_ref, ddt_ref, w_ref, x_ref, mod_ref, nw_ref, dx1_ref, *rest):
        gx_ref, st_ref = rest[-2:]
        @pl.when(pl.program_id(0) == 0)
        def _():
            st_ref[...] = jnp.zeros_like(st_ref)

        nt = (((1,), (1,)), ((), ()))
        dot = lambda a, lo, hi: lax.dot_general(a, w_ref[:, lo:hi], nt, preferred_element_type=f32)
        dh = dot(dz_ref[...], OFF_Z, OFF_Z + D)
        dh = dh + dot(dx_ref[...], OFF_XBC, OFF_XBC + D)
        dh = dh + dot(dbc_ref[...], OFF_XBC + D, OFF_XBC + D_XBC)
        dh = dh + dot(dcf_ref[0], OFF_CA, OFF_CA + D)
        dh = dh + dot(dcf_ref[1], OFF_CG, OFF_CG + D)
        dh = dh + dot(ddt_ref[...].astype(MX), OFF_DT, OFF_DT + LANES)
        st_ref[3:4, 0:LANES] += jnp.sum(ddt_ref[...], axis=0, keepdims=True)
        xv = x_ref[...]
        rstd = lax.rsqrt(jnp.mean(xv * xv, axis=-1, keepdims=True) + 1e-6)
        xh = xv * rstd
        nw = nw_ref[...]
        sc = 1.0 + mod_ref[:, D:2 * D]
        st_ref[0:1, :] += jnp.sum(dh, axis=0, keepdims=True)
        st_ref[1:2, :] += jnp.sum(dh * xh * nw, axis=0, keepdims=True)
        st_ref[2:3, :] += jnp.sum(dh * sc * xh, axis=0, keepdims=True)
        dxh = dh * sc * nw
        gx_ref[...] = dx1_ref[...] + rstd * (dxh - xh * jnp.mean(dxh * xh, axis=-1, keepdims=True))

    blk = pl.BlockSpec((TM, D), lambda i: (i, 0))
    return _call(
        body, name="inproj_bwd", grid=(t // TM,),
        out_shape=(jax.ShapeDtypeStruct((t, D), f32), jax.ShapeDtypeStruct((8, D), f32)),
        in_specs=[blk, blk, pl.BlockSpec((TM, 2 * CW), lambda i: (i, 0)), pl.BlockSpec((2, TM, D), lambda i: (0, i, 0)),
                  pl.BlockSpec((TM, LANES), lambda i: (i, 0)), _resident((D, W_PACK)), blk, _row(6 * D), _row(), blk]
        + [ANY] * len(extra),
        out_specs=(blk, pl.BlockSpec((8, D), lambda i: (0, 0))),
        sem=("arbitrary",), args=(d_z, d_xraw, d_bcraw, d_conf, d_dt, w_pack, x, mod, norm1_w, dx1, *extra))[0]


def _wgrad(a, d, name, bn=256, transposed=True):
    k, t = a.shape if transposed else a.shape[::-1]
    n = d.shape[1]
    out_dtype = MX
    contract = (((1,), (0,)), ((), ())) if transposed else (((0,), (0,)), ((), ()))

    def body(a_ref, d_ref, o_ref):
        o_ref[...] = lax.dot_general(a_ref[...], d_ref[...].astype(MX), contract, preferred_element_type=f32).astype(out_dtype)

    return pl.pallas_call(
        body, name=name, grid=(n // bn,), out_shape=jax.ShapeDtypeStruct((k, n), out_dtype),
        in_specs=[_resident(a.shape), pl.BlockSpec((t, bn), lambda j: (0, j))],
        out_specs=pl.BlockSpec((k, bn), lambda j: (0, j)), compiler_params=_cp("arbitrary"),
    )(a, d)


def _wgrad_stacked(at, d, name, bn):
    out_dtype = MX
    k, t = at.shape
    s, _, n = d.shape
    nb = n // bn

    def body(a_ref, d_ref, o_ref):
        o_ref[0] = jnp.dot(a_ref[...], d_ref[0], preferred_element_type=f32).astype(out_dtype)

    return pl.pallas_call(
        body, name=name, grid=(s, nb), out_shape=jax.ShapeDtypeStruct((s * nb, k, bn), out_dtype),
        in_specs=[_resident((k, t)), pl.BlockSpec((1, t, bn), lambda i, j: (i, 0, j))],
        out_specs=pl.BlockSpec((1, k, bn), lambda i, j: (i * nb + j, 0, 0)), compiler_params=_cp("arbitrary", "arbitrary"),
    )(at, d)


def _pad_row(v, width=LANES):
    return jnp.pad(v.reshape(1, -1), ((0, 0), (0, width - v.size)))


def _quarters(a):
    return a.reshape(4, 2, a.shape[0] // 8, a.shape[1])


def _local_step(x, mod, target, w_pack, late, small, reducer=None):
    dtb_row, alog_row = _pad_row(small["dt_bias"]), _pad_row(small["a_log"])
    dskip_row = jnp.repeat(small["d_skip"].reshape(-1), HEAD_P).reshape(1, D)

    red = reducer

    def hosted(host, args, swap=None, scatter=None, gather=None, sums=()):
        if red is None:
            return host(*args)[0]
        riders = ([red.scatter(scatter)] if scatter else []) + ([red.swap(*swap)] if swap else [])
        riders += [_SwapSumsRider([red.sums[n] for n in sums])] if sums else []
        riders += [_GatherRider([gather[0]], *gather[1:])] if gather is not None else []
        both = _Riders(riders)
        outs, extra = host(*args, rider=both)
        extra = both.split(extra)
        if scatter:
            red.scattered(scatter, extra.pop(0))
        if swap:
            red.swapped(swap[0], extra.pop(0))
        if sums:
            red.others.update(zip(sums, extra.pop(0)))
        return (outs, extra[0][0]) if gather is not None else outs

    w_out, w_up, w_down = late
    if red is None:
        proj, h_t = hosted(_ln_inproj, (x, mod, small["norm1_w"], w_pack))
        xbc_act, = hosted(_ssd_conv_fwd, (proj, small["ssd_conv_w"], small["ssd_conv_b"]))
        y_scan, states = hosted(_ssd_fwd, (xbc_act, proj, dtb_row, alog_row))
        u_conv, = hosted(_glu_conv_fwd, (proj, small["conf_conv_w"], small["conf_conv_b"]))
    else:
        (proj, h_t), w_out = hosted(_ln_inproj, (x, mod, small["norm1_w"], w_pack), gather=(w_out,))
        (xbc_act,), w_up = hosted(_ssd_conv_fwd, (proj, small["ssd_conv_w"], small["ssd_conv_b"]), gather=(w_up, 0, UP_EARLY_ROWS))
        (y_scan, states), w_up = hosted(_ssd_fwd, (xbc_act, proj, dtb_row, alog_row), gather=(w_up, UP_EARLY_ROWS, None))
        (u_conv,), w_down = hosted(_glu_conv_fwd, (proj, small["conf_conv_w"], small["conf_conv_b"]), gather=(w_down,))
        w_out, w_up, w_down = w_out.reshape(2 * D, D), w_up.reshape(4, D, UP_SHARD), w_down.reshape(D_FF, D)
    y_ssd = _ssd_gate_norm(y_scan, xbc_act, proj, dskip_row, small["ssd_norm_w"])
    u = _ln_silu(u_conv, small["conf_ln_w"], small["conf_ln_b"])
    mix, x1, h2_t, up = _outproj_ln2_up(y_ssd, u, w_out, x, mod, small["norm2_w"], w_up)
    act, = _ffn_conv_fwd(up, small["ffn_conv_w"], small["ffn_conv_b"])[0]
    dx2, d_ffn, d_act, st_down = _down_loss(act, w_down, x1, mod, small["final_norm_w"], target)

    g_down = _quarters(_wgrad(act, d_ffn, "wgrad_down", transposed=False))
    d_up, dw_ffn, db_ffn = hosted(_ffn_conv_bwd, (up, small["ffn_conv_w"], small["ffn_conv_b"], d_act), swap=("w_down", g_down))
    g_up = _wgrad_stacked(h2_t, d_up, "wgrad_up", D_FF // 2).reshape(4, 2, D // 2, UP_SHARD)
    dx1, d_mix, d_yssd, d_u, st_up = hosted(_up_bwd, (d_up, w_up, x1, mod, small["norm2_w"], dx2, mix, w_out),
                                            scatter="w_down", swap=("w_up", g_up))
    g_out = _quarters(jnp.concatenate([_wgrad(y_ssd, d_mix, "wgrad_out_y", transposed=False),
                                       _wgrad(u, d_mix, "wgrad_out_u", transposed=False)], axis=0))
    d_uconv, st_ln = _ln_silu_bwd(d_u, u_conv, small["conf_ln_w"], small["conf_ln_b"])
    d_conf, dw_conf, db_conf = hosted(_glu_conv_bwd, (proj, small["conf_conv_w"], d_uconv), scatter="w_up",
                                      swap=("w_out", g_out))
    d_y, d_z, st_gn = _ssd_gate_norm_bwd(d_yssd, y_scan, xbc_act, proj, dskip_row, small["ssd_norm_w"])
    d_xs, d_bc, d_dt, d_alog = hosted(_ssd_bwd, (xbc_act, proj, dtb_row, alog_row, states, d_y), scatter="w_out")
    d_xraw, dw_sx, db_sx = _ssd_conv_bwd_x(proj, small["ssd_conv_w"], small["ssd_conv_b"], d_xs, d_y, dskip_row)
    d_bcraw, dw_sbc, db_sbc = _ssd_conv_bwd_bc(proj, small["ssd_conv_w"], small["ssd_conv_b"], d_bc)
    g_in = _unpack_g_in(dict(
        z=_wgrad(h_t, d_z, "wgrad_in_z"), x=_wgrad(h_t, d_xraw, "wgrad_in_x"), bc=_wgrad(h_t, d_bcraw, "wgrad_in_bc"),
        conf=_wgrad_stacked(h_t, d_conf, "wgrad_in_conf", D), dt=_wgrad(h_t, d_dt, "wgrad_in_dt", bn=LANES)))
    g_in = g_in.reshape(4, 2, D // 2, W_IN_SHARD_PAD)
    args = (d_z, d_xraw, d_bcraw, d_conf, d_dt, w_pack, x, mod, small["norm1_w"], dx1)
    if red is None:
        grad_x, st_in = _inproj_bwd(*args)
    else:
        done = ("w_out", "w_up", "w_down")
        both = _Riders([red.swap("w_in", g_in), _SwapSumsRider([red.sums[n] for n in done])])
        handles, token = _split_start(both, "swap_start_w_in")
        grad_x, st_in = _inproj_bwd(*args, after=token)
        thru, outs = _split_wait(both, "swap_wait_w_in", handles, st_in)
        red.grads["w_in"] = thru[0]
        red.sums.update(zip(done, thru[1:]))
        got, others = both.split(outs)
        red.swapped("w_in", got)
        red.others.update(zip(done, others))

    gsmall = _pack_small_grads(st_in, st_up, st_down, st_ln, st_gn, d_alog, dw_sx, dw_sbc, db_sx, db_sbc, dw_conf, db_conf,
                               dw_ffn, db_ffn)
    gbig = None if reducer is not None else dict(w_in=g_in, w_out=g_out, w_up=g_up, w_down=g_down)
    return st_down[2, 0], grad_x, gbig, gsmall


VECTORS = ("ada_b", "norm1_w", "ssd_conv_b", "dt_bias", "a_log", "d_skip", "ssd_norm_w", "conf_conv_b", "conf_ln_w",
           "conf_ln_b", "norm2_w", "ffn_conv_b", "final_norm_w")
VECTOR_SIZES = (6 * D, D, D_XBC, HEADS, HEADS, HEADS, D, D, D, D, D, 2 * D_FF, D)
CONVS = {"ssd_conv_w": (K_SSD, D_XBC), "conf_conv_w": (K_CONF, D), "ffn_conv_w": (K_FFN, 2 * D_FF)}


def _pack_rows(items):
    n = -(-sum(w for _, w in items) // (8 * LANES)) * LANES
    while True:
        fill, place = [0] * 8, {}
        for key, w in sorted(items, key=lambda kv: -kv[1]):
            rows = [r for r in range(8) if fill[r] + w <= n]
            if not rows:
                break
            place[key] = (rows[0], fill[rows[0]])
            fill[rows[0]] += w
        if len(place) == len(items):
            return n, place
        n += LANES


FRONT_N, FRONT = _pack_rows([("c", D)] + [((nm, j), cols // 4) for nm, (taps, cols) in CONVS.items() for j in range(taps)])
BACK_N, BACK = _pack_rows([(nm, -(-sz // LANES) * LANES) for nm, sz in zip(VECTORS, VECTOR_SIZES)]
                          + [((nm, j), cols) for nm, (taps, cols) in CONVS.items() for j in range(taps)] + [("loss", LANES)])
_VM = pltpu.CompilerParams(vmem_limit_bytes=VMEM_LIMIT)


def _pack_front(c, shards):
    def body(c_ref, *refs):
        o_ref = refs[-1]
        o_ref[...] = jnp.zeros_like(o_ref)
        r, o = FRONT["c"]
        o_ref[r:r + 1, o:o + D] = c_ref[...]
        for ref, (nm, (taps, cols)) in zip(refs, CONVS.items()):
            for j in range(taps):
                r, o = FRONT[(nm, j)]
                o_ref[r:r + 1, o:o + cols // 4] = ref[0, j:j + 1, :]

    return pl.pallas_call(body, name="pack_front", out_shape=jax.ShapeDtypeStruct((8, FRONT_N), f32),
                          compiler_params=_VM)(c, *shards)


def _unpack_front(got):
    def body(g_ref, c_ref, *outs):
        r, o = FRONT["c"]
        for d in range(8):
            c_ref[d:d + 1, :] = g_ref[8 * d + r:8 * d + r + 1, o:o + D]
        for ref, (nm, (taps, cols)) in zip(outs, CONVS.items()):
            cw = cols // 4
            for j in range(taps):
                r, o = FRONT[(nm, j)]
                for k in range(4):
                    ref[j:j + 1, k * cw:(k + 1) * cw] = g_ref[16 * k + r:16 * k + r + 1, o:o + cw]

    return pl.pallas_call(
        body, name="unpack_front", compiler_params=_VM,
        out_shape=(jax.ShapeDtypeStruct((8, D), f32),) + tuple(jax.ShapeDtypeStruct(tc, f32) for tc in CONVS.values()),
    )(got)


def _pack_small_grads(st_in, st_up, st_down, st_ln, st_gn, d_alog, dw_sx, dw_sbc, db_sx, db_sbc, dw_conf, db_conf, dw_ffn,
                      db_ffn):
    def body(in_ref, up_ref, dn_ref, ln_ref, gn_ref, al_ref, wx_ref, wbc_ref, bx_ref, bbc_ref, wc_ref, bc_ref, wf_ref, bf_ref,
             o_ref):
        def put(key, val, shift=0):
            r, o = BACK[key]
            o_ref[r:r + 1, o + shift:o + shift + val.shape[1]] = val

        o_ref[...] = jnp.zeros_like(o_ref)
        for i, piece in enumerate((in_ref[0:1, :], in_ref[1:2, :], up_ref[3:4, :], up_ref[0:1, :], up_ref[1:2, :],
                                   dn_ref[1:2, :])):
            put("ada_b", piece, i * D)
        put("norm1_w", in_ref[2:3, :])
        put("ssd_conv_b", bx_ref[...])
        put("ssd_conv_b", bbc_ref[...], D)
        put("dt_bias", in_ref[3:4, 0:LANES])
        put("a_log", al_ref[...])
        lane = lax.broadcasted_iota(jnp.int32, (1, LANES), 1)
        col = lax.broadcasted_iota(jnp.int32, (1, D), 1)
        per_col = gn_ref[1:2, :]
        d_skip = jnp.zeros((1, LANES), f32)
        for h in range(HEADS):
            in_head = jnp.logical_and(col >= h * HEAD_P, col < (h + 1) * HEAD_P)
            s = jnp.sum(jnp.where(in_head, per_col, 0.0), axis=1, keepdims=True)
            d_skip = d_skip + jnp.where(lane == h, s, 0.0)
        put("d_skip", d_skip)
        put("ssd_norm_w", gn_ref[0:1, :])
        put("conf_conv_b", bc_ref[...])
        put("conf_ln_w", ln_ref[0:1, :])
        put("conf_ln_b", ln_ref[1:2, :])
        put("norm2_w", up_ref[2:3, :])
        put("ffn_conv_b", bf_ref[0])
        put("ffn_conv_b", bf_ref[1], D_FF)
        put("final_norm_w", dn_ref[0:1, :])
        put("loss", dn_ref[2:3, 0:LANES])
        for j in range(K_SSD):
            put(("ssd_conv_w", j), wx_ref[j:j + 1, :])
            put(("ssd_conv_w", j), wbc_ref[j:j + 1, :], D)
        for j in range(K_CONF):
            put(("conf_conv_w", j), wc_ref[j:j + 1, :])
        for j in range(K_FFN):
            put(("ffn_conv_w", j), wf_ref[0, j:j + 1, :])
            put(("ffn_conv_w", j), wf_ref[1, j:j + 1, :], D_FF)

    return pl.pallas_call(body, name="pack_small_grads", out_shape=jax.ShapeDtypeStruct((8, BACK_N), f32), compiler_params=_VM)(
        st_in, st_up, st_down, st_ln, st_gn, d_alog, dw_sx, dw_sbc, db_sx, db_sbc, dw_conf, db_conf, dw_ffn, db_ffn)


def _small_adamw(got, chip, w, m, v):
    names = VECTORS + tuple(CONVS)
    n_par = len(names)

    def body(chip_ref, g_ref, *refs):
        ins, outs = refs[:3 * n_par], refs[3 * n_par:]
        dm_ref, loss_ref, outs = outs[0], outs[1], outs[2:]
        chip_id = chip_ref[0]

        def summed(key, width):
            r, o = BACK[key]
            s = g_ref[r:r + 1, o:o + width]
            for d in range(1, 8):
                s = s + g_ref[8 * d + r:8 * d + r + 1, o:o + width]
            return s

        def mine(full, cw):
            out = full[:, 0:cw]
            for k in range(1, 4):
                out = jnp.where(chip_id == k, full[:, k * cw:(k + 1) * cw], out)
            return out

        r, o = BACK["ada_b"]
        for d in range(8):
            dm_ref[d:d + 1, :] = mine(g_ref[8 * d + r:8 * d + r + 1, o:o + 6 * D], 6 * D // 4)
        loss_ref[...] = summed("loss", LANES)
        for i, (nm, size) in enumerate(zip(VECTORS, VECTOR_SIZES)):
            g = summed(nm, -(-size // LANES) * LANES)[:, 0:size]
            res = _adam_math(ins[3 * i][...], g, ins[3 * i + 1][...], ins[3 * i + 2][...])
            for ref, val in zip(outs[4 * i:4 * i + 4], (g,) + res):
                ref[...] = val
        for i, (nm, (taps, cols)) in enumerate(CONVS.items(), start=len(VECTORS)):
            for j in range(taps):
                g = mine(summed((nm, j), cols), cols // 4)
                res = _adam_math(ins[3 * i][0, j:j + 1, :], g, ins[3 * i + 1][0, j:j + 1, :], ins[3 * i + 2][0, j:j + 1, :])
                for ref, val in zip(outs[4 * i:4 * i + 4], (g,) + res):
                    ref[0, j:j + 1, :] = val

    params = [a[nm] for nm in names for a in (w, m, v)]
    whole = lambda s: pl.BlockSpec(s, lambda i, chip, nd=len(s): (0,) * nd)
    out_shape = [jax.ShapeDtypeStruct((8, 6 * D // 4), f32), jax.ShapeDtypeStruct((1, LANES), f32)]
    out_shape += [jax.ShapeDtypeStruct(w[nm].shape, f32) for nm in names for _ in range(4)]
    outs = pl.pallas_call(
        body, name="small_adamw", out_shape=tuple(out_shape), compiler_params=_VM,
        grid_spec=pltpu.PrefetchScalarGridSpec(
            num_scalar_prefetch=1, grid=(1,), in_specs=[whole(got.shape)] + [whole(p.shape) for p in params],
            out_specs=tuple(whole(s.shape) for s in out_shape)),
    )(_scalar(chip), got, *params)
    return outs[0], outs[1][0, 0], {nm: outs[2 + 4 * i:6 + 4 * i] for i, nm in enumerate(names)}


W_IN_COLS = 4624
W_IN_SHARD = W_IN_COLS // 4
W_IN_SHARD_PAD = 1280
_SEGMENTS = ((0, 1024, OFF_Z), (1024, 2560, OFF_XBC), (2560, 2576, OFF_DT), (2576, 3600, OFF_CA), (3600, 4624, OFF_CG))


def _in_pieces(bounds=()):
    out = []
    for k in range(4):
        s0, s1 = k * W_IN_SHARD, (k + 1) * W_IN_SHARD
        for lo, hi, off in _SEGMENTS:
            a, b = max(lo, s0), min(hi, s1)
            while a < b:
                p = off + a - lo
                e = min([b - a] + [c - p for c in bounds if c > p])
                out.append((k, a - s0, p, e))
                a += e
    return out


def _pack_w_in(shards):
    pieces = _in_pieces()

    def body(s_ref, o_ref):
        o_ref[:, OFF_DT:W_PACK] = jnp.zeros((TM, W_PACK - OFF_DT), MX)
        for k, c, p, n in pieces:
            o_ref[:, p:p + n] = s_ref[k, :, c:c + n]

    return pl.pallas_call(
        body, name="pack_w_in", grid=(D // TM,), out_shape=jax.ShapeDtypeStruct((D, W_PACK), MX),
        in_specs=[pl.BlockSpec((4, TM, W_IN_SHARD_PAD), lambda i: (0, i, 0))],
        out_specs=pl.BlockSpec((TM, W_PACK), lambda i: (i, 0)), compiler_params=_cp("arbitrary"),
    )(shards)


def _unpack_g_in(g):
    srcs = ((OFF_Z, D), (OFF_XBC, D), (OFF_XBC + D, 2 * CW), (OFF_CA, D), (OFF_CG, D), (OFF_DT, LANES))
    pieces = _in_pieces(tuple(o for o, _ in srcs) + tuple(o + n for o, n in srcs))

    def body(z_ref, x_ref, bc_ref, cf_ref, dt_ref, o_ref):
        read = (lambda lo, hi: z_ref[:, lo:hi], lambda lo, hi: x_ref[:, lo:hi], lambda lo, hi: bc_ref[:, lo:hi],
                lambda lo, hi: cf_ref[0, :, lo:hi], lambda lo, hi: cf_ref[1, :, lo:hi], lambda lo, hi: dt_ref[:, lo:hi])
        o_ref[:, :, W_IN_SHARD - 4:W_IN_SHARD_PAD] = jnp.zeros((4, TM, W_IN_SHARD_PAD - W_IN_SHARD + 4), MX)
        for k, c, p, n in pieces:
            i = [q for q, (o, w) in enumerate(srcs) if o <= p < o + w][0]
            o_ref[k, :, c:c + n] = read[i](p - srcs[i][0], p - srcs[i][0] + n)

    blk = lambda w: pl.BlockSpec((TM, w), lambda i: (i, 0))
    return pl.pallas_call(
        body, name="unpack_g_in", grid=(D // TM,), out_shape=jax.ShapeDtypeStruct((4, D, W_IN_SHARD_PAD), MX),
        in_specs=[blk(D), blk(D), blk(2 * CW), pl.BlockSpec((2, TM, D), lambda i: (0, i, 0)), blk(LANES)],
        out_specs=pl.BlockSpec((4, TM, W_IN_SHARD_PAD), lambda i: (0, i, 0)), compiler_params=_cp("arbitrary"),
    )(g["z"], g["x"], g["bc"], g["conf"], g["dt"])


def _scalar(v):
    return jnp.reshape(v, (1,)).astype(jnp.int32)


def _cast_into_slot(w, width, chip):
    r, c = w.shape
    h = r // 2
    tm = _row_tile(h)
    nj = h // tm

    def body(chip_ref, w_ref, o_ref):
        v = w_ref[...].astype(MX)
        o_ref[0, 0] = v if width == c else jnp.concatenate([v, jnp.zeros((tm, width - c), MX)], axis=1)

    return pl.pallas_call(
        body, name=f"cast_into_slot_{r}x{c}", out_shape=jax.ShapeDtypeStruct((4, 2, h, width), MX),
        grid_spec=pltpu.PrefetchScalarGridSpec(
            num_scalar_prefetch=1, grid=(2, nj),
            in_specs=[pl.BlockSpec((tm, c), lambda i, j, chip: (i * nj + j, 0))],
            out_specs=pl.BlockSpec((1, 1, tm, width), lambda i, j, chip: (chip[0], i, j, 0))),
        compiler_params=_cp("arbitrary", "arbitrary"),
    )(_scalar(chip), w)


def _columns_first(w):
    return jnp.transpose(w, (2, 0, 1))


def _cast_into_slot_w_in(w_t, chip):
    h = D // 2
    nj = h // TM
    pad = W_IN_SHARD_PAD - W_IN_SHARD

    def body(chip_ref, w_ref, o_ref):
        cols = jnp.concatenate([w_ref[:, 0, :], jnp.zeros((pad, TM), f32)], axis=0)
        o_ref[0, 0] = cols.T.astype(MX)

    return pl.pallas_call(
        body, name="cast_into_slot_w_in", out_shape=jax.ShapeDtypeStruct((4, 2, h, W_IN_SHARD_PAD), MX),
        grid_spec=pltpu.PrefetchScalarGridSpec(
            num_scalar_prefetch=1, grid=(2, nj),
            in_specs=[pl.BlockSpec((W_IN_SHARD, 1, TM), lambda i, j, chip: (0, 0, i * nj + j))],
            out_specs=pl.BlockSpec((1, 1, TM, W_IN_SHARD_PAD), lambda i, j, chip: (chip[0], i, j, 0))),
        compiler_params=_cp("arbitrary", "arbitrary"),
    )(_scalar(chip), w_t)


def _adamw_w_in(w_t, mine, other, m_t, v_t, core):
    h = D // 2
    nj = h // TM

    def body(core_ref, w_ref, a_ref, b_ref, m_ref, v_ref, g_ref, d_ref, nm_ref, nv_ref):
        g = jnp.where(pl.program_id(0) == core_ref[0], a_ref[...], b_ref[...]).T[0:W_IN_SHARD, :]
        g_ref[:, 0, :] = g
        d_ref[:, 0, :], nm_ref[:, 0, :], nv_ref[:, 0, :] = _adam_math(w_ref[:, 0, :], g, m_ref[:, 0, :], v_ref[:, 0, :])

    blk = pl.BlockSpec((W_IN_SHARD, 1, TM), lambda i, j, core: (0, 0, i * nj + j))
    gblk = pl.BlockSpec((TM, W_IN_SHARD_PAD), lambda i, j, core: (j, 0))
    return pl.pallas_call(
        body, name="adamw_w_in", out_shape=tuple([jax.ShapeDtypeStruct((W_IN_SHARD, 1, D), f32)] * 4),
        grid_spec=pltpu.PrefetchScalarGridSpec(
            num_scalar_prefetch=1, grid=(2, nj), in_specs=[blk, gblk, gblk, blk, blk], out_specs=(blk,) * 4),
        compiler_params=_cp("arbitrary", "arbitrary"),
    )(_scalar(core), w_t, mine, other, m_t, v_t)


ANY = pl.BlockSpec(memory_space=pl.ANY)


def _place():
    x, y, c = lax.axis_index("x"), lax.axis_index("y"), lax.axis_index("c")
    return x, y, c, [(1 - x, y), (x, 1 - y), (1 - x, 1 - y)]


_GATHER_SEMS = [pltpu.SemaphoreType.DMA((7,)), pltpu.SemaphoreType.DMA((7,)), pltpu.SemaphoreType.DMA]


def _gather_rows_steps(x_ref, out_ref, send_sems, recv_sems, local_sem, after_first=None):
    m_per = x_ref.shape[0]
    x, y, c, chips = _place()
    me, sibling = (x, y, c), (x, y, 1 - c)

    def rows(px, py, pc):
        return out_ref.at[pl.ds((4 * px + 2 * py + pc) * m_per, m_per), :]

    def copy(k, blk, to, src=None):
        return pltpu.make_async_remote_copy(
            src_ref=rows(*blk) if src is None else src, dst_ref=rows(*blk), send_sem=send_sems.at[k],
            recv_sem=recv_sems.at[k], device_id=to, device_id_type=MESH)

    mine = pltpu.make_async_copy(x_ref, rows(*me), local_sem)
    mine.start()
    first = [copy(0, me, sibling, src=x_ref)]
    first += [copy(1 + j, me, (*chip, c), src=x_ref) for j, chip in enumerate(chips)]
    for cp in first:
        cp.start()
    if after_first is not None:
        after_first()
    passed = [copy(4 + j, (*chip, c), sibling) for j, chip in enumerate(chips)]
    for j, chip in enumerate(chips):
        copy(1 + j, (*chip, c), me).wait_recv()
        passed[j].start()
    copy(0, sibling, me).wait_recv()
    for j, chip in enumerate(chips):
        copy(4 + j, (*chip, 1 - c), me).wait_recv()
    for cp in first + passed:
        cp.wait_send()
    mine.wait()


def _gather_rows(block):
    m_per, n = block.shape

    def body(x_ref, out_ref, send_sems, recv_sems, local_sem):
        _gather_rows_steps(x_ref, out_ref, send_sems, recv_sems, local_sem)

    vmem = pl.BlockSpec(memory_space=pltpu.VMEM)
    return pl.pallas_call(
        body, name=f"gather_rows_{m_per}x{n}", out_shape=jax.ShapeDtypeStruct((8 * m_per, n), block.dtype),
        in_specs=[vmem], out_specs=vmem, scratch_shapes=list(_GATHER_SEMS), compiler_params=_VM)(block)


def _front(block, ada_w, slot, late):
    half = slot.shape[2] // 2
    riders = [_GatherRider([slot], 0, half), _GatherRider([slot], half, half)]
    n_mod = ada_w.shape[1]
    n_late = len(late)

    def body(x_ref, w_ref, *refs):
        cx, cy, _, _ = _place()
        late_refs, slot_ref = refs[:n_late], refs[n_late]
        out_ref, mod_ref, slot_out = refs[n_late + 1:n_late + 4]
        late_slots = refs[n_late + 4:2 * n_late + 4]
        c_scr, mine_scr = refs[2 * n_late + 4:2 * n_late + 6]
        bufs = refs[2 * n_late + 6:3 * n_late + 6]
        sems = refs[3 * n_late + 6:]
        r_scr, cast_sems = [sems[6:8], sems[8:10]], sems[10]
        casts = [pltpu.make_async_copy(bufs[k], late_slots[k].at[2 * cx + cy], cast_sems.at[k]) for k in range(n_late)]

        def first_half():
            riders[0].start([slot_ref], [slot_out], r_scr[0])
            for k in range(n_late):
                h = bufs[k].shape[1]
                bufs[k][0] = late_refs[k][0:h, :].astype(MX)
                bufs[k][1] = late_refs[k][h:2 * h, :].astype(MX)
                casts[k].start()

        _gather_rows_steps(x_ref, out_ref, *sems[0:3], after_first=first_half)
        r, o = FRONT["c"]
        for d in range(8):
            c_scr[d:d + 1, :] = out_ref[8 * d + r:8 * d + r + 1, o:o + D]
        mine_scr[...] = jnp.dot(_silu(c_scr[...]).astype(MX), w_ref[...].astype(MX), preferred_element_type=f32)
        _gather_rows_steps(mine_scr, mod_ref, *sems[3:6], after_first=lambda: riders[1].start([slot_ref], [slot_out], r_scr[1]))
        for rider, scr in zip(riders, r_scr):
            rider.finish([slot_ref], [slot_out], scr)
        for cp in casts:
            cp.wait()

    vmem = pl.BlockSpec(memory_space=pltpu.VMEM)
    late_shapes = [(4, 2, a.shape[0] // 2, a.shape[1]) for a in late]
    outs = pl.pallas_call(
        body, name="front",
        out_shape=(jax.ShapeDtypeStruct((64, block.shape[1]), f32), jax.ShapeDtypeStruct((64, n_mod), f32),
                   jax.ShapeDtypeStruct(slot.shape, slot.dtype)) + tuple(jax.ShapeDtypeStruct(s, MX) for s in late_shapes),
        in_specs=[vmem, vmem] + [vmem] * n_late + [ANY], out_specs=(vmem, vmem, ANY) + (ANY,) * n_late,
        scratch_shapes=[pltpu.VMEM((8, D), f32), pltpu.VMEM((8, n_mod), f32)] + [pltpu.VMEM(s[1:], MX) for s in late_shapes]
        + list(_GATHER_SEMS) * 2 + riders[0].scratch + riders[1].scratch + [pltpu.SemaphoreType.DMA((n_late,))],
        input_output_aliases={2 + n_late: 2}, compiler_params=_VM,
    )(block, ada_w, *late, slot)
    return outs[0], outs[1], outs[2], tuple(outs[3:])


class _GatherRider:
    def __init__(self, slots, row0=0, nrows=None):
        n = len(slots)
        self.n = n
        self.rows = (row0, slots[0].shape[2] - row0 if nrows is None else nrows)
        self.inputs = list(slots)
        self.out_shape = [jax.ShapeDtypeStruct(s.shape, s.dtype) for s in slots]
        self.scratch = [pltpu.SemaphoreType.DMA((n, 6)), pltpu.SemaphoreType.DMA((n, 6))]
        self.aliases = {a: a for a in range(n)}

    def _copy(self, outs, sems, a, j, k, half, to):
        dst = outs[a].at[k, half, pl.ds(*self.rows)]
        return pltpu.make_async_remote_copy(src_ref=dst, dst_ref=dst, send_sem=sems[0].at[a, j], recv_sem=sems[1].at[a, j],
                                            device_id=to, device_id_type=MESH)

    def _first(self, outs, sems):
        x, y, c, chips = _place()
        return [self._copy(outs, sems, a, j, 2 * x + y, c, (*chip, c)) for a in range(self.n) for j, chip in enumerate(chips)]

    def start(self, ins, outs, sems):
        for cp in self._first(outs, sems):
            cp.start()

    def finish(self, ins, outs, sems):
        x, y, c, chips = _place()
        passed = []
        for a in range(self.n):
            for j, (px, py) in enumerate(chips):
                self._copy(outs, sems, a, j, 2 * px + py, c, (x, y, c)).wait_recv()
                fwd = self._copy(outs, sems, a, 3 + j, 2 * px + py, c, (x, y, 1 - c))
                fwd.start()
                passed.append(fwd)
        for a in range(self.n):
            for j, (px, py) in enumerate(chips):
                self._copy(outs, sems, a, 3 + j, 2 * px + py, 1 - c, (x, y, c)).wait_recv()
        for cp in self._first(outs, sems) + passed:
            cp.wait_send()


class _ScatterRider:
    def __init__(self, parts):
        n = len(parts)
        self.n = n
        self.inputs = list(parts)
        self.out_shape = [jax.ShapeDtypeStruct((3,) + p.shape[1:], p.dtype) for p in parts]
        self.scratch = [pltpu.SemaphoreType.DMA((3 * n,)), pltpu.SemaphoreType.DMA((3 * n,))]
        self.aliases = {}

    def _copies(self, ins, outs, sems):
        x, y, c, chips = _place()
        return [pltpu.make_async_remote_copy(
            src_ref=ins[a].at[2 * px + py], dst_ref=outs[a].at[j], send_sem=sems[0].at[3 * a + j],
            recv_sem=sems[1].at[3 * a + j], device_id=(px, py, c), device_id_type=MESH)
            for a in range(self.n) for j, (px, py) in enumerate(chips)]

    def start(self, ins, outs, sems):
        for cp in self._copies(ins, outs, sems):
            cp.start()

    def finish(self, ins, outs, sems):
        for cp in self._copies(ins, outs, sems):
            cp.wait()


HBM = pl.BlockSpec(memory_space=pltpu.HBM)
SEM = pl.BlockSpec(memory_space=pltpu.SEMAPHORE)
EFFECT = pltpu.SideEffectType.DATAFLOW_SIDE_EFFECTING


def _split_start(rider, name, after=None):
    ni, no, ns = len(rider.inputs), len(rider.out_shape), len(rider.scratch)
    extra = [] if after is None else [after]

    def body(*refs):
        ins, lands = refs[:ni], refs[ni:ni + no]
        sems = refs[ni + no + len(extra):ni + no + len(extra) + ns]
        rider.start(ins, lands, sems)
        refs[-1][...] = jnp.zeros_like(refs[-1])

    bufs = list(rider.inputs) + [lax.empty(s.shape, s.dtype) for s in rider.out_shape]
    outs = pl.pallas_call(
        body, name=name,
        out_shape=tuple(rider.scratch) + tuple(pltpu.HBM(b.shape, b.dtype) for b in bufs) + (jax.ShapeDtypeStruct((8, LANES), f32),),
        in_specs=[HBM] * (ni + no) + [ANY] * len(extra),
        out_specs=(SEM,) * ns + (HBM,) * (ni + no) + (pl.BlockSpec(memory_space=pltpu.VMEM),),
        input_output_aliases={i: ns + i for i in range(ni + no)},
        compiler_params=pltpu.CompilerParams(has_side_effects=EFFECT),
    )(*[pltpu.with_memory_space_constraint(b, pltpu.HBM) for b in bufs], *extra)
    return outs[:-1], outs[-1]


def _split_wait(rider, name, handles, after):
    ni, no, ns = len(rider.inputs), len(rider.out_shape), len(rider.scratch)
    sems, bufs = handles[:ns], handles[ns:]

    def body(*refs):
        rider.finish(refs[:ni], refs[ni:ni + no], refs[ni + no:ni + no + ns])

    outs = pl.pallas_call(
        body, name=name, out_shape=tuple(pltpu.HBM(b.shape, b.dtype) for b in bufs),
        in_specs=[HBM] * (ni + no) + [SEM] * ns + [ANY], out_specs=(HBM,) * (ni + no),
        input_output_aliases={i: i for i in range(ni + no)}, compiler_params=pltpu.CompilerParams(has_side_effects=EFFECT),
    )(*bufs, *sems, after)
    return outs[:ni], outs[ni:]


def _ride_alone(rider, name):
    n = len(rider.inputs)

    def body(*refs):
        ins, outs, sems = refs[:n], refs[n:n + len(rider.out_shape)], refs[n + len(rider.out_shape):]
        rider.start(ins, outs, sems)
        rider.finish(ins, outs, sems)

    return pl.pallas_call(
        body, name=name, out_shape=tuple(rider.out_shape), in_specs=[ANY] * n, out_specs=tuple([ANY] * len(rider.out_shape)),
        input_output_aliases=dict(rider.aliases), scratch_shapes=list(rider.scratch),
    )(*rider.inputs)


class _SwapRider:
    def __init__(self, grads):
        n = len(grads)
        self.n = n
        self.inputs = list(grads)
        self.out_shape = [jax.ShapeDtypeStruct((4,) + g.shape[2:], g.dtype) for g in grads]
        self.scratch = [pltpu.SemaphoreType.DMA((4 * n,)), pltpu.SemaphoreType.DMA((4 * n,))]
        self.aliases = {}

    def _copies(self, ins, outs, sems):
        x, y, c, _ = _place()
        return [pltpu.make_async_remote_copy(
            src_ref=ins[a].at[k, 1 - c], dst_ref=outs[a].at[k], send_sem=sems[0].at[4 * a + k], recv_sem=sems[1].at[4 * a + k],
            device_id=(x, y, 1 - c), device_id_type=MESH) for a in range(self.n) for k in range(4)]

    def start(self, ins, outs, sems):
        for cp in self._copies(ins, outs, sems):
            cp.start()

    def finish(self, ins, outs, sems):
        for cp in self._copies(ins, outs, sems):
            cp.wait()


class _Riders:
    def __init__(self, riders):
        self.riders = list(riders)
        self.inputs = [a for r in riders for a in r.inputs]
        self.out_shape = [s for r in riders for s in r.out_shape]
        self.scratch = [s for r in riders for s in r.scratch]
        self.aliases = {}
        i = o = 0
        for r in riders:
            self.aliases.update({i + a: o + b for a, b in r.aliases.items()})
            i, o = i + len(r.inputs), o + len(r.out_shape)

    def _each(self, ins, outs, sems):
        i = o = s = 0
        for r in self.riders:
            yield r, ins[i:i + len(r.inputs)], outs[o:o + len(r.out_shape)], sems[s:s + len(r.scratch)]
            i, o, s = i + len(r.inputs), o + len(r.out_shape), s + len(r.scratch)

    def start(self, ins, outs, sems):
        for r, a, b, c in self._each(ins, outs, sems):
            r.start(a, b, c)

    def finish(self, ins, outs, sems):
        for r, a, b, c in self._each(ins, outs, sems):
            r.finish(a, b, c)

    def split(self, outs):
        res, o = [], 0
        for r in self.riders:
            res.append(outs[o:o + len(r.out_shape)])
            o += len(r.out_shape)
        return res


class _Reducer:
    def __init__(self, chip, core):
        self.chip, self.core, self.grads, self.parts, self.sums, self.others = chip, core, {}, {}, {}, {}

    def swap(self, name, grad):
        self.grads[name] = grad
        return _SwapRider([grad])

    def swapped(self, name, got):
        self.parts[name] = _add_pair(self.grads[name], got[0], self.core, name)

    def scatter(self, name):
        return _ScatterRider([self.parts[name]])

    def scattered(self, name, others):
        self.sums[name] = _add_chips(self.parts[name], others[0], self.chip, name)


class _SwapSumsRider:
    def __init__(self, halves):
        n = len(halves)
        self.n = n
        self.inputs = list(halves)
        self.out_shape = [jax.ShapeDtypeStruct(s.shape, s.dtype) for s in halves]
        self.scratch = [pltpu.SemaphoreType.DMA((n,)), pltpu.SemaphoreType.DMA((n,))]
        self.aliases = {}

    def _copies(self, ins, outs, sems):
        x, y, c, _ = _place()
        return [pltpu.make_async_remote_copy(
            src_ref=ins[a], dst_ref=outs[a], send_sem=sems[0].at[a], recv_sem=sems[1].at[a],
            device_id=(x, y, 1 - c), device_id_type=MESH) for a in range(self.n)]

    def start(self, ins, outs, sems):
        for cp in self._copies(ins, outs, sems):
            cp.start()

    def finish(self, ins, outs, sems):
        for cp in self._copies(ins, outs, sems):
            cp.wait()


def _row_tile(r):
    for tm in (TM, 176, 128, 64, 32, 16, 8):
        if r % tm == 0:
            return tm
    return r


def _add_pair(mine, got, core, name):
    k, _, h, c = mine.shape
    tm = _row_tile(h)

    def body(core_ref, a_ref, b_ref, o_ref):
        o_ref[0] = (a_ref[0, 0].astype(f32) + b_ref[0].astype(f32)).astype(MX)

    blk = pl.BlockSpec((1, tm, c), lambda i, j, core: (i, j, 0))
    return pl.pallas_call(
        body, name="add_pair_" + name, out_shape=jax.ShapeDtypeStruct((k, h, c), MX),
        grid_spec=pltpu.PrefetchScalarGridSpec(
            num_scalar_prefetch=1, grid=(k, h // tm),
            in_specs=[pl.BlockSpec((1, 1, tm, c), lambda i, j, core: (i, core[0], j, 0)), blk], out_specs=blk),
        compiler_params=_cp("arbitrary", "arbitrary"),
    )(_scalar(core), mine, got)


def _add_chips(parts, others, chip, name):
    _, n, c = others.shape
    tm = _row_tile(n)

    def body(chip_ref, a_ref, b_ref, o_ref):
        s = a_ref[0].astype(f32) + b_ref[0].astype(f32)
        o_ref[...] = (s + b_ref[1].astype(f32)) + b_ref[2].astype(f32)

    return pl.pallas_call(
        body, name="add_chips_" + name, out_shape=jax.ShapeDtypeStruct((n, c), f32),
        grid_spec=pltpu.PrefetchScalarGridSpec(
            num_scalar_prefetch=1, grid=(n // tm,),
            in_specs=[pl.BlockSpec((1, tm, c), lambda i, chip: (chip[0], i, 0)),
                      pl.BlockSpec((3, tm, c), lambda i, chip: (0, i, 0))],
            out_specs=pl.BlockSpec((tm, c), lambda i, chip: (i, 0))),
        compiler_params=_cp("arbitrary"),
    )(_scalar(chip), parts, others)


def _adam_math(w, g, m, v):
    m = ADAM_B1 * m + (1.0 - ADAM_B1) * g
    v = ADAM_B2 * v + (1.0 - ADAM_B2) * (g * g)
    m_hat = m / (1.0 - ADAM_B1 ** ADAM_STEP)
    v_hat = v / (1.0 - ADAM_B2 ** ADAM_STEP)
    return -ADAM_LR * (m_hat / (jnp.sqrt(v_hat) + ADAM_EPS) + ADAM_WD * w), m, v


def _adamw_halves(w, mine, other, m, v, core, name, after):
    r, c = w.shape
    h = r // 2
    tm = _row_tile(h)
    nj = h // tm
    cg = mine.shape[1]

    def body(core_ref, w_ref, a_ref, b_ref, m_ref, v_ref, after_ref, g_ref, d_ref, nm_ref, nv_ref):
        g = jnp.where(pl.program_id(0) == core_ref[0], a_ref[:, 0:c], b_ref[:, 0:c])
        g_ref[...] = g
        d_ref[...], nm_ref[...], nv_ref[...] = _adam_math(w_ref[...], g, m_ref[...], v_ref[...])

    blk = pl.BlockSpec((tm, c), lambda i, j, core: (i * nj + j, 0))
    gblk = pl.BlockSpec((tm, cg), lambda i, j, core: (j, 0))
    return _call(body, name=name, grid=(2, nj), out_shape=[jax.ShapeDtypeStruct((r, c), f32)] * 4,
                 in_specs=[blk, gblk, gblk, blk, blk, ANY], out_specs=(blk,) * 4, sem=("arbitrary", "arbitrary"),
                 prefetch=(_scalar(core),), args=(w, mine, other, m, v, after))[0]


def _ada_adamw(c_all_t, d_mod, w, m, v, after):
    r, c = w.shape
    tm = TM

    def body(ct_ref, dm_ref, w_ref, m_ref, v_ref, after_ref, g_ref, d_ref, nm_ref, nv_ref):
        ca = _silu(ct_ref[...])
        g = ca[:, 0:1] * dm_ref[0:1, :]
        for b in range(1, 8):
            g = g + ca[:, b:b + 1] * dm_ref[b:b + 1, :]
        g_ref[...] = g
        d_ref[...], nm_ref[...], nv_ref[...] = _adam_math(w_ref[...], g, m_ref[...], v_ref[...])

    blk = pl.BlockSpec((tm, c), lambda i: (i, 0))
    return _call(body, name="ada_adamw", grid=(r // tm,), out_shape=[jax.ShapeDtypeStruct((r, c), f32)] * 4,
                 in_specs=[pl.BlockSpec((tm, 8), lambda i: (i, 0)), pl.BlockSpec((8, c), lambda i: (0, 0)), blk, blk, blk, ANY],
                 out_specs=(blk,) * 4, sem=("arbitrary",), args=(c_all_t, d_mod, w, m, v, after))[0]


WEIGHTS = ("ada_w", "ada_b", "norm1_w", "w_in", "ssd_conv_w", "ssd_conv_b", "dt_bias", "a_log", "d_skip", "ssd_norm_w",
           "conf_conv_w", "conf_conv_b", "conf_ln_w", "conf_ln_b", "w_out", "norm2_w", "w_up", "ffn_conv_w", "ffn_conv_b",
           "w_down", "final_norm_w")


def kernel(x, c, ada_w, ada_b, norm1_w, w_in, ssd_conv_w, ssd_conv_b, dt_bias, a_log, d_skip, ssd_norm_w, conf_conv_w, conf_conv_b, conf_ln_w, conf_ln_b, w_out, norm2_w, w_up, ffn_conv_w, ffn_conv_b, w_down, final_norm_w, loss_target, m_ada_w, m_ada_b, m_norm1_w, m_w_in, m_ssd_conv_w, m_ssd_conv_b, m_dt_bias, m_a_log, m_d_skip, m_ssd_norm_w, m_conf_conv_w, m_conf_conv_b, m_conf_ln_w, m_conf_ln_b, m_w_out, m_norm2_w, m_w_up, m_ffn_conv_w, m_ffn_conv_b, m_w_down, m_final_norm_w, v_ada_w, v_ada_b, v_norm1_w, v_w_in, v_ssd_conv_w, v_ssd_conv_b, v_dt_bias, v_a_log, v_d_skip, v_ssd_norm_w, v_conf_conv_w, v_conf_conv_b, v_conf_ln_w, v_conf_ln_b, v_w_out, v_norm2_w, v_w_up, v_ffn_conv_w, v_ffn_conv_b, v_w_down, v_final_norm_w):
    given = dict(locals())
    w = {n: given[n] for n in WEIGHTS}
    mom = {n: given["m_" + n] for n in WEIGHTS}
    var = {n: given["v_" + n] for n in WEIGHTS}
    chip = 2 * lax.axis_index("x") + lax.axis_index("y")
    me = 2 * chip + lax.axis_index("c")

    core = lax.axis_index("c")
    got, mod_cols, a_in, late = _front(_pack_front(c, [w[n] for n in CONVS]), ada_w[0],
                                       _cast_into_slot_w_in(_columns_first(w_in), chip), [w_out[0], w_up[0], w_down[0]])
    c_all, *convs = _unpack_front(got)
    conv_full = dict(zip(CONVS, convs))
    mod_cols = mod_cols.reshape(8, 8, -1)[0::2]
    mod = lax.dynamic_index_in_dim(mod_cols, me, axis=1, keepdims=False).reshape(1, 6 * D) + ada_b
    w_pack = _pack_w_in(a_in.reshape(4, D, W_IN_SHARD_PAD))

    flat = lambda a: a.reshape(1, -1) if a.ndim == 1 else a
    small = {n: flat(w[n]) for n in VECTORS if n != "ada_b"}
    small.update(conv_full)
    reducer = _Reducer(chip, core)
    _, grad_x, _, gsmall = _local_step(x[0], mod, loss_target[0], w_pack, late, small, reducer)
    grads, delta, new_m, new_v = {}, {}, {}, {}

    names = VECTORS + tuple(CONVS)
    d_mod_mine, loss, res = _small_adamw(_gather_rows(gsmall), chip, *[{n: flat(d[n]) for n in names} for d in (w, mom, var)])
    for n in names:
        grads[n], delta[n], new_m[n], new_v[n] = [r.reshape(w[n].shape) for r in res[n]]

    scatter = reducer.scatter("w_in")
    handles, token = _split_start(scatter, "scatter_start_w_in", after=d_mod_mine)
    for n in ("w_up", "w_down", "w_out"):
        res = _adamw_halves(w[n][0], reducer.sums[n], reducer.others[n], mom[n][0], var[n][0], core, "adamw_" + n, token)
        grads[n], delta[n], new_m[n], new_v[n] = [r[None] for r in res]
    res = _ada_adamw(c_all.T, d_mod_mine, ada_w[0], m_ada_w[0], v_ada_w[0], token)
    grads["ada_w"], delta["ada_w"], new_m["ada_w"], new_v["ada_w"] = [r[None] for r in res]
    (reducer.parts["w_in"],), others = _split_wait(scatter, "scatter_wait_w_in", handles, res[1])
    reducer.scattered("w_in", others)
    reducer.others["w_in"], = _ride_alone(_SwapSumsRider([reducer.sums["w_in"]]), "swap_sums_w_in")
    res = _adamw_w_in(_columns_first(w_in), reducer.sums["w_in"], reducer.others["w_in"], _columns_first(m_w_in),
                      _columns_first(v_w_in), core)
    grads["w_in"], delta["w_in"], new_m["w_in"], new_v["w_in"] = [jnp.transpose(r, (1, 2, 0)) for r in res]

    return (loss, grad_x[None], *[grads[n] for n in WEIGHTS], *[delta[n] for n in WEIGHTS],
            *[new_m[n] for n in WEIGHTS], *[new_v[n] for n in WEIGHTS])
```

```python
import functools

import jax
import jax.numpy as jnp
from jax import lax
from jax.experimental import pallas as pl
from jax.experimental.pallas import tpu as pltpu

f32 = jnp.float32
MX = jnp.bfloat16

D = 1024
HEADS = 16
HEAD_P = 64
STATE_N = 128
D_XBC = 1536
D_FF = 2816
UP_SHARD = 2 * D_FF // 4
UP_EARLY_ROWS = 128
K_SSD, K_CONF, K_FFN = 4, 31, 3
CHUNK = 128
OFF_Z, OFF_XBC, OFF_CA, OFF_CG, OFF_DT = 0, 1024, 2560, 3584, 4608
W_PACK = 4736
TM = 256
CW = 256
RC = 64
LANES = 128
VMEM_LIMIT = 56 * 1024 * 1024

ADAM_LR, ADAM_B1, ADAM_B2, ADAM_EPS, ADAM_WD, ADAM_STEP = 0.001, 0.9, 0.999, 1e-08, 0.01, 10

MESH = pl.DeviceIdType.MESH


def _cp(*sem):
    return pltpu.CompilerParams(dimension_semantics=sem, vmem_limit_bytes=VMEM_LIMIT)


def _resident(shape):
    nd = len(shape)
    return pl.BlockSpec(shape, lambda *_: (0,) * nd, pipeline_mode=pl.Buffered(1))


def _row(width=D):
    return pl.BlockSpec((1, width), lambda *_: (0, 0))


def _call(body, *, name, grid, in_specs, out_specs, out_shape, args, sem, scratch_shapes=(), prefetch=(), rider=None):
    ni, no, ns, npf = len(in_specs), len(out_specs), len(scratch_shapes), len(prefetch)
    ri, ro = (len(rider.inputs), len(rider.out_shape)) if rider is not None else (0, 0)

    def full(*refs):
        pre, refs = refs[:npf], refs[npf:]
        base_in, r_in = refs[:ni], refs[ni:ni + ri]
        base_out, r_out = refs[ni + ri:ni + ri + no], refs[ni + ri + no:ni + ri + no + ro]
        base_scr, r_scr = refs[ni + ri + no + ro:ni + ri + no + ro + ns], refs[ni + ri + no + ro + ns:]
        if rider is None:
            return body(*pre, *base_in, *base_out, *base_scr)
        ids = [pl.program_id(a) for a in range(len(grid))]
        first = functools.reduce(jnp.logical_and, [i == 0 for i in ids])
        last = functools.reduce(jnp.logical_and, [i == g - 1 for i, g in zip(ids, grid)])

        @pl.when(first)
        def _():
            rider.start(r_in, r_out, r_scr)

        if hasattr(rider, "forward") and functools.reduce(lambda a, b: a * b, grid) > 1:
            @pl.when(last)
            def _():
                rider.forward(r_in, r_out, r_scr)

        body(*pre, *base_in, *base_out, *base_scr)

        @pl.when(last)
        def _():
            rider.finish(r_in, r_out, r_scr)

    extra = dict(shapes=[], scratch=[], aliases={}, inputs=[]) if rider is None else dict(
        shapes=rider.out_shape, scratch=rider.scratch, inputs=rider.inputs,
        aliases={npf + ni + i: no + j for i, j in rider.aliases.items()})
    outs = pl.pallas_call(
        full, name=name, out_shape=tuple(out_shape) + tuple(extra["shapes"]), input_output_aliases=extra["aliases"],
        grid_spec=pltpu.PrefetchScalarGridSpec(
            num_scalar_prefetch=npf, grid=grid, in_specs=list(in_specs) + [ANY] * ri,
            out_specs=tuple(out_specs) + (ANY,) * ro, scratch_shapes=list(scratch_shapes) + list(extra["scratch"])),
        compiler_params=_cp(*sem),
    )(*prefetch, *args, *extra["inputs"])
    return tuple(outs[:no]), tuple(outs[no:])


def _silu(v):
    return v * jax.nn.sigmoid(v)


def _dsilu(v):
    s = jax.nn.sigmoid(v)
    return s * (1.0 + v * (1.0 - s))


def _softplus(v):
    return jnp.maximum(v, 0.0) + jnp.log1p(jnp.exp(-jnp.abs(v)))


def _mm(a, b):
    return jnp.dot(a.astype(MX), b.astype(MX), preferred_element_type=f32)


def _mm_nt(a, b):
    return lax.dot_general(a.astype(MX), b.astype(MX), (((1,), (1,)), ((), ())), preferred_element_type=f32)


def _mm_tn(a, b):
    return lax.dot_general(a.astype(MX), b.astype(MX), (((0,), (0,)), ((), ())), preferred_element_type=f32)


def _ln_inproj(x, mod, norm1_w, w_pack, rider=None):
    t = x.shape[0]

    def body(x_ref, mod_ref, nw_ref, w_ref, proj_ref, ht_ref):
        xv = x_ref[...]
        rstd = lax.rsqrt(jnp.mean(xv * xv, axis=-1, keepdims=True) + 1e-6)
        h = (xv * rstd * nw_ref[...]) * (1.0 + mod_ref[:, D:2 * D]) + mod_ref[:, 0:D]
        hb = h.astype(MX)
        ht_ref[...] = hb.T
        proj_ref[...] = jnp.dot(hb, w_ref[...], preferred_element_type=f32)

    return _call(
        body, name="ln_inproj", grid=(t // TM,),
        out_shape=(jax.ShapeDtypeStruct((t, W_PACK), f32), jax.ShapeDtypeStruct((D, t), MX)),
        in_specs=[pl.BlockSpec((TM, D), lambda i: (i, 0)), _row(6 * D), _row(), _resident((D, W_PACK))],
        out_specs=(pl.BlockSpec((TM, W_PACK), lambda i: (i, 0)), pl.BlockSpec((D, TM), lambda i: (0, i))),
        sem=("arbitrary",), args=(x, mod, norm1_w, w_pack), rider=rider)


def _ssd_gate_norm(y_scan, xbc_act, proj, d_skip_row, ssd_norm_w):
    t = y_scan.shape[0]

    def body(y_ref, xs_ref, z_ref, dsk_ref, nw_ref, o_ref):
        y = y_ref[...] + xs_ref[...] * dsk_ref[...]
        yz = y * _silu(z_ref[...])
        rstd = lax.rsqrt(jnp.mean(yz * yz, axis=-1, keepdims=True) + 1e-6)
        o_ref[...] = (yz * rstd * nw_ref[...]).astype(MX)

    blk = pl.BlockSpec((TM, D), lambda i: (i, 0))
    return pl.pallas_call(
        body, name="ssd_gate_norm", grid=(t // TM,), out_shape=jax.ShapeDtypeStruct((t, D), MX),
        in_specs=[blk, blk, blk, _row(), _row()], out_specs=blk, compiler_params=_cp("arbitrary"),
    )(y_scan, xbc_act, proj, d_skip_row, ssd_norm_w)


def _ln_silu(u_conv, ln_w, ln_b):
    t = u_conv.shape[0]

    def body(u_ref, w_ref, b_ref, o_ref):
        u = u_ref[...]
        mu = jnp.mean(u, axis=-1, keepdims=True)
        uc = u - mu
        rstd = lax.rsqrt(jnp.mean(uc * uc, axis=-1, keepdims=True) + 1e-5)
        o_ref[...] = _silu(uc * rstd * w_ref[...] + b_ref[...]).astype(MX)

    blk = pl.BlockSpec((TM, D), lambda i: (i, 0))
    return pl.pallas_call(
        body, name="ln_silu", grid=(t // TM,), out_shape=jax.ShapeDtypeStruct((t, D), MX),
        in_specs=[blk, _row(), _row()], out_specs=blk, compiler_params=_cp("arbitrary"),
    )(u_conv, ln_w, ln_b)


def _outproj_ln2_up(y_ssd, u, w_out, x, mod, norm2_w, w_up):
    t = x.shape[0]

    def body(y_ref, u_ref, wo_ref, x_ref, mod_ref, nw_ref, wu_ref, mix_ref, x1_ref, h2t_ref, up_ref):
        mix = jnp.dot(y_ref[...], wo_ref[0:D, :], preferred_element_type=f32)
        mix = mix + jnp.dot(u_ref[...], wo_ref[D:2 * D, :], preferred_element_type=f32)
        mix_ref[...] = mix
        x1 = x_ref[...] + mod_ref[:, 2 * D:3 * D] * mix
        x1_ref[...] = x1
        rstd = lax.rsqrt(jnp.mean(x1 * x1, axis=-1, keepdims=True) + 1e-6)
        h2 = ((x1 * rstd * nw_ref[...]) * (1.0 + mod_ref[:, 4 * D:5 * D]) + mod_ref[:, 3 * D:4 * D]).astype(MX)
        h2t_ref[...] = h2.T
        for k in range(4):
            up_ref[:, k * UP_SHARD:(k + 1) * UP_SHARD] = jnp.dot(h2, wu_ref[k], preferred_element_type=f32)

    blk = pl.BlockSpec((TM, D), lambda i: (i, 0))
    return pl.pallas_call(
        body, name="outproj_ln2_up", grid=(t // TM,),
        out_shape=(jax.ShapeDtypeStruct((t, D), f32), jax.ShapeDtypeStruct((t, D), f32),
                   jax.ShapeDtypeStruct((D, t), MX), jax.ShapeDtypeStruct((t, 2 * D_FF), f32)),
        in_specs=[blk, blk, _resident((2 * D, D)), blk, _row(6 * D), _row(), _resident((4, D, UP_SHARD))],
        out_specs=(blk, blk, pl.BlockSpec((D, TM), lambda i: (0, i)), pl.BlockSpec((TM, 2 * D_FF), lambda i: (i, 0))),
        compiler_params=_cp("arbitrary"),
    )(y_ssd, u, w_out, x, mod, norm2_w, w_up)


def _down_loss(act, w_down, x1, mod, final_norm_w, target):
    t = x1.shape[0]

    def body(a_ref, wd_ref, x1_ref, mod_ref, wf_ref, tgt_ref, dx2_ref, dffn_ref, dact_ref, st_ref):
        @pl.when(pl.program_id(0) == 0)
        def _():
            st_ref[...] = jnp.zeros_like(st_ref)

        g2 = mod_ref[:, 5 * D:6 * D]
        ffn = jnp.dot(a_ref[...], wd_ref[...], preferred_element_type=f32)
        x2 = x1_ref[...] + g2 * ffn
        rstd = lax.rsqrt(jnp.mean(x2 * x2, axis=-1, keepdims=True) + 1e-6)
        xh = x2 * rstd
        wf = wf_ref[...]
        err = xh * wf - tgt_ref[...]
        dy = err * (1.0 / D)
        dxh = dy * wf
        dx2 = rstd * (dxh - xh * jnp.mean(dxh * xh, axis=-1, keepdims=True))
        dx2_ref[...] = dx2
        dffn = (g2 * dx2).astype(MX)
        dffn_ref[...] = dffn
        dact_ref[...] = lax.dot_general(dffn, wd_ref[...], (((1,), (1,)), ((), ())), preferred_element_type=f32)
        st_ref[0:1, :] += jnp.sum(dy * xh, axis=0, keepdims=True)
        st_ref[1:2, :] += jnp.sum(dx2 * ffn, axis=0, keepdims=True)
        st_ref[2:3, :] += jnp.sum(0.5 * jnp.mean(err * err, axis=-1, keepdims=True), axis=0, keepdims=True)

    blk = pl.BlockSpec((TM, D), lambda i: (i, 0))
    ablk = pl.BlockSpec((TM, D_FF), lambda i: (i, 0))
    return pl.pallas_call(
        body, name="down_loss", grid=(t // TM,),
        out_shape=(jax.ShapeDtypeStruct((t, D), f32), jax.ShapeDtypeStruct((t, D), MX),
                   jax.ShapeDtypeStruct((t, D_FF), f32), jax.ShapeDtypeStruct((8, D), f32)),
        in_specs=[ablk, _resident((D_FF, D)), blk, _row(6 * D), _row(), blk],
        out_specs=(blk, blk, ablk, pl.BlockSpec((8, D), lambda i: (0, 0))),
        compiler_params=_cp("arbitrary"),
    )(act, w_down, x1, mod, final_norm_w, target)


def _pad_of(k):
    return 8 * ((k - 1 + 7) // 8)


def _causal_win(ref, r, t, pad):
    base = pl.multiple_of(r * RC, RC)
    prev = ref[pl.ds(pl.multiple_of(jnp.maximum(base - pad, 0), 8), pad), :]
    prev = jnp.where(r > 0, prev, 0.0)
    return jnp.concatenate([prev, ref[pl.ds(base, RC), :]], axis=0)


def _anti_win(ref, r, t, pad):
    base = pl.multiple_of(r * RC, RC)
    nxt = ref[pl.ds(pl.multiple_of(jnp.minimum(base + RC, t - pad), 8), pad), :]
    nxt = jnp.where(r < t // RC - 1, nxt, 0.0)
    return jnp.concatenate([ref[pl.ds(base, RC), :], nxt], axis=0)


def _shifted(win, offsets):
    for r in range(8):
        mine = [o for o in offsets if o % 8 == r]
        if mine:
            rolled = win if r == 0 else pltpu.roll(win, win.shape[0] - r, 0)
            for o in mine:
                yield o, rolled[o - r:o - r + RC, :]


def _conv_taps(win, w_ref, k, pad):
    first = pad - (k - 1)
    acc = None
    for o, rows in _shifted(win, range(first, first + k)):
        term = w_ref[o - first:o - first + 1, :] * rows
        acc = term if acc is None else acc + term
    return acc


def _corr_taps(win, w_ref, k):
    acc = None
    for o, rows in _shifted(win, range(k)):
        term = w_ref[k - 1 - o:k - o, :] * rows
        acc = term if acc is None else acc + term
    return acc


def _dw_accumulate(dw_scr, d, win, k, pad):
    first = pad - (k - 1)
    for o, rows in _shifted(win, range(first, first + k)):
        j = o - first
        prod = d * rows
        dw_scr[8 * j:8 * j + 8, :] += prod.reshape(RC // 8, 8, prod.shape[-1]).sum(axis=0)


def _dw_finish(dw_scr, dw_ref, k):
    for j in range(k):
        dw_ref[j:j + 1, :] = jnp.sum(dw_scr[8 * j:8 * j + 8, :], axis=0, keepdims=True)


def _rows8(v):
    return v.reshape(RC // 8, 8, v.shape[-1]).sum(axis=0)


def _ssd_conv_fwd(proj, conv_w, conv_b, rider=None):
    t = proj.shape[0]
    pad = _pad_of(K_SSD)
    c0 = OFF_XBC // CW

    def body(x_ref, w_ref, b_ref, o_ref):
        def step(r, carry):
            win = _causal_win(x_ref, r, t, pad)
            o_ref[pl.ds(pl.multiple_of(r * RC, RC), RC), :] = _silu(_conv_taps(win, w_ref, K_SSD, pad) + b_ref[...])
            return carry
        lax.fori_loop(0, t // RC, step, 0)

    return _call(
        body, name="ssd_conv_fwd", grid=(D_XBC // CW,), out_shape=(jax.ShapeDtypeStruct((t, D_XBC), f32),),
        in_specs=[pl.BlockSpec((t, CW), lambda j: (0, c0 + j)), pl.BlockSpec((K_SSD, CW), lambda j: (0, j)),
                  pl.BlockSpec((1, CW), lambda j: (0, j))],
        out_specs=(pl.BlockSpec((t, CW), lambda j: (0, j)),), sem=("arbitrary",), args=(proj, conv_w, conv_b), rider=rider)


def _glu_conv_fwd(proj, conv_w, conv_b, rider=None):
    t = proj.shape[0]
    pad = _pad_of(K_CONF)
    ca, cg = OFF_CA // CW, OFF_CG // CW

    def body(a_ref, g_ref, w_ref, b_ref, o_ref, v_scr):
        def glu(r, carry):
            rows = pl.ds(pl.multiple_of(r * RC, RC), RC)
            v_scr[rows, :] = a_ref[rows, :] * jax.nn.sigmoid(g_ref[rows, :])
            return carry
        lax.fori_loop(0, t // RC, glu, 0)

        def step(r, carry):
            win = _causal_win(v_scr, r, t, pad)
            o_ref[pl.ds(pl.multiple_of(r * RC, RC), RC), :] = _conv_taps(win, w_ref, K_CONF, pad) + b_ref[...]
            return carry
        lax.fori_loop(0, t // RC, step, 0)

    return _call(
        body, name="glu_conv_fwd", grid=(D // CW,), out_shape=(jax.ShapeDtypeStruct((t, D), f32),),
        in_specs=[pl.BlockSpec((t, CW), lambda j: (0, ca + j)), pl.BlockSpec((t, CW), lambda j: (0, cg + j)),
                  pl.BlockSpec((K_CONF, CW), lambda j: (0, j)), pl.BlockSpec((1, CW), lambda j: (0, j))],
        out_specs=(pl.BlockSpec((t, CW), lambda j: (0, j)),),
        scratch_shapes=[pltpu.VMEM((t, CW), f32)], sem=("arbitrary",), args=(proj, proj, conv_w, conv_b), rider=rider)


def _ffn_conv_fwd(up, conv_w, conv_b, rider=None):
    t = up.shape[0]
    pad = _pad_of(K_FFN)
    nb = D_FF // CW

    def body(g_ref, v_ref, wg_ref, wv_ref, bg_ref, bv_ref, o_ref):
        def step(r, carry):
            gc = _conv_taps(_causal_win(g_ref, r, t, pad), wg_ref, K_FFN, pad) + bg_ref[...]
            vc = _conv_taps(_causal_win(v_ref, r, t, pad), wv_ref, K_FFN, pad) + bv_ref[...]
            o_ref[pl.ds(pl.multiple_of(r * RC, RC), RC), :] = (_silu(gc) * vc).astype(MX)
            return carry
        lax.fori_loop(0, t // RC, step, 0)

    return _call(
        body, name="ffn_conv_fwd", grid=(nb,), out_shape=(jax.ShapeDtypeStruct((t, D_FF), MX),),
        in_specs=[pl.BlockSpec((t, CW), lambda j: (0, j)), pl.BlockSpec((t, CW), lambda j: (0, nb + j)),
                  pl.BlockSpec((K_FFN, CW), lambda j: (0, j)), pl.BlockSpec((K_FFN, CW), lambda j: (0, nb + j)),
                  pl.BlockSpec((1, CW), lambda j: (0, j)), pl.BlockSpec((1, CW), lambda j: (0, nb + j))],
        out_specs=(pl.BlockSpec((t, CW), lambda j: (0, j)),), sem=("arbitrary",),
        args=(up, up, conv_w, conv_w, conv_b, conv_b), rider=rider)


def _ffn_conv_bwd(up, conv_w, conv_b, d_act, rider=None):
    t = up.shape[0]
    pad = _pad_of(K_FFN)
    nb = D_FF // CW

    def body(g_ref, v_ref, wg_ref, wv_ref, bg_ref, bv_ref, da_ref, dup_ref, dw_ref, db_ref,
             dg_scr, dv_scr, dwg_scr, dwv_scr, db_scr):
        dwg_scr[...] = jnp.zeros_like(dwg_scr)
        dwv_scr[...] = jnp.zeros_like(dwv_scr)
        db_scr[...] = jnp.zeros_like(db_scr)

        def first(r, carry):
            rows = pl.ds(pl.multiple_of(r * RC, RC), RC)
            gwin = _causal_win(g_ref, r, t, pad)
            vwin = _causal_win(v_ref, r, t, pad)
            gc = _conv_taps(gwin, wg_ref, K_FFN, pad) + bg_ref[...]
            vc = _conv_taps(vwin, wv_ref, K_FFN, pad) + bv_ref[...]
            da = da_ref[rows, :]
            dgc = da * vc * _dsilu(gc)
            dvc = da * _silu(gc)
            dg_scr[rows, :] = dgc
            dv_scr[rows, :] = dvc
            _dw_accumulate(dwg_scr, dgc, gwin, K_FFN, pad)
            _dw_accumulate(dwv_scr, dvc, vwin, K_FFN, pad)
            db_scr[0:8, :] += _rows8(dgc)
            db_scr[8:16, :] += _rows8(dvc)
            return carry
        lax.fori_loop(0, t // RC, first, 0)

        def second(r, carry):
            rows = pl.ds(pl.multiple_of(r * RC, RC), RC)
            dup_ref[0, rows, :] = _corr_taps(_anti_win(dg_scr, r, t, pad), wg_ref, K_FFN).astype(MX)
            dup_ref[1, rows, :] = _corr_taps(_anti_win(dv_scr, r, t, pad), wv_ref, K_FFN).astype(MX)
            return carry
        lax.fori_loop(0, t // RC, second, 0)

        for j in range(K_FFN):
            dw_ref[0, j:j + 1, :] = jnp.sum(dwg_scr[8 * j:8 * j + 8, :], axis=0, keepdims=True)
            dw_ref[1, j:j + 1, :] = jnp.sum(dwv_scr[8 * j:8 * j + 8, :], axis=0, keepdims=True)
        db_ref[0] = jnp.sum(db_scr[0:8, :], axis=0, keepdims=True)
        db_ref[1] = jnp.sum(db_scr[8:16, :], axis=0, keepdims=True)

    return _call(
        body, name="ffn_conv_bwd", grid=(nb,),
        out_shape=(jax.ShapeDtypeStruct((2, t, D_FF), MX), jax.ShapeDtypeStruct((2, K_FFN, D_FF), f32),
                   jax.ShapeDtypeStruct((2, 1, D_FF), f32)),
        in_specs=[pl.BlockSpec((t, CW), lambda j: (0, j)), pl.BlockSpec((t, CW), lambda j: (0, nb + j)),
                  pl.BlockSpec((K_FFN, CW), lambda j: (0, j)), pl.BlockSpec((K_FFN, CW), lambda j: (0, nb + j)),
                  pl.BlockSpec((1, CW), lambda j: (0, j)), pl.BlockSpec((1, CW), lambda j: (0, nb + j)),
                  pl.BlockSpec((t, CW), lambda j: (0, j))],
        out_specs=(pl.BlockSpec((2, t, CW), lambda j: (0, 0, j)), pl.BlockSpec((2, K_FFN, CW), lambda j: (0, 0, j)),
                   pl.BlockSpec((2, 1, CW), lambda j: (0, 0, j))),
        scratch_shapes=[pltpu.VMEM((t, CW), f32), pltpu.VMEM((t, CW), f32), pltpu.VMEM((8 * K_FFN, CW), f32),
                        pltpu.VMEM((8 * K_FFN, CW), f32), pltpu.VMEM((16, CW), f32)],
        sem=("arbitrary",), args=(up, up, conv_w, conv_w, conv_b, conv_b, d_act), rider=rider)


def _glu_conv_bwd(proj, conv_w, d_uconv, rider=None):
    t = proj.shape[0]
    pad = _pad_of(K_CONF)
    ca, cg = OFF_CA // CW, OFF_CG // CW

    def body(a_ref, g_ref, w_ref, du_ref, dc_ref, dw_ref, db_ref, v_scr, dw_scr, db_scr):
        dw_scr[...] = jnp.zeros_like(dw_scr)
        db_scr[...] = jnp.zeros_like(db_scr)

        def glu(r, carry):
            rows = pl.ds(pl.multiple_of(r * RC, RC), RC)
            v_scr[rows, :] = a_ref[rows, :] * jax.nn.sigmoid(g_ref[rows, :])
            return carry
        lax.fori_loop(0, t // RC, glu, 0)

        def step(r, carry):
            rows = pl.ds(pl.multiple_of(r * RC, RC), RC)
            du = du_ref[rows, :]
            _dw_accumulate(dw_scr, du, _causal_win(v_scr, r, t, pad), K_CONF, pad)
            db_scr[...] += _rows8(du)
            dv = _corr_taps(_anti_win(du_ref, r, t, pad), w_ref, K_CONF)
            a = a_ref[rows, :]
            s = jax.nn.sigmoid(g_ref[rows, :])
            dc_ref[0, rows, :] = (dv * s).astype(MX)
            dc_ref[1, rows, :] = (dv * a * s * (1.0 - s)).astype(MX)
            return carry
        lax.fori_loop(0, t // RC, step, 0)
        _dw_finish(dw_scr, dw_ref, K_CONF)
        db_ref[...] = jnp.sum(db_scr[...], axis=0, keepdims=True)

    return _call(
        body, name="glu_conv_bwd", grid=(D // CW,),
        out_shape=(jax.ShapeDtypeStruct((2, t, D), MX), jax.ShapeDtypeStruct((K_CONF, D), f32),
                   jax.ShapeDtypeStruct((1, D), f32)),
        in_specs=[pl.BlockSpec((t, CW), lambda j: (0, ca + j)), pl.BlockSpec((t, CW), lambda j: (0, cg + j)),
                  pl.BlockSpec((K_CONF, CW), lambda j: (0, j)), pl.BlockSpec((t, CW), lambda j: (0, j))],
        out_specs=(pl.BlockSpec((2, t, CW), lambda j: (0, 0, j)), pl.BlockSpec((K_CONF, CW), lambda j: (0, j)),
                   pl.BlockSpec((1, CW), lambda j: (0, j))),
        scratch_shapes=[pltpu.VMEM((t, CW), f32), pltpu.VMEM((8 * K_CONF, CW), f32), pltpu.VMEM((8, CW), f32)],
        sem=("arbitrary",), args=(proj, proj, conv_w, d_uconv), rider=rider)


def _ssd_conv_bwd_x(proj, conv_w, conv_b, d_xs, d_y, d_skip_row):
    t = proj.shape[0]
    pad = _pad_of(K_SSD)
    c0 = OFF_XBC // CW

    def body(x_ref, w_ref, b_ref, dxs_ref, dy_ref, dsk_ref, draw_ref, dw_ref, db_ref, dp_scr, dw_scr, db_scr):
        dw_scr[...] = jnp.zeros_like(dw_scr)
        db_scr[...] = jnp.zeros_like(db_scr)

        def first(r, carry):
            rows = pl.ds(pl.multiple_of(r * RC, RC), RC)
            win = _causal_win(x_ref, r, t, pad)
            pre = _conv_taps(win, w_ref, K_SSD, pad) + b_ref[...]
            dpre = (dxs_ref[rows, :] + dy_ref[rows, :] * dsk_ref[...]) * _dsilu(pre)
            dp_scr[rows, :] = dpre
            _dw_accumulate(dw_scr, dpre, win, K_SSD, pad)
            db_scr[...] += _rows8(dpre)
            return carry
        lax.fori_loop(0, t // RC, first, 0)

        def second(r, carry):
            rows = pl.ds(pl.multiple_of(r * RC, RC), RC)
            draw_ref[rows, :] = _corr_taps(_anti_win(dp_scr, r, t, pad), w_ref, K_SSD).astype(MX)
            return carry
        lax.fori_loop(0, t // RC, second, 0)
        _dw_finish(dw_scr, dw_ref, K_SSD)
        db_ref[...] = jnp.sum(db_scr[...], axis=0, keepdims=True)

    cb = pl.BlockSpec((t, CW), lambda j: (0, j))
    return pl.pallas_call(
        body, name="ssd_conv_bwd_x", grid=(D // CW,),
        out_shape=(jax.ShapeDtypeStruct((t, D), MX), jax.ShapeDtypeStruct((K_SSD, D), f32),
                   jax.ShapeDtypeStruct((1, D), f32)),
        in_specs=[pl.BlockSpec((t, CW), lambda j: (0, c0 + j)), pl.BlockSpec((K_SSD, CW), lambda j: (0, j)),
                  pl.BlockSpec((1, CW), lambda j: (0, j)), cb, cb, pl.BlockSpec((1, CW), lambda j: (0, j))],
        out_specs=(cb, pl.BlockSpec((K_SSD, CW), lambda j: (0, j)), pl.BlockSpec((1, CW), lambda j: (0, j))),
        scratch_shapes=[pltpu.VMEM((t, CW), f32), pltpu.VMEM((8 * K_SSD, CW), f32), pltpu.VMEM((8, CW), f32)],
        compiler_params=_cp("arbitrary"),
    )(proj, conv_w, conv_b, d_xs, d_y, d_skip_row)


def _ssd_conv_bwd_bc(proj, conv_w, conv_b, d_bc):
    t = proj.shape[0]
    pad = _pad_of(K_SSD)
    c0 = (OFF_XBC + D) // CW
    w0 = D // CW

    def body(x_ref, w_ref, b_ref, dbc_ref, draw_ref, dw_ref, db_ref, dp_scr, dw_scr, db_scr):
        dw_scr[...] = jnp.zeros_like(dw_scr)
        db_scr[...] = jnp.zeros_like(db_scr)

        def first(r, carry):
            rows = pl.ds(pl.multiple_of(r * RC, RC), RC)
            win = _causal_win(x_ref, r, t, pad)
            pre = _conv_taps(win, w_ref, K_SSD, pad) + b_ref[...]
            dpre = dbc_ref[0, rows, :] * _dsilu(pre)
            dp_scr[rows, :] = dpre
            _dw_accumulate(dw_scr, dpre, win, K_SSD, pad)
            db_scr[...] += _rows8(dpre)
            return carry
        lax.fori_loop(0, t // RC, first, 0)

        def second(r, carry):
            rows = pl.ds(pl.multiple_of(r * RC, RC), RC)
            draw_ref[rows, :] = _corr_taps(_anti_win(dp_scr, r, t, pad), w_ref, K_SSD).astype(MX)
            return carry
        lax.fori_loop(0, t // RC, second, 0)
        _dw_finish(dw_scr, dw_ref, K_SSD)
        db_ref[...] = jnp.sum(db_scr[...], axis=0, keepdims=True)

    return pl.pallas_call(
        body, name="ssd_conv_bwd_bc", grid=(2,),
        out_shape=(jax.ShapeDtypeStruct((t, 2 * CW), MX), jax.ShapeDtypeStruct((K_SSD, 2 * CW), f32),
                   jax.ShapeDtypeStruct((1, 2 * CW), f32)),
        in_specs=[pl.BlockSpec((t, CW), lambda j: (0, c0 + j)), pl.BlockSpec((K_SSD, CW), lambda j: (0, w0 + j)),
                  pl.BlockSpec((1, CW), lambda j: (0, w0 + j)), pl.BlockSpec((1, t, CW), lambda j: (j, 0, 0))],
        out_specs=(pl.BlockSpec((t, CW), lambda j: (0, j)), pl.BlockSpec((K_SSD, CW), lambda j: (0, j)),
                   pl.BlockSpec((1, CW), lambda j: (0, j))),
        scratch_shapes=[pltpu.VMEM((t, CW), f32), pltpu.VMEM((8 * K_SSD, CW), f32), pltpu.VMEM((8, CW), f32)],
        compiler_params=_cp("arbitrary"),
    )(proj, conv_w, conv_b, d_bc)


def _chunk_masks():
    ii = lax.broadcasted_iota(jnp.int32, (CHUNK, CHUNK), 0)
    jj = lax.broadcasted_iota(jnp.int32, (CHUNK, CHUNK), 1)
    return ii == jj, jj <= ii, jj >= ii


def _to_row(col, eye):
    return jnp.sum(jnp.where(eye, col, 0.0), axis=0, keepdims=True)


def _to_col(row, eye):
    return jnp.sum(jnp.where(eye, row, 0.0), axis=1, keepdims=True)


def _head_decay(dt_h, a_h, eye, tril):
    a_row = _to_row(dt_h * a_h, eye)
    cs = jnp.sum(jnp.where(tril, a_row, 0.0), axis=1, keepdims=True)
    cs_row = _to_row(cs, eye)
    decay = jnp.where(tril, jnp.exp(jnp.where(tril, cs - cs_row, 0.0)), 0.0)
    total = jnp.sum(a_row, axis=1, keepdims=True)
    return cs, decay, total


SCAN_UNROLL = 4


def _unrolled_loop(n, step, init):
    unroll = min(SCAN_UNROLL, n)
    assert n % unroll == 0

    def trip(i, carry):
        for u in range(unroll):
            carry = step(unroll * i + u, carry)
        return carry
    return lax.fori_loop(0, n // unroll, trip, init)


def _lane_pick(mat, lane, which):
    return jnp.sum(jnp.where(lane == which, mat, 0.0), axis=1, keepdims=True)


def _ssd_fwd(xbc_act, proj, dt_bias_row, a_log_row, rider=None):
    t = xbc_act.shape[0]
    nc = t // CHUNK
    cb, cc, cdt = D // LANES, (D + 2 * STATE_N) // LANES, OFF_DT // LANES

    def body(x_ref, b_ref, c_ref, dt_ref, dtb_ref, alog_ref, y_ref, st_ref):
        j = pl.program_id(0)
        eye, tril, _ = _chunk_masks()
        lane = lax.broadcasted_iota(jnp.int32, (1, LANES), 1)
        first = lane < HEAD_P
        a_row = -jnp.exp(alog_ref[...])
        a_heads = [jnp.sum(jnp.where(lane == 2 * j + h, a_row, 0.0), axis=1, keepdims=True) for h in range(2)]

        def chunk(c, hprev):
            rows = pl.ds(pl.multiple_of(c * CHUNK, CHUNK), CHUNK)
            xv, bm, cm = x_ref[rows, :], b_ref[rows, :], c_ref[rows, :]
            dt = _softplus(dt_ref[rows, :] + dtb_ref[...])
            st_ref[c] = hprev
            g = _mm_nt(cm, bm)
            ch = _mm(cm, hprev)
            dts = [_lane_pick(dt, lane, 2 * j + h) for h in range(2)]
            xdt = xv * jnp.where(first, dts[0], dts[1])
            ys, hs = [], []
            for h in range(2):
                cs, decay, total = _head_decay(dts[h], a_heads[h], eye, tril)
                y = _mm(g * decay, xdt) + jnp.exp(cs) * ch
                s = _mm_tn(bm * jnp.exp(total - cs), xdt)
                ys.append(y)
                hs.append(jnp.exp(total) * hprev + s)
            y_ref[rows, :] = jnp.where(first, ys[0], ys[1])
            return jnp.where(first, hs[0], hs[1])

        _unrolled_loop(nc, chunk, jnp.zeros((STATE_N, LANES), f32))

    blk = lambda f: pl.BlockSpec((t, LANES), f)
    return _call(
        body, name="ssd_fwd", grid=(D // LANES,),
        out_shape=(jax.ShapeDtypeStruct((t, D), f32), jax.ShapeDtypeStruct((nc, STATE_N, D), f32)),
        in_specs=[blk(lambda j: (0, j)), blk(lambda j: (0, cb + j // 4)), blk(lambda j: (0, cc + j // 4)),
                  blk(lambda j: (0, cdt)), _row(LANES), _row(LANES)],
        out_specs=(blk(lambda j: (0, j)), pl.BlockSpec((nc, STATE_N, LANES), lambda j: (0, 0, j))),
        sem=("arbitrary",), args=(xbc_act, xbc_act, xbc_act, proj, dt_bias_row, a_log_row), rider=rider)


def _ssd_bwd(xbc_act, proj, dt_bias_row, a_log_row, states, d_y, rider=None):
    t = xbc_act.shape[0]
    nc = t // CHUNK
    cb, cc, cdt = D // LANES, (D + 2 * STATE_N) // LANES, OFF_DT // LANES

    def body(x_ref, b_ref, c_ref, dt_ref, dtb_ref, alog_ref, st_ref, dy_ref, dx_ref, dbc_ref, ddt_ref, da_ref):
        grp, p = pl.program_id(0), pl.program_id(1)
        j = 4 * grp + p
        eye, tril, triu = _chunk_masks()
        lane = lax.broadcasted_iota(jnp.int32, (1, LANES), 1)
        first = lane < HEAD_P
        last_row = lax.broadcasted_iota(jnp.int32, (CHUNK, 1), 0) == CHUNK - 1
        a_row = -jnp.exp(alog_ref[...])
        a_heads = [jnp.sum(jnp.where(lane == 2 * j + h, a_row, 0.0), axis=1, keepdims=True) for h in range(2)]

        @pl.when(p == 0)
        def _():
            dbc_ref[...] = jnp.zeros_like(dbc_ref)

        @pl.when(j == 0)
        def _():
            ddt_ref[...] = jnp.zeros_like(ddt_ref)
            da_ref[...] = jnp.zeros_like(da_ref)

        def chunk(i, dh):
            c = nc - 1 - i
            rows = pl.ds(pl.multiple_of(c * CHUNK, CHUNK), CHUNK)
            xv, bm, cm = x_ref[rows, :], b_ref[rows, :], c_ref[rows, :]
            dtr = dt_ref[rows, :] + dtb_ref[...]
            dt = _softplus(dtr)
            hprev = st_ref[c]
            dy = dy_ref[rows, :]
            g = _mm_nt(cm, bm)
            dts = [_lane_pick(dt, lane, 2 * j + h) for h in range(2)]
            xdt = xv * jnp.where(first, dts[0], dts[1])
            dxs, dhs = [], []
            db_sum, dc_sum = None, None
            ddt_mat = jnp.zeros((CHUNK, LANES), f32)
            da_acc = jnp.zeros((1, LANES), f32)
            for h in range(2):
                mine = first if h == 0 else jnp.logical_not(first)
                cs, decay, total = _head_decay(dts[h], a_heads[h], eye, tril)
                e_cs, e_tot = jnp.exp(cs), jnp.exp(total)
                dec_s = jnp.exp(total - cs)
                dyh = jnp.where(mine, dy, 0.0)
                xdth = jnp.where(mine, xdt, 0.0)
                dhh = jnp.where(mine, dh, 0.0)
                hph = jnp.where(mine, hprev, 0.0)
                m = g * decay
                dm = _mm_nt(dyh, xdth)
                dg = dm * decay
                w = dm * m
                bdec = bm * dec_s
                dxdt = _mm_tn(m, dyh) + _mm(bdec, dhh)
                dc_off = _mm_nt(dyh, hph) * e_cs
                db_s = _mm_nt(xdth, dhh) * dec_s
                dc_h = _mm(dg, bm) + dc_off
                db_h = _mm_tn(dg, cm) + db_s
                r_s = jnp.sum(db_s * bm, axis=1, keepdims=True)
                dtotal = jnp.sum(r_s, axis=0, keepdims=True) + e_tot * jnp.sum(
                    jnp.sum(dhh * hph, axis=1, keepdims=True), axis=0, keepdims=True)
                dcs = (jnp.sum(w, axis=1, keepdims=True) - _to_col(jnp.sum(w, axis=0, keepdims=True), eye)
                       + jnp.sum(dc_off * cm, axis=1, keepdims=True) - r_s + jnp.where(last_row, dtotal, 0.0))
                da_col = jnp.sum(jnp.where(triu, _to_row(dcs, eye), 0.0), axis=1, keepdims=True)
                ddt = da_col * a_heads[h] + jnp.sum(jnp.where(mine, dxdt * xv, 0.0), axis=1, keepdims=True)
                ddt_mat = ddt_mat + jnp.where(lane == 2 * j + h, ddt, 0.0)
                da_acc = da_acc + jnp.where(lane == 2 * j + h, jnp.sum(da_col * dts[h], axis=0, keepdims=True), 0.0)
                dxs.append(dxdt * dts[h])
                dhs.append(e_tot * dhh + _mm_tn(cm * e_cs, dyh))
                db_sum = db_h if db_sum is None else db_sum + db_h
                dc_sum = dc_h if dc_sum is None else dc_sum + dc_h
            dx_ref[rows, :] = jnp.where(first, dxs[0], dxs[1])
            dbc_ref[0, rows, :] += db_sum
            dbc_ref[1, rows, :] += dc_sum
            ddt_ref[rows, :] += ddt_mat * jax.nn.sigmoid(dtr)
            da_ref[...] += da_acc * a_row
            return jnp.where(first, dhs[0], dhs[1])

        _unrolled_loop(nc, chunk, jnp.zeros((STATE_N, LANES), f32))

    blk = lambda f: pl.BlockSpec((t, LANES), f)
    return _call(
        body, name="ssd_bwd", grid=(2, 4),
        out_shape=(jax.ShapeDtypeStruct((t, D), f32), jax.ShapeDtypeStruct((2, t, 2 * STATE_N), f32),
                   jax.ShapeDtypeStruct((t, LANES), f32), jax.ShapeDtypeStruct((1, LANES), f32)),
        in_specs=[blk(lambda g, p: (0, 4 * g + p)), blk(lambda g, p: (0, cb + g)), blk(lambda g, p: (0, cc + g)),
                  blk(lambda g, p: (0, cdt)), _row(LANES), _row(LANES),
                  pl.BlockSpec((nc, STATE_N, LANES), lambda g, p: (0, 0, 4 * g + p)), blk(lambda g, p: (0, 4 * g + p))],
        out_specs=(blk(lambda g, p: (0, 4 * g + p)), pl.BlockSpec((2, t, LANES), lambda g, p: (0, 0, g)),
                   blk(lambda g, p: (0, 0)), _row(LANES)),
        sem=("arbitrary", "arbitrary"), args=(xbc_act, xbc_act, xbc_act, proj, dt_bias_row, a_log_row, states, d_y),
        rider=rider)


def _up_bwd(d_up, w_up, x1, mod, norm2_w, dx2, mix, w_out, rider=None):
    t = x1.shape[0]

    def body(dup_ref, wu_ref, x1_ref, mod_ref, nw_ref, dx2_ref, mix_ref, wo_ref,
             dx1_ref, dmix_ref, dys_ref, du_ref, st_ref):
        @pl.when(pl.program_id(0) == 0)
        def _():
            st_ref[...] = jnp.zeros_like(st_ref)

        nt = (((1,), (1,)), ((), ()))
        dh = None
        for k in range(4):
            lo = (k % 2) * UP_SHARD
            part = lax.dot_general(dup_ref[k // 2, :, lo:lo + UP_SHARD], wu_ref[k], nt, preferred_element_type=f32)
            dh = part if dh is None else dh + part
        x1 = x1_ref[...]
        rstd = lax.rsqrt(jnp.mean(x1 * x1, axis=-1, keepdims=True) + 1e-6)
        xh = x1 * rstd
        nw = nw_ref[...]
        sc = 1.0 + mod_ref[:, 4 * D:5 * D]
        st_ref[0:1, :] += jnp.sum(dh, axis=0, keepdims=True)
        st_ref[1:2, :] += jnp.sum(dh * xh * nw, axis=0, keepdims=True)
        st_ref[2:3, :] += jnp.sum(dh * sc * xh, axis=0, keepdims=True)
        dxh = dh * sc * nw
        dx1 = dx2_ref[...] + rstd * (dxh - xh * jnp.mean(dxh * xh, axis=-1, keepdims=True))
        dx1_ref[...] = dx1
        st_ref[3:4, :] += jnp.sum(dx1 * mix_ref[...], axis=0, keepdims=True)
        dmix = (mod_ref[:, 2 * D:3 * D] * dx1).astype(MX)
        dmix_ref[...] = dmix
        dys_ref[...] = lax.dot_general(dmix, wo_ref[0:D, :], nt, preferred_element_type=f32)
        du_ref[...] = lax.dot_general(dmix, wo_ref[D:2 * D, :], nt, preferred_element_type=f32)

    blk = pl.BlockSpec((TM, D), lambda i: (i, 0))
    return _call(
        body, name="up_bwd", grid=(t // TM,),
        out_shape=(jax.ShapeDtypeStruct((t, D), f32), jax.ShapeDtypeStruct((t, D), MX),
                   jax.ShapeDtypeStruct((t, D), f32), jax.ShapeDtypeStruct((t, D), f32),
                   jax.ShapeDtypeStruct((8, D), f32)),
        in_specs=[pl.BlockSpec((2, TM, D_FF), lambda i: (0, i, 0)), _resident((4, D, UP_SHARD)), blk, _row(6 * D), _row(),
                  blk, blk, _resident((2 * D, D))],
        out_specs=(blk, blk, blk, blk, pl.BlockSpec((8, D), lambda i: (0, 0))),
        sem=("arbitrary",), args=(d_up, w_up, x1, mod, norm2_w, dx2, mix, w_out), rider=rider)


def _ln_silu_bwd(d_u, u_conv, ln_w, ln_b):
    t = d_u.shape[0]

    def body(du_ref, u_ref, w_ref, b_ref, o_ref, st_ref):
        @pl.when(pl.program_id(0) == 0)
        def _():
            st_ref[...] = jnp.zeros_like(st_ref)

        u = u_ref[...]
        mu = jnp.mean(u, axis=-1, keepdims=True)
        uc = u - mu
        rstd = lax.rsqrt(jnp.mean(uc * uc, axis=-1, keepdims=True) + 1e-5)
        n = uc * rstd
        w = w_ref[...]
        dl = du_ref[...] * _dsilu(n * w + b_ref[...])
        st_ref[0:1, :] += jnp.sum(dl * n, axis=0, keepdims=True)
        st_ref[1:2, :] += jnp.sum(dl, axis=0, keepdims=True)
        dn = dl * w
        o_ref[...] = rstd * (dn - jnp.mean(dn, axis=-1, keepdims=True) - n * jnp.mean(dn * n, axis=-1, keepdims=True))

    blk = pl.BlockSpec((TM, D), lambda i: (i, 0))
    return pl.pallas_call(
        body, name="ln_silu_bwd", grid=(t // TM,),
        out_shape=(jax.ShapeDtypeStruct((t, D), f32), jax.ShapeDtypeStruct((8, D), f32)),
        in_specs=[blk, blk, _row(), _row()], out_specs=(blk, pl.BlockSpec((8, D), lambda i: (0, 0))),
        compiler_params=_cp("arbitrary"),
    )(d_u, u_conv, ln_w, ln_b)


def _ssd_gate_norm_bwd(d_out, y_scan, xbc_act, proj, d_skip_row, ssd_norm_w):
    t = d_out.shape[0]

    def body(do_ref, y_ref, xs_ref, z_ref, dsk_ref, nw_ref, dy_ref, dz_ref, st_ref):
        @pl.when(pl.program_id(0) == 0)
        def _():
            st_ref[...] = jnp.zeros_like(st_ref)

        xs = xs_ref[...]
        y = y_ref[...] + xs * dsk_ref[...]
        z = z_ref[...]
        s = _silu(z)
        yz = y * s
        rstd = lax.rsqrt(jnp.mean(yz * yz, axis=-1, keepdims=True) + 1e-6)
        n = yz * rstd
        do = do_ref[...]
        st_ref[0:1, :] += jnp.sum(do * n, axis=0, keepdims=True)
        dn = do * nw_ref[...]
        dyz = rstd * (dn - n * jnp.mean(dn * n, axis=-1, keepdims=True))
        dy = dyz * s
        dy_ref[...] = dy
        dz_ref[...] = (dyz * y * _dsilu(z)).astype(MX)
        st_ref[1:2, :] += jnp.sum(dy * xs, axis=0, keepdims=True)

    blk = pl.BlockSpec((TM, D), lambda i: (i, 0))
    return pl.pallas_call(
        body, name="ssd_gate_norm_bwd", grid=(t // TM,),
        out_shape=(jax.ShapeDtypeStruct((t, D), f32), jax.ShapeDtypeStruct((t, D), MX), jax.ShapeDtypeStruct((8, D), f32)),
        in_specs=[blk, blk, blk, blk, _row(), _row()], out_specs=(blk, blk, pl.BlockSpec((8, D), lambda i: (0, 0))),
        compiler_params=_cp("arbitrary"),
    )(d_out, y_scan, xbc_act, proj, d_skip_row, ssd_norm_w)


def _inproj_bwd(d_z, d_xraw, d_bcraw, d_conf, d_dt, w_pack, x, mod, norm1_w, dx1, after=None):
    t = x.shape[0]
    extra = [] if after is None else [after]

    def body(dz_ref, dx_ref, dbc_ref, dcf_ref, ddt_ref, w_ref, x_ref, mod_ref, nw_ref, dx1_ref, *rest):
        gx_ref, st_ref = rest[-2:]
        @pl.when(pl.program_id(0) == 0)
        def _():
            st_ref[...] = jnp.zeros_like(st_ref)

        nt = (((1,), (1,)), ((), ()))
        dot = lambda a, lo, hi: lax.dot_general(a, w_ref[:, lo:hi], nt, preferred_element_type=f32)
        dh = dot(dz_ref[...], OFF_Z, OFF_Z + D)
        dh = dh + dot(dx_ref[...], OFF_XBC, OFF_XBC + D)
        dh = dh + dot(dbc_ref[...], OFF_XBC + D, OFF_XBC + D_XBC)
        dh = dh + dot(dcf_ref[0], OFF_CA, OFF_CA + D)
        dh = dh + dot(dcf_ref[1], OFF_CG, OFF_CG + D)
        dh = dh + dot(ddt_ref[...].astype(MX), OFF_DT, OFF_DT + LANES)
        st_ref[3:4, 0:LANES] += jnp.sum(ddt_ref[...], axis=0, keepdims=True)
        xv = x_ref[...]
        rstd = lax.rsqrt(jnp.mean(xv * xv, axis=-1, keepdims=True) + 1e-6)
        xh = xv * rstd
        nw = nw_ref[...]
        sc = 1.0 + mod_ref[:, D:2 * D]
        st_ref[0:1, :] += jnp.sum(dh, axis=0, keepdims=True)
        st_ref[1:2, :] += jnp.sum(dh * xh * nw, axis=0, keepdims=True)
        st_ref[2:3, :] += jnp.sum(dh * sc * xh, axis=0, keepdims=True)
        dxh = dh * sc * nw
        gx_ref[...] = dx1_ref[...] + rstd * (dxh - xh * jnp.mean(dxh * xh, axis=-1, keepdims=True))

    blk = pl.BlockSpec((TM, D), lambda i: (i, 0))
    return _call(
        body, name="inproj_bwd", grid=(t // TM,),
        out_shape=(jax.ShapeDtypeStruct((t, D), f32), jax.ShapeDtypeStruct((8, D), f32)),
        in_specs=[blk, blk, pl.BlockSpec((TM, 2 * CW), lambda i: (i, 0)), pl.BlockSpec((2, TM, D), lambda i: (0, i, 0)),
                  pl.BlockSpec((TM, LANES), lambda i: (i, 0)), _resident((D, W_PACK)), blk, _row(6 * D), _row(), blk]
        + [ANY] * len(extra),
        out_specs=(blk, pl.BlockSpec((8, D), lambda i: (0, 0))),
        sem=("arbitrary",), args=(d_z, d_xraw, d_bcraw, d_conf, d_dt, w_pack, x, mod, norm1_w, dx1, *extra))[0]


def _wgrad(a, d, name, bn=256, transposed=True):
    k, t = a.shape if transposed else a.shape[::-1]
    n = d.shape[1]
    out_dtype = MX
    contract = (((1,), (0,)), ((), ())) if transposed else (((0,), (0,)), ((), ()))

    def body(a_ref, d_ref, o_ref):
        o_ref[...] = lax.dot_general(a_ref[...], d_ref[...].astype(MX), contract, preferred_element_type=f32).astype(out_dtype)

    return pl.pallas_call(
        body, name=name, grid=(n // bn,), out_shape=jax.ShapeDtypeStruct((k, n), out_dtype),
        in_specs=[_resident(a.shape), pl.BlockSpec((t, bn), lambda j: (0, j))],
        out_specs=pl.BlockSpec((k, bn), lambda j: (0, j)), compiler_params=_cp("arbitrary"),
    )(a, d)


def _wgrad_stacked(at, d, name, bn):
    out_dtype = MX
    k, t = at.shape
    s, _, n = d.shape
    nb = n // bn

    def body(a_ref, d_ref, o_ref):
        o_ref[0] = jnp.dot(a_ref[...], d_ref[0], preferred_element_type=f32).astype(out_dtype)

    return pl.pallas_call(
        body, name=name, grid=(s, nb), out_shape=jax.ShapeDtypeStruct((s * nb, k, bn), out_dtype),
        in_specs=[_resident((k, t)), pl.BlockSpec((1, t, bn), lambda i, j: (i, 0, j))],
        out_specs=pl.BlockSpec((1, k, bn), lambda i, j: (i * nb + j, 0, 0)), compiler_params=_cp("arbitrary", "arbitrary"),
    )(at, d)


def _pad_row(v, width=LANES):
    return jnp.pad(v.reshape(1, -1), ((0, 0), (0, width - v.size)))


def _quarters(a):
    return a.reshape(4, 2, a.shape[0] // 8, a.shape[1])


def _local_step(x, mod, target, w_pack, late, small, reducer=None):
    dtb_row, alog_row = _pad_row(small["dt_bias"]), _pad_row(small["a_log"])
    dskip_row = jnp.repeat(small["d_skip"].reshape(-1), HEAD_P).reshape(1, D)

    red = reducer

    def hosted(host, args, swap=None, scatter=None, gather=None, sums=()):
        if red is None:
            return host(*args)[0]
        riders = ([red.scatter(scatter)] if scatter else []) + ([red.swap(*swap)] if swap else [])
        riders += [_SwapSumsRider([red.sums[n] for n in sums])] if sums else []
        riders += [_GatherRider([gather[0]], *gather[1:])] if gather is not None else []
        both = _Riders(riders)
        outs, extra = host(*args, rider=both)
        extra = both.split(extra)
        if scatter:
            red.scattered(scatter, extra.pop(0))
        if swap:
            red.swapped(swap[0], extra.pop(0))
        if sums:
            red.others.update(zip(sums, extra.pop(0)))
        return (outs, extra[0][0]) if gather is not None else outs

    w_out, w_up, w_down = late
    if red is None:
        proj, h_t = hosted(_ln_inproj, (x, mod, small["norm1_w"], w_pack))
        xbc_act, = hosted(_ssd_conv_fwd, (proj, small["ssd_conv_w"], small["ssd_conv_b"]))
        y_scan, states = hosted(_ssd_fwd, (xbc_act, proj, dtb_row, alog_row))
        u_conv, = hosted(_glu_conv_fwd, (proj, small["conf_conv_w"], small["conf_conv_b"]))
    else:
        (proj, h_t), w_out = hosted(_ln_inproj, (x, mod, small["norm1_w"], w_pack), gather=(w_out,))
        (xbc_act,), w_up = hosted(_ssd_conv_fwd, (proj, small["ssd_conv_w"], small["ssd_conv_b"]), gather=(w_up, 0, UP_EARLY_ROWS))
        (y_scan, states), w_up = hosted(_ssd_fwd, (xbc_act, proj, dtb_row, alog_row), gather=(w_up, UP_EARLY_ROWS, None))
        (u_conv,), w_down = hosted(_glu_conv_fwd, (proj, small["conf_conv_w"], small["conf_conv_b"]), gather=(w_down,))
        w_out, w_up, w_down = w_out.reshape(2 * D, D), w_up.reshape(4, D, UP_SHARD), w_down.reshape(D_FF, D)
    y_ssd = _ssd_gate_norm(y_scan, xbc_act, proj, dskip_row, small["ssd_norm_w"])
    u = _ln_silu(u_conv, small["conf_ln_w"], small["conf_ln_b"])
    mix, x1, h2_t, up = _outproj_ln2_up(y_ssd, u, w_out, x, mod, small["norm2_w"], w_up)
    act, = _ffn_conv_fwd(up, small["ffn_conv_w"], small["ffn_conv_b"])[0]
    dx2, d_ffn, d_act, st_down = _down_loss(act, w_down, x1, mod, small["final_norm_w"], target)

    g_down = _quarters(_wgrad(act, d_ffn, "wgrad_down", transposed=False))
    d_up, dw_ffn, db_ffn = hosted(_ffn_conv_bwd, (up, small["ffn_conv_w"], small["ffn_conv_b"], d_act), swap=("w_down", g_down))
    g_up = _wgrad_stacked(h2_t, d_up, "wgrad_up", D_FF // 2).reshape(4, 2, D // 2, UP_SHARD)
    dx1, d_mix, d_yssd, d_u, st_up = hosted(_up_bwd, (d_up, w_up, x1, mod, small["norm2_w"], dx2, mix, w_out),
                                            scatter="w_down", swap=("w_up", g_up))
    g_out = _quarters(jnp.concatenate([_wgrad(y_ssd, d_mix, "wgrad_out_y", transposed=False),
                                       _wgrad(u, d_mix, "wgrad_out_u", transposed=False)], axis=0))
    d_uconv, st_ln = _ln_silu_bwd(d_u, u_conv, small["conf_ln_w"], small["conf_ln_b"])
    d_conf, dw_conf, db_conf = hosted(_glu_conv_bwd, (proj, small["conf_conv_w"], d_uconv), scatter="w_up",
                                      swap=("w_out", g_out))
    d_y, d_z, st_gn = _ssd_gate_norm_bwd(d_yssd, y_scan, xbc_act, proj, dskip_row, small["ssd_norm_w"])
    d_xs, d_bc, d_dt, d_alog = hosted(_ssd_bwd, (xbc_act, proj, dtb_row, alog_row, states, d_y), scatter="w_out")
    d_xraw, dw_sx, db_sx = _ssd_conv_bwd_x(proj, small["ssd_conv_w"], small["ssd_conv_b"], d_xs, d_y, dskip_row)
    d_bcraw, dw_sbc, db_sbc = _ssd_conv_bwd_bc(proj, small["ssd_conv_w"], small["ssd_conv_b"], d_bc)
    g_in = _unpack_g_in(dict(
        z=_wgrad(h_t, d_z, "wgrad_in_z"), x=_wgrad(h_t, d_xraw, "wgrad_in_x"), bc=_wgrad(h_t, d_bcraw, "wgrad_in_bc"),
        conf=_wgrad_stacked(h_t, d_conf, "wgrad_in_conf", D), dt=_wgrad(h_t, d_dt, "wgrad_in_dt", bn=LANES)))
    g_in = g_in.reshape(4, 2, D // 2, W_IN_SHARD_PAD)
    args = (d_z, d_xraw, d_bcraw, d_conf, d_dt, w_pack, x, mod, small["norm1_w"], dx1)
    if red is None:
        grad_x, st_in = _inproj_bwd(*args)
    else:
        done = ("w_out", "w_up", "w_down")
        both = _Riders([red.swap("w_in", g_in), _SwapSumsRider([red.sums[n] for n in done])])
        handles, token = _split_start(both, "swap_start_w_in")
        grad_x, st_in = _inproj_bwd(*args, after=token)
        thru, outs = _split_wait(both, "swap_wait_w_in", handles, st_in)
        red.grads["w_in"] = thru[0]
        red.sums.update(zip(done, thru[1:]))
        got, others = both.split(outs)
        red.swapped("w_in", got)
        red.others.update(zip(done, others))

    gsmall = _pack_small_grads(st_in, st_up, st_down, st_ln, st_gn, d_alog, dw_sx, dw_sbc, db_sx, db_sbc, dw_conf, db_conf,
                               dw_ffn, db_ffn)
    gbig = None if reducer is not None else dict(w_in=g_in, w_out=g_out, w_up=g_up, w_down=g_down)
    return st_down[2, 0], grad_x, gbig, gsmall


VECTORS = ("ada_b", "norm1_w", "ssd_conv_b", "dt_bias", "a_log", "d_skip", "ssd_norm_w", "conf_conv_b", "conf_ln_w",
           "conf_ln_b", "norm2_w", "ffn_conv_b", "final_norm_w")
VECTOR_SIZES = (6 * D, D, D_XBC, HEADS, HEADS, HEADS, D, D, D, D, D, 2 * D_FF, D)
CONVS = {"ssd_conv_w": (K_SSD, D_XBC), "conf_conv_w": (K_CONF, D), "ffn_conv_w": (K_FFN, 2 * D_FF)}


def _pack_rows(items):
    n = -(-sum(w for _, w in items) // (8 * LANES)) * LANES
    while True:
        fill, place = [0] * 8, {}
        for key, w in sorted(items, key=lambda kv: -kv[1]):
            rows = [r for r in range(8) if fill[r] + w <= n]
            if not rows:
                break
            place[key] = (rows[0], fill[rows[0]])
            fill[rows[0]] += w
        if len(place) == len(items):
            return n, place
        n += LANES


FRONT_N, FRONT = _pack_rows([("c", D)] + [((nm, j), cols // 4) for nm, (taps, cols) in CONVS.items() for j in range(taps)])
BACK_N, BACK = _pack_rows([(nm, -(-sz // LANES) * LANES) for nm, sz in zip(VECTORS, VECTOR_SIZES)]
                          + [((nm, j), cols) for nm, (taps, cols) in CONVS.items() for j in range(taps)] + [("loss", LANES)])
_VM = pltpu.CompilerParams(vmem_limit_bytes=VMEM_LIMIT)


def _pack_front(c, shards):
    def body(c_ref, *refs):
        o_ref = refs[-1]
        o_ref[...] = jnp.zeros_like(o_ref)
        r, o = FRONT["c"]
        o_ref[r:r + 1, o:o + D] = c_ref[...]
        for ref, (nm, (taps, cols)) in zip(refs, CONVS.items()):
            for j in range(taps):
                r, o = FRONT[(nm, j)]
                o_ref[r:r + 1, o:o + cols // 4] = ref[0, j:j + 1, :]

    return pl.pallas_call(body, name="pack_front", out_shape=jax.ShapeDtypeStruct((8, FRONT_N), f32),
                          compiler_params=_VM)(c, *shards)


def _unpack_front(got):
    def body(g_ref, c_ref, *outs):
        r, o = FRONT["c"]
        for d in range(8):
            c_ref[d:d + 1, :] = g_ref[8 * d + r:8 * d + r + 1, o:o + D]
        for ref, (nm, (taps, cols)) in zip(outs, CONVS.items()):
            cw = cols // 4
            for j in range(taps):
                r, o = FRONT[(nm, j)]
                for k in range(4):
                    ref[j:j + 1, k * cw:(k + 1) * cw] = g_ref[16 * k + r:16 * k + r + 1, o:o + cw]

    return pl.pallas_call(
        body, name="unpack_front", compiler_params=_VM,
        out_shape=(jax.ShapeDtypeStruct((8, D), f32),) + tuple(jax.ShapeDtypeStruct(tc, f32) for tc in CONVS.values()),
    )(got)


def _pack_small_grads(st_in, st_up, st_down, st_ln, st_gn, d_alog, dw_sx, dw_sbc, db_sx, db_sbc, dw_conf, db_conf, dw_ffn,
                      db_ffn):
    def body(in_ref, up_ref, dn_ref, ln_ref, gn_ref, al_ref, wx_ref, wbc_ref, bx_ref, bbc_ref, wc_ref, bc_ref, wf_ref, bf_ref,
             o_ref):
        def put(key, val, shift=0):
            r, o = BACK[key]
            o_ref[r:r + 1, o + shift:o + shift + val.shape[1]] = val

        o_ref[...] = jnp.zeros_like(o_ref)
        for i, piece in enumerate((in_ref[0:1, :], in_ref[1:2, :], up_ref[3:4, :], up_ref[0:1, :], up_ref[1:2, :],
                                   dn_ref[1:2, :])):
            put("ada_b", piece, i * D)
        put("norm1_w", in_ref[2:3, :])
        put("ssd_conv_b", bx_ref[...])
        put("ssd_conv_b", bbc_ref[...], D)
        put("dt_bias", in_ref[3:4, 0:LANES])
        put("a_log", al_ref[...])
        lane = lax.broadcasted_iota(jnp.int32, (1, LANES), 1)
        col = lax.broadcasted_iota(jnp.int32, (1, D), 1)
        per_col = gn_ref[1:2, :]
        d_skip = jnp.zeros((1, LANES), f32)
        for h in range(HEADS):
            in_head = jnp.logical_and(col >= h * HEAD_P, col < (h + 1) * HEAD_P)
            s = jnp.sum(jnp.where(in_head, per_col, 0.0), axis=1, keepdims=True)
            d_skip = d_skip + jnp.where(lane == h, s, 0.0)
        put("d_skip", d_skip)
        put("ssd_norm_w", gn_ref[0:1, :])
        put("conf_conv_b", bc_ref[...])
        put("conf_ln_w", ln_ref[0:1, :])
        put("conf_ln_b", ln_ref[1:2, :])
        put("norm2_w", up_ref[2:3, :])
        put("ffn_conv_b", bf_ref[0])
        put("ffn_conv_b", bf_ref[1], D_FF)
        put("final_norm_w", dn_ref[0:1, :])
        put("loss", dn_ref[2:3, 0:LANES])
        for j in range(K_SSD):
            put(("ssd_conv_w", j), wx_ref[j:j + 1, :])
            put(("ssd_conv_w", j), wbc_ref[j:j + 1, :], D)
        for j in range(K_CONF):
            put(("conf_conv_w", j), wc_ref[j:j + 1, :])
        for j in range(K_FFN):
            put(("ffn_conv_w", j), wf_ref[0, j:j + 1, :])
            put(("ffn_conv_w", j), wf_ref[1, j:j + 1, :], D_FF)

    return pl.pallas_call(body, name="pack_small_grads", out_shape=jax.ShapeDtypeStruct((8, BACK_N), f32), compiler_params=_VM)(
        st_in, st_up, st_down, st_ln, st_gn, d_alog, dw_sx, dw_sbc, db_sx, db_sbc, dw_conf, db_conf, dw_ffn, db_ffn)


def _small_adamw(got, chip, w, m, v):
    names = VECTORS + tuple(CONVS)
    n_par = len(names)

    def body(chip_ref, g_ref, *refs):
        ins, outs = refs[:3 * n_par], refs[3 * n_par:]
        dm_ref, loss_ref, outs = outs[0], outs[1], outs[2:]
        chip_id = chip_ref[0]

        def summed(key, width):
            r, o = BACK[key]
            s = g_ref[r:r + 1, o:o + width]
            for d in range(1, 8):
                s = s + g_ref[8 * d + r:8 * d + r + 1, o:o + width]
            return s

        def mine(full, cw):
            out = full[:, 0:cw]
            for k in range(1, 4):
                out = jnp.where(chip_id == k, full[:, k * cw:(k + 1) * cw], out)
            return out

        r, o = BACK["ada_b"]
        for d in range(8):
            dm_ref[d:d + 1, :] = mine(g_ref[8 * d + r:8 * d + r + 1, o:o + 6 * D], 6 * D // 4)
        loss_ref[...] = summed("loss", LANES)
        for i, (nm, size) in enumerate(zip(VECTORS, VECTOR_SIZES)):
            g = summed(nm, -(-size // LANES) * LANES)[:, 0:size]
            res = _adam_math(ins[3 * i][...], g, ins[3 * i + 1][...], ins[3 * i + 2][...])
            for ref, val in zip(outs[4 * i:4 * i + 4], (g,) + res):
                ref[...] = val
        for i, (nm, (taps, cols)) in enumerate(CONVS.items(), start=len(VECTORS)):
            for j in range(taps):
                g = mine(summed((nm, j), cols), cols // 4)
                res = _adam_math(ins[3 * i][0, j:j + 1, :], g, ins[3 * i + 1][0, j:j + 1, :], ins[3 * i + 2][0, j:j + 1, :])
                for ref, val in zip(outs[4 * i:4 * i + 4], (g,) + res):
                    ref[0, j:j + 1, :] = val

    params = [a[nm] for nm in names for a in (w, m, v)]
    whole = lambda s: pl.BlockSpec(s, lambda i, chip, nd=len(s): (0,) * nd)
    out_shape = [jax.ShapeDtypeStruct((8, 6 * D // 4), f32), jax.ShapeDtypeStruct((1, LANES), f32)]
    out_shape += [jax.ShapeDtypeStruct(w[nm].shape, f32) for nm in names for _ in range(4)]
    outs = pl.pallas_call(
        body, name="small_adamw", out_shape=tuple(out_shape), compiler_params=_VM,
        grid_spec=pltpu.PrefetchScalarGridSpec(
            num_scalar_prefetch=1, grid=(1,), in_specs=[whole(got.shape)] + [whole(p.shape) for p in params],
            out_specs=tuple(whole(s.shape) for s in out_shape)),
    )(_scalar(chip), got, *params)
    return outs[0], outs[1][0, 0], {nm: outs[2 + 4 * i:6 + 4 * i] for i, nm in enumerate(names)}


W_IN_COLS = 4624
W_IN_SHARD = W_IN_COLS // 4
W_IN_SHARD_PAD = 1280
_SEGMENTS = ((0, 1024, OFF_Z), (1024, 2560, OFF_XBC), (2560, 2576, OFF_DT), (2576, 3600, OFF_CA), (3600, 4624, OFF_CG))


def _in_pieces(bounds=()):
    out = []
    for k in range(4):
        s0, s1 = k * W_IN_SHARD, (k + 1) * W_IN_SHARD
        for lo, hi, off in _SEGMENTS:
            a, b = max(lo, s0), min(hi, s1)
            while a < b:
                p = off + a - lo
                e = min([b - a] + [c - p for c in bounds if c > p])
                out.append((k, a - s0, p, e))
                a += e
    return out


def _pack_w_in(shards):
    pieces = _in_pieces()

    def body(s_ref, o_ref):
        o_ref[:, OFF_DT:W_PACK] = jnp.zeros((TM, W_PACK - OFF_DT), MX)
        for k, c, p, n in pieces:
            o_ref[:, p:p + n] = s_ref[k, :, c:c + n]

    return pl.pallas_call(
        body, name="pack_w_in", grid=(D // TM,), out_shape=jax.ShapeDtypeStruct((D, W_PACK), MX),
        in_specs=[pl.BlockSpec((4, TM, W_IN_SHARD_PAD), lambda i: (0, i, 0))],
        out_specs=pl.BlockSpec((TM, W_PACK), lambda i: (i, 0)), compiler_params=_cp("arbitrary"),
    )(shards)


def _unpack_g_in(g):
    srcs = ((OFF_Z, D), (OFF_XBC, D), (OFF_XBC + D, 2 * CW), (OFF_CA, D), (OFF_CG, D), (OFF_DT, LANES))
    pieces = _in_pieces(tuple(o for o, _ in srcs) + tuple(o + n for o, n in srcs))

    def body(z_ref, x_ref, bc_ref, cf_ref, dt_ref, o_ref):
        read = (lambda lo, hi: z_ref[:, lo:hi], lambda lo, hi: x_ref[:, lo:hi], lambda lo, hi: bc_ref[:, lo:hi],
                lambda lo, hi: cf_ref[0, :, lo:hi], lambda lo, hi: cf_ref[1, :, lo:hi], lambda lo, hi: dt_ref[:, lo:hi])
        o_ref[:, :, W_IN_SHARD - 4:W_IN_SHARD_PAD] = jnp.zeros((4, TM, W_IN_SHARD_PAD - W_IN_SHARD + 4), MX)
        for k, c, p, n in pieces:
            i = [q for q, (o, w) in enumerate(srcs) if o <= p < o + w][0]
            o_ref[k, :, c:c + n] = read[i](p - srcs[i][0], p - srcs[i][0] + n)

    blk = lambda w: pl.BlockSpec((TM, w), lambda i: (i, 0))
    return pl.pallas_call(
        body, name="unpack_g_in", grid=(D // TM,), out_shape=jax.ShapeDtypeStruct((4, D, W_IN_SHARD_PAD), MX),
        in_specs=[blk(D), blk(D), blk(2 * CW), pl.BlockSpec((2, TM, D), lambda i: (0, i, 0)), blk(LANES)],
        out_specs=pl.BlockSpec((4, TM, W_IN_SHARD_PAD), lambda i: (0, i, 0)), compiler_params=_cp("arbitrary"),
    )(g["z"], g["x"], g["bc"], g["conf"], g["dt"])


def _scalar(v):
    return jnp.reshape(v, (1,)).astype(jnp.int32)


def _cast_into_slot(w, width, chip):
    r, c = w.shape
    h = r // 2
    tm = _row_tile(h)
    nj = h // tm

    def body(chip_ref, w_ref, o_ref):
        v = w_ref[...].astype(MX)
        o_ref[0, 0] = v if width == c else jnp.concatenate([v, jnp.zeros((tm, width - c), MX)], axis=1)

    return pl.pallas_call(
        body, name=f"cast_into_slot_{r}x{c}", out_shape=jax.ShapeDtypeStruct((4, 2, h, width), MX),
        grid_spec=pltpu.PrefetchScalarGridSpec(
            num_scalar_prefetch=1, grid=(2, nj),
            in_specs=[pl.BlockSpec((tm, c), lambda i, j, chip: (i * nj + j, 0))],
            out_specs=pl.BlockSpec((1, 1, tm, width), lambda i, j, chip: (chip[0], i, j, 0))),
        compiler_params=_cp("arbitrary", "arbitrary"),
    )(_scalar(chip), w)


def _columns_first(w):
    return jnp.transpose(w, (2, 0, 1))


def _cast_into_slot_w_in(w_t, chip):
    h = D // 2
    nj = h // TM
    pad = W_IN_SHARD_PAD - W_IN_SHARD

    def body(chip_ref, w_ref, o_ref):
        cols = jnp.concatenate([w_ref[:, 0, :], jnp.zeros((pad, TM), f32)], axis=0)
        o_ref[0, 0] = cols.T.astype(MX)

    return pl.pallas_call(
        body, name="cast_into_slot_w_in", out_shape=jax.ShapeDtypeStruct((4, 2, h, W_IN_SHARD_PAD), MX),
        grid_spec=pltpu.PrefetchScalarGridSpec(
            num_scalar_prefetch=1, grid=(2, nj),
            in_specs=[pl.BlockSpec((W_IN_SHARD, 1, TM), lambda i, j, chip: (0, 0, i * nj + j))],
            out_specs=pl.BlockSpec((1, 1, TM, W_IN_SHARD_PAD), lambda i, j, chip: (chip[0], i, j, 0))),
        compiler_params=_cp("arbitrary", "arbitrary"),
    )(_scalar(chip), w_t)


def _adamw_w_in(w_t, mine, other, m_t, v_t, core):
    h = D // 2
    nj = h // TM

    def body(core_ref, w_ref, a_ref, b_ref, m_ref, v_ref, g_ref, d_ref, nm_ref, nv_ref):
        g = jnp.where(pl.program_id(0) == core_ref[0], a_ref[...], b_ref[...]).T[0:W_IN_SHARD, :]
        g_ref[:, 0, :] = g
        d_ref[:, 0, :], nm_ref[:, 0, :], nv_ref[:, 0, :] = _adam_math(w_ref[:, 0, :], g, m_ref[:, 0, :], v_ref[:, 0, :])

    blk = pl.BlockSpec((W_IN_SHARD, 1, TM), lambda i, j, core: (0, 0, i * nj + j))
    gblk = pl.BlockSpec((TM, W_IN_SHARD_PAD), lambda i, j, core: (j, 0))
    return pl.pallas_call(
        body, name="adamw_w_in", out_shape=tuple([jax.ShapeDtypeStruct((W_IN_SHARD, 1, D), f32)] * 4),
        grid_spec=pltpu.PrefetchScalarGridSpec(
            num_scalar_prefetch=1, grid=(2, nj), in_specs=[blk, gblk, gblk, blk, blk], out_specs=(blk,) * 4),
        compiler_params=_cp("arbitrary", "arbitrary"),
    )(_scalar(core), w_t, mine, other, m_t, v_t)


ANY = pl.BlockSpec(memory_space=pl.ANY)


def _place():
    x, y, c = lax.axis_index("x"), lax.axis_index("y"), lax.axis_index("c")
    return x, y, c, [(1 - x, y), (x, 1 - y), (1 - x, 1 - y)]


_GATHER_SEMS = [pltpu.SemaphoreType.DMA((7,)), pltpu.SemaphoreType.DMA((7,)), pltpu.SemaphoreType.DMA]


def _gather_rows_steps(x_ref, out_ref, send_sems, recv_sems, local_sem, after_first=None):
    m_per = x_ref.shape[0]
    x, y, c, chips = _place()
    me, sibling = (x, y, c), (x, y, 1 - c)

    def rows(px, py, pc):
        return out_ref.at[pl.ds((4 * px + 2 * py + pc) * m_per, m_per), :]

    def copy(k, blk, to, src=None):
        return pltpu.make_async_remote_copy(
            src_ref=rows(*blk) if src is None else src, dst_ref=rows(*blk), send_sem=send_sems.at[k],
            recv_sem=recv_sems.at[k], device_id=to, device_id_type=MESH)

    mine = pltpu.make_async_copy(x_ref, rows(*me), local_sem)
    mine.start()
    first = [copy(0, me, sibling, src=x_ref)]
    first += [copy(1 + j, me, (*chip, c), src=x_ref) for j, chip in enumerate(chips)]
    for cp in first:
        cp.start()
    if after_first is not None:
        after_first()
    passed = [copy(4 + j, (*chip, c), sibling) for j, chip in enumerate(chips)]
    for j, chip in enumerate(chips):
        copy(1 + j, (*chip, c), me).wait_recv()
        passed[j].start()
    copy(0, sibling, me).wait_recv()
    for j, chip in enumerate(chips):
        copy(4 + j, (*chip, 1 - c), me).wait_recv()
    for cp in first + passed:
        cp.wait_send()
    mine.wait()


def _gather_rows(block):
    m_per, n = block.shape

    def body(x_ref, out_ref, send_sems, recv_sems, local_sem):
        _gather_rows_steps(x_ref, out_ref, send_sems, recv_sems, local_sem)

    vmem = pl.BlockSpec(memory_space=pltpu.VMEM)
    return pl.pallas_call(
        body, name=f"gather_rows_{m_per}x{n}", out_shape=jax.ShapeDtypeStruct((8 * m_per, n), block.dtype),
        in_specs=[vmem], out_specs=vmem, scratch_shapes=list(_GATHER_SEMS), compiler_params=_VM)(block)


def _front(block, ada_w, slot, late):
    half = slot.shape[2] // 2
    riders = [_GatherRider([slot], 0, half), _GatherRider([slot], half, half)]
    n_mod = ada_w.shape[1]
    n_late = len(late)

    def body(x_ref, w_ref, *refs):
        cx, cy, _, _ = _place()
        late_refs, slot_ref = refs[:n_late], refs[n_late]
        out_ref, mod_ref, slot_out = refs[n_late + 1:n_late + 4]
        late_slots = refs[n_late + 4:2 * n_late + 4]
        c_scr, mine_scr = refs[2 * n_late + 4:2 * n_late + 6]
        bufs = refs[2 * n_late + 6:3 * n_late + 6]
        sems = refs[3 * n_late + 6:]
        r_scr, cast_sems = [sems[6:8], sems[8:10]], sems[10]
        casts = [pltpu.make_async_copy(bufs[k], late_slots[k].at[2 * cx + cy], cast_sems.at[k]) for k in range(n_late)]

        def first_half():
            riders[0].start([slot_ref], [slot_out], r_scr[0])
            for k in range(n_late):
                h = bufs[k].shape[1]
                bufs[k][0] = late_refs[k][0:h, :].astype(MX)
                bufs[k][1] = late_refs[k][h:2 * h, :].astype(MX)
                casts[k].start()

        _gather_rows_steps(x_ref, out_ref, *sems[0:3], after_first=first_half)
        r, o = FRONT["c"]
        for d in range(8):
            c_scr[d:d + 1, :] = out_ref[8 * d + r:8 * d + r + 1, o:o + D]
        mine_scr[...] = jnp.dot(_silu(c_scr[...]).astype(MX), w_ref[...].astype(MX), preferred_element_type=f32)
        _gather_rows_steps(mine_scr, mod_ref, *sems[3:6], after_first=lambda: riders[1].start([slot_ref], [slot_out], r_scr[1]))
        for rider, scr in zip(riders, r_scr):
            rider.finish([slot_ref], [slot_out], scr)
        for cp in casts:
            cp.wait()

    vmem = pl.BlockSpec(memory_space=pltpu.VMEM)
    late_shapes = [(4, 2, a.shape[0] // 2, a.shape[1]) for a in late]
    outs = pl.pallas_call(
        body, name="front",
        out_shape=(jax.ShapeDtypeStruct((64, block.shape[1]), f32), jax.ShapeDtypeStruct((64, n_mod), f32),
                   jax.ShapeDtypeStruct(slot.shape, slot.dtype)) + tuple(jax.ShapeDtypeStruct(s, MX) for s in late_shapes),
        in_specs=[vmem, vmem] + [vmem] * n_late + [ANY], out_specs=(vmem, vmem, ANY) + (ANY,) * n_late,
        scratch_shapes=[pltpu.VMEM((8, D), f32), pltpu.VMEM((8, n_mod), f32)] + [pltpu.VMEM(s[1:], MX) for s in late_shapes]
        + list(_GATHER_SEMS) * 2 + riders[0].scratch + riders[1].scratch + [pltpu.SemaphoreType.DMA((n_late,))],
        input_output_aliases={2 + n_late: 2}, compiler_params=_VM,
    )(block, ada_w, *late, slot)
    return outs[0], outs[1], outs[2], tuple(outs[3:])


class _GatherRider:
    def __init__(self, slots, row0=0, nrows=None):
        n = len(slots)
        self.n = n
        self.rows = (row0, slots[0].shape[2] - row0 if nrows is None else nrows)
        self.inputs = list(slots)
        self.out_shape = [jax.ShapeDtypeStruct(s.shape, s.dtype) for s in slots]
        self.scratch = [pltpu.SemaphoreType.DMA((n, 6)), pltpu.SemaphoreType.DMA((n, 6))]
        self.aliases = {a: a for a in range(n)}

    def _copy(self, outs, sems, a, j, k, half, to):
        dst = outs[a].at[k, half, pl.ds(*self.rows)]
        return pltpu.make_async_remote_copy(src_ref=dst, dst_ref=dst, send_sem=sems[0].at[a, j], recv_sem=sems[1].at[a, j],
                                            device_id=to, device_id_type=MESH)

    def _first(self, outs, sems):
        x, y, c, chips = _place()
        return [self._copy(outs, sems, a, j, 2 * x + y, c, (*chip, c)) for a in range(self.n) for j, chip in enumerate(chips)]

    def start(self, ins, outs, sems):
        for cp in self._first(outs, sems):
            cp.start()

    def _passed(self, outs, sems):
        x, y, c, chips = _place()
        return [self._copy(outs, sems, a, 3 + j, 2 * px + py, c, (x, y, 1 - c))
                for a in range(self.n) for j, (px, py) in enumerate(chips)]

    def forward(self, ins, outs, sems):
        x, y, c, chips = _place()
        passed = self._passed(outs, sems)
        for a in range(self.n):
            for j, (px, py) in enumerate(chips):
                self._copy(outs, sems, a, j, 2 * px + py, c, (x, y, c)).wait_recv()
                passed[3 * a + j].start()
        self.forwarded = True

    def finish(self, ins, outs, sems):
        if not getattr(self, "forwarded", False):
            self.forward(ins, outs, sems)
        x, y, c, chips = _place()
        for a in range(self.n):
            for j, (px, py) in enumerate(chips):
                self._copy(outs, sems, a, 3 + j, 2 * px + py, 1 - c, (x, y, c)).wait_recv()
        for cp in self._first(outs, sems) + self._passed(outs, sems):
            cp.wait_send()


class _ScatterRider:
    def __init__(self, parts):
        n = len(parts)
        self.n = n
        self.inputs = list(parts)
        self.out_shape = [jax.ShapeDtypeStruct((3,) + p.shape[1:], p.dtype) for p in parts]
        self.scratch = [pltpu.SemaphoreType.DMA((3 * n,)), pltpu.SemaphoreType.DMA((3 * n,))]
        self.aliases = {}

    def _copies(self, ins, outs, sems):
        x, y, c, chips = _place()
        return [pltpu.make_async_remote_copy(
            src_ref=ins[a].at[2 * px + py], dst_ref=outs[a].at[j], send_sem=sems[0].at[3 * a + j],
            recv_sem=sems[1].at[3 * a + j], device_id=(px, py, c), device_id_type=MESH)
            for a in range(self.n) for j, (px, py) in enumerate(chips)]

    def start(self, ins, outs, sems):
        for cp in self._copies(ins, outs, sems):
            cp.start()

    def finish(self, ins, outs, sems):
        for cp in self._copies(ins, outs, sems):
            cp.wait()


HBM = pl.BlockSpec(memory_space=pltpu.HBM)
SEM = pl.BlockSpec(memory_space=pltpu.SEMAPHORE)
EFFECT = pltpu.SideEffectType.DATAFLOW_SIDE_EFFECTING


def _split_start(rider, name, after=None):
    ni, no, ns = len(rider.inputs), len(rider.out_shape), len(rider.scratch)
    extra = [] if after is None else [after]

    def body(*refs):
        ins, lands = refs[:ni], refs[ni:ni + no]
        sems = refs[ni + no + len(extra):ni + no + len(extra) + ns]
        rider.start(ins, lands, sems)
        refs[-1][...] = jnp.zeros_like(refs[-1])

    bufs = list(rider.inputs) + [lax.empty(s.shape, s.dtype) for s in rider.out_shape]
    outs = pl.pallas_call(
        body, name=name,
        out_shape=tuple(rider.scratch) + tuple(pltpu.HBM(b.shape, b.dtype) for b in bufs) + (jax.ShapeDtypeStruct((8, LANES), f32),),
        in_specs=[HBM] * (ni + no) + [ANY] * len(extra),
        out_specs=(SEM,) * ns + (HBM,) * (ni + no) + (pl.BlockSpec(memory_space=pltpu.VMEM),),
        input_output_aliases={i: ns + i for i in range(ni + no)},
        compiler_params=pltpu.CompilerParams(has_side_effects=EFFECT),
    )(*[pltpu.with_memory_space_constraint(b, pltpu.HBM) for b in bufs], *extra)
    return outs[:-1], outs[-1]


def _split_wait(rider, name, handles, after):
    ni, no, ns = len(rider.inputs), len(rider.out_shape), len(rider.scratch)
    sems, bufs = handles[:ns], handles[ns:]

    def body(*refs):
        rider.finish(refs[:ni], refs[ni:ni + no], refs[ni + no:ni + no + ns])

    outs = pl.pallas_call(
        body, name=name, out_shape=tuple(pltpu.HBM(b.shape, b.dtype) for b in bufs),
        in_specs=[HBM] * (ni + no) + [SEM] * ns + [ANY], out_specs=(HBM,) * (ni + no),
        input_output_aliases={i: i for i in range(ni + no)}, compiler_params=pltpu.CompilerParams(has_side_effects=EFFECT),
    )(*bufs, *sems, after)
    return outs[:ni], outs[ni:]


def _ride_alone(rider, name):
    n = len(rider.inputs)

    def body(*refs):
        ins, outs, sems = refs[:n], refs[n:n + len(rider.out_shape)], refs[n + len(rider.out_shape):]
        rider.start(ins, outs, sems)
        rider.finish(ins, outs, sems)

    return pl.pallas_call(
        body, name=name, out_shape=tuple(rider.out_shape), in_specs=[ANY] * n, out_specs=tuple([ANY] * len(rider.out_shape)),
        input_output_aliases=dict(rider.aliases), scratch_shapes=list(rider.scratch),
    )(*rider.inputs)


class _SwapRider:
    def __init__(self, grads):
        n = len(grads)
        self.n = n
        self.inputs = list(grads)
        self.out_shape = [jax.ShapeDtypeStruct((4,) + g.shape[2:], g.dtype) for g in grads]
        self.scratch = [pltpu.SemaphoreType.DMA((4 * n,)), pltpu.SemaphoreType.DMA((4 * n,))]
        self.aliases = {}

    def _copies(self, ins, outs, sems):
        x, y, c, _ = _place()
        return [pltpu.make_async_remote_copy(
            src_ref=ins[a].at[k, 1 - c], dst_ref=outs[a].at[k], send_sem=sems[0].at[4 * a + k], recv_sem=sems[1].at[4 * a + k],
            device_id=(x, y, 1 - c), device_id_type=MESH) for a in range(self.n) for k in range(4)]

    def start(self, ins, outs, sems):
        for cp in self._copies(ins, outs, sems):
            cp.start()

    def finish(self, ins, outs, sems):
        for cp in self._copies(ins, outs, sems):
            cp.wait()


class _Riders:
    def __init__(self, riders):
        self.riders = list(riders)
        self.inputs = [a for r in riders for a in r.inputs]
        self.out_shape = [s for r in riders for s in r.out_shape]
        self.scratch = [s for r in riders for s in r.scratch]
        self.aliases = {}
        i = o = 0
        for r in riders:
            self.aliases.update({i + a: o + b for a, b in r.aliases.items()})
            i, o = i + len(r.inputs), o + len(r.out_shape)

    def _each(self, ins, outs, sems):
        i = o = s = 0
        for r in self.riders:
            yield r, ins[i:i + len(r.inputs)], outs[o:o + len(r.out_shape)], sems[s:s + len(r.scratch)]
            i, o, s = i + len(r.inputs), o + len(r.out_shape), s + len(r.scratch)

    def start(self, ins, outs, sems):
        for r, a, b, c in self._each(ins, outs, sems):
            r.start(a, b, c)

    def forward(self, ins, outs, sems):
        for r, a, b, c in self._each(ins, outs, sems):
            if hasattr(r, "forward"):
                r.forward(a, b, c)

    def finish(self, ins, outs, sems):
        for r, a, b, c in self._each(ins, outs, sems):
            r.finish(a, b, c)

    def split(self, outs):
        res, o = [], 0
        for r in self.riders:
            res.append(outs[o:o + len(r.out_shape)])
            o += len(r.out_shape)
        return res


class _Reducer:
    def __init__(self, chip, core):
        self.chip, self.core, self.grads, self.parts, self.sums, self.others = chip, core, {}, {}, {}, {}

    def swap(self, name, grad):
        self.grads[name] = grad
        return _SwapRider([grad])

    def swapped(self, name, got):
        self.parts[name] = _add_pair(self.grads[name], got[0], self.core, name)

    def scatter(self, name):
        return _ScatterRider([self.parts[name]])

    def scattered(self, name, others):
        self.sums[name] = _add_chips(self.parts[name], others[0], self.chip, name)


class _SwapSumsRider:
    def __init__(self, halves):
        n = len(halves)
        self.n = n
        self.inputs = list(halves)
        self.out_shape = [jax.ShapeDtypeStruct(s.shape, s.dtype) for s in halves]
        self.scratch = [pltpu.SemaphoreType.DMA((n,)), pltpu.SemaphoreType.DMA((n,))]
        self.aliases = {}

    def _copies(self, ins, outs, sems):
        x, y, c, _ = _place()
        return [pltpu.make_async_remote_copy(
            src_ref=ins[a], dst_ref=outs[a], send_sem=sems[0].at[a], recv_sem=sems[1].at[a],
            device_id=(x, y, 1 - c), device_id_type=MESH) for a in range(self.n)]

    def start(self, ins, outs, sems):
        for cp in self._copies(ins, outs, sems):
            cp.start()

    def finish(self, ins, outs, sems):
        for cp in self._copies(ins, outs, sems):
            cp.wait()


def _row_tile(r):
    for tm in (TM, 176, 128, 64, 32, 16, 8):
        if r % tm == 0:
            return tm
    return r


def _add_pair(mine, got, core, name):
    k, _, h, c = mine.shape
    tm = _row_tile(h)

    def body(core_ref, a_ref, b_ref, o_ref):
        o_ref[0] = (a_ref[0, 0].astype(f32) + b_ref[0].astype(f32)).astype(MX)

    blk = pl.BlockSpec((1, tm, c), lambda i, j, core: (i, j, 0))
    return pl.pallas_call(
        body, name="add_pair_" + name, out_shape=jax.ShapeDtypeStruct((k, h, c), MX),
        grid_spec=pltpu.PrefetchScalarGridSpec(
            num_scalar_prefetch=1, grid=(k, h // tm),
            in_specs=[pl.BlockSpec((1, 1, tm, c), lambda i, j, core: (i, core[0], j, 0)), blk], out_specs=blk),
        compiler_params=_cp("arbitrary", "arbitrary"),
    )(_scalar(core), mine, got)


def _add_chips(parts, others, chip, name):
    _, n, c = others.shape
    tm = _row_tile(n)

    def body(chip_ref, a_ref, b_ref, o_ref):
        s = a_ref[0].astype(f32) + b_ref[0].astype(f32)
        o_ref[...] = (s + b_ref[1].astype(f32)) + b_ref[2].astype(f32)

    return pl.pallas_call(
        body, name="add_chips_" + name, out_shape=jax.ShapeDtypeStruct((n, c), f32),
        grid_spec=pltpu.PrefetchScalarGridSpec(
            num_scalar_prefetch=1, grid=(n // tm,),
            in_specs=[pl.BlockSpec((1, tm, c), lambda i, chip: (chip[0], i, 0)),
                      pl.BlockSpec((3, tm, c), lambda i, chip: (0, i, 0))],
            out_specs=pl.BlockSpec((tm, c), lambda i, chip: (i, 0))),
        compiler_params=_cp("arbitrary"),
    )(_scalar(chip), parts, others)


def _adam_math(w, g, m, v):
    m = ADAM_B1 * m + (1.0 - ADAM_B1) * g
    v = ADAM_B2 * v + (1.0 - ADAM_B2) * (g * g)
    m_hat = m / (1.0 - ADAM_B1 ** ADAM_STEP)
    v_hat = v / (1.0 - ADAM_B2 ** ADAM_STEP)
    return -ADAM_LR * (m_hat / (jnp.sqrt(v_hat) + ADAM_EPS) + ADAM_WD * w), m, v


def _adamw_halves(w, mine, other, m, v, core, name, after):
    r, c = w.shape
    h = r // 2
    tm = _row_tile(h)
    nj = h // tm
    cg = mine.shape[1]

    def body(core_ref, w_ref, a_ref, b_ref, m_ref, v_ref, after_ref, g_ref, d_ref, nm_ref, nv_ref):
        g = jnp.where(pl.program_id(0) == core_ref[0], a_ref[:, 0:c], b_ref[:, 0:c])
        g_ref[...] = g
        d_ref[...], nm_ref[...], nv_ref[...] = _adam_math(w_ref[...], g, m_ref[...], v_ref[...])

    blk = pl.BlockSpec((tm, c), lambda i, j, core: (i * nj + j, 0))
    gblk = pl.BlockSpec((tm, cg), lambda i, j, core: (j, 0))
    return _call(body, name=name, grid=(2, nj), out_shape=[jax.ShapeDtypeStruct((r, c), f32)] * 4,
                 in_specs=[blk, gblk, gblk, blk, blk, ANY], out_specs=(blk,) * 4, sem=("arbitrary", "arbitrary"),
                 prefetch=(_scalar(core),), args=(w, mine, other, m, v, after))[0]


def _ada_adamw(c_all_t, d_mod, w, m, v, after):
    r, c = w.shape
    tm = TM

    def body(ct_ref, dm_ref, w_ref, m_ref, v_ref, after_ref, g_ref, d_ref, nm_ref, nv_ref):
        ca = _silu(ct_ref[...])
        g = ca[:, 0:1] * dm_ref[0:1, :]
        for b in range(1, 8):
            g = g + ca[:, b:b + 1] * dm_ref[b:b + 1, :]
        g_ref[...] = g
        d_ref[...], nm_ref[...], nv_ref[...] = _adam_math(w_ref[...], g, m_ref[...], v_ref[...])

    blk = pl.BlockSpec((tm, c), lambda i: (i, 0))
    return _call(body, name="ada_adamw", grid=(r // tm,), out_shape=[jax.ShapeDtypeStruct((r, c), f32)] * 4,
                 in_specs=[pl.BlockSpec((tm, 8), lambda i: (i, 0)), pl.BlockSpec((8, c), lambda i: (0, 0)), blk, blk, blk, ANY],
                 out_specs=(blk,) * 4, sem=("arbitrary",), args=(c_all_t, d_mod, w, m, v, after))[0]


WEIGHTS = ("ada_w", "ada_b", "norm1_w", "w_in", "ssd_conv_w", "ssd_conv_b", "dt_bias", "a_log", "d_skip", "ssd_norm_w",
           "conf_conv_w", "conf_conv_b", "conf_ln_w", "conf_ln_b", "w_out", "norm2_w", "w_up", "ffn_conv_w", "ffn_conv_b",
           "w_down", "final_norm_w")


def kernel(x, c, ada_w, ada_b, norm1_w, w_in, ssd_conv_w, ssd_conv_b, dt_bias, a_log, d_skip, ssd_norm_w, conf_conv_w, conf_conv_b, conf_ln_w, conf_ln_b, w_out, norm2_w, w_up, ffn_conv_w, ffn_conv_b, w_down, final_norm_w, loss_target, m_ada_w, m_ada_b, m_norm1_w, m_w_in, m_ssd_conv_w, m_ssd_conv_b, m_dt_bias, m_a_log, m_d_skip, m_ssd_norm_w, m_conf_conv_w, m_conf_conv_b, m_conf_ln_w, m_conf_ln_b, m_w_out, m_norm2_w, m_w_up, m_ffn_conv_w, m_ffn_conv_b, m_w_down, m_final_norm_w, v_ada_w, v_ada_b, v_norm1_w, v_w_in, v_ssd_conv_w, v_ssd_conv_b, v_dt_bias, v_a_log, v_d_skip, v_ssd_norm_w, v_conf_conv_w, v_conf_conv_b, v_conf_ln_w, v_conf_ln_b, v_w_out, v_norm2_w, v_w_up, v_ffn_conv_w, v_ffn_conv_b, v_w_down, v_final_norm_w):
    given = dict(locals())
    w = {n: given[n] for n in WEIGHTS}
    mom = {n: given["m_" + n] for n in WEIGHTS}
    var = {n: given["v_" + n] for n in WEIGHTS}
    chip = 2 * lax.axis_index("x") + lax.axis_index("y")
    me = 2 * chip + lax.axis_index("c")

    core = lax.axis_index("c")
    got, mod_cols, a_in, late = _front(_pack_front(c, [w[n] for n in CONVS]), ada_w[0],
                                       _cast_into_slot_w_in(_columns_first(w_in), chip), [w_out[0], w_up[0], w_down[0]])
    c_all, *convs = _unpack_front(got)
    conv_full = dict(zip(CONVS, convs))
    mod_cols = mod_cols.reshape(8, 8, -1)[0::2]
    mod = lax.dynamic_index_in_dim(mod_cols, me, axis=1, keepdims=False).reshape(1, 6 * D) + ada_b
    w_pack = _pack_w_in(a_in.reshape(4, D, W_IN_SHARD_PAD))

    flat = lambda a: a.reshape(1, -1) if a.ndim == 1 else a
    small = {n: flat(w[n]) for n in VECTORS if n != "ada_b"}
    small.update(conv_full)
    reducer = _Reducer(chip, core)
    _, grad_x, _, gsmall = _local_step(x[0], mod, loss_target[0], w_pack, late, small, reducer)
    grads, delta, new_m, new_v = {}, {}, {}, {}

    names = VECTORS + tuple(CONVS)
    d_mod_mine, loss, res = _small_adamw(_gather_rows(gsmall), chip, *[{n: flat(d[n]) for n in names} for d in (w, mom, var)])
    for n in names:
        grads[n], delta[n], new_m[n], new_v[n] = [r.reshape(w[n].shape) for r in res[n]]

    scatter = reducer.scatter("w_in")
    handles, token = _split_start(scatter, "scatter_start_w_in", after=d_mod_mine)
    for n in ("w_up", "w_down", "w_out"):
        res = _adamw_halves(w[n][0], reducer.sums[n], reducer.others[n], mom[n][0], var[n][0], core, "adamw_" + n, token)
        grads[n], delta[n], new_m[n], new_v[n] = [r[None] for r in res]
    res = _ada_adamw(c_all.T, d_mod_mine, ada_w[0], m_ada_w[0], v_ada_w[0], token)
    grads["ada_w"], delta["ada_w"], new_m["ada_w"], new_v["ada_w"] = [r[None] for r in res]
    (reducer.parts["w_in"],), others = _split_wait(scatter, "scatter_wait_w_in", handles, res[1])
    reducer.scattered("w_in", others)
    reducer.others["w_in"], = _ride_alone(_SwapSumsRider([reducer.sums["w_in"]]), "swap_sums_w_in")
    res = _adamw_w_in(_columns_first(w_in), reducer.sums["w_in"], reducer.others["w_in"], _columns_first(m_w_in),
                      _columns_first(v_w_in), core)
    grads["w_in"], delta["w_in"], new_m["w_in"], new_v["w_in"] = [jnp.transpose(r, (1, 2, 0)) for r in res]

    return (loss, grad_x[None], *[grads[n] for n in WEIGHTS], *[delta[n] for n in WEIGHTS],
            *[new_m[n] for n in WEIGHTS], *[new_v[n] for n in WEIGHTS])
```
